```python
import math
import jax
import jax.numpy as jnp
from jax import lax
import numpy as np

D_MODEL = 1024
BATCH = 8
SEQ = 4096
DEPTH = 1

HEAD_DIM = 64
A_GROUPS = ((128, 1), (512, 4), (2048, 16))
A_HEADS = 4
A_QKV_COLS = len(A_GROUPS) * 3 * A_HEADS * HEAD_DIM
A_OUT = A_HEADS * HEAD_DIM
B_Q_HEADS = 8
B_KV_HEADS = 2
B_GROUP = B_Q_HEADS // B_KV_HEADS
B_WINDOW = 128
B_Q_COLS = B_Q_HEADS * HEAD_DIM
B_KV_COLS = B_KV_HEADS * HEAD_DIM
M_HEADS = 4
M_HEAD_DIM = 128
M_Q_COLS = M_HEADS * M_HEAD_DIM
MEM_LEN = 256
IN_SPLITS = (A_QKV_COLS,
             A_QKV_COLS + B_Q_COLS,
             A_QKV_COLS + B_Q_COLS + B_KV_COLS,
             A_QKV_COLS + B_Q_COLS + 2 * B_KV_COLS)
IN_COLS = IN_SPLITS[-1] + M_Q_COLS
N_BRANCH = 3
D_FF = 2816
CONV_WIDTH = 3
ROPE_THETA = 500000.0
ROPE_DIM_FRAC = 4
BLOCK = 128
EPS = 1e-6

kernel_name = 'hybrid_dilated_swa_memory_convffn'


def rmsnorm(t, gain):
    tf = t.astype(jnp.float32)
    y = tf * lax.rsqrt(jnp.mean(tf * tf, axis=-1, keepdims=True) + EPS)
    return (y * gain.astype(jnp.float32)).astype(t.dtype)


def rope_partial(t, positions):
    dh = t.shape[-1]
    rot = dh // ROPE_DIM_FRAC
    half = rot // 2
    freqs = jnp.exp(jnp.arange(half, dtype=jnp.float32) * (-2.0 * math.log(ROPE_THETA) / rot))
    ang = positions.astype(jnp.float32)[:, :, None, None] * freqs
    cos, sin = jnp.cos(ang), jnp.sin(ang)
    t1 = t[..., :half].astype(jnp.float32)
    t2 = t[..., half:rot].astype(jnp.float32)
    rotated = jnp.concatenate([t1 * cos - t2 * sin, t2 * cos + t1 * sin], axis=-1).astype(t.dtype)
    return jnp.concatenate([rotated, t[..., rot:]], axis=-1)


def banded_attn(q, k, v, max_dist, sink=None):
    assert max_dist <= BLOCK
    n, hk, g, L, dh = q.shape
    nb = -(-L // BLOCK)
    pad = nb * BLOCK - L
    q = jnp.pad(q, ((0, 0), (0, 0), (0, 0), (0, pad), (0, 0))).reshape(n, hk, g, nb, BLOCK, dh)
    k = jnp.pad(k, ((0, 0), (0, 0), (0, pad), (0, 0))).reshape(n, hk, nb, BLOCK, dh)
    v = jnp.pad(v, ((0, 0), (0, 0), (0, pad), (0, 0))).reshape(n, hk, nb, BLOCK, dh)

    def with_prev(t):
        prev = jnp.pad(t, ((0, 0), (0, 0), (1, 0), (0, 0), (0, 0)))[:, :, :-1]
        return jnp.concatenate([prev, t], axis=3)

    kw, vw = with_prev(k), with_prev(v)
    s = jnp.einsum('nhgbqd,nhbkd->nhgbqk', q, kw).astype(jnp.float32) * (dh ** -0.5)
    qi = jnp.arange(BLOCK)[:, None]
    kj = jnp.arange(2 * BLOCK)[None, :]
    dist = qi + BLOCK - kj
    band = (dist >= 0) & (dist <= max_dist)
    blk = jnp.arange(nb)[:, None, None]
    mask = band[None] & ((blk > 0) | (kj[None] >= BLOCK))
    s = jnp.where(mask, s, -jnp.inf)
    lse = jax.nn.logsumexp(s, axis=-1)
    if sink is not None:
        lse = jnp.logaddexp(lse, sink.astype(jnp.float32)[None, :, :, None, None])
    p = jnp.exp(s - lse[..., None])
    o = jnp.einsum('nhgbqk,nhbkd->nhgbqd', p.astype(v.dtype), vw)
    o = o.reshape(n, hk, g, nb * BLOCK, dh)[:, :, :, :L]
    lse = lse.reshape(n, hk, g, nb * BLOCK)[:, :, :, :L]
    return o, lse


def to_residue(t, d):
    b, s, h, dh = t.shape
    return t.reshape(b, s // d, d, h, dh).transpose(0, 2, 3, 1, 4).reshape(b * d, h, s // d, dh)


def from_residue(t, d):
    nd, h, L = t.shape[:3]
    rest = t.shape[3:]
    t = t.reshape(nd // d, d, h, L, *rest)
    t = jnp.moveaxis(t, 3, 1)
    return t.reshape(nd // d, L * d, h, *rest)


def dilated_mixture_attn(qkv_a, positions, q_gain, k_gain):
    b, s = qkv_a.shape[:2]
    outs, lses = [], []
    for gi, (window, dil) in enumerate(A_GROUPS):
        q = rope_partial(rmsnorm(qkv_a[:, :, gi, 0], q_gain[gi]), positions)
        k = rope_partial(rmsnorm(qkv_a[:, :, gi, 1], k_gain[gi]), positions)
        v = qkv_a[:, :, gi, 2]
        o, lse = banded_attn(to_residue(q, dil)[:, :, None], to_residue(k, dil),
                             to_residue(v, dil), window // dil)
        outs.append(from_residue(o[:, :, 0], dil))
        lses.append(from_residue(lse[:, :, 0], dil))
    w = jax.nn.softmax(jnp.stack(lses), axis=0)
    o = jnp.einsum('gbshd,gbsh->bshd', jnp.stack(outs), w.astype(qkv_a.dtype))
    return o.reshape(b, s, A_OUT)


def sink_swa_gqa(q, k, v, positions, q_gain, k_gain, sinks):
    b, s = q.shape[:2]
    q = rope_partial(rmsnorm(q, q_gain), positions)
    k = rope_partial(rmsnorm(k, k_gain), positions)
    qg = q.reshape(b, s, B_KV_HEADS, B_GROUP, HEAD_DIM).transpose(0, 2, 3, 1, 4)
    o, _ = banded_attn(qg, k.transpose(0, 2, 1, 3), v.transpose(0, 2, 1, 3),
                       B_WINDOW - 1, sink=sinks.reshape(B_KV_HEADS, B_GROUP))
    return o.transpose(0, 3, 1, 2, 4).reshape(b, s, B_Q_COLS)


def memory_attn(mq, mem, mem_gain, w_kv, q_gain, k_gain):
    b, s = mq.shape[:2]
    m = mem.shape[1]
    kv = (rmsnorm(mem, mem_gain) @ w_kv).reshape(b, m, 2, M_HEADS, M_HEAD_DIM)
    mk = rmsnorm(kv[:, :, 0], k_gain)
    mv = kv[:, :, 1]
    q = rmsnorm(mq.reshape(b, s, M_HEADS, M_HEAD_DIM), q_gain)
    sc = jnp.einsum('bshd,bmhd->bhsm', q, mk).astype(jnp.float32) * (M_HEAD_DIM ** -0.5)
    p = jax.nn.softmax(sc, axis=-1)
    o = jnp.einsum('bhsm,bmhd->bshd', p.astype(mv.dtype), mv)
    return o.reshape(b, s, M_Q_COLS)


def conv_ffn(h, w_up, conv_w, conv_b, w_down):
    s = h.shape[1]
    u = h @ w_up
    up = jnp.pad(u, ((0, 0), (CONV_WIDTH - 1, 0), (0, 0)))
    c = conv_b
    for j in range(CONV_WIDTH):
        c = c + conv_w[j] * up[:, j:j + s]
    a, g = jnp.split(c, 2, axis=-1)
    return (jax.nn.silu(a) * g) @ w_down


def _fwd_setup_inputs(seed: int = 0) -> dict:
    key = jax.random.key(seed)
    ks = jax.random.split(key, 32)
    f32 = jnp.float32
    L = DEPTH

    def w(k, shape, fan_in):
        return jax.random.normal(k, shape, f32) * (fan_in ** -0.5)

    def gain(k, shape):
        return 1.0 + 0.02 * jax.random.normal(k, shape, f32)

    positions = (jax.random.randint(ks[2], (BATCH, 1), 0, 1024, dtype=jnp.int32)
                 + jnp.arange(SEQ, dtype=jnp.int32)[None, :])
    return {
        'x': jax.random.normal(ks[0], (BATCH, SEQ, D_MODEL), f32),
        'mem': jax.random.normal(ks[1], (BATCH, MEM_LEN, D_MODEL), f32),
        'positions': positions,
        'attn_norm': gain(ks[3], (L, D_MODEL)),
        'w_in': w(ks[4], (L, D_MODEL, IN_COLS), D_MODEL),
        'a_q_norm': gain(ks[5], (L, len(A_GROUPS), HEAD_DIM)),
        'a_k_norm': gain(ks[6], (L, len(A_GROUPS), HEAD_DIM)),
        'b_q_norm': gain(ks[7], (L, HEAD_DIM)),
        'b_k_norm': gain(ks[8], (L, HEAD_DIM)),
        'b_sinks': jax.random.normal(ks[9], (L, B_Q_HEADS), f32),
        'mem_norm': gain(ks[10], (L, D_MODEL)),
        'w_mem_kv': w(ks[11], (L, D_MODEL, 2 * M_Q_COLS), D_MODEL),
        'm_q_norm': gain(ks[12], (L, M_HEAD_DIM)),
        'm_k_norm': gain(ks[13], (L, M_HEAD_DIM)),
        'w_o_a': w(ks[14], (L, A_OUT, D_MODEL), A_OUT),
        'w_o_b': w(ks[15], (L, B_Q_COLS, D_MODEL), B_Q_COLS),
        'w_o_m': w(ks[16], (L, M_Q_COLS, D_MODEL), M_Q_COLS),
        'w_gate': w(ks[17], (L, D_MODEL, N_BRANCH * D_MODEL), D_MODEL),
        'b_gate': 0.01 * jax.random.normal(ks[18], (L, N_BRANCH * D_MODEL), f32),
        'w_out': w(ks[19], (L, D_MODEL, D_MODEL), D_MODEL),
        'ffn_norm': gain(ks[20], (L, D_MODEL)),
        'w_up': w(ks[21], (L, D_MODEL, 2 * D_FF), D_MODEL),
        'conv_w': w(ks[22], (L, CONV_WIDTH, 2 * D_FF), CONV_WIDTH),
        'conv_b': 0.01 * jax.random.normal(ks[23], (L, 2 * D_FF), f32),
        'w_down': w(ks[24], (L, D_FF, D_MODEL), D_FF),
    }


def _fwd_reference(x, mem, positions, attn_norm, w_in, a_q_norm, a_k_norm, b_q_norm, b_k_norm,
              b_sinks, mem_norm, w_mem_kv, m_q_norm, m_k_norm, w_o_a, w_o_b, w_o_m,
              w_gate, b_gate, w_out, ffn_norm, w_up, conv_w, conv_b, w_down):
    b, s, _ = x.shape
    for l in range(DEPTH):
        h = rmsnorm(x, attn_norm[l])
        proj = h @ w_in[l]
        qkv_a, b_q, b_k, b_v, m_q = jnp.split(proj, IN_SPLITS, axis=-1)
        qkv_a = qkv_a.reshape(b, s, len(A_GROUPS), 3, A_HEADS, HEAD_DIM)
        o_a = dilated_mixture_attn(qkv_a, positions, a_q_norm[l], a_k_norm[l])
        o_b = sink_swa_gqa(b_q.reshape(b, s, B_Q_HEADS, HEAD_DIM),
                           b_k.reshape(b, s, B_KV_HEADS, HEAD_DIM),
                           b_v.reshape(b, s, B_KV_HEADS, HEAD_DIM),
                           positions, b_q_norm[l], b_k_norm[l], b_sinks[l])
        o_m = memory_attn(m_q, mem, mem_norm[l], w_mem_kv[l], m_q_norm[l], m_k_norm[l])
        gates = jax.nn.sigmoid((h @ w_gate[l] + b_gate[l]).astype(jnp.float32))
        gates = gates.astype(x.dtype).reshape(b, s, N_BRANCH, D_MODEL)
        merged = (gates[:, :, 0] * (o_a @ w_o_a[l])
                  + gates[:, :, 1] * (o_b @ w_o_b[l])
                  + gates[:, :, 2] * (o_m @ w_o_m[l]))
        x = x + merged @ w_out[l]
        x = x + conv_ffn(rmsnorm(x, ffn_norm[l]), w_up[l], conv_w[l], conv_b[l], w_down[l])
    return x


import jax as _jax
import jax.numpy as _jnp

TWIN_FORMAT = 'train_step'
FWD_PARAMS = ['x', 'mem', 'positions', 'attn_norm', 'w_in', 'a_q_norm', 'a_k_norm', 'b_q_norm', 'b_k_norm', 'b_sinks', 'mem_norm', 'w_mem_kv', 'm_q_norm', 'm_k_norm', 'w_o_a', 'w_o_b', 'w_o_m', 'w_gate', 'b_gate', 'w_out', 'ffn_norm', 'w_up', 'conv_w', 'conv_b', 'w_down']
TWIN_WEIGHTS = ['attn_norm', 'w_in', 'a_q_norm', 'a_k_norm', 'b_q_norm', 'b_k_norm', 'b_sinks', 'mem_norm', 'w_mem_kv', 'm_q_norm', 'm_k_norm', 'w_o_a', 'w_o_b', 'w_o_m', 'w_gate', 'b_gate', 'w_out', 'ffn_norm', 'w_up', 'conv_w', 'conv_b', 'w_down']
TWIN_DIFF_INPUT = 'x'
TWIN_INPUTS = ['x', 'mem', 'positions', 'attn_norm', 'w_in', 'a_q_norm', 'a_k_norm', 'b_q_norm', 'b_k_norm', 'b_sinks', 'mem_norm', 'w_mem_kv', 'm_q_norm', 'm_k_norm', 'w_o_a', 'w_o_b', 'w_o_m', 'w_gate', 'b_gate', 'w_out', 'ffn_norm', 'w_up', 'conv_w', 'conv_b', 'w_down', 'loss_target', 'm_attn_norm', 'm_w_in', 'm_a_q_norm', 'm_a_k_norm', 'm_b_q_norm', 'm_b_k_norm', 'm_b_sinks', 'm_mem_norm', 'm_w_mem_kv', 'm_m_q_norm', 'm_m_k_norm', 'm_w_o_a', 'm_w_o_b', 'm_w_o_m', 'm_w_gate', 'm_b_gate', 'm_w_out', 'm_ffn_norm', 'm_w_up', 'm_conv_w', 'm_conv_b', 'm_w_down', 'v_attn_norm', 'v_w_in', 'v_a_q_norm', 'v_a_k_norm', 'v_b_q_norm', 'v_b_k_norm', 'v_b_sinks', 'v_mem_norm', 'v_w_mem_kv', 'v_m_q_norm', 'v_m_k_norm', 'v_w_o_a', 'v_w_o_b', 'v_w_o_m', 'v_w_gate', 'v_b_gate', 'v_w_out', 'v_ffn_norm', 'v_w_up', 'v_conv_w', 'v_conv_b', 'v_w_down']
TWIN_OUTPUTS = ['loss', 'grad_x', 'grad_attn_norm', 'grad_w_in', 'grad_a_q_norm', 'grad_a_k_norm', 'grad_b_q_norm', 'grad_b_k_norm', 'grad_b_sinks', 'grad_mem_norm', 'grad_w_mem_kv', 'grad_m_q_norm', 'grad_m_k_norm', 'grad_w_o_a', 'grad_w_o_b', 'grad_w_o_m', 'grad_w_gate', 'grad_b_gate', 'grad_w_out', 'grad_ffn_norm', 'grad_w_up', 'grad_conv_w', 'grad_conv_b', 'grad_w_down', 'delta_attn_norm', 'delta_w_in', 'delta_a_q_norm', 'delta_a_k_norm', 'delta_b_q_norm', 'delta_b_k_norm', 'delta_b_sinks', 'delta_mem_norm', 'delta_w_mem_kv', 'delta_m_q_norm', 'delta_m_k_norm', 'delta_w_o_a', 'delta_w_o_b', 'delta_w_o_m', 'delta_w_gate', 'delta_b_gate', 'delta_w_out', 'delta_ffn_norm', 'delta_w_up', 'delta_conv_w', 'delta_conv_b', 'delta_w_down', 'new_m_attn_norm', 'new_m_w_in', 'new_m_a_q_norm', 'new_m_a_k_norm', 'new_m_b_q_norm', 'new_m_b_k_norm', 'new_m_b_sinks', 'new_m_mem_norm', 'new_m_w_mem_kv', 'new_m_m_q_norm', 'new_m_m_k_norm', 'new_m_w_o_a', 'new_m_w_o_b', 'new_m_w_o_m', 'new_m_w_gate', 'new_m_b_gate', 'new_m_w_out', 'new_m_ffn_norm', 'new_m_w_up', 'new_m_conv_w', 'new_m_conv_b', 'new_m_w_down', 'new_v_attn_norm', 'new_v_w_in', 'new_v_a_q_norm', 'new_v_a_k_norm', 'new_v_b_q_norm', 'new_v_b_k_norm', 'new_v_b_sinks', 'new_v_mem_norm', 'new_v_w_mem_kv', 'new_v_m_q_norm', 'new_v_m_k_norm', 'new_v_w_o_a', 'new_v_w_o_b', 'new_v_w_o_m', 'new_v_w_gate', 'new_v_b_gate', 'new_v_w_out', 'new_v_ffn_norm', 'new_v_w_up', 'new_v_conv_w', 'new_v_conv_b', 'new_v_w_down']
TWIN_LEAF_KINDS = {'loss': 'loss', 'grad_x': 'grad_x', 'grad_attn_norm': 'grad_w', 'grad_w_in': 'grad_w', 'grad_a_q_norm': 'grad_w', 'grad_a_k_norm': 'grad_w', 'grad_b_q_norm': 'grad_w', 'grad_b_k_norm': 'grad_w', 'grad_b_sinks': 'grad_w', 'grad_mem_norm': 'grad_w', 'grad_w_mem_kv': 'grad_w', 'grad_m_q_norm': 'grad_w', 'grad_m_k_norm': 'grad_w', 'grad_w_o_a': 'grad_w', 'grad_w_o_b': 'grad_w', 'grad_w_o_m': 'grad_w', 'grad_w_gate': 'grad_w', 'grad_b_gate': 'grad_w', 'grad_w_out': 'grad_w', 'grad_ffn_norm': 'grad_w', 'grad_w_up': 'grad_w', 'grad_conv_w': 'grad_w', 'grad_conv_b': 'grad_w', 'grad_w_down': 'grad_w', 'delta_attn_norm': 'delta_w', 'delta_w_in': 'delta_w', 'delta_a_q_norm': 'delta_w', 'delta_a_k_norm': 'delta_w', 'delta_b_q_norm': 'delta_w', 'delta_b_k_norm': 'delta_w', 'delta_b_sinks': 'delta_w', 'delta_mem_norm': 'delta_w', 'delta_w_mem_kv': 'delta_w', 'delta_m_q_norm': 'delta_w', 'delta_m_k_norm': 'delta_w', 'delta_w_o_a': 'delta_w', 'delta_w_o_b': 'delta_w', 'delta_w_o_m': 'delta_w', 'delta_w_gate': 'delta_w', 'delta_b_gate': 'delta_w', 'delta_w_out': 'delta_w', 'delta_ffn_norm': 'delta_w', 'delta_w_up': 'delta_w', 'delta_conv_w': 'delta_w', 'delta_conv_b': 'delta_w', 'delta_w_down': 'delta_w', 'new_m_attn_norm': 'new_m', 'new_m_w_in': 'new_m', 'new_m_a_q_norm': 'new_m', 'new_m_a_k_norm': 'new_m', 'new_m_b_q_norm': 'new_m', 'new_m_b_k_norm': 'new_m', 'new_m_b_sinks': 'new_m', 'new_m_mem_norm': 'new_m', 'new_m_w_mem_kv': 'new_m', 'new_m_m_q_norm': 'new_m', 'new_m_m_k_norm': 'new_m', 'new_m_w_o_a': 'new_m', 'new_m_w_o_b': 'new_m', 'new_m_w_o_m': 'new_m', 'new_m_w_gate': 'new_m', 'new_m_b_gate': 'new_m', 'new_m_w_out': 'new_m', 'new_m_ffn_norm': 'new_m', 'new_m_w_up': 'new_m', 'new_m_conv_w': 'new_m', 'new_m_conv_b': 'new_m', 'new_m_w_down': 'new_m', 'new_v_attn_norm': 'new_v', 'new_v_w_in': 'new_v', 'new_v_a_q_norm': 'new_v', 'new_v_a_k_norm': 'new_v', 'new_v_b_q_norm': 'new_v', 'new_v_b_k_norm': 'new_v', 'new_v_b_sinks': 'new_v', 'new_v_mem_norm': 'new_v', 'new_v_w_mem_kv': 'new_v', 'new_v_m_q_norm': 'new_v', 'new_v_m_k_norm': 'new_v', 'new_v_w_o_a': 'new_v', 'new_v_w_o_b': 'new_v', 'new_v_w_o_m': 'new_v', 'new_v_w_gate': 'new_v', 'new_v_b_gate': 'new_v', 'new_v_w_out': 'new_v', 'new_v_ffn_norm': 'new_v', 'new_v_w_up': 'new_v', 'new_v_conv_w': 'new_v', 'new_v_conv_b': 'new_v', 'new_v_w_down': 'new_v'}


def _forward(args):
    return _fwd_reference(*[args[k] for k in FWD_PARAMS])


def _output_shape():
    def fwd():
        inp = _fwd_setup_inputs(0)
        return _fwd_reference(*[inp[k] for k in FWD_PARAMS])
    out = _jax.eval_shape(fwd)
    return out.shape, out.dtype

N_MICROBATCH = 1
ADAM_LR = 0.001
ADAM_B1 = 0.9
ADAM_B2 = 0.999
ADAM_EPS = 1e-08
ADAM_WD = 0.01
ADAM_STEP = 10
PER_EXAMPLE_BATCH_AXIS = {'x': 0, 'mem': 0, 'positions': 0, 'loss_target': 0}
SHARED_INPUTS = []
_WEIGHT_DTYPES = {'attn_norm': _jnp.float32, 'w_in': _jnp.float32, 'a_q_norm': _jnp.float32, 'a_k_norm': _jnp.float32, 'b_q_norm': _jnp.float32, 'b_k_norm': _jnp.float32, 'b_sinks': _jnp.float32, 'mem_norm': _jnp.float32, 'w_mem_kv': _jnp.float32, 'm_q_norm': _jnp.float32, 'm_k_norm': _jnp.float32, 'w_o_a': _jnp.float32, 'w_o_b': _jnp.float32, 'w_o_m': _jnp.float32, 'w_gate': _jnp.float32, 'b_gate': _jnp.float32, 'w_out': _jnp.float32, 'ffn_norm': _jnp.float32, 'w_up': _jnp.float32, 'conv_w': _jnp.float32, 'conv_b': _jnp.float32, 'w_down': _jnp.float32}
MOMENT_SCALE = {'attn_norm': 2.508787e-01, 'w_in': 7.026085e-02, 'a_q_norm': 3.683274e-01, 'a_k_norm': 3.714104e-01, 'b_q_norm': 2.046954e+00, 'b_k_norm': 2.043896e+00, 'b_sinks': 7.243313e-01, 'mem_norm': 1.465323e-01, 'w_mem_kv': 1.098724e-01, 'm_q_norm': 7.698957e-01, 'm_k_norm': 7.694313e-01, 'w_o_a': 5.673480e-02, 'w_o_b': 6.591127e-02, 'w_o_m': 1.132418e-01, 'w_gate': 1.680274e-02, 'b_gate': 4.308644e-02, 'w_out': 9.998715e-02, 'ffn_norm': 2.657442e+01, 'w_up': 2.620076e-01, 'conv_w': 3.704000e+00, 'conv_b': 3.322983e+00, 'w_down': 3.817182e-01}


def _to_microbatches(a, axis):
    t = _jnp.moveaxis(a, axis, 0)
    t = t.reshape((N_MICROBATCH, t.shape[0] // N_MICROBATCH) + t.shape[1:])
    return _jnp.moveaxis(t, 1, axis + 1)


def setup_inputs(seed: int = 0) -> dict:
    inp = _fwd_setup_inputs(seed)
    key = _jax.random.fold_in(_jax.random.key(seed), 7919)
    shape, _ = _output_shape()
    out = dict(inp)
    out["loss_target"] = _jax.random.normal(_jax.random.fold_in(key, 0), shape, _jnp.float32)
    for i, name in enumerate(TWIN_WEIGHTS):
        w = inp[name].astype(_jnp.float32)
        if MOMENT_SCALE is None:
            s = _jnp.sqrt(_jnp.mean(_jnp.square(w)) + 1e-30)
        else:
            s = MOMENT_SCALE[name]
        km, kv = _jax.random.split(_jax.random.fold_in(key, i + 1))
        out[name] = w
        out["m_" + name] = s * _jax.random.normal(km, w.shape, _jnp.float32)
        out["v_" + name] = (s * s) * _jax.random.uniform(kv, w.shape, _jnp.float32, 0.5, 1.5)
    if N_MICROBATCH > 1:
        for name, axis in PER_EXAMPLE_BATCH_AXIS.items():
            out[name] = _to_microbatches(out[name], axis)
    return {'x': out['x'], 'mem': out['mem'], 'positions': out['positions'], 'attn_norm': out['attn_norm'], 'w_in': out['w_in'], 'a_q_norm': out['a_q_norm'], 'a_k_norm': out['a_k_norm'], 'b_q_norm': out['b_q_norm'], 'b_k_norm': out['b_k_norm'], 'b_sinks': out['b_sinks'], 'mem_norm': out['mem_norm'], 'w_mem_kv': out['w_mem_kv'], 'm_q_norm': out['m_q_norm'], 'm_k_norm': out['m_k_norm'], 'w_o_a': out['w_o_a'], 'w_o_b': out['w_o_b'], 'w_o_m': out['w_o_m'], 'w_gate': out['w_gate'], 'b_gate': out['b_gate'], 'w_out': out['w_out'], 'ffn_norm': out['ffn_norm'], 'w_up': out['w_up'], 'conv_w': out['conv_w'], 'conv_b': out['conv_b'], 'w_down': out['w_down'], 'loss_target': out['loss_target'], 'm_attn_norm': out['m_attn_norm'], 'm_w_in': out['m_w_in'], 'm_a_q_norm': out['m_a_q_norm'], 'm_a_k_norm': out['m_a_k_norm'], 'm_b_q_norm': out['m_b_q_norm'], 'm_b_k_norm': out['m_b_k_norm'], 'm_b_sinks': out['m_b_sinks'], 'm_mem_norm': out['m_mem_norm'], 'm_w_mem_kv': out['m_w_mem_kv'], 'm_m_q_norm': out['m_m_q_norm'], 'm_m_k_norm': out['m_m_k_norm'], 'm_w_o_a': out['m_w_o_a'], 'm_w_o_b': out['m_w_o_b'], 'm_w_o_m': out['m_w_o_m'], 'm_w_gate': out['m_w_gate'], 'm_b_gate': out['m_b_gate'], 'm_w_out': out['m_w_out'], 'm_ffn_norm': out['m_ffn_norm'], 'm_w_up': out['m_w_up'], 'm_conv_w': out['m_conv_w'], 'm_conv_b': out['m_conv_b'], 'm_w_down': out['m_w_down'], 'v_attn_norm': out['v_attn_norm'], 'v_w_in': out['v_w_in'], 'v_a_q_norm': out['v_a_q_norm'], 'v_a_k_norm': out['v_a_k_norm'], 'v_b_q_norm': out['v_b_q_norm'], 'v_b_k_norm': out['v_b_k_norm'], 'v_b_sinks': out['v_b_sinks'], 'v_mem_norm': out['v_mem_norm'], 'v_w_mem_kv': out['v_w_mem_kv'], 'v_m_q_norm': out['v_m_q_norm'], 'v_m_k_norm': out['v_m_k_norm'], 'v_w_o_a': out['v_w_o_a'], 'v_w_o_b': out['v_w_o_b'], 'v_w_o_m': out['v_w_o_m'], 'v_w_gate': out['v_w_gate'], 'v_b_gate': out['v_b_gate'], 'v_w_out': out['v_w_out'], 'v_ffn_norm': out['v_ffn_norm'], 'v_w_up': out['v_w_up'], 'v_conv_w': out['v_conv_w'], 'v_conv_b': out['v_conv_b'], 'v_w_down': out['v_w_down']}


def _loss(weights, diff, rest, loss_target):
    with _jax.named_scope("forward"):
        args = {**rest, TWIN_DIFF_INPUT: diff, **{k: w.astype(_WEIGHT_DTYPES[k]) for k, w in weights.items()}}
        y = _forward(args)
    with _jax.named_scope("loss_head"):
        err = _jnp.square(y.astype(_jnp.float32) - loss_target)
        return 0.5 * _jnp.sum(_jnp.mean(err, axis=-1)) if err.ndim else 0.5 * err


def _adamw(w, g, m, v):
    m = ADAM_B1 * m + (1.0 - ADAM_B1) * g
    v = ADAM_B2 * v + (1.0 - ADAM_B2) * _jnp.square(g)
    m_hat = m / (1.0 - ADAM_B1 ** ADAM_STEP)
    v_hat = v / (1.0 - ADAM_B2 ** ADAM_STEP)
    delta = -ADAM_LR * (m_hat / (_jnp.sqrt(v_hat) + ADAM_EPS) + ADAM_WD * w)
    return delta, m, v


def reference(x, mem, positions, attn_norm, w_in, a_q_norm, a_k_norm, b_q_norm, b_k_norm, b_sinks, mem_norm, w_mem_kv, m_q_norm, m_k_norm, w_o_a, w_o_b, w_o_m, w_gate, b_gate, w_out, ffn_norm, w_up, conv_w, conv_b, w_down, loss_target, m_attn_norm, m_w_in, m_a_q_norm, m_a_k_norm, m_b_q_norm, m_b_k_norm, m_b_sinks, m_mem_norm, m_w_mem_kv, m_m_q_norm, m_m_k_norm, m_w_o_a, m_w_o_b, m_w_o_m, m_w_gate, m_b_gate, m_w_out, m_ffn_norm, m_w_up, m_conv_w, m_conv_b, m_w_down, v_attn_norm, v_w_in, v_a_q_norm, v_a_k_norm, v_b_q_norm, v_b_k_norm, v_b_sinks, v_mem_norm, v_w_mem_kv, v_m_q_norm, v_m_k_norm, v_w_o_a, v_w_o_b, v_w_o_m, v_w_gate, v_b_gate, v_w_out, v_ffn_norm, v_w_up, v_conv_w, v_conv_b, v_w_down):
    given = dict(x=x, mem=mem, positions=positions, attn_norm=attn_norm, w_in=w_in, a_q_norm=a_q_norm, a_k_norm=a_k_norm, b_q_norm=b_q_norm, b_k_norm=b_k_norm, b_sinks=b_sinks, mem_norm=mem_norm, w_mem_kv=w_mem_kv, m_q_norm=m_q_norm, m_k_norm=m_k_norm, w_o_a=w_o_a, w_o_b=w_o_b, w_o_m=w_o_m, w_gate=w_gate, b_gate=b_gate, w_out=w_out, ffn_norm=ffn_norm, w_up=w_up, conv_w=conv_w, conv_b=conv_b, w_down=w_down, loss_target=loss_target, m_attn_norm=m_attn_norm, m_w_in=m_w_in, m_a_q_norm=m_a_q_norm, m_a_k_norm=m_a_k_norm, m_b_q_norm=m_b_q_norm, m_b_k_norm=m_b_k_norm, m_b_sinks=m_b_sinks, m_mem_norm=m_mem_norm, m_w_mem_kv=m_w_mem_kv, m_m_q_norm=m_m_q_norm, m_m_k_norm=m_m_k_norm, m_w_o_a=m_w_o_a, m_w_o_b=m_w_o_b, m_w_o_m=m_w_o_m, m_w_gate=m_w_gate, m_b_gate=m_b_gate, m_w_out=m_w_out, m_ffn_norm=m_ffn_norm, m_w_up=m_w_up, m_conv_w=m_conv_w, m_conv_b=m_conv_b, m_w_down=m_w_down, v_attn_norm=v_attn_norm, v_w_in=v_w_in, v_a_q_norm=v_a_q_norm, v_a_k_norm=v_a_k_norm, v_b_q_norm=v_b_q_norm, v_b_k_norm=v_b_k_norm, v_b_sinks=v_b_sinks, v_mem_norm=v_mem_norm, v_w_mem_kv=v_w_mem_kv, v_m_q_norm=v_m_q_norm, v_m_k_norm=v_m_k_norm, v_w_o_a=v_w_o_a, v_w_o_b=v_w_o_b, v_w_o_m=v_w_o_m, v_w_gate=v_w_gate, v_b_gate=v_b_gate, v_w_out=v_w_out, v_ffn_norm=v_ffn_norm, v_w_up=v_w_up, v_conv_w=v_conv_w, v_conv_b=v_conv_b, v_w_down=v_w_down)
    weights = {n: given[n] for n in TWIN_WEIGHTS}
    shared = {n: given[n] for n in SHARED_INPUTS}
    per_example = {n: given[n] for n in ['x', 'mem', 'positions']}
    grad_fn = _jax.value_and_grad(_loss, argnums=(0, 1))

    def one_microbatch(ex, loss_target):
        ex = dict(ex)
        diff = ex.pop(TWIN_DIFF_INPUT)
        return grad_fn(weights, diff, {**shared, **ex}, loss_target)

    if N_MICROBATCH == 1:
        loss, (grad_w, grad_x) = one_microbatch(per_example, given["loss_target"])
    else:
        def body(carry, xs):
            loss_sum, grad_sum = carry
            l_k, (gw_k, gx_k) = one_microbatch(xs[0], xs[1])
            with _jax.named_scope("update"):
                return (loss_sum + l_k, _jax.tree.map(_jnp.add, grad_sum, gw_k)), gx_k

        init = (_jnp.zeros((), _jnp.float32), _jax.tree.map(_jnp.zeros_like, weights))
        (loss, grad_w), grad_x = _jax.lax.scan(body, init, (per_example, given["loss_target"]))
    with _jax.named_scope("update"):
        delta_w, new_m, new_v = {}, {}, {}
        for n in TWIN_WEIGHTS:
            delta_w[n], new_m[n], new_v[n] = _adamw(weights[n], grad_w[n], given["m_" + n], given["v_" + n])
    return (loss, grad_x, *[grad_w[n] for n in TWIN_WEIGHTS], *[delta_w[n] for n in TWIN_WEIGHTS],
            *[new_m[n] for n in TWIN_WEIGHTS], *[new_v[n] for n in TWIN_WEIGHTS])
```

```python
import math

import jax
import jax.numpy as jnp
import numpy as np
from jax import lax
from jax.experimental import pallas as pl
from jax.experimental.pallas import tpu as pltpu

F32 = jnp.float32
_MM = jnp.bfloat16

D_MODEL = 1024
HEAD = 64
BLK = 128
A_GROUPS = ((128, 1), (512, 4), (2048, 16))
A_HEADS = 4
A_W = A_HEADS * HEAD
B_QH = 8
B_KVH = 2
B_WINDOW = 128
M_HEADS = 4
M_HD = 128
M_W = M_HEADS * M_HD
D_FF = 2816
EPS = 1e-6
NEG = -1e30
ROPE_THETA = 500000.0
ROPE_ROT = 16
CHIPS = 4
NDEV = 8
ADAM_LR, ADAM_B1, ADAM_B2, ADAM_EPS, ADAM_WD, ADAM_STEP = 0.001, 0.9, 0.999, 1e-08, 0.01, 10
VMEM_LIMIT = 58 * 1024 * 1024
MESH = pl.DeviceIdType.MESH


def _pc(body, *, name, grid, in_specs, out_specs, out_shape, scratch=()):
    return pl.pallas_call(
        body, name=name, grid=grid, in_specs=in_specs, out_specs=out_specs, out_shape=out_shape,
        scratch_shapes=list(scratch),
        compiler_params=pltpu.CompilerParams(dimension_semantics=("arbitrary",) * len(grid),
                                             vmem_limit_bytes=VMEM_LIMIT))


def _row(ts, c, col=0):
    return pl.BlockSpec((ts, c), lambda i: (i, col))


def _res(shape):
    n = len(shape)
    return pl.BlockSpec(tuple(shape), lambda i: (0,) * n, pipeline_mode=pl.Buffered(1))


def _acc(shape):
    n = len(shape)
    return pl.BlockSpec(tuple(shape), lambda i: (0,) * n)


def _sds(shape, dtype=F32):
    return jax.ShapeDtypeStruct(tuple(shape), dtype)


def _dot(a, b):
    return jnp.dot(a.astype(_MM), b.astype(_MM), preferred_element_type=F32)


def _dot_nt(a, b):
    return lax.dot_general(a.astype(_MM), b.astype(_MM), (((1,), (1,)), ((), ())), preferred_element_type=F32)


def _dot_tn(a, b):
    return lax.dot_general(a.astype(_MM), b.astype(_MM), (((0,), (0,)), ((), ())), preferred_element_type=F32)


def _sum8(v):
    ts, c = v.shape
    return jnp.sum(v.reshape(ts // 8, 8, c), axis=0)


def _sigmoid(z):
    return 1.0 / (1.0 + jnp.exp(-z))


def _rms(x):
    r = lax.rsqrt(jnp.mean(x * x, axis=-1, keepdims=True) + EPS)
    return x * r, r


def _rms_bwd(dy, xh, r, gain):
    z = dy * gain
    return r * (z - xh * jnp.mean(z * xh, axis=-1, keepdims=True))


def _seg_norm(x, seg):
    w = x.shape[1]
    xh, rr = [], []
    for s in range(w // seg):
        xs = x[:, s * seg:(s + 1) * seg]
        r = lax.rsqrt(jnp.mean(xs * xs, axis=-1, keepdims=True) + EPS)
        xh.append(xs * r)
        rr.append(jnp.broadcast_to(r, xs.shape))
    return jnp.concatenate(xh, axis=1), jnp.concatenate(rr, axis=1)


def _seg_mean(v, seg):
    w = v.shape[1]
    out = []
    for s in range(w // seg):
        vs = v[:, s * seg:(s + 1) * seg]
        out.append(jnp.broadcast_to(jnp.mean(vs, axis=-1, keepdims=True), vs.shape))
    return jnp.concatenate(out, axis=1)


def _rope(t, c, sa, sb):
    out = []
    for cb in range(t.shape[1] // 128):
        tc = t[:, cb * 128:(cb + 1) * 128]
        out.append(tc * c + pltpu.roll(tc, 120, 1) * sa + pltpu.roll(tc, 8, 1) * sb)
    return jnp.concatenate(out, axis=1) if len(out) > 1 else out[0]


def _rope_bwd(dy, c, sa, sb):
    out = []
    for cb in range(dy.shape[1] // 128):
        dc = dy[:, cb * 128:(cb + 1) * 128]
        out.append(dc * c + pltpu.roll(dc * sa, 8, 1) + pltpu.roll(dc * sb, 120, 1))
    return jnp.concatenate(out, axis=1) if len(out) > 1 else out[0]


def _rope_freqs():
    c = np.float32(-2.0 * math.log(ROPE_THETA) / ROPE_ROT)
    return [float(v) for v in np.exp(np.arange(ROPE_ROT // 2, dtype=np.float32) * c)]


def _k_rope(pos2d):
    n = pos2d.shape[0]
    freqs = _rope_freqs()
    nf = len(freqs)

    def body(p_ref, c_ref, s_ref):
        p = p_ref[...].astype(F32)
        for f in range(nf):
            ang = p * freqs[f]
            c_ref[f] = jnp.cos(ang)
            s_ref[f] = jnp.sin(ang)

    return _pc(body, name="rope_tables", grid=(1,),
               in_specs=[_acc((n, 128))], out_specs=[_acc((nf, n, 128)), _acc((nf, n, 128))],
               out_shape=[_sds((nf, n, 128)), _sds((nf, n, 128))])(pos2d)


def _rope_tables(pos_rows):
    r = pos_rows.shape[0]
    cos, sin = _k_rope(pos_rows.reshape(r // 128, 128))
    half = ROPE_ROT // 2
    cos = cos.reshape(half, r).T
    sin = sin.reshape(half, r).T
    one = jnp.ones((r, HEAD - ROPE_ROT), F32)
    zero = jnp.zeros((r, HEAD - ROPE_ROT), F32)
    z8 = jnp.zeros((r, half), F32)
    c64 = jnp.concatenate([cos, cos, one], axis=1)
    sa64 = jnp.concatenate([-sin, z8, zero], axis=1)
    sb64 = jnp.concatenate([z8, sin, zero], axis=1)
    return tuple(jnp.concatenate([t, t], axis=1) for t in (c64, sa64, sb64))


def _k_in(x, g1, w_in, w_gate, b_gate):
    s = x.shape[0]
    ts = min(256, s)
    nin, ng = w_in.shape[2], w_gate.shape[2]
    ncol = CHIPS * nin
    a_cols = 3 * A_W
    offs = [0, a_cols, 2 * a_cols, 3 * a_cols, 3 * a_cols + B_QH * HEAD,
            3 * a_cols + (B_QH + B_KVH) * HEAD, 3 * a_cols + (B_QH + 2 * B_KVH) * HEAD, ncol]

    def body(x_ref, g_ref, wi_ref, wg_ref, bg_ref, h_ref, a0, a1, a2, qb, kb, vb, mq, gt_ref, p_scr):
        xh, _ = _rms(x_ref[...])
        h = (xh * g_ref[...]).astype(_MM)
        h_ref[...] = h
        for j in range(CHIPS):
            p_scr[:, j * nin:(j + 1) * nin] = jnp.dot(h, wi_ref[j], preferred_element_type=F32)
            z = jnp.dot(h, wg_ref[j], preferred_element_type=F32) + bg_ref[:, j * ng:(j + 1) * ng]
            gt_ref[:, j * ng:(j + 1) * ng] = _sigmoid(z)
        for k, ref in enumerate((a0, a1, a2, qb, kb, vb, mq)):
            ref[...] = p_scr[:, offs[k]:offs[k + 1]]

    widths = [offs[k + 1] - offs[k] for k in range(7)]
    return _pc(
        body, name="in_proj", grid=(s // ts,),
        in_specs=[_row(ts, D_MODEL), _res((1, D_MODEL)), _res(w_in.shape), _res(w_gate.shape), _res(b_gate.shape)],
        out_specs=[_row(ts, D_MODEL)] + [_row(ts, w) for w in widths] + [_row(ts, CHIPS * ng)],
        out_shape=[_sds((s, D_MODEL), _MM)] + [_sds((s, w)) for w in widths] + [_sds((s, CHIPS * ng))],
        scratch=[pltpu.VMEM((ts, ncol), F32)])(x, g1, w_in, w_gate, b_gate)


def _k_prep(srcs, gq, gk, tabs, *, wq, wk, rows_per_gain, name):
    rows = srcs[0][0].shape[0]
    ts = min(256, rows)

    def body(q_ref, k_ref, v_ref, gq_ref, gk_ref, c_ref, sa_ref, sb_ref, qn_ref, kn_ref, vn_ref):
        c, sa, sb = c_ref[...], sa_ref[...], sb_ref[...]
        qh, _ = _seg_norm(q_ref[...], HEAD)
        qn_ref[...] = _rope(qh * gq_ref[...], c, sa, sb).astype(_MM)
        kh, _ = _seg_norm(k_ref[...], HEAD)
        kn_ref[...] = _rope(kh * gk_ref[...], c, sa, sb).astype(_MM)
        vn_ref[...] = v_ref[...].astype(_MM)

    gspec = lambda w: pl.BlockSpec((None, 1, w), lambda i: ((i * ts) // rows_per_gain, 0, 0))
    return _pc(
        body, name=name, grid=(rows // ts,),
        in_specs=[_row(ts, wq, srcs[0][1]), _row(ts, wk, srcs[1][1]), _row(ts, wk, srcs[2][1]),
                  gspec(wq), gspec(wk)] + [_row(ts, 128)] * 3,
        out_specs=[_row(ts, wq), _row(ts, wk), _row(ts, wk)],
        out_shape=[_sds((rows, wq), _MM), _sds((rows, wk), _MM), _sds((rows, wk), _MM)])(
            srcs[0][0], srcs[1][0], srcs[2][0], gq, gk, *tabs)


def _first_flag(b, segs, nb):
    first = b >= nb
    for k, (start, period) in enumerate(segs):
        end = segs[k + 1][0] if k + 1 < len(segs) else nb
        first = first | ((b >= start) & (b < end) & (lax.rem(b - start, jnp.int32(period)) == 0))
    return first


def _band_masks():
    qi = lax.broadcasted_iota(jnp.int32, (BLK, BLK), 0)
    kj = lax.broadcasted_iota(jnp.int32, (BLK, BLK), 1)
    return qi, kj


def _k_band_fwd(qn, kn, vn, *, hq, hk, max_dist, segs, sink, name):
    rows = qn.shape[0]
    nb = rows // BLK
    grp = hq // hk
    wq, wk = hq * HEAD, hk * HEAD
    scale = HEAD ** -0.5

    def body(*refs):
        if sink is None:
            q_ref, kc_ref, kp_ref, vc_ref, vp_ref, o_ref, l_ref = refs
        else:
            q_ref, kc_ref, kp_ref, vc_ref, vp_ref, sk_ref, o_ref, l_ref = refs
        b = pl.program_id(0)
        thr = jnp.where(_first_flag(b, segs, nb), 1 << 20, BLK - max_dist)
        qi, kj = _band_masks()
        m_cur = kj <= qi
        m_prev = kj >= qi + thr
        for g in range(hk):
            ks = slice(g * HEAD, (g + 1) * HEAD)
            kc, kp, vc, vp = kc_ref[:, ks], kp_ref[:, ks], vc_ref[:, ks], vp_ref[:, ks]
            for u in range(grp):
                hs = slice((g * grp + u) * HEAD, (g * grp + u + 1) * HEAD)
                q = q_ref[:, hs]
                sc = jnp.where(m_cur, _dot_nt(q, kc) * scale, NEG)
                sp = jnp.where(m_prev, _dot_nt(q, kp) * scale, NEG)
                m = jnp.maximum(jnp.max(sc, axis=-1, keepdims=True), jnp.max(sp, axis=-1, keepdims=True))
                den = (jnp.sum(jnp.exp(sc - m), axis=-1, keepdims=True)
                       + jnp.sum(jnp.exp(sp - m), axis=-1, keepdims=True))
                lse = m + jnp.log(den)
                if sink is not None:
                    sk = sk_ref[:, hs][:, 0:1]
                    mx = jnp.maximum(lse, sk)
                    lse = mx + jnp.log(jnp.exp(lse - mx) + jnp.exp(sk - mx))
                o_ref[:, hs] = _dot(jnp.exp(sc - lse), vc) + _dot(jnp.exp(sp - lse), vp)
                l_ref[:, hs] = jnp.broadcast_to(lse, (BLK, HEAD))

    cur = lambda w: pl.BlockSpec((BLK, w), lambda i: (i, 0))
    prev = lambda w: pl.BlockSpec((BLK, w), lambda i: (jnp.maximum(i - 1, 0), 0))
    in_specs = [cur(wq), cur(wk), prev(wk), cur(wk), prev(wk)]
    args = [qn, kn, kn, vn, vn]
    if sink is not None:
        in_specs.append(_res((1, wq)))
        args.append(sink)
    return _pc(body, name=name, grid=(nb,), in_specs=in_specs, out_specs=[cur(wq), cur(wq)],
               out_shape=[_sds((rows, wq)), _sds((rows, wq))])(*args)


def _k_memkv(mem, mem_norm, w_kv, m_k_norm):
    n = mem.shape[0]

    def body(m_ref, g_ref, w_ref, gk_ref, mn_ref, kv_ref, mk_ref, mv_ref):
        mh, _ = _rms(m_ref[...])
        mn = (mh * g_ref[...]).astype(_MM)
        mn_ref[...] = mn
        kv = jnp.dot(mn, w_ref[...], preferred_element_type=F32)
        kv_ref[...] = kv
        kh, _ = _seg_norm(kv[:, :M_W], M_HD)
        mk_ref[...] = (kh * gk_ref[...]).astype(_MM)
        mv_ref[...] = kv[:, M_W:].astype(_MM)

    return _pc(body, name="mem_kv", grid=(1,),
               in_specs=[_acc((n, D_MODEL)), _acc((1, D_MODEL)), _acc(w_kv.shape), _acc((1, M_W))],
               out_specs=[_acc((n, D_MODEL)), _acc((n, 2 * M_W)), _acc((n, M_W)), _acc((n, M_W))],
               out_shape=[_sds((n, D_MODEL), _MM), _sds((n, 2 * M_W)), _sds((n, M_W), _MM), _sds((n, M_W), _MM)])(
                   mem, mem_norm, w_kv, m_k_norm)


def _mem_probs(q, mk):
    sc = _dot_nt(q, mk) * (M_HD ** -0.5)
    e = jnp.exp(sc - jnp.max(sc, axis=-1, keepdims=True))
    return e / jnp.sum(e, axis=-1, keepdims=True)


def _k_mem_fwd(m_q, gq, mk, mv):
    s = m_q.shape[0]
    n = mk.shape[0]
    ts = min(256, s)

    def body(q_ref, g_ref, mk_ref, mv_ref, o_ref):
        qh, _ = _seg_norm(q_ref[...], M_HD)
        qn = (qh * g_ref[...]).astype(_MM)
        for h in range(M_HEADS):
            hs = slice(h * M_HD, (h + 1) * M_HD)
            o_ref[:, hs] = _dot(_mem_probs(qn[:, hs], mk_ref[:, hs]), mv_ref[:, hs])

    return _pc(body, name="mem_attn", grid=(s // ts,),
               in_specs=[_row(ts, M_W), _res((1, M_W)), _res((n, M_W)), _res((n, M_W))],
               out_specs=[_row(ts, M_W)], out_shape=[_sds((s, M_W))])(m_q, gq, mk, mv)[0]


def _group_weights(l0, l1, l2):
    m = jnp.maximum(jnp.maximum(l0, l1), l2)
    e0, e1, e2 = jnp.exp(l0 - m), jnp.exp(l1 - m), jnp.exp(l2 - m)
    inv = 1.0 / (e0 + e1 + e2)
    return e0 * inv, e1 * inv, e2 * inv


def _branch_products(oa, ob, om, woa_ref, wob_ref, wom_ref, j):
    return _dot(oa, woa_ref[j]), _dot(ob, wob_ref[j]), _dot(om, wom_ref[j])


def _k_merge(og, lg, o_b, o_m, gates, x, w_oa, w_ob, w_om, w_out, g2):
    s = x.shape[0]
    ts = min(256, s)
    nc = w_oa.shape[2]

    def body(o0, o1, o2, l0, l1, l2, ob_ref, om_ref, gt_ref, x_ref, woa, wob, wom, wout, g_ref,
             oa_ref, mer_ref, x1_ref, h2_ref, m_scr):
        w0, w1, w2 = _group_weights(l0[...], l1[...], l2[...])
        oa = w0 * o0[...] + w1 * o1[...] + w2 * o2[...]
        oa_ref[...] = oa
        ob, om = ob_ref[...], om_ref[...]
        for j in range(CHIPS):
            pa, pb, pm = _branch_products(oa, ob, om, woa, wob, wom, j)
            cs = lambda br: slice(br * D_MODEL + j * nc, br * D_MODEL + (j + 1) * nc)
            m_scr[:, j * nc:(j + 1) * nc] = gt_ref[:, cs(0)] * pa + gt_ref[:, cs(1)] * pb + gt_ref[:, cs(2)] * pm
        mer = m_scr[...].astype(_MM)
        mer_ref[...] = mer
        x1 = x_ref[...] + jnp.dot(mer, wout[...], preferred_element_type=F32)
        x1_ref[...] = x1
        xh, _ = _rms(x1)
        h2_ref[...] = (xh * g_ref[...]).astype(_MM)

    return _pc(
        body, name="merge_out", grid=(s // ts,),
        in_specs=[_row(ts, A_W)] * 6 + [_row(ts, B_QH * HEAD), _row(ts, M_W), _row(ts, 3 * D_MODEL), _row(ts, D_MODEL),
                                         _res(w_oa.shape), _res(w_ob.shape), _res(w_om.shape), _res(w_out.shape),
                                         _res((1, D_MODEL))],
        out_specs=[_row(ts, A_W), _row(ts, D_MODEL), _row(ts, D_MODEL), _row(ts, D_MODEL)],
        out_shape=[_sds((s, A_W)), _sds((s, D_MODEL), _MM), _sds((s, D_MODEL)), _sds((s, D_MODEL), _MM)],
        scratch=[pltpu.VMEM((ts, D_MODEL), F32)])(*og, *lg, o_b, o_m, gates, x, w_oa, w_ob, w_om, w_out, g2)


def _k_up(h2, w_up):
    s = h2.shape[0]
    ts = min(256, s)
    nu = w_up.shape[2]

    def body(h_ref, w_ref, u_ref):
        h = h_ref[...]
        for j in range(CHIPS):
            u_ref[:, j * nu:(j + 1) * nu] = jnp.dot(h, w_ref[j], preferred_element_type=F32)

    return _pc(body, name="up_proj", grid=(s // ts,), in_specs=[_row(ts, D_MODEL), _res(w_up.shape)],
               out_specs=[_row(ts, CHIPS * nu)], out_shape=[_sds((s, CHIPS * nu))])(h2, w_up)[0]


def _shift_down(v, halo, k):
    ts = v.shape[0]
    row = lax.broadcasted_iota(jnp.int32, v.shape, 0)
    out = pltpu.roll(v, k, 0)
    for r in range(k):
        out = jnp.where(row == r, halo[8 - k + r:8 - k + r + 1, :], out)
    return out


def _shift_up(v, halo, k):
    ts = v.shape[0]
    row = lax.broadcasted_iota(jnp.int32, v.shape, 0)
    out = pltpu.roll(v, ts - k, 0)
    for r in range(k):
        out = jnp.where(row == ts - k + r, halo[r:r + 1, :], out)
    return out


def _k_ffn(u, conv_w, conv_b, w_down, x1, target):
    s = u.shape[0]
    ts = min(128, s)
    nu = conv_w.shape[2]
    half = CHIPS // 2

    def body(u_ref, uh_ref, cw_ref, cb_ref, wd_ref, x1_ref, t_ref, dy_ref, f_ref, dc_ref, loss_ref, c_scr, f_scr):
        i = pl.program_id(0)
        halo = jnp.where(i > 0, uh_ref[...], 0.0)
        for j in range(CHIPS):
            cs = slice(j * nu, (j + 1) * nu)
            uj = u_ref[:, cs]
            hj = halo[:, cs]
            c_scr[:, cs] = (cb_ref[:, cs] + cw_ref[j, 0:1, :] * _shift_down(uj, hj, 2)
                            + cw_ref[j, 1:2, :] * _shift_down(uj, hj, 1) + cw_ref[j, 2:3, :] * uj)
        for j in range(half):
            a = c_scr[:, j * nu:(j + 1) * nu]
            g = c_scr[:, (half + j) * nu:(half + j + 1) * nu]
            f_scr[:, j * nu:(j + 1) * nu] = (a * _sigmoid(a) * g).astype(_MM)
        f = f_scr[...]
        f_ref[...] = f
        y = x1_ref[...] + jnp.dot(f, wd_ref[...], preferred_element_type=F32)
        err = y - t_ref[...]
        dy = err * (1.0 / D_MODEL)
        dy_ref[...] = dy

        @pl.when(i == 0)
        def _():
            loss_ref[...] = jnp.zeros_like(loss_ref)

        loss_ref[...] += _sum8(err * err)
        df = _dot_nt(dy, wd_ref[...])
        for j in range(half):
            a = c_scr[:, j * nu:(j + 1) * nu]
            g = c_scr[:, (half + j) * nu:(half + j + 1) * nu]
            sa = _sigmoid(a)
            dfj = df[:, j * nu:(j + 1) * nu]
            dc_ref[:, j * nu:(j + 1) * nu] = dfj * g * (sa * (1.0 + a * (1.0 - sa)))
            dc_ref[:, (half + j) * nu:(half + j + 1) * nu] = dfj * (a * sa)

    wide = CHIPS * nu
    return _pc(
        body, name="conv_ffn", grid=(s // ts,),
        in_specs=[_row(ts, wide), pl.BlockSpec((8, wide), lambda i: (jnp.maximum(i * (ts // 8) - 1, 0), 0)),
                  _res(conv_w.shape), _res((1, wide)), _res(w_down.shape), _row(ts, D_MODEL), _row(ts, D_MODEL)],
        out_specs=[_row(ts, D_MODEL), _row(ts, D_FF), _row(ts, wide), _acc((8, D_MODEL))],
        out_shape=[_sds((s, D_MODEL)), _sds((s, D_FF), _MM), _sds((s, wide)), _sds((8, D_MODEL))],
        scratch=[pltpu.VMEM((ts, wide), F32), pltpu.VMEM((ts, D_FF), _MM)])(u, u, conv_w, conv_b, w_down, x1, target)


def _k_conv_bwd(dc, u, conv_w, w_up, x1, g2, dy):
    s = u.shape[0]
    ts = min(128, s)
    nu = conv_w.shape[2]
    wide = CHIPS * nu
    last = s // ts - 1

    def body(dc_ref, dn_ref, u_ref, uh_ref, cw_ref, wu_ref, x1_ref, g_ref, dy_ref,
             dx1_ref, du_ref, cacc_ref, gacc_ref):
        i = pl.program_id(0)

        @pl.when(i == 0)
        def _():
            cacc_ref[...] = jnp.zeros_like(cacc_ref)
            gacc_ref[...] = jnp.zeros_like(gacc_ref)

        uhalo = jnp.where(i > 0, uh_ref[...], 0.0)
        dhalo = jnp.where(i < last, dn_ref[...], 0.0)
        dh2 = jnp.zeros((ts, D_MODEL), F32)
        for j in range(CHIPS):
            cs = slice(j * nu, (j + 1) * nu)
            dcj, uj = dc_ref[:, cs], u_ref[:, cs]
            cacc_ref[0, :, cs] += _sum8(dcj)
            cacc_ref[1, :, cs] += _sum8(dcj * _shift_down(uj, uhalo[:, cs], 2))
            cacc_ref[2, :, cs] += _sum8(dcj * _shift_down(uj, uhalo[:, cs], 1))
            cacc_ref[3, :, cs] += _sum8(dcj * uj)
            du = (cw_ref[j, 2:3, :] * dcj + cw_ref[j, 1:2, :] * _shift_up(dcj, dhalo[:, cs], 1)
                  + cw_ref[j, 0:1, :] * _shift_up(dcj, dhalo[:, cs], 2)).astype(_MM)
            du_ref[:, cs] = du
            dh2 = dh2 + _dot_nt(du, wu_ref[j])
        xh, r = _rms(x1_ref[...])
        gacc_ref[...] += _sum8(dh2 * xh)
        dx1_ref[...] = dy_ref[...] + _rms_bwd(dh2, xh, r, g_ref[...])

    return _pc(
        body, name="conv_up_bwd", grid=(s // ts,),
        in_specs=[_row(ts, wide),
                  pl.BlockSpec((8, wide), lambda i: (jnp.minimum((i + 1) * (ts // 8), s // 8 - 1), 0)),
                  _row(ts, wide), pl.BlockSpec((8, wide), lambda i: (jnp.maximum(i * (ts // 8) - 1, 0), 0)),
                  _res(conv_w.shape), _res(w_up.shape), _row(ts, D_MODEL), _res((1, D_MODEL)), _row(ts, D_MODEL)],
        out_specs=[_row(ts, D_MODEL), _row(ts, wide), _acc((4, 8, wide)), _acc((8, D_MODEL))],
        out_shape=[_sds((s, D_MODEL)), _sds((s, wide), _MM), _sds((4, 8, wide)), _sds((8, D_MODEL))])(
            dc, dc, u, u, conv_w, w_up, x1, g2, dy)


def _k_merge_bwd(dx1, og, lg, o_a, o_b, o_m, gates, w_oa, w_ob, w_om, w_out):
    s = dx1.shape[0]
    ts = min(256, s)
    nc = w_oa.shape[2]

    def body(dx_ref, o0, o1, o2, l0, l1, l2, oa_ref, ob_ref, om_ref, gt_ref, woa, wob, wom, wout,
             dgp_ref, dpa_ref, dpb_ref, dpm_ref, dog0, dog1, dog2, dl0, dl1, dl2, dob_ref, dom_ref, bacc_ref):
        i = pl.program_id(0)

        @pl.when(i == 0)
        def _():
            bacc_ref[...] = jnp.zeros_like(bacc_ref)

        dmer = _dot_nt(dx_ref[...], wout[...])
        oa, ob, om = oa_ref[...], ob_ref[...], om_ref[...]
        doa = jnp.zeros((ts, A_W), F32)
        dob = jnp.zeros((ts, B_QH * HEAD), F32)
        dom = jnp.zeros((ts, M_W), F32)
        for j in range(CHIPS):
            prods = _branch_products(oa, ob, om, woa, wob, wom, j)
            dmj = dmer[:, j * nc:(j + 1) * nc]
            dps = []
            for br, (p, dref) in enumerate(zip(prods, (dpa_ref, dpb_ref, dpm_ref))):
                cs = slice(br * D_MODEL + j * nc, br * D_MODEL + (j + 1) * nc)
                gt = gt_ref[:, cs]
                dgp = dmj * p * gt * (1.0 - gt)
                dgp_ref[:, cs] = dgp.astype(_MM)
                bacc_ref[:, cs] += _sum8(dgp)
                dp = (dmj * gt).astype(_MM)
                dref[:, j * nc:(j + 1) * nc] = dp
                dps.append(dp)
            doa = doa + _dot_nt(dps[0], woa[j])
            dob = dob + _dot_nt(dps[1], wob[j])
            dom = dom + _dot_nt(dps[2], wom[j])
        dob_ref[...] = dob
        dom_ref[...] = dom
        ws = _group_weights(l0[...], l1[...], l2[...])
        dsum = _seg_mean(doa * oa, HEAD) * float(HEAD)
        for w, dref, lref in zip(ws, (dog0, dog1, dog2), (dl0, dl1, dl2)):
            dref[...] = w * doa
            lref[...] = w * dsum

    return _pc(
        body, name="merge_out_bwd", grid=(s // ts,),
        in_specs=[_row(ts, D_MODEL)] + [_row(ts, A_W)] * 7 + [_row(ts, B_QH * HEAD), _row(ts, M_W), _row(ts, 3 * D_MODEL),
                                                              _res(w_oa.shape), _res(w_ob.shape), _res(w_om.shape),
                                                              _res(w_out.shape)],
        out_specs=[_row(ts, 3 * D_MODEL)] + [_row(ts, D_MODEL)] * 3 + [_row(ts, A_W)] * 6
        + [_row(ts, B_QH * HEAD), _row(ts, M_W), _acc((8, 3 * D_MODEL))],
        out_shape=[_sds((s, 3 * D_MODEL), _MM)] + [_sds((s, D_MODEL), _MM)] * 3 + [_sds((s, A_W))] * 6
        + [_sds((s, B_QH * HEAD)), _sds((s, M_W)), _sds((8, 3 * D_MODEL))])(
            dx1, *og, *lg, o_a, o_b, o_m, gates, w_oa, w_ob, w_om, w_out)


def _k_mem_bwd(m_q, gq, mk, mv, o_m, do_m):
    s = m_q.shape[0]
    n = mk.shape[0]
    ts = min(256, s)
    scale = M_HD ** -0.5

    def body(q_ref, g_ref, mk_ref, mv_ref, o_ref, do_ref, dq_ref, dmk_ref, dmv_ref, gacc_ref):
        i = pl.program_id(0)

        @pl.when(i == 0)
        def _():
            dmk_ref[...] = jnp.zeros_like(dmk_ref)
            dmv_ref[...] = jnp.zeros_like(dmv_ref)
            gacc_ref[...] = jnp.zeros_like(gacc_ref)

        gain = g_ref[...]
        qh, r = _seg_norm(q_ref[...], M_HD)
        qn = (qh * gain).astype(_MM)
        do = do_ref[...]
        delta = _seg_mean(do * o_ref[...], M_HD) * float(M_HD)
        dqn = []
        for h in range(M_HEADS):
            hs = slice(h * M_HD, (h + 1) * M_HD)
            p = _mem_probs(qn[:, hs], mk_ref[:, hs])
            dp = _dot_nt(do[:, hs], mv_ref[:, hs])
            ds = (p * (dp - delta[:, hs][:, 0:1]) * scale).astype(_MM)
            dqn.append(_dot(ds, mk_ref[:, hs]))
            dmk_ref[:, hs] += _dot_tn(ds, qn[:, hs])
            dmv_ref[:, hs] += _dot_tn(p, do[:, hs])
        dqn = jnp.concatenate(dqn, axis=1)
        gacc_ref[...] += _sum8(dqn * qh)
        z = dqn * gain
        dq_ref[...] = (r * (z - qh * _seg_mean(z * qh, M_HD))).astype(_MM)

    return _pc(
        body, name="mem_attn_bwd", grid=(s // ts,),
        in_specs=[_row(ts, M_W), _res((1, M_W)), _res((n, M_W)), _res((n, M_W)), _row(ts, M_W), _row(ts, M_W)],
        out_specs=[_row(ts, M_W), _acc((n, M_W)), _acc((n, M_W)), _acc((8, M_W))],
        out_shape=[_sds((s, M_W), _MM), _sds((n, M_W)), _sds((n, M_W)), _sds((8, M_W))])(m_q, gq, mk, mv, o_m, do_m)


def _k_memkv_bwd(mem, mem_norm, w_kv, m_k_norm, mem_n, kv, dmk, dmv):
    n = mem.shape[0]

    def body(m_ref, g_ref, w_ref, gk_ref, mn_ref, kv_ref, dmk_ref, dmv_ref, dw_ref, dg_ref, dgk_ref):
        gk = gk_ref[...]
        kh, r = _seg_norm(kv_ref[:, :M_W], M_HD)
        dmk = dmk_ref[...]
        dgk_ref[...] = _sum8(dmk * kh)
        z = dmk * gk
        dk = r * (z - kh * _seg_mean(z * kh, M_HD))
        dkv = jnp.concatenate([dk, dmv_ref[...]], axis=1).astype(_MM)
        dw_ref[...] = _dot_tn(mn_ref[...], dkv)
        dmn = _dot_nt(dkv, w_ref[...])
        mh, _ = _rms(m_ref[...])
        dg_ref[...] = _sum8(dmn * mh)

    return _pc(body, name="mem_kv_bwd", grid=(1,),
               in_specs=[_acc((n, D_MODEL)), _acc((1, D_MODEL)), _acc(w_kv.shape), _acc((1, M_W)), _acc((n, D_MODEL)),
                         _acc((n, 2 * M_W)), _acc((n, M_W)), _acc((n, M_W))],
               out_specs=[_acc(w_kv.shape), _acc((8, D_MODEL)), _acc((8, M_W))],
               out_shape=[_sds(w_kv.shape), _sds((8, D_MODEL)), _sds((8, M_W))])(
                   mem, mem_norm, w_kv, m_k_norm, mem_n, kv, dmk, dmv)


def _k_band_bwd(qn, kn, vn, do, lse, dl_or_o, *, hq, hk, max_dist, segs, sink, name):
    rows = qn.shape[0]
    nb = rows // BLK
    grp = hq // hk
    wq, wk = hq * HEAD, hk * HEAD
    scale = HEAD ** -0.5

    def body(*refs):
        (qb_ref, qx_ref, kb_ref, kp_ref, vb_ref, vp_ref, dob_ref, dox_ref, lb_ref, lx_ref, eb_ref, ex_ref) = refs[:12]
        if sink is None:
            dq_ref, dk_ref, dv_ref = refs[12:]
        else:
            sk_ref, dq_ref, dk_ref, dv_ref, sacc_ref = refs[12:]
        b = pl.program_id(0)
        thr_b = jnp.where(_first_flag(b, segs, nb), 1 << 20, BLK - max_dist)
        thr_x = jnp.where(_first_flag(b + 1, segs, nb), 1 << 20, BLK - max_dist)
        qi, kj = _band_masks()
        m_cur = kj <= qi
        m_prev = kj >= qi + thr_b
        m_next = kj >= qi + thr_x
        if sink is not None:
            @pl.when(b == 0)
            def _():
                sacc_ref[...] = jnp.zeros_like(sacc_ref)

        def tile(q, k, v, do, l, dlt, mask):
            p = jnp.exp(jnp.where(mask, _dot_nt(q, k) * scale, NEG) - l)
            ds = p * (_dot_nt(do, v) - dlt) * scale
            return p, ds

        for g in range(hk):
            ks = slice(g * HEAD, (g + 1) * HEAD)
            kb, kp, vb, vp = kb_ref[:, ks], kp_ref[:, ks], vb_ref[:, ks], vp_ref[:, ks]
            dk = jnp.zeros((BLK, HEAD), F32)
            dv = jnp.zeros((BLK, HEAD), F32)
            for u in range(grp):
                hs = slice((g * grp + u) * HEAD, (g * grp + u + 1) * HEAD)
                q, qx, do, dox = qb_ref[:, hs], qx_ref[:, hs], dob_ref[:, hs], dox_ref[:, hs]
                l, lx = lb_ref[:, hs][:, 0:1], lx_ref[:, hs][:, 0:1]
                if sink is None:
                    dlt, dltx = eb_ref[:, hs][:, 0:1], ex_ref[:, hs][:, 0:1]
                else:
                    dlt = jnp.sum(do * eb_ref[:, hs], axis=-1, keepdims=True)
                    dltx = jnp.sum(dox * ex_ref[:, hs], axis=-1, keepdims=True)
                    sk = sk_ref[:, hs][:, 0:1]
                    sacc_ref[:, hs] += jnp.broadcast_to(-jnp.exp(sk - l) * dlt, (BLK, HEAD))
                pc, dsc = tile(q, kb, vb, do, l, dlt, m_cur)
                _, dsp = tile(q, kp, vp, do, l, dlt, m_prev)
                px, dsx = tile(qx, kb, vb, dox, lx, dltx, m_next)
                dq_ref[:, hs] = _dot(dsc, kb) + _dot(dsp, kp)
                dk = dk + _dot_tn(dsc, q) + _dot_tn(dsx, qx)
                dv = dv + _dot_tn(pc, do) + _dot_tn(px, dox)
            dk_ref[:, ks] = dk
            dv_ref[:, ks] = dv.astype(_MM)

    cur = lambda w: pl.BlockSpec((BLK, w), lambda i: (i, 0))
    prev = lambda w: pl.BlockSpec((BLK, w), lambda i: (jnp.maximum(i - 1, 0), 0))
    nxt = lambda w: pl.BlockSpec((BLK, w), lambda i: (jnp.minimum(i + 1, nb - 1), 0))
    in_specs = [cur(wq), nxt(wq), cur(wk), prev(wk), cur(wk), prev(wk), cur(wq), nxt(wq), cur(wq), nxt(wq), cur(wq), nxt(wq)]
    args = [qn, qn, kn, kn, vn, vn, do, do, lse, lse, dl_or_o, dl_or_o]
    out_specs = [cur(wq), cur(wk), cur(wk)]
    out_shape = [_sds((rows, wq)), _sds((rows, wk)), _sds((rows, wk), _MM)]
    if sink is not None:
        in_specs.append(_res((1, wq)))
        args.append(sink)
        out_specs.append(_acc((BLK, wq)))
        out_shape.append(_sds((BLK, wq)))
    return _pc(body, name=name, grid=(nb,), in_specs=in_specs, out_specs=out_specs, out_shape=out_shape)(*args)


def _k_prep_bwd(srcs, dqn, dkn, gq, gk, tabs, *, wq, wk, rows_per_gain, name):
    rows = dqn.shape[0]
    ts = min(256, rows)
    ngain = gq.shape[0]

    def body(q_ref, k_ref, dq_ref, dk_ref, gq_ref, gk_ref, c_ref, sa_ref, sb_ref, oq_ref, ok_ref, aq_ref, ak_ref):
        i = pl.program_id(0)

        @pl.when(lax.rem(i * ts, rows_per_gain) == 0)
        def _():
            aq_ref[...] = jnp.zeros_like(aq_ref)
            ak_ref[...] = jnp.zeros_like(ak_ref)

        c, sa, sb = c_ref[...], sa_ref[...], sb_ref[...]
        for x_ref, d_ref, g_ref, o_ref, a_ref in ((q_ref, dq_ref, gq_ref, oq_ref, aq_ref),
                                                   (k_ref, dk_ref, gk_ref, ok_ref, ak_ref)):
            xh, r = _seg_norm(x_ref[...], HEAD)
            dt = _rope_bwd(d_ref[...], c, sa, sb)
            a_ref[...] += _sum8(dt * xh)
            z = dt * g_ref[...]
            o_ref[...] = (r * (z - xh * _seg_mean(z * xh, HEAD))).astype(_MM)

    gspec = lambda w: pl.BlockSpec((None, 1, w), lambda i: ((i * ts) // rows_per_gain, 0, 0))
    aspec = lambda w: pl.BlockSpec((None, 8, w), lambda i: ((i * ts) // rows_per_gain, 0, 0))
    return _pc(
        body, name=name, grid=(rows // ts,),
        in_specs=[_row(ts, wq, srcs[0][1]), _row(ts, wk, srcs[1][1]), _row(ts, wq), _row(ts, wk), gspec(wq), gspec(wk)]
        + [_row(ts, 128)] * 3,
        out_specs=[_row(ts, wq), _row(ts, wk), aspec(wq), aspec(wk)],
        out_shape=[_sds((rows, wq), _MM), _sds((rows, wk), _MM), _sds((ngain, 8, wq)), _sds((ngain, 8, wk))])(
            srcs[0][0], srcs[1][0], dqn, dkn, gq, gk, *tabs)


def _k_in_bwd(pieces, dgp, x, g1, dx1, w_in, w_gate):
    s = x.shape[0]
    ts = min(256, s)
    nin, ng = w_in.shape[2], w_gate.shape[2]
    widths = [p.shape[1] for p in pieces]
    ncol = sum(widths)

    def body(*refs):
        p_refs = refs[:len(pieces)]
        dgp_ref, x_ref, g_ref, dx1_ref, wi_ref, wg_ref, gx_ref, dpj_ref, gacc_ref = refs[len(pieces):]
        i = pl.program_id(0)

        @pl.when(i == 0)
        def _():
            gacc_ref[...] = jnp.zeros_like(gacc_ref)

        off = 0
        for p_ref, w in zip(p_refs, widths):
            dpj_ref[:, off:off + w] = p_ref[...]
            off += w
        dh = jnp.zeros((ts, D_MODEL), F32)
        for j in range(CHIPS):
            dh = dh + _dot_nt(dpj_ref[:, j * nin:(j + 1) * nin], wi_ref[j])
            dh = dh + _dot_nt(dgp_ref[:, j * ng:(j + 1) * ng], wg_ref[j])
        xh, r = _rms(x_ref[...])
        gacc_ref[...] += _sum8(dh * xh)
        gx_ref[...] = dx1_ref[...] + _rms_bwd(dh, xh, r, g_ref[...])

    return _pc(
        body, name="in_proj_bwd", grid=(s // ts,),
        in_specs=[_row(ts, w) for w in widths] + [_row(ts, CHIPS * ng), _row(ts, D_MODEL), _res((1, D_MODEL)),
                                                  _row(ts, D_MODEL), _res(w_in.shape), _res(w_gate.shape)],
        out_specs=[_row(ts, D_MODEL), _row(ts, ncol), _acc((8, D_MODEL))],
        out_shape=[_sds((s, D_MODEL)), _sds((s, ncol), _MM), _sds((8, D_MODEL))])(*pieces, dgp, x, g1, dx1, w_in, w_gate)


def _k_wgrad(a, b, *, nblk, stacked, name):
    s, k = a.shape
    n = b.shape[1]
    nb = n // nblk
    ts = min(512, s)

    def body(a_ref, b_ref, o_ref):
        @pl.when(pl.program_id(1) == 0)
        def _():
            o_ref[...] = jnp.zeros_like(o_ref)

        o_ref[...] += _dot_tn(a_ref[...], b_ref[...])

    if stacked:
        out_spec, out_shape = pl.BlockSpec((None, k, nb), lambda g, t: (g, 0, 0)), _sds((nblk, k, nb))
    else:
        out_spec, out_shape = pl.BlockSpec((k, nb), lambda g, t: (0, g)), _sds((k, n))
    return _pc(body, name=name, grid=(nblk, s // ts),
               in_specs=[pl.BlockSpec((ts, k), lambda g, t: (t, 0)), pl.BlockSpec((ts, nb), lambda g, t: (t, g))],
               out_specs=[out_spec], out_shape=[out_shape])(a, b)[0]


def _to_res(t, d):
    s, c = t.shape
    return t if d == 1 else t.reshape(s // d, d, c).transpose(1, 0, 2).reshape(s, c)


def _from_res(t, d):
    s, c = t.shape
    return t if d == 1 else t.reshape(d, s // d, c).transpose(1, 0, 2).reshape(s, c)


def _tile_gain(g, heads):
    return jnp.tile(g, (1,) * (g.ndim - 1) + (heads,))[..., None, :]


def _local_step(x, mem, pos, target, small, wts):
    s = x.shape[0]
    nblk = s // BLK
    g1, g2 = small["attn_norm"], small["ffn_norm"]

    pos_rows = jnp.concatenate([_to_res(pos[:, None], d)[:, 0] for _, d in A_GROUPS] + [pos])
    tabs = _rope_tables(pos_rows)
    tabs_a = tuple(t[:3 * s] for t in tabs)
    tabs_b = tuple(t[3 * s:] for t in tabs)

    h, qa0, qa1, qa2, q_b, k_b, v_b, m_q, gates = _k_in(x, g1, wts["w_in"], wts["w_gate"], small["b_gate"])

    qkv_a = jnp.concatenate([_to_res(t, d) for t, (_, d) in zip((qa0, qa1, qa2), A_GROUPS)], axis=0)
    gq_a = _tile_gain(small["a_q_norm"], A_HEADS)
    gk_a = _tile_gain(small["a_k_norm"], A_HEADS)
    src_a = ((qkv_a, 0), (qkv_a, 1), (qkv_a, 2))
    qn_a, kn_a, vn_a = _k_prep(src_a, gq_a, gk_a, tabs_a, wq=A_W, wk=A_W, rows_per_gain=s, name="prep_a")
    segs_a = tuple((gi * nblk, nblk // d) for gi, (_, d) in enumerate(A_GROUPS))
    o_res, l_res = _k_band_fwd(qn_a, kn_a, vn_a, hq=A_HEADS, hk=A_HEADS, max_dist=BLK, segs=segs_a, sink=None,
                               name="attn_a")
    og = [_from_res(o_res[gi * s:(gi + 1) * s], d) for gi, (_, d) in enumerate(A_GROUPS)]
    lg = [_from_res(l_res[gi * s:(gi + 1) * s], d) for gi, (_, d) in enumerate(A_GROUPS)]

    gq_b = _tile_gain(small["b_q_norm"], B_QH)
    gk_b = _tile_gain(small["b_k_norm"], B_KVH)
    src_b = ((q_b, 0), (k_b, 0), (v_b, 0))
    qn_b, kn_b, vn_b = _k_prep(src_b, gq_b, gk_b, tabs_b, wq=B_QH * HEAD, wk=B_KVH * HEAD, rows_per_gain=s,
                               name="prep_b")
    sink_x = jnp.repeat(small["b_sinks"], HEAD)[None, :]
    segs_b = ((0, nblk),)
    o_b, l_b = _k_band_fwd(qn_b, kn_b, vn_b, hq=B_QH, hk=B_KVH, max_dist=B_WINDOW - 1, segs=segs_b, sink=sink_x,
                           name="attn_b")

    gq_m = _tile_gain(small["m_q_norm"], M_HEADS)[0]
    gk_m = _tile_gain(small["m_k_norm"], M_HEADS)[0]
    mem_n, kv, mk, mv = _k_memkv(mem, small["mem_norm"], wts["w_mem_kv"], gk_m)
    o_m = _k_mem_fwd(m_q, gq_m, mk, mv)

    o_a, merged, x1, h2 = _k_merge(og, lg, o_b, o_m, gates, x, wts["w_o_a"], wts["w_o_b"], wts["w_o_m"],
                                   wts["w_out"], g2)
    u = _k_up(h2, wts["w_up"])
    dy, f, dc, loss_acc = _k_ffn(u, wts["conv_w"], small["conv_b"], wts["w_down"], x1, target)
    loss = (0.5 / D_MODEL) * jnp.sum(loss_acc)

    dx1, du, cacc, g2acc = _k_conv_bwd(dc, u, wts["conv_w"], wts["w_up"], x1, g2, dy)
    (dgp, dp_a, dp_b, dp_m, dog0, dog1, dog2, dl0, dl1, dl2, do_b, do_m, bacc) = _k_merge_bwd(
        dx1, og, lg, o_a, o_b, o_m, gates, wts["w_o_a"], wts["w_o_b"], wts["w_o_m"], wts["w_out"])

    dq_m, dmk, dmv, gqm_acc = _k_mem_bwd(m_q, gq_m, mk, mv, o_m, do_m)
    dw_kv, gmem_acc, gkm_acc = _k_memkv_bwd(mem, small["mem_norm"], wts["w_mem_kv"], gk_m, mem_n, kv, dmk, dmv)

    dq_bn, dk_bn, dv_b, sacc = _k_band_bwd(qn_b, kn_b, vn_b, do_b, l_b, o_b, hq=B_QH, hk=B_KVH,
                                           max_dist=B_WINDOW - 1, segs=segs_b, sink=sink_x, name="attn_b_bwd")
    dq_b, dk_b, gqb_acc, gkb_acc = _k_prep_bwd(src_b, dq_bn, dk_bn, gq_b, gk_b, tabs_b, wq=B_QH * HEAD,
                                               wk=B_KVH * HEAD, rows_per_gain=s, name="prep_b_bwd")

    do_res = jnp.concatenate([_to_res(t, d) for t, (_, d) in zip((dog0, dog1, dog2), A_GROUPS)], axis=0)
    dl_res = jnp.concatenate([_to_res(t, d) for t, (_, d) in zip((dl0, dl1, dl2), A_GROUPS)], axis=0)
    dq_an, dk_an, dv_a = _k_band_bwd(qn_a, kn_a, vn_a, do_res, l_res, dl_res, hq=A_HEADS, hk=A_HEADS, max_dist=BLK,
                                     segs=segs_a, sink=None, name="attn_a_bwd")
    dq_a, dk_a, gqa_acc, gka_acc = _k_prep_bwd(src_a, dq_an, dk_an, gq_a, gk_a, tabs_a, wq=A_W, wk=A_W,
                                               rows_per_gain=s, name="prep_a_bwd")
    pieces = []
    for gi, (_, d) in enumerate(A_GROUPS):
        rs = slice(gi * s, (gi + 1) * s)
        pieces += [_from_res(t[rs], d) for t in (dq_a, dk_a, dv_a)]
    pieces += [dq_b, dk_b, dv_b, dq_m]
    grad_x, dproj, g1acc = _k_in_bwd(pieces, dgp, x, g1, dx1, wts["w_in"], wts["w_gate"])

    big = {
        "w_in": _k_wgrad(h, dproj, nblk=CHIPS, stacked=True, name="dw_in"),
        "w_gate": _k_wgrad(h, dgp, nblk=CHIPS, stacked=True, name="dw_gate"),
        "w_o_a": _k_wgrad(o_a, dp_a, nblk=CHIPS, stacked=True, name="dw_o_a"),
        "w_o_b": _k_wgrad(o_b, dp_b, nblk=CHIPS, stacked=True, name="dw_o_b"),
        "w_o_m": _k_wgrad(o_m, dp_m, nblk=CHIPS, stacked=True, name="dw_o_m"),
        "w_up": _k_wgrad(h2, du, nblk=CHIPS, stacked=True, name="dw_up"),
        "w_out": _k_wgrad(merged, dx1, nblk=1, stacked=False, name="dw_out").reshape(CHIPS, -1, D_MODEL),
        "w_down": _k_wgrad(f, dy, nblk=2, stacked=False, name="dw_down").reshape(CHIPS, -1, D_MODEL),
        "w_mem_kv": dw_kv.reshape(CHIPS, -1, 2 * M_W),
    }

    def fold(acc, heads):
        v = jnp.sum(acc, axis=-2)
        return jnp.sum(v.reshape(v.shape[:-1] + (heads, -1)), axis=-2)

    csum = jnp.sum(cacc, axis=1)
    sml = {
        "attn_norm": jnp.sum(g1acc, axis=0), "a_q_norm": fold(gqa_acc, A_HEADS), "a_k_norm": fold(gka_acc, A_HEADS),
        "b_q_norm": fold(gqb_acc[0], B_QH), "b_k_norm": fold(gkb_acc[0], B_KVH),
        "b_sinks": jnp.sum(sacc, axis=0).reshape(B_QH, HEAD)[:, 0], "mem_norm": jnp.sum(gmem_acc, axis=0),
        "m_q_norm": fold(gqm_acc, M_HEADS), "m_k_norm": fold(gkm_acc, M_HEADS),
        "b_gate": jnp.sum(bacc, axis=0), "ffn_norm": jnp.sum(g2acc, axis=0),
        "conv_w": csum[1:], "conv_b": csum[0],
    }
    return loss, grad_x, big, sml


def _mesh_pos():
    return lax.axis_index("x"), lax.axis_index("y"), lax.axis_index("c")


def _chip_peers(x, y):
    return [(1 - x, y), (x, 1 - y), (1 - x, 1 - y)]


_ANY = pl.BlockSpec(memory_space=pl.ANY)


def _comm_call(body, *, name, n_in, out_shape, scratch):
    return pl.pallas_call(body, name=name, in_specs=[_ANY] * n_in, out_specs=[_ANY] * len(out_shape),
                          out_shape=out_shape, scratch_shapes=scratch)


def _gather_shards(shards):
    nt = len(shards)

    def body(*refs):
        ins, outs = refs[:nt], refs[nt:2 * nt]
        send_sems, recv_sems, loc_sems = refs[2 * nt:]
        x, y, c = _mesh_pos()
        me = 2 * x + y
        copies = []
        for t in range(nt):
            cp = pltpu.make_async_copy(ins[t], outs[t].at[me], loc_sems.at[t])
            cp.start()
            copies.append(cp)
            for k, (px, py) in enumerate(_chip_peers(x, y)):
                rc = pltpu.make_async_remote_copy(src_ref=ins[t], dst_ref=outs[t].at[me], send_sem=send_sems.at[t, k],
                                                  recv_sem=recv_sems.at[t, k], device_id=(px, py, c),
                                                  device_id_type=MESH)
                rc.start()
        for t in range(nt):
            copies[t].wait()
            for k, (px, py) in enumerate(_chip_peers(x, y)):
                pltpu.make_async_remote_copy(src_ref=ins[t], dst_ref=outs[t].at[2 * px + py],
                                             send_sem=send_sems.at[t, k], recv_sem=recv_sems.at[t, k],
                                             device_id=(px, py, c), device_id_type=MESH).wait()

    out_shape = [_sds((CHIPS,) + sh.shape, sh.dtype) for sh in shards]
    scratch = [pltpu.SemaphoreType.DMA((nt, 3)), pltpu.SemaphoreType.DMA((nt, 3)), pltpu.SemaphoreType.DMA((nt,))]
    return _comm_call(body, name="gather_weights", n_in=nt, out_shape=out_shape, scratch=scratch)(*shards)


def _pair_split(grads):
    nt = len(grads)

    def body(*refs):
        ins, mine, got = refs[:nt], refs[nt:2 * nt], refs[2 * nt:3 * nt]
        send_sems, recv_sems, loc_sems = refs[3 * nt:]
        x, y, c = _mesh_pos()
        cps = []
        for t in range(nt):
            hr = ins[t].shape[1] // 2
            keep = ins[t].at[:, pl.ds(pl.multiple_of(c * hr, 8), hr), :]
            give = ins[t].at[:, pl.ds(pl.multiple_of((1 - c) * hr, 8), hr), :]
            lc = pltpu.make_async_copy(keep, mine[t], loc_sems.at[t])
            lc.start()
            rc = pltpu.make_async_remote_copy(src_ref=give, dst_ref=got[t], send_sem=send_sems.at[t],
                                              recv_sem=recv_sems.at[t], device_id=(x, y, 1 - c), device_id_type=MESH)
            rc.start()
            cps.append((lc, rc))
        for lc, rc in cps:
            lc.wait()
            rc.wait()

    half = [_sds((CHIPS, g.shape[1] // 2, g.shape[2]), g.dtype) for g in grads]
    scratch = [pltpu.SemaphoreType.DMA((nt,)), pltpu.SemaphoreType.DMA((nt,)), pltpu.SemaphoreType.DMA((nt,))]
    outs = _comm_call(body, name="grad_pair_split", n_in=nt, out_shape=half + half, scratch=scratch)(*grads)
    return outs[:nt], outs[nt:]


def _chip_scatter(parts):
    nt = len(parts)

    def body(*refs):
        ins, outs = refs[:nt], refs[nt:2 * nt]
        send_sems, recv_sems, loc_sems = refs[2 * nt:]
        x, y, c = _mesh_pos()
        me = 2 * x + y
        cps = []
        for t in range(nt):
            lc = pltpu.make_async_copy(ins[t].at[me], outs[t].at[3], loc_sems.at[t])
            lc.start()
            cps.append(lc)
            for k, (px, py) in enumerate(_chip_peers(x, y)):
                rc = pltpu.make_async_remote_copy(src_ref=ins[t].at[2 * px + py], dst_ref=outs[t].at[k],
                                                  send_sem=send_sems.at[t, k], recv_sem=recv_sems.at[t, k],
                                                  device_id=(px, py, c), device_id_type=MESH)
                rc.start()
                cps.append(rc)
        for cp in cps:
            cp.wait()

    out_shape = [_sds(p.shape, p.dtype) for p in parts]
    scratch = [pltpu.SemaphoreType.DMA((nt, 3)), pltpu.SemaphoreType.DMA((nt, 3)), pltpu.SemaphoreType.DMA((nt,))]
    return _comm_call(body, name="grad_chip_scatter", n_in=nt, out_shape=out_shape, scratch=scratch)(*parts)


def _pair_join(halves):
    nt = len(halves)

    def body(*refs):
        ins, outs = refs[:nt], refs[nt:2 * nt]
        send_sems, recv_sems, loc_sems = refs[2 * nt:]
        x, y, c = _mesh_pos()
        cps = []
        for t in range(nt):
            hr = ins[t].shape[0]
            dst = outs[t].at[pl.ds(pl.multiple_of(c * hr, 8), hr), :]
            lc = pltpu.make_async_copy(ins[t], dst, loc_sems.at[t])
            lc.start()
            rc = pltpu.make_async_remote_copy(src_ref=ins[t], dst_ref=dst, send_sem=send_sems.at[t],
                                              recv_sem=recv_sems.at[t], device_id=(x, y, 1 - c), device_id_type=MESH)
            rc.start()
            cps.append((lc, rc, hr))
        for t, (lc, rc, hr) in enumerate(cps):
            lc.wait()
            rc.wait_send()
            other = outs[t].at[pl.ds(pl.multiple_of((1 - c) * hr, 8), hr), :]
            pltpu.make_async_remote_copy(src_ref=ins[t], dst_ref=other, send_sem=send_sems.at[t],
                                         recv_sem=recv_sems.at[t], device_id=(x, y, 1 - c),
                                         device_id_type=MESH).wait_recv()

    out_shape = [_sds((2 * hf.shape[0], hf.shape[1]), hf.dtype) for hf in halves]
    scratch = [pltpu.SemaphoreType.DMA((nt,)), pltpu.SemaphoreType.DMA((nt,)), pltpu.SemaphoreType.DMA((nt,))]
    return _comm_call(body, name="grad_pair_join", n_in=nt, out_shape=out_shape, scratch=scratch)(*halves)


def _gather_small(packed):
    n = packed.shape[0]

    def body(in_ref, out_ref, send_sems, recv_sems, loc_sem):
        x, y, c = _mesh_pos()
        me = 4 * x + 2 * y + c
        lc = pltpu.make_async_copy(in_ref, out_ref.at[me], loc_sem)
        lc.start()
        peers = []
        for k in range(1, NDEV):
            px, py, pc = x ^ (k >> 2), y ^ ((k >> 1) & 1), c ^ (k & 1)
            rc = pltpu.make_async_remote_copy(src_ref=in_ref, dst_ref=out_ref.at[me], send_sem=send_sems.at[k - 1],
                                              recv_sem=recv_sems.at[k - 1], device_id=(px, py, pc), device_id_type=MESH)
            rc.start()
            peers.append((k, px, py, pc))
        lc.wait()
        for k, px, py, pc in peers:
            pltpu.make_async_remote_copy(src_ref=in_ref, dst_ref=out_ref.at[4 * px + 2 * py + pc],
                                         send_sem=send_sems.at[k - 1], recv_sem=recv_sems.at[k - 1],
                                         device_id=(px, py, pc), device_id_type=MESH).wait()

    scratch = [pltpu.SemaphoreType.DMA((NDEV - 1,)), pltpu.SemaphoreType.DMA((NDEV - 1,)), pltpu.SemaphoreType.DMA]
    return _comm_call(body, name="gather_small_grads", n_in=1, out_shape=[_sds((NDEV, n, 128))],
                      scratch=scratch)(packed)[0]


def _row_tile(r, c):
    t = r
    while t * c * 4 > (1 << 20) and t % 16 == 0:
        t //= 2
    return t


def _k_add2(a, b, name):
    g, r, c = a.shape
    tr = _row_tile(r, c)

    def body(a_ref, b_ref, o_ref):
        o_ref[...] = a_ref[...] + b_ref[...]

    spec = pl.BlockSpec((None, tr, c), lambda i, j: (i, j, 0))
    return _pc(body, name=name, grid=(g, r // tr), in_specs=[spec, spec], out_specs=[spec],
               out_shape=[_sds(a.shape)])(a, b)[0]


def _k_sum4(a, name):
    _, r, c = a.shape
    tr = _row_tile(r, c)

    def body(a_ref, o_ref):
        o_ref[...] = ((a_ref[0] + a_ref[1]) + a_ref[2]) + a_ref[3]

    return _pc(body, name=name, grid=(r // tr,), in_specs=[pl.BlockSpec((4, tr, c), lambda i: (0, i, 0))],
               out_specs=[_row(tr, c)], out_shape=[_sds((r, c))])(a)[0]


def _adam(w, g, m, v):
    m = ADAM_B1 * m + (1.0 - ADAM_B1) * g
    v = ADAM_B2 * v + (1.0 - ADAM_B2) * (g * g)
    m_hat = m / (1.0 - ADAM_B1 ** ADAM_STEP)
    v_hat = v / (1.0 - ADAM_B2 ** ADAM_STEP)
    return -ADAM_LR * (m_hat / (jnp.sqrt(v_hat) + ADAM_EPS) + ADAM_WD * w), m, v


def _k_adam(w, g, m, v, name):
    r, c = w.shape
    tr = _row_tile(r, c)

    def body(w_ref, g_ref, m_ref, v_ref, d_ref, mo_ref, vo_ref):
        d_ref[...], mo_ref[...], vo_ref[...] = _adam(w_ref[...], g_ref[...], m_ref[...], v_ref[...])

    return _pc(body, name=name, grid=(r // tr,), in_specs=[_row(tr, c)] * 4, out_specs=[_row(tr, c)] * 3,
               out_shape=[_sds((r, c))] * 3)(w, g, m, v)


def _k_sum8(a):
    _, n, _ = a.shape

    def body(a_ref, o_ref):
        acc = a_ref[0]
        for k in range(1, NDEV):
            acc = acc + a_ref[k]
        o_ref[...] = acc

    return _pc(body, name="sum_small_grads", grid=(1,), in_specs=[_acc(a.shape)], out_specs=[_acc((n, 128))],
               out_shape=[_sds((n, 128))])(a)[0]


def _k_adam_small(w, g, m, v):
    n = w.shape[0]

    def body(w_ref, g_ref, m_ref, v_ref, d_ref, mo_ref, vo_ref):
        d_ref[...], mo_ref[...], vo_ref[...] = _adam(w_ref[...], g_ref[...], m_ref[...], v_ref[...])

    return _pc(body, name="adam_small", grid=(1,), in_specs=[_acc((n, 128))] * 4, out_specs=[_acc((n, 128))] * 3,
               out_shape=[_sds((n, 128))] * 3)(w, g, m, v)


def _pack(vals):
    rows = []
    for a in vals:
        flat = a.reshape(-1)
        n = -(-flat.shape[0] // 1024) * 1024
        rows.append(jnp.pad(flat, (0, n - flat.shape[0])).reshape(n // 128, 128))
    return jnp.concatenate(rows, axis=0)


def _unpack(packed, shapes):
    out, off = [], 0
    for sh in shapes:
        size = int(np.prod(sh))
        n = -(-size // 1024) * 1024
        out.append(packed[off // 128:(off + n) // 128].reshape(-1)[:size].reshape(sh))
        off += n
    return out


_WEIGHTS = ["attn_norm", "w_in", "a_q_norm", "a_k_norm", "b_q_norm", "b_k_norm", "b_sinks", "mem_norm", "w_mem_kv",
            "m_q_norm", "m_k_norm", "w_o_a", "w_o_b", "w_o_m", "w_gate", "b_gate", "w_out", "ffn_norm", "w_up",
            "conv_w", "conv_b", "w_down"]
_BIG = ["w_in", "w_mem_kv", "w_o_a", "w_o_b", "w_o_m", "w_gate", "w_out", "w_up", "w_down"]
_SMALL = [n for n in _WEIGHTS if n not in _BIG]


def kernel(x, mem, positions, attn_norm, w_in, a_q_norm, a_k_norm, b_q_norm, b_k_norm, b_sinks, mem_norm, w_mem_kv, m_q_norm, m_k_norm, w_o_a, w_o_b, w_o_m, w_gate, b_gate, w_out, ffn_norm, w_up, conv_w, conv_b, w_down, loss_target, m_attn_norm, m_w_in, m_a_q_norm, m_a_k_norm, m_b_q_norm, m_b_k_norm, m_b_sinks, m_mem_norm, m_w_mem_kv, m_m_q_norm, m_m_k_norm, m_w_o_a, m_w_o_b, m_w_o_m, m_w_gate, m_b_gate, m_w_out, m_ffn_norm, m_w_up, m_conv_w, m_conv_b, m_w_down, v_attn_norm, v_w_in, v_a_q_norm, v_a_k_norm, v_b_q_norm, v_b_k_norm, v_b_sinks, v_mem_norm, v_w_mem_kv, v_m_q_norm, v_m_k_norm, v_w_o_a, v_w_o_b, v_w_o_m, v_w_gate, v_b_gate, v_w_out, v_ffn_norm, v_w_up, v_conv_w, v_conv_b, v_w_down):
    given = dict(locals())
    w = {n: given[n][0] for n in _WEIGHTS}
    m1 = {n: given["m_" + n][0] for n in _WEIGHTS}
    m2 = {n: given["v_" + n][0] for n in _WEIGHTS}

    gathered = _gather_shards([w[n].astype(_MM) for n in _BIG] + [w["conv_w"]])
    wts = dict(zip(_BIG + ["conv_w"], gathered))
    for n in ("w_mem_kv", "w_out", "w_down"):
        wts[n] = wts[n].reshape(-1, wts[n].shape[-1])
    small = {n: (w[n][None, :] if w[n].ndim == 1 else w[n]) for n in _SMALL if n != "conv_w"}

    loss, grad_x, big, sml = _local_step(x[0], mem[0], positions[0], loss_target[0], small, wts)
    loss = lax.psum(loss, ("x", "y", "c"))

    kept, got = _pair_split([big[n] for n in _BIG])
    parts = [_k_add2(a, b, "pair_add_" + n) for n, a, b in zip(_BIG, kept, got)]
    slots = _chip_scatter(parts)
    halves = [_k_sum4(a, "chip_add_" + n) for n, a in zip(_BIG, slots)]
    grads = dict(zip(_BIG, _pair_join(halves)))

    shapes = [sml[n].shape for n in _SMALL]
    gsm = dict(zip(_SMALL, _unpack(_k_sum8(_gather_small(_pack([sml[n] for n in _SMALL]))), shapes)))
    nu = w["conv_w"].shape[1]
    chip = 2 * lax.axis_index("x") + lax.axis_index("y")
    gsm["conv_w"] = lax.dynamic_slice_in_dim(gsm["conv_w"], chip * nu, nu, axis=1)
    for n in _SMALL:
        grads[n] = gsm[n].reshape(w[n].shape)

    delta, new_m, new_v = {}, {}, {}
    for n in _BIG:
        delta[n], new_m[n], new_v[n] = _k_adam(w[n], grads[n], m1[n], m2[n], "adam_" + n)
    pk = lambda d: _pack([d[n] for n in _SMALL])
    sshapes = [w[n].shape for n in _SMALL]
    for dst, packed in zip((delta, new_m, new_v), _k_adam_small(pk(w), pk(grads), pk(m1), pk(m2))):
        dst.update(zip(_SMALL, _unpack(packed, sshapes)))

    lead = lambda d: [d[n][None] for n in _WEIGHTS]
    return (loss, grad_x[None], *lead(grads), *lead(delta), *lead(new_m), *lead(new_v))
```

```python
import math

import jax
import jax.numpy as jnp
import numpy as np
from jax import lax
from jax.experimental import pallas as pl
from jax.experimental.pallas import tpu as pltpu

F32 = jnp.float32
_MM = jnp.bfloat16
_WIRE = jnp.bfloat16

D_MODEL = 1024
HEAD = 64
BLK = 128
A_GROUPS = ((128, 1), (512, 4), (2048, 16))
A_HEADS = 4
A_W = A_HEADS * HEAD
B_QH = 8
B_KVH = 2
B_WINDOW = 128
M_HEADS = 4
M_HD = 128
M_W = M_HEADS * M_HD
D_FF = 2816
EPS = 1e-6
NEG = -1e30
ROPE_THETA = 500000.0
ROPE_ROT = 16
CHIPS = 4
NDEV = 8
ADAM_LR, ADAM_B1, ADAM_B2, ADAM_EPS, ADAM_WD, ADAM_STEP = 0.001, 0.9, 0.999, 1e-08, 0.01, 10
VMEM_LIMIT = 58 * 1024 * 1024
MESH = pl.DeviceIdType.MESH


def _pc(body, *, name, grid, in_specs, out_specs, out_shape, scratch=()):
    return pl.pallas_call(
        body, name=name, grid=grid, in_specs=in_specs, out_specs=out_specs, out_shape=out_shape,
        scratch_shapes=list(scratch),
        compiler_params=pltpu.CompilerParams(dimension_semantics=("arbitrary",) * len(grid),
                                             vmem_limit_bytes=VMEM_LIMIT))


def _row(ts, c, col=0):
    return pl.BlockSpec((ts, c), lambda i: (i, col))


def _res(shape):
    n = len(shape)
    return pl.BlockSpec(tuple(shape), lambda i: (0,) * n, pipeline_mode=pl.Buffered(1))


def _acc(shape):
    n = len(shape)
    return pl.BlockSpec(tuple(shape), lambda i: (0,) * n)


def _sds(shape, dtype=F32):
    return jax.ShapeDtypeStruct(tuple(shape), dtype)


def _dot(a, b):
    return jnp.dot(a.astype(_MM), b.astype(_MM), preferred_element_type=F32)


def _dot_nt(a, b):
    return lax.dot_general(a.astype(_MM), b.astype(_MM), (((1,), (1,)), ((), ())), preferred_element_type=F32)


def _dot_tn(a, b):
    return lax.dot_general(a.astype(_MM), b.astype(_MM), (((0,), (0,)), ((), ())), preferred_element_type=F32)


def _sum8(v):
    ts, c = v.shape
    return jnp.sum(v.reshape(ts // 8, 8, c), axis=0)


def _sigmoid(z):
    return 1.0 / (1.0 + jnp.exp(-z))


def _rms(x):
    r = lax.rsqrt(jnp.mean(x * x, axis=-1, keepdims=True) + EPS)
    return x * r, r


def _rms_bwd(dy, xh, r, gain):
    z = dy * gain
    return r * (z - xh * jnp.mean(z * xh, axis=-1, keepdims=True))


def _seg_norm(x, seg):
    w = x.shape[1]
    xh, rr = [], []
    for s in range(w // seg):
        xs = x[:, s * seg:(s + 1) * seg]
        r = lax.rsqrt(jnp.mean(xs * xs, axis=-1, keepdims=True) + EPS)
        xh.append(xs * r)
        rr.append(jnp.broadcast_to(r, xs.shape))
    return jnp.concatenate(xh, axis=1), jnp.concatenate(rr, axis=1)


def _seg_mean(v, seg):
    w = v.shape[1]
    out = []
    for s in range(w // seg):
        vs = v[:, s * seg:(s + 1) * seg]
        out.append(jnp.broadcast_to(jnp.mean(vs, axis=-1, keepdims=True), vs.shape))
    return jnp.concatenate(out, axis=1)


def _rope(t, c, sa, sb):
    out = []
    for cb in range(t.shape[1] // 128):
        tc = t[:, cb * 128:(cb + 1) * 128]
        out.append(tc * c + pltpu.roll(tc, 120, 1) * sa + pltpu.roll(tc, 8, 1) * sb)
    return jnp.concatenate(out, axis=1) if len(out) > 1 else out[0]


def _rope_bwd(dy, c, sa, sb):
    out = []
    for cb in range(dy.shape[1] // 128):
        dc = dy[:, cb * 128:(cb + 1) * 128]
        out.append(dc * c + pltpu.roll(dc * sa, 8, 1) + pltpu.roll(dc * sb, 120, 1))
    return jnp.concatenate(out, axis=1) if len(out) > 1 else out[0]


def _rope_freqs():
    c = np.float32(-2.0 * math.log(ROPE_THETA) / ROPE_ROT)
    return [float(v) for v in np.exp(np.arange(ROPE_ROT // 2, dtype=np.float32) * c)]


def _k_rope(pos2d):
    n = pos2d.shape[0]
    freqs = _rope_freqs()
    nf = len(freqs)

    def body(p_ref, c_ref, s_ref):
        p = p_ref[...].astype(F32)
        for f in range(nf):
            ang = p * freqs[f]
            c_ref[f] = jnp.cos(ang)
            s_ref[f] = jnp.sin(ang)

    return _pc(body, name="rope_tables", grid=(1,),
               in_specs=[_acc((n, 128))], out_specs=[_acc((nf, n, 128)), _acc((nf, n, 128))],
               out_shape=[_sds((nf, n, 128)), _sds((nf, n, 128))])(pos2d)


def _rope_tables(pos_rows):
    r = pos_rows.shape[0]
    cos, sin = _k_rope(pos_rows.reshape(r // 128, 128))
    half = ROPE_ROT // 2
    cos = cos.reshape(half, r).T
    sin = sin.reshape(half, r).T
    one = jnp.ones((r, HEAD - ROPE_ROT), F32)
    zero = jnp.zeros((r, HEAD - ROPE_ROT), F32)
    z8 = jnp.zeros((r, half), F32)
    c64 = jnp.concatenate([cos, cos, one], axis=1)
    sa64 = jnp.concatenate([-sin, z8, zero], axis=1)
    sb64 = jnp.concatenate([z8, sin, zero], axis=1)
    return tuple(jnp.concatenate([t, t], axis=1) for t in (c64, sa64, sb64))


def _k_in(x, g1, w_in, w_gate, b_gate):
    s = x.shape[0]
    ts = min(256, s)
    nin, ng = w_in.shape[2], w_gate.shape[2]
    ncol = CHIPS * nin
    a_cols = 3 * A_W
    offs = [0, a_cols, 2 * a_cols, 3 * a_cols, 3 * a_cols + B_QH * HEAD,
            3 * a_cols + (B_QH + B_KVH) * HEAD, 3 * a_cols + (B_QH + 2 * B_KVH) * HEAD, ncol]

    def body(x_ref, g_ref, wi_ref, wg_ref, bg_ref, h_ref, a0, a1, a2, qb, kb, vb, mq, gt_ref, p_scr):
        xh, _ = _rms(x_ref[...])
        h = (xh * g_ref[...]).astype(_MM)
        h_ref[...] = h
        for j in range(CHIPS):
            p_scr[:, j * nin:(j + 1) * nin] = jnp.dot(h, wi_ref[j], preferred_element_type=F32)
            z = jnp.dot(h, wg_ref[j], preferred_element_type=F32) + bg_ref[:, j * ng:(j + 1) * ng]
            gt_ref[:, j * ng:(j + 1) * ng] = _sigmoid(z)
        for k, ref in enumerate((a0, a1, a2, qb, kb, vb, mq)):
            ref[...] = p_scr[:, offs[k]:offs[k + 1]]

    widths = [offs[k + 1] - offs[k] for k in range(7)]
    return _pc(
        body, name="in_proj", grid=(s // ts,),
        in_specs=[_row(ts, D_MODEL), _res((1, D_MODEL)), _res(w_in.shape), _res(w_gate.shape), _res(b_gate.shape)],
        out_specs=[_row(ts, D_MODEL)] + [_row(ts, w) for w in widths] + [_row(ts, CHIPS * ng)],
        out_shape=[_sds((s, D_MODEL), _MM)] + [_sds((s, w)) for w in widths] + [_sds((s, CHIPS * ng))],
        scratch=[pltpu.VMEM((ts, ncol), F32)])(x, g1, w_in, w_gate, b_gate)


def _k_prep(srcs, gq, gk, tabs, *, wq, wk, rows_per_gain, name):
    rows = srcs[0][0].shape[0]
    ts = min(256, rows)

    def body(q_ref, k_ref, v_ref, gq_ref, gk_ref, c_ref, sa_ref, sb_ref, qn_ref, kn_ref, vn_ref):
        c, sa, sb = c_ref[...], sa_ref[...], sb_ref[...]
        qh, _ = _seg_norm(q_ref[...], HEAD)
        qn_ref[...] = _rope(qh * gq_ref[...], c, sa, sb).astype(_MM)
        kh, _ = _seg_norm(k_ref[...], HEAD)
        kn_ref[...] = _rope(kh * gk_ref[...], c, sa, sb).astype(_MM)
        vn_ref[...] = v_ref[...].astype(_MM)

    gspec = lambda w: pl.BlockSpec((None, 1, w), lambda i: ((i * ts) // rows_per_gain, 0, 0))
    return _pc(
        body, name=name, grid=(rows // ts,),
        in_specs=[_row(ts, wq, srcs[0][1]), _row(ts, wk, srcs[1][1]), _row(ts, wk, srcs[2][1]),
                  gspec(wq), gspec(wk)] + [_row(ts, 128)] * 3,
        out_specs=[_row(ts, wq), _row(ts, wk), _row(ts, wk)],
        out_shape=[_sds((rows, wq), _MM), _sds((rows, wk), _MM), _sds((rows, wk), _MM)])(
            srcs[0][0], srcs[1][0], srcs[2][0], gq, gk, *tabs)


def _first_flag(b, segs, nb):
    first = b >= nb
    for k, (start, period) in enumerate(segs):
        end = segs[k + 1][0] if k + 1 < len(segs) else nb
        first = first | ((b >= start) & (b < end) & (lax.rem(b - start, jnp.int32(period)) == 0))
    return first


def _band_masks():
    qi = lax.broadcasted_iota(jnp.int32, (BLK, BLK), 0)
    kj = lax.broadcasted_iota(jnp.int32, (BLK, BLK), 1)
    return qi, kj


def _k_band_fwd(qn, kn, vn, *, hq, hk, max_dist, segs, sink, name):
    rows = qn.shape[0]
    nb = rows // BLK
    grp = hq // hk
    wq, wk = hq * HEAD, hk * HEAD
    scale = HEAD ** -0.5

    def body(*refs):
        if sink is None:
            q_ref, kc_ref, kp_ref, vc_ref, vp_ref, o_ref, l_ref = refs
        else:
            q_ref, kc_ref, kp_ref, vc_ref, vp_ref, sk_ref, o_ref, l_ref = refs
        b = pl.program_id(0)
        thr = jnp.where(_first_flag(b, segs, nb), 1 << 20, BLK - max_dist)
        qi, kj = _band_masks()
        m_cur = kj <= qi
        m_prev = kj >= qi + thr
        for g in range(hk):
            ks = slice(g * HEAD, (g + 1) * HEAD)
            kc, kp, vc, vp = kc_ref[:, ks], kp_ref[:, ks], vc_ref[:, ks], vp_ref[:, ks]
            for u in range(grp):
                hs = slice((g * grp + u) * HEAD, (g * grp + u + 1) * HEAD)
                q = q_ref[:, hs]
                sc = jnp.where(m_cur, _dot_nt(q, kc) * scale, NEG)
                sp = jnp.where(m_prev, _dot_nt(q, kp) * scale, NEG)
                m = jnp.maximum(jnp.max(sc, axis=-1, keepdims=True), jnp.max(sp, axis=-1, keepdims=True))
                den = (jnp.sum(jnp.exp(sc - m), axis=-1, keepdims=True)
                       + jnp.sum(jnp.exp(sp - m), axis=-1, keepdims=True))
                lse = m + jnp.log(den)
                if sink is not None:
                    sk = sk_ref[:, hs][:, 0:1]
                    mx = jnp.maximum(lse, sk)
                    lse = mx + jnp.log(jnp.exp(lse - mx) + jnp.exp(sk - mx))
                o_ref[:, hs] = _dot(jnp.exp(sc - lse), vc) + _dot(jnp.exp(sp - lse), vp)
                l_ref[:, hs] = jnp.broadcast_to(lse, (BLK, HEAD))

    cur = lambda w: pl.BlockSpec((BLK, w), lambda i: (i, 0))
    prev = lambda w: pl.BlockSpec((BLK, w), lambda i: (jnp.maximum(i - 1, 0), 0))
    in_specs = [cur(wq), cur(wk), prev(wk), cur(wk), prev(wk)]
    args = [qn, kn, kn, vn, vn]
    if sink is not None:
        in_specs.append(_res((1, wq)))
        args.append(sink)
    return _pc(body, name=name, grid=(nb,), in_specs=in_specs, out_specs=[cur(wq), cur(wq)],
               out_shape=[_sds((rows, wq)), _sds((rows, wq))])(*args)


def _k_memkv(mem, mem_norm, w_kv, m_k_norm):
    n = mem.shape[0]

    def body(m_ref, g_ref, w_ref, gk_ref, mn_ref, kv_ref, mk_ref, mv_ref):
        mh, _ = _rms(m_ref[...])
        mn = (mh * g_ref[...]).astype(_MM)
        mn_ref[...] = mn
        kv = jnp.dot(mn, w_ref[...], preferred_element_type=F32)
        kv_ref[...] = kv
        kh, _ = _seg_norm(kv[:, :M_W], M_HD)
        mk_ref[...] = (kh * gk_ref[...]).astype(_MM)
        mv_ref[...] = kv[:, M_W:].astype(_MM)

    return _pc(body, name="mem_kv", grid=(1,),
               in_specs=[_acc((n, D_MODEL)), _acc((1, D_MODEL)), _acc(w_kv.shape), _acc((1, M_W))],
               out_specs=[_acc((n, D_MODEL)), _acc((n, 2 * M_W)), _acc((n, M_W)), _acc((n, M_W))],
               out_shape=[_sds((n, D_MODEL), _MM), _sds((n, 2 * M_W)), _sds((n, M_W), _MM), _sds((n, M_W), _MM)])(
                   mem, mem_norm, w_kv, m_k_norm)


def _mem_probs(q, mk):
    sc = _dot_nt(q, mk) * (M_HD ** -0.5)
    e = jnp.exp(sc - jnp.max(sc, axis=-1, keepdims=True))
    return e / jnp.sum(e, axis=-1, keepdims=True)


def _k_mem_fwd(m_q, gq, mk, mv):
    s = m_q.shape[0]
    n = mk.shape[0]
    ts = min(256, s)

    def body(q_ref, g_ref, mk_ref, mv_ref, o_ref):
        qh, _ = _seg_norm(q_ref[...], M_HD)
        qn = (qh * g_ref[...]).astype(_MM)
        for h in range(M_HEADS):
            hs = slice(h * M_HD, (h + 1) * M_HD)
            o_ref[:, hs] = _dot(_mem_probs(qn[:, hs], mk_ref[:, hs]), mv_ref[:, hs])

    return _pc(body, name="mem_attn", grid=(s // ts,),
               in_specs=[_row(ts, M_W), _res((1, M_W)), _res((n, M_W)), _res((n, M_W))],
               out_specs=[_row(ts, M_W)], out_shape=[_sds((s, M_W))])(m_q, gq, mk, mv)[0]


def _group_weights(l0, l1, l2):
    m = jnp.maximum(jnp.maximum(l0, l1), l2)
    e0, e1, e2 = jnp.exp(l0 - m), jnp.exp(l1 - m), jnp.exp(l2 - m)
    inv = 1.0 / (e0 + e1 + e2)
    return e0 * inv, e1 * inv, e2 * inv


def _branch_products(oa, ob, om, woa_ref, wob_ref, wom_ref, j):
    return _dot(oa, woa_ref[j]), _dot(ob, wob_ref[j]), _dot(om, wom_ref[j])


def _k_merge(og, lg, o_b, o_m, gates, x, w_oa, w_ob, w_om, w_out, g2):
    s = x.shape[0]
    ts = min(256, s)
    nc = w_oa.shape[2]

    def body(o0, o1, o2, l0, l1, l2, ob_ref, om_ref, gt_ref, x_ref, woa, wob, wom, wout, g_ref,
             oa_ref, mer_ref, x1_ref, h2_ref, m_scr):
        w0, w1, w2 = _group_weights(l0[...], l1[...], l2[...])
        oa = w0 * o0[...] + w1 * o1[...] + w2 * o2[...]
        oa_ref[...] = oa
        ob, om = ob_ref[...], om_ref[...]
        for j in range(CHIPS):
            pa, pb, pm = _branch_products(oa, ob, om, woa, wob, wom, j)
            cs = lambda br: slice(br * D_MODEL + j * nc, br * D_MODEL + (j + 1) * nc)
            m_scr[:, j * nc:(j + 1) * nc] = gt_ref[:, cs(0)] * pa + gt_ref[:, cs(1)] * pb + gt_ref[:, cs(2)] * pm
        mer = m_scr[...].astype(_MM)
        mer_ref[...] = mer
        x1 = x_ref[...] + jnp.dot(mer, wout[...], preferred_element_type=F32)
        x1_ref[...] = x1
        xh, _ = _rms(x1)
        h2_ref[...] = (xh * g_ref[...]).astype(_MM)

    return _pc(
        body, name="merge_out", grid=(s // ts,),
        in_specs=[_row(ts, A_W)] * 6 + [_row(ts, B_QH * HEAD), _row(ts, M_W), _row(ts, 3 * D_MODEL), _row(ts, D_MODEL),
                                         _res(w_oa.shape), _res(w_ob.shape), _res(w_om.shape), _res(w_out.shape),
                                         _res((1, D_MODEL))],
        out_specs=[_row(ts, A_W), _row(ts, D_MODEL), _row(ts, D_MODEL), _row(ts, D_MODEL)],
        out_shape=[_sds((s, A_W)), _sds((s, D_MODEL), _MM), _sds((s, D_MODEL)), _sds((s, D_MODEL), _MM)],
        scratch=[pltpu.VMEM((ts, D_MODEL), F32)])(*og, *lg, o_b, o_m, gates, x, w_oa, w_ob, w_om, w_out, g2)


def _k_up(h2, w_up):
    s = h2.shape[0]
    ts = min(256, s)
    nu = w_up.shape[2]

    def body(h_ref, w_ref, u_ref):
        h = h_ref[...]
        for j in range(CHIPS):
            u_ref[:, j * nu:(j + 1) * nu] = jnp.dot(h, w_ref[j], preferred_element_type=F32)

    return _pc(body, name="up_proj", grid=(s // ts,), in_specs=[_row(ts, D_MODEL), _res(w_up.shape)],
               out_specs=[_row(ts, CHIPS * nu)], out_shape=[_sds((s, CHIPS * nu))])(h2, w_up)[0]


def _shift_down(v, halo, k):
    ts = v.shape[0]
    row = lax.broadcasted_iota(jnp.int32, v.shape, 0)
    out = pltpu.roll(v, k, 0)
    for r in range(k):
        out = jnp.where(row == r, halo[8 - k + r:8 - k + r + 1, :], out)
    return out


def _shift_up(v, halo, k):
    ts = v.shape[0]
    row = lax.broadcasted_iota(jnp.int32, v.shape, 0)
    out = pltpu.roll(v, ts - k, 0)
    for r in range(k):
        out = jnp.where(row == ts - k + r, halo[r:r + 1, :], out)
    return out


def _k_ffn(u, conv_w, conv_b, w_down, x1, target):
    s = u.shape[0]
    ts = min(128, s)
    nu = conv_w.shape[2]
    half = CHIPS // 2

    def body(u_ref, uh_ref, cw_ref, cb_ref, wd_ref, x1_ref, t_ref, dy_ref, f_ref, dc_ref, loss_ref, c_scr, f_scr):
        i = pl.program_id(0)
        halo = jnp.where(i > 0, uh_ref[...], 0.0)
        for j in range(CHIPS):
            cs = slice(j * nu, (j + 1) * nu)
            uj = u_ref[:, cs]
            hj = halo[:, cs]
            c_scr[:, cs] = (cb_ref[:, cs] + cw_ref[j, 0:1, :] * _shift_down(uj, hj, 2)
                            + cw_ref[j, 1:2, :] * _shift_down(uj, hj, 1) + cw_ref[j, 2:3, :] * uj)
        for j in range(half):
            a = c_scr[:, j * nu:(j + 1) * nu]
            g = c_scr[:, (half + j) * nu:(half + j + 1) * nu]
            f_scr[:, j * nu:(j + 1) * nu] = (a * _sigmoid(a) * g).astype(_MM)
        f = f_scr[...]
        f_ref[...] = f
        y = x1_ref[...] + jnp.dot(f, wd_ref[...], preferred_element_type=F32)
        err = y - t_ref[...]
        dy = err * (1.0 / D_MODEL)
        dy_ref[...] = dy

        @pl.when(i == 0)
        def _():
            loss_ref[...] = jnp.zeros_like(loss_ref)

        loss_ref[...] += _sum8(err * err)
        df = _dot_nt(dy, wd_ref[...])
        for j in range(half):
            a = c_scr[:, j * nu:(j + 1) * nu]
            g = c_scr[:, (half + j) * nu:(half + j + 1) * nu]
            sa = _sigmoid(a)
            dfj = df[:, j * nu:(j + 1) * nu]
            dc_ref[:, j * nu:(j + 1) * nu] = dfj * g * (sa * (1.0 + a * (1.0 - sa)))
            dc_ref[:, (half + j) * nu:(half + j + 1) * nu] = dfj * (a * sa)

    wide = CHIPS * nu
    return _pc(
        body, name="conv_ffn", grid=(s // ts,),
        in_specs=[_row(ts, wide), pl.BlockSpec((8, wide), lambda i: (jnp.maximum(i * (ts // 8) - 1, 0), 0)),
                  _res(conv_w.shape), _res((1, wide)), _res(w_down.shape), _row(ts, D_MODEL), _row(ts, D_MODEL)],
        out_specs=[_row(ts, D_MODEL), _row(ts, D_FF), _row(ts, wide), _acc((8, D_MODEL))],
        out_shape=[_sds((s, D_MODEL)), _sds((s, D_FF), _MM), _sds((s, wide)), _sds((8, D_MODEL))],
        scratch=[pltpu.VMEM((ts, wide), F32), pltpu.VMEM((ts, D_FF), _MM)])(u, u, conv_w, conv_b, w_down, x1, target)


def _k_conv_bwd(dc, u, conv_w, w_up, x1, g2, dy):
    s = u.shape[0]
    ts = min(128, s)
    nu = conv_w.shape[2]
    wide = CHIPS * nu
    last = s // ts - 1

    def body(dc_ref, dn_ref, u_ref, uh_ref, cw_ref, wu_ref, x1_ref, g_ref, dy_ref,
             dx1_ref, du_ref, cacc_ref, gacc_ref):
        i = pl.program_id(0)

        @pl.when(i == 0)
        def _():
            cacc_ref[...] = jnp.zeros_like(cacc_ref)
            gacc_ref[...] = jnp.zeros_like(gacc_ref)

        uhalo = jnp.where(i > 0, uh_ref[...], 0.0)
        dhalo = jnp.where(i < last, dn_ref[...], 0.0)
        dh2 = jnp.zeros((ts, D_MODEL), F32)
        for j in range(CHIPS):
            cs = slice(j * nu, (j + 1) * nu)
            dcj, uj = dc_ref[:, cs], u_ref[:, cs]
            cacc_ref[0, :, cs] += _sum8(dcj)
            cacc_ref[1, :, cs] += _sum8(dcj * _shift_down(uj, uhalo[:, cs], 2))
            cacc_ref[2, :, cs] += _sum8(dcj * _shift_down(uj, uhalo[:, cs], 1))
            cacc_ref[3, :, cs] += _sum8(dcj * uj)
            du = (cw_ref[j, 2:3, :] * dcj + cw_ref[j, 1:2, :] * _shift_up(dcj, dhalo[:, cs], 1)
                  + cw_ref[j, 0:1, :] * _shift_up(dcj, dhalo[:, cs], 2)).astype(_MM)
            du_ref[:, cs] = du
            dh2 = dh2 + _dot_nt(du, wu_ref[j])
        xh, r = _rms(x1_ref[...])
        gacc_ref[...] += _sum8(dh2 * xh)
        dx1_ref[...] = dy_ref[...] + _rms_bwd(dh2, xh, r, g_ref[...])

    return _pc(
        body, name="conv_up_bwd", grid=(s // ts,),
        in_specs=[_row(ts, wide),
                  pl.BlockSpec((8, wide), lambda i: (jnp.minimum((i + 1) * (ts // 8), s // 8 - 1), 0)),
                  _row(ts, wide), pl.BlockSpec((8, wide), lambda i: (jnp.maximum(i * (ts // 8) - 1, 0), 0)),
                  _res(conv_w.shape), _res(w_up.shape), _row(ts, D_MODEL), _res((1, D_MODEL)), _row(ts, D_MODEL)],
        out_specs=[_row(ts, D_MODEL), _row(ts, wide), _acc((4, 8, wide)), _acc((8, D_MODEL))],
        out_shape=[_sds((s, D_MODEL)), _sds((s, wide), _MM), _sds((4, 8, wide)), _sds((8, D_MODEL))])(
            dc, dc, u, u, conv_w, w_up, x1, g2, dy)


def _k_merge_bwd(dx1, og, lg, o_a, o_b, o_m, gates, w_oa, w_ob, w_om, w_out):
    s = dx1.shape[0]
    ts = min(256, s)
    nc = w_oa.shape[2]

    def body(dx_ref, o0, o1, o2, l0, l1, l2, oa_ref, ob_ref, om_ref, gt_ref, woa, wob, wom, wout,
             dgp_ref, dpa_ref, dpb_ref, dpm_ref, dog0, dog1, dog2, dl0, dl1, dl2, dob_ref, dom_ref, bacc_ref):
        i = pl.program_id(0)

        @pl.when(i == 0)
        def _():
            bacc_ref[...] = jnp.zeros_like(bacc_ref)

        dmer = _dot_nt(dx_ref[...], wout[...])
        oa, ob, om = oa_ref[...], ob_ref[...], om_ref[...]
        doa = jnp.zeros((ts, A_W), F32)
        dob = jnp.zeros((ts, B_QH * HEAD), F32)
        dom = jnp.zeros((ts, M_W), F32)
        for j in range(CHIPS):
            prods = _branch_products(oa, ob, om, woa, wob, wom, j)
            dmj = dmer[:, j * nc:(j + 1) * nc]
            dps = []
            for br, (p, dref) in enumerate(zip(prods, (dpa_ref, dpb_ref, dpm_ref))):
                cs = slice(br * D_MODEL + j * nc, br * D_MODEL + (j + 1) * nc)
                gt = gt_ref[:, cs]
                dgp = dmj * p * gt * (1.0 - gt)
                dgp_ref[:, cs] = dgp.astype(_MM)
                bacc_ref[:, cs] += _sum8(dgp)
                dp = (dmj * gt).astype(_MM)
                dref[:, j * nc:(j + 1) * nc] = dp
                dps.append(dp)
            doa = doa + _dot_nt(dps[0], woa[j])
            dob = dob + _dot_nt(dps[1], wob[j])
            dom = dom + _dot_nt(dps[2], wom[j])
        dob_ref[...] = dob
        dom_ref[...] = dom
        ws = _group_weights(l0[...], l1[...], l2[...])
        dsum = _seg_mean(doa * oa, HEAD) * float(HEAD)
        for w, dref, lref in zip(ws, (dog0, dog1, dog2), (dl0, dl1, dl2)):
            dref[...] = w * doa
            lref[...] = w * dsum

    return _pc(
        body, name="merge_out_bwd", grid=(s // ts,),
        in_specs=[_row(ts, D_MODEL)] + [_row(ts, A_W)] * 7 + [_row(ts, B_QH * HEAD), _row(ts, M_W), _row(ts, 3 * D_MODEL),
                                                              _res(w_oa.shape), _res(w_ob.shape), _res(w_om.shape),
                                                              _res(w_out.shape)],
        out_specs=[_row(ts, 3 * D_MODEL)] + [_row(ts, D_MODEL)] * 3 + [_row(ts, A_W)] * 6
        + [_row(ts, B_QH * HEAD), _row(ts, M_W), _acc((8, 3 * D_MODEL))],
        out_shape=[_sds((s, 3 * D_MODEL), _MM)] + [_sds((s, D_MODEL), _MM)] * 3 + [_sds((s, A_W))] * 6
        + [_sds((s, B_QH * HEAD)), _sds((s, M_W)), _sds((8, 3 * D_MODEL))])(
            dx1, *og, *lg, o_a, o_b, o_m, gates, w_oa, w_ob, w_om, w_out)


def _k_mem_bwd(m_q, gq, mk, mv, o_m, do_m):
    s = m_q.shape[0]
    n = mk.shape[0]
    ts = min(256, s)
    scale = M_HD ** -0.5

    def body(q_ref, g_ref, mk_ref, mv_ref, o_ref, do_ref, dq_ref, dmk_ref, dmv_ref, gacc_ref):
        i = pl.program_id(0)

        @pl.when(i == 0)
        def _():
            dmk_ref[...] = jnp.zeros_like(dmk_ref)
            dmv_ref[...] = jnp.zeros_like(dmv_ref)
            gacc_ref[...] = jnp.zeros_like(gacc_ref)

        gain = g_ref[...]
        qh, r = _seg_norm(q_ref[...], M_HD)
        qn = (qh * gain).astype(_MM)
        do = do_ref[...]
        delta = _seg_mean(do * o_ref[...], M_HD) * float(M_HD)
        dqn = []
        for h in range(M_HEADS):
            hs = slice(h * M_HD, (h + 1) * M_HD)
            p = _mem_probs(qn[:, hs], mk_ref[:, hs])
            dp = _dot_nt(do[:, hs], mv_ref[:, hs])
            ds = (p * (dp - delta[:, hs][:, 0:1]) * scale).astype(_MM)
            dqn.append(_dot(ds, mk_ref[:, hs]))
            dmk_ref[:, hs] += _dot_tn(ds, qn[:, hs])
            dmv_ref[:, hs] += _dot_tn(p, do[:, hs])
        dqn = jnp.concatenate(dqn, axis=1)
        gacc_ref[...] += _sum8(dqn * qh)
        z = dqn * gain
        dq_ref[...] = (r * (z - qh * _seg_mean(z * qh, M_HD))).astype(_MM)

    return _pc(
        body, name="mem_attn_bwd", grid=(s // ts,),
        in_specs=[_row(ts, M_W), _res((1, M_W)), _res((n, M_W)), _res((n, M_W)), _row(ts, M_W), _row(ts, M_W)],
        out_specs=[_row(ts, M_W), _acc((n, M_W)), _acc((n, M_W)), _acc((8, M_W))],
        out_shape=[_sds((s, M_W), _MM), _sds((n, M_W)), _sds((n, M_W)), _sds((8, M_W))])(m_q, gq, mk, mv, o_m, do_m)


def _k_memkv_bwd(mem, mem_norm, w_kv, m_k_norm, mem_n, kv, dmk, dmv):
    n = mem.shape[0]

    def body(m_ref, g_ref, w_ref, gk_ref, mn_ref, kv_ref, dmk_ref, dmv_ref, dw_ref, dg_ref, dgk_ref):
        gk = gk_ref[...]
        kh, r = _seg_norm(kv_ref[:, :M_W], M_HD)
        dmk = dmk_ref[...]
        dgk_ref[...] = _sum8(dmk * kh)
        z = dmk * gk
        dk = r * (z - kh * _seg_mean(z * kh, M_HD))
        dkv = jnp.concatenate([dk, dmv_ref[...]], axis=1).astype(_MM)
        dw_ref[...] = _dot_tn(mn_ref[...], dkv)
        dmn = _dot_nt(dkv, w_ref[...])
        mh, _ = _rms(m_ref[...])
        dg_ref[...] = _sum8(dmn * mh)

    return _pc(body, name="mem_kv_bwd", grid=(1,),
               in_specs=[_acc((n, D_MODEL)), _acc((1, D_MODEL)), _acc(w_kv.shape), _acc((1, M_W)), _acc((n, D_MODEL)),
                         _acc((n, 2 * M_W)), _acc((n, M_W)), _acc((n, M_W))],
               out_specs=[_acc(w_kv.shape), _acc((8, D_MODEL)), _acc((8, M_W))],
               out_shape=[_sds(w_kv.shape), _sds((8, D_MODEL)), _sds((8, M_W))])(
                   mem, mem_norm, w_kv, m_k_norm, mem_n, kv, dmk, dmv)


def _k_band_bwd(qn, kn, vn, do, lse, dl_or_o, *, hq, hk, max_dist, segs, sink, name):
    rows = qn.shape[0]
    nb = rows // BLK
    grp = hq // hk
    wq, wk = hq * HEAD, hk * HEAD
    scale = HEAD ** -0.5

    def body(*refs):
        (qb_ref, qx_ref, kb_ref, kp_ref, vb_ref, vp_ref, dob_ref, dox_ref, lb_ref, lx_ref, eb_ref, ex_ref) = refs[:12]
        if sink is None:
            dq_ref, dk_ref, dv_ref = refs[12:]
        else:
            sk_ref, dq_ref, dk_ref, dv_ref, sacc_ref = refs[12:]
        b = pl.program_id(0)
        thr_b = jnp.where(_first_flag(b, segs, nb), 1 << 20, BLK - max_dist)
        thr_x = jnp.where(_first_flag(b + 1, segs, nb), 1 << 20, BLK - max_dist)
        qi, kj = _band_masks()
        m_cur = kj <= qi
        m_prev = kj >= qi + thr_b
        m_next = kj >= qi + thr_x
        if sink is not None:
            @pl.when(b == 0)
            def _():
                sacc_ref[...] = jnp.zeros_like(sacc_ref)

        def tile(q, k, v, do, l, dlt, mask):
            p = jnp.exp(jnp.where(mask, _dot_nt(q, k) * scale, NEG) - l)
            ds = p * (_dot_nt(do, v) - dlt) * scale
            return p, ds

        for g in range(hk):
            ks = slice(g * HEAD, (g + 1) * HEAD)
            kb, kp, vb, vp = kb_ref[:, ks], kp_ref[:, ks], vb_ref[:, ks], vp_ref[:, ks]
            dk = jnp.zeros((BLK, HEAD), F32)
            dv = jnp.zeros((BLK, HEAD), F32)
            for u in range(grp):
                hs = slice((g * grp + u) * HEAD, (g * grp + u + 1) * HEAD)
                q, qx, do, dox = qb_ref[:, hs], qx_ref[:, hs], dob_ref[:, hs], dox_ref[:, hs]
                l, lx = lb_ref[:, hs][:, 0:1], lx_ref[:, hs][:, 0:1]
                if sink is None:
                    dlt, dltx = eb_ref[:, hs][:, 0:1], ex_ref[:, hs][:, 0:1]
                else:
                    dlt = jnp.sum(do * eb_ref[:, hs], axis=-1, keepdims=True)
                    dltx = jnp.sum(dox * ex_ref[:, hs], axis=-1, keepdims=True)
                    sk = sk_ref[:, hs][:, 0:1]
                    sacc_ref[:, hs] += jnp.broadcast_to(-jnp.exp(sk - l) * dlt, (BLK, HEAD))
                pc, dsc = tile(q, kb, vb, do, l, dlt, m_cur)
                _, dsp = tile(q, kp, vp, do, l, dlt, m_prev)
                px, dsx = tile(qx, kb, vb, dox, lx, dltx, m_next)
                dq_ref[:, hs] = _dot(dsc, kb) + _dot(dsp, kp)
                dk = dk + _dot_tn(dsc, q) + _dot_tn(dsx, qx)
                dv = dv + _dot_tn(pc, do) + _dot_tn(px, dox)
            dk_ref[:, ks] = dk
            dv_ref[:, ks] = dv.astype(_MM)

    cur = lambda w: pl.BlockSpec((BLK, w), lambda i: (i, 0))
    prev = lambda w: pl.BlockSpec((BLK, w), lambda i: (jnp.maximum(i - 1, 0), 0))
    nxt = lambda w: pl.BlockSpec((BLK, w), lambda i: (jnp.minimum(i + 1, nb - 1), 0))
    in_specs = [cur(wq), nxt(wq), cur(wk), prev(wk), cur(wk), prev(wk), cur(wq), nxt(wq), cur(wq), nxt(wq), cur(wq), nxt(wq)]
    args = [qn, qn, kn, kn, vn, vn, do, do, lse, lse, dl_or_o, dl_or_o]
    out_specs = [cur(wq), cur(wk), cur(wk)]
    out_shape = [_sds((rows, wq)), _sds((rows, wk)), _sds((rows, wk), _MM)]
    if sink is not None:
        in_specs.append(_res((1, wq)))
        args.append(sink)
        out_specs.append(_acc((BLK, wq)))
        out_shape.append(_sds((BLK, wq)))
    return _pc(body, name=name, grid=(nb,), in_specs=in_specs, out_specs=out_specs, out_shape=out_shape)(*args)


def _k_prep_bwd(srcs, dqn, dkn, gq, gk, tabs, *, wq, wk, rows_per_gain, name):
    rows = dqn.shape[0]
    ts = min(256, rows)
    ngain = gq.shape[0]

    def body(q_ref, k_ref, dq_ref, dk_ref, gq_ref, gk_ref, c_ref, sa_ref, sb_ref, oq_ref, ok_ref, aq_ref, ak_ref):
        i = pl.program_id(0)

        @pl.when(lax.rem(i * ts, rows_per_gain) == 0)
        def _():
            aq_ref[...] = jnp.zeros_like(aq_ref)
            ak_ref[...] = jnp.zeros_like(ak_ref)

        c, sa, sb = c_ref[...], sa_ref[...], sb_ref[...]
        for x_ref, d_ref, g_ref, o_ref, a_ref in ((q_ref, dq_ref, gq_ref, oq_ref, aq_ref),
                                                   (k_ref, dk_ref, gk_ref, ok_ref, ak_ref)):
            xh, r = _seg_norm(x_ref[...], HEAD)
            dt = _rope_bwd(d_ref[...], c, sa, sb)
            a_ref[...] += _sum8(dt * xh)
            z = dt * g_ref[...]
            o_ref[...] = (r * (z - xh * _seg_mean(z * xh, HEAD))).astype(_MM)

    gspec = lambda w: pl.BlockSpec((None, 1, w), lambda i: ((i * ts) // rows_per_gain, 0, 0))
    aspec = lambda w: pl.BlockSpec((None, 8, w), lambda i: ((i * ts) // rows_per_gain, 0, 0))
    return _pc(
        body, name=name, grid=(rows // ts,),
        in_specs=[_row(ts, wq, srcs[0][1]), _row(ts, wk, srcs[1][1]), _row(ts, wq), _row(ts, wk), gspec(wq), gspec(wk)]
        + [_row(ts, 128)] * 3,
        out_specs=[_row(ts, wq), _row(ts, wk), aspec(wq), aspec(wk)],
        out_shape=[_sds((rows, wq), _MM), _sds((rows, wk), _MM), _sds((ngain, 8, wq)), _sds((ngain, 8, wk))])(
            srcs[0][0], srcs[1][0], dqn, dkn, gq, gk, *tabs)


def _k_in_bwd(pieces, dgp, x, g1, dx1, w_in, w_gate):
    s = x.shape[0]
    ts = min(256, s)
    nin, ng = w_in.shape[2], w_gate.shape[2]
    widths = [p.shape[1] for p in pieces]
    ncol = sum(widths)

    def body(*refs):
        p_refs = refs[:len(pieces)]
        dgp_ref, x_ref, g_ref, dx1_ref, wi_ref, wg_ref, gx_ref, dpj_ref, gacc_ref = refs[len(pieces):]
        i = pl.program_id(0)

        @pl.when(i == 0)
        def _():
            gacc_ref[...] = jnp.zeros_like(gacc_ref)

        off = 0
        for p_ref, w in zip(p_refs, widths):
            dpj_ref[:, off:off + w] = p_ref[...]
            off += w
        dh = jnp.zeros((ts, D_MODEL), F32)
        for j in range(CHIPS):
            dh = dh + _dot_nt(dpj_ref[:, j * nin:(j + 1) * nin], wi_ref[j])
            dh = dh + _dot_nt(dgp_ref[:, j * ng:(j + 1) * ng], wg_ref[j])
        xh, r = _rms(x_ref[...])
        gacc_ref[...] += _sum8(dh * xh)
        gx_ref[...] = dx1_ref[...] + _rms_bwd(dh, xh, r, g_ref[...])

    return _pc(
        body, name="in_proj_bwd", grid=(s // ts,),
        in_specs=[_row(ts, w) for w in widths] + [_row(ts, CHIPS * ng), _row(ts, D_MODEL), _res((1, D_MODEL)),
                                                  _row(ts, D_MODEL), _res(w_in.shape), _res(w_gate.shape)],
        out_specs=[_row(ts, D_MODEL), _row(ts, ncol), _acc((8, D_MODEL))],
        out_shape=[_sds((s, D_MODEL)), _sds((s, ncol), _MM), _sds((8, D_MODEL))])(*pieces, dgp, x, g1, dx1, w_in, w_gate)


def _k_wgrad(a, b, *, nblk, stacked, name):
    s, k = a.shape
    n = b.shape[1]
    nb = n // nblk
    ts = min(512, s)

    def body(a_ref, b_ref, o_ref):
        @pl.when(pl.program_id(1) == 0)
        def _():
            o_ref[...] = jnp.zeros_like(o_ref)

        o_ref[...] += _dot_tn(a_ref[...], b_ref[...])

    if stacked:
        out_spec, out_shape = pl.BlockSpec((None, k, nb), lambda g, t: (g, 0, 0)), _sds((nblk, k, nb))
    else:
        out_spec, out_shape = pl.BlockSpec((k, nb), lambda g, t: (0, g)), _sds((k, n))
    return _pc(body, name=name, grid=(nblk, s // ts),
               in_specs=[pl.BlockSpec((ts, k), lambda g, t: (t, 0)), pl.BlockSpec((ts, nb), lambda g, t: (t, g))],
               out_specs=[out_spec], out_shape=[out_shape])(a, b)[0]


def _to_res(t, d):
    s, c = t.shape
    return t if d == 1 else t.reshape(s // d, d, c).transpose(1, 0, 2).reshape(s, c)


def _from_res(t, d):
    s, c = t.shape
    return t if d == 1 else t.reshape(d, s // d, c).transpose(1, 0, 2).reshape(s, c)


def _tile_gain(g, heads):
    return jnp.tile(g, (1,) * (g.ndim - 1) + (heads,))[..., None, :]


def _local_step(x, mem, pos, target, small, wts):
    s = x.shape[0]
    nblk = s // BLK
    g1, g2 = small["attn_norm"], small["ffn_norm"]

    pos_rows = jnp.concatenate([_to_res(pos[:, None], d)[:, 0] for _, d in A_GROUPS] + [pos])
    tabs = _rope_tables(pos_rows)
    tabs_a = tuple(t[:3 * s] for t in tabs)
    tabs_b = tuple(t[3 * s:] for t in tabs)

    h, qa0, qa1, qa2, q_b, k_b, v_b, m_q, gates = _k_in(x, g1, wts["w_in"], wts["w_gate"], small["b_gate"])

    qkv_a = jnp.concatenate([_to_res(t, d) for t, (_, d) in zip((qa0, qa1, qa2), A_GROUPS)], axis=0)
    gq_a = _tile_gain(small["a_q_norm"], A_HEADS)
    gk_a = _tile_gain(small["a_k_norm"], A_HEADS)
    src_a = ((qkv_a, 0), (qkv_a, 1), (qkv_a, 2))
    qn_a, kn_a, vn_a = _k_prep(src_a, gq_a, gk_a, tabs_a, wq=A_W, wk=A_W, rows_per_gain=s, name="prep_a")
    segs_a = tuple((gi * nblk, nblk // d) for gi, (_, d) in enumerate(A_GROUPS))
    o_res, l_res = _k_band_fwd(qn_a, kn_a, vn_a, hq=A_HEADS, hk=A_HEADS, max_dist=BLK, segs=segs_a, sink=None,
                               name="attn_a")
    og = [_from_res(o_res[gi * s:(gi + 1) * s], d) for gi, (_, d) in enumerate(A_GROUPS)]
    lg = [_from_res(l_res[gi * s:(gi + 1) * s], d) for gi, (_, d) in enumerate(A_GROUPS)]

    gq_b = _tile_gain(small["b_q_norm"], B_QH)
    gk_b = _tile_gain(small["b_k_norm"], B_KVH)
    src_b = ((q_b, 0), (k_b, 0), (v_b, 0))
    qn_b, kn_b, vn_b = _k_prep(src_b, gq_b, gk_b, tabs_b, wq=B_QH * HEAD, wk=B_KVH * HEAD, rows_per_gain=s,
                               name="prep_b")
    sink_x = jnp.repeat(small["b_sinks"], HEAD)[None, :]
    segs_b = ((0, nblk),)
    o_b, l_b = _k_band_fwd(qn_b, kn_b, vn_b, hq=B_QH, hk=B_KVH, max_dist=B_WINDOW - 1, segs=segs_b, sink=sink_x,
                           name="attn_b")

    gq_m = _tile_gain(small["m_q_norm"], M_HEADS)[0]
    gk_m = _tile_gain(small["m_k_norm"], M_HEADS)[0]
    mem_n, kv, mk, mv = _k_memkv(mem, small["mem_norm"], wts["w_mem_kv"], gk_m)
    o_m = _k_mem_fwd(m_q, gq_m, mk, mv)

    o_a, merged, x1, h2 = _k_merge(og, lg, o_b, o_m, gates, x, wts["w_o_a"], wts["w_o_b"], wts["w_o_m"],
                                   wts["w_out"], g2)
    u = _k_up(h2, wts["w_up"])
    dy, f, dc, loss_acc = _k_ffn(u, wts["conv_w"], small["conv_b"], wts["w_down"], x1, target)
    loss = (0.5 / D_MODEL) * jnp.sum(loss_acc)

    dx1, du, cacc, g2acc = _k_conv_bwd(dc, u, wts["conv_w"], wts["w_up"], x1, g2, dy)
    (dgp, dp_a, dp_b, dp_m, dog0, dog1, dog2, dl0, dl1, dl2, do_b, do_m, bacc) = _k_merge_bwd(
        dx1, og, lg, o_a, o_b, o_m, gates, wts["w_o_a"], wts["w_o_b"], wts["w_o_m"], wts["w_out"])

    dq_m, dmk, dmv, gqm_acc = _k_mem_bwd(m_q, gq_m, mk, mv, o_m, do_m)
    dw_kv, gmem_acc, gkm_acc = _k_memkv_bwd(mem, small["mem_norm"], wts["w_mem_kv"], gk_m, mem_n, kv, dmk, dmv)

    dq_bn, dk_bn, dv_b, sacc = _k_band_bwd(qn_b, kn_b, vn_b, do_b, l_b, o_b, hq=B_QH, hk=B_KVH,
                                           max_dist=B_WINDOW - 1, segs=segs_b, sink=sink_x, name="attn_b_bwd")
    dq_b, dk_b, gqb_acc, gkb_acc = _k_prep_bwd(src_b, dq_bn, dk_bn, gq_b, gk_b, tabs_b, wq=B_QH * HEAD,
                                               wk=B_KVH * HEAD, rows_per_gain=s, name="prep_b_bwd")

    do_res = jnp.concatenate([_to_res(t, d) for t, (_, d) in zip((dog0, dog1, dog2), A_GROUPS)], axis=0)
    dl_res = jnp.concatenate([_to_res(t, d) for t, (_, d) in zip((dl0, dl1, dl2), A_GROUPS)], axis=0)
    dq_an, dk_an, dv_a = _k_band_bwd(qn_a, kn_a, vn_a, do_res, l_res, dl_res, hq=A_HEADS, hk=A_HEADS, max_dist=BLK,
                                     segs=segs_a, sink=None, name="attn_a_bwd")
    dq_a, dk_a, gqa_acc, gka_acc = _k_prep_bwd(src_a, dq_an, dk_an, gq_a, gk_a, tabs_a, wq=A_W, wk=A_W,
                                               rows_per_gain=s, name="prep_a_bwd")
    pieces = []
    for gi, (_, d) in enumerate(A_GROUPS):
        rs = slice(gi * s, (gi + 1) * s)
        pieces += [_from_res(t[rs], d) for t in (dq_a, dk_a, dv_a)]
    pieces += [dq_b, dk_b, dv_b, dq_m]
    grad_x, dproj, g1acc = _k_in_bwd(pieces, dgp, x, g1, dx1, wts["w_in"], wts["w_gate"])

    big = {
        "w_in": _k_wgrad(h, dproj, nblk=CHIPS, stacked=True, name="dw_in"),
        "w_gate": _k_wgrad(h, dgp, nblk=CHIPS, stacked=True, name="dw_gate"),
        "w_o_a": _k_wgrad(o_a, dp_a, nblk=CHIPS, stacked=True, name="dw_o_a"),
        "w_o_b": _k_wgrad(o_b, dp_b, nblk=CHIPS, stacked=True, name="dw_o_b"),
        "w_o_m": _k_wgrad(o_m, dp_m, nblk=CHIPS, stacked=True, name="dw_o_m"),
        "w_up": _k_wgrad(h2, du, nblk=CHIPS, stacked=True, name="dw_up"),
        "w_out": _k_wgrad(merged, dx1, nblk=1, stacked=False, name="dw_out").reshape(CHIPS, -1, D_MODEL),
        "w_down": _k_wgrad(f, dy, nblk=2, stacked=False, name="dw_down").reshape(CHIPS, -1, D_MODEL),
        "w_mem_kv": dw_kv.reshape(CHIPS, -1, 2 * M_W),
    }

    def fold(acc, heads):
        v = jnp.sum(acc, axis=-2)
        return jnp.sum(v.reshape(v.shape[:-1] + (heads, -1)), axis=-2)

    csum = jnp.sum(cacc, axis=1)
    sml = {
        "attn_norm": jnp.sum(g1acc, axis=0), "a_q_norm": fold(gqa_acc, A_HEADS), "a_k_norm": fold(gka_acc, A_HEADS),
        "b_q_norm": fold(gqb_acc[0], B_QH), "b_k_norm": fold(gkb_acc[0], B_KVH),
        "b_sinks": jnp.sum(sacc, axis=0).reshape(B_QH, HEAD)[:, 0], "mem_norm": jnp.sum(gmem_acc, axis=0),
        "m_q_norm": fold(gqm_acc, M_HEADS), "m_k_norm": fold(gkm_acc, M_HEADS),
        "b_gate": jnp.sum(bacc, axis=0), "ffn_norm": jnp.sum(g2acc, axis=0),
        "conv_w": csum[1:], "conv_b": csum[0],
    }
    return loss, grad_x, big, sml


def _mesh_pos():
    return lax.axis_index("x"), lax.axis_index("y"), lax.axis_index("c")


def _chip_peers(x, y):
    return [(1 - x, y), (x, 1 - y), (1 - x, 1 - y)]


_ANY = pl.BlockSpec(memory_space=pl.ANY)


def _comm_call(body, *, name, n_in, out_shape, scratch):
    return pl.pallas_call(body, name=name, in_specs=[_ANY] * n_in, out_specs=[_ANY] * len(out_shape),
                          out_shape=out_shape, scratch_shapes=scratch)


def _remote(src, dst, send_sem, recv_sem, dev):
    return pltpu.make_async_remote_copy(src_ref=src, dst_ref=dst, send_sem=send_sem, recv_sem=recv_sem,
                                        device_id=dev, device_id_type=MESH)


def _gather_shards(shards):
    nt = len(shards)
    split = [sh.shape[0] % 16 == 0 for sh in shards]

    def body(*refs):
        ins, outs = refs[:nt], refs[nt:2 * nt]
        ici_s, ici_r, fwd_s, fwd_r, own_s, own_r = refs[2 * nt:]
        x, y, c = _mesh_pos()
        me = 2 * x + y
        sib = (x, y, 1 - c)
        peers = _chip_peers(x, y)

        def half(ref, t, who):
            if not split[t]:
                return ref
            hr = shards[t].shape[0] // 2
            return ref.at[pl.ds(pl.multiple_of(who * hr, 8), hr), :]

        pending = []
        for t in range(nt):
            own = _remote(ins[t], outs[t].at[me], own_s.at[t], own_r.at[t], sib)
            own.start()
            pending.append(own.wait)
            for k, (px, py) in enumerate(peers):
                rc = _remote(half(ins[t], t, c), half(outs[t].at[me], t, c), ici_s.at[t, k], ici_r.at[t, k], (px, py, c))
                rc.start()
                pending.append(rc.wait_send)
        for t in range(nt):
            for k, (px, py) in enumerate(peers):
                land = half(outs[t].at[2 * px + py], t, c)
                _remote(land, land, ici_s.at[t, k], ici_r.at[t, k], (px, py, c)).wait_recv()
                if split[t]:
                    fw = _remote(land, land, fwd_s.at[t, k], fwd_r.at[t, k], sib)
                    fw.start()
                    pending.append(fw.wait_send)
                    other = half(outs[t].at[2 * px + py], t, 1 - c)
                    pending.append(_remote(other, other, fwd_s.at[t, k], fwd_r.at[t, k], sib).wait_recv)
        for wait in pending:
            wait()

    out_shape = [_sds((CHIPS,) + sh.shape, sh.dtype) for sh in shards]
    dma = pltpu.SemaphoreType.DMA
    scratch = [dma((nt, 3)), dma((nt, 3)), dma((nt, 3)), dma((nt, 3)), dma((nt,)), dma((nt,))]
    return _comm_call(body, name="gather_weights", n_in=nt, out_shape=out_shape, scratch=scratch)(*shards)


def _pair_split(grads):
    nt = len(grads)

    def body(*refs):
        ins, got = refs[:nt], refs[nt:2 * nt]
        send_sems, recv_sems = refs[2 * nt:]
        x, y, c = _mesh_pos()
        cps = []
        for t in range(nt):
            hr = ins[t].shape[1] // 2
            give = ins[t].at[:, pl.ds(pl.multiple_of((1 - c) * hr, 8), hr), :]
            rc = _remote(give, got[t], send_sems.at[t], recv_sems.at[t], (x, y, 1 - c))
            rc.start()
            cps.append(rc)
        for rc in cps:
            rc.wait()

    half = [_sds((CHIPS, g.shape[1] // 2, g.shape[2]), g.dtype) for g in grads]
    scratch = [pltpu.SemaphoreType.DMA((nt,)), pltpu.SemaphoreType.DMA((nt,))]
    return _comm_call(body, name="grad_pair_split", n_in=nt, out_shape=half, scratch=scratch)(*grads)


def _chip_scatter(parts):
    nt = len(parts)

    def body(*refs):
        ins, outs = refs[:nt], refs[nt:2 * nt]
        send_sems, recv_sems = refs[2 * nt:]
        x, y, c = _mesh_pos()
        cps = []
        for t in range(nt):
            for k, (px, py) in enumerate(_chip_peers(x, y)):
                rc = _remote(ins[t].at[2 * px + py], outs[t].at[k], send_sems.at[t, k], recv_sems.at[t, k], (px, py, c))
                rc.start()
                cps.append(rc)
        for cp in cps:
            cp.wait()

    out_shape = [_sds((3,) + p.shape[1:], p.dtype) for p in parts]
    scratch = [pltpu.SemaphoreType.DMA((nt, 3)), pltpu.SemaphoreType.DMA((nt, 3))]
    return _comm_call(body, name="grad_chip_scatter", n_in=nt, out_shape=out_shape, scratch=scratch)(*parts)


def _pair_join(halves):
    nt = len(halves)

    def body(*refs):
        ins, got = refs[:nt], refs[nt:2 * nt]
        send_sems, recv_sems = refs[2 * nt:]
        x, y, c = _mesh_pos()
        cps = []
        for t in range(nt):
            rc = _remote(ins[t], got[t], send_sems.at[t], recv_sems.at[t], (x, y, 1 - c))
            rc.start()
            cps.append(rc)
        for rc in cps:
            rc.wait()

    out_shape = [_sds(hf.shape, hf.dtype) for hf in halves]
    scratch = [pltpu.SemaphoreType.DMA((nt,)), pltpu.SemaphoreType.DMA((nt,))]
    return _comm_call(body, name="grad_pair_join", n_in=nt, out_shape=out_shape, scratch=scratch)(*halves)


def _gather_small(packed):
    n = packed.shape[0]

    def body(in_ref, out_ref, send_sems, recv_sems, loc_sem):
        x, y, c = _mesh_pos()
        me = 4 * x + 2 * y + c
        lc = pltpu.make_async_copy(in_ref, out_ref.at[me], loc_sem)
        lc.start()
        peers = []
        for k in range(1, NDEV):
            px, py, pc = x ^ (k >> 2), y ^ ((k >> 1) & 1), c ^ (k & 1)
            rc = pltpu.make_async_remote_copy(src_ref=in_ref, dst_ref=out_ref.at[me], send_sem=send_sems.at[k - 1],
                                              recv_sem=recv_sems.at[k - 1], device_id=(px, py, pc), device_id_type=MESH)
            rc.start()
            peers.append((k, px, py, pc))
        lc.wait()
        for k, px, py, pc in peers:
            pltpu.make_async_remote_copy(src_ref=in_ref, dst_ref=out_ref.at[4 * px + 2 * py + pc],
                                         send_sem=send_sems.at[k - 1], recv_sem=recv_sems.at[k - 1],
                                         device_id=(px, py, pc), device_id_type=MESH).wait()

    scratch = [pltpu.SemaphoreType.DMA((NDEV - 1,)), pltpu.SemaphoreType.DMA((NDEV - 1,)), pltpu.SemaphoreType.DMA]
    return _comm_call(body, name="gather_small_grads", n_in=1, out_shape=[_sds((NDEV, n, 128))],
                      scratch=scratch)(packed)[0]


def _row_tile(r, c):
    t = r
    while t * c * 4 > (1 << 20) and t % 16 == 0:
        t //= 2
    return t


def _k_pair_add(full, got, name):
    g, r, c = full.shape
    hr = r // 2
    tr = _row_tile(hr, c)
    nh = hr // tr

    def body(a_ref, b_ref, o_ref):
        o_ref[...] = (a_ref[...] + b_ref[...]).astype(_WIRE)

    mine = pl.BlockSpec((None, tr, c), lambda i, j: (i, lax.axis_index("c") * nh + j, 0))
    spec = pl.BlockSpec((None, tr, c), lambda i, j: (i, j, 0))
    return _pc(body, name=name, grid=(g, nh), in_specs=[mine, spec], out_specs=[spec],
               out_shape=[_sds((g, hr, c), _WIRE)])(full, got)[0]


def _k_chip_sum(parts, slots, name):
    _, r, c = parts.shape
    tr = _row_tile(r, c)

    def body(a_ref, s_ref, o_ref):
        acc = a_ref[...].astype(F32)
        for k in range(3):
            acc = acc + s_ref[k].astype(F32)
        o_ref[...] = acc

    own = pl.BlockSpec((None, tr, c), lambda i: (2 * lax.axis_index("x") + lax.axis_index("y"), i, 0))
    return _pc(body, name=name, grid=(r // tr,), in_specs=[own, pl.BlockSpec((3, tr, c), lambda i: (0, i, 0))],
               out_specs=[_row(tr, c)], out_shape=[_sds((r, c))])(parts, slots)[0]


def _adam(w, g, m, v):
    m = ADAM_B1 * m + (1.0 - ADAM_B1) * g
    v = ADAM_B2 * v + (1.0 - ADAM_B2) * (g * g)
    m_hat = m / (1.0 - ADAM_B1 ** ADAM_STEP)
    v_hat = v / (1.0 - ADAM_B2 ** ADAM_STEP)
    return -ADAM_LR * (m_hat / (jnp.sqrt(v_hat) + ADAM_EPS) + ADAM_WD * w), m, v


def _k_adam(w, mine, theirs, m, v, name):
    r, c = w.shape
    hr = r // 2
    tr = _row_tile(hr, c)
    nh = hr // tr

    def body(w_ref, a_ref, b_ref, m_ref, v_ref, g_ref, d_ref, mo_ref, vo_ref):
        upper = (pl.program_id(0) >= nh).astype(jnp.int32)
        g = jnp.where(upper == lax.axis_index("c"), a_ref[...], b_ref[...])
        g_ref[...] = g
        d_ref[...], mo_ref[...], vo_ref[...] = _adam(w_ref[...], g, m_ref[...], v_ref[...])

    hspec = pl.BlockSpec((tr, c), lambda i: (jnp.where(i >= nh, i - nh, i), 0))
    return _pc(body, name=name, grid=(r // tr,), in_specs=[_row(tr, c), hspec, hspec, _row(tr, c), _row(tr, c)],
               out_specs=[_row(tr, c)] * 4, out_shape=[_sds((r, c))] * 4)(w, mine, theirs, m, v)


def _k_sum8(a):
    _, n, _ = a.shape

    def body(a_ref, o_ref):
        acc = a_ref[0]
        for k in range(1, NDEV):
            acc = acc + a_ref[k]
        o_ref[...] = acc

    return _pc(body, name="sum_small_grads", grid=(1,), in_specs=[_acc(a.shape)], out_specs=[_acc((n, 128))],
               out_shape=[_sds((n, 128))])(a)[0]


def _k_adam_small(w, g, m, v):
    n = w.shape[0]

    def body(w_ref, g_ref, m_ref, v_ref, d_ref, mo_ref, vo_ref):
        d_ref[...], mo_ref[...], vo_ref[...] = _adam(w_ref[...], g_ref[...], m_ref[...], v_ref[...])

    return _pc(body, name="adam_small", grid=(1,), in_specs=[_acc((n, 128))] * 4, out_specs=[_acc((n, 128))] * 3,
               out_shape=[_sds((n, 128))] * 3)(w, g, m, v)


def _pack(vals):
    rows = []
    for a in vals:
        flat = a.reshape(-1)
        n = -(-flat.shape[0] // 1024) * 1024
        rows.append(jnp.pad(flat, (0, n - flat.shape[0])).reshape(n // 128, 128))
    return jnp.concatenate(rows, axis=0)


def _unpack(packed, shapes):
    out, off = [], 0
    for sh in shapes:
        size = int(np.prod(sh))
        n = -(-size // 1024) * 1024
        out.append(packed[off // 128:(off + n) // 128].reshape(-1)[:size].reshape(sh))
        off += n
    return out


_WEIGHTS = ["attn_norm", "w_in", "a_q_norm", "a_k_norm", "b_q_norm", "b_k_norm", "b_sinks", "mem_norm", "w_mem_kv",
            "m_q_norm", "m_k_norm", "w_o_a", "w_o_b", "w_o_m", "w_gate", "b_gate", "w_out", "ffn_norm", "w_up",
            "conv_w", "conv_b", "w_down"]
_BIG = ["w_in", "w_mem_kv", "w_o_a", "w_o_b", "w_o_m", "w_gate", "w_out", "w_up", "w_down"]
_SMALL = [n for n in _WEIGHTS if n not in _BIG]


def kernel(x, mem, positions, attn_norm, w_in, a_q_norm, a_k_norm, b_q_norm, b_k_norm, b_sinks, mem_norm, w_mem_kv, m_q_norm, m_k_norm, w_o_a, w_o_b, w_o_m, w_gate, b_gate, w_out, ffn_norm, w_up, conv_w, conv_b, w_down, loss_target, m_attn_norm, m_w_in, m_a_q_norm, m_a_k_norm, m_b_q_norm, m_b_k_norm, m_b_sinks, m_mem_norm, m_w_mem_kv, m_m_q_norm, m_m_k_norm, m_w_o_a, m_w_o_b, m_w_o_m, m_w_gate, m_b_gate, m_w_out, m_ffn_norm, m_w_up, m_conv_w, m_conv_b, m_w_down, v_attn_norm, v_w_in, v_a_q_norm, v_a_k_norm, v_b_q_norm, v_b_k_norm, v_b_sinks, v_mem_norm, v_w_mem_kv, v_m_q_norm, v_m_k_norm, v_w_o_a, v_w_o_b, v_w_o_m, v_w_gate, v_b_gate, v_w_out, v_ffn_norm, v_w_up, v_conv_w, v_conv_b, v_w_down):
    given = dict(locals())
    w = {n: given[n][0] for n in _WEIGHTS}
    m1 = {n: given["m_" + n][0] for n in _WEIGHTS}
    m2 = {n: given["v_" + n][0] for n in _WEIGHTS}

    gathered = _gather_shards([w[n].astype(_MM) for n in _BIG] + [w["conv_w"]])
    wts = dict(zip(_BIG + ["conv_w"], gathered))
    for n in ("w_mem_kv", "w_out", "w_down"):
        wts[n] = wts[n].reshape(-1, wts[n].shape[-1])
    small = {n: (w[n][None, :] if w[n].ndim == 1 else w[n]) for n in _SMALL if n != "conv_w"}

    loss, grad_x, big, sml = _local_step(x[0], mem[0], positions[0], loss_target[0], small, wts)
    loss = lax.psum(loss, ("x", "y", "c"))

    got = _pair_split([big[n] for n in _BIG])
    parts = [_k_pair_add(big[n], b, "pair_add_" + n) for n, b in zip(_BIG, got)]
    slots = _chip_scatter(parts)
    mine = [_k_chip_sum(a, b, "chip_add_" + n) for n, a, b in zip(_BIG, parts, slots)]
    theirs = _pair_join(mine)
    grads = {}

    shapes = [sml[n].shape for n in _SMALL]
    gsm = dict(zip(_SMALL, _unpack(_k_sum8(_gather_small(_pack([sml[n] for n in _SMALL]))), shapes)))
    nu = w["conv_w"].shape[1]
    chip = 2 * lax.axis_index("x") + lax.axis_index("y")
    gsm["conv_w"] = lax.dynamic_slice_in_dim(gsm["conv_w"], chip * nu, nu, axis=1)
    for n in _SMALL:
        grads[n] = gsm[n].reshape(w[n].shape)

    delta, new_m, new_v = {}, {}, {}
    for n, a, b in zip(_BIG, mine, theirs):
        grads[n], delta[n], new_m[n], new_v[n] = _k_adam(w[n], a, b, m1[n], m2[n], "adam_" + n)
    pk = lambda d: _pack([d[n] for n in _SMALL])
    sshapes = [w[n].shape for n in _SMALL]
    for dst, packed in zip((delta, new_m, new_v), _k_adam_small(pk(w), pk(grads), pk(m1), pk(m2))):
        dst.update(zip(_SMALL, _unpack(packed, sshapes)))

    lead = lambda d: [d[n][None] for n in _WEIGHTS]
    return (loss, grad_x[None], *lead(grads), *lead(delta), *lead(new_m), *lead(new_v))
```

```python
import math

import jax
import jax.numpy as jnp
import numpy as np
from jax import lax
from jax.experimental import pallas as pl
from jax.experimental.pallas import tpu as pltpu

F32 = jnp.float32
_MM = jnp.bfloat16
_WIRE = jnp.bfloat16

D_MODEL = 1024
HEAD = 64
BLK = 128
A_GROUPS = ((128, 1), (512, 4), (2048, 16))
A_HEADS = 4
A_W = A_HEADS * HEAD
B_QH = 8
B_KVH = 2
B_WINDOW = 128
M_HEADS = 4
M_HD = 128
M_W = M_HEADS * M_HD
D_FF = 2816
EPS = 1e-6
NEG = -1e30
ROPE_THETA = 500000.0
ROPE_ROT = 16
CHIPS = 4
NDEV = 8
ADAM_LR, ADAM_B1, ADAM_B2, ADAM_EPS, ADAM_WD, ADAM_STEP = 0.001, 0.9, 0.999, 1e-08, 0.01, 10
VMEM_LIMIT = 58 * 1024 * 1024
MESH = pl.DeviceIdType.MESH


def _pc(body, *, name, grid, in_specs, out_specs, out_shape, scratch=()):
    return pl.pallas_call(
        body, name=name, grid=grid, in_specs=in_specs, out_specs=out_specs, out_shape=out_shape,
        scratch_shapes=list(scratch),
        compiler_params=pltpu.CompilerParams(dimension_semantics=("arbitrary",) * len(grid),
                                             vmem_limit_bytes=VMEM_LIMIT))


def _row(ts, c, col=0):
    return pl.BlockSpec((ts, c), lambda i: (i, col))


def _res(shape):
    n = len(shape)
    return pl.BlockSpec(tuple(shape), lambda i: (0,) * n, pipeline_mode=pl.Buffered(1))


def _acc(shape):
    n = len(shape)
    return pl.BlockSpec(tuple(shape), lambda i: (0,) * n)


def _sds(shape, dtype=F32):
    return jax.ShapeDtypeStruct(tuple(shape), dtype)


def _dot(a, b):
    return jnp.dot(a.astype(_MM), b.astype(_MM), preferred_element_type=F32)


def _dot_nt(a, b):
    return lax.dot_general(a.astype(_MM), b.astype(_MM), (((1,), (1,)), ((), ())), preferred_element_type=F32)


def _dot_tn(a, b):
    return lax.dot_general(a.astype(_MM), b.astype(_MM), (((0,), (0,)), ((), ())), preferred_element_type=F32)


def _sum8(v):
    ts, c = v.shape
    return jnp.sum(v.reshape(ts // 8, 8, c), axis=0)


def _sigmoid(z):
    return 1.0 / (1.0 + jnp.exp(-z))


def _rms(x):
    r = lax.rsqrt(jnp.mean(x * x, axis=-1, keepdims=True) + EPS)
    return x * r, r


def _rms_bwd(dy, xh, r, gain):
    z = dy * gain
    return r * (z - xh * jnp.mean(z * xh, axis=-1, keepdims=True))


def _split_hi_lo(v):
    hi = v.astype(_MM)
    return hi, (v - hi.astype(F32)).astype(_MM)


def _lane_head(shape):
    return lax.shift_right_logical(lax.broadcasted_iota(jnp.int32, shape, len(shape) - 1), 6)


def _seg_sum64(v):
    w = v.shape[1]
    e = jnp.where(_lane_head((w, w)) == lax.shift_right_logical(lax.broadcasted_iota(jnp.int32, (w, w), 0), 6),
                  1.0, 0.0).astype(_MM)
    hi, lo = _split_hi_lo(v)
    return jnp.dot(hi, e, preferred_element_type=F32) + jnp.dot(lo, e, preferred_element_type=F32)


def _seg_norm(x, seg):
    if seg == HEAD:
        r = lax.rsqrt(_seg_sum64(x * x) * (1.0 / HEAD) + EPS)
        return x * r, r
    w = x.shape[1]
    xh, rr = [], []
    for s in range(w // seg):
        xs = x[:, s * seg:(s + 1) * seg]
        r = lax.rsqrt(jnp.mean(xs * xs, axis=-1, keepdims=True) + EPS)
        xh.append(xs * r)
        rr.append(jnp.broadcast_to(r, xs.shape))
    return jnp.concatenate(xh, axis=1), jnp.concatenate(rr, axis=1)


def _seg_mean(v, seg):
    if seg == HEAD:
        return _seg_sum64(v) * (1.0 / HEAD)
    w = v.shape[1]
    out = []
    for s in range(w // seg):
        vs = v[:, s * seg:(s + 1) * seg]
        out.append(jnp.broadcast_to(jnp.mean(vs, axis=-1, keepdims=True), vs.shape))
    return jnp.concatenate(out, axis=1)


def _rope(t, c, sa, sb):
    out = []
    for cb in range(t.shape[1] // 128):
        tc = t[:, cb * 128:(cb + 1) * 128]
        out.append(tc * c + pltpu.roll(tc, 120, 1) * sa + pltpu.roll(tc, 8, 1) * sb)
    return jnp.concatenate(out, axis=1) if len(out) > 1 else out[0]


def _rope_bwd(dy, c, sa, sb):
    out = []
    for cb in range(dy.shape[1] // 128):
        dc = dy[:, cb * 128:(cb + 1) * 128]
        out.append(dc * c + pltpu.roll(dc * sa, 8, 1) + pltpu.roll(dc * sb, 120, 1))
    return jnp.concatenate(out, axis=1) if len(out) > 1 else out[0]


def _rope_freqs():
    c = np.float32(-2.0 * math.log(ROPE_THETA) / ROPE_ROT)
    return [float(v) for v in np.exp(np.arange(ROPE_ROT // 2, dtype=np.float32) * c)]


def _k_rope(pos2d):
    n = pos2d.shape[0]
    freqs = _rope_freqs()
    nf = len(freqs)

    def body(p_ref, c_ref, s_ref):
        p = p_ref[...].astype(F32)
        for f in range(nf):
            ang = p * freqs[f]
            c_ref[f] = jnp.cos(ang)
            s_ref[f] = jnp.sin(ang)

    return _pc(body, name="rope_tables", grid=(1,),
               in_specs=[_acc((n, 128))], out_specs=[_acc((nf, n, 128)), _acc((nf, n, 128))],
               out_shape=[_sds((nf, n, 128)), _sds((nf, n, 128))])(pos2d)


def _rope_tables(pos_rows):
    r = pos_rows.shape[0]
    cos, sin = _k_rope(pos_rows.reshape(r // 128, 128))
    half = ROPE_ROT // 2
    cos = cos.reshape(half, r).T
    sin = sin.reshape(half, r).T
    one = jnp.ones((r, HEAD - ROPE_ROT), F32)
    zero = jnp.zeros((r, HEAD - ROPE_ROT), F32)
    z8 = jnp.zeros((r, half), F32)
    c64 = jnp.concatenate([cos, cos, one], axis=1)
    sa64 = jnp.concatenate([-sin, z8, zero], axis=1)
    sb64 = jnp.concatenate([z8, sin, zero], axis=1)
    return tuple(jnp.concatenate([t, t], axis=1) for t in (c64, sa64, sb64))


def _k_in(x, g1, w_in, w_gate, b_gate):
    s = x.shape[0]
    ts = min(256, s)
    nin, ng = w_in.shape[2], w_gate.shape[2]
    ncol = CHIPS * nin
    a_cols = 3 * A_W
    offs = [0, a_cols, 2 * a_cols, 3 * a_cols, 3 * a_cols + B_QH * HEAD,
            3 * a_cols + (B_QH + B_KVH) * HEAD, 3 * a_cols + (B_QH + 2 * B_KVH) * HEAD, ncol]

    def body(x_ref, g_ref, wi_ref, wg_ref, bg_ref, h_ref, a0, a1, a2, qb, kb, vb, mq, gt_ref, p_scr):
        xh, _ = _rms(x_ref[...])
        h = (xh * g_ref[...]).astype(_MM)
        h_ref[...] = h
        for j in range(CHIPS):
            p_scr[:, j * nin:(j + 1) * nin] = jnp.dot(h, wi_ref[j], preferred_element_type=F32)
            z = jnp.dot(h, wg_ref[j], preferred_element_type=F32) + bg_ref[:, j * ng:(j + 1) * ng]
            gt_ref[:, j * ng:(j + 1) * ng] = _sigmoid(z)
        for k, ref in enumerate((a0, a1, a2, qb, kb, vb, mq)):
            ref[...] = p_scr[:, offs[k]:offs[k + 1]]

    widths = [offs[k + 1] - offs[k] for k in range(7)]
    return _pc(
        body, name="in_proj", grid=(s // ts,),
        in_specs=[_row(ts, D_MODEL), _res((1, D_MODEL)), _res(w_in.shape), _res(w_gate.shape), _res(b_gate.shape)],
        out_specs=[_row(ts, D_MODEL)] + [_row(ts, w) for w in widths] + [_row(ts, CHIPS * ng)],
        out_shape=[_sds((s, D_MODEL), _MM)] + [_sds((s, w)) for w in widths] + [_sds((s, CHIPS * ng))],
        scratch=[pltpu.VMEM((ts, ncol), F32)])(x, g1, w_in, w_gate, b_gate)


def _k_prep(srcs, gq, gk, tabs, *, wq, wk, rows_per_gain, name):
    rows = srcs[0][0].shape[0]
    ts = min(256, rows)

    def body(q_ref, k_ref, v_ref, gq_ref, gk_ref, c_ref, sa_ref, sb_ref, qn_ref, kn_ref, vn_ref):
        c, sa, sb = c_ref[...], sa_ref[...], sb_ref[...]
        qh, _ = _seg_norm(q_ref[...], HEAD)
        qn_ref[...] = _rope(qh * gq_ref[...], c, sa, sb).astype(_MM)
        kh, _ = _seg_norm(k_ref[...], HEAD)
        kn_ref[...] = _rope(kh * gk_ref[...], c, sa, sb).astype(_MM)
        vn_ref[...] = v_ref[...].astype(_MM)

    gspec = lambda w: pl.BlockSpec((None, 1, w), lambda i: ((i * ts) // rows_per_gain, 0, 0))
    return _pc(
        body, name=name, grid=(rows // ts,),
        in_specs=[_row(ts, wq, srcs[0][1]), _row(ts, wk, srcs[1][1]), _row(ts, wk, srcs[2][1]),
                  gspec(wq), gspec(wk)] + [_row(ts, 128)] * 3,
        out_specs=[_row(ts, wq), _row(ts, wk), _row(ts, wk)],
        out_shape=[_sds((rows, wq), _MM), _sds((rows, wk), _MM), _sds((rows, wk), _MM)])(
            srcs[0][0], srcs[1][0], srcs[2][0], gq, gk, *tabs)


def _first_flag(b, segs, nb):
    first = b >= nb
    for k, (start, period) in enumerate(segs):
        end = segs[k + 1][0] if k + 1 < len(segs) else nb
        first = first | ((b >= start) & (b < end) & (lax.rem(b - start, jnp.int32(period)) == 0))
    return first


def _band_bias(thr, with_cur):
    qi = lax.broadcasted_iota(jnp.int32, (BLK, BLK), 0)
    kj = lax.broadcasted_iota(jnp.int32, (BLK, BLK), 1)
    prev = jnp.where(kj >= qi + thr, 0.0, NEG)
    return jnp.concatenate([prev, jnp.where(kj <= qi, 0.0, NEG)], axis=1) if with_cur else prev


def _blockdiag(t4):
    head = _lane_head((1, A_W))
    return jnp.concatenate([t4 * jnp.where(head == h, 1.0, 0.0).astype(t4.dtype) for h in range(A_HEADS)], axis=0)


def _fold_diag(t, n):
    head = _lane_head((n, A_W))
    out = t[3 * n:4 * n]
    for h in (2, 1, 0):
        out = jnp.where(head == h, t[h * n:(h + 1) * n], out)
    return out


def _expand_heads(cols):
    n = cols[0].shape[0]
    head = _lane_head((n, A_W))
    out = jnp.broadcast_to(cols[3], (n, A_W))
    for h in (2, 1, 0):
        out = jnp.where(head == h, cols[h], out)
    return out


def _unit_kv(p_ref, c_ref, u, shared):
    if not shared:
        return jnp.concatenate([p_ref[:, u * A_W:(u + 1) * A_W], c_ref[:, u * A_W:(u + 1) * A_W]], axis=0)
    kg = jnp.concatenate([p_ref[:, u * HEAD:(u + 1) * HEAD], c_ref[:, u * HEAD:(u + 1) * HEAD]], axis=0)
    return jnp.concatenate([kg] * A_HEADS, axis=1)


def _k_band_fwd(qn, kn, vn, *, hq, hk, max_dist, segs, sink, name):
    rows = qn.shape[0]
    nb = rows // BLK
    units = hq // A_HEADS
    shared = hk != hq
    wq, wk = hq * HEAD, hk * HEAD
    scale = HEAD ** -0.5

    def body(*refs):
        if sink is None:
            q_ref, kc_ref, kp_ref, vc_ref, vp_ref, o_ref, l_ref = refs
        else:
            q_ref, kc_ref, kp_ref, vc_ref, vp_ref, sk_ref, o_ref, l_ref = refs
        b = pl.program_id(0)
        bias = _band_bias(jnp.where(_first_flag(b, segs, nb), 1 << 20, BLK - max_dist), True)
        for u in range(units):
            us = slice(u * A_W, (u + 1) * A_W)
            kb = _blockdiag(_unit_kv(kp_ref, kc_ref, u, shared))
            vb = _blockdiag(_unit_kv(vp_ref, vc_ref, u, shared))
            s_all = _dot_nt(q_ref[:, us], kb) * scale
            ps, ls = [], []
            for h in range(A_HEADS):
                s = s_all[:, h * 2 * BLK:(h + 1) * 2 * BLK] + bias
                m = jnp.max(s, axis=-1, keepdims=True)
                e = jnp.exp(s - m)
                lse = m + jnp.log(jnp.sum(e, axis=-1, keepdims=True))
                if sink is not None:
                    sk = sk_ref[u * A_HEADS + h]
                    mx = jnp.maximum(lse, sk)
                    lse = mx + jnp.log(jnp.exp(lse - mx) + jnp.exp(sk - mx))
                ps.append((e * jnp.exp(m - lse)).astype(_MM))
                ls.append(lse)
            o_ref[:, us] = _dot(jnp.concatenate(ps, axis=1), vb)
            l_ref[:, us] = _expand_heads(ls)

    cur = lambda w: pl.BlockSpec((BLK, w), lambda i: (i, 0))
    prev = lambda w: pl.BlockSpec((BLK, w), lambda i: (jnp.maximum(i - 1, 0), 0))
    in_specs = [cur(wq), cur(wk), prev(wk), cur(wk), prev(wk)]
    args = [qn, kn, kn, vn, vn]
    if sink is not None:
        in_specs.append(pl.BlockSpec(memory_space=pltpu.SMEM))
        args.append(sink)
    return _pc(body, name=name, grid=(nb,), in_specs=in_specs, out_specs=[cur(wq), cur(wq)],
               out_shape=[_sds((rows, wq)), _sds((rows, wq))])(*args)


def _k_memkv(mem, mem_norm, w_kv, m_k_norm):
    n = mem.shape[0]

    def body(m_ref, g_ref, w_ref, gk_ref, mn_ref, kv_ref, mk_ref, mv_ref):
        mh, _ = _rms(m_ref[...])
        mn = (mh * g_ref[...]).astype(_MM)
        mn_ref[...] = mn
        kv = jnp.dot(mn, w_ref[...], preferred_element_type=F32)
        kv_ref[...] = kv
        kh, _ = _seg_norm(kv[:, :M_W], M_HD)
        mk_ref[...] = (kh * gk_ref[...]).astype(_MM)
        mv_ref[...] = kv[:, M_W:].astype(_MM)

    return _pc(body, name="mem_kv", grid=(1,),
               in_specs=[_acc((n, D_MODEL)), _acc((1, D_MODEL)), _acc(w_kv.shape), _acc((1, M_W))],
               out_specs=[_acc((n, D_MODEL)), _acc((n, 2 * M_W)), _acc((n, M_W)), _acc((n, M_W))],
               out_shape=[_sds((n, D_MODEL), _MM), _sds((n, 2 * M_W)), _sds((n, M_W), _MM), _sds((n, M_W), _MM)])(
                   mem, mem_norm, w_kv, m_k_norm)


def _mem_probs(q, mk):
    sc = _dot_nt(q, mk) * (M_HD ** -0.5)
    e = jnp.exp(sc - jnp.max(sc, axis=-1, keepdims=True))
    return e / jnp.sum(e, axis=-1, keepdims=True)


def _k_mem_fwd(m_q, gq, mk, mv):
    s = m_q.shape[0]
    n = mk.shape[0]
    ts = min(256, s)

    def body(q_ref, g_ref, mk_ref, mv_ref, o_ref):
        qh, _ = _seg_norm(q_ref[...], M_HD)
        qn = (qh * g_ref[...]).astype(_MM)
        for h in range(M_HEADS):
            hs = slice(h * M_HD, (h + 1) * M_HD)
            o_ref[:, hs] = _dot(_mem_probs(qn[:, hs], mk_ref[:, hs]), mv_ref[:, hs])

    return _pc(body, name="mem_attn", grid=(s // ts,),
               in_specs=[_row(ts, M_W), _res((1, M_W)), _res((n, M_W)), _res((n, M_W))],
               out_specs=[_row(ts, M_W)], out_shape=[_sds((s, M_W))])(m_q, gq, mk, mv)[0]


def _group_weights(l0, l1, l2):
    m = jnp.maximum(jnp.maximum(l0, l1), l2)
    e0, e1, e2 = jnp.exp(l0 - m), jnp.exp(l1 - m), jnp.exp(l2 - m)
    inv = 1.0 / (e0 + e1 + e2)
    return e0 * inv, e1 * inv, e2 * inv


def _branch_products(oa, ob, om, woa_ref, wob_ref, wom_ref, j):
    return _dot(oa, woa_ref[j]), _dot(ob, wob_ref[j]), _dot(om, wom_ref[j])


def _k_merge(og, lg, o_b, o_m, gates, x, w_oa, w_ob, w_om, w_out, g2):
    s = x.shape[0]
    ts = min(256, s)
    nc = w_oa.shape[2]

    def body(o0, o1, o2, l0, l1, l2, ob_ref, om_ref, gt_ref, x_ref, woa, wob, wom, wout, g_ref,
             oa_ref, mer_ref, x1_ref, h2_ref, m_scr):
        w0, w1, w2 = _group_weights(l0[...], l1[...], l2[...])
        oa = w0 * o0[...] + w1 * o1[...] + w2 * o2[...]
        oa_ref[...] = oa
        ob, om = ob_ref[...], om_ref[...]
        for j in range(CHIPS):
            pa, pb, pm = _branch_products(oa, ob, om, woa, wob, wom, j)
            cs = lambda br: slice(br * D_MODEL + j * nc, br * D_MODEL + (j + 1) * nc)
            m_scr[:, j * nc:(j + 1) * nc] = gt_ref[:, cs(0)] * pa + gt_ref[:, cs(1)] * pb + gt_ref[:, cs(2)] * pm
        mer = m_scr[...].astype(_MM)
        mer_ref[...] = mer
        x1 = x_ref[...] + jnp.dot(mer, wout[...], preferred_element_type=F32)
        x1_ref[...] = x1
        xh, _ = _rms(x1)
        h2_ref[...] = (xh * g_ref[...]).astype(_MM)

    return _pc(
        body, name="merge_out", grid=(s // ts,),
        in_specs=[_row(ts, A_W)] * 6 + [_row(ts, B_QH * HEAD), _row(ts, M_W), _row(ts, 3 * D_MODEL), _row(ts, D_MODEL),
                                         _res(w_oa.shape), _res(w_ob.shape), _res(w_om.shape), _res(w_out.shape),
                                         _res((1, D_MODEL))],
        out_specs=[_row(ts, A_W), _row(ts, D_MODEL), _row(ts, D_MODEL), _row(ts, D_MODEL)],
        out_shape=[_sds((s, A_W)), _sds((s, D_MODEL), _MM), _sds((s, D_MODEL)), _sds((s, D_MODEL), _MM)],
        scratch=[pltpu.VMEM((ts, D_MODEL), F32)])(*og, *lg, o_b, o_m, gates, x, w_oa, w_ob, w_om, w_out, g2)


def _k_up(h2, w_up):
    s = h2.shape[0]
    ts = min(256, s)
    nu = w_up.shape[2]

    def body(h_ref, w_ref, u_ref):
        h = h_ref[...]
        for j in range(CHIPS):
            u_ref[:, j * nu:(j + 1) * nu] = jnp.dot(h, w_ref[j], preferred_element_type=F32)

    return _pc(body, name="up_proj", grid=(s // ts,), in_specs=[_row(ts, D_MODEL), _res(w_up.shape)],
               out_specs=[_row(ts, CHIPS * nu)], out_shape=[_sds((s, CHIPS * nu))])(h2, w_up)[0]


def _shift_down(v, halo, k):
    ts = v.shape[0]
    row = lax.broadcasted_iota(jnp.int32, v.shape, 0)
    out = pltpu.roll(v, k, 0)
    for r in range(k):
        out = jnp.where(row == r, halo[8 - k + r:8 - k + r + 1, :], out)
    return out


def _shift_up(v, halo, k):
    ts = v.shape[0]
    row = lax.broadcasted_iota(jnp.int32, v.shape, 0)
    out = pltpu.roll(v, ts - k, 0)
    for r in range(k):
        out = jnp.where(row == ts - k + r, halo[r:r + 1, :], out)
    return out


def _k_ffn(u, conv_w, conv_b, w_down, x1, target):
    s = u.shape[0]
    ts = min(128, s)
    nu = conv_w.shape[2]
    half = CHIPS // 2

    def body(u_ref, uh_ref, cw_ref, cb_ref, wd_ref, x1_ref, t_ref, dy_ref, f_ref, dc_ref, loss_ref, c_scr, f_scr):
        i = pl.program_id(0)
        halo = jnp.where(i > 0, uh_ref[...], 0.0)
        for j in range(CHIPS):
            cs = slice(j * nu, (j + 1) * nu)
            uj = u_ref[:, cs]
            hj = halo[:, cs]
            c_scr[:, cs] = (cb_ref[:, cs] + cw_ref[j, 0:1, :] * _shift_down(uj, hj, 2)
                            + cw_ref[j, 1:2, :] * _shift_down(uj, hj, 1) + cw_ref[j, 2:3, :] * uj)
        for j in range(half):
            a = c_scr[:, j * nu:(j + 1) * nu]
            g = c_scr[:, (half + j) * nu:(half + j + 1) * nu]
            f_scr[:, j * nu:(j + 1) * nu] = (a * _sigmoid(a) * g).astype(_MM)
        f = f_scr[...]
        f_ref[...] = f
        y = x1_ref[...] + jnp.dot(f, wd_ref[...], preferred_element_type=F32)
        err = y - t_ref[...]
        dy = err * (1.0 / D_MODEL)
        dy_ref[...] = dy

        @pl.when(i == 0)
        def _():
            loss_ref[...] = jnp.zeros_like(loss_ref)

        loss_ref[...] += _sum8(err * err)
        df = _dot_nt(dy, wd_ref[...])
        for j in range(half):
            a = c_scr[:, j * nu:(j + 1) * nu]
            g = c_scr[:, (half + j) * nu:(half + j + 1) * nu]
            sa = _sigmoid(a)
            dfj = df[:, j * nu:(j + 1) * nu]
            dc_ref[:, j * nu:(j + 1) * nu] = dfj * g * (sa * (1.0 + a * (1.0 - sa)))
            dc_ref[:, (half + j) * nu:(half + j + 1) * nu] = dfj * (a * sa)

    wide = CHIPS * nu
    return _pc(
        body, name="conv_ffn", grid=(s // ts,),
        in_specs=[_row(ts, wide), pl.BlockSpec((8, wide), lambda i: (jnp.maximum(i * (ts // 8) - 1, 0), 0)),
                  _res(conv_w.shape), _res((1, wide)), _res(w_down.shape), _row(ts, D_MODEL), _row(ts, D_MODEL)],
        out_specs=[_row(ts, D_MODEL), _row(ts, D_FF), _row(ts, wide), _acc((8, D_MODEL))],
        out_shape=[_sds((s, D_MODEL)), _sds((s, D_FF), _MM), _sds((s, wide)), _sds((8, D_MODEL))],
        scratch=[pltpu.VMEM((ts, wide), F32), pltpu.VMEM((ts, D_FF), _MM)])(u, u, conv_w, conv_b, w_down, x1, target)


def _k_conv_bwd(dc, u, conv_w, w_up, x1, g2, dy):
    s = u.shape[0]
    ts = min(128, s)
    nu = conv_w.shape[2]
    wide = CHIPS * nu
    last = s // ts - 1

    def body(dc_ref, dn_ref, u_ref, uh_ref, cw_ref, wu_ref, x1_ref, g_ref, dy_ref,
             dx1_ref, du_ref, cacc_ref, gacc_ref):
        i = pl.program_id(0)

        @pl.when(i == 0)
        def _():
            cacc_ref[...] = jnp.zeros_like(cacc_ref)
            gacc_ref[...] = jnp.zeros_like(gacc_ref)

        uhalo = jnp.where(i > 0, uh_ref[...], 0.0)
        dhalo = jnp.where(i < last, dn_ref[...], 0.0)
        dh2 = jnp.zeros((ts, D_MODEL), F32)
        for j in range(CHIPS):
            cs = slice(j * nu, (j + 1) * nu)
            dcj, uj = dc_ref[:, cs], u_ref[:, cs]
            cacc_ref[0, :, cs] += _sum8(dcj)
            cacc_ref[1, :, cs] += _sum8(dcj * _shift_down(uj, uhalo[:, cs], 2))
            cacc_ref[2, :, cs] += _sum8(dcj * _shift_down(uj, uhalo[:, cs], 1))
            cacc_ref[3, :, cs] += _sum8(dcj * uj)
            du = (cw_ref[j, 2:3, :] * dcj + cw_ref[j, 1:2, :] * _shift_up(dcj, dhalo[:, cs], 1)
                  + cw_ref[j, 0:1, :] * _shift_up(dcj, dhalo[:, cs], 2)).astype(_MM)
            du_ref[:, cs] = du
            dh2 = dh2 + _dot_nt(du, wu_ref[j])
        xh, r = _rms(x1_ref[...])
        gacc_ref[...] += _sum8(dh2 * xh)
        dx1_ref[...] = dy_ref[...] + _rms_bwd(dh2, xh, r, g_ref[...])

    return _pc(
        body, name="conv_up_bwd", grid=(s // ts,),
        in_specs=[_row(ts, wide),
                  pl.BlockSpec((8, wide), lambda i: (jnp.minimum((i + 1) * (ts // 8), s // 8 - 1), 0)),
                  _row(ts, wide), pl.BlockSpec((8, wide), lambda i: (jnp.maximum(i * (ts // 8) - 1, 0), 0)),
                  _res(conv_w.shape), _res(w_up.shape), _row(ts, D_MODEL), _res((1, D_MODEL)), _row(ts, D_MODEL)],
        out_specs=[_row(ts, D_MODEL), _row(ts, wide), _acc((4, 8, wide)), _acc((8, D_MODEL))],
        out_shape=[_sds((s, D_MODEL)), _sds((s, wide), _MM), _sds((4, 8, wide)), _sds((8, D_MODEL))])(
            dc, dc, u, u, conv_w, w_up, x1, g2, dy)


def _k_merge_bwd(dx1, og, lg, o_a, o_b, o_m, gates, w_oa, w_ob, w_om, w_out):
    s = dx1.shape[0]
    ts = min(256, s)
    nc = w_oa.shape[2]

    def body(dx_ref, o0, o1, o2, l0, l1, l2, oa_ref, ob_ref, om_ref, gt_ref, woa, wob, wom, wout,
             dgp_ref, dpa_ref, dpb_ref, dpm_ref, dog0, dog1, dog2, dl0, dl1, dl2, dob_ref, dom_ref, bacc_ref):
        i = pl.program_id(0)

        @pl.when(i == 0)
        def _():
            bacc_ref[...] = jnp.zeros_like(bacc_ref)

        dmer = _dot_nt(dx_ref[...], wout[...])
        oa, ob, om = oa_ref[...], ob_ref[...], om_ref[...]
        doa = jnp.zeros((ts, A_W), F32)
        dob = jnp.zeros((ts, B_QH * HEAD), F32)
        dom = jnp.zeros((ts, M_W), F32)
        for j in range(CHIPS):
            prods = _branch_products(oa, ob, om, woa, wob, wom, j)
            dmj = dmer[:, j * nc:(j + 1) * nc]
            dps = []
            for br, (p, dref) in enumerate(zip(prods, (dpa_ref, dpb_ref, dpm_ref))):
                cs = slice(br * D_MODEL + j * nc, br * D_MODEL + (j + 1) * nc)
                gt = gt_ref[:, cs]
                dgp = dmj * p * gt * (1.0 - gt)
                dgp_ref[:, cs] = dgp.astype(_MM)
                bacc_ref[:, cs] += _sum8(dgp)
                dp = (dmj * gt).astype(_MM)
                dref[:, j * nc:(j + 1) * nc] = dp
                dps.append(dp)
            doa = doa + _dot_nt(dps[0], woa[j])
            dob = dob + _dot_nt(dps[1], wob[j])
            dom = dom + _dot_nt(dps[2], wom[j])
        dob_ref[...] = dob
        dom_ref[...] = dom
        ws = _group_weights(l0[...], l1[...], l2[...])
        dsum = _seg_mean(doa * oa, HEAD) * float(HEAD)
        for w, dref, lref in zip(ws, (dog0, dog1, dog2), (dl0, dl1, dl2)):
            dref[...] = w * doa
            lref[...] = w * dsum

    return _pc(
        body, name="merge_out_bwd", grid=(s // ts,),
        in_specs=[_row(ts, D_MODEL)] + [_row(ts, A_W)] * 7 + [_row(ts, B_QH * HEAD), _row(ts, M_W), _row(ts, 3 * D_MODEL),
                                                              _res(w_oa.shape), _res(w_ob.shape), _res(w_om.shape),
                                                              _res(w_out.shape)],
        out_specs=[_row(ts, 3 * D_MODEL)] + [_row(ts, D_MODEL)] * 3 + [_row(ts, A_W)] * 6
        + [_row(ts, B_QH * HEAD), _row(ts, M_W), _acc((8, 3 * D_MODEL))],
        out_shape=[_sds((s, 3 * D_MODEL), _MM)] + [_sds((s, D_MODEL), _MM)] * 3 + [_sds((s, A_W))] * 6
        + [_sds((s, B_QH * HEAD)), _sds((s, M_W)), _sds((8, 3 * D_MODEL))])(
            dx1, *og, *lg, o_a, o_b, o_m, gates, w_oa, w_ob, w_om, w_out)


def _k_mem_bwd(m_q, gq, mk, mv, o_m, do_m):
    s = m_q.shape[0]
    n = mk.shape[0]
    ts = min(256, s)
    scale = M_HD ** -0.5

    def body(q_ref, g_ref, mk_ref, mv_ref, o_ref, do_ref, dq_ref, dmk_ref, dmv_ref, gacc_ref):
        i = pl.program_id(0)

        @pl.when(i == 0)
        def _():
            dmk_ref[...] = jnp.zeros_like(dmk_ref)
            dmv_ref[...] = jnp.zeros_like(dmv_ref)
            gacc_ref[...] = jnp.zeros_like(gacc_ref)

        gain = g_ref[...]
        qh, r = _seg_norm(q_ref[...], M_HD)
        qn = (qh * gain).astype(_MM)
        do = do_ref[...]
        delta = _seg_mean(do * o_ref[...], M_HD) * float(M_HD)
        dqn = []
        for h in range(M_HEADS):
            hs = slice(h * M_HD, (h + 1) * M_HD)
            p = _mem_probs(qn[:, hs], mk_ref[:, hs])
            dp = _dot_nt(do[:, hs], mv_ref[:, hs])
            ds = (p * (dp - delta[:, hs][:, 0:1]) * scale).astype(_MM)
            dqn.append(_dot(ds, mk_ref[:, hs]))
            dmk_ref[:, hs] += _dot_tn(ds, qn[:, hs])
            dmv_ref[:, hs] += _dot_tn(p, do[:, hs])
        dqn = jnp.concatenate(dqn, axis=1)
        gacc_ref[...] += _sum8(dqn * qh)
        z = dqn * gain
        dq_ref[...] = (r * (z - qh * _seg_mean(z * qh, M_HD))).astype(_MM)

    return _pc(
        body, name="mem_attn_bwd", grid=(s // ts,),
        in_specs=[_row(ts, M_W), _res((1, M_W)), _res((n, M_W)), _res((n, M_W)), _row(ts, M_W), _row(ts, M_W)],
        out_specs=[_row(ts, M_W), _acc((n, M_W)), _acc((n, M_W)), _acc((8, M_W))],
        out_shape=[_sds((s, M_W), _MM), _sds((n, M_W)), _sds((n, M_W)), _sds((8, M_W))])(m_q, gq, mk, mv, o_m, do_m)


def _k_memkv_bwd(mem, mem_norm, w_kv, m_k_norm, mem_n, kv, dmk, dmv):
    n = mem.shape[0]

    def body(m_ref, g_ref, w_ref, gk_ref, mn_ref, kv_ref, dmk_ref, dmv_ref, dw_ref, dg_ref, dgk_ref):
        gk = gk_ref[...]
        kh, r = _seg_norm(kv_ref[:, :M_W], M_HD)
        dmk = dmk_ref[...]
        dgk_ref[...] = _sum8(dmk * kh)
        z = dmk * gk
        dk = r * (z - kh * _seg_mean(z * kh, M_HD))
        dkv = jnp.concatenate([dk, dmv_ref[...]], axis=1).astype(_MM)
        dw_ref[...] = _dot_tn(mn_ref[...], dkv)
        dmn = _dot_nt(dkv, w_ref[...])
        mh, _ = _rms(m_ref[...])
        dg_ref[...] = _sum8(dmn * mh)

    return _pc(body, name="mem_kv_bwd", grid=(1,),
               in_specs=[_acc((n, D_MODEL)), _acc((1, D_MODEL)), _acc(w_kv.shape), _acc((1, M_W)), _acc((n, D_MODEL)),
                         _acc((n, 2 * M_W)), _acc((n, M_W)), _acc((n, M_W))],
               out_specs=[_acc(w_kv.shape), _acc((8, D_MODEL)), _acc((8, M_W))],
               out_shape=[_sds(w_kv.shape), _sds((8, D_MODEL)), _sds((8, M_W))])(
                   mem, mem_norm, w_kv, m_k_norm, mem_n, kv, dmk, dmv)


def _k_band_bwd(qn, kn, vn, do, lse, dl_or_o, *, hq, hk, max_dist, segs, sink, name):
    rows = qn.shape[0]
    nb = rows // BLK
    units = hq // A_HEADS
    shared = hk != hq
    wq, wk = hq * HEAD, hk * HEAD
    scale = HEAD ** -0.5

    def body(*refs):
        (qb_ref, qx_ref, kb_ref, kp_ref, vb_ref, vp_ref, dob_ref, dox_ref, lb_ref, lx_ref, eb_ref, ex_ref) = refs[:12]
        if sink is None:
            dq_ref, dk_ref, dv_ref = refs[12:]
        else:
            sk_ref, dq_ref, dk_ref, dv_ref, sacc_ref = refs[12:]
        b = pl.program_id(0)
        bias1 = _band_bias(jnp.where(_first_flag(b, segs, nb), 1 << 20, BLK - max_dist), True)
        bias2 = _band_bias(jnp.where(_first_flag(b + 1, segs, nb), 1 << 20, BLK - max_dist), False)
        if sink is not None:
            @pl.when(b == 0)
            def _():
                sacc_ref[...] = jnp.zeros_like(sacc_ref)

        for u in range(units):
            us = slice(u * A_W, (u + 1) * A_W)
            q4, qx4, do4, dox4 = qb_ref[:, us], qx_ref[:, us], dob_ref[:, us], dox_ref[:, us]
            k4, v4 = _unit_kv(kp_ref, kb_ref, u, shared), _unit_kv(vp_ref, vb_ref, u, shared)
            kd, vd = _blockdiag(k4), _blockdiag(v4)
            kdc, vdc = _blockdiag(k4[BLK:]), _blockdiag(v4[BLK:])
            if sink is None:
                dlt_b, dlt_x = eb_ref[:, us], ex_ref[:, us]
            else:
                dlt_b = _seg_sum64(do4.astype(F32) * eb_ref[:, us])
                dlt_x = _seg_sum64(dox4.astype(F32) * ex_ref[:, us])
            s1, dp1 = _dot_nt(q4, kd) * scale, _dot_nt(do4, vd)
            s2, dp2 = _dot_nt(qx4, kdc) * scale, _dot_nt(dox4, vdc)
            ds1, ds1c, p1c, ds2, p2 = [], [], [], [], []
            for h in range(A_HEADS):
                col = slice(u * A_W + h * HEAD, u * A_W + h * HEAD + 1)
                ucol = slice(h * HEAD, h * HEAD + 1)
                wide, narrow = slice(h * 2 * BLK, (h + 1) * 2 * BLK), slice(h * BLK, (h + 1) * BLK)
                l_b, l_x = lb_ref[:, col], lx_ref[:, col]
                p = jnp.exp(s1[:, wide] + bias1 - l_b)
                ds = p * (dp1[:, wide] - dlt_b[:, ucol]) * scale
                ds1.append(ds.astype(_MM))
                ds1c.append(ds[:, BLK:].astype(_MM))
                p1c.append(p[:, BLK:].astype(_MM))
                px = jnp.exp(s2[:, narrow] + bias2 - l_x)
                ds2.append((px * (dp2[:, narrow] - dlt_x[:, ucol]) * scale).astype(_MM))
                p2.append(px.astype(_MM))
                if sink is not None:
                    j = u * A_HEADS + h
                    sacc_ref[:, j:j + 1] += -jnp.exp(sk_ref[j] - l_b) * dlt_b[:, ucol]
            dq_ref[:, us] = _dot(jnp.concatenate(ds1, axis=1), kd)
            dk4 = _fold_diag(_dot_tn(jnp.concatenate(ds1c, axis=1), q4) + _dot_tn(jnp.concatenate(ds2, axis=1), qx4), BLK)
            dv4 = _fold_diag(_dot_tn(jnp.concatenate(p1c, axis=1), do4) + _dot_tn(jnp.concatenate(p2, axis=1), dox4), BLK)
            if shared:
                fold = lambda t: (t[:, 0:HEAD] + t[:, HEAD:2 * HEAD]) + (t[:, 2 * HEAD:3 * HEAD] + t[:, 3 * HEAD:])
                dk_ref[:, u * HEAD:(u + 1) * HEAD] = fold(dk4)
                dv_ref[:, u * HEAD:(u + 1) * HEAD] = fold(dv4).astype(_MM)
            else:
                dk_ref[:, us] = dk4
                dv_ref[:, us] = dv4.astype(_MM)

    cur = lambda w: pl.BlockSpec((BLK, w), lambda i: (i, 0))
    prev = lambda w: pl.BlockSpec((BLK, w), lambda i: (jnp.maximum(i - 1, 0), 0))
    nxt = lambda w: pl.BlockSpec((BLK, w), lambda i: (jnp.minimum(i + 1, nb - 1), 0))
    in_specs = [cur(wq), nxt(wq), cur(wk), prev(wk), cur(wk), prev(wk), cur(wq), nxt(wq), cur(wq), nxt(wq), cur(wq), nxt(wq)]
    args = [qn, qn, kn, kn, vn, vn, do, do, lse, lse, dl_or_o, dl_or_o]
    out_specs = [cur(wq), cur(wk), cur(wk)]
    out_shape = [_sds((rows, wq)), _sds((rows, wk)), _sds((rows, wk), _MM)]
    if sink is not None:
        in_specs.append(pl.BlockSpec(memory_space=pltpu.SMEM))
        args.append(sink)
        out_specs.append(_acc((BLK, 128)))
        out_shape.append(_sds((BLK, 128)))
    return _pc(body, name=name, grid=(nb,), in_specs=in_specs, out_specs=out_specs, out_shape=out_shape)(*args)


def _k_prep_bwd(srcs, dqn, dkn, gq, gk, tabs, *, wq, wk, rows_per_gain, name):
    rows = dqn.shape[0]
    ts = min(256, rows)
    ngain = gq.shape[0]

    def body(q_ref, k_ref, dq_ref, dk_ref, gq_ref, gk_ref, c_ref, sa_ref, sb_ref, oq_ref, ok_ref, aq_ref, ak_ref):
        i = pl.program_id(0)

        @pl.when(lax.rem(i * ts, rows_per_gain) == 0)
        def _():
            aq_ref[...] = jnp.zeros_like(aq_ref)
            ak_ref[...] = jnp.zeros_like(ak_ref)

        c, sa, sb = c_ref[...], sa_ref[...], sb_ref[...]
        for x_ref, d_ref, g_ref, o_ref, a_ref in ((q_ref, dq_ref, gq_ref, oq_ref, aq_ref),
                                                   (k_ref, dk_ref, gk_ref, ok_ref, ak_ref)):
            xh, r = _seg_norm(x_ref[...], HEAD)
            dt = _rope_bwd(d_ref[...], c, sa, sb)
            a_ref[...] += _sum8(dt * xh)
            z = dt * g_ref[...]
            o_ref[...] = (r * (z - xh * _seg_mean(z * xh, HEAD))).astype(_MM)

    gspec = lambda w: pl.BlockSpec((None, 1, w), lambda i: ((i * ts) // rows_per_gain, 0, 0))
    aspec = lambda w: pl.BlockSpec((None, 8, w), lambda i: ((i * ts) // rows_per_gain, 0, 0))
    return _pc(
        body, name=name, grid=(rows // ts,),
        in_specs=[_row(ts, wq, srcs[0][1]), _row(ts, wk, srcs[1][1]), _row(ts, wq), _row(ts, wk), gspec(wq), gspec(wk)]
        + [_row(ts, 128)] * 3,
        out_specs=[_row(ts, wq), _row(ts, wk), aspec(wq), aspec(wk)],
        out_shape=[_sds((rows, wq), _MM), _sds((rows, wk), _MM), _sds((ngain, 8, wq)), _sds((ngain, 8, wk))])(
            srcs[0][0], srcs[1][0], dqn, dkn, gq, gk, *tabs)


def _k_in_bwd(pieces, dgp, x, g1, dx1, w_in, w_gate):
    s = x.shape[0]
    ts = min(256, s)
    nin, ng = w_in.shape[2], w_gate.shape[2]
    widths = [p.shape[1] for p in pieces]
    ncol = sum(widths)

    def body(*refs):
        p_refs = refs[:len(pieces)]
        dgp_ref, x_ref, g_ref, dx1_ref, wi_ref, wg_ref, gx_ref, dpj_ref, gacc_ref = refs[len(pieces):]
        i = pl.program_id(0)

        @pl.when(i == 0)
        def _():
            gacc_ref[...] = jnp.zeros_like(gacc_ref)

        off = 0
        for p_ref, w in zip(p_refs, widths):
            dpj_ref[:, off:off + w] = p_ref[...]
            off += w
        dh = jnp.zeros((ts, D_MODEL), F32)
        for j in range(CHIPS):
            dh = dh + _dot_nt(dpj_ref[:, j * nin:(j + 1) * nin], wi_ref[j])
            dh = dh + _dot_nt(dgp_ref[:, j * ng:(j + 1) * ng], wg_ref[j])
        xh, r = _rms(x_ref[...])
        gacc_ref[...] += _sum8(dh * xh)
        gx_ref[...] = dx1_ref[...] + _rms_bwd(dh, xh, r, g_ref[...])

    return _pc(
        body, name="in_proj_bwd", grid=(s // ts,),
        in_specs=[_row(ts, w) for w in widths] + [_row(ts, CHIPS * ng), _row(ts, D_MODEL), _res((1, D_MODEL)),
                                                  _row(ts, D_MODEL), _res(w_in.shape), _res(w_gate.shape)],
        out_specs=[_row(ts, D_MODEL), _row(ts, ncol), _acc((8, D_MODEL))],
        out_shape=[_sds((s, D_MODEL)), _sds((s, ncol), _MM), _sds((8, D_MODEL))])(*pieces, dgp, x, g1, dx1, w_in, w_gate)


def _k_wgrad(a, b, *, nblk, stacked, name):
    s, k = a.shape
    n = b.shape[1]
    nb = n // nblk
    ts = min(512, s)

    def body(a_ref, b_ref, o_ref):
        @pl.when(pl.program_id(1) == 0)
        def _():
            o_ref[...] = jnp.zeros_like(o_ref)

        o_ref[...] += _dot_tn(a_ref[...], b_ref[...])

    if stacked:
        out_spec, out_shape = pl.BlockSpec((None, k, nb), lambda g, t: (g, 0, 0)), _sds((nblk, k, nb))
    else:
        out_spec, out_shape = pl.BlockSpec((k, nb), lambda g, t: (0, g)), _sds((k, n))
    return _pc(body, name=name, grid=(nblk, s // ts),
               in_specs=[pl.BlockSpec((ts, k), lambda g, t: (t, 0)), pl.BlockSpec((ts, nb), lambda g, t: (t, g))],
               out_specs=[out_spec], out_shape=[out_shape])(a, b)[0]


def _to_res(t, d):
    s, c = t.shape
    return t if d == 1 else t.reshape(s // d, d, c).transpose(1, 0, 2).reshape(s, c)


def _from_res(t, d):
    s, c = t.shape
    return t if d == 1 else t.reshape(d, s // d, c).transpose(1, 0, 2).reshape(s, c)


def _tile_gain(g, heads):
    return jnp.tile(g, (1,) * (g.ndim - 1) + (heads,))[..., None, :]


def _local_step(x, mem, pos, target, small, wts):
    s = x.shape[0]
    nblk = s // BLK
    g1, g2 = small["attn_norm"], small["ffn_norm"]

    pos_rows = jnp.concatenate([_to_res(pos[:, None], d)[:, 0] for _, d in A_GROUPS] + [pos])
    tabs = _rope_tables(pos_rows)
    tabs_a = tuple(t[:3 * s] for t in tabs)
    tabs_b = tuple(t[3 * s:] for t in tabs)

    h, qa0, qa1, qa2, q_b, k_b, v_b, m_q, gates = _k_in(x, g1, wts["w_in"], wts["w_gate"], small["b_gate"])

    qkv_a = jnp.concatenate([_to_res(t, d) for t, (_, d) in zip((qa0, qa1, qa2), A_GROUPS)], axis=0)
    gq_a = _tile_gain(small["a_q_norm"], A_HEADS)
    gk_a = _tile_gain(small["a_k_norm"], A_HEADS)
    src_a = ((qkv_a, 0), (qkv_a, 1), (qkv_a, 2))
    qn_a, kn_a, vn_a = _k_prep(src_a, gq_a, gk_a, tabs_a, wq=A_W, wk=A_W, rows_per_gain=s, name="prep_a")
    segs_a = tuple((gi * nblk, nblk // d) for gi, (_, d) in enumerate(A_GROUPS))
    o_res, l_res = _k_band_fwd(qn_a, kn_a, vn_a, hq=A_HEADS, hk=A_HEADS, max_dist=BLK, segs=segs_a, sink=None,
                               name="attn_a")
    og = [_from_res(o_res[gi * s:(gi + 1) * s], d) for gi, (_, d) in enumerate(A_GROUPS)]
    lg = [_from_res(l_res[gi * s:(gi + 1) * s], d) for gi, (_, d) in enumerate(A_GROUPS)]

    gq_b = _tile_gain(small["b_q_norm"], B_QH)
    gk_b = _tile_gain(small["b_k_norm"], B_KVH)
    src_b = ((q_b, 0), (k_b, 0), (v_b, 0))
    qn_b, kn_b, vn_b = _k_prep(src_b, gq_b, gk_b, tabs_b, wq=B_QH * HEAD, wk=B_KVH * HEAD, rows_per_gain=s,
                               name="prep_b")
    sink_x = small["b_sinks"][0]
    segs_b = ((0, nblk),)
    o_b, l_b = _k_band_fwd(qn_b, kn_b, vn_b, hq=B_QH, hk=B_KVH, max_dist=B_WINDOW - 1, segs=segs_b, sink=sink_x,
                           name="attn_b")

    gq_m = _tile_gain(small["m_q_norm"], M_HEADS)[0]
    gk_m = _tile_gain(small["m_k_norm"], M_HEADS)[0]
    mem_n, kv, mk, mv = _k_memkv(mem, small["mem_norm"], wts["w_mem_kv"], gk_m)
    o_m = _k_mem_fwd(m_q, gq_m, mk, mv)

    o_a, merged, x1, h2 = _k_merge(og, lg, o_b, o_m, gates, x, wts["w_o_a"], wts["w_o_b"], wts["w_o_m"],
                                   wts["w_out"], g2)
    u = _k_up(h2, wts["w_up"])
    dy, f, dc, loss_acc = _k_ffn(u, wts["conv_w"], small["conv_b"], wts["w_down"], x1, target)
    loss = (0.5 / D_MODEL) * jnp.sum(loss_acc)

    dx1, du, cacc, g2acc = _k_conv_bwd(dc, u, wts["conv_w"], wts["w_up"], x1, g2, dy)
    (dgp, dp_a, dp_b, dp_m, dog0, dog1, dog2, dl0, dl1, dl2, do_b, do_m, bacc) = _k_merge_bwd(
        dx1, og, lg, o_a, o_b, o_m, gates, wts["w_o_a"], wts["w_o_b"], wts["w_o_m"], wts["w_out"])

    dq_m, dmk, dmv, gqm_acc = _k_mem_bwd(m_q, gq_m, mk, mv, o_m, do_m)
    dw_kv, gmem_acc, gkm_acc = _k_memkv_bwd(mem, small["mem_norm"], wts["w_mem_kv"], gk_m, mem_n, kv, dmk, dmv)

    dq_bn, dk_bn, dv_b, sacc = _k_band_bwd(qn_b, kn_b, vn_b, do_b, l_b, o_b, hq=B_QH, hk=B_KVH,
                                           max_dist=B_WINDOW - 1, segs=segs_b, sink=sink_x, name="attn_b_bwd")
    dq_b, dk_b, gqb_acc, gkb_acc = _k_prep_bwd(src_b, dq_bn, dk_bn, gq_b, gk_b, tabs_b, wq=B_QH * HEAD,
                                               wk=B_KVH * HEAD, rows_per_gain=s, name="prep_b_bwd")

    do_res = jnp.concatenate([_to_res(t, d) for t, (_, d) in zip((dog0, dog1, dog2), A_GROUPS)], axis=0)
    dl_res = jnp.concatenate([_to_res(t, d) for t, (_, d) in zip((dl0, dl1, dl2), A_GROUPS)], axis=0)
    dq_an, dk_an, dv_a = _k_band_bwd(qn_a, kn_a, vn_a, do_res, l_res, dl_res, hq=A_HEADS, hk=A_HEADS, max_dist=BLK,
                                     segs=segs_a, sink=None, name="attn_a_bwd")
    dq_a, dk_a, gqa_acc, gka_acc = _k_prep_bwd(src_a, dq_an, dk_an, gq_a, gk_a, tabs_a, wq=A_W, wk=A_W,
                                               rows_per_gain=s, name="prep_a_bwd")
    pieces = []
    for gi, (_, d) in enumerate(A_GROUPS):
        rs = slice(gi * s, (gi + 1) * s)
        pieces += [_from_res(t[rs], d) for t in (dq_a, dk_a, dv_a)]
    pieces += [dq_b, dk_b, dv_b, dq_m]
    grad_x, dproj, g1acc = _k_in_bwd(pieces, dgp, x, g1, dx1, wts["w_in"], wts["w_gate"])

    big = {
        "w_in": _k_wgrad(h, dproj, nblk=CHIPS, stacked=True, name="dw_in"),
        "w_gate": _k_wgrad(h, dgp, nblk=CHIPS, stacked=True, name="dw_gate"),
        "w_o_a": _k_wgrad(o_a, dp_a, nblk=CHIPS, stacked=True, name="dw_o_a"),
        "w_o_b": _k_wgrad(o_b, dp_b, nblk=CHIPS, stacked=True, name="dw_o_b"),
        "w_o_m": _k_wgrad(o_m, dp_m, nblk=CHIPS, stacked=True, name="dw_o_m"),
        "w_up": _k_wgrad(h2, du, nblk=CHIPS, stacked=True, name="dw_up"),
        "w_out": _k_wgrad(merged, dx1, nblk=1, stacked=False, name="dw_out").reshape(CHIPS, -1, D_MODEL),
        "w_down": _k_wgrad(f, dy, nblk=2, stacked=False, name="dw_down").reshape(CHIPS, -1, D_MODEL),
        "w_mem_kv": dw_kv.reshape(CHIPS, -1, 2 * M_W),
    }

    def fold(acc, heads):
        v = jnp.sum(acc, axis=-2)
        return jnp.sum(v.reshape(v.shape[:-1] + (heads, -1)), axis=-2)

    csum = jnp.sum(cacc, axis=1)
    sml = {
        "attn_norm": jnp.sum(g1acc, axis=0), "a_q_norm": fold(gqa_acc, A_HEADS), "a_k_norm": fold(gka_acc, A_HEADS),
        "b_q_norm": fold(gqb_acc[0], B_QH), "b_k_norm": fold(gkb_acc[0], B_KVH),
        "b_sinks": jnp.sum(sacc, axis=0)[:B_QH], "mem_norm": jnp.sum(gmem_acc, axis=0),
        "m_q_norm": fold(gqm_acc, M_HEADS), "m_k_norm": fold(gkm_acc, M_HEADS),
        "b_gate": jnp.sum(bacc, axis=0), "ffn_norm": jnp.sum(g2acc, axis=0),
        "conv_w": csum[1:], "conv_b": csum[0],
    }
    return loss, grad_x, big, sml


def _mesh_pos():
    return lax.axis_index("x"), lax.axis_index("y"), lax.axis_index("c")


def _chip_peers(x, y):
    return [(1 - x, y), (x, 1 - y), (1 - x, 1 - y)]


_ANY = pl.BlockSpec(memory_space=pl.ANY)


def _comm_call(body, *, name, n_in, out_shape, scratch):
    return pl.pallas_call(body, name=name, in_specs=[_ANY] * n_in, out_specs=[_ANY] * len(out_shape),
                          out_shape=out_shape, scratch_shapes=scratch)


def _remote(src, dst, send_sem, recv_sem, dev):
    return pltpu.make_async_remote_copy(src_ref=src, dst_ref=dst, send_sem=send_sem, recv_sem=recv_sem,
                                        device_id=dev, device_id_type=MESH)


def _gather_shards(shards):
    nt = len(shards)
    split = [sh.shape[0] % 16 == 0 for sh in shards]

    def body(*refs):
        ins, outs = refs[:nt], refs[nt:2 * nt]
        ici_s, ici_r, fwd_s, fwd_r, own_s, own_r = refs[2 * nt:]
        x, y, c = _mesh_pos()
        me = 2 * x + y
        sib = (x, y, 1 - c)
        peers = _chip_peers(x, y)

        def half(ref, t, who):
            if not split[t]:
                return ref
            hr = shards[t].shape[0] // 2
            return ref.at[pl.ds(pl.multiple_of(who * hr, 8), hr), :]

        pending = []
        for t in range(nt):
            own = _remote(ins[t], outs[t].at[me], own_s.at[t], own_r.at[t], sib)
            own.start()
            pending.append(own.wait)
            for k, (px, py) in enumerate(peers):
                rc = _remote(half(ins[t], t, c), half(outs[t].at[me], t, c), ici_s.at[t, k], ici_r.at[t, k], (px, py, c))
                rc.start()
                pending.append(rc.wait_send)
        for t in range(nt):
            for k, (px, py) in enumerate(peers):
                land = half(outs[t].at[2 * px + py], t, c)
                _remote(land, land, ici_s.at[t, k], ici_r.at[t, k], (px, py, c)).wait_recv()
                if split[t]:
                    fw = _remote(land, land, fwd_s.at[t, k], fwd_r.at[t, k], sib)
                    fw.start()
                    pending.append(fw.wait_send)
                    other = half(outs[t].at[2 * px + py], t, 1 - c)
                    pending.append(_remote(other, other, fwd_s.at[t, k], fwd_r.at[t, k], sib).wait_recv)
        for wait in pending:
            wait()

    out_shape = [_sds((CHIPS,) + sh.shape, sh.dtype) for sh in shards]
    dma = pltpu.SemaphoreType.DMA
    scratch = [dma((nt, 3)), dma((nt, 3)), dma((nt, 3)), dma((nt, 3)), dma((nt,)), dma((nt,))]
    return _comm_call(body, name="gather_weights", n_in=nt, out_shape=out_shape, scratch=scratch)(*shards)


def _pair_split(grads):
    nt = len(grads)

    def body(*refs):
        ins, got = refs[:nt], refs[nt:2 * nt]
        send_sems, recv_sems = refs[2 * nt:]
        x, y, c = _mesh_pos()
        cps = []
        for t in range(nt):
            hr = ins[t].shape[1] // 2
            give = ins[t].at[:, pl.ds(pl.multiple_of((1 - c) * hr, 8), hr), :]
            rc = _remote(give, got[t], send_sems.at[t], recv_sems.at[t], (x, y, 1 - c))
            rc.start()
            cps.append(rc)
        for rc in cps:
            rc.wait()

    half = [_sds((CHIPS, g.shape[1] // 2, g.shape[2]), g.dtype) for g in grads]
    scratch = [pltpu.SemaphoreType.DMA((nt,)), pltpu.SemaphoreType.DMA((nt,))]
    return _comm_call(body, name="grad_pair_split", n_in=nt, out_shape=half, scratch=scratch)(*grads)


def _chip_scatter(parts):
    nt = len(parts)

    def body(*refs):
        ins, outs = refs[:nt], refs[nt:2 * nt]
        send_sems, recv_sems = refs[2 * nt:]
        x, y, c = _mesh_pos()
        cps = []
        for t in range(nt):
            for k, (px, py) in enumerate(_chip_peers(x, y)):
                rc = _remote(ins[t].at[2 * px + py], outs[t].at[k], send_sems.at[t, k], recv_sems.at[t, k], (px, py, c))
                rc.start()
                cps.append(rc)
        for cp in cps:
            cp.wait()

    out_shape = [_sds((3,) + p.shape[1:], p.dtype) for p in parts]
    scratch = [pltpu.SemaphoreType.DMA((nt, 3)), pltpu.SemaphoreType.DMA((nt, 3))]
    return _comm_call(body, name="grad_chip_scatter", n_in=nt, out_shape=out_shape, scratch=scratch)(*parts)


def _pair_join(halves):
    nt = len(halves)

    def body(*refs):
        ins, got = refs[:nt], refs[nt:2 * nt]
        send_sems, recv_sems = refs[2 * nt:]
        x, y, c = _mesh_pos()
        cps = []
        for t in range(nt):
            rc = _remote(ins[t], got[t], send_sems.at[t], recv_sems.at[t], (x, y, 1 - c))
            rc.start()
            cps.append(rc)
        for rc in cps:
            rc.wait()

    out_shape = [_sds(hf.shape, hf.dtype) for hf in halves]
    scratch = [pltpu.SemaphoreType.DMA((nt,)), pltpu.SemaphoreType.DMA((nt,))]
    return _comm_call(body, name="grad_pair_join", n_in=nt, out_shape=out_shape, scratch=scratch)(*halves)


def _gather_small(packed):
    n = packed.shape[0]

    def body(in_ref, out_ref, send_sems, recv_sems, loc_sem):
        x, y, c = _mesh_pos()
        me = 4 * x + 2 * y + c
        lc = pltpu.make_async_copy(in_ref, out_ref.at[me], loc_sem)
        lc.start()
        peers = []
        for k in range(1, NDEV):
            px, py, pc = x ^ (k >> 2), y ^ ((k >> 1) & 1), c ^ (k & 1)
            rc = pltpu.make_async_remote_copy(src_ref=in_ref, dst_ref=out_ref.at[me], send_sem=send_sems.at[k - 1],
                                              recv_sem=recv_sems.at[k - 1], device_id=(px, py, pc), device_id_type=MESH)
            rc.start()
            peers.append((k, px, py, pc))
        lc.wait()
        for k, px, py, pc in peers:
            pltpu.make_async_remote_copy(src_ref=in_ref, dst_ref=out_ref.at[4 * px + 2 * py + pc],
                                         send_sem=send_sems.at[k - 1], recv_sem=recv_sems.at[k - 1],
                                         device_id=(px, py, pc), device_id_type=MESH).wait()

    scratch = [pltpu.SemaphoreType.DMA((NDEV - 1,)), pltpu.SemaphoreType.DMA((NDEV - 1,)), pltpu.SemaphoreType.DMA]
    return _comm_call(body, name="gather_small_grads", n_in=1, out_shape=[_sds((NDEV, n, 128))],
                      scratch=scratch)(packed)[0]


def _row_tile(r, c):
    t = r
    while t * c * 4 > (1 << 20) and t % 16 == 0:
        t //= 2
    return t


def _k_pair_add(full, got, name):
    g, r, c = full.shape
    hr = r // 2
    tr = _row_tile(hr, c)
    nh = hr // tr

    def body(a_ref, b_ref, o_ref):
        o_ref[...] = (a_ref[...] + b_ref[...]).astype(_WIRE)

    mine = pl.BlockSpec((None, tr, c), lambda i, j: (i, lax.axis_index("c") * nh + j, 0))
    spec = pl.BlockSpec((None, tr, c), lambda i, j: (i, j, 0))
    return _pc(body, name=name, grid=(g, nh), in_specs=[mine, spec], out_specs=[spec],
               out_shape=[_sds((g, hr, c), _WIRE)])(full, got)[0]


def _k_chip_sum(parts, slots, name):
    _, r, c = parts.shape
    tr = _row_tile(r, c)

    def body(a_ref, s_ref, o_ref):
        acc = a_ref[...].astype(F32)
        for k in range(3):
            acc = acc + s_ref[k].astype(F32)
        o_ref[...] = acc

    own = pl.BlockSpec((None, tr, c), lambda i: (2 * lax.axis_index("x") + lax.axis_index("y"), i, 0))
    return _pc(body, name=name, grid=(r // tr,), in_specs=[own, pl.BlockSpec((3, tr, c), lambda i: (0, i, 0))],
               out_specs=[_row(tr, c)], out_shape=[_sds((r, c))])(parts, slots)[0]


def _adam(w, g, m, v):
    m = ADAM_B1 * m + (1.0 - ADAM_B1) * g
    v = ADAM_B2 * v + (1.0 - ADAM_B2) * (g * g)
    m_hat = m / (1.0 - ADAM_B1 ** ADAM_STEP)
    v_hat = v / (1.0 - ADAM_B2 ** ADAM_STEP)
    return -ADAM_LR * (m_hat / (jnp.sqrt(v_hat) + ADAM_EPS) + ADAM_WD * w), m, v


def _k_adam(w, mine, theirs, m, v, name):
    r, c = w.shape
    hr = r // 2
    tr = _row_tile(hr, c)
    nh = hr // tr

    def body(w_ref, a_ref, b_ref, m_ref, v_ref, g_ref, d_ref, mo_ref, vo_ref):
        upper = (pl.program_id(0) >= nh).astype(jnp.int32)
        g = jnp.where(upper == lax.axis_index("c"), a_ref[...], b_ref[...])
        g_ref[...] = g
        d_ref[...], mo_ref[...], vo_ref[...] = _adam(w_ref[...], g, m_ref[...], v_ref[...])

    hspec = pl.BlockSpec((tr, c), lambda i: (jnp.where(i >= nh, i - nh, i), 0))
    return _pc(body, name=name, grid=(r // tr,), in_specs=[_row(tr, c), hspec, hspec, _row(tr, c), _row(tr, c)],
               out_specs=[_row(tr, c)] * 4, out_shape=[_sds((r, c))] * 4)(w, mine, theirs, m, v)


def _k_sum8(a):
    _, n, _ = a.shape

    def body(a_ref, o_ref):
        acc = a_ref[0]
        for k in range(1, NDEV):
            acc = acc + a_ref[k]
        o_ref[...] = acc

    return _pc(body, name="sum_small_grads", grid=(1,), in_specs=[_acc(a.shape)], out_specs=[_acc((n, 128))],
               out_shape=[_sds((n, 128))])(a)[0]


def _k_adam_small(w, g, m, v):
    n = w.shape[0]

    def body(w_ref, g_ref, m_ref, v_ref, d_ref, mo_ref, vo_ref):
        d_ref[...], mo_ref[...], vo_ref[...] = _adam(w_ref[...], g_ref[...], m_ref[...], v_ref[...])

    return _pc(body, name="adam_small", grid=(1,), in_specs=[_acc((n, 128))] * 4, out_specs=[_acc((n, 128))] * 3,
               out_shape=[_sds((n, 128))] * 3)(w, g, m, v)


def _pack(vals):
    rows = []
    for a in vals:
        flat = a.reshape(-1)
        n = -(-flat.shape[0] // 1024) * 1024
        rows.append(jnp.pad(flat, (0, n - flat.shape[0])).reshape(n // 128, 128))
    return jnp.concatenate(rows, axis=0)


def _unpack(packed, shapes):
    out, off = [], 0
    for sh in shapes:
        size = int(np.prod(sh))
        n = -(-size // 1024) * 1024
        out.append(packed[off // 128:(off + n) // 128].reshape(-1)[:size].reshape(sh))
        off += n
    return out


_WEIGHTS = ["attn_norm", "w_in", "a_q_norm", "a_k_norm", "b_q_norm", "b_k_norm", "b_sinks", "mem_norm", "w_mem_kv",
            "m_q_norm", "m_k_norm", "w_o_a", "w_o_b", "w_o_m", "w_gate", "b_gate", "w_out", "ffn_norm", "w_up",
            "conv_w", "conv_b", "w_down"]
_BIG = ["w_in", "w_mem_kv", "w_o_a", "w_o_b", "w_o_m", "w_gate", "w_out", "w_up", "w_down"]
_SMALL = [n for n in _WEIGHTS if n not in _BIG]


def kernel(x, mem, positions, attn_norm, w_in, a_q_norm, a_k_norm, b_q_norm, b_k_norm, b_sinks, mem_norm, w_mem_kv, m_q_norm, m_k_norm, w_o_a, w_o_b, w_o_m, w_gate, b_gate, w_out, ffn_norm, w_up, conv_w, conv_b, w_down, loss_target, m_attn_norm, m_w_in, m_a_q_norm, m_a_k_norm, m_b_q_norm, m_b_k_norm, m_b_sinks, m_mem_norm, m_w_mem_kv, m_m_q_norm, m_m_k_norm, m_w_o_a, m_w_o_b, m_w_o_m, m_w_gate, m_b_gate, m_w_out, m_ffn_norm, m_w_up, m_conv_w, m_conv_b, m_w_down, v_attn_norm, v_w_in, v_a_q_norm, v_a_k_norm, v_b_q_norm, v_b_k_norm, v_b_sinks, v_mem_norm, v_w_mem_kv, v_m_q_norm, v_m_k_norm, v_w_o_a, v_w_o_b, v_w_o_m, v_w_gate, v_b_gate, v_w_out, v_ffn_norm, v_w_up, v_conv_w, v_conv_b, v_w_down):
    given = dict(locals())
    w = {n: given[n][0] for n in _WEIGHTS}
    m1 = {n: given["m_" + n][0] for n in _WEIGHTS}
    m2 = {n: given["v_" + n][0] for n in _WEIGHTS}

    gathered = _gather_shards([w[n].astype(_MM) for n in _BIG] + [w["conv_w"]])
    wts = dict(zip(_BIG + ["conv_w"], gathered))
    for n in ("w_mem_kv", "w_out", "w_down"):
        wts[n] = wts[n].reshape(-1, wts[n].shape[-1])
    small = {n: (w[n][None, :] if w[n].ndim == 1 else w[n]) for n in _SMALL if n != "conv_w"}

    loss, grad_x, big, sml = _local_step(x[0], mem[0], positions[0], loss_target[0], small, wts)
    loss = lax.psum(loss, ("x", "y", "c"))

    got = _pair_split([big[n] for n in _BIG])
    parts = [_k_pair_add(big[n], b, "pair_add_" + n) for n, b in zip(_BIG, got)]
    slots = _chip_scatter(parts)
    mine = [_k_chip_sum(a, b, "chip_add_" + n) for n, a, b in zip(_BIG, parts, slots)]
    theirs = _pair_join(mine)
    grads = {}

    shapes = [sml[n].shape for n in _SMALL]
    gsm = dict(zip(_SMALL, _unpack(_k_sum8(_gather_small(_pack([sml[n] for n in _SMALL]))), shapes)))
    nu = w["conv_w"].shape[1]
    chip = 2 * lax.axis_index("x") + lax.axis_index("y")
    gsm["conv_w"] = lax.dynamic_slice_in_dim(gsm["conv_w"], chip * nu, nu, axis=1)
    for n in _SMALL:
        grads[n] = gsm[n].reshape(w[n].shape)

    delta, new_m, new_v = {}, {}, {}
    for n, a, b in zip(_BIG, mine, theirs):
        grads[n], delta[n], new_m[n], new_v[n] = _k_adam(w[n], a, b, m1[n], m2[n], "adam_" + n)
    pk = lambda d: _pack([d[n] for n in _SMALL])
    sshapes = [w[n].shape for n in _SMALL]
    for dst, packed in zip((delta, new_m, new_v), _k_adam_small(pk(w), pk(grads), pk(m1), pk(m2))):
        dst.update(zip(_SMALL, _unpack(packed, sshapes)))

    lead = lambda d: [d[n][None] for n in _WEIGHTS]
    return (loss, grad_x[None], *lead(grads), *lead(delta), *lead(new_m), *lead(new_v))
```

```python
import math

import jax
import jax.numpy as jnp
import numpy as np
from jax import lax
from jax.experimental import pallas as pl
from jax.experimental.pallas import tpu as pltpu

F32 = jnp.float32
_MM = jnp.bfloat16
_WIRE = jnp.bfloat16

D_MODEL = 1024
HEAD = 64
BLK = 128
A_GROUPS = ((128, 1), (512, 4), (2048, 16))
A_HEADS = 4
A_W = A_HEADS * HEAD
B_QH = 8
B_KVH = 2
B_WINDOW = 128
M_HEADS = 4
M_HD = 128
M_W = M_HEADS * M_HD
D_FF = 2816
EPS = 1e-6
NEG = -1e30
ROPE_THETA = 500000.0
ROPE_ROT = 16
CHIPS = 4
NDEV = 8
ADAM_LR, ADAM_B1, ADAM_B2, ADAM_EPS, ADAM_WD, ADAM_STEP = 0.001, 0.9, 0.999, 1e-08, 0.01, 10
VMEM_LIMIT = 58 * 1024 * 1024
MESH = pl.DeviceIdType.MESH


def _pc(body, *, name, grid, in_specs, out_specs, out_shape, scratch=()):
    return pl.pallas_call(
        body, name=name, grid=grid, in_specs=in_specs, out_specs=out_specs, out_shape=out_shape,
        scratch_shapes=list(scratch),
        compiler_params=pltpu.CompilerParams(dimension_semantics=("arbitrary",) * len(grid),
                                             vmem_limit_bytes=VMEM_LIMIT))


def _row(ts, c, col=0):
    return pl.BlockSpec((ts, c), lambda i: (i, col))


def _res(shape):
    n = len(shape)
    return pl.BlockSpec(tuple(shape), lambda i: (0,) * n, pipeline_mode=pl.Buffered(1))


def _acc(shape):
    n = len(shape)
    return pl.BlockSpec(tuple(shape), lambda i: (0,) * n)


def _sds(shape, dtype=F32):
    return jax.ShapeDtypeStruct(tuple(shape), dtype)


def _dot(a, b):
    return jnp.dot(a.astype(_MM), b.astype(_MM), preferred_element_type=F32)


def _dot_nt(a, b):
    return lax.dot_general(a.astype(_MM), b.astype(_MM), (((1,), (1,)), ((), ())), preferred_element_type=F32)


def _dot_tn(a, b):
    return lax.dot_general(a.astype(_MM), b.astype(_MM), (((0,), (0,)), ((), ())), preferred_element_type=F32)


def _sum8(v):
    ts, c = v.shape
    return jnp.sum(v.reshape(ts // 8, 8, c), axis=0)


def _sigmoid(z):
    return 1.0 / (1.0 + jnp.exp(-z))


def _rms(x):
    r = lax.rsqrt(jnp.mean(x * x, axis=-1, keepdims=True) + EPS)
    return x * r, r


def _rms_bwd(dy, xh, r, gain):
    z = dy * gain
    return r * (z - xh * jnp.mean(z * xh, axis=-1, keepdims=True))


def _split_hi_lo(v):
    hi = v.astype(_MM)
    return hi, (v - hi.astype(F32)).astype(_MM)


def _lane_head(shape):
    return lax.shift_right_logical(lax.broadcasted_iota(jnp.int32, shape, len(shape) - 1), 6)


def _seg_sum64(v):
    w = v.shape[1]
    e = jnp.where(_lane_head((w, w)) == lax.shift_right_logical(lax.broadcasted_iota(jnp.int32, (w, w), 0), 6),
                  1.0, 0.0).astype(_MM)
    hi, lo = _split_hi_lo(v)
    return jnp.dot(hi, e, preferred_element_type=F32) + jnp.dot(lo, e, preferred_element_type=F32)


def _seg_norm(x, seg):
    if seg == HEAD:
        r = lax.rsqrt(_seg_sum64(x * x) * (1.0 / HEAD) + EPS)
        return x * r, r
    w = x.shape[1]
    xh, rr = [], []
    for s in range(w // seg):
        xs = x[:, s * seg:(s + 1) * seg]
        r = lax.rsqrt(jnp.mean(xs * xs, axis=-1, keepdims=True) + EPS)
        xh.append(xs * r)
        rr.append(jnp.broadcast_to(r, xs.shape))
    return jnp.concatenate(xh, axis=1), jnp.concatenate(rr, axis=1)


def _seg_mean(v, seg):
    if seg == HEAD:
        return _seg_sum64(v) * (1.0 / HEAD)
    w = v.shape[1]
    out = []
    for s in range(w // seg):
        vs = v[:, s * seg:(s + 1) * seg]
        out.append(jnp.broadcast_to(jnp.mean(vs, axis=-1, keepdims=True), vs.shape))
    return jnp.concatenate(out, axis=1)


def _rope(t, c, sa, sb):
    out = []
    for cb in range(t.shape[1] // 128):
        tc = t[:, cb * 128:(cb + 1) * 128]
        out.append(tc * c + pltpu.roll(tc, 120, 1) * sa + pltpu.roll(tc, 8, 1) * sb)
    return jnp.concatenate(out, axis=1) if len(out) > 1 else out[0]


def _rope_bwd(dy, c, sa, sb):
    out = []
    for cb in range(dy.shape[1] // 128):
        dc = dy[:, cb * 128:(cb + 1) * 128]
        out.append(dc * c + pltpu.roll(dc * sa, 8, 1) + pltpu.roll(dc * sb, 120, 1))
    return jnp.concatenate(out, axis=1) if len(out) > 1 else out[0]


def _rope_freqs():
    c = np.float32(-2.0 * math.log(ROPE_THETA) / ROPE_ROT)
    return [float(v) for v in np.exp(np.arange(ROPE_ROT // 2, dtype=np.float32) * c)]


def _k_rope(pos2d):
    n = pos2d.shape[0]
    freqs = _rope_freqs()
    nf = len(freqs)

    def body(p_ref, c_ref, s_ref):
        p = p_ref[...].astype(F32)
        for f in range(nf):
            ang = p * freqs[f]
            c_ref[f] = jnp.cos(ang)
            s_ref[f] = jnp.sin(ang)

    return _pc(body, name="rope_tables", grid=(1,),
               in_specs=[_acc((n, 128))], out_specs=[_acc((nf, n, 128)), _acc((nf, n, 128))],
               out_shape=[_sds((nf, n, 128)), _sds((nf, n, 128))])(pos2d)


def _rope_tables(pos_rows):
    r = pos_rows.shape[0]
    cos, sin = _k_rope(pos_rows.reshape(r // 128, 128))
    half = ROPE_ROT // 2
    cos = cos.reshape(half, r).T
    sin = sin.reshape(half, r).T
    one = jnp.ones((r, HEAD - ROPE_ROT), F32)
    zero = jnp.zeros((r, HEAD - ROPE_ROT), F32)
    z8 = jnp.zeros((r, half), F32)
    c64 = jnp.concatenate([cos, cos, one], axis=1)
    sa64 = jnp.concatenate([-sin, z8, zero], axis=1)
    sb64 = jnp.concatenate([z8, sin, zero], axis=1)
    return tuple(jnp.concatenate([t, t], axis=1) for t in (c64, sa64, sb64))


def _k_in(x, g1, w_in):
    s = x.shape[0]
    ts = min(256, s)
    nin = w_in.shape[2]
    ncol = CHIPS * nin
    a_cols = 3 * A_W
    offs = [0, a_cols, 2 * a_cols, 3 * a_cols, 3 * a_cols + B_QH * HEAD,
            3 * a_cols + (B_QH + B_KVH) * HEAD, 3 * a_cols + (B_QH + 2 * B_KVH) * HEAD, ncol]

    def body(x_ref, g_ref, wi_ref, h_ref, a0, a1, a2, qb, kb, vb, mq, p_scr):
        xh, _ = _rms(x_ref[...])
        h = (xh * g_ref[...]).astype(_MM)
        h_ref[...] = h
        for j in range(CHIPS):
            p_scr[:, j * nin:(j + 1) * nin] = jnp.dot(h, wi_ref[j], preferred_element_type=F32)
        for k, ref in enumerate((a0, a1, a2, qb, kb, vb, mq)):
            ref[...] = p_scr[:, offs[k]:offs[k + 1]]

    widths = [offs[k + 1] - offs[k] for k in range(7)]
    return _pc(
        body, name="in_proj", grid=(s // ts,),
        in_specs=[_row(ts, D_MODEL), _res((1, D_MODEL)), _res(w_in.shape)],
        out_specs=[_row(ts, D_MODEL)] + [_row(ts, w) for w in widths],
        out_shape=[_sds((s, D_MODEL), _MM)] + [_sds((s, w)) for w in widths],
        scratch=[pltpu.VMEM((ts, ncol), F32)])(x, g1, w_in)


def _k_gate(h, w_gate, b_gate):
    s = h.shape[0]
    ts = min(256, s)
    ng = w_gate.shape[2]

    def body(h_ref, wg_ref, bg_ref, gt_ref):
        h = h_ref[...]
        for j in range(CHIPS):
            z = jnp.dot(h, wg_ref[j], preferred_element_type=F32) + bg_ref[:, j * ng:(j + 1) * ng]
            gt_ref[:, j * ng:(j + 1) * ng] = _sigmoid(z)

    return _pc(body, name="gate_proj", grid=(s // ts,),
               in_specs=[_row(ts, D_MODEL), _res(w_gate.shape), _res(b_gate.shape)],
               out_specs=[_row(ts, CHIPS * ng)], out_shape=[_sds((s, CHIPS * ng))])(h, w_gate, b_gate)[0]


def _k_prep(srcs, gq, gk, tabs, *, wq, wk, rows_per_gain, name):
    rows = srcs[0][0].shape[0]
    ts = min(256, rows)

    def body(q_ref, k_ref, v_ref, gq_ref, gk_ref, c_ref, sa_ref, sb_ref, qn_ref, kn_ref, vn_ref):
        c, sa, sb = c_ref[...], sa_ref[...], sb_ref[...]
        qh, _ = _seg_norm(q_ref[...], HEAD)
        qn_ref[...] = _rope(qh * gq_ref[...], c, sa, sb).astype(_MM)
        kh, _ = _seg_norm(k_ref[...], HEAD)
        kn_ref[...] = _rope(kh * gk_ref[...], c, sa, sb).astype(_MM)
        vn_ref[...] = v_ref[...].astype(_MM)

    gspec = lambda w: pl.BlockSpec((None, 1, w), lambda i: ((i * ts) // rows_per_gain, 0, 0))
    return _pc(
        body, name=name, grid=(rows // ts,),
        in_specs=[_row(ts, wq, srcs[0][1]), _row(ts, wk, srcs[1][1]), _row(ts, wk, srcs[2][1]),
                  gspec(wq), gspec(wk)] + [_row(ts, 128)] * 3,
        out_specs=[_row(ts, wq), _row(ts, wk), _row(ts, wk)],
        out_shape=[_sds((rows, wq), _MM), _sds((rows, wk), _MM), _sds((rows, wk), _MM)])(
            srcs[0][0], srcs[1][0], srcs[2][0], gq, gk, *tabs)


def _first_flag(b, segs, nb):
    first = b >= nb
    for k, (start, period) in enumerate(segs):
        end = segs[k + 1][0] if k + 1 < len(segs) else nb
        first = first | ((b >= start) & (b < end) & (lax.rem(b - start, jnp.int32(period)) == 0))
    return first


def _band_bias(thr, with_cur):
    qi = lax.broadcasted_iota(jnp.int32, (BLK, BLK), 0)
    kj = lax.broadcasted_iota(jnp.int32, (BLK, BLK), 1)
    prev = jnp.where(kj >= qi + thr, 0.0, NEG)
    return jnp.concatenate([prev, jnp.where(kj <= qi, 0.0, NEG)], axis=1) if with_cur else prev


def _blockdiag(t4):
    head = _lane_head((1, A_W))
    return jnp.concatenate([t4 * jnp.where(head == h, 1.0, 0.0).astype(t4.dtype) for h in range(A_HEADS)], axis=0)


def _fold_diag(t, n):
    head = _lane_head((n, A_W))
    out = t[3 * n:4 * n]
    for h in (2, 1, 0):
        out = jnp.where(head == h, t[h * n:(h + 1) * n], out)
    return out


def _expand_heads(cols):
    n = cols[0].shape[0]
    head = _lane_head((n, A_W))
    out = jnp.broadcast_to(cols[3], (n, A_W))
    for h in (2, 1, 0):
        out = jnp.where(head == h, cols[h], out)
    return out


def _unit_kv(p_ref, c_ref, u, shared):
    if not shared:
        return jnp.concatenate([p_ref[:, u * A_W:(u + 1) * A_W], c_ref[:, u * A_W:(u + 1) * A_W]], axis=0)
    kg = jnp.concatenate([p_ref[:, u * HEAD:(u + 1) * HEAD], c_ref[:, u * HEAD:(u + 1) * HEAD]], axis=0)
    return jnp.concatenate([kg] * A_HEADS, axis=1)


def _k_band_fwd(qn, kn, vn, *, hq, hk, max_dist, segs, sink, name):
    rows = qn.shape[0]
    nb = rows // BLK
    units = hq // A_HEADS
    shared = hk != hq
    wq, wk = hq * HEAD, hk * HEAD
    scale = HEAD ** -0.5

    def body(*refs):
        if sink is None:
            q_ref, kc_ref, kp_ref, vc_ref, vp_ref, o_ref, l_ref = refs
        else:
            q_ref, kc_ref, kp_ref, vc_ref, vp_ref, sk_ref, o_ref, l_ref = refs
        b = pl.program_id(0)
        bias = _band_bias(jnp.where(_first_flag(b, segs, nb), 1 << 20, BLK - max_dist), True)
        for u in range(units):
            us = slice(u * A_W, (u + 1) * A_W)
            kb = _blockdiag(_unit_kv(kp_ref, kc_ref, u, shared))
            vb = _blockdiag(_unit_kv(vp_ref, vc_ref, u, shared))
            s_all = _dot_nt(q_ref[:, us], kb) * scale
            ps, ls = [], []
            for h in range(A_HEADS):
                s = s_all[:, h * 2 * BLK:(h + 1) * 2 * BLK] + bias
                m = jnp.max(s, axis=-1, keepdims=True)
                e = jnp.exp(s - m)
                lse = m + jnp.log(jnp.sum(e, axis=-1, keepdims=True))
                if sink is not None:
                    sk = sk_ref[u * A_HEADS + h]
                    mx = jnp.maximum(lse, sk)
                    lse = mx + jnp.log(jnp.exp(lse - mx) + jnp.exp(sk - mx))
                ps.append((e * jnp.exp(m - lse)).astype(_MM))
                ls.append(lse)
            o_ref[:, us] = _dot(jnp.concatenate(ps, axis=1), vb)
            l_ref[:, us] = _expand_heads(ls)

    cur = lambda w: pl.BlockSpec((BLK, w), lambda i: (i, 0))
    prev = lambda w: pl.BlockSpec((BLK, w), lambda i: (jnp.maximum(i - 1, 0), 0))
    in_specs = [cur(wq), cur(wk), prev(wk), cur(wk), prev(wk)]
    args = [qn, kn, kn, vn, vn]
    if sink is not None:
        in_specs.append(pl.BlockSpec(memory_space=pltpu.SMEM))
        args.append(sink)
    return _pc(body, name=name, grid=(nb,), in_specs=in_specs, out_specs=[cur(wq), cur(wq)],
               out_shape=[_sds((rows, wq)), _sds((rows, wq))])(*args)


def _k_memkv(mem, mem_norm, w_kv, m_k_norm):
    n = mem.shape[0]

    def body(m_ref, g_ref, w_ref, gk_ref, mn_ref, kv_ref, mk_ref, mv_ref):
        mh, _ = _rms(m_ref[...])
        mn = (mh * g_ref[...]).astype(_MM)
        mn_ref[...] = mn
        kv = jnp.dot(mn, w_ref[...], preferred_element_type=F32)
        kv_ref[...] = kv
        kh, _ = _seg_norm(kv[:, :M_W], M_HD)
        mk_ref[...] = (kh * gk_ref[...]).astype(_MM)
        mv_ref[...] = kv[:, M_W:].astype(_MM)

    return _pc(body, name="mem_kv", grid=(1,),
               in_specs=[_acc((n, D_MODEL)), _acc((1, D_MODEL)), _acc(w_kv.shape), _acc((1, M_W))],
               out_specs=[_acc((n, D_MODEL)), _acc((n, 2 * M_W)), _acc((n, M_W)), _acc((n, M_W))],
               out_shape=[_sds((n, D_MODEL), _MM), _sds((n, 2 * M_W)), _sds((n, M_W), _MM), _sds((n, M_W), _MM)])(
                   mem, mem_norm, w_kv, m_k_norm)


def _mem_probs(q, mk):
    sc = _dot_nt(q, mk) * (M_HD ** -0.5)
    e = jnp.exp(sc - jnp.max(sc, axis=-1, keepdims=True))
    return e / jnp.sum(e, axis=-1, keepdims=True)


def _k_mem_fwd(m_q, gq, mk, mv):
    s = m_q.shape[0]
    n = mk.shape[0]
    ts = min(256, s)

    def body(q_ref, g_ref, mk_ref, mv_ref, o_ref):
        qh, _ = _seg_norm(q_ref[...], M_HD)
        qn = (qh * g_ref[...]).astype(_MM)
        for h in range(M_HEADS):
            hs = slice(h * M_HD, (h + 1) * M_HD)
            o_ref[:, hs] = _dot(_mem_probs(qn[:, hs], mk_ref[:, hs]), mv_ref[:, hs])

    return _pc(body, name="mem_attn", grid=(s // ts,),
               in_specs=[_row(ts, M_W), _res((1, M_W)), _res((n, M_W)), _res((n, M_W))],
               out_specs=[_row(ts, M_W)], out_shape=[_sds((s, M_W))])(m_q, gq, mk, mv)[0]


def _group_weights(l0, l1, l2):
    m = jnp.maximum(jnp.maximum(l0, l1), l2)
    e0, e1, e2 = jnp.exp(l0 - m), jnp.exp(l1 - m), jnp.exp(l2 - m)
    inv = 1.0 / (e0 + e1 + e2)
    return e0 * inv, e1 * inv, e2 * inv


def _branch_products(oa, ob, om, woa_ref, wob_ref, wom_ref, j):
    return _dot(oa, woa_ref[j]), _dot(ob, wob_ref[j]), _dot(om, wom_ref[j])


def _k_merge(og, lg, o_b, o_m, gates, x, w_oa, w_ob, w_om, w_out, g2):
    s = x.shape[0]
    ts = min(256, s)
    nc = w_oa.shape[2]

    def body(o0, o1, o2, l0, l1, l2, ob_ref, om_ref, gt_ref, x_ref, woa, wob, wom, wout, g_ref,
             oa_ref, mer_ref, x1_ref, h2_ref, m_scr):
        w0, w1, w2 = _group_weights(l0[...], l1[...], l2[...])
        oa = w0 * o0[...] + w1 * o1[...] + w2 * o2[...]
        oa_ref[...] = oa
        ob, om = ob_ref[...], om_ref[...]
        for j in range(CHIPS):
            pa, pb, pm = _branch_products(oa, ob, om, woa, wob, wom, j)
            cs = lambda br: slice(br * D_MODEL + j * nc, br * D_MODEL + (j + 1) * nc)
            m_scr[:, j * nc:(j + 1) * nc] = gt_ref[:, cs(0)] * pa + gt_ref[:, cs(1)] * pb + gt_ref[:, cs(2)] * pm
        mer = m_scr[...].astype(_MM)
        mer_ref[...] = mer
        x1 = x_ref[...] + jnp.dot(mer, wout[...], preferred_element_type=F32)
        x1_ref[...] = x1
        xh, _ = _rms(x1)
        h2_ref[...] = (xh * g_ref[...]).astype(_MM)

    return _pc(
        body, name="merge_out", grid=(s // ts,),
        in_specs=[_row(ts, A_W)] * 6 + [_row(ts, B_QH * HEAD), _row(ts, M_W), _row(ts, 3 * D_MODEL), _row(ts, D_MODEL),
                                         _res(w_oa.shape), _res(w_ob.shape), _res(w_om.shape), _res(w_out.shape),
                                         _res((1, D_MODEL))],
        out_specs=[_row(ts, A_W), _row(ts, D_MODEL), _row(ts, D_MODEL), _row(ts, D_MODEL)],
        out_shape=[_sds((s, A_W)), _sds((s, D_MODEL), _MM), _sds((s, D_MODEL)), _sds((s, D_MODEL), _MM)],
        scratch=[pltpu.VMEM((ts, D_MODEL), F32)])(*og, *lg, o_b, o_m, gates, x, w_oa, w_ob, w_om, w_out, g2)


def _k_up(h2, w_up):
    s = h2.shape[0]
    ts = min(256, s)
    nu = w_up.shape[2]

    def body(h_ref, w_ref, u_ref):
        h = h_ref[...]
        for j in range(CHIPS):
            u_ref[:, j * nu:(j + 1) * nu] = jnp.dot(h, w_ref[j], preferred_element_type=F32)

    return _pc(body, name="up_proj", grid=(s // ts,), in_specs=[_row(ts, D_MODEL), _res(w_up.shape)],
               out_specs=[_row(ts, CHIPS * nu)], out_shape=[_sds((s, CHIPS * nu))])(h2, w_up)[0]


def _shift_down(v, halo, k):
    ts = v.shape[0]
    row = lax.broadcasted_iota(jnp.int32, v.shape, 0)
    out = pltpu.roll(v, k, 0)
    for r in range(k):
        out = jnp.where(row == r, halo[8 - k + r:8 - k + r + 1, :], out)
    return out


def _shift_up(v, halo, k):
    ts = v.shape[0]
    row = lax.broadcasted_iota(jnp.int32, v.shape, 0)
    out = pltpu.roll(v, ts - k, 0)
    for r in range(k):
        out = jnp.where(row == ts - k + r, halo[r:r + 1, :], out)
    return out


def _k_ffn(u, conv_w, conv_b, w_down, x1, target):
    s = u.shape[0]
    ts = min(128, s)
    nu = conv_w.shape[2]
    half = CHIPS // 2

    def body(u_ref, uh_ref, cw_ref, cb_ref, wd_ref, x1_ref, t_ref, dy_ref, f_ref, dc_ref, loss_ref, c_scr, f_scr):
        i = pl.program_id(0)
        halo = jnp.where(i > 0, uh_ref[...], 0.0)
        for j in range(CHIPS):
            cs = slice(j * nu, (j + 1) * nu)
            uj = u_ref[:, cs]
            hj = halo[:, cs]
            c_scr[:, cs] = (cb_ref[:, cs] + cw_ref[j, 0:1, :] * _shift_down(uj, hj, 2)
                            + cw_ref[j, 1:2, :] * _shift_down(uj, hj, 1) + cw_ref[j, 2:3, :] * uj)
        for j in range(half):
            a = c_scr[:, j * nu:(j + 1) * nu]
            g = c_scr[:, (half + j) * nu:(half + j + 1) * nu]
            f_scr[:, j * nu:(j + 1) * nu] = (a * _sigmoid(a) * g).astype(_MM)
        f = f_scr[...]
        f_ref[...] = f
        y = x1_ref[...] + jnp.dot(f, wd_ref[...], preferred_element_type=F32)
        err = y - t_ref[...]
        dy = err * (1.0 / D_MODEL)
        dy_ref[...] = dy

        @pl.when(i == 0)
        def _():
            loss_ref[...] = jnp.zeros_like(loss_ref)

        loss_ref[...] += _sum8(err * err)
        df = _dot_nt(dy, wd_ref[...])
        for j in range(half):
            a = c_scr[:, j * nu:(j + 1) * nu]
            g = c_scr[:, (half + j) * nu:(half + j + 1) * nu]
            sa = _sigmoid(a)
            dfj = df[:, j * nu:(j + 1) * nu]
            dc_ref[:, j * nu:(j + 1) * nu] = dfj * g * (sa * (1.0 + a * (1.0 - sa)))
            dc_ref[:, (half + j) * nu:(half + j + 1) * nu] = dfj * (a * sa)

    wide = CHIPS * nu
    return _pc(
        body, name="conv_ffn", grid=(s // ts,),
        in_specs=[_row(ts, wide), pl.BlockSpec((8, wide), lambda i: (jnp.maximum(i * (ts // 8) - 1, 0), 0)),
                  _res(conv_w.shape), _res((1, wide)), _res(w_down.shape), _row(ts, D_MODEL), _row(ts, D_MODEL)],
        out_specs=[_row(ts, D_MODEL), _row(ts, D_FF), _row(ts, wide), _acc((8, D_MODEL))],
        out_shape=[_sds((s, D_MODEL)), _sds((s, D_FF), _MM), _sds((s, wide)), _sds((8, D_MODEL))],
        scratch=[pltpu.VMEM((ts, wide), F32), pltpu.VMEM((ts, D_FF), _MM)])(u, u, conv_w, conv_b, w_down, x1, target)


def _k_conv_bwd(dc, u, conv_w, w_up, x1, g2, dy):
    s = u.shape[0]
    ts = min(128, s)
    nu = conv_w.shape[2]
    wide = CHIPS * nu
    last = s // ts - 1

    def body(dc_ref, dn_ref, u_ref, uh_ref, cw_ref, wu_ref, x1_ref, g_ref, dy_ref,
             dx1_ref, du_ref, cacc_ref, gacc_ref):
        i = pl.program_id(0)

        @pl.when(i == 0)
        def _():
            cacc_ref[...] = jnp.zeros_like(cacc_ref)
            gacc_ref[...] = jnp.zeros_like(gacc_ref)

        uhalo = jnp.where(i > 0, uh_ref[...], 0.0)
        dhalo = jnp.where(i < last, dn_ref[...], 0.0)
        dh2 = jnp.zeros((ts, D_MODEL), F32)
        for j in range(CHIPS):
            cs = slice(j * nu, (j + 1) * nu)
            dcj, uj = dc_ref[:, cs], u_ref[:, cs]
            cacc_ref[0, :, cs] += _sum8(dcj)
            cacc_ref[1, :, cs] += _sum8(dcj * _shift_down(uj, uhalo[:, cs], 2))
            cacc_ref[2, :, cs] += _sum8(dcj * _shift_down(uj, uhalo[:, cs], 1))
            cacc_ref[3, :, cs] += _sum8(dcj * uj)
            du = (cw_ref[j, 2:3, :] * dcj + cw_ref[j, 1:2, :] * _shift_up(dcj, dhalo[:, cs], 1)
                  + cw_ref[j, 0:1, :] * _shift_up(dcj, dhalo[:, cs], 2)).astype(_MM)
            du_ref[:, cs] = du
            dh2 = dh2 + _dot_nt(du, wu_ref[j])
        xh, r = _rms(x1_ref[...])
        gacc_ref[...] += _sum8(dh2 * xh)
        dx1_ref[...] = dy_ref[...] + _rms_bwd(dh2, xh, r, g_ref[...])

    return _pc(
        body, name="conv_up_bwd", grid=(s // ts,),
        in_specs=[_row(ts, wide),
                  pl.BlockSpec((8, wide), lambda i: (jnp.minimum((i + 1) * (ts // 8), s // 8 - 1), 0)),
                  _row(ts, wide), pl.BlockSpec((8, wide), lambda i: (jnp.maximum(i * (ts // 8) - 1, 0), 0)),
                  _res(conv_w.shape), _res(w_up.shape), _row(ts, D_MODEL), _res((1, D_MODEL)), _row(ts, D_MODEL)],
        out_specs=[_row(ts, D_MODEL), _row(ts, wide), _acc((4, 8, wide)), _acc((8, D_MODEL))],
        out_shape=[_sds((s, D_MODEL)), _sds((s, wide), _MM), _sds((4, 8, wide)), _sds((8, D_MODEL))])(
            dc, dc, u, u, conv_w, w_up, x1, g2, dy)


def _k_merge_bwd(dx1, og, lg, o_a, o_b, o_m, gates, w_oa, w_ob, w_om, w_out, dep):
    s = dx1.shape[0]
    ts = min(256, s)
    nc = w_oa.shape[2]

    def body(dx_ref, o0, o1, o2, l0, l1, l2, oa_ref, ob_ref, om_ref, gt_ref, woa, wob, wom, wout, dep_ref,
             dgp_ref, dpa_ref, dpb_ref, dpm_ref, dog0, dog1, dog2, dl0, dl1, dl2, dob_ref, dom_ref, bacc_ref):
        i = pl.program_id(0)

        @pl.when(i == 0)
        def _():
            bacc_ref[...] = jnp.zeros_like(bacc_ref)

        dmer = _dot_nt(dx_ref[...], wout[...])
        oa, ob, om = oa_ref[...], ob_ref[...], om_ref[...]
        doa = jnp.zeros((ts, A_W), F32)
        dob = jnp.zeros((ts, B_QH * HEAD), F32)
        dom = jnp.zeros((ts, M_W), F32)
        for j in range(CHIPS):
            prods = _branch_products(oa, ob, om, woa, wob, wom, j)
            dmj = dmer[:, j * nc:(j + 1) * nc]
            dps = []
            for br, (p, dref) in enumerate(zip(prods, (dpa_ref, dpb_ref, dpm_ref))):
                cs = slice(br * D_MODEL + j * nc, br * D_MODEL + (j + 1) * nc)
                gt = gt_ref[:, cs]
                dgp = dmj * p * gt * (1.0 - gt)
                dgp_ref[:, cs] = dgp.astype(_MM)
                bacc_ref[:, cs] += _sum8(dgp)
                dp = (dmj * gt).astype(_MM)
                dref[:, j * nc:(j + 1) * nc] = dp
                dps.append(dp)
            doa = doa + _dot_nt(dps[0], woa[j])
            dob = dob + _dot_nt(dps[1], wob[j])
            dom = dom + _dot_nt(dps[2], wom[j])
        dob_ref[...] = dob
        dom_ref[...] = dom
        ws = _group_weights(l0[...], l1[...], l2[...])
        dsum = _seg_mean(doa * oa, HEAD) * float(HEAD)
        for w, dref, lref in zip(ws, (dog0, dog1, dog2), (dl0, dl1, dl2)):
            dref[...] = w * doa
            lref[...] = w * dsum

    return _pc(
        body, name="merge_out_bwd", grid=(s // ts,),
        in_specs=[_row(ts, D_MODEL)] + [_row(ts, A_W)] * 7 + [_row(ts, B_QH * HEAD), _row(ts, M_W), _row(ts, 3 * D_MODEL),
                                                              _res(w_oa.shape), _res(w_ob.shape), _res(w_om.shape),
                                                              _res(w_out.shape), _res((8, 128))],
        out_specs=[_row(ts, 3 * D_MODEL)] + [_row(ts, D_MODEL)] * 3 + [_row(ts, A_W)] * 6
        + [_row(ts, B_QH * HEAD), _row(ts, M_W), _acc((8, 3 * D_MODEL))],
        out_shape=[_sds((s, 3 * D_MODEL), _MM)] + [_sds((s, D_MODEL), _MM)] * 3 + [_sds((s, A_W))] * 6
        + [_sds((s, B_QH * HEAD)), _sds((s, M_W)), _sds((8, 3 * D_MODEL))])(
            dx1, *og, *lg, o_a, o_b, o_m, gates, w_oa, w_ob, w_om, w_out, dep)


def _k_mem_bwd(m_q, gq, mk, mv, o_m, do_m):
    s = m_q.shape[0]
    n = mk.shape[0]
    ts = min(256, s)
    scale = M_HD ** -0.5

    def body(q_ref, g_ref, mk_ref, mv_ref, o_ref, do_ref, dq_ref, dmk_ref, dmv_ref, gacc_ref):
        i = pl.program_id(0)

        @pl.when(i == 0)
        def _():
            dmk_ref[...] = jnp.zeros_like(dmk_ref)
            dmv_ref[...] = jnp.zeros_like(dmv_ref)
            gacc_ref[...] = jnp.zeros_like(gacc_ref)

        gain = g_ref[...]
        qh, r = _seg_norm(q_ref[...], M_HD)
        qn = (qh * gain).astype(_MM)
        do = do_ref[...]
        delta = _seg_mean(do * o_ref[...], M_HD) * float(M_HD)
        dqn = []
        for h in range(M_HEADS):
            hs = slice(h * M_HD, (h + 1) * M_HD)
            p = _mem_probs(qn[:, hs], mk_ref[:, hs])
            dp = _dot_nt(do[:, hs], mv_ref[:, hs])
            ds = (p * (dp - delta[:, hs][:, 0:1]) * scale).astype(_MM)
            dqn.append(_dot(ds, mk_ref[:, hs]))
            dmk_ref[:, hs] += _dot_tn(ds, qn[:, hs])
            dmv_ref[:, hs] += _dot_tn(p, do[:, hs])
        dqn = jnp.concatenate(dqn, axis=1)
        gacc_ref[...] += _sum8(dqn * qh)
        z = dqn * gain
        dq_ref[...] = (r * (z - qh * _seg_mean(z * qh, M_HD))).astype(_MM)

    return _pc(
        body, name="mem_attn_bwd", grid=(s // ts,),
        in_specs=[_row(ts, M_W), _res((1, M_W)), _res((n, M_W)), _res((n, M_W)), _row(ts, M_W), _row(ts, M_W)],
        out_specs=[_row(ts, M_W), _acc((n, M_W)), _acc((n, M_W)), _acc((8, M_W))],
        out_shape=[_sds((s, M_W), _MM), _sds((n, M_W)), _sds((n, M_W)), _sds((8, M_W))])(m_q, gq, mk, mv, o_m, do_m)


def _k_memkv_bwd(mem, mem_norm, w_kv, m_k_norm, mem_n, kv, dmk, dmv):
    n = mem.shape[0]

    def body(m_ref, g_ref, w_ref, gk_ref, mn_ref, kv_ref, dmk_ref, dmv_ref, dw_ref, dg_ref, dgk_ref):
        gk = gk_ref[...]
        kh, r = _seg_norm(kv_ref[:, :M_W], M_HD)
        dmk = dmk_ref[...]
        dgk_ref[...] = _sum8(dmk * kh)
        z = dmk * gk
        dk = r * (z - kh * _seg_mean(z * kh, M_HD))
        dkv = jnp.concatenate([dk, dmv_ref[...]], axis=1).astype(_MM)
        dw_ref[...] = _dot_tn(mn_ref[...], dkv)
        dmn = _dot_nt(dkv, w_ref[...])
        mh, _ = _rms(m_ref[...])
        dg_ref[...] = _sum8(dmn * mh)

    return _pc(body, name="mem_kv_bwd", grid=(1,),
               in_specs=[_acc((n, D_MODEL)), _acc((1, D_MODEL)), _acc(w_kv.shape), _acc((1, M_W)), _acc((n, D_MODEL)),
                         _acc((n, 2 * M_W)), _acc((n, M_W)), _acc((n, M_W))],
               out_specs=[_acc(w_kv.shape), _acc((8, D_MODEL)), _acc((8, M_W))],
               out_shape=[_sds(w_kv.shape), _sds((8, D_MODEL)), _sds((8, M_W))])(
                   mem, mem_norm, w_kv, m_k_norm, mem_n, kv, dmk, dmv)


def _k_band_bwd(qn, kn, vn, do, lse, dl_or_o, *, hq, hk, max_dist, segs, sink, name):
    rows = qn.shape[0]
    nb = rows // BLK
    units = hq // A_HEADS
    shared = hk != hq
    wq, wk = hq * HEAD, hk * HEAD
    scale = HEAD ** -0.5

    def body(*refs):
        (qb_ref, qx_ref, kb_ref, kp_ref, vb_ref, vp_ref, dob_ref, dox_ref, lb_ref, lx_ref, eb_ref, ex_ref) = refs[:12]
        if sink is None:
            dq_ref, dk_ref, dv_ref = refs[12:]
        else:
            sk_ref, dq_ref, dk_ref, dv_ref, sacc_ref = refs[12:]
        b = pl.program_id(0)
        bias1 = _band_bias(jnp.where(_first_flag(b, segs, nb), 1 << 20, BLK - max_dist), True)
        bias2 = _band_bias(jnp.where(_first_flag(b + 1, segs, nb), 1 << 20, BLK - max_dist), False)
        if sink is not None:
            @pl.when(b == 0)
            def _():
                sacc_ref[...] = jnp.zeros_like(sacc_ref)

        for u in range(units):
            us = slice(u * A_W, (u + 1) * A_W)
            q4, qx4, do4, dox4 = qb_ref[:, us], qx_ref[:, us], dob_ref[:, us], dox_ref[:, us]
            k4, v4 = _unit_kv(kp_ref, kb_ref, u, shared), _unit_kv(vp_ref, vb_ref, u, shared)
            kd, vd = _blockdiag(k4), _blockdiag(v4)
            kdc, vdc = _blockdiag(k4[BLK:]), _blockdiag(v4[BLK:])
            if sink is None:
                dlt_b, dlt_x = eb_ref[:, us], ex_ref[:, us]
            else:
                dlt_b = _seg_sum64(do4.astype(F32) * eb_ref[:, us])
                dlt_x = _seg_sum64(dox4.astype(F32) * ex_ref[:, us])
            s1, dp1 = _dot_nt(q4, kd) * scale, _dot_nt(do4, vd)
            s2, dp2 = _dot_nt(qx4, kdc) * scale, _dot_nt(dox4, vdc)
            ds1, ds1c, p1c, ds2, p2 = [], [], [], [], []
            for h in range(A_HEADS):
                col = slice(u * A_W + h * HEAD, u * A_W + h * HEAD + 1)
                ucol = slice(h * HEAD, h * HEAD + 1)
                wide, narrow = slice(h * 2 * BLK, (h + 1) * 2 * BLK), slice(h * BLK, (h + 1) * BLK)
                l_b, l_x = lb_ref[:, col], lx_ref[:, col]
                p = jnp.exp(s1[:, wide] + bias1 - l_b)
                ds = p * (dp1[:, wide] - dlt_b[:, ucol]) * scale
                ds1.append(ds.astype(_MM))
                ds1c.append(ds[:, BLK:].astype(_MM))
                p1c.append(p[:, BLK:].astype(_MM))
                px = jnp.exp(s2[:, narrow] + bias2 - l_x)
                ds2.append((px * (dp2[:, narrow] - dlt_x[:, ucol]) * scale).astype(_MM))
                p2.append(px.astype(_MM))
                if sink is not None:
                    j = u * A_HEADS + h
                    sacc_ref[:, j:j + 1] += -jnp.exp(sk_ref[j] - l_b) * dlt_b[:, ucol]
            dq_ref[:, us] = _dot(jnp.concatenate(ds1, axis=1), kd)
            dk4 = _fold_diag(_dot_tn(jnp.concatenate(ds1c, axis=1), q4) + _dot_tn(jnp.concatenate(ds2, axis=1), qx4), BLK)
            dv4 = _fold_diag(_dot_tn(jnp.concatenate(p1c, axis=1), do4) + _dot_tn(jnp.concatenate(p2, axis=1), dox4), BLK)
            if shared:
                fold = lambda t: (t[:, 0:HEAD] + t[:, HEAD:2 * HEAD]) + (t[:, 2 * HEAD:3 * HEAD] + t[:, 3 * HEAD:])
                dk_ref[:, u * HEAD:(u + 1) * HEAD] = fold(dk4)
                dv_ref[:, u * HEAD:(u + 1) * HEAD] = fold(dv4).astype(_MM)
            else:
                dk_ref[:, us] = dk4
                dv_ref[:, us] = dv4.astype(_MM)

    cur = lambda w: pl.BlockSpec((BLK, w), lambda i: (i, 0))
    prev = lambda w: pl.BlockSpec((BLK, w), lambda i: (jnp.maximum(i - 1, 0), 0))
    nxt = lambda w: pl.BlockSpec((BLK, w), lambda i: (jnp.minimum(i + 1, nb - 1), 0))
    in_specs = [cur(wq), nxt(wq), cur(wk), prev(wk), cur(wk), prev(wk), cur(wq), nxt(wq), cur(wq), nxt(wq), cur(wq), nxt(wq)]
    args = [qn, qn, kn, kn, vn, vn, do, do, lse, lse, dl_or_o, dl_or_o]
    out_specs = [cur(wq), cur(wk), cur(wk)]
    out_shape = [_sds((rows, wq)), _sds((rows, wk)), _sds((rows, wk), _MM)]
    if sink is not None:
        in_specs.append(pl.BlockSpec(memory_space=pltpu.SMEM))
        args.append(sink)
        out_specs.append(_acc((BLK, 128)))
        out_shape.append(_sds((BLK, 128)))
    return _pc(body, name=name, grid=(nb,), in_specs=in_specs, out_specs=out_specs, out_shape=out_shape)(*args)


def _k_prep_bwd(srcs, dqn, dkn, gq, gk, tabs, *, wq, wk, rows_per_gain, name):
    rows = dqn.shape[0]
    ts = min(256, rows)
    ngain = gq.shape[0]

    def body(q_ref, k_ref, dq_ref, dk_ref, gq_ref, gk_ref, c_ref, sa_ref, sb_ref, oq_ref, ok_ref, aq_ref, ak_ref):
        i = pl.program_id(0)

        @pl.when(lax.rem(i * ts, rows_per_gain) == 0)
        def _():
            aq_ref[...] = jnp.zeros_like(aq_ref)
            ak_ref[...] = jnp.zeros_like(ak_ref)

        c, sa, sb = c_ref[...], sa_ref[...], sb_ref[...]
        for x_ref, d_ref, g_ref, o_ref, a_ref in ((q_ref, dq_ref, gq_ref, oq_ref, aq_ref),
                                                   (k_ref, dk_ref, gk_ref, ok_ref, ak_ref)):
            xh, r = _seg_norm(x_ref[...], HEAD)
            dt = _rope_bwd(d_ref[...], c, sa, sb)
            a_ref[...] += _sum8(dt * xh)
            z = dt * g_ref[...]
            o_ref[...] = (r * (z - xh * _seg_mean(z * xh, HEAD))).astype(_MM)

    gspec = lambda w: pl.BlockSpec((None, 1, w), lambda i: ((i * ts) // rows_per_gain, 0, 0))
    aspec = lambda w: pl.BlockSpec((None, 8, w), lambda i: ((i * ts) // rows_per_gain, 0, 0))
    return _pc(
        body, name=name, grid=(rows // ts,),
        in_specs=[_row(ts, wq, srcs[0][1]), _row(ts, wk, srcs[1][1]), _row(ts, wq), _row(ts, wk), gspec(wq), gspec(wk)]
        + [_row(ts, 128)] * 3,
        out_specs=[_row(ts, wq), _row(ts, wk), aspec(wq), aspec(wk)],
        out_shape=[_sds((rows, wq), _MM), _sds((rows, wk), _MM), _sds((ngain, 8, wq)), _sds((ngain, 8, wk))])(
            srcs[0][0], srcs[1][0], dqn, dkn, gq, gk, *tabs)


def _k_in_bwd(pieces, dgp, x, g1, dx1, w_in, w_gate):
    s = x.shape[0]
    ts = min(256, s)
    nin, ng = w_in.shape[2], w_gate.shape[2]
    widths = [p.shape[1] for p in pieces]
    ncol = sum(widths)

    def body(*refs):
        p_refs = refs[:len(pieces)]
        dgp_ref, x_ref, g_ref, dx1_ref, wi_ref, wg_ref, gx_ref, dpj_ref, gacc_ref = refs[len(pieces):]
        i = pl.program_id(0)

        @pl.when(i == 0)
        def _():
            gacc_ref[...] = jnp.zeros_like(gacc_ref)

        off = 0
        for p_ref, w in zip(p_refs, widths):
            dpj_ref[:, off:off + w] = p_ref[...]
            off += w
        dh = jnp.zeros((ts, D_MODEL), F32)
        for j in range(CHIPS):
            dh = dh + _dot_nt(dpj_ref[:, j * nin:(j + 1) * nin], wi_ref[j])
            dh = dh + _dot_nt(dgp_ref[:, j * ng:(j + 1) * ng], wg_ref[j])
        xh, r = _rms(x_ref[...])
        gacc_ref[...] += _sum8(dh * xh)
        gx_ref[...] = dx1_ref[...] + _rms_bwd(dh, xh, r, g_ref[...])

    return _pc(
        body, name="in_proj_bwd", grid=(s // ts,),
        in_specs=[_row(ts, w) for w in widths] + [_row(ts, CHIPS * ng), _row(ts, D_MODEL), _res((1, D_MODEL)),
                                                  _row(ts, D_MODEL), _res(w_in.shape), _res(w_gate.shape)],
        out_specs=[_row(ts, D_MODEL), _row(ts, ncol), _acc((8, D_MODEL))],
        out_shape=[_sds((s, D_MODEL)), _sds((s, ncol), _MM), _sds((8, D_MODEL))])(*pieces, dgp, x, g1, dx1, w_in, w_gate)


def _k_wgrad(a, b, *, nblk, stacked, name):
    s, k = a.shape
    n = b.shape[1]
    nb = n // nblk
    ts = min(512, s)

    def body(a_ref, b_ref, o_ref):
        @pl.when(pl.program_id(1) == 0)
        def _():
            o_ref[...] = jnp.zeros_like(o_ref)

        o_ref[...] += _dot_tn(a_ref[...], b_ref[...])

    if stacked:
        out_spec, out_shape = pl.BlockSpec((None, k, nb), lambda g, t: (g, 0, 0)), _sds((nblk, k, nb))
    else:
        out_spec, out_shape = pl.BlockSpec((k, nb), lambda g, t: (0, g)), _sds((k, n))
    return _pc(body, name=name, grid=(nblk, s // ts),
               in_specs=[pl.BlockSpec((ts, k), lambda g, t: (t, 0)), pl.BlockSpec((ts, nb), lambda g, t: (t, g))],
               out_specs=[out_spec], out_shape=[out_shape])(a, b)[0]


def _to_res(t, d):
    s, c = t.shape
    return t if d == 1 else t.reshape(s // d, d, c).transpose(1, 0, 2).reshape(s, c)


def _from_res(t, d):
    s, c = t.shape
    return t if d == 1 else t.reshape(d, s // d, c).transpose(1, 0, 2).reshape(s, c)


def _tile_gain(g, heads):
    return jnp.tile(g, (1,) * (g.ndim - 1) + (heads,))[..., None, :]


def _local_step(x, mem, pos, target, small, w_in, get_rest, on_grads):
    s = x.shape[0]
    nblk = s // BLK
    g1, g2 = small["attn_norm"], small["ffn_norm"]

    pos_rows = jnp.concatenate([_to_res(pos[:, None], d)[:, 0] for _, d in A_GROUPS] + [pos])
    tabs = _rope_tables(pos_rows)
    tabs_a = tuple(t[:3 * s] for t in tabs)
    tabs_b = tuple(t[3 * s:] for t in tabs)

    h, qa0, qa1, qa2, q_b, k_b, v_b, m_q = _k_in(x, g1, w_in)

    qkv_a = jnp.concatenate([_to_res(t, d) for t, (_, d) in zip((qa0, qa1, qa2), A_GROUPS)], axis=0)
    gq_a = _tile_gain(small["a_q_norm"], A_HEADS)
    gk_a = _tile_gain(small["a_k_norm"], A_HEADS)
    src_a = ((qkv_a, 0), (qkv_a, 1), (qkv_a, 2))
    qn_a, kn_a, vn_a = _k_prep(src_a, gq_a, gk_a, tabs_a, wq=A_W, wk=A_W, rows_per_gain=s, name="prep_a")
    segs_a = tuple((gi * nblk, nblk // d) for gi, (_, d) in enumerate(A_GROUPS))
    o_res, l_res = _k_band_fwd(qn_a, kn_a, vn_a, hq=A_HEADS, hk=A_HEADS, max_dist=BLK, segs=segs_a, sink=None,
                               name="attn_a")
    og = [_from_res(o_res[gi * s:(gi + 1) * s], d) for gi, (_, d) in enumerate(A_GROUPS)]
    lg = [_from_res(l_res[gi * s:(gi + 1) * s], d) for gi, (_, d) in enumerate(A_GROUPS)]

    gq_b = _tile_gain(small["b_q_norm"], B_QH)
    gk_b = _tile_gain(small["b_k_norm"], B_KVH)
    src_b = ((q_b, 0), (k_b, 0), (v_b, 0))
    qn_b, kn_b, vn_b = _k_prep(src_b, gq_b, gk_b, tabs_b, wq=B_QH * HEAD, wk=B_KVH * HEAD, rows_per_gain=s,
                               name="prep_b")
    sink_x = small["b_sinks"][0]
    segs_b = ((0, nblk),)
    o_b, l_b = _k_band_fwd(qn_b, kn_b, vn_b, hq=B_QH, hk=B_KVH, max_dist=B_WINDOW - 1, segs=segs_b, sink=sink_x,
                           name="attn_b")

    wts = get_rest(o_b)
    gates = _k_gate(h, wts["w_gate"], small["b_gate"])

    gq_m = _tile_gain(small["m_q_norm"], M_HEADS)[0]
    gk_m = _tile_gain(small["m_k_norm"], M_HEADS)[0]
    mem_n, kv, mk, mv = _k_memkv(mem, small["mem_norm"], wts["w_mem_kv"], gk_m)
    o_m = _k_mem_fwd(m_q, gq_m, mk, mv)

    o_a, merged, x1, h2 = _k_merge(og, lg, o_b, o_m, gates, x, wts["w_o_a"], wts["w_o_b"], wts["w_o_m"],
                                   wts["w_out"], g2)
    u = _k_up(h2, wts["w_up"])
    dy, f, dc, loss_acc = _k_ffn(u, wts["conv_w"], small["conv_b"], wts["w_down"], x1, target)
    loss = (0.5 / D_MODEL) * jnp.sum(loss_acc)

    dx1, du, cacc, g2acc = _k_conv_bwd(dc, u, wts["conv_w"], wts["w_up"], x1, g2, dy)
    tok = on_grads({"w_up": _k_wgrad(h2, du, nblk=CHIPS, stacked=True, name="dw_up"),
                    "w_down": _k_wgrad(f, dy, nblk=2, stacked=False, name="dw_down").reshape(CHIPS, -1, D_MODEL)})
    (dgp, dp_a, dp_b, dp_m, dog0, dog1, dog2, dl0, dl1, dl2, do_b, do_m, bacc) = _k_merge_bwd(
        dx1, og, lg, o_a, o_b, o_m, gates, wts["w_o_a"], wts["w_o_b"], wts["w_o_m"], wts["w_out"], tok)
    tok = on_grads({"w_gate": _k_wgrad(h, dgp, nblk=CHIPS, stacked=True, name="dw_gate"),
                    "w_o_a": _k_wgrad(o_a, dp_a, nblk=CHIPS, stacked=True, name="dw_o_a"),
                    "w_o_b": _k_wgrad(o_b, dp_b, nblk=CHIPS, stacked=True, name="dw_o_b"),
                    "w_o_m": _k_wgrad(o_m, dp_m, nblk=CHIPS, stacked=True, name="dw_o_m"),
                    "w_out": _k_wgrad(merged, dx1, nblk=1, stacked=False, name="dw_out").reshape(CHIPS, -1, D_MODEL)})

    dq_m, dmk, dmv, gqm_acc = _k_mem_bwd(m_q, gq_m + tok[0:1, 0:1], mk, mv, o_m, do_m)
    dw_kv, gmem_acc, gkm_acc = _k_memkv_bwd(mem, small["mem_norm"], wts["w_mem_kv"], gk_m, mem_n, kv, dmk, dmv)

    dq_bn, dk_bn, dv_b, sacc = _k_band_bwd(qn_b, kn_b, vn_b, do_b, l_b, o_b, hq=B_QH, hk=B_KVH,
                                           max_dist=B_WINDOW - 1, segs=segs_b, sink=sink_x, name="attn_b_bwd")
    dq_b, dk_b, gqb_acc, gkb_acc = _k_prep_bwd(src_b, dq_bn, dk_bn, gq_b, gk_b, tabs_b, wq=B_QH * HEAD,
                                               wk=B_KVH * HEAD, rows_per_gain=s, name="prep_b_bwd")

    do_res = jnp.concatenate([_to_res(t, d) for t, (_, d) in zip((dog0, dog1, dog2), A_GROUPS)], axis=0)
    dl_res = jnp.concatenate([_to_res(t, d) for t, (_, d) in zip((dl0, dl1, dl2), A_GROUPS)], axis=0)
    dq_an, dk_an, dv_a = _k_band_bwd(qn_a, kn_a, vn_a, do_res, l_res, dl_res, hq=A_HEADS, hk=A_HEADS, max_dist=BLK,
                                     segs=segs_a, sink=None, name="attn_a_bwd")
    dq_a, dk_a, gqa_acc, gka_acc = _k_prep_bwd(src_a, dq_an, dk_an, gq_a, gk_a, tabs_a, wq=A_W, wk=A_W,
                                               rows_per_gain=s, name="prep_a_bwd")
    pieces = []
    for gi, (_, d) in enumerate(A_GROUPS):
        rs = slice(gi * s, (gi + 1) * s)
        pieces += [_from_res(t[rs], d) for t in (dq_a, dk_a, dv_a)]
    pieces += [dq_b, dk_b, dv_b, dq_m]
    grad_x, dproj, g1acc = _k_in_bwd(pieces, dgp, x, g1, dx1, w_in, wts["w_gate"])
    on_grads({"w_in": _k_wgrad(h, dproj, nblk=CHIPS, stacked=True, name="dw_in"),
              "w_mem_kv": dw_kv.reshape(CHIPS, -1, 2 * M_W)})

    def fold(acc, heads):
        v = jnp.sum(acc, axis=-2)
        return jnp.sum(v.reshape(v.shape[:-1] + (heads, -1)), axis=-2)

    csum = jnp.sum(cacc, axis=1)
    sml = {
        "attn_norm": jnp.sum(g1acc, axis=0), "a_q_norm": fold(gqa_acc, A_HEADS), "a_k_norm": fold(gka_acc, A_HEADS),
        "b_q_norm": fold(gqb_acc[0], B_QH), "b_k_norm": fold(gkb_acc[0], B_KVH),
        "b_sinks": jnp.sum(sacc, axis=0)[:B_QH], "mem_norm": jnp.sum(gmem_acc, axis=0),
        "m_q_norm": fold(gqm_acc, M_HEADS), "m_k_norm": fold(gkm_acc, M_HEADS),
        "b_gate": jnp.sum(bacc, axis=0), "ffn_norm": jnp.sum(g2acc, axis=0),
        "conv_w": csum[1:], "conv_b": csum[0],
    }
    return loss, grad_x, sml


def _mesh_pos():
    return lax.axis_index("x"), lax.axis_index("y"), lax.axis_index("c")


def _chip_peers(x, y):
    return [(1 - x, y), (x, 1 - y), (1 - x, 1 - y)]


_ANY = pl.BlockSpec(memory_space=pl.ANY)


def _comm_call(body, *, name, n_in, out_shape, scratch):
    return pl.pallas_call(body, name=name, in_specs=[_ANY] * n_in, out_specs=[_ANY] * len(out_shape),
                          out_shape=out_shape, scratch_shapes=scratch)


def _remote(src, dst, send_sem, recv_sem, dev):
    return pltpu.make_async_remote_copy(src_ref=src, dst_ref=dst, send_sem=send_sem, recv_sem=recv_sem,
                                        device_id=dev, device_id_type=MESH)


def _gather_shards(shards):
    nt = len(shards)
    split = [sh.shape[0] % 16 == 0 for sh in shards]

    def body(*refs):
        ins, outs = refs[:nt], refs[nt:2 * nt]
        ici_s, ici_r, fwd_s, fwd_r, own_s, own_r = refs[2 * nt:]
        x, y, c = _mesh_pos()
        me = 2 * x + y
        sib = (x, y, 1 - c)
        peers = _chip_peers(x, y)

        def half(ref, t, who):
            if not split[t]:
                return ref
            hr = shards[t].shape[0] // 2
            return ref.at[pl.ds(pl.multiple_of(who * hr, 8), hr), :]

        pending = []
        for t in range(nt):
            own = _remote(ins[t], outs[t].at[me], own_s.at[t], own_r.at[t], sib)
            own.start()
            pending.append(own.wait)
            for k, (px, py) in enumerate(peers):
                rc = _remote(half(ins[t], t, c), half(outs[t].at[me], t, c), ici_s.at[t, k], ici_r.at[t, k], (px, py, c))
                rc.start()
                pending.append(rc.wait_send)
        for t in range(nt):
            for k, (px, py) in enumerate(peers):
                land = half(outs[t].at[2 * px + py], t, c)
                _remote(land, land, ici_s.at[t, k], ici_r.at[t, k], (px, py, c)).wait_recv()
                if split[t]:
                    fw = _remote(land, land, fwd_s.at[t, k], fwd_r.at[t, k], sib)
                    fw.start()
                    pending.append(fw.wait_send)
                    other = half(outs[t].at[2 * px + py], t, 1 - c)
                    pending.append(_remote(other, other, fwd_s.at[t, k], fwd_r.at[t, k], sib).wait_recv)
        for wait in pending:
            wait()

    out_shape = [_sds((CHIPS,) + sh.shape, sh.dtype) for sh in shards]
    dma = pltpu.SemaphoreType.DMA
    scratch = [dma((nt, 3)), dma((nt, 3)), dma((nt, 3)), dma((nt, 3)), dma((nt,)), dma((nt,))]
    return _comm_call(body, name="gather_weights", n_in=nt, out_shape=out_shape, scratch=scratch)(*shards)


def _pair_split(grads, name):
    nt = len(grads)

    def body(*refs):
        ins, got = refs[:nt], refs[nt:2 * nt]
        send_sems, recv_sems = refs[2 * nt:]
        x, y, c = _mesh_pos()
        cps = []
        for t in range(nt):
            hr = ins[t].shape[1] // 2
            give = ins[t].at[:, pl.ds(pl.multiple_of((1 - c) * hr, 8), hr), :]
            rc = _remote(give, got[t], send_sems.at[t], recv_sems.at[t], (x, y, 1 - c))
            rc.start()
            cps.append(rc)
        for rc in cps:
            rc.wait()

    half = [_sds((CHIPS, g.shape[1] // 2, g.shape[2]), g.dtype) for g in grads]
    scratch = [pltpu.SemaphoreType.DMA((nt,)), pltpu.SemaphoreType.DMA((nt,))]
    return _comm_call(body, name=name, n_in=nt, out_shape=half, scratch=scratch)(*grads)


def _chip_scatter(parts, name):
    nt = len(parts)

    def body(*refs):
        ins, outs = refs[:nt], refs[nt:2 * nt]
        send_sems, recv_sems = refs[2 * nt:]
        x, y, c = _mesh_pos()
        cps = []
        for t in range(nt):
            for k, (px, py) in enumerate(_chip_peers(x, y)):
                rc = _remote(ins[t].at[2 * px + py], outs[t].at[k], send_sems.at[t, k], recv_sems.at[t, k], (px, py, c))
                rc.start()
                cps.append(rc)
        for cp in cps:
            cp.wait()

    out_shape = [_sds((3,) + p.shape[1:], p.dtype) for p in parts]
    scratch = [pltpu.SemaphoreType.DMA((nt, 3)), pltpu.SemaphoreType.DMA((nt, 3))]
    return _comm_call(body, name=name, n_in=nt, out_shape=out_shape, scratch=scratch)(*parts)


def _pair_join(halves):
    nt = len(halves)

    def body(*refs):
        ins, got = refs[:nt], refs[nt:2 * nt]
        send_sems, recv_sems = refs[2 * nt:]
        x, y, c = _mesh_pos()
        cps = []
        for t in range(nt):
            rc = _remote(ins[t], got[t], send_sems.at[t], recv_sems.at[t], (x, y, 1 - c))
            rc.start()
            cps.append(rc)
        for rc in cps:
            rc.wait()

    out_shape = [_sds(hf.shape, hf.dtype) for hf in halves]
    scratch = [pltpu.SemaphoreType.DMA((nt,)), pltpu.SemaphoreType.DMA((nt,))]
    return _comm_call(body, name="grad_pair_join", n_in=nt, out_shape=out_shape, scratch=scratch)(*halves)


_HBM = pl.BlockSpec(memory_space=pltpu.HBM)
_SEMS = pl.BlockSpec(memory_space=pltpu.SEMAPHORE)
_EFFECT = pltpu.SideEffectType.DATAFLOW_SIDE_EFFECTING


def _bcast_copies(ins, lands, send_sems, recv_sems):
    x, y, c = _mesh_pos()
    me = 2 * x + y
    targets = [((px, py, c), 2 * px + py) for px, py in _chip_peers(x, y)] + [((x, y, 1 - c), me)]
    out = []
    for t in range(len(ins)):
        for k, (dev, idx) in enumerate(targets):
            i = t * len(targets) + k
            arrival = lambda t=t, i=i, idx=idx, dev=dev: _remote(ins[t], lands[t].at[idx], send_sems.at[i],
                                                                 recv_sems.at[i], dev)
            out.append((_remote(ins[t], lands[t].at[me], send_sems.at[i], recv_sems.at[i], dev), arrival))
    return out


def _scatter_copies(ins, lands, send_sems, recv_sems):
    x, y, c = _mesh_pos()
    out = []
    for t in range(len(ins)):
        for k, (px, py) in enumerate(_chip_peers(x, y)):
            i = t * 3 + k
            cp = _remote(ins[t].at[2 * px + py], lands[t].at[k], send_sems.at[i], recv_sems.at[i], (px, py, c))
            out.append((cp, lambda cp=cp: cp))
    return out


def _split_start(copies, srcs, land_shapes, ncopy, dep, name):
    nt = len(srcs)

    def body(*refs):
        ins, lands = refs[:nt], refs[nt:2 * nt]
        send_sems, recv_sems, token = refs[2 * nt + 1], refs[2 * nt + 2], refs[-1]
        for send, _ in copies(ins, lands, send_sems, recv_sems):
            send.start()
        token[...] = jnp.zeros_like(token)

    lands = [pltpu.with_memory_space_constraint(lax.empty(sh, a.dtype), pltpu.HBM) for sh, a in zip(land_shapes, srcs)]
    srcs = [pltpu.with_memory_space_constraint(a, pltpu.HBM) for a in srcs]
    dma = pltpu.SemaphoreType.DMA
    out_shape = ([dma((nt * ncopy,)), dma((nt * ncopy,))] + [pltpu.HBM(a.shape, a.dtype) for a in srcs + lands]
                 + [_sds((8, 128))])
    outs = pl.pallas_call(
        body, name=name, in_specs=[_HBM] * (2 * nt) + [_ANY],
        out_specs=[_SEMS, _SEMS] + [_HBM] * (2 * nt) + [pl.BlockSpec(memory_space=pltpu.VMEM)], out_shape=out_shape,
        input_output_aliases={i: 2 + i for i in range(2 * nt)},
        compiler_params=pltpu.CompilerParams(has_side_effects=_EFFECT))(*srcs, *lands, dep)
    return outs[0], outs[1], outs[2:2 + nt], outs[2 + nt:2 + 2 * nt], outs[-1]


def _split_wait(copies, send_sems, recv_sems, srcs, lands, after, name):
    nt = len(srcs)

    def body(*refs):
        ins, lnd = refs[:nt], refs[nt:2 * nt]
        for send, arrival in copies(ins, lnd, refs[2 * nt], refs[2 * nt + 1]):
            send.wait_send()
            arrival().wait_recv()

    outs = pl.pallas_call(
        body, name=name, in_specs=[_HBM] * (2 * nt) + [_SEMS, _SEMS, _ANY], out_specs=[_HBM] * (2 * nt),
        out_shape=[pltpu.HBM(a.shape, a.dtype) for a in list(srcs) + list(lands)],
        input_output_aliases={i: i for i in range(2 * nt)},
        compiler_params=pltpu.CompilerParams(has_side_effects=_EFFECT))(*srcs, *lands, send_sems, recv_sems, after)
    return outs[nt:]


def _gather_small(packed):
    n = packed.shape[0]

    def body(in_ref, out_ref, send_sems, recv_sems, loc_sem):
        x, y, c = _mesh_pos()
        me = 4 * x + 2 * y + c
        lc = pltpu.make_async_copy(in_ref, out_ref.at[me], loc_sem)
        lc.start()
        peers = []
        for k in range(1, NDEV):
            px, py, pc = x ^ (k >> 2), y ^ ((k >> 1) & 1), c ^ (k & 1)
            rc = pltpu.make_async_remote_copy(src_ref=in_ref, dst_ref=out_ref.at[me], send_sem=send_sems.at[k - 1],
                                              recv_sem=recv_sems.at[k - 1], device_id=(px, py, pc), device_id_type=MESH)
            rc.start()
            peers.append((k, px, py, pc))
        lc.wait()
        for k, px, py, pc in peers:
            pltpu.make_async_remote_copy(src_ref=in_ref, dst_ref=out_ref.at[4 * px + 2 * py + pc],
                                         send_sem=send_sems.at[k - 1], recv_sem=recv_sems.at[k - 1],
                                         device_id=(px, py, pc), device_id_type=MESH).wait()

    scratch = [pltpu.SemaphoreType.DMA((NDEV - 1,)), pltpu.SemaphoreType.DMA((NDEV - 1,)), pltpu.SemaphoreType.DMA]
    return _comm_call(body, name="gather_small_grads", n_in=1, out_shape=[_sds((NDEV, n, 128))],
                      scratch=scratch)(packed)[0]


def _row_tile(r, c):
    t = r
    while t * c * 4 > (1 << 20) and t % 16 == 0:
        t //= 2
    return t


def _k_pair_add(full, got, name):
    g, r, c = full.shape
    hr = r // 2
    tr = _row_tile(hr, c)
    nh = hr // tr

    def body(a_ref, b_ref, o_ref):
        o_ref[...] = (a_ref[...] + b_ref[...]).astype(_WIRE)

    mine = pl.BlockSpec((None, tr, c), lambda i, j: (i, lax.axis_index("c") * nh + j, 0))
    spec = pl.BlockSpec((None, tr, c), lambda i, j: (i, j, 0))
    return _pc(body, name=name, grid=(g, nh), in_specs=[mine, spec], out_specs=[spec],
               out_shape=[_sds((g, hr, c), _WIRE)])(full, got)[0]


def _k_chip_sum(parts, slots, name):
    _, r, c = parts.shape
    tr = _row_tile(r, c)

    def body(a_ref, s_ref, o_ref):
        acc = a_ref[...].astype(F32)
        for k in range(3):
            acc = acc + s_ref[k].astype(F32)
        o_ref[...] = acc

    own = pl.BlockSpec((None, tr, c), lambda i: (2 * lax.axis_index("x") + lax.axis_index("y"), i, 0))
    return _pc(body, name=name, grid=(r // tr,), in_specs=[own, pl.BlockSpec((3, tr, c), lambda i: (0, i, 0))],
               out_specs=[_row(tr, c)], out_shape=[_sds((r, c))])(parts, slots)[0]


def _adam(w, g, m, v):
    m = ADAM_B1 * m + (1.0 - ADAM_B1) * g
    v = ADAM_B2 * v + (1.0 - ADAM_B2) * (g * g)
    m_hat = m / (1.0 - ADAM_B1 ** ADAM_STEP)
    v_hat = v / (1.0 - ADAM_B2 ** ADAM_STEP)
    return -ADAM_LR * (m_hat / (jnp.sqrt(v_hat) + ADAM_EPS) + ADAM_WD * w), m, v


def _k_adam(w, mine, theirs, m, v, name):
    r, c = w.shape
    hr = r // 2
    tr = _row_tile(hr, c)
    nh = hr // tr

    def body(w_ref, a_ref, b_ref, m_ref, v_ref, g_ref, d_ref, mo_ref, vo_ref):
        upper = (pl.program_id(0) >= nh).astype(jnp.int32)
        g = jnp.where(upper == lax.axis_index("c"), a_ref[...], b_ref[...])
        g_ref[...] = g
        d_ref[...], mo_ref[...], vo_ref[...] = _adam(w_ref[...], g, m_ref[...], v_ref[...])

    hspec = pl.BlockSpec((tr, c), lambda i: (jnp.where(i >= nh, i - nh, i), 0))
    return _pc(body, name=name, grid=(r // tr,), in_specs=[_row(tr, c), hspec, hspec, _row(tr, c), _row(tr, c)],
               out_specs=[_row(tr, c)] * 4, out_shape=[_sds((r, c))] * 4)(w, mine, theirs, m, v)


def _k_sum8(a):
    _, n, _ = a.shape

    def body(a_ref, o_ref):
        acc = a_ref[0]
        for k in range(1, NDEV):
            acc = acc + a_ref[k]
        o_ref[...] = acc

    return _pc(body, name="sum_small_grads", grid=(1,), in_specs=[_acc(a.shape)], out_specs=[_acc((n, 128))],
               out_shape=[_sds((n, 128))])(a)[0]


def _k_adam_small(w, g, m, v):
    n = w.shape[0]

    def body(w_ref, g_ref, m_ref, v_ref, d_ref, mo_ref, vo_ref):
        d_ref[...], mo_ref[...], vo_ref[...] = _adam(w_ref[...], g_ref[...], m_ref[...], v_ref[...])

    return _pc(body, name="adam_small", grid=(1,), in_specs=[_acc((n, 128))] * 4, out_specs=[_acc((n, 128))] * 3,
               out_shape=[_sds((n, 128))] * 3)(w, g, m, v)


def _pack(vals):
    rows = []
    for a in vals:
        flat = a.reshape(-1)
        n = -(-flat.shape[0] // 1024) * 1024
        rows.append(jnp.pad(flat, (0, n - flat.shape[0])).reshape(n // 128, 128))
    return jnp.concatenate(rows, axis=0)


def _unpack(packed, shapes):
    out, off = [], 0
    for sh in shapes:
        size = int(np.prod(sh))
        n = -(-size // 1024) * 1024
        out.append(packed[off // 128:(off + n) // 128].reshape(-1)[:size].reshape(sh))
        off += n
    return out


_WEIGHTS = ["attn_norm", "w_in", "a_q_norm", "a_k_norm", "b_q_norm", "b_k_norm", "b_sinks", "mem_norm", "w_mem_kv",
            "m_q_norm", "m_k_norm", "w_o_a", "w_o_b", "w_o_m", "w_gate", "b_gate", "w_out", "ffn_norm", "w_up",
            "conv_w", "conv_b", "w_down"]
_BIG = ["w_in", "w_mem_kv", "w_o_a", "w_o_b", "w_o_m", "w_gate", "w_out", "w_up", "w_down"]
_SMALL = [n for n in _WEIGHTS if n not in _BIG]


def kernel(x, mem, positions, attn_norm, w_in, a_q_norm, a_k_norm, b_q_norm, b_k_norm, b_sinks, mem_norm, w_mem_kv, m_q_norm, m_k_norm, w_o_a, w_o_b, w_o_m, w_gate, b_gate, w_out, ffn_norm, w_up, conv_w, conv_b, w_down, loss_target, m_attn_norm, m_w_in, m_a_q_norm, m_a_k_norm, m_b_q_norm, m_b_k_norm, m_b_sinks, m_mem_norm, m_w_mem_kv, m_m_q_norm, m_m_k_norm, m_w_o_a, m_w_o_b, m_w_o_m, m_w_gate, m_b_gate, m_w_out, m_ffn_norm, m_w_up, m_conv_w, m_conv_b, m_w_down, v_attn_norm, v_w_in, v_a_q_norm, v_a_k_norm, v_b_q_norm, v_b_k_norm, v_b_sinks, v_mem_norm, v_w_mem_kv, v_m_q_norm, v_m_k_norm, v_w_o_a, v_w_o_b, v_w_o_m, v_w_gate, v_b_gate, v_w_out, v_ffn_norm, v_w_up, v_conv_w, v_conv_b, v_w_down):
    given = dict(locals())
    w = {n: given[n][0] for n in _WEIGHTS}
    m1 = {n: given["m_" + n][0] for n in _WEIGHTS}
    m2 = {n: given["v_" + n][0] for n in _WEIGHTS}

    w_in = _gather_shards([w["w_in"].astype(_MM)])[0]
    rest = [n for n in _BIG if n != "w_in"] + ["conv_w"]
    shards = [w[n] if n == "conv_w" else w[n].astype(_MM) for n in rest]
    g_send, g_recv, g_srcs, g_lands, tok = _split_start(_bcast_copies, shards, [(CHIPS,) + a.shape for a in shards], 4,
                                                        w_in, "gather_rest_start")
    small = {n: (w[n][None, :] if w[n].ndim == 1 else w[n]) for n in _SMALL if n != "conv_w"}
    small["attn_norm"] = small["attn_norm"] + tok[0:1, 0:1]

    def get_rest(after):
        wts = dict(zip(rest, _split_wait(_bcast_copies, g_send, g_recv, g_srcs, g_lands, after, "gather_rest_wait")))
        for n in ("w_mem_kv", "w_out", "w_down"):
            wts[n] = wts[n].reshape(-1, wts[n].shape[-1])
        return wts

    parts, slots, pending = {}, {}, []

    def on_grads(group):
        names = list(group)
        tag = "_".join(names)
        got = _pair_split([group[n] for n in names], "grad_pair_split_" + tag)
        for n, b in zip(names, got):
            parts[n] = _k_pair_add(group[n], b, "pair_add_" + n)
        mine = [parts[n] for n in names]
        if "w_in" in group:
            slots.update(zip(names, _chip_scatter(mine, "grad_chip_scatter_" + tag)))
            return None
        shapes = [(3,) + p.shape[1:] for p in mine]
        send, recv, srcs, lands, token = _split_start(_scatter_copies, mine, shapes, 3, jnp.zeros((8, 128), F32),
                                                       "scatter_start_" + tag)
        pending.append((names, tag, send, recv, srcs, lands))
        return token

    loss, grad_x, sml = _local_step(x[0], mem[0], positions[0], loss_target[0], small, w_in, get_rest, on_grads)
    loss = lax.psum(loss, ("x", "y", "c"))
    for names, tag, send, recv, srcs, lands in pending:
        got = _split_wait(_scatter_copies, send, recv, srcs, lands, slots["w_in"], "scatter_wait_" + tag)
        slots.update(zip(names, got))
    mine = [_k_chip_sum(parts[n], slots[n], "chip_add_" + n) for n in _BIG]
    theirs = _pair_join(mine)
    grads = {}

    shapes = [sml[n].shape for n in _SMALL]
    gsm = dict(zip(_SMALL, _unpack(_k_sum8(_gather_small(_pack([sml[n] for n in _SMALL]))), shapes)))
    nu = w["conv_w"].shape[1]
    chip = 2 * lax.axis_index("x") + lax.axis_index("y")
    gsm["conv_w"] = lax.dynamic_slice_in_dim(gsm["conv_w"], chip * nu, nu, axis=1)
    for n in _SMALL:
        grads[n] = gsm[n].reshape(w[n].shape)

    delta, new_m, new_v = {}, {}, {}
    for n, a, b in zip(_BIG, mine, theirs):
        grads[n], delta[n], new_m[n], new_v[n] = _k_adam(w[n], a, b, m1[n], m2[n], "adam_" + n)
    pk = lambda d: _pack([d[n] for n in _SMALL])
    sshapes = [w[n].shape for n in _SMALL]
    for dst, packed in zip((delta, new_m, new_v), _k_adam_small(pk(w), pk(grads), pk(m1), pk(m2))):
        dst.update(zip(_SMALL, _unpack(packed, sshapes)))

    lead = lambda d: [d[n][None] for n in _WEIGHTS]
    return (loss, grad_x[None], *lead(grads), *lead(delta), *lead(new_m), *lead(new_v))
```

```python
import math

import jax
import jax.numpy as jnp
import numpy as np
from jax import lax
from jax.experimental import pallas as pl
from jax.experimental.pallas import tpu as pltpu

F32 = jnp.float32
_MM = jnp.bfloat16
_WIRE = jnp.bfloat16

D_MODEL = 1024
HEAD = 64
BLK = 128
A_GROUPS = ((128, 1), (512, 4), (2048, 16))
A_HEADS = 4
A_W = A_HEADS * HEAD
B_QH = 8
B_KVH = 2
B_WINDOW = 128
M_HEADS = 4
M_HD = 128
M_W = M_HEADS * M_HD
D_FF = 2816
EPS = 1e-6
NEG = -1e30
ROPE_THETA = 500000.0
ROPE_ROT = 16
CHIPS = 4
NDEV = 8
ADAM_LR, ADAM_B1, ADAM_B2, ADAM_EPS, ADAM_WD, ADAM_STEP = 0.001, 0.9, 0.999, 1e-08, 0.01, 10
VMEM_LIMIT = 58 * 1024 * 1024
MESH = pl.DeviceIdType.MESH


def _pc(body, *, name, grid, in_specs, out_specs, out_shape, scratch=()):
    return pl.pallas_call(
        body, name=name, grid=grid, in_specs=in_specs, out_specs=out_specs, out_shape=out_shape,
        scratch_shapes=list(scratch),
        compiler_params=pltpu.CompilerParams(dimension_semantics=("arbitrary",) * len(grid),
                                             vmem_limit_bytes=VMEM_LIMIT))


def _row(ts, c, col=0):
    return pl.BlockSpec((ts, c), lambda i: (i, col))


def _res(shape):
    n = len(shape)
    return pl.BlockSpec(tuple(shape), lambda i: (0,) * n, pipeline_mode=pl.Buffered(1))


def _acc(shape):
    n = len(shape)
    return pl.BlockSpec(tuple(shape), lambda i: (0,) * n)


def _sds(shape, dtype=F32):
    return jax.ShapeDtypeStruct(tuple(shape), dtype)


def _dot(a, b):
    return jnp.dot(a.astype(_MM), b.astype(_MM), preferred_element_type=F32)


def _dot_nt(a, b):
    return lax.dot_general(a.astype(_MM), b.astype(_MM), (((1,), (1,)), ((), ())), preferred_element_type=F32)


def _dot_tn(a, b):
    return lax.dot_general(a.astype(_MM), b.astype(_MM), (((0,), (0,)), ((), ())), preferred_element_type=F32)


def _sum8(v):
    ts, c = v.shape
    return jnp.sum(v.reshape(ts // 8, 8, c), axis=0)


def _sigmoid(z):
    return 1.0 / (1.0 + jnp.exp(-z))


def _rms(x):
    r = lax.rsqrt(jnp.mean(x * x, axis=-1, keepdims=True) + EPS)
    return x * r, r


def _rms_bwd(dy, xh, r, gain):
    z = dy * gain
    return r * (z - xh * jnp.mean(z * xh, axis=-1, keepdims=True))


def _split_hi_lo(v):
    hi = v.astype(_MM)
    return hi, (v - hi.astype(F32)).astype(_MM)


def _lane_head(shape):
    return lax.shift_right_logical(lax.broadcasted_iota(jnp.int32, shape, len(shape) - 1), 6)


def _seg_sum64(v):
    w = v.shape[1]
    e = jnp.where(_lane_head((w, w)) == lax.shift_right_logical(lax.broadcasted_iota(jnp.int32, (w, w), 0), 6),
                  1.0, 0.0).astype(_MM)
    hi, lo = _split_hi_lo(v)
    return jnp.dot(hi, e, preferred_element_type=F32) + jnp.dot(lo, e, preferred_element_type=F32)


def _seg_norm(x, seg):
    if seg == HEAD:
        r = lax.rsqrt(_seg_sum64(x * x) * (1.0 / HEAD) + EPS)
        return x * r, r
    w = x.shape[1]
    xh, rr = [], []
    for s in range(w // seg):
        xs = x[:, s * seg:(s + 1) * seg]
        r = lax.rsqrt(jnp.mean(xs * xs, axis=-1, keepdims=True) + EPS)
        xh.append(xs * r)
        rr.append(jnp.broadcast_to(r, xs.shape))
    return jnp.concatenate(xh, axis=1), jnp.concatenate(rr, axis=1)


def _seg_mean(v, seg):
    if seg == HEAD:
        return _seg_sum64(v) * (1.0 / HEAD)
    w = v.shape[1]
    out = []
    for s in range(w // seg):
        vs = v[:, s * seg:(s + 1) * seg]
        out.append(jnp.broadcast_to(jnp.mean(vs, axis=-1, keepdims=True), vs.shape))
    return jnp.concatenate(out, axis=1)


def _rope(t, c, sa, sb):
    out = []
    for cb in range(t.shape[1] // 128):
        tc = t[:, cb * 128:(cb + 1) * 128]
        out.append(tc * c + pltpu.roll(tc, 120, 1) * sa + pltpu.roll(tc, 8, 1) * sb)
    return jnp.concatenate(out, axis=1) if len(out) > 1 else out[0]


def _rope_bwd(dy, c, sa, sb):
    out = []
    for cb in range(dy.shape[1] // 128):
        dc = dy[:, cb * 128:(cb + 1) * 128]
        out.append(dc * c + pltpu.roll(dc * sa, 8, 1) + pltpu.roll(dc * sb, 120, 1))
    return jnp.concatenate(out, axis=1) if len(out) > 1 else out[0]


def _rope_freqs():
    c = np.float32(-2.0 * math.log(ROPE_THETA) / ROPE_ROT)
    return [float(v) for v in np.exp(np.arange(ROPE_ROT // 2, dtype=np.float32) * c)]


def _k_rope(pos2d):
    n = pos2d.shape[0]
    freqs = _rope_freqs()
    nf = len(freqs)

    def body(p_ref, c_ref, s_ref):
        p = p_ref[...].astype(F32)
        for f in range(nf):
            ang = p * freqs[f]
            c_ref[f] = jnp.cos(ang)
            s_ref[f] = jnp.sin(ang)

    return _pc(body, name="rope_tables", grid=(1,),
               in_specs=[_acc((n, 128))], out_specs=[_acc((nf, n, 128)), _acc((nf, n, 128))],
               out_shape=[_sds((nf, n, 128)), _sds((nf, n, 128))])(pos2d)


def _rope_tables(pos_rows):
    r = pos_rows.shape[0]
    cos, sin = _k_rope(pos_rows.reshape(r // 128, 128))
    half = ROPE_ROT // 2
    cos = cos.reshape(half, r).T
    sin = sin.reshape(half, r).T
    one = jnp.ones((r, HEAD - ROPE_ROT), F32)
    zero = jnp.zeros((r, HEAD - ROPE_ROT), F32)
    z8 = jnp.zeros((r, half), F32)
    c64 = jnp.concatenate([cos, cos, one], axis=1)
    sa64 = jnp.concatenate([-sin, z8, zero], axis=1)
    sb64 = jnp.concatenate([z8, sin, zero], axis=1)
    return tuple(jnp.concatenate([t, t], axis=1) for t in (c64, sa64, sb64))


def _k_in(x, g1, w_in):
    s = x.shape[0]
    ts = min(256, s)
    nin = w_in.shape[2]
    ncol = CHIPS * nin
    a_cols = 3 * A_W
    offs = [0, a_cols, 2 * a_cols, 3 * a_cols, 3 * a_cols + B_QH * HEAD,
            3 * a_cols + (B_QH + B_KVH) * HEAD, 3 * a_cols + (B_QH + 2 * B_KVH) * HEAD, ncol]

    def body(x_ref, g_ref, wi_ref, h_ref, a0, a1, a2, qb, kb, vb, mq, p_scr):
        xh, _ = _rms(x_ref[...])
        h = (xh * g_ref[...]).astype(_MM)
        h_ref[...] = h
        for j in range(CHIPS):
            p_scr[:, j * nin:(j + 1) * nin] = jnp.dot(h, wi_ref[j], preferred_element_type=F32)
        for k, ref in enumerate((a0, a1, a2, qb, kb, vb, mq)):
            ref[...] = p_scr[:, offs[k]:offs[k + 1]]

    widths = [offs[k + 1] - offs[k] for k in range(7)]
    return _pc(
        body, name="in_proj", grid=(s // ts,),
        in_specs=[_row(ts, D_MODEL), _res((1, D_MODEL)), _res(w_in.shape)],
        out_specs=[_row(ts, D_MODEL)] + [_row(ts, w) for w in widths],
        out_shape=[_sds((s, D_MODEL), _MM)] + [_sds((s, w)) for w in widths],
        scratch=[pltpu.VMEM((ts, ncol), F32)])(x, g1, w_in)


def _k_gate(h, w_gate, b_gate):
    s = h.shape[0]
    ts = min(256, s)
    ng = w_gate.shape[2]

    def body(h_ref, wg_ref, bg_ref, gt_ref):
        h = h_ref[...]
        for j in range(CHIPS):
            z = jnp.dot(h, wg_ref[j], preferred_element_type=F32) + bg_ref[:, j * ng:(j + 1) * ng]
            gt_ref[:, j * ng:(j + 1) * ng] = _sigmoid(z)

    return _pc(body, name="gate_proj", grid=(s // ts,),
               in_specs=[_row(ts, D_MODEL), _res(w_gate.shape), _res(b_gate.shape)],
               out_specs=[_row(ts, CHIPS * ng)], out_shape=[_sds((s, CHIPS * ng))])(h, w_gate, b_gate)[0]


def _k_prep(srcs, gq, gk, tabs, *, wq, wk, rows_per_gain, name):
    rows = srcs[0][0].shape[0]
    ts = min(256, rows)

    def body(q_ref, k_ref, v_ref, gq_ref, gk_ref, c_ref, sa_ref, sb_ref, qn_ref, kn_ref, vn_ref):
        c, sa, sb = c_ref[...], sa_ref[...], sb_ref[...]
        qh, _ = _seg_norm(q_ref[...], HEAD)
        qn_ref[...] = _rope(qh * gq_ref[...], c, sa, sb).astype(_MM)
        kh, _ = _seg_norm(k_ref[...], HEAD)
        kn_ref[...] = _rope(kh * gk_ref[...], c, sa, sb).astype(_MM)
        vn_ref[...] = v_ref[...].astype(_MM)

    gspec = lambda w: pl.BlockSpec((None, 1, w), lambda i: ((i * ts) // rows_per_gain, 0, 0))
    return _pc(
        body, name=name, grid=(rows // ts,),
        in_specs=[_row(ts, wq, srcs[0][1]), _row(ts, wk, srcs[1][1]), _row(ts, wk, srcs[2][1]),
                  gspec(wq), gspec(wk)] + [_row(ts, 128)] * 3,
        out_specs=[_row(ts, wq), _row(ts, wk), _row(ts, wk)],
        out_shape=[_sds((rows, wq), _MM), _sds((rows, wk), _MM), _sds((rows, wk), _MM)])(
            srcs[0][0], srcs[1][0], srcs[2][0], gq, gk, *tabs)


def _first_flag(b, segs, nb):
    first = b >= nb
    for k, (start, period) in enumerate(segs):
        end = segs[k + 1][0] if k + 1 < len(segs) else nb
        first = first | ((b >= start) & (b < end) & (lax.rem(b - start, jnp.int32(period)) == 0))
    return first


def _band_bias(thr, with_cur):
    qi = lax.broadcasted_iota(jnp.int32, (BLK, BLK), 0)
    kj = lax.broadcasted_iota(jnp.int32, (BLK, BLK), 1)
    prev = jnp.where(kj >= qi + thr, 0.0, NEG)
    return jnp.concatenate([prev, jnp.where(kj <= qi, 0.0, NEG)], axis=1) if with_cur else prev


def _blockdiag(t4):
    head = _lane_head((1, A_W))
    return jnp.concatenate([t4 * jnp.where(head == h, 1.0, 0.0).astype(t4.dtype) for h in range(A_HEADS)], axis=0)


def _fold_diag(t, n):
    head = _lane_head((n, A_W))
    out = t[3 * n:4 * n]
    for h in (2, 1, 0):
        out = jnp.where(head == h, t[h * n:(h + 1) * n], out)
    return out


def _expand_heads(cols):
    n = cols[0].shape[0]
    head = _lane_head((n, A_W))
    out = jnp.broadcast_to(cols[3], (n, A_W))
    for h in (2, 1, 0):
        out = jnp.where(head == h, cols[h], out)
    return out


def _unit_kv(p_ref, c_ref, u, shared):
    if not shared:
        return jnp.concatenate([p_ref[:, u * A_W:(u + 1) * A_W], c_ref[:, u * A_W:(u + 1) * A_W]], axis=0)
    kg = jnp.concatenate([p_ref[:, u * HEAD:(u + 1) * HEAD], c_ref[:, u * HEAD:(u + 1) * HEAD]], axis=0)
    return jnp.concatenate([kg] * A_HEADS, axis=1)


def _k_band_fwd(qn, kn, vn, *, hq, hk, max_dist, segs, sink, name):
    rows = qn.shape[0]
    nb = rows // BLK
    units = hq // A_HEADS
    shared = hk != hq
    wq, wk = hq * HEAD, hk * HEAD
    scale = HEAD ** -0.5

    def body(*refs):
        if sink is None:
            q_ref, kc_ref, kp_ref, vc_ref, vp_ref, o_ref, l_ref = refs
        else:
            q_ref, kc_ref, kp_ref, vc_ref, vp_ref, sk_ref, o_ref, l_ref = refs
        b = pl.program_id(0)
        bias = _band_bias(jnp.where(_first_flag(b, segs, nb), 1 << 20, BLK - max_dist), True)
        for u in range(units):
            us = slice(u * A_W, (u + 1) * A_W)
            kb = _blockdiag(_unit_kv(kp_ref, kc_ref, u, shared))
            vb = _blockdiag(_unit_kv(vp_ref, vc_ref, u, shared))
            s_all = _dot_nt(q_ref[:, us], kb) * scale
            ps, ls = [], []
            for h in range(A_HEADS):
                s = s_all[:, h * 2 * BLK:(h + 1) * 2 * BLK] + bias
                m = jnp.max(s, axis=-1, keepdims=True)
                e = jnp.exp(s - m)
                lse = m + jnp.log(jnp.sum(e, axis=-1, keepdims=True))
                if sink is not None:
                    sk = sk_ref[u * A_HEADS + h]
                    mx = jnp.maximum(lse, sk)
                    lse = mx + jnp.log(jnp.exp(lse - mx) + jnp.exp(sk - mx))
                ps.append((e * jnp.exp(m - lse)).astype(_MM))
                ls.append(lse)
            o_ref[:, us] = _dot(jnp.concatenate(ps, axis=1), vb)
            l_ref[:, us] = _expand_heads(ls)

    cur = lambda w: pl.BlockSpec((BLK, w), lambda i: (i, 0))
    prev = lambda w: pl.BlockSpec((BLK, w), lambda i: (jnp.maximum(i - 1, 0), 0))
    in_specs = [cur(wq), cur(wk), prev(wk), cur(wk), prev(wk)]
    args = [qn, kn, kn, vn, vn]
    if sink is not None:
        in_specs.append(pl.BlockSpec(memory_space=pltpu.SMEM))
        args.append(sink)
    return _pc(body, name=name, grid=(nb,), in_specs=in_specs, out_specs=[cur(wq), cur(wq)],
               out_shape=[_sds((rows, wq)), _sds((rows, wq))])(*args)


def _k_memkv(mem, mem_norm, w_kv, m_k_norm):
    n = mem.shape[0]

    def body(m_ref, g_ref, w_ref, gk_ref, mn_ref, kv_ref, mk_ref, mv_ref):
        mh, _ = _rms(m_ref[...])
        mn = (mh * g_ref[...]).astype(_MM)
        mn_ref[...] = mn
        kv = jnp.dot(mn, w_ref[...], preferred_element_type=F32)
        kv_ref[...] = kv
        kh, _ = _seg_norm(kv[:, :M_W], M_HD)
        mk_ref[...] = (kh * gk_ref[...]).astype(_MM)
        mv_ref[...] = kv[:, M_W:].astype(_MM)

    return _pc(body, name="mem_kv", grid=(1,),
               in_specs=[_acc((n, D_MODEL)), _acc((1, D_MODEL)), _acc(w_kv.shape), _acc((1, M_W))],
               out_specs=[_acc((n, D_MODEL)), _acc((n, 2 * M_W)), _acc((n, M_W)), _acc((n, M_W))],
               out_shape=[_sds((n, D_MODEL), _MM), _sds((n, 2 * M_W)), _sds((n, M_W), _MM), _sds((n, M_W), _MM)])(
                   mem, mem_norm, w_kv, m_k_norm)


def _mem_probs(q, mk):
    sc = _dot_nt(q, mk) * (M_HD ** -0.5)
    e = jnp.exp(sc - jnp.max(sc, axis=-1, keepdims=True))
    return e / jnp.sum(e, axis=-1, keepdims=True)


def _k_mem_fwd(m_q, gq, mk, mv):
    s = m_q.shape[0]
    n = mk.shape[0]
    ts = min(256, s)

    def body(q_ref, g_ref, mk_ref, mv_ref, o_ref):
        qh, _ = _seg_norm(q_ref[...], M_HD)
        qn = (qh * g_ref[...]).astype(_MM)
        for h in range(M_HEADS):
            hs = slice(h * M_HD, (h + 1) * M_HD)
            o_ref[:, hs] = _dot(_mem_probs(qn[:, hs], mk_ref[:, hs]), mv_ref[:, hs])

    return _pc(body, name="mem_attn", grid=(s // ts,),
               in_specs=[_row(ts, M_W), _res((1, M_W)), _res((n, M_W)), _res((n, M_W))],
               out_specs=[_row(ts, M_W)], out_shape=[_sds((s, M_W))])(m_q, gq, mk, mv)[0]


def _group_weights(l0, l1, l2):
    m = jnp.maximum(jnp.maximum(l0, l1), l2)
    e0, e1, e2 = jnp.exp(l0 - m), jnp.exp(l1 - m), jnp.exp(l2 - m)
    inv = 1.0 / (e0 + e1 + e2)
    return e0 * inv, e1 * inv, e2 * inv


def _branch_products(oa, ob, om, woa_ref, wob_ref, wom_ref, j):
    return _dot(oa, woa_ref[j]), _dot(ob, wob_ref[j]), _dot(om, wom_ref[j])


def _k_merge(og, lg, o_b, o_m, gates, x, w_oa, w_ob, w_om, w_out, g2):
    s = x.shape[0]
    ts = min(256, s)
    nc = w_oa.shape[2]

    def body(o0, o1, o2, l0, l1, l2, ob_ref, om_ref, gt_ref, x_ref, woa, wob, wom, wout, g_ref,
             oa_ref, mer_ref, x1_ref, h2_ref, m_scr):
        w0, w1, w2 = _group_weights(l0[...], l1[...], l2[...])
        oa = w0 * o0[...] + w1 * o1[...] + w2 * o2[...]
        oa_ref[...] = oa
        ob, om = ob_ref[...], om_ref[...]
        for j in range(CHIPS):
            pa, pb, pm = _branch_products(oa, ob, om, woa, wob, wom, j)
            cs = lambda br: slice(br * D_MODEL + j * nc, br * D_MODEL + (j + 1) * nc)
            m_scr[:, j * nc:(j + 1) * nc] = gt_ref[:, cs(0)] * pa + gt_ref[:, cs(1)] * pb + gt_ref[:, cs(2)] * pm
        mer = m_scr[...].astype(_MM)
        mer_ref[...] = mer
        x1 = x_ref[...] + jnp.dot(mer, wout[...], preferred_element_type=F32)
        x1_ref[...] = x1
        xh, _ = _rms(x1)
        h2_ref[...] = (xh * g_ref[...]).astype(_MM)

    return _pc(
        body, name="merge_out", grid=(s // ts,),
        in_specs=[_row(ts, A_W)] * 6 + [_row(ts, B_QH * HEAD), _row(ts, M_W), _row(ts, 3 * D_MODEL), _row(ts, D_MODEL),
                                         _res(w_oa.shape), _res(w_ob.shape), _res(w_om.shape), _res(w_out.shape),
                                         _res((1, D_MODEL))],
        out_specs=[_row(ts, A_W), _row(ts, D_MODEL), _row(ts, D_MODEL), _row(ts, D_MODEL)],
        out_shape=[_sds((s, A_W)), _sds((s, D_MODEL), _MM), _sds((s, D_MODEL)), _sds((s, D_MODEL), _MM)],
        scratch=[pltpu.VMEM((ts, D_MODEL), F32)])(*og, *lg, o_b, o_m, gates, x, w_oa, w_ob, w_om, w_out, g2)


def _k_up(h2, w_up):
    s = h2.shape[0]
    ts = min(256, s)
    nu = w_up.shape[2]

    def body(h_ref, w_ref, u_ref):
        h = h_ref[...]
        for j in range(CHIPS):
            u_ref[:, j * nu:(j + 1) * nu] = jnp.dot(h, w_ref[j], preferred_element_type=F32)

    return _pc(body, name="up_proj", grid=(s // ts,), in_specs=[_row(ts, D_MODEL), _res(w_up.shape)],
               out_specs=[_row(ts, CHIPS * nu)], out_shape=[_sds((s, CHIPS * nu))])(h2, w_up)[0]


def _shift_down(v, halo, k):
    ts = v.shape[0]
    row = lax.broadcasted_iota(jnp.int32, v.shape, 0)
    out = pltpu.roll(v, k, 0)
    for r in range(k):
        out = jnp.where(row == r, halo[8 - k + r:8 - k + r + 1, :], out)
    return out


def _shift_up(v, halo, k):
    ts = v.shape[0]
    row = lax.broadcasted_iota(jnp.int32, v.shape, 0)
    out = pltpu.roll(v, ts - k, 0)
    for r in range(k):
        out = jnp.where(row == ts - k + r, halo[r:r + 1, :], out)
    return out


def _k_ffn(u, conv_w, conv_b, w_down, x1, target):
    s = u.shape[0]
    ts = min(128, s)
    nu = conv_w.shape[2]
    half = CHIPS // 2

    def body(u_ref, uh_ref, cw_ref, cb_ref, wd_ref, x1_ref, t_ref, dy_ref, f_ref, dc_ref, loss_ref, c_scr, f_scr):
        i = pl.program_id(0)
        halo = jnp.where(i > 0, uh_ref[...], 0.0)
        for j in range(CHIPS):
            cs = slice(j * nu, (j + 1) * nu)
            uj = u_ref[:, cs]
            hj = halo[:, cs]
            c_scr[:, cs] = (cb_ref[:, cs] + cw_ref[j, 0:1, :] * _shift_down(uj, hj, 2)
                            + cw_ref[j, 1:2, :] * _shift_down(uj, hj, 1) + cw_ref[j, 2:3, :] * uj)
        for j in range(half):
            a = c_scr[:, j * nu:(j + 1) * nu]
            g = c_scr[:, (half + j) * nu:(half + j + 1) * nu]
            f_scr[:, j * nu:(j + 1) * nu] = (a * _sigmoid(a) * g).astype(_MM)
        f = f_scr[...]
        f_ref[...] = f
        y = x1_ref[...] + jnp.dot(f, wd_ref[...], preferred_element_type=F32)
        err = y - t_ref[...]
        dy = err * (1.0 / D_MODEL)
        dy_ref[...] = dy

        @pl.when(i == 0)
        def _():
            loss_ref[...] = jnp.zeros_like(loss_ref)

        loss_ref[...] += _sum8(err * err)
        df = _dot_nt(dy, wd_ref[...])
        for j in range(half):
            a = c_scr[:, j * nu:(j + 1) * nu]
            g = c_scr[:, (half + j) * nu:(half + j + 1) * nu]
            sa = _sigmoid(a)
            dfj = df[:, j * nu:(j + 1) * nu]
            dc_ref[:, j * nu:(j + 1) * nu] = dfj * g * (sa * (1.0 + a * (1.0 - sa)))
            dc_ref[:, (half + j) * nu:(half + j + 1) * nu] = dfj * (a * sa)

    wide = CHIPS * nu
    return _pc(
        body, name="conv_ffn", grid=(s // ts,),
        in_specs=[_row(ts, wide), pl.BlockSpec((8, wide), lambda i: (jnp.maximum(i * (ts // 8) - 1, 0), 0)),
                  _res(conv_w.shape), _res((1, wide)), _res(w_down.shape), _row(ts, D_MODEL), _row(ts, D_MODEL)],
        out_specs=[_row(ts, D_MODEL), _row(ts, D_FF), _row(ts, wide), _acc((8, D_MODEL))],
        out_shape=[_sds((s, D_MODEL)), _sds((s, D_FF), _MM), _sds((s, wide)), _sds((8, D_MODEL))],
        scratch=[pltpu.VMEM((ts, wide), F32), pltpu.VMEM((ts, D_FF), _MM)])(u, u, conv_w, conv_b, w_down, x1, target)


def _k_conv_bwd(dc, u, conv_w, w_up, x1, g2, dy):
    s = u.shape[0]
    ts = min(128, s)
    nu = conv_w.shape[2]
    wide = CHIPS * nu
    last = s // ts - 1

    def body(dc_ref, dn_ref, u_ref, uh_ref, cw_ref, wu_ref, x1_ref, g_ref, dy_ref,
             dx1_ref, du_ref, cacc_ref, gacc_ref):
        i = pl.program_id(0)

        @pl.when(i == 0)
        def _():
            cacc_ref[...] = jnp.zeros_like(cacc_ref)
            gacc_ref[...] = jnp.zeros_like(gacc_ref)

        uhalo = jnp.where(i > 0, uh_ref[...], 0.0)
        dhalo = jnp.where(i < last, dn_ref[...], 0.0)
        dh2 = jnp.zeros((ts, D_MODEL), F32)
        for j in range(CHIPS):
            cs = slice(j * nu, (j + 1) * nu)
            dcj, uj = dc_ref[:, cs], u_ref[:, cs]
            cacc_ref[0, :, cs] += _sum8(dcj)
            cacc_ref[1, :, cs] += _sum8(dcj * _shift_down(uj, uhalo[:, cs], 2))
            cacc_ref[2, :, cs] += _sum8(dcj * _shift_down(uj, uhalo[:, cs], 1))
            cacc_ref[3, :, cs] += _sum8(dcj * uj)
            du = (cw_ref[j, 2:3, :] * dcj + cw_ref[j, 1:2, :] * _shift_up(dcj, dhalo[:, cs], 1)
                  + cw_ref[j, 0:1, :] * _shift_up(dcj, dhalo[:, cs], 2)).astype(_MM)
            du_ref[:, cs] = du
            dh2 = dh2 + _dot_nt(du, wu_ref[j])
        xh, r = _rms(x1_ref[...])
        gacc_ref[...] += _sum8(dh2 * xh)
        dx1_ref[...] = dy_ref[...] + _rms_bwd(dh2, xh, r, g_ref[...])

    return _pc(
        body, name="conv_up_bwd", grid=(s // ts,),
        in_specs=[_row(ts, wide),
                  pl.BlockSpec((8, wide), lambda i: (jnp.minimum((i + 1) * (ts // 8), s // 8 - 1), 0)),
                  _row(ts, wide), pl.BlockSpec((8, wide), lambda i: (jnp.maximum(i * (ts // 8) - 1, 0), 0)),
                  _res(conv_w.shape), _res(w_up.shape), _row(ts, D_MODEL), _res((1, D_MODEL)), _row(ts, D_MODEL)],
        out_specs=[_row(ts, D_MODEL), _row(ts, wide), _acc((4, 8, wide)), _acc((8, D_MODEL))],
        out_shape=[_sds((s, D_MODEL)), _sds((s, wide), _MM), _sds((4, 8, wide)), _sds((8, D_MODEL))])(
            dc, dc, u, u, conv_w, w_up, x1, g2, dy)


def _k_merge_bwd(dx1, og, lg, o_a, o_b, o_m, gates, w_oa, w_ob, w_om, w_out, dep):
    s = dx1.shape[0]
    ts = min(256, s)
    nc = w_oa.shape[2]

    def body(dx_ref, o0, o1, o2, l0, l1, l2, oa_ref, ob_ref, om_ref, gt_ref, woa, wob, wom, wout, dep_ref,
             dgp_ref, dpa_ref, dpb_ref, dpm_ref, dog0, dog1, dog2, dl0, dl1, dl2, dob_ref, dom_ref, bacc_ref):
        i = pl.program_id(0)

        @pl.when(i == 0)
        def _():
            bacc_ref[...] = jnp.zeros_like(bacc_ref)

        dmer = _dot_nt(dx_ref[...], wout[...])
        oa, ob, om = oa_ref[...], ob_ref[...], om_ref[...]
        doa = jnp.zeros((ts, A_W), F32)
        dob = jnp.zeros((ts, B_QH * HEAD), F32)
        dom = jnp.zeros((ts, M_W), F32)
        for j in range(CHIPS):
            prods = _branch_products(oa, ob, om, woa, wob, wom, j)
            dmj = dmer[:, j * nc:(j + 1) * nc]
            dps = []
            for br, (p, dref) in enumerate(zip(prods, (dpa_ref, dpb_ref, dpm_ref))):
                cs = slice(br * D_MODEL + j * nc, br * D_MODEL + (j + 1) * nc)
                gt = gt_ref[:, cs]
                dgp = dmj * p * gt * (1.0 - gt)
                dgp_ref[:, cs] = dgp.astype(_MM)
                bacc_ref[:, cs] += _sum8(dgp)
                dp = (dmj * gt).astype(_MM)
                dref[:, j * nc:(j + 1) * nc] = dp
                dps.append(dp)
            doa = doa + _dot_nt(dps[0], woa[j])
            dob = dob + _dot_nt(dps[1], wob[j])
            dom = dom + _dot_nt(dps[2], wom[j])
        dob_ref[...] = dob
        dom_ref[...] = dom
        ws = _group_weights(l0[...], l1[...], l2[...])
        dsum = _seg_mean(doa * oa, HEAD) * float(HEAD)
        for w, dref, lref in zip(ws, (dog0, dog1, dog2), (dl0, dl1, dl2)):
            dref[...] = w * doa
            lref[...] = w * dsum

    return _pc(
        body, name="merge_out_bwd", grid=(s // ts,),
        in_specs=[_row(ts, D_MODEL)] + [_row(ts, A_W)] * 7 + [_row(ts, B_QH * HEAD), _row(ts, M_W), _row(ts, 3 * D_MODEL),
                                                              _res(w_oa.shape), _res(w_ob.shape), _res(w_om.shape),
                                                              _res(w_out.shape), _res((8, 128))],
        out_specs=[_row(ts, 3 * D_MODEL)] + [_row(ts, D_MODEL)] * 3 + [_row(ts, A_W)] * 6
        + [_row(ts, B_QH * HEAD), _row(ts, M_W), _acc((8, 3 * D_MODEL))],
        out_shape=[_sds((s, 3 * D_MODEL), _MM)] + [_sds((s, D_MODEL), _MM)] * 3 + [_sds((s, A_W))] * 6
        + [_sds((s, B_QH * HEAD)), _sds((s, M_W)), _sds((8, 3 * D_MODEL))])(
            dx1, *og, *lg, o_a, o_b, o_m, gates, w_oa, w_ob, w_om, w_out, dep)


def _k_mem_bwd(m_q, gq, mk, mv, o_m, do_m):
    s = m_q.shape[0]
    n = mk.shape[0]
    ts = min(256, s)
    scale = M_HD ** -0.5

    def body(q_ref, g_ref, mk_ref, mv_ref, o_ref, do_ref, dq_ref, dmk_ref, dmv_ref, gacc_ref):
        i = pl.program_id(0)

        @pl.when(i == 0)
        def _():
            dmk_ref[...] = jnp.zeros_like(dmk_ref)
            dmv_ref[...] = jnp.zeros_like(dmv_ref)
            gacc_ref[...] = jnp.zeros_like(gacc_ref)

        gain = g_ref[...]
        qh, r = _seg_norm(q_ref[...], M_HD)
        qn = (qh * gain).astype(_MM)
        do = do_ref[...]
        delta = _seg_mean(do * o_ref[...], M_HD) * float(M_HD)
        dqn = []
        for h in range(M_HEADS):
            hs = slice(h * M_HD, (h + 1) * M_HD)
            p = _mem_probs(qn[:, hs], mk_ref[:, hs])
            dp = _dot_nt(do[:, hs], mv_ref[:, hs])
            ds = (p * (dp - delta[:, hs][:, 0:1]) * scale).astype(_MM)
            dqn.append(_dot(ds, mk_ref[:, hs]))
            dmk_ref[:, hs] += _dot_tn(ds, qn[:, hs])
            dmv_ref[:, hs] += _dot_tn(p, do[:, hs])
        dqn = jnp.concatenate(dqn, axis=1)
        gacc_ref[...] += _sum8(dqn * qh)
        z = dqn * gain
        dq_ref[...] = (r * (z - qh * _seg_mean(z * qh, M_HD))).astype(_MM)

    return _pc(
        body, name="mem_attn_bwd", grid=(s // ts,),
        in_specs=[_row(ts, M_W), _res((1, M_W)), _res((n, M_W)), _res((n, M_W)), _row(ts, M_W), _row(ts, M_W)],
        out_specs=[_row(ts, M_W), _acc((n, M_W)), _acc((n, M_W)), _acc((8, M_W))],
        out_shape=[_sds((s, M_W), _MM), _sds((n, M_W)), _sds((n, M_W)), _sds((8, M_W))])(m_q, gq, mk, mv, o_m, do_m)


def _k_memkv_bwd(mem, mem_norm, w_kv, m_k_norm, mem_n, kv, dmk, dmv):
    n = mem.shape[0]

    def body(m_ref, g_ref, w_ref, gk_ref, mn_ref, kv_ref, dmk_ref, dmv_ref, dw_ref, dg_ref, dgk_ref):
        gk = gk_ref[...]
        kh, r = _seg_norm(kv_ref[:, :M_W], M_HD)
        dmk = dmk_ref[...]
        dgk_ref[...] = _sum8(dmk * kh)
        z = dmk * gk
        dk = r * (z - kh * _seg_mean(z * kh, M_HD))
        dkv = jnp.concatenate([dk, dmv_ref[...]], axis=1).astype(_MM)
        dw_ref[...] = _dot_tn(mn_ref[...], dkv)
        dmn = _dot_nt(dkv, w_ref[...])
        mh, _ = _rms(m_ref[...])
        dg_ref[...] = _sum8(dmn * mh)

    return _pc(body, name="mem_kv_bwd", grid=(1,),
               in_specs=[_acc((n, D_MODEL)), _acc((1, D_MODEL)), _acc(w_kv.shape), _acc((1, M_W)), _acc((n, D_MODEL)),
                         _acc((n, 2 * M_W)), _acc((n, M_W)), _acc((n, M_W))],
               out_specs=[_acc(w_kv.shape), _acc((8, D_MODEL)), _acc((8, M_W))],
               out_shape=[_sds(w_kv.shape), _sds((8, D_MODEL)), _sds((8, M_W))])(
                   mem, mem_norm, w_kv, m_k_norm, mem_n, kv, dmk, dmv)


def _k_band_bwd(qn, kn, vn, do, lse, dl_or_o, *, hq, hk, max_dist, segs, sink, name):
    rows = qn.shape[0]
    nb = rows // BLK
    units = hq // A_HEADS
    shared = hk != hq
    wq, wk = hq * HEAD, hk * HEAD
    scale = HEAD ** -0.5

    def body(*refs):
        (qb_ref, qx_ref, kb_ref, kp_ref, vb_ref, vp_ref, dob_ref, dox_ref, lb_ref, lx_ref, eb_ref, ex_ref) = refs[:12]
        if sink is None:
            dq_ref, dk_ref, dv_ref = refs[12:]
        else:
            sk_ref, dq_ref, dk_ref, dv_ref, sacc_ref = refs[12:]
        b = pl.program_id(0)
        bias1 = _band_bias(jnp.where(_first_flag(b, segs, nb), 1 << 20, BLK - max_dist), True)
        bias2 = _band_bias(jnp.where(_first_flag(b + 1, segs, nb), 1 << 20, BLK - max_dist), False)
        if sink is not None:
            @pl.when(b == 0)
            def _():
                sacc_ref[...] = jnp.zeros_like(sacc_ref)

        for u in range(units):
            us = slice(u * A_W, (u + 1) * A_W)
            q4, qx4, do4, dox4 = qb_ref[:, us], qx_ref[:, us], dob_ref[:, us], dox_ref[:, us]
            k4, v4 = _unit_kv(kp_ref, kb_ref, u, shared), _unit_kv(vp_ref, vb_ref, u, shared)
            kd, vd = _blockdiag(k4), _blockdiag(v4)
            kdc, vdc = _blockdiag(k4[BLK:]), _blockdiag(v4[BLK:])
            if sink is None:
                dlt_b, dlt_x = eb_ref[:, us], ex_ref[:, us]
            else:
                dlt_b = _seg_sum64(do4.astype(F32) * eb_ref[:, us])
                dlt_x = _seg_sum64(dox4.astype(F32) * ex_ref[:, us])
            s1, dp1 = _dot_nt(q4, kd) * scale, _dot_nt(do4, vd)
            s2, dp2 = _dot_nt(qx4, kdc) * scale, _dot_nt(dox4, vdc)
            ds1, ds1c, p1c, ds2, p2 = [], [], [], [], []
            for h in range(A_HEADS):
                col = slice(u * A_W + h * HEAD, u * A_W + h * HEAD + 1)
                ucol = slice(h * HEAD, h * HEAD + 1)
                wide, narrow = slice(h * 2 * BLK, (h + 1) * 2 * BLK), slice(h * BLK, (h + 1) * BLK)
                l_b, l_x = lb_ref[:, col], lx_ref[:, col]
                p = jnp.exp(s1[:, wide] + bias1 - l_b)
                ds = p * (dp1[:, wide] - dlt_b[:, ucol]) * scale
                ds1.append(ds.astype(_MM))
                ds1c.append(ds[:, BLK:].astype(_MM))
                p1c.append(p[:, BLK:].astype(_MM))
                px = jnp.exp(s2[:, narrow] + bias2 - l_x)
                ds2.append((px * (dp2[:, narrow] - dlt_x[:, ucol]) * scale).astype(_MM))
                p2.append(px.astype(_MM))
                if sink is not None:
                    j = u * A_HEADS + h
                    sacc_ref[:, j:j + 1] += -jnp.exp(sk_ref[j] - l_b) * dlt_b[:, ucol]
            dq_ref[:, us] = _dot(jnp.concatenate(ds1, axis=1), kd)
            dk4 = _fold_diag(_dot_tn(jnp.concatenate(ds1c, axis=1), q4) + _dot_tn(jnp.concatenate(ds2, axis=1), qx4), BLK)
            dv4 = _fold_diag(_dot_tn(jnp.concatenate(p1c, axis=1), do4) + _dot_tn(jnp.concatenate(p2, axis=1), dox4), BLK)
            if shared:
                fold = lambda t: (t[:, 0:HEAD] + t[:, HEAD:2 * HEAD]) + (t[:, 2 * HEAD:3 * HEAD] + t[:, 3 * HEAD:])
                dk_ref[:, u * HEAD:(u + 1) * HEAD] = fold(dk4)
                dv_ref[:, u * HEAD:(u + 1) * HEAD] = fold(dv4).astype(_MM)
            else:
                dk_ref[:, us] = dk4
                dv_ref[:, us] = dv4.astype(_MM)

    cur = lambda w: pl.BlockSpec((BLK, w), lambda i: (i, 0))
    prev = lambda w: pl.BlockSpec((BLK, w), lambda i: (jnp.maximum(i - 1, 0), 0))
    nxt = lambda w: pl.BlockSpec((BLK, w), lambda i: (jnp.minimum(i + 1, nb - 1), 0))
    in_specs = [cur(wq), nxt(wq), cur(wk), prev(wk), cur(wk), prev(wk), cur(wq), nxt(wq), cur(wq), nxt(wq), cur(wq), nxt(wq)]
    args = [qn, qn, kn, kn, vn, vn, do, do, lse, lse, dl_or_o, dl_or_o]
    out_specs = [cur(wq), cur(wk), cur(wk)]
    out_shape = [_sds((rows, wq)), _sds((rows, wk)), _sds((rows, wk), _MM)]
    if sink is not None:
        in_specs.append(pl.BlockSpec(memory_space=pltpu.SMEM))
        args.append(sink)
        out_specs.append(_acc((BLK, 128)))
        out_shape.append(_sds((BLK, 128)))
    return _pc(body, name=name, grid=(nb,), in_specs=in_specs, out_specs=out_specs, out_shape=out_shape)(*args)


def _k_prep_bwd(srcs, dqn, dkn, gq, gk, tabs, *, wq, wk, rows_per_gain, name):
    rows = dqn.shape[0]
    ts = min(256, rows)
    ngain = gq.shape[0]

    def body(q_ref, k_ref, dq_ref, dk_ref, gq_ref, gk_ref, c_ref, sa_ref, sb_ref, oq_ref, ok_ref, aq_ref, ak_ref):
        i = pl.program_id(0)

        @pl.when(lax.rem(i * ts, rows_per_gain) == 0)
        def _():
            aq_ref[...] = jnp.zeros_like(aq_ref)
            ak_ref[...] = jnp.zeros_like(ak_ref)

        c, sa, sb = c_ref[...], sa_ref[...], sb_ref[...]
        for x_ref, d_ref, g_ref, o_ref, a_ref in ((q_ref, dq_ref, gq_ref, oq_ref, aq_ref),
                                                   (k_ref, dk_ref, gk_ref, ok_ref, ak_ref)):
            xh, r = _seg_norm(x_ref[...], HEAD)
            dt = _rope_bwd(d_ref[...], c, sa, sb)
            a_ref[...] += _sum8(dt * xh)
            z = dt * g_ref[...]
            o_ref[...] = (r * (z - xh * _seg_mean(z * xh, HEAD))).astype(_MM)

    gspec = lambda w: pl.BlockSpec((None, 1, w), lambda i: ((i * ts) // rows_per_gain, 0, 0))
    aspec = lambda w: pl.BlockSpec((None, 8, w), lambda i: ((i * ts) // rows_per_gain, 0, 0))
    return _pc(
        body, name=name, grid=(rows // ts,),
        in_specs=[_row(ts, wq, srcs[0][1]), _row(ts, wk, srcs[1][1]), _row(ts, wq), _row(ts, wk), gspec(wq), gspec(wk)]
        + [_row(ts, 128)] * 3,
        out_specs=[_row(ts, wq), _row(ts, wk), aspec(wq), aspec(wk)],
        out_shape=[_sds((rows, wq), _MM), _sds((rows, wk), _MM), _sds((ngain, 8, wq)), _sds((ngain, 8, wk))])(
            srcs[0][0], srcs[1][0], dqn, dkn, gq, gk, *tabs)


def _k_in_bwd(pieces, dgp, x, g1, dx1, w_in, w_gate):
    s = x.shape[0]
    ts = min(256, s)
    nin, ng = w_in.shape[2], w_gate.shape[2]
    widths = [p.shape[1] for p in pieces]
    ncol = sum(widths)

    def body(*refs):
        p_refs = refs[:len(pieces)]
        dgp_ref, x_ref, g_ref, dx1_ref, wi_ref, wg_ref, gx_ref, dpj_ref, gacc_ref = refs[len(pieces):]
        i = pl.program_id(0)

        @pl.when(i == 0)
        def _():
            gacc_ref[...] = jnp.zeros_like(gacc_ref)

        off = 0
        for p_ref, w in zip(p_refs, widths):
            dpj_ref[:, off:off + w] = p_ref[...]
            off += w
        dh = jnp.zeros((ts, D_MODEL), F32)
        for j in range(CHIPS):
            dh = dh + _dot_nt(dpj_ref[:, j * nin:(j + 1) * nin], wi_ref[j])
            dh = dh + _dot_nt(dgp_ref[:, j * ng:(j + 1) * ng], wg_ref[j])
        xh, r = _rms(x_ref[...])
        gacc_ref[...] += _sum8(dh * xh)
        gx_ref[...] = dx1_ref[...] + _rms_bwd(dh, xh, r, g_ref[...])

    return _pc(
        body, name="in_proj_bwd", grid=(s // ts,),
        in_specs=[_row(ts, w) for w in widths] + [_row(ts, CHIPS * ng), _row(ts, D_MODEL), _res((1, D_MODEL)),
                                                  _row(ts, D_MODEL), _res(w_in.shape), _res(w_gate.shape)],
        out_specs=[_row(ts, D_MODEL), _row(ts, ncol), _acc((8, D_MODEL))],
        out_shape=[_sds((s, D_MODEL)), _sds((s, ncol), _MM), _sds((8, D_MODEL))])(*pieces, dgp, x, g1, dx1, w_in, w_gate)


def _k_wgrad(a, b, *, nblk, stacked, name):
    s, k = a.shape
    n = b.shape[1]
    nb = n // nblk
    ts = min(1024, s)

    def body(a_ref, b_ref, o_ref):
        @pl.when(pl.program_id(1) == 0)
        def _():
            o_ref[...] = jnp.zeros_like(o_ref)

        o_ref[...] += _dot_tn(a_ref[...], b_ref[...])

    if stacked:
        out_spec, out_shape = pl.BlockSpec((None, k, nb), lambda g, t: (g, 0, 0)), _sds((nblk, k, nb))
    else:
        out_spec, out_shape = pl.BlockSpec((k, nb), lambda g, t: (0, g)), _sds((k, n))
    return _pc(body, name=name, grid=(nblk, s // ts),
               in_specs=[pl.BlockSpec((ts, k), lambda g, t: (t, 0)), pl.BlockSpec((ts, nb), lambda g, t: (t, g))],
               out_specs=[out_spec], out_shape=[out_shape])(a, b)[0]


def _to_res(t, d):
    s, c = t.shape
    return t if d == 1 else t.reshape(s // d, d, c).transpose(1, 0, 2).reshape(s, c)


def _from_res(t, d):
    s, c = t.shape
    return t if d == 1 else t.reshape(d, s // d, c).transpose(1, 0, 2).reshape(s, c)


def _tile_gain(g, heads):
    return jnp.tile(g, (1,) * (g.ndim - 1) + (heads,))[..., None, :]


def _local_step(x, mem, pos, target, small, w_in, get_rest, on_grads):
    s = x.shape[0]
    nblk = s // BLK
    g1, g2 = small["attn_norm"], small["ffn_norm"]

    pos_rows = jnp.concatenate([_to_res(pos[:, None], d)[:, 0] for _, d in A_GROUPS] + [pos])
    tabs = _rope_tables(pos_rows)
    tabs_a = tuple(t[:3 * s] for t in tabs)
    tabs_b = tuple(t[3 * s:] for t in tabs)

    h, qa0, qa1, qa2, q_b, k_b, v_b, m_q = _k_in(x, g1, w_in)

    qkv_a = jnp.concatenate([_to_res(t, d) for t, (_, d) in zip((qa0, qa1, qa2), A_GROUPS)], axis=0)
    gq_a = _tile_gain(small["a_q_norm"], A_HEADS)
    gk_a = _tile_gain(small["a_k_norm"], A_HEADS)
    src_a = ((qkv_a, 0), (qkv_a, 1), (qkv_a, 2))
    qn_a, kn_a, vn_a = _k_prep(src_a, gq_a, gk_a, tabs_a, wq=A_W, wk=A_W, rows_per_gain=s, name="prep_a")
    segs_a = tuple((gi * nblk, nblk // d) for gi, (_, d) in enumerate(A_GROUPS))
    o_res, l_res = _k_band_fwd(qn_a, kn_a, vn_a, hq=A_HEADS, hk=A_HEADS, max_dist=BLK, segs=segs_a, sink=None,
                               name="attn_a")
    og = [_from_res(o_res[gi * s:(gi + 1) * s], d) for gi, (_, d) in enumerate(A_GROUPS)]
    lg = [_from_res(l_res[gi * s:(gi + 1) * s], d) for gi, (_, d) in enumerate(A_GROUPS)]

    gq_b = _tile_gain(small["b_q_norm"], B_QH)
    gk_b = _tile_gain(small["b_k_norm"], B_KVH)
    src_b = ((q_b, 0), (k_b, 0), (v_b, 0))
    qn_b, kn_b, vn_b = _k_prep(src_b, gq_b, gk_b, tabs_b, wq=B_QH * HEAD, wk=B_KVH * HEAD, rows_per_gain=s,
                               name="prep_b")
    sink_x = small["b_sinks"][0]
    segs_b = ((0, nblk),)
    o_b, l_b = _k_band_fwd(qn_b, kn_b, vn_b, hq=B_QH, hk=B_KVH, max_dist=B_WINDOW - 1, segs=segs_b, sink=sink_x,
                           name="attn_b")

    wts = get_rest(o_b)
    gates = _k_gate(h, wts["w_gate"], small["b_gate"])

    gq_m = _tile_gain(small["m_q_norm"], M_HEADS)[0]
    gk_m = _tile_gain(small["m_k_norm"], M_HEADS)[0]
    mem_n, kv, mk, mv = _k_memkv(mem, small["mem_norm"], wts["w_mem_kv"], gk_m)
    o_m = _k_mem_fwd(m_q, gq_m, mk, mv)

    o_a, merged, x1, h2 = _k_merge(og, lg, o_b, o_m, gates, x, wts["w_o_a"], wts["w_o_b"], wts["w_o_m"],
                                   wts["w_out"], g2)
    u = _k_up(h2, wts["w_up"])
    dy, f, dc, loss_acc = _k_ffn(u, wts["conv_w"], small["conv_b"], wts["w_down"], x1, target)
    loss = (0.5 / D_MODEL) * jnp.sum(loss_acc)

    dx1, du, cacc, g2acc = _k_conv_bwd(dc, u, wts["conv_w"], wts["w_up"], x1, g2, dy)
    tok = on_grads({"w_up": _k_wgrad(h2, du, nblk=CHIPS, stacked=True, name="dw_up"),
                    "w_down": _k_wgrad(f, dy, nblk=2, stacked=False, name="dw_down").reshape(CHIPS, -1, D_MODEL)}, dx1)
    (dgp, dp_a, dp_b, dp_m, dog0, dog1, dog2, dl0, dl1, dl2, do_b, do_m, bacc) = _k_merge_bwd(
        dx1, og, lg, o_a, o_b, o_m, gates, wts["w_o_a"], wts["w_o_b"], wts["w_o_m"], wts["w_out"], tok)
    tok = on_grads({"w_gate": _k_wgrad(h, dgp, nblk=CHIPS, stacked=True, name="dw_gate"),
                    "w_o_a": _k_wgrad(o_a, dp_a, nblk=CHIPS, stacked=True, name="dw_o_a"),
                    "w_o_b": _k_wgrad(o_b, dp_b, nblk=CHIPS, stacked=True, name="dw_o_b"),
                    "w_o_m": _k_wgrad(o_m, dp_m, nblk=CHIPS, stacked=True, name="dw_o_m"),
                    "w_out": _k_wgrad(merged, dx1, nblk=1, stacked=False, name="dw_out").reshape(CHIPS, -1, D_MODEL)},
                   do_m)

    dq_m, dmk, dmv, gqm_acc = _k_mem_bwd(m_q, gq_m + tok[0:1, 0:1], mk, mv, o_m, do_m)
    dw_kv, gmem_acc, gkm_acc = _k_memkv_bwd(mem, small["mem_norm"], wts["w_mem_kv"], gk_m, mem_n, kv, dmk, dmv)

    dq_bn, dk_bn, dv_b, sacc = _k_band_bwd(qn_b, kn_b, vn_b, do_b, l_b, o_b, hq=B_QH, hk=B_KVH,
                                           max_dist=B_WINDOW - 1, segs=segs_b, sink=sink_x, name="attn_b_bwd")
    tok = on_grads({}, dq_bn)
    dq_b, dk_b, gqb_acc, gkb_acc = _k_prep_bwd(src_b, dq_bn, dk_bn, gq_b + tok[0:1, 0:1], gk_b, tabs_b, wq=B_QH * HEAD,
                                               wk=B_KVH * HEAD, rows_per_gain=s, name="prep_b_bwd")

    do_res = jnp.concatenate([_to_res(t, d) for t, (_, d) in zip((dog0, dog1, dog2), A_GROUPS)], axis=0)
    dl_res = jnp.concatenate([_to_res(t, d) for t, (_, d) in zip((dl0, dl1, dl2), A_GROUPS)], axis=0)
    dq_an, dk_an, dv_a = _k_band_bwd(qn_a, kn_a, vn_a, do_res, l_res, dl_res, hq=A_HEADS, hk=A_HEADS, max_dist=BLK,
                                     segs=segs_a, sink=None, name="attn_a_bwd")
    dq_a, dk_a, gqa_acc, gka_acc = _k_prep_bwd(src_a, dq_an, dk_an, gq_a, gk_a, tabs_a, wq=A_W, wk=A_W,
                                               rows_per_gain=s, name="prep_a_bwd")
    pieces = []
    for gi, (_, d) in enumerate(A_GROUPS):
        rs = slice(gi * s, (gi + 1) * s)
        pieces += [_from_res(t[rs], d) for t in (dq_a, dk_a, dv_a)]
    pieces += [dq_b, dk_b, dv_b, dq_m]
    grad_x, dproj, g1acc = _k_in_bwd(pieces, dgp, x, g1, dx1, w_in, wts["w_gate"])
    on_grads({"w_in": _k_wgrad(h, dproj, nblk=CHIPS, stacked=True, name="dw_in"),
              "w_mem_kv": dw_kv.reshape(CHIPS, -1, 2 * M_W)}, grad_x)

    def fold(acc, heads):
        v = jnp.sum(acc, axis=-2)
        return jnp.sum(v.reshape(v.shape[:-1] + (heads, -1)), axis=-2)

    csum = jnp.sum(cacc, axis=1)
    sml = {
        "attn_norm": jnp.sum(g1acc, axis=0), "a_q_norm": fold(gqa_acc, A_HEADS), "a_k_norm": fold(gka_acc, A_HEADS),
        "b_q_norm": fold(gqb_acc[0], B_QH), "b_k_norm": fold(gkb_acc[0], B_KVH),
        "b_sinks": jnp.sum(sacc, axis=0)[:B_QH], "mem_norm": jnp.sum(gmem_acc, axis=0),
        "m_q_norm": fold(gqm_acc, M_HEADS), "m_k_norm": fold(gkm_acc, M_HEADS),
        "b_gate": jnp.sum(bacc, axis=0), "ffn_norm": jnp.sum(g2acc, axis=0),
        "conv_w": csum[1:], "conv_b": csum[0],
    }
    return loss, grad_x, sml


def _mesh_pos():
    return lax.axis_index("x"), lax.axis_index("y"), lax.axis_index("c")


def _chip_peers(x, y):
    return [(1 - x, y), (x, 1 - y), (1 - x, 1 - y)]


_ANY = pl.BlockSpec(memory_space=pl.ANY)


def _comm_call(body, *, name, n_in, out_shape, scratch):
    return pl.pallas_call(body, name=name, in_specs=[_ANY] * n_in, out_specs=[_ANY] * len(out_shape),
                          out_shape=out_shape, scratch_shapes=scratch)


def _remote(src, dst, send_sem, recv_sem, dev):
    return pltpu.make_async_remote_copy(src_ref=src, dst_ref=dst, send_sem=send_sem, recv_sem=recv_sem,
                                        device_id=dev, device_id_type=MESH)


def _gather_shards(shards):
    nt = len(shards)
    split = [sh.shape[0] % 16 == 0 for sh in shards]

    def body(*refs):
        ins, outs = refs[:nt], refs[nt:2 * nt]
        ici_s, ici_r, fwd_s, fwd_r, own_s, own_r = refs[2 * nt:]
        x, y, c = _mesh_pos()
        me = 2 * x + y
        sib = (x, y, 1 - c)
        peers = _chip_peers(x, y)

        def half(ref, t, who):
            if not split[t]:
                return ref
            hr = shards[t].shape[0] // 2
            return ref.at[pl.ds(pl.multiple_of(who * hr, 8), hr), :]

        pending = []
        for t in range(nt):
            own = _remote(ins[t], outs[t].at[me], own_s.at[t], own_r.at[t], sib)
            own.start()
            pending.append(own.wait)
            for k, (px, py) in enumerate(peers):
                rc = _remote(half(ins[t], t, c), half(outs[t].at[me], t, c), ici_s.at[t, k], ici_r.at[t, k], (px, py, c))
                rc.start()
                pending.append(rc.wait_send)
        for t in range(nt):
            for k, (px, py) in enumerate(peers):
                land = half(outs[t].at[2 * px + py], t, c)
                _remote(land, land, ici_s.at[t, k], ici_r.at[t, k], (px, py, c)).wait_recv()
                if split[t]:
                    fw = _remote(land, land, fwd_s.at[t, k], fwd_r.at[t, k], sib)
                    fw.start()
                    pending.append(fw.wait_send)
                    other = half(outs[t].at[2 * px + py], t, 1 - c)
                    pending.append(_remote(other, other, fwd_s.at[t, k], fwd_r.at[t, k], sib).wait_recv)
        for wait in pending:
            wait()

    out_shape = [_sds((CHIPS,) + sh.shape, sh.dtype) for sh in shards]
    dma = pltpu.SemaphoreType.DMA
    scratch = [dma((nt, 3)), dma((nt, 3)), dma((nt, 3)), dma((nt, 3)), dma((nt,)), dma((nt,))]
    return _comm_call(body, name="gather_weights", n_in=nt, out_shape=out_shape, scratch=scratch)(*shards)


def _pair_split(grads, name):
    nt = len(grads)

    def body(*refs):
        ins, got = refs[:nt], refs[nt:2 * nt]
        send_sems, recv_sems = refs[2 * nt:]
        x, y, c = _mesh_pos()
        cps = []
        for t in range(nt):
            hr = ins[t].shape[1] // 2
            give = ins[t].at[:, pl.ds(pl.multiple_of((1 - c) * hr, 8), hr), :]
            rc = _remote(give, got[t], send_sems.at[t], recv_sems.at[t], (x, y, 1 - c))
            rc.start()
            cps.append(rc)
        for rc in cps:
            rc.wait()

    half = [_sds((CHIPS, g.shape[1] // 2, g.shape[2]), g.dtype) for g in grads]
    scratch = [pltpu.SemaphoreType.DMA((nt,)), pltpu.SemaphoreType.DMA((nt,))]
    return _comm_call(body, name=name, n_in=nt, out_shape=half, scratch=scratch)(*grads)


def _chip_scatter(parts, name):
    nt = len(parts)

    def body(*refs):
        ins, outs = refs[:nt], refs[nt:2 * nt]
        send_sems, recv_sems = refs[2 * nt:]
        x, y, c = _mesh_pos()
        cps = []
        for t in range(nt):
            for k, (px, py) in enumerate(_chip_peers(x, y)):
                rc = _remote(ins[t].at[2 * px + py], outs[t].at[k], send_sems.at[t, k], recv_sems.at[t, k], (px, py, c))
                rc.start()
                cps.append(rc)
        for cp in cps:
            cp.wait()

    out_shape = [_sds((3,) + p.shape[1:], p.dtype) for p in parts]
    scratch = [pltpu.SemaphoreType.DMA((nt, 3)), pltpu.SemaphoreType.DMA((nt, 3))]
    return _comm_call(body, name=name, n_in=nt, out_shape=out_shape, scratch=scratch)(*parts)


def _pair_join(halves):
    nt = len(halves)

    def body(*refs):
        ins, got = refs[:nt], refs[nt:2 * nt]
        send_sems, recv_sems = refs[2 * nt:]
        x, y, c = _mesh_pos()
        cps = []
        for t in range(nt):
            rc = _remote(ins[t], got[t], send_sems.at[t], recv_sems.at[t], (x, y, 1 - c))
            rc.start()
            cps.append(rc)
        for rc in cps:
            rc.wait()

    out_shape = [_sds(hf.shape, hf.dtype) for hf in halves]
    scratch = [pltpu.SemaphoreType.DMA((nt,)), pltpu.SemaphoreType.DMA((nt,))]
    return _comm_call(body, name="grad_pair_join", n_in=nt, out_shape=out_shape, scratch=scratch)(*halves)


_HBM = pl.BlockSpec(memory_space=pltpu.HBM)
_SEMS = pl.BlockSpec(memory_space=pltpu.SEMAPHORE)
_EFFECT = pltpu.SideEffectType.DATAFLOW_SIDE_EFFECTING


def _bcast_copies(ins, lands, send_sems, recv_sems):
    x, y, c = _mesh_pos()
    me = 2 * x + y
    targets = [((px, py, c), 2 * px + py) for px, py in _chip_peers(x, y)] + [((x, y, 1 - c), me)]
    out = []
    for t in range(len(ins)):
        for k, (dev, idx) in enumerate(targets):
            i = t * len(targets) + k
            arrival = lambda t=t, i=i, idx=idx, dev=dev: _remote(ins[t], lands[t].at[idx], send_sems.at[i],
                                                                 recv_sems.at[i], dev)
            out.append((_remote(ins[t], lands[t].at[me], send_sems.at[i], recv_sems.at[i], dev), arrival))
    return out


def _scatter_copies(ins, lands, send_sems, recv_sems):
    x, y, c = _mesh_pos()
    out = []
    for t in range(len(ins)):
        for k, (px, py) in enumerate(_chip_peers(x, y)):
            i = t * 3 + k
            cp = _remote(ins[t].at[2 * px + py], lands[t].at[k], send_sems.at[i], recv_sems.at[i], (px, py, c))
            out.append((cp, lambda cp=cp: cp))
    return out


def _pair_copies(ins, lands, send_sems, recv_sems):
    x, y, c = _mesh_pos()
    out = []
    for t in range(len(ins)):
        hr = ins[t].shape[1] // 2
        give = ins[t].at[:, pl.ds(pl.multiple_of((1 - c) * hr, 8), hr), :]
        cp = _remote(give, lands[t], send_sems.at[t], recv_sems.at[t], (x, y, 1 - c))
        out.append((cp, lambda cp=cp: cp))
    return out


def _split_start(copies, srcs, land_shapes, ncopy, dep, name):
    nt = len(srcs)

    def body(*refs):
        ins, lands = refs[:nt], refs[nt:2 * nt]
        send_sems, recv_sems, token = refs[2 * nt + 1], refs[2 * nt + 2], refs[-1]
        for send, _ in copies(ins, lands, send_sems, recv_sems):
            send.start()
        token[...] = jnp.zeros_like(token)

    lands = [pltpu.with_memory_space_constraint(lax.empty(sh, a.dtype), pltpu.HBM) for sh, a in zip(land_shapes, srcs)]
    srcs = [pltpu.with_memory_space_constraint(a, pltpu.HBM) for a in srcs]
    dma = pltpu.SemaphoreType.DMA
    out_shape = ([dma((nt * ncopy,)), dma((nt * ncopy,))] + [pltpu.HBM(a.shape, a.dtype) for a in srcs + lands]
                 + [_sds((8, 128))])
    outs = pl.pallas_call(
        body, name=name, in_specs=[_HBM] * (2 * nt) + [_ANY],
        out_specs=[_SEMS, _SEMS] + [_HBM] * (2 * nt) + [pl.BlockSpec(memory_space=pltpu.VMEM)], out_shape=out_shape,
        input_output_aliases={i: 2 + i for i in range(2 * nt)},
        compiler_params=pltpu.CompilerParams(has_side_effects=_EFFECT))(*srcs, *lands, dep)
    return outs[0], outs[1], outs[2:2 + nt], outs[2 + nt:2 + 2 * nt], outs[-1]


def _split_wait(copies, send_sems, recv_sems, srcs, lands, after, name):
    nt = len(srcs)

    def body(*refs):
        ins, lnd = refs[:nt], refs[nt:2 * nt]
        for send, arrival in copies(ins, lnd, refs[2 * nt], refs[2 * nt + 1]):
            send.wait_send()
            arrival().wait_recv()

    outs = pl.pallas_call(
        body, name=name, in_specs=[_HBM] * (2 * nt) + [_SEMS, _SEMS, _ANY], out_specs=[_HBM] * (2 * nt),
        out_shape=[pltpu.HBM(a.shape, a.dtype) for a in list(srcs) + list(lands)],
        input_output_aliases={i: i for i in range(2 * nt)},
        compiler_params=pltpu.CompilerParams(has_side_effects=_EFFECT))(*srcs, *lands, send_sems, recv_sems, after)
    return outs[:nt], outs[nt:]


def _gather_small(packed):
    n = packed.shape[0]

    def body(in_ref, out_ref, send_sems, recv_sems, loc_sem):
        x, y, c = _mesh_pos()
        me = 4 * x + 2 * y + c
        lc = pltpu.make_async_copy(in_ref, out_ref.at[me], loc_sem)
        lc.start()
        peers = []
        for k in range(1, NDEV):
            px, py, pc = x ^ (k >> 2), y ^ ((k >> 1) & 1), c ^ (k & 1)
            rc = pltpu.make_async_remote_copy(src_ref=in_ref, dst_ref=out_ref.at[me], send_sem=send_sems.at[k - 1],
                                              recv_sem=recv_sems.at[k - 1], device_id=(px, py, pc), device_id_type=MESH)
            rc.start()
            peers.append((k, px, py, pc))
        lc.wait()
        for k, px, py, pc in peers:
            pltpu.make_async_remote_copy(src_ref=in_ref, dst_ref=out_ref.at[4 * px + 2 * py + pc],
                                         send_sem=send_sems.at[k - 1], recv_sem=recv_sems.at[k - 1],
                                         device_id=(px, py, pc), device_id_type=MESH).wait()

    scratch = [pltpu.SemaphoreType.DMA((NDEV - 1,)), pltpu.SemaphoreType.DMA((NDEV - 1,)), pltpu.SemaphoreType.DMA]
    return _comm_call(body, name="gather_small_grads", n_in=1, out_shape=[_sds((NDEV, n, 128))],
                      scratch=scratch)(packed)[0]


def _row_tile(r, c):
    t = r
    while t * c * 4 > (1 << 20) and t % 16 == 0:
        t //= 2
    return t


def _k_pair_add(full, got, name):
    g, r, c = full.shape
    hr = r // 2
    tr = _row_tile(hr, c)
    nh = hr // tr

    def body(a_ref, b_ref, o_ref):
        o_ref[...] = (a_ref[...] + b_ref[...]).astype(_WIRE)

    mine = pl.BlockSpec((None, tr, c), lambda i, j: (i, lax.axis_index("c") * nh + j, 0))
    spec = pl.BlockSpec((None, tr, c), lambda i, j: (i, j, 0))
    return _pc(body, name=name, grid=(g, nh), in_specs=[mine, spec], out_specs=[spec],
               out_shape=[_sds((g, hr, c), _WIRE)])(full, got)[0]


def _k_chip_sum(parts, slots, name):
    _, r, c = parts.shape
    tr = _row_tile(r, c)

    def body(a_ref, s_ref, o_ref):
        acc = a_ref[...].astype(F32)
        for k in range(3):
            acc = acc + s_ref[k].astype(F32)
        o_ref[...] = acc

    own = pl.BlockSpec((None, tr, c), lambda i: (2 * lax.axis_index("x") + lax.axis_index("y"), i, 0))
    return _pc(body, name=name, grid=(r // tr,), in_specs=[own, pl.BlockSpec((3, tr, c), lambda i: (0, i, 0))],
               out_specs=[_row(tr, c)], out_shape=[_sds((r, c))])(parts, slots)[0]


def _adam(w, g, m, v):
    m = ADAM_B1 * m + (1.0 - ADAM_B1) * g
    v = ADAM_B2 * v + (1.0 - ADAM_B2) * (g * g)
    m_hat = m / (1.0 - ADAM_B1 ** ADAM_STEP)
    v_hat = v / (1.0 - ADAM_B2 ** ADAM_STEP)
    return -ADAM_LR * (m_hat / (jnp.sqrt(v_hat) + ADAM_EPS) + ADAM_WD * w), m, v


def _k_adam(w, mine, theirs, m, v, name):
    r, c = w.shape
    hr = r // 2
    tr = _row_tile(hr, c)
    nh = hr // tr

    def body(w_ref, a_ref, b_ref, m_ref, v_ref, g_ref, d_ref, mo_ref, vo_ref):
        upper = (pl.program_id(0) >= nh).astype(jnp.int32)
        g = jnp.where(upper == lax.axis_index("c"), a_ref[...], b_ref[...])
        g_ref[...] = g
        d_ref[...], mo_ref[...], vo_ref[...] = _adam(w_ref[...], g, m_ref[...], v_ref[...])

    hspec = pl.BlockSpec((tr, c), lambda i: (jnp.where(i >= nh, i - nh, i), 0))
    return _pc(body, name=name, grid=(r // tr,), in_specs=[_row(tr, c), hspec, hspec, _row(tr, c), _row(tr, c)],
               out_specs=[_row(tr, c)] * 4, out_shape=[_sds((r, c))] * 4)(w, mine, theirs, m, v)


def _k_sum8(a):
    _, n, _ = a.shape

    def body(a_ref, o_ref):
        acc = a_ref[0]
        for k in range(1, NDEV):
            acc = acc + a_ref[k]
        o_ref[...] = acc

    return _pc(body, name="sum_small_grads", grid=(1,), in_specs=[_acc(a.shape)], out_specs=[_acc((n, 128))],
               out_shape=[_sds((n, 128))])(a)[0]


def _k_adam_small(w, g, m, v):
    n = w.shape[0]

    def body(w_ref, g_ref, m_ref, v_ref, d_ref, mo_ref, vo_ref):
        d_ref[...], mo_ref[...], vo_ref[...] = _adam(w_ref[...], g_ref[...], m_ref[...], v_ref[...])

    return _pc(body, name="adam_small", grid=(1,), in_specs=[_acc((n, 128))] * 4, out_specs=[_acc((n, 128))] * 3,
               out_shape=[_sds((n, 128))] * 3)(w, g, m, v)


def _pack(vals):
    rows = []
    for a in vals:
        flat = a.reshape(-1)
        n = -(-flat.shape[0] // 1024) * 1024
        rows.append(jnp.pad(flat, (0, n - flat.shape[0])).reshape(n // 128, 128))
    return jnp.concatenate(rows, axis=0)


def _unpack(packed, shapes):
    out, off = [], 0
    for sh in shapes:
        size = int(np.prod(sh))
        n = -(-size // 1024) * 1024
        out.append(packed[off // 128:(off + n) // 128].reshape(-1)[:size].reshape(sh))
        off += n
    return out


_WEIGHTS = ["attn_norm", "w_in", "a_q_norm", "a_k_norm", "b_q_norm", "b_k_norm", "b_sinks", "mem_norm", "w_mem_kv",
            "m_q_norm", "m_k_norm", "w_o_a", "w_o_b", "w_o_m", "w_gate", "b_gate", "w_out", "ffn_norm", "w_up",
            "conv_w", "conv_b", "w_down"]
_BIG = ["w_in", "w_mem_kv", "w_o_a", "w_o_b", "w_o_m", "w_gate", "w_out", "w_up", "w_down"]
_SMALL = [n for n in _WEIGHTS if n not in _BIG]


def kernel(x, mem, positions, attn_norm, w_in, a_q_norm, a_k_norm, b_q_norm, b_k_norm, b_sinks, mem_norm, w_mem_kv, m_q_norm, m_k_norm, w_o_a, w_o_b, w_o_m, w_gate, b_gate, w_out, ffn_norm, w_up, conv_w, conv_b, w_down, loss_target, m_attn_norm, m_w_in, m_a_q_norm, m_a_k_norm, m_b_q_norm, m_b_k_norm, m_b_sinks, m_mem_norm, m_w_mem_kv, m_m_q_norm, m_m_k_norm, m_w_o_a, m_w_o_b, m_w_o_m, m_w_gate, m_b_gate, m_w_out, m_ffn_norm, m_w_up, m_conv_w, m_conv_b, m_w_down, v_attn_norm, v_w_in, v_a_q_norm, v_a_k_norm, v_b_q_norm, v_b_k_norm, v_b_sinks, v_mem_norm, v_w_mem_kv, v_m_q_norm, v_m_k_norm, v_w_o_a, v_w_o_b, v_w_o_m, v_w_gate, v_b_gate, v_w_out, v_ffn_norm, v_w_up, v_conv_w, v_conv_b, v_w_down):
    given = dict(locals())
    w = {n: given[n][0] for n in _WEIGHTS}
    m1 = {n: given["m_" + n][0] for n in _WEIGHTS}
    m2 = {n: given["v_" + n][0] for n in _WEIGHTS}

    w_in = _gather_shards([w["w_in"].astype(_MM)])[0]
    rest = [n for n in _BIG if n != "w_in"] + ["conv_w"]
    shards = [w[n] if n == "conv_w" else w[n].astype(_MM) for n in rest]
    g_send, g_recv, g_srcs, g_lands, tok = _split_start(_bcast_copies, shards, [(CHIPS,) + a.shape for a in shards], 4,
                                                        w_in, "gather_rest_start")
    small = {n: (w[n][None, :] if w[n].ndim == 1 else w[n]) for n in _SMALL if n != "conv_w"}
    small["attn_norm"] = small["attn_norm"] + tok[0:1, 0:1]

    def get_rest(after):
        wts = dict(zip(rest, _split_wait(_bcast_copies, g_send, g_recv, g_srcs, g_lands, after, "gather_rest_wait")[1]))
        for n in ("w_mem_kv", "w_out", "w_down"):
            wts[n] = wts[n].reshape(-1, wts[n].shape[-1])
        return wts

    parts, slots, pair, scat = {}, {}, [], []
    zeros = jnp.zeros((8, 128), F32)

    def finish_pair(after):
        names, tag, send, recv, srcs, lands = pair.pop()
        full, got = _split_wait(_pair_copies, send, recv, srcs, lands, after, "pair_wait_" + tag)
        mine = [_k_pair_add(f, b, "pair_add_" + n) for n, f, b in zip(names, full, got)]
        shapes = [(3,) + p.shape[1:] for p in mine]
        send, recv, srcs, lands, token = _split_start(_scatter_copies, mine, shapes, 3, zeros, "scatter_start_" + tag)
        scat.append((names, tag, send, recv, srcs, lands))
        return token

    def on_grads(group, after):
        names = list(group)
        tag = "_".join(names)
        token = finish_pair(after) if pair else zeros
        if not group:
            return token
        grads_g = [group[n] for n in names]
        if "w_in" in group:
            got = _pair_split(grads_g, "grad_pair_split_" + tag)
            for n, f, b in zip(names, grads_g, got):
                parts[n] = _k_pair_add(f, b, "pair_add_" + n)
            slots.update(zip(names, _chip_scatter([parts[n] for n in names], "grad_chip_scatter_" + tag)))
            return token
        shapes = [(CHIPS, g.shape[1] // 2, g.shape[2]) for g in grads_g]
        send, recv, srcs, lands, token = _split_start(_pair_copies, grads_g, shapes, 1, token, "pair_start_" + tag)
        pair.append((names, tag, send, recv, srcs, lands))
        return token

    loss, grad_x, sml = _local_step(x[0], mem[0], positions[0], loss_target[0], small, w_in, get_rest, on_grads)
    loss = lax.psum(loss, ("x", "y", "c"))
    for names, tag, send, recv, srcs, lands in scat:
        mine, got = _split_wait(_scatter_copies, send, recv, srcs, lands, slots["w_in"], "scatter_wait_" + tag)
        parts.update(zip(names, mine))
        slots.update(zip(names, got))
    mine = [_k_chip_sum(parts[n], slots[n], "chip_add_" + n) for n in _BIG]
    theirs = _pair_join(mine)
    grads = {}

    shapes = [sml[n].shape for n in _SMALL]
    gsm = dict(zip(_SMALL, _unpack(_k_sum8(_gather_small(_pack([sml[n] for n in _SMALL]))), shapes)))
    nu = w["conv_w"].shape[1]
    chip = 2 * lax.axis_index("x") + lax.axis_index("y")
    gsm["conv_w"] = lax.dynamic_slice_in_dim(gsm["conv_w"], chip * nu, nu, axis=1)
    for n in _SMALL:
        grads[n] = gsm[n].reshape(w[n].shape)

    delta, new_m, new_v = {}, {}, {}
    for n, a, b in zip(_BIG, mine, theirs):
        grads[n], delta[n], new_m[n], new_v[n] = _k_adam(w[n], a, b, m1[n], m2[n], "adam_" + n)
    pk = lambda d: _pack([d[n] for n in _SMALL])
    sshapes = [w[n].shape for n in _SMALL]
    for dst, packed in zip((delta, new_m, new_v), _k_adam_small(pk(w), pk(grads), pk(m1), pk(m2))):
        dst.update(zip(_SMALL, _unpack(packed, sshapes)))

    lead = lambda d: [d[n][None] for n in _WEIGHTS]
    return (loss, grad_x[None], *lead(grads), *lead(delta), *lead(new_m), *lead(new_v))
```

```python
import math

import jax
import jax.numpy as jnp
import numpy as np
from jax import lax
from jax.experimental import pallas as pl
from jax.experimental.pallas import tpu as pltpu

F32 = jnp.float32
_MM = jnp.bfloat16
_WIRE = jnp.bfloat16

D_MODEL = 1024
HEAD = 64
BLK = 128
A_GROUPS = ((128, 1), (512, 4), (2048, 16))
A_HEADS = 4
A_W = A_HEADS * HEAD
B_QH = 8
B_KVH = 2
B_WINDOW = 128
M_HEADS = 4
M_HD = 128
M_W = M_HEADS * M_HD
D_FF = 2816
EPS = 1e-6
NEG = -1e30
ROPE_THETA = 500000.0
ROPE_ROT = 16
CHIPS = 4
NDEV = 8
ADAM_LR, ADAM_B1, ADAM_B2, ADAM_EPS, ADAM_WD, ADAM_STEP = 0.001, 0.9, 0.999, 1e-08, 0.01, 10
VMEM_LIMIT = 58 * 1024 * 1024
MESH = pl.DeviceIdType.MESH


def _pc(body, *, name, grid, in_specs, out_specs, out_shape, scratch=()):
    return pl.pallas_call(
        body, name=name, grid=grid, in_specs=in_specs, out_specs=out_specs, out_shape=out_shape,
        scratch_shapes=list(scratch),
        compiler_params=pltpu.CompilerParams(dimension_semantics=("arbitrary",) * len(grid),
                                             vmem_limit_bytes=VMEM_LIMIT))


def _row(ts, c, col=0):
    return pl.BlockSpec((ts, c), lambda i: (i, col))


def _res(shape):
    n = len(shape)
    return pl.BlockSpec(tuple(shape), lambda i: (0,) * n, pipeline_mode=pl.Buffered(1))


def _acc(shape):
    n = len(shape)
    return pl.BlockSpec(tuple(shape), lambda i: (0,) * n)


def _sds(shape, dtype=F32):
    return jax.ShapeDtypeStruct(tuple(shape), dtype)


def _dot(a, b):
    return jnp.dot(a.astype(_MM), b.astype(_MM), preferred_element_type=F32)


def _dot_nt(a, b):
    return lax.dot_general(a.astype(_MM), b.astype(_MM), (((1,), (1,)), ((), ())), preferred_element_type=F32)


def _dot_tn(a, b):
    return lax.dot_general(a.astype(_MM), b.astype(_MM), (((0,), (0,)), ((), ())), preferred_element_type=F32)


def _sum8(v):
    ts, c = v.shape
    return jnp.sum(v.reshape(ts // 8, 8, c), axis=0)


def _sigmoid(z):
    return 1.0 / (1.0 + jnp.exp(-z))


def _rms(x):
    r = lax.rsqrt(jnp.mean(x * x, axis=-1, keepdims=True) + EPS)
    return x * r, r


def _rms_bwd(dy, xh, r, gain):
    z = dy * gain
    return r * (z - xh * jnp.mean(z * xh, axis=-1, keepdims=True))


def _split_hi_lo(v):
    hi = v.astype(_MM)
    return hi, (v - hi.astype(F32)).astype(_MM)


def _lane_head(shape):
    return lax.shift_right_logical(lax.broadcasted_iota(jnp.int32, shape, len(shape) - 1), 6)


def _seg_sum64(v):
    w = v.shape[1]
    e = jnp.where(_lane_head((w, w)) == lax.shift_right_logical(lax.broadcasted_iota(jnp.int32, (w, w), 0), 6),
                  1.0, 0.0).astype(_MM)
    hi, lo = _split_hi_lo(v)
    return jnp.dot(hi, e, preferred_element_type=F32) + jnp.dot(lo, e, preferred_element_type=F32)


def _seg_norm(x, seg):
    if seg == HEAD:
        r = lax.rsqrt(_seg_sum64(x * x) * (1.0 / HEAD) + EPS)
        return x * r, r
    w = x.shape[1]
    xh, rr = [], []
    for s in range(w // seg):
        xs = x[:, s * seg:(s + 1) * seg]
        r = lax.rsqrt(jnp.mean(xs * xs, axis=-1, keepdims=True) + EPS)
        xh.append(xs * r)
        rr.append(jnp.broadcast_to(r, xs.shape))
    return jnp.concatenate(xh, axis=1), jnp.concatenate(rr, axis=1)


def _seg_mean(v, seg):
    if seg == HEAD:
        return _seg_sum64(v) * (1.0 / HEAD)
    w = v.shape[1]
    out = []
    for s in range(w // seg):
        vs = v[:, s * seg:(s + 1) * seg]
        out.append(jnp.broadcast_to(jnp.mean(vs, axis=-1, keepdims=True), vs.shape))
    return jnp.concatenate(out, axis=1)


def _rope(t, c, sa, sb):
    out = []
    for cb in range(t.shape[1] // 128):
        tc = t[:, cb * 128:(cb + 1) * 128]
        out.append(tc * c + pltpu.roll(tc, 120, 1) * sa + pltpu.roll(tc, 8, 1) * sb)
    return jnp.concatenate(out, axis=1) if len(out) > 1 else out[0]


def _rope_bwd(dy, c, sa, sb):
    out = []
    for cb in range(dy.shape[1] // 128):
        dc = dy[:, cb * 128:(cb + 1) * 128]
        out.append(dc * c + pltpu.roll(dc * sa, 8, 1) + pltpu.roll(dc * sb, 120, 1))
    return jnp.concatenate(out, axis=1) if len(out) > 1 else out[0]


def _rope_consts():
    half = ROPE_ROT // 2
    c = np.float32(-2.0 * math.log(ROPE_THETA) / ROPE_ROT)
    freqs = np.exp(np.arange(half, dtype=np.float32) * c).astype(np.float32)
    place = np.zeros((3, half, 128), np.float32)
    ones = np.zeros((1, 128), np.float32)
    for lane in range(128):
        d = lane % HEAD
        if d < half:
            place[0, d, lane], place[1, d, lane] = 1.0, -1.0
        elif d < ROPE_ROT:
            place[0, d - half, lane], place[2, d - half, lane] = 1.0, 1.0
        else:
            ones[0, lane] = 1.0
    return np.tile(freqs[:, None], (1, 128)), place, ones


def _rope_tables(pos_rows):
    r = pos_rows.shape[0]
    tr = min(1024, r)
    freqs, place, ones = _rope_consts()

    def split3(v):
        hi, mid = _split_hi_lo(v)
        lo = (v - hi.astype(F32) - mid.astype(F32)).astype(_MM)
        return hi, mid, lo

    def body(p_ref, f_ref, e_ref, one_ref, c_ref, sa_ref, sb_ref):
        for j in range(tr // 128):
            ang = p_ref[j:j + 1, :].astype(F32) * f_ref[...]
            rows = slice(j * 128, (j + 1) * 128)
            for ref, k, v in ((c_ref, 0, jnp.cos(ang)), (sa_ref, 1, jnp.sin(ang)), (sb_ref, 2, jnp.sin(ang))):
                e = e_ref[k].astype(_MM)
                out = sum(_dot_tn(part, e) for part in split3(v))
                ref[rows, :] = out + one_ref[...] if k == 0 else out

    return _pc(body, name="rope_tables", grid=(r // tr,),
               in_specs=[pl.BlockSpec((tr // 128, 128), lambda i: (i, 0)), _acc((ROPE_ROT // 2, 128)),
                         _acc((3, ROPE_ROT // 2, 128)), _acc((1, 128))],
               out_specs=[_row(tr, 128)] * 3, out_shape=[_sds((r, 128))] * 3)(
                   pos_rows.reshape(r // 128, 128), jnp.asarray(freqs), jnp.asarray(place), jnp.asarray(ones))


def _k_in(x, g1, w_in):
    s = x.shape[0]
    ts = min(256, s)
    nin = w_in.shape[2]
    ncol = CHIPS * nin
    a_cols = 3 * A_W
    offs = [0, a_cols, 2 * a_cols, 3 * a_cols, 3 * a_cols + B_QH * HEAD,
            3 * a_cols + (B_QH + B_KVH) * HEAD, 3 * a_cols + (B_QH + 2 * B_KVH) * HEAD, ncol]

    def body(x_ref, g_ref, wi_ref, h_ref, a0, a1, a2, qb, kb, vb, mq, p_scr):
        xh, _ = _rms(x_ref[...])
        h = (xh * g_ref[...]).astype(_MM)
        h_ref[...] = h
        for j in range(CHIPS):
            p_scr[:, j * nin:(j + 1) * nin] = jnp.dot(h, wi_ref[j], preferred_element_type=F32)
        for k, ref in enumerate((a0, a1, a2, qb, kb, vb, mq)):
            ref[...] = p_scr[:, offs[k]:offs[k + 1]]

    widths = [offs[k + 1] - offs[k] for k in range(7)]
    return _pc(
        body, name="in_proj", grid=(s // ts,),
        in_specs=[_row(ts, D_MODEL), _res((1, D_MODEL)), _res(w_in.shape)],
        out_specs=[_row(ts, D_MODEL)] + [_row(ts, w) for w in widths],
        out_shape=[_sds((s, D_MODEL), _MM)] + [_sds((s, w)) for w in widths],
        scratch=[pltpu.VMEM((ts, ncol), F32)])(x, g1, w_in)


def _k_gate(h, w_gate, b_gate):
    s = h.shape[0]
    ts = min(256, s)
    ng = w_gate.shape[2]

    def body(h_ref, wg_ref, bg_ref, gt_ref):
        h = h_ref[...]
        for j in range(CHIPS):
            z = jnp.dot(h, wg_ref[j], preferred_element_type=F32) + bg_ref[:, j * ng:(j + 1) * ng]
            gt_ref[:, j * ng:(j + 1) * ng] = _sigmoid(z)

    return _pc(body, name="gate_proj", grid=(s // ts,),
               in_specs=[_row(ts, D_MODEL), _res(w_gate.shape), _res(b_gate.shape)],
               out_specs=[_row(ts, CHIPS * ng)], out_shape=[_sds((s, CHIPS * ng))])(h, w_gate, b_gate)[0]


def _k_prep(srcs, gq, gk, tabs, tab_row, *, wq, wk, rows_per_gain, name):
    rows = srcs[0][0].shape[0]
    ts = min(256, rows)

    def body(q_ref, k_ref, v_ref, gq_ref, gk_ref, c_ref, sa_ref, sb_ref, qn_ref, kn_ref, vn_ref):
        c, sa, sb = c_ref[...], sa_ref[...], sb_ref[...]
        qh, _ = _seg_norm(q_ref[...], HEAD)
        qn_ref[...] = _rope(qh * gq_ref[...], c, sa, sb).astype(_MM)
        kh, _ = _seg_norm(k_ref[...], HEAD)
        kn_ref[...] = _rope(kh * gk_ref[...], c, sa, sb).astype(_MM)
        vn_ref[...] = v_ref[...].astype(_MM)

    gspec = lambda w: pl.BlockSpec((None, 1, w), lambda i: ((i * ts) // rows_per_gain, 0, 0))
    return _pc(
        body, name=name, grid=(rows // ts,),
        in_specs=[_row(ts, wq, srcs[0][1]), _row(ts, wk, srcs[1][1]), _row(ts, wk, srcs[2][1]),
                  gspec(wq), gspec(wk)] + [pl.BlockSpec((ts, 128), lambda i: (i + tab_row // ts, 0))] * 3,
        out_specs=[_row(ts, wq), _row(ts, wk), _row(ts, wk)],
        out_shape=[_sds((rows, wq), _MM), _sds((rows, wk), _MM), _sds((rows, wk), _MM)])(
            srcs[0][0], srcs[1][0], srcs[2][0], gq, gk, *tabs)


def _first_flag(b, segs, nb):
    first = b >= nb
    for k, (start, period) in enumerate(segs):
        end = segs[k + 1][0] if k + 1 < len(segs) else nb
        first = first | ((b >= start) & (b < end) & (lax.rem(b - start, jnp.int32(period)) == 0))
    return first


def _band_bias(thr, with_cur):
    qi = lax.broadcasted_iota(jnp.int32, (BLK, BLK), 0)
    kj = lax.broadcasted_iota(jnp.int32, (BLK, BLK), 1)
    prev = jnp.where(kj >= qi + thr, 0.0, NEG)
    return jnp.concatenate([prev, jnp.where(kj <= qi, 0.0, NEG)], axis=1) if with_cur else prev


def _blockdiag(t4):
    head = _lane_head((1, A_W))
    return jnp.concatenate([t4 * jnp.where(head == h, 1.0, 0.0).astype(t4.dtype) for h in range(A_HEADS)], axis=0)


def _fold_diag(t, n):
    head = _lane_head((n, A_W))
    out = t[3 * n:4 * n]
    for h in (2, 1, 0):
        out = jnp.where(head == h, t[h * n:(h + 1) * n], out)
    return out


def _expand_heads(cols):
    n = cols[0].shape[0]
    head = _lane_head((n, A_W))
    out = jnp.broadcast_to(cols[3], (n, A_W))
    for h in (2, 1, 0):
        out = jnp.where(head == h, cols[h], out)
    return out


def _unit_kv(p_ref, c_ref, u, shared):
    if not shared:
        return jnp.concatenate([p_ref[:, u * A_W:(u + 1) * A_W], c_ref[:, u * A_W:(u + 1) * A_W]], axis=0)
    kg = jnp.concatenate([p_ref[:, u * HEAD:(u + 1) * HEAD], c_ref[:, u * HEAD:(u + 1) * HEAD]], axis=0)
    return jnp.concatenate([kg] * A_HEADS, axis=1)


def _k_band_fwd(qn, kn, vn, *, hq, hk, max_dist, segs, sink, name):
    rows = qn.shape[0]
    nb = rows // BLK
    units = hq // A_HEADS
    shared = hk != hq
    wq, wk = hq * HEAD, hk * HEAD
    scale = HEAD ** -0.5

    def body(*refs):
        if sink is None:
            q_ref, kc_ref, kp_ref, vc_ref, vp_ref, o_ref, l_ref = refs
        else:
            q_ref, kc_ref, kp_ref, vc_ref, vp_ref, sk_ref, o_ref, l_ref = refs
        b = pl.program_id(0)
        bias = _band_bias(jnp.where(_first_flag(b, segs, nb), 1 << 20, BLK - max_dist), True)
        for u in range(units):
            us = slice(u * A_W, (u + 1) * A_W)
            kb = _blockdiag(_unit_kv(kp_ref, kc_ref, u, shared))
            vb = _blockdiag(_unit_kv(vp_ref, vc_ref, u, shared))
            s_all = _dot_nt(q_ref[:, us], kb) * scale
            ps, ls = [], []
            for h in range(A_HEADS):
                s = s_all[:, h * 2 * BLK:(h + 1) * 2 * BLK] + bias
                m = jnp.max(s, axis=-1, keepdims=True)
                e = jnp.exp(s - m)
                lse = m + jnp.log(jnp.sum(e, axis=-1, keepdims=True))
                if sink is not None:
                    sk = sk_ref[u * A_HEADS + h]
                    mx = jnp.maximum(lse, sk)
                    lse = mx + jnp.log(jnp.exp(lse - mx) + jnp.exp(sk - mx))
                ps.append((e * jnp.exp(m - lse)).astype(_MM))
                ls.append(lse)
            o_ref[:, us] = _dot(jnp.concatenate(ps, axis=1), vb)
            l_ref[:, us] = _expand_heads(ls)

    cur = lambda w: pl.BlockSpec((BLK, w), lambda i: (i, 0))
    prev = lambda w: pl.BlockSpec((BLK, w), lambda i: (jnp.maximum(i - 1, 0), 0))
    in_specs = [cur(wq), cur(wk), prev(wk), cur(wk), prev(wk)]
    args = [qn, kn, kn, vn, vn]
    if sink is not None:
        in_specs.append(pl.BlockSpec(memory_space=pltpu.SMEM))
        args.append(sink)
    return _pc(body, name=name, grid=(nb,), in_specs=in_specs, out_specs=[cur(wq), cur(wq)],
               out_shape=[_sds((rows, wq)), _sds((rows, wq))])(*args)


def _k_memkv(mem, mem_norm, w_kv, m_k_norm):
    n = mem.shape[0]

    def body(m_ref, g_ref, w_ref, gk_ref, mn_ref, kv_ref, mk_ref, mv_ref):
        mh, _ = _rms(m_ref[...])
        mn = (mh * g_ref[...]).astype(_MM)
        mn_ref[...] = mn
        kv = jnp.dot(mn, w_ref[...], preferred_element_type=F32)
        kv_ref[...] = kv
        kh, _ = _seg_norm(kv[:, :M_W], M_HD)
        mk_ref[...] = (kh * gk_ref[...]).astype(_MM)
        mv_ref[...] = kv[:, M_W:].astype(_MM)

    return _pc(body, name="mem_kv", grid=(1,),
               in_specs=[_acc((n, D_MODEL)), _acc((1, D_MODEL)), _acc(w_kv.shape), _acc((1, M_W))],
               out_specs=[_acc((n, D_MODEL)), _acc((n, 2 * M_W)), _acc((n, M_W)), _acc((n, M_W))],
               out_shape=[_sds((n, D_MODEL), _MM), _sds((n, 2 * M_W)), _sds((n, M_W), _MM), _sds((n, M_W), _MM)])(
                   mem, mem_norm, w_kv, m_k_norm)


def _mem_probs(q, mk):
    sc = _dot_nt(q, mk) * (M_HD ** -0.5)
    e = jnp.exp(sc - jnp.max(sc, axis=-1, keepdims=True))
    return e / jnp.sum(e, axis=-1, keepdims=True)


def _k_mem_fwd(m_q, gq, mk, mv):
    s = m_q.shape[0]
    n = mk.shape[0]
    ts = min(256, s)

    def body(q_ref, g_ref, mk_ref, mv_ref, o_ref):
        qh, _ = _seg_norm(q_ref[...], M_HD)
        qn = (qh * g_ref[...]).astype(_MM)
        for h in range(M_HEADS):
            hs = slice(h * M_HD, (h + 1) * M_HD)
            o_ref[:, hs] = _dot(_mem_probs(qn[:, hs], mk_ref[:, hs]), mv_ref[:, hs])

    return _pc(body, name="mem_attn", grid=(s // ts,),
               in_specs=[_row(ts, M_W), _res((1, M_W)), _res((n, M_W)), _res((n, M_W))],
               out_specs=[_row(ts, M_W)], out_shape=[_sds((s, M_W))])(m_q, gq, mk, mv)[0]


def _group_weights(l0, l1, l2):
    m = jnp.maximum(jnp.maximum(l0, l1), l2)
    e0, e1, e2 = jnp.exp(l0 - m), jnp.exp(l1 - m), jnp.exp(l2 - m)
    inv = 1.0 / (e0 + e1 + e2)
    return e0 * inv, e1 * inv, e2 * inv


def _branch_products(oa, ob, om, woa_ref, wob_ref, wom_ref, j):
    return _dot(oa, woa_ref[j]), _dot(ob, wob_ref[j]), _dot(om, wom_ref[j])


def _k_merge(og, lg, o_b, o_m, gates, x, w_oa, w_ob, w_om, w_out, g2):
    s = x.shape[0]
    ts = min(256, s)
    nc = w_oa.shape[2]

    def body(o0, o1, o2, l0, l1, l2, ob_ref, om_ref, gt_ref, x_ref, woa, wob, wom, wout, g_ref,
             oa_ref, mer_ref, x1_ref, h2_ref, m_scr):
        w0, w1, w2 = _group_weights(l0[...], l1[...], l2[...])
        oa = w0 * o0[...] + w1 * o1[...] + w2 * o2[...]
        oa_ref[...] = oa
        ob, om = ob_ref[...], om_ref[...]
        for j in range(CHIPS):
            pa, pb, pm = _branch_products(oa, ob, om, woa, wob, wom, j)
            cs = lambda br: slice(br * D_MODEL + j * nc, br * D_MODEL + (j + 1) * nc)
            m_scr[:, j * nc:(j + 1) * nc] = gt_ref[:, cs(0)] * pa + gt_ref[:, cs(1)] * pb + gt_ref[:, cs(2)] * pm
        mer = m_scr[...].astype(_MM)
        mer_ref[...] = mer
        x1 = x_ref[...] + jnp.dot(mer, wout[...], preferred_element_type=F32)
        x1_ref[...] = x1
        xh, _ = _rms(x1)
        h2_ref[...] = (xh * g_ref[...]).astype(_MM)

    return _pc(
        body, name="merge_out", grid=(s // ts,),
        in_specs=[_row(ts, A_W)] * 6 + [_row(ts, B_QH * HEAD), _row(ts, M_W), _row(ts, 3 * D_MODEL), _row(ts, D_MODEL),
                                         _res(w_oa.shape), _res(w_ob.shape), _res(w_om.shape), _res(w_out.shape),
                                         _res((1, D_MODEL))],
        out_specs=[_row(ts, A_W), _row(ts, D_MODEL), _row(ts, D_MODEL), _row(ts, D_MODEL)],
        out_shape=[_sds((s, A_W)), _sds((s, D_MODEL), _MM), _sds((s, D_MODEL)), _sds((s, D_MODEL), _MM)],
        scratch=[pltpu.VMEM((ts, D_MODEL), F32)])(*og, *lg, o_b, o_m, gates, x, w_oa, w_ob, w_om, w_out, g2)


def _k_up(h2, w_up):
    s = h2.shape[0]
    ts = min(256, s)
    nu = w_up.shape[2]

    def body(h_ref, w_ref, u_ref):
        h = h_ref[...]
        for j in range(CHIPS):
            u_ref[:, j * nu:(j + 1) * nu] = jnp.dot(h, w_ref[j], preferred_element_type=F32)

    return _pc(body, name="up_proj", grid=(s // ts,), in_specs=[_row(ts, D_MODEL), _res(w_up.shape)],
               out_specs=[_row(ts, CHIPS * nu)], out_shape=[_sds((s, CHIPS * nu))])(h2, w_up)[0]


def _shift_down(v, halo, k):
    ts = v.shape[0]
    row = lax.broadcasted_iota(jnp.int32, v.shape, 0)
    out = pltpu.roll(v, k, 0)
    for r in range(k):
        out = jnp.where(row == r, halo[8 - k + r:8 - k + r + 1, :], out)
    return out


def _shift_up(v, halo, k):
    ts = v.shape[0]
    row = lax.broadcasted_iota(jnp.int32, v.shape, 0)
    out = pltpu.roll(v, ts - k, 0)
    for r in range(k):
        out = jnp.where(row == ts - k + r, halo[r:r + 1, :], out)
    return out


def _k_ffn(u, conv_w, conv_b, w_down, x1, target):
    s = u.shape[0]
    ts = min(128, s)
    nu = conv_w.shape[2]
    half = CHIPS // 2

    def body(u_ref, uh_ref, cw_ref, cb_ref, wd_ref, x1_ref, t_ref, dy_ref, f_ref, dc_ref, loss_ref, c_scr, f_scr):
        i = pl.program_id(0)
        halo = jnp.where(i > 0, uh_ref[...], 0.0)
        for j in range(CHIPS):
            cs = slice(j * nu, (j + 1) * nu)
            uj = u_ref[:, cs]
            hj = halo[:, cs]
            c_scr[:, cs] = (cb_ref[:, cs] + cw_ref[j, 0:1, :] * _shift_down(uj, hj, 2)
                            + cw_ref[j, 1:2, :] * _shift_down(uj, hj, 1) + cw_ref[j, 2:3, :] * uj)
        for j in range(half):
            a = c_scr[:, j * nu:(j + 1) * nu]
            g = c_scr[:, (half + j) * nu:(half + j + 1) * nu]
            f_scr[:, j * nu:(j + 1) * nu] = (a * _sigmoid(a) * g).astype(_MM)
        f = f_scr[...]
        f_ref[...] = f
        y = x1_ref[...] + jnp.dot(f, wd_ref[...], preferred_element_type=F32)
        err = y - t_ref[...]
        dy = err * (1.0 / D_MODEL)
        dy_ref[...] = dy

        @pl.when(i == 0)
        def _():
            loss_ref[...] = jnp.zeros_like(loss_ref)

        loss_ref[...] += _sum8(err * err)
        df = _dot_nt(dy, wd_ref[...])
        for j in range(half):
            a = c_scr[:, j * nu:(j + 1) * nu]
            g = c_scr[:, (half + j) * nu:(half + j + 1) * nu]
            sa = _sigmoid(a)
            dfj = df[:, j * nu:(j + 1) * nu]
            dc_ref[:, j * nu:(j + 1) * nu] = dfj * g * (sa * (1.0 + a * (1.0 - sa)))
            dc_ref[:, (half + j) * nu:(half + j + 1) * nu] = dfj * (a * sa)

    wide = CHIPS * nu
    return _pc(
        body, name="conv_ffn", grid=(s // ts,),
        in_specs=[_row(ts, wide), pl.BlockSpec((8, wide), lambda i: (jnp.maximum(i * (ts // 8) - 1, 0), 0)),
                  _res(conv_w.shape), _res((1, wide)), _res(w_down.shape), _row(ts, D_MODEL), _row(ts, D_MODEL)],
        out_specs=[_row(ts, D_MODEL), _row(ts, D_FF), _row(ts, wide), _acc((8, D_MODEL))],
        out_shape=[_sds((s, D_MODEL)), _sds((s, D_FF), _MM), _sds((s, wide)), _sds((8, D_MODEL))],
        scratch=[pltpu.VMEM((ts, wide), F32), pltpu.VMEM((ts, D_FF), _MM)])(u, u, conv_w, conv_b, w_down, x1, target)


def _k_conv_bwd(dc, u, conv_w, w_up, x1, g2, dy):
    s = u.shape[0]
    ts = min(128, s)
    nu = conv_w.shape[2]
    wide = CHIPS * nu
    last = s // ts - 1

    def body(dc_ref, dn_ref, u_ref, uh_ref, cw_ref, wu_ref, x1_ref, g_ref, dy_ref,
             dx1_ref, du_ref, cacc_ref, gacc_ref):
        i = pl.program_id(0)

        @pl.when(i == 0)
        def _():
            cacc_ref[...] = jnp.zeros_like(cacc_ref)
            gacc_ref[...] = jnp.zeros_like(gacc_ref)

        uhalo = jnp.where(i > 0, uh_ref[...], 0.0)
        dhalo = jnp.where(i < last, dn_ref[...], 0.0)
        dh2 = jnp.zeros((ts, D_MODEL), F32)
        for j in range(CHIPS):
            cs = slice(j * nu, (j + 1) * nu)
            dcj, uj = dc_ref[:, cs], u_ref[:, cs]
            cacc_ref[0, :, cs] += _sum8(dcj)
            cacc_ref[1, :, cs] += _sum8(dcj * _shift_down(uj, uhalo[:, cs], 2))
            cacc_ref[2, :, cs] += _sum8(dcj * _shift_down(uj, uhalo[:, cs], 1))
            cacc_ref[3, :, cs] += _sum8(dcj * uj)
            du = (cw_ref[j, 2:3, :] * dcj + cw_ref[j, 1:2, :] * _shift_up(dcj, dhalo[:, cs], 1)
                  + cw_ref[j, 0:1, :] * _shift_up(dcj, dhalo[:, cs], 2)).astype(_MM)
            du_ref[:, cs] = du
            dh2 = dh2 + _dot_nt(du, wu_ref[j])
        xh, r = _rms(x1_ref[...])
        gacc_ref[...] += _sum8(dh2 * xh)
        dx1_ref[...] = dy_ref[...] + _rms_bwd(dh2, xh, r, g_ref[...])

    return _pc(
        body, name="conv_up_bwd", grid=(s // ts,),
        in_specs=[_row(ts, wide),
                  pl.BlockSpec((8, wide), lambda i: (jnp.minimum((i + 1) * (ts // 8), s // 8 - 1), 0)),
                  _row(ts, wide), pl.BlockSpec((8, wide), lambda i: (jnp.maximum(i * (ts // 8) - 1, 0), 0)),
                  _res(conv_w.shape), _res(w_up.shape), _row(ts, D_MODEL), _res((1, D_MODEL)), _row(ts, D_MODEL)],
        out_specs=[_row(ts, D_MODEL), _row(ts, wide), _acc((4, 8, wide)), _acc((8, D_MODEL))],
        out_shape=[_sds((s, D_MODEL)), _sds((s, wide), _MM), _sds((4, 8, wide)), _sds((8, D_MODEL))])(
            dc, dc, u, u, conv_w, w_up, x1, g2, dy)


def _k_merge_bwd(dx1, og, lg, o_a, o_b, o_m, gates, w_oa, w_ob, w_om, w_out, dep):
    s = dx1.shape[0]
    ts = min(256, s)
    nc = w_oa.shape[2]

    def body(dx_ref, o0, o1, o2, l0, l1, l2, oa_ref, ob_ref, om_ref, gt_ref, woa, wob, wom, wout, dep_ref,
             dgp_ref, dpa_ref, dpb_ref, dpm_ref, dog0, dog1, dog2, dl0, dl1, dl2, dob_ref, dom_ref, bacc_ref):
        i = pl.program_id(0)

        @pl.when(i == 0)
        def _():
            bacc_ref[...] = jnp.zeros_like(bacc_ref)

        dmer = _dot_nt(dx_ref[...], wout[...])
        oa, ob, om = oa_ref[...], ob_ref[...], om_ref[...]
        doa = jnp.zeros((ts, A_W), F32)
        dob = jnp.zeros((ts, B_QH * HEAD), F32)
        dom = jnp.zeros((ts, M_W), F32)
        for j in range(CHIPS):
            prods = _branch_products(oa, ob, om, woa, wob, wom, j)
            dmj = dmer[:, j * nc:(j + 1) * nc]
            dps = []
            for br, (p, dref) in enumerate(zip(prods, (dpa_ref, dpb_ref, dpm_ref))):
                cs = slice(br * D_MODEL + j * nc, br * D_MODEL + (j + 1) * nc)
                gt = gt_ref[:, cs]
                dgp = dmj * p * gt * (1.0 - gt)
                dgp_ref[:, cs] = dgp.astype(_MM)
                bacc_ref[:, cs] += _sum8(dgp)
                dp = (dmj * gt).astype(_MM)
                dref[:, j * nc:(j + 1) * nc] = dp
                dps.append(dp)
            doa = doa + _dot_nt(dps[0], woa[j])
            dob = dob + _dot_nt(dps[1], wob[j])
            dom = dom + _dot_nt(dps[2], wom[j])
        dob_ref[...] = dob
        dom_ref[...] = dom
        ws = _group_weights(l0[...], l1[...], l2[...])
        dsum = _seg_mean(doa * oa, HEAD) * float(HEAD)
        for w, dref, lref in zip(ws, (dog0, dog1, dog2), (dl0, dl1, dl2)):
            dref[...] = w * doa
            lref[...] = w * dsum

    return _pc(
        body, name="merge_out_bwd", grid=(s // ts,),
        in_specs=[_row(ts, D_MODEL)] + [_row(ts, A_W)] * 7 + [_row(ts, B_QH * HEAD), _row(ts, M_W), _row(ts, 3 * D_MODEL),
                                                              _res(w_oa.shape), _res(w_ob.shape), _res(w_om.shape),
                                                              _res(w_out.shape), _res((8, 128))],
        out_specs=[_row(ts, 3 * D_MODEL)] + [_row(ts, D_MODEL)] * 3 + [_row(ts, A_W)] * 6
        + [_row(ts, B_QH * HEAD), _row(ts, M_W), _acc((8, 3 * D_MODEL))],
        out_shape=[_sds((s, 3 * D_MODEL), _MM)] + [_sds((s, D_MODEL), _MM)] * 3 + [_sds((s, A_W))] * 6
        + [_sds((s, B_QH * HEAD)), _sds((s, M_W)), _sds((8, 3 * D_MODEL))])(
            dx1, *og, *lg, o_a, o_b, o_m, gates, w_oa, w_ob, w_om, w_out, dep)


def _k_mem_bwd(m_q, gq, mk, mv, o_m, do_m):
    s = m_q.shape[0]
    n = mk.shape[0]
    ts = min(256, s)
    scale = M_HD ** -0.5

    def body(q_ref, g_ref, mk_ref, mv_ref, o_ref, do_ref, dq_ref, dmk_ref, dmv_ref, gacc_ref):
        i = pl.program_id(0)

        @pl.when(i == 0)
        def _():
            dmk_ref[...] = jnp.zeros_like(dmk_ref)
            dmv_ref[...] = jnp.zeros_like(dmv_ref)
            gacc_ref[...] = jnp.zeros_like(gacc_ref)

        gain = g_ref[...]
        qh, r = _seg_norm(q_ref[...], M_HD)
        qn = (qh * gain).astype(_MM)
        do = do_ref[...]
        delta = _seg_mean(do * o_ref[...], M_HD) * float(M_HD)
        dqn = []
        for h in range(M_HEADS):
            hs = slice(h * M_HD, (h + 1) * M_HD)
            p = _mem_probs(qn[:, hs], mk_ref[:, hs])
            dp = _dot_nt(do[:, hs], mv_ref[:, hs])
            ds = (p * (dp - delta[:, hs][:, 0:1]) * scale).astype(_MM)
            dqn.append(_dot(ds, mk_ref[:, hs]))
            dmk_ref[:, hs] += _dot_tn(ds, qn[:, hs])
            dmv_ref[:, hs] += _dot_tn(p, do[:, hs])
        dqn = jnp.concatenate(dqn, axis=1)
        gacc_ref[...] += _sum8(dqn * qh)
        z = dqn * gain
        dq_ref[...] = (r * (z - qh * _seg_mean(z * qh, M_HD))).astype(_MM)

    return _pc(
        body, name="mem_attn_bwd", grid=(s // ts,),
        in_specs=[_row(ts, M_W), _res((1, M_W)), _res((n, M_W)), _res((n, M_W)), _row(ts, M_W), _row(ts, M_W)],
        out_specs=[_row(ts, M_W), _acc((n, M_W)), _acc((n, M_W)), _acc((8, M_W))],
        out_shape=[_sds((s, M_W), _MM), _sds((n, M_W)), _sds((n, M_W)), _sds((8, M_W))])(m_q, gq, mk, mv, o_m, do_m)


def _k_memkv_bwd(mem, mem_norm, w_kv, m_k_norm, mem_n, kv, dmk, dmv):
    n = mem.shape[0]

    def body(m_ref, g_ref, w_ref, gk_ref, mn_ref, kv_ref, dmk_ref, dmv_ref, dw_ref, dg_ref, dgk_ref):
        gk = gk_ref[...]
        kh, r = _seg_norm(kv_ref[:, :M_W], M_HD)
        dmk = dmk_ref[...]
        dgk_ref[...] = _sum8(dmk * kh)
        z = dmk * gk
        dk = r * (z - kh * _seg_mean(z * kh, M_HD))
        dkv = jnp.concatenate([dk, dmv_ref[...]], axis=1).astype(_MM)
        dw_ref[...] = _dot_tn(mn_ref[...], dkv)
        dmn = _dot_nt(dkv, w_ref[...])
        mh, _ = _rms(m_ref[...])
        dg_ref[...] = _sum8(dmn * mh)

    return _pc(body, name="mem_kv_bwd", grid=(1,),
               in_specs=[_acc((n, D_MODEL)), _acc((1, D_MODEL)), _acc(w_kv.shape), _acc((1, M_W)), _acc((n, D_MODEL)),
                         _acc((n, 2 * M_W)), _acc((n, M_W)), _acc((n, M_W))],
               out_specs=[_acc(w_kv.shape), _acc((8, D_MODEL)), _acc((8, M_W))],
               out_shape=[_sds(w_kv.shape), _sds((8, D_MODEL)), _sds((8, M_W))])(
                   mem, mem_norm, w_kv, m_k_norm, mem_n, kv, dmk, dmv)


def _k_band_bwd(qn, kn, vn, do, lse, dl_or_o, *, hq, hk, max_dist, segs, sink, name):
    rows = qn.shape[0]
    nb = rows // BLK
    units = hq // A_HEADS
    shared = hk != hq
    wq, wk = hq * HEAD, hk * HEAD
    scale = HEAD ** -0.5

    def body(*refs):
        (qb_ref, qx_ref, kb_ref, kp_ref, vb_ref, vp_ref, dob_ref, dox_ref, lb_ref, lx_ref, eb_ref, ex_ref) = refs[:12]
        if sink is None:
            dq_ref, dk_ref, dv_ref = refs[12:]
        else:
            sk_ref, dq_ref, dk_ref, dv_ref, sacc_ref = refs[12:]
        b = pl.program_id(0)
        bias1 = _band_bias(jnp.where(_first_flag(b, segs, nb), 1 << 20, BLK - max_dist), True)
        bias2 = _band_bias(jnp.where(_first_flag(b + 1, segs, nb), 1 << 20, BLK - max_dist), False)
        if sink is not None:
            @pl.when(b == 0)
            def _():
                sacc_ref[...] = jnp.zeros_like(sacc_ref)

        for u in range(units):
            us = slice(u * A_W, (u + 1) * A_W)
            q4, qx4, do4, dox4 = qb_ref[:, us], qx_ref[:, us], dob_ref[:, us], dox_ref[:, us]
            k4, v4 = _unit_kv(kp_ref, kb_ref, u, shared), _unit_kv(vp_ref, vb_ref, u, shared)
            kd, vd = _blockdiag(k4), _blockdiag(v4)
            kdc, vdc = _blockdiag(k4[BLK:]), _blockdiag(v4[BLK:])
            if sink is None:
                dlt_b, dlt_x = eb_ref[:, us], ex_ref[:, us]
            else:
                dlt_b = _seg_sum64(do4.astype(F32) * eb_ref[:, us])
                dlt_x = _seg_sum64(dox4.astype(F32) * ex_ref[:, us])
            s1, dp1 = _dot_nt(q4, kd) * scale, _dot_nt(do4, vd)
            s2, dp2 = _dot_nt(qx4, kdc) * scale, _dot_nt(dox4, vdc)
            ds1, ds1c, p1c, ds2, p2 = [], [], [], [], []
            for h in range(A_HEADS):
                col = slice(u * A_W + h * HEAD, u * A_W + h * HEAD + 1)
                ucol = slice(h * HEAD, h * HEAD + 1)
                wide, narrow = slice(h * 2 * BLK, (h + 1) * 2 * BLK), slice(h * BLK, (h + 1) * BLK)
                l_b, l_x = lb_ref[:, col], lx_ref[:, col]
                p = jnp.exp(s1[:, wide] + bias1 - l_b)
                ds = p * (dp1[:, wide] - dlt_b[:, ucol]) * scale
                ds1.append(ds.astype(_MM))
                ds1c.append(ds[:, BLK:].astype(_MM))
                p1c.append(p[:, BLK:].astype(_MM))
                px = jnp.exp(s2[:, narrow] + bias2 - l_x)
                ds2.append((px * (dp2[:, narrow] - dlt_x[:, ucol]) * scale).astype(_MM))
                p2.append(px.astype(_MM))
                if sink is not None:
                    j = u * A_HEADS + h
                    sacc_ref[:, j:j + 1] += -jnp.exp(sk_ref[j] - l_b) * dlt_b[:, ucol]
            dq_ref[:, us] = _dot(jnp.concatenate(ds1, axis=1), kd)
            dk4 = _fold_diag(_dot_tn(jnp.concatenate(ds1c, axis=1), q4) + _dot_tn(jnp.concatenate(ds2, axis=1), qx4), BLK)
            dv4 = _fold_diag(_dot_tn(jnp.concatenate(p1c, axis=1), do4) + _dot_tn(jnp.concatenate(p2, axis=1), dox4), BLK)
            if shared:
                fold = lambda t: (t[:, 0:HEAD] + t[:, HEAD:2 * HEAD]) + (t[:, 2 * HEAD:3 * HEAD] + t[:, 3 * HEAD:])
                dk_ref[:, u * HEAD:(u + 1) * HEAD] = fold(dk4)
                dv_ref[:, u * HEAD:(u + 1) * HEAD] = fold(dv4).astype(_MM)
            else:
                dk_ref[:, us] = dk4
                dv_ref[:, us] = dv4.astype(_MM)

    cur = lambda w: pl.BlockSpec((BLK, w), lambda i: (i, 0))
    prev = lambda w: pl.BlockSpec((BLK, w), lambda i: (jnp.maximum(i - 1, 0), 0))
    nxt = lambda w: pl.BlockSpec((BLK, w), lambda i: (jnp.minimum(i + 1, nb - 1), 0))
    in_specs = [cur(wq), nxt(wq), cur(wk), prev(wk), cur(wk), prev(wk), cur(wq), nxt(wq), cur(wq), nxt(wq), cur(wq), nxt(wq)]
    args = [qn, qn, kn, kn, vn, vn, do, do, lse, lse, dl_or_o, dl_or_o]
    out_specs = [cur(wq), cur(wk), cur(wk)]
    out_shape = [_sds((rows, wq)), _sds((rows, wk)), _sds((rows, wk), _MM)]
    if sink is not None:
        in_specs.append(pl.BlockSpec(memory_space=pltpu.SMEM))
        args.append(sink)
        out_specs.append(_acc((BLK, 128)))
        out_shape.append(_sds((BLK, 128)))
    return _pc(body, name=name, grid=(nb,), in_specs=in_specs, out_specs=out_specs, out_shape=out_shape)(*args)


def _k_prep_bwd(srcs, dqn, dkn, gq, gk, tabs, tab_row, *, wq, wk, rows_per_gain, name):
    rows = dqn.shape[0]
    ts = min(256, rows)
    ngain = gq.shape[0]

    def body(q_ref, k_ref, dq_ref, dk_ref, gq_ref, gk_ref, c_ref, sa_ref, sb_ref, oq_ref, ok_ref, aq_ref, ak_ref):
        i = pl.program_id(0)

        @pl.when(lax.rem(i * ts, rows_per_gain) == 0)
        def _():
            aq_ref[...] = jnp.zeros_like(aq_ref)
            ak_ref[...] = jnp.zeros_like(ak_ref)

        c, sa, sb = c_ref[...], sa_ref[...], sb_ref[...]
        for x_ref, d_ref, g_ref, o_ref, a_ref in ((q_ref, dq_ref, gq_ref, oq_ref, aq_ref),
                                                   (k_ref, dk_ref, gk_ref, ok_ref, ak_ref)):
            xh, r = _seg_norm(x_ref[...], HEAD)
            dt = _rope_bwd(d_ref[...], c, sa, sb)
            a_ref[...] += _sum8(dt * xh)
            z = dt * g_ref[...]
            o_ref[...] = (r * (z - xh * _seg_mean(z * xh, HEAD))).astype(_MM)

    gspec = lambda w: pl.BlockSpec((None, 1, w), lambda i: ((i * ts) // rows_per_gain, 0, 0))
    aspec = lambda w: pl.BlockSpec((None, 8, w), lambda i: ((i * ts) // rows_per_gain, 0, 0))
    return _pc(
        body, name=name, grid=(rows // ts,),
        in_specs=[_row(ts, wq, srcs[0][1]), _row(ts, wk, srcs[1][1]), _row(ts, wq), _row(ts, wk), gspec(wq), gspec(wk)]
        + [pl.BlockSpec((ts, 128), lambda i: (i + tab_row // ts, 0))] * 3,
        out_specs=[_row(ts, wq), _row(ts, wk), aspec(wq), aspec(wk)],
        out_shape=[_sds((rows, wq), _MM), _sds((rows, wk), _MM), _sds((ngain, 8, wq)), _sds((ngain, 8, wk))])(
            srcs[0][0], srcs[1][0], dqn, dkn, gq, gk, *tabs)


def _k_in_bwd(pieces, dgp, x, g1, dx1, w_in, w_gate):
    s = x.shape[0]
    ts = min(256, s)
    nin, ng = w_in.shape[2], w_gate.shape[2]
    widths = [p.shape[1] for p in pieces]
    ncol = sum(widths)

    def body(*refs):
        p_refs = refs[:len(pieces)]
        dgp_ref, x_ref, g_ref, dx1_ref, wi_ref, wg_ref, gx_ref, dpj_ref, gacc_ref = refs[len(pieces):]
        i = pl.program_id(0)

        @pl.when(i == 0)
        def _():
            gacc_ref[...] = jnp.zeros_like(gacc_ref)

        off = 0
        for p_ref, w in zip(p_refs, widths):
            dpj_ref[:, off:off + w] = p_ref[...]
            off += w
        dh = jnp.zeros((ts, D_MODEL), F32)
        for j in range(CHIPS):
            dh = dh + _dot_nt(dpj_ref[:, j * nin:(j + 1) * nin], wi_ref[j])
            dh = dh + _dot_nt(dgp_ref[:, j * ng:(j + 1) * ng], wg_ref[j])
        xh, r = _rms(x_ref[...])
        gacc_ref[...] += _sum8(dh * xh)
        gx_ref[...] = dx1_ref[...] + _rms_bwd(dh, xh, r, g_ref[...])

    return _pc(
        body, name="in_proj_bwd", grid=(s // ts,),
        in_specs=[_row(ts, w) for w in widths] + [_row(ts, CHIPS * ng), _row(ts, D_MODEL), _res((1, D_MODEL)),
                                                  _row(ts, D_MODEL), _res(w_in.shape), _res(w_gate.shape)],
        out_specs=[_row(ts, D_MODEL), _row(ts, ncol), _acc((8, D_MODEL))],
        out_shape=[_sds((s, D_MODEL)), _sds((s, ncol), _MM), _sds((8, D_MODEL))])(*pieces, dgp, x, g1, dx1, w_in, w_gate)


def _k_wgrad(a, b, *, nblk, stacked, name):
    s, k = a.shape
    n = b.shape[1]
    nb = n // nblk
    ts = min(1024, s)

    def body(a_ref, b_ref, o_ref):
        @pl.when(pl.program_id(1) == 0)
        def _():
            o_ref[...] = jnp.zeros_like(o_ref)

        o_ref[...] += _dot_tn(a_ref[...], b_ref[...])

    if stacked:
        out_spec, out_shape = pl.BlockSpec((None, k, nb), lambda g, t: (g, 0, 0)), _sds((nblk, k, nb))
    else:
        out_spec, out_shape = pl.BlockSpec((k, nb), lambda g, t: (0, g)), _sds((k, n))
    return _pc(body, name=name, grid=(nblk, s // ts),
               in_specs=[pl.BlockSpec((ts, k), lambda g, t: (t, 0)), pl.BlockSpec((ts, nb), lambda g, t: (t, g))],
               out_specs=[out_spec], out_shape=[out_shape])(a, b)[0]


def _to_res(t, d):
    s, c = t.shape
    return t if d == 1 else t.reshape(s // d, d, c).transpose(1, 0, 2).reshape(s, c)


def _from_res(t, d):
    s, c = t.shape
    return t if d == 1 else t.reshape(d, s // d, c).transpose(1, 0, 2).reshape(s, c)


def _tile_gain(g, heads):
    return jnp.tile(g, (1,) * (g.ndim - 1) + (heads,))[..., None, :]


def _local_step(x, mem, pos, target, small, w_in, get_rest, on_grads):
    s = x.shape[0]
    nblk = s // BLK
    g1, g2 = small["attn_norm"], small["ffn_norm"]

    pos_rows = jnp.concatenate([_to_res(pos[:, None], d)[:, 0] for _, d in A_GROUPS] + [pos])
    tabs = _rope_tables(pos_rows)

    h, qa0, qa1, qa2, q_b, k_b, v_b, m_q = _k_in(x, g1, w_in)

    qkv_a = jnp.concatenate([_to_res(t, d) for t, (_, d) in zip((qa0, qa1, qa2), A_GROUPS)], axis=0)
    gq_a = _tile_gain(small["a_q_norm"], A_HEADS)
    gk_a = _tile_gain(small["a_k_norm"], A_HEADS)
    src_a = ((qkv_a, 0), (qkv_a, 1), (qkv_a, 2))
    qn_a, kn_a, vn_a = _k_prep(src_a, gq_a, gk_a, tabs, 0, wq=A_W, wk=A_W, rows_per_gain=s, name="prep_a")
    segs_a = tuple((gi * nblk, nblk // d) for gi, (_, d) in enumerate(A_GROUPS))
    o_res, l_res = _k_band_fwd(qn_a, kn_a, vn_a, hq=A_HEADS, hk=A_HEADS, max_dist=BLK, segs=segs_a, sink=None,
                               name="attn_a")
    og = [_from_res(o_res[gi * s:(gi + 1) * s], d) for gi, (_, d) in enumerate(A_GROUPS)]
    lg = [_from_res(l_res[gi * s:(gi + 1) * s], d) for gi, (_, d) in enumerate(A_GROUPS)]

    gq_b = _tile_gain(small["b_q_norm"], B_QH)
    gk_b = _tile_gain(small["b_k_norm"], B_KVH)
    src_b = ((q_b, 0), (k_b, 0), (v_b, 0))
    qn_b, kn_b, vn_b = _k_prep(src_b, gq_b, gk_b, tabs, 3 * s, wq=B_QH * HEAD, wk=B_KVH * HEAD, rows_per_gain=s,
                               name="prep_b")
    sink_x = small["b_sinks"][0]
    segs_b = ((0, nblk),)
    o_b, l_b = _k_band_fwd(qn_b, kn_b, vn_b, hq=B_QH, hk=B_KVH, max_dist=B_WINDOW - 1, segs=segs_b, sink=sink_x,
                           name="attn_b")

    wts = get_rest(o_b)
    gates = _k_gate(h, wts["w_gate"], small["b_gate"])

    gq_m = _tile_gain(small["m_q_norm"], M_HEADS)[0]
    gk_m = _tile_gain(small["m_k_norm"], M_HEADS)[0]
    mem_n, kv, mk, mv = _k_memkv(mem, small["mem_norm"], wts["w_mem_kv"], gk_m)
    o_m = _k_mem_fwd(m_q, gq_m, mk, mv)

    o_a, merged, x1, h2 = _k_merge(og, lg, o_b, o_m, gates, x, wts["w_o_a"], wts["w_o_b"], wts["w_o_m"],
                                   wts["w_out"], g2)
    u = _k_up(h2, wts["w_up"])
    dy, f, dc, loss_acc = _k_ffn(u, wts["conv_w"], small["conv_b"], wts["w_down"], x1, target)
    loss = (0.5 / D_MODEL) * jnp.sum(loss_acc)

    dx1, du, cacc, g2acc = _k_conv_bwd(dc, u, wts["conv_w"], wts["w_up"], x1, g2, dy)
    tok = on_grads({"w_up": _k_wgrad(h2, du, nblk=CHIPS, stacked=True, name="dw_up"),
                    "w_down": _k_wgrad(f, dy, nblk=2, stacked=False, name="dw_down").reshape(CHIPS, -1, D_MODEL)}, dx1)
    (dgp, dp_a, dp_b, dp_m, dog0, dog1, dog2, dl0, dl1, dl2, do_b, do_m, bacc) = _k_merge_bwd(
        dx1, og, lg, o_a, o_b, o_m, gates, wts["w_o_a"], wts["w_o_b"], wts["w_o_m"], wts["w_out"], tok)
    tok = on_grads({"w_gate": _k_wgrad(h, dgp, nblk=CHIPS, stacked=True, name="dw_gate"),
                    "w_o_a": _k_wgrad(o_a, dp_a, nblk=CHIPS, stacked=True, name="dw_o_a"),
                    "w_o_b": _k_wgrad(o_b, dp_b, nblk=CHIPS, stacked=True, name="dw_o_b"),
                    "w_o_m": _k_wgrad(o_m, dp_m, nblk=CHIPS, stacked=True, name="dw_o_m"),
                    "w_out": _k_wgrad(merged, dx1, nblk=1, stacked=False, name="dw_out").reshape(CHIPS, -1, D_MODEL)},
                   do_m)

    dq_m, dmk, dmv, gqm_acc = _k_mem_bwd(m_q, gq_m + tok[0:1, 0:1], mk, mv, o_m, do_m)
    dw_kv, gmem_acc, gkm_acc = _k_memkv_bwd(mem, small["mem_norm"], wts["w_mem_kv"], gk_m, mem_n, kv, dmk, dmv)

    dq_bn, dk_bn, dv_b, sacc = _k_band_bwd(qn_b, kn_b, vn_b, do_b, l_b, o_b, hq=B_QH, hk=B_KVH,
                                           max_dist=B_WINDOW - 1, segs=segs_b, sink=sink_x, name="attn_b_bwd")
    tok = on_grads({}, dq_bn)
    dq_b, dk_b, gqb_acc, gkb_acc = _k_prep_bwd(src_b, dq_bn, dk_bn, gq_b + tok[0:1, 0:1], gk_b, tabs, 3 * s, wq=B_QH * HEAD,
                                               wk=B_KVH * HEAD, rows_per_gain=s, name="prep_b_bwd")

    do_res = jnp.concatenate([_to_res(t, d) for t, (_, d) in zip((dog0, dog1, dog2), A_GROUPS)], axis=0)
    dl_res = jnp.concatenate([_to_res(t, d) for t, (_, d) in zip((dl0, dl1, dl2), A_GROUPS)], axis=0)
    dq_an, dk_an, dv_a = _k_band_bwd(qn_a, kn_a, vn_a, do_res, l_res, dl_res, hq=A_HEADS, hk=A_HEADS, max_dist=BLK,
                                     segs=segs_a, sink=None, name="attn_a_bwd")
    dq_a, dk_a, gqa_acc, gka_acc = _k_prep_bwd(src_a, dq_an, dk_an, gq_a, gk_a, tabs, 0, wq=A_W, wk=A_W,
                                               rows_per_gain=s, name="prep_a_bwd")
    pieces = []
    for gi, (_, d) in enumerate(A_GROUPS):
        rs = slice(gi * s, (gi + 1) * s)
        pieces += [_from_res(t[rs], d) for t in (dq_a, dk_a, dv_a)]
    pieces += [dq_b, dk_b, dv_b, dq_m]
    grad_x, dproj, g1acc = _k_in_bwd(pieces, dgp, x, g1, dx1, w_in, wts["w_gate"])
    on_grads({"w_in": _k_wgrad(h, dproj, nblk=CHIPS, stacked=True, name="dw_in"),
              "w_mem_kv": dw_kv.reshape(CHIPS, -1, 2 * M_W)}, grad_x)

    def fold(acc, heads):
        v = jnp.sum(acc, axis=-2)
        return jnp.sum(v.reshape(v.shape[:-1] + (heads, -1)), axis=-2)

    csum = jnp.sum(cacc, axis=1)
    sml = {
        "attn_norm": jnp.sum(g1acc, axis=0), "a_q_norm": fold(gqa_acc, A_HEADS), "a_k_norm": fold(gka_acc, A_HEADS),
        "b_q_norm": fold(gqb_acc[0], B_QH), "b_k_norm": fold(gkb_acc[0], B_KVH),
        "b_sinks": jnp.sum(sacc, axis=0)[:B_QH], "mem_norm": jnp.sum(gmem_acc, axis=0),
        "m_q_norm": fold(gqm_acc, M_HEADS), "m_k_norm": fold(gkm_acc, M_HEADS),
        "b_gate": jnp.sum(bacc, axis=0), "ffn_norm": jnp.sum(g2acc, axis=0),
        "conv_w": csum[1:], "conv_b": csum[0],
    }
    return loss, grad_x, sml


def _mesh_pos():
    return lax.axis_index("x"), lax.axis_index("y"), lax.axis_index("c")


def _chip_peers(x, y):
    return [(1 - x, y), (x, 1 - y), (1 - x, 1 - y)]


_ANY = pl.BlockSpec(memory_space=pl.ANY)


def _comm_call(body, *, name, n_in, out_shape, scratch):
    return pl.pallas_call(body, name=name, in_specs=[_ANY] * n_in, out_specs=[_ANY] * len(out_shape),
                          out_shape=out_shape, scratch_shapes=scratch)


def _remote(src, dst, send_sem, recv_sem, dev):
    return pltpu.make_async_remote_copy(src_ref=src, dst_ref=dst, send_sem=send_sem, recv_sem=recv_sem,
                                        device_id=dev, device_id_type=MESH)


def _gather_shards(shards):
    nt = len(shards)
    split = [sh.shape[0] % 16 == 0 for sh in shards]

    def body(*refs):
        ins, outs = refs[:nt], refs[nt:2 * nt]
        ici_s, ici_r, fwd_s, fwd_r, own_s, own_r = refs[2 * nt:]
        x, y, c = _mesh_pos()
        me = 2 * x + y
        sib = (x, y, 1 - c)
        peers = _chip_peers(x, y)

        def half(ref, t, who):
            if not split[t]:
                return ref
            hr = shards[t].shape[0] // 2
            return ref.at[pl.ds(pl.multiple_of(who * hr, 8), hr), :]

        pending = []
        for t in range(nt):
            own = _remote(ins[t], outs[t].at[me], own_s.at[t], own_r.at[t], sib)
            own.start()
            pending.append(own.wait)
            for k, (px, py) in enumerate(peers):
                rc = _remote(half(ins[t], t, c), half(outs[t].at[me], t, c), ici_s.at[t, k], ici_r.at[t, k], (px, py, c))
                rc.start()
                pending.append(rc.wait_send)
        for t in range(nt):
            for k, (px, py) in enumerate(peers):
                land = half(outs[t].at[2 * px + py], t, c)
                _remote(land, land, ici_s.at[t, k], ici_r.at[t, k], (px, py, c)).wait_recv()
                if split[t]:
                    fw = _remote(land, land, fwd_s.at[t, k], fwd_r.at[t, k], sib)
                    fw.start()
                    pending.append(fw.wait_send)
                    other = half(outs[t].at[2 * px + py], t, 1 - c)
                    pending.append(_remote(other, other, fwd_s.at[t, k], fwd_r.at[t, k], sib).wait_recv)
        for wait in pending:
            wait()

    out_shape = [_sds((CHIPS,) + sh.shape, sh.dtype) for sh in shards]
    dma = pltpu.SemaphoreType.DMA
    scratch = [dma((nt, 3)), dma((nt, 3)), dma((nt, 3)), dma((nt, 3)), dma((nt,)), dma((nt,))]
    return _comm_call(body, name="gather_weights", n_in=nt, out_shape=out_shape, scratch=scratch)(*shards)


def _pair_split(grads, name):
    nt = len(grads)

    def body(*refs):
        ins, got = refs[:nt], refs[nt:2 * nt]
        send_sems, recv_sems = refs[2 * nt:]
        x, y, c = _mesh_pos()
        cps = []
        for t in range(nt):
            hr = ins[t].shape[1] // 2
            give = ins[t].at[:, pl.ds(pl.multiple_of((1 - c) * hr, 8), hr), :]
            rc = _remote(give, got[t], send_sems.at[t], recv_sems.at[t], (x, y, 1 - c))
            rc.start()
            cps.append(rc)
        for rc in cps:
            rc.wait()

    half = [_sds((CHIPS, g.shape[1] // 2, g.shape[2]), g.dtype) for g in grads]
    scratch = [pltpu.SemaphoreType.DMA((nt,)), pltpu.SemaphoreType.DMA((nt,))]
    return _comm_call(body, name=name, n_in=nt, out_shape=half, scratch=scratch)(*grads)


def _chip_scatter(parts, name):
    nt = len(parts)

    def body(*refs):
        ins, outs = refs[:nt], refs[nt:2 * nt]
        send_sems, recv_sems = refs[2 * nt:]
        x, y, c = _mesh_pos()
        cps = []
        for t in range(nt):
            for k, (px, py) in enumerate(_chip_peers(x, y)):
                rc = _remote(ins[t].at[2 * px + py], outs[t].at[k], send_sems.at[t, k], recv_sems.at[t, k], (px, py, c))
                rc.start()
                cps.append(rc)
        for cp in cps:
            cp.wait()

    out_shape = [_sds((3,) + p.shape[1:], p.dtype) for p in parts]
    scratch = [pltpu.SemaphoreType.DMA((nt, 3)), pltpu.SemaphoreType.DMA((nt, 3))]
    return _comm_call(body, name=name, n_in=nt, out_shape=out_shape, scratch=scratch)(*parts)


def _pair_join(halves):
    nt = len(halves)

    def body(*refs):
        ins, got = refs[:nt], refs[nt:2 * nt]
        send_sems, recv_sems = refs[2 * nt:]
        x, y, c = _mesh_pos()
        cps = []
        for t in range(nt):
            rc = _remote(ins[t], got[t], send_sems.at[t], recv_sems.at[t], (x, y, 1 - c))
            rc.start()
            cps.append(rc)
        for rc in cps:
            rc.wait()

    out_shape = [_sds(hf.shape, hf.dtype) for hf in halves]
    scratch = [pltpu.SemaphoreType.DMA((nt,)), pltpu.SemaphoreType.DMA((nt,))]
    return _comm_call(body, name="grad_pair_join", n_in=nt, out_shape=out_shape, scratch=scratch)(*halves)


_HBM = pl.BlockSpec(memory_space=pltpu.HBM)
_SEMS = pl.BlockSpec(memory_space=pltpu.SEMAPHORE)
_EFFECT = pltpu.SideEffectType.DATAFLOW_SIDE_EFFECTING


def _bcast_copies(ins, lands, send_sems, recv_sems):
    x, y, c = _mesh_pos()
    me = 2 * x + y
    targets = [((px, py, c), 2 * px + py) for px, py in _chip_peers(x, y)] + [((x, y, 1 - c), me)]
    out = []
    for t in range(len(ins)):
        for k, (dev, idx) in enumerate(targets):
            i = t * len(targets) + k
            arrival = lambda t=t, i=i, idx=idx, dev=dev: _remote(ins[t], lands[t].at[idx], send_sems.at[i],
                                                                 recv_sems.at[i], dev)
            out.append((_remote(ins[t], lands[t].at[me], send_sems.at[i], recv_sems.at[i], dev), arrival))
    return out


def _scatter_copies(ins, lands, send_sems, recv_sems):
    x, y, c = _mesh_pos()
    out = []
    for t in range(len(ins)):
        for k, (px, py) in enumerate(_chip_peers(x, y)):
            i = t * 3 + k
            cp = _remote(ins[t].at[2 * px + py], lands[t].at[k], send_sems.at[i], recv_sems.at[i], (px, py, c))
            out.append((cp, lambda cp=cp: cp))
    return out


def _pair_copies(ins, lands, send_sems, recv_sems):
    x, y, c = _mesh_pos()
    out = []
    for t in range(len(ins)):
        hr = ins[t].shape[1] // 2
        give = ins[t].at[:, pl.ds(pl.multiple_of((1 - c) * hr, 8), hr), :]
        cp = _remote(give, lands[t], send_sems.at[t], recv_sems.at[t], (x, y, 1 - c))
        out.append((cp, lambda cp=cp: cp))
    return out


def _split_start(copies, srcs, land_shapes, ncopy, dep, name):
    nt = len(srcs)

    def body(*refs):
        ins, lands = refs[:nt], refs[nt:2 * nt]
        send_sems, recv_sems, token = refs[2 * nt + 1], refs[2 * nt + 2], refs[-1]
        for send, _ in copies(ins, lands, send_sems, recv_sems):
            send.start()
        token[...] = jnp.zeros_like(token)

    lands = [pltpu.with_memory_space_constraint(lax.empty(sh, a.dtype), pltpu.HBM) for sh, a in zip(land_shapes, srcs)]
    srcs = [pltpu.with_memory_space_constraint(a, pltpu.HBM) for a in srcs]
    dma = pltpu.SemaphoreType.DMA
    out_shape = ([dma((nt * ncopy,)), dma((nt * ncopy,))] + [pltpu.HBM(a.shape, a.dtype) for a in srcs + lands]
                 + [_sds((8, 128))])
    outs = pl.pallas_call(
        body, name=name, in_specs=[_HBM] * (2 * nt) + [_ANY],
        out_specs=[_SEMS, _SEMS] + [_HBM] * (2 * nt) + [pl.BlockSpec(memory_space=pltpu.VMEM)], out_shape=out_shape,
        input_output_aliases={i: 2 + i for i in range(2 * nt)},
        compiler_params=pltpu.CompilerParams(has_side_effects=_EFFECT))(*srcs, *lands, dep)
    return outs[0], outs[1], outs[2:2 + nt], outs[2 + nt:2 + 2 * nt], outs[-1]


def _split_wait(copies, send_sems, recv_sems, srcs, lands, after, name):
    nt = len(srcs)

    def body(*refs):
        ins, lnd = refs[:nt], refs[nt:2 * nt]
        for send, arrival in copies(ins, lnd, refs[2 * nt], refs[2 * nt + 1]):
            send.wait_send()
            arrival().wait_recv()

    outs = pl.pallas_call(
        body, name=name, in_specs=[_HBM] * (2 * nt) + [_SEMS, _SEMS, _ANY], out_specs=[_HBM] * (2 * nt),
        out_shape=[pltpu.HBM(a.shape, a.dtype) for a in list(srcs) + list(lands)],
        input_output_aliases={i: i for i in range(2 * nt)},
        compiler_params=pltpu.CompilerParams(has_side_effects=_EFFECT))(*srcs, *lands, send_sems, recv_sems, after)
    return outs[:nt], outs[nt:]


def _gather_small(packed):
    n = packed.shape[0]

    def body(in_ref, out_ref, send_sems, recv_sems, loc_sem):
        x, y, c = _mesh_pos()
        me = 4 * x + 2 * y + c
        lc = pltpu.make_async_copy(in_ref, out_ref.at[me], loc_sem)
        lc.start()
        peers = []
        for k in range(1, NDEV):
            px, py, pc = x ^ (k >> 2), y ^ ((k >> 1) & 1), c ^ (k & 1)
            rc = pltpu.make_async_remote_copy(src_ref=in_ref, dst_ref=out_ref.at[me], send_sem=send_sems.at[k - 1],
                                              recv_sem=recv_sems.at[k - 1], device_id=(px, py, pc), device_id_type=MESH)
            rc.start()
            peers.append((k, px, py, pc))
        lc.wait()
        for k, px, py, pc in peers:
            pltpu.make_async_remote_copy(src_ref=in_ref, dst_ref=out_ref.at[4 * px + 2 * py + pc],
                                         send_sem=send_sems.at[k - 1], recv_sem=recv_sems.at[k - 1],
                                         device_id=(px, py, pc), device_id_type=MESH).wait()

    scratch = [pltpu.SemaphoreType.DMA((NDEV - 1,)), pltpu.SemaphoreType.DMA((NDEV - 1,)), pltpu.SemaphoreType.DMA]
    return _comm_call(body, name="gather_small_grads", n_in=1, out_shape=[_sds((NDEV, n, 128))],
                      scratch=scratch)(packed)[0]


def _row_tile(r, c):
    t = r
    while t * c * 4 > (1 << 20) and t % 16 == 0:
        t //= 2
    return t


def _k_pair_add(full, got, name):
    g, r, c = full.shape
    hr = r // 2
    tr = _row_tile(hr, c)
    nh = hr // tr

    def body(a_ref, b_ref, o_ref):
        o_ref[...] = (a_ref[...] + b_ref[...]).astype(_WIRE)

    mine = pl.BlockSpec((None, tr, c), lambda i, j: (i, lax.axis_index("c") * nh + j, 0))
    spec = pl.BlockSpec((None, tr, c), lambda i, j: (i, j, 0))
    return _pc(body, name=name, grid=(g, nh), in_specs=[mine, spec], out_specs=[spec],
               out_shape=[_sds((g, hr, c), _WIRE)])(full, got)[0]


def _k_chip_sum(parts, slots, name):
    _, r, c = parts.shape
    tr = _row_tile(r, c)

    def body(a_ref, s_ref, o_ref):
        acc = a_ref[...].astype(F32)
        for k in range(3):
            acc = acc + s_ref[k].astype(F32)
        o_ref[...] = acc

    own = pl.BlockSpec((None, tr, c), lambda i: (2 * lax.axis_index("x") + lax.axis_index("y"), i, 0))
    return _pc(body, name=name, grid=(r // tr,), in_specs=[own, pl.BlockSpec((3, tr, c), lambda i: (0, i, 0))],
               out_specs=[_row(tr, c)], out_shape=[_sds((r, c))])(parts, slots)[0]


def _adam(w, g, m, v):
    m = ADAM_B1 * m + (1.0 - ADAM_B1) * g
    v = ADAM_B2 * v + (1.0 - ADAM_B2) * (g * g)
    m_hat = m / (1.0 - ADAM_B1 ** ADAM_STEP)
    v_hat = v / (1.0 - ADAM_B2 ** ADAM_STEP)
    return -ADAM_LR * (m_hat / (jnp.sqrt(v_hat) + ADAM_EPS) + ADAM_WD * w), m, v


def _k_adam(w, mine, theirs, m, v, name):
    r, c = w.shape
    hr = r // 2
    tr = _row_tile(hr, c)
    nh = hr // tr

    def body(w_ref, a_ref, b_ref, m_ref, v_ref, g_ref, d_ref, mo_ref, vo_ref):
        upper = (pl.program_id(0) >= nh).astype(jnp.int32)
        g = jnp.where(upper == lax.axis_index("c"), a_ref[...], b_ref[...])
        g_ref[...] = g
        d_ref[...], mo_ref[...], vo_ref[...] = _adam(w_ref[...], g, m_ref[...], v_ref[...])

    hspec = pl.BlockSpec((tr, c), lambda i: (jnp.where(i >= nh, i - nh, i), 0))
    return _pc(body, name=name, grid=(r // tr,), in_specs=[_row(tr, c), hspec, hspec, _row(tr, c), _row(tr, c)],
               out_specs=[_row(tr, c)] * 4, out_shape=[_sds((r, c))] * 4)(w, mine, theirs, m, v)


def _k_sum8(a):
    _, n, _ = a.shape

    def body(a_ref, o_ref):
        acc = a_ref[0]
        for k in range(1, NDEV):
            acc = acc + a_ref[k]
        o_ref[...] = acc

    return _pc(body, name="sum_small_grads", grid=(1,), in_specs=[_acc(a.shape)], out_specs=[_acc((n, 128))],
               out_shape=[_sds((n, 128))])(a)[0]


def _k_adam_small(w, g, m, v):
    n = w.shape[0]

    def body(w_ref, g_ref, m_ref, v_ref, d_ref, mo_ref, vo_ref):
        d_ref[...], mo_ref[...], vo_ref[...] = _adam(w_ref[...], g_ref[...], m_ref[...], v_ref[...])

    return _pc(body, name="adam_small", grid=(1,), in_specs=[_acc((n, 128))] * 4, out_specs=[_acc((n, 128))] * 3,
               out_shape=[_sds((n, 128))] * 3)(w, g, m, v)


def _pack(vals):
    rows = []
    for a in vals:
        flat = a.reshape(-1)
        n = -(-flat.shape[0] // 1024) * 1024
        rows.append(jnp.pad(flat, (0, n - flat.shape[0])).reshape(n // 128, 128))
    return jnp.concatenate(rows, axis=0)


def _unpack(packed, shapes):
    out, off = [], 0
    for sh in shapes:
        size = int(np.prod(sh))
        n = -(-size // 1024) * 1024
        out.append(packed[off // 128:(off + n) // 128].reshape(-1)[:size].reshape(sh))
        off += n
    return out


_WEIGHTS = ["attn_norm", "w_in", "a_q_norm", "a_k_norm", "b_q_norm", "b_k_norm", "b_sinks", "mem_norm", "w_mem_kv",
            "m_q_norm", "m_k_norm", "w_o_a", "w_o_b", "w_o_m", "w_gate", "b_gate", "w_out", "ffn_norm", "w_up",
            "conv_w", "conv_b", "w_down"]
_BIG = ["w_in", "w_mem_kv", "w_o_a", "w_o_b", "w_o_m", "w_gate", "w_out", "w_up", "w_down"]
_SMALL = [n for n in _WEIGHTS if n not in _BIG]


def kernel(x, mem, positions, attn_norm, w_in, a_q_norm, a_k_norm, b_q_norm, b_k_norm, b_sinks, mem_norm, w_mem_kv, m_q_norm, m_k_norm, w_o_a, w_o_b, w_o_m, w_gate, b_gate, w_out, ffn_norm, w_up, conv_w, conv_b, w_down, loss_target, m_attn_norm, m_w_in, m_a_q_norm, m_a_k_norm, m_b_q_norm, m_b_k_norm, m_b_sinks, m_mem_norm, m_w_mem_kv, m_m_q_norm, m_m_k_norm, m_w_o_a, m_w_o_b, m_w_o_m, m_w_gate, m_b_gate, m_w_out, m_ffn_norm, m_w_up, m_conv_w, m_conv_b, m_w_down, v_attn_norm, v_w_in, v_a_q_norm, v_a_k_norm, v_b_q_norm, v_b_k_norm, v_b_sinks, v_mem_norm, v_w_mem_kv, v_m_q_norm, v_m_k_norm, v_w_o_a, v_w_o_b, v_w_o_m, v_w_gate, v_b_gate, v_w_out, v_ffn_norm, v_w_up, v_conv_w, v_conv_b, v_w_down):
    given = dict(locals())
    w = {n: given[n][0] for n in _WEIGHTS}
    m1 = {n: given["m_" + n][0] for n in _WEIGHTS}
    m2 = {n: given["v_" + n][0] for n in _WEIGHTS}

    w_in = _gather_shards([w["w_in"].astype(_MM)])[0]
    rest = [n for n in _BIG if n != "w_in"] + ["conv_w"]
    shards = [w[n] if n == "conv_w" else w[n].astype(_MM) for n in rest]
    g_send, g_recv, g_srcs, g_lands, tok = _split_start(_bcast_copies, shards, [(CHIPS,) + a.shape for a in shards], 4,
                                                        w_in, "gather_rest_start")
    small = {n: (w[n][None, :] if w[n].ndim == 1 else w[n]) for n in _SMALL if n != "conv_w"}
    small["attn_norm"] = small["attn_norm"] + tok[0:1, 0:1]

    def get_rest(after):
        wts = dict(zip(rest, _split_wait(_bcast_copies, g_send, g_recv, g_srcs, g_lands, after, "gather_rest_wait")[1]))
        for n in ("w_mem_kv", "w_out", "w_down"):
            wts[n] = wts[n].reshape(-1, wts[n].shape[-1])
        return wts

    parts, slots, pair, scat = {}, {}, [], []
    zeros = jnp.zeros((8, 128), F32)

    def finish_pair(after):
        names, tag, send, recv, srcs, lands = pair.pop()
        full, got = _split_wait(_pair_copies, send, recv, srcs, lands, after, "pair_wait_" + tag)
        mine = [_k_pair_add(f, b, "pair_add_" + n) for n, f, b in zip(names, full, got)]
        shapes = [(3,) + p.shape[1:] for p in mine]
        send, recv, srcs, lands, token = _split_start(_scatter_copies, mine, shapes, 3, zeros, "scatter_start_" + tag)
        scat.append((names, tag, send, recv, srcs, lands))
        return token

    def on_grads(group, after):
        names = list(group)
        tag = "_".join(names)
        token = finish_pair(after) if pair else zeros
        if not group:
            return token
        grads_g = [group[n] for n in names]
        if "w_in" in group:
            got = _pair_split(grads_g, "grad_pair_split_" + tag)
            for n, f, b in zip(names, grads_g, got):
                parts[n] = _k_pair_add(f, b, "pair_add_" + n)
            slots.update(zip(names, _chip_scatter([parts[n] for n in names], "grad_chip_scatter_" + tag)))
            return token
        shapes = [(CHIPS, g.shape[1] // 2, g.shape[2]) for g in grads_g]
        send, recv, srcs, lands, token = _split_start(_pair_copies, grads_g, shapes, 1, token, "pair_start_" + tag)
        pair.append((names, tag, send, recv, srcs, lands))
        return token

    loss, grad_x, sml = _local_step(x[0], mem[0], positions[0], loss_target[0], small, w_in, get_rest, on_grads)
    loss = lax.psum(loss, ("x", "y", "c"))
    for names, tag, send, recv, srcs, lands in scat:
        mine, got = _split_wait(_scatter_copies, send, recv, srcs, lands, slots["w_in"], "scatter_wait_" + tag)
        parts.update(zip(names, mine))
        slots.update(zip(names, got))
    mine = [_k_chip_sum(parts[n], slots[n], "chip_add_" + n) for n in _BIG]
    theirs = _pair_join(mine)
    grads = {}

    shapes = [sml[n].shape for n in _SMALL]
    gsm = dict(zip(_SMALL, _unpack(_k_sum8(_gather_small(_pack([sml[n] for n in _SMALL]))), shapes)))
    nu = w["conv_w"].shape[1]
    chip = 2 * lax.axis_index("x") + lax.axis_index("y")
    gsm["conv_w"] = lax.dynamic_slice_in_dim(gsm["conv_w"], chip * nu, nu, axis=1)
    for n in _SMALL:
        grads[n] = gsm[n].reshape(w[n].shape)

    delta, new_m, new_v = {}, {}, {}
    for n, a, b in zip(_BIG, mine, theirs):
        grads[n], delta[n], new_m[n], new_v[n] = _k_adam(w[n], a, b, m1[n], m2[n], "adam_" + n)
    pk = lambda d: _pack([d[n] for n in _SMALL])
    sshapes = [w[n].shape for n in _SMALL]
    for dst, packed in zip((delta, new_m, new_v), _k_adam_small(pk(w), pk(grads), pk(m1), pk(m2))):
        dst.update(zip(_SMALL, _unpack(packed, sshapes)))

    lead = lambda d: [d[n][None] for n in _WEIGHTS]
    return (loss, grad_x[None], *lead(grads), *lead(delta), *lead(new_m), *lead(new_v))
```

```python
import math

import jax
import jax.numpy as jnp
import numpy as np
from jax import lax
from jax.experimental import pallas as pl
from jax.experimental.pallas import tpu as pltpu

F32 = jnp.float32
_MM = jnp.bfloat16
_WIRE = jnp.bfloat16

D_MODEL = 1024
HEAD = 64
BLK = 128
A_GROUPS = ((128, 1), (512, 4), (2048, 16))
A_HEADS = 4
A_W = A_HEADS * HEAD
B_QH = 8
B_KVH = 2
B_WINDOW = 128
M_HEADS = 4
M_HD = 128
M_W = M_HEADS * M_HD
D_FF = 2816
EPS = 1e-6
NEG = -1e30
ROPE_THETA = 500000.0
ROPE_ROT = 16
CHIPS = 4
NDEV = 8
ADAM_LR, ADAM_B1, ADAM_B2, ADAM_EPS, ADAM_WD, ADAM_STEP = 0.001, 0.9, 0.999, 1e-08, 0.01, 10
VMEM_LIMIT = 58 * 1024 * 1024
MESH = pl.DeviceIdType.MESH


def _pc(body, *, name, grid, in_specs, out_specs, out_shape, scratch=()):
    return pl.pallas_call(
        body, name=name, grid=grid, in_specs=in_specs, out_specs=out_specs, out_shape=out_shape,
        scratch_shapes=list(scratch),
        compiler_params=pltpu.CompilerParams(dimension_semantics=("arbitrary",) * len(grid),
                                             vmem_limit_bytes=VMEM_LIMIT))


def _row(ts, c, col=0):
    return pl.BlockSpec((ts, c), lambda i: (i, col))


def _res(shape):
    n = len(shape)
    return pl.BlockSpec(tuple(shape), lambda i: (0,) * n, pipeline_mode=pl.Buffered(1))


def _acc(shape):
    n = len(shape)
    return pl.BlockSpec(tuple(shape), lambda i: (0,) * n)


def _sds(shape, dtype=F32):
    return jax.ShapeDtypeStruct(tuple(shape), dtype)


def _dot(a, b):
    return jnp.dot(a.astype(_MM), b.astype(_MM), preferred_element_type=F32)


def _dot_nt(a, b):
    return lax.dot_general(a.astype(_MM), b.astype(_MM), (((1,), (1,)), ((), ())), preferred_element_type=F32)


def _dot_tn(a, b):
    return lax.dot_general(a.astype(_MM), b.astype(_MM), (((0,), (0,)), ((), ())), preferred_element_type=F32)


def _sum8(v):
    ts, c = v.shape
    return jnp.sum(v.reshape(ts // 8, 8, c), axis=0)


def _sigmoid(z):
    return 1.0 / (1.0 + jnp.exp(-z))


def _rms(x):
    r = lax.rsqrt(jnp.mean(x * x, axis=-1, keepdims=True) + EPS)
    return x * r, r


def _rms_bwd(dy, xh, r, gain):
    z = dy * gain
    return r * (z - xh * jnp.mean(z * xh, axis=-1, keepdims=True))


def _split_hi_lo(v):
    hi = v.astype(_MM)
    return hi, (v - hi.astype(F32)).astype(_MM)


def _lane_head(shape):
    return lax.shift_right_logical(lax.broadcasted_iota(jnp.int32, shape, len(shape) - 1), 6)


def _seg_sum64(v):
    w = v.shape[1]
    e = jnp.where(_lane_head((w, w)) == lax.shift_right_logical(lax.broadcasted_iota(jnp.int32, (w, w), 0), 6),
                  1.0, 0.0).astype(_MM)
    hi, lo = _split_hi_lo(v)
    return jnp.dot(hi, e, preferred_element_type=F32) + jnp.dot(lo, e, preferred_element_type=F32)


def _seg_norm(x, seg):
    if seg == HEAD:
        r = lax.rsqrt(_seg_sum64(x * x) * (1.0 / HEAD) + EPS)
        return x * r, r
    w = x.shape[1]
    xh, rr = [], []
    for s in range(w // seg):
        xs = x[:, s * seg:(s + 1) * seg]
        r = lax.rsqrt(jnp.mean(xs * xs, axis=-1, keepdims=True) + EPS)
        xh.append(xs * r)
        rr.append(jnp.broadcast_to(r, xs.shape))
    return jnp.concatenate(xh, axis=1), jnp.concatenate(rr, axis=1)


def _seg_mean(v, seg):
    if seg == HEAD:
        return _seg_sum64(v) * (1.0 / HEAD)
    w = v.shape[1]
    out = []
    for s in range(w // seg):
        vs = v[:, s * seg:(s + 1) * seg]
        out.append(jnp.broadcast_to(jnp.mean(vs, axis=-1, keepdims=True), vs.shape))
    return jnp.concatenate(out, axis=1)


def _rope(t, c, sa, sb):
    out = []
    for cb in range(t.shape[1] // 128):
        tc = t[:, cb * 128:(cb + 1) * 128]
        out.append(tc * c + pltpu.roll(tc, 120, 1) * sa + pltpu.roll(tc, 8, 1) * sb)
    return jnp.concatenate(out, axis=1) if len(out) > 1 else out[0]


def _rope_bwd(dy, c, sa, sb):
    out = []
    for cb in range(dy.shape[1] // 128):
        dc = dy[:, cb * 128:(cb + 1) * 128]
        out.append(dc * c + pltpu.roll(dc * sa, 8, 1) + pltpu.roll(dc * sb, 120, 1))
    return jnp.concatenate(out, axis=1) if len(out) > 1 else out[0]


def _rope_consts():
    half = ROPE_ROT // 2
    c = np.float32(-2.0 * math.log(ROPE_THETA) / ROPE_ROT)
    freqs = np.exp(np.arange(half, dtype=np.float32) * c).astype(np.float32)
    place = np.zeros((3, half, 128), np.float32)
    ones = np.zeros((1, 128), np.float32)
    for lane in range(128):
        d = lane % HEAD
        if d < half:
            place[0, d, lane], place[1, d, lane] = 1.0, -1.0
        elif d < ROPE_ROT:
            place[0, d - half, lane], place[2, d - half, lane] = 1.0, 1.0
        else:
            ones[0, lane] = 1.0
    return np.tile(freqs[:, None], (1, 128)), place, ones


def _rope_tables(pos_rows):
    r = pos_rows.shape[0]
    tr = min(1024, r)
    freqs, place, ones = _rope_consts()

    def split3(v):
        hi, mid = _split_hi_lo(v)
        lo = (v - hi.astype(F32) - mid.astype(F32)).astype(_MM)
        return hi, mid, lo

    def body(p_ref, f_ref, e_ref, one_ref, c_ref, sa_ref, sb_ref):
        for j in range(tr // 128):
            ang = p_ref[j:j + 1, :].astype(F32) * f_ref[...]
            rows = slice(j * 128, (j + 1) * 128)
            for ref, k, v in ((c_ref, 0, jnp.cos(ang)), (sa_ref, 1, jnp.sin(ang)), (sb_ref, 2, jnp.sin(ang))):
                e = e_ref[k].astype(_MM)
                out = sum(_dot_tn(part, e) for part in split3(v))
                ref[rows, :] = out + one_ref[...] if k == 0 else out

    return _pc(body, name="rope_tables", grid=(r // tr,),
               in_specs=[pl.BlockSpec((tr // 128, 128), lambda i: (i, 0)), _acc((ROPE_ROT // 2, 128)),
                         _acc((3, ROPE_ROT // 2, 128)), _acc((1, 128))],
               out_specs=[_row(tr, 128)] * 3, out_shape=[_sds((r, 128))] * 3)(
                   pos_rows.reshape(r // 128, 128), jnp.asarray(freqs), jnp.asarray(place), jnp.asarray(ones))


def _k_in(x, g1, w_in):
    s = x.shape[0]
    ts = min(256, s)
    nin = w_in.shape[2]
    ncol = CHIPS * nin
    a_cols = 3 * A_W
    offs = [0, a_cols, 2 * a_cols, 3 * a_cols, 3 * a_cols + B_QH * HEAD,
            3 * a_cols + (B_QH + B_KVH) * HEAD, 3 * a_cols + (B_QH + 2 * B_KVH) * HEAD, ncol]

    def body(x_ref, g_ref, wi_ref, h_ref, a0, a1, a2, qb, kb, vb, mq, p_scr):
        xh, _ = _rms(x_ref[...])
        h = (xh * g_ref[...]).astype(_MM)
        h_ref[...] = h
        for j in range(CHIPS):
            p_scr[:, j * nin:(j + 1) * nin] = jnp.dot(h, wi_ref[j], preferred_element_type=F32)
        for k, ref in enumerate((a0, a1, a2, qb, kb, vb, mq)):
            ref[...] = p_scr[:, offs[k]:offs[k + 1]]

    widths = [offs[k + 1] - offs[k] for k in range(7)]
    return _pc(
        body, name="in_proj", grid=(s // ts,),
        in_specs=[_row(ts, D_MODEL), _res((1, D_MODEL)), _res(w_in.shape)],
        out_specs=[_row(ts, D_MODEL)] + [_row(ts, w) for w in widths],
        out_shape=[_sds((s, D_MODEL), _MM)] + [_sds((s, w)) for w in widths],
        scratch=[pltpu.VMEM((ts, ncol), F32)])(x, g1, w_in)


def _k_gate(h, w_gate, b_gate):
    s = h.shape[0]
    ts = min(256, s)
    ng = w_gate.shape[2]

    def body(h_ref, wg_ref, bg_ref, gt_ref):
        h = h_ref[...]
        for j in range(CHIPS):
            z = jnp.dot(h, wg_ref[j], preferred_element_type=F32) + bg_ref[:, j * ng:(j + 1) * ng]
            gt_ref[:, j * ng:(j + 1) * ng] = _sigmoid(z)

    return _pc(body, name="gate_proj", grid=(s // ts,),
               in_specs=[_row(ts, D_MODEL), _res(w_gate.shape), _res(b_gate.shape)],
               out_specs=[_row(ts, CHIPS * ng)], out_shape=[_sds((s, CHIPS * ng))])(h, w_gate, b_gate)[0]


def _k_prep(srcs, gq, gk, tabs, tab_row, *, wq, wk, rows_per_gain, name):
    rows = srcs[0][0].shape[0]
    ts = min(256, rows)

    def body(q_ref, k_ref, v_ref, gq_ref, gk_ref, c_ref, sa_ref, sb_ref, qn_ref, kn_ref, vn_ref):
        c, sa, sb = c_ref[...], sa_ref[...], sb_ref[...]
        qh, _ = _seg_norm(q_ref[...], HEAD)
        qn_ref[...] = _rope(qh * gq_ref[...], c, sa, sb).astype(_MM)
        kh, _ = _seg_norm(k_ref[...], HEAD)
        kn_ref[...] = _rope(kh * gk_ref[...], c, sa, sb).astype(_MM)
        vn_ref[...] = v_ref[...].astype(_MM)

    gspec = lambda w: pl.BlockSpec((None, 1, w), lambda i: ((i * ts) // rows_per_gain, 0, 0))
    return _pc(
        body, name=name, grid=(rows // ts,),
        in_specs=[_row(ts, wq, srcs[0][1]), _row(ts, wk, srcs[1][1]), _row(ts, wk, srcs[2][1]),
                  gspec(wq), gspec(wk)] + [pl.BlockSpec((ts, 128), lambda i: (i + tab_row // ts, 0))] * 3,
        out_specs=[_row(ts, wq), _row(ts, wk), _row(ts, wk)],
        out_shape=[_sds((rows, wq), _MM), _sds((rows, wk), _MM), _sds((rows, wk), _MM)])(
            srcs[0][0], srcs[1][0], srcs[2][0], gq, gk, *tabs)


def _first_flag(b, segs, nb):
    first = b >= nb
    for k, (start, period) in enumerate(segs):
        end = segs[k + 1][0] if k + 1 < len(segs) else nb
        first = first | ((b >= start) & (b < end) & (lax.rem(b - start, jnp.int32(period)) == 0))
    return first


def _band_bias(thr, with_cur):
    qi = lax.broadcasted_iota(jnp.int32, (BLK, BLK), 0)
    kj = lax.broadcasted_iota(jnp.int32, (BLK, BLK), 1)
    prev = jnp.where(kj >= qi + thr, 0.0, NEG)
    return jnp.concatenate([prev, jnp.where(kj <= qi, 0.0, NEG)], axis=1) if with_cur else prev


def _blockdiag(t4):
    head = _lane_head((1, A_W))
    return jnp.concatenate([t4 * jnp.where(head == h, 1.0, 0.0).astype(t4.dtype) for h in range(A_HEADS)], axis=0)


def _fold_diag(t, n):
    head = _lane_head((n, A_W))
    out = t[3 * n:4 * n]
    for h in (2, 1, 0):
        out = jnp.where(head == h, t[h * n:(h + 1) * n], out)
    return out


def _expand_heads(cols):
    n = cols[0].shape[0]
    head = _lane_head((n, A_W))
    out = jnp.broadcast_to(cols[3], (n, A_W))
    for h in (2, 1, 0):
        out = jnp.where(head == h, cols[h], out)
    return out


def _unit_kv(p_ref, c_ref, u, shared):
    if not shared:
        return jnp.concatenate([p_ref[:, u * A_W:(u + 1) * A_W], c_ref[:, u * A_W:(u + 1) * A_W]], axis=0)
    kg = jnp.concatenate([p_ref[:, u * HEAD:(u + 1) * HEAD], c_ref[:, u * HEAD:(u + 1) * HEAD]], axis=0)
    return jnp.concatenate([kg] * A_HEADS, axis=1)


def _k_band_fwd(qn, kn, vn, *, hq, hk, max_dist, segs, sink, name):
    rows = qn.shape[0]
    nb = rows // BLK
    units = hq // A_HEADS
    shared = hk != hq
    wq, wk = hq * HEAD, hk * HEAD
    scale = HEAD ** -0.5

    def body(*refs):
        if sink is None:
            q_ref, kc_ref, kp_ref, vc_ref, vp_ref, o_ref, l_ref = refs
        else:
            q_ref, kc_ref, kp_ref, vc_ref, vp_ref, sk_ref, o_ref, l_ref = refs
        b = pl.program_id(0)
        bias = _band_bias(jnp.where(_first_flag(b, segs, nb), 1 << 20, BLK - max_dist), True)
        for u in range(units):
            us = slice(u * A_W, (u + 1) * A_W)
            kb = _blockdiag(_unit_kv(kp_ref, kc_ref, u, shared))
            vb = _blockdiag(_unit_kv(vp_ref, vc_ref, u, shared))
            s_all = _dot_nt(q_ref[:, us], kb) * scale
            ps, ls = [], []
            for h in range(A_HEADS):
                s = s_all[:, h * 2 * BLK:(h + 1) * 2 * BLK] + bias
                m = jnp.max(s, axis=-1, keepdims=True)
                e = jnp.exp(s - m)
                lse = m + jnp.log(jnp.sum(e, axis=-1, keepdims=True))
                if sink is not None:
                    sk = sk_ref[u * A_HEADS + h]
                    mx = jnp.maximum(lse, sk)
                    lse = mx + jnp.log(jnp.exp(lse - mx) + jnp.exp(sk - mx))
                ps.append((e * jnp.exp(m - lse)).astype(_MM))
                ls.append(lse)
            o_ref[:, us] = _dot(jnp.concatenate(ps, axis=1), vb)
            l_ref[:, us] = _expand_heads(ls)

    cur = lambda w: pl.BlockSpec((BLK, w), lambda i: (i, 0))
    prev = lambda w: pl.BlockSpec((BLK, w), lambda i: (jnp.maximum(i - 1, 0), 0))
    in_specs = [cur(wq), cur(wk), prev(wk), cur(wk), prev(wk)]
    args = [qn, kn, kn, vn, vn]
    if sink is not None:
        in_specs.append(pl.BlockSpec(memory_space=pltpu.SMEM))
        args.append(sink)
    return _pc(body, name=name, grid=(nb,), in_specs=in_specs, out_specs=[cur(wq), cur(wq)],
               out_shape=[_sds((rows, wq)), _sds((rows, wq))])(*args)


def _k_memkv(mem, mem_norm, w_kv, m_k_norm):
    n = mem.shape[0]

    def body(m_ref, g_ref, w_ref, gk_ref, mn_ref, kv_ref, mk_ref, mv_ref):
        mh, _ = _rms(m_ref[...])
        mn = (mh * g_ref[...]).astype(_MM)
        mn_ref[...] = mn
        kv = jnp.dot(mn, w_ref[...], preferred_element_type=F32)
        kv_ref[...] = kv
        kh, _ = _seg_norm(kv[:, :M_W], M_HD)
        mk_ref[...] = (kh * gk_ref[...]).astype(_MM)
        mv_ref[...] = kv[:, M_W:].astype(_MM)

    return _pc(body, name="mem_kv", grid=(1,),
               in_specs=[_acc((n, D_MODEL)), _acc((1, D_MODEL)), _acc(w_kv.shape), _acc((1, M_W))],
               out_specs=[_acc((n, D_MODEL)), _acc((n, 2 * M_W)), _acc((n, M_W)), _acc((n, M_W))],
               out_shape=[_sds((n, D_MODEL), _MM), _sds((n, 2 * M_W)), _sds((n, M_W), _MM), _sds((n, M_W), _MM)])(
                   mem, mem_norm, w_kv, m_k_norm)


def _mem_probs(q, mk):
    sc = _dot_nt(q, mk) * (M_HD ** -0.5)
    e = jnp.exp(sc - jnp.max(sc, axis=-1, keepdims=True))
    return e / jnp.sum(e, axis=-1, keepdims=True)


def _k_mem_fwd(m_q, gq, mk, mv):
    s = m_q.shape[0]
    n = mk.shape[0]
    ts = min(256, s)

    def body(q_ref, g_ref, mk_ref, mv_ref, o_ref):
        qh, _ = _seg_norm(q_ref[...], M_HD)
        qn = (qh * g_ref[...]).astype(_MM)
        for h in range(M_HEADS):
            hs = slice(h * M_HD, (h + 1) * M_HD)
            o_ref[:, hs] = _dot(_mem_probs(qn[:, hs], mk_ref[:, hs]), mv_ref[:, hs])

    return _pc(body, name="mem_attn", grid=(s // ts,),
               in_specs=[_row(ts, M_W), _res((1, M_W)), _res((n, M_W)), _res((n, M_W))],
               out_specs=[_row(ts, M_W)], out_shape=[_sds((s, M_W))])(m_q, gq, mk, mv)[0]


def _group_weights(l0, l1, l2):
    m = jnp.maximum(jnp.maximum(l0, l1), l2)
    e0, e1, e2 = jnp.exp(l0 - m), jnp.exp(l1 - m), jnp.exp(l2 - m)
    inv = 1.0 / (e0 + e1 + e2)
    return e0 * inv, e1 * inv, e2 * inv


def _branch_products(oa, ob, om, woa_ref, wob_ref, wom_ref, j):
    return _dot(oa, woa_ref[j]), _dot(ob, wob_ref[j]), _dot(om, wom_ref[j])


def _k_merge(og, lg, o_b, o_m, gates, x, w_oa, w_ob, w_om, w_out, g2):
    s = x.shape[0]
    ts = min(256, s)
    nc = w_oa.shape[2]

    def body(o0, o1, o2, l0, l1, l2, ob_ref, om_ref, gt_ref, x_ref, woa, wob, wom, wout, g_ref,
             oa_ref, mer_ref, x1_ref, h2_ref, m_scr):
        w0, w1, w2 = _group_weights(l0[...], l1[...], l2[...])
        oa = w0 * o0[...] + w1 * o1[...] + w2 * o2[...]
        oa_ref[...] = oa
        ob, om = ob_ref[...], om_ref[...]
        for j in range(CHIPS):
            pa, pb, pm = _branch_products(oa, ob, om, woa, wob, wom, j)
            cs = lambda br: slice(br * D_MODEL + j * nc, br * D_MODEL + (j + 1) * nc)
            m_scr[:, j * nc:(j + 1) * nc] = gt_ref[:, cs(0)] * pa + gt_ref[:, cs(1)] * pb + gt_ref[:, cs(2)] * pm
        mer = m_scr[...].astype(_MM)
        mer_ref[...] = mer
        x1 = x_ref[...] + jnp.dot(mer, wout[...], preferred_element_type=F32)
        x1_ref[...] = x1
        xh, _ = _rms(x1)
        h2_ref[...] = (xh * g_ref[...]).astype(_MM)

    return _pc(
        body, name="merge_out", grid=(s // ts,),
        in_specs=[_row(ts, A_W)] * 6 + [_row(ts, B_QH * HEAD), _row(ts, M_W), _row(ts, 3 * D_MODEL), _row(ts, D_MODEL),
                                         _res(w_oa.shape), _res(w_ob.shape), _res(w_om.shape), _res(w_out.shape),
                                         _res((1, D_MODEL))],
        out_specs=[_row(ts, A_W), _row(ts, D_MODEL), _row(ts, D_MODEL), _row(ts, D_MODEL)],
        out_shape=[_sds((s, A_W)), _sds((s, D_MODEL), _MM), _sds((s, D_MODEL)), _sds((s, D_MODEL), _MM)],
        scratch=[pltpu.VMEM((ts, D_MODEL), F32)])(*og, *lg, o_b, o_m, gates, x, w_oa, w_ob, w_om, w_out, g2)


def _k_up(h2, w_up):
    s = h2.shape[0]
    ts = min(256, s)
    nu = w_up.shape[2]

    def body(h_ref, w_ref, u_ref):
        h = h_ref[...]
        for j in range(CHIPS):
            u_ref[:, j * nu:(j + 1) * nu] = jnp.dot(h, w_ref[j], preferred_element_type=F32)

    return _pc(body, name="up_proj", grid=(s // ts,), in_specs=[_row(ts, D_MODEL), _res(w_up.shape)],
               out_specs=[_row(ts, CHIPS * nu)], out_shape=[_sds((s, CHIPS * nu))])(h2, w_up)[0]


def _shift_down(v, halo, k):
    ts = v.shape[0]
    row = lax.broadcasted_iota(jnp.int32, v.shape, 0)
    out = pltpu.roll(v, k, 0)
    for r in range(k):
        out = jnp.where(row == r, halo[8 - k + r:8 - k + r + 1, :], out)
    return out


def _shift_up(v, halo, k):
    ts = v.shape[0]
    row = lax.broadcasted_iota(jnp.int32, v.shape, 0)
    out = pltpu.roll(v, ts - k, 0)
    for r in range(k):
        out = jnp.where(row == ts - k + r, halo[r:r + 1, :], out)
    return out


def _k_ffn(u, conv_w, conv_b, w_down, x1, target):
    s = u.shape[0]
    ts = min(128, s)
    nu = conv_w.shape[2]
    half = CHIPS // 2

    def body(u_ref, uh_ref, cw_ref, cb_ref, wd_ref, x1_ref, t_ref, dy_ref, f_ref, dc_ref, loss_ref, c_scr, f_scr):
        i = pl.program_id(0)
        halo = jnp.where(i > 0, uh_ref[...], 0.0)
        for j in range(CHIPS):
            cs = slice(j * nu, (j + 1) * nu)
            uj = u_ref[:, cs]
            hj = halo[:, cs]
            c_scr[:, cs] = (cb_ref[:, cs] + cw_ref[j, 0:1, :] * _shift_down(uj, hj, 2)
                            + cw_ref[j, 1:2, :] * _shift_down(uj, hj, 1) + cw_ref[j, 2:3, :] * uj)
        for j in range(half):
            a = c_scr[:, j * nu:(j + 1) * nu]
            g = c_scr[:, (half + j) * nu:(half + j + 1) * nu]
            f_scr[:, j * nu:(j + 1) * nu] = (a * _sigmoid(a) * g).astype(_MM)
        f = f_scr[...]
        f_ref[...] = f
        y = x1_ref[...] + jnp.dot(f, wd_ref[...], preferred_element_type=F32)
        err = y - t_ref[...]
        dy = err * (1.0 / D_MODEL)
        dy_ref[...] = dy

        @pl.when(i == 0)
        def _():
            loss_ref[...] = jnp.zeros_like(loss_ref)

        loss_ref[...] += _sum8(err * err)
        df = _dot_nt(dy, wd_ref[...])
        for j in range(half):
            a = c_scr[:, j * nu:(j + 1) * nu]
            g = c_scr[:, (half + j) * nu:(half + j + 1) * nu]
            sa = _sigmoid(a)
            dfj = df[:, j * nu:(j + 1) * nu]
            dc_ref[:, j * nu:(j + 1) * nu] = dfj * g * (sa * (1.0 + a * (1.0 - sa)))
            dc_ref[:, (half + j) * nu:(half + j + 1) * nu] = dfj * (a * sa)

    wide = CHIPS * nu
    return _pc(
        body, name="conv_ffn", grid=(s // ts,),
        in_specs=[_row(ts, wide), pl.BlockSpec((8, wide), lambda i: (jnp.maximum(i * (ts // 8) - 1, 0), 0)),
                  _res(conv_w.shape), _res((1, wide)), _res(w_down.shape), _row(ts, D_MODEL), _row(ts, D_MODEL)],
        out_specs=[_row(ts, D_MODEL), _row(ts, D_FF), _row(ts, wide), _acc((8, D_MODEL))],
        out_shape=[_sds((s, D_MODEL)), _sds((s, D_FF), _MM), _sds((s, wide)), _sds((8, D_MODEL))],
        scratch=[pltpu.VMEM((ts, wide), F32), pltpu.VMEM((ts, D_FF), _MM)])(u, u, conv_w, conv_b, w_down, x1, target)


def _k_conv_bwd(dc, u, conv_w, w_up, x1, g2, dy):
    s = u.shape[0]
    ts = min(128, s)
    nu = conv_w.shape[2]
    wide = CHIPS * nu
    last = s // ts - 1

    def body(dc_ref, dn_ref, u_ref, uh_ref, cw_ref, wu_ref, x1_ref, g_ref, dy_ref,
             dx1_ref, du_ref, cacc_ref, gacc_ref):
        i = pl.program_id(0)

        @pl.when(i == 0)
        def _():
            cacc_ref[...] = jnp.zeros_like(cacc_ref)
            gacc_ref[...] = jnp.zeros_like(gacc_ref)

        uhalo = jnp.where(i > 0, uh_ref[...], 0.0)
        dhalo = jnp.where(i < last, dn_ref[...], 0.0)
        dh2 = jnp.zeros((ts, D_MODEL), F32)
        for j in range(CHIPS):
            cs = slice(j * nu, (j + 1) * nu)
            dcj, uj = dc_ref[:, cs], u_ref[:, cs]
            cacc_ref[0, :, cs] += _sum8(dcj)
            cacc_ref[1, :, cs] += _sum8(dcj * _shift_down(uj, uhalo[:, cs], 2))
            cacc_ref[2, :, cs] += _sum8(dcj * _shift_down(uj, uhalo[:, cs], 1))
            cacc_ref[3, :, cs] += _sum8(dcj * uj)
            du = (cw_ref[j, 2:3, :] * dcj + cw_ref[j, 1:2, :] * _shift_up(dcj, dhalo[:, cs], 1)
                  + cw_ref[j, 0:1, :] * _shift_up(dcj, dhalo[:, cs], 2)).astype(_MM)
            du_ref[:, cs] = du
            dh2 = dh2 + _dot_nt(du, wu_ref[j])
        xh, r = _rms(x1_ref[...])
        gacc_ref[...] += _sum8(dh2 * xh)
        dx1_ref[...] = dy_ref[...] + _rms_bwd(dh2, xh, r, g_ref[...])

    return _pc(
        body, name="conv_up_bwd", grid=(s // ts,),
        in_specs=[_row(ts, wide),
                  pl.BlockSpec((8, wide), lambda i: (jnp.minimum((i + 1) * (ts // 8), s // 8 - 1), 0)),
                  _row(ts, wide), pl.BlockSpec((8, wide), lambda i: (jnp.maximum(i * (ts // 8) - 1, 0), 0)),
                  _res(conv_w.shape), _res(w_up.shape), _row(ts, D_MODEL), _res((1, D_MODEL)), _row(ts, D_MODEL)],
        out_specs=[_row(ts, D_MODEL), _row(ts, wide), _acc((4, 8, wide)), _acc((8, D_MODEL))],
        out_shape=[_sds((s, D_MODEL)), _sds((s, wide), _MM), _sds((4, 8, wide)), _sds((8, D_MODEL))])(
            dc, dc, u, u, conv_w, w_up, x1, g2, dy)


def _k_merge_bwd(dx1, og, lg, o_a, o_b, o_m, gates, w_oa, w_ob, w_om, w_out, dep):
    s = dx1.shape[0]
    ts = min(256, s)
    nc = w_oa.shape[2]

    def body(dx_ref, o0, o1, o2, l0, l1, l2, oa_ref, ob_ref, om_ref, gt_ref, woa, wob, wom, wout, dep_ref,
             dgp_ref, dpa_ref, dpb_ref, dpm_ref, dog0, dog1, dog2, dl0, dl1, dl2, dob_ref, dom_ref, bacc_ref):
        i = pl.program_id(0)

        @pl.when(i == 0)
        def _():
            bacc_ref[...] = jnp.zeros_like(bacc_ref)

        dmer = _dot_nt(dx_ref[...], wout[...])
        oa, ob, om = oa_ref[...], ob_ref[...], om_ref[...]
        doa = jnp.zeros((ts, A_W), F32)
        dob = jnp.zeros((ts, B_QH * HEAD), F32)
        dom = jnp.zeros((ts, M_W), F32)
        for j in range(CHIPS):
            prods = _branch_products(oa, ob, om, woa, wob, wom, j)
            dmj = dmer[:, j * nc:(j + 1) * nc]
            dps = []
            for br, (p, dref) in enumerate(zip(prods, (dpa_ref, dpb_ref, dpm_ref))):
                cs = slice(br * D_MODEL + j * nc, br * D_MODEL + (j + 1) * nc)
                gt = gt_ref[:, cs]
                dgp = dmj * p * gt * (1.0 - gt)
                dgp_ref[:, cs] = dgp.astype(_MM)
                bacc_ref[:, cs] += _sum8(dgp)
                dp = (dmj * gt).astype(_MM)
                dref[:, j * nc:(j + 1) * nc] = dp
                dps.append(dp)
            doa = doa + _dot_nt(dps[0], woa[j])
            dob = dob + _dot_nt(dps[1], wob[j])
            dom = dom + _dot_nt(dps[2], wom[j])
        dob_ref[...] = dob
        dom_ref[...] = dom
        ws = _group_weights(l0[...], l1[...], l2[...])
        dsum = _seg_mean(doa * oa, HEAD) * float(HEAD)
        for w, dref, lref in zip(ws, (dog0, dog1, dog2), (dl0, dl1, dl2)):
            dref[...] = w * doa
            lref[...] = w * dsum

    return _pc(
        body, name="merge_out_bwd", grid=(s // ts,),
        in_specs=[_row(ts, D_MODEL)] + [_row(ts, A_W)] * 7 + [_row(ts, B_QH * HEAD), _row(ts, M_W), _row(ts, 3 * D_MODEL),
                                                              _res(w_oa.shape), _res(w_ob.shape), _res(w_om.shape),
                                                              _res(w_out.shape), _res((8, 128))],
        out_specs=[_row(ts, 3 * D_MODEL)] + [_row(ts, D_MODEL)] * 3 + [_row(ts, A_W)] * 6
        + [_row(ts, B_QH * HEAD), _row(ts, M_W), _acc((8, 3 * D_MODEL))],
        out_shape=[_sds((s, 3 * D_MODEL), _MM)] + [_sds((s, D_MODEL), _MM)] * 3 + [_sds((s, A_W))] * 6
        + [_sds((s, B_QH * HEAD)), _sds((s, M_W)), _sds((8, 3 * D_MODEL))])(
            dx1, *og, *lg, o_a, o_b, o_m, gates, w_oa, w_ob, w_om, w_out, dep)


def _k_mem_bwd(m_q, gq, mk, mv, o_m, do_m):
    s = m_q.shape[0]
    n = mk.shape[0]
    ts = min(256, s)
    scale = M_HD ** -0.5

    def body(q_ref, g_ref, mk_ref, mv_ref, o_ref, do_ref, dq_ref, dmk_ref, dmv_ref, gacc_ref):
        i = pl.program_id(0)

        @pl.when(i == 0)
        def _():
            dmk_ref[...] = jnp.zeros_like(dmk_ref)
            dmv_ref[...] = jnp.zeros_like(dmv_ref)
            gacc_ref[...] = jnp.zeros_like(gacc_ref)

        gain = g_ref[...]
        qh, r = _seg_norm(q_ref[...], M_HD)
        qn = (qh * gain).astype(_MM)
        do = do_ref[...]
        delta = _seg_mean(do * o_ref[...], M_HD) * float(M_HD)
        dqn = []
        for h in range(M_HEADS):
            hs = slice(h * M_HD, (h + 1) * M_HD)
            p = _mem_probs(qn[:, hs], mk_ref[:, hs])
            dp = _dot_nt(do[:, hs], mv_ref[:, hs])
            ds = (p * (dp - delta[:, hs][:, 0:1]) * scale).astype(_MM)
            dqn.append(_dot(ds, mk_ref[:, hs]))
            dmk_ref[:, hs] += _dot_tn(ds, qn[:, hs])
            dmv_ref[:, hs] += _dot_tn(p, do[:, hs])
        dqn = jnp.concatenate(dqn, axis=1)
        gacc_ref[...] += _sum8(dqn * qh)
        z = dqn * gain
        dq_ref[...] = (r * (z - qh * _seg_mean(z * qh, M_HD))).astype(_MM)

    return _pc(
        body, name="mem_attn_bwd", grid=(s // ts,),
        in_specs=[_row(ts, M_W), _res((1, M_W)), _res((n, M_W)), _res((n, M_W)), _row(ts, M_W), _row(ts, M_W)],
        out_specs=[_row(ts, M_W), _acc((n, M_W)), _acc((n, M_W)), _acc((8, M_W))],
        out_shape=[_sds((s, M_W), _MM), _sds((n, M_W)), _sds((n, M_W)), _sds((8, M_W))])(m_q, gq, mk, mv, o_m, do_m)


def _k_memkv_bwd(mem, mem_norm, w_kv, m_k_norm, mem_n, kv, dmk, dmv):
    n = mem.shape[0]

    def body(m_ref, g_ref, w_ref, gk_ref, mn_ref, kv_ref, dmk_ref, dmv_ref, dw_ref, dg_ref, dgk_ref):
        gk = gk_ref[...]
        kh, r = _seg_norm(kv_ref[:, :M_W], M_HD)
        dmk = dmk_ref[...]
        dgk_ref[...] = _sum8(dmk * kh)
        z = dmk * gk
        dk = r * (z - kh * _seg_mean(z * kh, M_HD))
        dkv = jnp.concatenate([dk, dmv_ref[...]], axis=1).astype(_MM)
        dw_ref[...] = _dot_tn(mn_ref[...], dkv)
        dmn = _dot_nt(dkv, w_ref[...])
        mh, _ = _rms(m_ref[...])
        dg_ref[...] = _sum8(dmn * mh)

    return _pc(body, name="mem_kv_bwd", grid=(1,),
               in_specs=[_acc((n, D_MODEL)), _acc((1, D_MODEL)), _acc(w_kv.shape), _acc((1, M_W)), _acc((n, D_MODEL)),
                         _acc((n, 2 * M_W)), _acc((n, M_W)), _acc((n, M_W))],
               out_specs=[_acc(w_kv.shape), _acc((8, D_MODEL)), _acc((8, M_W))],
               out_shape=[_sds(w_kv.shape), _sds((8, D_MODEL)), _sds((8, M_W))])(
                   mem, mem_norm, w_kv, m_k_norm, mem_n, kv, dmk, dmv)


def _k_band_bwd(qn, kn, vn, do, lse, dl_or_o, *, hq, hk, max_dist, segs, sink, name):
    rows = qn.shape[0]
    nb = rows // BLK
    units = hq // A_HEADS
    shared = hk != hq
    wq, wk = hq * HEAD, hk * HEAD
    scale = HEAD ** -0.5

    def body(*refs):
        (qb_ref, qx_ref, kb_ref, kp_ref, vb_ref, vp_ref, dob_ref, dox_ref, lb_ref, lx_ref, eb_ref, ex_ref) = refs[:12]
        if sink is None:
            dq_ref, dk_ref, dv_ref = refs[12:]
        else:
            sk_ref, dq_ref, dk_ref, dv_ref, sacc_ref = refs[12:]
        b = pl.program_id(0)
        bias1 = _band_bias(jnp.where(_first_flag(b, segs, nb), 1 << 20, BLK - max_dist), True)
        bias2 = _band_bias(jnp.where(_first_flag(b + 1, segs, nb), 1 << 20, BLK - max_dist), False)
        if sink is not None:
            @pl.when(b == 0)
            def _():
                sacc_ref[...] = jnp.zeros_like(sacc_ref)

        for u in range(units):
            us = slice(u * A_W, (u + 1) * A_W)
            q4, qx4, do4, dox4 = qb_ref[:, us], qx_ref[:, us], dob_ref[:, us], dox_ref[:, us]
            k4, v4 = _unit_kv(kp_ref, kb_ref, u, shared), _unit_kv(vp_ref, vb_ref, u, shared)
            kd, vd = _blockdiag(k4), _blockdiag(v4)
            kdc, vdc = _blockdiag(k4[BLK:]), _blockdiag(v4[BLK:])
            if sink is None:
                dlt_b, dlt_x = eb_ref[:, us], ex_ref[:, us]
            else:
                dlt_b = _seg_sum64(do4.astype(F32) * eb_ref[:, us])
                dlt_x = _seg_sum64(dox4.astype(F32) * ex_ref[:, us])
            s1, dp1 = _dot_nt(q4, kd) * scale, _dot_nt(do4, vd)
            s2, dp2 = _dot_nt(qx4, kdc) * scale, _dot_nt(dox4, vdc)
            ds1, ds1c, p1c, ds2, p2 = [], [], [], [], []
            for h in range(A_HEADS):
                col = slice(u * A_W + h * HEAD, u * A_W + h * HEAD + 1)
                ucol = slice(h * HEAD, h * HEAD + 1)
                wide, narrow = slice(h * 2 * BLK, (h + 1) * 2 * BLK), slice(h * BLK, (h + 1) * BLK)
                l_b, l_x = lb_ref[:, col], lx_ref[:, col]
                p = jnp.exp(s1[:, wide] + bias1 - l_b)
                ds = p * (dp1[:, wide] - dlt_b[:, ucol]) * scale
                ds1.append(ds.astype(_MM))
                ds1c.append(ds[:, BLK:].astype(_MM))
                p1c.append(p[:, BLK:].astype(_MM))
                px = jnp.exp(s2[:, narrow] + bias2 - l_x)
                ds2.append((px * (dp2[:, narrow] - dlt_x[:, ucol]) * scale).astype(_MM))
                p2.append(px.astype(_MM))
                if sink is not None:
                    j = u * A_HEADS + h
                    sacc_ref[:, j:j + 1] += -jnp.exp(sk_ref[j] - l_b) * dlt_b[:, ucol]
            dq_ref[:, us] = _dot(jnp.concatenate(ds1, axis=1), kd)
            dk4 = _fold_diag(_dot_tn(jnp.concatenate(ds1c, axis=1), q4) + _dot_tn(jnp.concatenate(ds2, axis=1), qx4), BLK)
            dv4 = _fold_diag(_dot_tn(jnp.concatenate(p1c, axis=1), do4) + _dot_tn(jnp.concatenate(p2, axis=1), dox4), BLK)
            if shared:
                fold = lambda t: (t[:, 0:HEAD] + t[:, HEAD:2 * HEAD]) + (t[:, 2 * HEAD:3 * HEAD] + t[:, 3 * HEAD:])
                dk_ref[:, u * HEAD:(u + 1) * HEAD] = fold(dk4)
                dv_ref[:, u * HEAD:(u + 1) * HEAD] = fold(dv4).astype(_MM)
            else:
                dk_ref[:, us] = dk4
                dv_ref[:, us] = dv4.astype(_MM)

    cur = lambda w: pl.BlockSpec((BLK, w), lambda i: (i, 0))
    prev = lambda w: pl.BlockSpec((BLK, w), lambda i: (jnp.maximum(i - 1, 0), 0))
    nxt = lambda w: pl.BlockSpec((BLK, w), lambda i: (jnp.minimum(i + 1, nb - 1), 0))
    in_specs = [cur(wq), nxt(wq), cur(wk), prev(wk), cur(wk), prev(wk), cur(wq), nxt(wq), cur(wq), nxt(wq), cur(wq), nxt(wq)]
    args = [qn, qn, kn, kn, vn, vn, do, do, lse, lse, dl_or_o, dl_or_o]
    out_specs = [cur(wq), cur(wk), cur(wk)]
    out_shape = [_sds((rows, wq)), _sds((rows, wk)), _sds((rows, wk), _MM)]
    if sink is not None:
        in_specs.append(pl.BlockSpec(memory_space=pltpu.SMEM))
        args.append(sink)
        out_specs.append(_acc((BLK, 128)))
        out_shape.append(_sds((BLK, 128)))
    return _pc(body, name=name, grid=(nb,), in_specs=in_specs, out_specs=out_specs, out_shape=out_shape)(*args)


def _k_prep_bwd(srcs, dqn, dkn, gq, gk, tabs, tab_row, *, wq, wk, rows_per_gain, name):
    rows = dqn.shape[0]
    ts = min(256, rows)
    ngain = gq.shape[0]

    def body(q_ref, k_ref, dq_ref, dk_ref, gq_ref, gk_ref, c_ref, sa_ref, sb_ref, oq_ref, ok_ref, aq_ref, ak_ref):
        i = pl.program_id(0)

        @pl.when(lax.rem(i * ts, rows_per_gain) == 0)
        def _():
            aq_ref[...] = jnp.zeros_like(aq_ref)
            ak_ref[...] = jnp.zeros_like(ak_ref)

        c, sa, sb = c_ref[...], sa_ref[...], sb_ref[...]
        for x_ref, d_ref, g_ref, o_ref, a_ref in ((q_ref, dq_ref, gq_ref, oq_ref, aq_ref),
                                                   (k_ref, dk_ref, gk_ref, ok_ref, ak_ref)):
            xh, r = _seg_norm(x_ref[...], HEAD)
            dt = _rope_bwd(d_ref[...], c, sa, sb)
            a_ref[...] += _sum8(dt * xh)
            z = dt * g_ref[...]
            o_ref[...] = (r * (z - xh * _seg_mean(z * xh, HEAD))).astype(_MM)

    gspec = lambda w: pl.BlockSpec((None, 1, w), lambda i: ((i * ts) // rows_per_gain, 0, 0))
    aspec = lambda w: pl.BlockSpec((None, 8, w), lambda i: ((i * ts) // rows_per_gain, 0, 0))
    return _pc(
        body, name=name, grid=(rows // ts,),
        in_specs=[_row(ts, wq, srcs[0][1]), _row(ts, wk, srcs[1][1]), _row(ts, wq), _row(ts, wk), gspec(wq), gspec(wk)]
        + [pl.BlockSpec((ts, 128), lambda i: (i + tab_row // ts, 0))] * 3,
        out_specs=[_row(ts, wq), _row(ts, wk), aspec(wq), aspec(wk)],
        out_shape=[_sds((rows, wq), _MM), _sds((rows, wk), _MM), _sds((ngain, 8, wq)), _sds((ngain, 8, wk))])(
            srcs[0][0], srcs[1][0], dqn, dkn, gq, gk, *tabs)


def _k_in_bwd(pieces, dgp, x, g1, dx1, w_in, w_gate):
    s = x.shape[0]
    ts = min(256, s)
    nin, ng = w_in.shape[2], w_gate.shape[2]
    widths = [p.shape[1] for p in pieces]
    ncol = sum(widths)

    def body(*refs):
        p_refs = refs[:len(pieces)]
        dgp_ref, x_ref, g_ref, dx1_ref, wi_ref, wg_ref, gx_ref, dpj_ref, gacc_ref = refs[len(pieces):]
        i = pl.program_id(0)

        @pl.when(i == 0)
        def _():
            gacc_ref[...] = jnp.zeros_like(gacc_ref)

        off = 0
        for p_ref, w in zip(p_refs, widths):
            dpj_ref[:, off:off + w] = p_ref[...]
            off += w
        dh = jnp.zeros((ts, D_MODEL), F32)
        for j in range(CHIPS):
            dh = dh + _dot_nt(dpj_ref[:, j * nin:(j + 1) * nin], wi_ref[j])
            dh = dh + _dot_nt(dgp_ref[:, j * ng:(j + 1) * ng], wg_ref[j])
        xh, r = _rms(x_ref[...])
        gacc_ref[...] += _sum8(dh * xh)
        gx_ref[...] = dx1_ref[...] + _rms_bwd(dh, xh, r, g_ref[...])

    return _pc(
        body, name="in_proj_bwd", grid=(s // ts,),
        in_specs=[_row(ts, w) for w in widths] + [_row(ts, CHIPS * ng), _row(ts, D_MODEL), _res((1, D_MODEL)),
                                                  _row(ts, D_MODEL), _res(w_in.shape), _res(w_gate.shape)],
        out_specs=[_row(ts, D_MODEL), _row(ts, ncol), _acc((8, D_MODEL))],
        out_shape=[_sds((s, D_MODEL)), _sds((s, ncol), _MM), _sds((8, D_MODEL))])(*pieces, dgp, x, g1, dx1, w_in, w_gate)


def _k_wgrad(a, b, *, nblk, stacked, name):
    s, k = a.shape
    n = b.shape[1]
    nb = n // nblk
    ts = min(1024, s)

    def body(a_ref, b_ref, o_ref):
        @pl.when(pl.program_id(1) == 0)
        def _():
            o_ref[...] = jnp.zeros_like(o_ref)

        o_ref[...] += _dot_tn(a_ref[...], b_ref[...])

    if stacked:
        out_spec, out_shape = pl.BlockSpec((None, k, nb), lambda g, t: (g, 0, 0)), _sds((nblk, k, nb))
    else:
        out_spec, out_shape = pl.BlockSpec((k, nb), lambda g, t: (0, g)), _sds((k, n))
    return _pc(body, name=name, grid=(nblk, s // ts),
               in_specs=[pl.BlockSpec((ts, k), lambda g, t: (t, 0)), pl.BlockSpec((ts, nb), lambda g, t: (t, g))],
               out_specs=[out_spec], out_shape=[out_shape])(a, b)[0]


def _to_res(t, d):
    s, c = t.shape
    return t if d == 1 else t.reshape(s // d, d, c).transpose(1, 0, 2).reshape(s, c)


def _from_res(t, d):
    s, c = t.shape
    return t if d == 1 else t.reshape(d, s // d, c).transpose(1, 0, 2).reshape(s, c)


def _tile_gain(g, heads):
    return jnp.tile(g, (1,) * (g.ndim - 1) + (heads,))[..., None, :]


def _local_step(x, mem, pos, target, small, w_in, get_rest, on_grads):
    s = x.shape[0]
    nblk = s // BLK
    g1, g2 = small["attn_norm"], small["ffn_norm"]

    pos_rows = jnp.concatenate([_to_res(pos[:, None], d)[:, 0] for _, d in A_GROUPS] + [pos])
    tabs = _rope_tables(pos_rows)

    h, qa0, qa1, qa2, q_b, k_b, v_b, m_q = _k_in(x, g1, w_in)

    qkv_a = jnp.concatenate([_to_res(t, d) for t, (_, d) in zip((qa0, qa1, qa2), A_GROUPS)], axis=0)
    gq_a = _tile_gain(small["a_q_norm"], A_HEADS)
    gk_a = _tile_gain(small["a_k_norm"], A_HEADS)
    src_a = ((qkv_a, 0), (qkv_a, 1), (qkv_a, 2))
    qn_a, kn_a, vn_a = _k_prep(src_a, gq_a, gk_a, tabs, 0, wq=A_W, wk=A_W, rows_per_gain=s, name="prep_a")
    segs_a = tuple((gi * nblk, nblk // d) for gi, (_, d) in enumerate(A_GROUPS))
    o_res, l_res = _k_band_fwd(qn_a, kn_a, vn_a, hq=A_HEADS, hk=A_HEADS, max_dist=BLK, segs=segs_a, sink=None,
                               name="attn_a")
    og = [_from_res(o_res[gi * s:(gi + 1) * s], d) for gi, (_, d) in enumerate(A_GROUPS)]
    lg = [_from_res(l_res[gi * s:(gi + 1) * s], d) for gi, (_, d) in enumerate(A_GROUPS)]

    gq_b = _tile_gain(small["b_q_norm"], B_QH)
    gk_b = _tile_gain(small["b_k_norm"], B_KVH)
    src_b = ((q_b, 0), (k_b, 0), (v_b, 0))
    qn_b, kn_b, vn_b = _k_prep(src_b, gq_b, gk_b, tabs, 3 * s, wq=B_QH * HEAD, wk=B_KVH * HEAD, rows_per_gain=s,
                               name="prep_b")
    sink_x = small["b_sinks"][0]
    segs_b = ((0, nblk),)
    o_b, l_b = _k_band_fwd(qn_b, kn_b, vn_b, hq=B_QH, hk=B_KVH, max_dist=B_WINDOW - 1, segs=segs_b, sink=sink_x,
                           name="attn_b")

    wts = get_rest(0, o_b)
    gates = _k_gate(h, wts["w_gate"], small["b_gate"])

    gq_m = _tile_gain(small["m_q_norm"], M_HEADS)[0]
    gk_m = _tile_gain(small["m_k_norm"], M_HEADS)[0]
    mem_n, kv, mk, mv = _k_memkv(mem, small["mem_norm"], wts["w_mem_kv"], gk_m)
    o_m = _k_mem_fwd(m_q, gq_m, mk, mv)

    o_a, merged, x1, h2 = _k_merge(og, lg, o_b, o_m, gates, x, wts["w_o_a"], wts["w_o_b"], wts["w_o_m"],
                                   wts["w_out"], g2)
    wts.update(get_rest(1, x1))
    u = _k_up(h2, wts["w_up"])
    dy, f, dc, loss_acc = _k_ffn(u, wts["conv_w"], small["conv_b"], wts["w_down"], x1, target)
    loss = (0.5 / D_MODEL) * jnp.sum(loss_acc)

    dx1, du, cacc, g2acc = _k_conv_bwd(dc, u, wts["conv_w"], wts["w_up"], x1, g2, dy)
    tok = on_grads({"w_up": _k_wgrad(h2, du, nblk=CHIPS, stacked=True, name="dw_up"),
                    "w_down": _k_wgrad(f, dy, nblk=2, stacked=False, name="dw_down").reshape(CHIPS, -1, D_MODEL)}, dx1)
    (dgp, dp_a, dp_b, dp_m, dog0, dog1, dog2, dl0, dl1, dl2, do_b, do_m, bacc) = _k_merge_bwd(
        dx1, og, lg, o_a, o_b, o_m, gates, wts["w_o_a"], wts["w_o_b"], wts["w_o_m"], wts["w_out"], tok)
    tok = on_grads({"w_gate": _k_wgrad(h, dgp, nblk=CHIPS, stacked=True, name="dw_gate"),
                    "w_o_a": _k_wgrad(o_a, dp_a, nblk=CHIPS, stacked=True, name="dw_o_a"),
                    "w_o_b": _k_wgrad(o_b, dp_b, nblk=CHIPS, stacked=True, name="dw_o_b"),
                    "w_o_m": _k_wgrad(o_m, dp_m, nblk=CHIPS, stacked=True, name="dw_o_m"),
                    "w_out": _k_wgrad(merged, dx1, nblk=1, stacked=False, name="dw_out").reshape(CHIPS, -1, D_MODEL)},
                   do_m)

    dq_m, dmk, dmv, gqm_acc = _k_mem_bwd(m_q, gq_m + tok[0:1, 0:1], mk, mv, o_m, do_m)
    dw_kv, gmem_acc, gkm_acc = _k_memkv_bwd(mem, small["mem_norm"], wts["w_mem_kv"], gk_m, mem_n, kv, dmk, dmv)

    dq_bn, dk_bn, dv_b, sacc = _k_band_bwd(qn_b, kn_b, vn_b, do_b, l_b, o_b, hq=B_QH, hk=B_KVH,
                                           max_dist=B_WINDOW - 1, segs=segs_b, sink=sink_x, name="attn_b_bwd")
    tok = on_grads({}, dq_bn)
    dq_b, dk_b, gqb_acc, gkb_acc = _k_prep_bwd(src_b, dq_bn, dk_bn, gq_b + tok[0:1, 0:1], gk_b, tabs, 3 * s, wq=B_QH * HEAD,
                                               wk=B_KVH * HEAD, rows_per_gain=s, name="prep_b_bwd")

    do_res = jnp.concatenate([_to_res(t, d) for t, (_, d) in zip((dog0, dog1, dog2), A_GROUPS)], axis=0)
    dl_res = jnp.concatenate([_to_res(t, d) for t, (_, d) in zip((dl0, dl1, dl2), A_GROUPS)], axis=0)
    dq_an, dk_an, dv_a = _k_band_bwd(qn_a, kn_a, vn_a, do_res, l_res, dl_res, hq=A_HEADS, hk=A_HEADS, max_dist=BLK,
                                     segs=segs_a, sink=None, name="attn_a_bwd")
    dq_a, dk_a, gqa_acc, gka_acc = _k_prep_bwd(src_a, dq_an, dk_an, gq_a, gk_a, tabs, 0, wq=A_W, wk=A_W,
                                               rows_per_gain=s, name="prep_a_bwd")
    pieces = []
    for gi, (_, d) in enumerate(A_GROUPS):
        rs = slice(gi * s, (gi + 1) * s)
        pieces += [_from_res(t[rs], d) for t in (dq_a, dk_a, dv_a)]
    pieces += [dq_b, dk_b, dv_b, dq_m]
    grad_x, dproj, g1acc = _k_in_bwd(pieces, dgp, x, g1, dx1, w_in, wts["w_gate"])
    on_grads({"w_in": _k_wgrad(h, dproj, nblk=CHIPS, stacked=True, name="dw_in"),
              "w_mem_kv": dw_kv.reshape(CHIPS, -1, 2 * M_W)}, grad_x)

    def fold(acc, heads):
        v = jnp.sum(acc, axis=-2)
        return jnp.sum(v.reshape(v.shape[:-1] + (heads, -1)), axis=-2)

    csum = jnp.sum(cacc, axis=1)
    sml = {
        "attn_norm": jnp.sum(g1acc, axis=0), "a_q_norm": fold(gqa_acc, A_HEADS), "a_k_norm": fold(gka_acc, A_HEADS),
        "b_q_norm": fold(gqb_acc[0], B_QH), "b_k_norm": fold(gkb_acc[0], B_KVH),
        "b_sinks": jnp.sum(sacc, axis=0)[:B_QH], "mem_norm": jnp.sum(gmem_acc, axis=0),
        "m_q_norm": fold(gqm_acc, M_HEADS), "m_k_norm": fold(gkm_acc, M_HEADS),
        "b_gate": jnp.sum(bacc, axis=0), "ffn_norm": jnp.sum(g2acc, axis=0),
        "conv_w": csum[1:], "conv_b": csum[0],
    }
    return loss, grad_x, sml


def _mesh_pos():
    return lax.axis_index("x"), lax.axis_index("y"), lax.axis_index("c")


def _chip_peers(x, y):
    return [(1 - x, y), (x, 1 - y), (1 - x, 1 - y)]


_ANY = pl.BlockSpec(memory_space=pl.ANY)


def _comm_call(body, *, name, n_in, out_shape, scratch):
    return pl.pallas_call(body, name=name, in_specs=[_ANY] * n_in, out_specs=[_ANY] * len(out_shape),
                          out_shape=out_shape, scratch_shapes=scratch)


def _remote(src, dst, send_sem, recv_sem, dev):
    return pltpu.make_async_remote_copy(src_ref=src, dst_ref=dst, send_sem=send_sem, recv_sem=recv_sem,
                                        device_id=dev, device_id_type=MESH)


def _gather_shards(shards):
    nt = len(shards)
    split = [sh.shape[0] % 16 == 0 for sh in shards]

    def body(*refs):
        ins, outs = refs[:nt], refs[nt:2 * nt]
        ici_s, ici_r, fwd_s, fwd_r, own_s, own_r = refs[2 * nt:]
        x, y, c = _mesh_pos()
        me = 2 * x + y
        sib = (x, y, 1 - c)
        peers = _chip_peers(x, y)

        def half(ref, t, who):
            if not split[t]:
                return ref
            hr = shards[t].shape[0] // 2
            return ref.at[pl.ds(pl.multiple_of(who * hr, 8), hr), :]

        pending = []
        for t in range(nt):
            own = _remote(ins[t], outs[t].at[me], own_s.at[t], own_r.at[t], sib)
            own.start()
            pending.append(own.wait)
            for k, (px, py) in enumerate(peers):
                rc = _remote(half(ins[t], t, c), half(outs[t].at[me], t, c), ici_s.at[t, k], ici_r.at[t, k], (px, py, c))
                rc.start()
                pending.append(rc.wait_send)
        for t in range(nt):
            for k, (px, py) in enumerate(peers):
                land = half(outs[t].at[2 * px + py], t, c)
                _remote(land, land, ici_s.at[t, k], ici_r.at[t, k], (px, py, c)).wait_recv()
                if split[t]:
                    fw = _remote(land, land, fwd_s.at[t, k], fwd_r.at[t, k], sib)
                    fw.start()
                    pending.append(fw.wait_send)
                    other = half(outs[t].at[2 * px + py], t, 1 - c)
                    pending.append(_remote(other, other, fwd_s.at[t, k], fwd_r.at[t, k], sib).wait_recv)
        for wait in pending:
            wait()

    out_shape = [_sds((CHIPS,) + sh.shape, sh.dtype) for sh in shards]
    dma = pltpu.SemaphoreType.DMA
    scratch = [dma((nt, 3)), dma((nt, 3)), dma((nt, 3)), dma((nt, 3)), dma((nt,)), dma((nt,))]
    return _comm_call(body, name="gather_weights", n_in=nt, out_shape=out_shape, scratch=scratch)(*shards)


def _pair_split(grads, name):
    nt = len(grads)

    def body(*refs):
        ins, got = refs[:nt], refs[nt:2 * nt]
        send_sems, recv_sems = refs[2 * nt:]
        x, y, c = _mesh_pos()
        cps = []
        for t in range(nt):
            hr = ins[t].shape[1] // 2
            give = ins[t].at[:, pl.ds(pl.multiple_of((1 - c) * hr, 8), hr), :]
            rc = _remote(give, got[t], send_sems.at[t], recv_sems.at[t], (x, y, 1 - c))
            rc.start()
            cps.append(rc)
        for rc in cps:
            rc.wait()

    half = [_sds((CHIPS, g.shape[1] // 2, g.shape[2]), g.dtype) for g in grads]
    scratch = [pltpu.SemaphoreType.DMA((nt,)), pltpu.SemaphoreType.DMA((nt,))]
    return _comm_call(body, name=name, n_in=nt, out_shape=half, scratch=scratch)(*grads)


def _chip_scatter(parts, name):
    nt = len(parts)

    def body(*refs):
        ins, outs = refs[:nt], refs[nt:2 * nt]
        send_sems, recv_sems = refs[2 * nt:]
        x, y, c = _mesh_pos()
        cps = []
        for t in range(nt):
            for k, (px, py) in enumerate(_chip_peers(x, y)):
                rc = _remote(ins[t].at[2 * px + py], outs[t].at[k], send_sems.at[t, k], recv_sems.at[t, k], (px, py, c))
                rc.start()
                cps.append(rc)
        for cp in cps:
            cp.wait()

    out_shape = [_sds((3,) + p.shape[1:], p.dtype) for p in parts]
    scratch = [pltpu.SemaphoreType.DMA((nt, 3)), pltpu.SemaphoreType.DMA((nt, 3))]
    return _comm_call(body, name=name, n_in=nt, out_shape=out_shape, scratch=scratch)(*parts)


def _pair_join(halves):
    nt = len(halves)

    def body(*refs):
        ins, got = refs[:nt], refs[nt:2 * nt]
        send_sems, recv_sems = refs[2 * nt:]
        x, y, c = _mesh_pos()
        cps = []
        for t in range(nt):
            rc = _remote(ins[t], got[t], send_sems.at[t], recv_sems.at[t], (x, y, 1 - c))
            rc.start()
            cps.append(rc)
        for rc in cps:
            rc.wait()

    out_shape = [_sds(hf.shape, hf.dtype) for hf in halves]
    scratch = [pltpu.SemaphoreType.DMA((nt,)), pltpu.SemaphoreType.DMA((nt,))]
    return _comm_call(body, name="grad_pair_join", n_in=nt, out_shape=out_shape, scratch=scratch)(*halves)


_HBM = pl.BlockSpec(memory_space=pltpu.HBM)
_SEMS = pl.BlockSpec(memory_space=pltpu.SEMAPHORE)
_EFFECT = pltpu.SideEffectType.DATAFLOW_SIDE_EFFECTING


def _bcast_copies(ins, lands, send_sems, recv_sems):
    x, y, c = _mesh_pos()
    me = 2 * x + y
    targets = [((px, py, c), 2 * px + py) for px, py in _chip_peers(x, y)] + [((x, y, 1 - c), me)]
    out = []
    for t in range(len(ins)):
        for k, (dev, idx) in enumerate(targets):
            i = t * len(targets) + k
            arrival = lambda t=t, i=i, idx=idx, dev=dev: _remote(ins[t], lands[t].at[idx], send_sems.at[i],
                                                                 recv_sems.at[i], dev)
            out.append((_remote(ins[t], lands[t].at[me], send_sems.at[i], recv_sems.at[i], dev), arrival))
    return out


def _scatter_copies(ins, lands, send_sems, recv_sems):
    x, y, c = _mesh_pos()
    out = []
    for t in range(len(ins)):
        for k, (px, py) in enumerate(_chip_peers(x, y)):
            i = t * 3 + k
            cp = _remote(ins[t].at[2 * px + py], lands[t].at[k], send_sems.at[i], recv_sems.at[i], (px, py, c))
            out.append((cp, lambda cp=cp: cp))
    return out


def _pair_copies(ins, lands, send_sems, recv_sems):
    x, y, c = _mesh_pos()
    out = []
    for t in range(len(ins)):
        hr = ins[t].shape[1] // 2
        give = ins[t].at[:, pl.ds(pl.multiple_of((1 - c) * hr, 8), hr), :]
        cp = _remote(give, lands[t], send_sems.at[t], recv_sems.at[t], (x, y, 1 - c))
        out.append((cp, lambda cp=cp: cp))
    return out


def _split_start(copies, srcs, land_shapes, ncopy, dep, name):
    nt = len(srcs)

    def body(*refs):
        ins, lands = refs[:nt], refs[nt:2 * nt]
        send_sems, recv_sems, token = refs[2 * nt + 1], refs[2 * nt + 2], refs[-1]
        for send, _ in copies(ins, lands, send_sems, recv_sems):
            send.start()
        token[...] = jnp.zeros_like(token)

    lands = [pltpu.with_memory_space_constraint(lax.empty(sh, a.dtype), pltpu.HBM) for sh, a in zip(land_shapes, srcs)]
    srcs = [pltpu.with_memory_space_constraint(a, pltpu.HBM) for a in srcs]
    dma = pltpu.SemaphoreType.DMA
    out_shape = ([dma((nt * ncopy,)), dma((nt * ncopy,))] + [pltpu.HBM(a.shape, a.dtype) for a in srcs + lands]
                 + [_sds((8, 128))])
    outs = pl.pallas_call(
        body, name=name, in_specs=[_HBM] * (2 * nt) + [_ANY],
        out_specs=[_SEMS, _SEMS] + [_HBM] * (2 * nt) + [pl.BlockSpec(memory_space=pltpu.VMEM)], out_shape=out_shape,
        input_output_aliases={i: 2 + i for i in range(2 * nt)},
        compiler_params=pltpu.CompilerParams(has_side_effects=_EFFECT))(*srcs, *lands, dep)
    return outs[0], outs[1], outs[2:2 + nt], outs[2 + nt:2 + 2 * nt], outs[-1]


def _split_wait(copies, send_sems, recv_sems, srcs, lands, after, name):
    nt = len(srcs)

    def body(*refs):
        ins, lnd = refs[:nt], refs[nt:2 * nt]
        for send, arrival in copies(ins, lnd, refs[2 * nt], refs[2 * nt + 1]):
            send.wait_send()
            arrival().wait_recv()

    outs = pl.pallas_call(
        body, name=name, in_specs=[_HBM] * (2 * nt) + [_SEMS, _SEMS, _ANY], out_specs=[_HBM] * (2 * nt),
        out_shape=[pltpu.HBM(a.shape, a.dtype) for a in list(srcs) + list(lands)],
        input_output_aliases={i: i for i in range(2 * nt)},
        compiler_params=pltpu.CompilerParams(has_side_effects=_EFFECT))(*srcs, *lands, send_sems, recv_sems, after)
    return outs[:nt], outs[nt:]


def _gather_small(packed):
    n = packed.shape[0]

    def body(in_ref, out_ref, send_sems, recv_sems, loc_sem):
        x, y, c = _mesh_pos()
        me = 4 * x + 2 * y + c
        lc = pltpu.make_async_copy(in_ref, out_ref.at[me], loc_sem)
        lc.start()
        peers = []
        for k in range(1, NDEV):
            px, py, pc = x ^ (k >> 2), y ^ ((k >> 1) & 1), c ^ (k & 1)
            rc = pltpu.make_async_remote_copy(src_ref=in_ref, dst_ref=out_ref.at[me], send_sem=send_sems.at[k - 1],
                                              recv_sem=recv_sems.at[k - 1], device_id=(px, py, pc), device_id_type=MESH)
            rc.start()
            peers.append((k, px, py, pc))
        lc.wait()
        for k, px, py, pc in peers:
            pltpu.make_async_remote_copy(src_ref=in_ref, dst_ref=out_ref.at[4 * px + 2 * py + pc],
                                         send_sem=send_sems.at[k - 1], recv_sem=recv_sems.at[k - 1],
                                         device_id=(px, py, pc), device_id_type=MESH).wait()

    scratch = [pltpu.SemaphoreType.DMA((NDEV - 1,)), pltpu.SemaphoreType.DMA((NDEV - 1,)), pltpu.SemaphoreType.DMA]
    return _comm_call(body, name="gather_small_grads", n_in=1, out_shape=[_sds((NDEV, n, 128))],
                      scratch=scratch)(packed)[0]


def _row_tile(r, c):
    t = r
    while t * c * 4 > (1 << 20) and t % 16 == 0:
        t //= 2
    return t


def _k_pair_add(full, got, name):
    g, r, c = full.shape
    hr = r // 2
    tr = _row_tile(hr, c)
    nh = hr // tr

    def body(a_ref, b_ref, o_ref):
        o_ref[...] = (a_ref[...] + b_ref[...]).astype(_WIRE)

    mine = pl.BlockSpec((None, tr, c), lambda i, j: (i, lax.axis_index("c") * nh + j, 0))
    spec = pl.BlockSpec((None, tr, c), lambda i, j: (i, j, 0))
    return _pc(body, name=name, grid=(g, nh), in_specs=[mine, spec], out_specs=[spec],
               out_shape=[_sds((g, hr, c), _WIRE)])(full, got)[0]


def _k_chip_sum(parts, slots, name):
    _, r, c = parts.shape
    tr = _row_tile(r, c)

    def body(a_ref, s_ref, o_ref):
        acc = a_ref[...].astype(F32)
        for k in range(3):
            acc = acc + s_ref[k].astype(F32)
        o_ref[...] = acc

    own = pl.BlockSpec((None, tr, c), lambda i: (2 * lax.axis_index("x") + lax.axis_index("y"), i, 0))
    return _pc(body, name=name, grid=(r // tr,), in_specs=[own, pl.BlockSpec((3, tr, c), lambda i: (0, i, 0))],
               out_specs=[_row(tr, c)], out_shape=[_sds((r, c))])(parts, slots)[0]


def _adam(w, g, m, v):
    m = ADAM_B1 * m + (1.0 - ADAM_B1) * g
    v = ADAM_B2 * v + (1.0 - ADAM_B2) * (g * g)
    m_hat = m / (1.0 - ADAM_B1 ** ADAM_STEP)
    v_hat = v / (1.0 - ADAM_B2 ** ADAM_STEP)
    return -ADAM_LR * (m_hat / (jnp.sqrt(v_hat) + ADAM_EPS) + ADAM_WD * w), m, v


def _k_adam(w, mine, theirs, m, v, name):
    r, c = w.shape
    hr = r // 2
    tr = _row_tile(hr, c)
    nh = hr // tr

    def body(w_ref, a_ref, b_ref, m_ref, v_ref, g_ref, d_ref, mo_ref, vo_ref):
        upper = (pl.program_id(0) >= nh).astype(jnp.int32)
        g = jnp.where(upper == lax.axis_index("c"), a_ref[...], b_ref[...])
        g_ref[...] = g
        d_ref[...], mo_ref[...], vo_ref[...] = _adam(w_ref[...], g, m_ref[...], v_ref[...])

    hspec = pl.BlockSpec((tr, c), lambda i: (jnp.where(i >= nh, i - nh, i), 0))
    return _pc(body, name=name, grid=(r // tr,), in_specs=[_row(tr, c), hspec, hspec, _row(tr, c), _row(tr, c)],
               out_specs=[_row(tr, c)] * 4, out_shape=[_sds((r, c))] * 4)(w, mine, theirs, m, v)


def _k_sum8(a):
    _, n, _ = a.shape

    def body(a_ref, o_ref):
        acc = a_ref[0]
        for k in range(1, NDEV):
            acc = acc + a_ref[k]
        o_ref[...] = acc

    return _pc(body, name="sum_small_grads", grid=(1,), in_specs=[_acc(a.shape)], out_specs=[_acc((n, 128))],
               out_shape=[_sds((n, 128))])(a)[0]


def _k_adam_small(w, g, m, v):
    n = w.shape[0]

    def body(w_ref, g_ref, m_ref, v_ref, d_ref, mo_ref, vo_ref):
        d_ref[...], mo_ref[...], vo_ref[...] = _adam(w_ref[...], g_ref[...], m_ref[...], v_ref[...])

    return _pc(body, name="adam_small", grid=(1,), in_specs=[_acc((n, 128))] * 4, out_specs=[_acc((n, 128))] * 3,
               out_shape=[_sds((n, 128))] * 3)(w, g, m, v)


def _pack(vals):
    rows = []
    for a in vals:
        flat = a.reshape(-1)
        n = -(-flat.shape[0] // 1024) * 1024
        rows.append(jnp.pad(flat, (0, n - flat.shape[0])).reshape(n // 128, 128))
    return jnp.concatenate(rows, axis=0)


def _unpack(packed, shapes):
    out, off = [], 0
    for sh in shapes:
        size = int(np.prod(sh))
        n = -(-size // 1024) * 1024
        out.append(packed[off // 128:(off + n) // 128].reshape(-1)[:size].reshape(sh))
        off += n
    return out


_WEIGHTS = ["attn_norm", "w_in", "a_q_norm", "a_k_norm", "b_q_norm", "b_k_norm", "b_sinks", "mem_norm", "w_mem_kv",
            "m_q_norm", "m_k_norm", "w_o_a", "w_o_b", "w_o_m", "w_gate", "b_gate", "w_out", "ffn_norm", "w_up",
            "conv_w", "conv_b", "w_down"]
_BIG = ["w_in", "w_mem_kv", "w_o_a", "w_o_b", "w_o_m", "w_gate", "w_out", "w_up", "w_down"]
_SMALL = [n for n in _WEIGHTS if n not in _BIG]


def kernel(x, mem, positions, attn_norm, w_in, a_q_norm, a_k_norm, b_q_norm, b_k_norm, b_sinks, mem_norm, w_mem_kv, m_q_norm, m_k_norm, w_o_a, w_o_b, w_o_m, w_gate, b_gate, w_out, ffn_norm, w_up, conv_w, conv_b, w_down, loss_target, m_attn_norm, m_w_in, m_a_q_norm, m_a_k_norm, m_b_q_norm, m_b_k_norm, m_b_sinks, m_mem_norm, m_w_mem_kv, m_m_q_norm, m_m_k_norm, m_w_o_a, m_w_o_b, m_w_o_m, m_w_gate, m_b_gate, m_w_out, m_ffn_norm, m_w_up, m_conv_w, m_conv_b, m_w_down, v_attn_norm, v_w_in, v_a_q_norm, v_a_k_norm, v_b_q_norm, v_b_k_norm, v_b_sinks, v_mem_norm, v_w_mem_kv, v_m_q_norm, v_m_k_norm, v_w_o_a, v_w_o_b, v_w_o_m, v_w_gate, v_b_gate, v_w_out, v_ffn_norm, v_w_up, v_conv_w, v_conv_b, v_w_down):
    given = dict(locals())
    w = {n: given[n][0] for n in _WEIGHTS}
    m1 = {n: given["m_" + n][0] for n in _WEIGHTS}
    m2 = {n: given["v_" + n][0] for n in _WEIGHTS}

    w_in = _gather_shards([w["w_in"].astype(_MM)])[0]
    stages = (["w_gate", "w_mem_kv", "w_o_a", "w_o_b", "w_o_m", "w_out"], ["w_up", "w_down", "conv_w"])
    tok, started = w_in, []
    for k, names in enumerate(stages):
        shards = [w[n] if n == "conv_w" else w[n].astype(_MM) for n in names]
        *handles, tok = _split_start(_bcast_copies, shards, [(CHIPS,) + a.shape for a in shards], 4, tok,
                                     "gather_start_%d" % k)
        started.append(handles)
    small = {n: (w[n][None, :] if w[n].ndim == 1 else w[n]) for n in _SMALL if n != "conv_w"}
    small["attn_norm"] = small["attn_norm"] + tok[0:1, 0:1]

    def get_rest(stage, after):
        send, recv, srcs, lands = started[stage]
        got = _split_wait(_bcast_copies, send, recv, srcs, lands, after, "gather_wait_%d" % stage)[1]
        wts = dict(zip(stages[stage], got))
        for n in ("w_mem_kv", "w_out", "w_down"):
            if n in wts:
                wts[n] = wts[n].reshape(-1, wts[n].shape[-1])
        return wts

    parts, slots, pair, scat = {}, {}, [], []
    zeros = jnp.zeros((8, 128), F32)

    def finish_pair(after):
        names, tag, send, recv, srcs, lands = pair.pop()
        full, got = _split_wait(_pair_copies, send, recv, srcs, lands, after, "pair_wait_" + tag)
        mine = [_k_pair_add(f, b, "pair_add_" + n) for n, f, b in zip(names, full, got)]
        shapes = [(3,) + p.shape[1:] for p in mine]
        send, recv, srcs, lands, token = _split_start(_scatter_copies, mine, shapes, 3, zeros, "scatter_start_" + tag)
        scat.append((names, tag, send, recv, srcs, lands))
        return token

    def on_grads(group, after):
        names = list(group)
        tag = "_".join(names)
        token = finish_pair(after) if pair else zeros
        if not group:
            return token
        grads_g = [group[n] for n in names]
        if "w_in" in group:
            got = _pair_split(grads_g, "grad_pair_split_" + tag)
            for n, f, b in zip(names, grads_g, got):
                parts[n] = _k_pair_add(f, b, "pair_add_" + n)
            slots.update(zip(names, _chip_scatter([parts[n] for n in names], "grad_chip_scatter_" + tag)))
            return token
        shapes = [(CHIPS, g.shape[1] // 2, g.shape[2]) for g in grads_g]
        send, recv, srcs, lands, token = _split_start(_pair_copies, grads_g, shapes, 1, token, "pair_start_" + tag)
        pair.append((names, tag, send, recv, srcs, lands))
        return token

    loss, grad_x, sml = _local_step(x[0], mem[0], positions[0], loss_target[0], small, w_in, get_rest, on_grads)
    loss = lax.psum(loss, ("x", "y", "c"))
    for names, tag, send, recv, srcs, lands in scat:
        mine, got = _split_wait(_scatter_copies, send, recv, srcs, lands, slots["w_in"], "scatter_wait_" + tag)
        parts.update(zip(names, mine))
        slots.update(zip(names, got))
    mine = [_k_chip_sum(parts[n], slots[n], "chip_add_" + n) for n in _BIG]
    theirs = _pair_join(mine)
    grads = {}

    shapes = [sml[n].shape for n in _SMALL]
    gsm = dict(zip(_SMALL, _unpack(_k_sum8(_gather_small(_pack([sml[n] for n in _SMALL]))), shapes)))
    nu = w["conv_w"].shape[1]
    chip = 2 * lax.axis_index("x") + lax.axis_index("y")
    gsm["conv_w"] = lax.dynamic_slice_in_dim(gsm["conv_w"], chip * nu, nu, axis=1)
    for n in _SMALL:
        grads[n] = gsm[n].reshape(w[n].shape)

    delta, new_m, new_v = {}, {}, {}
    for n, a, b in zip(_BIG, mine, theirs):
        grads[n], delta[n], new_m[n], new_v[n] = _k_adam(w[n], a, b, m1[n], m2[n], "adam_" + n)
    pk = lambda d: _pack([d[n] for n in _SMALL])
    sshapes = [w[n].shape for n in _SMALL]
    for dst, packed in zip((delta, new_m, new_v), _k_adam_small(pk(w), pk(grads), pk(m1), pk(m2))):
        dst.update(zip(_SMALL, _unpack(packed, sshapes)))

    lead = lambda d: [d[n][None] for n in _WEIGHTS]
    return (loss, grad_x[None], *lead(grads), *lead(delta), *lead(new_m), *lead(new_v))
```

```python
import math

import jax
import jax.numpy as jnp
import numpy as np
from jax import lax
from jax.experimental import pallas as pl
from jax.experimental.pallas import tpu as pltpu

F32 = jnp.float32
_MM = jnp.bfloat16
_WIRE = jnp.bfloat16

D_MODEL = 1024
HEAD = 64
BLK = 128
A_GROUPS = ((128, 1), (512, 4), (2048, 16))
A_HEADS = 4
A_W = A_HEADS * HEAD
B_QH = 8
B_KVH = 2
B_WINDOW = 128
M_HEADS = 4
M_HD = 128
M_W = M_HEADS * M_HD
D_FF = 2816
EPS = 1e-6
NEG = -1e30
ROPE_THETA = 500000.0
ROPE_ROT = 16
CHIPS = 4
NDEV = 8
ADAM_LR, ADAM_B1, ADAM_B2, ADAM_EPS, ADAM_WD, ADAM_STEP = 0.001, 0.9, 0.999, 1e-08, 0.01, 10
VMEM_LIMIT = 58 * 1024 * 1024
MESH = pl.DeviceIdType.MESH


def _pc(body, *, name, grid, in_specs, out_specs, out_shape, scratch=()):
    return pl.pallas_call(
        body, name=name, grid=grid, in_specs=in_specs, out_specs=out_specs, out_shape=out_shape,
        scratch_shapes=list(scratch),
        compiler_params=pltpu.CompilerParams(dimension_semantics=("arbitrary",) * len(grid),
                                             vmem_limit_bytes=VMEM_LIMIT))


def _row(ts, c, col=0):
    return pl.BlockSpec((ts, c), lambda i: (i, col))


def _res(shape):
    n = len(shape)
    return pl.BlockSpec(tuple(shape), lambda i: (0,) * n, pipeline_mode=pl.Buffered(1))


def _acc(shape):
    n = len(shape)
    return pl.BlockSpec(tuple(shape), lambda i: (0,) * n)


def _sds(shape, dtype=F32):
    return jax.ShapeDtypeStruct(tuple(shape), dtype)


def _dot(a, b):
    return jnp.dot(a.astype(_MM), b.astype(_MM), preferred_element_type=F32)


def _dot_nt(a, b):
    return lax.dot_general(a.astype(_MM), b.astype(_MM), (((1,), (1,)), ((), ())), preferred_element_type=F32)


def _dot_tn(a, b):
    return lax.dot_general(a.astype(_MM), b.astype(_MM), (((0,), (0,)), ((), ())), preferred_element_type=F32)


def _sum8(v):
    ts, c = v.shape
    return jnp.sum(v.reshape(ts // 8, 8, c), axis=0)


def _sigmoid(z):
    return 1.0 / (1.0 + jnp.exp(-z))


def _rms(x):
    r = lax.rsqrt(jnp.mean(x * x, axis=-1, keepdims=True) + EPS)
    return x * r, r


def _rms_bwd(dy, xh, r, gain):
    z = dy * gain
    return r * (z - xh * jnp.mean(z * xh, axis=-1, keepdims=True))


def _split_hi_lo(v):
    hi = v.astype(_MM)
    return hi, (v - hi.astype(F32)).astype(_MM)


def _lane_head(shape):
    return lax.shift_right_logical(lax.broadcasted_iota(jnp.int32, shape, len(shape) - 1), 6)


def _seg_sum64(v):
    w = v.shape[1]
    e = jnp.where(_lane_head((w, w)) == lax.shift_right_logical(lax.broadcasted_iota(jnp.int32, (w, w), 0), 6),
                  1.0, 0.0).astype(_MM)
    hi, lo = _split_hi_lo(v)
    return jnp.dot(hi, e, preferred_element_type=F32) + jnp.dot(lo, e, preferred_element_type=F32)


def _seg_norm(x, seg):
    if seg == HEAD:
        r = lax.rsqrt(_seg_sum64(x * x) * (1.0 / HEAD) + EPS)
        return x * r, r
    w = x.shape[1]
    xh, rr = [], []
    for s in range(w // seg):
        xs = x[:, s * seg:(s + 1) * seg]
        r = lax.rsqrt(jnp.mean(xs * xs, axis=-1, keepdims=True) + EPS)
        xh.append(xs * r)
        rr.append(jnp.broadcast_to(r, xs.shape))
    return jnp.concatenate(xh, axis=1), jnp.concatenate(rr, axis=1)


def _seg_mean(v, seg):
    if seg == HEAD:
        return _seg_sum64(v) * (1.0 / HEAD)
    w = v.shape[1]
    out = []
    for s in range(w // seg):
        vs = v[:, s * seg:(s + 1) * seg]
        out.append(jnp.broadcast_to(jnp.mean(vs, axis=-1, keepdims=True), vs.shape))
    return jnp.concatenate(out, axis=1)


def _rope(t, c, sa, sb):
    out = []
    for cb in range(t.shape[1] // 128):
        tc = t[:, cb * 128:(cb + 1) * 128]
        out.append(tc * c + pltpu.roll(tc, 120, 1) * sa + pltpu.roll(tc, 8, 1) * sb)
    return jnp.concatenate(out, axis=1) if len(out) > 1 else out[0]


def _rope_bwd(dy, c, sa, sb):
    out = []
    for cb in range(dy.shape[1] // 128):
        dc = dy[:, cb * 128:(cb + 1) * 128]
        out.append(dc * c + pltpu.roll(dc * sa, 8, 1) + pltpu.roll(dc * sb, 120, 1))
    return jnp.concatenate(out, axis=1) if len(out) > 1 else out[0]


def _rope_consts():
    half = ROPE_ROT // 2
    c = np.float32(-2.0 * math.log(ROPE_THETA) / ROPE_ROT)
    freqs = np.exp(np.arange(half, dtype=np.float32) * c).astype(np.float32)
    place = np.zeros((3, half, 128), np.float32)
    ones = np.zeros((1, 128), np.float32)
    for lane in range(128):
        d = lane % HEAD
        if d < half:
            place[0, d, lane], place[1, d, lane] = 1.0, -1.0
        elif d < ROPE_ROT:
            place[0, d - half, lane], place[2, d - half, lane] = 1.0, 1.0
        else:
            ones[0, lane] = 1.0
    return np.tile(freqs[:, None], (1, 128)), place, ones


def _rope_tables(pos_rows):
    r = pos_rows.shape[0]
    tr = min(1024, r)
    freqs, place, ones = _rope_consts()

    def split3(v):
        hi, mid = _split_hi_lo(v)
        lo = (v - hi.astype(F32) - mid.astype(F32)).astype(_MM)
        return hi, mid, lo

    def body(p_ref, f_ref, e_ref, one_ref, c_ref, sa_ref, sb_ref):
        for j in range(tr // 128):
            ang = p_ref[j:j + 1, :].astype(F32) * f_ref[...]
            rows = slice(j * 128, (j + 1) * 128)
            for ref, k, v in ((c_ref, 0, jnp.cos(ang)), (sa_ref, 1, jnp.sin(ang)), (sb_ref, 2, jnp.sin(ang))):
                e = e_ref[k].astype(_MM)
                out = sum(_dot_tn(part, e) for part in split3(v))
                ref[rows, :] = out + one_ref[...] if k == 0 else out

    return _pc(body, name="rope_tables", grid=(r // tr,),
               in_specs=[pl.BlockSpec((tr // 128, 128), lambda i: (i, 0)), _acc((ROPE_ROT // 2, 128)),
                         _acc((3, ROPE_ROT // 2, 128)), _acc((1, 128))],
               out_specs=[_row(tr, 128)] * 3, out_shape=[_sds((r, 128))] * 3)(
                   pos_rows.reshape(r // 128, 128), jnp.asarray(freqs), jnp.asarray(place), jnp.asarray(ones))


def _k_in(x, g1, w_in):
    s = x.shape[0]
    ts = min(256, s)
    nin = w_in.shape[2]
    ncol = CHIPS * nin
    a_cols = 3 * A_W
    offs = [0, a_cols, 2 * a_cols, 3 * a_cols, 3 * a_cols + B_QH * HEAD,
            3 * a_cols + (B_QH + B_KVH) * HEAD, 3 * a_cols + (B_QH + 2 * B_KVH) * HEAD, ncol]

    def body(x_ref, g_ref, wi_ref, h_ref, a0, a1, a2, qb, kb, vb, mq, p_scr):
        xh, _ = _rms(x_ref[...])
        h = (xh * g_ref[...]).astype(_MM)
        h_ref[...] = h
        for j in range(CHIPS):
            p_scr[:, j * nin:(j + 1) * nin] = jnp.dot(h, wi_ref[j], preferred_element_type=F32)
        for k, ref in enumerate((a0, a1, a2, qb, kb, vb, mq)):
            ref[...] = p_scr[:, offs[k]:offs[k + 1]]

    widths = [offs[k + 1] - offs[k] for k in range(7)]
    return _pc(
        body, name="in_proj", grid=(s // ts,),
        in_specs=[_row(ts, D_MODEL), _res((1, D_MODEL)), _res(w_in.shape)],
        out_specs=[_row(ts, D_MODEL)] + [_row(ts, w) for w in widths],
        out_shape=[_sds((s, D_MODEL), _MM)] + [_sds((s, w)) for w in widths],
        scratch=[pltpu.VMEM((ts, ncol), F32)])(x, g1, w_in)


def _k_gate(h, w_gate, b_gate):
    s = h.shape[0]
    ts = min(256, s)
    ng = w_gate.shape[2]

    def body(h_ref, wg_ref, bg_ref, gt_ref):
        h = h_ref[...]
        for j in range(CHIPS):
            z = jnp.dot(h, wg_ref[j], preferred_element_type=F32) + bg_ref[:, j * ng:(j + 1) * ng]
            gt_ref[:, j * ng:(j + 1) * ng] = _sigmoid(z)

    return _pc(body, name="gate_proj", grid=(s // ts,),
               in_specs=[_row(ts, D_MODEL), _res(w_gate.shape), _res(b_gate.shape)],
               out_specs=[_row(ts, CHIPS * ng)], out_shape=[_sds((s, CHIPS * ng))])(h, w_gate, b_gate)[0]


def _k_prep(srcs, gq, gk, tabs, tab_row, *, wq, wk, rows_per_gain, name):
    rows = srcs[0][0].shape[0]
    ts = min(256, rows)

    def body(q_ref, k_ref, v_ref, gq_ref, gk_ref, c_ref, sa_ref, sb_ref, qn_ref, kn_ref, vn_ref):
        c, sa, sb = c_ref[...], sa_ref[...], sb_ref[...]
        qh, _ = _seg_norm(q_ref[...], HEAD)
        qn_ref[...] = _rope(qh * gq_ref[...], c, sa, sb).astype(_MM)
        kh, _ = _seg_norm(k_ref[...], HEAD)
        kn_ref[...] = _rope(kh * gk_ref[...], c, sa, sb).astype(_MM)
        vn_ref[...] = v_ref[...].astype(_MM)

    gspec = lambda w: pl.BlockSpec((None, 1, w), lambda i: ((i * ts) // rows_per_gain, 0, 0))
    return _pc(
        body, name=name, grid=(rows // ts,),
        in_specs=[_row(ts, wq, srcs[0][1]), _row(ts, wk, srcs[1][1]), _row(ts, wk, srcs[2][1]),
                  gspec(wq), gspec(wk)] + [pl.BlockSpec((ts, 128), lambda i: (i + tab_row // ts, 0))] * 3,
        out_specs=[_row(ts, wq), _row(ts, wk), _row(ts, wk)],
        out_shape=[_sds((rows, wq), _MM), _sds((rows, wk), _MM), _sds((rows, wk), _MM)])(
            srcs[0][0], srcs[1][0], srcs[2][0], gq, gk, *tabs)


def _first_flag(b, segs, nb):
    first = b >= nb
    for k, (start, period) in enumerate(segs):
        end = segs[k + 1][0] if k + 1 < len(segs) else nb
        first = first | ((b >= start) & (b < end) & (lax.rem(b - start, jnp.int32(period)) == 0))
    return first


def _band_bias(thr, with_cur):
    qi = lax.broadcasted_iota(jnp.int32, (BLK, BLK), 0)
    kj = lax.broadcasted_iota(jnp.int32, (BLK, BLK), 1)
    prev = jnp.where(kj >= qi + thr, 0.0, NEG)
    return jnp.concatenate([prev, jnp.where(kj <= qi, 0.0, NEG)], axis=1) if with_cur else prev


def _blockdiag(t4):
    head = _lane_head((1, A_W))
    return jnp.concatenate([t4 * jnp.where(head == h, 1.0, 0.0).astype(t4.dtype) for h in range(A_HEADS)], axis=0)


def _fold_diag(t, n):
    head = _lane_head((n, A_W))
    out = t[3 * n:4 * n]
    for h in (2, 1, 0):
        out = jnp.where(head == h, t[h * n:(h + 1) * n], out)
    return out


def _expand_heads(cols):
    n = cols[0].shape[0]
    head = _lane_head((n, A_W))
    out = jnp.broadcast_to(cols[3], (n, A_W))
    for h in (2, 1, 0):
        out = jnp.where(head == h, cols[h], out)
    return out


def _unit_kv(p_ref, c_ref, u, shared):
    if not shared:
        return jnp.concatenate([p_ref[:, u * A_W:(u + 1) * A_W], c_ref[:, u * A_W:(u + 1) * A_W]], axis=0)
    kg = jnp.concatenate([p_ref[:, u * HEAD:(u + 1) * HEAD], c_ref[:, u * HEAD:(u + 1) * HEAD]], axis=0)
    return jnp.concatenate([kg] * A_HEADS, axis=1)


def _k_band_fwd(qn, kn, vn, *, hq, hk, max_dist, segs, sink, name):
    rows = qn.shape[0]
    nb = rows // BLK
    units = hq // A_HEADS
    shared = hk != hq
    wq, wk = hq * HEAD, hk * HEAD
    scale = HEAD ** -0.5

    def body(*refs):
        if sink is None:
            q_ref, kc_ref, kp_ref, vc_ref, vp_ref, o_ref, l_ref = refs
        else:
            q_ref, kc_ref, kp_ref, vc_ref, vp_ref, sk_ref, o_ref, l_ref = refs
        b = pl.program_id(0)
        bias = _band_bias(jnp.where(_first_flag(b, segs, nb), 1 << 20, BLK - max_dist), True)
        for u in range(units):
            us = slice(u * A_W, (u + 1) * A_W)
            kb = _blockdiag(_unit_kv(kp_ref, kc_ref, u, shared))
            vb = _blockdiag(_unit_kv(vp_ref, vc_ref, u, shared))
            s_all = _dot_nt(q_ref[:, us], kb) * scale
            ps, ls = [], []
            for h in range(A_HEADS):
                s = s_all[:, h * 2 * BLK:(h + 1) * 2 * BLK] + bias
                m = jnp.max(s, axis=-1, keepdims=True)
                e = jnp.exp(s - m)
                lse = m + jnp.log(jnp.sum(e, axis=-1, keepdims=True))
                if sink is not None:
                    sk = sk_ref[u * A_HEADS + h]
                    mx = jnp.maximum(lse, sk)
                    lse = mx + jnp.log(jnp.exp(lse - mx) + jnp.exp(sk - mx))
                ps.append((e * jnp.exp(m - lse)).astype(_MM))
                ls.append(lse)
            o_ref[:, us] = _dot(jnp.concatenate(ps, axis=1), vb)
            l_ref[:, us] = _expand_heads(ls)

    cur = lambda w: pl.BlockSpec((BLK, w), lambda i: (i, 0))
    prev = lambda w: pl.BlockSpec((BLK, w), lambda i: (jnp.maximum(i - 1, 0), 0))
    in_specs = [cur(wq), cur(wk), prev(wk), cur(wk), prev(wk)]
    args = [qn, kn, kn, vn, vn]
    if sink is not None:
        in_specs.append(pl.BlockSpec(memory_space=pltpu.SMEM))
        args.append(sink)
    return _pc(body, name=name, grid=(nb,), in_specs=in_specs, out_specs=[cur(wq), cur(wq)],
               out_shape=[_sds((rows, wq)), _sds((rows, wq))])(*args)


def _k_memkv(mem, mem_norm, w_kv, m_k_norm):
    n = mem.shape[0]

    def body(m_ref, g_ref, w_ref, gk_ref, mn_ref, kv_ref, mk_ref, mv_ref):
        mh, _ = _rms(m_ref[...])
        mn = (mh * g_ref[...]).astype(_MM)
        mn_ref[...] = mn
        kv = jnp.dot(mn, w_ref[...], preferred_element_type=F32)
        kv_ref[...] = kv
        kh, _ = _seg_norm(kv[:, :M_W], M_HD)
        mk_ref[...] = (kh * gk_ref[...]).astype(_MM)
        mv_ref[...] = kv[:, M_W:].astype(_MM)

    return _pc(body, name="mem_kv", grid=(1,),
               in_specs=[_acc((n, D_MODEL)), _acc((1, D_MODEL)), _acc(w_kv.shape), _acc((1, M_W))],
               out_specs=[_acc((n, D_MODEL)), _acc((n, 2 * M_W)), _acc((n, M_W)), _acc((n, M_W))],
               out_shape=[_sds((n, D_MODEL), _MM), _sds((n, 2 * M_W)), _sds((n, M_W), _MM), _sds((n, M_W), _MM)])(
                   mem, mem_norm, w_kv, m_k_norm)


def _mem_probs(q, mk):
    sc = _dot_nt(q, mk) * (M_HD ** -0.5)
    e = jnp.exp(sc - jnp.max(sc, axis=-1, keepdims=True))
    return e / jnp.sum(e, axis=-1, keepdims=True)


def _k_mem_fwd(m_q, gq, mk, mv):
    s = m_q.shape[0]
    n = mk.shape[0]
    ts = min(256, s)

    def body(q_ref, g_ref, mk_ref, mv_ref, o_ref):
        qh, _ = _seg_norm(q_ref[...], M_HD)
        qn = (qh * g_ref[...]).astype(_MM)
        for h in range(M_HEADS):
            hs = slice(h * M_HD, (h + 1) * M_HD)
            o_ref[:, hs] = _dot(_mem_probs(qn[:, hs], mk_ref[:, hs]), mv_ref[:, hs])

    return _pc(body, name="mem_attn", grid=(s // ts,),
               in_specs=[_row(ts, M_W), _res((1, M_W)), _res((n, M_W)), _res((n, M_W))],
               out_specs=[_row(ts, M_W)], out_shape=[_sds((s, M_W))])(m_q, gq, mk, mv)[0]


def _group_weights(l0, l1, l2):
    m = jnp.maximum(jnp.maximum(l0, l1), l2)
    e0, e1, e2 = jnp.exp(l0 - m), jnp.exp(l1 - m), jnp.exp(l2 - m)
    inv = 1.0 / (e0 + e1 + e2)
    return e0 * inv, e1 * inv, e2 * inv


def _branch_products(oa, ob, om, woa_ref, wob_ref, wom_ref, j):
    return _dot(oa, woa_ref[j]), _dot(ob, wob_ref[j]), _dot(om, wom_ref[j])


def _k_merge(og, lg, o_b, o_m, gates, x, w_oa, w_ob, w_om, w_out, g2):
    s = x.shape[0]
    ts = min(256, s)
    nc = w_oa.shape[2]

    def body(o0, o1, o2, l0, l1, l2, ob_ref, om_ref, gt_ref, x_ref, woa, wob, wom, wout, g_ref,
             oa_ref, mer_ref, x1_ref, h2_ref, m_scr):
        w0, w1, w2 = _group_weights(l0[...], l1[...], l2[...])
        oa = w0 * o0[...] + w1 * o1[...] + w2 * o2[...]
        oa_ref[...] = oa
        ob, om = ob_ref[...], om_ref[...]
        for j in range(CHIPS):
            pa, pb, pm = _branch_products(oa, ob, om, woa, wob, wom, j)
            cs = lambda br: slice(br * D_MODEL + j * nc, br * D_MODEL + (j + 1) * nc)
            m_scr[:, j * nc:(j + 1) * nc] = gt_ref[:, cs(0)] * pa + gt_ref[:, cs(1)] * pb + gt_ref[:, cs(2)] * pm
        mer = m_scr[...].astype(_MM)
        mer_ref[...] = mer
        x1 = x_ref[...] + jnp.dot(mer, wout[...], preferred_element_type=F32)
        x1_ref[...] = x1
        xh, _ = _rms(x1)
        h2_ref[...] = (xh * g_ref[...]).astype(_MM)

    return _pc(
        body, name="merge_out", grid=(s // ts,),
        in_specs=[_row(ts, A_W)] * 6 + [_row(ts, B_QH * HEAD), _row(ts, M_W), _row(ts, 3 * D_MODEL), _row(ts, D_MODEL),
                                         _res(w_oa.shape), _res(w_ob.shape), _res(w_om.shape), _res(w_out.shape),
                                         _res((1, D_MODEL))],
        out_specs=[_row(ts, A_W), _row(ts, D_MODEL), _row(ts, D_MODEL), _row(ts, D_MODEL)],
        out_shape=[_sds((s, A_W)), _sds((s, D_MODEL), _MM), _sds((s, D_MODEL)), _sds((s, D_MODEL), _MM)],
        scratch=[pltpu.VMEM((ts, D_MODEL), F32)])(*og, *lg, o_b, o_m, gates, x, w_oa, w_ob, w_om, w_out, g2)


def _k_up(h2, w_up):
    s = h2.shape[0]
    ts = min(256, s)
    nu = w_up.shape[2]

    def body(h_ref, w_ref, u_ref):
        h = h_ref[...]
        for j in range(CHIPS):
            u_ref[:, j * nu:(j + 1) * nu] = jnp.dot(h, w_ref[j], preferred_element_type=F32)

    return _pc(body, name="up_proj", grid=(s // ts,), in_specs=[_row(ts, D_MODEL), _res(w_up.shape)],
               out_specs=[_row(ts, CHIPS * nu)], out_shape=[_sds((s, CHIPS * nu))])(h2, w_up)[0]


def _shift_down(v, halo, k):
    ts = v.shape[0]
    row = lax.broadcasted_iota(jnp.int32, v.shape, 0)
    out = pltpu.roll(v, k, 0)
    for r in range(k):
        out = jnp.where(row == r, halo[8 - k + r:8 - k + r + 1, :], out)
    return out


def _shift_up(v, halo, k):
    ts = v.shape[0]
    row = lax.broadcasted_iota(jnp.int32, v.shape, 0)
    out = pltpu.roll(v, ts - k, 0)
    for r in range(k):
        out = jnp.where(row == ts - k + r, halo[r:r + 1, :], out)
    return out


def _k_ffn(u, conv_w, conv_b, w_down, x1, target):
    s = u.shape[0]
    ts = min(256, s)
    nu = conv_w.shape[2]
    half = CHIPS // 2

    def body(u_ref, uh_ref, cw_ref, cb_ref, wd_ref, x1_ref, t_ref, dy_ref, f_ref, dc_ref, loss_ref, c_scr, f_scr):
        i = pl.program_id(0)
        halo = jnp.where(i > 0, uh_ref[...], 0.0)
        for j in range(CHIPS):
            cs = slice(j * nu, (j + 1) * nu)
            uj = u_ref[:, cs]
            hj = halo[:, cs]
            c_scr[:, cs] = (cb_ref[:, cs] + cw_ref[j, 0:1, :] * _shift_down(uj, hj, 2)
                            + cw_ref[j, 1:2, :] * _shift_down(uj, hj, 1) + cw_ref[j, 2:3, :] * uj)
        for j in range(half):
            a = c_scr[:, j * nu:(j + 1) * nu]
            g = c_scr[:, (half + j) * nu:(half + j + 1) * nu]
            f_scr[:, j * nu:(j + 1) * nu] = (a * _sigmoid(a) * g).astype(_MM)
        f = f_scr[...]
        f_ref[...] = f
        y = x1_ref[...] + jnp.dot(f, wd_ref[...], preferred_element_type=F32)
        err = y - t_ref[...]
        dy = err * (1.0 / D_MODEL)
        dy_ref[...] = dy

        @pl.when(i == 0)
        def _():
            loss_ref[...] = jnp.zeros_like(loss_ref)

        loss_ref[...] += _sum8(err * err)
        df = _dot_nt(dy, wd_ref[...])
        for j in range(half):
            a = c_scr[:, j * nu:(j + 1) * nu]
            g = c_scr[:, (half + j) * nu:(half + j + 1) * nu]
            sa = _sigmoid(a)
            dfj = df[:, j * nu:(j + 1) * nu]
            dc_ref[:, j * nu:(j + 1) * nu] = dfj * g * (sa * (1.0 + a * (1.0 - sa)))
            dc_ref[:, (half + j) * nu:(half + j + 1) * nu] = dfj * (a * sa)

    wide = CHIPS * nu
    return _pc(
        body, name="conv_ffn", grid=(s // ts,),
        in_specs=[_row(ts, wide), pl.BlockSpec((8, wide), lambda i: (jnp.maximum(i * (ts // 8) - 1, 0), 0)),
                  _res(conv_w.shape), _res((1, wide)), _res(w_down.shape), _row(ts, D_MODEL), _row(ts, D_MODEL)],
        out_specs=[_row(ts, D_MODEL), _row(ts, D_FF), _row(ts, wide), _acc((8, D_MODEL))],
        out_shape=[_sds((s, D_MODEL)), _sds((s, D_FF), _MM), _sds((s, wide)), _sds((8, D_MODEL))],
        scratch=[pltpu.VMEM((ts, wide), F32), pltpu.VMEM((ts, D_FF), _MM)])(u, u, conv_w, conv_b, w_down, x1, target)


def _k_conv_bwd(dc, u, conv_w, w_up, x1, g2, dy):
    s = u.shape[0]
    ts = min(256, s)
    nu = conv_w.shape[2]
    wide = CHIPS * nu
    last = s // ts - 1

    def body(dc_ref, dn_ref, u_ref, cw_ref, wu_ref, x1_ref, g_ref, dy_ref, dx1_ref, du_ref, cacc_ref, gacc_ref):
        i = pl.program_id(0)

        @pl.when(i == 0)
        def _():
            cacc_ref[...] = jnp.zeros_like(cacc_ref)
            gacc_ref[...] = jnp.zeros_like(gacc_ref)

        dhalo = jnp.where(i < last, dn_ref[...], 0.0)
        dh2 = jnp.zeros((ts, D_MODEL), F32)
        for j in range(CHIPS):
            cs = slice(j * nu, (j + 1) * nu)
            dcj, uj = dc_ref[:, cs], u_ref[:, cs]
            dc1, dc2 = _shift_up(dcj, dhalo[:, cs], 1), _shift_up(dcj, dhalo[:, cs], 2)
            cacc_ref[0, :, cs] += _sum8(dcj)
            cacc_ref[1, :, cs] += _sum8(dc2 * uj)
            cacc_ref[2, :, cs] += _sum8(dc1 * uj)
            cacc_ref[3, :, cs] += _sum8(dcj * uj)
            du = (cw_ref[j, 2:3, :] * dcj + cw_ref[j, 1:2, :] * dc1 + cw_ref[j, 0:1, :] * dc2).astype(_MM)
            du_ref[:, cs] = du
            dh2 = dh2 + _dot_nt(du, wu_ref[j])
        xh, r = _rms(x1_ref[...])
        gacc_ref[...] += _sum8(dh2 * xh)
        dx1_ref[...] = dy_ref[...] + _rms_bwd(dh2, xh, r, g_ref[...])

    return _pc(
        body, name="conv_up_bwd", grid=(s // ts,),
        in_specs=[_row(ts, wide),
                  pl.BlockSpec((8, wide), lambda i: (jnp.minimum((i + 1) * (ts // 8), s // 8 - 1), 0)),
                  _row(ts, wide), _res(conv_w.shape), _res(w_up.shape), _row(ts, D_MODEL), _res((1, D_MODEL)),
                  _row(ts, D_MODEL)],
        out_specs=[_row(ts, D_MODEL), _row(ts, wide), _acc((4, 8, wide)), _acc((8, D_MODEL))],
        out_shape=[_sds((s, D_MODEL)), _sds((s, wide), _MM), _sds((4, 8, wide)), _sds((8, D_MODEL))])(
            dc, dc, u, conv_w, w_up, x1, g2, dy)


def _k_merge_bwd(dx1, og, lg, o_a, o_b, o_m, gates, w_oa, w_ob, w_om, w_out, dep):
    s = dx1.shape[0]
    ts = min(256, s)
    nc = w_oa.shape[2]

    def body(dx_ref, o0, o1, o2, l0, l1, l2, oa_ref, ob_ref, om_ref, gt_ref, woa, wob, wom, wout, dep_ref,
             dgp_ref, dpa_ref, dpb_ref, dpm_ref, dog0, dog1, dog2, dl0, dl1, dl2, dob_ref, dom_ref, bacc_ref):
        i = pl.program_id(0)

        @pl.when(i == 0)
        def _():
            bacc_ref[...] = jnp.zeros_like(bacc_ref)

        dmer = _dot_nt(dx_ref[...], wout[...])
        oa, ob, om = oa_ref[...], ob_ref[...], om_ref[...]
        doa = jnp.zeros((ts, A_W), F32)
        dob = jnp.zeros((ts, B_QH * HEAD), F32)
        dom = jnp.zeros((ts, M_W), F32)
        for j in range(CHIPS):
            prods = _branch_products(oa, ob, om, woa, wob, wom, j)
            dmj = dmer[:, j * nc:(j + 1) * nc]
            dps = []
            for br, (p, dref) in enumerate(zip(prods, (dpa_ref, dpb_ref, dpm_ref))):
                cs = slice(br * D_MODEL + j * nc, br * D_MODEL + (j + 1) * nc)
                gt = gt_ref[:, cs]
                dgp = dmj * p * gt * (1.0 - gt)
                dgp_ref[:, cs] = dgp.astype(_MM)
                bacc_ref[:, cs] += _sum8(dgp)
                dp = (dmj * gt).astype(_MM)
                dref[:, j * nc:(j + 1) * nc] = dp
                dps.append(dp)
            doa = doa + _dot_nt(dps[0], woa[j])
            dob = dob + _dot_nt(dps[1], wob[j])
            dom = dom + _dot_nt(dps[2], wom[j])
        dob_ref[...] = dob
        dom_ref[...] = dom
        ws = _group_weights(l0[...], l1[...], l2[...])
        dsum = _seg_mean(doa * oa, HEAD) * float(HEAD)
        for w, dref, lref in zip(ws, (dog0, dog1, dog2), (dl0, dl1, dl2)):
            dref[...] = w * doa
            lref[...] = w * dsum

    return _pc(
        body, name="merge_out_bwd", grid=(s // ts,),
        in_specs=[_row(ts, D_MODEL)] + [_row(ts, A_W)] * 7 + [_row(ts, B_QH * HEAD), _row(ts, M_W), _row(ts, 3 * D_MODEL),
                                                              _res(w_oa.shape), _res(w_ob.shape), _res(w_om.shape),
                                                              _res(w_out.shape), _res((8, 128))],
        out_specs=[_row(ts, 3 * D_MODEL)] + [_row(ts, D_MODEL)] * 3 + [_row(ts, A_W)] * 6
        + [_row(ts, B_QH * HEAD), _row(ts, M_W), _acc((8, 3 * D_MODEL))],
        out_shape=[_sds((s, 3 * D_MODEL), _MM)] + [_sds((s, D_MODEL), _MM)] * 3 + [_sds((s, A_W))] * 6
        + [_sds((s, B_QH * HEAD)), _sds((s, M_W)), _sds((8, 3 * D_MODEL))])(
            dx1, *og, *lg, o_a, o_b, o_m, gates, w_oa, w_ob, w_om, w_out, dep)


def _k_mem_bwd(m_q, gq, mk, mv, o_m, do_m):
    s = m_q.shape[0]
    n = mk.shape[0]
    ts = min(256, s)
    scale = M_HD ** -0.5

    def body(q_ref, g_ref, mk_ref, mv_ref, o_ref, do_ref, dq_ref, dmk_ref, dmv_ref, gacc_ref):
        i = pl.program_id(0)

        @pl.when(i == 0)
        def _():
            dmk_ref[...] = jnp.zeros_like(dmk_ref)
            dmv_ref[...] = jnp.zeros_like(dmv_ref)
            gacc_ref[...] = jnp.zeros_like(gacc_ref)

        gain = g_ref[...]
        qh, r = _seg_norm(q_ref[...], M_HD)
        qn = (qh * gain).astype(_MM)
        do = do_ref[...]
        delta = _seg_mean(do * o_ref[...], M_HD) * float(M_HD)
        dqn = []
        for h in range(M_HEADS):
            hs = slice(h * M_HD, (h + 1) * M_HD)
            p = _mem_probs(qn[:, hs], mk_ref[:, hs])
            dp = _dot_nt(do[:, hs], mv_ref[:, hs])
            ds = (p * (dp - delta[:, hs][:, 0:1]) * scale).astype(_MM)
            dqn.append(_dot(ds, mk_ref[:, hs]))
            dmk_ref[:, hs] += _dot_tn(ds, qn[:, hs])
            dmv_ref[:, hs] += _dot_tn(p, do[:, hs])
        dqn = jnp.concatenate(dqn, axis=1)
        gacc_ref[...] += _sum8(dqn * qh)
        z = dqn * gain
        dq_ref[...] = (r * (z - qh * _seg_mean(z * qh, M_HD))).astype(_MM)

    return _pc(
        body, name="mem_attn_bwd", grid=(s // ts,),
        in_specs=[_row(ts, M_W), _res((1, M_W)), _res((n, M_W)), _res((n, M_W)), _row(ts, M_W), _row(ts, M_W)],
        out_specs=[_row(ts, M_W), _acc((n, M_W)), _acc((n, M_W)), _acc((8, M_W))],
        out_shape=[_sds((s, M_W), _MM), _sds((n, M_W)), _sds((n, M_W)), _sds((8, M_W))])(m_q, gq, mk, mv, o_m, do_m)


def _k_memkv_bwd(mem, mem_norm, w_kv, m_k_norm, mem_n, kv, dmk, dmv):
    n = mem.shape[0]

    def body(m_ref, g_ref, w_ref, gk_ref, mn_ref, kv_ref, dmk_ref, dmv_ref, dw_ref, dg_ref, dgk_ref):
        gk = gk_ref[...]
        kh, r = _seg_norm(kv_ref[:, :M_W], M_HD)
        dmk = dmk_ref[...]
        dgk_ref[...] = _sum8(dmk * kh)
        z = dmk * gk
        dk = r * (z - kh * _seg_mean(z * kh, M_HD))
        dkv = jnp.concatenate([dk, dmv_ref[...]], axis=1).astype(_MM)
        dw_ref[...] = _dot_tn(mn_ref[...], dkv)
        dmn = _dot_nt(dkv, w_ref[...])
        mh, _ = _rms(m_ref[...])
        dg_ref[...] = _sum8(dmn * mh)

    return _pc(body, name="mem_kv_bwd", grid=(1,),
               in_specs=[_acc((n, D_MODEL)), _acc((1, D_MODEL)), _acc(w_kv.shape), _acc((1, M_W)), _acc((n, D_MODEL)),
                         _acc((n, 2 * M_W)), _acc((n, M_W)), _acc((n, M_W))],
               out_specs=[_acc(w_kv.shape), _acc((8, D_MODEL)), _acc((8, M_W))],
               out_shape=[_sds(w_kv.shape), _sds((8, D_MODEL)), _sds((8, M_W))])(
                   mem, mem_norm, w_kv, m_k_norm, mem_n, kv, dmk, dmv)


def _k_band_bwd(qn, kn, vn, do, lse, dl_or_o, *, hq, hk, max_dist, segs, sink, name):
    rows = qn.shape[0]
    nb = rows // BLK
    units = hq // A_HEADS
    shared = hk != hq
    wq, wk = hq * HEAD, hk * HEAD
    scale = HEAD ** -0.5

    def body(*refs):
        (qb_ref, qx_ref, kb_ref, kp_ref, vb_ref, vp_ref, dob_ref, dox_ref, lb_ref, lx_ref, eb_ref, ex_ref) = refs[:12]
        if sink is None:
            dq_ref, dk_ref, dv_ref = refs[12:]
        else:
            sk_ref, dq_ref, dk_ref, dv_ref, sacc_ref = refs[12:]
        b = pl.program_id(0)
        bias1 = _band_bias(jnp.where(_first_flag(b, segs, nb), 1 << 20, BLK - max_dist), True)
        bias2 = _band_bias(jnp.where(_first_flag(b + 1, segs, nb), 1 << 20, BLK - max_dist), False)
        if sink is not None:
            @pl.when(b == 0)
            def _():
                sacc_ref[...] = jnp.zeros_like(sacc_ref)

        for u in range(units):
            us = slice(u * A_W, (u + 1) * A_W)
            q4, qx4, do4, dox4 = qb_ref[:, us], qx_ref[:, us], dob_ref[:, us], dox_ref[:, us]
            k4, v4 = _unit_kv(kp_ref, kb_ref, u, shared), _unit_kv(vp_ref, vb_ref, u, shared)
            kd, vd = _blockdiag(k4), _blockdiag(v4)
            kdc, vdc = _blockdiag(k4[BLK:]), _blockdiag(v4[BLK:])
            if sink is None:
                dlt_b, dlt_x = eb_ref[:, us], ex_ref[:, us]
            else:
                dlt_b = _seg_sum64(do4.astype(F32) * eb_ref[:, us])
                dlt_x = _seg_sum64(dox4.astype(F32) * ex_ref[:, us])
            s1, dp1 = _dot_nt(q4, kd) * scale, _dot_nt(do4, vd)
            s2, dp2 = _dot_nt(qx4, kdc) * scale, _dot_nt(dox4, vdc)
            ds1, ds1c, p1c, ds2, p2 = [], [], [], [], []
            for h in range(A_HEADS):
                col = slice(u * A_W + h * HEAD, u * A_W + h * HEAD + 1)
                ucol = slice(h * HEAD, h * HEAD + 1)
                wide, narrow = slice(h * 2 * BLK, (h + 1) * 2 * BLK), slice(h * BLK, (h + 1) * BLK)
                l_b, l_x = lb_ref[:, col], lx_ref[:, col]
                p = jnp.exp(s1[:, wide] + bias1 - l_b)
                ds = p * (dp1[:, wide] - dlt_b[:, ucol]) * scale
                ds1.append(ds.astype(_MM))
                ds1c.append(ds[:, BLK:].astype(_MM))
                p1c.append(p[:, BLK:].astype(_MM))
                px = jnp.exp(s2[:, narrow] + bias2 - l_x)
                ds2.append((px * (dp2[:, narrow] - dlt_x[:, ucol]) * scale).astype(_MM))
                p2.append(px.astype(_MM))
                if sink is not None:
                    j = u * A_HEADS + h
                    sacc_ref[:, j:j + 1] += -jnp.exp(sk_ref[j] - l_b) * dlt_b[:, ucol]
            dq_ref[:, us] = _dot(jnp.concatenate(ds1, axis=1), kd)
            dk4 = _fold_diag(_dot_tn(jnp.concatenate(ds1c, axis=1), q4) + _dot_tn(jnp.concatenate(ds2, axis=1), qx4), BLK)
            dv4 = _fold_diag(_dot_tn(jnp.concatenate(p1c, axis=1), do4) + _dot_tn(jnp.concatenate(p2, axis=1), dox4), BLK)
            if shared:
                fold = lambda t: (t[:, 0:HEAD] + t[:, HEAD:2 * HEAD]) + (t[:, 2 * HEAD:3 * HEAD] + t[:, 3 * HEAD:])
                dk_ref[:, u * HEAD:(u + 1) * HEAD] = fold(dk4)
                dv_ref[:, u * HEAD:(u + 1) * HEAD] = fold(dv4).astype(_MM)
            else:
                dk_ref[:, us] = dk4
                dv_ref[:, us] = dv4.astype(_MM)

    cur = lambda w: pl.BlockSpec((BLK, w), lambda i: (i, 0))
    prev = lambda w: pl.BlockSpec((BLK, w), lambda i: (jnp.maximum(i - 1, 0), 0))
    nxt = lambda w: pl.BlockSpec((BLK, w), lambda i: (jnp.minimum(i + 1, nb - 1), 0))
    in_specs = [cur(wq), nxt(wq), cur(wk), prev(wk), cur(wk), prev(wk), cur(wq), nxt(wq), cur(wq), nxt(wq), cur(wq), nxt(wq)]
    args = [qn, qn, kn, kn, vn, vn, do, do, lse, lse, dl_or_o, dl_or_o]
    out_specs = [cur(wq), cur(wk), cur(wk)]
    out_shape = [_sds((rows, wq)), _sds((rows, wk)), _sds((rows, wk), _MM)]
    if sink is not None:
        in_specs.append(pl.BlockSpec(memory_space=pltpu.SMEM))
        args.append(sink)
        out_specs.append(_acc((BLK, 128)))
        out_shape.append(_sds((BLK, 128)))
    return _pc(body, name=name, grid=(nb,), in_specs=in_specs, out_specs=out_specs, out_shape=out_shape)(*args)


def _k_prep_bwd(srcs, dqn, dkn, gq, gk, tabs, tab_row, *, wq, wk, rows_per_gain, name):
    rows = dqn.shape[0]
    ts = min(256, rows)
    ngain = gq.shape[0]

    def body(q_ref, k_ref, dq_ref, dk_ref, gq_ref, gk_ref, c_ref, sa_ref, sb_ref, oq_ref, ok_ref, aq_ref, ak_ref):
        i = pl.program_id(0)

        @pl.when(lax.rem(i * ts, rows_per_gain) == 0)
        def _():
            aq_ref[...] = jnp.zeros_like(aq_ref)
            ak_ref[...] = jnp.zeros_like(ak_ref)

        c, sa, sb = c_ref[...], sa_ref[...], sb_ref[...]
        for x_ref, d_ref, g_ref, o_ref, a_ref in ((q_ref, dq_ref, gq_ref, oq_ref, aq_ref),
                                                   (k_ref, dk_ref, gk_ref, ok_ref, ak_ref)):
            xh, r = _seg_norm(x_ref[...], HEAD)
            dt = _rope_bwd(d_ref[...], c, sa, sb)
            a_ref[...] += _sum8(dt * xh)
            z = dt * g_ref[...]
            o_ref[...] = (r * (z - xh * _seg_mean(z * xh, HEAD))).astype(_MM)

    gspec = lambda w: pl.BlockSpec((None, 1, w), lambda i: ((i * ts) // rows_per_gain, 0, 0))
    aspec = lambda w: pl.BlockSpec((None, 8, w), lambda i: ((i * ts) // rows_per_gain, 0, 0))
    return _pc(
        body, name=name, grid=(rows // ts,),
        in_specs=[_row(ts, wq, srcs[0][1]), _row(ts, wk, srcs[1][1]), _row(ts, wq), _row(ts, wk), gspec(wq), gspec(wk)]
        + [pl.BlockSpec((ts, 128), lambda i: (i + tab_row // ts, 0))] * 3,
        out_specs=[_row(ts, wq), _row(ts, wk), aspec(wq), aspec(wk)],
        out_shape=[_sds((rows, wq), _MM), _sds((rows, wk), _MM), _sds((ngain, 8, wq)), _sds((ngain, 8, wk))])(
            srcs[0][0], srcs[1][0], dqn, dkn, gq, gk, *tabs)


def _k_in_bwd(pieces, dgp, x, g1, dx1, w_in, w_gate):
    s = x.shape[0]
    ts = min(256, s)
    nin, ng = w_in.shape[2], w_gate.shape[2]
    widths = [p.shape[1] for p in pieces]
    ncol = sum(widths)

    def body(*refs):
        p_refs = refs[:len(pieces)]
        dgp_ref, x_ref, g_ref, dx1_ref, wi_ref, wg_ref, gx_ref, dpj_ref, gacc_ref = refs[len(pieces):]
        i = pl.program_id(0)

        @pl.when(i == 0)
        def _():
            gacc_ref[...] = jnp.zeros_like(gacc_ref)

        off = 0
        for p_ref, w in zip(p_refs, widths):
            dpj_ref[:, off:off + w] = p_ref[...]
            off += w
        dh = jnp.zeros((ts, D_MODEL), F32)
        for j in range(CHIPS):
            dh = dh + _dot_nt(dpj_ref[:, j * nin:(j + 1) * nin], wi_ref[j])
            dh = dh + _dot_nt(dgp_ref[:, j * ng:(j + 1) * ng], wg_ref[j])
        xh, r = _rms(x_ref[...])
        gacc_ref[...] += _sum8(dh * xh)
        gx_ref[...] = dx1_ref[...] + _rms_bwd(dh, xh, r, g_ref[...])

    return _pc(
        body, name="in_proj_bwd", grid=(s // ts,),
        in_specs=[_row(ts, w) for w in widths] + [_row(ts, CHIPS * ng), _row(ts, D_MODEL), _res((1, D_MODEL)),
                                                  _row(ts, D_MODEL), _res(w_in.shape), _res(w_gate.shape)],
        out_specs=[_row(ts, D_MODEL), _row(ts, ncol), _acc((8, D_MODEL))],
        out_shape=[_sds((s, D_MODEL)), _sds((s, ncol), _MM), _sds((8, D_MODEL))])(*pieces, dgp, x, g1, dx1, w_in, w_gate)


def _k_wgrad(a, b, *, nblk, stacked, name):
    s, k = a.shape
    n = b.shape[1]
    nb = n // nblk
    ts = min(1024, s)

    def body(a_ref, b_ref, o_ref):
        @pl.when(pl.program_id(1) == 0)
        def _():
            o_ref[...] = jnp.zeros_like(o_ref)

        o_ref[...] += _dot_tn(a_ref[...], b_ref[...])

    if stacked:
        out_spec, out_shape = pl.BlockSpec((None, k, nb), lambda g, t: (g, 0, 0)), _sds((nblk, k, nb))
    else:
        out_spec, out_shape = pl.BlockSpec((k, nb), lambda g, t: (0, g)), _sds((k, n))
    return _pc(body, name=name, grid=(nblk, s // ts),
               in_specs=[pl.BlockSpec((ts, k), lambda g, t: (t, 0)), pl.BlockSpec((ts, nb), lambda g, t: (t, g))],
               out_specs=[out_spec], out_shape=[out_shape])(a, b)[0]


def _to_res(t, d):
    s, c = t.shape
    return t if d == 1 else t.reshape(s // d, d, c).transpose(1, 0, 2).reshape(s, c)


def _from_res(t, d):
    s, c = t.shape
    return t if d == 1 else t.reshape(d, s // d, c).transpose(1, 0, 2).reshape(s, c)


def _tile_gain(g, heads):
    return jnp.tile(g, (1,) * (g.ndim - 1) + (heads,))[..., None, :]


def _local_step(x, mem, pos, target, small, w_in, get_rest, on_grads):
    s = x.shape[0]
    nblk = s // BLK
    g1, g2 = small["attn_norm"], small["ffn_norm"]

    pos_rows = jnp.concatenate([_to_res(pos[:, None], d)[:, 0] for _, d in A_GROUPS] + [pos])
    tabs = _rope_tables(pos_rows)

    h, qa0, qa1, qa2, q_b, k_b, v_b, m_q = _k_in(x, g1, w_in)

    qkv_a = jnp.concatenate([_to_res(t, d) for t, (_, d) in zip((qa0, qa1, qa2), A_GROUPS)], axis=0)
    gq_a = _tile_gain(small["a_q_norm"], A_HEADS)
    gk_a = _tile_gain(small["a_k_norm"], A_HEADS)
    src_a = ((qkv_a, 0), (qkv_a, 1), (qkv_a, 2))
    qn_a, kn_a, vn_a = _k_prep(src_a, gq_a, gk_a, tabs, 0, wq=A_W, wk=A_W, rows_per_gain=s, name="prep_a")
    segs_a = tuple((gi * nblk, nblk // d) for gi, (_, d) in enumerate(A_GROUPS))
    o_res, l_res = _k_band_fwd(qn_a, kn_a, vn_a, hq=A_HEADS, hk=A_HEADS, max_dist=BLK, segs=segs_a, sink=None,
                               name="attn_a")
    og = [_from_res(o_res[gi * s:(gi + 1) * s], d) for gi, (_, d) in enumerate(A_GROUPS)]
    lg = [_from_res(l_res[gi * s:(gi + 1) * s], d) for gi, (_, d) in enumerate(A_GROUPS)]

    gq_b = _tile_gain(small["b_q_norm"], B_QH)
    gk_b = _tile_gain(small["b_k_norm"], B_KVH)
    src_b = ((q_b, 0), (k_b, 0), (v_b, 0))
    qn_b, kn_b, vn_b = _k_prep(src_b, gq_b, gk_b, tabs, 3 * s, wq=B_QH * HEAD, wk=B_KVH * HEAD, rows_per_gain=s,
                               name="prep_b")
    sink_x = small["b_sinks"][0]
    segs_b = ((0, nblk),)
    o_b, l_b = _k_band_fwd(qn_b, kn_b, vn_b, hq=B_QH, hk=B_KVH, max_dist=B_WINDOW - 1, segs=segs_b, sink=sink_x,
                           name="attn_b")

    wts = get_rest(0, o_b)
    gates = _k_gate(h, wts["w_gate"], small["b_gate"])

    gq_m = _tile_gain(small["m_q_norm"], M_HEADS)[0]
    gk_m = _tile_gain(small["m_k_norm"], M_HEADS)[0]
    mem_n, kv, mk, mv = _k_memkv(mem, small["mem_norm"], wts["w_mem_kv"], gk_m)
    o_m = _k_mem_fwd(m_q, gq_m, mk, mv)

    o_a, merged, x1, h2 = _k_merge(og, lg, o_b, o_m, gates, x, wts["w_o_a"], wts["w_o_b"], wts["w_o_m"],
                                   wts["w_out"], g2)
    wts.update(get_rest(1, x1))
    u = _k_up(h2, wts["w_up"])
    dy, f, dc, loss_acc = _k_ffn(u, wts["conv_w"], small["conv_b"], wts["w_down"], x1, target)
    loss = (0.5 / D_MODEL) * jnp.sum(loss_acc)

    dx1, du, cacc, g2acc = _k_conv_bwd(dc, u, wts["conv_w"], wts["w_up"], x1, g2, dy)
    tok = on_grads({"w_up": _k_wgrad(h2, du, nblk=CHIPS, stacked=True, name="dw_up"),
                    "w_down": _k_wgrad(f, dy, nblk=2, stacked=False, name="dw_down").reshape(CHIPS, -1, D_MODEL)}, dx1)
    (dgp, dp_a, dp_b, dp_m, dog0, dog1, dog2, dl0, dl1, dl2, do_b, do_m, bacc) = _k_merge_bwd(
        dx1, og, lg, o_a, o_b, o_m, gates, wts["w_o_a"], wts["w_o_b"], wts["w_o_m"], wts["w_out"], tok)
    tok = on_grads({"w_gate": _k_wgrad(h, dgp, nblk=CHIPS, stacked=True, name="dw_gate"),
                    "w_o_a": _k_wgrad(o_a, dp_a, nblk=CHIPS, stacked=True, name="dw_o_a"),
                    "w_o_b": _k_wgrad(o_b, dp_b, nblk=CHIPS, stacked=True, name="dw_o_b"),
                    "w_o_m": _k_wgrad(o_m, dp_m, nblk=CHIPS, stacked=True, name="dw_o_m"),
                    "w_out": _k_wgrad(merged, dx1, nblk=1, stacked=False, name="dw_out").reshape(CHIPS, -1, D_MODEL)},
                   do_m)

    dq_m, dmk, dmv, gqm_acc = _k_mem_bwd(m_q, gq_m + tok[0:1, 0:1], mk, mv, o_m, do_m)
    dw_kv, gmem_acc, gkm_acc = _k_memkv_bwd(mem, small["mem_norm"], wts["w_mem_kv"], gk_m, mem_n, kv, dmk, dmv)

    dq_bn, dk_bn, dv_b, sacc = _k_band_bwd(qn_b, kn_b, vn_b, do_b, l_b, o_b, hq=B_QH, hk=B_KVH,
                                           max_dist=B_WINDOW - 1, segs=segs_b, sink=sink_x, name="attn_b_bwd")
    tok = on_grads({}, dq_bn)
    dq_b, dk_b, gqb_acc, gkb_acc = _k_prep_bwd(src_b, dq_bn, dk_bn, gq_b + tok[0:1, 0:1], gk_b, tabs, 3 * s, wq=B_QH * HEAD,
                                               wk=B_KVH * HEAD, rows_per_gain=s, name="prep_b_bwd")

    do_res = jnp.concatenate([_to_res(t, d) for t, (_, d) in zip((dog0, dog1, dog2), A_GROUPS)], axis=0)
    dl_res = jnp.concatenate([_to_res(t, d) for t, (_, d) in zip((dl0, dl1, dl2), A_GROUPS)], axis=0)
    dq_an, dk_an, dv_a = _k_band_bwd(qn_a, kn_a, vn_a, do_res, l_res, dl_res, hq=A_HEADS, hk=A_HEADS, max_dist=BLK,
                                     segs=segs_a, sink=None, name="attn_a_bwd")
    dq_a, dk_a, gqa_acc, gka_acc = _k_prep_bwd(src_a, dq_an, dk_an, gq_a, gk_a, tabs, 0, wq=A_W, wk=A_W,
                                               rows_per_gain=s, name="prep_a_bwd")
    pieces = []
    for gi, (_, d) in enumerate(A_GROUPS):
        rs = slice(gi * s, (gi + 1) * s)
        pieces += [_from_res(t[rs], d) for t in (dq_a, dk_a, dv_a)]
    pieces += [dq_b, dk_b, dv_b, dq_m]
    grad_x, dproj, g1acc = _k_in_bwd(pieces, dgp, x, g1, dx1, w_in, wts["w_gate"])
    on_grads({"w_in": _k_wgrad(h, dproj, nblk=CHIPS, stacked=True, name="dw_in"),
              "w_mem_kv": dw_kv.reshape(CHIPS, -1, 2 * M_W)}, grad_x)

    def fold(acc, heads):
        v = jnp.sum(acc, axis=-2)
        return jnp.sum(v.reshape(v.shape[:-1] + (heads, -1)), axis=-2)

    csum = jnp.sum(cacc, axis=1)
    sml = {
        "attn_norm": jnp.sum(g1acc, axis=0), "a_q_norm": fold(gqa_acc, A_HEADS), "a_k_norm": fold(gka_acc, A_HEADS),
        "b_q_norm": fold(gqb_acc[0], B_QH), "b_k_norm": fold(gkb_acc[0], B_KVH),
        "b_sinks": jnp.sum(sacc, axis=0)[:B_QH], "mem_norm": jnp.sum(gmem_acc, axis=0),
        "m_q_norm": fold(gqm_acc, M_HEADS), "m_k_norm": fold(gkm_acc, M_HEADS),
        "b_gate": jnp.sum(bacc, axis=0), "ffn_norm": jnp.sum(g2acc, axis=0),
        "conv_w": csum[1:], "conv_b": csum[0],
    }
    return loss, grad_x, sml


def _mesh_pos():
    return lax.axis_index("x"), lax.axis_index("y"), lax.axis_index("c")


def _chip_peers(x, y):
    return [(1 - x, y), (x, 1 - y), (1 - x, 1 - y)]


_ANY = pl.BlockSpec(memory_space=pl.ANY)


def _comm_call(body, *, name, n_in, out_shape, scratch):
    return pl.pallas_call(body, name=name, in_specs=[_ANY] * n_in, out_specs=[_ANY] * len(out_shape),
                          out_shape=out_shape, scratch_shapes=scratch)


def _remote(src, dst, send_sem, recv_sem, dev):
    return pltpu.make_async_remote_copy(src_ref=src, dst_ref=dst, send_sem=send_sem, recv_sem=recv_sem,
                                        device_id=dev, device_id_type=MESH)


def _gather_shards(shards):
    nt = len(shards)
    split = [sh.shape[0] % 16 == 0 for sh in shards]

    def body(*refs):
        ins, outs = refs[:nt], refs[nt:2 * nt]
        ici_s, ici_r, fwd_s, fwd_r, own_s, own_r = refs[2 * nt:]
        x, y, c = _mesh_pos()
        me = 2 * x + y
        sib = (x, y, 1 - c)
        peers = _chip_peers(x, y)

        def half(ref, t, who):
            if not split[t]:
                return ref
            hr = shards[t].shape[0] // 2
            return ref.at[pl.ds(pl.multiple_of(who * hr, 8), hr), :]

        pending = []
        for t in range(nt):
            own = _remote(ins[t], outs[t].at[me], own_s.at[t], own_r.at[t], sib)
            own.start()
            pending.append(own.wait)
            for k, (px, py) in enumerate(peers):
                rc = _remote(half(ins[t], t, c), half(outs[t].at[me], t, c), ici_s.at[t, k], ici_r.at[t, k], (px, py, c))
                rc.start()
                pending.append(rc.wait_send)
        for t in range(nt):
            for k, (px, py) in enumerate(peers):
                land = half(outs[t].at[2 * px + py], t, c)
                _remote(land, land, ici_s.at[t, k], ici_r.at[t, k], (px, py, c)).wait_recv()
                if split[t]:
                    fw = _remote(land, land, fwd_s.at[t, k], fwd_r.at[t, k], sib)
                    fw.start()
                    pending.append(fw.wait_send)
                    other = half(outs[t].at[2 * px + py], t, 1 - c)
                    pending.append(_remote(other, other, fwd_s.at[t, k], fwd_r.at[t, k], sib).wait_recv)
        for wait in pending:
            wait()

    out_shape = [_sds((CHIPS,) + sh.shape, sh.dtype) for sh in shards]
    dma = pltpu.SemaphoreType.DMA
    scratch = [dma((nt, 3)), dma((nt, 3)), dma((nt, 3)), dma((nt, 3)), dma((nt,)), dma((nt,))]
    return _comm_call(body, name="gather_weights", n_in=nt, out_shape=out_shape, scratch=scratch)(*shards)


def _pair_split(grads, name):
    nt = len(grads)

    def body(*refs):
        ins, got = refs[:nt], refs[nt:2 * nt]
        send_sems, recv_sems = refs[2 * nt:]
        x, y, c = _mesh_pos()
        cps = []
        for t in range(nt):
            hr = ins[t].shape[1] // 2
            give = ins[t].at[:, pl.ds(pl.multiple_of((1 - c) * hr, 8), hr), :]
            rc = _remote(give, got[t], send_sems.at[t], recv_sems.at[t], (x, y, 1 - c))
            rc.start()
            cps.append(rc)
        for rc in cps:
            rc.wait()

    half = [_sds((CHIPS, g.shape[1] // 2, g.shape[2]), g.dtype) for g in grads]
    scratch = [pltpu.SemaphoreType.DMA((nt,)), pltpu.SemaphoreType.DMA((nt,))]
    return _comm_call(body, name=name, n_in=nt, out_shape=half, scratch=scratch)(*grads)


def _chip_scatter(parts, name):
    nt = len(parts)

    def body(*refs):
        ins, outs = refs[:nt], refs[nt:2 * nt]
        send_sems, recv_sems = refs[2 * nt:]
        x, y, c = _mesh_pos()
        cps = []
        for t in range(nt):
            for k, (px, py) in enumerate(_chip_peers(x, y)):
                rc = _remote(ins[t].at[2 * px + py], outs[t].at[k], send_sems.at[t, k], recv_sems.at[t, k], (px, py, c))
                rc.start()
                cps.append(rc)
        for cp in cps:
            cp.wait()

    out_shape = [_sds((3,) + p.shape[1:], p.dtype) for p in parts]
    scratch = [pltpu.SemaphoreType.DMA((nt, 3)), pltpu.SemaphoreType.DMA((nt, 3))]
    return _comm_call(body, name=name, n_in=nt, out_shape=out_shape, scratch=scratch)(*parts)


def _pair_join(halves):
    nt = len(halves)

    def body(*refs):
        ins, got = refs[:nt], refs[nt:2 * nt]
        send_sems, recv_sems = refs[2 * nt:]
        x, y, c = _mesh_pos()
        cps = []
        for t in range(nt):
            rc = _remote(ins[t], got[t], send_sems.at[t], recv_sems.at[t], (x, y, 1 - c))
            rc.start()
            cps.append(rc)
        for rc in cps:
            rc.wait()

    out_shape = [_sds(hf.shape, hf.dtype) for hf in halves]
    scratch = [pltpu.SemaphoreType.DMA((nt,)), pltpu.SemaphoreType.DMA((nt,))]
    return _comm_call(body, name="grad_pair_join", n_in=nt, out_shape=out_shape, scratch=scratch)(*halves)


_HBM = pl.BlockSpec(memory_space=pltpu.HBM)
_SEMS = pl.BlockSpec(memory_space=pltpu.SEMAPHORE)
_EFFECT = pltpu.SideEffectType.DATAFLOW_SIDE_EFFECTING


def _bcast_copies(ins, lands, send_sems, recv_sems):
    x, y, c = _mesh_pos()
    me = 2 * x + y
    targets = [((px, py, c), 2 * px + py) for px, py in _chip_peers(x, y)] + [((x, y, 1 - c), me)]
    out = []
    for t in range(len(ins)):
        for k, (dev, idx) in enumerate(targets):
            i = t * len(targets) + k
            arrival = lambda t=t, i=i, idx=idx, dev=dev: _remote(ins[t], lands[t].at[idx], send_sems.at[i],
                                                                 recv_sems.at[i], dev)
            out.append((_remote(ins[t], lands[t].at[me], send_sems.at[i], recv_sems.at[i], dev), arrival))
    return out


def _scatter_copies(ins, lands, send_sems, recv_sems):
    x, y, c = _mesh_pos()
    out = []
    for t in range(len(ins)):
        for k, (px, py) in enumerate(_chip_peers(x, y)):
            i = t * 3 + k
            cp = _remote(ins[t].at[2 * px + py], lands[t].at[k], send_sems.at[i], recv_sems.at[i], (px, py, c))
            out.append((cp, lambda cp=cp: cp))
    return out


def _pair_copies(ins, lands, send_sems, recv_sems):
    x, y, c = _mesh_pos()
    out = []
    for t in range(len(ins)):
        hr = ins[t].shape[1] // 2
        give = ins[t].at[:, pl.ds(pl.multiple_of((1 - c) * hr, 8), hr), :]
        cp = _remote(give, lands[t], send_sems.at[t], recv_sems.at[t], (x, y, 1 - c))
        out.append((cp, lambda cp=cp: cp))
    return out


def _split_start(copies, srcs, land_shapes, ncopy, dep, name):
    nt = len(srcs)

    def body(*refs):
        ins, lands = refs[:nt], refs[nt:2 * nt]
        send_sems, recv_sems, token = refs[2 * nt + 1], refs[2 * nt + 2], refs[-1]
        for send, _ in copies(ins, lands, send_sems, recv_sems):
            send.start()
        token[...] = jnp.zeros_like(token)

    lands = [pltpu.with_memory_space_constraint(lax.empty(sh, a.dtype), pltpu.HBM) for sh, a in zip(land_shapes, srcs)]
    srcs = [pltpu.with_memory_space_constraint(a, pltpu.HBM) for a in srcs]
    dma = pltpu.SemaphoreType.DMA
    out_shape = ([dma((nt * ncopy,)), dma((nt * ncopy,))] + [pltpu.HBM(a.shape, a.dtype) for a in srcs + lands]
                 + [_sds((8, 128))])
    outs = pl.pallas_call(
        body, name=name, in_specs=[_HBM] * (2 * nt) + [_ANY],
        out_specs=[_SEMS, _SEMS] + [_HBM] * (2 * nt) + [pl.BlockSpec(memory_space=pltpu.VMEM)], out_shape=out_shape,
        input_output_aliases={i: 2 + i for i in range(2 * nt)},
        compiler_params=pltpu.CompilerParams(has_side_effects=_EFFECT))(*srcs, *lands, dep)
    return outs[0], outs[1], outs[2:2 + nt], outs[2 + nt:2 + 2 * nt], outs[-1]


def _split_wait(copies, send_sems, recv_sems, srcs, lands, after, name):
    nt = len(srcs)

    def body(*refs):
        ins, lnd = refs[:nt], refs[nt:2 * nt]
        for send, arrival in copies(ins, lnd, refs[2 * nt], refs[2 * nt + 1]):
            send.wait_send()
            arrival().wait_recv()

    outs = pl.pallas_call(
        body, name=name, in_specs=[_HBM] * (2 * nt) + [_SEMS, _SEMS, _ANY], out_specs=[_HBM] * (2 * nt),
        out_shape=[pltpu.HBM(a.shape, a.dtype) for a in list(srcs) + list(lands)],
        input_output_aliases={i: i for i in range(2 * nt)},
        compiler_params=pltpu.CompilerParams(has_side_effects=_EFFECT))(*srcs, *lands, send_sems, recv_sems, after)
    return outs[:nt], outs[nt:]


def _gather_small(packed):
    n = packed.shape[0]

    def body(in_ref, out_ref, send_sems, recv_sems, loc_sem):
        x, y, c = _mesh_pos()
        me = 4 * x + 2 * y + c
        lc = pltpu.make_async_copy(in_ref, out_ref.at[me], loc_sem)
        lc.start()
        peers = []
        for k in range(1, NDEV):
            px, py, pc = x ^ (k >> 2), y ^ ((k >> 1) & 1), c ^ (k & 1)
            rc = pltpu.make_async_remote_copy(src_ref=in_ref, dst_ref=out_ref.at[me], send_sem=send_sems.at[k - 1],
                                              recv_sem=recv_sems.at[k - 1], device_id=(px, py, pc), device_id_type=MESH)
            rc.start()
            peers.append((k, px, py, pc))
        lc.wait()
        for k, px, py, pc in peers:
            pltpu.make_async_remote_copy(src_ref=in_ref, dst_ref=out_ref.at[4 * px + 2 * py + pc],
                                         send_sem=send_sems.at[k - 1], recv_sem=recv_sems.at[k - 1],
                                         device_id=(px, py, pc), device_id_type=MESH).wait()

    scratch = [pltpu.SemaphoreType.DMA((NDEV - 1,)), pltpu.SemaphoreType.DMA((NDEV - 1,)), pltpu.SemaphoreType.DMA]
    return _comm_call(body, name="gather_small_grads", n_in=1, out_shape=[_sds((NDEV, n, 128))],
                      scratch=scratch)(packed)[0]


def _row_tile(r, c):
    t = r
    while t * c * 4 > (1 << 20) and t % 16 == 0:
        t //= 2
    return t


def _k_pair_add(full, got, name):
    g, r, c = full.shape
    hr = r // 2
    tr = _row_tile(hr, c)
    nh = hr // tr

    def body(a_ref, b_ref, o_ref):
        o_ref[...] = (a_ref[...] + b_ref[...]).astype(_WIRE)

    mine = pl.BlockSpec((None, tr, c), lambda i, j: (i, lax.axis_index("c") * nh + j, 0))
    spec = pl.BlockSpec((None, tr, c), lambda i, j: (i, j, 0))
    return _pc(body, name=name, grid=(g, nh), in_specs=[mine, spec], out_specs=[spec],
               out_shape=[_sds((g, hr, c), _WIRE)])(full, got)[0]


def _k_chip_sum(parts, slots, name):
    _, r, c = parts.shape
    tr = _row_tile(r, c)

    def body(a_ref, s_ref, o_ref):
        acc = a_ref[...].astype(F32)
        for k in range(3):
            acc = acc + s_ref[k].astype(F32)
        o_ref[...] = acc

    own = pl.BlockSpec((None, tr, c), lambda i: (2 * lax.axis_index("x") + lax.axis_index("y"), i, 0))
    return _pc(body, name=name, grid=(r // tr,), in_specs=[own, pl.BlockSpec((3, tr, c), lambda i: (0, i, 0))],
               out_specs=[_row(tr, c)], out_shape=[_sds((r, c))])(parts, slots)[0]


def _adam(w, g, m, v):
    m = ADAM_B1 * m + (1.0 - ADAM_B1) * g
    v = ADAM_B2 * v + (1.0 - ADAM_B2) * (g * g)
    m_hat = m / (1.0 - ADAM_B1 ** ADAM_STEP)
    v_hat = v / (1.0 - ADAM_B2 ** ADAM_STEP)
    return -ADAM_LR * (m_hat / (jnp.sqrt(v_hat) + ADAM_EPS) + ADAM_WD * w), m, v


def _k_adam(w, mine, theirs, m, v, name):
    r, c = w.shape
    hr = r // 2
    tr = _row_tile(hr, c)
    nh = hr // tr

    def body(w_ref, a_ref, b_ref, m_ref, v_ref, g_ref, d_ref, mo_ref, vo_ref):
        upper = (pl.program_id(0) >= nh).astype(jnp.int32)
        g = jnp.where(upper == lax.axis_index("c"), a_ref[...], b_ref[...])
        g_ref[...] = g
        d_ref[...], mo_ref[...], vo_ref[...] = _adam(w_ref[...], g, m_ref[...], v_ref[...])

    hspec = pl.BlockSpec((tr, c), lambda i: (jnp.where(i >= nh, i - nh, i), 0))
    return _pc(body, name=name, grid=(r // tr,), in_specs=[_row(tr, c), hspec, hspec, _row(tr, c), _row(tr, c)],
               out_specs=[_row(tr, c)] * 4, out_shape=[_sds((r, c))] * 4)(w, mine, theirs, m, v)


def _k_sum8(a):
    _, n, _ = a.shape

    def body(a_ref, o_ref):
        acc = a_ref[0]
        for k in range(1, NDEV):
            acc = acc + a_ref[k]
        o_ref[...] = acc

    return _pc(body, name="sum_small_grads", grid=(1,), in_specs=[_acc(a.shape)], out_specs=[_acc((n, 128))],
               out_shape=[_sds((n, 128))])(a)[0]


def _k_adam_small(ws, gs, ms, vs):
    n = len(ws)

    def body(*refs):
        for k in range(n):
            w_ref, g_ref, m_ref, v_ref, d_ref, mo_ref, vo_ref = refs[k::n]
            d_ref[...], mo_ref[...], vo_ref[...] = _adam(w_ref[...], g_ref[...], m_ref[...], v_ref[...])

    specs = [_acc(a.shape) for a in ws]
    outs = _pc(body, name="adam_small", grid=(1,), in_specs=specs * 4, out_specs=specs * 3,
               out_shape=[_sds(a.shape) for a in ws] * 3)(*ws, *gs, *ms, *vs)
    return outs[:n], outs[n:2 * n], outs[2 * n:]


def _pack(vals):
    rows = []
    for a in vals:
        flat = a.reshape(-1)
        n = -(-flat.shape[0] // 1024) * 1024
        rows.append(jnp.pad(flat, (0, n - flat.shape[0])).reshape(n // 128, 128))
    return jnp.concatenate(rows, axis=0)


def _unpack(packed, shapes):
    out, off = [], 0
    for sh in shapes:
        size = int(np.prod(sh))
        n = -(-size // 1024) * 1024
        out.append(packed[off // 128:(off + n) // 128].reshape(-1)[:size].reshape(sh))
        off += n
    return out


_WEIGHTS = ["attn_norm", "w_in", "a_q_norm", "a_k_norm", "b_q_norm", "b_k_norm", "b_sinks", "mem_norm", "w_mem_kv",
            "m_q_norm", "m_k_norm", "w_o_a", "w_o_b", "w_o_m", "w_gate", "b_gate", "w_out", "ffn_norm", "w_up",
            "conv_w", "conv_b", "w_down"]
_BIG = ["w_in", "w_mem_kv", "w_o_a", "w_o_b", "w_o_m", "w_gate", "w_out", "w_up", "w_down"]
_SMALL = [n for n in _WEIGHTS if n not in _BIG]


def kernel(x, mem, positions, attn_norm, w_in, a_q_norm, a_k_norm, b_q_norm, b_k_norm, b_sinks, mem_norm, w_mem_kv, m_q_norm, m_k_norm, w_o_a, w_o_b, w_o_m, w_gate, b_gate, w_out, ffn_norm, w_up, conv_w, conv_b, w_down, loss_target, m_attn_norm, m_w_in, m_a_q_norm, m_a_k_norm, m_b_q_norm, m_b_k_norm, m_b_sinks, m_mem_norm, m_w_mem_kv, m_m_q_norm, m_m_k_norm, m_w_o_a, m_w_o_b, m_w_o_m, m_w_gate, m_b_gate, m_w_out, m_ffn_norm, m_w_up, m_conv_w, m_conv_b, m_w_down, v_attn_norm, v_w_in, v_a_q_norm, v_a_k_norm, v_b_q_norm, v_b_k_norm, v_b_sinks, v_mem_norm, v_w_mem_kv, v_m_q_norm, v_m_k_norm, v_w_o_a, v_w_o_b, v_w_o_m, v_w_gate, v_b_gate, v_w_out, v_ffn_norm, v_w_up, v_conv_w, v_conv_b, v_w_down):
    given = dict(locals())
    w = {n: given[n][0] for n in _WEIGHTS}
    m1 = {n: given["m_" + n][0] for n in _WEIGHTS}
    m2 = {n: given["v_" + n][0] for n in _WEIGHTS}

    w_in = _gather_shards([w["w_in"].astype(_MM)])[0]
    stages = (["w_gate", "w_mem_kv", "w_o_a", "w_o_b", "w_o_m", "w_out"], ["w_up", "w_down", "conv_w"])
    tok, started = w_in, []
    for k, names in enumerate(stages):
        shards = [w[n] if n == "conv_w" else w[n].astype(_MM) for n in names]
        *handles, tok = _split_start(_bcast_copies, shards, [(CHIPS,) + a.shape for a in shards], 4, tok,
                                     "gather_start_%d" % k)
        started.append(handles)
    small = {n: (w[n][None, :] if w[n].ndim == 1 else w[n]) for n in _SMALL if n != "conv_w"}
    small["attn_norm"] = small["attn_norm"] + tok[0:1, 0:1]

    def get_rest(stage, after):
        send, recv, srcs, lands = started[stage]
        got = _split_wait(_bcast_copies, send, recv, srcs, lands, after, "gather_wait_%d" % stage)[1]
        wts = dict(zip(stages[stage], got))
        for n in ("w_mem_kv", "w_out", "w_down"):
            if n in wts:
                wts[n] = wts[n].reshape(-1, wts[n].shape[-1])
        return wts

    parts, slots, pair, scat = {}, {}, [], []
    zeros = jnp.zeros((8, 128), F32)

    def finish_pair(after):
        names, tag, send, recv, srcs, lands = pair.pop()
        full, got = _split_wait(_pair_copies, send, recv, srcs, lands, after, "pair_wait_" + tag)
        mine = [_k_pair_add(f, b, "pair_add_" + n) for n, f, b in zip(names, full, got)]
        shapes = [(3,) + p.shape[1:] for p in mine]
        send, recv, srcs, lands, token = _split_start(_scatter_copies, mine, shapes, 3, zeros, "scatter_start_" + tag)
        scat.append((names, tag, send, recv, srcs, lands))
        return token

    def on_grads(group, after):
        names = list(group)
        tag = "_".join(names)
        token = finish_pair(after) if pair else zeros
        if not group:
            return token
        grads_g = [group[n] for n in names]
        if "w_in" in group:
            got = _pair_split(grads_g, "grad_pair_split_" + tag)
            for n, f, b in zip(names, grads_g, got):
                parts[n] = _k_pair_add(f, b, "pair_add_" + n)
            slots.update(zip(names, _chip_scatter([parts[n] for n in names], "grad_chip_scatter_" + tag)))
            return token
        shapes = [(CHIPS, g.shape[1] // 2, g.shape[2]) for g in grads_g]
        send, recv, srcs, lands, token = _split_start(_pair_copies, grads_g, shapes, 1, token, "pair_start_" + tag)
        pair.append((names, tag, send, recv, srcs, lands))
        return token

    loss, grad_x, sml = _local_step(x[0], mem[0], positions[0], loss_target[0], small, w_in, get_rest, on_grads)
    loss = lax.psum(loss, ("x", "y", "c"))
    for names, tag, send, recv, srcs, lands in scat:
        mine, got = _split_wait(_scatter_copies, send, recv, srcs, lands, slots["w_in"], "scatter_wait_" + tag)
        parts.update(zip(names, mine))
        slots.update(zip(names, got))
    mine = [_k_chip_sum(parts[n], slots[n], "chip_add_" + n) for n in _BIG]
    theirs = _pair_join(mine)
    grads = {}

    shapes = [sml[n].shape for n in _SMALL]
    gsm = dict(zip(_SMALL, _unpack(_k_sum8(_gather_small(_pack([sml[n] for n in _SMALL]))), shapes)))
    nu = w["conv_w"].shape[1]
    chip = 2 * lax.axis_index("x") + lax.axis_index("y")
    gsm["conv_w"] = lax.dynamic_slice_in_dim(gsm["conv_w"], chip * nu, nu, axis=1)
    for n in _SMALL:
        grads[n] = gsm[n].reshape(w[n].shape)

    delta, new_m, new_v = {}, {}, {}
    for n, a, b in zip(_BIG, mine, theirs):
        grads[n], delta[n], new_m[n], new_v[n] = _k_adam(w[n], a, b, m1[n], m2[n], "adam_" + n)
    as2d = lambda d: [d[n][None, :] if d[n].ndim == 1 else d[n] for n in _SMALL]
    for dst, outs in zip((delta, new_m, new_v), _k_adam_small(as2d(w), as2d(grads), as2d(m1), as2d(m2))):
        dst.update((n, a.reshape(w[n].shape)) for n, a in zip(_SMALL, outs))

    lead = lambda d: [d[n][None] for n in _WEIGHTS]
    return (loss, grad_x[None], *lead(grads), *lead(delta), *lead(new_m), *lead(new_v))
```

```python
import math

import jax
import jax.numpy as jnp
import numpy as np
from jax import lax
from jax.experimental import pallas as pl
from jax.experimental.pallas import tpu as pltpu

F32 = jnp.float32
_MM = jnp.bfloat16
_WIRE = jnp.bfloat16

D_MODEL = 1024
HEAD = 64
BLK = 128
A_GROUPS = ((128, 1), (512, 4), (2048, 16))
A_HEADS = 4
A_W = A_HEADS * HEAD
B_QH = 8
B_KVH = 2
B_WINDOW = 128
M_HEADS = 4
M_HD = 128
M_W = M_HEADS * M_HD
D_FF = 2816
EPS = 1e-6
NEG = -1e30
ROPE_THETA = 500000.0
ROPE_ROT = 16
CHIPS = 4
NDEV = 8
ADAM_LR, ADAM_B1, ADAM_B2, ADAM_EPS, ADAM_WD, ADAM_STEP = 0.001, 0.9, 0.999, 1e-08, 0.01, 10
VMEM_LIMIT = 58 * 1024 * 1024
MESH = pl.DeviceIdType.MESH


def _pc(body, *, name, grid, in_specs, out_specs, out_shape, scratch=()):
    return pl.pallas_call(
        body, name=name, grid=grid, in_specs=in_specs, out_specs=out_specs, out_shape=out_shape,
        scratch_shapes=list(scratch),
        compiler_params=pltpu.CompilerParams(dimension_semantics=("arbitrary",) * len(grid),
                                             vmem_limit_bytes=VMEM_LIMIT))


def _row(ts, c, col=0):
    return pl.BlockSpec((ts, c), lambda i: (i, col))


def _res(shape):
    n = len(shape)
    return pl.BlockSpec(tuple(shape), lambda i: (0,) * n, pipeline_mode=pl.Buffered(1))


def _acc(shape):
    n = len(shape)
    return pl.BlockSpec(tuple(shape), lambda i: (0,) * n)


def _sds(shape, dtype=F32):
    return jax.ShapeDtypeStruct(tuple(shape), dtype)


def _dot(a, b):
    return jnp.dot(a.astype(_MM), b.astype(_MM), preferred_element_type=F32)


def _dot_nt(a, b):
    return lax.dot_general(a.astype(_MM), b.astype(_MM), (((1,), (1,)), ((), ())), preferred_element_type=F32)


def _dot_tn(a, b):
    return lax.dot_general(a.astype(_MM), b.astype(_MM), (((0,), (0,)), ((), ())), preferred_element_type=F32)


def _sum8(v):
    ts, c = v.shape
    return jnp.sum(v.reshape(ts // 8, 8, c), axis=0)


def _sigmoid(z):
    return 1.0 / (1.0 + jnp.exp(-z))


def _rms(x):
    r = lax.rsqrt(jnp.mean(x * x, axis=-1, keepdims=True) + EPS)
    return x * r, r


def _rms_bwd(dy, xh, r, gain):
    z = dy * gain
    return r * (z - xh * jnp.mean(z * xh, axis=-1, keepdims=True))


def _split_hi_lo(v):
    hi = v.astype(_MM)
    return hi, (v - hi.astype(F32)).astype(_MM)


def _lane_head(shape):
    return lax.shift_right_logical(lax.broadcasted_iota(jnp.int32, shape, len(shape) - 1), 6)


def _seg_sum64(v):
    w = v.shape[1]
    e = jnp.where(_lane_head((w, w)) == lax.shift_right_logical(lax.broadcasted_iota(jnp.int32, (w, w), 0), 6),
                  1.0, 0.0).astype(_MM)
    hi, lo = _split_hi_lo(v)
    return jnp.dot(hi, e, preferred_element_type=F32) + jnp.dot(lo, e, preferred_element_type=F32)


def _seg_norm(x, seg):
    if seg == HEAD:
        r = lax.rsqrt(_seg_sum64(x * x) * (1.0 / HEAD) + EPS)
        return x * r, r
    w = x.shape[1]
    xh, rr = [], []
    for s in range(w // seg):
        xs = x[:, s * seg:(s + 1) * seg]
        r = lax.rsqrt(jnp.mean(xs * xs, axis=-1, keepdims=True) + EPS)
        xh.append(xs * r)
        rr.append(jnp.broadcast_to(r, xs.shape))
    return jnp.concatenate(xh, axis=1), jnp.concatenate(rr, axis=1)


def _seg_mean(v, seg):
    if seg == HEAD:
        return _seg_sum64(v) * (1.0 / HEAD)
    w = v.shape[1]
    out = []
    for s in range(w // seg):
        vs = v[:, s * seg:(s + 1) * seg]
        out.append(jnp.broadcast_to(jnp.mean(vs, axis=-1, keepdims=True), vs.shape))
    return jnp.concatenate(out, axis=1)


def _rope(t, c, sa, sb):
    out = []
    for cb in range(t.shape[1] // 128):
        tc = t[:, cb * 128:(cb + 1) * 128]
        out.append(tc * c + pltpu.roll(tc, 120, 1) * sa + pltpu.roll(tc, 8, 1) * sb)
    return jnp.concatenate(out, axis=1) if len(out) > 1 else out[0]


def _rope_bwd(dy, c, sa, sb):
    out = []
    for cb in range(dy.shape[1] // 128):
        dc = dy[:, cb * 128:(cb + 1) * 128]
        out.append(dc * c + pltpu.roll(dc * sa, 8, 1) + pltpu.roll(dc * sb, 120, 1))
    return jnp.concatenate(out, axis=1) if len(out) > 1 else out[0]


def _rope_consts():
    half = ROPE_ROT // 2
    c = np.float32(-2.0 * math.log(ROPE_THETA) / ROPE_ROT)
    freqs = np.exp(np.arange(half, dtype=np.float32) * c).astype(np.float32)
    place = np.zeros((3, half, 128), np.float32)
    ones = np.zeros((1, 128), np.float32)
    for lane in range(128):
        d = lane % HEAD
        if d < half:
            place[0, d, lane], place[1, d, lane] = 1.0, -1.0
        elif d < ROPE_ROT:
            place[0, d - half, lane], place[2, d - half, lane] = 1.0, 1.0
        else:
            ones[0, lane] = 1.0
    return np.tile(freqs[:, None], (1, 128)), place, ones


def _rope_tables(pos_rows):
    r = pos_rows.shape[0]
    tr = min(1024, r)
    freqs, place, ones = _rope_consts()

    def split3(v):
        hi, mid = _split_hi_lo(v)
        lo = (v - hi.astype(F32) - mid.astype(F32)).astype(_MM)
        return hi, mid, lo

    def body(p_ref, f_ref, e_ref, one_ref, c_ref, sa_ref, sb_ref):
        for j in range(tr // 128):
            ang = p_ref[j:j + 1, :].astype(F32) * f_ref[...]
            rows = slice(j * 128, (j + 1) * 128)
            for ref, k, v in ((c_ref, 0, jnp.cos(ang)), (sa_ref, 1, jnp.sin(ang)), (sb_ref, 2, jnp.sin(ang))):
                e = e_ref[k].astype(_MM)
                out = sum(_dot_tn(part, e) for part in split3(v))
                ref[rows, :] = out + one_ref[...] if k == 0 else out

    return _pc(body, name="rope_tables", grid=(r // tr,),
               in_specs=[pl.BlockSpec((tr // 128, 128), lambda i: (i, 0)), _acc((ROPE_ROT // 2, 128)),
                         _acc((3, ROPE_ROT // 2, 128)), _acc((1, 128))],
               out_specs=[_row(tr, 128)] * 3, out_shape=[_sds((r, 128))] * 3)(
                   pos_rows.reshape(r // 128, 128), jnp.asarray(freqs), jnp.asarray(place), jnp.asarray(ones))


def _k_in(x, g1, w_in):
    s = x.shape[0]
    ts = min(256, s)
    nin = w_in.shape[2]
    ncol = CHIPS * nin
    a_cols = 3 * A_W
    offs = [0, a_cols, 2 * a_cols, 3 * a_cols, 3 * a_cols + B_QH * HEAD,
            3 * a_cols + (B_QH + B_KVH) * HEAD, 3 * a_cols + (B_QH + 2 * B_KVH) * HEAD, ncol]

    def body(x_ref, g_ref, wi_ref, h_ref, a0, a1, a2, qb, kb, vb, mq, p_scr):
        xh, _ = _rms(x_ref[...])
        h = (xh * g_ref[...]).astype(_MM)
        h_ref[...] = h
        for j in range(CHIPS):
            p_scr[:, j * nin:(j + 1) * nin] = jnp.dot(h, wi_ref[j], preferred_element_type=F32)
        for k, ref in enumerate((a0, a1, a2, qb, kb, vb, mq)):
            ref[...] = p_scr[:, offs[k]:offs[k + 1]]

    widths = [offs[k + 1] - offs[k] for k in range(7)]
    return _pc(
        body, name="in_proj", grid=(s // ts,),
        in_specs=[_row(ts, D_MODEL), _res((1, D_MODEL)), _res(w_in.shape)],
        out_specs=[_row(ts, D_MODEL)] + [_row(ts, w) for w in widths],
        out_shape=[_sds((s, D_MODEL), _MM)] + [_sds((s, w)) for w in widths],
        scratch=[pltpu.VMEM((ts, ncol), F32)])(x, g1, w_in)


def _k_gate(h, w_gate, b_gate):
    s = h.shape[0]
    ts = min(256, s)
    ng = w_gate.shape[2]

    def body(h_ref, wg_ref, bg_ref, gt_ref):
        h = h_ref[...]
        for j in range(CHIPS):
            z = jnp.dot(h, wg_ref[j], preferred_element_type=F32) + bg_ref[:, j * ng:(j + 1) * ng]
            gt_ref[:, j * ng:(j + 1) * ng] = _sigmoid(z)

    return _pc(body, name="gate_proj", grid=(s // ts,),
               in_specs=[_row(ts, D_MODEL), _res(w_gate.shape), _res(b_gate.shape)],
               out_specs=[_row(ts, CHIPS * ng)], out_shape=[_sds((s, CHIPS * ng))])(h, w_gate, b_gate)[0]


def _k_prep(srcs, gq, gk, tabs, tab_row, *, wq, wk, rows_per_gain, name):
    rows = srcs[0][0].shape[0]
    ts = min(256, rows)

    def body(q_ref, k_ref, v_ref, gq_ref, gk_ref, c_ref, sa_ref, sb_ref, qn_ref, kn_ref, vn_ref):
        c, sa, sb = c_ref[...], sa_ref[...], sb_ref[...]
        qh, _ = _seg_norm(q_ref[...], HEAD)
        qn_ref[...] = _rope(qh * gq_ref[...], c, sa, sb).astype(_MM)
        kh, _ = _seg_norm(k_ref[...], HEAD)
        kn_ref[...] = _rope(kh * gk_ref[...], c, sa, sb).astype(_MM)
        vn_ref[...] = v_ref[...].astype(_MM)

    gspec = lambda w: pl.BlockSpec((None, 1, w), lambda i: ((i * ts) // rows_per_gain, 0, 0))
    return _pc(
        body, name=name, grid=(rows // ts,),
        in_specs=[_row(ts, wq, srcs[0][1]), _row(ts, wk, srcs[1][1]), _row(ts, wk, srcs[2][1]),
                  gspec(wq), gspec(wk)] + [pl.BlockSpec((ts, 128), lambda i: (i + tab_row // ts, 0))] * 3,
        out_specs=[_row(ts, wq), _row(ts, wk), _row(ts, wk)],
        out_shape=[_sds((rows, wq), _MM), _sds((rows, wk), _MM), _sds((rows, wk), _MM)])(
            srcs[0][0], srcs[1][0], srcs[2][0], gq, gk, *tabs)


def _first_flag(b, segs, nb):
    first = b >= nb
    for k, (start, period) in enumerate(segs):
        end = segs[k + 1][0] if k + 1 < len(segs) else nb
        first = first | ((b >= start) & (b < end) & (lax.rem(b - start, jnp.int32(period)) == 0))
    return first


def _band_bias(thr, with_cur):
    qi = lax.broadcasted_iota(jnp.int32, (BLK, BLK), 0)
    kj = lax.broadcasted_iota(jnp.int32, (BLK, BLK), 1)
    prev = jnp.where(kj >= qi + thr, 0.0, NEG)
    return jnp.concatenate([prev, jnp.where(kj <= qi, 0.0, NEG)], axis=1) if with_cur else prev


def _blockdiag(t4):
    head = _lane_head((1, A_W))
    return jnp.concatenate([t4 * jnp.where(head == h, 1.0, 0.0).astype(t4.dtype) for h in range(A_HEADS)], axis=0)


def _fold_diag(t, n):
    head = _lane_head((n, A_W))
    out = t[3 * n:4 * n]
    for h in (2, 1, 0):
        out = jnp.where(head == h, t[h * n:(h + 1) * n], out)
    return out


def _expand_heads(cols):
    n = cols[0].shape[0]
    head = _lane_head((n, A_W))
    out = jnp.broadcast_to(cols[3], (n, A_W))
    for h in (2, 1, 0):
        out = jnp.where(head == h, cols[h], out)
    return out


def _unit_kv(p_ref, c_ref, u, shared):
    if not shared:
        return jnp.concatenate([p_ref[:, u * A_W:(u + 1) * A_W], c_ref[:, u * A_W:(u + 1) * A_W]], axis=0)
    kg = jnp.concatenate([p_ref[:, u * HEAD:(u + 1) * HEAD], c_ref[:, u * HEAD:(u + 1) * HEAD]], axis=0)
    return jnp.concatenate([kg] * A_HEADS, axis=1)


def _k_band_fwd(qn, kn, vn, *, hq, hk, max_dist, segs, sink, name):
    rows = qn.shape[0]
    nb = rows // BLK
    units = hq // A_HEADS
    shared = hk != hq
    wq, wk = hq * HEAD, hk * HEAD
    scale = HEAD ** -0.5

    def body(*refs):
        if sink is None:
            q_ref, kc_ref, kp_ref, vc_ref, vp_ref, o_ref, l_ref = refs
        else:
            q_ref, kc_ref, kp_ref, vc_ref, vp_ref, sk_ref, o_ref, l_ref = refs
        b = pl.program_id(0)
        bias = _band_bias(jnp.where(_first_flag(b, segs, nb), 1 << 20, BLK - max_dist), True)
        for u in range(units):
            us = slice(u * A_W, (u + 1) * A_W)
            kb = _blockdiag(_unit_kv(kp_ref, kc_ref, u, shared))
            vb = _blockdiag(_unit_kv(vp_ref, vc_ref, u, shared))
            s_all = _dot_nt(q_ref[:, us], kb) * scale
            ps, ls = [], []
            for h in range(A_HEADS):
                s = s_all[:, h * 2 * BLK:(h + 1) * 2 * BLK] + bias
                m = jnp.max(s, axis=-1, keepdims=True)
                e = jnp.exp(s - m)
                lse = m + jnp.log(jnp.sum(e, axis=-1, keepdims=True))
                if sink is not None:
                    sk = sk_ref[u * A_HEADS + h]
                    mx = jnp.maximum(lse, sk)
                    lse = mx + jnp.log(jnp.exp(lse - mx) + jnp.exp(sk - mx))
                ps.append((e * jnp.exp(m - lse)).astype(_MM))
                ls.append(lse)
            o_ref[:, us] = _dot(jnp.concatenate(ps, axis=1), vb)
            l_ref[:, us] = _expand_heads(ls)

    cur = lambda w: pl.BlockSpec((BLK, w), lambda i: (i, 0))
    prev = lambda w: pl.BlockSpec((BLK, w), lambda i: (jnp.maximum(i - 1, 0), 0))
    in_specs = [cur(wq), cur(wk), prev(wk), cur(wk), prev(wk)]
    args = [qn, kn, kn, vn, vn]
    if sink is not None:
        in_specs.append(pl.BlockSpec(memory_space=pltpu.SMEM))
        args.append(sink)
    return _pc(body, name=name, grid=(nb,), in_specs=in_specs, out_specs=[cur(wq), cur(wq)],
               out_shape=[_sds((rows, wq)), _sds((rows, wq))])(*args)


def _k_memkv(mem, mem_norm, w_kv, m_k_norm):
    n = mem.shape[0]

    def body(m_ref, g_ref, w_ref, gk_ref, mn_ref, kv_ref, mk_ref, mv_ref):
        mh, _ = _rms(m_ref[...])
        mn = (mh * g_ref[...]).astype(_MM)
        mn_ref[...] = mn
        kv = jnp.dot(mn, w_ref[...], preferred_element_type=F32)
        kv_ref[...] = kv
        kh, _ = _seg_norm(kv[:, :M_W], M_HD)
        mk_ref[...] = (kh * gk_ref[...]).astype(_MM)
        mv_ref[...] = kv[:, M_W:].astype(_MM)

    return _pc(body, name="mem_kv", grid=(1,),
               in_specs=[_acc((n, D_MODEL)), _acc((1, D_MODEL)), _acc(w_kv.shape), _acc((1, M_W))],
               out_specs=[_acc((n, D_MODEL)), _acc((n, 2 * M_W)), _acc((n, M_W)), _acc((n, M_W))],
               out_shape=[_sds((n, D_MODEL), _MM), _sds((n, 2 * M_W)), _sds((n, M_W), _MM), _sds((n, M_W), _MM)])(
                   mem, mem_norm, w_kv, m_k_norm)


def _mem_probs(q, mk):
    sc = _dot_nt(q, mk) * (M_HD ** -0.5)
    e = jnp.exp(sc - jnp.max(sc, axis=-1, keepdims=True))
    return e / jnp.sum(e, axis=-1, keepdims=True)


def _k_mem_fwd(m_q, gq, mk, mv):
    s = m_q.shape[0]
    n = mk.shape[0]
    ts = min(256, s)

    def body(q_ref, g_ref, mk_ref, mv_ref, o_ref):
        qh, _ = _seg_norm(q_ref[...], M_HD)
        qn = (qh * g_ref[...]).astype(_MM)
        for h in range(M_HEADS):
            hs = slice(h * M_HD, (h + 1) * M_HD)
            o_ref[:, hs] = _dot(_mem_probs(qn[:, hs], mk_ref[:, hs]), mv_ref[:, hs])

    return _pc(body, name="mem_attn", grid=(s // ts,),
               in_specs=[_row(ts, M_W), _res((1, M_W)), _res((n, M_W)), _res((n, M_W))],
               out_specs=[_row(ts, M_W)], out_shape=[_sds((s, M_W))])(m_q, gq, mk, mv)[0]


def _group_weights(l0, l1, l2):
    m = jnp.maximum(jnp.maximum(l0, l1), l2)
    e0, e1, e2 = jnp.exp(l0 - m), jnp.exp(l1 - m), jnp.exp(l2 - m)
    inv = 1.0 / (e0 + e1 + e2)
    return e0 * inv, e1 * inv, e2 * inv


def _branch_products(oa, ob, om, woa_ref, wob_ref, wom_ref, j):
    return _dot(oa, woa_ref[j]), _dot(ob, wob_ref[j]), _dot(om, wom_ref[j])


def _k_merge(og, lg, o_b, o_m, gates, x, w_oa, w_ob, w_om, w_out, g2):
    s = x.shape[0]
    ts = min(256, s)
    nc = w_oa.shape[2]

    def body(o0, o1, o2, l0, l1, l2, ob_ref, om_ref, gt_ref, x_ref, woa, wob, wom, wout, g_ref,
             oa_ref, mer_ref, x1_ref, h2_ref, m_scr):
        w0, w1, w2 = _group_weights(l0[...], l1[...], l2[...])
        oa = w0 * o0[...] + w1 * o1[...] + w2 * o2[...]
        oa_ref[...] = oa
        ob, om = ob_ref[...], om_ref[...]
        for j in range(CHIPS):
            pa, pb, pm = _branch_products(oa, ob, om, woa, wob, wom, j)
            cs = lambda br: slice(br * D_MODEL + j * nc, br * D_MODEL + (j + 1) * nc)
            m_scr[:, j * nc:(j + 1) * nc] = gt_ref[:, cs(0)] * pa + gt_ref[:, cs(1)] * pb + gt_ref[:, cs(2)] * pm
        mer = m_scr[...].astype(_MM)
        mer_ref[...] = mer
        x1 = x_ref[...] + jnp.dot(mer, wout[...], preferred_element_type=F32)
        x1_ref[...] = x1
        xh, _ = _rms(x1)
        h2_ref[...] = (xh * g_ref[...]).astype(_MM)

    return _pc(
        body, name="merge_out", grid=(s // ts,),
        in_specs=[_row(ts, A_W)] * 6 + [_row(ts, B_QH * HEAD), _row(ts, M_W), _row(ts, 3 * D_MODEL), _row(ts, D_MODEL),
                                         _res(w_oa.shape), _res(w_ob.shape), _res(w_om.shape), _res(w_out.shape),
                                         _res((1, D_MODEL))],
        out_specs=[_row(ts, A_W), _row(ts, D_MODEL), _row(ts, D_MODEL), _row(ts, D_MODEL)],
        out_shape=[_sds((s, A_W)), _sds((s, D_MODEL), _MM), _sds((s, D_MODEL)), _sds((s, D_MODEL), _MM)],
        scratch=[pltpu.VMEM((ts, D_MODEL), F32)])(*og, *lg, o_b, o_m, gates, x, w_oa, w_ob, w_om, w_out, g2)


def _k_up(h2, w_up):
    s = h2.shape[0]
    ts = min(256, s)
    nu = w_up.shape[2]

    def body(h_ref, w_ref, u_ref):
        h = h_ref[...]
        for j in range(CHIPS):
            u_ref[:, j * nu:(j + 1) * nu] = jnp.dot(h, w_ref[j], preferred_element_type=F32)

    return _pc(body, name="up_proj", grid=(s // ts,), in_specs=[_row(ts, D_MODEL), _res(w_up.shape)],
               out_specs=[_row(ts, CHIPS * nu)], out_shape=[_sds((s, CHIPS * nu))])(h2, w_up)[0]


def _shift_down(v, halo, k):
    ts = v.shape[0]
    row = lax.broadcasted_iota(jnp.int32, v.shape, 0)
    out = pltpu.roll(v, k, 0)
    for r in range(k):
        out = jnp.where(row == r, halo[8 - k + r:8 - k + r + 1, :], out)
    return out


def _shift_up(v, halo, k):
    ts = v.shape[0]
    row = lax.broadcasted_iota(jnp.int32, v.shape, 0)
    out = pltpu.roll(v, ts - k, 0)
    for r in range(k):
        out = jnp.where(row == ts - k + r, halo[r:r + 1, :], out)
    return out


def _k_ffn(u, conv_w, conv_b, w_down, x1, target):
    s = u.shape[0]
    ts = min(256, s)
    nu = conv_w.shape[2]
    half = CHIPS // 2

    def body(u_ref, uh_ref, cw_ref, cb_ref, wd_ref, x1_ref, t_ref, dy_ref, f_ref, dc_ref, loss_ref, c_scr, f_scr):
        i = pl.program_id(0)
        halo = jnp.where(i > 0, uh_ref[...], 0.0)
        for j in range(CHIPS):
            cs = slice(j * nu, (j + 1) * nu)
            uj = u_ref[:, cs]
            hj = halo[:, cs]
            c_scr[:, cs] = (cb_ref[:, cs] + cw_ref[j, 0:1, :] * _shift_down(uj, hj, 2)
                            + cw_ref[j, 1:2, :] * _shift_down(uj, hj, 1) + cw_ref[j, 2:3, :] * uj)
        for j in range(half):
            a = c_scr[:, j * nu:(j + 1) * nu]
            g = c_scr[:, (half + j) * nu:(half + j + 1) * nu]
            f_scr[:, j * nu:(j + 1) * nu] = (a * _sigmoid(a) * g).astype(_MM)
        f = f_scr[...]
        f_ref[...] = f
        y = x1_ref[...] + jnp.dot(f, wd_ref[...], preferred_element_type=F32)
        err = y - t_ref[...]
        dy = err * (1.0 / D_MODEL)
        dy_ref[...] = dy

        @pl.when(i == 0)
        def _():
            loss_ref[...] = jnp.zeros_like(loss_ref)

        loss_ref[...] += _sum8(err * err)
        df = _dot_nt(dy, wd_ref[...])
        for j in range(half):
            a = c_scr[:, j * nu:(j + 1) * nu]
            g = c_scr[:, (half + j) * nu:(half + j + 1) * nu]
            sa = _sigmoid(a)
            dfj = df[:, j * nu:(j + 1) * nu]
            dc_ref[:, j * nu:(j + 1) * nu] = dfj * g * (sa * (1.0 + a * (1.0 - sa)))
            dc_ref[:, (half + j) * nu:(half + j + 1) * nu] = dfj * (a * sa)

    wide = CHIPS * nu
    return _pc(
        body, name="conv_ffn", grid=(s // ts,),
        in_specs=[_row(ts, wide), pl.BlockSpec((8, wide), lambda i: (jnp.maximum(i * (ts // 8) - 1, 0), 0)),
                  _res(conv_w.shape), _res((1, wide)), _res(w_down.shape), _row(ts, D_MODEL), _row(ts, D_MODEL)],
        out_specs=[_row(ts, D_MODEL), _row(ts, D_FF), _row(ts, wide), _acc((8, D_MODEL))],
        out_shape=[_sds((s, D_MODEL)), _sds((s, D_FF), _MM), _sds((s, wide)), _sds((8, D_MODEL))],
        scratch=[pltpu.VMEM((ts, wide), F32), pltpu.VMEM((ts, D_FF), _MM)])(u, u, conv_w, conv_b, w_down, x1, target)


def _k_conv_bwd(dc, u, conv_w, w_up, x1, g2, dy):
    s = u.shape[0]
    ts = min(256, s)
    nu = conv_w.shape[2]
    wide = CHIPS * nu
    last = s // ts - 1

    def body(dc_ref, dn_ref, u_ref, cw_ref, wu_ref, x1_ref, g_ref, dy_ref, dx1_ref, du_ref, cacc_ref, gacc_ref):
        i = pl.program_id(0)

        @pl.when(i == 0)
        def _():
            cacc_ref[...] = jnp.zeros_like(cacc_ref)
            gacc_ref[...] = jnp.zeros_like(gacc_ref)

        dhalo = jnp.where(i < last, dn_ref[...], 0.0)
        dh2 = jnp.zeros((ts, D_MODEL), F32)
        for j in range(CHIPS):
            cs = slice(j * nu, (j + 1) * nu)
            dcj, uj = dc_ref[:, cs], u_ref[:, cs]
            dc1, dc2 = _shift_up(dcj, dhalo[:, cs], 1), _shift_up(dcj, dhalo[:, cs], 2)
            cacc_ref[0, :, cs] += _sum8(dcj)
            cacc_ref[1, :, cs] += _sum8(dc2 * uj)
            cacc_ref[2, :, cs] += _sum8(dc1 * uj)
            cacc_ref[3, :, cs] += _sum8(dcj * uj)
            du = (cw_ref[j, 2:3, :] * dcj + cw_ref[j, 1:2, :] * dc1 + cw_ref[j, 0:1, :] * dc2).astype(_MM)
            du_ref[:, cs] = du
            dh2 = dh2 + _dot_nt(du, wu_ref[j])
        xh, r = _rms(x1_ref[...])
        gacc_ref[...] += _sum8(dh2 * xh)
        dx1_ref[...] = dy_ref[...] + _rms_bwd(dh2, xh, r, g_ref[...])

    return _pc(
        body, name="conv_up_bwd", grid=(s // ts,),
        in_specs=[_row(ts, wide),
                  pl.BlockSpec((8, wide), lambda i: (jnp.minimum((i + 1) * (ts // 8), s // 8 - 1), 0)),
                  _row(ts, wide), _res(conv_w.shape), _res(w_up.shape), _row(ts, D_MODEL), _res((1, D_MODEL)),
                  _row(ts, D_MODEL)],
        out_specs=[_row(ts, D_MODEL), _row(ts, wide), _acc((4, 8, wide)), _acc((8, D_MODEL))],
        out_shape=[_sds((s, D_MODEL)), _sds((s, wide), _MM), _sds((4, 8, wide)), _sds((8, D_MODEL))])(
            dc, dc, u, conv_w, w_up, x1, g2, dy)


def _k_merge_bwd(dx1, og, lg, o_a, o_b, o_m, gates, w_oa, w_ob, w_om, w_out, dep):
    s = dx1.shape[0]
    ts = min(256, s)
    nc = w_oa.shape[2]

    def body(dx_ref, o0, o1, o2, l0, l1, l2, oa_ref, ob_ref, om_ref, gt_ref, woa, wob, wom, wout, dep_ref,
             dgp_ref, dpa_ref, dpb_ref, dpm_ref, dog0, dog1, dog2, dl0, dl1, dl2, dob_ref, dom_ref, bacc_ref):
        i = pl.program_id(0)

        @pl.when(i == 0)
        def _():
            bacc_ref[...] = jnp.zeros_like(bacc_ref)

        dmer = _dot_nt(dx_ref[...], wout[...])
        oa, ob, om = oa_ref[...], ob_ref[...], om_ref[...]
        doa = jnp.zeros((ts, A_W), F32)
        dob = jnp.zeros((ts, B_QH * HEAD), F32)
        dom = jnp.zeros((ts, M_W), F32)
        for j in range(CHIPS):
            prods = _branch_products(oa, ob, om, woa, wob, wom, j)
            dmj = dmer[:, j * nc:(j + 1) * nc]
            dps = []
            for br, (p, dref) in enumerate(zip(prods, (dpa_ref, dpb_ref, dpm_ref))):
                cs = slice(br * D_MODEL + j * nc, br * D_MODEL + (j + 1) * nc)
                gt = gt_ref[:, cs]
                dgp = dmj * p * gt * (1.0 - gt)
                dgp_ref[:, cs] = dgp.astype(_MM)
                bacc_ref[:, cs] += _sum8(dgp)
                dp = (dmj * gt).astype(_MM)
                dref[:, j * nc:(j + 1) * nc] = dp
                dps.append(dp)
            doa = doa + _dot_nt(dps[0], woa[j])
            dob = dob + _dot_nt(dps[1], wob[j])
            dom = dom + _dot_nt(dps[2], wom[j])
        dob_ref[...] = dob
        dom_ref[...] = dom
        ws = _group_weights(l0[...], l1[...], l2[...])
        dsum = _seg_mean(doa * oa, HEAD) * float(HEAD)
        for w, dref, lref in zip(ws, (dog0, dog1, dog2), (dl0, dl1, dl2)):
            dref[...] = w * doa
            lref[...] = w * dsum

    return _pc(
        body, name="merge_out_bwd", grid=(s // ts,),
        in_specs=[_row(ts, D_MODEL)] + [_row(ts, A_W)] * 7 + [_row(ts, B_QH * HEAD), _row(ts, M_W), _row(ts, 3 * D_MODEL),
                                                              _res(w_oa.shape), _res(w_ob.shape), _res(w_om.shape),
                                                              _res(w_out.shape), _res((8, 128))],
        out_specs=[_row(ts, 3 * D_MODEL)] + [_row(ts, D_MODEL)] * 3 + [_row(ts, A_W)] * 6
        + [_row(ts, B_QH * HEAD), _row(ts, M_W), _acc((8, 3 * D_MODEL))],
        out_shape=[_sds((s, 3 * D_MODEL), _MM)] + [_sds((s, D_MODEL), _MM)] * 3 + [_sds((s, A_W))] * 6
        + [_sds((s, B_QH * HEAD)), _sds((s, M_W)), _sds((8, 3 * D_MODEL))])(
            dx1, *og, *lg, o_a, o_b, o_m, gates, w_oa, w_ob, w_om, w_out, dep)


def _k_mem_bwd(m_q, gq, mk, mv, o_m, do_m):
    s = m_q.shape[0]
    n = mk.shape[0]
    ts = min(256, s)
    scale = M_HD ** -0.5

    def body(q_ref, g_ref, mk_ref, mv_ref, o_ref, do_ref, dq_ref, dmk_ref, dmv_ref, gacc_ref):
        i = pl.program_id(0)

        @pl.when(i == 0)
        def _():
            dmk_ref[...] = jnp.zeros_like(dmk_ref)
            dmv_ref[...] = jnp.zeros_like(dmv_ref)
            gacc_ref[...] = jnp.zeros_like(gacc_ref)

        gain = g_ref[...]
        qh, r = _seg_norm(q_ref[...], M_HD)
        qn = (qh * gain).astype(_MM)
        do = do_ref[...]
        delta = _seg_mean(do * o_ref[...], M_HD) * float(M_HD)
        dqn = []
        for h in range(M_HEADS):
            hs = slice(h * M_HD, (h + 1) * M_HD)
            p = _mem_probs(qn[:, hs], mk_ref[:, hs])
            dp = _dot_nt(do[:, hs], mv_ref[:, hs])
            ds = (p * (dp - delta[:, hs][:, 0:1]) * scale).astype(_MM)
            dqn.append(_dot(ds, mk_ref[:, hs]))
            dmk_ref[:, hs] += _dot_tn(ds, qn[:, hs])
            dmv_ref[:, hs] += _dot_tn(p, do[:, hs])
        dqn = jnp.concatenate(dqn, axis=1)
        gacc_ref[...] += _sum8(dqn * qh)
        z = dqn * gain
        dq_ref[...] = (r * (z - qh * _seg_mean(z * qh, M_HD))).astype(_MM)

    return _pc(
        body, name="mem_attn_bwd", grid=(s // ts,),
        in_specs=[_row(ts, M_W), _res((1, M_W)), _res((n, M_W)), _res((n, M_W)), _row(ts, M_W), _row(ts, M_W)],
        out_specs=[_row(ts, M_W), _acc((n, M_W)), _acc((n, M_W)), _acc((8, M_W))],
        out_shape=[_sds((s, M_W), _MM), _sds((n, M_W)), _sds((n, M_W)), _sds((8, M_W))])(m_q, gq, mk, mv, o_m, do_m)


def _k_memkv_bwd(mem, mem_norm, w_kv, m_k_norm, mem_n, kv, dmk, dmv):
    n = mem.shape[0]

    def body(m_ref, g_ref, w_ref, gk_ref, mn_ref, kv_ref, dmk_ref, dmv_ref, dw_ref, dg_ref, dgk_ref):
        gk = gk_ref[...]
        kh, r = _seg_norm(kv_ref[:, :M_W], M_HD)
        dmk = dmk_ref[...]
        dgk_ref[...] = _sum8(dmk * kh)
        z = dmk * gk
        dk = r * (z - kh * _seg_mean(z * kh, M_HD))
        dkv = jnp.concatenate([dk, dmv_ref[...]], axis=1).astype(_MM)
        dw_ref[...] = _dot_tn(mn_ref[...], dkv)
        dmn = _dot_nt(dkv, w_ref[...])
        mh, _ = _rms(m_ref[...])
        dg_ref[...] = _sum8(dmn * mh)

    return _pc(body, name="mem_kv_bwd", grid=(1,),
               in_specs=[_acc((n, D_MODEL)), _acc((1, D_MODEL)), _acc(w_kv.shape), _acc((1, M_W)), _acc((n, D_MODEL)),
                         _acc((n, 2 * M_W)), _acc((n, M_W)), _acc((n, M_W))],
               out_specs=[_acc(w_kv.shape), _acc((8, D_MODEL)), _acc((8, M_W))],
               out_shape=[_sds(w_kv.shape), _sds((8, D_MODEL)), _sds((8, M_W))])(
                   mem, mem_norm, w_kv, m_k_norm, mem_n, kv, dmk, dmv)


def _k_band_bwd(qn, kn, vn, do, lse, dl_or_o, *, hq, hk, max_dist, segs, sink, name):
    rows = qn.shape[0]
    nb = rows // BLK
    units = hq // A_HEADS
    shared = hk != hq
    wq, wk = hq * HEAD, hk * HEAD
    scale = HEAD ** -0.5

    def body(*refs):
        (qb_ref, qx_ref, kb_ref, kp_ref, vb_ref, vp_ref, dob_ref, dox_ref, lb_ref, lx_ref, eb_ref, ex_ref) = refs[:12]
        if sink is None:
            dq_ref, dk_ref, dv_ref = refs[12:]
        else:
            sk_ref, dq_ref, dk_ref, dv_ref, sacc_ref = refs[12:]
        b = pl.program_id(0)
        bias1 = _band_bias(jnp.where(_first_flag(b, segs, nb), 1 << 20, BLK - max_dist), True)
        bias2 = _band_bias(jnp.where(_first_flag(b + 1, segs, nb), 1 << 20, BLK - max_dist), False)
        if sink is not None:
            @pl.when(b == 0)
            def _():
                sacc_ref[...] = jnp.zeros_like(sacc_ref)

        for u in range(units):
            us = slice(u * A_W, (u + 1) * A_W)
            q4, qx4, do4, dox4 = qb_ref[:, us], qx_ref[:, us], dob_ref[:, us], dox_ref[:, us]
            k4, v4 = _unit_kv(kp_ref, kb_ref, u, shared), _unit_kv(vp_ref, vb_ref, u, shared)
            kd, vd = _blockdiag(k4), _blockdiag(v4)
            kdc, vdc = _blockdiag(k4[BLK:]), _blockdiag(v4[BLK:])
            if sink is None:
                dlt_b, dlt_x = eb_ref[:, us], ex_ref[:, us]
            else:
                dlt_b = _seg_sum64(do4.astype(F32) * eb_ref[:, us])
                dlt_x = _seg_sum64(dox4.astype(F32) * ex_ref[:, us])
            s1, dp1 = _dot_nt(q4, kd) * scale, _dot_nt(do4, vd)
            s2, dp2 = _dot_nt(qx4, kdc) * scale, _dot_nt(dox4, vdc)
            ds1, ds1c, p1c, ds2, p2 = [], [], [], [], []
            for h in range(A_HEADS):
                col = slice(u * A_W + h * HEAD, u * A_W + h * HEAD + 1)
                ucol = slice(h * HEAD, h * HEAD + 1)
                wide, narrow = slice(h * 2 * BLK, (h + 1) * 2 * BLK), slice(h * BLK, (h + 1) * BLK)
                l_b, l_x = lb_ref[:, col], lx_ref[:, col]
                p = jnp.exp(s1[:, wide] + bias1 - l_b)
                ds = p * (dp1[:, wide] - dlt_b[:, ucol]) * scale
                ds1.append(ds.astype(_MM))
                ds1c.append(ds[:, BLK:].astype(_MM))
                p1c.append(p[:, BLK:].astype(_MM))
                px = jnp.exp(s2[:, narrow] + bias2 - l_x)
                ds2.append((px * (dp2[:, narrow] - dlt_x[:, ucol]) * scale).astype(_MM))
                p2.append(px.astype(_MM))
                if sink is not None:
                    j = u * A_HEADS + h
                    sacc_ref[:, j:j + 1] += -jnp.exp(sk_ref[j] - l_b) * dlt_b[:, ucol]
            dq_ref[:, us] = _dot(jnp.concatenate(ds1, axis=1), kd)
            dk4 = _fold_diag(_dot_tn(jnp.concatenate(ds1c, axis=1), q4) + _dot_tn(jnp.concatenate(ds2, axis=1), qx4), BLK)
            dv4 = _fold_diag(_dot_tn(jnp.concatenate(p1c, axis=1), do4) + _dot_tn(jnp.concatenate(p2, axis=1), dox4), BLK)
            if shared:
                fold = lambda t: (t[:, 0:HEAD] + t[:, HEAD:2 * HEAD]) + (t[:, 2 * HEAD:3 * HEAD] + t[:, 3 * HEAD:])
                dk_ref[:, u * HEAD:(u + 1) * HEAD] = fold(dk4)
                dv_ref[:, u * HEAD:(u + 1) * HEAD] = fold(dv4).astype(_MM)
            else:
                dk_ref[:, us] = dk4
                dv_ref[:, us] = dv4.astype(_MM)

    cur = lambda w: pl.BlockSpec((BLK, w), lambda i: (i, 0))
    prev = lambda w: pl.BlockSpec((BLK, w), lambda i: (jnp.maximum(i - 1, 0), 0))
    nxt = lambda w: pl.BlockSpec((BLK, w), lambda i: (jnp.minimum(i + 1, nb - 1), 0))
    in_specs = [cur(wq), nxt(wq), cur(wk), prev(wk), cur(wk), prev(wk), cur(wq), nxt(wq), cur(wq), nxt(wq), cur(wq), nxt(wq)]
    args = [qn, qn, kn, kn, vn, vn, do, do, lse, lse, dl_or_o, dl_or_o]
    out_specs = [cur(wq), cur(wk), cur(wk)]
    out_shape = [_sds((rows, wq)), _sds((rows, wk)), _sds((rows, wk), _MM)]
    if sink is not None:
        in_specs.append(pl.BlockSpec(memory_space=pltpu.SMEM))
        args.append(sink)
        out_specs.append(_acc((BLK, 128)))
        out_shape.append(_sds((BLK, 128)))
    return _pc(body, name=name, grid=(nb,), in_specs=in_specs, out_specs=out_specs, out_shape=out_shape)(*args)


def _k_prep_bwd(srcs, dqn, dkn, gq, gk, tabs, tab_row, *, wq, wk, rows_per_gain, name):
    rows = dqn.shape[0]
    ts = min(256, rows)
    ngain = gq.shape[0]

    def body(q_ref, k_ref, dq_ref, dk_ref, gq_ref, gk_ref, c_ref, sa_ref, sb_ref, oq_ref, ok_ref, aq_ref, ak_ref):
        i = pl.program_id(0)

        @pl.when(lax.rem(i * ts, rows_per_gain) == 0)
        def _():
            aq_ref[...] = jnp.zeros_like(aq_ref)
            ak_ref[...] = jnp.zeros_like(ak_ref)

        c, sa, sb = c_ref[...], sa_ref[...], sb_ref[...]
        for x_ref, d_ref, g_ref, o_ref, a_ref in ((q_ref, dq_ref, gq_ref, oq_ref, aq_ref),
                                                   (k_ref, dk_ref, gk_ref, ok_ref, ak_ref)):
            xh, r = _seg_norm(x_ref[...], HEAD)
            dt = _rope_bwd(d_ref[...], c, sa, sb)
            a_ref[...] += _sum8(dt * xh)
            z = dt * g_ref[...]
            o_ref[...] = (r * (z - xh * _seg_mean(z * xh, HEAD))).astype(_MM)

    gspec = lambda w: pl.BlockSpec((None, 1, w), lambda i: ((i * ts) // rows_per_gain, 0, 0))
    aspec = lambda w: pl.BlockSpec((None, 8, w), lambda i: ((i * ts) // rows_per_gain, 0, 0))
    return _pc(
        body, name=name, grid=(rows // ts,),
        in_specs=[_row(ts, wq, srcs[0][1]), _row(ts, wk, srcs[1][1]), _row(ts, wq), _row(ts, wk), gspec(wq), gspec(wk)]
        + [pl.BlockSpec((ts, 128), lambda i: (i + tab_row // ts, 0))] * 3,
        out_specs=[_row(ts, wq), _row(ts, wk), aspec(wq), aspec(wk)],
        out_shape=[_sds((rows, wq), _MM), _sds((rows, wk), _MM), _sds((ngain, 8, wq)), _sds((ngain, 8, wk))])(
            srcs[0][0], srcs[1][0], dqn, dkn, gq, gk, *tabs)


def _k_in_bwd(pieces, dgp, x, g1, dx1, w_in, w_gate):
    s = x.shape[0]
    ts = min(256, s)
    nin, ng = w_in.shape[2], w_gate.shape[2]
    widths = [p.shape[1] for p in pieces]
    ncol = sum(widths)

    def body(*refs):
        p_refs = refs[:len(pieces)]
        dgp_ref, x_ref, g_ref, dx1_ref, wi_ref, wg_ref, gx_ref, dpj_ref, gacc_ref = refs[len(pieces):]
        i = pl.program_id(0)

        @pl.when(i == 0)
        def _():
            gacc_ref[...] = jnp.zeros_like(gacc_ref)

        off = 0
        for p_ref, w in zip(p_refs, widths):
            dpj_ref[:, off:off + w] = p_ref[...]
            off += w
        dh = jnp.zeros((ts, D_MODEL), F32)
        for j in range(CHIPS):
            dh = dh + _dot_nt(dpj_ref[:, j * nin:(j + 1) * nin], wi_ref[j])
            dh = dh + _dot_nt(dgp_ref[:, j * ng:(j + 1) * ng], wg_ref[j])
        xh, r = _rms(x_ref[...])
        gacc_ref[...] += _sum8(dh * xh)
        gx_ref[...] = dx1_ref[...] + _rms_bwd(dh, xh, r, g_ref[...])

    return _pc(
        body, name="in_proj_bwd", grid=(s // ts,),
        in_specs=[_row(ts, w) for w in widths] + [_row(ts, CHIPS * ng), _row(ts, D_MODEL), _res((1, D_MODEL)),
                                                  _row(ts, D_MODEL), _res(w_in.shape), _res(w_gate.shape)],
        out_specs=[_row(ts, D_MODEL), _row(ts, ncol), _acc((8, D_MODEL))],
        out_shape=[_sds((s, D_MODEL)), _sds((s, ncol), _MM), _sds((8, D_MODEL))])(*pieces, dgp, x, g1, dx1, w_in, w_gate)


def _k_wgrad(a, b, *, nblk, stacked, name):
    s, k = a.shape
    n = b.shape[1]
    nb = n // nblk
    ts = min(2048 if k <= 1024 else 1024, s)

    def body(a_ref, b_ref, o_ref):
        @pl.when(pl.program_id(1) == 0)
        def _():
            o_ref[...] = jnp.zeros_like(o_ref)

        o_ref[...] += _dot_tn(a_ref[...], b_ref[...])

    if stacked:
        out_spec, out_shape = pl.BlockSpec((None, k, nb), lambda g, t: (g, 0, 0)), _sds((nblk, k, nb))
    else:
        out_spec, out_shape = pl.BlockSpec((k, nb), lambda g, t: (0, g)), _sds((k, n))
    return _pc(body, name=name, grid=(nblk, s // ts),
               in_specs=[pl.BlockSpec((ts, k), lambda g, t: (t, 0)), pl.BlockSpec((ts, nb), lambda g, t: (t, g))],
               out_specs=[out_spec], out_shape=[out_shape])(a, b)[0]


def _to_res(t, d):
    s, c = t.shape
    return t if d == 1 else t.reshape(s // d, d, c).transpose(1, 0, 2).reshape(s, c)


def _from_res(t, d):
    s, c = t.shape
    return t if d == 1 else t.reshape(d, s // d, c).transpose(1, 0, 2).reshape(s, c)


def _tile_gain(g, heads):
    return jnp.tile(g, (1,) * (g.ndim - 1) + (heads,))[..., None, :]


def _local_step(x, mem, pos, target, small, w_in, get_rest, on_grads):
    s = x.shape[0]
    nblk = s // BLK
    g1, g2 = small["attn_norm"], small["ffn_norm"]

    pos_rows = jnp.concatenate([_to_res(pos[:, None], d)[:, 0] for _, d in A_GROUPS] + [pos])
    tabs = _rope_tables(pos_rows)

    h, qa0, qa1, qa2, q_b, k_b, v_b, m_q = _k_in(x, g1, w_in)

    qkv_a = jnp.concatenate([_to_res(t, d) for t, (_, d) in zip((qa0, qa1, qa2), A_GROUPS)], axis=0)
    gq_a = _tile_gain(small["a_q_norm"], A_HEADS)
    gk_a = _tile_gain(small["a_k_norm"], A_HEADS)
    src_a = ((qkv_a, 0), (qkv_a, 1), (qkv_a, 2))
    qn_a, kn_a, vn_a = _k_prep(src_a, gq_a, gk_a, tabs, 0, wq=A_W, wk=A_W, rows_per_gain=s, name="prep_a")
    segs_a = tuple((gi * nblk, nblk // d) for gi, (_, d) in enumerate(A_GROUPS))
    o_res, l_res = _k_band_fwd(qn_a, kn_a, vn_a, hq=A_HEADS, hk=A_HEADS, max_dist=BLK, segs=segs_a, sink=None,
                               name="attn_a")
    og = [_from_res(o_res[gi * s:(gi + 1) * s], d) for gi, (_, d) in enumerate(A_GROUPS)]
    lg = [_from_res(l_res[gi * s:(gi + 1) * s], d) for gi, (_, d) in enumerate(A_GROUPS)]

    gq_b = _tile_gain(small["b_q_norm"], B_QH)
    gk_b = _tile_gain(small["b_k_norm"], B_KVH)
    src_b = ((q_b, 0), (k_b, 0), (v_b, 0))
    qn_b, kn_b, vn_b = _k_prep(src_b, gq_b, gk_b, tabs, 3 * s, wq=B_QH * HEAD, wk=B_KVH * HEAD, rows_per_gain=s,
                               name="prep_b")
    sink_x = small["b_sinks"][0]
    segs_b = ((0, nblk),)
    o_b, l_b = _k_band_fwd(qn_b, kn_b, vn_b, hq=B_QH, hk=B_KVH, max_dist=B_WINDOW - 1, segs=segs_b, sink=sink_x,
                           name="attn_b")

    wts = get_rest(0, o_b)
    gates = _k_gate(h, wts["w_gate"], small["b_gate"])

    gq_m = _tile_gain(small["m_q_norm"], M_HEADS)[0]
    gk_m = _tile_gain(small["m_k_norm"], M_HEADS)[0]
    mem_n, kv, mk, mv = _k_memkv(mem, small["mem_norm"], wts["w_mem_kv"], gk_m)
    o_m = _k_mem_fwd(m_q, gq_m, mk, mv)

    o_a, merged, x1, h2 = _k_merge(og, lg, o_b, o_m, gates, x, wts["w_o_a"], wts["w_o_b"], wts["w_o_m"],
                                   wts["w_out"], g2)
    wts.update(get_rest(1, x1))
    u = _k_up(h2, wts["w_up"])
    dy, f, dc, loss_acc = _k_ffn(u, wts["conv_w"], small["conv_b"], wts["w_down"], x1, target)
    loss = (0.5 / D_MODEL) * jnp.sum(loss_acc)

    dx1, du, cacc, g2acc = _k_conv_bwd(dc, u, wts["conv_w"], wts["w_up"], x1, g2, dy)
    tok = on_grads({"w_up": _k_wgrad(h2, du, nblk=CHIPS, stacked=True, name="dw_up"),
                    "w_down": _k_wgrad(f, dy, nblk=2, stacked=False, name="dw_down").reshape(CHIPS, -1, D_MODEL)}, dx1)
    (dgp, dp_a, dp_b, dp_m, dog0, dog1, dog2, dl0, dl1, dl2, do_b, do_m, bacc) = _k_merge_bwd(
        dx1, og, lg, o_a, o_b, o_m, gates, wts["w_o_a"], wts["w_o_b"], wts["w_o_m"], wts["w_out"], tok)
    tok = on_grads({"w_gate": _k_wgrad(h, dgp, nblk=CHIPS, stacked=True, name="dw_gate"),
                    "w_o_a": _k_wgrad(o_a, dp_a, nblk=CHIPS, stacked=True, name="dw_o_a"),
                    "w_o_b": _k_wgrad(o_b, dp_b, nblk=CHIPS, stacked=True, name="dw_o_b"),
                    "w_o_m": _k_wgrad(o_m, dp_m, nblk=CHIPS, stacked=True, name="dw_o_m"),
                    "w_out": _k_wgrad(merged, dx1, nblk=1, stacked=False, name="dw_out").reshape(CHIPS, -1, D_MODEL)},
                   do_m)

    dq_m, dmk, dmv, gqm_acc = _k_mem_bwd(m_q, gq_m + tok[0:1, 0:1], mk, mv, o_m, do_m)
    dw_kv, gmem_acc, gkm_acc = _k_memkv_bwd(mem, small["mem_norm"], wts["w_mem_kv"], gk_m, mem_n, kv, dmk, dmv)

    dq_bn, dk_bn, dv_b, sacc = _k_band_bwd(qn_b, kn_b, vn_b, do_b, l_b, o_b, hq=B_QH, hk=B_KVH,
                                           max_dist=B_WINDOW - 1, segs=segs_b, sink=sink_x, name="attn_b_bwd")
    tok = on_grads({}, dq_bn)
    dq_b, dk_b, gqb_acc, gkb_acc = _k_prep_bwd(src_b, dq_bn, dk_bn, gq_b + tok[0:1, 0:1], gk_b, tabs, 3 * s, wq=B_QH * HEAD,
                                               wk=B_KVH * HEAD, rows_per_gain=s, name="prep_b_bwd")

    do_res = jnp.concatenate([_to_res(t, d) for t, (_, d) in zip((dog0, dog1, dog2), A_GROUPS)], axis=0)
    dl_res = jnp.concatenate([_to_res(t, d) for t, (_, d) in zip((dl0, dl1, dl2), A_GROUPS)], axis=0)
    dq_an, dk_an, dv_a = _k_band_bwd(qn_a, kn_a, vn_a, do_res, l_res, dl_res, hq=A_HEADS, hk=A_HEADS, max_dist=BLK,
                                     segs=segs_a, sink=None, name="attn_a_bwd")
    dq_a, dk_a, gqa_acc, gka_acc = _k_prep_bwd(src_a, dq_an, dk_an, gq_a, gk_a, tabs, 0, wq=A_W, wk=A_W,
                                               rows_per_gain=s, name="prep_a_bwd")
    pieces = []
    for gi, (_, d) in enumerate(A_GROUPS):
        rs = slice(gi * s, (gi + 1) * s)
        pieces += [_from_res(t[rs], d) for t in (dq_a, dk_a, dv_a)]
    pieces += [dq_b, dk_b, dv_b, dq_m]
    grad_x, dproj, g1acc = _k_in_bwd(pieces, dgp, x, g1, dx1, w_in, wts["w_gate"])
    on_grads({"w_in": _k_wgrad(h, dproj, nblk=CHIPS, stacked=True, name="dw_in"),
              "w_mem_kv": dw_kv.reshape(CHIPS, -1, 2 * M_W)}, grad_x)

    def fold(acc, heads):
        v = jnp.sum(acc, axis=-2)
        return jnp.sum(v.reshape(v.shape[:-1] + (heads, -1)), axis=-2)

    csum = jnp.sum(cacc, axis=1)
    sml = {
        "attn_norm": jnp.sum(g1acc, axis=0), "a_q_norm": fold(gqa_acc, A_HEADS), "a_k_norm": fold(gka_acc, A_HEADS),
        "b_q_norm": fold(gqb_acc[0], B_QH), "b_k_norm": fold(gkb_acc[0], B_KVH),
        "b_sinks": jnp.sum(sacc, axis=0)[:B_QH], "mem_norm": jnp.sum(gmem_acc, axis=0),
        "m_q_norm": fold(gqm_acc, M_HEADS), "m_k_norm": fold(gkm_acc, M_HEADS),
        "b_gate": jnp.sum(bacc, axis=0), "ffn_norm": jnp.sum(g2acc, axis=0),
        "conv_w": csum[1:], "conv_b": csum[0],
    }
    return loss, grad_x, sml


def _mesh_pos():
    return lax.axis_index("x"), lax.axis_index("y"), lax.axis_index("c")


def _chip_peers(x, y):
    return [(1 - x, y), (x, 1 - y), (1 - x, 1 - y)]


_ANY = pl.BlockSpec(memory_space=pl.ANY)


def _comm_call(body, *, name, n_in, out_shape, scratch):
    return pl.pallas_call(body, name=name, in_specs=[_ANY] * n_in, out_specs=[_ANY] * len(out_shape),
                          out_shape=out_shape, scratch_shapes=scratch)


def _remote(src, dst, send_sem, recv_sem, dev):
    return pltpu.make_async_remote_copy(src_ref=src, dst_ref=dst, send_sem=send_sem, recv_sem=recv_sem,
                                        device_id=dev, device_id_type=MESH)


def _gather_shards(shards):
    nt = len(shards)
    split = [sh.shape[0] % 16 == 0 for sh in shards]

    def body(*refs):
        ins, outs = refs[:nt], refs[nt:2 * nt]
        ici_s, ici_r, fwd_s, fwd_r, own_s, own_r = refs[2 * nt:]
        x, y, c = _mesh_pos()
        me = 2 * x + y
        sib = (x, y, 1 - c)
        peers = _chip_peers(x, y)

        def half(ref, t, who):
            if not split[t]:
                return ref
            hr = shards[t].shape[0] // 2
            return ref.at[pl.ds(pl.multiple_of(who * hr, 8), hr), :]

        pending = []
        for t in range(nt):
            own = _remote(ins[t], outs[t].at[me], own_s.at[t], own_r.at[t], sib)
            own.start()
            pending.append(own.wait)
            for k, (px, py) in enumerate(peers):
                rc = _remote(half(ins[t], t, c), half(outs[t].at[me], t, c), ici_s.at[t, k], ici_r.at[t, k], (px, py, c))
                rc.start()
                pending.append(rc.wait_send)
        for t in range(nt):
            for k, (px, py) in enumerate(peers):
                land = half(outs[t].at[2 * px + py], t, c)
                _remote(land, land, ici_s.at[t, k], ici_r.at[t, k], (px, py, c)).wait_recv()
                if split[t]:
                    fw = _remote(land, land, fwd_s.at[t, k], fwd_r.at[t, k], sib)
                    fw.start()
                    pending.append(fw.wait_send)
                    other = half(outs[t].at[2 * px + py], t, 1 - c)
                    pending.append(_remote(other, other, fwd_s.at[t, k], fwd_r.at[t, k], sib).wait_recv)
        for wait in pending:
            wait()

    out_shape = [_sds((CHIPS,) + sh.shape, sh.dtype) for sh in shards]
    dma = pltpu.SemaphoreType.DMA
    scratch = [dma((nt, 3)), dma((nt, 3)), dma((nt, 3)), dma((nt, 3)), dma((nt,)), dma((nt,))]
    return _comm_call(body, name="gather_weights", n_in=nt, out_shape=out_shape, scratch=scratch)(*shards)


def _pair_join(halves, name):
    nt = len(halves)

    def body(*refs):
        ins, got = refs[:nt], refs[nt:2 * nt]
        send_sems, recv_sems = refs[2 * nt:]
        x, y, c = _mesh_pos()
        cps = []
        for t in range(nt):
            rc = _remote(ins[t], got[t], send_sems.at[t], recv_sems.at[t], (x, y, 1 - c))
            rc.start()
            cps.append(rc)
        for rc in cps:
            rc.wait()

    out_shape = [_sds(hf.shape, hf.dtype) for hf in halves]
    scratch = [pltpu.SemaphoreType.DMA((nt,)), pltpu.SemaphoreType.DMA((nt,))]
    return _comm_call(body, name=name, n_in=nt, out_shape=out_shape, scratch=scratch)(*halves)


_HBM = pl.BlockSpec(memory_space=pltpu.HBM)
_SEMS = pl.BlockSpec(memory_space=pltpu.SEMAPHORE)
_EFFECT = pltpu.SideEffectType.DATAFLOW_SIDE_EFFECTING


def _bcast_copies(ins, lands, send_sems, recv_sems):
    x, y, c = _mesh_pos()
    me = 2 * x + y
    targets = [((px, py, c), 2 * px + py) for px, py in _chip_peers(x, y)] + [((x, y, 1 - c), me)]
    out = []
    for t in range(len(ins)):
        for k, (dev, idx) in enumerate(targets):
            i = t * len(targets) + k
            arrival = lambda t=t, i=i, idx=idx, dev=dev: _remote(ins[t], lands[t].at[idx], send_sems.at[i],
                                                                 recv_sems.at[i], dev)
            out.append((_remote(ins[t], lands[t].at[me], send_sems.at[i], recv_sems.at[i], dev), arrival))
    return out


def _scatter_copies(ins, lands, send_sems, recv_sems):
    x, y, c = _mesh_pos()
    out = []
    for t in range(len(ins)):
        for k, (px, py) in enumerate(_chip_peers(x, y)):
            i = t * 3 + k
            cp = _remote(ins[t].at[2 * px + py], lands[t].at[k], send_sems.at[i], recv_sems.at[i], (px, py, c))
            out.append((cp, lambda cp=cp: cp))
    return out


def _pair_copies(ins, lands, send_sems, recv_sems):
    x, y, c = _mesh_pos()
    out = []
    for t in range(len(ins)):
        hr = ins[t].shape[1] // 2
        give = ins[t].at[:, pl.ds(pl.multiple_of((1 - c) * hr, 8), hr), :]
        cp = _remote(give, lands[t], send_sems.at[t], recv_sems.at[t], (x, y, 1 - c))
        out.append((cp, lambda cp=cp: cp))
    return out


def _split_start(copies, srcs, land_shapes, ncopy, dep, name):
    nt = len(srcs)

    def body(*refs):
        ins, lands = refs[:nt], refs[nt:2 * nt]
        send_sems, recv_sems, token = refs[2 * nt + 1], refs[2 * nt + 2], refs[-1]
        for send, _ in copies(ins, lands, send_sems, recv_sems):
            send.start()
        token[...] = jnp.zeros_like(token)

    lands = [pltpu.with_memory_space_constraint(lax.empty(sh, a.dtype), pltpu.HBM) for sh, a in zip(land_shapes, srcs)]
    srcs = [pltpu.with_memory_space_constraint(a, pltpu.HBM) for a in srcs]
    dma = pltpu.SemaphoreType.DMA
    out_shape = ([dma((nt * ncopy,)), dma((nt * ncopy,))] + [pltpu.HBM(a.shape, a.dtype) for a in srcs + lands]
                 + [_sds((8, 128))])
    outs = pl.pallas_call(
        body, name=name, in_specs=[_HBM] * (2 * nt) + [_ANY],
        out_specs=[_SEMS, _SEMS] + [_HBM] * (2 * nt) + [pl.BlockSpec(memory_space=pltpu.VMEM)], out_shape=out_shape,
        input_output_aliases={i: 2 + i for i in range(2 * nt)},
        compiler_params=pltpu.CompilerParams(has_side_effects=_EFFECT))(*srcs, *lands, dep)
    return outs[0], outs[1], outs[2:2 + nt], outs[2 + nt:2 + 2 * nt], outs[-1]


def _split_wait(copies, send_sems, recv_sems, srcs, lands, after, name):
    nt = len(srcs)

    def body(*refs):
        ins, lnd = refs[:nt], refs[nt:2 * nt]
        for send, arrival in copies(ins, lnd, refs[2 * nt], refs[2 * nt + 1]):
            send.wait_send()
            arrival().wait_recv()

    outs = pl.pallas_call(
        body, name=name, in_specs=[_HBM] * (2 * nt) + [_SEMS, _SEMS, _ANY], out_specs=[_HBM] * (2 * nt),
        out_shape=[pltpu.HBM(a.shape, a.dtype) for a in list(srcs) + list(lands)],
        input_output_aliases={i: i for i in range(2 * nt)},
        compiler_params=pltpu.CompilerParams(has_side_effects=_EFFECT))(*srcs, *lands, send_sems, recv_sems, after)
    return outs[:nt], outs[nt:]


def _gather_small(packed):
    n = packed.shape[0]

    def body(in_ref, out_ref, send_sems, recv_sems, loc_sem):
        x, y, c = _mesh_pos()
        me = 4 * x + 2 * y + c
        lc = pltpu.make_async_copy(in_ref, out_ref.at[me], loc_sem)
        lc.start()
        peers = []
        for k in range(1, NDEV):
            px, py, pc = x ^ (k >> 2), y ^ ((k >> 1) & 1), c ^ (k & 1)
            rc = pltpu.make_async_remote_copy(src_ref=in_ref, dst_ref=out_ref.at[me], send_sem=send_sems.at[k - 1],
                                              recv_sem=recv_sems.at[k - 1], device_id=(px, py, pc), device_id_type=MESH)
            rc.start()
            peers.append((k, px, py, pc))
        lc.wait()
        for k, px, py, pc in peers:
            pltpu.make_async_remote_copy(src_ref=in_ref, dst_ref=out_ref.at[4 * px + 2 * py + pc],
                                         send_sem=send_sems.at[k - 1], recv_sem=recv_sems.at[k - 1],
                                         device_id=(px, py, pc), device_id_type=MESH).wait()

    scratch = [pltpu.SemaphoreType.DMA((NDEV - 1,)), pltpu.SemaphoreType.DMA((NDEV - 1,)), pltpu.SemaphoreType.DMA]
    return _comm_call(body, name="gather_small_grads", n_in=1, out_shape=[_sds((NDEV, n, 128))],
                      scratch=scratch)(packed)[0]


def _row_tile(r, c):
    t = r
    while t * c * 4 > (1 << 20) and t % 16 == 0:
        t //= 2
    return t


def _k_pair_add(full, got, name):
    g, r, c = full.shape
    hr = r // 2
    tr = _row_tile(hr, c)
    nh = hr // tr

    def body(a_ref, b_ref, o_ref):
        o_ref[...] = (a_ref[...] + b_ref[...]).astype(_WIRE)

    mine = pl.BlockSpec((None, tr, c), lambda i, j: (i, lax.axis_index("c") * nh + j, 0))
    spec = pl.BlockSpec((None, tr, c), lambda i, j: (i, j, 0))
    return _pc(body, name=name, grid=(g, nh), in_specs=[mine, spec], out_specs=[spec],
               out_shape=[_sds((g, hr, c), _WIRE)])(full, got)[0]


def _k_chip_sum(parts, slots, name):
    _, r, c = parts.shape
    tr = _row_tile(r, c)

    def body(a_ref, s_ref, o_ref):
        acc = a_ref[...].astype(F32)
        for k in range(3):
            acc = acc + s_ref[k].astype(F32)
        o_ref[...] = acc

    own = pl.BlockSpec((None, tr, c), lambda i: (2 * lax.axis_index("x") + lax.axis_index("y"), i, 0))
    return _pc(body, name=name, grid=(r // tr,), in_specs=[own, pl.BlockSpec((3, tr, c), lambda i: (0, i, 0))],
               out_specs=[_row(tr, c)], out_shape=[_sds((r, c))])(parts, slots)[0]


def _adam(w, g, m, v):
    m = ADAM_B1 * m + (1.0 - ADAM_B1) * g
    v = ADAM_B2 * v + (1.0 - ADAM_B2) * (g * g)
    m_hat = m / (1.0 - ADAM_B1 ** ADAM_STEP)
    v_hat = v / (1.0 - ADAM_B2 ** ADAM_STEP)
    return -ADAM_LR * (m_hat / (jnp.sqrt(v_hat) + ADAM_EPS) + ADAM_WD * w), m, v


def _k_adam(w, mine, theirs, m, v, dep, name):
    r, c = w.shape
    hr = r // 2
    tr = _row_tile(hr, c)
    nh = hr // tr

    def body(w_ref, a_ref, b_ref, m_ref, v_ref, dep_ref, g_ref, d_ref, mo_ref, vo_ref):
        upper = (pl.program_id(0) >= nh).astype(jnp.int32)
        g = jnp.where(upper == lax.axis_index("c"), a_ref[...], b_ref[...])
        g_ref[...] = g
        d_ref[...], mo_ref[...], vo_ref[...] = _adam(w_ref[...], g, m_ref[...], v_ref[...])

    hspec = pl.BlockSpec((tr, c), lambda i: (jnp.where(i >= nh, i - nh, i), 0))
    return _pc(body, name=name, grid=(r // tr,),
               in_specs=[_row(tr, c), hspec, hspec, _row(tr, c), _row(tr, c), _res((8, 128))],
               out_specs=[_row(tr, c)] * 4, out_shape=[_sds((r, c))] * 4)(w, mine, theirs, m, v, dep)


def _k_sum8(a):
    _, n, _ = a.shape

    def body(a_ref, o_ref):
        acc = a_ref[0]
        for k in range(1, NDEV):
            acc = acc + a_ref[k]
        o_ref[...] = acc

    return _pc(body, name="sum_small_grads", grid=(1,), in_specs=[_acc(a.shape)], out_specs=[_acc((n, 128))],
               out_shape=[_sds((n, 128))])(a)[0]


def _k_adam_small(ws, gs, ms, vs):
    n = len(ws)

    def body(*refs):
        for k in range(n):
            w_ref, g_ref, m_ref, v_ref, d_ref, mo_ref, vo_ref = refs[k::n]
            d_ref[...], mo_ref[...], vo_ref[...] = _adam(w_ref[...], g_ref[...], m_ref[...], v_ref[...])

    specs = [_acc(a.shape) for a in ws]
    outs = _pc(body, name="adam_small", grid=(1,), in_specs=specs * 4, out_specs=specs * 3,
               out_shape=[_sds(a.shape) for a in ws] * 3)(*ws, *gs, *ms, *vs)
    return outs[:n], outs[n:2 * n], outs[2 * n:]


def _pack(vals):
    rows = []
    for a in vals:
        flat = a.reshape(-1)
        n = -(-flat.shape[0] // 1024) * 1024
        rows.append(jnp.pad(flat, (0, n - flat.shape[0])).reshape(n // 128, 128))
    return jnp.concatenate(rows, axis=0)


def _unpack(packed, shapes):
    out, off = [], 0
    for sh in shapes:
        size = int(np.prod(sh))
        n = -(-size // 1024) * 1024
        out.append(packed[off // 128:(off + n) // 128].reshape(-1)[:size].reshape(sh))
        off += n
    return out


_WEIGHTS = ["attn_norm", "w_in", "a_q_norm", "a_k_norm", "b_q_norm", "b_k_norm", "b_sinks", "mem_norm", "w_mem_kv",
            "m_q_norm", "m_k_norm", "w_o_a", "w_o_b", "w_o_m", "w_gate", "b_gate", "w_out", "ffn_norm", "w_up",
            "conv_w", "conv_b", "w_down"]
_BIG = ["w_in", "w_mem_kv", "w_o_a", "w_o_b", "w_o_m", "w_gate", "w_out", "w_up", "w_down"]
_SMALL = [n for n in _WEIGHTS if n not in _BIG]


def kernel(x, mem, positions, attn_norm, w_in, a_q_norm, a_k_norm, b_q_norm, b_k_norm, b_sinks, mem_norm, w_mem_kv, m_q_norm, m_k_norm, w_o_a, w_o_b, w_o_m, w_gate, b_gate, w_out, ffn_norm, w_up, conv_w, conv_b, w_down, loss_target, m_attn_norm, m_w_in, m_a_q_norm, m_a_k_norm, m_b_q_norm, m_b_k_norm, m_b_sinks, m_mem_norm, m_w_mem_kv, m_m_q_norm, m_m_k_norm, m_w_o_a, m_w_o_b, m_w_o_m, m_w_gate, m_b_gate, m_w_out, m_ffn_norm, m_w_up, m_conv_w, m_conv_b, m_w_down, v_attn_norm, v_w_in, v_a_q_norm, v_a_k_norm, v_b_q_norm, v_b_k_norm, v_b_sinks, v_mem_norm, v_w_mem_kv, v_m_q_norm, v_m_k_norm, v_w_o_a, v_w_o_b, v_w_o_m, v_w_gate, v_b_gate, v_w_out, v_ffn_norm, v_w_up, v_conv_w, v_conv_b, v_w_down):
    given = dict(locals())
    w = {n: given[n][0] for n in _WEIGHTS}
    m1 = {n: given["m_" + n][0] for n in _WEIGHTS}
    m2 = {n: given["v_" + n][0] for n in _WEIGHTS}

    w_in = _gather_shards([w["w_in"].astype(_MM)])[0]
    stages = (["w_gate", "w_mem_kv", "w_o_a", "w_o_b", "w_o_m", "w_out"], ["w_up", "w_down", "conv_w"])
    tok, started = w_in, []
    for k, names in enumerate(stages):
        shards = [w[n] if n == "conv_w" else w[n].astype(_MM) for n in names]
        *handles, tok = _split_start(_bcast_copies, shards, [(CHIPS,) + a.shape for a in shards], 4, tok,
                                     "gather_start_%d" % k)
        started.append(handles)
    small = {n: (w[n][None, :] if w[n].ndim == 1 else w[n]) for n in _SMALL if n != "conv_w"}
    small["attn_norm"] = small["attn_norm"] + tok[0:1, 0:1]

    def get_rest(stage, after):
        send, recv, srcs, lands = started[stage]
        got = _split_wait(_bcast_copies, send, recv, srcs, lands, after, "gather_wait_%d" % stage)[1]
        wts = dict(zip(stages[stage], got))
        for n in ("w_mem_kv", "w_out", "w_down"):
            if n in wts:
                wts[n] = wts[n].reshape(-1, wts[n].shape[-1])
        return wts

    parts, slots, pair, scat = {}, {}, [], []
    zeros = jnp.zeros((8, 128), F32)

    def finish_pair(after):
        names, tag, send, recv, srcs, lands = pair.pop()
        full, got = _split_wait(_pair_copies, send, recv, srcs, lands, after, "pair_wait_" + tag)
        mine = [_k_pair_add(f, b, "pair_add_" + n) for n, f, b in zip(names, full, got)]
        shapes = [(3,) + p.shape[1:] for p in mine]
        send, recv, srcs, lands, token = _split_start(_scatter_copies, mine, shapes, 3, zeros, "scatter_start_" + tag)
        scat.append((names, tag, send, recv, srcs, lands))
        return token

    def on_grads(group, after):
        names = list(group)
        tag = "_".join(names)
        token = finish_pair(after) if pair else zeros
        if not group:
            return token
        grads_g = [group[n] for n in names]
        shapes = [(CHIPS, g.shape[1] // 2, g.shape[2]) for g in grads_g]
        send, recv, srcs, lands, token = _split_start(_pair_copies, grads_g, shapes, 1, token, "pair_start_" + tag)
        pair.append((names, tag, send, recv, srcs, lands))
        return token

    loss, grad_x, sml = _local_step(x[0], mem[0], positions[0], loss_target[0], small, w_in, get_rest, on_grads)
    loss = lax.psum(loss, ("x", "y", "c"))
    early = [n for names, *_ in scat for n in names]
    for names, tag, send, recv, srcs, lands in scat:
        mine, got = _split_wait(_scatter_copies, send, recv, srcs, lands, grad_x, "scatter_wait_" + tag)
        parts.update(zip(names, mine))
        slots.update(zip(names, got))
    scat.clear()
    reduced = {n: _k_chip_sum(parts[n], slots[n], "chip_add_" + n) for n in early}
    tok = finish_pair(reduced[early[-1]])
    theirs = dict(zip(early, _pair_join([reduced[n] for n in early], "grad_pair_join_early")))
    grads = {}

    shapes = [sml[n].shape for n in _SMALL]
    gsm = dict(zip(_SMALL, _unpack(_k_sum8(_gather_small(_pack([sml[n] for n in _SMALL]))), shapes)))
    nu = w["conv_w"].shape[1]
    chip = 2 * lax.axis_index("x") + lax.axis_index("y")
    gsm["conv_w"] = lax.dynamic_slice_in_dim(gsm["conv_w"], chip * nu, nu, axis=1)
    for n in _SMALL:
        grads[n] = gsm[n].reshape(w[n].shape)

    delta, new_m, new_v = {}, {}, {}
    for n in early:
        grads[n], delta[n], new_m[n], new_v[n] = _k_adam(w[n], reduced[n], theirs[n], m1[n], m2[n], tok, "adam_" + n)
    as2d = lambda d: [d[n][None, :] if d[n].ndim == 1 else d[n] for n in _SMALL]
    for dst, outs in zip((delta, new_m, new_v), _k_adam_small(as2d(w), as2d(grads), as2d(m1), as2d(m2))):
        dst.update((n, a.reshape(w[n].shape)) for n, a in zip(_SMALL, outs))
    late, tag, send, recv, srcs, lands = scat.pop()
    mine, got = _split_wait(_scatter_copies, send, recv, srcs, lands, delta[early[-1]], "scatter_wait_" + tag)
    for n, a, b in zip(late, mine, got):
        reduced[n] = _k_chip_sum(a, b, "chip_add_" + n)
    theirs.update(zip(late, _pair_join([reduced[n] for n in late], "grad_pair_join_late")))
    for n in late:
        grads[n], delta[n], new_m[n], new_v[n] = _k_adam(w[n], reduced[n], theirs[n], m1[n], m2[n], zeros, "adam_" + n)

    lead = lambda d: [d[n][None] for n in _WEIGHTS]
    return (loss, grad_x[None], *lead(grads), *lead(delta), *lead(new_m), *lead(new_v))
```

```python
import math

import jax
import jax.numpy as jnp
import numpy as np
from jax import lax
from jax.experimental import pallas as pl
from jax.experimental.pallas import tpu as pltpu

F32 = jnp.float32
_MM = jnp.bfloat16
_WIRE = jnp.bfloat16

D_MODEL = 1024
HEAD = 64
BLK = 128
A_GROUPS = ((128, 1), (512, 4), (2048, 16))
A_HEADS = 4
A_W = A_HEADS * HEAD
B_QH = 8
B_KVH = 2
B_WINDOW = 128
M_HEADS = 4
M_HD = 128
M_W = M_HEADS * M_HD
D_FF = 2816
EPS = 1e-6
NEG = -1e30
ROPE_THETA = 500000.0
ROPE_ROT = 16
CHIPS = 4
NDEV = 8
ADAM_LR, ADAM_B1, ADAM_B2, ADAM_EPS, ADAM_WD, ADAM_STEP = 0.001, 0.9, 0.999, 1e-08, 0.01, 10
VMEM_LIMIT = 58 * 1024 * 1024
MESH = pl.DeviceIdType.MESH


def _pc(body, *, name, grid, in_specs, out_specs, out_shape, scratch=()):
    return pl.pallas_call(
        body, name=name, grid=grid, in_specs=in_specs, out_specs=out_specs, out_shape=out_shape,
        scratch_shapes=list(scratch),
        compiler_params=pltpu.CompilerParams(dimension_semantics=("arbitrary",) * len(grid),
                                             vmem_limit_bytes=VMEM_LIMIT))


def _row(ts, c, col=0):
    return pl.BlockSpec((ts, c), lambda i: (i, col))


def _res(shape):
    n = len(shape)
    return pl.BlockSpec(tuple(shape), lambda i: (0,) * n, pipeline_mode=pl.Buffered(1))


def _acc(shape):
    n = len(shape)
    return pl.BlockSpec(tuple(shape), lambda i: (0,) * n)


def _sds(shape, dtype=F32):
    return jax.ShapeDtypeStruct(tuple(shape), dtype)


def _dot(a, b):
    return jnp.dot(a.astype(_MM), b.astype(_MM), preferred_element_type=F32)


def _dot_nt(a, b):
    return lax.dot_general(a.astype(_MM), b.astype(_MM), (((1,), (1,)), ((), ())), preferred_element_type=F32)


def _dot_tn(a, b):
    return lax.dot_general(a.astype(_MM), b.astype(_MM), (((0,), (0,)), ((), ())), preferred_element_type=F32)


def _sum8(v):
    ts, c = v.shape
    return jnp.sum(v.reshape(ts // 8, 8, c), axis=0)


def _sigmoid(z):
    return 1.0 / (1.0 + jnp.exp(-z))


def _rms(x):
    r = lax.rsqrt(jnp.mean(x * x, axis=-1, keepdims=True) + EPS)
    return x * r, r


def _rms_bwd(dy, xh, r, gain):
    z = dy * gain
    return r * (z - xh * jnp.mean(z * xh, axis=-1, keepdims=True))


def _split_hi_lo(v):
    hi = v.astype(_MM)
    return hi, (v - hi.astype(F32)).astype(_MM)


def _lane_head(shape):
    return lax.shift_right_logical(lax.broadcasted_iota(jnp.int32, shape, len(shape) - 1), 6)


def _seg_sum64(v):
    w = v.shape[1]
    e = jnp.where(_lane_head((w, w)) == lax.shift_right_logical(lax.broadcasted_iota(jnp.int32, (w, w), 0), 6),
                  1.0, 0.0).astype(_MM)
    hi, lo = _split_hi_lo(v)
    return jnp.dot(hi, e, preferred_element_type=F32) + jnp.dot(lo, e, preferred_element_type=F32)


def _seg_norm(x, seg):
    if seg == HEAD:
        r = lax.rsqrt(_seg_sum64(x * x) * (1.0 / HEAD) + EPS)
        return x * r, r
    w = x.shape[1]
    xh, rr = [], []
    for s in range(w // seg):
        xs = x[:, s * seg:(s + 1) * seg]
        r = lax.rsqrt(jnp.mean(xs * xs, axis=-1, keepdims=True) + EPS)
        xh.append(xs * r)
        rr.append(jnp.broadcast_to(r, xs.shape))
    return jnp.concatenate(xh, axis=1), jnp.concatenate(rr, axis=1)


def _seg_mean(v, seg):
    if seg == HEAD:
        return _seg_sum64(v) * (1.0 / HEAD)
    w = v.shape[1]
    out = []
    for s in range(w // seg):
        vs = v[:, s * seg:(s + 1) * seg]
        out.append(jnp.broadcast_to(jnp.mean(vs, axis=-1, keepdims=True), vs.shape))
    return jnp.concatenate(out, axis=1)


def _rope(t, c, sa, sb):
    out = []
    for cb in range(t.shape[1] // 128):
        tc = t[:, cb * 128:(cb + 1) * 128]
        out.append(tc * c + pltpu.roll(tc, 120, 1) * sa + pltpu.roll(tc, 8, 1) * sb)
    return jnp.concatenate(out, axis=1) if len(out) > 1 else out[0]


def _rope_bwd(dy, c, sa, sb):
    out = []
    for cb in range(dy.shape[1] // 128):
        dc = dy[:, cb * 128:(cb + 1) * 128]
        out.append(dc * c + pltpu.roll(dc * sa, 8, 1) + pltpu.roll(dc * sb, 120, 1))
    return jnp.concatenate(out, axis=1) if len(out) > 1 else out[0]


def _rope_consts():
    half = ROPE_ROT // 2
    c = np.float32(-2.0 * math.log(ROPE_THETA) / ROPE_ROT)
    freqs = np.exp(np.arange(half, dtype=np.float32) * c).astype(np.float32)
    place = np.zeros((3, half, 128), np.float32)
    ones = np.zeros((1, 128), np.float32)
    for lane in range(128):
        d = lane % HEAD
        if d < half:
            place[0, d, lane], place[1, d, lane] = 1.0, -1.0
        elif d < ROPE_ROT:
            place[0, d - half, lane], place[2, d - half, lane] = 1.0, 1.0
        else:
            ones[0, lane] = 1.0
    return np.tile(freqs[:, None], (1, 128)), place, ones


def _rope_tables(pos_rows):
    r = pos_rows.shape[0]
    tr = min(1024, r)
    freqs, place, ones = _rope_consts()

    def split3(v):
        hi, mid = _split_hi_lo(v)
        lo = (v - hi.astype(F32) - mid.astype(F32)).astype(_MM)
        return hi, mid, lo

    def body(p_ref, f_ref, e_ref, one_ref, c_ref, sa_ref, sb_ref):
        for j in range(tr // 128):
            ang = p_ref[j:j + 1, :].astype(F32) * f_ref[...]
            rows = slice(j * 128, (j + 1) * 128)
            for ref, k, v in ((c_ref, 0, jnp.cos(ang)), (sa_ref, 1, jnp.sin(ang)), (sb_ref, 2, jnp.sin(ang))):
                e = e_ref[k].astype(_MM)
                out = sum(_dot_tn(part, e) for part in split3(v))
                ref[rows, :] = out + one_ref[...] if k == 0 else out

    return _pc(body, name="rope_tables", grid=(r // tr,),
               in_specs=[pl.BlockSpec((tr // 128, 128), lambda i: (i, 0)), _acc((ROPE_ROT // 2, 128)),
                         _acc((3, ROPE_ROT // 2, 128)), _acc((1, 128))],
               out_specs=[_row(tr, 128)] * 3, out_shape=[_sds((r, 128))] * 3)(
                   pos_rows.reshape(r // 128, 128), jnp.asarray(freqs), jnp.asarray(place), jnp.asarray(ones))


def _k_in(x, g1, w_in):
    s = x.shape[0]
    ts = min(256, s)
    nin = w_in.shape[2]
    ncol = CHIPS * nin
    a_cols = 3 * A_W
    offs = [0, a_cols, 2 * a_cols, 3 * a_cols, 3 * a_cols + B_QH * HEAD,
            3 * a_cols + (B_QH + B_KVH) * HEAD, 3 * a_cols + (B_QH + 2 * B_KVH) * HEAD, ncol]

    def body(x_ref, g_ref, wi_ref, h_ref, a0, a1, a2, qb, kb, vb, mq, p_scr):
        xh, _ = _rms(x_ref[...])
        h = (xh * g_ref[...]).astype(_MM)
        h_ref[...] = h
        for j in range(CHIPS):
            p_scr[:, j * nin:(j + 1) * nin] = jnp.dot(h, wi_ref[j], preferred_element_type=F32)
        for k, ref in enumerate((a0, a1, a2, qb, kb, vb, mq)):
            ref[...] = p_scr[:, offs[k]:offs[k + 1]]

    widths = [offs[k + 1] - offs[k] for k in range(7)]
    return _pc(
        body, name="in_proj", grid=(s // ts,),
        in_specs=[_row(ts, D_MODEL), _res((1, D_MODEL)), _res(w_in.shape)],
        out_specs=[_row(ts, D_MODEL)] + [_row(ts, w) for w in widths],
        out_shape=[_sds((s, D_MODEL), _MM)] + [_sds((s, w)) for w in widths],
        scratch=[pltpu.VMEM((ts, ncol), F32)])(x, g1, w_in)


def _k_gate(h, w_gate, b_gate):
    s = h.shape[0]
    ts = min(256, s)
    ng = w_gate.shape[2]

    def body(h_ref, wg_ref, bg_ref, gt_ref):
        h = h_ref[...]
        for j in range(CHIPS):
            z = jnp.dot(h, wg_ref[j], preferred_element_type=F32) + bg_ref[:, j * ng:(j + 1) * ng]
            gt_ref[:, j * ng:(j + 1) * ng] = _sigmoid(z)

    return _pc(body, name="gate_proj", grid=(s // ts,),
               in_specs=[_row(ts, D_MODEL), _res(w_gate.shape), _res(b_gate.shape)],
               out_specs=[_row(ts, CHIPS * ng)], out_shape=[_sds((s, CHIPS * ng))])(h, w_gate, b_gate)[0]


def _k_prep(srcs, gq, gk, tabs, tab_row, *, wq, wk, rows_per_gain, name):
    rows = srcs[0][0].shape[0]
    ts = min(256, rows)

    def body(q_ref, k_ref, v_ref, gq_ref, gk_ref, c_ref, sa_ref, sb_ref, qn_ref, kn_ref, vn_ref):
        c, sa, sb = c_ref[...], sa_ref[...], sb_ref[...]
        qh, _ = _seg_norm(q_ref[...], HEAD)
        qn_ref[...] = _rope(qh * gq_ref[...], c, sa, sb).astype(_MM)
        kh, _ = _seg_norm(k_ref[...], HEAD)
        kn_ref[...] = _rope(kh * gk_ref[...], c, sa, sb).astype(_MM)
        vn_ref[...] = v_ref[...].astype(_MM)

    gspec = lambda w: pl.BlockSpec((None, 1, w), lambda i: ((i * ts) // rows_per_gain, 0, 0))
    return _pc(
        body, name=name, grid=(rows // ts,),
        in_specs=[_row(ts, wq, srcs[0][1]), _row(ts, wk, srcs[1][1]), _row(ts, wk, srcs[2][1]),
                  gspec(wq), gspec(wk)] + [pl.BlockSpec((ts, 128), lambda i: (i + tab_row // ts, 0))] * 3,
        out_specs=[_row(ts, wq), _row(ts, wk), _row(ts, wk)],
        out_shape=[_sds((rows, wq), _MM), _sds((rows, wk), _MM), _sds((rows, wk), _MM)])(
            srcs[0][0], srcs[1][0], srcs[2][0], gq, gk, *tabs)


def _first_flag(b, segs, nb):
    first = b >= nb
    for k, (start, period) in enumerate(segs):
        end = segs[k + 1][0] if k + 1 < len(segs) else nb
        first = first | ((b >= start) & (b < end) & (lax.rem(b - start, jnp.int32(period)) == 0))
    return first


def _band_bias(thr, with_cur):
    qi = lax.broadcasted_iota(jnp.int32, (BLK, BLK), 0)
    kj = lax.broadcasted_iota(jnp.int32, (BLK, BLK), 1)
    prev = jnp.where(kj >= qi + thr, 0.0, NEG)
    return jnp.concatenate([prev, jnp.where(kj <= qi, 0.0, NEG)], axis=1) if with_cur else prev


def _blockdiag(t4):
    head = _lane_head((1, A_W))
    return jnp.concatenate([t4 * jnp.where(head == h, 1.0, 0.0).astype(t4.dtype) for h in range(A_HEADS)], axis=0)


def _fold_diag(t, n):
    head = _lane_head((n, A_W))
    out = t[3 * n:4 * n]
    for h in (2, 1, 0):
        out = jnp.where(head == h, t[h * n:(h + 1) * n], out)
    return out


def _expand_heads(cols):
    n = cols[0].shape[0]
    head = _lane_head((n, A_W))
    out = jnp.broadcast_to(cols[3], (n, A_W))
    for h in (2, 1, 0):
        out = jnp.where(head == h, cols[h], out)
    return out


def _unit_kv(pieces, u, shared):
    cols = slice(u * HEAD, (u + 1) * HEAD) if shared else slice(u * A_W, (u + 1) * A_W)
    rows = [ref[rs, cols] for ref, rs in pieces]
    k = rows[0] if len(rows) == 1 else jnp.concatenate(rows, axis=0)
    return jnp.concatenate([k] * A_HEADS, axis=1) if shared else k


_LO, _HI, _BOTH = slice(0, BLK), slice(BLK, 2 * BLK), slice(0, 2 * BLK)


def _k_band_fwd(qn, kn, vn, *, hq, hk, max_dist, segs, sink, name):
    rows = qn.shape[0]
    nb = rows // BLK
    units = hq // A_HEADS
    shared = hk != hq
    wq, wk = hq * HEAD, hk * HEAD
    scale = HEAD ** -0.5

    def body(*refs):
        if sink is None:
            q_ref, kc_ref, kp_ref, vc_ref, vp_ref, o_ref, l_ref = refs
        else:
            q_ref, kc_ref, kp_ref, vc_ref, vp_ref, sk_ref, o_ref, l_ref = refs
        i = pl.program_id(0)
        for half, rs in enumerate((_LO, _HI)):
            bias = _band_bias(jnp.where(_first_flag(2 * i + half, segs, nb), 1 << 20, BLK - max_dist), True)
            kpieces = ((kp_ref, _LO), (kc_ref, _LO)) if half == 0 else ((kc_ref, _BOTH),)
            vpieces = ((vp_ref, _LO), (vc_ref, _LO)) if half == 0 else ((vc_ref, _BOTH),)
            for u in range(units):
                us = slice(u * A_W, (u + 1) * A_W)
                kb = _blockdiag(_unit_kv(kpieces, u, shared))
                vb = _blockdiag(_unit_kv(vpieces, u, shared))
                s_all = _dot_nt(q_ref[rs, us], kb) * scale
                ps, ls = [], []
                for h in range(A_HEADS):
                    s = s_all[:, h * 2 * BLK:(h + 1) * 2 * BLK] + bias
                    m = jnp.max(s, axis=-1, keepdims=True)
                    e = jnp.exp(s - m)
                    lse = m + jnp.log(jnp.sum(e, axis=-1, keepdims=True))
                    if sink is not None:
                        sk = sk_ref[u * A_HEADS + h]
                        mx = jnp.maximum(lse, sk)
                        lse = mx + jnp.log(jnp.exp(lse - mx) + jnp.exp(sk - mx))
                    ps.append((e * jnp.exp(m - lse)).astype(_MM))
                    ls.append(lse)
                o_ref[rs, us] = _dot(jnp.concatenate(ps, axis=1), vb)
                l_ref[rs, us] = _expand_heads(ls)

    two = lambda w: pl.BlockSpec((2 * BLK, w), lambda i: (i, 0))
    prev = lambda w: pl.BlockSpec((BLK, w), lambda i: (jnp.maximum(2 * i - 1, 0), 0))
    in_specs = [two(wq), two(wk), prev(wk), two(wk), prev(wk)]
    args = [qn, kn, kn, vn, vn]
    if sink is not None:
        in_specs.append(pl.BlockSpec(memory_space=pltpu.SMEM))
        args.append(sink)
    return _pc(body, name=name, grid=(nb // 2,), in_specs=in_specs, out_specs=[two(wq), two(wq)],
               out_shape=[_sds((rows, wq)), _sds((rows, wq))])(*args)


def _k_memkv(mem, mem_norm, w_kv, m_k_norm):
    n = mem.shape[0]

    def body(m_ref, g_ref, w_ref, gk_ref, mn_ref, kv_ref, mk_ref, mv_ref):
        mh, _ = _rms(m_ref[...])
        mn = (mh * g_ref[...]).astype(_MM)
        mn_ref[...] = mn
        kv = jnp.dot(mn, w_ref[...], preferred_element_type=F32)
        kv_ref[...] = kv
        kh, _ = _seg_norm(kv[:, :M_W], M_HD)
        mk_ref[...] = (kh * gk_ref[...]).astype(_MM)
        mv_ref[...] = kv[:, M_W:].astype(_MM)

    return _pc(body, name="mem_kv", grid=(1,),
               in_specs=[_acc((n, D_MODEL)), _acc((1, D_MODEL)), _acc(w_kv.shape), _acc((1, M_W))],
               out_specs=[_acc((n, D_MODEL)), _acc((n, 2 * M_W)), _acc((n, M_W)), _acc((n, M_W))],
               out_shape=[_sds((n, D_MODEL), _MM), _sds((n, 2 * M_W)), _sds((n, M_W), _MM), _sds((n, M_W), _MM)])(
                   mem, mem_norm, w_kv, m_k_norm)


def _mem_probs(q, mk):
    sc = _dot_nt(q, mk) * (M_HD ** -0.5)
    e = jnp.exp(sc - jnp.max(sc, axis=-1, keepdims=True))
    return e / jnp.sum(e, axis=-1, keepdims=True)


def _k_mem_fwd(m_q, gq, mk, mv):
    s = m_q.shape[0]
    n = mk.shape[0]
    ts = min(256, s)

    def body(q_ref, g_ref, mk_ref, mv_ref, o_ref):
        qh, _ = _seg_norm(q_ref[...], M_HD)
        qn = (qh * g_ref[...]).astype(_MM)
        for h in range(M_HEADS):
            hs = slice(h * M_HD, (h + 1) * M_HD)
            o_ref[:, hs] = _dot(_mem_probs(qn[:, hs], mk_ref[:, hs]), mv_ref[:, hs])

    return _pc(body, name="mem_attn", grid=(s // ts,),
               in_specs=[_row(ts, M_W), _res((1, M_W)), _res((n, M_W)), _res((n, M_W))],
               out_specs=[_row(ts, M_W)], out_shape=[_sds((s, M_W))])(m_q, gq, mk, mv)[0]


def _group_weights(l0, l1, l2):
    m = jnp.maximum(jnp.maximum(l0, l1), l2)
    e0, e1, e2 = jnp.exp(l0 - m), jnp.exp(l1 - m), jnp.exp(l2 - m)
    inv = 1.0 / (e0 + e1 + e2)
    return e0 * inv, e1 * inv, e2 * inv


def _branch_products(oa, ob, om, woa_ref, wob_ref, wom_ref, j):
    return _dot(oa, woa_ref[j]), _dot(ob, wob_ref[j]), _dot(om, wom_ref[j])


def _k_merge(og, lg, o_b, o_m, gates, x, w_oa, w_ob, w_om, w_out, g2):
    s = x.shape[0]
    ts = min(256, s)
    nc = w_oa.shape[2]

    def body(o0, o1, o2, l0, l1, l2, ob_ref, om_ref, gt_ref, x_ref, woa, wob, wom, wout, g_ref,
             oa_ref, mer_ref, x1_ref, h2_ref, m_scr):
        w0, w1, w2 = _group_weights(l0[...], l1[...], l2[...])
        oa = w0 * o0[...] + w1 * o1[...] + w2 * o2[...]
        oa_ref[...] = oa
        ob, om = ob_ref[...], om_ref[...]
        for j in range(CHIPS):
            pa, pb, pm = _branch_products(oa, ob, om, woa, wob, wom, j)
            cs = lambda br: slice(br * D_MODEL + j * nc, br * D_MODEL + (j + 1) * nc)
            m_scr[:, j * nc:(j + 1) * nc] = gt_ref[:, cs(0)] * pa + gt_ref[:, cs(1)] * pb + gt_ref[:, cs(2)] * pm
        mer = m_scr[...].astype(_MM)
        mer_ref[...] = mer
        x1 = x_ref[...] + jnp.dot(mer, wout[...], preferred_element_type=F32)
        x1_ref[...] = x1
        xh, _ = _rms(x1)
        h2_ref[...] = (xh * g_ref[...]).astype(_MM)

    return _pc(
        body, name="merge_out", grid=(s // ts,),
        in_specs=[_row(ts, A_W)] * 6 + [_row(ts, B_QH * HEAD), _row(ts, M_W), _row(ts, 3 * D_MODEL), _row(ts, D_MODEL),
                                         _res(w_oa.shape), _res(w_ob.shape), _res(w_om.shape), _res(w_out.shape),
                                         _res((1, D_MODEL))],
        out_specs=[_row(ts, A_W), _row(ts, D_MODEL), _row(ts, D_MODEL), _row(ts, D_MODEL)],
        out_shape=[_sds((s, A_W)), _sds((s, D_MODEL), _MM), _sds((s, D_MODEL)), _sds((s, D_MODEL), _MM)],
        scratch=[pltpu.VMEM((ts, D_MODEL), F32)])(*og, *lg, o_b, o_m, gates, x, w_oa, w_ob, w_om, w_out, g2)


def _k_up(h2, w_up):
    s = h2.shape[0]
    ts = min(256, s)
    nu = w_up.shape[2]

    def body(h_ref, w_ref, u_ref):
        h = h_ref[...]
        for j in range(CHIPS):
            u_ref[:, j * nu:(j + 1) * nu] = jnp.dot(h, w_ref[j], preferred_element_type=F32)

    return _pc(body, name="up_proj", grid=(s // ts,), in_specs=[_row(ts, D_MODEL), _res(w_up.shape)],
               out_specs=[_row(ts, CHIPS * nu)], out_shape=[_sds((s, CHIPS * nu))])(h2, w_up)[0]


def _shift_down(v, halo, k):
    ts = v.shape[0]
    row = lax.broadcasted_iota(jnp.int32, v.shape, 0)
    out = pltpu.roll(v, k, 0)
    for r in range(k):
        out = jnp.where(row == r, halo[8 - k + r:8 - k + r + 1, :], out)
    return out


def _shift_up(v, halo, k):
    ts = v.shape[0]
    row = lax.broadcasted_iota(jnp.int32, v.shape, 0)
    out = pltpu.roll(v, ts - k, 0)
    for r in range(k):
        out = jnp.where(row == ts - k + r, halo[r:r + 1, :], out)
    return out


def _k_ffn(u, conv_w, conv_b, w_down, x1, target):
    s = u.shape[0]
    ts = min(256, s)
    nu = conv_w.shape[2]
    half = CHIPS // 2

    def body(u_ref, uh_ref, cw_ref, cb_ref, wd_ref, x1_ref, t_ref, dy_ref, f_ref, dc_ref, loss_ref, c_scr, f_scr):
        i = pl.program_id(0)
        halo = jnp.where(i > 0, uh_ref[...], 0.0)
        for j in range(CHIPS):
            cs = slice(j * nu, (j + 1) * nu)
            uj = u_ref[:, cs]
            hj = halo[:, cs]
            c_scr[:, cs] = (cb_ref[:, cs] + cw_ref[j, 0:1, :] * _shift_down(uj, hj, 2)
                            + cw_ref[j, 1:2, :] * _shift_down(uj, hj, 1) + cw_ref[j, 2:3, :] * uj)
        for j in range(half):
            a = c_scr[:, j * nu:(j + 1) * nu]
            g = c_scr[:, (half + j) * nu:(half + j + 1) * nu]
            f_scr[:, j * nu:(j + 1) * nu] = (a * _sigmoid(a) * g).astype(_MM)
        f = f_scr[...]
        f_ref[...] = f
        y = x1_ref[...] + jnp.dot(f, wd_ref[...], preferred_element_type=F32)
        err = y - t_ref[...]
        dy = err * (1.0 / D_MODEL)
        dy_ref[...] = dy

        @pl.when(i == 0)
        def _():
            loss_ref[...] = jnp.zeros_like(loss_ref)

        loss_ref[...] += _sum8(err * err)
        df = _dot_nt(dy, wd_ref[...])
        for j in range(half):
            a = c_scr[:, j * nu:(j + 1) * nu]
            g = c_scr[:, (half + j) * nu:(half + j + 1) * nu]
            sa = _sigmoid(a)
            dfj = df[:, j * nu:(j + 1) * nu]
            dc_ref[:, j * nu:(j + 1) * nu] = dfj * g * (sa * (1.0 + a * (1.0 - sa)))
            dc_ref[:, (half + j) * nu:(half + j + 1) * nu] = dfj * (a * sa)

    wide = CHIPS * nu
    return _pc(
        body, name="conv_ffn", grid=(s // ts,),
        in_specs=[_row(ts, wide), pl.BlockSpec((8, wide), lambda i: (jnp.maximum(i * (ts // 8) - 1, 0), 0)),
                  _res(conv_w.shape), _res((1, wide)), _res(w_down.shape), _row(ts, D_MODEL), _row(ts, D_MODEL)],
        out_specs=[_row(ts, D_MODEL), _row(ts, D_FF), _row(ts, wide), _acc((8, D_MODEL))],
        out_shape=[_sds((s, D_MODEL)), _sds((s, D_FF), _MM), _sds((s, wide)), _sds((8, D_MODEL))],
        scratch=[pltpu.VMEM((ts, wide), F32), pltpu.VMEM((ts, D_FF), _MM)])(u, u, conv_w, conv_b, w_down, x1, target)


def _k_conv_bwd(dc, u, conv_w, w_up, x1, g2, dy):
    s = u.shape[0]
    ts = min(256, s)
    nu = conv_w.shape[2]
    wide = CHIPS * nu
    last = s // ts - 1

    def body(dc_ref, dn_ref, u_ref, cw_ref, wu_ref, x1_ref, g_ref, dy_ref, dx1_ref, du_ref, cacc_ref, gacc_ref):
        i = pl.program_id(0)

        @pl.when(i == 0)
        def _():
            cacc_ref[...] = jnp.zeros_like(cacc_ref)
            gacc_ref[...] = jnp.zeros_like(gacc_ref)

        dhalo = jnp.where(i < last, dn_ref[...], 0.0)
        dh2 = jnp.zeros((ts, D_MODEL), F32)
        for j in range(CHIPS):
            cs = slice(j * nu, (j + 1) * nu)
            dcj, uj = dc_ref[:, cs], u_ref[:, cs]
            dc1, dc2 = _shift_up(dcj, dhalo[:, cs], 1), _shift_up(dcj, dhalo[:, cs], 2)
            cacc_ref[0, :, cs] += _sum8(dcj)
            cacc_ref[1, :, cs] += _sum8(dc2 * uj)
            cacc_ref[2, :, cs] += _sum8(dc1 * uj)
            cacc_ref[3, :, cs] += _sum8(dcj * uj)
            du = (cw_ref[j, 2:3, :] * dcj + cw_ref[j, 1:2, :] * dc1 + cw_ref[j, 0:1, :] * dc2).astype(_MM)
            du_ref[:, cs] = du
            dh2 = dh2 + _dot_nt(du, wu_ref[j])
        xh, r = _rms(x1_ref[...])
        gacc_ref[...] += _sum8(dh2 * xh)
        dx1_ref[...] = dy_ref[...] + _rms_bwd(dh2, xh, r, g_ref[...])

    return _pc(
        body, name="conv_up_bwd", grid=(s // ts,),
        in_specs=[_row(ts, wide),
                  pl.BlockSpec((8, wide), lambda i: (jnp.minimum((i + 1) * (ts // 8), s // 8 - 1), 0)),
                  _row(ts, wide), _res(conv_w.shape), _res(w_up.shape), _row(ts, D_MODEL), _res((1, D_MODEL)),
                  _row(ts, D_MODEL)],
        out_specs=[_row(ts, D_MODEL), _row(ts, wide), _acc((4, 8, wide)), _acc((8, D_MODEL))],
        out_shape=[_sds((s, D_MODEL)), _sds((s, wide), _MM), _sds((4, 8, wide)), _sds((8, D_MODEL))])(
            dc, dc, u, conv_w, w_up, x1, g2, dy)


def _k_merge_bwd(dx1, og, lg, o_a, o_b, o_m, gates, w_oa, w_ob, w_om, w_out, dep):
    s = dx1.shape[0]
    ts = min(256, s)
    nc = w_oa.shape[2]

    def body(dx_ref, o0, o1, o2, l0, l1, l2, oa_ref, ob_ref, om_ref, gt_ref, woa, wob, wom, wout, dep_ref,
             dgp_ref, dpa_ref, dpb_ref, dpm_ref, dog0, dog1, dog2, dl0, dl1, dl2, dob_ref, dom_ref, bacc_ref):
        i = pl.program_id(0)

        @pl.when(i == 0)
        def _():
            bacc_ref[...] = jnp.zeros_like(bacc_ref)

        dmer = _dot_nt(dx_ref[...], wout[...])
        oa, ob, om = oa_ref[...], ob_ref[...], om_ref[...]
        doa = jnp.zeros((ts, A_W), F32)
        dob = jnp.zeros((ts, B_QH * HEAD), F32)
        dom = jnp.zeros((ts, M_W), F32)
        for j in range(CHIPS):
            prods = _branch_products(oa, ob, om, woa, wob, wom, j)
            dmj = dmer[:, j * nc:(j + 1) * nc]
            dps = []
            for br, (p, dref) in enumerate(zip(prods, (dpa_ref, dpb_ref, dpm_ref))):
                cs = slice(br * D_MODEL + j * nc, br * D_MODEL + (j + 1) * nc)
                gt = gt_ref[:, cs]
                dgp = dmj * p * gt * (1.0 - gt)
                dgp_ref[:, cs] = dgp.astype(_MM)
                bacc_ref[:, cs] += _sum8(dgp)
                dp = (dmj * gt).astype(_MM)
                dref[:, j * nc:(j + 1) * nc] = dp
                dps.append(dp)
            doa = doa + _dot_nt(dps[0], woa[j])
            dob = dob + _dot_nt(dps[1], wob[j])
            dom = dom + _dot_nt(dps[2], wom[j])
        dob_ref[...] = dob
        dom_ref[...] = dom
        ws = _group_weights(l0[...], l1[...], l2[...])
        dsum = _seg_mean(doa * oa, HEAD) * float(HEAD)
        for w, dref, lref in zip(ws, (dog0, dog1, dog2), (dl0, dl1, dl2)):
            dref[...] = w * doa
            lref[...] = w * dsum

    return _pc(
        body, name="merge_out_bwd", grid=(s // ts,),
        in_specs=[_row(ts, D_MODEL)] + [_row(ts, A_W)] * 7 + [_row(ts, B_QH * HEAD), _row(ts, M_W), _row(ts, 3 * D_MODEL),
                                                              _res(w_oa.shape), _res(w_ob.shape), _res(w_om.shape),
                                                              _res(w_out.shape), _res((8, 128))],
        out_specs=[_row(ts, 3 * D_MODEL)] + [_row(ts, D_MODEL)] * 3 + [_row(ts, A_W)] * 6
        + [_row(ts, B_QH * HEAD), _row(ts, M_W), _acc((8, 3 * D_MODEL))],
        out_shape=[_sds((s, 3 * D_MODEL), _MM)] + [_sds((s, D_MODEL), _MM)] * 3 + [_sds((s, A_W))] * 6
        + [_sds((s, B_QH * HEAD)), _sds((s, M_W)), _sds((8, 3 * D_MODEL))])(
            dx1, *og, *lg, o_a, o_b, o_m, gates, w_oa, w_ob, w_om, w_out, dep)


def _k_mem_bwd(m_q, gq, mk, mv, o_m, do_m):
    s = m_q.shape[0]
    n = mk.shape[0]
    ts = min(256, s)
    scale = M_HD ** -0.5

    def body(q_ref, g_ref, mk_ref, mv_ref, o_ref, do_ref, dq_ref, dmk_ref, dmv_ref, gacc_ref):
        i = pl.program_id(0)

        @pl.when(i == 0)
        def _():
            dmk_ref[...] = jnp.zeros_like(dmk_ref)
            dmv_ref[...] = jnp.zeros_like(dmv_ref)
            gacc_ref[...] = jnp.zeros_like(gacc_ref)

        gain = g_ref[...]
        qh, r = _seg_norm(q_ref[...], M_HD)
        qn = (qh * gain).astype(_MM)
        do = do_ref[...]
        delta = _seg_mean(do * o_ref[...], M_HD) * float(M_HD)
        dqn = []
        for h in range(M_HEADS):
            hs = slice(h * M_HD, (h + 1) * M_HD)
            p = _mem_probs(qn[:, hs], mk_ref[:, hs])
            dp = _dot_nt(do[:, hs], mv_ref[:, hs])
            ds = (p * (dp - delta[:, hs][:, 0:1]) * scale).astype(_MM)
            dqn.append(_dot(ds, mk_ref[:, hs]))
            dmk_ref[:, hs] += _dot_tn(ds, qn[:, hs])
            dmv_ref[:, hs] += _dot_tn(p, do[:, hs])
        dqn = jnp.concatenate(dqn, axis=1)
        gacc_ref[...] += _sum8(dqn * qh)
        z = dqn * gain
        dq_ref[...] = (r * (z - qh * _seg_mean(z * qh, M_HD))).astype(_MM)

    return _pc(
        body, name="mem_attn_bwd", grid=(s // ts,),
        in_specs=[_row(ts, M_W), _res((1, M_W)), _res((n, M_W)), _res((n, M_W)), _row(ts, M_W), _row(ts, M_W)],
        out_specs=[_row(ts, M_W), _acc((n, M_W)), _acc((n, M_W)), _acc((8, M_W))],
        out_shape=[_sds((s, M_W), _MM), _sds((n, M_W)), _sds((n, M_W)), _sds((8, M_W))])(m_q, gq, mk, mv, o_m, do_m)


def _k_memkv_bwd(mem, mem_norm, w_kv, m_k_norm, mem_n, kv, dmk, dmv):
    n = mem.shape[0]

    def body(m_ref, g_ref, w_ref, gk_ref, mn_ref, kv_ref, dmk_ref, dmv_ref, dw_ref, dg_ref, dgk_ref):
        gk = gk_ref[...]
        kh, r = _seg_norm(kv_ref[:, :M_W], M_HD)
        dmk = dmk_ref[...]
        dgk_ref[...] = _sum8(dmk * kh)
        z = dmk * gk
        dk = r * (z - kh * _seg_mean(z * kh, M_HD))
        dkv = jnp.concatenate([dk, dmv_ref[...]], axis=1).astype(_MM)
        dw_ref[...] = _dot_tn(mn_ref[...], dkv)
        dmn = _dot_nt(dkv, w_ref[...])
        mh, _ = _rms(m_ref[...])
        dg_ref[...] = _sum8(dmn * mh)

    return _pc(body, name="mem_kv_bwd", grid=(1,),
               in_specs=[_acc((n, D_MODEL)), _acc((1, D_MODEL)), _acc(w_kv.shape), _acc((1, M_W)), _acc((n, D_MODEL)),
                         _acc((n, 2 * M_W)), _acc((n, M_W)), _acc((n, M_W))],
               out_specs=[_acc(w_kv.shape), _acc((8, D_MODEL)), _acc((8, M_W))],
               out_shape=[_sds(w_kv.shape), _sds((8, D_MODEL)), _sds((8, M_W))])(
                   mem, mem_norm, w_kv, m_k_norm, mem_n, kv, dmk, dmv)


def _k_band_bwd(qn, kn, vn, do, lse, dl_or_o, *, hq, hk, max_dist, segs, sink, name):
    rows = qn.shape[0]
    nb = rows // BLK
    units = hq // A_HEADS
    shared = hk != hq
    wq, wk = hq * HEAD, hk * HEAD
    scale = HEAD ** -0.5

    def body(*refs):
        (q2_ref, qx_ref, kc_ref, kp_ref, vc_ref, vp_ref, do2_ref, dox_ref, l2_ref, lx_ref, e2_ref, ex_ref) = refs[:12]
        if sink is None:
            dq_ref, dk_ref, dv_ref = refs[12:]
        else:
            sk_ref, dq_ref, dk_ref, dv_ref, sacc_ref = refs[12:]
        i = pl.program_id(0)
        thr = lambda b: jnp.where(_first_flag(b, segs, nb), 1 << 20, BLK - max_dist)
        bias_a, bias_b = _band_bias(thr(2 * i), True), _band_bias(thr(2 * i + 1), True)
        bias_c = _band_bias(thr(2 * i + 2), False)
        if sink is not None:
            @pl.when(i == 0)
            def _():
                sacc_ref[...] = jnp.zeros_like(sacc_ref)

        def tile(q4, do4, l_cols, dlt, kd, vd, bias, width):
            s, dp = _dot_nt(q4, kd) * scale, _dot_nt(do4, vd)
            ps, dss = [], []
            for h in range(A_HEADS):
                seg = slice(h * width, (h + 1) * width)
                p = jnp.exp(s[:, seg] + bias - l_cols[h])
                ps.append(p)
                dss.append(p * (dp[:, seg] - dlt[:, h * HEAD:h * HEAD + 1]) * scale)
            return ps, dss

        cat = lambda parts: jnp.concatenate([t.astype(_MM) for t in parts], axis=1)
        for u in range(units):
            us = slice(u * A_W, (u + 1) * A_W)
            k_a = _unit_kv(((kp_ref, _LO), (kc_ref, _LO)), u, shared)
            v_a = _unit_kv(((vp_ref, _LO), (vc_ref, _LO)), u, shared)
            k_b, v_b = _unit_kv(((kc_ref, _BOTH),), u, shared), _unit_kv(((vc_ref, _BOTH),), u, shared)
            kd_a, vd_a, kd_b, vd_b = _blockdiag(k_a), _blockdiag(v_a), _blockdiag(k_b), _blockdiag(v_b)
            kd_c, vd_c = _blockdiag(k_b[BLK:]), _blockdiag(v_b[BLK:])
            qs = (q2_ref[_LO, us], q2_ref[_HI, us], qx_ref[:, us])
            dos = (do2_ref[_LO, us], do2_ref[_HI, us], dox_ref[:, us])
            lcols = [[ref[rs, u * A_W + h * HEAD:u * A_W + h * HEAD + 1] for h in range(A_HEADS)]
                     for ref, rs in ((l2_ref, _LO), (l2_ref, _HI), (lx_ref, _LO))]
            if sink is None:
                dlts = (e2_ref[_LO, us], e2_ref[_HI, us], ex_ref[:, us])
            else:
                dlts = tuple(_seg_sum64(d.astype(F32) * ref[rs, us])
                             for d, (ref, rs) in zip(dos, ((e2_ref, _LO), (e2_ref, _HI), (ex_ref, _LO))))
                for t in range(2):
                    for h in range(A_HEADS):
                        j = u * A_HEADS + h
                        sacc_ref[:, j:j + 1] += -jnp.exp(sk_ref[j] - lcols[t][h]) * dlts[t][:, h * HEAD:h * HEAD + 1]
            p_a, ds_a = tile(qs[0], dos[0], lcols[0], dlts[0], kd_a, vd_a, bias_a, 2 * BLK)
            p_b, ds_b = tile(qs[1], dos[1], lcols[1], dlts[1], kd_b, vd_b, bias_b, 2 * BLK)
            p_c, ds_c = tile(qs[2], dos[2], lcols[2], dlts[2], kd_c, vd_c, bias_c, BLK)
            dq_ref[_LO, us] = _dot(cat(ds_a), kd_a)
            dq_ref[_HI, us] = _dot(cat(ds_b), kd_b)
            outs = []
            for pa, pb, pc, lhs in ((ds_a, ds_b, ds_c, qs), (p_a, p_b, p_c, dos)):
                from_a = _fold_diag(_dot_tn(cat([t[:, BLK:] for t in pa]), lhs[0]), BLK)
                from_b = _fold_diag(_dot_tn(cat(pb), lhs[1]), 2 * BLK)
                from_c = _fold_diag(_dot_tn(cat(pc), lhs[2]), BLK)
                outs.append(jnp.concatenate([from_a + from_b[:BLK], from_b[BLK:] + from_c], axis=0))
            dk4, dv4 = outs
            if shared:
                fold = lambda t: (t[:, 0:HEAD] + t[:, HEAD:2 * HEAD]) + (t[:, 2 * HEAD:3 * HEAD] + t[:, 3 * HEAD:])
                dk_ref[:, u * HEAD:(u + 1) * HEAD] = fold(dk4)
                dv_ref[:, u * HEAD:(u + 1) * HEAD] = fold(dv4).astype(_MM)
            else:
                dk_ref[:, us] = dk4
                dv_ref[:, us] = dv4.astype(_MM)

    two = lambda w: pl.BlockSpec((2 * BLK, w), lambda i: (i, 0))
    prev = lambda w: pl.BlockSpec((BLK, w), lambda i: (jnp.maximum(2 * i - 1, 0), 0))
    nxt = lambda w: pl.BlockSpec((BLK, w), lambda i: (jnp.minimum(2 * i + 2, nb - 1), 0))
    in_specs = [two(wq), nxt(wq), two(wk), prev(wk), two(wk), prev(wk), two(wq), nxt(wq), two(wq), nxt(wq), two(wq), nxt(wq)]
    args = [qn, qn, kn, kn, vn, vn, do, do, lse, lse, dl_or_o, dl_or_o]
    out_specs = [two(wq), two(wk), two(wk)]
    out_shape = [_sds((rows, wq)), _sds((rows, wk)), _sds((rows, wk), _MM)]
    if sink is not None:
        in_specs.append(pl.BlockSpec(memory_space=pltpu.SMEM))
        args.append(sink)
        out_specs.append(_acc((BLK, 128)))
        out_shape.append(_sds((BLK, 128)))
    return _pc(body, name=name, grid=(nb // 2,), in_specs=in_specs, out_specs=out_specs, out_shape=out_shape)(*args)


def _k_prep_bwd(srcs, dqn, dkn, gq, gk, tabs, tab_row, *, wq, wk, rows_per_gain, name):
    rows = dqn.shape[0]
    ts = min(256, rows)
    ngain = gq.shape[0]

    def body(q_ref, k_ref, dq_ref, dk_ref, gq_ref, gk_ref, c_ref, sa_ref, sb_ref, oq_ref, ok_ref, aq_ref, ak_ref):
        i = pl.program_id(0)

        @pl.when(lax.rem(i * ts, rows_per_gain) == 0)
        def _():
            aq_ref[...] = jnp.zeros_like(aq_ref)
            ak_ref[...] = jnp.zeros_like(ak_ref)

        c, sa, sb = c_ref[...], sa_ref[...], sb_ref[...]
        for x_ref, d_ref, g_ref, o_ref, a_ref in ((q_ref, dq_ref, gq_ref, oq_ref, aq_ref),
                                                   (k_ref, dk_ref, gk_ref, ok_ref, ak_ref)):
            xh, r = _seg_norm(x_ref[...], HEAD)
            dt = _rope_bwd(d_ref[...], c, sa, sb)
            a_ref[...] += _sum8(dt * xh)
            z = dt * g_ref[...]
            o_ref[...] = (r * (z - xh * _seg_mean(z * xh, HEAD))).astype(_MM)

    gspec = lambda w: pl.BlockSpec((None, 1, w), lambda i: ((i * ts) // rows_per_gain, 0, 0))
    aspec = lambda w: pl.BlockSpec((None, 8, w), lambda i: ((i * ts) // rows_per_gain, 0, 0))
    return _pc(
        body, name=name, grid=(rows // ts,),
        in_specs=[_row(ts, wq, srcs[0][1]), _row(ts, wk, srcs[1][1]), _row(ts, wq), _row(ts, wk), gspec(wq), gspec(wk)]
        + [pl.BlockSpec((ts, 128), lambda i: (i + tab_row // ts, 0))] * 3,
        out_specs=[_row(ts, wq), _row(ts, wk), aspec(wq), aspec(wk)],
        out_shape=[_sds((rows, wq), _MM), _sds((rows, wk), _MM), _sds((ngain, 8, wq)), _sds((ngain, 8, wk))])(
            srcs[0][0], srcs[1][0], dqn, dkn, gq, gk, *tabs)


def _k_in_bwd(pieces, dgp, x, g1, dx1, w_in, w_gate):
    s = x.shape[0]
    ts = min(256, s)
    nin, ng = w_in.shape[2], w_gate.shape[2]
    widths = [p.shape[1] for p in pieces]
    ncol = sum(widths)

    def body(*refs):
        p_refs = refs[:len(pieces)]
        dgp_ref, x_ref, g_ref, dx1_ref, wi_ref, wg_ref, gx_ref, dpj_ref, gacc_ref = refs[len(pieces):]
        i = pl.program_id(0)

        @pl.when(i == 0)
        def _():
            gacc_ref[...] = jnp.zeros_like(gacc_ref)

        off = 0
        for p_ref, w in zip(p_refs, widths):
            dpj_ref[:, off:off + w] = p_ref[...]
            off += w
        dh = jnp.zeros((ts, D_MODEL), F32)
        for j in range(CHIPS):
            dh = dh + _dot_nt(dpj_ref[:, j * nin:(j + 1) * nin], wi_ref[j])
            dh = dh + _dot_nt(dgp_ref[:, j * ng:(j + 1) * ng], wg_ref[j])
        xh, r = _rms(x_ref[...])
        gacc_ref[...] += _sum8(dh * xh)
        gx_ref[...] = dx1_ref[...] + _rms_bwd(dh, xh, r, g_ref[...])

    return _pc(
        body, name="in_proj_bwd", grid=(s // ts,),
        in_specs=[_row(ts, w) for w in widths] + [_row(ts, CHIPS * ng), _row(ts, D_MODEL), _res((1, D_MODEL)),
                                                  _row(ts, D_MODEL), _res(w_in.shape), _res(w_gate.shape)],
        out_specs=[_row(ts, D_MODEL), _row(ts, ncol), _acc((8, D_MODEL))],
        out_shape=[_sds((s, D_MODEL)), _sds((s, ncol), _MM), _sds((8, D_MODEL))])(*pieces, dgp, x, g1, dx1, w_in, w_gate)


def _k_wgrad(a, b, *, nblk, stacked, name):
    s, k = a.shape
    n = b.shape[1]
    nb = n // nblk
    ts = min(2048 if k <= 1024 else 1024, s)

    def body(a_ref, b_ref, o_ref):
        @pl.when(pl.program_id(1) == 0)
        def _():
            o_ref[...] = jnp.zeros_like(o_ref)

        o_ref[...] += _dot_tn(a_ref[...], b_ref[...])

    if stacked:
        out_spec, out_shape = pl.BlockSpec((None, k, nb), lambda g, t: (g, 0, 0)), _sds((nblk, k, nb))
    else:
        out_spec, out_shape = pl.BlockSpec((k, nb), lambda g, t: (0, g)), _sds((k, n))
    return _pc(body, name=name, grid=(nblk, s // ts),
               in_specs=[pl.BlockSpec((ts, k), lambda g, t: (t, 0)), pl.BlockSpec((ts, nb), lambda g, t: (t, g))],
               out_specs=[out_spec], out_shape=[out_shape])(a, b)[0]


def _to_res(t, d):
    s, c = t.shape
    return t if d == 1 else t.reshape(s // d, d, c).transpose(1, 0, 2).reshape(s, c)


def _from_res(t, d):
    s, c = t.shape
    return t if d == 1 else t.reshape(d, s // d, c).transpose(1, 0, 2).reshape(s, c)


def _tile_gain(g, heads):
    return jnp.tile(g, (1,) * (g.ndim - 1) + (heads,))[..., None, :]


def _local_step(x, mem, pos, target, small, w_in, get_rest, on_grads):
    s = x.shape[0]
    nblk = s // BLK
    g1, g2 = small["attn_norm"], small["ffn_norm"]

    pos_rows = jnp.concatenate([_to_res(pos[:, None], d)[:, 0] for _, d in A_GROUPS] + [pos])
    tabs = _rope_tables(pos_rows)

    h, qa0, qa1, qa2, q_b, k_b, v_b, m_q = _k_in(x, g1, w_in)

    qkv_a = jnp.concatenate([_to_res(t, d) for t, (_, d) in zip((qa0, qa1, qa2), A_GROUPS)], axis=0)
    gq_a = _tile_gain(small["a_q_norm"], A_HEADS)
    gk_a = _tile_gain(small["a_k_norm"], A_HEADS)
    src_a = ((qkv_a, 0), (qkv_a, 1), (qkv_a, 2))
    qn_a, kn_a, vn_a = _k_prep(src_a, gq_a, gk_a, tabs, 0, wq=A_W, wk=A_W, rows_per_gain=s, name="prep_a")
    segs_a = tuple((gi * nblk, nblk // d) for gi, (_, d) in enumerate(A_GROUPS))
    o_res, l_res = _k_band_fwd(qn_a, kn_a, vn_a, hq=A_HEADS, hk=A_HEADS, max_dist=BLK, segs=segs_a, sink=None,
                               name="attn_a")
    og = [_from_res(o_res[gi * s:(gi + 1) * s], d) for gi, (_, d) in enumerate(A_GROUPS)]
    lg = [_from_res(l_res[gi * s:(gi + 1) * s], d) for gi, (_, d) in enumerate(A_GROUPS)]

    gq_b = _tile_gain(small["b_q_norm"], B_QH)
    gk_b = _tile_gain(small["b_k_norm"], B_KVH)
    src_b = ((q_b, 0), (k_b, 0), (v_b, 0))
    qn_b, kn_b, vn_b = _k_prep(src_b, gq_b, gk_b, tabs, 3 * s, wq=B_QH * HEAD, wk=B_KVH * HEAD, rows_per_gain=s,
                               name="prep_b")
    sink_x = small["b_sinks"][0]
    segs_b = ((0, nblk),)
    o_b, l_b = _k_band_fwd(qn_b, kn_b, vn_b, hq=B_QH, hk=B_KVH, max_dist=B_WINDOW - 1, segs=segs_b, sink=sink_x,
                           name="attn_b")

    wts = get_rest(0, o_b)
    gates = _k_gate(h, wts["w_gate"], small["b_gate"])

    gq_m = _tile_gain(small["m_q_norm"], M_HEADS)[0]
    gk_m = _tile_gain(small["m_k_norm"], M_HEADS)[0]
    mem_n, kv, mk, mv = _k_memkv(mem, small["mem_norm"], wts["w_mem_kv"], gk_m)
    o_m = _k_mem_fwd(m_q, gq_m, mk, mv)

    o_a, merged, x1, h2 = _k_merge(og, lg, o_b, o_m, gates, x, wts["w_o_a"], wts["w_o_b"], wts["w_o_m"],
                                   wts["w_out"], g2)
    wts.update(get_rest(1, x1))
    u = _k_up(h2, wts["w_up"])
    dy, f, dc, loss_acc = _k_ffn(u, wts["conv_w"], small["conv_b"], wts["w_down"], x1, target)
    loss = (0.5 / D_MODEL) * jnp.sum(loss_acc)

    dx1, du, cacc, g2acc = _k_conv_bwd(dc, u, wts["conv_w"], wts["w_up"], x1, g2, dy)
    tok = on_grads({"w_up": _k_wgrad(h2, du, nblk=CHIPS, stacked=True, name="dw_up"),
                    "w_down": _k_wgrad(f, dy, nblk=2, stacked=False, name="dw_down").reshape(CHIPS, -1, D_MODEL)}, dx1)
    (dgp, dp_a, dp_b, dp_m, dog0, dog1, dog2, dl0, dl1, dl2, do_b, do_m, bacc) = _k_merge_bwd(
        dx1, og, lg, o_a, o_b, o_m, gates, wts["w_o_a"], wts["w_o_b"], wts["w_o_m"], wts["w_out"], tok)
    tok = on_grads({"w_gate": _k_wgrad(h, dgp, nblk=CHIPS, stacked=True, name="dw_gate"),
                    "w_o_a": _k_wgrad(o_a, dp_a, nblk=CHIPS, stacked=True, name="dw_o_a"),
                    "w_o_b": _k_wgrad(o_b, dp_b, nblk=CHIPS, stacked=True, name="dw_o_b"),
                    "w_o_m": _k_wgrad(o_m, dp_m, nblk=CHIPS, stacked=True, name="dw_o_m"),
                    "w_out": _k_wgrad(merged, dx1, nblk=1, stacked=False, name="dw_out").reshape(CHIPS, -1, D_MODEL)},
                   do_m)

    dq_m, dmk, dmv, gqm_acc = _k_mem_bwd(m_q, gq_m + tok[0:1, 0:1], mk, mv, o_m, do_m)
    dw_kv, gmem_acc, gkm_acc = _k_memkv_bwd(mem, small["mem_norm"], wts["w_mem_kv"], gk_m, mem_n, kv, dmk, dmv)

    dq_bn, dk_bn, dv_b, sacc = _k_band_bwd(qn_b, kn_b, vn_b, do_b, l_b, o_b, hq=B_QH, hk=B_KVH,
                                           max_dist=B_WINDOW - 1, segs=segs_b, sink=sink_x, name="attn_b_bwd")
    tok = on_grads({}, dq_bn)
    dq_b, dk_b, gqb_acc, gkb_acc = _k_prep_bwd(src_b, dq_bn, dk_bn, gq_b + tok[0:1, 0:1], gk_b, tabs, 3 * s, wq=B_QH * HEAD,
                                               wk=B_KVH * HEAD, rows_per_gain=s, name="prep_b_bwd")

    do_res = jnp.concatenate([_to_res(t, d) for t, (_, d) in zip((dog0, dog1, dog2), A_GROUPS)], axis=0)
    dl_res = jnp.concatenate([_to_res(t, d) for t, (_, d) in zip((dl0, dl1, dl2), A_GROUPS)], axis=0)
    dq_an, dk_an, dv_a = _k_band_bwd(qn_a, kn_a, vn_a, do_res, l_res, dl_res, hq=A_HEADS, hk=A_HEADS, max_dist=BLK,
                                     segs=segs_a, sink=None, name="attn_a_bwd")
    dq_a, dk_a, gqa_acc, gka_acc = _k_prep_bwd(src_a, dq_an, dk_an, gq_a, gk_a, tabs, 0, wq=A_W, wk=A_W,
                                               rows_per_gain=s, name="prep_a_bwd")
    pieces = []
    for gi, (_, d) in enumerate(A_GROUPS):
        rs = slice(gi * s, (gi + 1) * s)
        pieces += [_from_res(t[rs], d) for t in (dq_a, dk_a, dv_a)]
    pieces += [dq_b, dk_b, dv_b, dq_m]
    grad_x, dproj, g1acc = _k_in_bwd(pieces, dgp, x, g1, dx1, w_in, wts["w_gate"])
    on_grads({"w_in": _k_wgrad(h, dproj, nblk=CHIPS, stacked=True, name="dw_in"),
              "w_mem_kv": dw_kv.reshape(CHIPS, -1, 2 * M_W)}, grad_x)

    def fold(acc, heads):
        v = jnp.sum(acc, axis=-2)
        return jnp.sum(v.reshape(v.shape[:-1] + (heads, -1)), axis=-2)

    csum = jnp.sum(cacc, axis=1)
    sml = {
        "attn_norm": jnp.sum(g1acc, axis=0), "a_q_norm": fold(gqa_acc, A_HEADS), "a_k_norm": fold(gka_acc, A_HEADS),
        "b_q_norm": fold(gqb_acc[0], B_QH), "b_k_norm": fold(gkb_acc[0], B_KVH),
        "b_sinks": jnp.sum(sacc, axis=0)[:B_QH], "mem_norm": jnp.sum(gmem_acc, axis=0),
        "m_q_norm": fold(gqm_acc, M_HEADS), "m_k_norm": fold(gkm_acc, M_HEADS),
        "b_gate": jnp.sum(bacc, axis=0), "ffn_norm": jnp.sum(g2acc, axis=0),
        "conv_w": csum[1:], "conv_b": csum[0],
    }
    return loss, grad_x, sml


def _mesh_pos():
    return lax.axis_index("x"), lax.axis_index("y"), lax.axis_index("c")


def _chip_peers(x, y):
    return [(1 - x, y), (x, 1 - y), (1 - x, 1 - y)]


_ANY = pl.BlockSpec(memory_space=pl.ANY)


def _comm_call(body, *, name, n_in, out_shape, scratch):
    return pl.pallas_call(body, name=name, in_specs=[_ANY] * n_in, out_specs=[_ANY] * len(out_shape),
                          out_shape=out_shape, scratch_shapes=scratch)


def _remote(src, dst, send_sem, recv_sem, dev):
    return pltpu.make_async_remote_copy(src_ref=src, dst_ref=dst, send_sem=send_sem, recv_sem=recv_sem,
                                        device_id=dev, device_id_type=MESH)


def _gather_shards(shards):
    nt = len(shards)
    split = [sh.shape[0] % 16 == 0 for sh in shards]

    def body(*refs):
        ins, outs = refs[:nt], refs[nt:2 * nt]
        ici_s, ici_r, fwd_s, fwd_r, own_s, own_r = refs[2 * nt:]
        x, y, c = _mesh_pos()
        me = 2 * x + y
        sib = (x, y, 1 - c)
        peers = _chip_peers(x, y)

        def half(ref, t, who):
            if not split[t]:
                return ref
            hr = shards[t].shape[0] // 2
            return ref.at[pl.ds(pl.multiple_of(who * hr, 8), hr), :]

        pending = []
        for t in range(nt):
            own = _remote(ins[t], outs[t].at[me], own_s.at[t], own_r.at[t], sib)
            own.start()
            pending.append(own.wait)
            for k, (px, py) in enumerate(peers):
                rc = _remote(half(ins[t], t, c), half(outs[t].at[me], t, c), ici_s.at[t, k], ici_r.at[t, k], (px, py, c))
                rc.start()
                pending.append(rc.wait_send)
        for t in range(nt):
            for k, (px, py) in enumerate(peers):
                land = half(outs[t].at[2 * px + py], t, c)
                _remote(land, land, ici_s.at[t, k], ici_r.at[t, k], (px, py, c)).wait_recv()
                if split[t]:
                    fw = _remote(land, land, fwd_s.at[t, k], fwd_r.at[t, k], sib)
                    fw.start()
                    pending.append(fw.wait_send)
                    other = half(outs[t].at[2 * px + py], t, 1 - c)
                    pending.append(_remote(other, other, fwd_s.at[t, k], fwd_r.at[t, k], sib).wait_recv)
        for wait in pending:
            wait()

    out_shape = [_sds((CHIPS,) + sh.shape, sh.dtype) for sh in shards]
    dma = pltpu.SemaphoreType.DMA
    scratch = [dma((nt, 3)), dma((nt, 3)), dma((nt, 3)), dma((nt, 3)), dma((nt,)), dma((nt,))]
    return _comm_call(body, name="gather_weights", n_in=nt, out_shape=out_shape, scratch=scratch)(*shards)


def _pair_join(halves, name):
    nt = len(halves)

    def body(*refs):
        ins, got = refs[:nt], refs[nt:2 * nt]
        send_sems, recv_sems = refs[2 * nt:]
        x, y, c = _mesh_pos()
        cps = []
        for t in range(nt):
            rc = _remote(ins[t], got[t], send_sems.at[t], recv_sems.at[t], (x, y, 1 - c))
            rc.start()
            cps.append(rc)
        for rc in cps:
            rc.wait()

    out_shape = [_sds(hf.shape, hf.dtype) for hf in halves]
    scratch = [pltpu.SemaphoreType.DMA((nt,)), pltpu.SemaphoreType.DMA((nt,))]
    return _comm_call(body, name=name, n_in=nt, out_shape=out_shape, scratch=scratch)(*halves)


_HBM = pl.BlockSpec(memory_space=pltpu.HBM)
_SEMS = pl.BlockSpec(memory_space=pltpu.SEMAPHORE)
_EFFECT = pltpu.SideEffectType.DATAFLOW_SIDE_EFFECTING


def _bcast_copies(ins, lands, send_sems, recv_sems):
    x, y, c = _mesh_pos()
    me = 2 * x + y
    targets = [((px, py, c), 2 * px + py) for px, py in _chip_peers(x, y)] + [((x, y, 1 - c), me)]
    out = []
    for t in range(len(ins)):
        for k, (dev, idx) in enumerate(targets):
            i = t * len(targets) + k
            arrival = lambda t=t, i=i, idx=idx, dev=dev: _remote(ins[t], lands[t].at[idx], send_sems.at[i],
                                                                 recv_sems.at[i], dev)
            out.append((_remote(ins[t], lands[t].at[me], send_sems.at[i], recv_sems.at[i], dev), arrival))
    return out


def _scatter_copies(ins, lands, send_sems, recv_sems):
    x, y, c = _mesh_pos()
    out = []
    for t in range(len(ins)):
        for k, (px, py) in enumerate(_chip_peers(x, y)):
            i = t * 3 + k
            cp = _remote(ins[t].at[2 * px + py], lands[t].at[k], send_sems.at[i], recv_sems.at[i], (px, py, c))
            out.append((cp, lambda cp=cp: cp))
    return out


def _pair_copies(ins, lands, send_sems, recv_sems):
    x, y, c = _mesh_pos()
    out = []
    for t in range(len(ins)):
        hr = ins[t].shape[1] // 2
        give = ins[t].at[:, pl.ds(pl.multiple_of((1 - c) * hr, 8), hr), :]
        cp = _remote(give, lands[t], send_sems.at[t], recv_sems.at[t], (x, y, 1 - c))
        out.append((cp, lambda cp=cp: cp))
    return out


def _split_start(copies, srcs, land_shapes, ncopy, dep, name):
    nt = len(srcs)

    def body(*refs):
        ins, lands = refs[:nt], refs[nt:2 * nt]
        send_sems, recv_sems, token = refs[2 * nt + 1], refs[2 * nt + 2], refs[-1]
        for send, _ in copies(ins, lands, send_sems, recv_sems):
            send.start()
        token[...] = jnp.zeros_like(token)

    lands = [pltpu.with_memory_space_constraint(lax.empty(sh, a.dtype), pltpu.HBM) for sh, a in zip(land_shapes, srcs)]
    srcs = [pltpu.with_memory_space_constraint(a, pltpu.HBM) for a in srcs]
    dma = pltpu.SemaphoreType.DMA
    out_shape = ([dma((nt * ncopy,)), dma((nt * ncopy,))] + [pltpu.HBM(a.shape, a.dtype) for a in srcs + lands]
                 + [_sds((8, 128))])
    outs = pl.pallas_call(
        body, name=name, in_specs=[_HBM] * (2 * nt) + [_ANY],
        out_specs=[_SEMS, _SEMS] + [_HBM] * (2 * nt) + [pl.BlockSpec(memory_space=pltpu.VMEM)], out_shape=out_shape,
        input_output_aliases={i: 2 + i for i in range(2 * nt)},
        compiler_params=pltpu.CompilerParams(has_side_effects=_EFFECT))(*srcs, *lands, dep)
    return outs[0], outs[1], outs[2:2 + nt], outs[2 + nt:2 + 2 * nt], outs[-1]


def _split_wait(copies, send_sems, recv_sems, srcs, lands, after, name):
    nt = len(srcs)

    def body(*refs):
        ins, lnd = refs[:nt], refs[nt:2 * nt]
        for send, arrival in copies(ins, lnd, refs[2 * nt], refs[2 * nt + 1]):
            send.wait_send()
            arrival().wait_recv()

    outs = pl.pallas_call(
        body, name=name, in_specs=[_HBM] * (2 * nt) + [_SEMS, _SEMS, _ANY], out_specs=[_HBM] * (2 * nt),
        out_shape=[pltpu.HBM(a.shape, a.dtype) for a in list(srcs) + list(lands)],
        input_output_aliases={i: i for i in range(2 * nt)},
        compiler_params=pltpu.CompilerParams(has_side_effects=_EFFECT))(*srcs, *lands, send_sems, recv_sems, after)
    return outs[:nt], outs[nt:]


def _gather_small(packed):
    n = packed.shape[0]

    def body(in_ref, out_ref, send_sems, recv_sems, loc_sem):
        x, y, c = _mesh_pos()
        me = 4 * x + 2 * y + c
        lc = pltpu.make_async_copy(in_ref, out_ref.at[me], loc_sem)
        lc.start()
        peers = []
        for k in range(1, NDEV):
            px, py, pc = x ^ (k >> 2), y ^ ((k >> 1) & 1), c ^ (k & 1)
            rc = pltpu.make_async_remote_copy(src_ref=in_ref, dst_ref=out_ref.at[me], send_sem=send_sems.at[k - 1],
                                              recv_sem=recv_sems.at[k - 1], device_id=(px, py, pc), device_id_type=MESH)
            rc.start()
            peers.append((k, px, py, pc))
        lc.wait()
        for k, px, py, pc in peers:
            pltpu.make_async_remote_copy(src_ref=in_ref, dst_ref=out_ref.at[4 * px + 2 * py + pc],
                                         send_sem=send_sems.at[k - 1], recv_sem=recv_sems.at[k - 1],
                                         device_id=(px, py, pc), device_id_type=MESH).wait()

    scratch = [pltpu.SemaphoreType.DMA((NDEV - 1,)), pltpu.SemaphoreType.DMA((NDEV - 1,)), pltpu.SemaphoreType.DMA]
    return _comm_call(body, name="gather_small_grads", n_in=1, out_shape=[_sds((NDEV, n, 128))],
                      scratch=scratch)(packed)[0]


def _row_tile(r, c):
    t = r
    while t * c * 4 > (1 << 20) and t % 16 == 0:
        t //= 2
    return t


def _k_pair_add(full, got, name):
    g, r, c = full.shape
    hr = r // 2
    tr = _row_tile(hr, c)
    nh = hr // tr

    def body(a_ref, b_ref, o_ref):
        o_ref[...] = (a_ref[...] + b_ref[...]).astype(_WIRE)

    mine = pl.BlockSpec((None, tr, c), lambda i, j: (i, lax.axis_index("c") * nh + j, 0))
    spec = pl.BlockSpec((None, tr, c), lambda i, j: (i, j, 0))
    return _pc(body, name=name, grid=(g, nh), in_specs=[mine, spec], out_specs=[spec],
               out_shape=[_sds((g, hr, c), _WIRE)])(full, got)[0]


def _k_chip_sum(parts, slots, name):
    _, r, c = parts.shape
    tr = _row_tile(r, c)

    def body(a_ref, s_ref, o_ref):
        acc = a_ref[...].astype(F32)
        for k in range(3):
            acc = acc + s_ref[k].astype(F32)
        o_ref[...] = acc

    own = pl.BlockSpec((None, tr, c), lambda i: (2 * lax.axis_index("x") + lax.axis_index("y"), i, 0))
    return _pc(body, name=name, grid=(r // tr,), in_specs=[own, pl.BlockSpec((3, tr, c), lambda i: (0, i, 0))],
               out_specs=[_row(tr, c)], out_shape=[_sds((r, c))])(parts, slots)[0]


def _adam(w, g, m, v):
    m = ADAM_B1 * m + (1.0 - ADAM_B1) * g
    v = ADAM_B2 * v + (1.0 - ADAM_B2) * (g * g)
    m_hat = m / (1.0 - ADAM_B1 ** ADAM_STEP)
    v_hat = v / (1.0 - ADAM_B2 ** ADAM_STEP)
    return -ADAM_LR * (m_hat / (jnp.sqrt(v_hat) + ADAM_EPS) + ADAM_WD * w), m, v


def _k_adam(w, mine, theirs, m, v, dep, name):
    r, c = w.shape
    hr = r // 2
    tr = _row_tile(hr, c)
    nh = hr // tr

    def body(w_ref, a_ref, b_ref, m_ref, v_ref, dep_ref, g_ref, d_ref, mo_ref, vo_ref):
        upper = (pl.program_id(0) >= nh).astype(jnp.int32)
        g = jnp.where(upper == lax.axis_index("c"), a_ref[...], b_ref[...])
        g_ref[...] = g
        d_ref[...], mo_ref[...], vo_ref[...] = _adam(w_ref[...], g, m_ref[...], v_ref[...])

    hspec = pl.BlockSpec((tr, c), lambda i: (jnp.where(i >= nh, i - nh, i), 0))
    return _pc(body, name=name, grid=(r // tr,),
               in_specs=[_row(tr, c), hspec, hspec, _row(tr, c), _row(tr, c), _res((8, 128))],
               out_specs=[_row(tr, c)] * 4, out_shape=[_sds((r, c))] * 4)(w, mine, theirs, m, v, dep)


def _k_sum8(a):
    _, n, _ = a.shape

    def body(a_ref, o_ref):
        acc = a_ref[0]
        for k in range(1, NDEV):
            acc = acc + a_ref[k]
        o_ref[...] = acc

    return _pc(body, name="sum_small_grads", grid=(1,), in_specs=[_acc(a.shape)], out_specs=[_acc((n, 128))],
               out_shape=[_sds((n, 128))])(a)[0]


def _k_adam_small(ws, gs, ms, vs):
    n = len(ws)

    def body(*refs):
        for k in range(n):
            w_ref, g_ref, m_ref, v_ref, d_ref, mo_ref, vo_ref = refs[k::n]
            d_ref[...], mo_ref[...], vo_ref[...] = _adam(w_ref[...], g_ref[...], m_ref[...], v_ref[...])

    specs = [_acc(a.shape) for a in ws]
    outs = _pc(body, name="adam_small", grid=(1,), in_specs=specs * 4, out_specs=specs * 3,
               out_shape=[_sds(a.shape) for a in ws] * 3)(*ws, *gs, *ms, *vs)
    return outs[:n], outs[n:2 * n], outs[2 * n:]


def _pack(vals):
    rows = []
    for a in vals:
        flat = a.reshape(-1)
        n = -(-flat.shape[0] // 1024) * 1024
        rows.append(jnp.pad(flat, (0, n - flat.shape[0])).reshape(n // 128, 128))
    return jnp.concatenate(rows, axis=0)


def _unpack(packed, shapes):
    out, off = [], 0
    for sh in shapes:
        size = int(np.prod(sh))
        n = -(-size // 1024) * 1024
        out.append(packed[off // 128:(off + n) // 128].reshape(-1)[:size].reshape(sh))
        off += n
    return out


_WEIGHTS = ["attn_norm", "w_in", "a_q_norm", "a_k_norm", "b_q_norm", "b_k_norm", "b_sinks", "mem_norm", "w_mem_kv",
            "m_q_norm", "m_k_norm", "w_o_a", "w_o_b", "w_o_m", "w_gate", "b_gate", "w_out", "ffn_norm", "w_up",
            "conv_w", "conv_b", "w_down"]
_BIG = ["w_in", "w_mem_kv", "w_o_a", "w_o_b", "w_o_m", "w_gate", "w_out", "w_up", "w_down"]
_SMALL = [n for n in _WEIGHTS if n not in _BIG]


def kernel(x, mem, positions, attn_norm, w_in, a_q_norm, a_k_norm, b_q_norm, b_k_norm, b_sinks, mem_norm, w_mem_kv, m_q_norm, m_k_norm, w_o_a, w_o_b, w_o_m, w_gate, b_gate, w_out, ffn_norm, w_up, conv_w, conv_b, w_down, loss_target, m_attn_norm, m_w_in, m_a_q_norm, m_a_k_norm, m_b_q_norm, m_b_k_norm, m_b_sinks, m_mem_norm, m_w_mem_kv, m_m_q_norm, m_m_k_norm, m_w_o_a, m_w_o_b, m_w_o_m, m_w_gate, m_b_gate, m_w_out, m_ffn_norm, m_w_up, m_conv_w, m_conv_b, m_w_down, v_attn_norm, v_w_in, v_a_q_norm, v_a_k_norm, v_b_q_norm, v_b_k_norm, v_b_sinks, v_mem_norm, v_w_mem_kv, v_m_q_norm, v_m_k_norm, v_w_o_a, v_w_o_b, v_w_o_m, v_w_gate, v_b_gate, v_w_out, v_ffn_norm, v_w_up, v_conv_w, v_conv_b, v_w_down):
    given = dict(locals())
    w = {n: given[n][0] for n in _WEIGHTS}
    m1 = {n: given["m_" + n][0] for n in _WEIGHTS}
    m2 = {n: given["v_" + n][0] for n in _WEIGHTS}

    w_in = _gather_shards([w["w_in"].astype(_MM)])[0]
    stages = (["w_gate", "w_mem_kv", "w_o_a", "w_o_b", "w_o_m", "w_out"], ["w_up", "w_down", "conv_w"])
    tok, started = w_in, []
    for k, names in enumerate(stages):
        shards = [w[n] if n == "conv_w" else w[n].astype(_MM) for n in names]
        *handles, tok = _split_start(_bcast_copies, shards, [(CHIPS,) + a.shape for a in shards], 4, tok,
                                     "gather_start_%d" % k)
        started.append(handles)
    small = {n: (w[n][None, :] if w[n].ndim == 1 else w[n]) for n in _SMALL if n != "conv_w"}
    small["attn_norm"] = small["attn_norm"] + tok[0:1, 0:1]

    def get_rest(stage, after):
        send, recv, srcs, lands = started[stage]
        got = _split_wait(_bcast_copies, send, recv, srcs, lands, after, "gather_wait_%d" % stage)[1]
        wts = dict(zip(stages[stage], got))
        for n in ("w_mem_kv", "w_out", "w_down"):
            if n in wts:
                wts[n] = wts[n].reshape(-1, wts[n].shape[-1])
        return wts

    parts, slots, pair, scat = {}, {}, [], []
    zeros = jnp.zeros((8, 128), F32)

    def finish_pair(after):
        names, tag, send, recv, srcs, lands = pair.pop()
        full, got = _split_wait(_pair_copies, send, recv, srcs, lands, after, "pair_wait_" + tag)
        mine = [_k_pair_add(f, b, "pair_add_" + n) for n, f, b in zip(names, full, got)]
        shapes = [(3,) + p.shape[1:] for p in mine]
        send, recv, srcs, lands, token = _split_start(_scatter_copies, mine, shapes, 3, zeros, "scatter_start_" + tag)
        scat.append((names, tag, send, recv, srcs, lands))
        return token

    def on_grads(group, after):
        names = list(group)
        tag = "_".join(names)
        token = finish_pair(after) if pair else zeros
        if not group:
            return token
        grads_g = [group[n] for n in names]
        shapes = [(CHIPS, g.shape[1] // 2, g.shape[2]) for g in grads_g]
        send, recv, srcs, lands, token = _split_start(_pair_copies, grads_g, shapes, 1, token, "pair_start_" + tag)
        pair.append((names, tag, send, recv, srcs, lands))
        return token

    loss, grad_x, sml = _local_step(x[0], mem[0], positions[0], loss_target[0], small, w_in, get_rest, on_grads)
    loss = lax.psum(loss, ("x", "y", "c"))
    early = [n for names, *_ in scat for n in names]
    for names, tag, send, recv, srcs, lands in scat:
        mine, got = _split_wait(_scatter_copies, send, recv, srcs, lands, grad_x, "scatter_wait_" + tag)
        parts.update(zip(names, mine))
        slots.update(zip(names, got))
    scat.clear()
    reduced = {n: _k_chip_sum(parts[n], slots[n], "chip_add_" + n) for n in early}
    tok = finish_pair(reduced[early[-1]])
    theirs = dict(zip(early, _pair_join([reduced[n] for n in early], "grad_pair_join_early")))
    grads = {}

    shapes = [sml[n].shape for n in _SMALL]
    gsm = dict(zip(_SMALL, _unpack(_k_sum8(_gather_small(_pack([sml[n] for n in _SMALL]))), shapes)))
    nu = w["conv_w"].shape[1]
    chip = 2 * lax.axis_index("x") + lax.axis_index("y")
    gsm["conv_w"] = lax.dynamic_slice_in_dim(gsm["conv_w"], chip * nu, nu, axis=1)
    for n in _SMALL:
        grads[n] = gsm[n].reshape(w[n].shape)

    delta, new_m, new_v = {}, {}, {}
    for n in early:
        grads[n], delta[n], new_m[n], new_v[n] = _k_adam(w[n], reduced[n], theirs[n], m1[n], m2[n], tok, "adam_" + n)
    as2d = lambda d: [d[n][None, :] if d[n].ndim == 1 else d[n] for n in _SMALL]
    for dst, outs in zip((delta, new_m, new_v), _k_adam_small(as2d(w), as2d(grads), as2d(m1), as2d(m2))):
        dst.update((n, a.reshape(w[n].shape)) for n, a in zip(_SMALL, outs))
    late, tag, send, recv, srcs, lands = scat.pop()
    mine, got = _split_wait(_scatter_copies, send, recv, srcs, lands, delta[early[-1]], "scatter_wait_" + tag)
    for n, a, b in zip(late, mine, got):
        reduced[n] = _k_chip_sum(a, b, "chip_add_" + n)
    theirs.update(zip(late, _pair_join([reduced[n] for n in late], "grad_pair_join_late")))
    for n in late:
        grads[n], delta[n], new_m[n], new_v[n] = _k_adam(w[n], reduced[n], theirs[n], m1[n], m2[n], zeros, "adam_" + n)

    lead = lambda d: [d[n][None] for n in _WEIGHTS]
    return (loss, grad_x[None], *lead(grads), *lead(delta), *lead(new_m), *lead(new_v))
```

```python
import math

import jax
import jax.numpy as jnp
import numpy as np
from jax import lax
from jax.experimental import pallas as pl
from jax.experimental.pallas import tpu as pltpu

F32 = jnp.float32
_MM = jnp.bfloat16
_WIRE = jnp.bfloat16

D_MODEL = 1024
HEAD = 64
BLK = 128
A_GROUPS = ((128, 1), (512, 4), (2048, 16))
A_HEADS = 4
A_W = A_HEADS * HEAD
B_QH = 8
B_KVH = 2
B_WINDOW = 128
M_HEADS = 4
M_HD = 128
M_W = M_HEADS * M_HD
D_FF = 2816
EPS = 1e-6
NEG = -1e30
ROPE_THETA = 500000.0
ROPE_ROT = 16
CHIPS = 4
NDEV = 8
ADAM_LR, ADAM_B1, ADAM_B2, ADAM_EPS, ADAM_WD, ADAM_STEP = 0.001, 0.9, 0.999, 1e-08, 0.01, 10
VMEM_LIMIT = 58 * 1024 * 1024
MESH = pl.DeviceIdType.MESH


def _pc(body, *, name, grid, in_specs, out_specs, out_shape, scratch=()):
    return pl.pallas_call(
        body, name=name, grid=grid, in_specs=in_specs, out_specs=out_specs, out_shape=out_shape,
        scratch_shapes=list(scratch),
        compiler_params=pltpu.CompilerParams(dimension_semantics=("arbitrary",) * len(grid),
                                             vmem_limit_bytes=VMEM_LIMIT))


def _row(ts, c, col=0):
    return pl.BlockSpec((ts, c), lambda i: (i, col))


def _res(shape):
    n = len(shape)
    return pl.BlockSpec(tuple(shape), lambda i: (0,) * n, pipeline_mode=pl.Buffered(1))


def _acc(shape):
    n = len(shape)
    return pl.BlockSpec(tuple(shape), lambda i: (0,) * n)


def _sds(shape, dtype=F32):
    return jax.ShapeDtypeStruct(tuple(shape), dtype)


def _dot(a, b):
    return jnp.dot(a.astype(_MM), b.astype(_MM), preferred_element_type=F32)


def _dot_nt(a, b):
    return lax.dot_general(a.astype(_MM), b.astype(_MM), (((1,), (1,)), ((), ())), preferred_element_type=F32)


def _dot_tn(a, b):
    return lax.dot_general(a.astype(_MM), b.astype(_MM), (((0,), (0,)), ((), ())), preferred_element_type=F32)


def _sum8(v):
    ts, c = v.shape
    return jnp.sum(v.reshape(ts // 8, 8, c), axis=0)


def _sigmoid(z):
    return 1.0 / (1.0 + jnp.exp(-z))


def _rms(x):
    r = lax.rsqrt(jnp.mean(x * x, axis=-1, keepdims=True) + EPS)
    return x * r, r


def _rms_bwd(dy, xh, r, gain):
    z = dy * gain
    return r * (z - xh * jnp.mean(z * xh, axis=-1, keepdims=True))


def _split_hi_lo(v):
    hi = v.astype(_MM)
    return hi, (v - hi.astype(F32)).astype(_MM)


def _lane_head(shape):
    return lax.shift_right_logical(lax.broadcasted_iota(jnp.int32, shape, len(shape) - 1), 6)


def _seg_sum64(v):
    w = v.shape[1]
    e = jnp.where(_lane_head((w, w)) == lax.shift_right_logical(lax.broadcasted_iota(jnp.int32, (w, w), 0), 6),
                  1.0, 0.0).astype(_MM)
    hi, lo = _split_hi_lo(v)
    return jnp.dot(hi, e, preferred_element_type=F32) + jnp.dot(lo, e, preferred_element_type=F32)


def _seg_norm(x, seg):
    if seg == HEAD:
        r = lax.rsqrt(_seg_sum64(x * x) * (1.0 / HEAD) + EPS)
        return x * r, r
    w = x.shape[1]
    xh, rr = [], []
    for s in range(w // seg):
        xs = x[:, s * seg:(s + 1) * seg]
        r = lax.rsqrt(jnp.mean(xs * xs, axis=-1, keepdims=True) + EPS)
        xh.append(xs * r)
        rr.append(jnp.broadcast_to(r, xs.shape))
    return jnp.concatenate(xh, axis=1), jnp.concatenate(rr, axis=1)


def _seg_mean(v, seg):
    if seg == HEAD:
        return _seg_sum64(v) * (1.0 / HEAD)
    w = v.shape[1]
    out = []
    for s in range(w // seg):
        vs = v[:, s * seg:(s + 1) * seg]
        out.append(jnp.broadcast_to(jnp.mean(vs, axis=-1, keepdims=True), vs.shape))
    return jnp.concatenate(out, axis=1)


def _rope(t, c, sa, sb):
    out = []
    for cb in range(t.shape[1] // 128):
        tc = t[:, cb * 128:(cb + 1) * 128]
        out.append(tc * c + pltpu.roll(tc, 120, 1) * sa + pltpu.roll(tc, 8, 1) * sb)
    return jnp.concatenate(out, axis=1) if len(out) > 1 else out[0]


def _rope_bwd(dy, c, sa, sb):
    out = []
    for cb in range(dy.shape[1] // 128):
        dc = dy[:, cb * 128:(cb + 1) * 128]
        out.append(dc * c + pltpu.roll(dc * sa, 8, 1) + pltpu.roll(dc * sb, 120, 1))
    return jnp.concatenate(out, axis=1) if len(out) > 1 else out[0]


def _rope_consts():
    half = ROPE_ROT // 2
    c = np.float32(-2.0 * math.log(ROPE_THETA) / ROPE_ROT)
    freqs = np.exp(np.arange(half, dtype=np.float32) * c).astype(np.float32)
    place = np.zeros((3, half, 128), np.float32)
    ones = np.zeros((1, 128), np.float32)
    for lane in range(128):
        d = lane % HEAD
        if d < half:
            place[0, d, lane], place[1, d, lane] = 1.0, -1.0
        elif d < ROPE_ROT:
            place[0, d - half, lane], place[2, d - half, lane] = 1.0, 1.0
        else:
            ones[0, lane] = 1.0
    return np.tile(freqs[:, None], (1, 128)), place, ones


def _rope_tables(pos_rows):
    r = pos_rows.shape[0]
    tr = min(1024, r)
    freqs, place, ones = _rope_consts()

    def split3(v):
        hi, mid = _split_hi_lo(v)
        lo = (v - hi.astype(F32) - mid.astype(F32)).astype(_MM)
        return hi, mid, lo

    def body(p_ref, f_ref, e_ref, one_ref, c_ref, sa_ref, sb_ref):
        for j in range(tr // 128):
            ang = p_ref[j:j + 1, :].astype(F32) * f_ref[...]
            rows = slice(j * 128, (j + 1) * 128)
            for ref, k, v in ((c_ref, 0, jnp.cos(ang)), (sa_ref, 1, jnp.sin(ang)), (sb_ref, 2, jnp.sin(ang))):
                e = e_ref[k].astype(_MM)
                out = sum(_dot_tn(part, e) for part in split3(v))
                ref[rows, :] = out + one_ref[...] if k == 0 else out

    return _pc(body, name="rope_tables", grid=(r // tr,),
               in_specs=[pl.BlockSpec((tr // 128, 128), lambda i: (i, 0)), _acc((ROPE_ROT // 2, 128)),
                         _acc((3, ROPE_ROT // 2, 128)), _acc((1, 128))],
               out_specs=[_row(tr, 128)] * 3, out_shape=[_sds((r, 128))] * 3)(
                   pos_rows.reshape(r // 128, 128), jnp.asarray(freqs), jnp.asarray(place), jnp.asarray(ones))


def _k_in(x, g1, w_in):
    s = x.shape[0]
    ts = min(512, s)
    nin = w_in.shape[2]
    ncol = CHIPS * nin
    a_cols = 3 * A_W
    offs = [0, a_cols, 2 * a_cols, 3 * a_cols, 3 * a_cols + B_QH * HEAD,
            3 * a_cols + (B_QH + B_KVH) * HEAD, 3 * a_cols + (B_QH + 2 * B_KVH) * HEAD, ncol]

    def body(x_ref, g_ref, wi_ref, h_ref, a0, a1, a2, qb, kb, vb, mq, p_scr):
        xh, _ = _rms(x_ref[...])
        h = (xh * g_ref[...]).astype(_MM)
        h_ref[...] = h
        for j in range(CHIPS):
            p_scr[:, j * nin:(j + 1) * nin] = jnp.dot(h, wi_ref[j], preferred_element_type=F32)
        for k, ref in enumerate((a0, a1, a2, qb, kb, vb, mq)):
            ref[...] = p_scr[:, offs[k]:offs[k + 1]]

    widths = [offs[k + 1] - offs[k] for k in range(7)]
    return _pc(
        body, name="in_proj", grid=(s // ts,),
        in_specs=[_row(ts, D_MODEL), _res((1, D_MODEL)), _res(w_in.shape)],
        out_specs=[_row(ts, D_MODEL)] + [_row(ts, w) for w in widths],
        out_shape=[_sds((s, D_MODEL), _MM)] + [_sds((s, w)) for w in widths],
        scratch=[pltpu.VMEM((ts, ncol), F32)])(x, g1, w_in)


def _k_gate(h, w_gate, b_gate):
    s = h.shape[0]
    ts = min(512, s)
    ng = w_gate.shape[2]

    def body(h_ref, wg_ref, bg_ref, gt_ref):
        h = h_ref[...]
        for j in range(CHIPS):
            z = jnp.dot(h, wg_ref[j], preferred_element_type=F32) + bg_ref[:, j * ng:(j + 1) * ng]
            gt_ref[:, j * ng:(j + 1) * ng] = _sigmoid(z)

    return _pc(body, name="gate_proj", grid=(s // ts,),
               in_specs=[_row(ts, D_MODEL), _res(w_gate.shape), _res(b_gate.shape)],
               out_specs=[_row(ts, CHIPS * ng)], out_shape=[_sds((s, CHIPS * ng))])(h, w_gate, b_gate)[0]


def _k_prep(srcs, gq, gk, tabs, tab_row, *, wq, wk, rows_per_gain, name):
    rows = srcs[0][0].shape[0]
    ts = min(512, rows)

    def body(q_ref, k_ref, v_ref, gq_ref, gk_ref, c_ref, sa_ref, sb_ref, qn_ref, kn_ref, vn_ref):
        c, sa, sb = c_ref[...], sa_ref[...], sb_ref[...]
        qh, _ = _seg_norm(q_ref[...], HEAD)
        qn_ref[...] = _rope(qh * gq_ref[...], c, sa, sb).astype(_MM)
        kh, _ = _seg_norm(k_ref[...], HEAD)
        kn_ref[...] = _rope(kh * gk_ref[...], c, sa, sb).astype(_MM)
        vn_ref[...] = v_ref[...].astype(_MM)

    gspec = lambda w: pl.BlockSpec((None, 1, w), lambda i: ((i * ts) // rows_per_gain, 0, 0))
    return _pc(
        body, name=name, grid=(rows // ts,),
        in_specs=[_row(ts, wq, srcs[0][1]), _row(ts, wk, srcs[1][1]), _row(ts, wk, srcs[2][1]),
                  gspec(wq), gspec(wk)] + [pl.BlockSpec((ts, 128), lambda i: (i + tab_row // ts, 0))] * 3,
        out_specs=[_row(ts, wq), _row(ts, wk), _row(ts, wk)],
        out_shape=[_sds((rows, wq), _MM), _sds((rows, wk), _MM), _sds((rows, wk), _MM)])(
            srcs[0][0], srcs[1][0], srcs[2][0], gq, gk, *tabs)


def _first_flag(b, segs, nb):
    first = b >= nb
    for k, (start, period) in enumerate(segs):
        end = segs[k + 1][0] if k + 1 < len(segs) else nb
        first = first | ((b >= start) & (b < end) & (lax.rem(b - start, jnp.int32(period)) == 0))
    return first


def _band_bias(thr, with_cur):
    qi = lax.broadcasted_iota(jnp.int32, (BLK, BLK), 0)
    kj = lax.broadcasted_iota(jnp.int32, (BLK, BLK), 1)
    prev = jnp.where(kj >= qi + thr, 0.0, NEG)
    return jnp.concatenate([prev, jnp.where(kj <= qi, 0.0, NEG)], axis=1) if with_cur else prev


def _blockdiag(t4):
    head = _lane_head((1, A_W))
    return jnp.concatenate([t4 * jnp.where(head == h, 1.0, 0.0).astype(t4.dtype) for h in range(A_HEADS)], axis=0)


def _fold_diag(t, n):
    head = _lane_head((n, A_W))
    out = t[3 * n:4 * n]
    for h in (2, 1, 0):
        out = jnp.where(head == h, t[h * n:(h + 1) * n], out)
    return out


def _expand_heads(cols):
    n = cols[0].shape[0]
    head = _lane_head((n, A_W))
    out = jnp.broadcast_to(cols[3], (n, A_W))
    for h in (2, 1, 0):
        out = jnp.where(head == h, cols[h], out)
    return out


def _unit_kv(pieces, u, shared):
    cols = slice(u * HEAD, (u + 1) * HEAD) if shared else slice(u * A_W, (u + 1) * A_W)
    rows = [ref[rs, cols] for ref, rs in pieces]
    k = rows[0] if len(rows) == 1 else jnp.concatenate(rows, axis=0)
    return jnp.concatenate([k] * A_HEADS, axis=1) if shared else k


_LO, _HI, _BOTH = slice(0, BLK), slice(BLK, 2 * BLK), slice(0, 2 * BLK)


def _k_band_fwd(qn, kn, vn, *, hq, hk, max_dist, segs, sink, name):
    rows = qn.shape[0]
    nb = rows // BLK
    units = hq // A_HEADS
    shared = hk != hq
    wq, wk = hq * HEAD, hk * HEAD
    scale = HEAD ** -0.5

    def body(*refs):
        if sink is None:
            q_ref, kc_ref, kp_ref, vc_ref, vp_ref, o_ref, l_ref = refs
        else:
            q_ref, kc_ref, kp_ref, vc_ref, vp_ref, sk_ref, o_ref, l_ref = refs
        i = pl.program_id(0)
        for half, rs in enumerate((_LO, _HI)):
            bias = _band_bias(jnp.where(_first_flag(2 * i + half, segs, nb), 1 << 20, BLK - max_dist), True)
            kpieces = ((kp_ref, _LO), (kc_ref, _LO)) if half == 0 else ((kc_ref, _BOTH),)
            vpieces = ((vp_ref, _LO), (vc_ref, _LO)) if half == 0 else ((vc_ref, _BOTH),)
            for u in range(units):
                us = slice(u * A_W, (u + 1) * A_W)
                kb = _blockdiag(_unit_kv(kpieces, u, shared))
                vb = _blockdiag(_unit_kv(vpieces, u, shared))
                s_all = _dot_nt(q_ref[rs, us], kb) * scale
                ps, ls = [], []
                for h in range(A_HEADS):
                    s = s_all[:, h * 2 * BLK:(h + 1) * 2 * BLK] + bias
                    m = jnp.max(s, axis=-1, keepdims=True)
                    e = jnp.exp(s - m)
                    lse = m + jnp.log(jnp.sum(e, axis=-1, keepdims=True))
                    if sink is not None:
                        sk = sk_ref[u * A_HEADS + h]
                        mx = jnp.maximum(lse, sk)
                        lse = mx + jnp.log(jnp.exp(lse - mx) + jnp.exp(sk - mx))
                    ps.append((e * jnp.exp(m - lse)).astype(_MM))
                    ls.append(lse)
                o_ref[rs, us] = _dot(jnp.concatenate(ps, axis=1), vb)
                l_ref[rs, us] = _expand_heads(ls)

    two = lambda w: pl.BlockSpec((2 * BLK, w), lambda i: (i, 0))
    prev = lambda w: pl.BlockSpec((BLK, w), lambda i: (jnp.maximum(2 * i - 1, 0), 0))
    in_specs = [two(wq), two(wk), prev(wk), two(wk), prev(wk)]
    args = [qn, kn, kn, vn, vn]
    if sink is not None:
        in_specs.append(pl.BlockSpec(memory_space=pltpu.SMEM))
        args.append(sink)
    return _pc(body, name=name, grid=(nb // 2,), in_specs=in_specs, out_specs=[two(wq), two(wq)],
               out_shape=[_sds((rows, wq)), _sds((rows, wq))])(*args)


def _k_memkv(mem, mem_norm, w_kv, m_k_norm):
    n = mem.shape[0]

    def body(m_ref, g_ref, w_ref, gk_ref, mn_ref, kv_ref, mk_ref, mv_ref):
        mh, _ = _rms(m_ref[...])
        mn = (mh * g_ref[...]).astype(_MM)
        mn_ref[...] = mn
        kv = jnp.dot(mn, w_ref[...], preferred_element_type=F32)
        kv_ref[...] = kv
        kh, _ = _seg_norm(kv[:, :M_W], M_HD)
        mk_ref[...] = (kh * gk_ref[...]).astype(_MM)
        mv_ref[...] = kv[:, M_W:].astype(_MM)

    return _pc(body, name="mem_kv", grid=(1,),
               in_specs=[_acc((n, D_MODEL)), _acc((1, D_MODEL)), _acc(w_kv.shape), _acc((1, M_W))],
               out_specs=[_acc((n, D_MODEL)), _acc((n, 2 * M_W)), _acc((n, M_W)), _acc((n, M_W))],
               out_shape=[_sds((n, D_MODEL), _MM), _sds((n, 2 * M_W)), _sds((n, M_W), _MM), _sds((n, M_W), _MM)])(
                   mem, mem_norm, w_kv, m_k_norm)


def _mem_probs(q, mk):
    sc = _dot_nt(q, mk) * (M_HD ** -0.5)
    e = jnp.exp(sc - jnp.max(sc, axis=-1, keepdims=True))
    return e / jnp.sum(e, axis=-1, keepdims=True)


def _k_mem_fwd(m_q, gq, mk, mv):
    s = m_q.shape[0]
    n = mk.shape[0]
    ts = min(512, s)

    def body(q_ref, g_ref, mk_ref, mv_ref, o_ref):
        qh, _ = _seg_norm(q_ref[...], M_HD)
        qn = (qh * g_ref[...]).astype(_MM)
        for h in range(M_HEADS):
            hs = slice(h * M_HD, (h + 1) * M_HD)
            o_ref[:, hs] = _dot(_mem_probs(qn[:, hs], mk_ref[:, hs]), mv_ref[:, hs])

    return _pc(body, name="mem_attn", grid=(s // ts,),
               in_specs=[_row(ts, M_W), _res((1, M_W)), _res((n, M_W)), _res((n, M_W))],
               out_specs=[_row(ts, M_W)], out_shape=[_sds((s, M_W))])(m_q, gq, mk, mv)[0]


def _group_weights(l0, l1, l2):
    m = jnp.maximum(jnp.maximum(l0, l1), l2)
    e0, e1, e2 = jnp.exp(l0 - m), jnp.exp(l1 - m), jnp.exp(l2 - m)
    inv = 1.0 / (e0 + e1 + e2)
    return e0 * inv, e1 * inv, e2 * inv


def _branch_products(oa, ob, om, woa_ref, wob_ref, wom_ref, j):
    return _dot(oa, woa_ref[j]), _dot(ob, wob_ref[j]), _dot(om, wom_ref[j])


def _k_merge(og, lg, o_b, o_m, gates, x, w_oa, w_ob, w_om, w_out, g2):
    s = x.shape[0]
    ts = min(256, s)
    nc = w_oa.shape[2]

    def body(o0, o1, o2, l0, l1, l2, ob_ref, om_ref, gt_ref, x_ref, woa, wob, wom, wout, g_ref,
             oa_ref, mer_ref, x1_ref, h2_ref, m_scr):
        w0, w1, w2 = _group_weights(l0[...], l1[...], l2[...])
        oa = w0 * o0[...] + w1 * o1[...] + w2 * o2[...]
        oa_ref[...] = oa
        ob, om = ob_ref[...], om_ref[...]
        for j in range(CHIPS):
            pa, pb, pm = _branch_products(oa, ob, om, woa, wob, wom, j)
            cs = lambda br: slice(br * D_MODEL + j * nc, br * D_MODEL + (j + 1) * nc)
            m_scr[:, j * nc:(j + 1) * nc] = gt_ref[:, cs(0)] * pa + gt_ref[:, cs(1)] * pb + gt_ref[:, cs(2)] * pm
        mer = m_scr[...].astype(_MM)
        mer_ref[...] = mer
        x1 = x_ref[...] + jnp.dot(mer, wout[...], preferred_element_type=F32)
        x1_ref[...] = x1
        xh, _ = _rms(x1)
        h2_ref[...] = (xh * g_ref[...]).astype(_MM)

    return _pc(
        body, name="merge_out", grid=(s // ts,),
        in_specs=[_row(ts, A_W)] * 6 + [_row(ts, B_QH * HEAD), _row(ts, M_W), _row(ts, 3 * D_MODEL), _row(ts, D_MODEL),
                                         _res(w_oa.shape), _res(w_ob.shape), _res(w_om.shape), _res(w_out.shape),
                                         _res((1, D_MODEL))],
        out_specs=[_row(ts, A_W), _row(ts, D_MODEL), _row(ts, D_MODEL), _row(ts, D_MODEL)],
        out_shape=[_sds((s, A_W)), _sds((s, D_MODEL), _MM), _sds((s, D_MODEL)), _sds((s, D_MODEL), _MM)],
        scratch=[pltpu.VMEM((ts, D_MODEL), F32)])(*og, *lg, o_b, o_m, gates, x, w_oa, w_ob, w_om, w_out, g2)


def _k_up(h2, w_up):
    s = h2.shape[0]
    ts = min(512, s)
    nu = w_up.shape[2]

    def body(h_ref, w_ref, u_ref):
        h = h_ref[...]
        for j in range(CHIPS):
            u_ref[:, j * nu:(j + 1) * nu] = jnp.dot(h, w_ref[j], preferred_element_type=F32)

    return _pc(body, name="up_proj", grid=(s // ts,), in_specs=[_row(ts, D_MODEL), _res(w_up.shape)],
               out_specs=[_row(ts, CHIPS * nu)], out_shape=[_sds((s, CHIPS * nu))])(h2, w_up)[0]


def _shift_down(v, halo, k):
    rolled = pltpu.roll(v, k, 0)
    row = lax.broadcasted_iota(jnp.int32, (8, v.shape[1]), 0)
    slab = rolled[0:8]
    for r in range(k):
        slab = jnp.where(row == r, halo[8 - k + r:8 - k + r + 1, :], slab)
    return jnp.concatenate([slab, rolled[8:]], axis=0)


def _shift_up(v, halo, k):
    ts = v.shape[0]
    rolled = pltpu.roll(v, ts - k, 0)
    row = lax.broadcasted_iota(jnp.int32, (8, v.shape[1]), 0)
    slab = rolled[ts - 8:]
    for r in range(k):
        slab = jnp.where(row == 8 - k + r, halo[r:r + 1, :], slab)
    return jnp.concatenate([rolled[:ts - 8], slab], axis=0)


def _k_ffn(u, conv_w, conv_b, w_down, w_down_t, x1, target):
    s = u.shape[0]
    ts = min(256, s)
    nu = conv_w.shape[2]
    half = CHIPS // 2

    def body(u_ref, uh_ref, cw_ref, cb_ref, wd_ref, wdt_ref, x1_ref, t_ref, dy_ref, f_ref, dc_ref, loss_ref, c_scr,
             f_scr, s_scr):
        i = pl.program_id(0)
        halo = jnp.where(i > 0, uh_ref[...], 0.0)
        for j in range(CHIPS):
            cs = slice(j * nu, (j + 1) * nu)
            uj = u_ref[:, cs]
            hj = halo[:, cs]
            c_scr[:, cs] = (cb_ref[:, cs] + cw_ref[j, 0:1, :] * _shift_down(uj, hj, 2)
                            + cw_ref[j, 1:2, :] * _shift_down(uj, hj, 1) + cw_ref[j, 2:3, :] * uj)
        for j in range(half):
            a = c_scr[:, j * nu:(j + 1) * nu]
            g = c_scr[:, (half + j) * nu:(half + j + 1) * nu]
            sa = _sigmoid(a)
            s_scr[:, j * nu:(j + 1) * nu] = sa
            f_scr[:, j * nu:(j + 1) * nu] = (a * sa * g).astype(_MM)
        f = f_scr[...]
        f_ref[...] = f
        y = x1_ref[...] + jnp.dot(f, wd_ref[...], preferred_element_type=F32)
        err = y - t_ref[...]
        dy = err * (1.0 / D_MODEL)
        dy_ref[...] = dy

        @pl.when(i == 0)
        def _():
            loss_ref[...] = jnp.zeros_like(loss_ref)

        loss_ref[...] += _sum8(err * err)
        df = _dot(dy, wdt_ref[...])
        for j in range(half):
            a = c_scr[:, j * nu:(j + 1) * nu]
            g = c_scr[:, (half + j) * nu:(half + j + 1) * nu]
            sa = s_scr[:, j * nu:(j + 1) * nu]
            dfj = df[:, j * nu:(j + 1) * nu]
            dc_ref[:, j * nu:(j + 1) * nu] = dfj * g * (sa * (1.0 + a * (1.0 - sa)))
            dc_ref[:, (half + j) * nu:(half + j + 1) * nu] = dfj * (a * sa)

    wide = CHIPS * nu
    return _pc(
        body, name="conv_ffn", grid=(s // ts,),
        in_specs=[_row(ts, wide), pl.BlockSpec((8, wide), lambda i: (jnp.maximum(i * (ts // 8) - 1, 0), 0)),
                  _res(conv_w.shape), _res((1, wide)), _res(w_down.shape), _res(w_down_t.shape), _row(ts, D_MODEL),
                  _row(ts, D_MODEL)],
        out_specs=[_row(ts, D_MODEL), _row(ts, D_FF), _row(ts, wide), _acc((8, D_MODEL))],
        out_shape=[_sds((s, D_MODEL)), _sds((s, D_FF), _MM), _sds((s, wide)), _sds((8, D_MODEL))],
        scratch=[pltpu.VMEM((ts, wide), F32), pltpu.VMEM((ts, D_FF), _MM), pltpu.VMEM((ts, D_FF), F32)])(
            u, u, conv_w, conv_b, w_down, w_down_t, x1, target)


def _k_conv_bwd(dc, u, conv_w, w_up, x1, g2, dy):
    s = u.shape[0]
    ts = min(256, s)
    nu = conv_w.shape[2]
    wide = CHIPS * nu
    last = s // ts - 1

    def body(dc_ref, dn_ref, u_ref, cw_ref, wu_ref, x1_ref, g_ref, dy_ref, dx1_ref, du_ref, cacc_ref, gacc_ref):
        i = pl.program_id(0)

        @pl.when(i == 0)
        def _():
            cacc_ref[...] = jnp.zeros_like(cacc_ref)
            gacc_ref[...] = jnp.zeros_like(gacc_ref)

        dhalo = jnp.where(i < last, dn_ref[...], 0.0)
        dh2 = jnp.zeros((ts, D_MODEL), F32)
        for j in range(CHIPS):
            cs = slice(j * nu, (j + 1) * nu)
            dcj, uj = dc_ref[:, cs], u_ref[:, cs]
            dc1, dc2 = _shift_up(dcj, dhalo[:, cs], 1), _shift_up(dcj, dhalo[:, cs], 2)
            cacc_ref[0, :, cs] += _sum8(dcj)
            cacc_ref[1, :, cs] += _sum8(dc2 * uj)
            cacc_ref[2, :, cs] += _sum8(dc1 * uj)
            cacc_ref[3, :, cs] += _sum8(dcj * uj)
            du = (cw_ref[j, 2:3, :] * dcj + cw_ref[j, 1:2, :] * dc1 + cw_ref[j, 0:1, :] * dc2).astype(_MM)
            du_ref[:, cs] = du
            dh2 = dh2 + _dot_nt(du, wu_ref[j])
        xh, r = _rms(x1_ref[...])
        gacc_ref[...] += _sum8(dh2 * xh)
        dx1_ref[...] = dy_ref[...] + _rms_bwd(dh2, xh, r, g_ref[...])

    return _pc(
        body, name="conv_up_bwd", grid=(s // ts,),
        in_specs=[_row(ts, wide),
                  pl.BlockSpec((8, wide), lambda i: (jnp.minimum((i + 1) * (ts // 8), s // 8 - 1), 0)),
                  _row(ts, wide), _res(conv_w.shape), _res(w_up.shape), _row(ts, D_MODEL), _res((1, D_MODEL)),
                  _row(ts, D_MODEL)],
        out_specs=[_row(ts, D_MODEL), _row(ts, wide), _acc((4, 8, wide)), _acc((8, D_MODEL))],
        out_shape=[_sds((s, D_MODEL)), _sds((s, wide), _MM), _sds((4, 8, wide)), _sds((8, D_MODEL))])(
            dc, dc, u, conv_w, w_up, x1, g2, dy)


def _k_merge_bwd(dx1, og, lg, o_a, o_b, o_m, gates, w_oa, w_ob, w_om, w_out, dep):
    s = dx1.shape[0]
    ts = min(256, s)
    nc = w_oa.shape[2]

    def body(dx_ref, o0, o1, o2, l0, l1, l2, oa_ref, ob_ref, om_ref, gt_ref, woa, wob, wom, wout, dep_ref,
             dgp_ref, dpa_ref, dpb_ref, dpm_ref, dog0, dog1, dog2, dl0, dl1, dl2, dob_ref, dom_ref, bacc_ref):
        i = pl.program_id(0)

        @pl.when(i == 0)
        def _():
            bacc_ref[...] = jnp.zeros_like(bacc_ref)

        dmer = _dot_nt(dx_ref[...], wout[...])
        oa, ob, om = oa_ref[...], ob_ref[...], om_ref[...]
        doa = jnp.zeros((ts, A_W), F32)
        dob = jnp.zeros((ts, B_QH * HEAD), F32)
        dom = jnp.zeros((ts, M_W), F32)
        for j in range(CHIPS):
            prods = _branch_products(oa, ob, om, woa, wob, wom, j)
            dmj = dmer[:, j * nc:(j + 1) * nc]
            dps = []
            for br, (p, dref) in enumerate(zip(prods, (dpa_ref, dpb_ref, dpm_ref))):
                cs = slice(br * D_MODEL + j * nc, br * D_MODEL + (j + 1) * nc)
                gt = gt_ref[:, cs]
                dgp = dmj * p * gt * (1.0 - gt)
                dgp_ref[:, cs] = dgp.astype(_MM)
                bacc_ref[:, cs] += _sum8(dgp)
                dp = (dmj * gt).astype(_MM)
                dref[:, j * nc:(j + 1) * nc] = dp
                dps.append(dp)
            doa = doa + _dot_nt(dps[0], woa[j])
            dob = dob + _dot_nt(dps[1], wob[j])
            dom = dom + _dot_nt(dps[2], wom[j])
        dob_ref[...] = dob
        dom_ref[...] = dom
        ws = _group_weights(l0[...], l1[...], l2[...])
        dsum = _seg_mean(doa * oa, HEAD) * float(HEAD)
        for w, dref, lref in zip(ws, (dog0, dog1, dog2), (dl0, dl1, dl2)):
            dref[...] = w * doa
            lref[...] = w * dsum

    return _pc(
        body, name="merge_out_bwd", grid=(s // ts,),
        in_specs=[_row(ts, D_MODEL)] + [_row(ts, A_W)] * 7 + [_row(ts, B_QH * HEAD), _row(ts, M_W), _row(ts, 3 * D_MODEL),
                                                              _res(w_oa.shape), _res(w_ob.shape), _res(w_om.shape),
                                                              _res(w_out.shape), _res((8, 128))],
        out_specs=[_row(ts, 3 * D_MODEL)] + [_row(ts, D_MODEL)] * 3 + [_row(ts, A_W)] * 6
        + [_row(ts, B_QH * HEAD), _row(ts, M_W), _acc((8, 3 * D_MODEL))],
        out_shape=[_sds((s, 3 * D_MODEL), _MM)] + [_sds((s, D_MODEL), _MM)] * 3 + [_sds((s, A_W))] * 6
        + [_sds((s, B_QH * HEAD)), _sds((s, M_W)), _sds((8, 3 * D_MODEL))])(
            dx1, *og, *lg, o_a, o_b, o_m, gates, w_oa, w_ob, w_om, w_out, dep)


def _k_mem_bwd(m_q, gq, mk, mv, o_m, do_m):
    s = m_q.shape[0]
    n = mk.shape[0]
    ts = min(512, s)
    scale = M_HD ** -0.5

    def body(q_ref, g_ref, mk_ref, mv_ref, o_ref, do_ref, dq_ref, dmk_ref, dmv_ref, gacc_ref):
        i = pl.program_id(0)

        @pl.when(i == 0)
        def _():
            dmk_ref[...] = jnp.zeros_like(dmk_ref)
            dmv_ref[...] = jnp.zeros_like(dmv_ref)
            gacc_ref[...] = jnp.zeros_like(gacc_ref)

        gain = g_ref[...]
        qh, r = _seg_norm(q_ref[...], M_HD)
        qn = (qh * gain).astype(_MM)
        do = do_ref[...]
        delta = _seg_mean(do * o_ref[...], M_HD) * float(M_HD)
        dqn = []
        for h in range(M_HEADS):
            hs = slice(h * M_HD, (h + 1) * M_HD)
            p = _mem_probs(qn[:, hs], mk_ref[:, hs])
            dp = _dot_nt(do[:, hs], mv_ref[:, hs])
            ds = (p * (dp - delta[:, hs][:, 0:1]) * scale).astype(_MM)
            dqn.append(_dot(ds, mk_ref[:, hs]))
            dmk_ref[:, hs] += _dot_tn(ds, qn[:, hs])
            dmv_ref[:, hs] += _dot_tn(p, do[:, hs])
        dqn = jnp.concatenate(dqn, axis=1)
        gacc_ref[...] += _sum8(dqn * qh)
        z = dqn * gain
        dq_ref[...] = (r * (z - qh * _seg_mean(z * qh, M_HD))).astype(_MM)

    return _pc(
        body, name="mem_attn_bwd", grid=(s // ts,),
        in_specs=[_row(ts, M_W), _res((1, M_W)), _res((n, M_W)), _res((n, M_W)), _row(ts, M_W), _row(ts, M_W)],
        out_specs=[_row(ts, M_W), _acc((n, M_W)), _acc((n, M_W)), _acc((8, M_W))],
        out_shape=[_sds((s, M_W), _MM), _sds((n, M_W)), _sds((n, M_W)), _sds((8, M_W))])(m_q, gq, mk, mv, o_m, do_m)


def _k_memkv_bwd(mem, mem_norm, w_kv, m_k_norm, mem_n, kv, dmk, dmv):
    n = mem.shape[0]

    def body(m_ref, g_ref, w_ref, gk_ref, mn_ref, kv_ref, dmk_ref, dmv_ref, dw_ref, dg_ref, dgk_ref):
        gk = gk_ref[...]
        kh, r = _seg_norm(kv_ref[:, :M_W], M_HD)
        dmk = dmk_ref[...]
        dgk_ref[...] = _sum8(dmk * kh)
        z = dmk * gk
        dk = r * (z - kh * _seg_mean(z * kh, M_HD))
        dkv = jnp.concatenate([dk, dmv_ref[...]], axis=1).astype(_MM)
        dw_ref[...] = _dot_tn(mn_ref[...], dkv)
        dmn = _dot_nt(dkv, w_ref[...])
        mh, _ = _rms(m_ref[...])
        dg_ref[...] = _sum8(dmn * mh)

    return _pc(body, name="mem_kv_bwd", grid=(1,),
               in_specs=[_acc((n, D_MODEL)), _acc((1, D_MODEL)), _acc(w_kv.shape), _acc((1, M_W)), _acc((n, D_MODEL)),
                         _acc((n, 2 * M_W)), _acc((n, M_W)), _acc((n, M_W))],
               out_specs=[_acc(w_kv.shape), _acc((8, D_MODEL)), _acc((8, M_W))],
               out_shape=[_sds(w_kv.shape), _sds((8, D_MODEL)), _sds((8, M_W))])(
                   mem, mem_norm, w_kv, m_k_norm, mem_n, kv, dmk, dmv)


def _k_band_bwd(qn, kn, vn, do, lse, dl_or_o, *, hq, hk, max_dist, segs, sink, name):
    rows = qn.shape[0]
    nb = rows // BLK
    units = hq // A_HEADS
    shared = hk != hq
    wq, wk = hq * HEAD, hk * HEAD
    scale = HEAD ** -0.5

    def body(*refs):
        (q2_ref, qx_ref, kc_ref, kp_ref, vc_ref, vp_ref, do2_ref, dox_ref, l2_ref, lx_ref, e2_ref, ex_ref) = refs[:12]
        if sink is None:
            dq_ref, dk_ref, dv_ref = refs[12:]
        else:
            sk_ref, dq_ref, dk_ref, dv_ref, sacc_ref = refs[12:]
        i = pl.program_id(0)
        thr = lambda b: jnp.where(_first_flag(b, segs, nb), 1 << 20, BLK - max_dist)
        bias_a, bias_b = _band_bias(thr(2 * i), True), _band_bias(thr(2 * i + 1), True)
        bias_c = _band_bias(thr(2 * i + 2), False)
        if sink is not None:
            @pl.when(i == 0)
            def _():
                sacc_ref[...] = jnp.zeros_like(sacc_ref)

        def tile(q4, do4, l_cols, dlt, kd, vd, bias, width):
            s, dp = _dot_nt(q4, kd) * scale, _dot_nt(do4, vd)
            ps, dss = [], []
            for h in range(A_HEADS):
                seg = slice(h * width, (h + 1) * width)
                p = jnp.exp(s[:, seg] + bias - l_cols[h])
                ps.append(p)
                dss.append(p * (dp[:, seg] - dlt[:, h * HEAD:h * HEAD + 1]) * scale)
            return ps, dss

        cat = lambda parts: jnp.concatenate([t.astype(_MM) for t in parts], axis=1)
        for u in range(units):
            us = slice(u * A_W, (u + 1) * A_W)
            k_a = _unit_kv(((kp_ref, _LO), (kc_ref, _LO)), u, shared)
            v_a = _unit_kv(((vp_ref, _LO), (vc_ref, _LO)), u, shared)
            k_b, v_b = _unit_kv(((kc_ref, _BOTH),), u, shared), _unit_kv(((vc_ref, _BOTH),), u, shared)
            kd_a, vd_a, kd_b, vd_b = _blockdiag(k_a), _blockdiag(v_a), _blockdiag(k_b), _blockdiag(v_b)
            kd_c, vd_c = _blockdiag(k_b[BLK:]), _blockdiag(v_b[BLK:])
            qs = (q2_ref[_LO, us], q2_ref[_HI, us], qx_ref[:, us])
            dos = (do2_ref[_LO, us], do2_ref[_HI, us], dox_ref[:, us])
            lcols = [[ref[rs, u * A_W + h * HEAD:u * A_W + h * HEAD + 1] for h in range(A_HEADS)]
                     for ref, rs in ((l2_ref, _LO), (l2_ref, _HI), (lx_ref, _LO))]
            if sink is None:
                dlts = (e2_ref[_LO, us], e2_ref[_HI, us], ex_ref[:, us])
            else:
                dlts = tuple(_seg_sum64(d.astype(F32) * ref[rs, us])
                             for d, (ref, rs) in zip(dos, ((e2_ref, _LO), (e2_ref, _HI), (ex_ref, _LO))))
                for t in range(2):
                    for h in range(A_HEADS):
                        j = u * A_HEADS + h
                        sacc_ref[:, j:j + 1] += -jnp.exp(sk_ref[j] - lcols[t][h]) * dlts[t][:, h * HEAD:h * HEAD + 1]
            p_a, ds_a = tile(qs[0], dos[0], lcols[0], dlts[0], kd_a, vd_a, bias_a, 2 * BLK)
            p_b, ds_b = tile(qs[1], dos[1], lcols[1], dlts[1], kd_b, vd_b, bias_b, 2 * BLK)
            p_c, ds_c = tile(qs[2], dos[2], lcols[2], dlts[2], kd_c, vd_c, bias_c, BLK)
            dq_ref[_LO, us] = _dot(cat(ds_a), kd_a)
            dq_ref[_HI, us] = _dot(cat(ds_b), kd_b)
            outs = []
            for pa, pb, pc, lhs in ((ds_a, ds_b, ds_c, qs), (p_a, p_b, p_c, dos)):
                from_a = _fold_diag(_dot_tn(cat([t[:, BLK:] for t in pa]), lhs[0]), BLK)
                from_b = _fold_diag(_dot_tn(cat(pb), lhs[1]), 2 * BLK)
                from_c = _fold_diag(_dot_tn(cat(pc), lhs[2]), BLK)
                outs.append(jnp.concatenate([from_a + from_b[:BLK], from_b[BLK:] + from_c], axis=0))
            dk4, dv4 = outs
            if shared:
                fold = lambda t: (t[:, 0:HEAD] + t[:, HEAD:2 * HEAD]) + (t[:, 2 * HEAD:3 * HEAD] + t[:, 3 * HEAD:])
                dk_ref[:, u * HEAD:(u + 1) * HEAD] = fold(dk4)
                dv_ref[:, u * HEAD:(u + 1) * HEAD] = fold(dv4).astype(_MM)
            else:
                dk_ref[:, us] = dk4
                dv_ref[:, us] = dv4.astype(_MM)

    two = lambda w: pl.BlockSpec((2 * BLK, w), lambda i: (i, 0))
    prev = lambda w: pl.BlockSpec((BLK, w), lambda i: (jnp.maximum(2 * i - 1, 0), 0))
    nxt = lambda w: pl.BlockSpec((BLK, w), lambda i: (jnp.minimum(2 * i + 2, nb - 1), 0))
    in_specs = [two(wq), nxt(wq), two(wk), prev(wk), two(wk), prev(wk), two(wq), nxt(wq), two(wq), nxt(wq), two(wq), nxt(wq)]
    args = [qn, qn, kn, kn, vn, vn, do, do, lse, lse, dl_or_o, dl_or_o]
    out_specs = [two(wq), two(wk), two(wk)]
    out_shape = [_sds((rows, wq)), _sds((rows, wk)), _sds((rows, wk), _MM)]
    if sink is not None:
        in_specs.append(pl.BlockSpec(memory_space=pltpu.SMEM))
        args.append(sink)
        out_specs.append(_acc((BLK, 128)))
        out_shape.append(_sds((BLK, 128)))
    return _pc(body, name=name, grid=(nb // 2,), in_specs=in_specs, out_specs=out_specs, out_shape=out_shape)(*args)


def _k_prep_bwd(srcs, dqn, dkn, gq, gk, tabs, tab_row, *, wq, wk, rows_per_gain, name):
    rows = dqn.shape[0]
    ts = min(512, rows)
    ngain = gq.shape[0]

    def body(q_ref, k_ref, dq_ref, dk_ref, gq_ref, gk_ref, c_ref, sa_ref, sb_ref, oq_ref, ok_ref, aq_ref, ak_ref):
        i = pl.program_id(0)

        @pl.when(lax.rem(i * ts, rows_per_gain) == 0)
        def _():
            aq_ref[...] = jnp.zeros_like(aq_ref)
            ak_ref[...] = jnp.zeros_like(ak_ref)

        c, sa, sb = c_ref[...], sa_ref[...], sb_ref[...]
        for x_ref, d_ref, g_ref, o_ref, a_ref in ((q_ref, dq_ref, gq_ref, oq_ref, aq_ref),
                                                   (k_ref, dk_ref, gk_ref, ok_ref, ak_ref)):
            xh, r = _seg_norm(x_ref[...], HEAD)
            dt = _rope_bwd(d_ref[...], c, sa, sb)
            a_ref[...] += _sum8(dt * xh)
            z = dt * g_ref[...]
            o_ref[...] = (r * (z - xh * _seg_mean(z * xh, HEAD))).astype(_MM)

    gspec = lambda w: pl.BlockSpec((None, 1, w), lambda i: ((i * ts) // rows_per_gain, 0, 0))
    aspec = lambda w: pl.BlockSpec((None, 8, w), lambda i: ((i * ts) // rows_per_gain, 0, 0))
    return _pc(
        body, name=name, grid=(rows // ts,),
        in_specs=[_row(ts, wq, srcs[0][1]), _row(ts, wk, srcs[1][1]), _row(ts, wq), _row(ts, wk), gspec(wq), gspec(wk)]
        + [pl.BlockSpec((ts, 128), lambda i: (i + tab_row // ts, 0))] * 3,
        out_specs=[_row(ts, wq), _row(ts, wk), aspec(wq), aspec(wk)],
        out_shape=[_sds((rows, wq), _MM), _sds((rows, wk), _MM), _sds((ngain, 8, wq)), _sds((ngain, 8, wk))])(
            srcs[0][0], srcs[1][0], dqn, dkn, gq, gk, *tabs)


def _k_in_bwd(pieces, dgp, x, g1, dx1, w_in, w_gate):
    s = x.shape[0]
    ts = min(256, s)
    nin, ng = w_in.shape[2], w_gate.shape[2]
    widths = [p.shape[1] for p in pieces]
    ncol = sum(widths)

    def body(*refs):
        p_refs = refs[:len(pieces)]
        dgp_ref, x_ref, g_ref, dx1_ref, wi_ref, wg_ref, gx_ref, dpj_ref, gacc_ref = refs[len(pieces):]
        i = pl.program_id(0)

        @pl.when(i == 0)
        def _():
            gacc_ref[...] = jnp.zeros_like(gacc_ref)

        off = 0
        for p_ref, w in zip(p_refs, widths):
            dpj_ref[:, off:off + w] = p_ref[...]
            off += w
        dh = jnp.zeros((ts, D_MODEL), F32)
        for j in range(CHIPS):
            dh = dh + _dot_nt(dpj_ref[:, j * nin:(j + 1) * nin], wi_ref[j])
            dh = dh + _dot_nt(dgp_ref[:, j * ng:(j + 1) * ng], wg_ref[j])
        xh, r = _rms(x_ref[...])
        gacc_ref[...] += _sum8(dh * xh)
        gx_ref[...] = dx1_ref[...] + _rms_bwd(dh, xh, r, g_ref[...])

    return _pc(
        body, name="in_proj_bwd", grid=(s // ts,),
        in_specs=[_row(ts, w) for w in widths] + [_row(ts, CHIPS * ng), _row(ts, D_MODEL), _res((1, D_MODEL)),
                                                  _row(ts, D_MODEL), _res(w_in.shape), _res(w_gate.shape)],
        out_specs=[_row(ts, D_MODEL), _row(ts, ncol), _acc((8, D_MODEL))],
        out_shape=[_sds((s, D_MODEL)), _sds((s, ncol), _MM), _sds((8, D_MODEL))])(*pieces, dgp, x, g1, dx1, w_in, w_gate)


def _k_wgrad(a, b, *, nblk, stacked, name):
    s, k = a.shape
    n = b.shape[1]
    nb = n // nblk
    ts = min(2048 if k <= 1024 else 1024, s)

    def body(a_ref, b_ref, o_ref):
        @pl.when(pl.program_id(1) == 0)
        def _():
            o_ref[...] = jnp.zeros_like(o_ref)

        o_ref[...] += _dot_tn(a_ref[...], b_ref[...])

    if stacked:
        out_spec, out_shape = pl.BlockSpec((None, k, nb), lambda g, t: (g, 0, 0)), _sds((nblk, k, nb))
    else:
        out_spec, out_shape = pl.BlockSpec((k, nb), lambda g, t: (0, g)), _sds((k, n))
    return _pc(body, name=name, grid=(nblk, s // ts),
               in_specs=[pl.BlockSpec((ts, k), lambda g, t: (t, 0)), pl.BlockSpec((ts, nb), lambda g, t: (t, g))],
               out_specs=[out_spec], out_shape=[out_shape])(a, b)[0]


def _to_res(t, d):
    s, c = t.shape
    return t if d == 1 else t.reshape(s // d, d, c).transpose(1, 0, 2).reshape(s, c)


def _from_res(t, d):
    s, c = t.shape
    return t if d == 1 else t.reshape(d, s // d, c).transpose(1, 0, 2).reshape(s, c)


def _tile_gain(g, heads):
    return jnp.tile(g, (1,) * (g.ndim - 1) + (heads,))[..., None, :]


def _local_step(x, mem, pos, target, small, w_in, get_rest, on_grads):
    s = x.shape[0]
    nblk = s // BLK
    g1, g2 = small["attn_norm"], small["ffn_norm"]

    pos_rows = jnp.concatenate([_to_res(pos[:, None], d)[:, 0] for _, d in A_GROUPS] + [pos])
    tabs = _rope_tables(pos_rows)

    h, qa0, qa1, qa2, q_b, k_b, v_b, m_q = _k_in(x, g1, w_in)

    qkv_a = jnp.concatenate([_to_res(t, d) for t, (_, d) in zip((qa0, qa1, qa2), A_GROUPS)], axis=0)
    gq_a = _tile_gain(small["a_q_norm"], A_HEADS)
    gk_a = _tile_gain(small["a_k_norm"], A_HEADS)
    src_a = ((qkv_a, 0), (qkv_a, 1), (qkv_a, 2))
    qn_a, kn_a, vn_a = _k_prep(src_a, gq_a, gk_a, tabs, 0, wq=A_W, wk=A_W, rows_per_gain=s, name="prep_a")
    segs_a = tuple((gi * nblk, nblk // d) for gi, (_, d) in enumerate(A_GROUPS))
    o_res, l_res = _k_band_fwd(qn_a, kn_a, vn_a, hq=A_HEADS, hk=A_HEADS, max_dist=BLK, segs=segs_a, sink=None,
                               name="attn_a")
    og = [_from_res(o_res[gi * s:(gi + 1) * s], d) for gi, (_, d) in enumerate(A_GROUPS)]
    lg = [_from_res(l_res[gi * s:(gi + 1) * s], d) for gi, (_, d) in enumerate(A_GROUPS)]

    gq_b = _tile_gain(small["b_q_norm"], B_QH)
    gk_b = _tile_gain(small["b_k_norm"], B_KVH)
    src_b = ((q_b, 0), (k_b, 0), (v_b, 0))
    qn_b, kn_b, vn_b = _k_prep(src_b, gq_b, gk_b, tabs, 3 * s, wq=B_QH * HEAD, wk=B_KVH * HEAD, rows_per_gain=s,
                               name="prep_b")
    sink_x = small["b_sinks"][0]
    segs_b = ((0, nblk),)
    o_b, l_b = _k_band_fwd(qn_b, kn_b, vn_b, hq=B_QH, hk=B_KVH, max_dist=B_WINDOW - 1, segs=segs_b, sink=sink_x,
                           name="attn_b")

    wts = get_rest(0, o_b)
    gates = _k_gate(h, wts["w_gate"], small["b_gate"])

    gq_m = _tile_gain(small["m_q_norm"], M_HEADS)[0]
    gk_m = _tile_gain(small["m_k_norm"], M_HEADS)[0]
    mem_n, kv, mk, mv = _k_memkv(mem, small["mem_norm"], wts["w_mem_kv"], gk_m)
    o_m = _k_mem_fwd(m_q, gq_m, mk, mv)

    o_a, merged, x1, h2 = _k_merge(og, lg, o_b, o_m, gates, x, wts["w_o_a"], wts["w_o_b"], wts["w_o_m"],
                                   wts["w_out"], g2)
    wts.update(get_rest(1, x1))
    u = _k_up(h2, wts["w_up"])
    dy, f, dc, loss_acc = _k_ffn(u, wts["conv_w"], small["conv_b"], wts["w_down"], wts["w_down"].T, x1, target)
    loss = (0.5 / D_MODEL) * jnp.sum(loss_acc)

    dx1, du, cacc, g2acc = _k_conv_bwd(dc, u, wts["conv_w"], wts["w_up"], x1, g2, dy)
    tok = on_grads({"w_up": _k_wgrad(h2, du, nblk=CHIPS, stacked=True, name="dw_up"),
                    "w_down": _k_wgrad(f, dy, nblk=2, stacked=False, name="dw_down").reshape(CHIPS, -1, D_MODEL)}, dx1)
    (dgp, dp_a, dp_b, dp_m, dog0, dog1, dog2, dl0, dl1, dl2, do_b, do_m, bacc) = _k_merge_bwd(
        dx1, og, lg, o_a, o_b, o_m, gates, wts["w_o_a"], wts["w_o_b"], wts["w_o_m"], wts["w_out"], tok)
    tok = on_grads({"w_gate": _k_wgrad(h, dgp, nblk=CHIPS, stacked=True, name="dw_gate"),
                    "w_o_a": _k_wgrad(o_a, dp_a, nblk=CHIPS, stacked=True, name="dw_o_a"),
                    "w_o_b": _k_wgrad(o_b, dp_b, nblk=CHIPS, stacked=True, name="dw_o_b"),
                    "w_o_m": _k_wgrad(o_m, dp_m, nblk=CHIPS, stacked=True, name="dw_o_m"),
                    "w_out": _k_wgrad(merged, dx1, nblk=1, stacked=False, name="dw_out").reshape(CHIPS, -1, D_MODEL)},
                   do_m)

    dq_m, dmk, dmv, gqm_acc = _k_mem_bwd(m_q, gq_m + tok[0:1, 0:1], mk, mv, o_m, do_m)
    dw_kv, gmem_acc, gkm_acc = _k_memkv_bwd(mem, small["mem_norm"], wts["w_mem_kv"], gk_m, mem_n, kv, dmk, dmv)

    dq_bn, dk_bn, dv_b, sacc = _k_band_bwd(qn_b, kn_b, vn_b, do_b, l_b, o_b, hq=B_QH, hk=B_KVH,
                                           max_dist=B_WINDOW - 1, segs=segs_b, sink=sink_x, name="attn_b_bwd")
    tok = on_grads({}, dq_bn)
    dq_b, dk_b, gqb_acc, gkb_acc = _k_prep_bwd(src_b, dq_bn, dk_bn, gq_b + tok[0:1, 0:1], gk_b, tabs, 3 * s, wq=B_QH * HEAD,
                                               wk=B_KVH * HEAD, rows_per_gain=s, name="prep_b_bwd")

    do_res = jnp.concatenate([_to_res(t, d) for t, (_, d) in zip((dog0, dog1, dog2), A_GROUPS)], axis=0)
    dl_res = jnp.concatenate([_to_res(t, d) for t, (_, d) in zip((dl0, dl1, dl2), A_GROUPS)], axis=0)
    dq_an, dk_an, dv_a = _k_band_bwd(qn_a, kn_a, vn_a, do_res, l_res, dl_res, hq=A_HEADS, hk=A_HEADS, max_dist=BLK,
                                     segs=segs_a, sink=None, name="attn_a_bwd")
    dq_a, dk_a, gqa_acc, gka_acc = _k_prep_bwd(src_a, dq_an, dk_an, gq_a, gk_a, tabs, 0, wq=A_W, wk=A_W,
                                               rows_per_gain=s, name="prep_a_bwd")
    pieces = []
    for gi, (_, d) in enumerate(A_GROUPS):
        rs = slice(gi * s, (gi + 1) * s)
        pieces += [_from_res(t[rs], d) for t in (dq_a, dk_a, dv_a)]
    pieces += [dq_b, dk_b, dv_b, dq_m]
    grad_x, dproj, g1acc = _k_in_bwd(pieces, dgp, x, g1, dx1, w_in, wts["w_gate"])
    on_grads({"w_in": _k_wgrad(h, dproj, nblk=CHIPS, stacked=True, name="dw_in"),
              "w_mem_kv": dw_kv.reshape(CHIPS, -1, 2 * M_W)}, grad_x)

    def fold(acc, heads):
        v = jnp.sum(acc, axis=-2)
        return jnp.sum(v.reshape(v.shape[:-1] + (heads, -1)), axis=-2)

    csum = jnp.sum(cacc, axis=1)
    sml = {
        "attn_norm": jnp.sum(g1acc, axis=0), "a_q_norm": fold(gqa_acc, A_HEADS), "a_k_norm": fold(gka_acc, A_HEADS),
        "b_q_norm": fold(gqb_acc[0], B_QH), "b_k_norm": fold(gkb_acc[0], B_KVH),
        "b_sinks": jnp.sum(sacc, axis=0)[:B_QH], "mem_norm": jnp.sum(gmem_acc, axis=0),
        "m_q_norm": fold(gqm_acc, M_HEADS), "m_k_norm": fold(gkm_acc, M_HEADS),
        "b_gate": jnp.sum(bacc, axis=0), "ffn_norm": jnp.sum(g2acc, axis=0),
        "conv_w": csum[1:], "conv_b": csum[0],
    }
    return loss, grad_x, sml


def _mesh_pos():
    return lax.axis_index("x"), lax.axis_index("y"), lax.axis_index("c")


def _chip_peers(x, y):
    return [(1 - x, y), (x, 1 - y), (1 - x, 1 - y)]


_ANY = pl.BlockSpec(memory_space=pl.ANY)


def _comm_call(body, *, name, n_in, out_shape, scratch):
    return pl.pallas_call(body, name=name, in_specs=[_ANY] * n_in, out_specs=[_ANY] * len(out_shape),
                          out_shape=out_shape, scratch_shapes=scratch)


def _remote(src, dst, send_sem, recv_sem, dev):
    return pltpu.make_async_remote_copy(src_ref=src, dst_ref=dst, send_sem=send_sem, recv_sem=recv_sem,
                                        device_id=dev, device_id_type=MESH)


def _gather_shards(shards):
    nt = len(shards)
    split = [sh.shape[0] % 16 == 0 for sh in shards]

    def body(*refs):
        ins, outs = refs[:nt], refs[nt:2 * nt]
        ici_s, ici_r, fwd_s, fwd_r, own_s, own_r = refs[2 * nt:]
        x, y, c = _mesh_pos()
        me = 2 * x + y
        sib = (x, y, 1 - c)
        peers = _chip_peers(x, y)

        def half(ref, t, who):
            if not split[t]:
                return ref
            hr = shards[t].shape[0] // 2
            return ref.at[pl.ds(pl.multiple_of(who * hr, 8), hr), :]

        pending = []
        for t in range(nt):
            own = _remote(ins[t], outs[t].at[me], own_s.at[t], own_r.at[t], sib)
            own.start()
            pending.append(own.wait)
            for k, (px, py) in enumerate(peers):
                rc = _remote(half(ins[t], t, c), half(outs[t].at[me], t, c), ici_s.at[t, k], ici_r.at[t, k], (px, py, c))
                rc.start()
                pending.append(rc.wait_send)
        for t in range(nt):
            for k, (px, py) in enumerate(peers):
                land = half(outs[t].at[2 * px + py], t, c)
                _remote(land, land, ici_s.at[t, k], ici_r.at[t, k], (px, py, c)).wait_recv()
                if split[t]:
                    fw = _remote(land, land, fwd_s.at[t, k], fwd_r.at[t, k], sib)
                    fw.start()
                    pending.append(fw.wait_send)
                    other = half(outs[t].at[2 * px + py], t, 1 - c)
                    pending.append(_remote(other, other, fwd_s.at[t, k], fwd_r.at[t, k], sib).wait_recv)
        for wait in pending:
            wait()

    out_shape = [_sds((CHIPS,) + sh.shape, sh.dtype) for sh in shards]
    dma = pltpu.SemaphoreType.DMA
    scratch = [dma((nt, 3)), dma((nt, 3)), dma((nt, 3)), dma((nt, 3)), dma((nt,)), dma((nt,))]
    return _comm_call(body, name="gather_weights", n_in=nt, out_shape=out_shape, scratch=scratch)(*shards)


def _pair_join(halves, name):
    nt = len(halves)

    def body(*refs):
        ins, got = refs[:nt], refs[nt:2 * nt]
        send_sems, recv_sems = refs[2 * nt:]
        x, y, c = _mesh_pos()
        cps = []
        for t in range(nt):
            rc = _remote(ins[t], got[t], send_sems.at[t], recv_sems.at[t], (x, y, 1 - c))
            rc.start()
            cps.append(rc)
        for rc in cps:
            rc.wait()

    out_shape = [_sds(hf.shape, hf.dtype) for hf in halves]
    scratch = [pltpu.SemaphoreType.DMA((nt,)), pltpu.SemaphoreType.DMA((nt,))]
    return _comm_call(body, name=name, n_in=nt, out_shape=out_shape, scratch=scratch)(*halves)


_HBM = pl.BlockSpec(memory_space=pltpu.HBM)
_SEMS = pl.BlockSpec(memory_space=pltpu.SEMAPHORE)
_EFFECT = pltpu.SideEffectType.DATAFLOW_SIDE_EFFECTING


def _bcast_copies(ins, lands, send_sems, recv_sems):
    x, y, c = _mesh_pos()
    me = 2 * x + y
    targets = [((px, py, c), 2 * px + py) for px, py in _chip_peers(x, y)] + [((x, y, 1 - c), me)]
    out = []
    for t in range(len(ins)):
        for k, (dev, idx) in enumerate(targets):
            i = t * len(targets) + k
            arrival = lambda t=t, i=i, idx=idx, dev=dev: _remote(ins[t], lands[t].at[idx], send_sems.at[i],
                                                                 recv_sems.at[i], dev)
            out.append((_remote(ins[t], lands[t].at[me], send_sems.at[i], recv_sems.at[i], dev), arrival))
    return out


def _scatter_copies(ins, lands, send_sems, recv_sems):
    x, y, c = _mesh_pos()
    out = []
    for t in range(len(ins)):
        for k, (px, py) in enumerate(_chip_peers(x, y)):
            i = t * 3 + k
            cp = _remote(ins[t].at[2 * px + py], lands[t].at[k], send_sems.at[i], recv_sems.at[i], (px, py, c))
            out.append((cp, lambda cp=cp: cp))
    return out


def _pair_copies(ins, lands, send_sems, recv_sems):
    x, y, c = _mesh_pos()
    out = []
    for t in range(len(ins)):
        hr = ins[t].shape[1] // 2
        give = ins[t].at[:, pl.ds(pl.multiple_of((1 - c) * hr, 8), hr), :]
        cp = _remote(give, lands[t], send_sems.at[t], recv_sems.at[t], (x, y, 1 - c))
        out.append((cp, lambda cp=cp: cp))
    return out


def _split_start(copies, srcs, land_shapes, ncopy, dep, name):
    nt = len(srcs)

    def body(*refs):
        ins, lands = refs[:nt], refs[nt:2 * nt]
        send_sems, recv_sems, token = refs[2 * nt + 1], refs[2 * nt + 2], refs[-1]
        for send, _ in copies(ins, lands, send_sems, recv_sems):
            send.start()
        token[...] = jnp.zeros_like(token)

    lands = [pltpu.with_memory_space_constraint(lax.empty(sh, a.dtype), pltpu.HBM) for sh, a in zip(land_shapes, srcs)]
    srcs = [pltpu.with_memory_space_constraint(a, pltpu.HBM) for a in srcs]
    dma = pltpu.SemaphoreType.DMA
    out_shape = ([dma((nt * ncopy,)), dma((nt * ncopy,))] + [pltpu.HBM(a.shape, a.dtype) for a in srcs + lands]
                 + [_sds((8, 128))])
    outs = pl.pallas_call(
        body, name=name, in_specs=[_HBM] * (2 * nt) + [_ANY],
        out_specs=[_SEMS, _SEMS] + [_HBM] * (2 * nt) + [pl.BlockSpec(memory_space=pltpu.VMEM)], out_shape=out_shape,
        input_output_aliases={i: 2 + i for i in range(2 * nt)},
        compiler_params=pltpu.CompilerParams(has_side_effects=_EFFECT))(*srcs, *lands, dep)
    return outs[0], outs[1], outs[2:2 + nt], outs[2 + nt:2 + 2 * nt], outs[-1]


def _split_wait(copies, send_sems, recv_sems, srcs, lands, after, name):
    nt = len(srcs)

    def body(*refs):
        ins, lnd = refs[:nt], refs[nt:2 * nt]
        for send, arrival in copies(ins, lnd, refs[2 * nt], refs[2 * nt + 1]):
            send.wait_send()
            arrival().wait_recv()

    outs = pl.pallas_call(
        body, name=name, in_specs=[_HBM] * (2 * nt) + [_SEMS, _SEMS, _ANY], out_specs=[_HBM] * (2 * nt),
        out_shape=[pltpu.HBM(a.shape, a.dtype) for a in list(srcs) + list(lands)],
        input_output_aliases={i: i for i in range(2 * nt)},
        compiler_params=pltpu.CompilerParams(has_side_effects=_EFFECT))(*srcs, *lands, send_sems, recv_sems, after)
    return outs[:nt], outs[nt:]


def _gather_small(packed):
    n = packed.shape[0]

    def body(in_ref, out_ref, send_sems, recv_sems, loc_sem):
        x, y, c = _mesh_pos()
        me = 4 * x + 2 * y + c
        lc = pltpu.make_async_copy(in_ref, out_ref.at[me], loc_sem)
        lc.start()
        peers = []
        for k in range(1, NDEV):
            px, py, pc = x ^ (k >> 2), y ^ ((k >> 1) & 1), c ^ (k & 1)
            rc = pltpu.make_async_remote_copy(src_ref=in_ref, dst_ref=out_ref.at[me], send_sem=send_sems.at[k - 1],
                                              recv_sem=recv_sems.at[k - 1], device_id=(px, py, pc), device_id_type=MESH)
            rc.start()
            peers.append((k, px, py, pc))
        lc.wait()
        for k, px, py, pc in peers:
            pltpu.make_async_remote_copy(src_ref=in_ref, dst_ref=out_ref.at[4 * px + 2 * py + pc],
                                         send_sem=send_sems.at[k - 1], recv_sem=recv_sems.at[k - 1],
                                         device_id=(px, py, pc), device_id_type=MESH).wait()

    scratch = [pltpu.SemaphoreType.DMA((NDEV - 1,)), pltpu.SemaphoreType.DMA((NDEV - 1,)), pltpu.SemaphoreType.DMA]
    return _comm_call(body, name="gather_small_grads", n_in=1, out_shape=[_sds((NDEV, n, 128))],
                      scratch=scratch)(packed)[0]


def _row_tile(r, c):
    t = r
    while t * c * 4 > (1 << 20) and t % 16 == 0:
        t //= 2
    return t


def _k_pair_add(full, got, name):
    g, r, c = full.shape
    hr = r // 2
    tr = _row_tile(hr, c)
    nh = hr // tr

    def body(a_ref, b_ref, o_ref):
        o_ref[...] = (a_ref[...] + b_ref[...]).astype(_WIRE)

    mine = pl.BlockSpec((None, tr, c), lambda i, j: (i, lax.axis_index("c") * nh + j, 0))
    spec = pl.BlockSpec((None, tr, c), lambda i, j: (i, j, 0))
    return _pc(body, name=name, grid=(g, nh), in_specs=[mine, spec], out_specs=[spec],
               out_shape=[_sds((g, hr, c), _WIRE)])(full, got)[0]


def _k_chip_sum(parts, slots, name):
    _, r, c = parts.shape
    tr = _row_tile(r, c)

    def body(a_ref, s_ref, o_ref):
        acc = a_ref[...].astype(F32)
        for k in range(3):
            acc = acc + s_ref[k].astype(F32)
        o_ref[...] = acc

    own = pl.BlockSpec((None, tr, c), lambda i: (2 * lax.axis_index("x") + lax.axis_index("y"), i, 0))
    return _pc(body, name=name, grid=(r // tr,), in_specs=[own, pl.BlockSpec((3, tr, c), lambda i: (0, i, 0))],
               out_specs=[_row(tr, c)], out_shape=[_sds((r, c))])(parts, slots)[0]


def _adam(w, g, m, v):
    m = ADAM_B1 * m + (1.0 - ADAM_B1) * g
    v = ADAM_B2 * v + (1.0 - ADAM_B2) * (g * g)
    m_hat = m / (1.0 - ADAM_B1 ** ADAM_STEP)
    v_hat = v / (1.0 - ADAM_B2 ** ADAM_STEP)
    return -ADAM_LR * (m_hat / (jnp.sqrt(v_hat) + ADAM_EPS) + ADAM_WD * w), m, v


def _k_adam(w, mine, theirs, m, v, dep, name):
    r, c = w.shape
    hr = r // 2
    tr = _row_tile(hr, c)
    nh = hr // tr

    def body(w_ref, a_ref, b_ref, m_ref, v_ref, dep_ref, g_ref, d_ref, mo_ref, vo_ref):
        upper = (pl.program_id(0) >= nh).astype(jnp.int32)
        g = jnp.where(upper == lax.axis_index("c"), a_ref[...], b_ref[...])
        g_ref[...] = g
        d_ref[...], mo_ref[...], vo_ref[...] = _adam(w_ref[...], g, m_ref[...], v_ref[...])

    hspec = pl.BlockSpec((tr, c), lambda i: (jnp.where(i >= nh, i - nh, i), 0))
    return _pc(body, name=name, grid=(r // tr,),
               in_specs=[_row(tr, c), hspec, hspec, _row(tr, c), _row(tr, c), _res((8, 128))],
               out_specs=[_row(tr, c)] * 4, out_shape=[_sds((r, c))] * 4)(w, mine, theirs, m, v, dep)


def _k_sum8(a):
    _, n, _ = a.shape

    def body(a_ref, o_ref):
        acc = a_ref[0]
        for k in range(1, NDEV):
            acc = acc + a_ref[k]
        o_ref[...] = acc

    return _pc(body, name="sum_small_grads", grid=(1,), in_specs=[_acc(a.shape)], out_specs=[_acc((n, 128))],
               out_shape=[_sds((n, 128))])(a)[0]


def _k_adam_small(ws, gs, ms, vs):
    n = len(ws)

    def body(*refs):
        for k in range(n):
            w_ref, g_ref, m_ref, v_ref, d_ref, mo_ref, vo_ref = refs[k::n]
            d_ref[...], mo_ref[...], vo_ref[...] = _adam(w_ref[...], g_ref[...], m_ref[...], v_ref[...])

    specs = [_acc(a.shape) for a in ws]
    outs = _pc(body, name="adam_small", grid=(1,), in_specs=specs * 4, out_specs=specs * 3,
               out_shape=[_sds(a.shape) for a in ws] * 3)(*ws, *gs, *ms, *vs)
    return outs[:n], outs[n:2 * n], outs[2 * n:]


def _pack(vals):
    rows = []
    for a in vals:
        flat = a.reshape(-1)
        n = -(-flat.shape[0] // 1024) * 1024
        rows.append(jnp.pad(flat, (0, n - flat.shape[0])).reshape(n // 128, 128))
    return jnp.concatenate(rows, axis=0)


def _unpack(packed, shapes):
    out, off = [], 0
    for sh in shapes:
        size = int(np.prod(sh))
        n = -(-size // 1024) * 1024
        out.append(packed[off // 128:(off + n) // 128].reshape(-1)[:size].reshape(sh))
        off += n
    return out


_WEIGHTS = ["attn_norm", "w_in", "a_q_norm", "a_k_norm", "b_q_norm", "b_k_norm", "b_sinks", "mem_norm", "w_mem_kv",
            "m_q_norm", "m_k_norm", "w_o_a", "w_o_b", "w_o_m", "w_gate", "b_gate", "w_out", "ffn_norm", "w_up",
            "conv_w", "conv_b", "w_down"]
_BIG = ["w_in", "w_mem_kv", "w_o_a", "w_o_b", "w_o_m", "w_gate", "w_out", "w_up", "w_down"]
_SMALL = [n for n in _WEIGHTS if n not in _BIG]


def kernel(x, mem, positions, attn_norm, w_in, a_q_norm, a_k_norm, b_q_norm, b_k_norm, b_sinks, mem_norm, w_mem_kv, m_q_norm, m_k_norm, w_o_a, w_o_b, w_o_m, w_gate, b_gate, w_out, ffn_norm, w_up, conv_w, conv_b, w_down, loss_target, m_attn_norm, m_w_in, m_a_q_norm, m_a_k_norm, m_b_q_norm, m_b_k_norm, m_b_sinks, m_mem_norm, m_w_mem_kv, m_m_q_norm, m_m_k_norm, m_w_o_a, m_w_o_b, m_w_o_m, m_w_gate, m_b_gate, m_w_out, m_ffn_norm, m_w_up, m_conv_w, m_conv_b, m_w_down, v_attn_norm, v_w_in, v_a_q_norm, v_a_k_norm, v_b_q_norm, v_b_k_norm, v_b_sinks, v_mem_norm, v_w_mem_kv, v_m_q_norm, v_m_k_norm, v_w_o_a, v_w_o_b, v_w_o_m, v_w_gate, v_b_gate, v_w_out, v_ffn_norm, v_w_up, v_conv_w, v_conv_b, v_w_down):
    given = dict(locals())
    w = {n: given[n][0] for n in _WEIGHTS}
    m1 = {n: given["m_" + n][0] for n in _WEIGHTS}
    m2 = {n: given["v_" + n][0] for n in _WEIGHTS}

    w_in = _gather_shards([w["w_in"].astype(_MM)])[0]
    stages = (["w_gate", "w_mem_kv", "w_o_a", "w_o_b", "w_o_m", "w_out"], ["w_up", "w_down", "conv_w"])
    tok, started = w_in, []
    for k, names in enumerate(stages):
        shards = [w[n] if n == "conv_w" else w[n].astype(_MM) for n in names]
        *handles, tok = _split_start(_bcast_copies, shards, [(CHIPS,) + a.shape for a in shards], 4, tok,
                                     "gather_start_%d" % k)
        started.append(handles)
    small = {n: (w[n][None, :] if w[n].ndim == 1 else w[n]) for n in _SMALL if n != "conv_w"}
    small["attn_norm"] = small["attn_norm"] + tok[0:1, 0:1]

    def get_rest(stage, after):
        send, recv, srcs, lands = started[stage]
        got = _split_wait(_bcast_copies, send, recv, srcs, lands, after, "gather_wait_%d" % stage)[1]
        wts = dict(zip(stages[stage], got))
        for n in ("w_mem_kv", "w_out", "w_down"):
            if n in wts:
                wts[n] = wts[n].reshape(-1, wts[n].shape[-1])
        return wts

    parts, slots, pair, scat = {}, {}, [], []
    zeros = jnp.zeros((8, 128), F32)

    def finish_pair(after):
        names, tag, send, recv, srcs, lands = pair.pop()
        full, got = _split_wait(_pair_copies, send, recv, srcs, lands, after, "pair_wait_" + tag)
        mine = [_k_pair_add(f, b, "pair_add_" + n) for n, f, b in zip(names, full, got)]
        shapes = [(3,) + p.shape[1:] for p in mine]
        send, recv, srcs, lands, token = _split_start(_scatter_copies, mine, shapes, 3, zeros, "scatter_start_" + tag)
        scat.append((names, tag, send, recv, srcs, lands))
        return token

    def on_grads(group, after):
        names = list(group)
        tag = "_".join(names)
        token = finish_pair(after) if pair else zeros
        if not group:
            return token
        grads_g = [group[n] for n in names]
        shapes = [(CHIPS, g.shape[1] // 2, g.shape[2]) for g in grads_g]
        send, recv, srcs, lands, token = _split_start(_pair_copies, grads_g, shapes, 1, token, "pair_start_" + tag)
        pair.append((names, tag, send, recv, srcs, lands))
        return token

    loss, grad_x, sml = _local_step(x[0], mem[0], positions[0], loss_target[0], small, w_in, get_rest, on_grads)
    loss = lax.psum(loss, ("x", "y", "c"))
    early = [n for names, *_ in scat for n in names]
    for names, tag, send, recv, srcs, lands in scat:
        mine, got = _split_wait(_scatter_copies, send, recv, srcs, lands, grad_x, "scatter_wait_" + tag)
        parts.update(zip(names, mine))
        slots.update(zip(names, got))
    scat.clear()
    reduced = {n: _k_chip_sum(parts[n], slots[n], "chip_add_" + n) for n in early}
    tok = finish_pair(reduced[early[-1]])
    theirs = dict(zip(early, _pair_join([reduced[n] for n in early], "grad_pair_join_early")))
    grads = {}

    shapes = [sml[n].shape for n in _SMALL]
    gsm = dict(zip(_SMALL, _unpack(_k_sum8(_gather_small(_pack([sml[n] for n in _SMALL]))), shapes)))
    nu = w["conv_w"].shape[1]
    chip = 2 * lax.axis_index("x") + lax.axis_index("y")
    gsm["conv_w"] = lax.dynamic_slice_in_dim(gsm["conv_w"], chip * nu, nu, axis=1)
    for n in _SMALL:
        grads[n] = gsm[n].reshape(w[n].shape)

    delta, new_m, new_v = {}, {}, {}
    for n in early:
        grads[n], delta[n], new_m[n], new_v[n] = _k_adam(w[n], reduced[n], theirs[n], m1[n], m2[n], tok, "adam_" + n)
    as2d = lambda d: [d[n][None, :] if d[n].ndim == 1 else d[n] for n in _SMALL]
    for dst, outs in zip((delta, new_m, new_v), _k_adam_small(as2d(w), as2d(grads), as2d(m1), as2d(m2))):
        dst.update((n, a.reshape(w[n].shape)) for n, a in zip(_SMALL, outs))
    late, tag, send, recv, srcs, lands = scat.pop()
    mine, got = _split_wait(_scatter_copies, send, recv, srcs, lands, delta[early[-1]], "scatter_wait_" + tag)
    for n, a, b in zip(late, mine, got):
        reduced[n] = _k_chip_sum(a, b, "chip_add_" + n)
    theirs.update(zip(late, _pair_join([reduced[n] for n in late], "grad_pair_join_late")))
    for n in late:
        grads[n], delta[n], new_m[n], new_v[n] = _k_adam(w[n], reduced[n], theirs[n], m1[n], m2[n], zeros, "adam_" + n)

    lead = lambda d: [d[n][None] for n in _WEIGHTS]
    return (loss, grad_x[None], *lead(grads), *lead(delta), *lead(new_m), *lead(new_v))
```

```python
import math

import jax
import jax.numpy as jnp
import numpy as np
from jax import lax
from jax.experimental import pallas as pl
from jax.experimental.pallas import tpu as pltpu

F32 = jnp.float32
_MM = jnp.bfloat16
_WIRE = jnp.bfloat16

D_MODEL = 1024
HEAD = 64
BLK = 128
A_GROUPS = ((128, 1), (512, 4), (2048, 16))
A_HEADS = 4
A_W = A_HEADS * HEAD
B_QH = 8
B_KVH = 2
B_WINDOW = 128
M_HEADS = 4
M_HD = 128
M_W = M_HEADS * M_HD
D_FF = 2816
EPS = 1e-6
NEG = -1e30
ROPE_THETA = 500000.0
ROPE_ROT = 16
CHIPS = 4
NDEV = 8
ADAM_LR, ADAM_B1, ADAM_B2, ADAM_EPS, ADAM_WD, ADAM_STEP = 0.001, 0.9, 0.999, 1e-08, 0.01, 10
VMEM_LIMIT = 58 * 1024 * 1024
MESH = pl.DeviceIdType.MESH


def _pc(body, *, name, grid, in_specs, out_specs, out_shape, scratch=()):
    return pl.pallas_call(
        body, name=name, grid=grid, in_specs=in_specs, out_specs=out_specs, out_shape=out_shape,
        scratch_shapes=list(scratch),
        compiler_params=pltpu.CompilerParams(dimension_semantics=("arbitrary",) * len(grid),
                                             vmem_limit_bytes=VMEM_LIMIT))


def _row(ts, c, col=0):
    return pl.BlockSpec((ts, c), lambda i: (i, col))


def _res(shape):
    n = len(shape)
    return pl.BlockSpec(tuple(shape), lambda i: (0,) * n, pipeline_mode=pl.Buffered(1))


def _acc(shape):
    n = len(shape)
    return pl.BlockSpec(tuple(shape), lambda i: (0,) * n)


def _sds(shape, dtype=F32):
    return jax.ShapeDtypeStruct(tuple(shape), dtype)


def _dot(a, b):
    return jnp.dot(a.astype(_MM), b.astype(_MM), preferred_element_type=F32)


def _dot_nt(a, b):
    return lax.dot_general(a.astype(_MM), b.astype(_MM), (((1,), (1,)), ((), ())), preferred_element_type=F32)


def _dot_tn(a, b):
    return lax.dot_general(a.astype(_MM), b.astype(_MM), (((0,), (0,)), ((), ())), preferred_element_type=F32)


def _sum8(v):
    ts, c = v.shape
    return jnp.sum(v.reshape(ts // 8, 8, c), axis=0)


def _sigmoid(z):
    return 1.0 / (1.0 + jnp.exp(-z))


def _rms(x):
    r = lax.rsqrt(jnp.mean(x * x, axis=-1, keepdims=True) + EPS)
    return x * r, r


def _rms_bwd(dy, xh, r, gain):
    z = dy * gain
    return r * (z - xh * jnp.mean(z * xh, axis=-1, keepdims=True))


def _split_hi_lo(v):
    hi = v.astype(_MM)
    return hi, (v - hi.astype(F32)).astype(_MM)


def _lane_head(shape):
    return lax.shift_right_logical(lax.broadcasted_iota(jnp.int32, shape, len(shape) - 1), 6)


def _seg_sum64(v):
    w = v.shape[1]
    e = jnp.where(_lane_head((w, w)) == lax.shift_right_logical(lax.broadcasted_iota(jnp.int32, (w, w), 0), 6),
                  1.0, 0.0).astype(_MM)
    hi, lo = _split_hi_lo(v)
    return jnp.dot(hi, e, preferred_element_type=F32) + jnp.dot(lo, e, preferred_element_type=F32)


def _seg_norm(x, seg):
    if seg == HEAD:
        r = lax.rsqrt(_seg_sum64(x * x) * (1.0 / HEAD) + EPS)
        return x * r, r
    w = x.shape[1]
    xh, rr = [], []
    for s in range(w // seg):
        xs = x[:, s * seg:(s + 1) * seg]
        r = lax.rsqrt(jnp.mean(xs * xs, axis=-1, keepdims=True) + EPS)
        xh.append(xs * r)
        rr.append(jnp.broadcast_to(r, xs.shape))
    return jnp.concatenate(xh, axis=1), jnp.concatenate(rr, axis=1)


def _seg_mean(v, seg):
    if seg == HEAD:
        return _seg_sum64(v) * (1.0 / HEAD)
    w = v.shape[1]
    out = []
    for s in range(w // seg):
        vs = v[:, s * seg:(s + 1) * seg]
        out.append(jnp.broadcast_to(jnp.mean(vs, axis=-1, keepdims=True), vs.shape))
    return jnp.concatenate(out, axis=1)


def _rope(t, c, sa, sb):
    out = []
    for cb in range(t.shape[1] // 128):
        tc = t[:, cb * 128:(cb + 1) * 128]
        out.append(tc * c + pltpu.roll(tc, 120, 1) * sa + pltpu.roll(tc, 8, 1) * sb)
    return jnp.concatenate(out, axis=1) if len(out) > 1 else out[0]


def _rope_bwd(dy, c, sa, sb):
    out = []
    for cb in range(dy.shape[1] // 128):
        dc = dy[:, cb * 128:(cb + 1) * 128]
        out.append(dc * c + pltpu.roll(dc * sa, 8, 1) + pltpu.roll(dc * sb, 120, 1))
    return jnp.concatenate(out, axis=1) if len(out) > 1 else out[0]


def _rope_consts():
    half = ROPE_ROT // 2
    c = np.float32(-2.0 * math.log(ROPE_THETA) / ROPE_ROT)
    freqs = np.exp(np.arange(half, dtype=np.float32) * c).astype(np.float32)
    place = np.zeros((3, half, 128), np.float32)
    ones = np.zeros((1, 128), np.float32)
    for lane in range(128):
        d = lane % HEAD
        if d < half:
            place[0, d, lane], place[1, d, lane] = 1.0, -1.0
        elif d < ROPE_ROT:
            place[0, d - half, lane], place[2, d - half, lane] = 1.0, 1.0
        else:
            ones[0, lane] = 1.0
    return np.tile(freqs[:, None], (1, 128)), place, ones


def _rope_tables(pos_rows):
    r = pos_rows.shape[0]
    tr = min(1024, r)
    freqs, place, ones = _rope_consts()

    def split3(v):
        hi, mid = _split_hi_lo(v)
        lo = (v - hi.astype(F32) - mid.astype(F32)).astype(_MM)
        return hi, mid, lo

    def body(p_ref, f_ref, e_ref, one_ref, c_ref, sa_ref, sb_ref):
        for j in range(tr // 128):
            ang = p_ref[j:j + 1, :].astype(F32) * f_ref[...]
            rows = slice(j * 128, (j + 1) * 128)
            for ref, k, v in ((c_ref, 0, jnp.cos(ang)), (sa_ref, 1, jnp.sin(ang)), (sb_ref, 2, jnp.sin(ang))):
                e = e_ref[k].astype(_MM)
                out = sum(_dot_tn(part, e) for part in split3(v))
                ref[rows, :] = out + one_ref[...] if k == 0 else out

    return _pc(body, name="rope_tables", grid=(r // tr,),
               in_specs=[pl.BlockSpec((tr // 128, 128), lambda i: (i, 0)), _acc((ROPE_ROT // 2, 128)),
                         _acc((3, ROPE_ROT // 2, 128)), _acc((1, 128))],
               out_specs=[_row(tr, 128)] * 3, out_shape=[_sds((r, 128))] * 3)(
                   pos_rows.reshape(r // 128, 128), jnp.asarray(freqs), jnp.asarray(place), jnp.asarray(ones))


def _k_in(x, g1, w_in):
    s = x.shape[0]
    ts = min(512, s)
    nin = w_in.shape[2]
    ncol = CHIPS * nin
    a_cols = 3 * A_W
    offs = [0, a_cols, 2 * a_cols, 3 * a_cols, 3 * a_cols + B_QH * HEAD,
            3 * a_cols + (B_QH + B_KVH) * HEAD, 3 * a_cols + (B_QH + 2 * B_KVH) * HEAD, ncol]

    def body(x_ref, g_ref, wi_ref, h_ref, a0, a1, a2, qb, kb, vb, mq, p_scr):
        xh, _ = _rms(x_ref[...])
        h = (xh * g_ref[...]).astype(_MM)
        h_ref[...] = h
        for j in range(CHIPS):
            p_scr[:, j * nin:(j + 1) * nin] = jnp.dot(h, wi_ref[j], preferred_element_type=F32)
        for k, ref in enumerate((a0, a1, a2, qb, kb, vb, mq)):
            ref[...] = p_scr[:, offs[k]:offs[k + 1]]

    widths = [offs[k + 1] - offs[k] for k in range(7)]
    return _pc(
        body, name="in_proj", grid=(s // ts,),
        in_specs=[_row(ts, D_MODEL), _res((1, D_MODEL)), _res(w_in.shape)],
        out_specs=[_row(ts, D_MODEL)] + [_row(ts, w) for w in widths],
        out_shape=[_sds((s, D_MODEL), _MM)] + [_sds((s, w)) for w in widths],
        scratch=[pltpu.VMEM((ts, ncol), F32)])(x, g1, w_in)


def _k_gate(h, w_gate, b_gate):
    s = h.shape[0]
    ts = min(256, s)
    ng = w_gate.shape[2]

    def body(h_ref, wg_ref, bg_ref, gt_ref):
        h = h_ref[...]
        for j in range(CHIPS):
            z = jnp.dot(h, wg_ref[j], preferred_element_type=F32) + bg_ref[:, j * ng:(j + 1) * ng]
            gt_ref[:, j * ng:(j + 1) * ng] = _sigmoid(z)

    return _pc(body, name="gate_proj", grid=(s // ts,),
               in_specs=[_row(ts, D_MODEL), _res(w_gate.shape), _res(b_gate.shape)],
               out_specs=[_row(ts, CHIPS * ng)], out_shape=[_sds((s, CHIPS * ng))])(h, w_gate, b_gate)[0]


def _k_prep(srcs, gq, gk, tabs, tab_row, *, wq, wk, rows_per_gain, name):
    rows = srcs[0][0].shape[0]
    ts = min(512, rows)

    def body(q_ref, k_ref, v_ref, gq_ref, gk_ref, c_ref, sa_ref, sb_ref, qn_ref, kn_ref, vn_ref):
        c, sa, sb = c_ref[...], sa_ref[...], sb_ref[...]
        qh, _ = _seg_norm(q_ref[...], HEAD)
        qn_ref[...] = _rope(qh * gq_ref[...], c, sa, sb).astype(_MM)
        kh, _ = _seg_norm(k_ref[...], HEAD)
        kn_ref[...] = _rope(kh * gk_ref[...], c, sa, sb).astype(_MM)
        vn_ref[...] = v_ref[...].astype(_MM)

    gspec = lambda w: pl.BlockSpec((None, 1, w), lambda i: ((i * ts) // rows_per_gain, 0, 0))
    return _pc(
        body, name=name, grid=(rows // ts,),
        in_specs=[_row(ts, wq, srcs[0][1]), _row(ts, wk, srcs[1][1]), _row(ts, wk, srcs[2][1]),
                  gspec(wq), gspec(wk)] + [pl.BlockSpec((ts, 128), lambda i: (i + tab_row // ts, 0))] * 3,
        out_specs=[_row(ts, wq), _row(ts, wk), _row(ts, wk)],
        out_shape=[_sds((rows, wq), _MM), _sds((rows, wk), _MM), _sds((rows, wk), _MM)])(
            srcs[0][0], srcs[1][0], srcs[2][0], gq, gk, *tabs)


def _first_flag(b, segs, nb):
    first = b >= nb
    for k, (start, period) in enumerate(segs):
        end = segs[k + 1][0] if k + 1 < len(segs) else nb
        first = first | ((b >= start) & (b < end) & (lax.rem(b - start, jnp.int32(period)) == 0))
    return first


def _band_bias(thr, with_cur):
    qi = lax.broadcasted_iota(jnp.int32, (BLK, BLK), 0)
    kj = lax.broadcasted_iota(jnp.int32, (BLK, BLK), 1)
    prev = jnp.where(kj >= qi + thr, 0.0, NEG)
    return jnp.concatenate([prev, jnp.where(kj <= qi, 0.0, NEG)], axis=1) if with_cur else prev


def _blockdiag(t4):
    head = _lane_head((1, A_W))
    return jnp.concatenate([t4 * jnp.where(head == h, 1.0, 0.0).astype(t4.dtype) for h in range(A_HEADS)], axis=0)


def _fold_diag(t, n):
    head = _lane_head((n, A_W))
    out = t[3 * n:4 * n]
    for h in (2, 1, 0):
        out = jnp.where(head == h, t[h * n:(h + 1) * n], out)
    return out


def _expand_heads(cols):
    n = cols[0].shape[0]
    head = _lane_head((n, A_W))
    out = jnp.broadcast_to(cols[3], (n, A_W))
    for h in (2, 1, 0):
        out = jnp.where(head == h, cols[h], out)
    return out


def _unit_kv(pieces, u, shared):
    cols = slice(u * HEAD, (u + 1) * HEAD) if shared else slice(u * A_W, (u + 1) * A_W)
    rows = [ref[rs, cols] for ref, rs in pieces]
    k = rows[0] if len(rows) == 1 else jnp.concatenate(rows, axis=0)
    return jnp.concatenate([k] * A_HEADS, axis=1) if shared else k


_LO, _HI, _BOTH = slice(0, BLK), slice(BLK, 2 * BLK), slice(0, 2 * BLK)


def _k_band_fwd(qn, kn, vn, *, hq, hk, max_dist, segs, sink, name):
    rows = qn.shape[0]
    nb = rows // BLK
    units = hq // A_HEADS
    shared = hk != hq
    wq, wk = hq * HEAD, hk * HEAD
    scale = HEAD ** -0.5

    def body(*refs):
        if sink is None:
            q_ref, kc_ref, kp_ref, vc_ref, vp_ref, o_ref, l_ref = refs
        else:
            q_ref, kc_ref, kp_ref, vc_ref, vp_ref, sk_ref, o_ref, l_ref = refs
        i = pl.program_id(0)
        for half, rs in enumerate((_LO, _HI)):
            bias = _band_bias(jnp.where(_first_flag(2 * i + half, segs, nb), 1 << 20, BLK - max_dist), True)
            kpieces = ((kp_ref, _LO), (kc_ref, _LO)) if half == 0 else ((kc_ref, _BOTH),)
            vpieces = ((vp_ref, _LO), (vc_ref, _LO)) if half == 0 else ((vc_ref, _BOTH),)
            for u in range(units):
                us = slice(u * A_W, (u + 1) * A_W)
                kb = _blockdiag(_unit_kv(kpieces, u, shared))
                vb = _blockdiag(_unit_kv(vpieces, u, shared))
                s_all = _dot_nt(q_ref[rs, us], kb) * scale
                ps, ls = [], []
                for h in range(A_HEADS):
                    s = s_all[:, h * 2 * BLK:(h + 1) * 2 * BLK] + bias
                    m = jnp.max(s, axis=-1, keepdims=True)
                    e = jnp.exp(s - m)
                    lse = m + jnp.log(jnp.sum(e, axis=-1, keepdims=True))
                    if sink is not None:
                        sk = sk_ref[u * A_HEADS + h]
                        mx = jnp.maximum(lse, sk)
                        lse = mx + jnp.log(jnp.exp(lse - mx) + jnp.exp(sk - mx))
                    ps.append((e * jnp.exp(m - lse)).astype(_MM))
                    ls.append(lse)
                o_ref[rs, us] = _dot(jnp.concatenate(ps, axis=1), vb)
                l_ref[rs, us] = _expand_heads(ls)

    two = lambda w: pl.BlockSpec((2 * BLK, w), lambda i: (i, 0))
    prev = lambda w: pl.BlockSpec((BLK, w), lambda i: (jnp.maximum(2 * i - 1, 0), 0))
    in_specs = [two(wq), two(wk), prev(wk), two(wk), prev(wk)]
    args = [qn, kn, kn, vn, vn]
    if sink is not None:
        in_specs.append(pl.BlockSpec(memory_space=pltpu.SMEM))
        args.append(sink)
    return _pc(body, name=name, grid=(nb // 2,), in_specs=in_specs, out_specs=[two(wq), two(wq)],
               out_shape=[_sds((rows, wq)), _sds((rows, wq))])(*args)


def _k_memkv(mem, mem_norm, w_kv, m_k_norm):
    n = mem.shape[0]

    def body(m_ref, g_ref, w_ref, gk_ref, mn_ref, kv_ref, mk_ref, mv_ref):
        mh, _ = _rms(m_ref[...])
        mn = (mh * g_ref[...]).astype(_MM)
        mn_ref[...] = mn
        kv = jnp.dot(mn, w_ref[...], preferred_element_type=F32)
        kv_ref[...] = kv
        kh, _ = _seg_norm(kv[:, :M_W], M_HD)
        mk_ref[...] = (kh * gk_ref[...]).astype(_MM)
        mv_ref[...] = kv[:, M_W:].astype(_MM)

    return _pc(body, name="mem_kv", grid=(1,),
               in_specs=[_acc((n, D_MODEL)), _acc((1, D_MODEL)), _acc(w_kv.shape), _acc((1, M_W))],
               out_specs=[_acc((n, D_MODEL)), _acc((n, 2 * M_W)), _acc((n, M_W)), _acc((n, M_W))],
               out_shape=[_sds((n, D_MODEL), _MM), _sds((n, 2 * M_W)), _sds((n, M_W), _MM), _sds((n, M_W), _MM)])(
                   mem, mem_norm, w_kv, m_k_norm)


def _mem_probs(q, mk):
    sc = _dot_nt(q, mk) * (M_HD ** -0.5)
    e = jnp.exp(sc - jnp.max(sc, axis=-1, keepdims=True))
    return e / jnp.sum(e, axis=-1, keepdims=True)


def _k_mem_fwd(m_q, gq, mk, mv):
    s = m_q.shape[0]
    n = mk.shape[0]
    ts = min(512, s)

    def body(q_ref, g_ref, mk_ref, mv_ref, o_ref):
        qh, _ = _seg_norm(q_ref[...], M_HD)
        qn = (qh * g_ref[...]).astype(_MM)
        for h in range(M_HEADS):
            hs = slice(h * M_HD, (h + 1) * M_HD)
            o_ref[:, hs] = _dot(_mem_probs(qn[:, hs], mk_ref[:, hs]), mv_ref[:, hs])

    return _pc(body, name="mem_attn", grid=(s // ts,),
               in_specs=[_row(ts, M_W), _res((1, M_W)), _res((n, M_W)), _res((n, M_W))],
               out_specs=[_row(ts, M_W)], out_shape=[_sds((s, M_W))])(m_q, gq, mk, mv)[0]


def _group_weights(l0, l1, l2):
    m = jnp.maximum(jnp.maximum(l0, l1), l2)
    e0, e1, e2 = jnp.exp(l0 - m), jnp.exp(l1 - m), jnp.exp(l2 - m)
    inv = 1.0 / (e0 + e1 + e2)
    return e0 * inv, e1 * inv, e2 * inv


def _branch_products(oa, ob, om, woa_ref, wob_ref, wom_ref, j):
    return _dot(oa, woa_ref[j]), _dot(ob, wob_ref[j]), _dot(om, wom_ref[j])


def _k_merge(og, lg, o_b, o_m, gates, x, w_oa, w_ob, w_om, w_out, g2):
    s = x.shape[0]
    ts = min(256, s)
    nc = w_oa.shape[2]

    def body(o0, o1, o2, l0, l1, l2, ob_ref, om_ref, gt_ref, x_ref, woa, wob, wom, wout, g_ref,
             oa_ref, mer_ref, x1_ref, h2_ref, m_scr):
        w0, w1, w2 = _group_weights(l0[...], l1[...], l2[...])
        oa = w0 * o0[...] + w1 * o1[...] + w2 * o2[...]
        oa_ref[...] = oa
        ob, om = ob_ref[...], om_ref[...]
        for j in range(CHIPS):
            pa, pb, pm = _branch_products(oa, ob, om, woa, wob, wom, j)
            cs = lambda br: slice(br * D_MODEL + j * nc, br * D_MODEL + (j + 1) * nc)
            m_scr[:, j * nc:(j + 1) * nc] = gt_ref[:, cs(0)] * pa + gt_ref[:, cs(1)] * pb + gt_ref[:, cs(2)] * pm
        mer = m_scr[...].astype(_MM)
        mer_ref[...] = mer
        x1 = x_ref[...] + jnp.dot(mer, wout[...], preferred_element_type=F32)
        x1_ref[...] = x1
        xh, _ = _rms(x1)
        h2_ref[...] = (xh * g_ref[...]).astype(_MM)

    return _pc(
        body, name="merge_out", grid=(s // ts,),
        in_specs=[_row(ts, A_W)] * 6 + [_row(ts, B_QH * HEAD), _row(ts, M_W), _row(ts, 3 * D_MODEL), _row(ts, D_MODEL),
                                         _res(w_oa.shape), _res(w_ob.shape), _res(w_om.shape), _res(w_out.shape),
                                         _res((1, D_MODEL))],
        out_specs=[_row(ts, A_W), _row(ts, D_MODEL), _row(ts, D_MODEL), _row(ts, D_MODEL)],
        out_shape=[_sds((s, A_W)), _sds((s, D_MODEL), _MM), _sds((s, D_MODEL)), _sds((s, D_MODEL), _MM)],
        scratch=[pltpu.VMEM((ts, D_MODEL), F32)])(*og, *lg, o_b, o_m, gates, x, w_oa, w_ob, w_om, w_out, g2)


def _k_up(h2, w_up):
    s = h2.shape[0]
    ts = min(256, s)
    nu = w_up.shape[2]

    def body(h_ref, w_ref, u_ref):
        h = h_ref[...]
        for j in range(CHIPS):
            u_ref[:, j * nu:(j + 1) * nu] = jnp.dot(h, w_ref[j], preferred_element_type=F32)

    return _pc(body, name="up_proj", grid=(s // ts,), in_specs=[_row(ts, D_MODEL), _res(w_up.shape)],
               out_specs=[_row(ts, CHIPS * nu)], out_shape=[_sds((s, CHIPS * nu))])(h2, w_up)[0]


def _shift_down(v, halo, k):
    rolled = pltpu.roll(v, k, 0)
    row = lax.broadcasted_iota(jnp.int32, (8, v.shape[1]), 0)
    slab = rolled[0:8]
    for r in range(k):
        slab = jnp.where(row == r, halo[8 - k + r:8 - k + r + 1, :], slab)
    return jnp.concatenate([slab, rolled[8:]], axis=0)


def _shift_up(v, halo, k):
    ts = v.shape[0]
    rolled = pltpu.roll(v, ts - k, 0)
    row = lax.broadcasted_iota(jnp.int32, (8, v.shape[1]), 0)
    slab = rolled[ts - 8:]
    for r in range(k):
        slab = jnp.where(row == 8 - k + r, halo[r:r + 1, :], slab)
    return jnp.concatenate([rolled[:ts - 8], slab], axis=0)


def _k_ffn(u, conv_w, conv_b, w_down, w_down_t, x1, target):
    s = u.shape[0]
    ts = min(256, s)
    nu = conv_w.shape[2]
    half = CHIPS // 2

    def body(u_ref, uh_ref, cw_ref, cb_ref, wd_ref, wdt_ref, x1_ref, t_ref, dy_ref, f_ref, dc_ref, loss_ref, c_scr,
             f_scr, s_scr):
        i = pl.program_id(0)
        halo = jnp.where(i > 0, uh_ref[...], 0.0)
        for j in range(CHIPS):
            cs = slice(j * nu, (j + 1) * nu)
            uj = u_ref[:, cs]
            hj = halo[:, cs]
            c_scr[:, cs] = (cb_ref[:, cs] + cw_ref[j, 0:1, :] * _shift_down(uj, hj, 2)
                            + cw_ref[j, 1:2, :] * _shift_down(uj, hj, 1) + cw_ref[j, 2:3, :] * uj)
        for j in range(half):
            a = c_scr[:, j * nu:(j + 1) * nu]
            g = c_scr[:, (half + j) * nu:(half + j + 1) * nu]
            sa = _sigmoid(a)
            s_scr[:, j * nu:(j + 1) * nu] = sa
            f_scr[:, j * nu:(j + 1) * nu] = (a * sa * g).astype(_MM)
        f = f_scr[...]
        f_ref[...] = f
        y = x1_ref[...] + jnp.dot(f, wd_ref[...], preferred_element_type=F32)
        err = y - t_ref[...]
        dy = err * (1.0 / D_MODEL)
        dy_ref[...] = dy

        @pl.when(i == 0)
        def _():
            loss_ref[...] = jnp.zeros_like(loss_ref)

        loss_ref[...] += _sum8(err * err)
        df = _dot(dy, wdt_ref[...])
        for j in range(half):
            a = c_scr[:, j * nu:(j + 1) * nu]
            g = c_scr[:, (half + j) * nu:(half + j + 1) * nu]
            sa = s_scr[:, j * nu:(j + 1) * nu]
            dfj = df[:, j * nu:(j + 1) * nu]
            dc_ref[:, j * nu:(j + 1) * nu] = dfj * g * (sa * (1.0 + a * (1.0 - sa)))
            dc_ref[:, (half + j) * nu:(half + j + 1) * nu] = dfj * (a * sa)

    wide = CHIPS * nu
    return _pc(
        body, name="conv_ffn", grid=(s // ts,),
        in_specs=[_row(ts, wide), pl.BlockSpec((8, wide), lambda i: (jnp.maximum(i * (ts // 8) - 1, 0), 0)),
                  _res(conv_w.shape), _res((1, wide)), _res(w_down.shape), _res(w_down_t.shape), _row(ts, D_MODEL),
                  _row(ts, D_MODEL)],
        out_specs=[_row(ts, D_MODEL), _row(ts, D_FF), _row(ts, wide), _acc((8, D_MODEL))],
        out_shape=[_sds((s, D_MODEL)), _sds((s, D_FF), _MM), _sds((s, wide)), _sds((8, D_MODEL))],
        scratch=[pltpu.VMEM((ts, wide), F32), pltpu.VMEM((ts, D_FF), _MM), pltpu.VMEM((ts, D_FF), F32)])(
            u, u, conv_w, conv_b, w_down, w_down_t, x1, target)


def _k_conv_bwd(dc, u, conv_w, w_up, x1, g2, dy):
    s = u.shape[0]
    ts = min(256, s)
    nu = conv_w.shape[2]
    wide = CHIPS * nu
    last = s // ts - 1

    def body(dc_ref, dn_ref, u_ref, cw_ref, wu_ref, x1_ref, g_ref, dy_ref, dx1_ref, du_ref, cacc_ref, gacc_ref):
        i = pl.program_id(0)

        @pl.when(i == 0)
        def _():
            cacc_ref[...] = jnp.zeros_like(cacc_ref)
            gacc_ref[...] = jnp.zeros_like(gacc_ref)

        dhalo = jnp.where(i < last, dn_ref[...], 0.0)
        dh2 = jnp.zeros((ts, D_MODEL), F32)
        for j in range(CHIPS):
            cs = slice(j * nu, (j + 1) * nu)
            dcj, uj = dc_ref[:, cs], u_ref[:, cs]
            dc1, dc2 = _shift_up(dcj, dhalo[:, cs], 1), _shift_up(dcj, dhalo[:, cs], 2)
            cacc_ref[0, :, cs] += _sum8(dcj)
            cacc_ref[1, :, cs] += _sum8(dc2 * uj)
            cacc_ref[2, :, cs] += _sum8(dc1 * uj)
            cacc_ref[3, :, cs] += _sum8(dcj * uj)
            du = (cw_ref[j, 2:3, :] * dcj + cw_ref[j, 1:2, :] * dc1 + cw_ref[j, 0:1, :] * dc2).astype(_MM)
            du_ref[:, cs] = du
            dh2 = dh2 + _dot_nt(du, wu_ref[j])
        xh, r = _rms(x1_ref[...])
        gacc_ref[...] += _sum8(dh2 * xh)
        dx1_ref[...] = dy_ref[...] + _rms_bwd(dh2, xh, r, g_ref[...])

    return _pc(
        body, name="conv_up_bwd", grid=(s // ts,),
        in_specs=[_row(ts, wide),
                  pl.BlockSpec((8, wide), lambda i: (jnp.minimum((i + 1) * (ts // 8), s // 8 - 1), 0)),
                  _row(ts, wide), _res(conv_w.shape), _res(w_up.shape), _row(ts, D_MODEL), _res((1, D_MODEL)),
                  _row(ts, D_MODEL)],
        out_specs=[_row(ts, D_MODEL), _row(ts, wide), _acc((4, 8, wide)), _acc((8, D_MODEL))],
        out_shape=[_sds((s, D_MODEL)), _sds((s, wide), _MM), _sds((4, 8, wide)), _sds((8, D_MODEL))])(
            dc, dc, u, conv_w, w_up, x1, g2, dy)


def _k_merge_bwd(dx1, og, lg, o_a, o_b, o_m, gates, w_oa, w_ob, w_om, w_out, dep):
    s = dx1.shape[0]
    ts = min(256, s)
    nc = w_oa.shape[2]

    def body(dx_ref, o0, o1, o2, l0, l1, l2, oa_ref, ob_ref, om_ref, gt_ref, woa, wob, wom, wout, dep_ref,
             dgp_ref, dpa_ref, dpb_ref, dpm_ref, dog0, dog1, dog2, dl0, dl1, dl2, dob_ref, dom_ref, bacc_ref):
        i = pl.program_id(0)

        @pl.when(i == 0)
        def _():
            bacc_ref[...] = jnp.zeros_like(bacc_ref)

        dmer = _dot_nt(dx_ref[...], wout[...])
        oa, ob, om = oa_ref[...], ob_ref[...], om_ref[...]
        doa = jnp.zeros((ts, A_W), F32)
        dob = jnp.zeros((ts, B_QH * HEAD), F32)
        dom = jnp.zeros((ts, M_W), F32)
        for j in range(CHIPS):
            prods = _branch_products(oa, ob, om, woa, wob, wom, j)
            dmj = dmer[:, j * nc:(j + 1) * nc]
            dps = []
            for br, (p, dref) in enumerate(zip(prods, (dpa_ref, dpb_ref, dpm_ref))):
                cs = slice(br * D_MODEL + j * nc, br * D_MODEL + (j + 1) * nc)
                gt = gt_ref[:, cs]
                dgp = dmj * p * gt * (1.0 - gt)
                dgp_ref[:, cs] = dgp.astype(_MM)
                bacc_ref[:, cs] += _sum8(dgp)
                dp = (dmj * gt).astype(_MM)
                dref[:, j * nc:(j + 1) * nc] = dp
                dps.append(dp)
            doa = doa + _dot_nt(dps[0], woa[j])
            dob = dob + _dot_nt(dps[1], wob[j])
            dom = dom + _dot_nt(dps[2], wom[j])
        dob_ref[...] = dob
        dom_ref[...] = dom
        ws = _group_weights(l0[...], l1[...], l2[...])
        dsum = _seg_mean(doa * oa, HEAD) * float(HEAD)
        for w, dref, lref in zip(ws, (dog0, dog1, dog2), (dl0, dl1, dl2)):
            dref[...] = w * doa
            lref[...] = w * dsum

    return _pc(
        body, name="merge_out_bwd", grid=(s // ts,),
        in_specs=[_row(ts, D_MODEL)] + [_row(ts, A_W)] * 7 + [_row(ts, B_QH * HEAD), _row(ts, M_W), _row(ts, 3 * D_MODEL),
                                                              _res(w_oa.shape), _res(w_ob.shape), _res(w_om.shape),
                                                              _res(w_out.shape), _res((8, 128))],
        out_specs=[_row(ts, 3 * D_MODEL)] + [_row(ts, D_MODEL)] * 3 + [_row(ts, A_W)] * 6
        + [_row(ts, B_QH * HEAD), _row(ts, M_W), _acc((8, 3 * D_MODEL))],
        out_shape=[_sds((s, 3 * D_MODEL), _MM)] + [_sds((s, D_MODEL), _MM)] * 3 + [_sds((s, A_W))] * 6
        + [_sds((s, B_QH * HEAD)), _sds((s, M_W)), _sds((8, 3 * D_MODEL))])(
            dx1, *og, *lg, o_a, o_b, o_m, gates, w_oa, w_ob, w_om, w_out, dep)


def _k_mem_bwd(m_q, gq, mk, mv, o_m, do_m):
    s = m_q.shape[0]
    n = mk.shape[0]
    ts = min(512, s)
    scale = M_HD ** -0.5

    def body(q_ref, g_ref, mk_ref, mv_ref, o_ref, do_ref, dq_ref, dmk_ref, dmv_ref, gacc_ref):
        i = pl.program_id(0)

        @pl.when(i == 0)
        def _():
            dmk_ref[...] = jnp.zeros_like(dmk_ref)
            dmv_ref[...] = jnp.zeros_like(dmv_ref)
            gacc_ref[...] = jnp.zeros_like(gacc_ref)

        gain = g_ref[...]
        qh, r = _seg_norm(q_ref[...], M_HD)
        qn = (qh * gain).astype(_MM)
        do = do_ref[...]
        delta = _seg_mean(do * o_ref[...], M_HD) * float(M_HD)
        dqn = []
        for h in range(M_HEADS):
            hs = slice(h * M_HD, (h + 1) * M_HD)
            p = _mem_probs(qn[:, hs], mk_ref[:, hs])
            dp = _dot_nt(do[:, hs], mv_ref[:, hs])
            ds = (p * (dp - delta[:, hs][:, 0:1]) * scale).astype(_MM)
            dqn.append(_dot(ds, mk_ref[:, hs]))
            dmk_ref[:, hs] += _dot_tn(ds, qn[:, hs])
            dmv_ref[:, hs] += _dot_tn(p, do[:, hs])
        dqn = jnp.concatenate(dqn, axis=1)
        gacc_ref[...] += _sum8(dqn * qh)
        z = dqn * gain
        dq_ref[...] = (r * (z - qh * _seg_mean(z * qh, M_HD))).astype(_MM)

    return _pc(
        body, name="mem_attn_bwd", grid=(s // ts,),
        in_specs=[_row(ts, M_W), _res((1, M_W)), _res((n, M_W)), _res((n, M_W)), _row(ts, M_W), _row(ts, M_W)],
        out_specs=[_row(ts, M_W), _acc((n, M_W)), _acc((n, M_W)), _acc((8, M_W))],
        out_shape=[_sds((s, M_W), _MM), _sds((n, M_W)), _sds((n, M_W)), _sds((8, M_W))])(m_q, gq, mk, mv, o_m, do_m)


def _k_memkv_bwd(mem, mem_norm, w_kv, m_k_norm, mem_n, kv, dmk, dmv):
    n = mem.shape[0]

    def body(m_ref, g_ref, w_ref, gk_ref, mn_ref, kv_ref, dmk_ref, dmv_ref, dw_ref, dg_ref, dgk_ref):
        gk = gk_ref[...]
        kh, r = _seg_norm(kv_ref[:, :M_W], M_HD)
        dmk = dmk_ref[...]
        dgk_ref[...] = _sum8(dmk * kh)
        z = dmk * gk
        dk = r * (z - kh * _seg_mean(z * kh, M_HD))
        dkv = jnp.concatenate([dk, dmv_ref[...]], axis=1).astype(_MM)
        dw_ref[...] = _dot_tn(mn_ref[...], dkv)
        dmn = _dot_nt(dkv, w_ref[...])
        mh, _ = _rms(m_ref[...])
        dg_ref[...] = _sum8(dmn * mh)

    return _pc(body, name="mem_kv_bwd", grid=(1,),
               in_specs=[_acc((n, D_MODEL)), _acc((1, D_MODEL)), _acc(w_kv.shape), _acc((1, M_W)), _acc((n, D_MODEL)),
                         _acc((n, 2 * M_W)), _acc((n, M_W)), _acc((n, M_W))],
               out_specs=[_acc(w_kv.shape), _acc((8, D_MODEL)), _acc((8, M_W))],
               out_shape=[_sds(w_kv.shape), _sds((8, D_MODEL)), _sds((8, M_W))])(
                   mem, mem_norm, w_kv, m_k_norm, mem_n, kv, dmk, dmv)


def _k_band_bwd(qn, kn, vn, do, lse, dl_or_o, *, hq, hk, max_dist, segs, sink, name):
    rows = qn.shape[0]
    nb = rows // BLK
    units = hq // A_HEADS
    shared = hk != hq
    wq, wk = hq * HEAD, hk * HEAD
    scale = HEAD ** -0.5

    def body(*refs):
        (q2_ref, qx_ref, kc_ref, kp_ref, vc_ref, vp_ref, do2_ref, dox_ref, l2_ref, lx_ref, e2_ref, ex_ref) = refs[:12]
        if sink is None:
            dq_ref, dk_ref, dv_ref = refs[12:]
        else:
            sk_ref, dq_ref, dk_ref, dv_ref, sacc_ref = refs[12:]
        i = pl.program_id(0)
        thr = lambda b: jnp.where(_first_flag(b, segs, nb), 1 << 20, BLK - max_dist)
        bias_a, bias_b = _band_bias(thr(2 * i), True), _band_bias(thr(2 * i + 1), True)
        bias_c = _band_bias(thr(2 * i + 2), False)
        if sink is not None:
            @pl.when(i == 0)
            def _():
                sacc_ref[...] = jnp.zeros_like(sacc_ref)

        def tile(q4, do4, l_cols, dlt, kd, vd, bias, width):
            s, dp = _dot_nt(q4, kd) * scale, _dot_nt(do4, vd)
            ps, dss = [], []
            for h in range(A_HEADS):
                seg = slice(h * width, (h + 1) * width)
                p = jnp.exp(s[:, seg] + bias - l_cols[h])
                ps.append(p)
                dss.append(p * (dp[:, seg] - dlt[:, h * HEAD:h * HEAD + 1]) * scale)
            return ps, dss

        cat = lambda parts: jnp.concatenate([t.astype(_MM) for t in parts], axis=1)
        for u in range(units):
            us = slice(u * A_W, (u + 1) * A_W)
            k_a = _unit_kv(((kp_ref, _LO), (kc_ref, _LO)), u, shared)
            v_a = _unit_kv(((vp_ref, _LO), (vc_ref, _LO)), u, shared)
            k_b, v_b = _unit_kv(((kc_ref, _BOTH),), u, shared), _unit_kv(((vc_ref, _BOTH),), u, shared)
            kd_a, vd_a, kd_b, vd_b = _blockdiag(k_a), _blockdiag(v_a), _blockdiag(k_b), _blockdiag(v_b)
            kd_c, vd_c = _blockdiag(k_b[BLK:]), _blockdiag(v_b[BLK:])
            qs = (q2_ref[_LO, us], q2_ref[_HI, us], qx_ref[:, us])
            dos = (do2_ref[_LO, us], do2_ref[_HI, us], dox_ref[:, us])
            lcols = [[ref[rs, u * A_W + h * HEAD:u * A_W + h * HEAD + 1] for h in range(A_HEADS)]
                     for ref, rs in ((l2_ref, _LO), (l2_ref, _HI), (lx_ref, _LO))]
            if sink is None:
                dlts = (e2_ref[_LO, us], e2_ref[_HI, us], ex_ref[:, us])
            else:
                dlts = tuple(_seg_sum64(d.astype(F32) * ref[rs, us])
                             for d, (ref, rs) in zip(dos, ((e2_ref, _LO), (e2_ref, _HI), (ex_ref, _LO))))
                for t in range(2):
                    for h in range(A_HEADS):
                        j = u * A_HEADS + h
                        sacc_ref[:, j:j + 1] += -jnp.exp(sk_ref[j] - lcols[t][h]) * dlts[t][:, h * HEAD:h * HEAD + 1]
            p_a, ds_a = tile(qs[0], dos[0], lcols[0], dlts[0], kd_a, vd_a, bias_a, 2 * BLK)
            p_b, ds_b = tile(qs[1], dos[1], lcols[1], dlts[1], kd_b, vd_b, bias_b, 2 * BLK)
            p_c, ds_c = tile(qs[2], dos[2], lcols[2], dlts[2], kd_c, vd_c, bias_c, BLK)
            dq_ref[_LO, us] = _dot(cat(ds_a), kd_a)
            dq_ref[_HI, us] = _dot(cat(ds_b), kd_b)
            outs = []
            for pa, pb, pc, lhs in ((ds_a, ds_b, ds_c, qs), (p_a, p_b, p_c, dos)):
                from_a = _fold_diag(_dot_tn(cat([t[:, BLK:] for t in pa]), lhs[0]), BLK)
                from_b = _fold_diag(_dot_tn(cat(pb), lhs[1]), 2 * BLK)
                from_c = _fold_diag(_dot_tn(cat(pc), lhs[2]), BLK)
                outs.append(jnp.concatenate([from_a + from_b[:BLK], from_b[BLK:] + from_c], axis=0))
            dk4, dv4 = outs
            if shared:
                fold = lambda t: (t[:, 0:HEAD] + t[:, HEAD:2 * HEAD]) + (t[:, 2 * HEAD:3 * HEAD] + t[:, 3 * HEAD:])
                dk_ref[:, u * HEAD:(u + 1) * HEAD] = fold(dk4)
                dv_ref[:, u * HEAD:(u + 1) * HEAD] = fold(dv4).astype(_MM)
            else:
                dk_ref[:, us] = dk4
                dv_ref[:, us] = dv4.astype(_MM)

    two = lambda w: pl.BlockSpec((2 * BLK, w), lambda i: (i, 0))
    prev = lambda w: pl.BlockSpec((BLK, w), lambda i: (jnp.maximum(2 * i - 1, 0), 0))
    nxt = lambda w: pl.BlockSpec((BLK, w), lambda i: (jnp.minimum(2 * i + 2, nb - 1), 0))
    in_specs = [two(wq), nxt(wq), two(wk), prev(wk), two(wk), prev(wk), two(wq), nxt(wq), two(wq), nxt(wq), two(wq), nxt(wq)]
    args = [qn, qn, kn, kn, vn, vn, do, do, lse, lse, dl_or_o, dl_or_o]
    out_specs = [two(wq), two(wk), two(wk)]
    out_shape = [_sds((rows, wq)), _sds((rows, wk)), _sds((rows, wk), _MM)]
    if sink is not None:
        in_specs.append(pl.BlockSpec(memory_space=pltpu.SMEM))
        args.append(sink)
        out_specs.append(_acc((BLK, 128)))
        out_shape.append(_sds((BLK, 128)))
    return _pc(body, name=name, grid=(nb // 2,), in_specs=in_specs, out_specs=out_specs, out_shape=out_shape)(*args)


def _k_prep_bwd(srcs, dqn, dkn, gq, gk, tabs, tab_row, *, wq, wk, rows_per_gain, name):
    rows = dqn.shape[0]
    ts = min(512, rows)
    ngain = gq.shape[0]

    def body(q_ref, k_ref, dq_ref, dk_ref, gq_ref, gk_ref, c_ref, sa_ref, sb_ref, oq_ref, ok_ref, aq_ref, ak_ref):
        i = pl.program_id(0)

        @pl.when(lax.rem(i * ts, rows_per_gain) == 0)
        def _():
            aq_ref[...] = jnp.zeros_like(aq_ref)
            ak_ref[...] = jnp.zeros_like(ak_ref)

        c, sa, sb = c_ref[...], sa_ref[...], sb_ref[...]
        for x_ref, d_ref, g_ref, o_ref, a_ref in ((q_ref, dq_ref, gq_ref, oq_ref, aq_ref),
                                                   (k_ref, dk_ref, gk_ref, ok_ref, ak_ref)):
            xh, r = _seg_norm(x_ref[...], HEAD)
            dt = _rope_bwd(d_ref[...], c, sa, sb)
            a_ref[...] += _sum8(dt * xh)
            z = dt * g_ref[...]
            o_ref[...] = (r * (z - xh * _seg_mean(z * xh, HEAD))).astype(_MM)

    gspec = lambda w: pl.BlockSpec((None, 1, w), lambda i: ((i * ts) // rows_per_gain, 0, 0))
    aspec = lambda w: pl.BlockSpec((None, 8, w), lambda i: ((i * ts) // rows_per_gain, 0, 0))
    return _pc(
        body, name=name, grid=(rows // ts,),
        in_specs=[_row(ts, wq, srcs[0][1]), _row(ts, wk, srcs[1][1]), _row(ts, wq), _row(ts, wk), gspec(wq), gspec(wk)]
        + [pl.BlockSpec((ts, 128), lambda i: (i + tab_row // ts, 0))] * 3,
        out_specs=[_row(ts, wq), _row(ts, wk), aspec(wq), aspec(wk)],
        out_shape=[_sds((rows, wq), _MM), _sds((rows, wk), _MM), _sds((ngain, 8, wq)), _sds((ngain, 8, wk))])(
            srcs[0][0], srcs[1][0], dqn, dkn, gq, gk, *tabs)


def _k_in_bwd(pieces, dgp, x, g1, dx1, w_in, w_gate):
    s = x.shape[0]
    ts = min(256, s)
    nin, ng = w_in.shape[2], w_gate.shape[2]
    widths = [p.shape[1] for p in pieces]
    ncol = sum(widths)

    def body(*refs):
        p_refs = refs[:len(pieces)]
        dgp_ref, x_ref, g_ref, dx1_ref, wi_ref, wg_ref, gx_ref, dpj_ref, gacc_ref = refs[len(pieces):]
        i = pl.program_id(0)

        @pl.when(i == 0)
        def _():
            gacc_ref[...] = jnp.zeros_like(gacc_ref)

        off = 0
        for p_ref, w in zip(p_refs, widths):
            dpj_ref[:, off:off + w] = p_ref[...]
            off += w
        dh = jnp.zeros((ts, D_MODEL), F32)
        for j in range(CHIPS):
            dh = dh + _dot_nt(dpj_ref[:, j * nin:(j + 1) * nin], wi_ref[j])
            dh = dh + _dot_nt(dgp_ref[:, j * ng:(j + 1) * ng], wg_ref[j])
        xh, r = _rms(x_ref[...])
        gacc_ref[...] += _sum8(dh * xh)
        gx_ref[...] = dx1_ref[...] + _rms_bwd(dh, xh, r, g_ref[...])

    return _pc(
        body, name="in_proj_bwd", grid=(s // ts,),
        in_specs=[_row(ts, w) for w in widths] + [_row(ts, CHIPS * ng), _row(ts, D_MODEL), _res((1, D_MODEL)),
                                                  _row(ts, D_MODEL), _res(w_in.shape), _res(w_gate.shape)],
        out_specs=[_row(ts, D_MODEL), _row(ts, ncol), _acc((8, D_MODEL))],
        out_shape=[_sds((s, D_MODEL)), _sds((s, ncol), _MM), _sds((8, D_MODEL))])(*pieces, dgp, x, g1, dx1, w_in, w_gate)


def _k_wgrad(a, b, *, nblk, stacked, name):
    s, k = a.shape
    n = b.shape[1]
    nb = n // nblk
    ts = min(2048 if k <= 1024 else 1024, s)

    def body(a_ref, b_ref, o_ref):
        @pl.when(pl.program_id(1) == 0)
        def _():
            o_ref[...] = jnp.zeros_like(o_ref)

        o_ref[...] += _dot_tn(a_ref[...], b_ref[...])

    if stacked:
        out_spec, out_shape = pl.BlockSpec((None, k, nb), lambda g, t: (g, 0, 0)), _sds((nblk, k, nb))
    else:
        out_spec, out_shape = pl.BlockSpec((k, nb), lambda g, t: (0, g)), _sds((k, n))
    return _pc(body, name=name, grid=(nblk, s // ts),
               in_specs=[pl.BlockSpec((ts, k), lambda g, t: (t, 0)), pl.BlockSpec((ts, nb), lambda g, t: (t, g))],
               out_specs=[out_spec], out_shape=[out_shape])(a, b)[0]


def _to_res(t, d):
    s, c = t.shape
    return t if d == 1 else t.reshape(s // d, d, c).transpose(1, 0, 2).reshape(s, c)


def _from_res(t, d):
    s, c = t.shape
    return t if d == 1 else t.reshape(d, s // d, c).transpose(1, 0, 2).reshape(s, c)


def _tile_gain(g, heads):
    return jnp.tile(g, (1,) * (g.ndim - 1) + (heads,))[..., None, :]


def _local_step(x, mem, pos, target, small, w_in, get_rest, on_grads):
    s = x.shape[0]
    nblk = s // BLK
    g1, g2 = small["attn_norm"], small["ffn_norm"]

    pos_rows = jnp.concatenate([_to_res(pos[:, None], d)[:, 0] for _, d in A_GROUPS] + [pos])
    tabs = _rope_tables(pos_rows)

    h, qa0, qa1, qa2, q_b, k_b, v_b, m_q = _k_in(x, g1, w_in)

    qkv_a = jnp.concatenate([_to_res(t, d) for t, (_, d) in zip((qa0, qa1, qa2), A_GROUPS)], axis=0)
    gq_a = _tile_gain(small["a_q_norm"], A_HEADS)
    gk_a = _tile_gain(small["a_k_norm"], A_HEADS)
    src_a = ((qkv_a, 0), (qkv_a, 1), (qkv_a, 2))
    qn_a, kn_a, vn_a = _k_prep(src_a, gq_a, gk_a, tabs, 0, wq=A_W, wk=A_W, rows_per_gain=s, name="prep_a")
    segs_a = tuple((gi * nblk, nblk // d) for gi, (_, d) in enumerate(A_GROUPS))
    o_res, l_res = _k_band_fwd(qn_a, kn_a, vn_a, hq=A_HEADS, hk=A_HEADS, max_dist=BLK, segs=segs_a, sink=None,
                               name="attn_a")
    og = [_from_res(o_res[gi * s:(gi + 1) * s], d) for gi, (_, d) in enumerate(A_GROUPS)]
    lg = [_from_res(l_res[gi * s:(gi + 1) * s], d) for gi, (_, d) in enumerate(A_GROUPS)]

    gq_b = _tile_gain(small["b_q_norm"], B_QH)
    gk_b = _tile_gain(small["b_k_norm"], B_KVH)
    src_b = ((q_b, 0), (k_b, 0), (v_b, 0))
    qn_b, kn_b, vn_b = _k_prep(src_b, gq_b, gk_b, tabs, 3 * s, wq=B_QH * HEAD, wk=B_KVH * HEAD, rows_per_gain=s,
                               name="prep_b")
    sink_x = small["b_sinks"][0]
    segs_b = ((0, nblk),)
    o_b, l_b = _k_band_fwd(qn_b, kn_b, vn_b, hq=B_QH, hk=B_KVH, max_dist=B_WINDOW - 1, segs=segs_b, sink=sink_x,
                           name="attn_b")

    wts = get_rest(0, o_b)
    gates = _k_gate(h, wts["w_gate"], small["b_gate"])

    gq_m = _tile_gain(small["m_q_norm"], M_HEADS)[0]
    gk_m = _tile_gain(small["m_k_norm"], M_HEADS)[0]
    mem_n, kv, mk, mv = _k_memkv(mem, small["mem_norm"], wts["w_mem_kv"], gk_m)
    o_m = _k_mem_fwd(m_q, gq_m, mk, mv)

    o_a, merged, x1, h2 = _k_merge(og, lg, o_b, o_m, gates, x, wts["w_o_a"], wts["w_o_b"], wts["w_o_m"],
                                   wts["w_out"], g2)
    wts.update(get_rest(1, x1))
    u = _k_up(h2, wts["w_up"])
    dy, f, dc, loss_acc = _k_ffn(u, wts["conv_w"], small["conv_b"], wts["w_down"], wts["w_down"].T, x1, target)
    loss = (0.5 / D_MODEL) * jnp.sum(loss_acc)

    dx1, du, cacc, g2acc = _k_conv_bwd(dc, u, wts["conv_w"], wts["w_up"], x1, g2, dy)
    tok = on_grads({"w_up": _k_wgrad(h2, du, nblk=CHIPS, stacked=True, name="dw_up"),
                    "w_down": _k_wgrad(f, dy, nblk=2, stacked=False, name="dw_down").reshape(CHIPS, -1, D_MODEL)}, dx1)
    (dgp, dp_a, dp_b, dp_m, dog0, dog1, dog2, dl0, dl1, dl2, do_b, do_m, bacc) = _k_merge_bwd(
        dx1, og, lg, o_a, o_b, o_m, gates, wts["w_o_a"], wts["w_o_b"], wts["w_o_m"], wts["w_out"], tok)
    tok = on_grads({"w_gate": _k_wgrad(h, dgp, nblk=CHIPS, stacked=True, name="dw_gate"),
                    "w_o_a": _k_wgrad(o_a, dp_a, nblk=CHIPS, stacked=True, name="dw_o_a"),
                    "w_o_b": _k_wgrad(o_b, dp_b, nblk=CHIPS, stacked=True, name="dw_o_b"),
                    "w_o_m": _k_wgrad(o_m, dp_m, nblk=CHIPS, stacked=True, name="dw_o_m"),
                    "w_out": _k_wgrad(merged, dx1, nblk=1, stacked=False, name="dw_out").reshape(CHIPS, -1, D_MODEL)},
                   do_m)

    dq_m, dmk, dmv, gqm_acc = _k_mem_bwd(m_q, gq_m + tok[0:1, 0:1], mk, mv, o_m, do_m)
    dw_kv, gmem_acc, gkm_acc = _k_memkv_bwd(mem, small["mem_norm"], wts["w_mem_kv"], gk_m, mem_n, kv, dmk, dmv)

    dq_bn, dk_bn, dv_b, sacc = _k_band_bwd(qn_b, kn_b, vn_b, do_b, l_b, o_b, hq=B_QH, hk=B_KVH,
                                           max_dist=B_WINDOW - 1, segs=segs_b, sink=sink_x, name="attn_b_bwd")
    tok = on_grads({}, dq_bn)
    dq_b, dk_b, gqb_acc, gkb_acc = _k_prep_bwd(src_b, dq_bn, dk_bn, gq_b + tok[0:1, 0:1], gk_b, tabs, 3 * s, wq=B_QH * HEAD,
                                               wk=B_KVH * HEAD, rows_per_gain=s, name="prep_b_bwd")

    do_res = jnp.concatenate([_to_res(t, d) for t, (_, d) in zip((dog0, dog1, dog2), A_GROUPS)], axis=0)
    dl_res = jnp.concatenate([_to_res(t, d) for t, (_, d) in zip((dl0, dl1, dl2), A_GROUPS)], axis=0)
    dq_an, dk_an, dv_a = _k_band_bwd(qn_a, kn_a, vn_a, do_res, l_res, dl_res, hq=A_HEADS, hk=A_HEADS, max_dist=BLK,
                                     segs=segs_a, sink=None, name="attn_a_bwd")
    dq_a, dk_a, gqa_acc, gka_acc = _k_prep_bwd(src_a, dq_an, dk_an, gq_a, gk_a, tabs, 0, wq=A_W, wk=A_W,
                                               rows_per_gain=s, name="prep_a_bwd")
    pieces = []
    for gi, (_, d) in enumerate(A_GROUPS):
        rs = slice(gi * s, (gi + 1) * s)
        pieces += [_from_res(t[rs], d) for t in (dq_a, dk_a, dv_a)]
    pieces += [dq_b, dk_b, dv_b, dq_m]
    grad_x, dproj, g1acc = _k_in_bwd(pieces, dgp, x, g1, dx1, w_in, wts["w_gate"])
    on_grads({"w_in": _k_wgrad(h, dproj, nblk=CHIPS, stacked=True, name="dw_in"),
              "w_mem_kv": dw_kv.reshape(CHIPS, -1, 2 * M_W)}, grad_x)

    def fold(acc, heads):
        v = jnp.sum(acc, axis=-2)
        return jnp.sum(v.reshape(v.shape[:-1] + (heads, -1)), axis=-2)

    csum = jnp.sum(cacc, axis=1)
    sml = {
        "attn_norm": jnp.sum(g1acc, axis=0), "a_q_norm": fold(gqa_acc, A_HEADS), "a_k_norm": fold(gka_acc, A_HEADS),
        "b_q_norm": fold(gqb_acc[0], B_QH), "b_k_norm": fold(gkb_acc[0], B_KVH),
        "b_sinks": jnp.sum(sacc, axis=0)[:B_QH], "mem_norm": jnp.sum(gmem_acc, axis=0),
        "m_q_norm": fold(gqm_acc, M_HEADS), "m_k_norm": fold(gkm_acc, M_HEADS),
        "b_gate": jnp.sum(bacc, axis=0), "ffn_norm": jnp.sum(g2acc, axis=0),
        "conv_w": csum[1:], "conv_b": csum[0],
    }
    return loss, grad_x, sml


def _mesh_pos():
    return lax.axis_index("x"), lax.axis_index("y"), lax.axis_index("c")


def _chip_peers(x, y):
    return [(1 - x, y), (x, 1 - y), (1 - x, 1 - y)]


_ANY = pl.BlockSpec(memory_space=pl.ANY)


def _comm_call(body, *, name, n_in, out_shape, scratch):
    return pl.pallas_call(body, name=name, in_specs=[_ANY] * n_in, out_specs=[_ANY] * len(out_shape),
                          out_shape=out_shape, scratch_shapes=scratch)


def _remote(src, dst, send_sem, recv_sem, dev):
    return pltpu.make_async_remote_copy(src_ref=src, dst_ref=dst, send_sem=send_sem, recv_sem=recv_sem,
                                        device_id=dev, device_id_type=MESH)


def _gather_shards(shards):
    nt = len(shards)
    split = [sh.shape[0] % 16 == 0 for sh in shards]

    def body(*refs):
        ins, outs = refs[:nt], refs[nt:2 * nt]
        ici_s, ici_r, fwd_s, fwd_r, own_s, own_r = refs[2 * nt:]
        x, y, c = _mesh_pos()
        me = 2 * x + y
        sib = (x, y, 1 - c)
        peers = _chip_peers(x, y)

        def half(ref, t, who):
            if not split[t]:
                return ref
            hr = shards[t].shape[0] // 2
            return ref.at[pl.ds(pl.multiple_of(who * hr, 8), hr), :]

        pending = []
        for t in range(nt):
            own = _remote(ins[t], outs[t].at[me], own_s.at[t], own_r.at[t], sib)
            own.start()
            pending.append(own.wait)
            for k, (px, py) in enumerate(peers):
                rc = _remote(half(ins[t], t, c), half(outs[t].at[me], t, c), ici_s.at[t, k], ici_r.at[t, k], (px, py, c))
                rc.start()
                pending.append(rc.wait_send)
        for t in range(nt):
            for k, (px, py) in enumerate(peers):
                land = half(outs[t].at[2 * px + py], t, c)
                _remote(land, land, ici_s.at[t, k], ici_r.at[t, k], (px, py, c)).wait_recv()
                if split[t]:
                    fw = _remote(land, land, fwd_s.at[t, k], fwd_r.at[t, k], sib)
                    fw.start()
                    pending.append(fw.wait_send)
                    other = half(outs[t].at[2 * px + py], t, 1 - c)
                    pending.append(_remote(other, other, fwd_s.at[t, k], fwd_r.at[t, k], sib).wait_recv)
        for wait in pending:
            wait()

    out_shape = [_sds((CHIPS,) + sh.shape, sh.dtype) for sh in shards]
    dma = pltpu.SemaphoreType.DMA
    scratch = [dma((nt, 3)), dma((nt, 3)), dma((nt, 3)), dma((nt, 3)), dma((nt,)), dma((nt,))]
    return _comm_call(body, name="gather_weights", n_in=nt, out_shape=out_shape, scratch=scratch)(*shards)


def _pair_join(halves, name):
    nt = len(halves)

    def body(*refs):
        ins, got = refs[:nt], refs[nt:2 * nt]
        send_sems, recv_sems = refs[2 * nt:]
        x, y, c = _mesh_pos()
        cps = []
        for t in range(nt):
            rc = _remote(ins[t], got[t], send_sems.at[t], recv_sems.at[t], (x, y, 1 - c))
            rc.start()
            cps.append(rc)
        for rc in cps:
            rc.wait()

    out_shape = [_sds(hf.shape, hf.dtype) for hf in halves]
    scratch = [pltpu.SemaphoreType.DMA((nt,)), pltpu.SemaphoreType.DMA((nt,))]
    return _comm_call(body, name=name, n_in=nt, out_shape=out_shape, scratch=scratch)(*halves)


_HBM = pl.BlockSpec(memory_space=pltpu.HBM)
_SEMS = pl.BlockSpec(memory_space=pltpu.SEMAPHORE)
_EFFECT = pltpu.SideEffectType.DATAFLOW_SIDE_EFFECTING


def _bcast_copies(ins, lands, send_sems, recv_sems):
    x, y, c = _mesh_pos()
    me = 2 * x + y
    targets = [((px, py, c), 2 * px + py) for px, py in _chip_peers(x, y)] + [((x, y, 1 - c), me)]
    out = []
    for t in range(len(ins)):
        for k, (dev, idx) in enumerate(targets):
            i = t * len(targets) + k
            arrival = lambda t=t, i=i, idx=idx, dev=dev: _remote(ins[t], lands[t].at[idx], send_sems.at[i],
                                                                 recv_sems.at[i], dev)
            out.append((_remote(ins[t], lands[t].at[me], send_sems.at[i], recv_sems.at[i], dev), arrival))
    return out


def _scatter_copies(ins, lands, send_sems, recv_sems):
    x, y, c = _mesh_pos()
    out = []
    for t in range(len(ins)):
        for k, (px, py) in enumerate(_chip_peers(x, y)):
            i = t * 3 + k
            cp = _remote(ins[t].at[2 * px + py], lands[t].at[k], send_sems.at[i], recv_sems.at[i], (px, py, c))
            out.append((cp, lambda cp=cp: cp))
    return out


def _pair_copies(ins, lands, send_sems, recv_sems):
    x, y, c = _mesh_pos()
    out = []
    for t in range(len(ins)):
        hr = ins[t].shape[1] // 2
        give = ins[t].at[:, pl.ds(pl.multiple_of((1 - c) * hr, 8), hr), :]
        cp = _remote(give, lands[t], send_sems.at[t], recv_sems.at[t], (x, y, 1 - c))
        out.append((cp, lambda cp=cp: cp))
    return out


def _split_start(copies, srcs, land_shapes, ncopy, dep, name):
    nt = len(srcs)

    def body(*refs):
        ins, lands = refs[:nt], refs[nt:2 * nt]
        send_sems, recv_sems, token = refs[2 * nt + 1], refs[2 * nt + 2], refs[-1]
        for send, _ in copies(ins, lands, send_sems, recv_sems):
            send.start()
        token[...] = jnp.zeros_like(token)

    lands = [pltpu.with_memory_space_constraint(lax.empty(sh, a.dtype), pltpu.HBM) for sh, a in zip(land_shapes, srcs)]
    srcs = [pltpu.with_memory_space_constraint(a, pltpu.HBM) for a in srcs]
    dma = pltpu.SemaphoreType.DMA
    out_shape = ([dma((nt * ncopy,)), dma((nt * ncopy,))] + [pltpu.HBM(a.shape, a.dtype) for a in srcs + lands]
                 + [_sds((8, 128))])
    outs = pl.pallas_call(
        body, name=name, in_specs=[_HBM] * (2 * nt) + [_ANY],
        out_specs=[_SEMS, _SEMS] + [_HBM] * (2 * nt) + [pl.BlockSpec(memory_space=pltpu.VMEM)], out_shape=out_shape,
        input_output_aliases={i: 2 + i for i in range(2 * nt)},
        compiler_params=pltpu.CompilerParams(has_side_effects=_EFFECT))(*srcs, *lands, dep)
    return outs[0], outs[1], outs[2:2 + nt], outs[2 + nt:2 + 2 * nt], outs[-1]


def _split_wait(copies, send_sems, recv_sems, srcs, lands, after, name):
    nt = len(srcs)

    def body(*refs):
        ins, lnd = refs[:nt], refs[nt:2 * nt]
        for send, arrival in copies(ins, lnd, refs[2 * nt], refs[2 * nt + 1]):
            send.wait_send()
            arrival().wait_recv()

    outs = pl.pallas_call(
        body, name=name, in_specs=[_HBM] * (2 * nt) + [_SEMS, _SEMS, _ANY], out_specs=[_HBM] * (2 * nt),
        out_shape=[pltpu.HBM(a.shape, a.dtype) for a in list(srcs) + list(lands)],
        input_output_aliases={i: i for i in range(2 * nt)},
        compiler_params=pltpu.CompilerParams(has_side_effects=_EFFECT))(*srcs, *lands, send_sems, recv_sems, after)
    return outs[:nt], outs[nt:]


def _gather_small(packed):
    n = packed.shape[0]

    def body(in_ref, out_ref, send_sems, recv_sems, loc_sem):
        x, y, c = _mesh_pos()
        me = 4 * x + 2 * y + c
        lc = pltpu.make_async_copy(in_ref, out_ref.at[me], loc_sem)
        lc.start()
        peers = []
        for k in range(1, NDEV):
            px, py, pc = x ^ (k >> 2), y ^ ((k >> 1) & 1), c ^ (k & 1)
            rc = pltpu.make_async_remote_copy(src_ref=in_ref, dst_ref=out_ref.at[me], send_sem=send_sems.at[k - 1],
                                              recv_sem=recv_sems.at[k - 1], device_id=(px, py, pc), device_id_type=MESH)
            rc.start()
            peers.append((k, px, py, pc))
        lc.wait()
        for k, px, py, pc in peers:
            pltpu.make_async_remote_copy(src_ref=in_ref, dst_ref=out_ref.at[4 * px + 2 * py + pc],
                                         send_sem=send_sems.at[k - 1], recv_sem=recv_sems.at[k - 1],
                                         device_id=(px, py, pc), device_id_type=MESH).wait()

    scratch = [pltpu.SemaphoreType.DMA((NDEV - 1,)), pltpu.SemaphoreType.DMA((NDEV - 1,)), pltpu.SemaphoreType.DMA]
    return _comm_call(body, name="gather_small_grads", n_in=1, out_shape=[_sds((NDEV, n, 128))],
                      scratch=scratch)(packed)[0]


def _row_tile(r, c):
    t = r
    while t * c * 4 > (1 << 20) and t % 16 == 0:
        t //= 2
    return t


def _k_pair_add(full, got, name):
    g, r, c = full.shape
    hr = r // 2
    tr = _row_tile(hr, c)
    nh = hr // tr

    def body(a_ref, b_ref, o_ref):
        o_ref[...] = (a_ref[...] + b_ref[...]).astype(_WIRE)

    mine = pl.BlockSpec((None, tr, c), lambda i, j: (i, lax.axis_index("c") * nh + j, 0))
    spec = pl.BlockSpec((None, tr, c), lambda i, j: (i, j, 0))
    return _pc(body, name=name, grid=(g, nh), in_specs=[mine, spec], out_specs=[spec],
               out_shape=[_sds((g, hr, c), _WIRE)])(full, got)[0]


def _k_chip_sum(parts, slots, name):
    _, r, c = parts.shape
    tr = _row_tile(r, c)

    def body(a_ref, s_ref, o_ref):
        acc = a_ref[...].astype(F32)
        for k in range(3):
            acc = acc + s_ref[k].astype(F32)
        o_ref[...] = acc

    own = pl.BlockSpec((None, tr, c), lambda i: (2 * lax.axis_index("x") + lax.axis_index("y"), i, 0))
    return _pc(body, name=name, grid=(r // tr,), in_specs=[own, pl.BlockSpec((3, tr, c), lambda i: (0, i, 0))],
               out_specs=[_row(tr, c)], out_shape=[_sds((r, c))])(parts, slots)[0]


def _adam(w, g, m, v):
    m = ADAM_B1 * m + (1.0 - ADAM_B1) * g
    v = ADAM_B2 * v + (1.0 - ADAM_B2) * (g * g)
    m_hat = m / (1.0 - ADAM_B1 ** ADAM_STEP)
    v_hat = v / (1.0 - ADAM_B2 ** ADAM_STEP)
    return -ADAM_LR * (m_hat / (jnp.sqrt(v_hat) + ADAM_EPS) + ADAM_WD * w), m, v


def _k_adam(w, mine, theirs, m, v, dep, name):
    r, c = w.shape
    hr = r // 2
    tr = _row_tile(hr, c)
    nh = hr // tr

    def body(w_ref, a_ref, b_ref, m_ref, v_ref, dep_ref, g_ref, d_ref, mo_ref, vo_ref):
        upper = (pl.program_id(0) >= nh).astype(jnp.int32)
        g = jnp.where(upper == lax.axis_index("c"), a_ref[...], b_ref[...])
        g_ref[...] = g
        d_ref[...], mo_ref[...], vo_ref[...] = _adam(w_ref[...], g, m_ref[...], v_ref[...])

    hspec = pl.BlockSpec((tr, c), lambda i: (jnp.where(i >= nh, i - nh, i), 0))
    return _pc(body, name=name, grid=(r // tr,),
               in_specs=[_row(tr, c), hspec, hspec, _row(tr, c), _row(tr, c), _res((8, 128))],
               out_specs=[_row(tr, c)] * 4, out_shape=[_sds((r, c))] * 4)(w, mine, theirs, m, v, dep)


def _k_sum8(a):
    _, n, _ = a.shape

    def body(a_ref, o_ref):
        acc = a_ref[0]
        for k in range(1, NDEV):
            acc = acc + a_ref[k]
        o_ref[...] = acc

    return _pc(body, name="sum_small_grads", grid=(1,), in_specs=[_acc(a.shape)], out_specs=[_acc((n, 128))],
               out_shape=[_sds((n, 128))])(a)[0]


def _k_adam_small(ws, gs, ms, vs):
    n = len(ws)

    def body(*refs):
        for k in range(n):
            w_ref, g_ref, m_ref, v_ref, d_ref, mo_ref, vo_ref = refs[k::n]
            d_ref[...], mo_ref[...], vo_ref[...] = _adam(w_ref[...], g_ref[...], m_ref[...], v_ref[...])

    specs = [_acc(a.shape) for a in ws]
    outs = _pc(body, name="adam_small", grid=(1,), in_specs=specs * 4, out_specs=specs * 3,
               out_shape=[_sds(a.shape) for a in ws] * 3)(*ws, *gs, *ms, *vs)
    return outs[:n], outs[n:2 * n], outs[2 * n:]


def _pack(vals):
    rows = []
    for a in vals:
        flat = a.reshape(-1)
        n = -(-flat.shape[0] // 1024) * 1024
        rows.append(jnp.pad(flat, (0, n - flat.shape[0])).reshape(n // 128, 128))
    return jnp.concatenate(rows, axis=0)


def _unpack(packed, shapes):
    out, off = [], 0
    for sh in shapes:
        size = int(np.prod(sh))
        n = -(-size // 1024) * 1024
        out.append(packed[off // 128:(off + n) // 128].reshape(-1)[:size].reshape(sh))
        off += n
    return out


_WEIGHTS = ["attn_norm", "w_in", "a_q_norm", "a_k_norm", "b_q_norm", "b_k_norm", "b_sinks", "mem_norm", "w_mem_kv",
            "m_q_norm", "m_k_norm", "w_o_a", "w_o_b", "w_o_m", "w_gate", "b_gate", "w_out", "ffn_norm", "w_up",
            "conv_w", "conv_b", "w_down"]
_BIG = ["w_in", "w_mem_kv", "w_o_a", "w_o_b", "w_o_m", "w_gate", "w_out", "w_up", "w_down"]
_SMALL = [n for n in _WEIGHTS if n not in _BIG]


def kernel(x, mem, positions, attn_norm, w_in, a_q_norm, a_k_norm, b_q_norm, b_k_norm, b_sinks, mem_norm, w_mem_kv, m_q_norm, m_k_norm, w_o_a, w_o_b, w_o_m, w_gate, b_gate, w_out, ffn_norm, w_up, conv_w, conv_b, w_down, loss_target, m_attn_norm, m_w_in, m_a_q_norm, m_a_k_norm, m_b_q_norm, m_b_k_norm, m_b_sinks, m_mem_norm, m_w_mem_kv, m_m_q_norm, m_m_k_norm, m_w_o_a, m_w_o_b, m_w_o_m, m_w_gate, m_b_gate, m_w_out, m_ffn_norm, m_w_up, m_conv_w, m_conv_b, m_w_down, v_attn_norm, v_w_in, v_a_q_norm, v_a_k_norm, v_b_q_norm, v_b_k_norm, v_b_sinks, v_mem_norm, v_w_mem_kv, v_m_q_norm, v_m_k_norm, v_w_o_a, v_w_o_b, v_w_o_m, v_w_gate, v_b_gate, v_w_out, v_ffn_norm, v_w_up, v_conv_w, v_conv_b, v_w_down):
    given = dict(locals())
    w = {n: given[n][0] for n in _WEIGHTS}
    m1 = {n: given["m_" + n][0] for n in _WEIGHTS}
    m2 = {n: given["v_" + n][0] for n in _WEIGHTS}

    w_in = _gather_shards([w["w_in"].astype(_MM)])[0]
    stages = (["w_gate", "w_mem_kv", "w_o_a", "w_o_b", "w_o_m", "w_out"], ["w_up", "w_down", "conv_w"])
    tok, started = w_in, []
    for k, names in enumerate(stages):
        shards = [w[n] if n == "conv_w" else w[n].astype(_MM) for n in names]
        *handles, tok = _split_start(_bcast_copies, shards, [(CHIPS,) + a.shape for a in shards], 4, tok,
                                     "gather_start_%d" % k)
        started.append(handles)
    small = {n: (w[n][None, :] if w[n].ndim == 1 else w[n]) for n in _SMALL if n != "conv_w"}
    small["attn_norm"] = small["attn_norm"] + tok[0:1, 0:1]

    def get_rest(stage, after):
        send, recv, srcs, lands = started[stage]
        got = _split_wait(_bcast_copies, send, recv, srcs, lands, after, "gather_wait_%d" % stage)[1]
        wts = dict(zip(stages[stage], got))
        for n in ("w_mem_kv", "w_out", "w_down"):
            if n in wts:
                wts[n] = wts[n].reshape(-1, wts[n].shape[-1])
        return wts

    parts, slots, pair, scat = {}, {}, [], []
    zeros = jnp.zeros((8, 128), F32)

    def finish_pair(after):
        names, tag, send, recv, srcs, lands = pair.pop()
        full, got = _split_wait(_pair_copies, send, recv, srcs, lands, after, "pair_wait_" + tag)
        mine = [_k_pair_add(f, b, "pair_add_" + n) for n, f, b in zip(names, full, got)]
        shapes = [(3,) + p.shape[1:] for p in mine]
        send, recv, srcs, lands, token = _split_start(_scatter_copies, mine, shapes, 3, zeros, "scatter_start_" + tag)
        scat.append((names, tag, send, recv, srcs, lands))
        return token

    def on_grads(group, after):
        names = list(group)
        tag = "_".join(names)
        token = finish_pair(after) if pair else zeros
        if not group:
            return token
        grads_g = [group[n] for n in names]
        shapes = [(CHIPS, g.shape[1] // 2, g.shape[2]) for g in grads_g]
        send, recv, srcs, lands, token = _split_start(_pair_copies, grads_g, shapes, 1, token, "pair_start_" + tag)
        pair.append((names, tag, send, recv, srcs, lands))
        return token

    loss, grad_x, sml = _local_step(x[0], mem[0], positions[0], loss_target[0], small, w_in, get_rest, on_grads)
    loss = lax.psum(loss, ("x", "y", "c"))
    early = [n for names, *_ in scat for n in names]
    for names, tag, send, recv, srcs, lands in scat:
        mine, got = _split_wait(_scatter_copies, send, recv, srcs, lands, grad_x, "scatter_wait_" + tag)
        parts.update(zip(names, mine))
        slots.update(zip(names, got))
    scat.clear()
    reduced = {n: _k_chip_sum(parts[n], slots[n], "chip_add_" + n) for n in early}
    theirs = dict(zip(early, _pair_join([reduced[n] for n in early], "grad_pair_join_early")))
    grads = {}

    shapes = [sml[n].shape for n in _SMALL]
    gsm = dict(zip(_SMALL, _unpack(_k_sum8(_gather_small(_pack([sml[n] for n in _SMALL]))), shapes)))
    nu = w["conv_w"].shape[1]
    chip = 2 * lax.axis_index("x") + lax.axis_index("y")
    gsm["conv_w"] = lax.dynamic_slice_in_dim(gsm["conv_w"], chip * nu, nu, axis=1)
    for n in _SMALL:
        grads[n] = gsm[n].reshape(w[n].shape)

    delta, new_m, new_v = {}, {}, {}
    dep = zeros
    for k, n in enumerate(early):
        grads[n], delta[n], new_m[n], new_v[n] = _k_adam(w[n], reduced[n], theirs[n], m1[n], m2[n], dep, "adam_" + n)
        dep = finish_pair(delta[n]) if k == 0 else delta[n]
    as2d = lambda d: [d[n][None, :] if d[n].ndim == 1 else d[n] for n in _SMALL]
    for dst, outs in zip((delta, new_m, new_v), _k_adam_small(as2d(w), as2d(grads), as2d(m1), as2d(m2))):
        dst.update((n, a.reshape(w[n].shape)) for n, a in zip(_SMALL, outs))
    late, tag, send, recv, srcs, lands = scat.pop()
    mine, got = _split_wait(_scatter_copies, send, recv, srcs, lands, dep, "scatter_wait_" + tag)
    for n, a, b in zip(late, mine, got):
        reduced[n] = _k_chip_sum(a, b, "chip_add_" + n)
    theirs.update(zip(late, _pair_join([reduced[n] for n in late], "grad_pair_join_late")))
    for n in late:
        grads[n], delta[n], new_m[n], new_v[n] = _k_adam(w[n], reduced[n], theirs[n], m1[n], m2[n], zeros, "adam_" + n)

    lead = lambda d: [d[n][None] for n in _WEIGHTS]
    return (loss, grad_x[None], *lead(grads), *lead(delta), *lead(new_m), *lead(new_v))
```

```python
import math

import jax
import jax.numpy as jnp
import numpy as np
from jax import lax
from jax.experimental import pallas as pl
from jax.experimental.pallas import tpu as pltpu

F32 = jnp.float32
_MM = jnp.bfloat16
_WIRE = jnp.bfloat16

D_MODEL = 1024
HEAD = 64
BLK = 128
A_GROUPS = ((128, 1), (512, 4), (2048, 16))
A_HEADS = 4
A_W = A_HEADS * HEAD
B_QH = 8
B_KVH = 2
B_WINDOW = 128
M_HEADS = 4
M_HD = 128
M_W = M_HEADS * M_HD
D_FF = 2816
EPS = 1e-6
NEG = -1e30
ROPE_THETA = 500000.0
ROPE_ROT = 16
CHIPS = 4
NDEV = 8
ADAM_LR, ADAM_B1, ADAM_B2, ADAM_EPS, ADAM_WD, ADAM_STEP = 0.001, 0.9, 0.999, 1e-08, 0.01, 10
VMEM_LIMIT = 58 * 1024 * 1024
MESH = pl.DeviceIdType.MESH


def _pc(body, *, name, grid, in_specs, out_specs, out_shape, scratch=()):
    return pl.pallas_call(
        body, name=name, grid=grid, in_specs=in_specs, out_specs=out_specs, out_shape=out_shape,
        scratch_shapes=list(scratch),
        compiler_params=pltpu.CompilerParams(dimension_semantics=("arbitrary",) * len(grid),
                                             vmem_limit_bytes=VMEM_LIMIT))


def _row(ts, c, col=0):
    return pl.BlockSpec((ts, c), lambda i: (i, col))


def _res(shape):
    n = len(shape)
    return pl.BlockSpec(tuple(shape), lambda i: (0,) * n, pipeline_mode=pl.Buffered(1))


def _acc(shape):
    n = len(shape)
    return pl.BlockSpec(tuple(shape), lambda i: (0,) * n)


def _sds(shape, dtype=F32):
    return jax.ShapeDtypeStruct(tuple(shape), dtype)


def _dot(a, b):
    return jnp.dot(a.astype(_MM), b.astype(_MM), preferred_element_type=F32)


def _dot_nt(a, b):
    return lax.dot_general(a.astype(_MM), b.astype(_MM), (((1,), (1,)), ((), ())), preferred_element_type=F32)


def _dot_tn(a, b):
    return lax.dot_general(a.astype(_MM), b.astype(_MM), (((0,), (0,)), ((), ())), preferred_element_type=F32)


def _sum8(v):
    ts, c = v.shape
    return jnp.sum(v.reshape(ts // 8, 8, c), axis=0)


def _sigmoid(z):
    return 1.0 / (1.0 + jnp.exp(-z))


def _rms(x):
    r = lax.rsqrt(jnp.mean(x * x, axis=-1, keepdims=True) + EPS)
    return x * r, r


def _rms_bwd(dy, xh, r, gain):
    z = dy * gain
    return r * (z - xh * jnp.mean(z * xh, axis=-1, keepdims=True))


def _split_hi_lo(v):
    hi = v.astype(_MM)
    return hi, (v - hi.astype(F32)).astype(_MM)


def _lane_head(shape):
    return lax.shift_right_logical(lax.broadcasted_iota(jnp.int32, shape, len(shape) - 1), 6)


def _seg_sum64(v):
    w = v.shape[1]
    e = jnp.where(_lane_head((w, w)) == lax.shift_right_logical(lax.broadcasted_iota(jnp.int32, (w, w), 0), 6),
                  1.0, 0.0).astype(_MM)
    hi, lo = _split_hi_lo(v)
    return jnp.dot(hi, e, preferred_element_type=F32) + jnp.dot(lo, e, preferred_element_type=F32)


def _seg_norm(x, seg):
    if seg == HEAD:
        r = lax.rsqrt(_seg_sum64(x * x) * (1.0 / HEAD) + EPS)
        return x * r, r
    w = x.shape[1]
    xh, rr = [], []
    for s in range(w // seg):
        xs = x[:, s * seg:(s + 1) * seg]
        r = lax.rsqrt(jnp.mean(xs * xs, axis=-1, keepdims=True) + EPS)
        xh.append(xs * r)
        rr.append(jnp.broadcast_to(r, xs.shape))
    return jnp.concatenate(xh, axis=1), jnp.concatenate(rr, axis=1)


def _seg_mean(v, seg):
    if seg == HEAD:
        return _seg_sum64(v) * (1.0 / HEAD)
    w = v.shape[1]
    out = []
    for s in range(w // seg):
        vs = v[:, s * seg:(s + 1) * seg]
        out.append(jnp.broadcast_to(jnp.mean(vs, axis=-1, keepdims=True), vs.shape))
    return jnp.concatenate(out, axis=1)


def _rope(t, c, sa, sb):
    out = []
    for cb in range(t.shape[1] // 128):
        tc = t[:, cb * 128:(cb + 1) * 128]
        out.append(tc * c + pltpu.roll(tc, 120, 1) * sa + pltpu.roll(tc, 8, 1) * sb)
    return jnp.concatenate(out, axis=1) if len(out) > 1 else out[0]


def _rope_bwd(dy, c, sa, sb):
    out = []
    for cb in range(dy.shape[1] // 128):
        dc = dy[:, cb * 128:(cb + 1) * 128]
        out.append(dc * c + pltpu.roll(dc * sa, 8, 1) + pltpu.roll(dc * sb, 120, 1))
    return jnp.concatenate(out, axis=1) if len(out) > 1 else out[0]


def _rope_consts():
    half = ROPE_ROT // 2
    c = np.float32(-2.0 * math.log(ROPE_THETA) / ROPE_ROT)
    freqs = np.exp(np.arange(half, dtype=np.float32) * c).astype(np.float32)
    place = np.zeros((3, half, 128), np.float32)
    ones = np.zeros((1, 128), np.float32)
    for lane in range(128):
        d = lane % HEAD
        if d < half:
            place[0, d, lane], place[1, d, lane] = 1.0, -1.0
        elif d < ROPE_ROT:
            place[0, d - half, lane], place[2, d - half, lane] = 1.0, 1.0
        else:
            ones[0, lane] = 1.0
    return np.tile(freqs[:, None], (1, 128)), place, ones


def _rope_tables(pos_rows):
    r = pos_rows.shape[0]
    tr = min(1024, r)
    freqs, place, ones = _rope_consts()

    def split3(v):
        hi, mid = _split_hi_lo(v)
        lo = (v - hi.astype(F32) - mid.astype(F32)).astype(_MM)
        return hi, mid, lo

    def body(p_ref, f_ref, e_ref, one_ref, c_ref, sa_ref, sb_ref):
        ang = jnp.concatenate([p_ref[j:j + 1, :].astype(F32) * f_ref[...] for j in range(tr // 128)], axis=1)
        cos, sin = jnp.cos(ang), jnp.sin(ang)
        for ref, k, v in ((c_ref, 0, cos), (sa_ref, 1, sin), (sb_ref, 2, sin)):
            e = e_ref[k].astype(_MM)
            out = sum(_dot_tn(part, e) for part in split3(v))
            ref[...] = out + one_ref[...] if k == 0 else out

    return _pc(body, name="rope_tables", grid=(r // tr,),
               in_specs=[pl.BlockSpec((tr // 128, 128), lambda i: (i, 0)), _acc((ROPE_ROT // 2, 128)),
                         _acc((3, ROPE_ROT // 2, 128)), _acc((1, 128))],
               out_specs=[_row(tr, 128)] * 3, out_shape=[_sds((r, 128))] * 3)(
                   pos_rows.reshape(r // 128, 128), jnp.asarray(freqs), jnp.asarray(place), jnp.asarray(ones))


def _k_in(x, g1, w_in):
    s = x.shape[0]
    ts = min(512, s)
    nin = w_in.shape[2]
    ncol = CHIPS * nin
    a_cols = 3 * A_W
    offs = [0, a_cols, 2 * a_cols, 3 * a_cols, 3 * a_cols + B_QH * HEAD,
            3 * a_cols + (B_QH + B_KVH) * HEAD, 3 * a_cols + (B_QH + 2 * B_KVH) * HEAD, ncol]

    def body(x_ref, g_ref, wi_ref, h_ref, a0, a1, a2, qb, kb, vb, mq, p_scr):
        xh, _ = _rms(x_ref[...])
        h = (xh * g_ref[...]).astype(_MM)
        h_ref[...] = h
        for j in range(CHIPS):
            p_scr[:, j * nin:(j + 1) * nin] = jnp.dot(h, wi_ref[j], preferred_element_type=F32)
        for k, ref in enumerate((a0, a1, a2, qb, kb, vb, mq)):
            ref[...] = p_scr[:, offs[k]:offs[k + 1]]

    widths = [offs[k + 1] - offs[k] for k in range(7)]
    return _pc(
        body, name="in_proj", grid=(s // ts,),
        in_specs=[_row(ts, D_MODEL), _res((1, D_MODEL)), _res(w_in.shape)],
        out_specs=[_row(ts, D_MODEL)] + [_row(ts, w) for w in widths],
        out_shape=[_sds((s, D_MODEL), _MM)] + [_sds((s, w)) for w in widths],
        scratch=[pltpu.VMEM((ts, ncol), F32)])(x, g1, w_in)


def _k_gate(h, w_gate, b_gate):
    s = h.shape[0]
    ts = min(256, s)
    ng = w_gate.shape[2]

    def body(h_ref, wg_ref, bg_ref, gt_ref):
        h = h_ref[...]
        for j in range(CHIPS):
            z = jnp.dot(h, wg_ref[j], preferred_element_type=F32) + bg_ref[:, j * ng:(j + 1) * ng]
            gt_ref[:, j * ng:(j + 1) * ng] = _sigmoid(z)

    return _pc(body, name="gate_proj", grid=(s // ts,),
               in_specs=[_row(ts, D_MODEL), _res(w_gate.shape), _res(b_gate.shape)],
               out_specs=[_row(ts, CHIPS * ng)], out_shape=[_sds((s, CHIPS * ng))])(h, w_gate, b_gate)[0]


def _k_prep(srcs, gq, gk, tabs, tab_row, *, wq, wk, rows_per_gain, name):
    rows = srcs[0][0].shape[0]
    ts = min(512, rows)

    def body(q_ref, k_ref, v_ref, gq_ref, gk_ref, c_ref, sa_ref, sb_ref, qn_ref, kn_ref, vn_ref):
        c, sa, sb = c_ref[...], sa_ref[...], sb_ref[...]
        qh, _ = _seg_norm(q_ref[...], HEAD)
        qn_ref[...] = _rope(qh * gq_ref[...], c, sa, sb).astype(_MM)
        kh, _ = _seg_norm(k_ref[...], HEAD)
        kn_ref[...] = _rope(kh * gk_ref[...], c, sa, sb).astype(_MM)
        vn_ref[...] = v_ref[...].astype(_MM)

    gspec = lambda w: pl.BlockSpec((None, 1, w), lambda i: ((i * ts) // rows_per_gain, 0, 0))
    return _pc(
        body, name=name, grid=(rows // ts,),
        in_specs=[_row(ts, wq, srcs[0][1]), _row(ts, wk, srcs[1][1]), _row(ts, wk, srcs[2][1]),
                  gspec(wq), gspec(wk)] + [pl.BlockSpec((ts, 128), lambda i: (i + tab_row // ts, 0))] * 3,
        out_specs=[_row(ts, wq), _row(ts, wk), _row(ts, wk)],
        out_shape=[_sds((rows, wq), _MM), _sds((rows, wk), _MM), _sds((rows, wk), _MM)])(
            srcs[0][0], srcs[1][0], srcs[2][0], gq, gk, *tabs)


def _first_flag(b, segs, nb):
    first = b >= nb
    for k, (start, period) in enumerate(segs):
        end = segs[k + 1][0] if k + 1 < len(segs) else nb
        first = first | ((b >= start) & (b < end) & (lax.rem(b - start, jnp.int32(period)) == 0))
    return first


def _band_bias(thr, with_cur):
    qi = lax.broadcasted_iota(jnp.int32, (BLK, BLK), 0)
    kj = lax.broadcasted_iota(jnp.int32, (BLK, BLK), 1)
    prev = jnp.where(kj >= qi + thr, 0.0, NEG)
    return jnp.concatenate([prev, jnp.where(kj <= qi, 0.0, NEG)], axis=1) if with_cur else prev


def _blockdiag(t4):
    head = _lane_head((1, A_W))
    return jnp.concatenate([t4 * jnp.where(head == h, 1.0, 0.0).astype(t4.dtype) for h in range(A_HEADS)], axis=0)


def _fold_diag(t, n):
    head = _lane_head((n, A_W))
    out = t[3 * n:4 * n]
    for h in (2, 1, 0):
        out = jnp.where(head == h, t[h * n:(h + 1) * n], out)
    return out


def _expand_heads(cols):
    n = cols[0].shape[0]
    head = _lane_head((n, A_W))
    out = jnp.broadcast_to(cols[3], (n, A_W))
    for h in (2, 1, 0):
        out = jnp.where(head == h, cols[h], out)
    return out


def _unit_kv(pieces, u, shared):
    cols = slice(u * HEAD, (u + 1) * HEAD) if shared else slice(u * A_W, (u + 1) * A_W)
    rows = [ref[rs, cols] for ref, rs in pieces]
    k = rows[0] if len(rows) == 1 else jnp.concatenate(rows, axis=0)
    return jnp.concatenate([k] * A_HEADS, axis=1) if shared else k


_LO, _HI, _BOTH = slice(0, BLK), slice(BLK, 2 * BLK), slice(0, 2 * BLK)


def _k_band_fwd(qn, kn, vn, *, hq, hk, max_dist, segs, sink, name):
    rows = qn.shape[0]
    nb = rows // BLK
    units = hq // A_HEADS
    shared = hk != hq
    wq, wk = hq * HEAD, hk * HEAD
    scale = HEAD ** -0.5

    def body(*refs):
        if sink is None:
            q_ref, kc_ref, kp_ref, vc_ref, vp_ref, o_ref, l_ref = refs
        else:
            q_ref, kc_ref, kp_ref, vc_ref, vp_ref, sk_ref, o_ref, l_ref = refs
        i = pl.program_id(0)
        for half, rs in enumerate((_LO, _HI)):
            bias = _band_bias(jnp.where(_first_flag(2 * i + half, segs, nb), 1 << 20, BLK - max_dist), True)
            kpieces = ((kp_ref, _LO), (kc_ref, _LO)) if half == 0 else ((kc_ref, _BOTH),)
            vpieces = ((vp_ref, _LO), (vc_ref, _LO)) if half == 0 else ((vc_ref, _BOTH),)
            for u in range(units):
                us = slice(u * A_W, (u + 1) * A_W)
                kb = _blockdiag(_unit_kv(kpieces, u, shared))
                vb = _blockdiag(_unit_kv(vpieces, u, shared))
                s_all = _dot_nt(q_ref[rs, us], kb) * scale
                ps, ls = [], []
                for h in range(A_HEADS):
                    s = s_all[:, h * 2 * BLK:(h + 1) * 2 * BLK] + bias
                    m = jnp.max(s, axis=-1, keepdims=True)
                    e = jnp.exp(s - m)
                    lse = m + jnp.log(jnp.sum(e, axis=-1, keepdims=True))
                    if sink is not None:
                        sk = sk_ref[u * A_HEADS + h]
                        mx = jnp.maximum(lse, sk)
                        lse = mx + jnp.log(jnp.exp(lse - mx) + jnp.exp(sk - mx))
                    ps.append((e * jnp.exp(m - lse)).astype(_MM))
                    ls.append(lse)
                o_ref[rs, us] = _dot(jnp.concatenate(ps, axis=1), vb)
                l_ref[rs, us] = _expand_heads(ls)

    two = lambda w: pl.BlockSpec((2 * BLK, w), lambda i: (i, 0))
    prev = lambda w: pl.BlockSpec((BLK, w), lambda i: (jnp.maximum(2 * i - 1, 0), 0))
    in_specs = [two(wq), two(wk), prev(wk), two(wk), prev(wk)]
    args = [qn, kn, kn, vn, vn]
    if sink is not None:
        in_specs.append(pl.BlockSpec(memory_space=pltpu.SMEM))
        args.append(sink)
    return _pc(body, name=name, grid=(nb // 2,), in_specs=in_specs, out_specs=[two(wq), two(wq)],
               out_shape=[_sds((rows, wq)), _sds((rows, wq))])(*args)


def _k_memkv(mem, mem_norm, w_kv, m_k_norm):
    n = mem.shape[0]

    def body(m_ref, g_ref, w_ref, gk_ref, mn_ref, kv_ref, mk_ref, mv_ref):
        mh, _ = _rms(m_ref[...])
        mn = (mh * g_ref[...]).astype(_MM)
        mn_ref[...] = mn
        kv = jnp.dot(mn, w_ref[...], preferred_element_type=F32)
        kv_ref[...] = kv
        kh, _ = _seg_norm(kv[:, :M_W], M_HD)
        mk_ref[...] = (kh * gk_ref[...]).astype(_MM)
        mv_ref[...] = kv[:, M_W:].astype(_MM)

    return _pc(body, name="mem_kv", grid=(1,),
               in_specs=[_acc((n, D_MODEL)), _acc((1, D_MODEL)), _acc(w_kv.shape), _acc((1, M_W))],
               out_specs=[_acc((n, D_MODEL)), _acc((n, 2 * M_W)), _acc((n, M_W)), _acc((n, M_W))],
               out_shape=[_sds((n, D_MODEL), _MM), _sds((n, 2 * M_W)), _sds((n, M_W), _MM), _sds((n, M_W), _MM)])(
                   mem, mem_norm, w_kv, m_k_norm)


def _mem_probs(q, mk):
    sc = _dot_nt(q, mk) * (M_HD ** -0.5)
    e = jnp.exp(sc - jnp.max(sc, axis=-1, keepdims=True))
    return e / jnp.sum(e, axis=-1, keepdims=True)


def _k_mem_fwd(m_q, gq, mk, mv):
    s = m_q.shape[0]
    n = mk.shape[0]
    ts = min(512, s)

    def body(q_ref, g_ref, mk_ref, mv_ref, o_ref):
        qh, _ = _seg_norm(q_ref[...], M_HD)
        qn = (qh * g_ref[...]).astype(_MM)
        for h in range(M_HEADS):
            hs = slice(h * M_HD, (h + 1) * M_HD)
            o_ref[:, hs] = _dot(_mem_probs(qn[:, hs], mk_ref[:, hs]), mv_ref[:, hs])

    return _pc(body, name="mem_attn", grid=(s // ts,),
               in_specs=[_row(ts, M_W), _res((1, M_W)), _res((n, M_W)), _res((n, M_W))],
               out_specs=[_row(ts, M_W)], out_shape=[_sds((s, M_W))])(m_q, gq, mk, mv)[0]


def _group_weights(l0, l1, l2):
    m = jnp.maximum(jnp.maximum(l0, l1), l2)
    e0, e1, e2 = jnp.exp(l0 - m), jnp.exp(l1 - m), jnp.exp(l2 - m)
    inv = 1.0 / (e0 + e1 + e2)
    return e0 * inv, e1 * inv, e2 * inv


def _branch_products(oa, ob, om, woa_ref, wob_ref, wom_ref, j):
    return _dot(oa, woa_ref[j]), _dot(ob, wob_ref[j]), _dot(om, wom_ref[j])


def _k_merge(og, lg, o_b, o_m, gates, x, w_oa, w_ob, w_om, w_out, g2):
    s = x.shape[0]
    ts = min(256, s)
    nc = w_oa.shape[2]

    def body(o0, o1, o2, l0, l1, l2, ob_ref, om_ref, gt_ref, x_ref, woa, wob, wom, wout, g_ref,
             oa_ref, mer_ref, x1_ref, h2_ref, m_scr):
        w0, w1, w2 = _group_weights(l0[...], l1[...], l2[...])
        oa = w0 * o0[...] + w1 * o1[...] + w2 * o2[...]
        oa_ref[...] = oa
        ob, om = ob_ref[...], om_ref[...]
        for j in range(CHIPS):
            pa, pb, pm = _branch_products(oa, ob, om, woa, wob, wom, j)
            cs = lambda br: slice(br * D_MODEL + j * nc, br * D_MODEL + (j + 1) * nc)
            m_scr[:, j * nc:(j + 1) * nc] = gt_ref[:, cs(0)] * pa + gt_ref[:, cs(1)] * pb + gt_ref[:, cs(2)] * pm
        mer = m_scr[...].astype(_MM)
        mer_ref[...] = mer
        x1 = x_ref[...] + jnp.dot(mer, wout[...], preferred_element_type=F32)
        x1_ref[...] = x1
        xh, _ = _rms(x1)
        h2_ref[...] = (xh * g_ref[...]).astype(_MM)

    return _pc(
        body, name="merge_out", grid=(s // ts,),
        in_specs=[_row(ts, A_W)] * 6 + [_row(ts, B_QH * HEAD), _row(ts, M_W), _row(ts, 3 * D_MODEL), _row(ts, D_MODEL),
                                         _res(w_oa.shape), _res(w_ob.shape), _res(w_om.shape), _res(w_out.shape),
                                         _res((1, D_MODEL))],
        out_specs=[_row(ts, A_W), _row(ts, D_MODEL), _row(ts, D_MODEL), _row(ts, D_MODEL)],
        out_shape=[_sds((s, A_W)), _sds((s, D_MODEL), _MM), _sds((s, D_MODEL)), _sds((s, D_MODEL), _MM)],
        scratch=[pltpu.VMEM((ts, D_MODEL), F32)])(*og, *lg, o_b, o_m, gates, x, w_oa, w_ob, w_om, w_out, g2)


def _k_up(h2, w_up):
    s = h2.shape[0]
    ts = min(256, s)
    nu = w_up.shape[2]

    def body(h_ref, w_ref, u_ref):
        h = h_ref[...]
        for j in range(CHIPS):
            u_ref[:, j * nu:(j + 1) * nu] = jnp.dot(h, w_ref[j], preferred_element_type=F32)

    return _pc(body, name="up_proj", grid=(s // ts,), in_specs=[_row(ts, D_MODEL), _res(w_up.shape)],
               out_specs=[_row(ts, CHIPS * nu)], out_shape=[_sds((s, CHIPS * nu))])(h2, w_up)[0]


def _shift_down(v, halo, k):
    rolled = pltpu.roll(v, k, 0)
    row = lax.broadcasted_iota(jnp.int32, (8, v.shape[1]), 0)
    slab = rolled[0:8]
    for r in range(k):
        slab = jnp.where(row == r, halo[8 - k + r:8 - k + r + 1, :], slab)
    return jnp.concatenate([slab, rolled[8:]], axis=0)


def _shift_up(v, halo, k):
    ts = v.shape[0]
    rolled = pltpu.roll(v, ts - k, 0)
    row = lax.broadcasted_iota(jnp.int32, (8, v.shape[1]), 0)
    slab = rolled[ts - 8:]
    for r in range(k):
        slab = jnp.where(row == 8 - k + r, halo[r:r + 1, :], slab)
    return jnp.concatenate([rolled[:ts - 8], slab], axis=0)


def _k_ffn(u, conv_w, conv_b, w_down, w_down_t, x1, target):
    s = u.shape[0]
    ts = min(256, s)
    nu = conv_w.shape[2]
    half = CHIPS // 2

    def body(u_ref, uh_ref, cw_ref, cb_ref, wd_ref, wdt_ref, x1_ref, t_ref, dy_ref, f_ref, dc_ref, loss_ref, c_scr,
             f_scr, s_scr):
        i = pl.program_id(0)
        halo = jnp.where(i > 0, uh_ref[...], 0.0)
        for j in range(CHIPS):
            cs = slice(j * nu, (j + 1) * nu)
            uj = u_ref[:, cs]
            hj = halo[:, cs]
            c_scr[:, cs] = (cb_ref[:, cs] + cw_ref[j, 0:1, :] * _shift_down(uj, hj, 2)
                            + cw_ref[j, 1:2, :] * _shift_down(uj, hj, 1) + cw_ref[j, 2:3, :] * uj)
        for j in range(half):
            a = c_scr[:, j * nu:(j + 1) * nu]
            g = c_scr[:, (half + j) * nu:(half + j + 1) * nu]
            sa = _sigmoid(a)
            s_scr[:, j * nu:(j + 1) * nu] = sa
            f_scr[:, j * nu:(j + 1) * nu] = (a * sa * g).astype(_MM)
        f = f_scr[...]
        f_ref[...] = f
        y = x1_ref[...] + jnp.dot(f, wd_ref[...], preferred_element_type=F32)
        err = y - t_ref[...]
        dy = err * (1.0 / D_MODEL)
        dy_ref[...] = dy

        @pl.when(i == 0)
        def _():
            loss_ref[...] = jnp.zeros_like(loss_ref)

        loss_ref[...] += _sum8(err * err)
        df = _dot(dy, wdt_ref[...])
        for j in range(half):
            a = c_scr[:, j * nu:(j + 1) * nu]
            g = c_scr[:, (half + j) * nu:(half + j + 1) * nu]
            sa = s_scr[:, j * nu:(j + 1) * nu]
            dfj = df[:, j * nu:(j + 1) * nu]
            dc_ref[:, j * nu:(j + 1) * nu] = dfj * g * (sa * (1.0 + a * (1.0 - sa)))
            dc_ref[:, (half + j) * nu:(half + j + 1) * nu] = dfj * (a * sa)

    wide = CHIPS * nu
    return _pc(
        body, name="conv_ffn", grid=(s // ts,),
        in_specs=[_row(ts, wide), pl.BlockSpec((8, wide), lambda i: (jnp.maximum(i * (ts // 8) - 1, 0), 0)),
                  _res(conv_w.shape), _res((1, wide)), _res(w_down.shape), _res(w_down_t.shape), _row(ts, D_MODEL),
                  _row(ts, D_MODEL)],
        out_specs=[_row(ts, D_MODEL), _row(ts, D_FF), _row(ts, wide), _acc((8, D_MODEL))],
        out_shape=[_sds((s, D_MODEL)), _sds((s, D_FF), _MM), _sds((s, wide)), _sds((8, D_MODEL))],
        scratch=[pltpu.VMEM((ts, wide), F32), pltpu.VMEM((ts, D_FF), _MM), pltpu.VMEM((ts, D_FF), F32)])(
            u, u, conv_w, conv_b, w_down, w_down_t, x1, target)


def _k_conv_bwd(dc, u, conv_w, w_up, x1, g2, dy):
    s = u.shape[0]
    ts = min(256, s)
    nu = conv_w.shape[2]
    wide = CHIPS * nu
    last = s // ts - 1

    def body(dc_ref, dn_ref, u_ref, cw_ref, wu_ref, x1_ref, g_ref, dy_ref, dx1_ref, du_ref, cacc_ref, gacc_ref):
        i = pl.program_id(0)

        @pl.when(i == 0)
        def _():
            cacc_ref[...] = jnp.zeros_like(cacc_ref)
            gacc_ref[...] = jnp.zeros_like(gacc_ref)

        dhalo = jnp.where(i < last, dn_ref[...], 0.0)
        dh2 = jnp.zeros((ts, D_MODEL), F32)
        for j in range(CHIPS):
            cs = slice(j * nu, (j + 1) * nu)
            dcj, uj = dc_ref[:, cs], u_ref[:, cs]
            dc1, dc2 = _shift_up(dcj, dhalo[:, cs], 1), _shift_up(dcj, dhalo[:, cs], 2)
            cacc_ref[0, :, cs] += _sum8(dcj)
            cacc_ref[1, :, cs] += _sum8(dc2 * uj)
            cacc_ref[2, :, cs] += _sum8(dc1 * uj)
            cacc_ref[3, :, cs] += _sum8(dcj * uj)
            du = (cw_ref[j, 2:3, :] * dcj + cw_ref[j, 1:2, :] * dc1 + cw_ref[j, 0:1, :] * dc2).astype(_MM)
            du_ref[:, cs] = du
            dh2 = dh2 + _dot_nt(du, wu_ref[j])
        xh, r = _rms(x1_ref[...])
        gacc_ref[...] += _sum8(dh2 * xh)
        dx1_ref[...] = dy_ref[...] + _rms_bwd(dh2, xh, r, g_ref[...])

    return _pc(
        body, name="conv_up_bwd", grid=(s // ts,),
        in_specs=[_row(ts, wide),
                  pl.BlockSpec((8, wide), lambda i: (jnp.minimum((i + 1) * (ts // 8), s // 8 - 1), 0)),
                  _row(ts, wide), _res(conv_w.shape), _res(w_up.shape), _row(ts, D_MODEL), _res((1, D_MODEL)),
                  _row(ts, D_MODEL)],
        out_specs=[_row(ts, D_MODEL), _row(ts, wide), _acc((4, 8, wide)), _acc((8, D_MODEL))],
        out_shape=[_sds((s, D_MODEL)), _sds((s, wide), _MM), _sds((4, 8, wide)), _sds((8, D_MODEL))])(
            dc, dc, u, conv_w, w_up, x1, g2, dy)


def _k_merge_bwd(dx1, og, lg, o_a, o_b, o_m, gates, w_oa, w_ob, w_om, w_out, dep):
    s = dx1.shape[0]
    ts = min(256, s)
    nc = w_oa.shape[2]

    def body(dx_ref, o0, o1, o2, l0, l1, l2, oa_ref, ob_ref, om_ref, gt_ref, woa, wob, wom, wout, dep_ref,
             dgp_ref, dpa_ref, dpb_ref, dpm_ref, dog0, dog1, dog2, dl0, dl1, dl2, dob_ref, dom_ref, bacc_ref):
        i = pl.program_id(0)

        @pl.when(i == 0)
        def _():
            bacc_ref[...] = jnp.zeros_like(bacc_ref)

        dmer = _dot_nt(dx_ref[...], wout[...])
        oa, ob, om = oa_ref[...], ob_ref[...], om_ref[...]
        doa = jnp.zeros((ts, A_W), F32)
        dob = jnp.zeros((ts, B_QH * HEAD), F32)
        dom = jnp.zeros((ts, M_W), F32)
        for j in range(CHIPS):
            prods = _branch_products(oa, ob, om, woa, wob, wom, j)
            dmj = dmer[:, j * nc:(j + 1) * nc]
            dps = []
            for br, (p, dref) in enumerate(zip(prods, (dpa_ref, dpb_ref, dpm_ref))):
                cs = slice(br * D_MODEL + j * nc, br * D_MODEL + (j + 1) * nc)
                gt = gt_ref[:, cs]
                dgp = dmj * p * gt * (1.0 - gt)
                dgp_ref[:, cs] = dgp.astype(_MM)
                bacc_ref[:, cs] += _sum8(dgp)
                dp = (dmj * gt).astype(_MM)
                dref[:, j * nc:(j + 1) * nc] = dp
                dps.append(dp)
            doa = doa + _dot_nt(dps[0], woa[j])
            dob = dob + _dot_nt(dps[1], wob[j])
            dom = dom + _dot_nt(dps[2], wom[j])
        dob_ref[...] = dob
        dom_ref[...] = dom
        ws = _group_weights(l0[...], l1[...], l2[...])
        dsum = _seg_mean(doa * oa, HEAD) * float(HEAD)
        for w, dref, lref in zip(ws, (dog0, dog1, dog2), (dl0, dl1, dl2)):
            dref[...] = w * doa
            lref[...] = w * dsum

    return _pc(
        body, name="merge_out_bwd", grid=(s // ts,),
        in_specs=[_row(ts, D_MODEL)] + [_row(ts, A_W)] * 7 + [_row(ts, B_QH * HEAD), _row(ts, M_W), _row(ts, 3 * D_MODEL),
                                                              _res(w_oa.shape), _res(w_ob.shape), _res(w_om.shape),
                                                              _res(w_out.shape), _res((8, 128))],
        out_specs=[_row(ts, 3 * D_MODEL)] + [_row(ts, D_MODEL)] * 3 + [_row(ts, A_W)] * 6
        + [_row(ts, B_QH * HEAD), _row(ts, M_W), _acc((8, 3 * D_MODEL))],
        out_shape=[_sds((s, 3 * D_MODEL), _MM)] + [_sds((s, D_MODEL), _MM)] * 3 + [_sds((s, A_W))] * 6
        + [_sds((s, B_QH * HEAD)), _sds((s, M_W)), _sds((8, 3 * D_MODEL))])(
            dx1, *og, *lg, o_a, o_b, o_m, gates, w_oa, w_ob, w_om, w_out, dep)


def _k_mem_bwd(m_q, gq, mk, mv, o_m, do_m):
    s = m_q.shape[0]
    n = mk.shape[0]
    ts = min(512, s)
    scale = M_HD ** -0.5

    def body(q_ref, g_ref, mk_ref, mv_ref, o_ref, do_ref, dq_ref, dmk_ref, dmv_ref, gacc_ref):
        i = pl.program_id(0)

        @pl.when(i == 0)
        def _():
            dmk_ref[...] = jnp.zeros_like(dmk_ref)
            dmv_ref[...] = jnp.zeros_like(dmv_ref)
            gacc_ref[...] = jnp.zeros_like(gacc_ref)

        gain = g_ref[...]
        qh, r = _seg_norm(q_ref[...], M_HD)
        qn = (qh * gain).astype(_MM)
        do = do_ref[...]
        delta = _seg_mean(do * o_ref[...], M_HD) * float(M_HD)
        dqn = []
        for h in range(M_HEADS):
            hs = slice(h * M_HD, (h + 1) * M_HD)
            p = _mem_probs(qn[:, hs], mk_ref[:, hs])
            dp = _dot_nt(do[:, hs], mv_ref[:, hs])
            ds = (p * (dp - delta[:, hs][:, 0:1]) * scale).astype(_MM)
            dqn.append(_dot(ds, mk_ref[:, hs]))
            dmk_ref[:, hs] += _dot_tn(ds, qn[:, hs])
            dmv_ref[:, hs] += _dot_tn(p, do[:, hs])
        dqn = jnp.concatenate(dqn, axis=1)
        gacc_ref[...] += _sum8(dqn * qh)
        z = dqn * gain
        dq_ref[...] = (r * (z - qh * _seg_mean(z * qh, M_HD))).astype(_MM)

    return _pc(
        body, name="mem_attn_bwd", grid=(s // ts,),
        in_specs=[_row(ts, M_W), _res((1, M_W)), _res((n, M_W)), _res((n, M_W)), _row(ts, M_W), _row(ts, M_W)],
        out_specs=[_row(ts, M_W), _acc((n, M_W)), _acc((n, M_W)), _acc((8, M_W))],
        out_shape=[_sds((s, M_W), _MM), _sds((n, M_W)), _sds((n, M_W)), _sds((8, M_W))])(m_q, gq, mk, mv, o_m, do_m)


def _k_memkv_bwd(mem, mem_norm, w_kv, m_k_norm, mem_n, kv, dmk, dmv):
    n = mem.shape[0]

    def body(m_ref, g_ref, w_ref, gk_ref, mn_ref, kv_ref, dmk_ref, dmv_ref, dw_ref, dg_ref, dgk_ref):
        gk = gk_ref[...]
        kh, r = _seg_norm(kv_ref[:, :M_W], M_HD)
        dmk = dmk_ref[...]
        dgk_ref[...] = _sum8(dmk * kh)
        z = dmk * gk
        dk = r * (z - kh * _seg_mean(z * kh, M_HD))
        dkv = jnp.concatenate([dk, dmv_ref[...]], axis=1).astype(_MM)
        dw_ref[...] = _dot_tn(mn_ref[...], dkv)
        dmn = _dot_nt(dkv, w_ref[...])
        mh, _ = _rms(m_ref[...])
        dg_ref[...] = _sum8(dmn * mh)

    return _pc(body, name="mem_kv_bwd", grid=(1,),
               in_specs=[_acc((n, D_MODEL)), _acc((1, D_MODEL)), _acc(w_kv.shape), _acc((1, M_W)), _acc((n, D_MODEL)),
                         _acc((n, 2 * M_W)), _acc((n, M_W)), _acc((n, M_W))],
               out_specs=[_acc(w_kv.shape), _acc((8, D_MODEL)), _acc((8, M_W))],
               out_shape=[_sds(w_kv.shape), _sds((8, D_MODEL)), _sds((8, M_W))])(
                   mem, mem_norm, w_kv, m_k_norm, mem_n, kv, dmk, dmv)


def _k_band_bwd(qn, kn, vn, do, lse, dl_or_o, *, hq, hk, max_dist, segs, sink, name):
    rows = qn.shape[0]
    nb = rows // BLK
    units = hq // A_HEADS
    shared = hk != hq
    wq, wk = hq * HEAD, hk * HEAD
    scale = HEAD ** -0.5

    def body(*refs):
        (q2_ref, qx_ref, kc_ref, kp_ref, vc_ref, vp_ref, do2_ref, dox_ref, l2_ref, lx_ref, e2_ref, ex_ref) = refs[:12]
        if sink is None:
            dq_ref, dk_ref, dv_ref = refs[12:]
        else:
            sk_ref, dq_ref, dk_ref, dv_ref, sacc_ref = refs[12:]
        i = pl.program_id(0)
        thr = lambda b: jnp.where(_first_flag(b, segs, nb), 1 << 20, BLK - max_dist)
        bias_a, bias_b = _band_bias(thr(2 * i), True), _band_bias(thr(2 * i + 1), True)
        bias_c = _band_bias(thr(2 * i + 2), False)
        if sink is not None:
            @pl.when(i == 0)
            def _():
                sacc_ref[...] = jnp.zeros_like(sacc_ref)

        def tile(q4, do4, l_cols, dlt, kd, vd, bias, width):
            s, dp = _dot_nt(q4, kd) * scale, _dot_nt(do4, vd)
            ps, dss = [], []
            for h in range(A_HEADS):
                seg = slice(h * width, (h + 1) * width)
                p = jnp.exp(s[:, seg] + bias - l_cols[h])
                ps.append(p)
                dss.append(p * (dp[:, seg] - dlt[:, h * HEAD:h * HEAD + 1]) * scale)
            return ps, dss

        cat = lambda parts: jnp.concatenate([t.astype(_MM) for t in parts], axis=1)
        for u in range(units):
            us = slice(u * A_W, (u + 1) * A_W)
            k_a = _unit_kv(((kp_ref, _LO), (kc_ref, _LO)), u, shared)
            v_a = _unit_kv(((vp_ref, _LO), (vc_ref, _LO)), u, shared)
            k_b, v_b = _unit_kv(((kc_ref, _BOTH),), u, shared), _unit_kv(((vc_ref, _BOTH),), u, shared)
            kd_a, vd_a, kd_b, vd_b = _blockdiag(k_a), _blockdiag(v_a), _blockdiag(k_b), _blockdiag(v_b)
            kd_c, vd_c = _blockdiag(k_b[BLK:]), _blockdiag(v_b[BLK:])
            qs = (q2_ref[_LO, us], q2_ref[_HI, us], qx_ref[:, us])
            dos = (do2_ref[_LO, us], do2_ref[_HI, us], dox_ref[:, us])
            lcols = [[ref[rs, u * A_W + h * HEAD:u * A_W + h * HEAD + 1] for h in range(A_HEADS)]
                     for ref, rs in ((l2_ref, _LO), (l2_ref, _HI), (lx_ref, _LO))]
            if sink is None:
                dlts = (e2_ref[_LO, us], e2_ref[_HI, us], ex_ref[:, us])
            else:
                dlts = tuple(_seg_sum64(d.astype(F32) * ref[rs, us])
                             for d, (ref, rs) in zip(dos, ((e2_ref, _LO), (e2_ref, _HI), (ex_ref, _LO))))
                for t in range(2):
                    for h in range(A_HEADS):
                        j = u * A_HEADS + h
                        sacc_ref[:, j:j + 1] += -jnp.exp(sk_ref[j] - lcols[t][h]) * dlts[t][:, h * HEAD:h * HEAD + 1]
            p_a, ds_a = tile(qs[0], dos[0], lcols[0], dlts[0], kd_a, vd_a, bias_a, 2 * BLK)
            p_b, ds_b = tile(qs[1], dos[1], lcols[1], dlts[1], kd_b, vd_b, bias_b, 2 * BLK)
            p_c, ds_c = tile(qs[2], dos[2], lcols[2], dlts[2], kd_c, vd_c, bias_c, BLK)
            dq_ref[_LO, us] = _dot(cat(ds_a), kd_a)
            dq_ref[_HI, us] = _dot(cat(ds_b), kd_b)
            outs = []
            for pa, pb, pc, lhs in ((ds_a, ds_b, ds_c, qs), (p_a, p_b, p_c, dos)):
                from_a = _fold_diag(_dot_tn(cat([t[:, BLK:] for t in pa]), lhs[0]), BLK)
                from_b = _fold_diag(_dot_tn(cat(pb), lhs[1]), 2 * BLK)
                from_c = _fold_diag(_dot_tn(cat(pc), lhs[2]), BLK)
                outs.append(jnp.concatenate([from_a + from_b[:BLK], from_b[BLK:] + from_c], axis=0))
            dk4, dv4 = outs
            if shared:
                fold = lambda t: (t[:, 0:HEAD] + t[:, HEAD:2 * HEAD]) + (t[:, 2 * HEAD:3 * HEAD] + t[:, 3 * HEAD:])
                dk_ref[:, u * HEAD:(u + 1) * HEAD] = fold(dk4)
                dv_ref[:, u * HEAD:(u + 1) * HEAD] = fold(dv4).astype(_MM)
            else:
                dk_ref[:, us] = dk4
                dv_ref[:, us] = dv4.astype(_MM)

    two = lambda w: pl.BlockSpec((2 * BLK, w), lambda i: (i, 0))
    prev = lambda w: pl.BlockSpec((BLK, w), lambda i: (jnp.maximum(2 * i - 1, 0), 0))
    nxt = lambda w: pl.BlockSpec((BLK, w), lambda i: (jnp.minimum(2 * i + 2, nb - 1), 0))
    in_specs = [two(wq), nxt(wq), two(wk), prev(wk), two(wk), prev(wk), two(wq), nxt(wq), two(wq), nxt(wq), two(wq), nxt(wq)]
    args = [qn, qn, kn, kn, vn, vn, do, do, lse, lse, dl_or_o, dl_or_o]
    out_specs = [two(wq), two(wk), two(wk)]
    out_shape = [_sds((rows, wq)), _sds((rows, wk)), _sds((rows, wk), _MM)]
    if sink is not None:
        in_specs.append(pl.BlockSpec(memory_space=pltpu.SMEM))
        args.append(sink)
        out_specs.append(_acc((BLK, 128)))
        out_shape.append(_sds((BLK, 128)))
    return _pc(body, name=name, grid=(nb // 2,), in_specs=in_specs, out_specs=out_specs, out_shape=out_shape)(*args)


def _k_prep_bwd(srcs, dqn, dkn, gq, gk, tabs, tab_row, *, wq, wk, rows_per_gain, name):
    rows = dqn.shape[0]
    ts = min(512, rows)
    ngain = gq.shape[0]

    def body(q_ref, k_ref, dq_ref, dk_ref, gq_ref, gk_ref, c_ref, sa_ref, sb_ref, oq_ref, ok_ref, aq_ref, ak_ref):
        i = pl.program_id(0)

        @pl.when(lax.rem(i * ts, rows_per_gain) == 0)
        def _():
            aq_ref[...] = jnp.zeros_like(aq_ref)
            ak_ref[...] = jnp.zeros_like(ak_ref)

        c, sa, sb = c_ref[...], sa_ref[...], sb_ref[...]
        for x_ref, d_ref, g_ref, o_ref, a_ref in ((q_ref, dq_ref, gq_ref, oq_ref, aq_ref),
                                                   (k_ref, dk_ref, gk_ref, ok_ref, ak_ref)):
            xh, r = _seg_norm(x_ref[...], HEAD)
            dt = _rope_bwd(d_ref[...], c, sa, sb)
            a_ref[...] += _sum8(dt * xh)
            z = dt * g_ref[...]
            o_ref[...] = (r * (z - xh * _seg_mean(z * xh, HEAD))).astype(_MM)

    gspec = lambda w: pl.BlockSpec((None, 1, w), lambda i: ((i * ts) // rows_per_gain, 0, 0))
    aspec = lambda w: pl.BlockSpec((None, 8, w), lambda i: ((i * ts) // rows_per_gain, 0, 0))
    return _pc(
        body, name=name, grid=(rows // ts,),
        in_specs=[_row(ts, wq, srcs[0][1]), _row(ts, wk, srcs[1][1]), _row(ts, wq), _row(ts, wk), gspec(wq), gspec(wk)]
        + [pl.BlockSpec((ts, 128), lambda i: (i + tab_row // ts, 0))] * 3,
        out_specs=[_row(ts, wq), _row(ts, wk), aspec(wq), aspec(wk)],
        out_shape=[_sds((rows, wq), _MM), _sds((rows, wk), _MM), _sds((ngain, 8, wq)), _sds((ngain, 8, wk))])(
            srcs[0][0], srcs[1][0], dqn, dkn, gq, gk, *tabs)


def _k_in_bwd(pieces, dgp, x, g1, dx1, w_in, w_gate):
    s = x.shape[0]
    ts = min(256, s)
    nin, ng = w_in.shape[2], w_gate.shape[2]
    widths = [p.shape[1] for p in pieces]
    ncol = sum(widths)

    def body(*refs):
        p_refs = refs[:len(pieces)]
        dgp_ref, x_ref, g_ref, dx1_ref, wi_ref, wg_ref, gx_ref, dpj_ref, gacc_ref = refs[len(pieces):]
        i = pl.program_id(0)

        @pl.when(i == 0)
        def _():
            gacc_ref[...] = jnp.zeros_like(gacc_ref)

        off = 0
        for p_ref, w in zip(p_refs, widths):
            dpj_ref[:, off:off + w] = p_ref[...]
            off += w
        dh = jnp.zeros((ts, D_MODEL), F32)
        for j in range(CHIPS):
            dh = dh + _dot_nt(dpj_ref[:, j * nin:(j + 1) * nin], wi_ref[j])
            dh = dh + _dot_nt(dgp_ref[:, j * ng:(j + 1) * ng], wg_ref[j])
        xh, r = _rms(x_ref[...])
        gacc_ref[...] += _sum8(dh * xh)
        gx_ref[...] = dx1_ref[...] + _rms_bwd(dh, xh, r, g_ref[...])

    return _pc(
        body, name="in_proj_bwd", grid=(s // ts,),
        in_specs=[_row(ts, w) for w in widths] + [_row(ts, CHIPS * ng), _row(ts, D_MODEL), _res((1, D_MODEL)),
                                                  _row(ts, D_MODEL), _res(w_in.shape), _res(w_gate.shape)],
        out_specs=[_row(ts, D_MODEL), _row(ts, ncol), _acc((8, D_MODEL))],
        out_shape=[_sds((s, D_MODEL)), _sds((s, ncol), _MM), _sds((8, D_MODEL))])(*pieces, dgp, x, g1, dx1, w_in, w_gate)


def _k_wgrad(a, b, *, nblk, stacked, name):
    s, k = a.shape
    n = b.shape[1]
    nb = n // nblk
    ts = min(2048 if k <= 1024 else 1024, s)

    def body(a_ref, b_ref, o_ref):
        @pl.when(pl.program_id(1) == 0)
        def _():
            o_ref[...] = jnp.zeros_like(o_ref)

        o_ref[...] += _dot_tn(a_ref[...], b_ref[...])

    if stacked:
        out_spec, out_shape = pl.BlockSpec((None, k, nb), lambda g, t: (g, 0, 0)), _sds((nblk, k, nb))
    else:
        out_spec, out_shape = pl.BlockSpec((k, nb), lambda g, t: (0, g)), _sds((k, n))
    return _pc(body, name=name, grid=(nblk, s // ts),
               in_specs=[pl.BlockSpec((ts, k), lambda g, t: (t, 0)), pl.BlockSpec((ts, nb), lambda g, t: (t, g))],
               out_specs=[out_spec], out_shape=[out_shape])(a, b)[0]


def _to_res(t, d):
    s, c = t.shape
    return t if d == 1 else t.reshape(s // d, d, c).transpose(1, 0, 2).reshape(s, c)


def _from_res(t, d):
    s, c = t.shape
    return t if d == 1 else t.reshape(d, s // d, c).transpose(1, 0, 2).reshape(s, c)


def _tile_gain(g, heads):
    return jnp.tile(g, (1,) * (g.ndim - 1) + (heads,))[..., None, :]


def _local_step(x, mem, pos, target, small, get_w_in, get_rest, on_grads):
    s = x.shape[0]
    nblk = s // BLK
    g1, g2 = small["attn_norm"], small["ffn_norm"]

    pos_rows = jnp.concatenate([_to_res(pos[:, None], d)[:, 0] for _, d in A_GROUPS] + [pos])
    tabs = _rope_tables(pos_rows)
    w_in = get_w_in(tabs[0])

    h, qa0, qa1, qa2, q_b, k_b, v_b, m_q = _k_in(x, g1, w_in)

    qkv_a = jnp.concatenate([_to_res(t, d) for t, (_, d) in zip((qa0, qa1, qa2), A_GROUPS)], axis=0)
    gq_a = _tile_gain(small["a_q_norm"], A_HEADS)
    gk_a = _tile_gain(small["a_k_norm"], A_HEADS)
    src_a = ((qkv_a, 0), (qkv_a, 1), (qkv_a, 2))
    qn_a, kn_a, vn_a = _k_prep(src_a, gq_a, gk_a, tabs, 0, wq=A_W, wk=A_W, rows_per_gain=s, name="prep_a")
    segs_a = tuple((gi * nblk, nblk // d) for gi, (_, d) in enumerate(A_GROUPS))
    o_res, l_res = _k_band_fwd(qn_a, kn_a, vn_a, hq=A_HEADS, hk=A_HEADS, max_dist=BLK, segs=segs_a, sink=None,
                               name="attn_a")
    og = [_from_res(o_res[gi * s:(gi + 1) * s], d) for gi, (_, d) in enumerate(A_GROUPS)]
    lg = [_from_res(l_res[gi * s:(gi + 1) * s], d) for gi, (_, d) in enumerate(A_GROUPS)]

    gq_b = _tile_gain(small["b_q_norm"], B_QH)
    gk_b = _tile_gain(small["b_k_norm"], B_KVH)
    src_b = ((q_b, 0), (k_b, 0), (v_b, 0))
    qn_b, kn_b, vn_b = _k_prep(src_b, gq_b, gk_b, tabs, 3 * s, wq=B_QH * HEAD, wk=B_KVH * HEAD, rows_per_gain=s,
                               name="prep_b")
    sink_x = small["b_sinks"][0]
    segs_b = ((0, nblk),)
    o_b, l_b = _k_band_fwd(qn_b, kn_b, vn_b, hq=B_QH, hk=B_KVH, max_dist=B_WINDOW - 1, segs=segs_b, sink=sink_x,
                           name="attn_b")

    wts = get_rest(0, o_b)
    gates = _k_gate(h, wts["w_gate"], small["b_gate"])

    gq_m = _tile_gain(small["m_q_norm"], M_HEADS)[0]
    gk_m = _tile_gain(small["m_k_norm"], M_HEADS)[0]
    mem_n, kv, mk, mv = _k_memkv(mem, small["mem_norm"], wts["w_mem_kv"], gk_m)
    o_m = _k_mem_fwd(m_q, gq_m, mk, mv)

    o_a, merged, x1, h2 = _k_merge(og, lg, o_b, o_m, gates, x, wts["w_o_a"], wts["w_o_b"], wts["w_o_m"],
                                   wts["w_out"], g2)
    wts.update(get_rest(1, x1))
    u = _k_up(h2, wts["w_up"])
    dy, f, dc, loss_acc = _k_ffn(u, wts["conv_w"], small["conv_b"], wts["w_down"], wts["w_down"].T, x1, target)
    loss = (0.5 / D_MODEL) * jnp.sum(loss_acc)

    dx1, du, cacc, g2acc = _k_conv_bwd(dc, u, wts["conv_w"], wts["w_up"], x1, g2, dy)
    tok = on_grads({"w_up": _k_wgrad(h2, du, nblk=CHIPS, stacked=True, name="dw_up"),
                    "w_down": _k_wgrad(f, dy, nblk=2, stacked=False, name="dw_down").reshape(CHIPS, -1, D_MODEL)}, dx1)
    (dgp, dp_a, dp_b, dp_m, dog0, dog1, dog2, dl0, dl1, dl2, do_b, do_m, bacc) = _k_merge_bwd(
        dx1, og, lg, o_a, o_b, o_m, gates, wts["w_o_a"], wts["w_o_b"], wts["w_o_m"], wts["w_out"], tok)
    tok = on_grads({"w_gate": _k_wgrad(h, dgp, nblk=CHIPS, stacked=True, name="dw_gate"),
                    "w_o_a": _k_wgrad(o_a, dp_a, nblk=CHIPS, stacked=True, name="dw_o_a"),
                    "w_o_b": _k_wgrad(o_b, dp_b, nblk=CHIPS, stacked=True, name="dw_o_b"),
                    "w_o_m": _k_wgrad(o_m, dp_m, nblk=CHIPS, stacked=True, name="dw_o_m"),
                    "w_out": _k_wgrad(merged, dx1, nblk=1, stacked=False, name="dw_out").reshape(CHIPS, -1, D_MODEL)},
                   do_m)

    dq_m, dmk, dmv, gqm_acc = _k_mem_bwd(m_q, gq_m + tok[0:1, 0:1], mk, mv, o_m, do_m)
    dw_kv, gmem_acc, gkm_acc = _k_memkv_bwd(mem, small["mem_norm"], wts["w_mem_kv"], gk_m, mem_n, kv, dmk, dmv)

    dq_bn, dk_bn, dv_b, sacc = _k_band_bwd(qn_b, kn_b, vn_b, do_b, l_b, o_b, hq=B_QH, hk=B_KVH,
                                           max_dist=B_WINDOW - 1, segs=segs_b, sink=sink_x, name="attn_b_bwd")
    tok = on_grads({}, dq_bn)
    dq_b, dk_b, gqb_acc, gkb_acc = _k_prep_bwd(src_b, dq_bn, dk_bn, gq_b + tok[0:1, 0:1], gk_b, tabs, 3 * s, wq=B_QH * HEAD,
                                               wk=B_KVH * HEAD, rows_per_gain=s, name="prep_b_bwd")

    do_res = jnp.concatenate([_to_res(t, d) for t, (_, d) in zip((dog0, dog1, dog2), A_GROUPS)], axis=0)
    dl_res = jnp.concatenate([_to_res(t, d) for t, (_, d) in zip((dl0, dl1, dl2), A_GROUPS)], axis=0)
    dq_an, dk_an, dv_a = _k_band_bwd(qn_a, kn_a, vn_a, do_res, l_res, dl_res, hq=A_HEADS, hk=A_HEADS, max_dist=BLK,
                                     segs=segs_a, sink=None, name="attn_a_bwd")
    dq_a, dk_a, gqa_acc, gka_acc = _k_prep_bwd(src_a, dq_an, dk_an, gq_a, gk_a, tabs, 0, wq=A_W, wk=A_W,
                                               rows_per_gain=s, name="prep_a_bwd")
    pieces = []
    for gi, (_, d) in enumerate(A_GROUPS):
        rs = slice(gi * s, (gi + 1) * s)
        pieces += [_from_res(t[rs], d) for t in (dq_a, dk_a, dv_a)]
    pieces += [dq_b, dk_b, dv_b, dq_m]
    grad_x, dproj, g1acc = _k_in_bwd(pieces, dgp, x, g1, dx1, w_in, wts["w_gate"])
    on_grads({"w_in": _k_wgrad(h, dproj, nblk=CHIPS, stacked=True, name="dw_in"),
              "w_mem_kv": dw_kv.reshape(CHIPS, -1, 2 * M_W)}, grad_x)

    def fold(acc, heads):
        v = jnp.sum(acc, axis=-2)
        return jnp.sum(v.reshape(v.shape[:-1] + (heads, -1)), axis=-2)

    csum = jnp.sum(cacc, axis=1)
    sml = {
        "attn_norm": jnp.sum(g1acc, axis=0), "a_q_norm": fold(gqa_acc, A_HEADS), "a_k_norm": fold(gka_acc, A_HEADS),
        "b_q_norm": fold(gqb_acc[0], B_QH), "b_k_norm": fold(gkb_acc[0], B_KVH),
        "b_sinks": jnp.sum(sacc, axis=0)[:B_QH], "mem_norm": jnp.sum(gmem_acc, axis=0),
        "m_q_norm": fold(gqm_acc, M_HEADS), "m_k_norm": fold(gkm_acc, M_HEADS),
        "b_gate": jnp.sum(bacc, axis=0), "ffn_norm": jnp.sum(g2acc, axis=0),
        "conv_w": csum[1:], "conv_b": csum[0],
    }
    return loss, grad_x, sml


def _mesh_pos():
    return lax.axis_index("x"), lax.axis_index("y"), lax.axis_index("c")


def _chip_peers(x, y):
    return [(1 - x, y), (x, 1 - y), (1 - x, 1 - y)]


_ANY = pl.BlockSpec(memory_space=pl.ANY)


def _comm_call(body, *, name, n_in, out_shape, scratch):
    return pl.pallas_call(body, name=name, in_specs=[_ANY] * n_in, out_specs=[_ANY] * len(out_shape),
                          out_shape=out_shape, scratch_shapes=scratch)


def _remote(src, dst, send_sem, recv_sem, dev):
    return pltpu.make_async_remote_copy(src_ref=src, dst_ref=dst, send_sem=send_sem, recv_sem=recv_sem,
                                        device_id=dev, device_id_type=MESH)


def _pair_join(halves, name):
    nt = len(halves)

    def body(*refs):
        ins, got = refs[:nt], refs[nt:2 * nt]
        send_sems, recv_sems = refs[2 * nt:]
        x, y, c = _mesh_pos()
        cps = []
        for t in range(nt):
            rc = _remote(ins[t], got[t], send_sems.at[t], recv_sems.at[t], (x, y, 1 - c))
            rc.start()
            cps.append(rc)
        for rc in cps:
            rc.wait()

    out_shape = [_sds(hf.shape, hf.dtype) for hf in halves]
    scratch = [pltpu.SemaphoreType.DMA((nt,)), pltpu.SemaphoreType.DMA((nt,))]
    return _comm_call(body, name=name, n_in=nt, out_shape=out_shape, scratch=scratch)(*halves)


_HBM = pl.BlockSpec(memory_space=pltpu.HBM)
_SEMS = pl.BlockSpec(memory_space=pltpu.SEMAPHORE)
_EFFECT = pltpu.SideEffectType.DATAFLOW_SIDE_EFFECTING


def _bcast_copies(ins, lands, send_sems, recv_sems):
    x, y, c = _mesh_pos()
    me = 2 * x + y
    targets = [((px, py, c), 2 * px + py) for px, py in _chip_peers(x, y)] + [((x, y, 1 - c), me)]
    out = []
    for t in range(len(ins)):
        for k, (dev, idx) in enumerate(targets):
            i = t * len(targets) + k
            arrival = lambda t=t, i=i, idx=idx, dev=dev: _remote(ins[t], lands[t].at[idx], send_sems.at[i],
                                                                 recv_sems.at[i], dev)
            out.append((_remote(ins[t], lands[t].at[me], send_sems.at[i], recv_sems.at[i], dev), arrival))
    return out


def _scatter_copies(ins, lands, send_sems, recv_sems):
    x, y, c = _mesh_pos()
    out = []
    for t in range(len(ins)):
        for k, (px, py) in enumerate(_chip_peers(x, y)):
            i = t * 3 + k
            cp = _remote(ins[t].at[2 * px + py], lands[t].at[k], send_sems.at[i], recv_sems.at[i], (px, py, c))
            out.append((cp, lambda cp=cp: cp))
    return out


def _pair_copies(ins, lands, send_sems, recv_sems):
    x, y, c = _mesh_pos()
    out = []
    for t in range(len(ins)):
        hr = ins[t].shape[1] // 2
        give = ins[t].at[:, pl.ds(pl.multiple_of((1 - c) * hr, 8), hr), :]
        cp = _remote(give, lands[t], send_sems.at[t], recv_sems.at[t], (x, y, 1 - c))
        out.append((cp, lambda cp=cp: cp))
    return out


def _half_copies(ins, lands, send_sems, recv_sems):
    x, y, c = _mesh_pos()
    me = 2 * x + y
    out = []
    for t in range(len(ins)):
        hr = ins[t].shape[0] // 2
        rows = pl.ds(pl.multiple_of(c * hr, 8), hr)
        for k, (px, py) in enumerate(_chip_peers(x, y)):
            i = t * 3 + k
            arrival = lambda t=t, i=i, px=px, py=py, rows=rows: _remote(
                ins[t].at[rows, :], lands[t].at[2 * px + py].at[rows, :], send_sems.at[i], recv_sems.at[i], (px, py, c))
            out.append((_remote(ins[t].at[rows, :], lands[t].at[me].at[rows, :], send_sems.at[i], recv_sems.at[i],
                                (px, py, c)), arrival))
    return out


def _finish_halves(shards, stacks):
    nt = len(shards)

    def body(*refs):
        ins, held, outs = refs[:nt], refs[nt:2 * nt], refs[2 * nt:3 * nt]
        fwd_s, fwd_r, own_s, own_r = refs[3 * nt:]
        x, y, c = _mesh_pos()
        me = 2 * x + y
        sib = (x, y, 1 - c)
        pending = []
        for t in range(nt):
            hr = shards[t].shape[0] // 2
            half = lambda ref, who: ref.at[pl.ds(pl.multiple_of(who * hr, 8), hr), :]
            own = _remote(ins[t], outs[t].at[me], own_s.at[t], own_r.at[t], sib)
            own.start()
            pending.append(own.wait)
            for k, (px, py) in enumerate(_chip_peers(x, y)):
                pj = 2 * px + py
                fw = _remote(half(held[t].at[pj], c), half(outs[t].at[pj], c), fwd_s.at[t, k], fwd_r.at[t, k], sib)
                fw.start()
                pending.append(fw.wait_send)
                other = half(outs[t].at[pj], 1 - c)
                pending.append(_remote(other, other, fwd_s.at[t, k], fwd_r.at[t, k], sib).wait_recv)
        for wait in pending:
            wait()

    dma = pltpu.SemaphoreType.DMA
    return pl.pallas_call(
        body, name="gather_w_in_finish", in_specs=[_ANY] * (2 * nt), out_specs=[_ANY] * nt,
        out_shape=[_sds(a.shape, a.dtype) for a in stacks], input_output_aliases={nt + i: i for i in range(nt)},
        scratch_shapes=[dma((nt, 3)), dma((nt, 3)), dma((nt,)), dma((nt,))])(*shards, *stacks)


def _split_start(copies, srcs, land_shapes, ncopy, dep, name):
    nt = len(srcs)

    def body(*refs):
        ins, lands = refs[:nt], refs[nt:2 * nt]
        send_sems, recv_sems, token = refs[2 * nt + 1], refs[2 * nt + 2], refs[-1]
        for send, _ in copies(ins, lands, send_sems, recv_sems):
            send.start()
        token[...] = jnp.zeros_like(token)

    lands = [pltpu.with_memory_space_constraint(lax.empty(sh, a.dtype), pltpu.HBM) for sh, a in zip(land_shapes, srcs)]
    srcs = [pltpu.with_memory_space_constraint(a, pltpu.HBM) for a in srcs]
    dma = pltpu.SemaphoreType.DMA
    out_shape = ([dma((nt * ncopy,)), dma((nt * ncopy,))] + [pltpu.HBM(a.shape, a.dtype) for a in srcs + lands]
                 + [_sds((8, 128))])
    outs = pl.pallas_call(
        body, name=name, in_specs=[_HBM] * (2 * nt) + [_ANY],
        out_specs=[_SEMS, _SEMS] + [_HBM] * (2 * nt) + [pl.BlockSpec(memory_space=pltpu.VMEM)], out_shape=out_shape,
        input_output_aliases={i: 2 + i for i in range(2 * nt)},
        compiler_params=pltpu.CompilerParams(has_side_effects=_EFFECT))(*srcs, *lands, dep)
    return outs[0], outs[1], outs[2:2 + nt], outs[2 + nt:2 + 2 * nt], outs[-1]


def _split_wait(copies, send_sems, recv_sems, srcs, lands, after, name):
    nt = len(srcs)

    def body(*refs):
        ins, lnd = refs[:nt], refs[nt:2 * nt]
        for send, arrival in copies(ins, lnd, refs[2 * nt], refs[2 * nt + 1]):
            send.wait_send()
            arrival().wait_recv()

    outs = pl.pallas_call(
        body, name=name, in_specs=[_HBM] * (2 * nt) + [_SEMS, _SEMS, _ANY], out_specs=[_HBM] * (2 * nt),
        out_shape=[pltpu.HBM(a.shape, a.dtype) for a in list(srcs) + list(lands)],
        input_output_aliases={i: i for i in range(2 * nt)},
        compiler_params=pltpu.CompilerParams(has_side_effects=_EFFECT))(*srcs, *lands, send_sems, recv_sems, after)
    return outs[:nt], outs[nt:]


def _gather_small(packed):
    n = packed.shape[0]

    def body(in_ref, out_ref, send_sems, recv_sems, loc_sem):
        x, y, c = _mesh_pos()
        me = 4 * x + 2 * y + c
        lc = pltpu.make_async_copy(in_ref, out_ref.at[me], loc_sem)
        lc.start()
        peers = []
        for k in range(1, NDEV):
            px, py, pc = x ^ (k >> 2), y ^ ((k >> 1) & 1), c ^ (k & 1)
            rc = pltpu.make_async_remote_copy(src_ref=in_ref, dst_ref=out_ref.at[me], send_sem=send_sems.at[k - 1],
                                              recv_sem=recv_sems.at[k - 1], device_id=(px, py, pc), device_id_type=MESH)
            rc.start()
            peers.append((k, px, py, pc))
        lc.wait()
        for k, px, py, pc in peers:
            pltpu.make_async_remote_copy(src_ref=in_ref, dst_ref=out_ref.at[4 * px + 2 * py + pc],
                                         send_sem=send_sems.at[k - 1], recv_sem=recv_sems.at[k - 1],
                                         device_id=(px, py, pc), device_id_type=MESH).wait()

    scratch = [pltpu.SemaphoreType.DMA((NDEV - 1,)), pltpu.SemaphoreType.DMA((NDEV - 1,)), pltpu.SemaphoreType.DMA]
    return _comm_call(body, name="gather_small_grads", n_in=1, out_shape=[_sds((NDEV, n, 128))],
                      scratch=scratch)(packed)[0]


def _row_tile(r, c):
    t = r
    while t * c * 4 > (1 << 20) and t % 16 == 0:
        t //= 2
    return t


def _k_pair_add(full, got, name):
    g, r, c = full.shape
    hr = r // 2
    tr = _row_tile(hr, c)
    nh = hr // tr

    def body(a_ref, b_ref, o_ref):
        o_ref[...] = (a_ref[...] + b_ref[...]).astype(_WIRE)

    mine = pl.BlockSpec((None, tr, c), lambda i, j: (i, lax.axis_index("c") * nh + j, 0))
    spec = pl.BlockSpec((None, tr, c), lambda i, j: (i, j, 0))
    return _pc(body, name=name, grid=(g, nh), in_specs=[mine, spec], out_specs=[spec],
               out_shape=[_sds((g, hr, c), _WIRE)])(full, got)[0]


def _k_chip_sum(parts, slots, name):
    _, r, c = parts.shape
    tr = _row_tile(r, c)

    def body(a_ref, s_ref, o_ref):
        acc = a_ref[...].astype(F32)
        for k in range(3):
            acc = acc + s_ref[k].astype(F32)
        o_ref[...] = acc

    own = pl.BlockSpec((None, tr, c), lambda i: (2 * lax.axis_index("x") + lax.axis_index("y"), i, 0))
    return _pc(body, name=name, grid=(r // tr,), in_specs=[own, pl.BlockSpec((3, tr, c), lambda i: (0, i, 0))],
               out_specs=[_row(tr, c)], out_shape=[_sds((r, c))])(parts, slots)[0]


def _adam(w, g, m, v):
    m = ADAM_B1 * m + (1.0 - ADAM_B1) * g
    v = ADAM_B2 * v + (1.0 - ADAM_B2) * (g * g)
    m_hat = m / (1.0 - ADAM_B1 ** ADAM_STEP)
    v_hat = v / (1.0 - ADAM_B2 ** ADAM_STEP)
    return -ADAM_LR * (m_hat / (jnp.sqrt(v_hat) + ADAM_EPS) + ADAM_WD * w), m, v


def _k_adam(w, mine, theirs, m, v, dep, name):
    r, c = w.shape
    hr = r // 2
    tr = _row_tile(hr, c)
    nh = hr // tr

    def body(w_ref, a_ref, b_ref, m_ref, v_ref, dep_ref, g_ref, d_ref, mo_ref, vo_ref):
        upper = (pl.program_id(0) >= nh).astype(jnp.int32)
        g = jnp.where(upper == lax.axis_index("c"), a_ref[...], b_ref[...])
        g_ref[...] = g
        d_ref[...], mo_ref[...], vo_ref[...] = _adam(w_ref[...], g, m_ref[...], v_ref[...])

    hspec = pl.BlockSpec((tr, c), lambda i: (jnp.where(i >= nh, i - nh, i), 0))
    return _pc(body, name=name, grid=(r // tr,),
               in_specs=[_row(tr, c), hspec, hspec, _row(tr, c), _row(tr, c), _res((8, 128))],
               out_specs=[_row(tr, c)] * 4, out_shape=[_sds((r, c))] * 4)(w, mine, theirs, m, v, dep)


def _k_sum8(a):
    _, n, _ = a.shape

    def body(a_ref, o_ref):
        acc = a_ref[0]
        for k in range(1, NDEV):
            acc = acc + a_ref[k]
        o_ref[...] = acc

    return _pc(body, name="sum_small_grads", grid=(1,), in_specs=[_acc(a.shape)], out_specs=[_acc((n, 128))],
               out_shape=[_sds((n, 128))])(a)[0]


def _k_adam_small(ws, gs, ms, vs):
    n = len(ws)

    def body(*refs):
        for k in range(n):
            w_ref, g_ref, m_ref, v_ref, d_ref, mo_ref, vo_ref = refs[k::n]
            d_ref[...], mo_ref[...], vo_ref[...] = _adam(w_ref[...], g_ref[...], m_ref[...], v_ref[...])

    specs = [_acc(a.shape) for a in ws]
    outs = _pc(body, name="adam_small", grid=(1,), in_specs=specs * 4, out_specs=specs * 3,
               out_shape=[_sds(a.shape) for a in ws] * 3)(*ws, *gs, *ms, *vs)
    return outs[:n], outs[n:2 * n], outs[2 * n:]


def _pack(vals):
    rows = []
    for a in vals:
        flat = a.reshape(-1)
        n = -(-flat.shape[0] // 1024) * 1024
        rows.append(jnp.pad(flat, (0, n - flat.shape[0])).reshape(n // 128, 128))
    return jnp.concatenate(rows, axis=0)


def _unpack(packed, shapes):
    out, off = [], 0
    for sh in shapes:
        size = int(np.prod(sh))
        n = -(-size // 1024) * 1024
        out.append(packed[off // 128:(off + n) // 128].reshape(-1)[:size].reshape(sh))
        off += n
    return out


_WEIGHTS = ["attn_norm", "w_in", "a_q_norm", "a_k_norm", "b_q_norm", "b_k_norm", "b_sinks", "mem_norm", "w_mem_kv",
            "m_q_norm", "m_k_norm", "w_o_a", "w_o_b", "w_o_m", "w_gate", "b_gate", "w_out", "ffn_norm", "w_up",
            "conv_w", "conv_b", "w_down"]
_BIG = ["w_in", "w_mem_kv", "w_o_a", "w_o_b", "w_o_m", "w_gate", "w_out", "w_up", "w_down"]
_SMALL = [n for n in _WEIGHTS if n not in _BIG]


def kernel(x, mem, positions, attn_norm, w_in, a_q_norm, a_k_norm, b_q_norm, b_k_norm, b_sinks, mem_norm, w_mem_kv, m_q_norm, m_k_norm, w_o_a, w_o_b, w_o_m, w_gate, b_gate, w_out, ffn_norm, w_up, conv_w, conv_b, w_down, loss_target, m_attn_norm, m_w_in, m_a_q_norm, m_a_k_norm, m_b_q_norm, m_b_k_norm, m_b_sinks, m_mem_norm, m_w_mem_kv, m_m_q_norm, m_m_k_norm, m_w_o_a, m_w_o_b, m_w_o_m, m_w_gate, m_b_gate, m_w_out, m_ffn_norm, m_w_up, m_conv_w, m_conv_b, m_w_down, v_attn_norm, v_w_in, v_a_q_norm, v_a_k_norm, v_b_q_norm, v_b_k_norm, v_b_sinks, v_mem_norm, v_w_mem_kv, v_m_q_norm, v_m_k_norm, v_w_o_a, v_w_o_b, v_w_o_m, v_w_gate, v_b_gate, v_w_out, v_ffn_norm, v_w_up, v_conv_w, v_conv_b, v_w_down):
    given = dict(locals())
    w = {n: given[n][0] for n in _WEIGHTS}
    m1 = {n: given["m_" + n][0] for n in _WEIGHTS}
    m2 = {n: given["v_" + n][0] for n in _WEIGHTS}

    zeros = jnp.zeros((8, 128), F32)
    w_in_shard = w["w_in"].astype(_MM)
    *w_in_handles, tok = _split_start(_half_copies, [w_in_shard], [(CHIPS,) + w_in_shard.shape], 3, zeros,
                                      "gather_w_in_start")

    def get_w_in(after):
        send, recv, srcs, lands = w_in_handles
        srcs, lands = _split_wait(_half_copies, send, recv, srcs, lands, after, "gather_w_in_wait")
        return _finish_halves(srcs, lands)[0]

    stages = (["w_gate", "w_mem_kv", "w_o_a", "w_o_b", "w_o_m", "w_out"], ["w_up", "w_down", "conv_w"])
    started = []
    for k, names in enumerate(stages):
        shards = [w[n] if n == "conv_w" else w[n].astype(_MM) for n in names]
        *handles, tok = _split_start(_bcast_copies, shards, [(CHIPS,) + a.shape for a in shards], 4, tok,
                                     "gather_start_%d" % k)
        started.append(handles)
    small = {n: (w[n][None, :] if w[n].ndim == 1 else w[n]) for n in _SMALL if n != "conv_w"}
    positions = positions + tok[0:1, 0:1].astype(positions.dtype)

    def get_rest(stage, after):
        send, recv, srcs, lands = started[stage]
        got = _split_wait(_bcast_copies, send, recv, srcs, lands, after, "gather_wait_%d" % stage)[1]
        wts = dict(zip(stages[stage], got))
        for n in ("w_mem_kv", "w_out", "w_down"):
            if n in wts:
                wts[n] = wts[n].reshape(-1, wts[n].shape[-1])
        return wts

    parts, slots, pair, scat = {}, {}, [], []

    def finish_pair(after):
        names, tag, send, recv, srcs, lands = pair.pop()
        full, got = _split_wait(_pair_copies, send, recv, srcs, lands, after, "pair_wait_" + tag)
        mine = [_k_pair_add(f, b, "pair_add_" + n) for n, f, b in zip(names, full, got)]
        shapes = [(3,) + p.shape[1:] for p in mine]
        send, recv, srcs, lands, token = _split_start(_scatter_copies, mine, shapes, 3, zeros, "scatter_start_" + tag)
        scat.append((names, tag, send, recv, srcs, lands))
        return token

    def on_grads(group, after):
        names = list(group)
        tag = "_".join(names)
        token = finish_pair(after) if pair else zeros
        if not group:
            return token
        grads_g = [group[n] for n in names]
        shapes = [(CHIPS, g.shape[1] // 2, g.shape[2]) for g in grads_g]
        send, recv, srcs, lands, token = _split_start(_pair_copies, grads_g, shapes, 1, token, "pair_start_" + tag)
        pair.append((names, tag, send, recv, srcs, lands))
        return token

    loss, grad_x, sml = _local_step(x[0], mem[0], positions[0], loss_target[0], small, get_w_in, get_rest, on_grads)
    loss = lax.psum(loss, ("x", "y", "c"))
    early = [n for names, *_ in scat for n in names]
    for names, tag, send, recv, srcs, lands in scat:
        mine, got = _split_wait(_scatter_copies, send, recv, srcs, lands, grad_x, "scatter_wait_" + tag)
        parts.update(zip(names, mine))
        slots.update(zip(names, got))
    scat.clear()
    reduced = {n: _k_chip_sum(parts[n], slots[n], "chip_add_" + n) for n in early}
    theirs = dict(zip(early, _pair_join([reduced[n] for n in early], "grad_pair_join_early")))
    grads = {}

    shapes = [sml[n].shape for n in _SMALL]
    gsm = dict(zip(_SMALL, _unpack(_k_sum8(_gather_small(_pack([sml[n] for n in _SMALL]))), shapes)))
    nu = w["conv_w"].shape[1]
    chip = 2 * lax.axis_index("x") + lax.axis_index("y")
    gsm["conv_w"] = lax.dynamic_slice_in_dim(gsm["conv_w"], chip * nu, nu, axis=1)
    for n in _SMALL:
        grads[n] = gsm[n].reshape(w[n].shape)

    delta, new_m, new_v = {}, {}, {}
    dep = zeros
    for k, n in enumerate(early):
        grads[n], delta[n], new_m[n], new_v[n] = _k_adam(w[n], reduced[n], theirs[n], m1[n], m2[n], dep, "adam_" + n)
        dep = finish_pair(delta[n]) if k == 0 else delta[n]
    as2d = lambda d: [d[n][None, :] if d[n].ndim == 1 else d[n] for n in _SMALL]
    for dst, outs in zip((delta, new_m, new_v), _k_adam_small(as2d(w), as2d(grads), as2d(m1), as2d(m2))):
        dst.update((n, a.reshape(w[n].shape)) for n, a in zip(_SMALL, outs))
    late, tag, send, recv, srcs, lands = scat.pop()
    mine, got = _split_wait(_scatter_copies, send, recv, srcs, lands, dep, "scatter_wait_" + tag)
    for n, a, b in zip(late, mine, got):
        reduced[n] = _k_chip_sum(a, b, "chip_add_" + n)
    theirs.update(zip(late, _pair_join([reduced[n] for n in late], "grad_pair_join_late")))
    for n in late:
        grads[n], delta[n], new_m[n], new_v[n] = _k_adam(w[n], reduced[n], theirs[n], m1[n], m2[n], zeros, "adam_" + n)

    lead = lambda d: [d[n][None] for n in _WEIGHTS]
    return (loss, grad_x[None], *lead(grads), *lead(delta), *lead(new_m), *lead(new_v))
```

```python
import math

import jax
import jax.numpy as jnp
import numpy as np
from jax import lax
from jax.experimental import pallas as pl
from jax.experimental.pallas import tpu as pltpu

F32 = jnp.float32
_MM = jnp.bfloat16
_WIRE = jnp.bfloat16

D_MODEL = 1024
HEAD = 64
BLK = 128
A_GROUPS = ((128, 1), (512, 4), (2048, 16))
A_HEADS = 4
A_W = A_HEADS * HEAD
B_QH = 8
B_KVH = 2
B_WINDOW = 128
M_HEADS = 4
M_HD = 128
M_W = M_HEADS * M_HD
D_FF = 2816
EPS = 1e-6
NEG = -1e30
ROPE_THETA = 500000.0
ROPE_ROT = 16
CHIPS = 4
NDEV = 8
ADAM_LR, ADAM_B1, ADAM_B2, ADAM_EPS, ADAM_WD, ADAM_STEP = 0.001, 0.9, 0.999, 1e-08, 0.01, 10
VMEM_LIMIT = 58 * 1024 * 1024
MESH = pl.DeviceIdType.MESH


def _pc(body, *, name, grid, in_specs, out_specs, out_shape, scratch=()):
    return pl.pallas_call(
        body, name=name, grid=grid, in_specs=in_specs, out_specs=out_specs, out_shape=out_shape,
        scratch_shapes=list(scratch),
        compiler_params=pltpu.CompilerParams(dimension_semantics=("arbitrary",) * len(grid),
                                             vmem_limit_bytes=VMEM_LIMIT))


def _row(ts, c, col=0):
    return pl.BlockSpec((ts, c), lambda i: (i, col))


def _res(shape):
    n = len(shape)
    return pl.BlockSpec(tuple(shape), lambda i: (0,) * n, pipeline_mode=pl.Buffered(1))


def _acc(shape):
    n = len(shape)
    return pl.BlockSpec(tuple(shape), lambda i: (0,) * n)


def _sds(shape, dtype=F32):
    return jax.ShapeDtypeStruct(tuple(shape), dtype)


def _dot(a, b):
    return jnp.dot(a.astype(_MM), b.astype(_MM), preferred_element_type=F32)


def _dot_nt(a, b):
    return lax.dot_general(a.astype(_MM), b.astype(_MM), (((1,), (1,)), ((), ())), preferred_element_type=F32)


def _dot_tn(a, b):
    return lax.dot_general(a.astype(_MM), b.astype(_MM), (((0,), (0,)), ((), ())), preferred_element_type=F32)


def _sum8(v):
    ts, c = v.shape
    return jnp.sum(v.reshape(ts // 8, 8, c), axis=0)


def _sigmoid(z):
    return 1.0 / (1.0 + jnp.exp(-z))


def _rms(x):
    r = lax.rsqrt(jnp.mean(x * x, axis=-1, keepdims=True) + EPS)
    return x * r, r


def _rms_bwd(dy, xh, r, gain):
    z = dy * gain
    return r * (z - xh * jnp.mean(z * xh, axis=-1, keepdims=True))


def _split_hi_lo(v):
    hi = v.astype(_MM)
    return hi, (v - hi.astype(F32)).astype(_MM)


def _lane_head(shape):
    return lax.shift_right_logical(lax.broadcasted_iota(jnp.int32, shape, len(shape) - 1), 6)


def _seg_sum64(v):
    w = v.shape[1]
    e = jnp.where(_lane_head((w, w)) == lax.shift_right_logical(lax.broadcasted_iota(jnp.int32, (w, w), 0), 6),
                  1.0, 0.0).astype(_MM)
    hi, lo = _split_hi_lo(v)
    return jnp.dot(hi, e, preferred_element_type=F32) + jnp.dot(lo, e, preferred_element_type=F32)


def _seg_norm(x, seg):
    if seg == HEAD:
        r = lax.rsqrt(_seg_sum64(x * x) * (1.0 / HEAD) + EPS)
        return x * r, r
    w = x.shape[1]
    xh, rr = [], []
    for s in range(w // seg):
        xs = x[:, s * seg:(s + 1) * seg]
        r = lax.rsqrt(jnp.mean(xs * xs, axis=-1, keepdims=True) + EPS)
        xh.append(xs * r)
        rr.append(jnp.broadcast_to(r, xs.shape))
    return jnp.concatenate(xh, axis=1), jnp.concatenate(rr, axis=1)


def _seg_mean(v, seg):
    if seg == HEAD:
        return _seg_sum64(v) * (1.0 / HEAD)
    w = v.shape[1]
    out = []
    for s in range(w // seg):
        vs = v[:, s * seg:(s + 1) * seg]
        out.append(jnp.broadcast_to(jnp.mean(vs, axis=-1, keepdims=True), vs.shape))
    return jnp.concatenate(out, axis=1)


def _rope(t, c, sa, sb):
    out = []
    for cb in range(t.shape[1] // 128):
        tc = t[:, cb * 128:(cb + 1) * 128]
        out.append(tc * c + pltpu.roll(tc, 120, 1) * sa + pltpu.roll(tc, 8, 1) * sb)
    return jnp.concatenate(out, axis=1) if len(out) > 1 else out[0]


def _rope_bwd(dy, c, sa, sb):
    out = []
    for cb in range(dy.shape[1] // 128):
        dc = dy[:, cb * 128:(cb + 1) * 128]
        out.append(dc * c + pltpu.roll(dc * sa, 8, 1) + pltpu.roll(dc * sb, 120, 1))
    return jnp.concatenate(out, axis=1) if len(out) > 1 else out[0]


def _rope_consts():
    half = ROPE_ROT // 2
    c = np.float32(-2.0 * math.log(ROPE_THETA) / ROPE_ROT)
    freqs = np.exp(np.arange(half, dtype=np.float32) * c).astype(np.float32)
    place = np.zeros((3, half, 128), np.float32)
    ones = np.zeros((1, 128), np.float32)
    for lane in range(128):
        d = lane % HEAD
        if d < half:
            place[0, d, lane], place[1, d, lane] = 1.0, -1.0
        elif d < ROPE_ROT:
            place[0, d - half, lane], place[2, d - half, lane] = 1.0, 1.0
        else:
            ones[0, lane] = 1.0
    return np.tile(freqs[:, None], (1, 128)), place, ones


def _rope_tables(pos_rows):
    r = pos_rows.shape[0]
    tr = min(1024, r)
    freqs, place, ones = _rope_consts()

    def split3(v):
        hi, mid = _split_hi_lo(v)
        lo = (v - hi.astype(F32) - mid.astype(F32)).astype(_MM)
        return hi, mid, lo

    def body(p_ref, f_ref, e_ref, one_ref, c_ref, sa_ref, sb_ref):
        ang = jnp.concatenate([p_ref[j:j + 1, :].astype(F32) * f_ref[...] for j in range(tr // 128)], axis=1)
        cos, sin = jnp.cos(ang), jnp.sin(ang)
        for ref, k, v in ((c_ref, 0, cos), (sa_ref, 1, sin), (sb_ref, 2, sin)):
            e = e_ref[k].astype(_MM)
            out = sum(_dot_tn(part, e) for part in split3(v))
            ref[...] = out + one_ref[...] if k == 0 else out

    return _pc(body, name="rope_tables", grid=(r // tr,),
               in_specs=[pl.BlockSpec((tr // 128, 128), lambda i: (i, 0)), _acc((ROPE_ROT // 2, 128)),
                         _acc((3, ROPE_ROT // 2, 128)), _acc((1, 128))],
               out_specs=[_row(tr, 128)] * 3, out_shape=[_sds((r, 128))] * 3)(
                   pos_rows.reshape(r // 128, 128), jnp.asarray(freqs), jnp.asarray(place), jnp.asarray(ones))


def _k_in(x, g1, w_in):
    s = x.shape[0]
    ts = min(512, s)
    nin = w_in.shape[2]
    ncol = CHIPS * nin
    a_cols = 3 * A_W
    offs = [0, a_cols, 2 * a_cols, 3 * a_cols, 3 * a_cols + B_QH * HEAD,
            3 * a_cols + (B_QH + B_KVH) * HEAD, 3 * a_cols + (B_QH + 2 * B_KVH) * HEAD, ncol]

    def body(x_ref, g_ref, wi_ref, h_ref, a0, a1, a2, qb, kb, vb, mq, p_scr):
        xh, _ = _rms(x_ref[...])
        h = (xh * g_ref[...]).astype(_MM)
        h_ref[...] = h
        for j in range(CHIPS):
            p_scr[:, j * nin:(j + 1) * nin] = jnp.dot(h, wi_ref[j], preferred_element_type=F32)
        for k, ref in enumerate((a0, a1, a2, qb, kb, vb, mq)):
            ref[...] = p_scr[:, offs[k]:offs[k + 1]]

    widths = [offs[k + 1] - offs[k] for k in range(7)]
    return _pc(
        body, name="in_proj", grid=(s // ts,),
        in_specs=[_row(ts, D_MODEL), _res((1, D_MODEL)), _res(w_in.shape)],
        out_specs=[_row(ts, D_MODEL)] + [_row(ts, w) for w in widths],
        out_shape=[_sds((s, D_MODEL), _MM)] + [_sds((s, w)) for w in widths],
        scratch=[pltpu.VMEM((ts, ncol), F32)])(x, g1, w_in)


def _k_gate(h, w_gate, b_gate):
    s = h.shape[0]
    ts = min(256, s)
    ng = w_gate.shape[2]

    def body(h_ref, wg_ref, bg_ref, gt_ref):
        h = h_ref[...]
        for j in range(CHIPS):
            z = jnp.dot(h, wg_ref[j], preferred_element_type=F32) + bg_ref[:, j * ng:(j + 1) * ng]
            gt_ref[:, j * ng:(j + 1) * ng] = _sigmoid(z)

    return _pc(body, name="gate_proj", grid=(s // ts,),
               in_specs=[_row(ts, D_MODEL), _res(w_gate.shape), _res(b_gate.shape)],
               out_specs=[_row(ts, CHIPS * ng)], out_shape=[_sds((s, CHIPS * ng))])(h, w_gate, b_gate)[0]


def _k_prep(srcs, gq, gk, tabs, tab_row, *, wq, wk, rows_per_gain, name):
    rows = srcs[0][0].shape[0]
    ts = min(512, rows)

    def body(q_ref, k_ref, v_ref, gq_ref, gk_ref, c_ref, sa_ref, sb_ref, qn_ref, kn_ref, vn_ref):
        c, sa, sb = c_ref[...], sa_ref[...], sb_ref[...]
        qh, _ = _seg_norm(q_ref[...], HEAD)
        qn_ref[...] = _rope(qh * gq_ref[...], c, sa, sb).astype(_MM)
        kh, _ = _seg_norm(k_ref[...], HEAD)
        kn_ref[...] = _rope(kh * gk_ref[...], c, sa, sb).astype(_MM)
        vn_ref[...] = v_ref[...].astype(_MM)

    gspec = lambda w: pl.BlockSpec((None, 1, w), lambda i: ((i * ts) // rows_per_gain, 0, 0))
    return _pc(
        body, name=name, grid=(rows // ts,),
        in_specs=[_row(ts, wq, srcs[0][1]), _row(ts, wk, srcs[1][1]), _row(ts, wk, srcs[2][1]),
                  gspec(wq), gspec(wk)] + [pl.BlockSpec((ts, 128), lambda i: (i + tab_row // ts, 0))] * 3,
        out_specs=[_row(ts, wq), _row(ts, wk), _row(ts, wk)],
        out_shape=[_sds((rows, wq), _MM), _sds((rows, wk), _MM), _sds((rows, wk), _MM)])(
            srcs[0][0], srcs[1][0], srcs[2][0], gq, gk, *tabs)


def _first_flag(b, segs, nb):
    first = b >= nb
    for k, (start, period) in enumerate(segs):
        end = segs[k + 1][0] if k + 1 < len(segs) else nb
        first = first | ((b >= start) & (b < end) & (lax.rem(b - start, jnp.int32(period)) == 0))
    return first


def _band_bias(thr, with_cur):
    qi = lax.broadcasted_iota(jnp.int32, (BLK, BLK), 0)
    kj = lax.broadcasted_iota(jnp.int32, (BLK, BLK), 1)
    prev = jnp.where(kj >= qi + thr, 0.0, NEG)
    return jnp.concatenate([prev, jnp.where(kj <= qi, 0.0, NEG)], axis=1) if with_cur else prev


def _blockdiag(t4):
    head = _lane_head((1, A_W))
    return jnp.concatenate([t4 * jnp.where(head == h, 1.0, 0.0).astype(t4.dtype) for h in range(A_HEADS)], axis=0)


def _fold_diag(t, n):
    head = _lane_head((n, A_W))
    out = t[3 * n:4 * n]
    for h in (2, 1, 0):
        out = jnp.where(head == h, t[h * n:(h + 1) * n], out)
    return out


def _expand_heads(cols):
    n = cols[0].shape[0]
    head = _lane_head((n, A_W))
    out = jnp.broadcast_to(cols[3], (n, A_W))
    for h in (2, 1, 0):
        out = jnp.where(head == h, cols[h], out)
    return out


def _unit_kv(pieces, u, shared):
    cols = slice(u * HEAD, (u + 1) * HEAD) if shared else slice(u * A_W, (u + 1) * A_W)
    rows = [ref[rs, cols] for ref, rs in pieces]
    k = rows[0] if len(rows) == 1 else jnp.concatenate(rows, axis=0)
    return jnp.concatenate([k] * A_HEADS, axis=1) if shared else k


_LO, _HI, _BOTH = slice(0, BLK), slice(BLK, 2 * BLK), slice(0, 2 * BLK)


def _k_band_fwd(qn, kn, vn, *, hq, hk, max_dist, segs, sink, name):
    rows = qn.shape[0]
    nb = rows // BLK
    units = hq // A_HEADS
    shared = hk != hq
    wq, wk = hq * HEAD, hk * HEAD
    scale = HEAD ** -0.5

    def body(*refs):
        if sink is None:
            q_ref, kc_ref, kp_ref, vc_ref, vp_ref, o_ref, l_ref = refs
        else:
            q_ref, kc_ref, kp_ref, vc_ref, vp_ref, sk_ref, o_ref, l_ref = refs
        i = pl.program_id(0)
        for half, rs in enumerate((_LO, _HI)):
            bias = _band_bias(jnp.where(_first_flag(2 * i + half, segs, nb), 1 << 20, BLK - max_dist), True)
            kpieces = ((kp_ref, _LO), (kc_ref, _LO)) if half == 0 else ((kc_ref, _BOTH),)
            vpieces = ((vp_ref, _LO), (vc_ref, _LO)) if half == 0 else ((vc_ref, _BOTH),)
            for u in range(units):
                us = slice(u * A_W, (u + 1) * A_W)
                kb = _blockdiag(_unit_kv(kpieces, u, shared))
                vb = _blockdiag(_unit_kv(vpieces, u, shared))
                s_all = _dot_nt(q_ref[rs, us], kb) * scale
                ps, ls = [], []
                for h in range(A_HEADS):
                    s = s_all[:, h * 2 * BLK:(h + 1) * 2 * BLK] + bias
                    m = jnp.max(s, axis=-1, keepdims=True)
                    e = jnp.exp(s - m)
                    lse = m + jnp.log(jnp.sum(e, axis=-1, keepdims=True))
                    if sink is not None:
                        sk = sk_ref[u * A_HEADS + h]
                        mx = jnp.maximum(lse, sk)
                        lse = mx + jnp.log(jnp.exp(lse - mx) + jnp.exp(sk - mx))
                    ps.append((e * jnp.exp(m - lse)).astype(_MM))
                    ls.append(lse)
                o_ref[rs, us] = _dot(jnp.concatenate(ps, axis=1), vb)
                l_ref[rs, us] = _expand_heads(ls)

    two = lambda w: pl.BlockSpec((2 * BLK, w), lambda i: (i, 0))
    prev = lambda w: pl.BlockSpec((BLK, w), lambda i: (jnp.maximum(2 * i - 1, 0), 0))
    in_specs = [two(wq), two(wk), prev(wk), two(wk), prev(wk)]
    args = [qn, kn, kn, vn, vn]
    if sink is not None:
        in_specs.append(pl.BlockSpec(memory_space=pltpu.SMEM))
        args.append(sink)
    return _pc(body, name=name, grid=(nb // 2,), in_specs=in_specs, out_specs=[two(wq), two(wq)],
               out_shape=[_sds((rows, wq)), _sds((rows, wq))])(*args)


def _k_memkv(mem, mem_norm, w_kv, m_k_norm):
    n = mem.shape[0]

    def body(m_ref, g_ref, w_ref, gk_ref, mn_ref, kv_ref, mk_ref, mv_ref):
        mh, _ = _rms(m_ref[...])
        mn = (mh * g_ref[...]).astype(_MM)
        mn_ref[...] = mn
        kv = jnp.dot(mn, w_ref[...], preferred_element_type=F32)
        kv_ref[...] = kv
        kh, _ = _seg_norm(kv[:, :M_W], M_HD)
        mk_ref[...] = (kh * gk_ref[...]).astype(_MM)
        mv_ref[...] = kv[:, M_W:].astype(_MM)

    return _pc(body, name="mem_kv", grid=(1,),
               in_specs=[_acc((n, D_MODEL)), _acc((1, D_MODEL)), _acc(w_kv.shape), _acc((1, M_W))],
               out_specs=[_acc((n, D_MODEL)), _acc((n, 2 * M_W)), _acc((n, M_W)), _acc((n, M_W))],
               out_shape=[_sds((n, D_MODEL), _MM), _sds((n, 2 * M_W)), _sds((n, M_W), _MM), _sds((n, M_W), _MM)])(
                   mem, mem_norm, w_kv, m_k_norm)


def _mem_probs(q, mk):
    sc = _dot_nt(q, mk) * (M_HD ** -0.5)
    e = jnp.exp(sc - jnp.max(sc, axis=-1, keepdims=True))
    return e / jnp.sum(e, axis=-1, keepdims=True)


def _k_mem_fwd(m_q, gq, mk, mv):
    s = m_q.shape[0]
    n = mk.shape[0]
    ts = min(512, s)

    def body(q_ref, g_ref, mk_ref, mv_ref, o_ref):
        qh, _ = _seg_norm(q_ref[...], M_HD)
        qn = (qh * g_ref[...]).astype(_MM)
        for h in range(M_HEADS):
            hs = slice(h * M_HD, (h + 1) * M_HD)
            o_ref[:, hs] = _dot(_mem_probs(qn[:, hs], mk_ref[:, hs]), mv_ref[:, hs])

    return _pc(body, name="mem_attn", grid=(s // ts,),
               in_specs=[_row(ts, M_W), _res((1, M_W)), _res((n, M_W)), _res((n, M_W))],
               out_specs=[_row(ts, M_W)], out_shape=[_sds((s, M_W))])(m_q, gq, mk, mv)[0]


def _group_weights(l0, l1, l2):
    m = jnp.maximum(jnp.maximum(l0, l1), l2)
    e0, e1, e2 = jnp.exp(l0 - m), jnp.exp(l1 - m), jnp.exp(l2 - m)
    inv = 1.0 / (e0 + e1 + e2)
    return e0 * inv, e1 * inv, e2 * inv


def _branch_products(oa, ob, om, woa_ref, wob_ref, wom_ref, j):
    return _dot(oa, woa_ref[j]), _dot(ob, wob_ref[j]), _dot(om, wom_ref[j])


def _k_merge(og, lg, o_b, o_m, gates, x, w_oa, w_ob, w_om, w_out, g2):
    s = x.shape[0]
    ts = min(256, s)
    nc = w_oa.shape[2]

    def body(o0, o1, o2, l0, l1, l2, ob_ref, om_ref, gt_ref, x_ref, woa, wob, wom, wout, g_ref,
             oa_ref, mer_ref, x1_ref, h2_ref, m_scr):
        w0, w1, w2 = _group_weights(l0[...], l1[...], l2[...])
        oa = w0 * o0[...] + w1 * o1[...] + w2 * o2[...]
        oa_ref[...] = oa
        ob, om = ob_ref[...], om_ref[...]
        for j in range(CHIPS):
            pa, pb, pm = _branch_products(oa, ob, om, woa, wob, wom, j)
            cs = lambda br: slice(br * D_MODEL + j * nc, br * D_MODEL + (j + 1) * nc)
            m_scr[:, j * nc:(j + 1) * nc] = gt_ref[:, cs(0)] * pa + gt_ref[:, cs(1)] * pb + gt_ref[:, cs(2)] * pm
        mer = m_scr[...].astype(_MM)
        mer_ref[...] = mer
        x1 = x_ref[...] + jnp.dot(mer, wout[...], preferred_element_type=F32)
        x1_ref[...] = x1
        xh, _ = _rms(x1)
        h2_ref[...] = (xh * g_ref[...]).astype(_MM)

    return _pc(
        body, name="merge_out", grid=(s // ts,),
        in_specs=[_row(ts, A_W)] * 6 + [_row(ts, B_QH * HEAD), _row(ts, M_W), _row(ts, 3 * D_MODEL), _row(ts, D_MODEL),
                                         _res(w_oa.shape), _res(w_ob.shape), _res(w_om.shape), _res(w_out.shape),
                                         _res((1, D_MODEL))],
        out_specs=[_row(ts, A_W), _row(ts, D_MODEL), _row(ts, D_MODEL), _row(ts, D_MODEL)],
        out_shape=[_sds((s, A_W)), _sds((s, D_MODEL), _MM), _sds((s, D_MODEL)), _sds((s, D_MODEL), _MM)],
        scratch=[pltpu.VMEM((ts, D_MODEL), F32)])(*og, *lg, o_b, o_m, gates, x, w_oa, w_ob, w_om, w_out, g2)


def _k_up(h2, w_up):
    s = h2.shape[0]
    ts = min(256, s)
    nu = w_up.shape[2]

    def body(h_ref, w_ref, u_ref):
        h = h_ref[...]
        for j in range(CHIPS):
            u_ref[:, j * nu:(j + 1) * nu] = jnp.dot(h, w_ref[j], preferred_element_type=F32)

    return _pc(body, name="up_proj", grid=(s // ts,), in_specs=[_row(ts, D_MODEL), _res(w_up.shape)],
               out_specs=[_row(ts, CHIPS * nu)], out_shape=[_sds((s, CHIPS * nu))])(h2, w_up)[0]


def _shift_down(v, halo, k):
    rolled = pltpu.roll(v, k, 0)
    row = lax.broadcasted_iota(jnp.int32, (8, v.shape[1]), 0)
    slab = rolled[0:8]
    for r in range(k):
        slab = jnp.where(row == r, halo[8 - k + r:8 - k + r + 1, :], slab)
    return jnp.concatenate([slab, rolled[8:]], axis=0)


def _shift_up(v, halo, k):
    ts = v.shape[0]
    rolled = pltpu.roll(v, ts - k, 0)
    row = lax.broadcasted_iota(jnp.int32, (8, v.shape[1]), 0)
    slab = rolled[ts - 8:]
    for r in range(k):
        slab = jnp.where(row == 8 - k + r, halo[r:r + 1, :], slab)
    return jnp.concatenate([rolled[:ts - 8], slab], axis=0)


def _k_ffn(u, conv_w, conv_b, w_down, w_down_t, x1, target):
    s = u.shape[0]
    ts = min(256, s)
    nu = conv_w.shape[2]
    half = CHIPS // 2

    def body(u_ref, uh_ref, cw_ref, cb_ref, wd_ref, wdt_ref, x1_ref, t_ref, dy_ref, f_ref, dc_ref, loss_ref, c_scr,
             f_scr, s_scr):
        i = pl.program_id(0)
        halo = jnp.where(i > 0, uh_ref[...], 0.0)
        for j in range(CHIPS):
            cs = slice(j * nu, (j + 1) * nu)
            uj = u_ref[:, cs]
            hj = halo[:, cs]
            c_scr[:, cs] = (cb_ref[:, cs] + cw_ref[j, 0:1, :] * _shift_down(uj, hj, 2)
                            + cw_ref[j, 1:2, :] * _shift_down(uj, hj, 1) + cw_ref[j, 2:3, :] * uj)
        for j in range(half):
            a = c_scr[:, j * nu:(j + 1) * nu]
            g = c_scr[:, (half + j) * nu:(half + j + 1) * nu]
            sa = _sigmoid(a)
            s_scr[:, j * nu:(j + 1) * nu] = sa
            f_scr[:, j * nu:(j + 1) * nu] = (a * sa * g).astype(_MM)
        f = f_scr[...]
        f_ref[...] = f
        y = x1_ref[...] + jnp.dot(f, wd_ref[...], preferred_element_type=F32)
        err = y - t_ref[...]
        dy = err * (1.0 / D_MODEL)
        dy_ref[...] = dy

        @pl.when(i == 0)
        def _():
            loss_ref[...] = jnp.zeros_like(loss_ref)

        loss_ref[...] += _sum8(err * err)
        df = _dot(dy, wdt_ref[...])
        for j in range(half):
            a = c_scr[:, j * nu:(j + 1) * nu]
            g = c_scr[:, (half + j) * nu:(half + j + 1) * nu]
            sa = s_scr[:, j * nu:(j + 1) * nu]
            dfj = df[:, j * nu:(j + 1) * nu]
            dc_ref[:, j * nu:(j + 1) * nu] = dfj * g * (sa * (1.0 + a * (1.0 - sa)))
            dc_ref[:, (half + j) * nu:(half + j + 1) * nu] = dfj * (a * sa)

    wide = CHIPS * nu
    return _pc(
        body, name="conv_ffn", grid=(s // ts,),
        in_specs=[_row(ts, wide), pl.BlockSpec((8, wide), lambda i: (jnp.maximum(i * (ts // 8) - 1, 0), 0)),
                  _res(conv_w.shape), _res((1, wide)), _res(w_down.shape), _res(w_down_t.shape), _row(ts, D_MODEL),
                  _row(ts, D_MODEL)],
        out_specs=[_row(ts, D_MODEL), _row(ts, D_FF), _row(ts, wide), _acc((8, D_MODEL))],
        out_shape=[_sds((s, D_MODEL)), _sds((s, D_FF), _MM), _sds((s, wide)), _sds((8, D_MODEL))],
        scratch=[pltpu.VMEM((ts, wide), F32), pltpu.VMEM((ts, D_FF), _MM), pltpu.VMEM((ts, D_FF), F32)])(
            u, u, conv_w, conv_b, w_down, w_down_t, x1, target)


def _k_conv_bwd(dc, u, conv_w, w_up, x1, g2, dy):
    s = u.shape[0]
    ts = min(256, s)
    nu = conv_w.shape[2]
    wide = CHIPS * nu
    last = s // ts - 1

    def body(dc_ref, dn_ref, u_ref, cw_ref, wu_ref, x1_ref, g_ref, dy_ref, dx1_ref, du_ref, cacc_ref, gacc_ref):
        i = pl.program_id(0)

        @pl.when(i == 0)
        def _():
            cacc_ref[...] = jnp.zeros_like(cacc_ref)
            gacc_ref[...] = jnp.zeros_like(gacc_ref)

        dhalo = jnp.where(i < last, dn_ref[...], 0.0)
        dh2 = jnp.zeros((ts, D_MODEL), F32)
        for j in range(CHIPS):
            cs = slice(j * nu, (j + 1) * nu)
            dcj, uj = dc_ref[:, cs], u_ref[:, cs]
            dc1, dc2 = _shift_up(dcj, dhalo[:, cs], 1), _shift_up(dcj, dhalo[:, cs], 2)
            cacc_ref[0, :, cs] += _sum8(dcj)
            cacc_ref[1, :, cs] += _sum8(dc2 * uj)
            cacc_ref[2, :, cs] += _sum8(dc1 * uj)
            cacc_ref[3, :, cs] += _sum8(dcj * uj)
            du = (cw_ref[j, 2:3, :] * dcj + cw_ref[j, 1:2, :] * dc1 + cw_ref[j, 0:1, :] * dc2).astype(_MM)
            du_ref[:, cs] = du
            dh2 = dh2 + _dot_nt(du, wu_ref[j])
        xh, r = _rms(x1_ref[...])
        gacc_ref[...] += _sum8(dh2 * xh)
        dx1_ref[...] = dy_ref[...] + _rms_bwd(dh2, xh, r, g_ref[...])

    return _pc(
        body, name="conv_up_bwd", grid=(s // ts,),
        in_specs=[_row(ts, wide),
                  pl.BlockSpec((8, wide), lambda i: (jnp.minimum((i + 1) * (ts // 8), s // 8 - 1), 0)),
                  _row(ts, wide), _res(conv_w.shape), _res(w_up.shape), _row(ts, D_MODEL), _res((1, D_MODEL)),
                  _row(ts, D_MODEL)],
        out_specs=[_row(ts, D_MODEL), _row(ts, wide), _acc((4, 8, wide)), _acc((8, D_MODEL))],
        out_shape=[_sds((s, D_MODEL)), _sds((s, wide), _MM), _sds((4, 8, wide)), _sds((8, D_MODEL))])(
            dc, dc, u, conv_w, w_up, x1, g2, dy)


def _k_merge_bwd(dx1, og, lg, o_a, o_b, o_m, gates, w_oa, w_ob, w_om, w_out, dep):
    s = dx1.shape[0]
    ts = min(256, s)
    nc = w_oa.shape[2]

    def body(dx_ref, o0, o1, o2, l0, l1, l2, oa_ref, ob_ref, om_ref, gt_ref, woa, wob, wom, wout, dep_ref,
             dgp_ref, dpa_ref, dpb_ref, dpm_ref, dog0, dog1, dog2, dl0, dl1, dl2, dob_ref, dom_ref, bacc_ref):
        i = pl.program_id(0)

        @pl.when(i == 0)
        def _():
            bacc_ref[...] = jnp.zeros_like(bacc_ref)

        dmer = _dot_nt(dx_ref[...], wout[...])
        oa, ob, om = oa_ref[...], ob_ref[...], om_ref[...]
        doa = jnp.zeros((ts, A_W), F32)
        dob = jnp.zeros((ts, B_QH * HEAD), F32)
        dom = jnp.zeros((ts, M_W), F32)
        for j in range(CHIPS):
            prods = _branch_products(oa, ob, om, woa, wob, wom, j)
            dmj = dmer[:, j * nc:(j + 1) * nc]
            dps = []
            for br, (p, dref) in enumerate(zip(prods, (dpa_ref, dpb_ref, dpm_ref))):
                cs = slice(br * D_MODEL + j * nc, br * D_MODEL + (j + 1) * nc)
                gt = gt_ref[:, cs]
                dgp = dmj * p * gt * (1.0 - gt)
                dgp_ref[:, cs] = dgp.astype(_MM)
                bacc_ref[:, cs] += _sum8(dgp)
                dp = (dmj * gt).astype(_MM)
                dref[:, j * nc:(j + 1) * nc] = dp
                dps.append(dp)
            doa = doa + _dot_nt(dps[0], woa[j])
            dob = dob + _dot_nt(dps[1], wob[j])
            dom = dom + _dot_nt(dps[2], wom[j])
        dob_ref[...] = dob
        dom_ref[...] = dom
        ws = _group_weights(l0[...], l1[...], l2[...])
        dsum = _seg_mean(doa * oa, HEAD) * float(HEAD)
        for w, dref, lref in zip(ws, (dog0, dog1, dog2), (dl0, dl1, dl2)):
            dref[...] = w * doa
            lref[...] = w * dsum

    return _pc(
        body, name="merge_out_bwd", grid=(s // ts,),
        in_specs=[_row(ts, D_MODEL)] + [_row(ts, A_W)] * 7 + [_row(ts, B_QH * HEAD), _row(ts, M_W), _row(ts, 3 * D_MODEL),
                                                              _res(w_oa.shape), _res(w_ob.shape), _res(w_om.shape),
                                                              _res(w_out.shape), _res((8, 128))],
        out_specs=[_row(ts, 3 * D_MODEL)] + [_row(ts, D_MODEL)] * 3 + [_row(ts, A_W)] * 6
        + [_row(ts, B_QH * HEAD), _row(ts, M_W), _acc((8, 3 * D_MODEL))],
        out_shape=[_sds((s, 3 * D_MODEL), _MM)] + [_sds((s, D_MODEL), _MM)] * 3 + [_sds((s, A_W))] * 6
        + [_sds((s, B_QH * HEAD)), _sds((s, M_W)), _sds((8, 3 * D_MODEL))])(
            dx1, *og, *lg, o_a, o_b, o_m, gates, w_oa, w_ob, w_om, w_out, dep)


def _k_mem_bwd(m_q, gq, mk, mv, o_m, do_m):
    s = m_q.shape[0]
    n = mk.shape[0]
    ts = min(512, s)
    scale = M_HD ** -0.5

    def body(q_ref, g_ref, mk_ref, mv_ref, o_ref, do_ref, dq_ref, dmk_ref, dmv_ref, gacc_ref):
        i = pl.program_id(0)

        @pl.when(i == 0)
        def _():
            dmk_ref[...] = jnp.zeros_like(dmk_ref)
            dmv_ref[...] = jnp.zeros_like(dmv_ref)
            gacc_ref[...] = jnp.zeros_like(gacc_ref)

        gain = g_ref[...]
        qh, r = _seg_norm(q_ref[...], M_HD)
        qn = (qh * gain).astype(_MM)
        do = do_ref[...]
        delta = _seg_mean(do * o_ref[...], M_HD) * float(M_HD)
        dqn = []
        for h in range(M_HEADS):
            hs = slice(h * M_HD, (h + 1) * M_HD)
            p = _mem_probs(qn[:, hs], mk_ref[:, hs])
            dp = _dot_nt(do[:, hs], mv_ref[:, hs])
            ds = (p * (dp - delta[:, hs][:, 0:1]) * scale).astype(_MM)
            dqn.append(_dot(ds, mk_ref[:, hs]))
            dmk_ref[:, hs] += _dot_tn(ds, qn[:, hs])
            dmv_ref[:, hs] += _dot_tn(p, do[:, hs])
        dqn = jnp.concatenate(dqn, axis=1)
        gacc_ref[...] += _sum8(dqn * qh)
        z = dqn * gain
        dq_ref[...] = (r * (z - qh * _seg_mean(z * qh, M_HD))).astype(_MM)

    return _pc(
        body, name="mem_attn_bwd", grid=(s // ts,),
        in_specs=[_row(ts, M_W), _res((1, M_W)), _res((n, M_W)), _res((n, M_W)), _row(ts, M_W), _row(ts, M_W)],
        out_specs=[_row(ts, M_W), _acc((n, M_W)), _acc((n, M_W)), _acc((8, M_W))],
        out_shape=[_sds((s, M_W), _MM), _sds((n, M_W)), _sds((n, M_W)), _sds((8, M_W))])(m_q, gq, mk, mv, o_m, do_m)


def _k_memkv_bwd(mem, mem_norm, w_kv, m_k_norm, mem_n, kv, dmk, dmv):
    n = mem.shape[0]

    def body(m_ref, g_ref, w_ref, gk_ref, mn_ref, kv_ref, dmk_ref, dmv_ref, dw_ref, dg_ref, dgk_ref):
        gk = gk_ref[...]
        kh, r = _seg_norm(kv_ref[:, :M_W], M_HD)
        dmk = dmk_ref[...]
        dgk_ref[...] = _sum8(dmk * kh)
        z = dmk * gk
        dk = r * (z - kh * _seg_mean(z * kh, M_HD))
        dkv = jnp.concatenate([dk, dmv_ref[...]], axis=1).astype(_MM)
        dw_ref[...] = _dot_tn(mn_ref[...], dkv)
        dmn = _dot_nt(dkv, w_ref[...])
        mh, _ = _rms(m_ref[...])
        dg_ref[...] = _sum8(dmn * mh)

    return _pc(body, name="mem_kv_bwd", grid=(1,),
               in_specs=[_acc((n, D_MODEL)), _acc((1, D_MODEL)), _acc(w_kv.shape), _acc((1, M_W)), _acc((n, D_MODEL)),
                         _acc((n, 2 * M_W)), _acc((n, M_W)), _acc((n, M_W))],
               out_specs=[_acc(w_kv.shape), _acc((8, D_MODEL)), _acc((8, M_W))],
               out_shape=[_sds(w_kv.shape), _sds((8, D_MODEL)), _sds((8, M_W))])(
                   mem, mem_norm, w_kv, m_k_norm, mem_n, kv, dmk, dmv)


def _k_band_bwd(qn, kn, vn, do, lse, dl_or_o, *, hq, hk, max_dist, segs, sink, name):
    rows = qn.shape[0]
    nb = rows // BLK
    units = hq // A_HEADS
    shared = hk != hq
    wq, wk = hq * HEAD, hk * HEAD
    scale = HEAD ** -0.5

    def body(*refs):
        (q2_ref, qx_ref, kc_ref, kp_ref, vc_ref, vp_ref, do2_ref, dox_ref, l2_ref, lx_ref, e2_ref, ex_ref) = refs[:12]
        if sink is None:
            dq_ref, dk_ref, dv_ref = refs[12:]
        else:
            sk_ref, dq_ref, dk_ref, dv_ref, sacc_ref = refs[12:]
        i = pl.program_id(0)
        thr = lambda b: jnp.where(_first_flag(b, segs, nb), 1 << 20, BLK - max_dist)
        bias_a, bias_b = _band_bias(thr(2 * i), True), _band_bias(thr(2 * i + 1), True)
        bias_c = _band_bias(thr(2 * i + 2), False)
        if sink is not None:
            @pl.when(i == 0)
            def _():
                sacc_ref[...] = jnp.zeros_like(sacc_ref)

        def tile(q4, do4, l_cols, dlt, kd, vd, bias, width):
            s, dp = _dot_nt(q4, kd) * scale, _dot_nt(do4, vd)
            ps, dss = [], []
            for h in range(A_HEADS):
                seg = slice(h * width, (h + 1) * width)
                p = jnp.exp(s[:, seg] + bias - l_cols[h])
                ps.append(p)
                dss.append(p * (dp[:, seg] - dlt[:, h * HEAD:h * HEAD + 1]) * scale)
            return ps, dss

        cat = lambda parts: jnp.concatenate([t.astype(_MM) for t in parts], axis=1)
        for u in range(units):
            us = slice(u * A_W, (u + 1) * A_W)
            k_a = _unit_kv(((kp_ref, _LO), (kc_ref, _LO)), u, shared)
            v_a = _unit_kv(((vp_ref, _LO), (vc_ref, _LO)), u, shared)
            k_b, v_b = _unit_kv(((kc_ref, _BOTH),), u, shared), _unit_kv(((vc_ref, _BOTH),), u, shared)
            kd_a, vd_a, kd_b, vd_b = _blockdiag(k_a), _blockdiag(v_a), _blockdiag(k_b), _blockdiag(v_b)
            kd_c, vd_c = _blockdiag(k_b[BLK:]), _blockdiag(v_b[BLK:])
            qs = (q2_ref[_LO, us], q2_ref[_HI, us], qx_ref[:, us])
            dos = (do2_ref[_LO, us], do2_ref[_HI, us], dox_ref[:, us])
            lcols = [[ref[rs, u * A_W + h * HEAD:u * A_W + h * HEAD + 1] for h in range(A_HEADS)]
                     for ref, rs in ((l2_ref, _LO), (l2_ref, _HI), (lx_ref, _LO))]
            if sink is None:
                dlts = (e2_ref[_LO, us], e2_ref[_HI, us], ex_ref[:, us])
            else:
                dlts = tuple(_seg_sum64(d.astype(F32) * ref[rs, us])
                             for d, (ref, rs) in zip(dos, ((e2_ref, _LO), (e2_ref, _HI), (ex_ref, _LO))))
                for t in range(2):
                    for h in range(A_HEADS):
                        j = u * A_HEADS + h
                        sacc_ref[:, j:j + 1] += -jnp.exp(sk_ref[j] - lcols[t][h]) * dlts[t][:, h * HEAD:h * HEAD + 1]
            p_a, ds_a = tile(qs[0], dos[0], lcols[0], dlts[0], kd_a, vd_a, bias_a, 2 * BLK)
            p_b, ds_b = tile(qs[1], dos[1], lcols[1], dlts[1], kd_b, vd_b, bias_b, 2 * BLK)
            p_c, ds_c = tile(qs[2], dos[2], lcols[2], dlts[2], kd_c, vd_c, bias_c, BLK)
            dq_ref[_LO, us] = _dot(cat(ds_a), kd_a)
            dq_ref[_HI, us] = _dot(cat(ds_b), kd_b)
            outs = []
            for pa, pb, pc, lhs in ((ds_a, ds_b, ds_c, qs), (p_a, p_b, p_c, dos)):
                from_a = _fold_diag(_dot_tn(cat([t[:, BLK:] for t in pa]), lhs[0]), BLK)
                from_b = _fold_diag(_dot_tn(cat(pb), lhs[1]), 2 * BLK)
                from_c = _fold_diag(_dot_tn(cat(pc), lhs[2]), BLK)
                outs.append(jnp.concatenate([from_a + from_b[:BLK], from_b[BLK:] + from_c], axis=0))
            dk4, dv4 = outs
            if shared:
                fold = lambda t: (t[:, 0:HEAD] + t[:, HEAD:2 * HEAD]) + (t[:, 2 * HEAD:3 * HEAD] + t[:, 3 * HEAD:])
                dk_ref[:, u * HEAD:(u + 1) * HEAD] = fold(dk4)
                dv_ref[:, u * HEAD:(u + 1) * HEAD] = fold(dv4).astype(_MM)
            else:
                dk_ref[:, us] = dk4
                dv_ref[:, us] = dv4.astype(_MM)

    two = lambda w: pl.BlockSpec((2 * BLK, w), lambda i: (i, 0))
    prev = lambda w: pl.BlockSpec((BLK, w), lambda i: (jnp.maximum(2 * i - 1, 0), 0))
    nxt = lambda w: pl.BlockSpec((BLK, w), lambda i: (jnp.minimum(2 * i + 2, nb - 1), 0))
    in_specs = [two(wq), nxt(wq), two(wk), prev(wk), two(wk), prev(wk), two(wq), nxt(wq), two(wq), nxt(wq), two(wq), nxt(wq)]
    args = [qn, qn, kn, kn, vn, vn, do, do, lse, lse, dl_or_o, dl_or_o]
    out_specs = [two(wq), two(wk), two(wk)]
    out_shape = [_sds((rows, wq)), _sds((rows, wk)), _sds((rows, wk), _MM)]
    if sink is not None:
        in_specs.append(pl.BlockSpec(memory_space=pltpu.SMEM))
        args.append(sink)
        out_specs.append(_acc((BLK, 128)))
        out_shape.append(_sds((BLK, 128)))
    return _pc(body, name=name, grid=(nb // 2,), in_specs=in_specs, out_specs=out_specs, out_shape=out_shape)(*args)


def _k_prep_bwd(srcs, dqn, dkn, gq, gk, tabs, tab_row, *, wq, wk, rows_per_gain, name):
    rows = dqn.shape[0]
    ts = min(512, rows)
    ngain = gq.shape[0]

    def body(q_ref, k_ref, dq_ref, dk_ref, gq_ref, gk_ref, c_ref, sa_ref, sb_ref, oq_ref, ok_ref, aq_ref, ak_ref):
        i = pl.program_id(0)

        @pl.when(lax.rem(i * ts, rows_per_gain) == 0)
        def _():
            aq_ref[...] = jnp.zeros_like(aq_ref)
            ak_ref[...] = jnp.zeros_like(ak_ref)

        c, sa, sb = c_ref[...], sa_ref[...], sb_ref[...]
        for x_ref, d_ref, g_ref, o_ref, a_ref in ((q_ref, dq_ref, gq_ref, oq_ref, aq_ref),
                                                   (k_ref, dk_ref, gk_ref, ok_ref, ak_ref)):
            xh, r = _seg_norm(x_ref[...], HEAD)
            dt = _rope_bwd(d_ref[...], c, sa, sb)
            a_ref[...] += _sum8(dt * xh)
            z = dt * g_ref[...]
            o_ref[...] = (r * (z - xh * _seg_mean(z * xh, HEAD))).astype(_MM)

    gspec = lambda w: pl.BlockSpec((None, 1, w), lambda i: ((i * ts) // rows_per_gain, 0, 0))
    aspec = lambda w: pl.BlockSpec((None, 8, w), lambda i: ((i * ts) // rows_per_gain, 0, 0))
    return _pc(
        body, name=name, grid=(rows // ts,),
        in_specs=[_row(ts, wq, srcs[0][1]), _row(ts, wk, srcs[1][1]), _row(ts, wq), _row(ts, wk), gspec(wq), gspec(wk)]
        + [pl.BlockSpec((ts, 128), lambda i: (i + tab_row // ts, 0))] * 3,
        out_specs=[_row(ts, wq), _row(ts, wk), aspec(wq), aspec(wk)],
        out_shape=[_sds((rows, wq), _MM), _sds((rows, wk), _MM), _sds((ngain, 8, wq)), _sds((ngain, 8, wk))])(
            srcs[0][0], srcs[1][0], dqn, dkn, gq, gk, *tabs)


def _k_in_bwd(pieces, dgp, x, g1, dx1, w_in, w_gate):
    s = x.shape[0]
    ts = min(256, s)
    nin, ng = w_in.shape[2], w_gate.shape[2]
    widths = [p.shape[1] for p in pieces]
    ncol = sum(widths)

    def body(*refs):
        p_refs = refs[:len(pieces)]
        dgp_ref, x_ref, g_ref, dx1_ref, wi_ref, wg_ref, gx_ref, dpj_ref, gacc_ref = refs[len(pieces):]
        i = pl.program_id(0)

        @pl.when(i == 0)
        def _():
            gacc_ref[...] = jnp.zeros_like(gacc_ref)

        off = 0
        for p_ref, w in zip(p_refs, widths):
            dpj_ref[:, off:off + w] = p_ref[...]
            off += w
        dh = jnp.zeros((ts, D_MODEL), F32)
        for j in range(CHIPS):
            dh = dh + _dot_nt(dpj_ref[:, j * nin:(j + 1) * nin], wi_ref[j])
            dh = dh + _dot_nt(dgp_ref[:, j * ng:(j + 1) * ng], wg_ref[j])
        xh, r = _rms(x_ref[...])
        gacc_ref[...] += _sum8(dh * xh)
        gx_ref[...] = dx1_ref[...] + _rms_bwd(dh, xh, r, g_ref[...])

    return _pc(
        body, name="in_proj_bwd", grid=(s // ts,),
        in_specs=[_row(ts, w) for w in widths] + [_row(ts, CHIPS * ng), _row(ts, D_MODEL), _res((1, D_MODEL)),
                                                  _row(ts, D_MODEL), _res(w_in.shape), _res(w_gate.shape)],
        out_specs=[_row(ts, D_MODEL), _row(ts, ncol), _acc((8, D_MODEL))],
        out_shape=[_sds((s, D_MODEL)), _sds((s, ncol), _MM), _sds((8, D_MODEL))])(*pieces, dgp, x, g1, dx1, w_in, w_gate)


def _k_wgrad(a, b, *, nblk, stacked, name):
    s, k = a.shape
    n = b.shape[1]
    nb = n // nblk
    ts = min(2048 if k <= 1024 else 1024, s)

    def body(a_ref, b_ref, o_ref):
        @pl.when(pl.program_id(1) == 0)
        def _():
            o_ref[...] = jnp.zeros_like(o_ref)

        o_ref[...] += _dot_tn(a_ref[...], b_ref[...])

    if stacked:
        out_spec, out_shape = pl.BlockSpec((None, k, nb), lambda g, t: (g, 0, 0)), _sds((nblk, k, nb))
    else:
        out_spec, out_shape = pl.BlockSpec((k, nb), lambda g, t: (0, g)), _sds((k, n))
    return _pc(body, name=name, grid=(nblk, s // ts),
               in_specs=[pl.BlockSpec((ts, k), lambda g, t: (t, 0)), pl.BlockSpec((ts, nb), lambda g, t: (t, g))],
               out_specs=[out_spec], out_shape=[out_shape])(a, b)[0]


def _to_res(t, d):
    s, c = t.shape
    return t if d == 1 else t.reshape(s // d, d, c).transpose(1, 0, 2).reshape(s, c)


def _from_res(t, d):
    s, c = t.shape
    return t if d == 1 else t.reshape(d, s // d, c).transpose(1, 0, 2).reshape(s, c)


def _tile_gain(g, heads):
    return jnp.tile(g, (1,) * (g.ndim - 1) + (heads,))[..., None, :]


def _local_step(x, mem, pos, target, small, get_w_in, get_rest, on_grads):
    s = x.shape[0]
    nblk = s // BLK
    g1, g2 = small["attn_norm"], small["ffn_norm"]

    pos_rows = jnp.concatenate([_to_res(pos[:, None], d)[:, 0] for _, d in A_GROUPS] + [pos])
    tabs = _rope_tables(pos_rows)
    w_in = get_w_in(tabs[0])

    h, qa0, qa1, qa2, q_b, k_b, v_b, m_q = _k_in(x, g1, w_in)

    qkv_a = jnp.concatenate([_to_res(t, d) for t, (_, d) in zip((qa0, qa1, qa2), A_GROUPS)], axis=0)
    gq_a = _tile_gain(small["a_q_norm"], A_HEADS)
    gk_a = _tile_gain(small["a_k_norm"], A_HEADS)
    src_a = ((qkv_a, 0), (qkv_a, 1), (qkv_a, 2))
    qn_a, kn_a, vn_a = _k_prep(src_a, gq_a, gk_a, tabs, 0, wq=A_W, wk=A_W, rows_per_gain=s, name="prep_a")
    segs_a = tuple((gi * nblk, nblk // d) for gi, (_, d) in enumerate(A_GROUPS))
    o_res, l_res = _k_band_fwd(qn_a, kn_a, vn_a, hq=A_HEADS, hk=A_HEADS, max_dist=BLK, segs=segs_a, sink=None,
                               name="attn_a")
    og = [_from_res(o_res[gi * s:(gi + 1) * s], d) for gi, (_, d) in enumerate(A_GROUPS)]
    lg = [_from_res(l_res[gi * s:(gi + 1) * s], d) for gi, (_, d) in enumerate(A_GROUPS)]

    gq_b = _tile_gain(small["b_q_norm"], B_QH)
    gk_b = _tile_gain(small["b_k_norm"], B_KVH)
    src_b = ((q_b, 0), (k_b, 0), (v_b, 0))
    qn_b, kn_b, vn_b = _k_prep(src_b, gq_b, gk_b, tabs, 3 * s, wq=B_QH * HEAD, wk=B_KVH * HEAD, rows_per_gain=s,
                               name="prep_b")
    sink_x = small["b_sinks"][0]
    segs_b = ((0, nblk),)
    o_b, l_b = _k_band_fwd(qn_b, kn_b, vn_b, hq=B_QH, hk=B_KVH, max_dist=B_WINDOW - 1, segs=segs_b, sink=sink_x,
                           name="attn_b")

    wts = get_rest(0, o_b)
    gates = _k_gate(h, wts["w_gate"], small["b_gate"])

    gq_m = _tile_gain(small["m_q_norm"], M_HEADS)[0]
    gk_m = _tile_gain(small["m_k_norm"], M_HEADS)[0]
    mem_n, kv, mk, mv = _k_memkv(mem, small["mem_norm"], wts["w_mem_kv"], gk_m)
    o_m = _k_mem_fwd(m_q, gq_m, mk, mv)

    o_a, merged, x1, h2 = _k_merge(og, lg, o_b, o_m, gates, x, wts["w_o_a"], wts["w_o_b"], wts["w_o_m"],
                                   wts["w_out"], g2)
    wts.update(get_rest(1, x1))
    u = _k_up(h2, wts["w_up"])
    dy, f, dc, loss_acc = _k_ffn(u, wts["conv_w"], small["conv_b"], wts["w_down"], wts["w_down"].T, x1, target)
    loss = (0.5 / D_MODEL) * jnp.sum(loss_acc)

    dx1, du, cacc, g2acc = _k_conv_bwd(dc, u, wts["conv_w"], wts["w_up"], x1, g2, dy)
    tok = on_grads({"w_up": _k_wgrad(h2, du, nblk=CHIPS, stacked=True, name="dw_up"),
                    "w_down": _k_wgrad(f, dy, nblk=2, stacked=False, name="dw_down").reshape(CHIPS, -1, D_MODEL)}, dx1)
    (dgp, dp_a, dp_b, dp_m, dog0, dog1, dog2, dl0, dl1, dl2, do_b, do_m, bacc) = _k_merge_bwd(
        dx1, og, lg, o_a, o_b, o_m, gates, wts["w_o_a"], wts["w_o_b"], wts["w_o_m"], wts["w_out"], tok)
    tok = on_grads({"w_gate": _k_wgrad(h, dgp, nblk=CHIPS, stacked=True, name="dw_gate"),
                    "w_o_a": _k_wgrad(o_a, dp_a, nblk=CHIPS, stacked=True, name="dw_o_a"),
                    "w_o_b": _k_wgrad(o_b, dp_b, nblk=CHIPS, stacked=True, name="dw_o_b"),
                    "w_o_m": _k_wgrad(o_m, dp_m, nblk=CHIPS, stacked=True, name="dw_o_m"),
                    "w_out": _k_wgrad(merged, dx1, nblk=1, stacked=False, name="dw_out").reshape(CHIPS, -1, D_MODEL)},
                   do_m)

    dq_m, dmk, dmv, gqm_acc = _k_mem_bwd(m_q, gq_m + tok[0:1, 0:1], mk, mv, o_m, do_m)
    dw_kv, gmem_acc, gkm_acc = _k_memkv_bwd(mem, small["mem_norm"], wts["w_mem_kv"], gk_m, mem_n, kv, dmk, dmv)

    dq_bn, dk_bn, dv_b, sacc = _k_band_bwd(qn_b, kn_b, vn_b, do_b, l_b, o_b, hq=B_QH, hk=B_KVH,
                                           max_dist=B_WINDOW - 1, segs=segs_b, sink=sink_x, name="attn_b_bwd")
    tok = on_grads({}, dq_bn)
    dq_b, dk_b, gqb_acc, gkb_acc = _k_prep_bwd(src_b, dq_bn, dk_bn, gq_b + tok[0:1, 0:1], gk_b, tabs, 3 * s, wq=B_QH * HEAD,
                                               wk=B_KVH * HEAD, rows_per_gain=s, name="prep_b_bwd")

    do_res = jnp.concatenate([_to_res(t, d) for t, (_, d) in zip((dog0, dog1, dog2), A_GROUPS)], axis=0)
    dl_res = jnp.concatenate([_to_res(t, d) for t, (_, d) in zip((dl0, dl1, dl2), A_GROUPS)], axis=0)
    dq_an, dk_an, dv_a = _k_band_bwd(qn_a, kn_a, vn_a, do_res, l_res, dl_res, hq=A_HEADS, hk=A_HEADS, max_dist=BLK,
                                     segs=segs_a, sink=None, name="attn_a_bwd")
    dq_a, dk_a, gqa_acc, gka_acc = _k_prep_bwd(src_a, dq_an, dk_an, gq_a, gk_a, tabs, 0, wq=A_W, wk=A_W,
                                               rows_per_gain=s, name="prep_a_bwd")
    pieces = []
    for gi, (_, d) in enumerate(A_GROUPS):
        rs = slice(gi * s, (gi + 1) * s)
        pieces += [_from_res(t[rs], d) for t in (dq_a, dk_a, dv_a)]
    pieces += [dq_b, dk_b, dv_b, dq_m]
    grad_x, dproj, g1acc = _k_in_bwd(pieces, dgp, x, g1, dx1, w_in, wts["w_gate"])
    on_grads({"w_in": _k_wgrad(h, dproj, nblk=CHIPS, stacked=True, name="dw_in"),
              "w_mem_kv": dw_kv.reshape(CHIPS, -1, 2 * M_W)}, grad_x)

    def fold(acc, heads):
        v = jnp.sum(acc, axis=-2)
        return jnp.sum(v.reshape(v.shape[:-1] + (heads, -1)), axis=-2)

    csum = jnp.sum(cacc, axis=1)
    sml = {
        "attn_norm": jnp.sum(g1acc, axis=0), "a_q_norm": fold(gqa_acc, A_HEADS), "a_k_norm": fold(gka_acc, A_HEADS),
        "b_q_norm": fold(gqb_acc[0], B_QH), "b_k_norm": fold(gkb_acc[0], B_KVH),
        "b_sinks": jnp.sum(sacc, axis=0)[:B_QH], "mem_norm": jnp.sum(gmem_acc, axis=0),
        "m_q_norm": fold(gqm_acc, M_HEADS), "m_k_norm": fold(gkm_acc, M_HEADS),
        "b_gate": jnp.sum(bacc, axis=0), "ffn_norm": jnp.sum(g2acc, axis=0),
        "conv_w": csum[1:], "conv_b": csum[0],
    }
    return loss, grad_x, sml


def _mesh_pos():
    return lax.axis_index("x"), lax.axis_index("y"), lax.axis_index("c")


def _chip_peers(x, y):
    return [(1 - x, y), (x, 1 - y), (1 - x, 1 - y)]


_ANY = pl.BlockSpec(memory_space=pl.ANY)


def _comm_call(body, *, name, n_in, out_shape, scratch):
    return pl.pallas_call(body, name=name, in_specs=[_ANY] * n_in, out_specs=[_ANY] * len(out_shape),
                          out_shape=out_shape, scratch_shapes=scratch)


def _remote(src, dst, send_sem, recv_sem, dev):
    return pltpu.make_async_remote_copy(src_ref=src, dst_ref=dst, send_sem=send_sem, recv_sem=recv_sem,
                                        device_id=dev, device_id_type=MESH)


def _pair_join(halves, name):
    nt = len(halves)

    def body(*refs):
        ins, got = refs[:nt], refs[nt:2 * nt]
        send_sems, recv_sems = refs[2 * nt:]
        x, y, c = _mesh_pos()
        cps = []
        for t in range(nt):
            rc = _remote(ins[t], got[t], send_sems.at[t], recv_sems.at[t], (x, y, 1 - c))
            rc.start()
            cps.append(rc)
        for rc in cps:
            rc.wait()

    out_shape = [_sds(hf.shape, hf.dtype) for hf in halves]
    scratch = [pltpu.SemaphoreType.DMA((nt,)), pltpu.SemaphoreType.DMA((nt,))]
    return _comm_call(body, name=name, n_in=nt, out_shape=out_shape, scratch=scratch)(*halves)


_HBM = pl.BlockSpec(memory_space=pltpu.HBM)
_SEMS = pl.BlockSpec(memory_space=pltpu.SEMAPHORE)
_EFFECT = pltpu.SideEffectType.DATAFLOW_SIDE_EFFECTING


def _bcast_copies(ins, lands, send_sems, recv_sems):
    x, y, c = _mesh_pos()
    me = 2 * x + y
    targets = [((px, py, c), 2 * px + py) for px, py in _chip_peers(x, y)] + [((x, y, 1 - c), me)]
    out = []
    for t in range(len(ins)):
        for k, (dev, idx) in enumerate(targets):
            i = t * len(targets) + k
            arrival = lambda t=t, i=i, idx=idx, dev=dev: _remote(ins[t], lands[t].at[idx], send_sems.at[i],
                                                                 recv_sems.at[i], dev)
            out.append((_remote(ins[t], lands[t].at[me], send_sems.at[i], recv_sems.at[i], dev), arrival))
    return out


def _scatter_copies(ins, lands, send_sems, recv_sems):
    x, y, c = _mesh_pos()
    out = []
    for t in range(len(ins)):
        for k, (px, py) in enumerate(_chip_peers(x, y)):
            i = t * 3 + k
            cp = _remote(ins[t].at[2 * px + py], lands[t].at[k], send_sems.at[i], recv_sems.at[i], (px, py, c))
            out.append((cp, lambda cp=cp: cp))
    return out


def _pair_copies(ins, lands, send_sems, recv_sems):
    x, y, c = _mesh_pos()
    out = []
    for t in range(len(ins)):
        hr = ins[t].shape[1] // 2
        give = ins[t].at[:, pl.ds(pl.multiple_of((1 - c) * hr, 8), hr), :]
        cp = _remote(give, lands[t], send_sems.at[t], recv_sems.at[t], (x, y, 1 - c))
        out.append((cp, lambda cp=cp: cp))
    return out


def _half_copies(ins, lands, send_sems, recv_sems):
    x, y, c = _mesh_pos()
    me = 2 * x + y
    out = []
    for t in range(len(ins)):
        hr = ins[t].shape[0] // 2
        rows = pl.ds(pl.multiple_of(c * hr, 8), hr)
        for k, (px, py) in enumerate(_chip_peers(x, y)):
            i = t * 3 + k
            arrival = lambda t=t, i=i, px=px, py=py, rows=rows: _remote(
                ins[t].at[rows, :], lands[t].at[2 * px + py].at[rows, :], send_sems.at[i], recv_sems.at[i], (px, py, c))
            out.append((_remote(ins[t].at[rows, :], lands[t].at[me].at[rows, :], send_sems.at[i], recv_sems.at[i],
                                (px, py, c)), arrival))
    return out


def _finish_halves(shards, stacks):
    nt = len(shards)

    def body(*refs):
        ins, held, outs = refs[:nt], refs[nt:2 * nt], refs[2 * nt:3 * nt]
        fwd_s, fwd_r, own_s, own_r = refs[3 * nt:]
        x, y, c = _mesh_pos()
        me = 2 * x + y
        sib = (x, y, 1 - c)
        pending = []
        for t in range(nt):
            hr = shards[t].shape[0] // 2
            half = lambda ref, who: ref.at[pl.ds(pl.multiple_of(who * hr, 8), hr), :]
            own = _remote(ins[t], outs[t].at[me], own_s.at[t], own_r.at[t], sib)
            own.start()
            pending.append(own.wait)
            for k, (px, py) in enumerate(_chip_peers(x, y)):
                pj = 2 * px + py
                fw = _remote(half(held[t].at[pj], c), half(outs[t].at[pj], c), fwd_s.at[t, k], fwd_r.at[t, k], sib)
                fw.start()
                pending.append(fw.wait_send)
                other = half(outs[t].at[pj], 1 - c)
                pending.append(_remote(other, other, fwd_s.at[t, k], fwd_r.at[t, k], sib).wait_recv)
        for wait in pending:
            wait()

    dma = pltpu.SemaphoreType.DMA
    return pl.pallas_call(
        body, name="gather_w_in_finish", in_specs=[_ANY] * (2 * nt), out_specs=[_ANY] * nt,
        out_shape=[_sds(a.shape, a.dtype) for a in stacks], input_output_aliases={nt + i: i for i in range(nt)},
        scratch_shapes=[dma((nt, 3)), dma((nt, 3)), dma((nt,)), dma((nt,))])(*shards, *stacks)


def _split_start(copies, srcs, land_shapes, ncopy, dep, name):
    nt = len(srcs)

    def body(*refs):
        ins, lands = refs[:nt], refs[nt:2 * nt]
        send_sems, recv_sems, token = refs[2 * nt + 1], refs[2 * nt + 2], refs[-1]
        for send, _ in copies(ins, lands, send_sems, recv_sems):
            send.start()
        token[...] = jnp.zeros_like(token)

    lands = [pltpu.with_memory_space_constraint(lax.empty(sh, a.dtype), pltpu.HBM) for sh, a in zip(land_shapes, srcs)]
    srcs = [pltpu.with_memory_space_constraint(a, pltpu.HBM) for a in srcs]
    dma = pltpu.SemaphoreType.DMA
    out_shape = ([dma((nt * ncopy,)), dma((nt * ncopy,))] + [pltpu.HBM(a.shape, a.dtype) for a in srcs + lands]
                 + [_sds((8, 128))])
    outs = pl.pallas_call(
        body, name=name, in_specs=[_HBM] * (2 * nt) + [_ANY],
        out_specs=[_SEMS, _SEMS] + [_HBM] * (2 * nt) + [pl.BlockSpec(memory_space=pltpu.VMEM)], out_shape=out_shape,
        input_output_aliases={i: 2 + i for i in range(2 * nt)},
        compiler_params=pltpu.CompilerParams(has_side_effects=_EFFECT))(*srcs, *lands, dep)
    return outs[0], outs[1], outs[2:2 + nt], outs[2 + nt:2 + 2 * nt], outs[-1]


def _split_wait(copies, send_sems, recv_sems, srcs, lands, after, name):
    nt = len(srcs)

    def body(*refs):
        ins, lnd = refs[:nt], refs[nt:2 * nt]
        for send, arrival in copies(ins, lnd, refs[2 * nt], refs[2 * nt + 1]):
            send.wait_send()
            arrival().wait_recv()

    outs = pl.pallas_call(
        body, name=name, in_specs=[_HBM] * (2 * nt) + [_SEMS, _SEMS, _ANY], out_specs=[_HBM] * (2 * nt),
        out_shape=[pltpu.HBM(a.shape, a.dtype) for a in list(srcs) + list(lands)],
        input_output_aliases={i: i for i in range(2 * nt)},
        compiler_params=pltpu.CompilerParams(has_side_effects=_EFFECT))(*srcs, *lands, send_sems, recv_sems, after)
    return outs[:nt], outs[nt:]


def _gather_small(packed):
    n = packed.shape[0]

    def body(in_ref, out_ref, send_sems, recv_sems, loc_sem):
        x, y, c = _mesh_pos()
        me = 4 * x + 2 * y + c
        lc = pltpu.make_async_copy(in_ref, out_ref.at[me], loc_sem)
        lc.start()
        peers = []
        for k in range(1, NDEV):
            px, py, pc = x ^ (k >> 2), y ^ ((k >> 1) & 1), c ^ (k & 1)
            rc = pltpu.make_async_remote_copy(src_ref=in_ref, dst_ref=out_ref.at[me], send_sem=send_sems.at[k - 1],
                                              recv_sem=recv_sems.at[k - 1], device_id=(px, py, pc), device_id_type=MESH)
            rc.start()
            peers.append((k, px, py, pc))
        lc.wait()
        for k, px, py, pc in peers:
            pltpu.make_async_remote_copy(src_ref=in_ref, dst_ref=out_ref.at[4 * px + 2 * py + pc],
                                         send_sem=send_sems.at[k - 1], recv_sem=recv_sems.at[k - 1],
                                         device_id=(px, py, pc), device_id_type=MESH).wait()

    scratch = [pltpu.SemaphoreType.DMA((NDEV - 1,)), pltpu.SemaphoreType.DMA((NDEV - 1,)), pltpu.SemaphoreType.DMA]
    return _comm_call(body, name="gather_small_grads", n_in=1, out_shape=[_sds((NDEV, n, 128))],
                      scratch=scratch)(packed)[0]


def _row_tile(r, c):
    t = r
    while t * c * 4 > (1 << 20) and t % 16 == 0:
        t //= 2
    return t


def _k_pair_add(full, got, name):
    g, r, c = full.shape
    hr = r // 2
    tr = _row_tile(hr, c)
    nh = hr // tr

    def body(a_ref, b_ref, o_ref):
        o_ref[...] = (a_ref[...] + b_ref[...]).astype(_WIRE)

    mine = pl.BlockSpec((None, tr, c), lambda i, j: (i, lax.axis_index("c") * nh + j, 0))
    spec = pl.BlockSpec((None, tr, c), lambda i, j: (i, j, 0))
    return _pc(body, name=name, grid=(g, nh), in_specs=[mine, spec], out_specs=[spec],
               out_shape=[_sds((g, hr, c), _WIRE)])(full, got)[0]


def _k_chip_sum(parts, slots, name):
    _, r, c = parts.shape
    tr = _row_tile(r, c)

    def body(a_ref, s_ref, o_ref):
        acc = a_ref[...].astype(F32)
        for k in range(3):
            acc = acc + s_ref[k].astype(F32)
        o_ref[...] = acc

    own = pl.BlockSpec((None, tr, c), lambda i: (2 * lax.axis_index("x") + lax.axis_index("y"), i, 0))
    return _pc(body, name=name, grid=(r // tr,), in_specs=[own, pl.BlockSpec((3, tr, c), lambda i: (0, i, 0))],
               out_specs=[_row(tr, c)], out_shape=[_sds((r, c))])(parts, slots)[0]


def _adam(w, g, m, v):
    m = ADAM_B1 * m + (1.0 - ADAM_B1) * g
    v = ADAM_B2 * v + (1.0 - ADAM_B2) * (g * g)
    m_hat = m / (1.0 - ADAM_B1 ** ADAM_STEP)
    v_hat = v / (1.0 - ADAM_B2 ** ADAM_STEP)
    return -ADAM_LR * (m_hat / (jnp.sqrt(v_hat) + ADAM_EPS) + ADAM_WD * w), m, v


def _k_adam(w, mine, theirs, m, v, dep, name):
    r, c = w.shape
    hr = r // 2
    tr = _row_tile(hr, c)
    nh = hr // tr

    def body(w_ref, a_ref, b_ref, m_ref, v_ref, dep_ref, g_ref, d_ref, mo_ref, vo_ref):
        upper = (pl.program_id(0) >= nh).astype(jnp.int32)
        g = jnp.where(upper == lax.axis_index("c"), a_ref[...], b_ref[...])
        g_ref[...] = g
        d_ref[...], mo_ref[...], vo_ref[...] = _adam(w_ref[...], g, m_ref[...], v_ref[...])

    hspec = pl.BlockSpec((tr, c), lambda i: (jnp.where(i >= nh, i - nh, i), 0))
    return _pc(body, name=name, grid=(r // tr,),
               in_specs=[_row(tr, c), hspec, hspec, _row(tr, c), _row(tr, c), _res((8, 128))],
               out_specs=[_row(tr, c)] * 4, out_shape=[_sds((r, c))] * 4)(w, mine, theirs, m, v, dep)


def _k_sum8(a):
    _, n, _ = a.shape

    def body(a_ref, o_ref):
        acc = a_ref[0]
        for k in range(1, NDEV):
            acc = acc + a_ref[k]
        o_ref[...] = acc

    return _pc(body, name="sum_small_grads", grid=(1,), in_specs=[_acc(a.shape)], out_specs=[_acc((n, 128))],
               out_shape=[_sds((n, 128))])(a)[0]


def _k_adam_small(ws, gs, ms, vs):
    n = len(ws)

    def body(*refs):
        for k in range(n):
            w_ref, g_ref, m_ref, v_ref, d_ref, mo_ref, vo_ref = refs[k::n]
            d_ref[...], mo_ref[...], vo_ref[...] = _adam(w_ref[...], g_ref[...], m_ref[...], v_ref[...])

    specs = [_acc(a.shape) for a in ws]
    outs = _pc(body, name="adam_small", grid=(1,), in_specs=specs * 4, out_specs=specs * 3,
               out_shape=[_sds(a.shape) for a in ws] * 3)(*ws, *gs, *ms, *vs)
    return outs[:n], outs[n:2 * n], outs[2 * n:]


def _pack(vals):
    rows = []
    for a in vals:
        flat = a.reshape(-1)
        n = -(-flat.shape[0] // 1024) * 1024
        rows.append(jnp.pad(flat, (0, n - flat.shape[0])).reshape(n // 128, 128))
    return jnp.concatenate(rows, axis=0)


def _unpack(packed, shapes):
    out, off = [], 0
    for sh in shapes:
        size = int(np.prod(sh))
        n = -(-size // 1024) * 1024
        out.append(packed[off // 128:(off + n) // 128].reshape(-1)[:size].reshape(sh))
        off += n
    return out


_WEIGHTS = ["attn_norm", "w_in", "a_q_norm", "a_k_norm", "b_q_norm", "b_k_norm", "b_sinks", "mem_norm", "w_mem_kv",
            "m_q_norm", "m_k_norm", "w_o_a", "w_o_b", "w_o_m", "w_gate", "b_gate", "w_out", "ffn_norm", "w_up",
            "conv_w", "conv_b", "w_down"]
_BIG = ["w_in", "w_mem_kv", "w_o_a", "w_o_b", "w_o_m", "w_gate", "w_out", "w_up", "w_down"]
_SMALL = [n for n in _WEIGHTS if n not in _BIG]


def kernel(x, mem, positions, attn_norm, w_in, a_q_norm, a_k_norm, b_q_norm, b_k_norm, b_sinks, mem_norm, w_mem_kv, m_q_norm, m_k_norm, w_o_a, w_o_b, w_o_m, w_gate, b_gate, w_out, ffn_norm, w_up, conv_w, conv_b, w_down, loss_target, m_attn_norm, m_w_in, m_a_q_norm, m_a_k_norm, m_b_q_norm, m_b_k_norm, m_b_sinks, m_mem_norm, m_w_mem_kv, m_m_q_norm, m_m_k_norm, m_w_o_a, m_w_o_b, m_w_o_m, m_w_gate, m_b_gate, m_w_out, m_ffn_norm, m_w_up, m_conv_w, m_conv_b, m_w_down, v_attn_norm, v_w_in, v_a_q_norm, v_a_k_norm, v_b_q_norm, v_b_k_norm, v_b_sinks, v_mem_norm, v_w_mem_kv, v_m_q_norm, v_m_k_norm, v_w_o_a, v_w_o_b, v_w_o_m, v_w_gate, v_b_gate, v_w_out, v_ffn_norm, v_w_up, v_conv_w, v_conv_b, v_w_down):
    given = dict(locals())
    w = {n: given[n][0] for n in _WEIGHTS}
    m1 = {n: given["m_" + n][0] for n in _WEIGHTS}
    m2 = {n: given["v_" + n][0] for n in _WEIGHTS}

    zeros = jnp.zeros((8, 128), F32)
    w_in_shard = w["w_in"].astype(_MM)
    *w_in_handles, tok = _split_start(_half_copies, [w_in_shard], [(CHIPS,) + w_in_shard.shape], 3, zeros,
                                      "gather_w_in_start")

    def get_w_in(after):
        send, recv, srcs, lands = w_in_handles
        srcs, lands = _split_wait(_half_copies, send, recv, srcs, lands, after, "gather_w_in_wait")
        return _finish_halves(srcs, lands)[0]

    stages = (["w_gate", "w_mem_kv", "w_o_a", "w_o_b", "w_o_m", "w_out"], ["w_up", "w_down", "conv_w"])
    started = []
    for k, names in enumerate(stages):
        shards = [w[n] if n == "conv_w" else w[n].astype(_MM) for n in names]
        *handles, tok = _split_start(_bcast_copies, shards, [(CHIPS,) + a.shape for a in shards], 4, tok,
                                     "gather_start_%d" % k)
        started.append(handles)
    small = {n: (w[n][None, :] if w[n].ndim == 1 else w[n]) for n in _SMALL if n != "conv_w"}
    positions = positions + tok[0:1, 0:1].astype(positions.dtype)

    def get_rest(stage, after):
        send, recv, srcs, lands = started[stage]
        got = _split_wait(_bcast_copies, send, recv, srcs, lands, after, "gather_wait_%d" % stage)[1]
        wts = dict(zip(stages[stage], got))
        for n in ("w_mem_kv", "w_out", "w_down"):
            if n in wts:
                wts[n] = wts[n].reshape(-1, wts[n].shape[-1])
        return wts

    parts, slots, pair, scat, started_pair = {}, {}, [], [], [None]

    def finish_pair(after):
        names, tag, send, recv, srcs, lands = pair.pop()
        full, got = _split_wait(_pair_copies, send, recv, srcs, lands, after, "pair_wait_" + tag)
        mine = [_k_pair_add(f, b, "pair_add_" + n) for n, f, b in zip(names, full, got)]
        shapes = [(3,) + p.shape[1:] for p in mine]
        send, recv, srcs, lands, token = _split_start(_scatter_copies, mine, shapes, 3, zeros, "scatter_start_" + tag)
        scat.append((names, tag, send, recv, srcs, lands))
        return token

    def on_grads(group, after):
        names = list(group)
        tag = "_".join(names)
        token = finish_pair(after) if pair else zeros
        if not group:
            return token
        grads_g = [group[n] for n in names]
        shapes = [(CHIPS, g.shape[1] // 2, g.shape[2]) for g in grads_g]
        send, recv, srcs, lands, token = _split_start(_pair_copies, grads_g, shapes, 1, token, "pair_start_" + tag)
        pair.append((names, tag, send, recv, srcs, lands))
        started_pair[0] = token
        return token

    loss, grad_x, sml = _local_step(x[0], mem[0], positions[0], loss_target[0], small, get_w_in, get_rest, on_grads)
    loss = lax.psum(loss, ("x", "y", "c"))
    early = [n for names, *_ in scat for n in names]
    for names, tag, send, recv, srcs, lands in scat:
        mine, got = _split_wait(_scatter_copies, send, recv, srcs, lands, started_pair[0], "scatter_wait_" + tag)
        parts.update(zip(names, mine))
        slots.update(zip(names, got))
    scat.clear()
    reduced = {n: _k_chip_sum(parts[n], slots[n], "chip_add_" + n) for n in early}
    theirs = dict(zip(early, _pair_join([reduced[n] for n in early], "grad_pair_join_early")))
    grads = {}

    shapes = [sml[n].shape for n in _SMALL]
    gsm = dict(zip(_SMALL, _unpack(_k_sum8(_gather_small(_pack([sml[n] for n in _SMALL]))), shapes)))
    nu = w["conv_w"].shape[1]
    chip = 2 * lax.axis_index("x") + lax.axis_index("y")
    gsm["conv_w"] = lax.dynamic_slice_in_dim(gsm["conv_w"], chip * nu, nu, axis=1)
    for n in _SMALL:
        grads[n] = gsm[n].reshape(w[n].shape)

    delta, new_m, new_v = {}, {}, {}
    dep = finish_pair(theirs[early[0]])
    for n in early:
        grads[n], delta[n], new_m[n], new_v[n] = _k_adam(w[n], reduced[n], theirs[n], m1[n], m2[n], dep, "adam_" + n)
        dep = delta[n]
    as2d = lambda d: [d[n][None, :] if d[n].ndim == 1 else d[n] for n in _SMALL]
    for dst, outs in zip((delta, new_m, new_v), _k_adam_small(as2d(w), as2d(grads), as2d(m1), as2d(m2))):
        dst.update((n, a.reshape(w[n].shape)) for n, a in zip(_SMALL, outs))
    late, tag, send, recv, srcs, lands = scat.pop()
    mine, got = _split_wait(_scatter_copies, send, recv, srcs, lands, dep, "scatter_wait_" + tag)
    for n, a, b in zip(late, mine, got):
        reduced[n] = _k_chip_sum(a, b, "chip_add_" + n)
    theirs.update(zip(late, _pair_join([reduced[n] for n in late], "grad_pair_join_late")))
    for n in late:
        grads[n], delta[n], new_m[n], new_v[n] = _k_adam(w[n], reduced[n], theirs[n], m1[n], m2[n], zeros, "adam_" + n)

    lead = lambda d: [d[n][None] for n in _WEIGHTS]
    return (loss, grad_x[None], *lead(grads), *lead(delta), *lead(new_m), *lead(new_v))
```

```python
import math

import jax
import jax.numpy as jnp
import numpy as np
from jax import lax
from jax.experimental import pallas as pl
from jax.experimental.pallas import tpu as pltpu

F32 = jnp.float32
_MM = jnp.bfloat16
_WIRE = jnp.bfloat16

D_MODEL = 1024
HEAD = 64
BLK = 128
A_GROUPS = ((128, 1), (512, 4), (2048, 16))
A_HEADS = 4
A_W = A_HEADS * HEAD
B_QH = 8
B_KVH = 2
B_WINDOW = 128
M_HEADS = 4
M_HD = 128
M_W = M_HEADS * M_HD
D_FF = 2816
EPS = 1e-6
NEG = -1e30
ROPE_THETA = 500000.0
ROPE_ROT = 16
CHIPS = 4
NDEV = 8
ADAM_LR, ADAM_B1, ADAM_B2, ADAM_EPS, ADAM_WD, ADAM_STEP = 0.001, 0.9, 0.999, 1e-08, 0.01, 10
VMEM_LIMIT = 58 * 1024 * 1024
MESH = pl.DeviceIdType.MESH


def _pc(body, *, name, grid, in_specs, out_specs, out_shape, scratch=()):
    return pl.pallas_call(
        body, name=name, grid=grid, in_specs=in_specs, out_specs=out_specs, out_shape=out_shape,
        scratch_shapes=list(scratch),
        compiler_params=pltpu.CompilerParams(dimension_semantics=("arbitrary",) * len(grid),
                                             vmem_limit_bytes=VMEM_LIMIT))


def _row(ts, c, col=0):
    return pl.BlockSpec((ts, c), lambda i: (i, col))


def _res(shape):
    n = len(shape)
    return pl.BlockSpec(tuple(shape), lambda i: (0,) * n, pipeline_mode=pl.Buffered(1))


def _acc(shape):
    n = len(shape)
    return pl.BlockSpec(tuple(shape), lambda i: (0,) * n)


def _sds(shape, dtype=F32):
    return jax.ShapeDtypeStruct(tuple(shape), dtype)


def _dot(a, b):
    return jnp.dot(a.astype(_MM), b.astype(_MM), preferred_element_type=F32)


def _dot_nt(a, b):
    return lax.dot_general(a.astype(_MM), b.astype(_MM), (((1,), (1,)), ((), ())), preferred_element_type=F32)


def _dot_tn(a, b):
    return lax.dot_general(a.astype(_MM), b.astype(_MM), (((0,), (0,)), ((), ())), preferred_element_type=F32)


def _sum8(v):
    ts, c = v.shape
    return jnp.sum(v.reshape(ts // 8, 8, c), axis=0)


def _sigmoid(z):
    return 1.0 / (1.0 + jnp.exp(-z))


def _rms(x):
    r = lax.rsqrt(jnp.mean(x * x, axis=-1, keepdims=True) + EPS)
    return x * r, r


def _rms_bwd(dy, xh, r, gain):
    z = dy * gain
    return r * (z - xh * jnp.mean(z * xh, axis=-1, keepdims=True))


def _split_hi_lo(v):
    hi = v.astype(_MM)
    return hi, (v - hi.astype(F32)).astype(_MM)


def _lane_head(shape):
    return lax.shift_right_logical(lax.broadcasted_iota(jnp.int32, shape, len(shape) - 1), 6)


def _seg_sum64(v):
    w = v.shape[1]
    e = jnp.where(_lane_head((w, w)) == lax.shift_right_logical(lax.broadcasted_iota(jnp.int32, (w, w), 0), 6),
                  1.0, 0.0).astype(_MM)
    hi, lo = _split_hi_lo(v)
    return jnp.dot(hi, e, preferred_element_type=F32) + jnp.dot(lo, e, preferred_element_type=F32)


def _seg_norm(x, seg):
    if seg == HEAD:
        r = lax.rsqrt(_seg_sum64(x * x) * (1.0 / HEAD) + EPS)
        return x * r, r
    w = x.shape[1]
    xh, rr = [], []
    for s in range(w // seg):
        xs = x[:, s * seg:(s + 1) * seg]
        r = lax.rsqrt(jnp.mean(xs * xs, axis=-1, keepdims=True) + EPS)
        xh.append(xs * r)
        rr.append(jnp.broadcast_to(r, xs.shape))
    return jnp.concatenate(xh, axis=1), jnp.concatenate(rr, axis=1)


def _seg_mean(v, seg):
    if seg == HEAD:
        return _seg_sum64(v) * (1.0 / HEAD)
    w = v.shape[1]
    out = []
    for s in range(w // seg):
        vs = v[:, s * seg:(s + 1) * seg]
        out.append(jnp.broadcast_to(jnp.mean(vs, axis=-1, keepdims=True), vs.shape))
    return jnp.concatenate(out, axis=1)


def _rope(t, c, sa, sb):
    out = []
    for cb in range(t.shape[1] // 128):
        tc = t[:, cb * 128:(cb + 1) * 128]
        out.append(tc * c + pltpu.roll(tc, 120, 1) * sa + pltpu.roll(tc, 8, 1) * sb)
    return jnp.concatenate(out, axis=1) if len(out) > 1 else out[0]


def _rope_bwd(dy, c, sa, sb):
    out = []
    for cb in range(dy.shape[1] // 128):
        dc = dy[:, cb * 128:(cb + 1) * 128]
        out.append(dc * c + pltpu.roll(dc * sa, 8, 1) + pltpu.roll(dc * sb, 120, 1))
    return jnp.concatenate(out, axis=1) if len(out) > 1 else out[0]


def _rope_consts():
    half = ROPE_ROT // 2
    c = np.float32(-2.0 * math.log(ROPE_THETA) / ROPE_ROT)
    freqs = np.exp(np.arange(half, dtype=np.float32) * c).astype(np.float32)
    place = np.zeros((3, half, 128), np.float32)
    ones = np.zeros((1, 128), np.float32)
    for lane in range(128):
        d = lane % HEAD
        if d < half:
            place[0, d, lane], place[1, d, lane] = 1.0, -1.0
        elif d < ROPE_ROT:
            place[0, d - half, lane], place[2, d - half, lane] = 1.0, 1.0
        else:
            ones[0, lane] = 1.0
    return np.tile(freqs[:, None], (1, 128)), place, ones


def _rope_tables(pos_rows):
    r = pos_rows.shape[0]
    tr = min(1024, r)
    freqs, place, ones = _rope_consts()

    def split3(v):
        hi, mid = _split_hi_lo(v)
        lo = (v - hi.astype(F32) - mid.astype(F32)).astype(_MM)
        return hi, mid, lo

    def body(p_ref, f_ref, e_ref, one_ref, c_ref, sa_ref, sb_ref):
        ang = jnp.concatenate([p_ref[j:j + 1, :].astype(F32) * f_ref[...] for j in range(tr // 128)], axis=1)
        cos, sin = jnp.cos(ang), jnp.sin(ang)
        for ref, k, v in ((c_ref, 0, cos), (sa_ref, 1, sin), (sb_ref, 2, sin)):
            e = e_ref[k].astype(_MM)
            out = sum(_dot_tn(part, e) for part in split3(v))
            ref[...] = out + one_ref[...] if k == 0 else out

    return _pc(body, name="rope_tables", grid=(r // tr,),
               in_specs=[pl.BlockSpec((tr // 128, 128), lambda i: (i, 0)), _acc((ROPE_ROT // 2, 128)),
                         _acc((3, ROPE_ROT // 2, 128)), _acc((1, 128))],
               out_specs=[_row(tr, 128)] * 3, out_shape=[_sds((r, 128))] * 3)(
                   pos_rows.reshape(r // 128, 128), jnp.asarray(freqs), jnp.asarray(place), jnp.asarray(ones))


def _k_in(x, g1, w_in):
    s = x.shape[0]
    ts = min(512, s)
    nin = w_in.shape[2]
    ncol = CHIPS * nin
    a_cols = 3 * A_W
    offs = [0, a_cols, 2 * a_cols, 3 * a_cols, 3 * a_cols + B_QH * HEAD,
            3 * a_cols + (B_QH + B_KVH) * HEAD, 3 * a_cols + (B_QH + 2 * B_KVH) * HEAD, ncol]

    def body(x_ref, g_ref, wi_ref, h_ref, a0, a1, a2, qb, kb, vb, mq, p_scr):
        xh, _ = _rms(x_ref[...])
        h = (xh * g_ref[...]).astype(_MM)
        h_ref[...] = h
        for j in range(CHIPS):
            p_scr[:, j * nin:(j + 1) * nin] = jnp.dot(h, wi_ref[j], preferred_element_type=F32)
        for k, ref in enumerate((a0, a1, a2, qb, kb, vb, mq)):
            ref[...] = p_scr[:, offs[k]:offs[k + 1]]

    widths = [offs[k + 1] - offs[k] for k in range(7)]
    return _pc(
        body, name="in_proj", grid=(s // ts,),
        in_specs=[_row(ts, D_MODEL), _res((1, D_MODEL)), _res(w_in.shape)],
        out_specs=[_row(ts, D_MODEL)] + [_row(ts, w) for w in widths],
        out_shape=[_sds((s, D_MODEL), _MM)] + [_sds((s, w)) for w in widths],
        scratch=[pltpu.VMEM((ts, ncol), F32)])(x, g1, w_in)


def _k_gate(h, w_gate, b_gate):
    s = h.shape[0]
    ts = min(256, s)
    ng = w_gate.shape[2]

    def body(h_ref, wg_ref, bg_ref, gt_ref):
        h = h_ref[...]
        for j in range(CHIPS):
            z = jnp.dot(h, wg_ref[j], preferred_element_type=F32) + bg_ref[:, j * ng:(j + 1) * ng]
            gt_ref[:, j * ng:(j + 1) * ng] = _sigmoid(z)

    return _pc(body, name="gate_proj", grid=(s // ts,),
               in_specs=[_row(ts, D_MODEL), _res(w_gate.shape), _res(b_gate.shape)],
               out_specs=[_row(ts, CHIPS * ng)], out_shape=[_sds((s, CHIPS * ng))])(h, w_gate, b_gate)[0]


def _k_prep(srcs, gq, gk, tabs, tab_row, *, wq, wk, rows_per_gain, name):
    rows = srcs[0][0].shape[0]
    ts = min(512, rows)

    def body(q_ref, k_ref, v_ref, gq_ref, gk_ref, c_ref, sa_ref, sb_ref, qn_ref, kn_ref, vn_ref):
        c, sa, sb = c_ref[...], sa_ref[...], sb_ref[...]
        qh, _ = _seg_norm(q_ref[...], HEAD)
        qn_ref[...] = _rope(qh * gq_ref[...], c, sa, sb).astype(_MM)
        kh, _ = _seg_norm(k_ref[...], HEAD)
        kn_ref[...] = _rope(kh * gk_ref[...], c, sa, sb).astype(_MM)
        vn_ref[...] = v_ref[...].astype(_MM)

    gspec = lambda w: pl.BlockSpec((None, 1, w), lambda i: ((i * ts) // rows_per_gain, 0, 0))
    return _pc(
        body, name=name, grid=(rows // ts,),
        in_specs=[_row(ts, wq, srcs[0][1]), _row(ts, wk, srcs[1][1]), _row(ts, wk, srcs[2][1]),
                  gspec(wq), gspec(wk)] + [pl.BlockSpec((ts, 128), lambda i: (i + tab_row // ts, 0))] * 3,
        out_specs=[_row(ts, wq), _row(ts, wk), _row(ts, wk)],
        out_shape=[_sds((rows, wq), _MM), _sds((rows, wk), _MM), _sds((rows, wk), _MM)])(
            srcs[0][0], srcs[1][0], srcs[2][0], gq, gk, *tabs)


def _first_flag(b, segs, nb):
    first = b >= nb
    for k, (start, period) in enumerate(segs):
        end = segs[k + 1][0] if k + 1 < len(segs) else nb
        first = first | ((b >= start) & (b < end) & (lax.rem(b - start, jnp.int32(period)) == 0))
    return first


def _band_bias(thr, with_cur):
    qi = lax.broadcasted_iota(jnp.int32, (BLK, BLK), 0)
    kj = lax.broadcasted_iota(jnp.int32, (BLK, BLK), 1)
    prev = jnp.where(kj >= qi + thr, 0.0, NEG)
    return jnp.concatenate([prev, jnp.where(kj <= qi, 0.0, NEG)], axis=1) if with_cur else prev


def _blockdiag(t4):
    head = _lane_head((1, A_W))
    return jnp.concatenate([t4 * jnp.where(head == h, 1.0, 0.0).astype(t4.dtype) for h in range(A_HEADS)], axis=0)


def _fold_diag(t, n):
    head = _lane_head((n, A_W))
    out = t[3 * n:4 * n]
    for h in (2, 1, 0):
        out = jnp.where(head == h, t[h * n:(h + 1) * n], out)
    return out


def _expand_heads(cols):
    n = cols[0].shape[0]
    head = _lane_head((n, A_W))
    out = jnp.broadcast_to(cols[3], (n, A_W))
    for h in (2, 1, 0):
        out = jnp.where(head == h, cols[h], out)
    return out


def _unit_kv(pieces, u, shared):
    cols = slice(u * HEAD, (u + 1) * HEAD) if shared else slice(u * A_W, (u + 1) * A_W)
    rows = [ref[rs, cols] for ref, rs in pieces]
    k = rows[0] if len(rows) == 1 else jnp.concatenate(rows, axis=0)
    return jnp.concatenate([k] * A_HEADS, axis=1) if shared else k


_LO, _HI, _BOTH = slice(0, BLK), slice(BLK, 2 * BLK), slice(0, 2 * BLK)


def _k_band_fwd(qn, kn, vn, *, hq, hk, max_dist, segs, sink, name):
    rows = qn.shape[0]
    nb = rows // BLK
    units = hq // A_HEADS
    shared = hk != hq
    wq, wk = hq * HEAD, hk * HEAD
    scale = HEAD ** -0.5

    def body(*refs):
        if sink is None:
            q_ref, kc_ref, kp_ref, vc_ref, vp_ref, o_ref, l_ref = refs
        else:
            q_ref, kc_ref, kp_ref, vc_ref, vp_ref, sk_ref, o_ref, l_ref = refs
        i = pl.program_id(0)
        for half, rs in enumerate((_LO, _HI)):
            bias = _band_bias(jnp.where(_first_flag(2 * i + half, segs, nb), 1 << 20, BLK - max_dist), True)
            kpieces = ((kp_ref, _LO), (kc_ref, _LO)) if half == 0 else ((kc_ref, _BOTH),)
            vpieces = ((vp_ref, _LO), (vc_ref, _LO)) if half == 0 else ((vc_ref, _BOTH),)
            for u in range(units):
                us = slice(u * A_W, (u + 1) * A_W)
                kb = _blockdiag(_unit_kv(kpieces, u, shared))
                vb = _blockdiag(_unit_kv(vpieces, u, shared))
                s_all = _dot_nt(q_ref[rs, us], kb) * scale
                ps, ls = [], []
                for h in range(A_HEADS):
                    s = s_all[:, h * 2 * BLK:(h + 1) * 2 * BLK] + bias
                    m = jnp.max(s, axis=-1, keepdims=True)
                    e = jnp.exp(s - m)
                    lse = m + jnp.log(jnp.sum(e, axis=-1, keepdims=True))
                    if sink is not None:
                        sk = sk_ref[u * A_HEADS + h]
                        mx = jnp.maximum(lse, sk)
                        lse = mx + jnp.log(jnp.exp(lse - mx) + jnp.exp(sk - mx))
                    ps.append((e * jnp.exp(m - lse)).astype(_MM))
                    ls.append(lse)
                o_ref[rs, us] = _dot(jnp.concatenate(ps, axis=1), vb)
                l_ref[rs, us] = _expand_heads(ls)

    two = lambda w: pl.BlockSpec((2 * BLK, w), lambda i: (i, 0))
    prev = lambda w: pl.BlockSpec((BLK, w), lambda i: (jnp.maximum(2 * i - 1, 0), 0))
    in_specs = [two(wq), two(wk), prev(wk), two(wk), prev(wk)]
    args = [qn, kn, kn, vn, vn]
    if sink is not None:
        in_specs.append(pl.BlockSpec(memory_space=pltpu.SMEM))
        args.append(sink)
    return _pc(body, name=name, grid=(nb // 2,), in_specs=in_specs, out_specs=[two(wq), two(wq)],
               out_shape=[_sds((rows, wq)), _sds((rows, wq))])(*args)


def _k_memkv(mem, mem_norm, w_kv, m_k_norm):
    n = mem.shape[0]

    def body(m_ref, g_ref, w_ref, gk_ref, mn_ref, kv_ref, mk_ref, mv_ref):
        mh, _ = _rms(m_ref[...])
        mn = (mh * g_ref[...]).astype(_MM)
        mn_ref[...] = mn
        kv = jnp.dot(mn, w_ref[...], preferred_element_type=F32)
        kv_ref[...] = kv
        kh, _ = _seg_norm(kv[:, :M_W], M_HD)
        mk_ref[...] = (kh * gk_ref[...]).astype(_MM)
        mv_ref[...] = kv[:, M_W:].astype(_MM)

    return _pc(body, name="mem_kv", grid=(1,),
               in_specs=[_acc((n, D_MODEL)), _acc((1, D_MODEL)), _acc(w_kv.shape), _acc((1, M_W))],
               out_specs=[_acc((n, D_MODEL)), _acc((n, 2 * M_W)), _acc((n, M_W)), _acc((n, M_W))],
               out_shape=[_sds((n, D_MODEL), _MM), _sds((n, 2 * M_W)), _sds((n, M_W), _MM), _sds((n, M_W), _MM)])(
                   mem, mem_norm, w_kv, m_k_norm)


def _mem_probs(q, mk):
    sc = _dot_nt(q, mk) * (M_HD ** -0.5)
    e = jnp.exp(sc - jnp.max(sc, axis=-1, keepdims=True))
    return e / jnp.sum(e, axis=-1, keepdims=True)


def _k_mem_fwd(m_q, gq, mk, mv):
    s = m_q.shape[0]
    n = mk.shape[0]
    ts = min(512, s)

    def body(q_ref, g_ref, mk_ref, mv_ref, o_ref):
        qh, _ = _seg_norm(q_ref[...], M_HD)
        qn = (qh * g_ref[...]).astype(_MM)
        for h in range(M_HEADS):
            hs = slice(h * M_HD, (h + 1) * M_HD)
            o_ref[:, hs] = _dot(_mem_probs(qn[:, hs], mk_ref[:, hs]), mv_ref[:, hs])

    return _pc(body, name="mem_attn", grid=(s // ts,),
               in_specs=[_row(ts, M_W), _res((1, M_W)), _res((n, M_W)), _res((n, M_W))],
               out_specs=[_row(ts, M_W)], out_shape=[_sds((s, M_W))])(m_q, gq, mk, mv)[0]


def _group_weights(l0, l1, l2):
    m = jnp.maximum(jnp.maximum(l0, l1), l2)
    e0, e1, e2 = jnp.exp(l0 - m), jnp.exp(l1 - m), jnp.exp(l2 - m)
    inv = 1.0 / (e0 + e1 + e2)
    return e0 * inv, e1 * inv, e2 * inv


def _branch_products(oa, ob, om, woa_ref, wob_ref, wom_ref, j):
    return _dot(oa, woa_ref[j]), _dot(ob, wob_ref[j]), _dot(om, wom_ref[j])


def _k_merge(og, lg, o_b, o_m, gates, x, w_oa, w_ob, w_om, w_out, g2):
    s = x.shape[0]
    ts = min(256, s)
    nc = w_oa.shape[2]

    def body(o0, o1, o2, l0, l1, l2, ob_ref, om_ref, gt_ref, x_ref, woa, wob, wom, wout, g_ref,
             oa_ref, mer_ref, x1_ref, h2_ref, m_scr):
        w0, w1, w2 = _group_weights(l0[...], l1[...], l2[...])
        oa = w0 * o0[...] + w1 * o1[...] + w2 * o2[...]
        oa_ref[...] = oa
        ob, om = ob_ref[...], om_ref[...]
        for j in range(CHIPS):
            pa, pb, pm = _branch_products(oa, ob, om, woa, wob, wom, j)
            cs = lambda br: slice(br * D_MODEL + j * nc, br * D_MODEL + (j + 1) * nc)
            m_scr[:, j * nc:(j + 1) * nc] = gt_ref[:, cs(0)] * pa + gt_ref[:, cs(1)] * pb + gt_ref[:, cs(2)] * pm
        mer = m_scr[...].astype(_MM)
        mer_ref[...] = mer
        x1 = x_ref[...] + jnp.dot(mer, wout[...], preferred_element_type=F32)
        x1_ref[...] = x1
        xh, _ = _rms(x1)
        h2_ref[...] = (xh * g_ref[...]).astype(_MM)

    return _pc(
        body, name="merge_out", grid=(s // ts,),
        in_specs=[_row(ts, A_W)] * 6 + [_row(ts, B_QH * HEAD), _row(ts, M_W), _row(ts, 3 * D_MODEL), _row(ts, D_MODEL),
                                         _res(w_oa.shape), _res(w_ob.shape), _res(w_om.shape), _res(w_out.shape),
                                         _res((1, D_MODEL))],
        out_specs=[_row(ts, A_W), _row(ts, D_MODEL), _row(ts, D_MODEL), _row(ts, D_MODEL)],
        out_shape=[_sds((s, A_W)), _sds((s, D_MODEL), _MM), _sds((s, D_MODEL)), _sds((s, D_MODEL), _MM)],
        scratch=[pltpu.VMEM((ts, D_MODEL), F32)])(*og, *lg, o_b, o_m, gates, x, w_oa, w_ob, w_om, w_out, g2)


def _k_up(h2, w_up):
    s = h2.shape[0]
    ts = min(256, s)
    nu = w_up.shape[2]

    def body(h_ref, w_ref, u_ref):
        h = h_ref[...]
        for j in range(CHIPS):
            u_ref[:, j * nu:(j + 1) * nu] = jnp.dot(h, w_ref[j], preferred_element_type=F32)

    return _pc(body, name="up_proj", grid=(s // ts,), in_specs=[_row(ts, D_MODEL), _res(w_up.shape)],
               out_specs=[_row(ts, CHIPS * nu)], out_shape=[_sds((s, CHIPS * nu))])(h2, w_up)[0]


def _shift_down(v, halo, k):
    rolled = pltpu.roll(v, k, 0)
    row = lax.broadcasted_iota(jnp.int32, (8, v.shape[1]), 0)
    slab = rolled[0:8]
    for r in range(k):
        slab = jnp.where(row == r, halo[8 - k + r:8 - k + r + 1, :], slab)
    return jnp.concatenate([slab, rolled[8:]], axis=0)


def _shift_up(v, halo, k):
    ts = v.shape[0]
    rolled = pltpu.roll(v, ts - k, 0)
    row = lax.broadcasted_iota(jnp.int32, (8, v.shape[1]), 0)
    slab = rolled[ts - 8:]
    for r in range(k):
        slab = jnp.where(row == 8 - k + r, halo[r:r + 1, :], slab)
    return jnp.concatenate([rolled[:ts - 8], slab], axis=0)


def _k_ffn(u, conv_w, conv_b, w_down, w_down_t, x1, target):
    s = u.shape[0]
    ts = min(256, s)
    nu = conv_w.shape[2]
    half = CHIPS // 2

    def body(u_ref, uh_ref, cw_ref, cb_ref, wd_ref, wdt_ref, x1_ref, t_ref, dy_ref, f_ref, dc_ref, loss_ref, c_scr,
             f_scr, s_scr):
        i = pl.program_id(0)
        halo = jnp.where(i > 0, uh_ref[...], 0.0)
        for j in range(CHIPS):
            cs = slice(j * nu, (j + 1) * nu)
            uj = u_ref[:, cs]
            hj = halo[:, cs]
            c_scr[:, cs] = (cb_ref[:, cs] + cw_ref[j, 0:1, :] * _shift_down(uj, hj, 2)
                            + cw_ref[j, 1:2, :] * _shift_down(uj, hj, 1) + cw_ref[j, 2:3, :] * uj)
        for j in range(half):
            a = c_scr[:, j * nu:(j + 1) * nu]
            g = c_scr[:, (half + j) * nu:(half + j + 1) * nu]
            sa = _sigmoid(a)
            s_scr[:, j * nu:(j + 1) * nu] = sa
            f_scr[:, j * nu:(j + 1) * nu] = (a * sa * g).astype(_MM)
        f = f_scr[...]
        f_ref[...] = f
        y = x1_ref[...] + jnp.dot(f, wd_ref[...], preferred_element_type=F32)
        err = y - t_ref[...]
        dy = err * (1.0 / D_MODEL)
        dy_ref[...] = dy

        @pl.when(i == 0)
        def _():
            loss_ref[...] = jnp.zeros_like(loss_ref)

        loss_ref[...] += _sum8(err * err)
        df = _dot(dy, wdt_ref[...])
        for j in range(half):
            a = c_scr[:, j * nu:(j + 1) * nu]
            g = c_scr[:, (half + j) * nu:(half + j + 1) * nu]
            sa = s_scr[:, j * nu:(j + 1) * nu]
            dfj = df[:, j * nu:(j + 1) * nu]
            dc_ref[:, j * nu:(j + 1) * nu] = dfj * g * (sa * (1.0 + a * (1.0 - sa)))
            dc_ref[:, (half + j) * nu:(half + j + 1) * nu] = dfj * (a * sa)

    wide = CHIPS * nu
    return _pc(
        body, name="conv_ffn", grid=(s // ts,),
        in_specs=[_row(ts, wide), pl.BlockSpec((8, wide), lambda i: (jnp.maximum(i * (ts // 8) - 1, 0), 0)),
                  _res(conv_w.shape), _res((1, wide)), _res(w_down.shape), _res(w_down_t.shape), _row(ts, D_MODEL),
                  _row(ts, D_MODEL)],
        out_specs=[_row(ts, D_MODEL), _row(ts, D_FF), _row(ts, wide), _acc((8, D_MODEL))],
        out_shape=[_sds((s, D_MODEL)), _sds((s, D_FF), _MM), _sds((s, wide)), _sds((8, D_MODEL))],
        scratch=[pltpu.VMEM((ts, wide), F32), pltpu.VMEM((ts, D_FF), _MM), pltpu.VMEM((ts, D_FF), F32)])(
            u, u, conv_w, conv_b, w_down, w_down_t, x1, target)


def _k_conv_bwd(dc, u, conv_w, w_up, x1, g2, dy):
    s = u.shape[0]
    ts = min(256, s)
    nu = conv_w.shape[2]
    wide = CHIPS * nu
    last = s // ts - 1

    def body(dc_ref, dn_ref, u_ref, cw_ref, wu_ref, x1_ref, g_ref, dy_ref, dx1_ref, du_ref, cacc_ref, gacc_ref):
        i = pl.program_id(0)

        @pl.when(i == 0)
        def _():
            cacc_ref[...] = jnp.zeros_like(cacc_ref)
            gacc_ref[...] = jnp.zeros_like(gacc_ref)

        dhalo = jnp.where(i < last, dn_ref[...], 0.0)
        dh2 = jnp.zeros((ts, D_MODEL), F32)
        for j in range(CHIPS):
            cs = slice(j * nu, (j + 1) * nu)
            dcj, uj = dc_ref[:, cs], u_ref[:, cs]
            dc1, dc2 = _shift_up(dcj, dhalo[:, cs], 1), _shift_up(dcj, dhalo[:, cs], 2)
            cacc_ref[0, :, cs] += _sum8(dcj)
            cacc_ref[1, :, cs] += _sum8(dc2 * uj)
            cacc_ref[2, :, cs] += _sum8(dc1 * uj)
            cacc_ref[3, :, cs] += _sum8(dcj * uj)
            du = (cw_ref[j, 2:3, :] * dcj + cw_ref[j, 1:2, :] * dc1 + cw_ref[j, 0:1, :] * dc2).astype(_MM)
            du_ref[:, cs] = du
            dh2 = dh2 + _dot_nt(du, wu_ref[j])
        xh, r = _rms(x1_ref[...])
        gacc_ref[...] += _sum8(dh2 * xh)
        dx1_ref[...] = dy_ref[...] + _rms_bwd(dh2, xh, r, g_ref[...])

    return _pc(
        body, name="conv_up_bwd", grid=(s // ts,),
        in_specs=[_row(ts, wide),
                  pl.BlockSpec((8, wide), lambda i: (jnp.minimum((i + 1) * (ts // 8), s // 8 - 1), 0)),
                  _row(ts, wide), _res(conv_w.shape), _res(w_up.shape), _row(ts, D_MODEL), _res((1, D_MODEL)),
                  _row(ts, D_MODEL)],
        out_specs=[_row(ts, D_MODEL), _row(ts, wide), _acc((4, 8, wide)), _acc((8, D_MODEL))],
        out_shape=[_sds((s, D_MODEL)), _sds((s, wide), _MM), _sds((4, 8, wide)), _sds((8, D_MODEL))])(
            dc, dc, u, conv_w, w_up, x1, g2, dy)


def _k_merge_bwd(dx1, og, lg, o_a, o_b, o_m, gates, w_oa, w_ob, w_om, w_out, dep):
    s = dx1.shape[0]
    ts = min(256, s)
    nc = w_oa.shape[2]

    def body(dx_ref, o0, o1, o2, l0, l1, l2, oa_ref, ob_ref, om_ref, gt_ref, woa, wob, wom, wout, dep_ref,
             dgp_ref, dpa_ref, dpb_ref, dpm_ref, dog0, dog1, dog2, dl0, dl1, dl2, dob_ref, dom_ref, bacc_ref):
        i = pl.program_id(0)

        @pl.when(i == 0)
        def _():
            bacc_ref[...] = jnp.zeros_like(bacc_ref)

        dmer = _dot_nt(dx_ref[...], wout[...])
        oa, ob, om = oa_ref[...], ob_ref[...], om_ref[...]
        doa = jnp.zeros((ts, A_W), F32)
        dob = jnp.zeros((ts, B_QH * HEAD), F32)
        dom = jnp.zeros((ts, M_W), F32)
        for j in range(CHIPS):
            prods = _branch_products(oa, ob, om, woa, wob, wom, j)
            dmj = dmer[:, j * nc:(j + 1) * nc]
            dps = []
            for br, (p, dref) in enumerate(zip(prods, (dpa_ref, dpb_ref, dpm_ref))):
                cs = slice(br * D_MODEL + j * nc, br * D_MODEL + (j + 1) * nc)
                gt = gt_ref[:, cs]
                dgp = dmj * p * gt * (1.0 - gt)
                dgp_ref[:, cs] = dgp.astype(_MM)
                bacc_ref[:, cs] += _sum8(dgp)
                dp = (dmj * gt).astype(_MM)
                dref[:, j * nc:(j + 1) * nc] = dp
                dps.append(dp)
            doa = doa + _dot_nt(dps[0], woa[j])
            dob = dob + _dot_nt(dps[1], wob[j])
            dom = dom + _dot_nt(dps[2], wom[j])
        dob_ref[...] = dob
        dom_ref[...] = dom
        ws = _group_weights(l0[...], l1[...], l2[...])
        dsum = _seg_mean(doa * oa, HEAD) * float(HEAD)
        for w, dref, lref in zip(ws, (dog0, dog1, dog2), (dl0, dl1, dl2)):
            dref[...] = w * doa
            lref[...] = w * dsum

    return _pc(
        body, name="merge_out_bwd", grid=(s // ts,),
        in_specs=[_row(ts, D_MODEL)] + [_row(ts, A_W)] * 7 + [_row(ts, B_QH * HEAD), _row(ts, M_W), _row(ts, 3 * D_MODEL),
                                                              _res(w_oa.shape), _res(w_ob.shape), _res(w_om.shape),
                                                              _res(w_out.shape), _res((8, 128))],
        out_specs=[_row(ts, 3 * D_MODEL)] + [_row(ts, D_MODEL)] * 3 + [_row(ts, A_W)] * 6
        + [_row(ts, B_QH * HEAD), _row(ts, M_W), _acc((8, 3 * D_MODEL))],
        out_shape=[_sds((s, 3 * D_MODEL), _MM)] + [_sds((s, D_MODEL), _MM)] * 3 + [_sds((s, A_W))] * 6
        + [_sds((s, B_QH * HEAD)), _sds((s, M_W)), _sds((8, 3 * D_MODEL))])(
            dx1, *og, *lg, o_a, o_b, o_m, gates, w_oa, w_ob, w_om, w_out, dep)


def _k_mem_bwd(m_q, gq, mk, mv, o_m, do_m):
    s = m_q.shape[0]
    n = mk.shape[0]
    ts = min(512, s)
    scale = M_HD ** -0.5

    def body(q_ref, g_ref, mk_ref, mv_ref, o_ref, do_ref, dq_ref, dmk_ref, dmv_ref, gacc_ref):
        i = pl.program_id(0)

        @pl.when(i == 0)
        def _():
            dmk_ref[...] = jnp.zeros_like(dmk_ref)
            dmv_ref[...] = jnp.zeros_like(dmv_ref)
            gacc_ref[...] = jnp.zeros_like(gacc_ref)

        gain = g_ref[...]
        qh, r = _seg_norm(q_ref[...], M_HD)
        qn = (qh * gain).astype(_MM)
        do = do_ref[...]
        delta = _seg_mean(do * o_ref[...], M_HD) * float(M_HD)
        dqn = []
        for h in range(M_HEADS):
            hs = slice(h * M_HD, (h + 1) * M_HD)
            p = _mem_probs(qn[:, hs], mk_ref[:, hs])
            dp = _dot_nt(do[:, hs], mv_ref[:, hs])
            ds = (p * (dp - delta[:, hs][:, 0:1]) * scale).astype(_MM)
            dqn.append(_dot(ds, mk_ref[:, hs]))
            dmk_ref[:, hs] += _dot_tn(ds, qn[:, hs])
            dmv_ref[:, hs] += _dot_tn(p, do[:, hs])
        dqn = jnp.concatenate(dqn, axis=1)
        gacc_ref[...] += _sum8(dqn * qh)
        z = dqn * gain
        dq_ref[...] = (r * (z - qh * _seg_mean(z * qh, M_HD))).astype(_MM)

    return _pc(
        body, name="mem_attn_bwd", grid=(s // ts,),
        in_specs=[_row(ts, M_W), _res((1, M_W)), _res((n, M_W)), _res((n, M_W)), _row(ts, M_W), _row(ts, M_W)],
        out_specs=[_row(ts, M_W), _acc((n, M_W)), _acc((n, M_W)), _acc((8, M_W))],
        out_shape=[_sds((s, M_W), _MM), _sds((n, M_W)), _sds((n, M_W)), _sds((8, M_W))])(m_q, gq, mk, mv, o_m, do_m)


def _k_memkv_bwd(mem, mem_norm, w_kv, m_k_norm, mem_n, kv, dmk, dmv):
    n = mem.shape[0]

    def body(m_ref, g_ref, w_ref, gk_ref, mn_ref, kv_ref, dmk_ref, dmv_ref, dw_ref, dg_ref, dgk_ref):
        gk = gk_ref[...]
        kh, r = _seg_norm(kv_ref[:, :M_W], M_HD)
        dmk = dmk_ref[...]
        dgk_ref[...] = _sum8(dmk * kh)
        z = dmk * gk
        dk = r * (z - kh * _seg_mean(z * kh, M_HD))
        dkv = jnp.concatenate([dk, dmv_ref[...]], axis=1).astype(_MM)
        dw_ref[...] = _dot_tn(mn_ref[...], dkv)
        dmn = _dot_nt(dkv, w_ref[...])
        mh, _ = _rms(m_ref[...])
        dg_ref[...] = _sum8(dmn * mh)

    return _pc(body, name="mem_kv_bwd", grid=(1,),
               in_specs=[_acc((n, D_MODEL)), _acc((1, D_MODEL)), _acc(w_kv.shape), _acc((1, M_W)), _acc((n, D_MODEL)),
                         _acc((n, 2 * M_W)), _acc((n, M_W)), _acc((n, M_W))],
               out_specs=[_acc(w_kv.shape), _acc((8, D_MODEL)), _acc((8, M_W))],
               out_shape=[_sds(w_kv.shape), _sds((8, D_MODEL)), _sds((8, M_W))])(
                   mem, mem_norm, w_kv, m_k_norm, mem_n, kv, dmk, dmv)


def _k_band_bwd(qn, kn, vn, do, lse, dl_or_o, *, hq, hk, max_dist, segs, sink, name):
    rows = qn.shape[0]
    nb = rows // BLK
    units = hq // A_HEADS
    shared = hk != hq
    wq, wk = hq * HEAD, hk * HEAD
    scale = HEAD ** -0.5

    def body(*refs):
        (q2_ref, qx_ref, kc_ref, kp_ref, vc_ref, vp_ref, do2_ref, dox_ref, l2_ref, lx_ref, e2_ref, ex_ref) = refs[:12]
        if sink is None:
            dq_ref, dk_ref, dv_ref = refs[12:]
        else:
            sk_ref, dq_ref, dk_ref, dv_ref, sacc_ref = refs[12:]
        i = pl.program_id(0)
        thr = lambda b: jnp.where(_first_flag(b, segs, nb), 1 << 20, BLK - max_dist)
        bias_a, bias_b = _band_bias(thr(2 * i), True), _band_bias(thr(2 * i + 1), True)
        bias_c = _band_bias(thr(2 * i + 2), False)
        if sink is not None:
            @pl.when(i == 0)
            def _():
                sacc_ref[...] = jnp.zeros_like(sacc_ref)

        def tile(q4, do4, l_cols, dlt, kd, vd, bias, width):
            s, dp = _dot_nt(q4, kd) * scale, _dot_nt(do4, vd)
            ps, dss = [], []
            for h in range(A_HEADS):
                seg = slice(h * width, (h + 1) * width)
                p = jnp.exp(s[:, seg] + bias - l_cols[h])
                ps.append(p)
                dss.append(p * (dp[:, seg] - dlt[:, h * HEAD:h * HEAD + 1]) * scale)
            return ps, dss

        cat = lambda parts: jnp.concatenate([t.astype(_MM) for t in parts], axis=1)
        for u in range(units):
            us = slice(u * A_W, (u + 1) * A_W)
            k_a = _unit_kv(((kp_ref, _LO), (kc_ref, _LO)), u, shared)
            v_a = _unit_kv(((vp_ref, _LO), (vc_ref, _LO)), u, shared)
            k_b, v_b = _unit_kv(((kc_ref, _BOTH),), u, shared), _unit_kv(((vc_ref, _BOTH),), u, shared)
            kd_a, vd_a, kd_b, vd_b = _blockdiag(k_a), _blockdiag(v_a), _blockdiag(k_b), _blockdiag(v_b)
            kd_c, vd_c = _blockdiag(k_b[BLK:]), _blockdiag(v_b[BLK:])
            qs = (q2_ref[_LO, us], q2_ref[_HI, us], qx_ref[:, us])
            dos = (do2_ref[_LO, us], do2_ref[_HI, us], dox_ref[:, us])
            lcols = [[ref[rs, u * A_W + h * HEAD:u * A_W + h * HEAD + 1] for h in range(A_HEADS)]
                     for ref, rs in ((l2_ref, _LO), (l2_ref, _HI), (lx_ref, _LO))]
            if sink is None:
                dlts = (e2_ref[_LO, us], e2_ref[_HI, us], ex_ref[:, us])
            else:
                dlts = tuple(_seg_sum64(d.astype(F32) * ref[rs, us])
                             for d, (ref, rs) in zip(dos, ((e2_ref, _LO), (e2_ref, _HI), (ex_ref, _LO))))
                for t in range(2):
                    for h in range(A_HEADS):
                        j = u * A_HEADS + h
                        sacc_ref[:, j:j + 1] += -jnp.exp(sk_ref[j] - lcols[t][h]) * dlts[t][:, h * HEAD:h * HEAD + 1]
            p_a, ds_a = tile(qs[0], dos[0], lcols[0], dlts[0], kd_a, vd_a, bias_a, 2 * BLK)
            p_b, ds_b = tile(qs[1], dos[1], lcols[1], dlts[1], kd_b, vd_b, bias_b, 2 * BLK)
            p_c, ds_c = tile(qs[2], dos[2], lcols[2], dlts[2], kd_c, vd_c, bias_c, BLK)
            dq_ref[_LO, us] = _dot(cat(ds_a), kd_a)
            dq_ref[_HI, us] = _dot(cat(ds_b), kd_b)
            outs = []
            for pa, pb, pc, lhs in ((ds_a, ds_b, ds_c, qs), (p_a, p_b, p_c, dos)):
                from_a = _fold_diag(_dot_tn(cat([t[:, BLK:] for t in pa]), lhs[0]), BLK)
                from_b = _fold_diag(_dot_tn(cat(pb), lhs[1]), 2 * BLK)
                from_c = _fold_diag(_dot_tn(cat(pc), lhs[2]), BLK)
                outs.append(jnp.concatenate([from_a + from_b[:BLK], from_b[BLK:] + from_c], axis=0))
            dk4, dv4 = outs
            if shared:
                fold = lambda t: (t[:, 0:HEAD] + t[:, HEAD:2 * HEAD]) + (t[:, 2 * HEAD:3 * HEAD] + t[:, 3 * HEAD:])
                dk_ref[:, u * HEAD:(u + 1) * HEAD] = fold(dk4)
                dv_ref[:, u * HEAD:(u + 1) * HEAD] = fold(dv4).astype(_MM)
            else:
                dk_ref[:, us] = dk4
                dv_ref[:, us] = dv4.astype(_MM)

    two = lambda w: pl.BlockSpec((2 * BLK, w), lambda i: (i, 0))
    prev = lambda w: pl.BlockSpec((BLK, w), lambda i: (jnp.maximum(2 * i - 1, 0), 0))
    nxt = lambda w: pl.BlockSpec((BLK, w), lambda i: (jnp.minimum(2 * i + 2, nb - 1), 0))
    in_specs = [two(wq), nxt(wq), two(wk), prev(wk), two(wk), prev(wk), two(wq), nxt(wq), two(wq), nxt(wq), two(wq), nxt(wq)]
    args = [qn, qn, kn, kn, vn, vn, do, do, lse, lse, dl_or_o, dl_or_o]
    out_specs = [two(wq), two(wk), two(wk)]
    out_shape = [_sds((rows, wq)), _sds((rows, wk)), _sds((rows, wk), _MM)]
    if sink is not None:
        in_specs.append(pl.BlockSpec(memory_space=pltpu.SMEM))
        args.append(sink)
        out_specs.append(_acc((BLK, 128)))
        out_shape.append(_sds((BLK, 128)))
    return _pc(body, name=name, grid=(nb // 2,), in_specs=in_specs, out_specs=out_specs, out_shape=out_shape)(*args)


def _k_prep_bwd(srcs, dqn, dkn, gq, gk, tabs, tab_row, *, wq, wk, rows_per_gain, name):
    rows = dqn.shape[0]
    ts = min(512, rows)
    ngain = gq.shape[0]

    def body(q_ref, k_ref, dq_ref, dk_ref, gq_ref, gk_ref, c_ref, sa_ref, sb_ref, oq_ref, ok_ref, aq_ref, ak_ref):
        i = pl.program_id(0)

        @pl.when(lax.rem(i * ts, rows_per_gain) == 0)
        def _():
            aq_ref[...] = jnp.zeros_like(aq_ref)
            ak_ref[...] = jnp.zeros_like(ak_ref)

        c, sa, sb = c_ref[...], sa_ref[...], sb_ref[...]
        for x_ref, d_ref, g_ref, o_ref, a_ref in ((q_ref, dq_ref, gq_ref, oq_ref, aq_ref),
                                                   (k_ref, dk_ref, gk_ref, ok_ref, ak_ref)):
            xh, r = _seg_norm(x_ref[...], HEAD)
            dt = _rope_bwd(d_ref[...], c, sa, sb)
            a_ref[...] += _sum8(dt * xh)
            z = dt * g_ref[...]
            o_ref[...] = (r * (z - xh * _seg_mean(z * xh, HEAD))).astype(_MM)

    gspec = lambda w: pl.BlockSpec((None, 1, w), lambda i: ((i * ts) // rows_per_gain, 0, 0))
    aspec = lambda w: pl.BlockSpec((None, 8, w), lambda i: ((i * ts) // rows_per_gain, 0, 0))
    return _pc(
        body, name=name, grid=(rows // ts,),
        in_specs=[_row(ts, wq, srcs[0][1]), _row(ts, wk, srcs[1][1]), _row(ts, wq), _row(ts, wk), gspec(wq), gspec(wk)]
        + [pl.BlockSpec((ts, 128), lambda i: (i + tab_row // ts, 0))] * 3,
        out_specs=[_row(ts, wq), _row(ts, wk), aspec(wq), aspec(wk)],
        out_shape=[_sds((rows, wq), _MM), _sds((rows, wk), _MM), _sds((ngain, 8, wq)), _sds((ngain, 8, wk))])(
            srcs[0][0], srcs[1][0], dqn, dkn, gq, gk, *tabs)


def _k_in_bwd(pieces, dgp, x, g1, dx1, w_in, w_gate):
    s = x.shape[0]
    ts = min(256, s)
    nin, ng = w_in.shape[2], w_gate.shape[2]
    widths = [p.shape[1] for p in pieces]
    ncol = sum(widths)

    def body(*refs):
        p_refs = refs[:len(pieces)]
        dgp_ref, x_ref, g_ref, dx1_ref, wi_ref, wg_ref, gx_ref, dpj_ref, gacc_ref = refs[len(pieces):]
        i = pl.program_id(0)

        @pl.when(i == 0)
        def _():
            gacc_ref[...] = jnp.zeros_like(gacc_ref)

        off = 0
        for p_ref, w in zip(p_refs, widths):
            dpj_ref[:, off:off + w] = p_ref[...]
            off += w
        dh = jnp.zeros((ts, D_MODEL), F32)
        for j in range(CHIPS):
            dh = dh + _dot_nt(dpj_ref[:, j * nin:(j + 1) * nin], wi_ref[j])
            dh = dh + _dot_nt(dgp_ref[:, j * ng:(j + 1) * ng], wg_ref[j])
        xh, r = _rms(x_ref[...])
        gacc_ref[...] += _sum8(dh * xh)
        gx_ref[...] = dx1_ref[...] + _rms_bwd(dh, xh, r, g_ref[...])

    return _pc(
        body, name="in_proj_bwd", grid=(s // ts,),
        in_specs=[_row(ts, w) for w in widths] + [_row(ts, CHIPS * ng), _row(ts, D_MODEL), _res((1, D_MODEL)),
                                                  _row(ts, D_MODEL), _res(w_in.shape), _res(w_gate.shape)],
        out_specs=[_row(ts, D_MODEL), _row(ts, ncol), _acc((8, D_MODEL))],
        out_shape=[_sds((s, D_MODEL)), _sds((s, ncol), _MM), _sds((8, D_MODEL))])(*pieces, dgp, x, g1, dx1, w_in, w_gate)


def _k_wgrad(a, b, *, nblk, stacked, name):
    s, k = a.shape
    n = b.shape[1]
    nb = n // nblk
    ts = min(2048 if k <= 1024 else 1024, s)

    def body(a_ref, b_ref, o_ref):
        @pl.when(pl.program_id(1) == 0)
        def _():
            o_ref[...] = jnp.zeros_like(o_ref)

        o_ref[...] += _dot_tn(a_ref[...], b_ref[...])

    if stacked:
        out_spec, out_shape = pl.BlockSpec((None, k, nb), lambda g, t: (g, 0, 0)), _sds((nblk, k, nb))
    else:
        out_spec, out_shape = pl.BlockSpec((k, nb), lambda g, t: (0, g)), _sds((k, n))
    return _pc(body, name=name, grid=(nblk, s // ts),
               in_specs=[pl.BlockSpec((ts, k), lambda g, t: (t, 0)), pl.BlockSpec((ts, nb), lambda g, t: (t, g))],
               out_specs=[out_spec], out_shape=[out_shape])(a, b)[0]


def _to_res(t, d):
    s, c = t.shape
    return t if d == 1 else t.reshape(s // d, d, c).transpose(1, 0, 2).reshape(s, c)


def _from_res(t, d):
    s, c = t.shape
    return t if d == 1 else t.reshape(d, s // d, c).transpose(1, 0, 2).reshape(s, c)


def _tile_gain(g, heads):
    return jnp.tile(g, (1,) * (g.ndim - 1) + (heads,))[..., None, :]


def _local_step(x, mem, pos, target, small, get_w_in, get_rest, on_grads):
    s = x.shape[0]
    nblk = s // BLK
    g1, g2 = small["attn_norm"], small["ffn_norm"]

    pos_rows = jnp.concatenate([_to_res(pos[:, None], d)[:, 0] for _, d in A_GROUPS] + [pos])
    tabs = _rope_tables(pos_rows)
    w_in = get_w_in(tabs[0])

    h, qa0, qa1, qa2, q_b, k_b, v_b, m_q = _k_in(x, g1, w_in)

    qkv_a = jnp.concatenate([_to_res(t, d) for t, (_, d) in zip((qa0, qa1, qa2), A_GROUPS)], axis=0)
    gq_a = _tile_gain(small["a_q_norm"], A_HEADS)
    gk_a = _tile_gain(small["a_k_norm"], A_HEADS)
    src_a = ((qkv_a, 0), (qkv_a, 1), (qkv_a, 2))
    qn_a, kn_a, vn_a = _k_prep(src_a, gq_a, gk_a, tabs, 0, wq=A_W, wk=A_W, rows_per_gain=s, name="prep_a")
    segs_a = tuple((gi * nblk, nblk // d) for gi, (_, d) in enumerate(A_GROUPS))
    o_res, l_res = _k_band_fwd(qn_a, kn_a, vn_a, hq=A_HEADS, hk=A_HEADS, max_dist=BLK, segs=segs_a, sink=None,
                               name="attn_a")
    og = [_from_res(o_res[gi * s:(gi + 1) * s], d) for gi, (_, d) in enumerate(A_GROUPS)]
    lg = [_from_res(l_res[gi * s:(gi + 1) * s], d) for gi, (_, d) in enumerate(A_GROUPS)]

    gq_b = _tile_gain(small["b_q_norm"], B_QH)
    gk_b = _tile_gain(small["b_k_norm"], B_KVH)
    src_b = ((q_b, 0), (k_b, 0), (v_b, 0))
    qn_b, kn_b, vn_b = _k_prep(src_b, gq_b, gk_b, tabs, 3 * s, wq=B_QH * HEAD, wk=B_KVH * HEAD, rows_per_gain=s,
                               name="prep_b")
    sink_x = small["b_sinks"][0]
    segs_b = ((0, nblk),)
    o_b, l_b = _k_band_fwd(qn_b, kn_b, vn_b, hq=B_QH, hk=B_KVH, max_dist=B_WINDOW - 1, segs=segs_b, sink=sink_x,
                           name="attn_b")

    wts = get_rest(0, o_b)
    gates = _k_gate(h, wts["w_gate"], small["b_gate"])

    gq_m = _tile_gain(small["m_q_norm"], M_HEADS)[0]
    gk_m = _tile_gain(small["m_k_norm"], M_HEADS)[0]
    mem_n, kv, mk, mv = _k_memkv(mem, small["mem_norm"], wts["w_mem_kv"], gk_m)
    o_m = _k_mem_fwd(m_q, gq_m, mk, mv)

    o_a, merged, x1, h2 = _k_merge(og, lg, o_b, o_m, gates, x, wts["w_o_a"], wts["w_o_b"], wts["w_o_m"],
                                   wts["w_out"], g2)
    wts.update(get_rest(1, x1))
    u = _k_up(h2, wts["w_up"])
    dy, f, dc, loss_acc = _k_ffn(u, wts["conv_w"], small["conv_b"], wts["w_down"], wts["w_down"].T, x1, target)
    loss = (0.5 / D_MODEL) * jnp.sum(loss_acc)

    dx1, du, cacc, g2acc = _k_conv_bwd(dc, u, wts["conv_w"], wts["w_up"], x1, g2, dy)
    tok = on_grads({"w_up": _k_wgrad(h2, du, nblk=CHIPS, stacked=True, name="dw_up"),
                    "w_down": _k_wgrad(f, dy, nblk=2, stacked=False, name="dw_down").reshape(CHIPS, -1, D_MODEL)}, dx1)
    (dgp, dp_a, dp_b, dp_m, dog0, dog1, dog2, dl0, dl1, dl2, do_b, do_m, bacc) = _k_merge_bwd(
        dx1, og, lg, o_a, o_b, o_m, gates, wts["w_o_a"], wts["w_o_b"], wts["w_o_m"], wts["w_out"], tok)
    tok = on_grads({"w_gate": _k_wgrad(h, dgp, nblk=CHIPS, stacked=True, name="dw_gate"),
                    "w_o_a": _k_wgrad(o_a, dp_a, nblk=CHIPS, stacked=True, name="dw_o_a"),
                    "w_o_b": _k_wgrad(o_b, dp_b, nblk=CHIPS, stacked=True, name="dw_o_b"),
                    "w_o_m": _k_wgrad(o_m, dp_m, nblk=CHIPS, stacked=True, name="dw_o_m"),
                    "w_out": _k_wgrad(merged, dx1, nblk=1, stacked=False, name="dw_out").reshape(CHIPS, -1, D_MODEL)},
                   do_m)

    dq_m, dmk, dmv, gqm_acc = _k_mem_bwd(m_q, gq_m + tok[0:1, 0:1], mk, mv, o_m, do_m)
    dw_kv, gmem_acc, gkm_acc = _k_memkv_bwd(mem, small["mem_norm"], wts["w_mem_kv"], gk_m, mem_n, kv, dmk, dmv)

    dq_bn, dk_bn, dv_b, sacc = _k_band_bwd(qn_b, kn_b, vn_b, do_b, l_b, o_b, hq=B_QH, hk=B_KVH,
                                           max_dist=B_WINDOW - 1, segs=segs_b, sink=sink_x, name="attn_b_bwd")
    tok = on_grads({}, dq_bn)
    dq_b, dk_b, gqb_acc, gkb_acc = _k_prep_bwd(src_b, dq_bn, dk_bn, gq_b + tok[0:1, 0:1], gk_b, tabs, 3 * s, wq=B_QH * HEAD,
                                               wk=B_KVH * HEAD, rows_per_gain=s, name="prep_b_bwd")

    do_res = jnp.concatenate([_to_res(t, d) for t, (_, d) in zip((dog0, dog1, dog2), A_GROUPS)], axis=0)
    dl_res = jnp.concatenate([_to_res(t, d) for t, (_, d) in zip((dl0, dl1, dl2), A_GROUPS)], axis=0)
    dq_an, dk_an, dv_a = _k_band_bwd(qn_a, kn_a, vn_a, do_res, l_res, dl_res, hq=A_HEADS, hk=A_HEADS, max_dist=BLK,
                                     segs=segs_a, sink=None, name="attn_a_bwd")
    dq_a, dk_a, gqa_acc, gka_acc = _k_prep_bwd(src_a, dq_an, dk_an, gq_a, gk_a, tabs, 0, wq=A_W, wk=A_W,
                                               rows_per_gain=s, name="prep_a_bwd")
    pieces = []
    for gi, (_, d) in enumerate(A_GROUPS):
        rs = slice(gi * s, (gi + 1) * s)
        pieces += [_from_res(t[rs], d) for t in (dq_a, dk_a, dv_a)]
    pieces += [dq_b, dk_b, dv_b, dq_m]
    grad_x, dproj, g1acc = _k_in_bwd(pieces, dgp, x, g1, dx1, w_in, wts["w_gate"])
    on_grads({"w_in": _k_wgrad(h, dproj, nblk=CHIPS, stacked=True, name="dw_in"),
              "w_mem_kv": dw_kv.reshape(CHIPS, -1, 2 * M_W)}, grad_x)

    def fold(acc, heads):
        v = jnp.sum(acc, axis=-2)
        return jnp.sum(v.reshape(v.shape[:-1] + (heads, -1)), axis=-2)

    csum = jnp.sum(cacc, axis=1)
    sml = {
        "attn_norm": jnp.sum(g1acc, axis=0), "a_q_norm": fold(gqa_acc, A_HEADS), "a_k_norm": fold(gka_acc, A_HEADS),
        "b_q_norm": fold(gqb_acc[0], B_QH), "b_k_norm": fold(gkb_acc[0], B_KVH),
        "b_sinks": jnp.sum(sacc, axis=0)[:B_QH], "mem_norm": jnp.sum(gmem_acc, axis=0),
        "m_q_norm": fold(gqm_acc, M_HEADS), "m_k_norm": fold(gkm_acc, M_HEADS),
        "b_gate": jnp.sum(bacc, axis=0), "ffn_norm": jnp.sum(g2acc, axis=0),
        "conv_w": csum[1:], "conv_b": csum[0],
    }
    return loss, grad_x, sml


def _mesh_pos():
    return lax.axis_index("x"), lax.axis_index("y"), lax.axis_index("c")


def _chip_peers(x, y):
    return [(1 - x, y), (x, 1 - y), (1 - x, 1 - y)]


_ANY = pl.BlockSpec(memory_space=pl.ANY)


def _comm_call(body, *, name, n_in, out_shape, scratch):
    return pl.pallas_call(body, name=name, in_specs=[_ANY] * n_in, out_specs=[_ANY] * len(out_shape),
                          out_shape=out_shape, scratch_shapes=scratch)


def _remote(src, dst, send_sem, recv_sem, dev):
    return pltpu.make_async_remote_copy(src_ref=src, dst_ref=dst, send_sem=send_sem, recv_sem=recv_sem,
                                        device_id=dev, device_id_type=MESH)


def _pair_join(halves, name):
    nt = len(halves)

    def body(*refs):
        ins, got = refs[:nt], refs[nt:2 * nt]
        send_sems, recv_sems = refs[2 * nt:]
        x, y, c = _mesh_pos()
        cps = []
        for t in range(nt):
            rc = _remote(ins[t], got[t], send_sems.at[t], recv_sems.at[t], (x, y, 1 - c))
            rc.start()
            cps.append(rc)
        for rc in cps:
            rc.wait()

    out_shape = [_sds(hf.shape, hf.dtype) for hf in halves]
    scratch = [pltpu.SemaphoreType.DMA((nt,)), pltpu.SemaphoreType.DMA((nt,))]
    return _comm_call(body, name=name, n_in=nt, out_shape=out_shape, scratch=scratch)(*halves)


_HBM = pl.BlockSpec(memory_space=pltpu.HBM)
_SEMS = pl.BlockSpec(memory_space=pltpu.SEMAPHORE)
_EFFECT = pltpu.SideEffectType.DATAFLOW_SIDE_EFFECTING


def _bcast_copies(ins, lands, send_sems, recv_sems):
    x, y, c = _mesh_pos()
    me = 2 * x + y
    targets = [((px, py, c), 2 * px + py) for px, py in _chip_peers(x, y)] + [((x, y, 1 - c), me)]
    out = []
    for t in range(len(ins)):
        for k, (dev, idx) in enumerate(targets):
            i = t * len(targets) + k
            arrival = lambda t=t, i=i, idx=idx, dev=dev: _remote(ins[t], lands[t].at[idx], send_sems.at[i],
                                                                 recv_sems.at[i], dev)
            out.append((_remote(ins[t], lands[t].at[me], send_sems.at[i], recv_sems.at[i], dev), arrival))
    return out


def _scatter_copies(ins, lands, send_sems, recv_sems):
    x, y, c = _mesh_pos()
    out = []
    for t in range(len(ins)):
        for k, (px, py) in enumerate(_chip_peers(x, y)):
            i = t * 3 + k
            cp = _remote(ins[t].at[2 * px + py], lands[t].at[k], send_sems.at[i], recv_sems.at[i], (px, py, c))
            out.append((cp, lambda cp=cp: cp))
    return out


def _pair_copies(ins, lands, send_sems, recv_sems):
    x, y, c = _mesh_pos()
    out = []
    for t in range(len(ins)):
        hr = ins[t].shape[1] // 2
        give = ins[t].at[:, pl.ds(pl.multiple_of((1 - c) * hr, 8), hr), :]
        cp = _remote(give, lands[t], send_sems.at[t], recv_sems.at[t], (x, y, 1 - c))
        out.append((cp, lambda cp=cp: cp))
    return out


def _join_copies(ins, lands, send_sems, recv_sems):
    x, y, c = _mesh_pos()
    out = []
    for t in range(len(ins)):
        cp = _remote(ins[t], lands[t], send_sems.at[t], recv_sems.at[t], (x, y, 1 - c))
        out.append((cp, lambda cp=cp: cp))
    return out


def _half_copies(ins, lands, send_sems, recv_sems):
    x, y, c = _mesh_pos()
    me = 2 * x + y
    out = []
    for t in range(len(ins)):
        hr = ins[t].shape[0] // 2
        rows = pl.ds(pl.multiple_of(c * hr, 8), hr)
        for k, (px, py) in enumerate(_chip_peers(x, y)):
            i = t * 3 + k
            arrival = lambda t=t, i=i, px=px, py=py, rows=rows: _remote(
                ins[t].at[rows, :], lands[t].at[2 * px + py].at[rows, :], send_sems.at[i], recv_sems.at[i], (px, py, c))
            out.append((_remote(ins[t].at[rows, :], lands[t].at[me].at[rows, :], send_sems.at[i], recv_sems.at[i],
                                (px, py, c)), arrival))
    return out


def _finish_halves(shards, stacks):
    nt = len(shards)

    def body(*refs):
        ins, held, outs = refs[:nt], refs[nt:2 * nt], refs[2 * nt:3 * nt]
        fwd_s, fwd_r, own_s, own_r = refs[3 * nt:]
        x, y, c = _mesh_pos()
        me = 2 * x + y
        sib = (x, y, 1 - c)
        pending = []
        for t in range(nt):
            hr = shards[t].shape[0] // 2
            half = lambda ref, who: ref.at[pl.ds(pl.multiple_of(who * hr, 8), hr), :]
            own = _remote(ins[t], outs[t].at[me], own_s.at[t], own_r.at[t], sib)
            own.start()
            pending.append(own.wait)
            for k, (px, py) in enumerate(_chip_peers(x, y)):
                pj = 2 * px + py
                fw = _remote(half(held[t].at[pj], c), half(outs[t].at[pj], c), fwd_s.at[t, k], fwd_r.at[t, k], sib)
                fw.start()
                pending.append(fw.wait_send)
                other = half(outs[t].at[pj], 1 - c)
                pending.append(_remote(other, other, fwd_s.at[t, k], fwd_r.at[t, k], sib).wait_recv)
        for wait in pending:
            wait()

    dma = pltpu.SemaphoreType.DMA
    return pl.pallas_call(
        body, name="gather_w_in_finish", in_specs=[_ANY] * (2 * nt), out_specs=[_ANY] * nt,
        out_shape=[_sds(a.shape, a.dtype) for a in stacks], input_output_aliases={nt + i: i for i in range(nt)},
        scratch_shapes=[dma((nt, 3)), dma((nt, 3)), dma((nt,)), dma((nt,))])(*shards, *stacks)


def _split_start(copies, srcs, land_shapes, ncopy, dep, name):
    nt = len(srcs)

    def body(*refs):
        ins, lands = refs[:nt], refs[nt:2 * nt]
        send_sems, recv_sems, token = refs[2 * nt + 1], refs[2 * nt + 2], refs[-1]
        for send, _ in copies(ins, lands, send_sems, recv_sems):
            send.start()
        token[...] = jnp.zeros_like(token)

    lands = [pltpu.with_memory_space_constraint(lax.empty(sh, a.dtype), pltpu.HBM) for sh, a in zip(land_shapes, srcs)]
    srcs = [pltpu.with_memory_space_constraint(a, pltpu.HBM) for a in srcs]
    dma = pltpu.SemaphoreType.DMA
    out_shape = ([dma((nt * ncopy,)), dma((nt * ncopy,))] + [pltpu.HBM(a.shape, a.dtype) for a in srcs + lands]
                 + [_sds((8, 128))])
    outs = pl.pallas_call(
        body, name=name, in_specs=[_HBM] * (2 * nt) + [_ANY],
        out_specs=[_SEMS, _SEMS] + [_HBM] * (2 * nt) + [pl.BlockSpec(memory_space=pltpu.VMEM)], out_shape=out_shape,
        input_output_aliases={i: 2 + i for i in range(2 * nt)},
        compiler_params=pltpu.CompilerParams(has_side_effects=_EFFECT))(*srcs, *lands, dep)
    return outs[0], outs[1], outs[2:2 + nt], outs[2 + nt:2 + 2 * nt], outs[-1]


def _split_wait(copies, send_sems, recv_sems, srcs, lands, after, name):
    nt = len(srcs)

    def body(*refs):
        ins, lnd = refs[:nt], refs[nt:2 * nt]
        for send, arrival in copies(ins, lnd, refs[2 * nt], refs[2 * nt + 1]):
            send.wait_send()
            arrival().wait_recv()

    outs = pl.pallas_call(
        body, name=name, in_specs=[_HBM] * (2 * nt) + [_SEMS, _SEMS, _ANY], out_specs=[_HBM] * (2 * nt),
        out_shape=[pltpu.HBM(a.shape, a.dtype) for a in list(srcs) + list(lands)],
        input_output_aliases={i: i for i in range(2 * nt)},
        compiler_params=pltpu.CompilerParams(has_side_effects=_EFFECT))(*srcs, *lands, send_sems, recv_sems, after)
    return outs[:nt], outs[nt:]


def _gather_small(packed):
    n = packed.shape[0]

    def body(in_ref, out_ref, send_sems, recv_sems, loc_sem):
        x, y, c = _mesh_pos()
        me = 4 * x + 2 * y + c
        lc = pltpu.make_async_copy(in_ref, out_ref.at[me], loc_sem)
        lc.start()
        peers = []
        for k in range(1, NDEV):
            px, py, pc = x ^ (k >> 2), y ^ ((k >> 1) & 1), c ^ (k & 1)
            rc = pltpu.make_async_remote_copy(src_ref=in_ref, dst_ref=out_ref.at[me], send_sem=send_sems.at[k - 1],
                                              recv_sem=recv_sems.at[k - 1], device_id=(px, py, pc), device_id_type=MESH)
            rc.start()
            peers.append((k, px, py, pc))
        lc.wait()
        for k, px, py, pc in peers:
            pltpu.make_async_remote_copy(src_ref=in_ref, dst_ref=out_ref.at[4 * px + 2 * py + pc],
                                         send_sem=send_sems.at[k - 1], recv_sem=recv_sems.at[k - 1],
                                         device_id=(px, py, pc), device_id_type=MESH).wait()

    scratch = [pltpu.SemaphoreType.DMA((NDEV - 1,)), pltpu.SemaphoreType.DMA((NDEV - 1,)), pltpu.SemaphoreType.DMA]
    return _comm_call(body, name="gather_small_grads", n_in=1, out_shape=[_sds((NDEV, n, 128))],
                      scratch=scratch)(packed)[0]


def _row_tile(r, c, mib=1):
    t = r
    while t * c * 4 > (mib << 20) and t % 16 == 0:
        t //= 2
    return t


def _k_pair_add(full, got, name):
    g, r, c = full.shape
    hr = r // 2
    tr = _row_tile(hr, c, 4)
    nh = hr // tr

    def body(a_ref, b_ref, o_ref):
        o_ref[...] = (a_ref[...] + b_ref[...]).astype(_WIRE)

    mine = pl.BlockSpec((None, tr, c), lambda i, j: (i, lax.axis_index("c") * nh + j, 0))
    spec = pl.BlockSpec((None, tr, c), lambda i, j: (i, j, 0))
    return _pc(body, name=name, grid=(g, nh), in_specs=[mine, spec], out_specs=[spec],
               out_shape=[_sds((g, hr, c), _WIRE)])(full, got)[0]


def _k_chip_sum(parts, slots, name):
    _, r, c = parts.shape
    tr = _row_tile(r, c, 4)

    def body(a_ref, s_ref, o_ref):
        acc = a_ref[...].astype(F32)
        for k in range(3):
            acc = acc + s_ref[k].astype(F32)
        o_ref[...] = acc

    own = pl.BlockSpec((None, tr, c), lambda i: (2 * lax.axis_index("x") + lax.axis_index("y"), i, 0))
    return _pc(body, name=name, grid=(r // tr,), in_specs=[own, pl.BlockSpec((3, tr, c), lambda i: (0, i, 0))],
               out_specs=[_row(tr, c)], out_shape=[_sds((r, c))])(parts, slots)[0]


def _adam(w, g, m, v):
    m = ADAM_B1 * m + (1.0 - ADAM_B1) * g
    v = ADAM_B2 * v + (1.0 - ADAM_B2) * (g * g)
    m_hat = m / (1.0 - ADAM_B1 ** ADAM_STEP)
    v_hat = v / (1.0 - ADAM_B2 ** ADAM_STEP)
    return -ADAM_LR * (m_hat / (jnp.sqrt(v_hat) + ADAM_EPS) + ADAM_WD * w), m, v


def _k_adam(w, mine, theirs, m, v, dep, name):
    r, c = w.shape
    hr = r // 2
    tr = _row_tile(hr, c, 2)
    nh = hr // tr

    def body(w_ref, a_ref, b_ref, m_ref, v_ref, dep_ref, g_ref, d_ref, mo_ref, vo_ref):
        upper = (pl.program_id(0) >= nh).astype(jnp.int32)
        g = jnp.where(upper == lax.axis_index("c"), a_ref[...], b_ref[...])
        g_ref[...] = g
        d_ref[...], mo_ref[...], vo_ref[...] = _adam(w_ref[...], g, m_ref[...], v_ref[...])

    hspec = pl.BlockSpec((tr, c), lambda i: (jnp.where(i >= nh, i - nh, i), 0))
    return _pc(body, name=name, grid=(r // tr,),
               in_specs=[_row(tr, c), hspec, hspec, _row(tr, c), _row(tr, c), _res((8, 128))],
               out_specs=[_row(tr, c)] * 4, out_shape=[_sds((r, c))] * 4)(w, mine, theirs, m, v, dep)


def _k_sum8(a):
    _, n, _ = a.shape

    def body(a_ref, o_ref):
        acc = a_ref[0]
        for k in range(1, NDEV):
            acc = acc + a_ref[k]
        o_ref[...] = acc

    return _pc(body, name="sum_small_grads", grid=(1,), in_specs=[_acc(a.shape)], out_specs=[_acc((n, 128))],
               out_shape=[_sds((n, 128))])(a)[0]


def _k_adam_small(ws, gs, ms, vs):
    n = len(ws)

    def body(*refs):
        for k in range(n):
            w_ref, g_ref, m_ref, v_ref, d_ref, mo_ref, vo_ref = refs[k::n]
            d_ref[...], mo_ref[...], vo_ref[...] = _adam(w_ref[...], g_ref[...], m_ref[...], v_ref[...])

    specs = [_acc(a.shape) for a in ws]
    outs = _pc(body, name="adam_small", grid=(1,), in_specs=specs * 4, out_specs=specs * 3,
               out_shape=[_sds(a.shape) for a in ws] * 3)(*ws, *gs, *ms, *vs)
    return outs[:n], outs[n:2 * n], outs[2 * n:]


def _pack(vals):
    rows = []
    for a in vals:
        flat = a.reshape(-1)
        n = -(-flat.shape[0] // 1024) * 1024
        rows.append(jnp.pad(flat, (0, n - flat.shape[0])).reshape(n // 128, 128))
    return jnp.concatenate(rows, axis=0)


def _unpack(packed, shapes):
    out, off = [], 0
    for sh in shapes:
        size = int(np.prod(sh))
        n = -(-size // 1024) * 1024
        out.append(packed[off // 128:(off + n) // 128].reshape(-1)[:size].reshape(sh))
        off += n
    return out


_WEIGHTS = ["attn_norm", "w_in", "a_q_norm", "a_k_norm", "b_q_norm", "b_k_norm", "b_sinks", "mem_norm", "w_mem_kv",
            "m_q_norm", "m_k_norm", "w_o_a", "w_o_b", "w_o_m", "w_gate", "b_gate", "w_out", "ffn_norm", "w_up",
            "conv_w", "conv_b", "w_down"]
_BIG = ["w_in", "w_mem_kv", "w_o_a", "w_o_b", "w_o_m", "w_gate", "w_out", "w_up", "w_down"]
_SMALL = [n for n in _WEIGHTS if n not in _BIG]


def kernel(x, mem, positions, attn_norm, w_in, a_q_norm, a_k_norm, b_q_norm, b_k_norm, b_sinks, mem_norm, w_mem_kv, m_q_norm, m_k_norm, w_o_a, w_o_b, w_o_m, w_gate, b_gate, w_out, ffn_norm, w_up, conv_w, conv_b, w_down, loss_target, m_attn_norm, m_w_in, m_a_q_norm, m_a_k_norm, m_b_q_norm, m_b_k_norm, m_b_sinks, m_mem_norm, m_w_mem_kv, m_m_q_norm, m_m_k_norm, m_w_o_a, m_w_o_b, m_w_o_m, m_w_gate, m_b_gate, m_w_out, m_ffn_norm, m_w_up, m_conv_w, m_conv_b, m_w_down, v_attn_norm, v_w_in, v_a_q_norm, v_a_k_norm, v_b_q_norm, v_b_k_norm, v_b_sinks, v_mem_norm, v_w_mem_kv, v_m_q_norm, v_m_k_norm, v_w_o_a, v_w_o_b, v_w_o_m, v_w_gate, v_b_gate, v_w_out, v_ffn_norm, v_w_up, v_conv_w, v_conv_b, v_w_down):
    given = dict(locals())
    w = {n: given[n][0] for n in _WEIGHTS}
    m1 = {n: given["m_" + n][0] for n in _WEIGHTS}
    m2 = {n: given["v_" + n][0] for n in _WEIGHTS}

    zeros = jnp.zeros((8, 128), F32)
    w_in_shard = w["w_in"].astype(_MM)
    *w_in_handles, tok = _split_start(_half_copies, [w_in_shard], [(CHIPS,) + w_in_shard.shape], 3, zeros,
                                      "gather_w_in_start")

    def get_w_in(after):
        send, recv, srcs, lands = w_in_handles
        srcs, lands = _split_wait(_half_copies, send, recv, srcs, lands, after, "gather_w_in_wait")
        return _finish_halves(srcs, lands)[0]

    stages = (["w_gate", "w_mem_kv", "w_o_a", "w_o_b", "w_o_m", "w_out"], ["w_up", "w_down", "conv_w"])
    started = []
    for k, names in enumerate(stages):
        shards = [w[n] if n == "conv_w" else w[n].astype(_MM) for n in names]
        *handles, tok = _split_start(_bcast_copies, shards, [(CHIPS,) + a.shape for a in shards], 4, tok,
                                     "gather_start_%d" % k)
        started.append(handles)
    small = {n: (w[n][None, :] if w[n].ndim == 1 else w[n]) for n in _SMALL if n != "conv_w"}
    positions = positions + tok[0:1, 0:1].astype(positions.dtype)

    def get_rest(stage, after):
        send, recv, srcs, lands = started[stage]
        got = _split_wait(_bcast_copies, send, recv, srcs, lands, after, "gather_wait_%d" % stage)[1]
        wts = dict(zip(stages[stage], got))
        for n in ("w_mem_kv", "w_out", "w_down"):
            if n in wts:
                wts[n] = wts[n].reshape(-1, wts[n].shape[-1])
        return wts

    parts, slots, pair, scat, started_pair = {}, {}, [], [], [None]

    def finish_pair(after):
        names, tag, send, recv, srcs, lands = pair.pop()
        full, got = _split_wait(_pair_copies, send, recv, srcs, lands, after, "pair_wait_" + tag)
        mine = [_k_pair_add(f, b, "pair_add_" + n) for n, f, b in zip(names, full, got)]
        shapes = [(3,) + p.shape[1:] for p in mine]
        send, recv, srcs, lands, token = _split_start(_scatter_copies, mine, shapes, 3, zeros, "scatter_start_" + tag)
        scat.append((names, tag, send, recv, srcs, lands))
        return token

    def on_grads(group, after):
        names = list(group)
        tag = "_".join(names)
        token = finish_pair(after) if pair else zeros
        if not group:
            return token
        grads_g = [group[n] for n in names]
        shapes = [(CHIPS, g.shape[1] // 2, g.shape[2]) for g in grads_g]
        send, recv, srcs, lands, token = _split_start(_pair_copies, grads_g, shapes, 1, token, "pair_start_" + tag)
        pair.append((names, tag, send, recv, srcs, lands))
        started_pair[0] = token
        return token

    loss, grad_x, sml = _local_step(x[0], mem[0], positions[0], loss_target[0], small, get_w_in, get_rest, on_grads)
    loss = lax.psum(loss, ("x", "y", "c"))
    early = [n for names, *_ in scat for n in names]
    for names, tag, send, recv, srcs, lands in scat:
        mine, got = _split_wait(_scatter_copies, send, recv, srcs, lands, started_pair[0], "scatter_wait_" + tag)
        parts.update(zip(names, mine))
        slots.update(zip(names, got))
    scat.clear()
    reduced = {n: _k_chip_sum(parts[n], slots[n], "chip_add_" + n) for n in early}
    halves = [reduced[n] for n in early]
    *join, tok = _split_start(_join_copies, halves, [a.shape for a in halves], 1, zeros, "pair_join_start_early")
    grads = {}

    shapes = [sml[n].shape for n in _SMALL]
    gsm = dict(zip(_SMALL, _unpack(_k_sum8(_gather_small(_pack([sml[n] for n in _SMALL]))), shapes)))
    nu = w["conv_w"].shape[1]
    chip = 2 * lax.axis_index("x") + lax.axis_index("y")
    gsm["conv_w"] = lax.dynamic_slice_in_dim(gsm["conv_w"], chip * nu, nu, axis=1)
    for n in _SMALL:
        grads[n] = gsm[n].reshape(w[n].shape)

    delta, new_m, new_v = {}, {}, {}
    dep = finish_pair(tok)
    mine, got = _split_wait(_join_copies, *join, dep, "pair_join_wait_early")
    reduced.update(zip(early, mine))
    theirs = dict(zip(early, got))
    for n in early:
        grads[n], delta[n], new_m[n], new_v[n] = _k_adam(w[n], reduced[n], theirs[n], m1[n], m2[n], dep, "adam_" + n)
        dep = delta[n]
    as2d = lambda d: [d[n][None, :] if d[n].ndim == 1 else d[n] for n in _SMALL]
    for dst, outs in zip((delta, new_m, new_v), _k_adam_small(as2d(w), as2d(grads), as2d(m1), as2d(m2))):
        dst.update((n, a.reshape(w[n].shape)) for n, a in zip(_SMALL, outs))
    late, tag, send, recv, srcs, lands = scat.pop()
    mine, got = _split_wait(_scatter_copies, send, recv, srcs, lands, dep, "scatter_wait_" + tag)
    for n, a, b in zip(late, mine, got):
        reduced[n] = _k_chip_sum(a, b, "chip_add_" + n)
    theirs.update(zip(late, _pair_join([reduced[n] for n in late], "grad_pair_join_late")))
    for n in late:
        grads[n], delta[n], new_m[n], new_v[n] = _k_adam(w[n], reduced[n], theirs[n], m1[n], m2[n], zeros, "adam_" + n)

    lead = lambda d: [d[n][None] for n in _WEIGHTS]
    return (loss, grad_x[None], *lead(grads), *lead(delta), *lead(new_m), *lead(new_v))
```

```python
import math

import jax
import jax.numpy as jnp
import numpy as np
from jax import lax
from jax.experimental import pallas as pl
from jax.experimental.pallas import tpu as pltpu

F32 = jnp.float32
_MM = jnp.bfloat16
_WIRE = jnp.bfloat16

D_MODEL = 1024
HEAD = 64
BLK = 128
A_GROUPS = ((128, 1), (512, 4), (2048, 16))
A_HEADS = 4
A_W = A_HEADS * HEAD
B_QH = 8
B_KVH = 2
B_WINDOW = 128
M_HEADS = 4
M_HD = 128
M_W = M_HEADS * M_HD
D_FF = 2816
EPS = 1e-6
NEG = -1e30
ROPE_THETA = 500000.0
ROPE_ROT = 16
CHIPS = 4
NDEV = 8
ADAM_LR, ADAM_B1, ADAM_B2, ADAM_EPS, ADAM_WD, ADAM_STEP = 0.001, 0.9, 0.999, 1e-08, 0.01, 10
VMEM_LIMIT = 58 * 1024 * 1024
MESH = pl.DeviceIdType.MESH


def _pc(body, *, name, grid, in_specs, out_specs, out_shape, scratch=()):
    return pl.pallas_call(
        body, name=name, grid=grid, in_specs=in_specs, out_specs=out_specs, out_shape=out_shape,
        scratch_shapes=list(scratch),
        compiler_params=pltpu.CompilerParams(dimension_semantics=("arbitrary",) * len(grid),
                                             vmem_limit_bytes=VMEM_LIMIT))


def _row(ts, c, col=0):
    return pl.BlockSpec((ts, c), lambda i: (i, col))


def _res(shape):
    n = len(shape)
    return pl.BlockSpec(tuple(shape), lambda i: (0,) * n, pipeline_mode=pl.Buffered(1))


def _acc(shape):
    n = len(shape)
    return pl.BlockSpec(tuple(shape), lambda i: (0,) * n)


def _sds(shape, dtype=F32):
    return jax.ShapeDtypeStruct(tuple(shape), dtype)


def _dot(a, b):
    return jnp.dot(a.astype(_MM), b.astype(_MM), preferred_element_type=F32)


def _dot_nt(a, b):
    return lax.dot_general(a.astype(_MM), b.astype(_MM), (((1,), (1,)), ((), ())), preferred_element_type=F32)


def _dot_tn(a, b):
    return lax.dot_general(a.astype(_MM), b.astype(_MM), (((0,), (0,)), ((), ())), preferred_element_type=F32)


def _sum8(v):
    ts, c = v.shape
    return jnp.sum(v.reshape(ts // 8, 8, c), axis=0)


def _sigmoid(z):
    return 1.0 / (1.0 + jnp.exp(-z))


def _rms(x):
    r = lax.rsqrt(jnp.mean(x * x, axis=-1, keepdims=True) + EPS)
    return x * r, r


def _rms_bwd(dy, xh, r, gain):
    z = dy * gain
    return r * (z - xh * jnp.mean(z * xh, axis=-1, keepdims=True))


def _split_hi_lo(v):
    hi = v.astype(_MM)
    return hi, (v - hi.astype(F32)).astype(_MM)


def _lane_head(shape):
    return lax.shift_right_logical(lax.broadcasted_iota(jnp.int32, shape, len(shape) - 1), 6)


def _seg_sum64(v):
    w = v.shape[1]
    e = jnp.where(_lane_head((w, w)) == lax.shift_right_logical(lax.broadcasted_iota(jnp.int32, (w, w), 0), 6),
                  1.0, 0.0).astype(_MM)
    hi, lo = _split_hi_lo(v)
    return jnp.dot(hi, e, preferred_element_type=F32) + jnp.dot(lo, e, preferred_element_type=F32)


def _seg_norm(x, seg):
    if seg == HEAD:
        r = lax.rsqrt(_seg_sum64(x * x) * (1.0 / HEAD) + EPS)
        return x * r, r
    w = x.shape[1]
    xh, rr = [], []
    for s in range(w // seg):
        xs = x[:, s * seg:(s + 1) * seg]
        r = lax.rsqrt(jnp.mean(xs * xs, axis=-1, keepdims=True) + EPS)
        xh.append(xs * r)
        rr.append(jnp.broadcast_to(r, xs.shape))
    return jnp.concatenate(xh, axis=1), jnp.concatenate(rr, axis=1)


def _seg_mean(v, seg):
    if seg == HEAD:
        return _seg_sum64(v) * (1.0 / HEAD)
    w = v.shape[1]
    out = []
    for s in range(w // seg):
        vs = v[:, s * seg:(s + 1) * seg]
        out.append(jnp.broadcast_to(jnp.mean(vs, axis=-1, keepdims=True), vs.shape))
    return jnp.concatenate(out, axis=1)


def _rope(t, c, sa, sb):
    out = []
    for cb in range(t.shape[1] // 128):
        tc = t[:, cb * 128:(cb + 1) * 128]
        out.append(tc * c + pltpu.roll(tc, 120, 1) * sa + pltpu.roll(tc, 8, 1) * sb)
    return jnp.concatenate(out, axis=1) if len(out) > 1 else out[0]


def _rope_bwd(dy, c, sa, sb):
    out = []
    for cb in range(dy.shape[1] // 128):
        dc = dy[:, cb * 128:(cb + 1) * 128]
        out.append(dc * c + pltpu.roll(dc * sa, 8, 1) + pltpu.roll(dc * sb, 120, 1))
    return jnp.concatenate(out, axis=1) if len(out) > 1 else out[0]


def _rope_consts():
    half = ROPE_ROT // 2
    c = np.float32(-2.0 * math.log(ROPE_THETA) / ROPE_ROT)
    freqs = np.exp(np.arange(half, dtype=np.float32) * c).astype(np.float32)
    place = np.zeros((3, half, 128), np.float32)
    ones = np.zeros((1, 128), np.float32)
    for lane in range(128):
        d = lane % HEAD
        if d < half:
            place[0, d, lane], place[1, d, lane] = 1.0, -1.0
        elif d < ROPE_ROT:
            place[0, d - half, lane], place[2, d - half, lane] = 1.0, 1.0
        else:
            ones[0, lane] = 1.0
    return np.tile(freqs[:, None], (1, 128)), place, ones


def _rope_tables(pos_rows):
    r = pos_rows.shape[0]
    tr = min(1024, r)
    freqs, place, ones = _rope_consts()

    def split3(v):
        hi, mid = _split_hi_lo(v)
        lo = (v - hi.astype(F32) - mid.astype(F32)).astype(_MM)
        return hi, mid, lo

    def body(p_ref, f_ref, e_ref, one_ref, c_ref, sa_ref, sb_ref):
        ang = jnp.concatenate([p_ref[j:j + 1, :].astype(F32) * f_ref[...] for j in range(tr // 128)], axis=1)
        cos, sin = jnp.cos(ang), jnp.sin(ang)
        for ref, k, v in ((c_ref, 0, cos), (sa_ref, 1, sin), (sb_ref, 2, sin)):
            e = e_ref[k].astype(_MM)
            out = sum(_dot_tn(part, e) for part in split3(v))
            ref[...] = out + one_ref[...] if k == 0 else out

    return _pc(body, name="rope_tables", grid=(r // tr,),
               in_specs=[pl.BlockSpec((tr // 128, 128), lambda i: (i, 0)), _acc((ROPE_ROT // 2, 128)),
                         _acc((3, ROPE_ROT // 2, 128)), _acc((1, 128))],
               out_specs=[_row(tr, 128)] * 3, out_shape=[_sds((r, 128))] * 3)(
                   pos_rows.reshape(r // 128, 128), jnp.asarray(freqs), jnp.asarray(place), jnp.asarray(ones))


def _k_in(x, g1, w_in):
    s = x.shape[0]
    ts = min(512, s)
    nin = w_in.shape[2]
    ncol = CHIPS * nin
    a_cols = 3 * A_W
    offs = [0, a_cols, 2 * a_cols, 3 * a_cols, 3 * a_cols + B_QH * HEAD,
            3 * a_cols + (B_QH + B_KVH) * HEAD, 3 * a_cols + (B_QH + 2 * B_KVH) * HEAD, ncol]

    def body(x_ref, g_ref, wi_ref, h_ref, a0, a1, a2, qb, kb, vb, mq, p_scr):
        xh, _ = _rms(x_ref[...])
        h = (xh * g_ref[...]).astype(_MM)
        h_ref[...] = h
        for j in range(CHIPS):
            p_scr[:, j * nin:(j + 1) * nin] = jnp.dot(h, wi_ref[j], preferred_element_type=F32)
        for k, ref in enumerate((a0, a1, a2, qb, kb, vb, mq)):
            ref[...] = p_scr[:, offs[k]:offs[k + 1]]

    widths = [offs[k + 1] - offs[k] for k in range(7)]
    return _pc(
        body, name="in_proj", grid=(s // ts,),
        in_specs=[_row(ts, D_MODEL), _res((1, D_MODEL)), _res(w_in.shape)],
        out_specs=[_row(ts, D_MODEL)] + [_row(ts, w) for w in widths],
        out_shape=[_sds((s, D_MODEL), _MM)] + [_sds((s, w)) for w in widths],
        scratch=[pltpu.VMEM((ts, ncol), F32)])(x, g1, w_in)


def _k_gate(h, w_gate, b_gate):
    s = h.shape[0]
    ts = min(256, s)
    ng = w_gate.shape[2]

    def body(h_ref, wg_ref, bg_ref, gt_ref):
        h = h_ref[...]
        for j in range(CHIPS):
            z = jnp.dot(h, wg_ref[j], preferred_element_type=F32) + bg_ref[:, j * ng:(j + 1) * ng]
            gt_ref[:, j * ng:(j + 1) * ng] = _sigmoid(z)

    return _pc(body, name="gate_proj", grid=(s // ts,),
               in_specs=[_row(ts, D_MODEL), _res(w_gate.shape), _res(b_gate.shape)],
               out_specs=[_row(ts, CHIPS * ng)], out_shape=[_sds((s, CHIPS * ng))])(h, w_gate, b_gate)[0]


def _k_prep(srcs, gq, gk, tabs, tab_row, *, wq, wk, rows_per_gain, name):
    rows = srcs[0][0].shape[0]
    ts = min(512, rows)

    def body(q_ref, k_ref, v_ref, gq_ref, gk_ref, c_ref, sa_ref, sb_ref, qn_ref, kn_ref, vn_ref):
        c, sa, sb = c_ref[...], sa_ref[...], sb_ref[...]
        qh, _ = _seg_norm(q_ref[...], HEAD)
        qn_ref[...] = _rope(qh * gq_ref[...], c, sa, sb).astype(_MM)
        kh, _ = _seg_norm(k_ref[...], HEAD)
        kn_ref[...] = _rope(kh * gk_ref[...], c, sa, sb).astype(_MM)
        vn_ref[...] = v_ref[...].astype(_MM)

    gspec = lambda w: pl.BlockSpec((None, 1, w), lambda i: ((i * ts) // rows_per_gain, 0, 0))
    return _pc(
        body, name=name, grid=(rows // ts,),
        in_specs=[_row(ts, wq, srcs[0][1]), _row(ts, wk, srcs[1][1]), _row(ts, wk, srcs[2][1]),
                  gspec(wq), gspec(wk)] + [pl.BlockSpec((ts, 128), lambda i: (i + tab_row // ts, 0))] * 3,
        out_specs=[_row(ts, wq), _row(ts, wk), _row(ts, wk)],
        out_shape=[_sds((rows, wq), _MM), _sds((rows, wk), _MM), _sds((rows, wk), _MM)])(
            srcs[0][0], srcs[1][0], srcs[2][0], gq, gk, *tabs)


def _first_flag(b, segs, nb):
    first = b >= nb
    for k, (start, period) in enumerate(segs):
        end = segs[k + 1][0] if k + 1 < len(segs) else nb
        first = first | ((b >= start) & (b < end) & (lax.rem(b - start, jnp.int32(period)) == 0))
    return first


def _band_bias(thr, with_cur):
    qi = lax.broadcasted_iota(jnp.int32, (BLK, BLK), 0)
    kj = lax.broadcasted_iota(jnp.int32, (BLK, BLK), 1)
    prev = jnp.where(kj >= qi + thr, 0.0, NEG)
    return jnp.concatenate([prev, jnp.where(kj <= qi, 0.0, NEG)], axis=1) if with_cur else prev


def _blockdiag(t4):
    head = _lane_head((1, A_W))
    return jnp.concatenate([t4 * jnp.where(head == h, 1.0, 0.0).astype(t4.dtype) for h in range(A_HEADS)], axis=0)


def _fold_diag(t, n):
    head = _lane_head((n, A_W))
    out = t[3 * n:4 * n]
    for h in (2, 1, 0):
        out = jnp.where(head == h, t[h * n:(h + 1) * n], out)
    return out


def _expand_heads(cols):
    n = cols[0].shape[0]
    head = _lane_head((n, A_W))
    out = jnp.broadcast_to(cols[3], (n, A_W))
    for h in (2, 1, 0):
        out = jnp.where(head == h, cols[h], out)
    return out


def _unit_kv(pieces, u, shared):
    cols = slice(u * HEAD, (u + 1) * HEAD) if shared else slice(u * A_W, (u + 1) * A_W)
    rows = [ref[rs, cols] for ref, rs in pieces]
    k = rows[0] if len(rows) == 1 else jnp.concatenate(rows, axis=0)
    return jnp.concatenate([k] * A_HEADS, axis=1) if shared else k


_LO, _HI, _BOTH = slice(0, BLK), slice(BLK, 2 * BLK), slice(0, 2 * BLK)


def _k_band_fwd(qn, kn, vn, *, hq, hk, max_dist, segs, sink, name):
    rows = qn.shape[0]
    nb = rows // BLK
    units = hq // A_HEADS
    shared = hk != hq
    wq, wk = hq * HEAD, hk * HEAD
    scale = HEAD ** -0.5

    def body(*refs):
        if sink is None:
            q_ref, kc_ref, kp_ref, vc_ref, vp_ref, o_ref, l_ref = refs
        else:
            q_ref, kc_ref, kp_ref, vc_ref, vp_ref, sk_ref, o_ref, l_ref = refs
        i = pl.program_id(0)
        for half, rs in enumerate((_LO, _HI)):
            bias = _band_bias(jnp.where(_first_flag(2 * i + half, segs, nb), 1 << 20, BLK - max_dist), True)
            kpieces = ((kp_ref, _LO), (kc_ref, _LO)) if half == 0 else ((kc_ref, _BOTH),)
            vpieces = ((vp_ref, _LO), (vc_ref, _LO)) if half == 0 else ((vc_ref, _BOTH),)
            for u in range(units):
                us = slice(u * A_W, (u + 1) * A_W)
                kb = _blockdiag(_unit_kv(kpieces, u, shared))
                vb = _blockdiag(_unit_kv(vpieces, u, shared))
                s_all = _dot_nt(q_ref[rs, us], kb) * scale
                ps, ls = [], []
                for h in range(A_HEADS):
                    s = s_all[:, h * 2 * BLK:(h + 1) * 2 * BLK] + bias
                    m = jnp.max(s, axis=-1, keepdims=True)
                    e = jnp.exp(s - m)
                    lse = m + jnp.log(jnp.sum(e, axis=-1, keepdims=True))
                    if sink is not None:
                        sk = sk_ref[u * A_HEADS + h]
                        mx = jnp.maximum(lse, sk)
                        lse = mx + jnp.log(jnp.exp(lse - mx) + jnp.exp(sk - mx))
                    ps.append((e * jnp.exp(m - lse)).astype(_MM))
                    ls.append(lse)
                o_ref[rs, us] = _dot(jnp.concatenate(ps, axis=1), vb)
                l_ref[rs, us] = _expand_heads(ls)

    two = lambda w: pl.BlockSpec((2 * BLK, w), lambda i: (i, 0))
    prev = lambda w: pl.BlockSpec((BLK, w), lambda i: (jnp.maximum(2 * i - 1, 0), 0))
    in_specs = [two(wq), two(wk), prev(wk), two(wk), prev(wk)]
    args = [qn, kn, kn, vn, vn]
    if sink is not None:
        in_specs.append(pl.BlockSpec(memory_space=pltpu.SMEM))
        args.append(sink)
    return _pc(body, name=name, grid=(nb // 2,), in_specs=in_specs, out_specs=[two(wq), two(wq)],
               out_shape=[_sds((rows, wq)), _sds((rows, wq))])(*args)


def _k_memkv(mem, mem_norm, w_kv, m_k_norm):
    n = mem.shape[0]

    def body(m_ref, g_ref, w_ref, gk_ref, mn_ref, kv_ref, mk_ref, mv_ref):
        mh, _ = _rms(m_ref[...])
        mn = (mh * g_ref[...]).astype(_MM)
        mn_ref[...] = mn
        kv = jnp.dot(mn, w_ref[...], preferred_element_type=F32)
        kv_ref[...] = kv
        kh, _ = _seg_norm(kv[:, :M_W], M_HD)
        mk_ref[...] = (kh * gk_ref[...]).astype(_MM)
        mv_ref[...] = kv[:, M_W:].astype(_MM)

    return _pc(body, name="mem_kv", grid=(1,),
               in_specs=[_acc((n, D_MODEL)), _acc((1, D_MODEL)), _acc(w_kv.shape), _acc((1, M_W))],
               out_specs=[_acc((n, D_MODEL)), _acc((n, 2 * M_W)), _acc((n, M_W)), _acc((n, M_W))],
               out_shape=[_sds((n, D_MODEL), _MM), _sds((n, 2 * M_W)), _sds((n, M_W), _MM), _sds((n, M_W), _MM)])(
                   mem, mem_norm, w_kv, m_k_norm)


def _mem_probs(q, mk):
    sc = _dot_nt(q, mk) * (M_HD ** -0.5)
    e = jnp.exp(sc - jnp.max(sc, axis=-1, keepdims=True))
    return e / jnp.sum(e, axis=-1, keepdims=True)


def _k_mem_fwd(m_q, gq, mk, mv):
    s = m_q.shape[0]
    n = mk.shape[0]
    ts = min(512, s)

    def body(q_ref, g_ref, mk_ref, mv_ref, o_ref):
        qh, _ = _seg_norm(q_ref[...], M_HD)
        qn = (qh * g_ref[...]).astype(_MM)
        for h in range(M_HEADS):
            hs = slice(h * M_HD, (h + 1) * M_HD)
            o_ref[:, hs] = _dot(_mem_probs(qn[:, hs], mk_ref[:, hs]), mv_ref[:, hs])

    return _pc(body, name="mem_attn", grid=(s // ts,),
               in_specs=[_row(ts, M_W), _res((1, M_W)), _res((n, M_W)), _res((n, M_W))],
               out_specs=[_row(ts, M_W)], out_shape=[_sds((s, M_W))])(m_q, gq, mk, mv)[0]


def _group_weights(l0, l1, l2):
    m = jnp.maximum(jnp.maximum(l0, l1), l2)
    e0, e1, e2 = jnp.exp(l0 - m), jnp.exp(l1 - m), jnp.exp(l2 - m)
    inv = 1.0 / (e0 + e1 + e2)
    return e0 * inv, e1 * inv, e2 * inv


def _branch_products(oa, ob, om, woa_ref, wob_ref, wom_ref, j):
    return _dot(oa, woa_ref[j]), _dot(ob, wob_ref[j]), _dot(om, wom_ref[j])


def _k_merge(og, lg, o_b, o_m, gates, x, w_oa, w_ob, w_om, w_out, g2):
    s = x.shape[0]
    ts = min(256, s)
    nc = w_oa.shape[2]

    def body(o0, o1, o2, l0, l1, l2, ob_ref, om_ref, gt_ref, x_ref, woa, wob, wom, wout, g_ref,
             oa_ref, mer_ref, x1_ref, h2_ref, m_scr):
        w0, w1, w2 = _group_weights(l0[...], l1[...], l2[...])
        oa = w0 * o0[...] + w1 * o1[...] + w2 * o2[...]
        oa_ref[...] = oa
        ob, om = ob_ref[...], om_ref[...]
        for j in range(CHIPS):
            pa, pb, pm = _branch_products(oa, ob, om, woa, wob, wom, j)
            cs = lambda br: slice(br * D_MODEL + j * nc, br * D_MODEL + (j + 1) * nc)
            m_scr[:, j * nc:(j + 1) * nc] = gt_ref[:, cs(0)] * pa + gt_ref[:, cs(1)] * pb + gt_ref[:, cs(2)] * pm
        mer = m_scr[...].astype(_MM)
        mer_ref[...] = mer
        x1 = x_ref[...] + jnp.dot(mer, wout[...], preferred_element_type=F32)
        x1_ref[...] = x1
        xh, _ = _rms(x1)
        h2_ref[...] = (xh * g_ref[...]).astype(_MM)

    return _pc(
        body, name="merge_out", grid=(s // ts,),
        in_specs=[_row(ts, A_W)] * 6 + [_row(ts, B_QH * HEAD), _row(ts, M_W), _row(ts, 3 * D_MODEL), _row(ts, D_MODEL),
                                         _res(w_oa.shape), _res(w_ob.shape), _res(w_om.shape), _res(w_out.shape),
                                         _res((1, D_MODEL))],
        out_specs=[_row(ts, A_W), _row(ts, D_MODEL), _row(ts, D_MODEL), _row(ts, D_MODEL)],
        out_shape=[_sds((s, A_W)), _sds((s, D_MODEL), _MM), _sds((s, D_MODEL)), _sds((s, D_MODEL), _MM)],
        scratch=[pltpu.VMEM((ts, D_MODEL), F32)])(*og, *lg, o_b, o_m, gates, x, w_oa, w_ob, w_om, w_out, g2)


def _k_up(h2, w_up):
    s = h2.shape[0]
    ts = min(256, s)
    nu = w_up.shape[2]

    def body(h_ref, w_ref, u_ref):
        h = h_ref[...]
        for j in range(CHIPS):
            u_ref[:, j * nu:(j + 1) * nu] = jnp.dot(h, w_ref[j], preferred_element_type=F32)

    return _pc(body, name="up_proj", grid=(s // ts,), in_specs=[_row(ts, D_MODEL), _res(w_up.shape)],
               out_specs=[_row(ts, CHIPS * nu)], out_shape=[_sds((s, CHIPS * nu))])(h2, w_up)[0]


def _shift_down(v, halo, k):
    rolled = pltpu.roll(v, k, 0)
    row = lax.broadcasted_iota(jnp.int32, (8, v.shape[1]), 0)
    slab = rolled[0:8]
    for r in range(k):
        slab = jnp.where(row == r, halo[8 - k + r:8 - k + r + 1, :], slab)
    return jnp.concatenate([slab, rolled[8:]], axis=0)


def _shift_up(v, halo, k):
    ts = v.shape[0]
    rolled = pltpu.roll(v, ts - k, 0)
    row = lax.broadcasted_iota(jnp.int32, (8, v.shape[1]), 0)
    slab = rolled[ts - 8:]
    for r in range(k):
        slab = jnp.where(row == 8 - k + r, halo[r:r + 1, :], slab)
    return jnp.concatenate([rolled[:ts - 8], slab], axis=0)


def _k_ffn(u, conv_w, conv_b, w_down, w_down_t, x1, target):
    s = u.shape[0]
    ts = min(256, s)
    nu = conv_w.shape[2]
    half = CHIPS // 2

    def body(u_ref, uh_ref, cw_ref, cb_ref, wd_ref, wdt_ref, x1_ref, t_ref, dy_ref, f_ref, dc_ref, loss_ref, c_scr,
             f_scr, s_scr):
        i = pl.program_id(0)
        halo = jnp.where(i > 0, uh_ref[...], 0.0)
        for j in range(CHIPS):
            cs = slice(j * nu, (j + 1) * nu)
            uj = u_ref[:, cs]
            hj = halo[:, cs]
            c_scr[:, cs] = (cb_ref[:, cs] + cw_ref[j, 0:1, :] * _shift_down(uj, hj, 2)
                            + cw_ref[j, 1:2, :] * _shift_down(uj, hj, 1) + cw_ref[j, 2:3, :] * uj)
        for j in range(half):
            a = c_scr[:, j * nu:(j + 1) * nu]
            g = c_scr[:, (half + j) * nu:(half + j + 1) * nu]
            sa = _sigmoid(a)
            s_scr[:, j * nu:(j + 1) * nu] = sa
            f_scr[:, j * nu:(j + 1) * nu] = (a * sa * g).astype(_MM)
        f = f_scr[...]
        f_ref[...] = f
        y = x1_ref[...] + jnp.dot(f, wd_ref[...], preferred_element_type=F32)
        err = y - t_ref[...]
        dy = err * (1.0 / D_MODEL)
        dy_ref[...] = dy

        @pl.when(i == 0)
        def _():
            loss_ref[...] = jnp.zeros_like(loss_ref)

        loss_ref[...] += _sum8(err * err)
        df = _dot(dy, wdt_ref[...])
        for j in range(half):
            a = c_scr[:, j * nu:(j + 1) * nu]
            g = c_scr[:, (half + j) * nu:(half + j + 1) * nu]
            sa = s_scr[:, j * nu:(j + 1) * nu]
            dfj = df[:, j * nu:(j + 1) * nu]
            dc_ref[:, j * nu:(j + 1) * nu] = dfj * g * (sa * (1.0 + a * (1.0 - sa)))
            dc_ref[:, (half + j) * nu:(half + j + 1) * nu] = dfj * (a * sa)

    wide = CHIPS * nu
    return _pc(
        body, name="conv_ffn", grid=(s // ts,),
        in_specs=[_row(ts, wide), pl.BlockSpec((8, wide), lambda i: (jnp.maximum(i * (ts // 8) - 1, 0), 0)),
                  _res(conv_w.shape), _res((1, wide)), _res(w_down.shape), _res(w_down_t.shape), _row(ts, D_MODEL),
                  _row(ts, D_MODEL)],
        out_specs=[_row(ts, D_MODEL), _row(ts, D_FF), _row(ts, wide), _acc((8, D_MODEL))],
        out_shape=[_sds((s, D_MODEL)), _sds((s, D_FF), _MM), _sds((s, wide)), _sds((8, D_MODEL))],
        scratch=[pltpu.VMEM((ts, wide), F32), pltpu.VMEM((ts, D_FF), _MM), pltpu.VMEM((ts, D_FF), F32)])(
            u, u, conv_w, conv_b, w_down, w_down_t, x1, target)


def _k_conv_bwd(dc, u, conv_w, w_up, x1, g2, dy):
    s = u.shape[0]
    ts = min(256, s)
    nu = conv_w.shape[2]
    wide = CHIPS * nu
    last = s // ts - 1

    def body(dc_ref, dn_ref, u_ref, cw_ref, wu_ref, x1_ref, g_ref, dy_ref, dx1_ref, du_ref, cacc_ref, gacc_ref):
        i = pl.program_id(0)

        @pl.when(i == 0)
        def _():
            cacc_ref[...] = jnp.zeros_like(cacc_ref)
            gacc_ref[...] = jnp.zeros_like(gacc_ref)

        dhalo = jnp.where(i < last, dn_ref[...], 0.0)
        dh2 = jnp.zeros((ts, D_MODEL), F32)
        for j in range(CHIPS):
            cs = slice(j * nu, (j + 1) * nu)
            dcj, uj = dc_ref[:, cs], u_ref[:, cs]
            dc1, dc2 = _shift_up(dcj, dhalo[:, cs], 1), _shift_up(dcj, dhalo[:, cs], 2)
            cacc_ref[0, :, cs] += _sum8(dcj)
            cacc_ref[1, :, cs] += _sum8(dc2 * uj)
            cacc_ref[2, :, cs] += _sum8(dc1 * uj)
            cacc_ref[3, :, cs] += _sum8(dcj * uj)
            du = (cw_ref[j, 2:3, :] * dcj + cw_ref[j, 1:2, :] * dc1 + cw_ref[j, 0:1, :] * dc2).astype(_MM)
            du_ref[:, cs] = du
            dh2 = dh2 + _dot_nt(du, wu_ref[j])
        xh, r = _rms(x1_ref[...])
        gacc_ref[...] += _sum8(dh2 * xh)
        dx1_ref[...] = dy_ref[...] + _rms_bwd(dh2, xh, r, g_ref[...])

    return _pc(
        body, name="conv_up_bwd", grid=(s // ts,),
        in_specs=[_row(ts, wide),
                  pl.BlockSpec((8, wide), lambda i: (jnp.minimum((i + 1) * (ts // 8), s // 8 - 1), 0)),
                  _row(ts, wide), _res(conv_w.shape), _res(w_up.shape), _row(ts, D_MODEL), _res((1, D_MODEL)),
                  _row(ts, D_MODEL)],
        out_specs=[_row(ts, D_MODEL), _row(ts, wide), _acc((4, 8, wide)), _acc((8, D_MODEL))],
        out_shape=[_sds((s, D_MODEL)), _sds((s, wide), _MM), _sds((4, 8, wide)), _sds((8, D_MODEL))])(
            dc, dc, u, conv_w, w_up, x1, g2, dy)


def _k_merge_bwd(dx1, og, lg, o_a, o_b, o_m, gates, w_oa, w_ob, w_om, w_out, dep):
    s = dx1.shape[0]
    ts = min(256, s)
    nc = w_oa.shape[2]

    def body(dx_ref, o0, o1, o2, l0, l1, l2, oa_ref, ob_ref, om_ref, gt_ref, woa, wob, wom, wout, dep_ref,
             dgp_ref, dpa_ref, dpb_ref, dpm_ref, dog0, dog1, dog2, dl0, dl1, dl2, dob_ref, dom_ref, bacc_ref):
        i = pl.program_id(0)

        @pl.when(i == 0)
        def _():
            bacc_ref[...] = jnp.zeros_like(bacc_ref)

        dmer = _dot_nt(dx_ref[...], wout[...])
        oa, ob, om = oa_ref[...], ob_ref[...], om_ref[...]
        doa = jnp.zeros((ts, A_W), F32)
        dob = jnp.zeros((ts, B_QH * HEAD), F32)
        dom = jnp.zeros((ts, M_W), F32)
        for j in range(CHIPS):
            prods = _branch_products(oa, ob, om, woa, wob, wom, j)
            dmj = dmer[:, j * nc:(j + 1) * nc]
            dps = []
            for br, (p, dref) in enumerate(zip(prods, (dpa_ref, dpb_ref, dpm_ref))):
                cs = slice(br * D_MODEL + j * nc, br * D_MODEL + (j + 1) * nc)
                gt = gt_ref[:, cs]
                dgp = dmj * p * gt * (1.0 - gt)
                dgp_ref[:, cs] = dgp.astype(_MM)
                bacc_ref[:, cs] += _sum8(dgp)
                dp = (dmj * gt).astype(_MM)
                dref[:, j * nc:(j + 1) * nc] = dp
                dps.append(dp)
            doa = doa + _dot_nt(dps[0], woa[j])
            dob = dob + _dot_nt(dps[1], wob[j])
            dom = dom + _dot_nt(dps[2], wom[j])
        dob_ref[...] = dob
        dom_ref[...] = dom
        ws = _group_weights(l0[...], l1[...], l2[...])
        dsum = _seg_mean(doa * oa, HEAD) * float(HEAD)
        for w, dref, lref in zip(ws, (dog0, dog1, dog2), (dl0, dl1, dl2)):
            dref[...] = w * doa
            lref[...] = w * dsum

    return _pc(
        body, name="merge_out_bwd", grid=(s // ts,),
        in_specs=[_row(ts, D_MODEL)] + [_row(ts, A_W)] * 7 + [_row(ts, B_QH * HEAD), _row(ts, M_W), _row(ts, 3 * D_MODEL),
                                                              _res(w_oa.shape), _res(w_ob.shape), _res(w_om.shape),
                                                              _res(w_out.shape), _res((8, 128))],
        out_specs=[_row(ts, 3 * D_MODEL)] + [_row(ts, D_MODEL)] * 3 + [_row(ts, A_W)] * 6
        + [_row(ts, B_QH * HEAD), _row(ts, M_W), _acc((8, 3 * D_MODEL))],
        out_shape=[_sds((s, 3 * D_MODEL), _MM)] + [_sds((s, D_MODEL), _MM)] * 3 + [_sds((s, A_W))] * 6
        + [_sds((s, B_QH * HEAD)), _sds((s, M_W)), _sds((8, 3 * D_MODEL))])(
            dx1, *og, *lg, o_a, o_b, o_m, gates, w_oa, w_ob, w_om, w_out, dep)


def _k_mem_bwd(m_q, gq, mk, mv, o_m, do_m):
    s = m_q.shape[0]
    n = mk.shape[0]
    ts = min(512, s)
    scale = M_HD ** -0.5

    def body(q_ref, g_ref, mk_ref, mv_ref, o_ref, do_ref, dq_ref, dmk_ref, dmv_ref, gacc_ref):
        i = pl.program_id(0)

        @pl.when(i == 0)
        def _():
            dmk_ref[...] = jnp.zeros_like(dmk_ref)
            dmv_ref[...] = jnp.zeros_like(dmv_ref)
            gacc_ref[...] = jnp.zeros_like(gacc_ref)

        gain = g_ref[...]
        qh, r = _seg_norm(q_ref[...], M_HD)
        qn = (qh * gain).astype(_MM)
        do = do_ref[...]
        delta = _seg_mean(do * o_ref[...], M_HD) * float(M_HD)
        dqn = []
        for h in range(M_HEADS):
            hs = slice(h * M_HD, (h + 1) * M_HD)
            p = _mem_probs(qn[:, hs], mk_ref[:, hs])
            dp = _dot_nt(do[:, hs], mv_ref[:, hs])
            ds = (p * (dp - delta[:, hs][:, 0:1]) * scale).astype(_MM)
            dqn.append(_dot(ds, mk_ref[:, hs]))
            dmk_ref[:, hs] += _dot_tn(ds, qn[:, hs])
            dmv_ref[:, hs] += _dot_tn(p, do[:, hs])
        dqn = jnp.concatenate(dqn, axis=1)
        gacc_ref[...] += _sum8(dqn * qh)
        z = dqn * gain
        dq_ref[...] = (r * (z - qh * _seg_mean(z * qh, M_HD))).astype(_MM)

    return _pc(
        body, name="mem_attn_bwd", grid=(s // ts,),
        in_specs=[_row(ts, M_W), _res((1, M_W)), _res((n, M_W)), _res((n, M_W)), _row(ts, M_W), _row(ts, M_W)],
        out_specs=[_row(ts, M_W), _acc((n, M_W)), _acc((n, M_W)), _acc((8, M_W))],
        out_shape=[_sds((s, M_W), _MM), _sds((n, M_W)), _sds((n, M_W)), _sds((8, M_W))])(m_q, gq, mk, mv, o_m, do_m)


def _k_memkv_bwd(mem, mem_norm, w_kv, m_k_norm, mem_n, kv, dmk, dmv):
    n = mem.shape[0]

    def body(m_ref, g_ref, w_ref, gk_ref, mn_ref, kv_ref, dmk_ref, dmv_ref, dw_ref, dg_ref, dgk_ref):
        gk = gk_ref[...]
        kh, r = _seg_norm(kv_ref[:, :M_W], M_HD)
        dmk = dmk_ref[...]
        dgk_ref[...] = _sum8(dmk * kh)
        z = dmk * gk
        dk = r * (z - kh * _seg_mean(z * kh, M_HD))
        dkv = jnp.concatenate([dk, dmv_ref[...]], axis=1).astype(_MM)
        dw_ref[...] = _dot_tn(mn_ref[...], dkv)
        dmn = _dot_nt(dkv, w_ref[...])
        mh, _ = _rms(m_ref[...])
        dg_ref[...] = _sum8(dmn * mh)

    return _pc(body, name="mem_kv_bwd", grid=(1,),
               in_specs=[_acc((n, D_MODEL)), _acc((1, D_MODEL)), _acc(w_kv.shape), _acc((1, M_W)), _acc((n, D_MODEL)),
                         _acc((n, 2 * M_W)), _acc((n, M_W)), _acc((n, M_W))],
               out_specs=[_acc(w_kv.shape), _acc((8, D_MODEL)), _acc((8, M_W))],
               out_shape=[_sds(w_kv.shape), _sds((8, D_MODEL)), _sds((8, M_W))])(
                   mem, mem_norm, w_kv, m_k_norm, mem_n, kv, dmk, dmv)


def _k_band_bwd(qn, kn, vn, do, lse, dl_or_o, *, hq, hk, max_dist, segs, sink, name):
    rows = qn.shape[0]
    nb = rows // BLK
    units = hq // A_HEADS
    shared = hk != hq
    wq, wk = hq * HEAD, hk * HEAD
    scale = HEAD ** -0.5

    def body(*refs):
        (q2_ref, qx_ref, kc_ref, kp_ref, vc_ref, vp_ref, do2_ref, dox_ref, l2_ref, lx_ref, e2_ref, ex_ref) = refs[:12]
        if sink is None:
            dq_ref, dk_ref, dv_ref = refs[12:]
        else:
            sk_ref, dq_ref, dk_ref, dv_ref, sacc_ref = refs[12:]
        i = pl.program_id(0)
        thr = lambda b: jnp.where(_first_flag(b, segs, nb), 1 << 20, BLK - max_dist)
        bias_a, bias_b = _band_bias(thr(2 * i), True), _band_bias(thr(2 * i + 1), True)
        bias_c = _band_bias(thr(2 * i + 2), False)
        if sink is not None:
            @pl.when(i == 0)
            def _():
                sacc_ref[...] = jnp.zeros_like(sacc_ref)

        def tile(q4, do4, l_cols, dlt, kd, vd, bias, width):
            s, dp = _dot_nt(q4, kd) * scale, _dot_nt(do4, vd)
            ps, dss = [], []
            for h in range(A_HEADS):
                seg = slice(h * width, (h + 1) * width)
                p = jnp.exp(s[:, seg] + bias - l_cols[h])
                ps.append(p)
                dss.append(p * (dp[:, seg] - dlt[:, h * HEAD:h * HEAD + 1]) * scale)
            return ps, dss

        cat = lambda parts: jnp.concatenate([t.astype(_MM) for t in parts], axis=1)
        for u in range(units):
            us = slice(u * A_W, (u + 1) * A_W)
            k_a = _unit_kv(((kp_ref, _LO), (kc_ref, _LO)), u, shared)
            v_a = _unit_kv(((vp_ref, _LO), (vc_ref, _LO)), u, shared)
            k_b, v_b = _unit_kv(((kc_ref, _BOTH),), u, shared), _unit_kv(((vc_ref, _BOTH),), u, shared)
            kd_a, vd_a, kd_b, vd_b = _blockdiag(k_a), _blockdiag(v_a), _blockdiag(k_b), _blockdiag(v_b)
            kd_c, vd_c = _blockdiag(k_b[BLK:]), _blockdiag(v_b[BLK:])
            qs = (q2_ref[_LO, us], q2_ref[_HI, us], qx_ref[:, us])
            dos = (do2_ref[_LO, us], do2_ref[_HI, us], dox_ref[:, us])
            lcols = [[ref[rs, u * A_W + h * HEAD:u * A_W + h * HEAD + 1] for h in range(A_HEADS)]
                     for ref, rs in ((l2_ref, _LO), (l2_ref, _HI), (lx_ref, _LO))]
            if sink is None:
                dlts = (e2_ref[_LO, us], e2_ref[_HI, us], ex_ref[:, us])
            else:
                dlts = tuple(_seg_sum64(d.astype(F32) * ref[rs, us])
                             for d, (ref, rs) in zip(dos, ((e2_ref, _LO), (e2_ref, _HI), (ex_ref, _LO))))
                for t in range(2):
                    for h in range(A_HEADS):
                        j = u * A_HEADS + h
                        sacc_ref[:, j:j + 1] += -jnp.exp(sk_ref[j] - lcols[t][h]) * dlts[t][:, h * HEAD:h * HEAD + 1]
            p_a, ds_a = tile(qs[0], dos[0], lcols[0], dlts[0], kd_a, vd_a, bias_a, 2 * BLK)
            p_b, ds_b = tile(qs[1], dos[1], lcols[1], dlts[1], kd_b, vd_b, bias_b, 2 * BLK)
            p_c, ds_c = tile(qs[2], dos[2], lcols[2], dlts[2], kd_c, vd_c, bias_c, BLK)
            dq_ref[_LO, us] = _dot(cat(ds_a), kd_a)
            dq_ref[_HI, us] = _dot(cat(ds_b), kd_b)
            outs = []
            for pa, pb, pc, lhs in ((ds_a, ds_b, ds_c, qs), (p_a, p_b, p_c, dos)):
                from_a = _fold_diag(_dot_tn(cat([t[:, BLK:] for t in pa]), lhs[0]), BLK)
                from_b = _fold_diag(_dot_tn(cat(pb), lhs[1]), 2 * BLK)
                from_c = _fold_diag(_dot_tn(cat(pc), lhs[2]), BLK)
                outs.append(jnp.concatenate([from_a + from_b[:BLK], from_b[BLK:] + from_c], axis=0))
            dk4, dv4 = outs
            if shared:
                fold = lambda t: (t[:, 0:HEAD] + t[:, HEAD:2 * HEAD]) + (t[:, 2 * HEAD:3 * HEAD] + t[:, 3 * HEAD:])
                dk_ref[:, u * HEAD:(u + 1) * HEAD] = fold(dk4)
                dv_ref[:, u * HEAD:(u + 1) * HEAD] = fold(dv4).astype(_MM)
            else:
                dk_ref[:, us] = dk4
                dv_ref[:, us] = dv4.astype(_MM)

    two = lambda w: pl.BlockSpec((2 * BLK, w), lambda i: (i, 0))
    prev = lambda w: pl.BlockSpec((BLK, w), lambda i: (jnp.maximum(2 * i - 1, 0), 0))
    nxt = lambda w: pl.BlockSpec((BLK, w), lambda i: (jnp.minimum(2 * i + 2, nb - 1), 0))
    in_specs = [two(wq), nxt(wq), two(wk), prev(wk), two(wk), prev(wk), two(wq), nxt(wq), two(wq), nxt(wq), two(wq), nxt(wq)]
    args = [qn, qn, kn, kn, vn, vn, do, do, lse, lse, dl_or_o, dl_or_o]
    out_specs = [two(wq), two(wk), two(wk)]
    out_shape = [_sds((rows, wq)), _sds((rows, wk)), _sds((rows, wk), _MM)]
    if sink is not None:
        in_specs.append(pl.BlockSpec(memory_space=pltpu.SMEM))
        args.append(sink)
        out_specs.append(_acc((BLK, 128)))
        out_shape.append(_sds((BLK, 128)))
    return _pc(body, name=name, grid=(nb // 2,), in_specs=in_specs, out_specs=out_specs, out_shape=out_shape)(*args)


def _k_prep_bwd(srcs, dqn, dkn, gq, gk, tabs, tab_row, *, wq, wk, rows_per_gain, name):
    rows = dqn.shape[0]
    ts = min(512, rows)
    ngain = gq.shape[0]

    def body(q_ref, k_ref, dq_ref, dk_ref, gq_ref, gk_ref, c_ref, sa_ref, sb_ref, oq_ref, ok_ref, aq_ref, ak_ref):
        i = pl.program_id(0)

        @pl.when(lax.rem(i * ts, rows_per_gain) == 0)
        def _():
            aq_ref[...] = jnp.zeros_like(aq_ref)
            ak_ref[...] = jnp.zeros_like(ak_ref)

        c, sa, sb = c_ref[...], sa_ref[...], sb_ref[...]
        for x_ref, d_ref, g_ref, o_ref, a_ref in ((q_ref, dq_ref, gq_ref, oq_ref, aq_ref),
                                                   (k_ref, dk_ref, gk_ref, ok_ref, ak_ref)):
            xh, r = _seg_norm(x_ref[...], HEAD)
            dt = _rope_bwd(d_ref[...], c, sa, sb)
            a_ref[...] += _sum8(dt * xh)
            z = dt * g_ref[...]
            o_ref[...] = (r * (z - xh * _seg_mean(z * xh, HEAD))).astype(_MM)

    gspec = lambda w: pl.BlockSpec((None, 1, w), lambda i: ((i * ts) // rows_per_gain, 0, 0))
    aspec = lambda w: pl.BlockSpec((None, 8, w), lambda i: ((i * ts) // rows_per_gain, 0, 0))
    return _pc(
        body, name=name, grid=(rows // ts,),
        in_specs=[_row(ts, wq, srcs[0][1]), _row(ts, wk, srcs[1][1]), _row(ts, wq), _row(ts, wk), gspec(wq), gspec(wk)]
        + [pl.BlockSpec((ts, 128), lambda i: (i + tab_row // ts, 0))] * 3,
        out_specs=[_row(ts, wq), _row(ts, wk), aspec(wq), aspec(wk)],
        out_shape=[_sds((rows, wq), _MM), _sds((rows, wk), _MM), _sds((ngain, 8, wq)), _sds((ngain, 8, wk))])(
            srcs[0][0], srcs[1][0], dqn, dkn, gq, gk, *tabs)


def _k_in_bwd(pieces, dgp, x, g1, dx1, w_in, w_gate):
    s = x.shape[0]
    ts = min(256, s)
    nin, ng = w_in.shape[2], w_gate.shape[2]
    widths = [p.shape[1] for p in pieces]
    ncol = sum(widths)

    def body(*refs):
        p_refs = refs[:len(pieces)]
        dgp_ref, x_ref, g_ref, dx1_ref, wi_ref, wg_ref, gx_ref, dpj_ref, gacc_ref = refs[len(pieces):]
        i = pl.program_id(0)

        @pl.when(i == 0)
        def _():
            gacc_ref[...] = jnp.zeros_like(gacc_ref)

        off = 0
        for p_ref, w in zip(p_refs, widths):
            dpj_ref[:, off:off + w] = p_ref[...]
            off += w
        dh = jnp.zeros((ts, D_MODEL), F32)
        for j in range(CHIPS):
            dh = dh + _dot_nt(dpj_ref[:, j * nin:(j + 1) * nin], wi_ref[j])
            dh = dh + _dot_nt(dgp_ref[:, j * ng:(j + 1) * ng], wg_ref[j])
        xh, r = _rms(x_ref[...])
        gacc_ref[...] += _sum8(dh * xh)
        gx_ref[...] = dx1_ref[...] + _rms_bwd(dh, xh, r, g_ref[...])

    return _pc(
        body, name="in_proj_bwd", grid=(s // ts,),
        in_specs=[_row(ts, w) for w in widths] + [_row(ts, CHIPS * ng), _row(ts, D_MODEL), _res((1, D_MODEL)),
                                                  _row(ts, D_MODEL), _res(w_in.shape), _res(w_gate.shape)],
        out_specs=[_row(ts, D_MODEL), _row(ts, ncol), _acc((8, D_MODEL))],
        out_shape=[_sds((s, D_MODEL)), _sds((s, ncol), _MM), _sds((8, D_MODEL))])(*pieces, dgp, x, g1, dx1, w_in, w_gate)


def _k_wgrad(a, b, *, nblk, stacked, name):
    s, k = a.shape
    n = b.shape[1]
    nb = n // nblk
    ts = min(2048 if k <= 1024 else 1024, s)

    def body(a_ref, b_ref, o_ref):
        @pl.when(pl.program_id(1) == 0)
        def _():
            o_ref[...] = jnp.zeros_like(o_ref)

        o_ref[...] += _dot_tn(a_ref[...], b_ref[...])

    if stacked:
        out_spec, out_shape = pl.BlockSpec((None, k, nb), lambda g, t: (g, 0, 0)), _sds((nblk, k, nb))
    else:
        out_spec, out_shape = pl.BlockSpec((k, nb), lambda g, t: (0, g)), _sds((k, n))
    return _pc(body, name=name, grid=(nblk, s // ts),
               in_specs=[pl.BlockSpec((ts, k), lambda g, t: (t, 0)), pl.BlockSpec((ts, nb), lambda g, t: (t, g))],
               out_specs=[out_spec], out_shape=[out_shape])(a, b)[0]


def _to_res(t, d):
    s, c = t.shape
    return t if d == 1 else t.reshape(s // d, d, c).transpose(1, 0, 2).reshape(s, c)


def _from_res(t, d):
    s, c = t.shape
    return t if d == 1 else t.reshape(d, s // d, c).transpose(1, 0, 2).reshape(s, c)


def _tile_gain(g, heads):
    return jnp.tile(g, (1,) * (g.ndim - 1) + (heads,))[..., None, :]


def _local_step(x, mem, pos, target, small, get_w_in, get_rest, on_grads):
    s = x.shape[0]
    nblk = s // BLK
    g1, g2 = small["attn_norm"], small["ffn_norm"]

    pos_rows = jnp.concatenate([_to_res(pos[:, None], d)[:, 0] for _, d in A_GROUPS] + [pos])
    tabs = _rope_tables(pos_rows)
    w_in = get_w_in(tabs[0])

    h, qa0, qa1, qa2, q_b, k_b, v_b, m_q = _k_in(x, g1, w_in)

    qkv_a = jnp.concatenate([_to_res(t, d) for t, (_, d) in zip((qa0, qa1, qa2), A_GROUPS)], axis=0)
    gq_a = _tile_gain(small["a_q_norm"], A_HEADS)
    gk_a = _tile_gain(small["a_k_norm"], A_HEADS)
    src_a = ((qkv_a, 0), (qkv_a, 1), (qkv_a, 2))
    qn_a, kn_a, vn_a = _k_prep(src_a, gq_a, gk_a, tabs, 0, wq=A_W, wk=A_W, rows_per_gain=s, name="prep_a")
    segs_a = tuple((gi * nblk, nblk // d) for gi, (_, d) in enumerate(A_GROUPS))
    o_res, l_res = _k_band_fwd(qn_a, kn_a, vn_a, hq=A_HEADS, hk=A_HEADS, max_dist=BLK, segs=segs_a, sink=None,
                               name="attn_a")
    og = [_from_res(o_res[gi * s:(gi + 1) * s], d) for gi, (_, d) in enumerate(A_GROUPS)]
    lg = [_from_res(l_res[gi * s:(gi + 1) * s], d) for gi, (_, d) in enumerate(A_GROUPS)]

    gq_b = _tile_gain(small["b_q_norm"], B_QH)
    gk_b = _tile_gain(small["b_k_norm"], B_KVH)
    src_b = ((q_b, 0), (k_b, 0), (v_b, 0))
    qn_b, kn_b, vn_b = _k_prep(src_b, gq_b, gk_b, tabs, 3 * s, wq=B_QH * HEAD, wk=B_KVH * HEAD, rows_per_gain=s,
                               name="prep_b")
    sink_x = small["b_sinks"][0]
    segs_b = ((0, nblk),)
    o_b, l_b = _k_band_fwd(qn_b, kn_b, vn_b, hq=B_QH, hk=B_KVH, max_dist=B_WINDOW - 1, segs=segs_b, sink=sink_x,
                           name="attn_b")

    wts = get_rest(0, o_b)
    gates = _k_gate(h, wts["w_gate"], small["b_gate"])

    gq_m = _tile_gain(small["m_q_norm"], M_HEADS)[0]
    gk_m = _tile_gain(small["m_k_norm"], M_HEADS)[0]
    mem_n, kv, mk, mv = _k_memkv(mem, small["mem_norm"], wts["w_mem_kv"], gk_m)
    o_m = _k_mem_fwd(m_q, gq_m, mk, mv)

    o_a, merged, x1, h2 = _k_merge(og, lg, o_b, o_m, gates, x, wts["w_o_a"], wts["w_o_b"], wts["w_o_m"],
                                   wts["w_out"], g2)
    wts.update(get_rest(1, x1))
    u = _k_up(h2, wts["w_up"])
    dy, f, dc, loss_acc = _k_ffn(u, wts["conv_w"], small["conv_b"], wts["w_down"], wts["w_down"].T, x1, target)
    loss = (0.5 / D_MODEL) * jnp.sum(loss_acc)

    dx1, du, cacc, g2acc = _k_conv_bwd(dc, u, wts["conv_w"], wts["w_up"], x1, g2, dy)
    tok = on_grads({"w_up": _k_wgrad(h2, du, nblk=CHIPS, stacked=True, name="dw_up"),
                    "w_down": _k_wgrad(f, dy, nblk=2, stacked=False, name="dw_down").reshape(CHIPS, -1, D_MODEL)}, dx1)
    (dgp, dp_a, dp_b, dp_m, dog0, dog1, dog2, dl0, dl1, dl2, do_b, do_m, bacc) = _k_merge_bwd(
        dx1, og, lg, o_a, o_b, o_m, gates, wts["w_o_a"], wts["w_o_b"], wts["w_o_m"], wts["w_out"], tok)
    tok = on_grads({"w_gate": _k_wgrad(h, dgp, nblk=CHIPS, stacked=True, name="dw_gate"),
                    "w_o_a": _k_wgrad(o_a, dp_a, nblk=CHIPS, stacked=True, name="dw_o_a"),
                    "w_o_b": _k_wgrad(o_b, dp_b, nblk=CHIPS, stacked=True, name="dw_o_b"),
                    "w_o_m": _k_wgrad(o_m, dp_m, nblk=CHIPS, stacked=True, name="dw_o_m"),
                    "w_out": _k_wgrad(merged, dx1, nblk=1, stacked=False, name="dw_out").reshape(CHIPS, -1, D_MODEL)},
                   do_m)

    dq_m, dmk, dmv, gqm_acc = _k_mem_bwd(m_q, gq_m + tok[0:1, 0:1], mk, mv, o_m, do_m)
    dw_kv, gmem_acc, gkm_acc = _k_memkv_bwd(mem, small["mem_norm"], wts["w_mem_kv"], gk_m, mem_n, kv, dmk, dmv)

    dq_bn, dk_bn, dv_b, sacc = _k_band_bwd(qn_b, kn_b, vn_b, do_b, l_b, o_b, hq=B_QH, hk=B_KVH,
                                           max_dist=B_WINDOW - 1, segs=segs_b, sink=sink_x, name="attn_b_bwd")
    tok = on_grads({}, dq_bn)
    dq_b, dk_b, gqb_acc, gkb_acc = _k_prep_bwd(src_b, dq_bn, dk_bn, gq_b + tok[0:1, 0:1], gk_b, tabs, 3 * s, wq=B_QH * HEAD,
                                               wk=B_KVH * HEAD, rows_per_gain=s, name="prep_b_bwd")

    do_res = jnp.concatenate([_to_res(t, d) for t, (_, d) in zip((dog0, dog1, dog2), A_GROUPS)], axis=0)
    dl_res = jnp.concatenate([_to_res(t, d) for t, (_, d) in zip((dl0, dl1, dl2), A_GROUPS)], axis=0)
    dq_an, dk_an, dv_a = _k_band_bwd(qn_a, kn_a, vn_a, do_res, l_res, dl_res, hq=A_HEADS, hk=A_HEADS, max_dist=BLK,
                                     segs=segs_a, sink=None, name="attn_a_bwd")
    dq_a, dk_a, gqa_acc, gka_acc = _k_prep_bwd(src_a, dq_an, dk_an, gq_a, gk_a, tabs, 0, wq=A_W, wk=A_W,
                                               rows_per_gain=s, name="prep_a_bwd")
    pieces = []
    for gi, (_, d) in enumerate(A_GROUPS):
        rs = slice(gi * s, (gi + 1) * s)
        pieces += [_from_res(t[rs], d) for t in (dq_a, dk_a, dv_a)]
    pieces += [dq_b, dk_b, dv_b, dq_m]
    grad_x, dproj, g1acc = _k_in_bwd(pieces, dgp, x, g1, dx1, w_in, wts["w_gate"])
    on_grads({"w_in": _k_wgrad(h, dproj, nblk=CHIPS, stacked=True, name="dw_in"),
              "w_mem_kv": dw_kv.reshape(CHIPS, -1, 2 * M_W)}, grad_x)

    def fold(acc, heads):
        v = jnp.sum(acc, axis=-2)
        return jnp.sum(v.reshape(v.shape[:-1] + (heads, -1)), axis=-2)

    csum = jnp.sum(cacc, axis=1)
    sml = {
        "attn_norm": jnp.sum(g1acc, axis=0), "a_q_norm": fold(gqa_acc, A_HEADS), "a_k_norm": fold(gka_acc, A_HEADS),
        "b_q_norm": fold(gqb_acc[0], B_QH), "b_k_norm": fold(gkb_acc[0], B_KVH),
        "b_sinks": jnp.sum(sacc, axis=0)[:B_QH], "mem_norm": jnp.sum(gmem_acc, axis=0),
        "m_q_norm": fold(gqm_acc, M_HEADS), "m_k_norm": fold(gkm_acc, M_HEADS),
        "b_gate": jnp.sum(bacc, axis=0), "ffn_norm": jnp.sum(g2acc, axis=0),
        "conv_w": csum[1:], "conv_b": csum[0],
    }
    return loss, grad_x, sml


def _mesh_pos():
    return lax.axis_index("x"), lax.axis_index("y"), lax.axis_index("c")


def _chip_peers(x, y):
    return [(1 - x, y), (x, 1 - y), (1 - x, 1 - y)]


_ANY = pl.BlockSpec(memory_space=pl.ANY)


def _comm_call(body, *, name, n_in, out_shape, scratch):
    return pl.pallas_call(body, name=name, in_specs=[_ANY] * n_in, out_specs=[_ANY] * len(out_shape),
                          out_shape=out_shape, scratch_shapes=scratch)


def _remote(src, dst, send_sem, recv_sem, dev):
    return pltpu.make_async_remote_copy(src_ref=src, dst_ref=dst, send_sem=send_sem, recv_sem=recv_sem,
                                        device_id=dev, device_id_type=MESH)


def _pair_join(halves, name):
    nt = len(halves)

    def body(*refs):
        ins, got = refs[:nt], refs[nt:2 * nt]
        send_sems, recv_sems = refs[2 * nt:]
        x, y, c = _mesh_pos()
        cps = []
        for t in range(nt):
            rc = _remote(ins[t], got[t], send_sems.at[t], recv_sems.at[t], (x, y, 1 - c))
            rc.start()
            cps.append(rc)
        for rc in cps:
            rc.wait()

    out_shape = [_sds(hf.shape, hf.dtype) for hf in halves]
    scratch = [pltpu.SemaphoreType.DMA((nt,)), pltpu.SemaphoreType.DMA((nt,))]
    return _comm_call(body, name=name, n_in=nt, out_shape=out_shape, scratch=scratch)(*halves)


_HBM = pl.BlockSpec(memory_space=pltpu.HBM)
_SEMS = pl.BlockSpec(memory_space=pltpu.SEMAPHORE)
_EFFECT = pltpu.SideEffectType.DATAFLOW_SIDE_EFFECTING


def _bcast_copies(ins, lands, send_sems, recv_sems):
    x, y, c = _mesh_pos()
    me = 2 * x + y
    targets = [((px, py, c), 2 * px + py) for px, py in _chip_peers(x, y)] + [((x, y, 1 - c), me)]
    out = []
    for t in range(len(ins)):
        for k, (dev, idx) in enumerate(targets):
            i = t * len(targets) + k
            arrival = lambda t=t, i=i, idx=idx, dev=dev: _remote(ins[t], lands[t].at[idx], send_sems.at[i],
                                                                 recv_sems.at[i], dev)
            out.append((_remote(ins[t], lands[t].at[me], send_sems.at[i], recv_sems.at[i], dev), arrival))
    return out


def _scatter_copies(ins, lands, send_sems, recv_sems):
    x, y, c = _mesh_pos()
    out = []
    for t in range(len(ins)):
        for k, (px, py) in enumerate(_chip_peers(x, y)):
            i = t * 3 + k
            cp = _remote(ins[t].at[2 * px + py], lands[t].at[k], send_sems.at[i], recv_sems.at[i], (px, py, c))
            out.append((cp, lambda cp=cp: cp))
    return out


def _pair_copies(ins, lands, send_sems, recv_sems):
    x, y, c = _mesh_pos()
    out = []
    for t in range(len(ins)):
        hr = ins[t].shape[1] // 2
        give = ins[t].at[:, pl.ds(pl.multiple_of((1 - c) * hr, 8), hr), :]
        cp = _remote(give, lands[t], send_sems.at[t], recv_sems.at[t], (x, y, 1 - c))
        out.append((cp, lambda cp=cp: cp))
    return out


def _join_copies(ins, lands, send_sems, recv_sems):
    x, y, c = _mesh_pos()
    out = []
    for t in range(len(ins)):
        cp = _remote(ins[t], lands[t], send_sems.at[t], recv_sems.at[t], (x, y, 1 - c))
        out.append((cp, lambda cp=cp: cp))
    return out


def _half_copies(ins, lands, send_sems, recv_sems):
    x, y, c = _mesh_pos()
    me = 2 * x + y
    out = []
    for t in range(len(ins)):
        hr = ins[t].shape[0] // 2
        rows = pl.ds(pl.multiple_of(c * hr, 8), hr)
        for k, (px, py) in enumerate(_chip_peers(x, y)):
            i = t * 3 + k
            arrival = lambda t=t, i=i, px=px, py=py, rows=rows: _remote(
                ins[t].at[rows, :], lands[t].at[2 * px + py].at[rows, :], send_sems.at[i], recv_sems.at[i], (px, py, c))
            out.append((_remote(ins[t].at[rows, :], lands[t].at[me].at[rows, :], send_sems.at[i], recv_sems.at[i],
                                (px, py, c)), arrival))
    return out


def _finish_halves(shards, stacks):
    nt = len(shards)

    def body(*refs):
        ins, held, outs = refs[:nt], refs[nt:2 * nt], refs[2 * nt:3 * nt]
        fwd_s, fwd_r, own_s, own_r = refs[3 * nt:]
        x, y, c = _mesh_pos()
        me = 2 * x + y
        sib = (x, y, 1 - c)
        pending = []
        for t in range(nt):
            hr = shards[t].shape[0] // 2
            half = lambda ref, who: ref.at[pl.ds(pl.multiple_of(who * hr, 8), hr), :]
            own = _remote(ins[t], outs[t].at[me], own_s.at[t], own_r.at[t], sib)
            own.start()
            pending.append(own.wait)
            for k, (px, py) in enumerate(_chip_peers(x, y)):
                pj = 2 * px + py
                fw = _remote(half(held[t].at[pj], c), half(outs[t].at[pj], c), fwd_s.at[t, k], fwd_r.at[t, k], sib)
                fw.start()
                pending.append(fw.wait_send)
                other = half(outs[t].at[pj], 1 - c)
                pending.append(_remote(other, other, fwd_s.at[t, k], fwd_r.at[t, k], sib).wait_recv)
        for wait in pending:
            wait()

    dma = pltpu.SemaphoreType.DMA
    return pl.pallas_call(
        body, name="gather_w_in_finish", in_specs=[_ANY] * (2 * nt), out_specs=[_ANY] * nt,
        out_shape=[_sds(a.shape, a.dtype) for a in stacks], input_output_aliases={nt + i: i for i in range(nt)},
        scratch_shapes=[dma((nt, 3)), dma((nt, 3)), dma((nt,)), dma((nt,))])(*shards, *stacks)


def _split_start(copies, srcs, land_shapes, ncopy, dep, name, lands=None):
    nt = len(srcs)

    def body(*refs):
        ins, lands = refs[:nt], refs[nt:2 * nt]
        send_sems, recv_sems, token = refs[2 * nt + 1], refs[2 * nt + 2], refs[-1]
        for send, _ in copies(ins, lands, send_sems, recv_sems):
            send.start()
        token[...] = jnp.zeros_like(token)

    if lands is None:
        lands = [lax.empty(sh, a.dtype) for sh, a in zip(land_shapes, srcs)]
    lands = [pltpu.with_memory_space_constraint(a, pltpu.HBM) for a in lands]
    srcs = [pltpu.with_memory_space_constraint(a, pltpu.HBM) for a in srcs]
    dma = pltpu.SemaphoreType.DMA
    out_shape = ([dma((nt * ncopy,)), dma((nt * ncopy,))] + [pltpu.HBM(a.shape, a.dtype) for a in srcs + lands]
                 + [_sds((8, 128))])
    outs = pl.pallas_call(
        body, name=name, in_specs=[_HBM] * (2 * nt) + [_ANY],
        out_specs=[_SEMS, _SEMS] + [_HBM] * (2 * nt) + [pl.BlockSpec(memory_space=pltpu.VMEM)], out_shape=out_shape,
        input_output_aliases={i: 2 + i for i in range(2 * nt)},
        compiler_params=pltpu.CompilerParams(has_side_effects=_EFFECT))(*srcs, *lands, dep)
    return outs[0], outs[1], outs[2:2 + nt], outs[2 + nt:2 + 2 * nt], outs[-1]


def _split_wait(copies, send_sems, recv_sems, srcs, lands, after, name):
    nt = len(srcs)

    def body(*refs):
        ins, lnd = refs[:nt], refs[nt:2 * nt]
        for send, arrival in copies(ins, lnd, refs[2 * nt], refs[2 * nt + 1]):
            send.wait_send()
            arrival().wait_recv()

    outs = pl.pallas_call(
        body, name=name, in_specs=[_HBM] * (2 * nt) + [_SEMS, _SEMS, _ANY], out_specs=[_HBM] * (2 * nt),
        out_shape=[pltpu.HBM(a.shape, a.dtype) for a in list(srcs) + list(lands)],
        input_output_aliases={i: i for i in range(2 * nt)},
        compiler_params=pltpu.CompilerParams(has_side_effects=_EFFECT))(*srcs, *lands, send_sems, recv_sems, after)
    return outs[:nt], outs[nt:]


def _small_copies(ins, lands, send_sems, recv_sems):
    x, y, c = _mesh_pos()
    me = 4 * x + 2 * y + c
    out = []
    for k in range(1, NDEV):
        px, py, pc = x ^ (k >> 2), y ^ ((k >> 1) & 1), c ^ (k & 1)
        arrival = lambda k=k, px=px, py=py, pc=pc: _remote(ins[0], lands[0].at[4 * px + 2 * py + pc], send_sems.at[k - 1],
                                                            recv_sems.at[k - 1], (px, py, pc))
        out.append((_remote(ins[0], lands[0].at[me], send_sems.at[k - 1], recv_sems.at[k - 1], (px, py, pc)), arrival))
    return out


def _row_tile(r, c, mib=1):
    t = r
    while t * c * 4 > (mib << 20) and t % 16 == 0:
        t //= 2
    return t


def _k_pair_add(full, got, name):
    g, r, c = full.shape
    hr = r // 2
    tr = _row_tile(hr, c, 4)
    nh = hr // tr

    def body(a_ref, b_ref, o_ref):
        o_ref[...] = (a_ref[...] + b_ref[...]).astype(_WIRE)

    mine = pl.BlockSpec((None, tr, c), lambda i, j: (i, lax.axis_index("c") * nh + j, 0))
    spec = pl.BlockSpec((None, tr, c), lambda i, j: (i, j, 0))
    return _pc(body, name=name, grid=(g, nh), in_specs=[mine, spec], out_specs=[spec],
               out_shape=[_sds((g, hr, c), _WIRE)])(full, got)[0]


def _k_chip_sum(parts, slots, name):
    _, r, c = parts.shape
    tr = _row_tile(r, c, 4)

    def body(a_ref, s_ref, o_ref):
        acc = a_ref[...].astype(F32)
        for k in range(3):
            acc = acc + s_ref[k].astype(F32)
        o_ref[...] = acc

    own = pl.BlockSpec((None, tr, c), lambda i: (2 * lax.axis_index("x") + lax.axis_index("y"), i, 0))
    return _pc(body, name=name, grid=(r // tr,), in_specs=[own, pl.BlockSpec((3, tr, c), lambda i: (0, i, 0))],
               out_specs=[_row(tr, c)], out_shape=[_sds((r, c))])(parts, slots)[0]


def _adam(w, g, m, v):
    m = ADAM_B1 * m + (1.0 - ADAM_B1) * g
    v = ADAM_B2 * v + (1.0 - ADAM_B2) * (g * g)
    m_hat = m / (1.0 - ADAM_B1 ** ADAM_STEP)
    v_hat = v / (1.0 - ADAM_B2 ** ADAM_STEP)
    return -ADAM_LR * (m_hat / (jnp.sqrt(v_hat) + ADAM_EPS) + ADAM_WD * w), m, v


def _k_adam(w, mine, theirs, m, v, dep, name):
    r, c = w.shape
    hr = r // 2
    tr = _row_tile(hr, c, 2)
    nh = hr // tr

    def body(w_ref, a_ref, b_ref, m_ref, v_ref, dep_ref, g_ref, d_ref, mo_ref, vo_ref):
        upper = (pl.program_id(0) >= nh).astype(jnp.int32)
        g = jnp.where(upper == lax.axis_index("c"), a_ref[...], b_ref[...])
        g_ref[...] = g
        d_ref[...], mo_ref[...], vo_ref[...] = _adam(w_ref[...], g, m_ref[...], v_ref[...])

    hspec = pl.BlockSpec((tr, c), lambda i: (jnp.where(i >= nh, i - nh, i), 0))
    return _pc(body, name=name, grid=(r // tr,),
               in_specs=[_row(tr, c), hspec, hspec, _row(tr, c), _row(tr, c), _res((8, 128))],
               out_specs=[_row(tr, c)] * 4, out_shape=[_sds((r, c))] * 4)(w, mine, theirs, m, v, dep)


def _k_sum8(a):
    _, n, _ = a.shape

    def body(a_ref, o_ref):
        acc = a_ref[0]
        for k in range(1, NDEV):
            acc = acc + a_ref[k]
        o_ref[...] = acc

    return _pc(body, name="sum_small_grads", grid=(1,), in_specs=[_acc(a.shape)], out_specs=[_acc((n, 128))],
               out_shape=[_sds((n, 128))])(a)[0]


def _k_adam_small(ws, gs, ms, vs):
    n = len(ws)

    def body(*refs):
        for k in range(n):
            w_ref, g_ref, m_ref, v_ref, d_ref, mo_ref, vo_ref = refs[k::n]
            d_ref[...], mo_ref[...], vo_ref[...] = _adam(w_ref[...], g_ref[...], m_ref[...], v_ref[...])

    specs = [_acc(a.shape) for a in ws]
    outs = _pc(body, name="adam_small", grid=(1,), in_specs=specs * 4, out_specs=specs * 3,
               out_shape=[_sds(a.shape) for a in ws] * 3)(*ws, *gs, *ms, *vs)
    return outs[:n], outs[n:2 * n], outs[2 * n:]


def _pack(vals):
    rows = []
    for a in vals:
        flat = a.reshape(-1)
        n = -(-flat.shape[0] // 1024) * 1024
        rows.append(jnp.pad(flat, (0, n - flat.shape[0])).reshape(n // 128, 128))
    return jnp.concatenate(rows, axis=0)


def _unpack(packed, shapes):
    out, off = [], 0
    for sh in shapes:
        size = int(np.prod(sh))
        n = -(-size // 1024) * 1024
        out.append(packed[off // 128:(off + n) // 128].reshape(-1)[:size].reshape(sh))
        off += n
    return out


_WEIGHTS = ["attn_norm", "w_in", "a_q_norm", "a_k_norm", "b_q_norm", "b_k_norm", "b_sinks", "mem_norm", "w_mem_kv",
            "m_q_norm", "m_k_norm", "w_o_a", "w_o_b", "w_o_m", "w_gate", "b_gate", "w_out", "ffn_norm", "w_up",
            "conv_w", "conv_b", "w_down"]
_BIG = ["w_in", "w_mem_kv", "w_o_a", "w_o_b", "w_o_m", "w_gate", "w_out", "w_up", "w_down"]
_SMALL = [n for n in _WEIGHTS if n not in _BIG]


def kernel(x, mem, positions, attn_norm, w_in, a_q_norm, a_k_norm, b_q_norm, b_k_norm, b_sinks, mem_norm, w_mem_kv, m_q_norm, m_k_norm, w_o_a, w_o_b, w_o_m, w_gate, b_gate, w_out, ffn_norm, w_up, conv_w, conv_b, w_down, loss_target, m_attn_norm, m_w_in, m_a_q_norm, m_a_k_norm, m_b_q_norm, m_b_k_norm, m_b_sinks, m_mem_norm, m_w_mem_kv, m_m_q_norm, m_m_k_norm, m_w_o_a, m_w_o_b, m_w_o_m, m_w_gate, m_b_gate, m_w_out, m_ffn_norm, m_w_up, m_conv_w, m_conv_b, m_w_down, v_attn_norm, v_w_in, v_a_q_norm, v_a_k_norm, v_b_q_norm, v_b_k_norm, v_b_sinks, v_mem_norm, v_w_mem_kv, v_m_q_norm, v_m_k_norm, v_w_o_a, v_w_o_b, v_w_o_m, v_w_gate, v_b_gate, v_w_out, v_ffn_norm, v_w_up, v_conv_w, v_conv_b, v_w_down):
    given = dict(locals())
    w = {n: given[n][0] for n in _WEIGHTS}
    m1 = {n: given["m_" + n][0] for n in _WEIGHTS}
    m2 = {n: given["v_" + n][0] for n in _WEIGHTS}

    zeros = jnp.zeros((8, 128), F32)
    w_in_shard = w["w_in"].astype(_MM)
    *w_in_handles, tok = _split_start(_half_copies, [w_in_shard], [(CHIPS,) + w_in_shard.shape], 3, zeros,
                                      "gather_w_in_start")

    def get_w_in(after):
        send, recv, srcs, lands = w_in_handles
        srcs, lands = _split_wait(_half_copies, send, recv, srcs, lands, after, "gather_w_in_wait")
        return _finish_halves(srcs, lands)[0]

    stages = (["w_gate", "w_mem_kv", "w_o_a", "w_o_b", "w_o_m", "w_out"], ["w_up", "w_down", "conv_w"])
    started = []
    for k, names in enumerate(stages):
        shards = [w[n] if n == "conv_w" else w[n].astype(_MM) for n in names]
        *handles, tok = _split_start(_bcast_copies, shards, [(CHIPS,) + a.shape for a in shards], 4, tok,
                                     "gather_start_%d" % k)
        started.append(handles)
    small = {n: (w[n][None, :] if w[n].ndim == 1 else w[n]) for n in _SMALL if n != "conv_w"}
    positions = positions + tok[0:1, 0:1].astype(positions.dtype)

    def get_rest(stage, after):
        send, recv, srcs, lands = started[stage]
        got = _split_wait(_bcast_copies, send, recv, srcs, lands, after, "gather_wait_%d" % stage)[1]
        wts = dict(zip(stages[stage], got))
        for n in ("w_mem_kv", "w_out", "w_down"):
            if n in wts:
                wts[n] = wts[n].reshape(-1, wts[n].shape[-1])
        return wts

    parts, slots, pair, scat, started_pair = {}, {}, [], [], [None]

    def finish_pair(after):
        names, tag, send, recv, srcs, lands = pair.pop()
        full, got = _split_wait(_pair_copies, send, recv, srcs, lands, after, "pair_wait_" + tag)
        mine = [_k_pair_add(f, b, "pair_add_" + n) for n, f, b in zip(names, full, got)]
        shapes = [(3,) + p.shape[1:] for p in mine]
        send, recv, srcs, lands, token = _split_start(_scatter_copies, mine, shapes, 3, zeros, "scatter_start_" + tag)
        scat.append((names, tag, send, recv, srcs, lands))
        return token

    def on_grads(group, after):
        names = list(group)
        tag = "_".join(names)
        token = finish_pair(after) if pair else zeros
        if not group:
            return token
        grads_g = [group[n] for n in names]
        shapes = [(CHIPS, g.shape[1] // 2, g.shape[2]) for g in grads_g]
        send, recv, srcs, lands, token = _split_start(_pair_copies, grads_g, shapes, 1, token, "pair_start_" + tag)
        pair.append((names, tag, send, recv, srcs, lands))
        started_pair[0] = token
        return token

    loss, grad_x, sml = _local_step(x[0], mem[0], positions[0], loss_target[0], small, get_w_in, get_rest, on_grads)

    packed = _pack([sml[n] for n in _SMALL] + [loss.reshape(1)])
    me = 4 * lax.axis_index("x") + 2 * lax.axis_index("y") + lax.axis_index("c")
    land = lax.dynamic_update_slice(jnp.zeros((NDEV,) + packed.shape, F32), packed[None], (me, 0, 0))
    *small_h, tok = _split_start(_small_copies, [packed], None, NDEV - 1, zeros, "gather_small_start", lands=[land])
    started_pair[0] = started_pair[0] + tok

    early = [n for names, *_ in scat for n in names]
    for names, tag, send, recv, srcs, lands in scat:
        mine, got = _split_wait(_scatter_copies, send, recv, srcs, lands, started_pair[0], "scatter_wait_" + tag)
        parts.update(zip(names, mine))
        slots.update(zip(names, got))
    scat.clear()
    reduced = {n: _k_chip_sum(parts[n], slots[n], "chip_add_" + n) for n in early}
    halves = [reduced[n] for n in early]
    *join, tok = _split_start(_join_copies, halves, [a.shape for a in halves], 1, zeros, "pair_join_start_early")
    grads = {}

    delta, new_m, new_v = {}, {}, {}
    dep = finish_pair(tok)
    mine, got = _split_wait(_join_copies, *join, dep, "pair_join_wait_early")
    reduced.update(zip(early, mine))
    theirs = dict(zip(early, got))
    for n in early:
        grads[n], delta[n], new_m[n], new_v[n] = _k_adam(w[n], reduced[n], theirs[n], m1[n], m2[n], dep, "adam_" + n)
        dep = delta[n]
    gathered = _split_wait(_small_copies, *small_h, dep, "gather_small_wait")[1][0]
    shapes = [sml[n].shape for n in _SMALL] + [(1,)]
    *gsmall, loss = _unpack(_k_sum8(gathered), shapes)
    loss = loss[0]
    gsm = dict(zip(_SMALL, gsmall))
    nu = w["conv_w"].shape[1]
    chip = 2 * lax.axis_index("x") + lax.axis_index("y")
    gsm["conv_w"] = lax.dynamic_slice_in_dim(gsm["conv_w"], chip * nu, nu, axis=1)
    for n in _SMALL:
        grads[n] = gsm[n].reshape(w[n].shape)
    as2d = lambda d: [d[n][None, :] if d[n].ndim == 1 else d[n] for n in _SMALL]
    for dst, outs in zip((delta, new_m, new_v), _k_adam_small(as2d(w), as2d(grads), as2d(m1), as2d(m2))):
        dst.update((n, a.reshape(w[n].shape)) for n, a in zip(_SMALL, outs))
    late, tag, send, recv, srcs, lands = scat.pop()
    mine, got = _split_wait(_scatter_copies, send, recv, srcs, lands, dep, "scatter_wait_" + tag)
    for n, a, b in zip(late, mine, got):
        reduced[n] = _k_chip_sum(a, b, "chip_add_" + n)
    theirs.update(zip(late, _pair_join([reduced[n] for n in late], "grad_pair_join_late")))
    for n in late:
        grads[n], delta[n], new_m[n], new_v[n] = _k_adam(w[n], reduced[n], theirs[n], m1[n], m2[n], zeros, "adam_" + n)

    lead = lambda d: [d[n][None] for n in _WEIGHTS]
    return (loss, grad_x[None], *lead(grads), *lead(delta), *lead(new_m), *lead(new_v))
```

```python
import math

import jax
import jax.numpy as jnp
import numpy as np
from jax import lax
from jax.experimental import pallas as pl
from jax.experimental.pallas import tpu as pltpu

F32 = jnp.float32
_MM = jnp.bfloat16
_WIRE = jnp.bfloat16

D_MODEL = 1024
HEAD = 64
BLK = 128
A_GROUPS = ((128, 1), (512, 4), (2048, 16))
A_HEADS = 4
A_W = A_HEADS * HEAD
B_QH = 8
B_KVH = 2
B_WINDOW = 128
M_HEADS = 4
M_HD = 128
M_W = M_HEADS * M_HD
D_FF = 2816
EPS = 1e-6
NEG = -1e30
ROPE_THETA = 500000.0
ROPE_ROT = 16
CHIPS = 4
NDEV = 8
ADAM_LR, ADAM_B1, ADAM_B2, ADAM_EPS, ADAM_WD, ADAM_STEP = 0.001, 0.9, 0.999, 1e-08, 0.01, 10
VMEM_LIMIT = 58 * 1024 * 1024
MESH = pl.DeviceIdType.MESH


def _pc(body, *, name, grid, in_specs, out_specs, out_shape, scratch=()):
    return pl.pallas_call(
        body, name=name, grid=grid, in_specs=in_specs, out_specs=out_specs, out_shape=out_shape,
        scratch_shapes=list(scratch),
        compiler_params=pltpu.CompilerParams(dimension_semantics=("arbitrary",) * len(grid),
                                             vmem_limit_bytes=VMEM_LIMIT))


def _row(ts, c, col=0):
    return pl.BlockSpec((ts, c), lambda i: (i, col))


def _res(shape):
    n = len(shape)
    return pl.BlockSpec(tuple(shape), lambda i: (0,) * n, pipeline_mode=pl.Buffered(1))


def _acc(shape):
    n = len(shape)
    return pl.BlockSpec(tuple(shape), lambda i: (0,) * n)


def _sds(shape, dtype=F32):
    return jax.ShapeDtypeStruct(tuple(shape), dtype)


def _dot(a, b):
    return jnp.dot(a.astype(_MM), b.astype(_MM), preferred_element_type=F32)


def _dot_nt(a, b):
    return lax.dot_general(a.astype(_MM), b.astype(_MM), (((1,), (1,)), ((), ())), preferred_element_type=F32)


def _dot_tn(a, b):
    return lax.dot_general(a.astype(_MM), b.astype(_MM), (((0,), (0,)), ((), ())), preferred_element_type=F32)


def _sum8(v):
    ts, c = v.shape
    return jnp.sum(v.reshape(ts // 8, 8, c), axis=0)


def _sigmoid(z):
    return 1.0 / (1.0 + jnp.exp(-z))


def _rms(x):
    r = lax.rsqrt(jnp.mean(x * x, axis=-1, keepdims=True) + EPS)
    return x * r, r


def _rms_bwd(dy, xh, r, gain):
    z = dy * gain
    return r * (z - xh * jnp.mean(z * xh, axis=-1, keepdims=True))


def _split_hi_lo(v):
    hi = v.astype(_MM)
    return hi, (v - hi.astype(F32)).astype(_MM)


def _lane_head(shape):
    return lax.shift_right_logical(lax.broadcasted_iota(jnp.int32, shape, len(shape) - 1), 6)


def _seg_sum64(v):
    w = v.shape[1]
    e = jnp.where(_lane_head((w, w)) == lax.shift_right_logical(lax.broadcasted_iota(jnp.int32, (w, w), 0), 6),
                  1.0, 0.0).astype(_MM)
    hi, lo = _split_hi_lo(v)
    return jnp.dot(hi, e, preferred_element_type=F32) + jnp.dot(lo, e, preferred_element_type=F32)


def _seg_norm(x, seg):
    if seg == HEAD:
        r = lax.rsqrt(_seg_sum64(x * x) * (1.0 / HEAD) + EPS)
        return x * r, r
    w = x.shape[1]
    xh, rr = [], []
    for s in range(w // seg):
        xs = x[:, s * seg:(s + 1) * seg]
        r = lax.rsqrt(jnp.mean(xs * xs, axis=-1, keepdims=True) + EPS)
        xh.append(xs * r)
        rr.append(jnp.broadcast_to(r, xs.shape))
    return jnp.concatenate(xh, axis=1), jnp.concatenate(rr, axis=1)


def _seg_mean(v, seg):
    if seg == HEAD:
        return _seg_sum64(v) * (1.0 / HEAD)
    w = v.shape[1]
    out = []
    for s in range(w // seg):
        vs = v[:, s * seg:(s + 1) * seg]
        out.append(jnp.broadcast_to(jnp.mean(vs, axis=-1, keepdims=True), vs.shape))
    return jnp.concatenate(out, axis=1)


def _rope(t, c, sa, sb):
    out = []
    for cb in range(t.shape[1] // 128):
        tc = t[:, cb * 128:(cb + 1) * 128]
        out.append(tc * c + pltpu.roll(tc, 120, 1) * sa + pltpu.roll(tc, 8, 1) * sb)
    return jnp.concatenate(out, axis=1) if len(out) > 1 else out[0]


def _rope_bwd(dy, c, sa, sb):
    out = []
    for cb in range(dy.shape[1] // 128):
        dc = dy[:, cb * 128:(cb + 1) * 128]
        out.append(dc * c + pltpu.roll(dc * sa, 8, 1) + pltpu.roll(dc * sb, 120, 1))
    return jnp.concatenate(out, axis=1) if len(out) > 1 else out[0]


def _rope_consts():
    half = ROPE_ROT // 2
    c = np.float32(-2.0 * math.log(ROPE_THETA) / ROPE_ROT)
    freqs = np.exp(np.arange(half, dtype=np.float32) * c).astype(np.float32)
    place = np.zeros((3, half, 128), np.float32)
    ones = np.zeros((1, 128), np.float32)
    for lane in range(128):
        d = lane % HEAD
        if d < half:
            place[0, d, lane], place[1, d, lane] = 1.0, -1.0
        elif d < ROPE_ROT:
            place[0, d - half, lane], place[2, d - half, lane] = 1.0, 1.0
        else:
            ones[0, lane] = 1.0
    return np.tile(freqs[:, None], (1, 128)), place, ones


def _rope_tables(pos_rows):
    r = pos_rows.shape[0]
    tr = min(1024, r)
    freqs, place, ones = _rope_consts()

    def split3(v):
        hi, mid = _split_hi_lo(v)
        lo = (v - hi.astype(F32) - mid.astype(F32)).astype(_MM)
        return hi, mid, lo

    def body(p_ref, f_ref, e_ref, one_ref, c_ref, sa_ref, sb_ref):
        ang = jnp.concatenate([p_ref[j:j + 1, :].astype(F32) * f_ref[...] for j in range(tr // 128)], axis=1)
        cos, sin = jnp.cos(ang), jnp.sin(ang)
        for ref, k, v in ((c_ref, 0, cos), (sa_ref, 1, sin), (sb_ref, 2, sin)):
            e = e_ref[k].astype(_MM)
            out = sum(_dot_tn(part, e) for part in split3(v))
            ref[...] = out + one_ref[...] if k == 0 else out

    return _pc(body, name="rope_tables", grid=(r // tr,),
               in_specs=[pl.BlockSpec((tr // 128, 128), lambda i: (i, 0)), _acc((ROPE_ROT // 2, 128)),
                         _acc((3, ROPE_ROT // 2, 128)), _acc((1, 128))],
               out_specs=[_row(tr, 128)] * 3, out_shape=[_sds((r, 128))] * 3)(
                   pos_rows.reshape(r // 128, 128), jnp.asarray(freqs), jnp.asarray(place), jnp.asarray(ones))


def _k_in(x, g1, w_in):
    s = x.shape[0]
    ts = min(512, s)
    nin = w_in.shape[2]
    ncol = CHIPS * nin
    a_cols = 3 * A_W
    offs = [0, a_cols, 2 * a_cols, 3 * a_cols, 3 * a_cols + B_QH * HEAD,
            3 * a_cols + (B_QH + B_KVH) * HEAD, 3 * a_cols + (B_QH + 2 * B_KVH) * HEAD, ncol]

    def body(x_ref, g_ref, wi_ref, h_ref, a0, a1, a2, qb, kb, vb, mq, p_scr):
        xh, _ = _rms(x_ref[...])
        h = (xh * g_ref[...]).astype(_MM)
        h_ref[...] = h
        for j in range(CHIPS):
            p_scr[:, j * nin:(j + 1) * nin] = jnp.dot(h, wi_ref[j], preferred_element_type=F32)
        for k, ref in enumerate((a0, a1, a2, qb, kb, vb, mq)):
            ref[...] = p_scr[:, offs[k]:offs[k + 1]]

    widths = [offs[k + 1] - offs[k] for k in range(7)]
    return _pc(
        body, name="in_proj", grid=(s // ts,),
        in_specs=[_row(ts, D_MODEL), _res((1, D_MODEL)), _res(w_in.shape)],
        out_specs=[_row(ts, D_MODEL)] + [_row(ts, w) for w in widths],
        out_shape=[_sds((s, D_MODEL), _MM)] + [_sds((s, w)) for w in widths],
        scratch=[pltpu.VMEM((ts, ncol), F32)])(x, g1, w_in)


def _k_gate(h, w_gate, b_gate):
    s = h.shape[0]
    ts = min(256, s)
    ng = w_gate.shape[2]

    def body(h_ref, wg_ref, bg_ref, gt_ref):
        h = h_ref[...]
        for j in range(CHIPS):
            z = jnp.dot(h, wg_ref[j], preferred_element_type=F32) + bg_ref[:, j * ng:(j + 1) * ng]
            gt_ref[:, j * ng:(j + 1) * ng] = _sigmoid(z)

    return _pc(body, name="gate_proj", grid=(s // ts,),
               in_specs=[_row(ts, D_MODEL), _res(w_gate.shape), _res(b_gate.shape)],
               out_specs=[_row(ts, CHIPS * ng)], out_shape=[_sds((s, CHIPS * ng))])(h, w_gate, b_gate)[0]


def _k_prep(srcs, gq, gk, tabs, tab_row, *, wq, wk, rows_per_gain, name):
    rows = srcs[0][0].shape[0]
    ts = min(512, rows)

    def body(q_ref, k_ref, v_ref, gq_ref, gk_ref, c_ref, sa_ref, sb_ref, qn_ref, kn_ref, vn_ref):
        c, sa, sb = c_ref[...], sa_ref[...], sb_ref[...]
        qh, _ = _seg_norm(q_ref[...], HEAD)
        qn_ref[...] = _rope(qh * gq_ref[...], c, sa, sb).astype(_MM)
        kh, _ = _seg_norm(k_ref[...], HEAD)
        kn_ref[...] = _rope(kh * gk_ref[...], c, sa, sb).astype(_MM)
        vn_ref[...] = v_ref[...].astype(_MM)

    gspec = lambda w: pl.BlockSpec((None, 1, w), lambda i: ((i * ts) // rows_per_gain, 0, 0))
    return _pc(
        body, name=name, grid=(rows // ts,),
        in_specs=[_row(ts, wq, srcs[0][1]), _row(ts, wk, srcs[1][1]), _row(ts, wk, srcs[2][1]),
                  gspec(wq), gspec(wk)] + [pl.BlockSpec((ts, 128), lambda i: (i + tab_row // ts, 0))] * 3,
        out_specs=[_row(ts, wq), _row(ts, wk), _row(ts, wk)],
        out_shape=[_sds((rows, wq), _MM), _sds((rows, wk), _MM), _sds((rows, wk), _MM)])(
            srcs[0][0], srcs[1][0], srcs[2][0], gq, gk, *tabs)


def _first_flag(b, segs, nb):
    first = b >= nb
    for k, (start, period) in enumerate(segs):
        end = segs[k + 1][0] if k + 1 < len(segs) else nb
        first = first | ((b >= start) & (b < end) & (lax.rem(b - start, jnp.int32(period)) == 0))
    return first


def _band_bias(thr, with_cur):
    qi = lax.broadcasted_iota(jnp.int32, (BLK, BLK), 0)
    kj = lax.broadcasted_iota(jnp.int32, (BLK, BLK), 1)
    prev = jnp.where(kj >= qi + thr, 0.0, NEG)
    return jnp.concatenate([prev, jnp.where(kj <= qi, 0.0, NEG)], axis=1) if with_cur else prev


def _blockdiag(t4):
    head = _lane_head((1, A_W))
    return jnp.concatenate([t4 * jnp.where(head == h, 1.0, 0.0).astype(t4.dtype) for h in range(A_HEADS)], axis=0)


def _fold_diag(t, n):
    head = _lane_head((n, A_W))
    out = t[3 * n:4 * n]
    for h in (2, 1, 0):
        out = jnp.where(head == h, t[h * n:(h + 1) * n], out)
    return out


def _expand_heads(cols):
    n = cols[0].shape[0]
    head = _lane_head((n, A_W))
    out = jnp.broadcast_to(cols[3], (n, A_W))
    for h in (2, 1, 0):
        out = jnp.where(head == h, cols[h], out)
    return out


def _unit_kv(pieces, u, shared):
    cols = slice(u * HEAD, (u + 1) * HEAD) if shared else slice(u * A_W, (u + 1) * A_W)
    rows = [ref[rs, cols] for ref, rs in pieces]
    k = rows[0] if len(rows) == 1 else jnp.concatenate(rows, axis=0)
    return jnp.concatenate([k] * A_HEADS, axis=1) if shared else k


_LO, _HI, _BOTH = slice(0, BLK), slice(BLK, 2 * BLK), slice(0, 2 * BLK)


def _k_band_fwd(qn, kn, vn, *, hq, hk, max_dist, segs, sink, name):
    rows = qn.shape[0]
    nb = rows // BLK
    units = hq // A_HEADS
    shared = hk != hq
    wq, wk = hq * HEAD, hk * HEAD
    scale = HEAD ** -0.5

    def body(*refs):
        if sink is None:
            q_ref, kc_ref, kp_ref, vc_ref, vp_ref, o_ref, l_ref = refs
        else:
            q_ref, kc_ref, kp_ref, vc_ref, vp_ref, sk_ref, o_ref, l_ref = refs
        i = pl.program_id(0)
        for half, rs in enumerate((_LO, _HI)):
            bias = _band_bias(jnp.where(_first_flag(2 * i + half, segs, nb), 1 << 20, BLK - max_dist), True)
            kpieces = ((kp_ref, _LO), (kc_ref, _LO)) if half == 0 else ((kc_ref, _BOTH),)
            vpieces = ((vp_ref, _LO), (vc_ref, _LO)) if half == 0 else ((vc_ref, _BOTH),)
            for u in range(units):
                us = slice(u * A_W, (u + 1) * A_W)
                kb = _blockdiag(_unit_kv(kpieces, u, shared))
                vb = _blockdiag(_unit_kv(vpieces, u, shared))
                s_all = _dot_nt(q_ref[rs, us], kb) * scale
                ps, ls = [], []
                for h in range(A_HEADS):
                    s = s_all[:, h * 2 * BLK:(h + 1) * 2 * BLK] + bias
                    m = jnp.max(s, axis=-1, keepdims=True)
                    e = jnp.exp(s - m)
                    lse = m + jnp.log(jnp.sum(e, axis=-1, keepdims=True))
                    if sink is not None:
                        sk = sk_ref[u * A_HEADS + h]
                        mx = jnp.maximum(lse, sk)
                        lse = mx + jnp.log(jnp.exp(lse - mx) + jnp.exp(sk - mx))
                    ps.append((e * jnp.exp(m - lse)).astype(_MM))
                    ls.append(lse)
                o_ref[rs, us] = _dot(jnp.concatenate(ps, axis=1), vb)
                l_ref[rs, us] = _expand_heads(ls)

    two = lambda w: pl.BlockSpec((2 * BLK, w), lambda i: (i, 0))
    prev = lambda w: pl.BlockSpec((BLK, w), lambda i: (jnp.maximum(2 * i - 1, 0), 0))
    in_specs = [two(wq), two(wk), prev(wk), two(wk), prev(wk)]
    args = [qn, kn, kn, vn, vn]
    if sink is not None:
        in_specs.append(pl.BlockSpec(memory_space=pltpu.SMEM))
        args.append(sink)
    return _pc(body, name=name, grid=(nb // 2,), in_specs=in_specs, out_specs=[two(wq), two(wq)],
               out_shape=[_sds((rows, wq)), _sds((rows, wq))])(*args)


def _k_memkv(mem, mem_norm, w_kv, m_k_norm):
    n = mem.shape[0]

    def body(m_ref, g_ref, w_ref, gk_ref, mn_ref, kv_ref, mk_ref, mv_ref):
        mh, _ = _rms(m_ref[...])
        mn = (mh * g_ref[...]).astype(_MM)
        mn_ref[...] = mn
        kv = jnp.dot(mn, w_ref[...], preferred_element_type=F32)
        kv_ref[...] = kv
        kh, _ = _seg_norm(kv[:, :M_W], M_HD)
        mk_ref[...] = (kh * gk_ref[...]).astype(_MM)
        mv_ref[...] = kv[:, M_W:].astype(_MM)

    return _pc(body, name="mem_kv", grid=(1,),
               in_specs=[_acc((n, D_MODEL)), _acc((1, D_MODEL)), _acc(w_kv.shape), _acc((1, M_W))],
               out_specs=[_acc((n, D_MODEL)), _acc((n, 2 * M_W)), _acc((n, M_W)), _acc((n, M_W))],
               out_shape=[_sds((n, D_MODEL), _MM), _sds((n, 2 * M_W)), _sds((n, M_W), _MM), _sds((n, M_W), _MM)])(
                   mem, mem_norm, w_kv, m_k_norm)


def _mem_probs(q, mk):
    sc = _dot_nt(q, mk) * (M_HD ** -0.5)
    e = jnp.exp(sc - jnp.max(sc, axis=-1, keepdims=True))
    return e / jnp.sum(e, axis=-1, keepdims=True)


def _k_mem_fwd(m_q, gq, mk, mv):
    s = m_q.shape[0]
    n = mk.shape[0]
    ts = min(512, s)

    def body(q_ref, g_ref, mk_ref, mv_ref, o_ref):
        qh, _ = _seg_norm(q_ref[...], M_HD)
        qn = (qh * g_ref[...]).astype(_MM)
        for h in range(M_HEADS):
            hs = slice(h * M_HD, (h + 1) * M_HD)
            o_ref[:, hs] = _dot(_mem_probs(qn[:, hs], mk_ref[:, hs]), mv_ref[:, hs])

    return _pc(body, name="mem_attn", grid=(s // ts,),
               in_specs=[_row(ts, M_W), _res((1, M_W)), _res((n, M_W)), _res((n, M_W))],
               out_specs=[_row(ts, M_W)], out_shape=[_sds((s, M_W))])(m_q, gq, mk, mv)[0]


def _group_weights(l0, l1, l2):
    m = jnp.maximum(jnp.maximum(l0, l1), l2)
    e0, e1, e2 = jnp.exp(l0 - m), jnp.exp(l1 - m), jnp.exp(l2 - m)
    inv = 1.0 / (e0 + e1 + e2)
    return e0 * inv, e1 * inv, e2 * inv


def _branch_products(oa, ob, om, woa_ref, wob_ref, wom_ref, j):
    return _dot(oa, woa_ref[j]), _dot(ob, wob_ref[j]), _dot(om, wom_ref[j])


def _k_merge(og, lg, o_b, o_m, gates, x, w_oa, w_ob, w_om, w_out, g2):
    s = x.shape[0]
    ts = min(256, s)
    nc = w_oa.shape[2]

    def body(o0, o1, o2, l0, l1, l2, ob_ref, om_ref, gt_ref, x_ref, woa, wob, wom, wout, g_ref,
             oa_ref, mer_ref, x1_ref, h2_ref, m_scr):
        w0, w1, w2 = _group_weights(l0[...], l1[...], l2[...])
        oa = w0 * o0[...] + w1 * o1[...] + w2 * o2[...]
        oa_ref[...] = oa
        ob, om = ob_ref[...], om_ref[...]
        for j in range(CHIPS):
            pa, pb, pm = _branch_products(oa, ob, om, woa, wob, wom, j)
            cs = lambda br: slice(br * D_MODEL + j * nc, br * D_MODEL + (j + 1) * nc)
            m_scr[:, j * nc:(j + 1) * nc] = gt_ref[:, cs(0)] * pa + gt_ref[:, cs(1)] * pb + gt_ref[:, cs(2)] * pm
        mer = m_scr[...].astype(_MM)
        mer_ref[...] = mer
        x1 = x_ref[...] + jnp.dot(mer, wout[...], preferred_element_type=F32)
        x1_ref[...] = x1
        xh, _ = _rms(x1)
        h2_ref[...] = (xh * g_ref[...]).astype(_MM)

    return _pc(
        body, name="merge_out", grid=(s // ts,),
        in_specs=[_row(ts, A_W)] * 6 + [_row(ts, B_QH * HEAD), _row(ts, M_W), _row(ts, 3 * D_MODEL), _row(ts, D_MODEL),
                                         _res(w_oa.shape), _res(w_ob.shape), _res(w_om.shape), _res(w_out.shape),
                                         _res((1, D_MODEL))],
        out_specs=[_row(ts, A_W), _row(ts, D_MODEL), _row(ts, D_MODEL), _row(ts, D_MODEL)],
        out_shape=[_sds((s, A_W)), _sds((s, D_MODEL), _MM), _sds((s, D_MODEL)), _sds((s, D_MODEL), _MM)],
        scratch=[pltpu.VMEM((ts, D_MODEL), F32)])(*og, *lg, o_b, o_m, gates, x, w_oa, w_ob, w_om, w_out, g2)


def _k_up(h2, w_up):
    s = h2.shape[0]
    ts = min(256, s)
    nu = w_up.shape[2]

    def body(h_ref, w_ref, u_ref):
        h = h_ref[...]
        for j in range(CHIPS):
            u_ref[:, j * nu:(j + 1) * nu] = jnp.dot(h, w_ref[j], preferred_element_type=F32)

    return _pc(body, name="up_proj", grid=(s // ts,), in_specs=[_row(ts, D_MODEL), _res(w_up.shape)],
               out_specs=[_row(ts, CHIPS * nu)], out_shape=[_sds((s, CHIPS * nu))])(h2, w_up)[0]


def _shift_down(v, halo, k):
    rolled = pltpu.roll(v, k, 0)
    row = lax.broadcasted_iota(jnp.int32, (8, v.shape[1]), 0)
    slab = rolled[0:8]
    for r in range(k):
        slab = jnp.where(row == r, halo[8 - k + r:8 - k + r + 1, :], slab)
    return jnp.concatenate([slab, rolled[8:]], axis=0)


def _shift_up(v, halo, k):
    ts = v.shape[0]
    rolled = pltpu.roll(v, ts - k, 0)
    row = lax.broadcasted_iota(jnp.int32, (8, v.shape[1]), 0)
    slab = rolled[ts - 8:]
    for r in range(k):
        slab = jnp.where(row == 8 - k + r, halo[r:r + 1, :], slab)
    return jnp.concatenate([rolled[:ts - 8], slab], axis=0)


def _k_ffn(u, conv_w, conv_b, w_down, w_down_t, x1, target):
    s = u.shape[0]
    ts = min(256, s)
    nu = conv_w.shape[2]
    half = CHIPS // 2

    def body(u_ref, uh_ref, cw_ref, cb_ref, wd_ref, wdt_ref, x1_ref, t_ref, dy_ref, f_ref, dc_ref, loss_ref, c_scr,
             f_scr, s_scr):
        i = pl.program_id(0)
        halo = jnp.where(i > 0, uh_ref[...], 0.0)
        for j in range(CHIPS):
            cs = slice(j * nu, (j + 1) * nu)
            uj = u_ref[:, cs]
            hj = halo[:, cs]
            c_scr[:, cs] = (cb_ref[:, cs] + cw_ref[j, 0:1, :] * _shift_down(uj, hj, 2)
                            + cw_ref[j, 1:2, :] * _shift_down(uj, hj, 1) + cw_ref[j, 2:3, :] * uj)
        for j in range(half):
            a = c_scr[:, j * nu:(j + 1) * nu]
            g = c_scr[:, (half + j) * nu:(half + j + 1) * nu]
            sa = _sigmoid(a)
            s_scr[:, j * nu:(j + 1) * nu] = sa
            f_scr[:, j * nu:(j + 1) * nu] = (a * sa * g).astype(_MM)
        f = f_scr[...]
        f_ref[...] = f
        y = x1_ref[...] + jnp.dot(f, wd_ref[...], preferred_element_type=F32)
        err = y - t_ref[...]
        dy = err * (1.0 / D_MODEL)
        dy_ref[...] = dy

        @pl.when(i == 0)
        def _():
            loss_ref[...] = jnp.zeros_like(loss_ref)

        loss_ref[...] += _sum8(err * err)
        df = _dot(dy, wdt_ref[...])
        for j in range(half):
            a = c_scr[:, j * nu:(j + 1) * nu]
            g = c_scr[:, (half + j) * nu:(half + j + 1) * nu]
            sa = s_scr[:, j * nu:(j + 1) * nu]
            dfj = df[:, j * nu:(j + 1) * nu]
            dc_ref[:, j * nu:(j + 1) * nu] = dfj * g * (sa * (1.0 + a * (1.0 - sa)))
            dc_ref[:, (half + j) * nu:(half + j + 1) * nu] = dfj * (a * sa)

    wide = CHIPS * nu
    return _pc(
        body, name="conv_ffn", grid=(s // ts,),
        in_specs=[_row(ts, wide), pl.BlockSpec((8, wide), lambda i: (jnp.maximum(i * (ts // 8) - 1, 0), 0)),
                  _res(conv_w.shape), _res((1, wide)), _res(w_down.shape), _res(w_down_t.shape), _row(ts, D_MODEL),
                  _row(ts, D_MODEL)],
        out_specs=[_row(ts, D_MODEL), _row(ts, D_FF), _row(ts, wide), _acc((8, D_MODEL))],
        out_shape=[_sds((s, D_MODEL)), _sds((s, D_FF), _MM), _sds((s, wide)), _sds((8, D_MODEL))],
        scratch=[pltpu.VMEM((ts, wide), F32), pltpu.VMEM((ts, D_FF), _MM), pltpu.VMEM((ts, D_FF), F32)])(
            u, u, conv_w, conv_b, w_down, w_down_t, x1, target)


def _k_conv_bwd(dc, u, conv_w, w_up, x1, g2, dy):
    s = u.shape[0]
    ts = min(256, s)
    nu = conv_w.shape[2]
    wide = CHIPS * nu
    last = s // ts - 1

    def body(dc_ref, dn_ref, u_ref, cw_ref, wu_ref, x1_ref, g_ref, dy_ref, dx1_ref, du_ref, cacc_ref, gacc_ref):
        i = pl.program_id(0)

        @pl.when(i == 0)
        def _():
            cacc_ref[...] = jnp.zeros_like(cacc_ref)
            gacc_ref[...] = jnp.zeros_like(gacc_ref)

        dhalo = jnp.where(i < last, dn_ref[...], 0.0)
        dh2 = jnp.zeros((ts, D_MODEL), F32)
        for j in range(CHIPS):
            cs = slice(j * nu, (j + 1) * nu)
            dcj, uj = dc_ref[:, cs], u_ref[:, cs]
            dc1, dc2 = _shift_up(dcj, dhalo[:, cs], 1), _shift_up(dcj, dhalo[:, cs], 2)
            cacc_ref[0, :, cs] += _sum8(dcj)
            cacc_ref[1, :, cs] += _sum8(dc2 * uj)
            cacc_ref[2, :, cs] += _sum8(dc1 * uj)
            cacc_ref[3, :, cs] += _sum8(dcj * uj)
            du = (cw_ref[j, 2:3, :] * dcj + cw_ref[j, 1:2, :] * dc1 + cw_ref[j, 0:1, :] * dc2).astype(_MM)
            du_ref[:, cs] = du
            dh2 = dh2 + _dot_nt(du, wu_ref[j])
        xh, r = _rms(x1_ref[...])
        gacc_ref[...] += _sum8(dh2 * xh)
        dx1_ref[...] = dy_ref[...] + _rms_bwd(dh2, xh, r, g_ref[...])

    return _pc(
        body, name="conv_up_bwd", grid=(s // ts,),
        in_specs=[_row(ts, wide),
                  pl.BlockSpec((8, wide), lambda i: (jnp.minimum((i + 1) * (ts // 8), s // 8 - 1), 0)),
                  _row(ts, wide), _res(conv_w.shape), _res(w_up.shape), _row(ts, D_MODEL), _res((1, D_MODEL)),
                  _row(ts, D_MODEL)],
        out_specs=[_row(ts, D_MODEL), _row(ts, wide), _acc((4, 8, wide)), _acc((8, D_MODEL))],
        out_shape=[_sds((s, D_MODEL)), _sds((s, wide), _MM), _sds((4, 8, wide)), _sds((8, D_MODEL))])(
            dc, dc, u, conv_w, w_up, x1, g2, dy)


def _k_merge_bwd(dx1, og, lg, o_a, o_b, o_m, gates, merged, w_oa, w_ob, w_om, w_out, dep):
    s = dx1.shape[0]
    ts = min(256, s)
    nc = w_oa.shape[2]

    def body(dx_ref, o0, o1, o2, l0, l1, l2, oa_ref, ob_ref, om_ref, gt_ref, mer_ref, woa, wob, wom, wout, dep_ref,
             dgp_ref, dog0, dog1, dog2, dl0, dl1, dl2, dob_ref, dom_ref, bacc_ref, dwa_ref, dwb_ref, dwm_ref, dwo_ref):
        i = pl.program_id(0)

        @pl.when(i == 0)
        def _():
            for ref in (bacc_ref, dwa_ref, dwb_ref, dwm_ref, dwo_ref):
                ref[...] = jnp.zeros_like(ref)

        dx = dx_ref[...]
        dwo_ref[...] += _dot_tn(mer_ref[...], dx)
        dmer = _dot_nt(dx, wout[...])
        oa, ob, om = oa_ref[...], ob_ref[...], om_ref[...]
        doa = jnp.zeros((ts, A_W), F32)
        dob = jnp.zeros((ts, B_QH * HEAD), F32)
        dom = jnp.zeros((ts, M_W), F32)
        for j in range(CHIPS):
            prods = _branch_products(oa, ob, om, woa, wob, wom, j)
            dmj = dmer[:, j * nc:(j + 1) * nc]
            dps = []
            for br, (p, o, dw_ref) in enumerate(zip(prods, (oa, ob, om), (dwa_ref, dwb_ref, dwm_ref))):
                cs = slice(br * D_MODEL + j * nc, br * D_MODEL + (j + 1) * nc)
                gt = gt_ref[:, cs]
                dgp = dmj * p * gt * (1.0 - gt)
                dgp_ref[:, cs] = dgp.astype(_MM)
                bacc_ref[:, cs] += _sum8(dgp)
                dp = (dmj * gt).astype(_MM)
                dw_ref[j] += _dot_tn(o, dp)
                dps.append(dp)
            doa = doa + _dot_nt(dps[0], woa[j])
            dob = dob + _dot_nt(dps[1], wob[j])
            dom = dom + _dot_nt(dps[2], wom[j])
        dob_ref[...] = dob
        dom_ref[...] = dom
        ws = _group_weights(l0[...], l1[...], l2[...])
        dsum = _seg_mean(doa * oa, HEAD) * float(HEAD)
        for w, dref, lref in zip(ws, (dog0, dog1, dog2), (dl0, dl1, dl2)):
            dref[...] = w * doa
            lref[...] = w * dsum

    return _pc(
        body, name="merge_out_bwd", grid=(s // ts,),
        in_specs=[_row(ts, D_MODEL)] + [_row(ts, A_W)] * 7 + [_row(ts, B_QH * HEAD), _row(ts, M_W), _row(ts, 3 * D_MODEL),
                                                              _row(ts, D_MODEL), _res(w_oa.shape), _res(w_ob.shape),
                                                              _res(w_om.shape), _res(w_out.shape), _res((8, 128))],
        out_specs=[_row(ts, 3 * D_MODEL)] + [_row(ts, A_W)] * 6
        + [_row(ts, B_QH * HEAD), _row(ts, M_W), _acc((8, 3 * D_MODEL)), _acc(w_oa.shape), _acc(w_ob.shape),
           _acc(w_om.shape), _acc(w_out.shape)],
        out_shape=[_sds((s, 3 * D_MODEL), _MM)] + [_sds((s, A_W))] * 6
        + [_sds((s, B_QH * HEAD)), _sds((s, M_W)), _sds((8, 3 * D_MODEL)), _sds(w_oa.shape), _sds(w_ob.shape),
           _sds(w_om.shape), _sds(w_out.shape)])(
            dx1, *og, *lg, o_a, o_b, o_m, gates, merged, w_oa, w_ob, w_om, w_out, dep)


def _k_mem_bwd(m_q, gq, mk, mv, o_m, do_m):
    s = m_q.shape[0]
    n = mk.shape[0]
    ts = min(512, s)
    scale = M_HD ** -0.5

    def body(q_ref, g_ref, mk_ref, mv_ref, o_ref, do_ref, dq_ref, dmk_ref, dmv_ref, gacc_ref):
        i = pl.program_id(0)

        @pl.when(i == 0)
        def _():
            dmk_ref[...] = jnp.zeros_like(dmk_ref)
            dmv_ref[...] = jnp.zeros_like(dmv_ref)
            gacc_ref[...] = jnp.zeros_like(gacc_ref)

        gain = g_ref[...]
        qh, r = _seg_norm(q_ref[...], M_HD)
        qn = (qh * gain).astype(_MM)
        do = do_ref[...]
        delta = _seg_mean(do * o_ref[...], M_HD) * float(M_HD)
        dqn = []
        for h in range(M_HEADS):
            hs = slice(h * M_HD, (h + 1) * M_HD)
            p = _mem_probs(qn[:, hs], mk_ref[:, hs])
            dp = _dot_nt(do[:, hs], mv_ref[:, hs])
            ds = (p * (dp - delta[:, hs][:, 0:1]) * scale).astype(_MM)
            dqn.append(_dot(ds, mk_ref[:, hs]))
            dmk_ref[:, hs] += _dot_tn(ds, qn[:, hs])
            dmv_ref[:, hs] += _dot_tn(p, do[:, hs])
        dqn = jnp.concatenate(dqn, axis=1)
        gacc_ref[...] += _sum8(dqn * qh)
        z = dqn * gain
        dq_ref[...] = (r * (z - qh * _seg_mean(z * qh, M_HD))).astype(_MM)

    return _pc(
        body, name="mem_attn_bwd", grid=(s // ts,),
        in_specs=[_row(ts, M_W), _res((1, M_W)), _res((n, M_W)), _res((n, M_W)), _row(ts, M_W), _row(ts, M_W)],
        out_specs=[_row(ts, M_W), _acc((n, M_W)), _acc((n, M_W)), _acc((8, M_W))],
        out_shape=[_sds((s, M_W), _MM), _sds((n, M_W)), _sds((n, M_W)), _sds((8, M_W))])(m_q, gq, mk, mv, o_m, do_m)


def _k_memkv_bwd(mem, mem_norm, w_kv, m_k_norm, mem_n, kv, dmk, dmv):
    n = mem.shape[0]

    def body(m_ref, g_ref, w_ref, gk_ref, mn_ref, kv_ref, dmk_ref, dmv_ref, dw_ref, dg_ref, dgk_ref):
        gk = gk_ref[...]
        kh, r = _seg_norm(kv_ref[:, :M_W], M_HD)
        dmk = dmk_ref[...]
        dgk_ref[...] = _sum8(dmk * kh)
        z = dmk * gk
        dk = r * (z - kh * _seg_mean(z * kh, M_HD))
        dkv = jnp.concatenate([dk, dmv_ref[...]], axis=1).astype(_MM)
        dw_ref[...] = _dot_tn(mn_ref[...], dkv)
        dmn = _dot_nt(dkv, w_ref[...])
        mh, _ = _rms(m_ref[...])
        dg_ref[...] = _sum8(dmn * mh)

    return _pc(body, name="mem_kv_bwd", grid=(1,),
               in_specs=[_acc((n, D_MODEL)), _acc((1, D_MODEL)), _acc(w_kv.shape), _acc((1, M_W)), _acc((n, D_MODEL)),
                         _acc((n, 2 * M_W)), _acc((n, M_W)), _acc((n, M_W))],
               out_specs=[_acc(w_kv.shape), _acc((8, D_MODEL)), _acc((8, M_W))],
               out_shape=[_sds(w_kv.shape), _sds((8, D_MODEL)), _sds((8, M_W))])(
                   mem, mem_norm, w_kv, m_k_norm, mem_n, kv, dmk, dmv)


def _k_band_bwd(qn, kn, vn, do, lse, dl_or_o, *, hq, hk, max_dist, segs, sink, name):
    rows = qn.shape[0]
    nb = rows // BLK
    units = hq // A_HEADS
    shared = hk != hq
    wq, wk = hq * HEAD, hk * HEAD
    scale = HEAD ** -0.5

    def body(*refs):
        (q2_ref, qx_ref, kc_ref, kp_ref, vc_ref, vp_ref, do2_ref, dox_ref, l2_ref, lx_ref, e2_ref, ex_ref) = refs[:12]
        if sink is None:
            dq_ref, dk_ref, dv_ref = refs[12:]
        else:
            sk_ref, dq_ref, dk_ref, dv_ref, sacc_ref = refs[12:]
        i = pl.program_id(0)
        thr = lambda b: jnp.where(_first_flag(b, segs, nb), 1 << 20, BLK - max_dist)
        bias_a, bias_b = _band_bias(thr(2 * i), True), _band_bias(thr(2 * i + 1), True)
        bias_c = _band_bias(thr(2 * i + 2), False)
        if sink is not None:
            @pl.when(i == 0)
            def _():
                sacc_ref[...] = jnp.zeros_like(sacc_ref)

        def tile(q4, do4, l_cols, dlt, kd, vd, bias, width):
            s, dp = _dot_nt(q4, kd) * scale, _dot_nt(do4, vd)
            ps, dss = [], []
            for h in range(A_HEADS):
                seg = slice(h * width, (h + 1) * width)
                p = jnp.exp(s[:, seg] + bias - l_cols[h])
                ps.append(p)
                dss.append(p * (dp[:, seg] - dlt[:, h * HEAD:h * HEAD + 1]) * scale)
            return ps, dss

        cat = lambda parts: jnp.concatenate([t.astype(_MM) for t in parts], axis=1)
        for u in range(units):
            us = slice(u * A_W, (u + 1) * A_W)
            k_a = _unit_kv(((kp_ref, _LO), (kc_ref, _LO)), u, shared)
            v_a = _unit_kv(((vp_ref, _LO), (vc_ref, _LO)), u, shared)
            k_b, v_b = _unit_kv(((kc_ref, _BOTH),), u, shared), _unit_kv(((vc_ref, _BOTH),), u, shared)
            kd_a, vd_a, kd_b, vd_b = _blockdiag(k_a), _blockdiag(v_a), _blockdiag(k_b), _blockdiag(v_b)
            kd_c, vd_c = _blockdiag(k_b[BLK:]), _blockdiag(v_b[BLK:])
            qs = (q2_ref[_LO, us], q2_ref[_HI, us], qx_ref[:, us])
            dos = (do2_ref[_LO, us], do2_ref[_HI, us], dox_ref[:, us])
            lcols = [[ref[rs, u * A_W + h * HEAD:u * A_W + h * HEAD + 1] for h in range(A_HEADS)]
                     for ref, rs in ((l2_ref, _LO), (l2_ref, _HI), (lx_ref, _LO))]
            if sink is None:
                dlts = (e2_ref[_LO, us], e2_ref[_HI, us], ex_ref[:, us])
            else:
                dlts = tuple(_seg_sum64(d.astype(F32) * ref[rs, us])
                             for d, (ref, rs) in zip(dos, ((e2_ref, _LO), (e2_ref, _HI), (ex_ref, _LO))))
                for t in range(2):
                    for h in range(A_HEADS):
                        j = u * A_HEADS + h
                        sacc_ref[:, j:j + 1] += -jnp.exp(sk_ref[j] - lcols[t][h]) * dlts[t][:, h * HEAD:h * HEAD + 1]
            p_a, ds_a = tile(qs[0], dos[0], lcols[0], dlts[0], kd_a, vd_a, bias_a, 2 * BLK)
            p_b, ds_b = tile(qs[1], dos[1], lcols[1], dlts[1], kd_b, vd_b, bias_b, 2 * BLK)
            p_c, ds_c = tile(qs[2], dos[2], lcols[2], dlts[2], kd_c, vd_c, bias_c, BLK)
            dq_ref[_LO, us] = _dot(cat(ds_a), kd_a)
            dq_ref[_HI, us] = _dot(cat(ds_b), kd_b)
            outs = []
            for pa, pb, pc, lhs in ((ds_a, ds_b, ds_c, qs), (p_a, p_b, p_c, dos)):
                from_a = _fold_diag(_dot_tn(cat([t[:, BLK:] for t in pa]), lhs[0]), BLK)
                from_b = _fold_diag(_dot_tn(cat(pb), lhs[1]), 2 * BLK)
                from_c = _fold_diag(_dot_tn(cat(pc), lhs[2]), BLK)
                outs.append(jnp.concatenate([from_a + from_b[:BLK], from_b[BLK:] + from_c], axis=0))
            dk4, dv4 = outs
            if shared:
                fold = lambda t: (t[:, 0:HEAD] + t[:, HEAD:2 * HEAD]) + (t[:, 2 * HEAD:3 * HEAD] + t[:, 3 * HEAD:])
                dk_ref[:, u * HEAD:(u + 1) * HEAD] = fold(dk4)
                dv_ref[:, u * HEAD:(u + 1) * HEAD] = fold(dv4).astype(_MM)
            else:
                dk_ref[:, us] = dk4
                dv_ref[:, us] = dv4.astype(_MM)

    two = lambda w: pl.BlockSpec((2 * BLK, w), lambda i: (i, 0))
    prev = lambda w: pl.BlockSpec((BLK, w), lambda i: (jnp.maximum(2 * i - 1, 0), 0))
    nxt = lambda w: pl.BlockSpec((BLK, w), lambda i: (jnp.minimum(2 * i + 2, nb - 1), 0))
    in_specs = [two(wq), nxt(wq), two(wk), prev(wk), two(wk), prev(wk), two(wq), nxt(wq), two(wq), nxt(wq), two(wq), nxt(wq)]
    args = [qn, qn, kn, kn, vn, vn, do, do, lse, lse, dl_or_o, dl_or_o]
    out_specs = [two(wq), two(wk), two(wk)]
    out_shape = [_sds((rows, wq)), _sds((rows, wk)), _sds((rows, wk), _MM)]
    if sink is not None:
        in_specs.append(pl.BlockSpec(memory_space=pltpu.SMEM))
        args.append(sink)
        out_specs.append(_acc((BLK, 128)))
        out_shape.append(_sds((BLK, 128)))
    return _pc(body, name=name, grid=(nb // 2,), in_specs=in_specs, out_specs=out_specs, out_shape=out_shape)(*args)


def _k_prep_bwd(srcs, dqn, dkn, gq, gk, tabs, tab_row, *, wq, wk, rows_per_gain, name):
    rows = dqn.shape[0]
    ts = min(512, rows)
    ngain = gq.shape[0]

    def body(q_ref, k_ref, dq_ref, dk_ref, gq_ref, gk_ref, c_ref, sa_ref, sb_ref, oq_ref, ok_ref, aq_ref, ak_ref):
        i = pl.program_id(0)

        @pl.when(lax.rem(i * ts, rows_per_gain) == 0)
        def _():
            aq_ref[...] = jnp.zeros_like(aq_ref)
            ak_ref[...] = jnp.zeros_like(ak_ref)

        c, sa, sb = c_ref[...], sa_ref[...], sb_ref[...]
        for x_ref, d_ref, g_ref, o_ref, a_ref in ((q_ref, dq_ref, gq_ref, oq_ref, aq_ref),
                                                   (k_ref, dk_ref, gk_ref, ok_ref, ak_ref)):
            xh, r = _seg_norm(x_ref[...], HEAD)
            dt = _rope_bwd(d_ref[...], c, sa, sb)
            a_ref[...] += _sum8(dt * xh)
            z = dt * g_ref[...]
            o_ref[...] = (r * (z - xh * _seg_mean(z * xh, HEAD))).astype(_MM)

    gspec = lambda w: pl.BlockSpec((None, 1, w), lambda i: ((i * ts) // rows_per_gain, 0, 0))
    aspec = lambda w: pl.BlockSpec((None, 8, w), lambda i: ((i * ts) // rows_per_gain, 0, 0))
    return _pc(
        body, name=name, grid=(rows // ts,),
        in_specs=[_row(ts, wq, srcs[0][1]), _row(ts, wk, srcs[1][1]), _row(ts, wq), _row(ts, wk), gspec(wq), gspec(wk)]
        + [pl.BlockSpec((ts, 128), lambda i: (i + tab_row // ts, 0))] * 3,
        out_specs=[_row(ts, wq), _row(ts, wk), aspec(wq), aspec(wk)],
        out_shape=[_sds((rows, wq), _MM), _sds((rows, wk), _MM), _sds((ngain, 8, wq)), _sds((ngain, 8, wk))])(
            srcs[0][0], srcs[1][0], dqn, dkn, gq, gk, *tabs)


def _k_in_bwd(pieces, dgp, x, g1, dx1, w_in, w_gate):
    s = x.shape[0]
    ts = min(256, s)
    nin, ng = w_in.shape[2], w_gate.shape[2]
    widths = [p.shape[1] for p in pieces]
    ncol = sum(widths)

    def body(*refs):
        p_refs = refs[:len(pieces)]
        dgp_ref, x_ref, g_ref, dx1_ref, wi_ref, wg_ref, gx_ref, dpj_ref, gacc_ref = refs[len(pieces):]
        i = pl.program_id(0)

        @pl.when(i == 0)
        def _():
            gacc_ref[...] = jnp.zeros_like(gacc_ref)

        off = 0
        for p_ref, w in zip(p_refs, widths):
            dpj_ref[:, off:off + w] = p_ref[...]
            off += w
        dh = jnp.zeros((ts, D_MODEL), F32)
        for j in range(CHIPS):
            dh = dh + _dot_nt(dpj_ref[:, j * nin:(j + 1) * nin], wi_ref[j])
            dh = dh + _dot_nt(dgp_ref[:, j * ng:(j + 1) * ng], wg_ref[j])
        xh, r = _rms(x_ref[...])
        gacc_ref[...] += _sum8(dh * xh)
        gx_ref[...] = dx1_ref[...] + _rms_bwd(dh, xh, r, g_ref[...])

    return _pc(
        body, name="in_proj_bwd", grid=(s // ts,),
        in_specs=[_row(ts, w) for w in widths] + [_row(ts, CHIPS * ng), _row(ts, D_MODEL), _res((1, D_MODEL)),
                                                  _row(ts, D_MODEL), _res(w_in.shape), _res(w_gate.shape)],
        out_specs=[_row(ts, D_MODEL), _row(ts, ncol), _acc((8, D_MODEL))],
        out_shape=[_sds((s, D_MODEL)), _sds((s, ncol), _MM), _sds((8, D_MODEL))])(*pieces, dgp, x, g1, dx1, w_in, w_gate)


def _k_wgrad(a, b, *, nblk, stacked, name):
    s, k = a.shape
    n = b.shape[1]
    nb = n // nblk
    ts = min(2048 if k <= 1024 else 1024, s)

    def body(a_ref, b_ref, o_ref):
        @pl.when(pl.program_id(1) == 0)
        def _():
            o_ref[...] = jnp.zeros_like(o_ref)

        o_ref[...] += _dot_tn(a_ref[...], b_ref[...])

    if stacked:
        out_spec, out_shape = pl.BlockSpec((None, k, nb), lambda g, t: (g, 0, 0)), _sds((nblk, k, nb))
    else:
        out_spec, out_shape = pl.BlockSpec((k, nb), lambda g, t: (0, g)), _sds((k, n))
    return _pc(body, name=name, grid=(nblk, s // ts),
               in_specs=[pl.BlockSpec((ts, k), lambda g, t: (t, 0)), pl.BlockSpec((ts, nb), lambda g, t: (t, g))],
               out_specs=[out_spec], out_shape=[out_shape])(a, b)[0]


def _to_res(t, d):
    s, c = t.shape
    return t if d == 1 else t.reshape(s // d, d, c).transpose(1, 0, 2).reshape(s, c)


def _from_res(t, d):
    s, c = t.shape
    return t if d == 1 else t.reshape(d, s // d, c).transpose(1, 0, 2).reshape(s, c)


def _tile_gain(g, heads):
    return jnp.tile(g, (1,) * (g.ndim - 1) + (heads,))[..., None, :]


def _local_step(x, mem, pos, target, small, get_w_in, get_rest, on_grads):
    s = x.shape[0]
    nblk = s // BLK
    g1, g2 = small["attn_norm"], small["ffn_norm"]

    pos_rows = jnp.concatenate([_to_res(pos[:, None], d)[:, 0] for _, d in A_GROUPS] + [pos])
    tabs = _rope_tables(pos_rows)
    w_in = get_w_in(tabs[0])

    h, qa0, qa1, qa2, q_b, k_b, v_b, m_q = _k_in(x, g1, w_in)

    qkv_a = jnp.concatenate([_to_res(t, d) for t, (_, d) in zip((qa0, qa1, qa2), A_GROUPS)], axis=0)
    gq_a = _tile_gain(small["a_q_norm"], A_HEADS)
    gk_a = _tile_gain(small["a_k_norm"], A_HEADS)
    src_a = ((qkv_a, 0), (qkv_a, 1), (qkv_a, 2))
    qn_a, kn_a, vn_a = _k_prep(src_a, gq_a, gk_a, tabs, 0, wq=A_W, wk=A_W, rows_per_gain=s, name="prep_a")
    segs_a = tuple((gi * nblk, nblk // d) for gi, (_, d) in enumerate(A_GROUPS))
    o_res, l_res = _k_band_fwd(qn_a, kn_a, vn_a, hq=A_HEADS, hk=A_HEADS, max_dist=BLK, segs=segs_a, sink=None,
                               name="attn_a")
    og = [_from_res(o_res[gi * s:(gi + 1) * s], d) for gi, (_, d) in enumerate(A_GROUPS)]
    lg = [_from_res(l_res[gi * s:(gi + 1) * s], d) for gi, (_, d) in enumerate(A_GROUPS)]

    gq_b = _tile_gain(small["b_q_norm"], B_QH)
    gk_b = _tile_gain(small["b_k_norm"], B_KVH)
    src_b = ((q_b, 0), (k_b, 0), (v_b, 0))
    qn_b, kn_b, vn_b = _k_prep(src_b, gq_b, gk_b, tabs, 3 * s, wq=B_QH * HEAD, wk=B_KVH * HEAD, rows_per_gain=s,
                               name="prep_b")
    sink_x = small["b_sinks"][0]
    segs_b = ((0, nblk),)
    o_b, l_b = _k_band_fwd(qn_b, kn_b, vn_b, hq=B_QH, hk=B_KVH, max_dist=B_WINDOW - 1, segs=segs_b, sink=sink_x,
                           name="attn_b")

    wts = get_rest(0, o_b)
    gates = _k_gate(h, wts["w_gate"], small["b_gate"])

    gq_m = _tile_gain(small["m_q_norm"], M_HEADS)[0]
    gk_m = _tile_gain(small["m_k_norm"], M_HEADS)[0]
    mem_n, kv, mk, mv = _k_memkv(mem, small["mem_norm"], wts["w_mem_kv"], gk_m)
    o_m = _k_mem_fwd(m_q, gq_m, mk, mv)

    o_a, merged, x1, h2 = _k_merge(og, lg, o_b, o_m, gates, x, wts["w_o_a"], wts["w_o_b"], wts["w_o_m"],
                                   wts["w_out"], g2)
    wts.update(get_rest(1, x1))
    u = _k_up(h2, wts["w_up"])
    dy, f, dc, loss_acc = _k_ffn(u, wts["conv_w"], small["conv_b"], wts["w_down"], wts["w_down"].T, x1, target)
    loss = (0.5 / D_MODEL) * jnp.sum(loss_acc)

    dx1, du, cacc, g2acc = _k_conv_bwd(dc, u, wts["conv_w"], wts["w_up"], x1, g2, dy)
    tok = on_grads({"w_up": _k_wgrad(h2, du, nblk=CHIPS, stacked=True, name="dw_up"),
                    "w_down": _k_wgrad(f, dy, nblk=2, stacked=False, name="dw_down").reshape(CHIPS, -1, D_MODEL)}, dx1)
    (dgp, dog0, dog1, dog2, dl0, dl1, dl2, do_b, do_m, bacc, dw_oa, dw_ob, dw_om, dw_out) = _k_merge_bwd(
        dx1, og, lg, o_a, o_b, o_m, gates, merged, wts["w_o_a"], wts["w_o_b"], wts["w_o_m"], wts["w_out"], tok)
    tok = on_grads({"w_gate": _k_wgrad(h, dgp, nblk=CHIPS, stacked=True, name="dw_gate"),
                    "w_o_a": dw_oa, "w_o_b": dw_ob, "w_o_m": dw_om, "w_out": dw_out.reshape(CHIPS, -1, D_MODEL)}, do_m)

    dq_m, dmk, dmv, gqm_acc = _k_mem_bwd(m_q, gq_m + tok[0:1, 0:1], mk, mv, o_m, do_m)
    dw_kv, gmem_acc, gkm_acc = _k_memkv_bwd(mem, small["mem_norm"], wts["w_mem_kv"], gk_m, mem_n, kv, dmk, dmv)

    dq_bn, dk_bn, dv_b, sacc = _k_band_bwd(qn_b, kn_b, vn_b, do_b, l_b, o_b, hq=B_QH, hk=B_KVH,
                                           max_dist=B_WINDOW - 1, segs=segs_b, sink=sink_x, name="attn_b_bwd")
    tok = on_grads({}, dq_bn)
    dq_b, dk_b, gqb_acc, gkb_acc = _k_prep_bwd(src_b, dq_bn, dk_bn, gq_b + tok[0:1, 0:1], gk_b, tabs, 3 * s, wq=B_QH * HEAD,
                                               wk=B_KVH * HEAD, rows_per_gain=s, name="prep_b_bwd")

    do_res = jnp.concatenate([_to_res(t, d) for t, (_, d) in zip((dog0, dog1, dog2), A_GROUPS)], axis=0)
    dl_res = jnp.concatenate([_to_res(t, d) for t, (_, d) in zip((dl0, dl1, dl2), A_GROUPS)], axis=0)
    dq_an, dk_an, dv_a = _k_band_bwd(qn_a, kn_a, vn_a, do_res, l_res, dl_res, hq=A_HEADS, hk=A_HEADS, max_dist=BLK,
                                     segs=segs_a, sink=None, name="attn_a_bwd")
    dq_a, dk_a, gqa_acc, gka_acc = _k_prep_bwd(src_a, dq_an, dk_an, gq_a, gk_a, tabs, 0, wq=A_W, wk=A_W,
                                               rows_per_gain=s, name="prep_a_bwd")
    pieces = []
    for gi, (_, d) in enumerate(A_GROUPS):
        rs = slice(gi * s, (gi + 1) * s)
        pieces += [_from_res(t[rs], d) for t in (dq_a, dk_a, dv_a)]
    pieces += [dq_b, dk_b, dv_b, dq_m]
    grad_x, dproj, g1acc = _k_in_bwd(pieces, dgp, x, g1, dx1, w_in, wts["w_gate"])
    on_grads({"w_in": _k_wgrad(h, dproj, nblk=CHIPS, stacked=True, name="dw_in"),
              "w_mem_kv": dw_kv.reshape(CHIPS, -1, 2 * M_W)}, grad_x)

    def fold(acc, heads):
        v = jnp.sum(acc, axis=-2)
        return jnp.sum(v.reshape(v.shape[:-1] + (heads, -1)), axis=-2)

    csum = jnp.sum(cacc, axis=1)
    sml = {
        "attn_norm": jnp.sum(g1acc, axis=0), "a_q_norm": fold(gqa_acc, A_HEADS), "a_k_norm": fold(gka_acc, A_HEADS),
        "b_q_norm": fold(gqb_acc[0], B_QH), "b_k_norm": fold(gkb_acc[0], B_KVH),
        "b_sinks": jnp.sum(sacc, axis=0)[:B_QH], "mem_norm": jnp.sum(gmem_acc, axis=0),
        "m_q_norm": fold(gqm_acc, M_HEADS), "m_k_norm": fold(gkm_acc, M_HEADS),
        "b_gate": jnp.sum(bacc, axis=0), "ffn_norm": jnp.sum(g2acc, axis=0),
        "conv_w": csum[1:], "conv_b": csum[0],
    }
    return loss, grad_x, sml


def _mesh_pos():
    return lax.axis_index("x"), lax.axis_index("y"), lax.axis_index("c")


def _chip_peers(x, y):
    return [(1 - x, y), (x, 1 - y), (1 - x, 1 - y)]


_ANY = pl.BlockSpec(memory_space=pl.ANY)


def _comm_call(body, *, name, n_in, out_shape, scratch):
    return pl.pallas_call(body, name=name, in_specs=[_ANY] * n_in, out_specs=[_ANY] * len(out_shape),
                          out_shape=out_shape, scratch_shapes=scratch)


def _remote(src, dst, send_sem, recv_sem, dev):
    return pltpu.make_async_remote_copy(src_ref=src, dst_ref=dst, send_sem=send_sem, recv_sem=recv_sem,
                                        device_id=dev, device_id_type=MESH)


def _pair_join(halves, name):
    nt = len(halves)

    def body(*refs):
        ins, got = refs[:nt], refs[nt:2 * nt]
        send_sems, recv_sems = refs[2 * nt:]
        x, y, c = _mesh_pos()
        cps = []
        for t in range(nt):
            rc = _remote(ins[t], got[t], send_sems.at[t], recv_sems.at[t], (x, y, 1 - c))
            rc.start()
            cps.append(rc)
        for rc in cps:
            rc.wait()

    out_shape = [_sds(hf.shape, hf.dtype) for hf in halves]
    scratch = [pltpu.SemaphoreType.DMA((nt,)), pltpu.SemaphoreType.DMA((nt,))]
    return _comm_call(body, name=name, n_in=nt, out_shape=out_shape, scratch=scratch)(*halves)


_HBM = pl.BlockSpec(memory_space=pltpu.HBM)
_SEMS = pl.BlockSpec(memory_space=pltpu.SEMAPHORE)
_EFFECT = pltpu.SideEffectType.DATAFLOW_SIDE_EFFECTING


def _bcast_copies(ins, lands, send_sems, recv_sems):
    x, y, c = _mesh_pos()
    me = 2 * x + y
    targets = [((px, py, c), 2 * px + py) for px, py in _chip_peers(x, y)] + [((x, y, 1 - c), me)]
    out = []
    for t in range(len(ins)):
        for k, (dev, idx) in enumerate(targets):
            i = t * len(targets) + k
            arrival = lambda t=t, i=i, idx=idx, dev=dev: _remote(ins[t], lands[t].at[idx], send_sems.at[i],
                                                                 recv_sems.at[i], dev)
            out.append((_remote(ins[t], lands[t].at[me], send_sems.at[i], recv_sems.at[i], dev), arrival))
    return out


def _scatter_copies(ins, lands, send_sems, recv_sems):
    x, y, c = _mesh_pos()
    out = []
    for t in range(len(ins)):
        for k, (px, py) in enumerate(_chip_peers(x, y)):
            i = t * 3 + k
            cp = _remote(ins[t].at[2 * px + py], lands[t].at[k], send_sems.at[i], recv_sems.at[i], (px, py, c))
            out.append((cp, lambda cp=cp: cp))
    return out


def _pair_copies(ins, lands, send_sems, recv_sems):
    x, y, c = _mesh_pos()
    out = []
    for t in range(len(ins)):
        hr = ins[t].shape[1] // 2
        give = ins[t].at[:, pl.ds(pl.multiple_of((1 - c) * hr, 8), hr), :]
        cp = _remote(give, lands[t], send_sems.at[t], recv_sems.at[t], (x, y, 1 - c))
        out.append((cp, lambda cp=cp: cp))
    return out


def _join_copies(ins, lands, send_sems, recv_sems):
    x, y, c = _mesh_pos()
    out = []
    for t in range(len(ins)):
        cp = _remote(ins[t], lands[t], send_sems.at[t], recv_sems.at[t], (x, y, 1 - c))
        out.append((cp, lambda cp=cp: cp))
    return out


def _half_copies(ins, lands, send_sems, recv_sems):
    x, y, c = _mesh_pos()
    me = 2 * x + y
    out = []
    for t in range(len(ins)):
        hr = ins[t].shape[0] // 2
        rows = pl.ds(pl.multiple_of(c * hr, 8), hr)
        for k, (px, py) in enumerate(_chip_peers(x, y)):
            i = t * 3 + k
            arrival = lambda t=t, i=i, px=px, py=py, rows=rows: _remote(
                ins[t].at[rows, :], lands[t].at[2 * px + py].at[rows, :], send_sems.at[i], recv_sems.at[i], (px, py, c))
            out.append((_remote(ins[t].at[rows, :], lands[t].at[me].at[rows, :], send_sems.at[i], recv_sems.at[i],
                                (px, py, c)), arrival))
    return out


def _finish_halves(shards, stacks):
    nt = len(shards)

    def body(*refs):
        ins, held, outs = refs[:nt], refs[nt:2 * nt], refs[2 * nt:3 * nt]
        fwd_s, fwd_r, own_s, own_r = refs[3 * nt:]
        x, y, c = _mesh_pos()
        me = 2 * x + y
        sib = (x, y, 1 - c)
        pending = []
        for t in range(nt):
            hr = shards[t].shape[0] // 2
            half = lambda ref, who: ref.at[pl.ds(pl.multiple_of(who * hr, 8), hr), :]
            own = _remote(ins[t], outs[t].at[me], own_s.at[t], own_r.at[t], sib)
            own.start()
            pending.append(own.wait)
            for k, (px, py) in enumerate(_chip_peers(x, y)):
                pj = 2 * px + py
                fw = _remote(half(held[t].at[pj], c), half(outs[t].at[pj], c), fwd_s.at[t, k], fwd_r.at[t, k], sib)
                fw.start()
                pending.append(fw.wait_send)
                other = half(outs[t].at[pj], 1 - c)
                pending.append(_remote(other, other, fwd_s.at[t, k], fwd_r.at[t, k], sib).wait_recv)
        for wait in pending:
            wait()

    dma = pltpu.SemaphoreType.DMA
    return pl.pallas_call(
        body, name="gather_w_in_finish", in_specs=[_ANY] * (2 * nt), out_specs=[_ANY] * nt,
        out_shape=[_sds(a.shape, a.dtype) for a in stacks], input_output_aliases={nt + i: i for i in range(nt)},
        scratch_shapes=[dma((nt, 3)), dma((nt, 3)), dma((nt,)), dma((nt,))])(*shards, *stacks)


def _split_start(copies, srcs, land_shapes, ncopy, dep, name, lands=None):
    nt = len(srcs)

    def body(*refs):
        ins, lands = refs[:nt], refs[nt:2 * nt]
        send_sems, recv_sems, token = refs[2 * nt + 1], refs[2 * nt + 2], refs[-1]
        for send, _ in copies(ins, lands, send_sems, recv_sems):
            send.start()
        token[...] = jnp.zeros_like(token)

    if lands is None:
        lands = [lax.empty(sh, a.dtype) for sh, a in zip(land_shapes, srcs)]
    lands = [pltpu.with_memory_space_constraint(a, pltpu.HBM) for a in lands]
    srcs = [pltpu.with_memory_space_constraint(a, pltpu.HBM) for a in srcs]
    dma = pltpu.SemaphoreType.DMA
    out_shape = ([dma((nt * ncopy,)), dma((nt * ncopy,))] + [pltpu.HBM(a.shape, a.dtype) for a in srcs + lands]
                 + [_sds((8, 128))])
    outs = pl.pallas_call(
        body, name=name, in_specs=[_HBM] * (2 * nt) + [_ANY],
        out_specs=[_SEMS, _SEMS] + [_HBM] * (2 * nt) + [pl.BlockSpec(memory_space=pltpu.VMEM)], out_shape=out_shape,
        input_output_aliases={i: 2 + i for i in range(2 * nt)},
        compiler_params=pltpu.CompilerParams(has_side_effects=_EFFECT))(*srcs, *lands, dep)
    return outs[0], outs[1], outs[2:2 + nt], outs[2 + nt:2 + 2 * nt], outs[-1]


def _split_wait(copies, send_sems, recv_sems, srcs, lands, after, name):
    nt = len(srcs)

    def body(*refs):
        ins, lnd = refs[:nt], refs[nt:2 * nt]
        for send, arrival in copies(ins, lnd, refs[2 * nt], refs[2 * nt + 1]):
            send.wait_send()
            arrival().wait_recv()

    outs = pl.pallas_call(
        body, name=name, in_specs=[_HBM] * (2 * nt) + [_SEMS, _SEMS, _ANY], out_specs=[_HBM] * (2 * nt),
        out_shape=[pltpu.HBM(a.shape, a.dtype) for a in list(srcs) + list(lands)],
        input_output_aliases={i: i for i in range(2 * nt)},
        compiler_params=pltpu.CompilerParams(has_side_effects=_EFFECT))(*srcs, *lands, send_sems, recv_sems, after)
    return outs[:nt], outs[nt:]


def _small_copies(ins, lands, send_sems, recv_sems):
    x, y, c = _mesh_pos()
    me = 4 * x + 2 * y + c
    out = []
    for k in range(1, NDEV):
        px, py, pc = x ^ (k >> 2), y ^ ((k >> 1) & 1), c ^ (k & 1)
        arrival = lambda k=k, px=px, py=py, pc=pc: _remote(ins[0], lands[0].at[4 * px + 2 * py + pc], send_sems.at[k - 1],
                                                            recv_sems.at[k - 1], (px, py, pc))
        out.append((_remote(ins[0], lands[0].at[me], send_sems.at[k - 1], recv_sems.at[k - 1], (px, py, pc)), arrival))
    return out


def _row_tile(r, c, mib=1):
    t = r
    while t * c * 4 > (mib << 20) and t % 16 == 0:
        t //= 2
    return t


def _k_pair_add(full, got, name):
    g, r, c = full.shape
    hr = r // 2
    tr = _row_tile(hr, c, 4)
    nh = hr // tr

    def body(a_ref, b_ref, o_ref):
        o_ref[...] = (a_ref[...] + b_ref[...]).astype(_WIRE)

    mine = pl.BlockSpec((None, tr, c), lambda i, j: (i, lax.axis_index("c") * nh + j, 0))
    spec = pl.BlockSpec((None, tr, c), lambda i, j: (i, j, 0))
    return _pc(body, name=name, grid=(g, nh), in_specs=[mine, spec], out_specs=[spec],
               out_shape=[_sds((g, hr, c), _WIRE)])(full, got)[0]


def _k_chip_sum(parts, slots, name):
    _, r, c = parts.shape
    tr = _row_tile(r, c, 4)

    def body(a_ref, s_ref, o_ref):
        acc = a_ref[...].astype(F32)
        for k in range(3):
            acc = acc + s_ref[k].astype(F32)
        o_ref[...] = acc

    own = pl.BlockSpec((None, tr, c), lambda i: (2 * lax.axis_index("x") + lax.axis_index("y"), i, 0))
    return _pc(body, name=name, grid=(r // tr,), in_specs=[own, pl.BlockSpec((3, tr, c), lambda i: (0, i, 0))],
               out_specs=[_row(tr, c)], out_shape=[_sds((r, c))])(parts, slots)[0]


def _adam(w, g, m, v):
    m = ADAM_B1 * m + (1.0 - ADAM_B1) * g
    v = ADAM_B2 * v + (1.0 - ADAM_B2) * (g * g)
    m_hat = m / (1.0 - ADAM_B1 ** ADAM_STEP)
    v_hat = v / (1.0 - ADAM_B2 ** ADAM_STEP)
    return -ADAM_LR * (m_hat / (jnp.sqrt(v_hat) + ADAM_EPS) + ADAM_WD * w), m, v


def _k_adam(w, mine, theirs, m, v, dep, name):
    r, c = w.shape
    hr = r // 2
    tr = _row_tile(hr, c, 2)
    nh = hr // tr

    def body(w_ref, a_ref, b_ref, m_ref, v_ref, dep_ref, g_ref, d_ref, mo_ref, vo_ref):
        upper = (pl.program_id(0) >= nh).astype(jnp.int32)
        g = jnp.where(upper == lax.axis_index("c"), a_ref[...], b_ref[...])
        g_ref[...] = g
        d_ref[...], mo_ref[...], vo_ref[...] = _adam(w_ref[...], g, m_ref[...], v_ref[...])

    hspec = pl.BlockSpec((tr, c), lambda i: (jnp.where(i >= nh, i - nh, i), 0))
    return _pc(body, name=name, grid=(r // tr,),
               in_specs=[_row(tr, c), hspec, hspec, _row(tr, c), _row(tr, c), _res((8, 128))],
               out_specs=[_row(tr, c)] * 4, out_shape=[_sds((r, c))] * 4)(w, mine, theirs, m, v, dep)


def _k_sum8(a):
    _, n, _ = a.shape

    def body(a_ref, o_ref):
        acc = a_ref[0]
        for k in range(1, NDEV):
            acc = acc + a_ref[k]
        o_ref[...] = acc

    return _pc(body, name="sum_small_grads", grid=(1,), in_specs=[_acc(a.shape)], out_specs=[_acc((n, 128))],
               out_shape=[_sds((n, 128))])(a)[0]


def _k_adam_small(ws, gs, ms, vs):
    n = len(ws)

    def body(*refs):
        for k in range(n):
            w_ref, g_ref, m_ref, v_ref, d_ref, mo_ref, vo_ref = refs[k::n]
            d_ref[...], mo_ref[...], vo_ref[...] = _adam(w_ref[...], g_ref[...], m_ref[...], v_ref[...])

    specs = [_acc(a.shape) for a in ws]
    outs = _pc(body, name="adam_small", grid=(1,), in_specs=specs * 4, out_specs=specs * 3,
               out_shape=[_sds(a.shape) for a in ws] * 3)(*ws, *gs, *ms, *vs)
    return outs[:n], outs[n:2 * n], outs[2 * n:]


def _pack(vals):
    rows = []
    for a in vals:
        flat = a.reshape(-1)
        n = -(-flat.shape[0] // 1024) * 1024
        rows.append(jnp.pad(flat, (0, n - flat.shape[0])).reshape(n // 128, 128))
    return jnp.concatenate(rows, axis=0)


def _unpack(packed, shapes):
    out, off = [], 0
    for sh in shapes:
        size = int(np.prod(sh))
        n = -(-size // 1024) * 1024
        out.append(packed[off // 128:(off + n) // 128].reshape(-1)[:size].reshape(sh))
        off += n
    return out


_WEIGHTS = ["attn_norm", "w_in", "a_q_norm", "a_k_norm", "b_q_norm", "b_k_norm", "b_sinks", "mem_norm", "w_mem_kv",
            "m_q_norm", "m_k_norm", "w_o_a", "w_o_b", "w_o_m", "w_gate", "b_gate", "w_out", "ffn_norm", "w_up",
            "conv_w", "conv_b", "w_down"]
_BIG = ["w_in", "w_mem_kv", "w_o_a", "w_o_b", "w_o_m", "w_gate", "w_out", "w_up", "w_down"]
_SMALL = [n for n in _WEIGHTS if n not in _BIG]


def kernel(x, mem, positions, attn_norm, w_in, a_q_norm, a_k_norm, b_q_norm, b_k_norm, b_sinks, mem_norm, w_mem_kv, m_q_norm, m_k_norm, w_o_a, w_o_b, w_o_m, w_gate, b_gate, w_out, ffn_norm, w_up, conv_w, conv_b, w_down, loss_target, m_attn_norm, m_w_in, m_a_q_norm, m_a_k_norm, m_b_q_norm, m_b_k_norm, m_b_sinks, m_mem_norm, m_w_mem_kv, m_m_q_norm, m_m_k_norm, m_w_o_a, m_w_o_b, m_w_o_m, m_w_gate, m_b_gate, m_w_out, m_ffn_norm, m_w_up, m_conv_w, m_conv_b, m_w_down, v_attn_norm, v_w_in, v_a_q_norm, v_a_k_norm, v_b_q_norm, v_b_k_norm, v_b_sinks, v_mem_norm, v_w_mem_kv, v_m_q_norm, v_m_k_norm, v_w_o_a, v_w_o_b, v_w_o_m, v_w_gate, v_b_gate, v_w_out, v_ffn_norm, v_w_up, v_conv_w, v_conv_b, v_w_down):
    given = dict(locals())
    w = {n: given[n][0] for n in _WEIGHTS}
    m1 = {n: given["m_" + n][0] for n in _WEIGHTS}
    m2 = {n: given["v_" + n][0] for n in _WEIGHTS}

    zeros = jnp.zeros((8, 128), F32)
    w_in_shard = w["w_in"].astype(_MM)
    *w_in_handles, tok = _split_start(_half_copies, [w_in_shard], [(CHIPS,) + w_in_shard.shape], 3, zeros,
                                      "gather_w_in_start")

    def get_w_in(after):
        send, recv, srcs, lands = w_in_handles
        srcs, lands = _split_wait(_half_copies, send, recv, srcs, lands, after, "gather_w_in_wait")
        return _finish_halves(srcs, lands)[0]

    stages = (["w_gate", "w_mem_kv", "w_o_a", "w_o_b", "w_o_m", "w_out"], ["w_up", "w_down", "conv_w"])
    started = []
    for k, names in enumerate(stages):
        shards = [w[n] if n == "conv_w" else w[n].astype(_MM) for n in names]
        *handles, tok = _split_start(_bcast_copies, shards, [(CHIPS,) + a.shape for a in shards], 4, tok,
                                     "gather_start_%d" % k)
        started.append(handles)
    small = {n: (w[n][None, :] if w[n].ndim == 1 else w[n]) for n in _SMALL if n != "conv_w"}
    positions = positions + tok[0:1, 0:1].astype(positions.dtype)

    def get_rest(stage, after):
        send, recv, srcs, lands = started[stage]
        got = _split_wait(_bcast_copies, send, recv, srcs, lands, after, "gather_wait_%d" % stage)[1]
        wts = dict(zip(stages[stage], got))
        for n in ("w_mem_kv", "w_out", "w_down"):
            if n in wts:
                wts[n] = wts[n].reshape(-1, wts[n].shape[-1])
        return wts

    parts, slots, pair, scat, started_pair = {}, {}, [], [], [None]

    def finish_pair(after):
        names, tag, send, recv, srcs, lands = pair.pop()
        full, got = _split_wait(_pair_copies, send, recv, srcs, lands, after, "pair_wait_" + tag)
        mine = [_k_pair_add(f, b, "pair_add_" + n) for n, f, b in zip(names, full, got)]
        shapes = [(3,) + p.shape[1:] for p in mine]
        send, recv, srcs, lands, token = _split_start(_scatter_copies, mine, shapes, 3, zeros, "scatter_start_" + tag)
        scat.append((names, tag, send, recv, srcs, lands))
        return token

    def on_grads(group, after):
        names = list(group)
        tag = "_".join(names)
        token = finish_pair(after) if pair else zeros
        if not group:
            return token
        grads_g = [group[n] for n in names]
        shapes = [(CHIPS, g.shape[1] // 2, g.shape[2]) for g in grads_g]
        send, recv, srcs, lands, token = _split_start(_pair_copies, grads_g, shapes, 1, token, "pair_start_" + tag)
        pair.append((names, tag, send, recv, srcs, lands))
        started_pair[0] = token
        return token

    loss, grad_x, sml = _local_step(x[0], mem[0], positions[0], loss_target[0], small, get_w_in, get_rest, on_grads)

    packed = _pack([sml[n] for n in _SMALL] + [loss.reshape(1)])
    me = 4 * lax.axis_index("x") + 2 * lax.axis_index("y") + lax.axis_index("c")
    land = lax.dynamic_update_slice(jnp.zeros((NDEV,) + packed.shape, F32), packed[None], (me, 0, 0))
    *small_h, tok = _split_start(_small_copies, [packed], None, NDEV - 1, zeros, "gather_small_start", lands=[land])
    started_pair[0] = started_pair[0] + tok

    early = [n for names, *_ in scat for n in names]
    for names, tag, send, recv, srcs, lands in scat:
        mine, got = _split_wait(_scatter_copies, send, recv, srcs, lands, started_pair[0], "scatter_wait_" + tag)
        parts.update(zip(names, mine))
        slots.update(zip(names, got))
    scat.clear()
    reduced = {n: _k_chip_sum(parts[n], slots[n], "chip_add_" + n) for n in early}
    halves = [reduced[n] for n in early]
    *join, tok = _split_start(_join_copies, halves, [a.shape for a in halves], 1, zeros, "pair_join_start_early")
    grads = {}

    delta, new_m, new_v = {}, {}, {}
    dep = finish_pair(tok)
    mine, got = _split_wait(_join_copies, *join, dep, "pair_join_wait_early")
    reduced.update(zip(early, mine))
    theirs = dict(zip(early, got))
    for n in early:
        grads[n], delta[n], new_m[n], new_v[n] = _k_adam(w[n], reduced[n], theirs[n], m1[n], m2[n], dep, "adam_" + n)
        dep = delta[n]
    gathered = _split_wait(_small_copies, *small_h, dep, "gather_small_wait")[1][0]
    shapes = [sml[n].shape for n in _SMALL] + [(1,)]
    *gsmall, loss = _unpack(_k_sum8(gathered), shapes)
    loss = loss[0]
    gsm = dict(zip(_SMALL, gsmall))
    nu = w["conv_w"].shape[1]
    chip = 2 * lax.axis_index("x") + lax.axis_index("y")
    gsm["conv_w"] = lax.dynamic_slice_in_dim(gsm["conv_w"], chip * nu, nu, axis=1)
    for n in _SMALL:
        grads[n] = gsm[n].reshape(w[n].shape)
    as2d = lambda d: [d[n][None, :] if d[n].ndim == 1 else d[n] for n in _SMALL]
    for dst, outs in zip((delta, new_m, new_v), _k_adam_small(as2d(w), as2d(grads), as2d(m1), as2d(m2))):
        dst.update((n, a.reshape(w[n].shape)) for n, a in zip(_SMALL, outs))
    late, tag, send, recv, srcs, lands = scat.pop()
    mine, got = _split_wait(_scatter_copies, send, recv, srcs, lands, dep, "scatter_wait_" + tag)
    for n, a, b in zip(late, mine, got):
        reduced[n] = _k_chip_sum(a, b, "chip_add_" + n)
    theirs.update(zip(late, _pair_join([reduced[n] for n in late], "grad_pair_join_late")))
    for n in late:
        grads[n], delta[n], new_m[n], new_v[n] = _k_adam(w[n], reduced[n], theirs[n], m1[n], m2[n], zeros, "adam_" + n)

    lead = lambda d: [d[n][None] for n in _WEIGHTS]
    return (loss, grad_x[None], *lead(grads), *lead(delta), *lead(new_m), *lead(new_v))
```

```python
import math

import jax
import jax.numpy as jnp
import numpy as np
from jax import lax
from jax.experimental import pallas as pl
from jax.experimental.pallas import tpu as pltpu

F32 = jnp.float32
_MM = jnp.bfloat16
_WIRE = jnp.bfloat16

D_MODEL = 1024
HEAD = 64
BLK = 128
A_GROUPS = ((128, 1), (512, 4), (2048, 16))
A_HEADS = 4
A_W = A_HEADS * HEAD
B_QH = 8
B_KVH = 2
B_WINDOW = 128
M_HEADS = 4
M_HD = 128
M_W = M_HEADS * M_HD
D_FF = 2816
EPS = 1e-6
NEG = -1e30
ROPE_THETA = 500000.0
ROPE_ROT = 16
CHIPS = 4
NDEV = 8
ADAM_LR, ADAM_B1, ADAM_B2, ADAM_EPS, ADAM_WD, ADAM_STEP = 0.001, 0.9, 0.999, 1e-08, 0.01, 10
VMEM_LIMIT = 58 * 1024 * 1024
MESH = pl.DeviceIdType.MESH


def _pc(body, *, name, grid, in_specs, out_specs, out_shape, scratch=()):
    return pl.pallas_call(
        body, name=name, grid=grid, in_specs=in_specs, out_specs=out_specs, out_shape=out_shape,
        scratch_shapes=list(scratch),
        compiler_params=pltpu.CompilerParams(dimension_semantics=("arbitrary",) * len(grid),
                                             vmem_limit_bytes=VMEM_LIMIT))


def _row(ts, c, col=0):
    return pl.BlockSpec((ts, c), lambda i: (i, col))


def _res(shape):
    n = len(shape)
    return pl.BlockSpec(tuple(shape), lambda i: (0,) * n, pipeline_mode=pl.Buffered(1))


def _acc(shape):
    n = len(shape)
    return pl.BlockSpec(tuple(shape), lambda i: (0,) * n)


def _sds(shape, dtype=F32):
    return jax.ShapeDtypeStruct(tuple(shape), dtype)


def _dot(a, b):
    return jnp.dot(a.astype(_MM), b.astype(_MM), preferred_element_type=F32)


def _dot_nt(a, b):
    return lax.dot_general(a.astype(_MM), b.astype(_MM), (((1,), (1,)), ((), ())), preferred_element_type=F32)


def _dot_tn(a, b):
    return lax.dot_general(a.astype(_MM), b.astype(_MM), (((0,), (0,)), ((), ())), preferred_element_type=F32)


def _sum8(v):
    ts, c = v.shape
    return jnp.sum(v.reshape(ts // 8, 8, c), axis=0)


def _sigmoid(z):
    return 1.0 / (1.0 + jnp.exp(-z))


def _rms(x):
    r = lax.rsqrt(jnp.mean(x * x, axis=-1, keepdims=True) + EPS)
    return x * r, r


def _rms_bwd(dy, xh, r, gain):
    z = dy * gain
    return r * (z - xh * jnp.mean(z * xh, axis=-1, keepdims=True))


def _split_hi_lo(v):
    hi = v.astype(_MM)
    return hi, (v - hi.astype(F32)).astype(_MM)


def _lane_head(shape):
    return lax.shift_right_logical(lax.broadcasted_iota(jnp.int32, shape, len(shape) - 1), 6)


def _seg_sum64(v):
    w = v.shape[1]
    e = jnp.where(_lane_head((w, w)) == lax.shift_right_logical(lax.broadcasted_iota(jnp.int32, (w, w), 0), 6),
                  1.0, 0.0).astype(_MM)
    hi, lo = _split_hi_lo(v)
    return jnp.dot(hi, e, preferred_element_type=F32) + jnp.dot(lo, e, preferred_element_type=F32)


def _seg_norm(x, seg):
    if seg == HEAD:
        r = lax.rsqrt(_seg_sum64(x * x) * (1.0 / HEAD) + EPS)
        return x * r, r
    w = x.shape[1]
    xh, rr = [], []
    for s in range(w // seg):
        xs = x[:, s * seg:(s + 1) * seg]
        r = lax.rsqrt(jnp.mean(xs * xs, axis=-1, keepdims=True) + EPS)
        xh.append(xs * r)
        rr.append(jnp.broadcast_to(r, xs.shape))
    return jnp.concatenate(xh, axis=1), jnp.concatenate(rr, axis=1)


def _seg_mean(v, seg):
    if seg == HEAD:
        return _seg_sum64(v) * (1.0 / HEAD)
    w = v.shape[1]
    out = []
    for s in range(w // seg):
        vs = v[:, s * seg:(s + 1) * seg]
        out.append(jnp.broadcast_to(jnp.mean(vs, axis=-1, keepdims=True), vs.shape))
    return jnp.concatenate(out, axis=1)


def _rope(t, c, sa, sb):
    out = []
    for cb in range(t.shape[1] // 128):
        tc = t[:, cb * 128:(cb + 1) * 128]
        out.append(tc * c + pltpu.roll(tc, 120, 1) * sa + pltpu.roll(tc, 8, 1) * sb)
    return jnp.concatenate(out, axis=1) if len(out) > 1 else out[0]


def _rope_bwd(dy, c, sa, sb):
    out = []
    for cb in range(dy.shape[1] // 128):
        dc = dy[:, cb * 128:(cb + 1) * 128]
        out.append(dc * c + pltpu.roll(dc * sa, 8, 1) + pltpu.roll(dc * sb, 120, 1))
    return jnp.concatenate(out, axis=1) if len(out) > 1 else out[0]


def _rope_consts():
    half = ROPE_ROT // 2
    c = np.float32(-2.0 * math.log(ROPE_THETA) / ROPE_ROT)
    freqs = np.exp(np.arange(half, dtype=np.float32) * c).astype(np.float32)
    place = np.zeros((3, half, 128), np.float32)
    ones = np.zeros((1, 128), np.float32)
    for lane in range(128):
        d = lane % HEAD
        if d < half:
            place[0, d, lane], place[1, d, lane] = 1.0, -1.0
        elif d < ROPE_ROT:
            place[0, d - half, lane], place[2, d - half, lane] = 1.0, 1.0
        else:
            ones[0, lane] = 1.0
    return np.tile(freqs[:, None], (1, 128)), place, ones


def _rope_tables(pos_rows):
    r = pos_rows.shape[0]
    tr = min(1024, r)
    freqs, place, ones = _rope_consts()

    def split3(v):
        hi, mid = _split_hi_lo(v)
        lo = (v - hi.astype(F32) - mid.astype(F32)).astype(_MM)
        return hi, mid, lo

    def body(p_ref, f_ref, e_ref, one_ref, c_ref, sa_ref, sb_ref):
        ang = jnp.concatenate([p_ref[j:j + 1, :].astype(F32) * f_ref[...] for j in range(tr // 128)], axis=1)
        cos, sin = jnp.cos(ang), jnp.sin(ang)
        for ref, k, v in ((c_ref, 0, cos), (sa_ref, 1, sin), (sb_ref, 2, sin)):
            e = e_ref[k].astype(_MM)
            out = sum(_dot_tn(part, e) for part in split3(v))
            ref[...] = out + one_ref[...] if k == 0 else out

    return _pc(body, name="rope_tables", grid=(r // tr,),
               in_specs=[pl.BlockSpec((tr // 128, 128), lambda i: (i, 0)), _acc((ROPE_ROT // 2, 128)),
                         _acc((3, ROPE_ROT // 2, 128)), _acc((1, 128))],
               out_specs=[_row(tr, 128)] * 3, out_shape=[_sds((r, 128))] * 3)(
                   pos_rows.reshape(r // 128, 128), jnp.asarray(freqs), jnp.asarray(place), jnp.asarray(ones))


def _k_in(x, g1, w_in):
    s = x.shape[0]
    ts = min(512, s)
    nin = w_in.shape[2]
    ncol = CHIPS * nin
    a_cols = 3 * A_W
    offs = [0, a_cols, 2 * a_cols, 3 * a_cols, 3 * a_cols + B_QH * HEAD,
            3 * a_cols + (B_QH + B_KVH) * HEAD, 3 * a_cols + (B_QH + 2 * B_KVH) * HEAD, ncol]

    def body(x_ref, g_ref, wi_ref, h_ref, a0, a1, a2, qb, kb, vb, mq, p_scr):
        xh, _ = _rms(x_ref[...])
        h = (xh * g_ref[...]).astype(_MM)
        h_ref[...] = h
        for j in range(CHIPS):
            p_scr[:, j * nin:(j + 1) * nin] = jnp.dot(h, wi_ref[j], preferred_element_type=F32)
        for k, ref in enumerate((a0, a1, a2, qb, kb, vb, mq)):
            ref[...] = p_scr[:, offs[k]:offs[k + 1]]

    widths = [offs[k + 1] - offs[k] for k in range(7)]
    return _pc(
        body, name="in_proj", grid=(s // ts,),
        in_specs=[_row(ts, D_MODEL), _res((1, D_MODEL)), _res(w_in.shape)],
        out_specs=[_row(ts, D_MODEL)] + [_row(ts, w) for w in widths],
        out_shape=[_sds((s, D_MODEL), _MM)] + [_sds((s, w)) for w in widths],
        scratch=[pltpu.VMEM((ts, ncol), F32)])(x, g1, w_in)


def _k_prep(srcs, gq, gk, tabs, tab_row, *, wq, wk, rows_per_gain, name):
    rows = srcs[0][0].shape[0]
    ts = min(512, rows)

    def body(q_ref, k_ref, v_ref, gq_ref, gk_ref, c_ref, sa_ref, sb_ref, qn_ref, kn_ref, vn_ref):
        c, sa, sb = c_ref[...], sa_ref[...], sb_ref[...]
        qh, _ = _seg_norm(q_ref[...], HEAD)
        qn_ref[...] = _rope(qh * gq_ref[...], c, sa, sb).astype(_MM)
        kh, _ = _seg_norm(k_ref[...], HEAD)
        kn_ref[...] = _rope(kh * gk_ref[...], c, sa, sb).astype(_MM)
        vn_ref[...] = v_ref[...].astype(_MM)

    gspec = lambda w: pl.BlockSpec((None, 1, w), lambda i: ((i * ts) // rows_per_gain, 0, 0))
    return _pc(
        body, name=name, grid=(rows // ts,),
        in_specs=[_row(ts, wq, srcs[0][1]), _row(ts, wk, srcs[1][1]), _row(ts, wk, srcs[2][1]),
                  gspec(wq), gspec(wk)] + [pl.BlockSpec((ts, 128), lambda i: (i + tab_row // ts, 0))] * 3,
        out_specs=[_row(ts, wq), _row(ts, wk), _row(ts, wk)],
        out_shape=[_sds((rows, wq), _MM), _sds((rows, wk), _MM), _sds((rows, wk), _MM)])(
            srcs[0][0], srcs[1][0], srcs[2][0], gq, gk, *tabs)


def _first_flag(b, segs, nb):
    first = b >= nb
    for k, (start, period) in enumerate(segs):
        end = segs[k + 1][0] if k + 1 < len(segs) else nb
        first = first | ((b >= start) & (b < end) & (lax.rem(b - start, jnp.int32(period)) == 0))
    return first


def _band_bias(thr, with_cur):
    qi = lax.broadcasted_iota(jnp.int32, (BLK, BLK), 0)
    kj = lax.broadcasted_iota(jnp.int32, (BLK, BLK), 1)
    prev = jnp.where(kj >= qi + thr, 0.0, NEG)
    return jnp.concatenate([prev, jnp.where(kj <= qi, 0.0, NEG)], axis=1) if with_cur else prev


def _blockdiag(t4):
    head = _lane_head((1, A_W))
    return jnp.concatenate([t4 * jnp.where(head == h, 1.0, 0.0).astype(t4.dtype) for h in range(A_HEADS)], axis=0)


def _fold_diag(t, n):
    head = _lane_head((n, A_W))
    out = t[3 * n:4 * n]
    for h in (2, 1, 0):
        out = jnp.where(head == h, t[h * n:(h + 1) * n], out)
    return out


def _expand_heads(cols):
    n = cols[0].shape[0]
    head = _lane_head((n, A_W))
    out = jnp.broadcast_to(cols[3], (n, A_W))
    for h in (2, 1, 0):
        out = jnp.where(head == h, cols[h], out)
    return out


def _unit_kv(pieces, u, shared):
    cols = slice(u * HEAD, (u + 1) * HEAD) if shared else slice(u * A_W, (u + 1) * A_W)
    rows = [ref[rs, cols] for ref, rs in pieces]
    k = rows[0] if len(rows) == 1 else jnp.concatenate(rows, axis=0)
    return jnp.concatenate([k] * A_HEADS, axis=1) if shared else k


_LO, _HI, _BOTH = slice(0, BLK), slice(BLK, 2 * BLK), slice(0, 2 * BLK)


def _k_band_fwd(qn, kn, vn, *, hq, hk, max_dist, segs, sink, name):
    rows = qn.shape[0]
    nb = rows // BLK
    units = hq // A_HEADS
    shared = hk != hq
    wq, wk = hq * HEAD, hk * HEAD
    scale = HEAD ** -0.5

    def body(*refs):
        if sink is None:
            q_ref, kc_ref, kp_ref, vc_ref, vp_ref, o_ref, l_ref = refs
        else:
            q_ref, kc_ref, kp_ref, vc_ref, vp_ref, sk_ref, o_ref, l_ref = refs
        i = pl.program_id(0)
        for half, rs in enumerate((_LO, _HI)):
            bias = _band_bias(jnp.where(_first_flag(2 * i + half, segs, nb), 1 << 20, BLK - max_dist), True)
            kpieces = ((kp_ref, _LO), (kc_ref, _LO)) if half == 0 else ((kc_ref, _BOTH),)
            vpieces = ((vp_ref, _LO), (vc_ref, _LO)) if half == 0 else ((vc_ref, _BOTH),)
            for u in range(units):
                us = slice(u * A_W, (u + 1) * A_W)
                kb = _blockdiag(_unit_kv(kpieces, u, shared))
                vb = _blockdiag(_unit_kv(vpieces, u, shared))
                s_all = _dot_nt(q_ref[rs, us], kb) * scale
                ps, ls = [], []
                for h in range(A_HEADS):
                    s = s_all[:, h * 2 * BLK:(h + 1) * 2 * BLK] + bias
                    m = jnp.max(s, axis=-1, keepdims=True)
                    e = jnp.exp(s - m)
                    lse = m + jnp.log(jnp.sum(e, axis=-1, keepdims=True))
                    if sink is not None:
                        sk = sk_ref[u * A_HEADS + h]
                        mx = jnp.maximum(lse, sk)
                        lse = mx + jnp.log(jnp.exp(lse - mx) + jnp.exp(sk - mx))
                    ps.append((e * jnp.exp(m - lse)).astype(_MM))
                    ls.append(lse)
                o_ref[rs, us] = _dot(jnp.concatenate(ps, axis=1), vb)
                l_ref[rs, us] = _expand_heads(ls)

    two = lambda w: pl.BlockSpec((2 * BLK, w), lambda i: (i, 0))
    prev = lambda w: pl.BlockSpec((BLK, w), lambda i: (jnp.maximum(2 * i - 1, 0), 0))
    in_specs = [two(wq), two(wk), prev(wk), two(wk), prev(wk)]
    args = [qn, kn, kn, vn, vn]
    if sink is not None:
        in_specs.append(pl.BlockSpec(memory_space=pltpu.SMEM))
        args.append(sink)
    return _pc(body, name=name, grid=(nb // 2,), in_specs=in_specs, out_specs=[two(wq), two(wq)],
               out_shape=[_sds((rows, wq)), _sds((rows, wq))])(*args)


def _k_memkv(mem, mem_norm, w_kv, m_k_norm):
    n = mem.shape[0]

    def body(m_ref, g_ref, w_ref, gk_ref, mn_ref, kv_ref, mk_ref, mv_ref):
        mh, _ = _rms(m_ref[...])
        mn = (mh * g_ref[...]).astype(_MM)
        mn_ref[...] = mn
        kv = jnp.dot(mn, w_ref[...], preferred_element_type=F32)
        kv_ref[...] = kv
        kh, _ = _seg_norm(kv[:, :M_W], M_HD)
        mk_ref[...] = (kh * gk_ref[...]).astype(_MM)
        mv_ref[...] = kv[:, M_W:].astype(_MM)

    return _pc(body, name="mem_kv", grid=(1,),
               in_specs=[_acc((n, D_MODEL)), _acc((1, D_MODEL)), _acc(w_kv.shape), _acc((1, M_W))],
               out_specs=[_acc((n, D_MODEL)), _acc((n, 2 * M_W)), _acc((n, M_W)), _acc((n, M_W))],
               out_shape=[_sds((n, D_MODEL), _MM), _sds((n, 2 * M_W)), _sds((n, M_W), _MM), _sds((n, M_W), _MM)])(
                   mem, mem_norm, w_kv, m_k_norm)


def _mem_probs(q, mk):
    sc = _dot_nt(q, mk) * (M_HD ** -0.5)
    e = jnp.exp(sc - jnp.max(sc, axis=-1, keepdims=True))
    return e / jnp.sum(e, axis=-1, keepdims=True)


def _k_mem_fwd(m_q, gq, mk, mv):
    s = m_q.shape[0]
    n = mk.shape[0]
    ts = min(512, s)

    def body(q_ref, g_ref, mk_ref, mv_ref, o_ref):
        qh, _ = _seg_norm(q_ref[...], M_HD)
        qn = (qh * g_ref[...]).astype(_MM)
        for h in range(M_HEADS):
            hs = slice(h * M_HD, (h + 1) * M_HD)
            o_ref[:, hs] = _dot(_mem_probs(qn[:, hs], mk_ref[:, hs]), mv_ref[:, hs])

    return _pc(body, name="mem_attn", grid=(s // ts,),
               in_specs=[_row(ts, M_W), _res((1, M_W)), _res((n, M_W)), _res((n, M_W))],
               out_specs=[_row(ts, M_W)], out_shape=[_sds((s, M_W))])(m_q, gq, mk, mv)[0]


def _group_weights(l0, l1, l2):
    m = jnp.maximum(jnp.maximum(l0, l1), l2)
    e0, e1, e2 = jnp.exp(l0 - m), jnp.exp(l1 - m), jnp.exp(l2 - m)
    inv = 1.0 / (e0 + e1 + e2)
    return e0 * inv, e1 * inv, e2 * inv


def _branch_products(oa, ob, om, woa_ref, wob_ref, wom_ref, j):
    return _dot(oa, woa_ref[j]), _dot(ob, wob_ref[j]), _dot(om, wom_ref[j])


def _k_merge(og, lg, o_b, o_m, h, x, w_gate, b_gate, w_oa, w_ob, w_om, w_out, g2):
    s = x.shape[0]
    ts = min(256, s)
    nc = w_oa.shape[2]
    ng = w_gate.shape[2]

    def body(o0, o1, o2, l0, l1, l2, ob_ref, om_ref, h_ref, x_ref, wg, bg_ref, woa, wob, wom, wout, g_ref,
             oa_ref, mer_ref, x1_ref, h2_ref, gt_ref, m_scr):
        h = h_ref[...]
        for j in range(CHIPS):
            z = jnp.dot(h, wg[j], preferred_element_type=F32) + bg_ref[:, j * ng:(j + 1) * ng]
            gt_ref[:, j * ng:(j + 1) * ng] = _sigmoid(z)
        w0, w1, w2 = _group_weights(l0[...], l1[...], l2[...])
        oa = w0 * o0[...] + w1 * o1[...] + w2 * o2[...]
        oa_ref[...] = oa
        ob, om = ob_ref[...], om_ref[...]
        for j in range(CHIPS):
            pa, pb, pm = _branch_products(oa, ob, om, woa, wob, wom, j)
            cs = lambda br: slice(br * D_MODEL + j * nc, br * D_MODEL + (j + 1) * nc)
            m_scr[:, j * nc:(j + 1) * nc] = gt_ref[:, cs(0)] * pa + gt_ref[:, cs(1)] * pb + gt_ref[:, cs(2)] * pm
        mer = m_scr[...].astype(_MM)
        mer_ref[...] = mer
        x1 = x_ref[...] + jnp.dot(mer, wout[...], preferred_element_type=F32)
        x1_ref[...] = x1
        xh, _ = _rms(x1)
        h2_ref[...] = (xh * g_ref[...]).astype(_MM)

    return _pc(
        body, name="merge_out", grid=(s // ts,),
        in_specs=[_row(ts, A_W)] * 6 + [_row(ts, B_QH * HEAD), _row(ts, M_W), _row(ts, D_MODEL), _row(ts, D_MODEL),
                                         _res(w_gate.shape), _res(b_gate.shape), _res(w_oa.shape), _res(w_ob.shape),
                                         _res(w_om.shape), _res(w_out.shape), _res((1, D_MODEL))],
        out_specs=[_row(ts, A_W), _row(ts, D_MODEL), _row(ts, D_MODEL), _row(ts, D_MODEL), _row(ts, CHIPS * ng)],
        out_shape=[_sds((s, A_W)), _sds((s, D_MODEL), _MM), _sds((s, D_MODEL)), _sds((s, D_MODEL), _MM),
                   _sds((s, CHIPS * ng))],
        scratch=[pltpu.VMEM((ts, D_MODEL), F32)])(
            *og, *lg, o_b, o_m, h, x, w_gate, b_gate, w_oa, w_ob, w_om, w_out, g2)


def _k_up(h2, w_up):
    s = h2.shape[0]
    ts = min(256, s)
    nu = w_up.shape[2]

    def body(h_ref, w_ref, u_ref):
        h = h_ref[...]
        for j in range(CHIPS):
            u_ref[:, j * nu:(j + 1) * nu] = jnp.dot(h, w_ref[j], preferred_element_type=F32)

    return _pc(body, name="up_proj", grid=(s // ts,), in_specs=[_row(ts, D_MODEL), _res(w_up.shape)],
               out_specs=[_row(ts, CHIPS * nu)], out_shape=[_sds((s, CHIPS * nu))])(h2, w_up)[0]


def _shift_down(v, halo, k):
    rolled = pltpu.roll(v, k, 0)
    row = lax.broadcasted_iota(jnp.int32, (8, v.shape[1]), 0)
    slab = rolled[0:8]
    for r in range(k):
        slab = jnp.where(row == r, halo[8 - k + r:8 - k + r + 1, :], slab)
    return jnp.concatenate([slab, rolled[8:]], axis=0)


def _shift_up(v, halo, k):
    ts = v.shape[0]
    rolled = pltpu.roll(v, ts - k, 0)
    row = lax.broadcasted_iota(jnp.int32, (8, v.shape[1]), 0)
    slab = rolled[ts - 8:]
    for r in range(k):
        slab = jnp.where(row == 8 - k + r, halo[r:r + 1, :], slab)
    return jnp.concatenate([rolled[:ts - 8], slab], axis=0)


def _k_ffn(u, conv_w, conv_b, w_down, w_down_t, x1, target):
    s = u.shape[0]
    ts = min(256, s)
    nu = conv_w.shape[2]
    half = CHIPS // 2

    def body(u_ref, uh_ref, cw_ref, cb_ref, wd_ref, wdt_ref, x1_ref, t_ref, dy_ref, f_ref, dc_ref, loss_ref, c_scr,
             f_scr, s_scr):
        i = pl.program_id(0)
        halo = jnp.where(i > 0, uh_ref[...], 0.0)
        for j in range(CHIPS):
            cs = slice(j * nu, (j + 1) * nu)
            uj = u_ref[:, cs]
            hj = halo[:, cs]
            c_scr[:, cs] = (cb_ref[:, cs] + cw_ref[j, 0:1, :] * _shift_down(uj, hj, 2)
                            + cw_ref[j, 1:2, :] * _shift_down(uj, hj, 1) + cw_ref[j, 2:3, :] * uj)
        for j in range(half):
            a = c_scr[:, j * nu:(j + 1) * nu]
            g = c_scr[:, (half + j) * nu:(half + j + 1) * nu]
            sa = _sigmoid(a)
            s_scr[:, j * nu:(j + 1) * nu] = sa
            f_scr[:, j * nu:(j + 1) * nu] = (a * sa * g).astype(_MM)
        f = f_scr[...]
        f_ref[...] = f
        y = x1_ref[...] + jnp.dot(f, wd_ref[...], preferred_element_type=F32)
        err = y - t_ref[...]
        dy = err * (1.0 / D_MODEL)
        dy_ref[...] = dy

        @pl.when(i == 0)
        def _():
            loss_ref[...] = jnp.zeros_like(loss_ref)

        loss_ref[...] += _sum8(err * err)
        df = _dot(dy, wdt_ref[...])
        for j in range(half):
            a = c_scr[:, j * nu:(j + 1) * nu]
            g = c_scr[:, (half + j) * nu:(half + j + 1) * nu]
            sa = s_scr[:, j * nu:(j + 1) * nu]
            dfj = df[:, j * nu:(j + 1) * nu]
            dc_ref[:, j * nu:(j + 1) * nu] = dfj * g * (sa * (1.0 + a * (1.0 - sa)))
            dc_ref[:, (half + j) * nu:(half + j + 1) * nu] = dfj * (a * sa)

    wide = CHIPS * nu
    return _pc(
        body, name="conv_ffn", grid=(s // ts,),
        in_specs=[_row(ts, wide), pl.BlockSpec((8, wide), lambda i: (jnp.maximum(i * (ts // 8) - 1, 0), 0)),
                  _res(conv_w.shape), _res((1, wide)), _res(w_down.shape), _res(w_down_t.shape), _row(ts, D_MODEL),
                  _row(ts, D_MODEL)],
        out_specs=[_row(ts, D_MODEL), _row(ts, D_FF), _row(ts, wide), _acc((8, D_MODEL))],
        out_shape=[_sds((s, D_MODEL)), _sds((s, D_FF), _MM), _sds((s, wide)), _sds((8, D_MODEL))],
        scratch=[pltpu.VMEM((ts, wide), F32), pltpu.VMEM((ts, D_FF), _MM), pltpu.VMEM((ts, D_FF), F32)])(
            u, u, conv_w, conv_b, w_down, w_down_t, x1, target)


def _k_conv_bwd(dc, u, conv_w, w_up, x1, g2, dy):
    s = u.shape[0]
    ts = min(256, s)
    nu = conv_w.shape[2]
    wide = CHIPS * nu
    last = s // ts - 1

    def body(dc_ref, dn_ref, u_ref, cw_ref, wu_ref, x1_ref, g_ref, dy_ref, dx1_ref, du_ref, cacc_ref, gacc_ref):
        i = pl.program_id(0)

        @pl.when(i == 0)
        def _():
            cacc_ref[...] = jnp.zeros_like(cacc_ref)
            gacc_ref[...] = jnp.zeros_like(gacc_ref)

        dhalo = jnp.where(i < last, dn_ref[...], 0.0)
        dh2 = jnp.zeros((ts, D_MODEL), F32)
        for j in range(CHIPS):
            cs = slice(j * nu, (j + 1) * nu)
            dcj, uj = dc_ref[:, cs], u_ref[:, cs]
            dc1, dc2 = _shift_up(dcj, dhalo[:, cs], 1), _shift_up(dcj, dhalo[:, cs], 2)
            cacc_ref[0, :, cs] += _sum8(dcj)
            cacc_ref[1, :, cs] += _sum8(dc2 * uj)
            cacc_ref[2, :, cs] += _sum8(dc1 * uj)
            cacc_ref[3, :, cs] += _sum8(dcj * uj)
            du = (cw_ref[j, 2:3, :] * dcj + cw_ref[j, 1:2, :] * dc1 + cw_ref[j, 0:1, :] * dc2).astype(_MM)
            du_ref[:, cs] = du
            dh2 = dh2 + _dot_nt(du, wu_ref[j])
        xh, r = _rms(x1_ref[...])
        gacc_ref[...] += _sum8(dh2 * xh)
        dx1_ref[...] = dy_ref[...] + _rms_bwd(dh2, xh, r, g_ref[...])

    return _pc(
        body, name="conv_up_bwd", grid=(s // ts,),
        in_specs=[_row(ts, wide),
                  pl.BlockSpec((8, wide), lambda i: (jnp.minimum((i + 1) * (ts // 8), s // 8 - 1), 0)),
                  _row(ts, wide), _res(conv_w.shape), _res(w_up.shape), _row(ts, D_MODEL), _res((1, D_MODEL)),
                  _row(ts, D_MODEL)],
        out_specs=[_row(ts, D_MODEL), _row(ts, wide), _acc((4, 8, wide)), _acc((8, D_MODEL))],
        out_shape=[_sds((s, D_MODEL)), _sds((s, wide), _MM), _sds((4, 8, wide)), _sds((8, D_MODEL))])(
            dc, dc, u, conv_w, w_up, x1, g2, dy)


def _k_merge_bwd(dx1, og, lg, o_a, o_b, o_m, gates, merged, w_oa, w_ob, w_om, w_out, dep):
    s = dx1.shape[0]
    ts = min(256, s)
    nc = w_oa.shape[2]

    def body(dx_ref, o0, o1, o2, l0, l1, l2, oa_ref, ob_ref, om_ref, gt_ref, mer_ref, woa, wob, wom, wout, dep_ref,
             dgp_ref, dog0, dog1, dog2, dl0, dl1, dl2, dob_ref, dom_ref, bacc_ref, dwa_ref, dwb_ref, dwm_ref, dwo_ref):
        i = pl.program_id(0)

        @pl.when(i == 0)
        def _():
            for ref in (bacc_ref, dwa_ref, dwb_ref, dwm_ref, dwo_ref):
                ref[...] = jnp.zeros_like(ref)

        dx = dx_ref[...]
        dwo_ref[...] += _dot_tn(mer_ref[...], dx)
        dmer = _dot_nt(dx, wout[...])
        oa, ob, om = oa_ref[...], ob_ref[...], om_ref[...]
        doa = jnp.zeros((ts, A_W), F32)
        dob = jnp.zeros((ts, B_QH * HEAD), F32)
        dom = jnp.zeros((ts, M_W), F32)
        for j in range(CHIPS):
            prods = _branch_products(oa, ob, om, woa, wob, wom, j)
            dmj = dmer[:, j * nc:(j + 1) * nc]
            dps = []
            for br, (p, o, dw_ref) in enumerate(zip(prods, (oa, ob, om), (dwa_ref, dwb_ref, dwm_ref))):
                cs = slice(br * D_MODEL + j * nc, br * D_MODEL + (j + 1) * nc)
                gt = gt_ref[:, cs]
                dgp = dmj * p * gt * (1.0 - gt)
                dgp_ref[:, cs] = dgp.astype(_MM)
                bacc_ref[:, cs] += _sum8(dgp)
                dp = (dmj * gt).astype(_MM)
                dw_ref[j] += _dot_tn(o, dp)
                dps.append(dp)
            doa = doa + _dot_nt(dps[0], woa[j])
            dob = dob + _dot_nt(dps[1], wob[j])
            dom = dom + _dot_nt(dps[2], wom[j])
        dob_ref[...] = dob
        dom_ref[...] = dom
        ws = _group_weights(l0[...], l1[...], l2[...])
        dsum = _seg_mean(doa * oa, HEAD) * float(HEAD)
        for w, dref, lref in zip(ws, (dog0, dog1, dog2), (dl0, dl1, dl2)):
            dref[...] = w * doa
            lref[...] = w * dsum

    return _pc(
        body, name="merge_out_bwd", grid=(s // ts,),
        in_specs=[_row(ts, D_MODEL)] + [_row(ts, A_W)] * 7 + [_row(ts, B_QH * HEAD), _row(ts, M_W), _row(ts, 3 * D_MODEL),
                                                              _row(ts, D_MODEL), _res(w_oa.shape), _res(w_ob.shape),
                                                              _res(w_om.shape), _res(w_out.shape), _res((8, 128))],
        out_specs=[_row(ts, 3 * D_MODEL)] + [_row(ts, A_W)] * 6
        + [_row(ts, B_QH * HEAD), _row(ts, M_W), _acc((8, 3 * D_MODEL)), _acc(w_oa.shape), _acc(w_ob.shape),
           _acc(w_om.shape), _acc(w_out.shape)],
        out_shape=[_sds((s, 3 * D_MODEL), _MM)] + [_sds((s, A_W))] * 6
        + [_sds((s, B_QH * HEAD)), _sds((s, M_W)), _sds((8, 3 * D_MODEL)), _sds(w_oa.shape), _sds(w_ob.shape),
           _sds(w_om.shape), _sds(w_out.shape)])(
            dx1, *og, *lg, o_a, o_b, o_m, gates, merged, w_oa, w_ob, w_om, w_out, dep)


def _k_mem_bwd(m_q, gq, mk, mv, o_m, do_m):
    s = m_q.shape[0]
    n = mk.shape[0]
    ts = min(512, s)
    scale = M_HD ** -0.5

    def body(q_ref, g_ref, mk_ref, mv_ref, o_ref, do_ref, dq_ref, dmk_ref, dmv_ref, gacc_ref):
        i = pl.program_id(0)

        @pl.when(i == 0)
        def _():
            dmk_ref[...] = jnp.zeros_like(dmk_ref)
            dmv_ref[...] = jnp.zeros_like(dmv_ref)
            gacc_ref[...] = jnp.zeros_like(gacc_ref)

        gain = g_ref[...]
        qh, r = _seg_norm(q_ref[...], M_HD)
        qn = (qh * gain).astype(_MM)
        do = do_ref[...]
        delta = _seg_mean(do * o_ref[...], M_HD) * float(M_HD)
        dqn = []
        for h in range(M_HEADS):
            hs = slice(h * M_HD, (h + 1) * M_HD)
            p = _mem_probs(qn[:, hs], mk_ref[:, hs])
            dp = _dot_nt(do[:, hs], mv_ref[:, hs])
            ds = (p * (dp - delta[:, hs][:, 0:1]) * scale).astype(_MM)
            dqn.append(_dot(ds, mk_ref[:, hs]))
            dmk_ref[:, hs] += _dot_tn(ds, qn[:, hs])
            dmv_ref[:, hs] += _dot_tn(p, do[:, hs])
        dqn = jnp.concatenate(dqn, axis=1)
        gacc_ref[...] += _sum8(dqn * qh)
        z = dqn * gain
        dq_ref[...] = (r * (z - qh * _seg_mean(z * qh, M_HD))).astype(_MM)

    return _pc(
        body, name="mem_attn_bwd", grid=(s // ts,),
        in_specs=[_row(ts, M_W), _res((1, M_W)), _res((n, M_W)), _res((n, M_W)), _row(ts, M_W), _row(ts, M_W)],
        out_specs=[_row(ts, M_W), _acc((n, M_W)), _acc((n, M_W)), _acc((8, M_W))],
        out_shape=[_sds((s, M_W), _MM), _sds((n, M_W)), _sds((n, M_W)), _sds((8, M_W))])(m_q, gq, mk, mv, o_m, do_m)


def _k_memkv_bwd(mem, mem_norm, w_kv, m_k_norm, mem_n, kv, dmk, dmv):
    n = mem.shape[0]

    def body(m_ref, g_ref, w_ref, gk_ref, mn_ref, kv_ref, dmk_ref, dmv_ref, dw_ref, dg_ref, dgk_ref):
        gk = gk_ref[...]
        kh, r = _seg_norm(kv_ref[:, :M_W], M_HD)
        dmk = dmk_ref[...]
        dgk_ref[...] = _sum8(dmk * kh)
        z = dmk * gk
        dk = r * (z - kh * _seg_mean(z * kh, M_HD))
        dkv = jnp.concatenate([dk, dmv_ref[...]], axis=1).astype(_MM)
        dw_ref[...] = _dot_tn(mn_ref[...], dkv)
        dmn = _dot_nt(dkv, w_ref[...])
        mh, _ = _rms(m_ref[...])
        dg_ref[...] = _sum8(dmn * mh)

    return _pc(body, name="mem_kv_bwd", grid=(1,),
               in_specs=[_acc((n, D_MODEL)), _acc((1, D_MODEL)), _acc(w_kv.shape), _acc((1, M_W)), _acc((n, D_MODEL)),
                         _acc((n, 2 * M_W)), _acc((n, M_W)), _acc((n, M_W))],
               out_specs=[_acc(w_kv.shape), _acc((8, D_MODEL)), _acc((8, M_W))],
               out_shape=[_sds(w_kv.shape), _sds((8, D_MODEL)), _sds((8, M_W))])(
                   mem, mem_norm, w_kv, m_k_norm, mem_n, kv, dmk, dmv)


def _k_band_bwd(qn, kn, vn, do, lse, dl_or_o, *, hq, hk, max_dist, segs, sink, name):
    rows = qn.shape[0]
    nb = rows // BLK
    units = hq // A_HEADS
    shared = hk != hq
    wq, wk = hq * HEAD, hk * HEAD
    scale = HEAD ** -0.5

    def body(*refs):
        (q2_ref, qx_ref, kc_ref, kp_ref, vc_ref, vp_ref, do2_ref, dox_ref, l2_ref, lx_ref, e2_ref, ex_ref) = refs[:12]
        if sink is None:
            dq_ref, dk_ref, dv_ref = refs[12:]
        else:
            sk_ref, dq_ref, dk_ref, dv_ref, sacc_ref = refs[12:]
        i = pl.program_id(0)
        thr = lambda b: jnp.where(_first_flag(b, segs, nb), 1 << 20, BLK - max_dist)
        bias_a, bias_b = _band_bias(thr(2 * i), True), _band_bias(thr(2 * i + 1), True)
        bias_c = _band_bias(thr(2 * i + 2), False)
        if sink is not None:
            @pl.when(i == 0)
            def _():
                sacc_ref[...] = jnp.zeros_like(sacc_ref)

        def tile(q4, do4, l_cols, dlt, kd, vd, bias, width):
            s, dp = _dot_nt(q4, kd) * scale, _dot_nt(do4, vd)
            ps, dss = [], []
            for h in range(A_HEADS):
                seg = slice(h * width, (h + 1) * width)
                p = jnp.exp(s[:, seg] + bias - l_cols[h])
                ps.append(p)
                dss.append(p * (dp[:, seg] - dlt[:, h * HEAD:h * HEAD + 1]) * scale)
            return ps, dss

        cat = lambda parts: jnp.concatenate([t.astype(_MM) for t in parts], axis=1)
        for u in range(units):
            us = slice(u * A_W, (u + 1) * A_W)
            k_a = _unit_kv(((kp_ref, _LO), (kc_ref, _LO)), u, shared)
            v_a = _unit_kv(((vp_ref, _LO), (vc_ref, _LO)), u, shared)
            k_b, v_b = _unit_kv(((kc_ref, _BOTH),), u, shared), _unit_kv(((vc_ref, _BOTH),), u, shared)
            kd_a, vd_a, kd_b, vd_b = _blockdiag(k_a), _blockdiag(v_a), _blockdiag(k_b), _blockdiag(v_b)
            kd_c, vd_c = _blockdiag(k_b[BLK:]), _blockdiag(v_b[BLK:])
            qs = (q2_ref[_LO, us], q2_ref[_HI, us], qx_ref[:, us])
            dos = (do2_ref[_LO, us], do2_ref[_HI, us], dox_ref[:, us])
            lcols = [[ref[rs, u * A_W + h * HEAD:u * A_W + h * HEAD + 1] for h in range(A_HEADS)]
                     for ref, rs in ((l2_ref, _LO), (l2_ref, _HI), (lx_ref, _LO))]
            if sink is None:
                dlts = (e2_ref[_LO, us], e2_ref[_HI, us], ex_ref[:, us])
            else:
                dlts = tuple(_seg_sum64(d.astype(F32) * ref[rs, us])
                             for d, (ref, rs) in zip(dos, ((e2_ref, _LO), (e2_ref, _HI), (ex_ref, _LO))))
                for t in range(2):
                    for h in range(A_HEADS):
                        j = u * A_HEADS + h
                        sacc_ref[:, j:j + 1] += -jnp.exp(sk_ref[j] - lcols[t][h]) * dlts[t][:, h * HEAD:h * HEAD + 1]
            p_a, ds_a = tile(qs[0], dos[0], lcols[0], dlts[0], kd_a, vd_a, bias_a, 2 * BLK)
            p_b, ds_b = tile(qs[1], dos[1], lcols[1], dlts[1], kd_b, vd_b, bias_b, 2 * BLK)
            p_c, ds_c = tile(qs[2], dos[2], lcols[2], dlts[2], kd_c, vd_c, bias_c, BLK)
            dq_ref[_LO, us] = _dot(cat(ds_a), kd_a)
            dq_ref[_HI, us] = _dot(cat(ds_b), kd_b)
            outs = []
            for pa, pb, pc, lhs in ((ds_a, ds_b, ds_c, qs), (p_a, p_b, p_c, dos)):
                from_a = _fold_diag(_dot_tn(cat([t[:, BLK:] for t in pa]), lhs[0]), BLK)
                from_b = _fold_diag(_dot_tn(cat(pb), lhs[1]), 2 * BLK)
                from_c = _fold_diag(_dot_tn(cat(pc), lhs[2]), BLK)
                outs.append(jnp.concatenate([from_a + from_b[:BLK], from_b[BLK:] + from_c], axis=0))
            dk4, dv4 = outs
            if shared:
                fold = lambda t: (t[:, 0:HEAD] + t[:, HEAD:2 * HEAD]) + (t[:, 2 * HEAD:3 * HEAD] + t[:, 3 * HEAD:])
                dk_ref[:, u * HEAD:(u + 1) * HEAD] = fold(dk4)
                dv_ref[:, u * HEAD:(u + 1) * HEAD] = fold(dv4).astype(_MM)
            else:
                dk_ref[:, us] = dk4
                dv_ref[:, us] = dv4.astype(_MM)

    two = lambda w: pl.BlockSpec((2 * BLK, w), lambda i: (i, 0))
    prev = lambda w: pl.BlockSpec((BLK, w), lambda i: (jnp.maximum(2 * i - 1, 0), 0))
    nxt = lambda w: pl.BlockSpec((BLK, w), lambda i: (jnp.minimum(2 * i + 2, nb - 1), 0))
    in_specs = [two(wq), nxt(wq), two(wk), prev(wk), two(wk), prev(wk), two(wq), nxt(wq), two(wq), nxt(wq), two(wq), nxt(wq)]
    args = [qn, qn, kn, kn, vn, vn, do, do, lse, lse, dl_or_o, dl_or_o]
    out_specs = [two(wq), two(wk), two(wk)]
    out_shape = [_sds((rows, wq)), _sds((rows, wk)), _sds((rows, wk), _MM)]
    if sink is not None:
        in_specs.append(pl.BlockSpec(memory_space=pltpu.SMEM))
        args.append(sink)
        out_specs.append(_acc((BLK, 128)))
        out_shape.append(_sds((BLK, 128)))
    return _pc(body, name=name, grid=(nb // 2,), in_specs=in_specs, out_specs=out_specs, out_shape=out_shape)(*args)


def _k_prep_bwd(srcs, dqn, dkn, gq, gk, tabs, tab_row, *, wq, wk, rows_per_gain, name):
    rows = dqn.shape[0]
    ts = min(512, rows)
    ngain = gq.shape[0]

    def body(q_ref, k_ref, dq_ref, dk_ref, gq_ref, gk_ref, c_ref, sa_ref, sb_ref, oq_ref, ok_ref, aq_ref, ak_ref):
        i = pl.program_id(0)

        @pl.when(lax.rem(i * ts, rows_per_gain) == 0)
        def _():
            aq_ref[...] = jnp.zeros_like(aq_ref)
            ak_ref[...] = jnp.zeros_like(ak_ref)

        c, sa, sb = c_ref[...], sa_ref[...], sb_ref[...]
        for x_ref, d_ref, g_ref, o_ref, a_ref in ((q_ref, dq_ref, gq_ref, oq_ref, aq_ref),
                                                   (k_ref, dk_ref, gk_ref, ok_ref, ak_ref)):
            xh, r = _seg_norm(x_ref[...], HEAD)
            dt = _rope_bwd(d_ref[...], c, sa, sb)
            a_ref[...] += _sum8(dt * xh)
            z = dt * g_ref[...]
            o_ref[...] = (r * (z - xh * _seg_mean(z * xh, HEAD))).astype(_MM)

    gspec = lambda w: pl.BlockSpec((None, 1, w), lambda i: ((i * ts) // rows_per_gain, 0, 0))
    aspec = lambda w: pl.BlockSpec((None, 8, w), lambda i: ((i * ts) // rows_per_gain, 0, 0))
    return _pc(
        body, name=name, grid=(rows // ts,),
        in_specs=[_row(ts, wq, srcs[0][1]), _row(ts, wk, srcs[1][1]), _row(ts, wq), _row(ts, wk), gspec(wq), gspec(wk)]
        + [pl.BlockSpec((ts, 128), lambda i: (i + tab_row // ts, 0))] * 3,
        out_specs=[_row(ts, wq), _row(ts, wk), aspec(wq), aspec(wk)],
        out_shape=[_sds((rows, wq), _MM), _sds((rows, wk), _MM), _sds((ngain, 8, wq)), _sds((ngain, 8, wk))])(
            srcs[0][0], srcs[1][0], dqn, dkn, gq, gk, *tabs)


def _k_in_bwd(pieces, dgp, x, g1, dx1, w_in, w_gate):
    s = x.shape[0]
    ts = min(256, s)
    nin, ng = w_in.shape[2], w_gate.shape[2]
    widths = [p.shape[1] for p in pieces]
    ncol = sum(widths)

    def body(*refs):
        p_refs = refs[:len(pieces)]
        dgp_ref, x_ref, g_ref, dx1_ref, wi_ref, wg_ref, gx_ref, dpj_ref, gacc_ref = refs[len(pieces):]
        i = pl.program_id(0)

        @pl.when(i == 0)
        def _():
            gacc_ref[...] = jnp.zeros_like(gacc_ref)

        off = 0
        for p_ref, w in zip(p_refs, widths):
            dpj_ref[:, off:off + w] = p_ref[...]
            off += w
        dh = jnp.zeros((ts, D_MODEL), F32)
        for j in range(CHIPS):
            dh = dh + _dot_nt(dpj_ref[:, j * nin:(j + 1) * nin], wi_ref[j])
            dh = dh + _dot_nt(dgp_ref[:, j * ng:(j + 1) * ng], wg_ref[j])
        xh, r = _rms(x_ref[...])
        gacc_ref[...] += _sum8(dh * xh)
        gx_ref[...] = dx1_ref[...] + _rms_bwd(dh, xh, r, g_ref[...])

    return _pc(
        body, name="in_proj_bwd", grid=(s // ts,),
        in_specs=[_row(ts, w) for w in widths] + [_row(ts, CHIPS * ng), _row(ts, D_MODEL), _res((1, D_MODEL)),
                                                  _row(ts, D_MODEL), _res(w_in.shape), _res(w_gate.shape)],
        out_specs=[_row(ts, D_MODEL), _row(ts, ncol), _acc((8, D_MODEL))],
        out_shape=[_sds((s, D_MODEL)), _sds((s, ncol), _MM), _sds((8, D_MODEL))])(*pieces, dgp, x, g1, dx1, w_in, w_gate)


def _k_wgrad(a, b, *, nblk, stacked, name):
    s, k = a.shape
    n = b.shape[1]
    nb = n // nblk
    ts = min(2048 if k <= 1024 else 1024, s)

    def body(a_ref, b_ref, o_ref):
        @pl.when(pl.program_id(1) == 0)
        def _():
            o_ref[...] = jnp.zeros_like(o_ref)

        o_ref[...] += _dot_tn(a_ref[...], b_ref[...])

    if stacked:
        out_spec, out_shape = pl.BlockSpec((None, k, nb), lambda g, t: (g, 0, 0)), _sds((nblk, k, nb))
    else:
        out_spec, out_shape = pl.BlockSpec((k, nb), lambda g, t: (0, g)), _sds((k, n))
    return _pc(body, name=name, grid=(nblk, s // ts),
               in_specs=[pl.BlockSpec((ts, k), lambda g, t: (t, 0)), pl.BlockSpec((ts, nb), lambda g, t: (t, g))],
               out_specs=[out_spec], out_shape=[out_shape])(a, b)[0]


def _to_res(t, d):
    s, c = t.shape
    return t if d == 1 else t.reshape(s // d, d, c).transpose(1, 0, 2).reshape(s, c)


def _from_res(t, d):
    s, c = t.shape
    return t if d == 1 else t.reshape(d, s // d, c).transpose(1, 0, 2).reshape(s, c)


def _tile_gain(g, heads):
    return jnp.tile(g, (1,) * (g.ndim - 1) + (heads,))[..., None, :]


def _local_step(x, mem, pos, target, small, get_w_in, get_rest, on_grads):
    s = x.shape[0]
    nblk = s // BLK
    g1, g2 = small["attn_norm"], small["ffn_norm"]

    pos_rows = jnp.concatenate([_to_res(pos[:, None], d)[:, 0] for _, d in A_GROUPS] + [pos])
    tabs = _rope_tables(pos_rows)
    w_in = get_w_in(tabs[0])

    h, qa0, qa1, qa2, q_b, k_b, v_b, m_q = _k_in(x, g1, w_in)

    qkv_a = jnp.concatenate([_to_res(t, d) for t, (_, d) in zip((qa0, qa1, qa2), A_GROUPS)], axis=0)
    gq_a = _tile_gain(small["a_q_norm"], A_HEADS)
    gk_a = _tile_gain(small["a_k_norm"], A_HEADS)
    src_a = ((qkv_a, 0), (qkv_a, 1), (qkv_a, 2))
    qn_a, kn_a, vn_a = _k_prep(src_a, gq_a, gk_a, tabs, 0, wq=A_W, wk=A_W, rows_per_gain=s, name="prep_a")
    segs_a = tuple((gi * nblk, nblk // d) for gi, (_, d) in enumerate(A_GROUPS))
    o_res, l_res = _k_band_fwd(qn_a, kn_a, vn_a, hq=A_HEADS, hk=A_HEADS, max_dist=BLK, segs=segs_a, sink=None,
                               name="attn_a")
    og = [_from_res(o_res[gi * s:(gi + 1) * s], d) for gi, (_, d) in enumerate(A_GROUPS)]
    lg = [_from_res(l_res[gi * s:(gi + 1) * s], d) for gi, (_, d) in enumerate(A_GROUPS)]

    gq_b = _tile_gain(small["b_q_norm"], B_QH)
    gk_b = _tile_gain(small["b_k_norm"], B_KVH)
    src_b = ((q_b, 0), (k_b, 0), (v_b, 0))
    qn_b, kn_b, vn_b = _k_prep(src_b, gq_b, gk_b, tabs, 3 * s, wq=B_QH * HEAD, wk=B_KVH * HEAD, rows_per_gain=s,
                               name="prep_b")
    sink_x = small["b_sinks"][0]
    segs_b = ((0, nblk),)
    o_b, l_b = _k_band_fwd(qn_b, kn_b, vn_b, hq=B_QH, hk=B_KVH, max_dist=B_WINDOW - 1, segs=segs_b, sink=sink_x,
                           name="attn_b")

    wts = get_rest(0, o_b)

    gq_m = _tile_gain(small["m_q_norm"], M_HEADS)[0]
    gk_m = _tile_gain(small["m_k_norm"], M_HEADS)[0]
    mem_n, kv, mk, mv = _k_memkv(mem, small["mem_norm"], wts["w_mem_kv"], gk_m)
    o_m = _k_mem_fwd(m_q, gq_m, mk, mv)

    o_a, merged, x1, h2, gates = _k_merge(og, lg, o_b, o_m, h, x, wts["w_gate"], small["b_gate"], wts["w_o_a"],
                                          wts["w_o_b"], wts["w_o_m"], wts["w_out"], g2)
    wts.update(get_rest(1, x1))
    u = _k_up(h2, wts["w_up"])
    dy, f, dc, loss_acc = _k_ffn(u, wts["conv_w"], small["conv_b"], wts["w_down"], wts["w_down"].T, x1, target)
    loss = (0.5 / D_MODEL) * jnp.sum(loss_acc)

    dx1, du, cacc, g2acc = _k_conv_bwd(dc, u, wts["conv_w"], wts["w_up"], x1, g2, dy)
    tok = on_grads({"w_up": _k_wgrad(h2, du, nblk=CHIPS, stacked=True, name="dw_up"),
                    "w_down": _k_wgrad(f, dy, nblk=2, stacked=False, name="dw_down").reshape(CHIPS, -1, D_MODEL)}, dx1)
    (dgp, dog0, dog1, dog2, dl0, dl1, dl2, do_b, do_m, bacc, dw_oa, dw_ob, dw_om, dw_out) = _k_merge_bwd(
        dx1, og, lg, o_a, o_b, o_m, gates, merged, wts["w_o_a"], wts["w_o_b"], wts["w_o_m"], wts["w_out"], tok)
    tok = on_grads({"w_gate": _k_wgrad(h, dgp, nblk=CHIPS, stacked=True, name="dw_gate"),
                    "w_o_a": dw_oa, "w_o_b": dw_ob, "w_o_m": dw_om, "w_out": dw_out.reshape(CHIPS, -1, D_MODEL)}, do_m)

    dq_m, dmk, dmv, gqm_acc = _k_mem_bwd(m_q, gq_m + tok[0:1, 0:1], mk, mv, o_m, do_m)
    dw_kv, gmem_acc, gkm_acc = _k_memkv_bwd(mem, small["mem_norm"], wts["w_mem_kv"], gk_m, mem_n, kv, dmk, dmv)

    dq_bn, dk_bn, dv_b, sacc = _k_band_bwd(qn_b, kn_b, vn_b, do_b, l_b, o_b, hq=B_QH, hk=B_KVH,
                                           max_dist=B_WINDOW - 1, segs=segs_b, sink=sink_x, name="attn_b_bwd")
    tok = on_grads({}, dq_bn)
    dq_b, dk_b, gqb_acc, gkb_acc = _k_prep_bwd(src_b, dq_bn, dk_bn, gq_b + tok[0:1, 0:1], gk_b, tabs, 3 * s, wq=B_QH * HEAD,
                                               wk=B_KVH * HEAD, rows_per_gain=s, name="prep_b_bwd")

    do_res = jnp.concatenate([_to_res(t, d) for t, (_, d) in zip((dog0, dog1, dog2), A_GROUPS)], axis=0)
    dl_res = jnp.concatenate([_to_res(t, d) for t, (_, d) in zip((dl0, dl1, dl2), A_GROUPS)], axis=0)
    dq_an, dk_an, dv_a = _k_band_bwd(qn_a, kn_a, vn_a, do_res, l_res, dl_res, hq=A_HEADS, hk=A_HEADS, max_dist=BLK,
                                     segs=segs_a, sink=None, name="attn_a_bwd")
    dq_a, dk_a, gqa_acc, gka_acc = _k_prep_bwd(src_a, dq_an, dk_an, gq_a, gk_a, tabs, 0, wq=A_W, wk=A_W,
                                               rows_per_gain=s, name="prep_a_bwd")
    pieces = []
    for gi, (_, d) in enumerate(A_GROUPS):
        rs = slice(gi * s, (gi + 1) * s)
        pieces += [_from_res(t[rs], d) for t in (dq_a, dk_a, dv_a)]
    pieces += [dq_b, dk_b, dv_b, dq_m]
    grad_x, dproj, g1acc = _k_in_bwd(pieces, dgp, x, g1, dx1, w_in, wts["w_gate"])
    on_grads({"w_in": _k_wgrad(h, dproj, nblk=CHIPS, stacked=True, name="dw_in"),
              "w_mem_kv": dw_kv.reshape(CHIPS, -1, 2 * M_W)}, grad_x)

    def fold(acc, heads):
        v = jnp.sum(acc, axis=-2)
        return jnp.sum(v.reshape(v.shape[:-1] + (heads, -1)), axis=-2)

    csum = jnp.sum(cacc, axis=1)
    sml = {
        "attn_norm": jnp.sum(g1acc, axis=0), "a_q_norm": fold(gqa_acc, A_HEADS), "a_k_norm": fold(gka_acc, A_HEADS),
        "b_q_norm": fold(gqb_acc[0], B_QH), "b_k_norm": fold(gkb_acc[0], B_KVH),
        "b_sinks": jnp.sum(sacc, axis=0)[:B_QH], "mem_norm": jnp.sum(gmem_acc, axis=0),
        "m_q_norm": fold(gqm_acc, M_HEADS), "m_k_norm": fold(gkm_acc, M_HEADS),
        "b_gate": jnp.sum(bacc, axis=0), "ffn_norm": jnp.sum(g2acc, axis=0),
        "conv_w": csum[1:], "conv_b": csum[0],
    }
    return loss, grad_x, sml


def _mesh_pos():
    return lax.axis_index("x"), lax.axis_index("y"), lax.axis_index("c")


def _chip_peers(x, y):
    return [(1 - x, y), (x, 1 - y), (1 - x, 1 - y)]


_ANY = pl.BlockSpec(memory_space=pl.ANY)


def _comm_call(body, *, name, n_in, out_shape, scratch):
    return pl.pallas_call(body, name=name, in_specs=[_ANY] * n_in, out_specs=[_ANY] * len(out_shape),
                          out_shape=out_shape, scratch_shapes=scratch)


def _remote(src, dst, send_sem, recv_sem, dev):
    return pltpu.make_async_remote_copy(src_ref=src, dst_ref=dst, send_sem=send_sem, recv_sem=recv_sem,
                                        device_id=dev, device_id_type=MESH)


def _pair_join(halves, name):
    nt = len(halves)

    def body(*refs):
        ins, got = refs[:nt], refs[nt:2 * nt]
        send_sems, recv_sems = refs[2 * nt:]
        x, y, c = _mesh_pos()
        cps = []
        for t in range(nt):
            rc = _remote(ins[t], got[t], send_sems.at[t], recv_sems.at[t], (x, y, 1 - c))
            rc.start()
            cps.append(rc)
        for rc in cps:
            rc.wait()

    out_shape = [_sds(hf.shape, hf.dtype) for hf in halves]
    scratch = [pltpu.SemaphoreType.DMA((nt,)), pltpu.SemaphoreType.DMA((nt,))]
    return _comm_call(body, name=name, n_in=nt, out_shape=out_shape, scratch=scratch)(*halves)


_HBM = pl.BlockSpec(memory_space=pltpu.HBM)
_SEMS = pl.BlockSpec(memory_space=pltpu.SEMAPHORE)
_EFFECT = pltpu.SideEffectType.DATAFLOW_SIDE_EFFECTING


def _bcast_copies(ins, lands, send_sems, recv_sems):
    x, y, c = _mesh_pos()
    me = 2 * x + y
    targets = [((px, py, c), 2 * px + py) for px, py in _chip_peers(x, y)] + [((x, y, 1 - c), me)]
    out = []
    for t in range(len(ins)):
        for k, (dev, idx) in enumerate(targets):
            i = t * len(targets) + k
            arrival = lambda t=t, i=i, idx=idx, dev=dev: _remote(ins[t], lands[t].at[idx], send_sems.at[i],
                                                                 recv_sems.at[i], dev)
            out.append((_remote(ins[t], lands[t].at[me], send_sems.at[i], recv_sems.at[i], dev), arrival))
    return out


def _scatter_copies(ins, lands, send_sems, recv_sems):
    x, y, c = _mesh_pos()
    out = []
    for t in range(len(ins)):
        for k, (px, py) in enumerate(_chip_peers(x, y)):
            i = t * 3 + k
            cp = _remote(ins[t].at[2 * px + py], lands[t].at[k], send_sems.at[i], recv_sems.at[i], (px, py, c))
            out.append((cp, lambda cp=cp: cp))
    return out


def _pair_copies(ins, lands, send_sems, recv_sems):
    x, y, c = _mesh_pos()
    out = []
    for t in range(len(ins)):
        hr = ins[t].shape[1] // 2
        give = ins[t].at[:, pl.ds(pl.multiple_of((1 - c) * hr, 8), hr), :]
        cp = _remote(give, lands[t], send_sems.at[t], recv_sems.at[t], (x, y, 1 - c))
        out.append((cp, lambda cp=cp: cp))
    return out


def _join_copies(ins, lands, send_sems, recv_sems):
    x, y, c = _mesh_pos()
    out = []
    for t in range(len(ins)):
        cp = _remote(ins[t], lands[t], send_sems.at[t], recv_sems.at[t], (x, y, 1 - c))
        out.append((cp, lambda cp=cp: cp))
    return out


def _half_copies(ins, lands, send_sems, recv_sems):
    x, y, c = _mesh_pos()
    me = 2 * x + y
    out = []
    for t in range(len(ins)):
        hr = ins[t].shape[0] // 2
        rows = pl.ds(pl.multiple_of(c * hr, 8), hr)
        for k, (px, py) in enumerate(_chip_peers(x, y)):
            i = t * 3 + k
            arrival = lambda t=t, i=i, px=px, py=py, rows=rows: _remote(
                ins[t].at[rows, :], lands[t].at[2 * px + py].at[rows, :], send_sems.at[i], recv_sems.at[i], (px, py, c))
            out.append((_remote(ins[t].at[rows, :], lands[t].at[me].at[rows, :], send_sems.at[i], recv_sems.at[i],
                                (px, py, c)), arrival))
    return out


def _finish_halves(shards, stacks):
    nt = len(shards)

    def body(*refs):
        ins, held, outs = refs[:nt], refs[nt:2 * nt], refs[2 * nt:3 * nt]
        fwd_s, fwd_r, own_s, own_r = refs[3 * nt:]
        x, y, c = _mesh_pos()
        me = 2 * x + y
        sib = (x, y, 1 - c)
        pending = []
        for t in range(nt):
            hr = shards[t].shape[0] // 2
            half = lambda ref, who: ref.at[pl.ds(pl.multiple_of(who * hr, 8), hr), :]
            own = _remote(ins[t], outs[t].at[me], own_s.at[t], own_r.at[t], sib)
            own.start()
            pending.append(own.wait)
            for k, (px, py) in enumerate(_chip_peers(x, y)):
                pj = 2 * px + py
                fw = _remote(half(held[t].at[pj], c), half(outs[t].at[pj], c), fwd_s.at[t, k], fwd_r.at[t, k], sib)
                fw.start()
                pending.append(fw.wait_send)
                other = half(outs[t].at[pj], 1 - c)
                pending.append(_remote(other, other, fwd_s.at[t, k], fwd_r.at[t, k], sib).wait_recv)
        for wait in pending:
            wait()

    dma = pltpu.SemaphoreType.DMA
    return pl.pallas_call(
        body, name="gather_w_in_finish", in_specs=[_ANY] * (2 * nt), out_specs=[_ANY] * nt,
        out_shape=[_sds(a.shape, a.dtype) for a in stacks], input_output_aliases={nt + i: i for i in range(nt)},
        scratch_shapes=[dma((nt, 3)), dma((nt, 3)), dma((nt,)), dma((nt,))])(*shards, *stacks)


def _split_start(copies, srcs, land_shapes, ncopy, dep, name, lands=None):
    nt = len(srcs)

    def body(*refs):
        ins, lands = refs[:nt], refs[nt:2 * nt]
        send_sems, recv_sems, token = refs[2 * nt + 1], refs[2 * nt + 2], refs[-1]
        for send, _ in copies(ins, lands, send_sems, recv_sems):
            send.start()
        token[...] = jnp.zeros_like(token)

    if lands is None:
        lands = [lax.empty(sh, a.dtype) for sh, a in zip(land_shapes, srcs)]
    lands = [pltpu.with_memory_space_constraint(a, pltpu.HBM) for a in lands]
    srcs = [pltpu.with_memory_space_constraint(a, pltpu.HBM) for a in srcs]
    dma = pltpu.SemaphoreType.DMA
    out_shape = ([dma((nt * ncopy,)), dma((nt * ncopy,))] + [pltpu.HBM(a.shape, a.dtype) for a in srcs + lands]
                 + [_sds((8, 128))])
    outs = pl.pallas_call(
        body, name=name, in_specs=[_HBM] * (2 * nt) + [_ANY],
        out_specs=[_SEMS, _SEMS] + [_HBM] * (2 * nt) + [pl.BlockSpec(memory_space=pltpu.VMEM)], out_shape=out_shape,
        input_output_aliases={i: 2 + i for i in range(2 * nt)},
        compiler_params=pltpu.CompilerParams(has_side_effects=_EFFECT))(*srcs, *lands, dep)
    return outs[0], outs[1], outs[2:2 + nt], outs[2 + nt:2 + 2 * nt], outs[-1]


def _split_wait(copies, send_sems, recv_sems, srcs, lands, after, name):
    nt = len(srcs)

    def body(*refs):
        ins, lnd = refs[:nt], refs[nt:2 * nt]
        for send, arrival in copies(ins, lnd, refs[2 * nt], refs[2 * nt + 1]):
            send.wait_send()
            arrival().wait_recv()

    outs = pl.pallas_call(
        body, name=name, in_specs=[_HBM] * (2 * nt) + [_SEMS, _SEMS, _ANY], out_specs=[_HBM] * (2 * nt),
        out_shape=[pltpu.HBM(a.shape, a.dtype) for a in list(srcs) + list(lands)],
        input_output_aliases={i: i for i in range(2 * nt)},
        compiler_params=pltpu.CompilerParams(has_side_effects=_EFFECT))(*srcs, *lands, send_sems, recv_sems, after)
    return outs[:nt], outs[nt:]


def _small_copies(ins, lands, send_sems, recv_sems):
    x, y, c = _mesh_pos()
    me = 4 * x + 2 * y + c
    out = []
    for k in range(1, NDEV):
        px, py, pc = x ^ (k >> 2), y ^ ((k >> 1) & 1), c ^ (k & 1)
        arrival = lambda k=k, px=px, py=py, pc=pc: _remote(ins[0], lands[0].at[4 * px + 2 * py + pc], send_sems.at[k - 1],
                                                            recv_sems.at[k - 1], (px, py, pc))
        out.append((_remote(ins[0], lands[0].at[me], send_sems.at[k - 1], recv_sems.at[k - 1], (px, py, pc)), arrival))
    return out


def _row_tile(r, c, mib=1):
    t = r
    while t * c * 4 > (mib << 20) and t % 16 == 0:
        t //= 2
    return t


def _k_pair_add(full, got, name):
    g, r, c = full.shape
    hr = r // 2
    tr = _row_tile(hr, c, 4)
    nh = hr // tr

    def body(a_ref, b_ref, o_ref):
        o_ref[...] = (a_ref[...] + b_ref[...]).astype(_WIRE)

    mine = pl.BlockSpec((None, tr, c), lambda i, j: (i, lax.axis_index("c") * nh + j, 0))
    spec = pl.BlockSpec((None, tr, c), lambda i, j: (i, j, 0))
    return _pc(body, name=name, grid=(g, nh), in_specs=[mine, spec], out_specs=[spec],
               out_shape=[_sds((g, hr, c), _WIRE)])(full, got)[0]


def _k_chip_sum(parts, slots, name):
    _, r, c = parts.shape
    tr = _row_tile(r, c, 4)

    def body(a_ref, s_ref, o_ref):
        acc = a_ref[...].astype(F32)
        for k in range(3):
            acc = acc + s_ref[k].astype(F32)
        o_ref[...] = acc

    own = pl.BlockSpec((None, tr, c), lambda i: (2 * lax.axis_index("x") + lax.axis_index("y"), i, 0))
    return _pc(body, name=name, grid=(r // tr,), in_specs=[own, pl.BlockSpec((3, tr, c), lambda i: (0, i, 0))],
               out_specs=[_row(tr, c)], out_shape=[_sds((r, c))])(parts, slots)[0]


def _adam(w, g, m, v):
    m = ADAM_B1 * m + (1.0 - ADAM_B1) * g
    v = ADAM_B2 * v + (1.0 - ADAM_B2) * (g * g)
    m_hat = m / (1.0 - ADAM_B1 ** ADAM_STEP)
    v_hat = v / (1.0 - ADAM_B2 ** ADAM_STEP)
    return -ADAM_LR * (m_hat / (jnp.sqrt(v_hat) + ADAM_EPS) + ADAM_WD * w), m, v


def _k_adam(w, mine, theirs, m, v, dep, name):
    r, c = w.shape
    hr = r // 2
    tr = _row_tile(hr, c, 2)
    nh = hr // tr

    def body(w_ref, a_ref, b_ref, m_ref, v_ref, dep_ref, g_ref, d_ref, mo_ref, vo_ref):
        upper = (pl.program_id(0) >= nh).astype(jnp.int32)
        g = jnp.where(upper == lax.axis_index("c"), a_ref[...], b_ref[...])
        g_ref[...] = g
        d_ref[...], mo_ref[...], vo_ref[...] = _adam(w_ref[...], g, m_ref[...], v_ref[...])

    hspec = pl.BlockSpec((tr, c), lambda i: (jnp.where(i >= nh, i - nh, i), 0))
    return _pc(body, name=name, grid=(r // tr,),
               in_specs=[_row(tr, c), hspec, hspec, _row(tr, c), _row(tr, c), _res((8, 128))],
               out_specs=[_row(tr, c)] * 4, out_shape=[_sds((r, c))] * 4)(w, mine, theirs, m, v, dep)


def _k_sum8(a):
    _, n, _ = a.shape

    def body(a_ref, o_ref):
        acc = a_ref[0]
        for k in range(1, NDEV):
            acc = acc + a_ref[k]
        o_ref[...] = acc

    return _pc(body, name="sum_small_grads", grid=(1,), in_specs=[_acc(a.shape)], out_specs=[_acc((n, 128))],
               out_shape=[_sds((n, 128))])(a)[0]


def _k_adam_small(ws, gs, ms, vs):
    n = len(ws)

    def body(*refs):
        for k in range(n):
            w_ref, g_ref, m_ref, v_ref, d_ref, mo_ref, vo_ref = refs[k::n]
            d_ref[...], mo_ref[...], vo_ref[...] = _adam(w_ref[...], g_ref[...], m_ref[...], v_ref[...])

    specs = [_acc(a.shape) for a in ws]
    outs = _pc(body, name="adam_small", grid=(1,), in_specs=specs * 4, out_specs=specs * 3,
               out_shape=[_sds(a.shape) for a in ws] * 3)(*ws, *gs, *ms, *vs)
    return outs[:n], outs[n:2 * n], outs[2 * n:]


def _pack(vals):
    rows = []
    for a in vals:
        flat = a.reshape(-1)
        n = -(-flat.shape[0] // 1024) * 1024
        rows.append(jnp.pad(flat, (0, n - flat.shape[0])).reshape(n // 128, 128))
    return jnp.concatenate(rows, axis=0)


def _unpack(packed, shapes):
    out, off = [], 0
    for sh in shapes:
        size = int(np.prod(sh))
        n = -(-size // 1024) * 1024
        out.append(packed[off // 128:(off + n) // 128].reshape(-1)[:size].reshape(sh))
        off += n
    return out


_WEIGHTS = ["attn_norm", "w_in", "a_q_norm", "a_k_norm", "b_q_norm", "b_k_norm", "b_sinks", "mem_norm", "w_mem_kv",
            "m_q_norm", "m_k_norm", "w_o_a", "w_o_b", "w_o_m", "w_gate", "b_gate", "w_out", "ffn_norm", "w_up",
            "conv_w", "conv_b", "w_down"]
_BIG = ["w_in", "w_mem_kv", "w_o_a", "w_o_b", "w_o_m", "w_gate", "w_out", "w_up", "w_down"]
_SMALL = [n for n in _WEIGHTS if n not in _BIG]


def kernel(x, mem, positions, attn_norm, w_in, a_q_norm, a_k_norm, b_q_norm, b_k_norm, b_sinks, mem_norm, w_mem_kv, m_q_norm, m_k_norm, w_o_a, w_o_b, w_o_m, w_gate, b_gate, w_out, ffn_norm, w_up, conv_w, conv_b, w_down, loss_target, m_attn_norm, m_w_in, m_a_q_norm, m_a_k_norm, m_b_q_norm, m_b_k_norm, m_b_sinks, m_mem_norm, m_w_mem_kv, m_m_q_norm, m_m_k_norm, m_w_o_a, m_w_o_b, m_w_o_m, m_w_gate, m_b_gate, m_w_out, m_ffn_norm, m_w_up, m_conv_w, m_conv_b, m_w_down, v_attn_norm, v_w_in, v_a_q_norm, v_a_k_norm, v_b_q_norm, v_b_k_norm, v_b_sinks, v_mem_norm, v_w_mem_kv, v_m_q_norm, v_m_k_norm, v_w_o_a, v_w_o_b, v_w_o_m, v_w_gate, v_b_gate, v_w_out, v_ffn_norm, v_w_up, v_conv_w, v_conv_b, v_w_down):
    given = dict(locals())
    w = {n: given[n][0] for n in _WEIGHTS}
    m1 = {n: given["m_" + n][0] for n in _WEIGHTS}
    m2 = {n: given["v_" + n][0] for n in _WEIGHTS}

    zeros = jnp.zeros((8, 128), F32)
    w_in_shard = w["w_in"].astype(_MM)
    *w_in_handles, tok = _split_start(_half_copies, [w_in_shard], [(CHIPS,) + w_in_shard.shape], 3, zeros,
                                      "gather_w_in_start")

    def get_w_in(after):
        send, recv, srcs, lands = w_in_handles
        srcs, lands = _split_wait(_half_copies, send, recv, srcs, lands, after, "gather_w_in_wait")
        return _finish_halves(srcs, lands)[0]

    stages = (["w_gate", "w_mem_kv", "w_o_a", "w_o_b", "w_o_m", "w_out"], ["w_up", "w_down", "conv_w"])
    started = []
    for k, names in enumerate(stages):
        shards = [w[n] if n == "conv_w" else w[n].astype(_MM) for n in names]
        *handles, tok = _split_start(_bcast_copies, shards, [(CHIPS,) + a.shape for a in shards], 4, tok,
                                     "gather_start_%d" % k)
        started.append(handles)
    small = {n: (w[n][None, :] if w[n].ndim == 1 else w[n]) for n in _SMALL if n != "conv_w"}
    positions = positions + tok[0:1, 0:1].astype(positions.dtype)

    def get_rest(stage, after):
        send, recv, srcs, lands = started[stage]
        got = _split_wait(_bcast_copies, send, recv, srcs, lands, after, "gather_wait_%d" % stage)[1]
        wts = dict(zip(stages[stage], got))
        for n in ("w_mem_kv", "w_out", "w_down"):
            if n in wts:
                wts[n] = wts[n].reshape(-1, wts[n].shape[-1])
        return wts

    parts, slots, pair, scat, started_pair = {}, {}, [], [], [None]

    def finish_pair(after):
        names, tag, send, recv, srcs, lands = pair.pop()
        full, got = _split_wait(_pair_copies, send, recv, srcs, lands, after, "pair_wait_" + tag)
        mine = [_k_pair_add(f, b, "pair_add_" + n) for n, f, b in zip(names, full, got)]
        shapes = [(3,) + p.shape[1:] for p in mine]
        send, recv, srcs, lands, token = _split_start(_scatter_copies, mine, shapes, 3, zeros, "scatter_start_" + tag)
        scat.append((names, tag, send, recv, srcs, lands))
        return token

    def on_grads(group, after):
        names = list(group)
        tag = "_".join(names)
        token = finish_pair(after) if pair else zeros
        if not group:
            return token
        grads_g = [group[n] for n in names]
        shapes = [(CHIPS, g.shape[1] // 2, g.shape[2]) for g in grads_g]
        send, recv, srcs, lands, token = _split_start(_pair_copies, grads_g, shapes, 1, token, "pair_start_" + tag)
        pair.append((names, tag, send, recv, srcs, lands))
        started_pair[0] = token
        return token

    loss, grad_x, sml = _local_step(x[0], mem[0], positions[0], loss_target[0], small, get_w_in, get_rest, on_grads)

    packed = _pack([sml[n] for n in _SMALL] + [loss.reshape(1)])
    me = 4 * lax.axis_index("x") + 2 * lax.axis_index("y") + lax.axis_index("c")
    land = lax.dynamic_update_slice(jnp.zeros((NDEV,) + packed.shape, F32), packed[None], (me, 0, 0))
    *small_h, tok = _split_start(_small_copies, [packed], None, NDEV - 1, zeros, "gather_small_start", lands=[land])
    started_pair[0] = started_pair[0] + tok

    early = [n for names, *_ in scat for n in names]
    for names, tag, send, recv, srcs, lands in scat:
        mine, got = _split_wait(_scatter_copies, send, recv, srcs, lands, started_pair[0], "scatter_wait_" + tag)
        parts.update(zip(names, mine))
        slots.update(zip(names, got))
    scat.clear()
    reduced = {n: _k_chip_sum(parts[n], slots[n], "chip_add_" + n) for n in early}
    halves = [reduced[n] for n in early]
    *join, tok = _split_start(_join_copies, halves, [a.shape for a in halves], 1, zeros, "pair_join_start_early")
    grads = {}

    delta, new_m, new_v = {}, {}, {}
    dep = finish_pair(tok)
    mine, got = _split_wait(_join_copies, *join, dep, "pair_join_wait_early")
    reduced.update(zip(early, mine))
    theirs = dict(zip(early, got))
    for n in early:
        grads[n], delta[n], new_m[n], new_v[n] = _k_adam(w[n], reduced[n], theirs[n], m1[n], m2[n], dep, "adam_" + n)
        dep = delta[n]
    gathered = _split_wait(_small_copies, *small_h, dep, "gather_small_wait")[1][0]
    shapes = [sml[n].shape for n in _SMALL] + [(1,)]
    *gsmall, loss = _unpack(_k_sum8(gathered), shapes)
    loss = loss[0]
    gsm = dict(zip(_SMALL, gsmall))
    nu = w["conv_w"].shape[1]
    chip = 2 * lax.axis_index("x") + lax.axis_index("y")
    gsm["conv_w"] = lax.dynamic_slice_in_dim(gsm["conv_w"], chip * nu, nu, axis=1)
    for n in _SMALL:
        grads[n] = gsm[n].reshape(w[n].shape)
    as2d = lambda d: [d[n][None, :] if d[n].ndim == 1 else d[n] for n in _SMALL]
    for dst, outs in zip((delta, new_m, new_v), _k_adam_small(as2d(w), as2d(grads), as2d(m1), as2d(m2))):
        dst.update((n, a.reshape(w[n].shape)) for n, a in zip(_SMALL, outs))
    late, tag, send, recv, srcs, lands = scat.pop()
    mine, got = _split_wait(_scatter_copies, send, recv, srcs, lands, dep, "scatter_wait_" + tag)
    for n, a, b in zip(late, mine, got):
        reduced[n] = _k_chip_sum(a, b, "chip_add_" + n)
    theirs.update(zip(late, _pair_join([reduced[n] for n in late], "grad_pair_join_late")))
    for n in late:
        grads[n], delta[n], new_m[n], new_v[n] = _k_adam(w[n], reduced[n], theirs[n], m1[n], m2[n], zeros, "adam_" + n)

    lead = lambda d: [d[n][None] for n in _WEIGHTS]
    return (loss, grad_x[None], *lead(grads), *lead(delta), *lead(new_m), *lead(new_v))
```

```python
import math

import jax
import jax.numpy as jnp
import numpy as np
from jax import lax
from jax.experimental import pallas as pl
from jax.experimental.pallas import tpu as pltpu

F32 = jnp.float32
_MM = jnp.bfloat16
_WIRE = jnp.bfloat16

D_MODEL = 1024
HEAD = 64
BLK = 128
A_GROUPS = ((128, 1), (512, 4), (2048, 16))
A_HEADS = 4
A_W = A_HEADS * HEAD
B_QH = 8
B_KVH = 2
B_WINDOW = 128
M_HEADS = 4
M_HD = 128
M_W = M_HEADS * M_HD
D_FF = 2816
EPS = 1e-6
NEG = -1e30
ROPE_THETA = 500000.0
ROPE_ROT = 16
CHIPS = 4
NDEV = 8
ADAM_LR, ADAM_B1, ADAM_B2, ADAM_EPS, ADAM_WD, ADAM_STEP = 0.001, 0.9, 0.999, 1e-08, 0.01, 10
VMEM_LIMIT = 58 * 1024 * 1024
MESH = pl.DeviceIdType.MESH


def _pc(body, *, name, grid, in_specs, out_specs, out_shape, scratch=()):
    return pl.pallas_call(
        body, name=name, grid=grid, in_specs=in_specs, out_specs=out_specs, out_shape=out_shape,
        scratch_shapes=list(scratch),
        compiler_params=pltpu.CompilerParams(dimension_semantics=("arbitrary",) * len(grid),
                                             vmem_limit_bytes=VMEM_LIMIT))


def _row(ts, c, col=0):
    return pl.BlockSpec((ts, c), lambda i: (i, col))


def _res(shape):
    n = len(shape)
    return pl.BlockSpec(tuple(shape), lambda i: (0,) * n, pipeline_mode=pl.Buffered(1))


def _acc(shape):
    n = len(shape)
    return pl.BlockSpec(tuple(shape), lambda i: (0,) * n)


def _sds(shape, dtype=F32):
    return jax.ShapeDtypeStruct(tuple(shape), dtype)


def _dot(a, b):
    return jnp.dot(a.astype(_MM), b.astype(_MM), preferred_element_type=F32)


def _dot_nt(a, b):
    return lax.dot_general(a.astype(_MM), b.astype(_MM), (((1,), (1,)), ((), ())), preferred_element_type=F32)


def _dot_tn(a, b):
    return lax.dot_general(a.astype(_MM), b.astype(_MM), (((0,), (0,)), ((), ())), preferred_element_type=F32)


def _sum8(v):
    ts, c = v.shape
    return jnp.sum(v.reshape(ts // 8, 8, c), axis=0)


def _sigmoid(z):
    return 1.0 / (1.0 + jnp.exp(-z))


def _rms(x):
    r = lax.rsqrt(jnp.mean(x * x, axis=-1, keepdims=True) + EPS)
    return x * r, r


def _rms_bwd(dy, xh, r, gain):
    z = dy * gain
    return r * (z - xh * jnp.mean(z * xh, axis=-1, keepdims=True))


def _split_hi_lo(v):
    hi = v.astype(_MM)
    return hi, (v - hi.astype(F32)).astype(_MM)


def _lane_head(shape):
    return lax.shift_right_logical(lax.broadcasted_iota(jnp.int32, shape, len(shape) - 1), 6)


def _seg_sum64(v):
    w = v.shape[1]
    e = jnp.where(_lane_head((w, w)) == lax.shift_right_logical(lax.broadcasted_iota(jnp.int32, (w, w), 0), 6),
                  1.0, 0.0).astype(_MM)
    hi, lo = _split_hi_lo(v)
    return jnp.dot(hi, e, preferred_element_type=F32) + jnp.dot(lo, e, preferred_element_type=F32)


def _seg_norm(x, seg):
    if seg == HEAD:
        r = lax.rsqrt(_seg_sum64(x * x) * (1.0 / HEAD) + EPS)
        return x * r, r
    w = x.shape[1]
    xh, rr = [], []
    for s in range(w // seg):
        xs = x[:, s * seg:(s + 1) * seg]
        r = lax.rsqrt(jnp.mean(xs * xs, axis=-1, keepdims=True) + EPS)
        xh.append(xs * r)
        rr.append(jnp.broadcast_to(r, xs.shape))
    return jnp.concatenate(xh, axis=1), jnp.concatenate(rr, axis=1)


def _seg_mean(v, seg):
    if seg == HEAD:
        return _seg_sum64(v) * (1.0 / HEAD)
    w = v.shape[1]
    out = []
    for s in range(w // seg):
        vs = v[:, s * seg:(s + 1) * seg]
        out.append(jnp.broadcast_to(jnp.mean(vs, axis=-1, keepdims=True), vs.shape))
    return jnp.concatenate(out, axis=1)


def _rope(t, c, sa, sb):
    out = []
    for cb in range(t.shape[1] // 128):
        tc = t[:, cb * 128:(cb + 1) * 128]
        out.append(tc * c + pltpu.roll(tc, 120, 1) * sa + pltpu.roll(tc, 8, 1) * sb)
    return jnp.concatenate(out, axis=1) if len(out) > 1 else out[0]


def _rope_bwd(dy, c, sa, sb):
    out = []
    for cb in range(dy.shape[1] // 128):
        dc = dy[:, cb * 128:(cb + 1) * 128]
        out.append(dc * c + pltpu.roll(dc * sa, 8, 1) + pltpu.roll(dc * sb, 120, 1))
    return jnp.concatenate(out, axis=1) if len(out) > 1 else out[0]


def _rope_consts():
    half = ROPE_ROT // 2
    c = np.float32(-2.0 * math.log(ROPE_THETA) / ROPE_ROT)
    freqs = np.exp(np.arange(half, dtype=np.float32) * c).astype(np.float32)
    place = np.zeros((3, half, 128), np.float32)
    ones = np.zeros((1, 128), np.float32)
    for lane in range(128):
        d = lane % HEAD
        if d < half:
            place[0, d, lane], place[1, d, lane] = 1.0, -1.0
        elif d < ROPE_ROT:
            place[0, d - half, lane], place[2, d - half, lane] = 1.0, 1.0
        else:
            ones[0, lane] = 1.0
    return np.tile(freqs[:, None], (1, 128)), place, ones


def _rope_tables(pos_rows):
    r = pos_rows.shape[0]
    tr = min(1024, r)
    freqs, place, ones = _rope_consts()

    def split3(v):
        hi, mid = _split_hi_lo(v)
        lo = (v - hi.astype(F32) - mid.astype(F32)).astype(_MM)
        return hi, mid, lo

    def body(p_ref, f_ref, e_ref, one_ref, c_ref, sa_ref, sb_ref):
        ang = jnp.concatenate([p_ref[j:j + 1, :].astype(F32) * f_ref[...] for j in range(tr // 128)], axis=1)
        cos, sin = jnp.cos(ang), jnp.sin(ang)
        for ref, k, v in ((c_ref, 0, cos), (sa_ref, 1, sin), (sb_ref, 2, sin)):
            e = e_ref[k].astype(_MM)
            out = sum(_dot_tn(part, e) for part in split3(v))
            ref[...] = out + one_ref[...] if k == 0 else out

    return _pc(body, name="rope_tables", grid=(r // tr,),
               in_specs=[pl.BlockSpec((tr // 128, 128), lambda i: (i, 0)), _acc((ROPE_ROT // 2, 128)),
                         _acc((3, ROPE_ROT // 2, 128)), _acc((1, 128))],
               out_specs=[_row(tr, 128)] * 3, out_shape=[_sds((r, 128))] * 3)(
                   pos_rows.reshape(r // 128, 128), jnp.asarray(freqs), jnp.asarray(place), jnp.asarray(ones))


def _k_in(x, g1, w_in):
    s = x.shape[0]
    ts = min(512, s)
    nin = w_in.shape[2]
    ncol = CHIPS * nin
    a_cols = 3 * A_W
    offs = [0, a_cols, 2 * a_cols, 3 * a_cols, 3 * a_cols + B_QH * HEAD,
            3 * a_cols + (B_QH + B_KVH) * HEAD, 3 * a_cols + (B_QH + 2 * B_KVH) * HEAD, ncol]

    def body(x_ref, g_ref, wi_ref, h_ref, a0, a1, a2, qb, kb, vb, mq, p_scr):
        xh, _ = _rms(x_ref[...])
        h = (xh * g_ref[...]).astype(_MM)
        h_ref[...] = h
        for j in range(CHIPS):
            p_scr[:, j * nin:(j + 1) * nin] = jnp.dot(h, wi_ref[j], preferred_element_type=F32)
        for k, ref in enumerate((a0, a1, a2, qb, kb, vb, mq)):
            ref[...] = p_scr[:, offs[k]:offs[k + 1]]

    widths = [offs[k + 1] - offs[k] for k in range(7)]
    return _pc(
        body, name="in_proj", grid=(s // ts,),
        in_specs=[_row(ts, D_MODEL), _res((1, D_MODEL)), _res(w_in.shape)],
        out_specs=[_row(ts, D_MODEL)] + [_row(ts, w) for w in widths],
        out_shape=[_sds((s, D_MODEL), _MM)] + [_sds((s, w)) for w in widths],
        scratch=[pltpu.VMEM((ts, ncol), F32)])(x, g1, w_in)


def _k_prep(srcs, gq, gk, tabs, tab_row, *, wq, wk, rows_per_gain, name):
    rows = srcs[0][0].shape[0]
    ts = min(512, rows)

    def body(q_ref, k_ref, v_ref, gq_ref, gk_ref, c_ref, sa_ref, sb_ref, qn_ref, kn_ref, vn_ref):
        c, sa, sb = c_ref[...], sa_ref[...], sb_ref[...]
        qh, _ = _seg_norm(q_ref[...], HEAD)
        qn_ref[...] = _rope(qh * gq_ref[...], c, sa, sb).astype(_MM)
        kh, _ = _seg_norm(k_ref[...], HEAD)
        kn_ref[...] = _rope(kh * gk_ref[...], c, sa, sb).astype(_MM)
        vn_ref[...] = v_ref[...].astype(_MM)

    gspec = lambda w: pl.BlockSpec((None, 1, w), lambda i: ((i * ts) // rows_per_gain, 0, 0))
    return _pc(
        body, name=name, grid=(rows // ts,),
        in_specs=[_row(ts, wq, srcs[0][1]), _row(ts, wk, srcs[1][1]), _row(ts, wk, srcs[2][1]),
                  gspec(wq), gspec(wk)] + [pl.BlockSpec((ts, 128), lambda i: (i + tab_row // ts, 0))] * 3,
        out_specs=[_row(ts, wq), _row(ts, wk), _row(ts, wk)],
        out_shape=[_sds((rows, wq), _MM), _sds((rows, wk), _MM), _sds((rows, wk), _MM)])(
            srcs[0][0], srcs[1][0], srcs[2][0], gq, gk, *tabs)


def _first_flag(b, segs, nb):
    first = b >= nb
    for k, (start, period) in enumerate(segs):
        end = segs[k + 1][0] if k + 1 < len(segs) else nb
        first = first | ((b >= start) & (b < end) & (lax.rem(b - start, jnp.int32(period)) == 0))
    return first


def _band_bias(thr, with_cur):
    qi = lax.broadcasted_iota(jnp.int32, (BLK, BLK), 0)
    kj = lax.broadcasted_iota(jnp.int32, (BLK, BLK), 1)
    prev = jnp.where(kj >= qi + thr, 0.0, NEG)
    return jnp.concatenate([prev, jnp.where(kj <= qi, 0.0, NEG)], axis=1) if with_cur else prev


def _blockdiag(t4):
    head = _lane_head((1, A_W))
    return jnp.concatenate([t4 * jnp.where(head == h, 1.0, 0.0).astype(t4.dtype) for h in range(A_HEADS)], axis=0)


def _fold_diag(t, n):
    head = _lane_head((n, A_W))
    out = t[3 * n:4 * n]
    for h in (2, 1, 0):
        out = jnp.where(head == h, t[h * n:(h + 1) * n], out)
    return out


def _expand_heads(cols):
    n = cols[0].shape[0]
    head = _lane_head((n, A_W))
    out = jnp.broadcast_to(cols[3], (n, A_W))
    for h in (2, 1, 0):
        out = jnp.where(head == h, cols[h], out)
    return out


def _unit_kv(pieces, u, shared):
    cols = slice(u * HEAD, (u + 1) * HEAD) if shared else slice(u * A_W, (u + 1) * A_W)
    rows = [ref[rs, cols] for ref, rs in pieces]
    k = rows[0] if len(rows) == 1 else jnp.concatenate(rows, axis=0)
    return jnp.concatenate([k] * A_HEADS, axis=1) if shared else k


_LO, _HI, _BOTH = slice(0, BLK), slice(BLK, 2 * BLK), slice(0, 2 * BLK)


def _k_band_fwd(qn, kn, vn, *, hq, hk, max_dist, segs, sink, name):
    rows = qn.shape[0]
    nb = rows // BLK
    units = hq // A_HEADS
    shared = hk != hq
    wq, wk = hq * HEAD, hk * HEAD
    scale = HEAD ** -0.5

    def body(*refs):
        if sink is None:
            q_ref, kc_ref, kp_ref, vc_ref, vp_ref, o_ref, l_ref = refs
        else:
            q_ref, kc_ref, kp_ref, vc_ref, vp_ref, sk_ref, o_ref, l_ref = refs
        i = pl.program_id(0)
        for half, rs in enumerate((_LO, _HI)):
            bias = _band_bias(jnp.where(_first_flag(2 * i + half, segs, nb), 1 << 20, BLK - max_dist), True)
            kpieces = ((kp_ref, _LO), (kc_ref, _LO)) if half == 0 else ((kc_ref, _BOTH),)
            vpieces = ((vp_ref, _LO), (vc_ref, _LO)) if half == 0 else ((vc_ref, _BOTH),)
            for u in range(units):
                us = slice(u * A_W, (u + 1) * A_W)
                kb = _blockdiag(_unit_kv(kpieces, u, shared))
                vb = _blockdiag(_unit_kv(vpieces, u, shared))
                s_all = _dot_nt(q_ref[rs, us], kb) * scale
                ps, ls = [], []
                for h in range(A_HEADS):
                    s = s_all[:, h * 2 * BLK:(h + 1) * 2 * BLK] + bias
                    m = jnp.max(s, axis=-1, keepdims=True)
                    e = jnp.exp(s - m)
                    lse = m + jnp.log(jnp.sum(e, axis=-1, keepdims=True))
                    if sink is not None:
                        sk = sk_ref[u * A_HEADS + h]
                        mx = jnp.maximum(lse, sk)
                        lse = mx + jnp.log(jnp.exp(lse - mx) + jnp.exp(sk - mx))
                    ps.append((e * jnp.exp(m - lse)).astype(_MM))
                    ls.append(lse)
                o_ref[rs, us] = _dot(jnp.concatenate(ps, axis=1), vb)
                l_ref[rs, us] = _expand_heads(ls)

    two = lambda w: pl.BlockSpec((2 * BLK, w), lambda i: (i, 0))
    prev = lambda w: pl.BlockSpec((BLK, w), lambda i: (jnp.maximum(2 * i - 1, 0), 0))
    in_specs = [two(wq), two(wk), prev(wk), two(wk), prev(wk)]
    args = [qn, kn, kn, vn, vn]
    if sink is not None:
        in_specs.append(pl.BlockSpec(memory_space=pltpu.SMEM))
        args.append(sink)
    return _pc(body, name=name, grid=(nb // 2,), in_specs=in_specs, out_specs=[two(wq), two(wq)],
               out_shape=[_sds((rows, wq)), _sds((rows, wq))])(*args)


def _k_memkv(mem, mem_norm, w_kv, m_k_norm):
    n = mem.shape[0]

    def body(m_ref, g_ref, w_ref, gk_ref, mn_ref, kv_ref, mk_ref, mv_ref):
        mh, _ = _rms(m_ref[...])
        mn = (mh * g_ref[...]).astype(_MM)
        mn_ref[...] = mn
        kv = jnp.dot(mn, w_ref[...], preferred_element_type=F32)
        kv_ref[...] = kv
        kh, _ = _seg_norm(kv[:, :M_W], M_HD)
        mk_ref[...] = (kh * gk_ref[...]).astype(_MM)
        mv_ref[...] = kv[:, M_W:].astype(_MM)

    return _pc(body, name="mem_kv", grid=(1,),
               in_specs=[_acc((n, D_MODEL)), _acc((1, D_MODEL)), _acc(w_kv.shape), _acc((1, M_W))],
               out_specs=[_acc((n, D_MODEL)), _acc((n, 2 * M_W)), _acc((n, M_W)), _acc((n, M_W))],
               out_shape=[_sds((n, D_MODEL), _MM), _sds((n, 2 * M_W)), _sds((n, M_W), _MM), _sds((n, M_W), _MM)])(
                   mem, mem_norm, w_kv, m_k_norm)


def _mem_probs(q, mk):
    sc = _dot_nt(q, mk) * (M_HD ** -0.5)
    e = jnp.exp(sc - jnp.max(sc, axis=-1, keepdims=True))
    return e / jnp.sum(e, axis=-1, keepdims=True)


def _k_mem_fwd(m_q, gq, mk, mv):
    s = m_q.shape[0]
    n = mk.shape[0]
    ts = min(512, s)

    def body(q_ref, g_ref, mk_ref, mv_ref, o_ref):
        qh, _ = _seg_norm(q_ref[...], M_HD)
        qn = (qh * g_ref[...]).astype(_MM)
        for h in range(M_HEADS):
            hs = slice(h * M_HD, (h + 1) * M_HD)
            o_ref[:, hs] = _dot(_mem_probs(qn[:, hs], mk_ref[:, hs]), mv_ref[:, hs])

    return _pc(body, name="mem_attn", grid=(s // ts,),
               in_specs=[_row(ts, M_W), _res((1, M_W)), _res((n, M_W)), _res((n, M_W))],
               out_specs=[_row(ts, M_W)], out_shape=[_sds((s, M_W))])(m_q, gq, mk, mv)[0]


def _group_weights(l0, l1, l2):
    m = jnp.maximum(jnp.maximum(l0, l1), l2)
    e0, e1, e2 = jnp.exp(l0 - m), jnp.exp(l1 - m), jnp.exp(l2 - m)
    inv = 1.0 / (e0 + e1 + e2)
    return e0 * inv, e1 * inv, e2 * inv


def _branch_products(oa, ob, om, woa_ref, wob_ref, wom_ref, j):
    return _dot(oa, woa_ref[j]), _dot(ob, wob_ref[j]), _dot(om, wom_ref[j])


def _k_merge(og, lg, o_b, o_m, h, x, w_gate, b_gate, w_oa, w_ob, w_om, w_out, g2):
    s = x.shape[0]
    ts = min(512, s)
    nc = w_oa.shape[2]
    ng = w_gate.shape[2]

    def body(o0, o1, o2, l0, l1, l2, ob_ref, om_ref, h_ref, x_ref, wg, bg_ref, woa, wob, wom, wout, g_ref,
             oa_ref, mer_ref, x1_ref, h2_ref, gt_ref, m_scr):
        h = h_ref[...]
        for j in range(CHIPS):
            z = jnp.dot(h, wg[j], preferred_element_type=F32) + bg_ref[:, j * ng:(j + 1) * ng]
            gt_ref[:, j * ng:(j + 1) * ng] = _sigmoid(z)
        w0, w1, w2 = _group_weights(l0[...], l1[...], l2[...])
        oa = w0 * o0[...] + w1 * o1[...] + w2 * o2[...]
        oa_ref[...] = oa
        ob, om = ob_ref[...], om_ref[...]
        for j in range(CHIPS):
            pa, pb, pm = _branch_products(oa, ob, om, woa, wob, wom, j)
            cs = lambda br: slice(br * D_MODEL + j * nc, br * D_MODEL + (j + 1) * nc)
            m_scr[:, j * nc:(j + 1) * nc] = gt_ref[:, cs(0)] * pa + gt_ref[:, cs(1)] * pb + gt_ref[:, cs(2)] * pm
        mer = m_scr[...].astype(_MM)
        mer_ref[...] = mer
        x1 = x_ref[...] + jnp.dot(mer, wout[...], preferred_element_type=F32)
        x1_ref[...] = x1
        xh, _ = _rms(x1)
        h2_ref[...] = (xh * g_ref[...]).astype(_MM)

    return _pc(
        body, name="merge_out", grid=(s // ts,),
        in_specs=[_row(ts, A_W)] * 6 + [_row(ts, B_QH * HEAD), _row(ts, M_W), _row(ts, D_MODEL), _row(ts, D_MODEL),
                                         _res(w_gate.shape), _res(b_gate.shape), _res(w_oa.shape), _res(w_ob.shape),
                                         _res(w_om.shape), _res(w_out.shape), _res((1, D_MODEL))],
        out_specs=[_row(ts, A_W), _row(ts, D_MODEL), _row(ts, D_MODEL), _row(ts, D_MODEL), _row(ts, CHIPS * ng)],
        out_shape=[_sds((s, A_W)), _sds((s, D_MODEL), _MM), _sds((s, D_MODEL)), _sds((s, D_MODEL), _MM),
                   _sds((s, CHIPS * ng))],
        scratch=[pltpu.VMEM((ts, D_MODEL), F32)])(
            *og, *lg, o_b, o_m, h, x, w_gate, b_gate, w_oa, w_ob, w_om, w_out, g2)


def _k_up(h2, w_up):
    s = h2.shape[0]
    ts = min(256, s)
    nu = w_up.shape[2]

    def body(h_ref, w_ref, u_ref):
        h = h_ref[...]
        for j in range(CHIPS):
            u_ref[:, j * nu:(j + 1) * nu] = jnp.dot(h, w_ref[j], preferred_element_type=F32)

    return _pc(body, name="up_proj", grid=(s // ts,), in_specs=[_row(ts, D_MODEL), _res(w_up.shape)],
               out_specs=[_row(ts, CHIPS * nu)], out_shape=[_sds((s, CHIPS * nu))])(h2, w_up)[0]


def _shift_down(v, halo, k):
    rolled = pltpu.roll(v, k, 0)
    row = lax.broadcasted_iota(jnp.int32, (8, v.shape[1]), 0)
    slab = rolled[0:8]
    for r in range(k):
        slab = jnp.where(row == r, halo[8 - k + r:8 - k + r + 1, :], slab)
    return jnp.concatenate([slab, rolled[8:]], axis=0)


def _shift_up(v, halo, k):
    ts = v.shape[0]
    rolled = pltpu.roll(v, ts - k, 0)
    row = lax.broadcasted_iota(jnp.int32, (8, v.shape[1]), 0)
    slab = rolled[ts - 8:]
    for r in range(k):
        slab = jnp.where(row == 8 - k + r, halo[r:r + 1, :], slab)
    return jnp.concatenate([rolled[:ts - 8], slab], axis=0)


def _k_ffn(u, conv_w, conv_b, w_down, w_down_t, x1, target):
    s = u.shape[0]
    ts = min(256, s)
    nu = conv_w.shape[2]
    half = CHIPS // 2

    def body(u_ref, uh_ref, cw_ref, cb_ref, wd_ref, wdt_ref, x1_ref, t_ref, dy_ref, f_ref, dc_ref, loss_ref, c_scr,
             f_scr, s_scr):
        i = pl.program_id(0)
        halo = jnp.where(i > 0, uh_ref[...], 0.0)
        for j in range(CHIPS):
            cs = slice(j * nu, (j + 1) * nu)
            uj = u_ref[:, cs]
            hj = halo[:, cs]
            c_scr[:, cs] = (cb_ref[:, cs] + cw_ref[j, 0:1, :] * _shift_down(uj, hj, 2)
                            + cw_ref[j, 1:2, :] * _shift_down(uj, hj, 1) + cw_ref[j, 2:3, :] * uj)
        for j in range(half):
            a = c_scr[:, j * nu:(j + 1) * nu]
            g = c_scr[:, (half + j) * nu:(half + j + 1) * nu]
            sa = _sigmoid(a)
            s_scr[:, j * nu:(j + 1) * nu] = sa
            f_scr[:, j * nu:(j + 1) * nu] = (a * sa * g).astype(_MM)
        f = f_scr[...]
        f_ref[...] = f
        y = x1_ref[...] + jnp.dot(f, wd_ref[...], preferred_element_type=F32)
        err = y - t_ref[...]
        dy = err * (1.0 / D_MODEL)
        dy_ref[...] = dy

        @pl.when(i == 0)
        def _():
            loss_ref[...] = jnp.zeros_like(loss_ref)

        loss_ref[...] += _sum8(err * err)
        df = _dot(dy, wdt_ref[...])
        for j in range(half):
            a = c_scr[:, j * nu:(j + 1) * nu]
            g = c_scr[:, (half + j) * nu:(half + j + 1) * nu]
            sa = s_scr[:, j * nu:(j + 1) * nu]
            dfj = df[:, j * nu:(j + 1) * nu]
            dc_ref[:, j * nu:(j + 1) * nu] = dfj * g * (sa * (1.0 + a * (1.0 - sa)))
            dc_ref[:, (half + j) * nu:(half + j + 1) * nu] = dfj * (a * sa)

    wide = CHIPS * nu
    return _pc(
        body, name="conv_ffn", grid=(s // ts,),
        in_specs=[_row(ts, wide), pl.BlockSpec((8, wide), lambda i: (jnp.maximum(i * (ts // 8) - 1, 0), 0)),
                  _res(conv_w.shape), _res((1, wide)), _res(w_down.shape), _res(w_down_t.shape), _row(ts, D_MODEL),
                  _row(ts, D_MODEL)],
        out_specs=[_row(ts, D_MODEL), _row(ts, D_FF), _row(ts, wide), _acc((8, D_MODEL))],
        out_shape=[_sds((s, D_MODEL)), _sds((s, D_FF), _MM), _sds((s, wide)), _sds((8, D_MODEL))],
        scratch=[pltpu.VMEM((ts, wide), F32), pltpu.VMEM((ts, D_FF), _MM), pltpu.VMEM((ts, D_FF), F32)])(
            u, u, conv_w, conv_b, w_down, w_down_t, x1, target)


def _k_conv_bwd(dc, u, conv_w, w_up, x1, g2, dy):
    s = u.shape[0]
    ts = min(256, s)
    nu = conv_w.shape[2]
    wide = CHIPS * nu
    last = s // ts - 1

    def body(dc_ref, dn_ref, u_ref, cw_ref, wu_ref, x1_ref, g_ref, dy_ref, dx1_ref, du_ref, cacc_ref, gacc_ref):
        i = pl.program_id(0)

        @pl.when(i == 0)
        def _():
            cacc_ref[...] = jnp.zeros_like(cacc_ref)
            gacc_ref[...] = jnp.zeros_like(gacc_ref)

        dhalo = jnp.where(i < last, dn_ref[...], 0.0)
        dh2 = jnp.zeros((ts, D_MODEL), F32)
        for j in range(CHIPS):
            cs = slice(j * nu, (j + 1) * nu)
            dcj, uj = dc_ref[:, cs], u_ref[:, cs]
            dc1, dc2 = _shift_up(dcj, dhalo[:, cs], 1), _shift_up(dcj, dhalo[:, cs], 2)
            cacc_ref[0, :, cs] += _sum8(dcj)
            cacc_ref[1, :, cs] += _sum8(dc2 * uj)
            cacc_ref[2, :, cs] += _sum8(dc1 * uj)
            cacc_ref[3, :, cs] += _sum8(dcj * uj)
            du = (cw_ref[j, 2:3, :] * dcj + cw_ref[j, 1:2, :] * dc1 + cw_ref[j, 0:1, :] * dc2).astype(_MM)
            du_ref[:, cs] = du
            dh2 = dh2 + _dot_nt(du, wu_ref[j])
        xh, r = _rms(x1_ref[...])
        gacc_ref[...] += _sum8(dh2 * xh)
        dx1_ref[...] = dy_ref[...] + _rms_bwd(dh2, xh, r, g_ref[...])

    return _pc(
        body, name="conv_up_bwd", grid=(s // ts,),
        in_specs=[_row(ts, wide),
                  pl.BlockSpec((8, wide), lambda i: (jnp.minimum((i + 1) * (ts // 8), s // 8 - 1), 0)),
                  _row(ts, wide), _res(conv_w.shape), _res(w_up.shape), _row(ts, D_MODEL), _res((1, D_MODEL)),
                  _row(ts, D_MODEL)],
        out_specs=[_row(ts, D_MODEL), _row(ts, wide), _acc((4, 8, wide)), _acc((8, D_MODEL))],
        out_shape=[_sds((s, D_MODEL)), _sds((s, wide), _MM), _sds((4, 8, wide)), _sds((8, D_MODEL))])(
            dc, dc, u, conv_w, w_up, x1, g2, dy)


def _k_merge_bwd(dx1, og, lg, o_a, o_b, o_m, gates, merged, w_oa, w_ob, w_om, w_out, dep):
    s = dx1.shape[0]
    ts = min(256, s)
    nc = w_oa.shape[2]

    def body(dx_ref, o0, o1, o2, l0, l1, l2, oa_ref, ob_ref, om_ref, gt_ref, mer_ref, woa, wob, wom, wout, dep_ref,
             dgp_ref, dog0, dog1, dog2, dl0, dl1, dl2, dob_ref, dom_ref, bacc_ref, dwa_ref, dwb_ref, dwm_ref, dwo_ref):
        i = pl.program_id(0)

        @pl.when(i == 0)
        def _():
            for ref in (bacc_ref, dwa_ref, dwb_ref, dwm_ref, dwo_ref):
                ref[...] = jnp.zeros_like(ref)

        dx = dx_ref[...]
        dwo_ref[...] += _dot_tn(mer_ref[...], dx)
        dmer = _dot_nt(dx, wout[...])
        oa, ob, om = oa_ref[...], ob_ref[...], om_ref[...]
        doa = jnp.zeros((ts, A_W), F32)
        dob = jnp.zeros((ts, B_QH * HEAD), F32)
        dom = jnp.zeros((ts, M_W), F32)
        for j in range(CHIPS):
            prods = _branch_products(oa, ob, om, woa, wob, wom, j)
            dmj = dmer[:, j * nc:(j + 1) * nc]
            dps = []
            for br, (p, o, dw_ref) in enumerate(zip(prods, (oa, ob, om), (dwa_ref, dwb_ref, dwm_ref))):
                cs = slice(br * D_MODEL + j * nc, br * D_MODEL + (j + 1) * nc)
                gt = gt_ref[:, cs]
                dgp = dmj * p * gt * (1.0 - gt)
                dgp_ref[:, cs] = dgp.astype(_MM)
                bacc_ref[:, cs] += _sum8(dgp)
                dp = (dmj * gt).astype(_MM)
                dw_ref[j] += _dot_tn(o, dp)
                dps.append(dp)
            doa = doa + _dot_nt(dps[0], woa[j])
            dob = dob + _dot_nt(dps[1], wob[j])
            dom = dom + _dot_nt(dps[2], wom[j])
        dob_ref[...] = dob
        dom_ref[...] = dom
        ws = _group_weights(l0[...], l1[...], l2[...])
        dsum = _seg_mean(doa * oa, HEAD) * float(HEAD)
        for w, dref, lref in zip(ws, (dog0, dog1, dog2), (dl0, dl1, dl2)):
            dref[...] = w * doa
            lref[...] = w * dsum

    return _pc(
        body, name="merge_out_bwd", grid=(s // ts,),
        in_specs=[_row(ts, D_MODEL)] + [_row(ts, A_W)] * 7 + [_row(ts, B_QH * HEAD), _row(ts, M_W), _row(ts, 3 * D_MODEL),
                                                              _row(ts, D_MODEL), _res(w_oa.shape), _res(w_ob.shape),
                                                              _res(w_om.shape), _res(w_out.shape), _res((8, 128))],
        out_specs=[_row(ts, 3 * D_MODEL)] + [_row(ts, A_W)] * 6
        + [_row(ts, B_QH * HEAD), _row(ts, M_W), _acc((8, 3 * D_MODEL)), _acc(w_oa.shape), _acc(w_ob.shape),
           _acc(w_om.shape), _acc(w_out.shape)],
        out_shape=[_sds((s, 3 * D_MODEL), _MM)] + [_sds((s, A_W))] * 6
        + [_sds((s, B_QH * HEAD)), _sds((s, M_W)), _sds((8, 3 * D_MODEL)), _sds(w_oa.shape), _sds(w_ob.shape),
           _sds(w_om.shape), _sds(w_out.shape)])(
            dx1, *og, *lg, o_a, o_b, o_m, gates, merged, w_oa, w_ob, w_om, w_out, dep)


def _k_mem_bwd(m_q, gq, mk, mv, o_m, do_m):
    s = m_q.shape[0]
    n = mk.shape[0]
    ts = min(512, s)
    scale = M_HD ** -0.5

    def body(q_ref, g_ref, mk_ref, mv_ref, o_ref, do_ref, dq_ref, dmk_ref, dmv_ref, gacc_ref):
        i = pl.program_id(0)

        @pl.when(i == 0)
        def _():
            dmk_ref[...] = jnp.zeros_like(dmk_ref)
            dmv_ref[...] = jnp.zeros_like(dmv_ref)
            gacc_ref[...] = jnp.zeros_like(gacc_ref)

        gain = g_ref[...]
        qh, r = _seg_norm(q_ref[...], M_HD)
        qn = (qh * gain).astype(_MM)
        do = do_ref[...]
        delta = _seg_mean(do * o_ref[...], M_HD) * float(M_HD)
        dqn = []
        for h in range(M_HEADS):
            hs = slice(h * M_HD, (h + 1) * M_HD)
            p = _mem_probs(qn[:, hs], mk_ref[:, hs])
            dp = _dot_nt(do[:, hs], mv_ref[:, hs])
            ds = (p * (dp - delta[:, hs][:, 0:1]) * scale).astype(_MM)
            dqn.append(_dot(ds, mk_ref[:, hs]))
            dmk_ref[:, hs] += _dot_tn(ds, qn[:, hs])
            dmv_ref[:, hs] += _dot_tn(p, do[:, hs])
        dqn = jnp.concatenate(dqn, axis=1)
        gacc_ref[...] += _sum8(dqn * qh)
        z = dqn * gain
        dq_ref[...] = (r * (z - qh * _seg_mean(z * qh, M_HD))).astype(_MM)

    return _pc(
        body, name="mem_attn_bwd", grid=(s // ts,),
        in_specs=[_row(ts, M_W), _res((1, M_W)), _res((n, M_W)), _res((n, M_W)), _row(ts, M_W), _row(ts, M_W)],
        out_specs=[_row(ts, M_W), _acc((n, M_W)), _acc((n, M_W)), _acc((8, M_W))],
        out_shape=[_sds((s, M_W), _MM), _sds((n, M_W)), _sds((n, M_W)), _sds((8, M_W))])(m_q, gq, mk, mv, o_m, do_m)


def _k_memkv_bwd(mem, mem_norm, w_kv, m_k_norm, mem_n, kv, dmk, dmv):
    n = mem.shape[0]

    def body(m_ref, g_ref, w_ref, gk_ref, mn_ref, kv_ref, dmk_ref, dmv_ref, dw_ref, dg_ref, dgk_ref):
        gk = gk_ref[...]
        kh, r = _seg_norm(kv_ref[:, :M_W], M_HD)
        dmk = dmk_ref[...]
        dgk_ref[...] = _sum8(dmk * kh)
        z = dmk * gk
        dk = r * (z - kh * _seg_mean(z * kh, M_HD))
        dkv = jnp.concatenate([dk, dmv_ref[...]], axis=1).astype(_MM)
        dw_ref[...] = _dot_tn(mn_ref[...], dkv)
        dmn = _dot_nt(dkv, w_ref[...])
        mh, _ = _rms(m_ref[...])
        dg_ref[...] = _sum8(dmn * mh)

    return _pc(body, name="mem_kv_bwd", grid=(1,),
               in_specs=[_acc((n, D_MODEL)), _acc((1, D_MODEL)), _acc(w_kv.shape), _acc((1, M_W)), _acc((n, D_MODEL)),
                         _acc((n, 2 * M_W)), _acc((n, M_W)), _acc((n, M_W))],
               out_specs=[_acc(w_kv.shape), _acc((8, D_MODEL)), _acc((8, M_W))],
               out_shape=[_sds(w_kv.shape), _sds((8, D_MODEL)), _sds((8, M_W))])(
                   mem, mem_norm, w_kv, m_k_norm, mem_n, kv, dmk, dmv)


def _k_band_bwd(qn, kn, vn, do, lse, dl_or_o, *, hq, hk, max_dist, segs, sink, name):
    rows = qn.shape[0]
    nb = rows // BLK
    units = hq // A_HEADS
    shared = hk != hq
    wq, wk = hq * HEAD, hk * HEAD
    scale = HEAD ** -0.5

    def body(*refs):
        (q2_ref, qx_ref, kc_ref, kp_ref, vc_ref, vp_ref, do2_ref, dox_ref, l2_ref, lx_ref, e2_ref, ex_ref) = refs[:12]
        if sink is None:
            dq_ref, dk_ref, dv_ref = refs[12:]
        else:
            sk_ref, dq_ref, dk_ref, dv_ref, sacc_ref = refs[12:]
        i = pl.program_id(0)
        thr = lambda b: jnp.where(_first_flag(b, segs, nb), 1 << 20, BLK - max_dist)
        bias_a, bias_b = _band_bias(thr(2 * i), True), _band_bias(thr(2 * i + 1), True)
        bias_c = _band_bias(thr(2 * i + 2), False)
        if sink is not None:
            @pl.when(i == 0)
            def _():
                sacc_ref[...] = jnp.zeros_like(sacc_ref)

        def tile(q4, do4, l_cols, dlt, kd, vd, bias, width):
            s, dp = _dot_nt(q4, kd) * scale, _dot_nt(do4, vd)
            ps, dss = [], []
            for h in range(A_HEADS):
                seg = slice(h * width, (h + 1) * width)
                p = jnp.exp(s[:, seg] + bias - l_cols[h])
                ps.append(p)
                dss.append(p * (dp[:, seg] - dlt[:, h * HEAD:h * HEAD + 1]) * scale)
            return ps, dss

        cat = lambda parts: jnp.concatenate([t.astype(_MM) for t in parts], axis=1)
        for u in range(units):
            us = slice(u * A_W, (u + 1) * A_W)
            k_a = _unit_kv(((kp_ref, _LO), (kc_ref, _LO)), u, shared)
            v_a = _unit_kv(((vp_ref, _LO), (vc_ref, _LO)), u, shared)
            k_b, v_b = _unit_kv(((kc_ref, _BOTH),), u, shared), _unit_kv(((vc_ref, _BOTH),), u, shared)
            kd_a, vd_a, kd_b, vd_b = _blockdiag(k_a), _blockdiag(v_a), _blockdiag(k_b), _blockdiag(v_b)
            kd_c, vd_c = _blockdiag(k_b[BLK:]), _blockdiag(v_b[BLK:])
            qs = (q2_ref[_LO, us], q2_ref[_HI, us], qx_ref[:, us])
            dos = (do2_ref[_LO, us], do2_ref[_HI, us], dox_ref[:, us])
            lcols = [[ref[rs, u * A_W + h * HEAD:u * A_W + h * HEAD + 1] for h in range(A_HEADS)]
                     for ref, rs in ((l2_ref, _LO), (l2_ref, _HI), (lx_ref, _LO))]
            if sink is None:
                dlts = (e2_ref[_LO, us], e2_ref[_HI, us], ex_ref[:, us])
            else:
                dlts = tuple(_seg_sum64(d.astype(F32) * ref[rs, us])
                             for d, (ref, rs) in zip(dos, ((e2_ref, _LO), (e2_ref, _HI), (ex_ref, _LO))))
                for t in range(2):
                    for h in range(A_HEADS):
                        j = u * A_HEADS + h
                        sacc_ref[:, j:j + 1] += -jnp.exp(sk_ref[j] - lcols[t][h]) * dlts[t][:, h * HEAD:h * HEAD + 1]
            p_a, ds_a = tile(qs[0], dos[0], lcols[0], dlts[0], kd_a, vd_a, bias_a, 2 * BLK)
            p_b, ds_b = tile(qs[1], dos[1], lcols[1], dlts[1], kd_b, vd_b, bias_b, 2 * BLK)
            p_c, ds_c = tile(qs[2], dos[2], lcols[2], dlts[2], kd_c, vd_c, bias_c, BLK)
            dq_ref[_LO, us] = _dot(cat(ds_a), kd_a)
            dq_ref[_HI, us] = _dot(cat(ds_b), kd_b)
            outs = []
            for pa, pb, pc, lhs in ((ds_a, ds_b, ds_c, qs), (p_a, p_b, p_c, dos)):
                from_a = _fold_diag(_dot_tn(cat([t[:, BLK:] for t in pa]), lhs[0]), BLK)
                from_b = _fold_diag(_dot_tn(cat(pb), lhs[1]), 2 * BLK)
                from_c = _fold_diag(_dot_tn(cat(pc), lhs[2]), BLK)
                outs.append(jnp.concatenate([from_a + from_b[:BLK], from_b[BLK:] + from_c], axis=0))
            dk4, dv4 = outs
            if shared:
                fold = lambda t: (t[:, 0:HEAD] + t[:, HEAD:2 * HEAD]) + (t[:, 2 * HEAD:3 * HEAD] + t[:, 3 * HEAD:])
                dk_ref[:, u * HEAD:(u + 1) * HEAD] = fold(dk4)
                dv_ref[:, u * HEAD:(u + 1) * HEAD] = fold(dv4).astype(_MM)
            else:
                dk_ref[:, us] = dk4
                dv_ref[:, us] = dv4.astype(_MM)

    two = lambda w: pl.BlockSpec((2 * BLK, w), lambda i: (i, 0))
    prev = lambda w: pl.BlockSpec((BLK, w), lambda i: (jnp.maximum(2 * i - 1, 0), 0))
    nxt = lambda w: pl.BlockSpec((BLK, w), lambda i: (jnp.minimum(2 * i + 2, nb - 1), 0))
    in_specs = [two(wq), nxt(wq), two(wk), prev(wk), two(wk), prev(wk), two(wq), nxt(wq), two(wq), nxt(wq), two(wq), nxt(wq)]
    args = [qn, qn, kn, kn, vn, vn, do, do, lse, lse, dl_or_o, dl_or_o]
    out_specs = [two(wq), two(wk), two(wk)]
    out_shape = [_sds((rows, wq)), _sds((rows, wk)), _sds((rows, wk), _MM)]
    if sink is not None:
        in_specs.append(pl.BlockSpec(memory_space=pltpu.SMEM))
        args.append(sink)
        out_specs.append(_acc((BLK, 128)))
        out_shape.append(_sds((BLK, 128)))
    return _pc(body, name=name, grid=(nb // 2,), in_specs=in_specs, out_specs=out_specs, out_shape=out_shape)(*args)


def _k_prep_bwd(srcs, dqn, dkn, gq, gk, tabs, tab_row, *, wq, wk, rows_per_gain, name):
    rows = dqn.shape[0]
    ts = min(512, rows)
    ngain = gq.shape[0]

    def body(q_ref, k_ref, dq_ref, dk_ref, gq_ref, gk_ref, c_ref, sa_ref, sb_ref, oq_ref, ok_ref, aq_ref, ak_ref):
        i = pl.program_id(0)

        @pl.when(lax.rem(i * ts, rows_per_gain) == 0)
        def _():
            aq_ref[...] = jnp.zeros_like(aq_ref)
            ak_ref[...] = jnp.zeros_like(ak_ref)

        c, sa, sb = c_ref[...], sa_ref[...], sb_ref[...]
        for x_ref, d_ref, g_ref, o_ref, a_ref in ((q_ref, dq_ref, gq_ref, oq_ref, aq_ref),
                                                   (k_ref, dk_ref, gk_ref, ok_ref, ak_ref)):
            xh, r = _seg_norm(x_ref[...], HEAD)
            dt = _rope_bwd(d_ref[...], c, sa, sb)
            a_ref[...] += _sum8(dt * xh)
            z = dt * g_ref[...]
            o_ref[...] = (r * (z - xh * _seg_mean(z * xh, HEAD))).astype(_MM)

    gspec = lambda w: pl.BlockSpec((None, 1, w), lambda i: ((i * ts) // rows_per_gain, 0, 0))
    aspec = lambda w: pl.BlockSpec((None, 8, w), lambda i: ((i * ts) // rows_per_gain, 0, 0))
    return _pc(
        body, name=name, grid=(rows // ts,),
        in_specs=[_row(ts, wq, srcs[0][1]), _row(ts, wk, srcs[1][1]), _row(ts, wq), _row(ts, wk), gspec(wq), gspec(wk)]
        + [pl.BlockSpec((ts, 128), lambda i: (i + tab_row // ts, 0))] * 3,
        out_specs=[_row(ts, wq), _row(ts, wk), aspec(wq), aspec(wk)],
        out_shape=[_sds((rows, wq), _MM), _sds((rows, wk), _MM), _sds((ngain, 8, wq)), _sds((ngain, 8, wk))])(
            srcs[0][0], srcs[1][0], dqn, dkn, gq, gk, *tabs)


def _k_in_bwd(pieces, dgp, x, g1, dx1, w_in, w_gate):
    s = x.shape[0]
    ts = min(512, s)
    nin, ng = w_in.shape[2], w_gate.shape[2]
    widths = [p.shape[1] for p in pieces]
    ncol = sum(widths)

    def body(*refs):
        p_refs = refs[:len(pieces)]
        dgp_ref, x_ref, g_ref, dx1_ref, wi_ref, wg_ref, gx_ref, dpj_ref, gacc_ref = refs[len(pieces):]
        i = pl.program_id(0)

        @pl.when(i == 0)
        def _():
            gacc_ref[...] = jnp.zeros_like(gacc_ref)

        off = 0
        for p_ref, w in zip(p_refs, widths):
            dpj_ref[:, off:off + w] = p_ref[...]
            off += w
        dh = jnp.zeros((ts, D_MODEL), F32)
        for j in range(CHIPS):
            dh = dh + _dot_nt(dpj_ref[:, j * nin:(j + 1) * nin], wi_ref[j])
            dh = dh + _dot_nt(dgp_ref[:, j * ng:(j + 1) * ng], wg_ref[j])
        xh, r = _rms(x_ref[...])
        gacc_ref[...] += _sum8(dh * xh)
        gx_ref[...] = dx1_ref[...] + _rms_bwd(dh, xh, r, g_ref[...])

    return _pc(
        body, name="in_proj_bwd", grid=(s // ts,),
        in_specs=[_row(ts, w) for w in widths] + [_row(ts, CHIPS * ng), _row(ts, D_MODEL), _res((1, D_MODEL)),
                                                  _row(ts, D_MODEL), _res(w_in.shape), _res(w_gate.shape)],
        out_specs=[_row(ts, D_MODEL), _row(ts, ncol), _acc((8, D_MODEL))],
        out_shape=[_sds((s, D_MODEL)), _sds((s, ncol), _MM), _sds((8, D_MODEL))])(*pieces, dgp, x, g1, dx1, w_in, w_gate)


def _k_wgrad(a, b, *, nblk, stacked, name):
    s, k = a.shape
    n = b.shape[1]
    nb = n // nblk
    ts = min(2048 if k <= 1024 else 1024, s)

    def body(a_ref, b_ref, o_ref):
        @pl.when(pl.program_id(1) == 0)
        def _():
            o_ref[...] = jnp.zeros_like(o_ref)

        o_ref[...] += _dot_tn(a_ref[...], b_ref[...])

    if stacked:
        out_spec, out_shape = pl.BlockSpec((None, k, nb), lambda g, t: (g, 0, 0)), _sds((nblk, k, nb))
    else:
        out_spec, out_shape = pl.BlockSpec((k, nb), lambda g, t: (0, g)), _sds((k, n))
    return _pc(body, name=name, grid=(nblk, s // ts),
               in_specs=[pl.BlockSpec((ts, k), lambda g, t: (t, 0)), pl.BlockSpec((ts, nb), lambda g, t: (t, g))],
               out_specs=[out_spec], out_shape=[out_shape])(a, b)[0]


def _to_res(t, d):
    s, c = t.shape
    return t if d == 1 else t.reshape(s // d, d, c).transpose(1, 0, 2).reshape(s, c)


def _from_res(t, d):
    s, c = t.shape
    return t if d == 1 else t.reshape(d, s // d, c).transpose(1, 0, 2).reshape(s, c)


def _tile_gain(g, heads):
    return jnp.tile(g, (1,) * (g.ndim - 1) + (heads,))[..., None, :]


def _local_step(x, mem, pos, target, small, get_w_in, get_rest, on_grads):
    s = x.shape[0]
    nblk = s // BLK
    g1, g2 = small["attn_norm"], small["ffn_norm"]

    pos_rows = jnp.concatenate([_to_res(pos[:, None], d)[:, 0] for _, d in A_GROUPS] + [pos])
    tabs = _rope_tables(pos_rows)
    w_in = get_w_in(tabs[0])

    h, qa0, qa1, qa2, q_b, k_b, v_b, m_q = _k_in(x, g1, w_in)

    qkv_a = jnp.concatenate([_to_res(t, d) for t, (_, d) in zip((qa0, qa1, qa2), A_GROUPS)], axis=0)
    gq_a = _tile_gain(small["a_q_norm"], A_HEADS)
    gk_a = _tile_gain(small["a_k_norm"], A_HEADS)
    src_a = ((qkv_a, 0), (qkv_a, 1), (qkv_a, 2))
    qn_a, kn_a, vn_a = _k_prep(src_a, gq_a, gk_a, tabs, 0, wq=A_W, wk=A_W, rows_per_gain=s, name="prep_a")
    segs_a = tuple((gi * nblk, nblk // d) for gi, (_, d) in enumerate(A_GROUPS))
    o_res, l_res = _k_band_fwd(qn_a, kn_a, vn_a, hq=A_HEADS, hk=A_HEADS, max_dist=BLK, segs=segs_a, sink=None,
                               name="attn_a")
    og = [_from_res(o_res[gi * s:(gi + 1) * s], d) for gi, (_, d) in enumerate(A_GROUPS)]
    lg = [_from_res(l_res[gi * s:(gi + 1) * s], d) for gi, (_, d) in enumerate(A_GROUPS)]

    gq_b = _tile_gain(small["b_q_norm"], B_QH)
    gk_b = _tile_gain(small["b_k_norm"], B_KVH)
    src_b = ((q_b, 0), (k_b, 0), (v_b, 0))
    qn_b, kn_b, vn_b = _k_prep(src_b, gq_b, gk_b, tabs, 3 * s, wq=B_QH * HEAD, wk=B_KVH * HEAD, rows_per_gain=s,
                               name="prep_b")
    sink_x = small["b_sinks"][0]
    segs_b = ((0, nblk),)
    o_b, l_b = _k_band_fwd(qn_b, kn_b, vn_b, hq=B_QH, hk=B_KVH, max_dist=B_WINDOW - 1, segs=segs_b, sink=sink_x,
                           name="attn_b")

    wts = get_rest(0, o_b)

    gq_m = _tile_gain(small["m_q_norm"], M_HEADS)[0]
    gk_m = _tile_gain(small["m_k_norm"], M_HEADS)[0]
    mem_n, kv, mk, mv = _k_memkv(mem, small["mem_norm"], wts["w_mem_kv"], gk_m)
    o_m = _k_mem_fwd(m_q, gq_m, mk, mv)

    o_a, merged, x1, h2, gates = _k_merge(og, lg, o_b, o_m, h, x, wts["w_gate"], small["b_gate"], wts["w_o_a"],
                                          wts["w_o_b"], wts["w_o_m"], wts["w_out"], g2)
    wts.update(get_rest(1, x1))
    u = _k_up(h2, wts["w_up"])
    dy, f, dc, loss_acc = _k_ffn(u, wts["conv_w"], small["conv_b"], wts["w_down"], wts["w_down"].T, x1, target)
    loss = (0.5 / D_MODEL) * jnp.sum(loss_acc)

    dx1, du, cacc, g2acc = _k_conv_bwd(dc, u, wts["conv_w"], wts["w_up"], x1, g2, dy)
    tok = on_grads({"w_up": _k_wgrad(h2, du, nblk=CHIPS, stacked=True, name="dw_up"),
                    "w_down": _k_wgrad(f, dy, nblk=2, stacked=False, name="dw_down").reshape(CHIPS, -1, D_MODEL)}, dx1)
    (dgp, dog0, dog1, dog2, dl0, dl1, dl2, do_b, do_m, bacc, dw_oa, dw_ob, dw_om, dw_out) = _k_merge_bwd(
        dx1, og, lg, o_a, o_b, o_m, gates, merged, wts["w_o_a"], wts["w_o_b"], wts["w_o_m"], wts["w_out"], tok)
    tok = on_grads({"w_gate": _k_wgrad(h, dgp, nblk=CHIPS, stacked=True, name="dw_gate"),
                    "w_o_a": dw_oa, "w_o_b": dw_ob, "w_o_m": dw_om, "w_out": dw_out.reshape(CHIPS, -1, D_MODEL)}, do_m)

    dq_m, dmk, dmv, gqm_acc = _k_mem_bwd(m_q, gq_m + tok[0:1, 0:1], mk, mv, o_m, do_m)
    dw_kv, gmem_acc, gkm_acc = _k_memkv_bwd(mem, small["mem_norm"], wts["w_mem_kv"], gk_m, mem_n, kv, dmk, dmv)

    dq_bn, dk_bn, dv_b, sacc = _k_band_bwd(qn_b, kn_b, vn_b, do_b, l_b, o_b, hq=B_QH, hk=B_KVH,
                                           max_dist=B_WINDOW - 1, segs=segs_b, sink=sink_x, name="attn_b_bwd")
    tok = on_grads({}, dq_bn)
    dq_b, dk_b, gqb_acc, gkb_acc = _k_prep_bwd(src_b, dq_bn, dk_bn, gq_b + tok[0:1, 0:1], gk_b, tabs, 3 * s, wq=B_QH * HEAD,
                                               wk=B_KVH * HEAD, rows_per_gain=s, name="prep_b_bwd")

    do_res = jnp.concatenate([_to_res(t, d) for t, (_, d) in zip((dog0, dog1, dog2), A_GROUPS)], axis=0)
    dl_res = jnp.concatenate([_to_res(t, d) for t, (_, d) in zip((dl0, dl1, dl2), A_GROUPS)], axis=0)
    dq_an, dk_an, dv_a = _k_band_bwd(qn_a, kn_a, vn_a, do_res, l_res, dl_res, hq=A_HEADS, hk=A_HEADS, max_dist=BLK,
                                     segs=segs_a, sink=None, name="attn_a_bwd")
    dq_a, dk_a, gqa_acc, gka_acc = _k_prep_bwd(src_a, dq_an, dk_an, gq_a, gk_a, tabs, 0, wq=A_W, wk=A_W,
                                               rows_per_gain=s, name="prep_a_bwd")
    pieces = []
    for gi, (_, d) in enumerate(A_GROUPS):
        rs = slice(gi * s, (gi + 1) * s)
        pieces += [_from_res(t[rs], d) for t in (dq_a, dk_a, dv_a)]
    pieces += [dq_b, dk_b, dv_b, dq_m]
    grad_x, dproj, g1acc = _k_in_bwd(pieces, dgp, x, g1, dx1, w_in, wts["w_gate"])
    on_grads({"w_in": _k_wgrad(h, dproj, nblk=CHIPS, stacked=True, name="dw_in"),
              "w_mem_kv": dw_kv.reshape(CHIPS, -1, 2 * M_W)}, grad_x)

    def fold(acc, heads):
        v = jnp.sum(acc, axis=-2)
        return jnp.sum(v.reshape(v.shape[:-1] + (heads, -1)), axis=-2)

    csum = jnp.sum(cacc, axis=1)
    sml = {
        "attn_norm": jnp.sum(g1acc, axis=0), "a_q_norm": fold(gqa_acc, A_HEADS), "a_k_norm": fold(gka_acc, A_HEADS),
        "b_q_norm": fold(gqb_acc[0], B_QH), "b_k_norm": fold(gkb_acc[0], B_KVH),
        "b_sinks": jnp.sum(sacc, axis=0)[:B_QH], "mem_norm": jnp.sum(gmem_acc, axis=0),
        "m_q_norm": fold(gqm_acc, M_HEADS), "m_k_norm": fold(gkm_acc, M_HEADS),
        "b_gate": jnp.sum(bacc, axis=0), "ffn_norm": jnp.sum(g2acc, axis=0),
        "conv_w": csum[1:], "conv_b": csum[0],
    }
    return loss, grad_x, sml


def _mesh_pos():
    return lax.axis_index("x"), lax.axis_index("y"), lax.axis_index("c")


def _chip_peers(x, y):
    return [(1 - x, y), (x, 1 - y), (1 - x, 1 - y)]


_ANY = pl.BlockSpec(memory_space=pl.ANY)


def _comm_call(body, *, name, n_in, out_shape, scratch):
    return pl.pallas_call(body, name=name, in_specs=[_ANY] * n_in, out_specs=[_ANY] * len(out_shape),
                          out_shape=out_shape, scratch_shapes=scratch)


def _remote(src, dst, send_sem, recv_sem, dev):
    return pltpu.make_async_remote_copy(src_ref=src, dst_ref=dst, send_sem=send_sem, recv_sem=recv_sem,
                                        device_id=dev, device_id_type=MESH)


def _pair_join(halves, name):
    nt = len(halves)

    def body(*refs):
        ins, got = refs[:nt], refs[nt:2 * nt]
        send_sems, recv_sems = refs[2 * nt:]
        x, y, c = _mesh_pos()
        cps = []
        for t in range(nt):
            rc = _remote(ins[t], got[t], send_sems.at[t], recv_sems.at[t], (x, y, 1 - c))
            rc.start()
            cps.append(rc)
        for rc in cps:
            rc.wait()

    out_shape = [_sds(hf.shape, hf.dtype) for hf in halves]
    scratch = [pltpu.SemaphoreType.DMA((nt,)), pltpu.SemaphoreType.DMA((nt,))]
    return _comm_call(body, name=name, n_in=nt, out_shape=out_shape, scratch=scratch)(*halves)


_HBM = pl.BlockSpec(memory_space=pltpu.HBM)
_SEMS = pl.BlockSpec(memory_space=pltpu.SEMAPHORE)
_EFFECT = pltpu.SideEffectType.DATAFLOW_SIDE_EFFECTING


def _bcast_copies(ins, lands, send_sems, recv_sems):
    x, y, c = _mesh_pos()
    me = 2 * x + y
    targets = [((px, py, c), 2 * px + py) for px, py in _chip_peers(x, y)] + [((x, y, 1 - c), me)]
    out = []
    for t in range(len(ins)):
        for k, (dev, idx) in enumerate(targets):
            i = t * len(targets) + k
            arrival = lambda t=t, i=i, idx=idx, dev=dev: _remote(ins[t], lands[t].at[idx], send_sems.at[i],
                                                                 recv_sems.at[i], dev)
            out.append((_remote(ins[t], lands[t].at[me], send_sems.at[i], recv_sems.at[i], dev), arrival))
    return out


def _scatter_copies(ins, lands, send_sems, recv_sems):
    x, y, c = _mesh_pos()
    out = []
    for t in range(len(ins)):
        for k, (px, py) in enumerate(_chip_peers(x, y)):
            i = t * 3 + k
            cp = _remote(ins[t].at[2 * px + py], lands[t].at[k], send_sems.at[i], recv_sems.at[i], (px, py, c))
            out.append((cp, lambda cp=cp: cp))
    return out


def _pair_copies(ins, lands, send_sems, recv_sems):
    x, y, c = _mesh_pos()
    out = []
    for t in range(len(ins)):
        hr = ins[t].shape[1] // 2
        give = ins[t].at[:, pl.ds(pl.multiple_of((1 - c) * hr, 8), hr), :]
        cp = _remote(give, lands[t], send_sems.at[t], recv_sems.at[t], (x, y, 1 - c))
        out.append((cp, lambda cp=cp: cp))
    return out


def _join_copies(ins, lands, send_sems, recv_sems):
    x, y, c = _mesh_pos()
    out = []
    for t in range(len(ins)):
        cp = _remote(ins[t], lands[t], send_sems.at[t], recv_sems.at[t], (x, y, 1 - c))
        out.append((cp, lambda cp=cp: cp))
    return out


def _half_copies(ins, lands, send_sems, recv_sems):
    x, y, c = _mesh_pos()
    me = 2 * x + y
    out = []
    for t in range(len(ins)):
        hr = ins[t].shape[0] // 2
        rows = pl.ds(pl.multiple_of(c * hr, 8), hr)
        for k, (px, py) in enumerate(_chip_peers(x, y)):
            i = t * 3 + k
            arrival = lambda t=t, i=i, px=px, py=py, rows=rows: _remote(
                ins[t].at[rows, :], lands[t].at[2 * px + py].at[rows, :], send_sems.at[i], recv_sems.at[i], (px, py, c))
            out.append((_remote(ins[t].at[rows, :], lands[t].at[me].at[rows, :], send_sems.at[i], recv_sems.at[i],
                                (px, py, c)), arrival))
    return out


def _finish_halves(shards, stacks):
    nt = len(shards)

    def body(*refs):
        ins, held, outs = refs[:nt], refs[nt:2 * nt], refs[2 * nt:3 * nt]
        fwd_s, fwd_r, own_s, own_r = refs[3 * nt:]
        x, y, c = _mesh_pos()
        me = 2 * x + y
        sib = (x, y, 1 - c)
        pending = []
        for t in range(nt):
            hr = shards[t].shape[0] // 2
            half = lambda ref, who: ref.at[pl.ds(pl.multiple_of(who * hr, 8), hr), :]
            own = _remote(ins[t], outs[t].at[me], own_s.at[t], own_r.at[t], sib)
            own.start()
            pending.append(own.wait)
            for k, (px, py) in enumerate(_chip_peers(x, y)):
                pj = 2 * px + py
                fw = _remote(half(held[t].at[pj], c), half(outs[t].at[pj], c), fwd_s.at[t, k], fwd_r.at[t, k], sib)
                fw.start()
                pending.append(fw.wait_send)
                other = half(outs[t].at[pj], 1 - c)
                pending.append(_remote(other, other, fwd_s.at[t, k], fwd_r.at[t, k], sib).wait_recv)
        for wait in pending:
            wait()

    dma = pltpu.SemaphoreType.DMA
    return pl.pallas_call(
        body, name="gather_w_in_finish", in_specs=[_ANY] * (2 * nt), out_specs=[_ANY] * nt,
        out_shape=[_sds(a.shape, a.dtype) for a in stacks], input_output_aliases={nt + i: i for i in range(nt)},
        scratch_shapes=[dma((nt, 3)), dma((nt, 3)), dma((nt,)), dma((nt,))])(*shards, *stacks)


def _split_start(copies, srcs, land_shapes, ncopy, dep, name, lands=None):
    nt = len(srcs)

    def body(*refs):
        ins, lands = refs[:nt], refs[nt:2 * nt]
        send_sems, recv_sems, token = refs[2 * nt + 1], refs[2 * nt + 2], refs[-1]
        for send, _ in copies(ins, lands, send_sems, recv_sems):
            send.start()
        token[...] = jnp.zeros_like(token)

    if lands is None:
        lands = [lax.empty(sh, a.dtype) for sh, a in zip(land_shapes, srcs)]
    lands = [pltpu.with_memory_space_constraint(a, pltpu.HBM) for a in lands]
    srcs = [pltpu.with_memory_space_constraint(a, pltpu.HBM) for a in srcs]
    dma = pltpu.SemaphoreType.DMA
    out_shape = ([dma((nt * ncopy,)), dma((nt * ncopy,))] + [pltpu.HBM(a.shape, a.dtype) for a in srcs + lands]
                 + [_sds((8, 128))])
    outs = pl.pallas_call(
        body, name=name, in_specs=[_HBM] * (2 * nt) + [_ANY],
        out_specs=[_SEMS, _SEMS] + [_HBM] * (2 * nt) + [pl.BlockSpec(memory_space=pltpu.VMEM)], out_shape=out_shape,
        input_output_aliases={i: 2 + i for i in range(2 * nt)},
        compiler_params=pltpu.CompilerParams(has_side_effects=_EFFECT))(*srcs, *lands, dep)
    return outs[0], outs[1], outs[2:2 + nt], outs[2 + nt:2 + 2 * nt], outs[-1]


def _split_wait(copies, send_sems, recv_sems, srcs, lands, after, name):
    nt = len(srcs)

    def body(*refs):
        ins, lnd = refs[:nt], refs[nt:2 * nt]
        for send, arrival in copies(ins, lnd, refs[2 * nt], refs[2 * nt + 1]):
            send.wait_send()
            arrival().wait_recv()

    outs = pl.pallas_call(
        body, name=name, in_specs=[_HBM] * (2 * nt) + [_SEMS, _SEMS, _ANY], out_specs=[_HBM] * (2 * nt),
        out_shape=[pltpu.HBM(a.shape, a.dtype) for a in list(srcs) + list(lands)],
        input_output_aliases={i: i for i in range(2 * nt)},
        compiler_params=pltpu.CompilerParams(has_side_effects=_EFFECT))(*srcs, *lands, send_sems, recv_sems, after)
    return outs[:nt], outs[nt:]


def _small_copies(ins, lands, send_sems, recv_sems):
    x, y, c = _mesh_pos()
    me = 4 * x + 2 * y + c
    out = []
    for k in range(1, NDEV):
        px, py, pc = x ^ (k >> 2), y ^ ((k >> 1) & 1), c ^ (k & 1)
        arrival = lambda k=k, px=px, py=py, pc=pc: _remote(ins[0], lands[0].at[4 * px + 2 * py + pc], send_sems.at[k - 1],
                                                            recv_sems.at[k - 1], (px, py, pc))
        out.append((_remote(ins[0], lands[0].at[me], send_sems.at[k - 1], recv_sems.at[k - 1], (px, py, pc)), arrival))
    return out


def _row_tile(r, c, mib=1):
    t = r
    while t * c * 4 > (mib << 20) and t % 16 == 0:
        t //= 2
    return t


def _k_pair_add(full, got, name):
    g, r, c = full.shape
    hr = r // 2
    tr = _row_tile(hr, c, 4)
    nh = hr // tr

    def body(a_ref, b_ref, o_ref):
        o_ref[...] = (a_ref[...] + b_ref[...]).astype(_WIRE)

    mine = pl.BlockSpec((None, tr, c), lambda i, j: (i, lax.axis_index("c") * nh + j, 0))
    spec = pl.BlockSpec((None, tr, c), lambda i, j: (i, j, 0))
    return _pc(body, name=name, grid=(g, nh), in_specs=[mine, spec], out_specs=[spec],
               out_shape=[_sds((g, hr, c), _WIRE)])(full, got)[0]


def _k_chip_sum(parts, slots, name):
    _, r, c = parts.shape
    tr = _row_tile(r, c, 4)

    def body(a_ref, s_ref, o_ref):
        acc = a_ref[...].astype(F32)
        for k in range(3):
            acc = acc + s_ref[k].astype(F32)
        o_ref[...] = acc

    own = pl.BlockSpec((None, tr, c), lambda i: (2 * lax.axis_index("x") + lax.axis_index("y"), i, 0))
    return _pc(body, name=name, grid=(r // tr,), in_specs=[own, pl.BlockSpec((3, tr, c), lambda i: (0, i, 0))],
               out_specs=[_row(tr, c)], out_shape=[_sds((r, c))])(parts, slots)[0]


def _adam(w, g, m, v):
    m = ADAM_B1 * m + (1.0 - ADAM_B1) * g
    v = ADAM_B2 * v + (1.0 - ADAM_B2) * (g * g)
    m_hat = m / (1.0 - ADAM_B1 ** ADAM_STEP)
    v_hat = v / (1.0 - ADAM_B2 ** ADAM_STEP)
    return -ADAM_LR * (m_hat / (jnp.sqrt(v_hat) + ADAM_EPS) + ADAM_WD * w), m, v


def _k_adam(w, mine, theirs, m, v, dep, name):
    r, c = w.shape
    hr = r // 2
    tr = _row_tile(hr, c, 2)
    nh = hr // tr

    def body(w_ref, a_ref, b_ref, m_ref, v_ref, dep_ref, g_ref, d_ref, mo_ref, vo_ref):
        upper = (pl.program_id(0) >= nh).astype(jnp.int32)
        g = jnp.where(upper == lax.axis_index("c"), a_ref[...], b_ref[...])
        g_ref[...] = g
        d_ref[...], mo_ref[...], vo_ref[...] = _adam(w_ref[...], g, m_ref[...], v_ref[...])

    hspec = pl.BlockSpec((tr, c), lambda i: (jnp.where(i >= nh, i - nh, i), 0))
    return _pc(body, name=name, grid=(r // tr,),
               in_specs=[_row(tr, c), hspec, hspec, _row(tr, c), _row(tr, c), _res((8, 128))],
               out_specs=[_row(tr, c)] * 4, out_shape=[_sds((r, c))] * 4)(w, mine, theirs, m, v, dep)


def _k_sum8(a):
    _, n, _ = a.shape

    def body(a_ref, o_ref):
        acc = a_ref[0]
        for k in range(1, NDEV):
            acc = acc + a_ref[k]
        o_ref[...] = acc

    return _pc(body, name="sum_small_grads", grid=(1,), in_specs=[_acc(a.shape)], out_specs=[_acc((n, 128))],
               out_shape=[_sds((n, 128))])(a)[0]


def _k_adam_small(ws, gs, ms, vs):
    n = len(ws)

    def body(*refs):
        for k in range(n):
            w_ref, g_ref, m_ref, v_ref, d_ref, mo_ref, vo_ref = refs[k::n]
            d_ref[...], mo_ref[...], vo_ref[...] = _adam(w_ref[...], g_ref[...], m_ref[...], v_ref[...])

    specs = [_acc(a.shape) for a in ws]
    outs = _pc(body, name="adam_small", grid=(1,), in_specs=specs * 4, out_specs=specs * 3,
               out_shape=[_sds(a.shape) for a in ws] * 3)(*ws, *gs, *ms, *vs)
    return outs[:n], outs[n:2 * n], outs[2 * n:]


def _pack(vals):
    rows = []
    for a in vals:
        flat = a.reshape(-1)
        n = -(-flat.shape[0] // 1024) * 1024
        rows.append(jnp.pad(flat, (0, n - flat.shape[0])).reshape(n // 128, 128))
    return jnp.concatenate(rows, axis=0)


def _unpack(packed, shapes):
    out, off = [], 0
    for sh in shapes:
        size = int(np.prod(sh))
        n = -(-size // 1024) * 1024
        out.append(packed[off // 128:(off + n) // 128].reshape(-1)[:size].reshape(sh))
        off += n
    return out


_WEIGHTS = ["attn_norm", "w_in", "a_q_norm", "a_k_norm", "b_q_norm", "b_k_norm", "b_sinks", "mem_norm", "w_mem_kv",
            "m_q_norm", "m_k_norm", "w_o_a", "w_o_b", "w_o_m", "w_gate", "b_gate", "w_out", "ffn_norm", "w_up",
            "conv_w", "conv_b", "w_down"]
_BIG = ["w_in", "w_mem_kv", "w_o_a", "w_o_b", "w_o_m", "w_gate", "w_out", "w_up", "w_down"]
_SMALL = [n for n in _WEIGHTS if n not in _BIG]


def kernel(x, mem, positions, attn_norm, w_in, a_q_norm, a_k_norm, b_q_norm, b_k_norm, b_sinks, mem_norm, w_mem_kv, m_q_norm, m_k_norm, w_o_a, w_o_b, w_o_m, w_gate, b_gate, w_out, ffn_norm, w_up, conv_w, conv_b, w_down, loss_target, m_attn_norm, m_w_in, m_a_q_norm, m_a_k_norm, m_b_q_norm, m_b_k_norm, m_b_sinks, m_mem_norm, m_w_mem_kv, m_m_q_norm, m_m_k_norm, m_w_o_a, m_w_o_b, m_w_o_m, m_w_gate, m_b_gate, m_w_out, m_ffn_norm, m_w_up, m_conv_w, m_conv_b, m_w_down, v_attn_norm, v_w_in, v_a_q_norm, v_a_k_norm, v_b_q_norm, v_b_k_norm, v_b_sinks, v_mem_norm, v_w_mem_kv, v_m_q_norm, v_m_k_norm, v_w_o_a, v_w_o_b, v_w_o_m, v_w_gate, v_b_gate, v_w_out, v_ffn_norm, v_w_up, v_conv_w, v_conv_b, v_w_down):
    given = dict(locals())
    w = {n: given[n][0] for n in _WEIGHTS}
    m1 = {n: given["m_" + n][0] for n in _WEIGHTS}
    m2 = {n: given["v_" + n][0] for n in _WEIGHTS}

    zeros = jnp.zeros((8, 128), F32)
    w_in_shard = w["w_in"].astype(_MM)
    *w_in_handles, tok = _split_start(_half_copies, [w_in_shard], [(CHIPS,) + w_in_shard.shape], 3, zeros,
                                      "gather_w_in_start")

    def get_w_in(after):
        send, recv, srcs, lands = w_in_handles
        srcs, lands = _split_wait(_half_copies, send, recv, srcs, lands, after, "gather_w_in_wait")
        return _finish_halves(srcs, lands)[0]

    stages = (["w_gate", "w_mem_kv", "w_o_a", "w_o_b", "w_o_m", "w_out"], ["w_up", "w_down", "conv_w"])
    started = []
    for k, names in enumerate(stages):
        shards = [w[n] if n == "conv_w" else w[n].astype(_MM) for n in names]
        *handles, tok = _split_start(_bcast_copies, shards, [(CHIPS,) + a.shape for a in shards], 4, tok,
                                     "gather_start_%d" % k)
        started.append(handles)
    small = {n: (w[n][None, :] if w[n].ndim == 1 else w[n]) for n in _SMALL if n != "conv_w"}
    positions = positions + tok[0:1, 0:1].astype(positions.dtype)

    def get_rest(stage, after):
        send, recv, srcs, lands = started[stage]
        got = _split_wait(_bcast_copies, send, recv, srcs, lands, after, "gather_wait_%d" % stage)[1]
        wts = dict(zip(stages[stage], got))
        for n in ("w_mem_kv", "w_out", "w_down"):
            if n in wts:
                wts[n] = wts[n].reshape(-1, wts[n].shape[-1])
        return wts

    parts, slots, pair, scat, started_pair = {}, {}, [], [], [None]

    def finish_pair(after):
        names, tag, send, recv, srcs, lands = pair.pop()
        full, got = _split_wait(_pair_copies, send, recv, srcs, lands, after, "pair_wait_" + tag)
        mine = [_k_pair_add(f, b, "pair_add_" + n) for n, f, b in zip(names, full, got)]
        shapes = [(3,) + p.shape[1:] for p in mine]
        send, recv, srcs, lands, token = _split_start(_scatter_copies, mine, shapes, 3, zeros, "scatter_start_" + tag)
        scat.append((names, tag, send, recv, srcs, lands))
        return token

    def on_grads(group, after):
        names = list(group)
        tag = "_".join(names)
        token = finish_pair(after) if pair else zeros
        if not group:
            return token
        grads_g = [group[n] for n in names]
        shapes = [(CHIPS, g.shape[1] // 2, g.shape[2]) for g in grads_g]
        send, recv, srcs, lands, token = _split_start(_pair_copies, grads_g, shapes, 1, token, "pair_start_" + tag)
        pair.append((names, tag, send, recv, srcs, lands))
        started_pair[0] = token
        return token

    loss, grad_x, sml = _local_step(x[0], mem[0], positions[0], loss_target[0], small, get_w_in, get_rest, on_grads)

    packed = _pack([sml[n] for n in _SMALL] + [loss.reshape(1)])
    me = 4 * lax.axis_index("x") + 2 * lax.axis_index("y") + lax.axis_index("c")
    land = lax.dynamic_update_slice(jnp.zeros((NDEV,) + packed.shape, F32), packed[None], (me, 0, 0))
    *small_h, tok = _split_start(_small_copies, [packed], None, NDEV - 1, zeros, "gather_small_start", lands=[land])
    started_pair[0] = started_pair[0] + tok

    early = [n for names, *_ in scat for n in names]
    for names, tag, send, recv, srcs, lands in scat:
        mine, got = _split_wait(_scatter_copies, send, recv, srcs, lands, started_pair[0], "scatter_wait_" + tag)
        parts.update(zip(names, mine))
        slots.update(zip(names, got))
    scat.clear()
    reduced = {n: _k_chip_sum(parts[n], slots[n], "chip_add_" + n) for n in early}
    halves = [reduced[n] for n in early]
    *join, tok = _split_start(_join_copies, halves, [a.shape for a in halves], 1, zeros, "pair_join_start_early")
    grads = {}

    delta, new_m, new_v = {}, {}, {}
    dep = finish_pair(tok)
    mine, got = _split_wait(_join_copies, *join, dep, "pair_join_wait_early")
    reduced.update(zip(early, mine))
    theirs = dict(zip(early, got))
    for n in early:
        grads[n], delta[n], new_m[n], new_v[n] = _k_adam(w[n], reduced[n], theirs[n], m1[n], m2[n], dep, "adam_" + n)
        dep = delta[n]
    gathered = _split_wait(_small_copies, *small_h, dep, "gather_small_wait")[1][0]
    shapes = [sml[n].shape for n in _SMALL] + [(1,)]
    *gsmall, loss = _unpack(_k_sum8(gathered), shapes)
    loss = loss[0]
    gsm = dict(zip(_SMALL, gsmall))
    nu = w["conv_w"].shape[1]
    chip = 2 * lax.axis_index("x") + lax.axis_index("y")
    gsm["conv_w"] = lax.dynamic_slice_in_dim(gsm["conv_w"], chip * nu, nu, axis=1)
    for n in _SMALL:
        grads[n] = gsm[n].reshape(w[n].shape)
    as2d = lambda d: [d[n][None, :] if d[n].ndim == 1 else d[n] for n in _SMALL]
    for dst, outs in zip((delta, new_m, new_v), _k_adam_small(as2d(w), as2d(grads), as2d(m1), as2d(m2))):
        dst.update((n, a.reshape(w[n].shape)) for n, a in zip(_SMALL, outs))
    late, tag, send, recv, srcs, lands = scat.pop()
    mine, got = _split_wait(_scatter_copies, send, recv, srcs, lands, dep, "scatter_wait_" + tag)
    for n, a, b in zip(late, mine, got):
        reduced[n] = _k_chip_sum(a, b, "chip_add_" + n)
    theirs.update(zip(late, _pair_join([reduced[n] for n in late], "grad_pair_join_late")))
    for n in late:
        grads[n], delta[n], new_m[n], new_v[n] = _k_adam(w[n], reduced[n], theirs[n], m1[n], m2[n], zeros, "adam_" + n)

    lead = lambda d: [d[n][None] for n in _WEIGHTS]
    return (loss, grad_x[None], *lead(grads), *lead(delta), *lead(new_m), *lead(new_v))
```

```python
import math

import jax
import jax.numpy as jnp
import numpy as np
from jax import lax
from jax.experimental import pallas as pl
from jax.experimental.pallas import tpu as pltpu

F32 = jnp.float32
_MM = jnp.bfloat16
_WIRE = jnp.bfloat16

D_MODEL = 1024
HEAD = 64
BLK = 128
A_GROUPS = ((128, 1), (512, 4), (2048, 16))
A_HEADS = 4
A_W = A_HEADS * HEAD
B_QH = 8
B_KVH = 2
B_WINDOW = 128
M_HEADS = 4
M_HD = 128
M_W = M_HEADS * M_HD
D_FF = 2816
EPS = 1e-6
NEG = -1e30
ROPE_THETA = 500000.0
ROPE_ROT = 16
CHIPS = 4
NDEV = 8
ADAM_LR, ADAM_B1, ADAM_B2, ADAM_EPS, ADAM_WD, ADAM_STEP = 0.001, 0.9, 0.999, 1e-08, 0.01, 10
VMEM_LIMIT = 58 * 1024 * 1024
MESH = pl.DeviceIdType.MESH


def _pc(body, *, name, grid, in_specs, out_specs, out_shape, scratch=()):
    return pl.pallas_call(
        body, name=name, grid=grid, in_specs=in_specs, out_specs=out_specs, out_shape=out_shape,
        scratch_shapes=list(scratch),
        compiler_params=pltpu.CompilerParams(dimension_semantics=("arbitrary",) * len(grid),
                                             vmem_limit_bytes=VMEM_LIMIT))


def _row(ts, c, col=0):
    return pl.BlockSpec((ts, c), lambda i: (i, col))


def _res(shape):
    n = len(shape)
    return pl.BlockSpec(tuple(shape), lambda i: (0,) * n, pipeline_mode=pl.Buffered(1))


def _acc(shape):
    n = len(shape)
    return pl.BlockSpec(tuple(shape), lambda i: (0,) * n, pipeline_mode=pl.Buffered(1))


def _sds(shape, dtype=F32):
    return jax.ShapeDtypeStruct(tuple(shape), dtype)


def _dot(a, b):
    return jnp.dot(a.astype(_MM), b.astype(_MM), preferred_element_type=F32)


def _dot_nt(a, b):
    return lax.dot_general(a.astype(_MM), b.astype(_MM), (((1,), (1,)), ((), ())), preferred_element_type=F32)


def _dot_tn(a, b):
    return lax.dot_general(a.astype(_MM), b.astype(_MM), (((0,), (0,)), ((), ())), preferred_element_type=F32)


def _sum8(v):
    ts, c = v.shape
    return jnp.sum(v.reshape(ts // 8, 8, c), axis=0)


def _sigmoid(z):
    return 1.0 / (1.0 + jnp.exp(-z))


def _rms(x):
    r = lax.rsqrt(jnp.mean(x * x, axis=-1, keepdims=True) + EPS)
    return x * r, r


def _rms_bwd(dy, xh, r, gain):
    z = dy * gain
    return r * (z - xh * jnp.mean(z * xh, axis=-1, keepdims=True))


def _split_hi_lo(v):
    hi = v.astype(_MM)
    return hi, (v - hi.astype(F32)).astype(_MM)


def _lane_head(shape):
    return lax.shift_right_logical(lax.broadcasted_iota(jnp.int32, shape, len(shape) - 1), 6)


def _seg_sum64(v):
    w = v.shape[1]
    e = jnp.where(_lane_head((w, w)) == lax.shift_right_logical(lax.broadcasted_iota(jnp.int32, (w, w), 0), 6),
                  1.0, 0.0).astype(_MM)
    hi, lo = _split_hi_lo(v)
    return jnp.dot(hi, e, preferred_element_type=F32) + jnp.dot(lo, e, preferred_element_type=F32)


def _seg_norm(x, seg):
    if seg == HEAD:
        r = lax.rsqrt(_seg_sum64(x * x) * (1.0 / HEAD) + EPS)
        return x * r, r
    w = x.shape[1]
    xh, rr = [], []
    for s in range(w // seg):
        xs = x[:, s * seg:(s + 1) * seg]
        r = lax.rsqrt(jnp.mean(xs * xs, axis=-1, keepdims=True) + EPS)
        xh.append(xs * r)
        rr.append(jnp.broadcast_to(r, xs.shape))
    return jnp.concatenate(xh, axis=1), jnp.concatenate(rr, axis=1)


def _seg_mean(v, seg):
    if seg == HEAD:
        return _seg_sum64(v) * (1.0 / HEAD)
    w = v.shape[1]
    out = []
    for s in range(w // seg):
        vs = v[:, s * seg:(s + 1) * seg]
        out.append(jnp.broadcast_to(jnp.mean(vs, axis=-1, keepdims=True), vs.shape))
    return jnp.concatenate(out, axis=1)


def _rope(t, c, sa, sb):
    out = []
    for cb in range(t.shape[1] // 128):
        tc = t[:, cb * 128:(cb + 1) * 128]
        out.append(tc * c + pltpu.roll(tc, 120, 1) * sa + pltpu.roll(tc, 8, 1) * sb)
    return jnp.concatenate(out, axis=1) if len(out) > 1 else out[0]


def _rope_bwd(dy, c, sa, sb):
    out = []
    for cb in range(dy.shape[1] // 128):
        dc = dy[:, cb * 128:(cb + 1) * 128]
        out.append(dc * c + pltpu.roll(dc * sa, 8, 1) + pltpu.roll(dc * sb, 120, 1))
    return jnp.concatenate(out, axis=1) if len(out) > 1 else out[0]


def _rope_consts():
    half = ROPE_ROT // 2
    c = np.float32(-2.0 * math.log(ROPE_THETA) / ROPE_ROT)
    freqs = np.exp(np.arange(half, dtype=np.float32) * c).astype(np.float32)
    place = np.zeros((3, half, 128), np.float32)
    ones = np.zeros((1, 128), np.float32)
    for lane in range(128):
        d = lane % HEAD
        if d < half:
            place[0, d, lane], place[1, d, lane] = 1.0, -1.0
        elif d < ROPE_ROT:
            place[0, d - half, lane], place[2, d - half, lane] = 1.0, 1.0
        else:
            ones[0, lane] = 1.0
    return np.tile(freqs[:, None], (1, 128)), place, ones


def _rope_tables(pos_rows):
    r = pos_rows.shape[0]
    tr = min(1024, r)
    freqs, place, ones = _rope_consts()

    def split3(v):
        hi, mid = _split_hi_lo(v)
        lo = (v - hi.astype(F32) - mid.astype(F32)).astype(_MM)
        return hi, mid, lo

    def body(p_ref, f_ref, e_ref, one_ref, c_ref, sa_ref, sb_ref):
        ang = jnp.concatenate([p_ref[j:j + 1, :].astype(F32) * f_ref[...] for j in range(tr // 128)], axis=1)
        cos, sin = jnp.cos(ang), jnp.sin(ang)
        for ref, k, v in ((c_ref, 0, cos), (sa_ref, 1, sin), (sb_ref, 2, sin)):
            e = e_ref[k].astype(_MM)
            out = sum(_dot_tn(part, e) for part in split3(v))
            ref[...] = out + one_ref[...] if k == 0 else out

    return _pc(body, name="rope_tables", grid=(r // tr,),
               in_specs=[pl.BlockSpec((tr // 128, 128), lambda i: (i, 0)), _acc((ROPE_ROT // 2, 128)),
                         _acc((3, ROPE_ROT // 2, 128)), _acc((1, 128))],
               out_specs=[_row(tr, 128)] * 3, out_shape=[_sds((r, 128))] * 3)(
                   pos_rows.reshape(r // 128, 128), jnp.asarray(freqs), jnp.asarray(place), jnp.asarray(ones))


def _k_in(x, g1, w_in):
    s = x.shape[0]
    ts = min(512, s)
    nin = w_in.shape[2]
    ncol = CHIPS * nin
    a_cols = 3 * A_W
    offs = [0, a_cols, 2 * a_cols, 3 * a_cols, 3 * a_cols + B_QH * HEAD,
            3 * a_cols + (B_QH + B_KVH) * HEAD, 3 * a_cols + (B_QH + 2 * B_KVH) * HEAD, ncol]

    def body(x_ref, g_ref, wi_ref, h_ref, a0, a1, a2, qb, kb, vb, mq, p_scr):
        xh, _ = _rms(x_ref[...])
        h = (xh * g_ref[...]).astype(_MM)
        h_ref[...] = h
        for j in range(CHIPS):
            p_scr[:, j * nin:(j + 1) * nin] = jnp.dot(h, wi_ref[j], preferred_element_type=F32)
        for k, ref in enumerate((a0, a1, a2, qb, kb, vb, mq)):
            ref[...] = p_scr[:, offs[k]:offs[k + 1]]

    widths = [offs[k + 1] - offs[k] for k in range(7)]
    return _pc(
        body, name="in_proj", grid=(s // ts,),
        in_specs=[_row(ts, D_MODEL), _res((1, D_MODEL)), _res(w_in.shape)],
        out_specs=[_row(ts, D_MODEL)] + [_row(ts, w) for w in widths],
        out_shape=[_sds((s, D_MODEL), _MM)] + [_sds((s, w)) for w in widths],
        scratch=[pltpu.VMEM((ts, ncol), F32)])(x, g1, w_in)


def _k_prep(srcs, gq, gk, tabs, tab_row, *, wq, wk, rows_per_gain, name):
    rows = srcs[0][0].shape[0]
    ts = min(512, rows)

    def body(q_ref, k_ref, v_ref, gq_ref, gk_ref, c_ref, sa_ref, sb_ref, qn_ref, kn_ref, vn_ref):
        c, sa, sb = c_ref[...], sa_ref[...], sb_ref[...]
        qh, _ = _seg_norm(q_ref[...], HEAD)
        qn_ref[...] = _rope(qh * gq_ref[...], c, sa, sb).astype(_MM)
        kh, _ = _seg_norm(k_ref[...], HEAD)
        kn_ref[...] = _rope(kh * gk_ref[...], c, sa, sb).astype(_MM)
        vn_ref[...] = v_ref[...].astype(_MM)

    gspec = lambda w: pl.BlockSpec((None, 1, w), lambda i: ((i * ts) // rows_per_gain, 0, 0))
    return _pc(
        body, name=name, grid=(rows // ts,),
        in_specs=[_row(ts, wq, srcs[0][1]), _row(ts, wk, srcs[1][1]), _row(ts, wk, srcs[2][1]),
                  gspec(wq), gspec(wk)] + [pl.BlockSpec((ts, 128), lambda i: (i + tab_row // ts, 0))] * 3,
        out_specs=[_row(ts, wq), _row(ts, wk), _row(ts, wk)],
        out_shape=[_sds((rows, wq), _MM), _sds((rows, wk), _MM), _sds((rows, wk), _MM)])(
            srcs[0][0], srcs[1][0], srcs[2][0], gq, gk, *tabs)


def _first_flag(b, segs, nb):
    first = b >= nb
    for k, (start, period) in enumerate(segs):
        end = segs[k + 1][0] if k + 1 < len(segs) else nb
        first = first | ((b >= start) & (b < end) & (lax.rem(b - start, jnp.int32(period)) == 0))
    return first


def _band_bias(thr, with_cur):
    qi = lax.broadcasted_iota(jnp.int32, (BLK, BLK), 0)
    kj = lax.broadcasted_iota(jnp.int32, (BLK, BLK), 1)
    prev = jnp.where(kj >= qi + thr, 0.0, NEG)
    return jnp.concatenate([prev, jnp.where(kj <= qi, 0.0, NEG)], axis=1) if with_cur else prev


def _blockdiag(t4):
    head = _lane_head((1, A_W))
    return jnp.concatenate([t4 * jnp.where(head == h, 1.0, 0.0).astype(t4.dtype) for h in range(A_HEADS)], axis=0)


def _fold_diag(t, n):
    head = _lane_head((n, A_W))
    out = t[3 * n:4 * n]
    for h in (2, 1, 0):
        out = jnp.where(head == h, t[h * n:(h + 1) * n], out)
    return out


def _expand_heads(cols):
    n = cols[0].shape[0]
    head = _lane_head((n, A_W))
    out = jnp.broadcast_to(cols[3], (n, A_W))
    for h in (2, 1, 0):
        out = jnp.where(head == h, cols[h], out)
    return out


def _unit_kv(pieces, u, shared):
    cols = slice(u * HEAD, (u + 1) * HEAD) if shared else slice(u * A_W, (u + 1) * A_W)
    rows = [ref[rs, cols] for ref, rs in pieces]
    k = rows[0] if len(rows) == 1 else jnp.concatenate(rows, axis=0)
    return jnp.concatenate([k] * A_HEADS, axis=1) if shared else k


_LO, _HI, _BOTH = slice(0, BLK), slice(BLK, 2 * BLK), slice(0, 2 * BLK)


def _k_band_fwd(qn, kn, vn, *, hq, hk, max_dist, segs, sink, name):
    rows = qn.shape[0]
    nb = rows // BLK
    units = hq // A_HEADS
    shared = hk != hq
    wq, wk = hq * HEAD, hk * HEAD
    scale = HEAD ** -0.5

    def body(*refs):
        if sink is None:
            q_ref, kc_ref, kp_ref, vc_ref, vp_ref, o_ref, l_ref = refs
        else:
            q_ref, kc_ref, kp_ref, vc_ref, vp_ref, sk_ref, o_ref, l_ref = refs
        i = pl.program_id(0)
        for half, rs in enumerate((_LO, _HI)):
            bias = _band_bias(jnp.where(_first_flag(2 * i + half, segs, nb), 1 << 20, BLK - max_dist), True)
            kpieces = ((kp_ref, _LO), (kc_ref, _LO)) if half == 0 else ((kc_ref, _BOTH),)
            vpieces = ((vp_ref, _LO), (vc_ref, _LO)) if half == 0 else ((vc_ref, _BOTH),)
            for u in range(units):
                us = slice(u * A_W, (u + 1) * A_W)
                kb = _blockdiag(_unit_kv(kpieces, u, shared))
                vb = _blockdiag(_unit_kv(vpieces, u, shared))
                s_all = _dot_nt(q_ref[rs, us], kb) * scale
                ps, ls = [], []
                for h in range(A_HEADS):
                    s = s_all[:, h * 2 * BLK:(h + 1) * 2 * BLK] + bias
                    m = jnp.max(s, axis=-1, keepdims=True)
                    e = jnp.exp(s - m)
                    lse = m + jnp.log(jnp.sum(e, axis=-1, keepdims=True))
                    if sink is not None:
                        sk = sk_ref[u * A_HEADS + h]
                        mx = jnp.maximum(lse, sk)
                        lse = mx + jnp.log(jnp.exp(lse - mx) + jnp.exp(sk - mx))
                    ps.append((e * jnp.exp(m - lse)).astype(_MM))
                    ls.append(lse)
                o_ref[rs, us] = _dot(jnp.concatenate(ps, axis=1), vb)
                l_ref[rs, us] = _expand_heads(ls)

    two = lambda w: pl.BlockSpec((2 * BLK, w), lambda i: (i, 0))
    prev = lambda w: pl.BlockSpec((BLK, w), lambda i: (jnp.maximum(2 * i - 1, 0), 0))
    in_specs = [two(wq), two(wk), prev(wk), two(wk), prev(wk)]
    args = [qn, kn, kn, vn, vn]
    if sink is not None:
        in_specs.append(pl.BlockSpec(memory_space=pltpu.SMEM))
        args.append(sink)
    return _pc(body, name=name, grid=(nb // 2,), in_specs=in_specs, out_specs=[two(wq), two(wq)],
               out_shape=[_sds((rows, wq)), _sds((rows, wq))])(*args)


def _k_memkv(mem, mem_norm, w_kv, m_k_norm):
    n = mem.shape[0]

    def body(m_ref, g_ref, w_ref, gk_ref, mn_ref, kv_ref, mk_ref, mv_ref):
        mh, _ = _rms(m_ref[...])
        mn = (mh * g_ref[...]).astype(_MM)
        mn_ref[...] = mn
        kv = jnp.dot(mn, w_ref[...], preferred_element_type=F32)
        kv_ref[...] = kv
        kh, _ = _seg_norm(kv[:, :M_W], M_HD)
        mk_ref[...] = (kh * gk_ref[...]).astype(_MM)
        mv_ref[...] = kv[:, M_W:].astype(_MM)

    return _pc(body, name="mem_kv", grid=(1,),
               in_specs=[_acc((n, D_MODEL)), _acc((1, D_MODEL)), _acc(w_kv.shape), _acc((1, M_W))],
               out_specs=[_acc((n, D_MODEL)), _acc((n, 2 * M_W)), _acc((n, M_W)), _acc((n, M_W))],
               out_shape=[_sds((n, D_MODEL), _MM), _sds((n, 2 * M_W)), _sds((n, M_W), _MM), _sds((n, M_W), _MM)])(
                   mem, mem_norm, w_kv, m_k_norm)


def _mem_probs(q, mk):
    sc = _dot_nt(q, mk) * (M_HD ** -0.5)
    e = jnp.exp(sc - jnp.max(sc, axis=-1, keepdims=True))
    return e / jnp.sum(e, axis=-1, keepdims=True)


def _k_mem_fwd(m_q, gq, mk, mv):
    s = m_q.shape[0]
    n = mk.shape[0]
    ts = min(512, s)

    def body(q_ref, g_ref, mk_ref, mv_ref, o_ref):
        qh, _ = _seg_norm(q_ref[...], M_HD)
        qn = (qh * g_ref[...]).astype(_MM)
        for h in range(M_HEADS):
            hs = slice(h * M_HD, (h + 1) * M_HD)
            o_ref[:, hs] = _dot(_mem_probs(qn[:, hs], mk_ref[:, hs]), mv_ref[:, hs])

    return _pc(body, name="mem_attn", grid=(s // ts,),
               in_specs=[_row(ts, M_W), _res((1, M_W)), _res((n, M_W)), _res((n, M_W))],
               out_specs=[_row(ts, M_W)], out_shape=[_sds((s, M_W))])(m_q, gq, mk, mv)[0]


def _group_weights(l0, l1, l2):
    m = jnp.maximum(jnp.maximum(l0, l1), l2)
    e0, e1, e2 = jnp.exp(l0 - m), jnp.exp(l1 - m), jnp.exp(l2 - m)
    inv = 1.0 / (e0 + e1 + e2)
    return e0 * inv, e1 * inv, e2 * inv


def _branch_products(oa, ob, om, woa_ref, wob_ref, wom_ref, j):
    return _dot(oa, woa_ref[j]), _dot(ob, wob_ref[j]), _dot(om, wom_ref[j])


def _k_merge(og, lg, o_b, o_m, h, x, w_gate, b_gate, w_oa, w_ob, w_om, w_out, g2):
    s = x.shape[0]
    ts = min(512, s)
    nc = w_oa.shape[2]
    ng = w_gate.shape[2]

    def body(o0, o1, o2, l0, l1, l2, ob_ref, om_ref, h_ref, x_ref, wg, bg_ref, woa, wob, wom, wout, g_ref,
             oa_ref, mer_ref, x1_ref, h2_ref, gt_ref, m_scr):
        h = h_ref[...]
        for j in range(CHIPS):
            z = jnp.dot(h, wg[j], preferred_element_type=F32) + bg_ref[:, j * ng:(j + 1) * ng]
            gt_ref[:, j * ng:(j + 1) * ng] = _sigmoid(z)
        w0, w1, w2 = _group_weights(l0[...], l1[...], l2[...])
        oa = w0 * o0[...] + w1 * o1[...] + w2 * o2[...]
        oa_ref[...] = oa
        ob, om = ob_ref[...], om_ref[...]
        for j in range(CHIPS):
            pa, pb, pm = _branch_products(oa, ob, om, woa, wob, wom, j)
            cs = lambda br: slice(br * D_MODEL + j * nc, br * D_MODEL + (j + 1) * nc)
            m_scr[:, j * nc:(j + 1) * nc] = gt_ref[:, cs(0)] * pa + gt_ref[:, cs(1)] * pb + gt_ref[:, cs(2)] * pm
        mer = m_scr[...].astype(_MM)
        mer_ref[...] = mer
        x1 = x_ref[...] + jnp.dot(mer, wout[...], preferred_element_type=F32)
        x1_ref[...] = x1
        xh, _ = _rms(x1)
        h2_ref[...] = (xh * g_ref[...]).astype(_MM)

    return _pc(
        body, name="merge_out", grid=(s // ts,),
        in_specs=[_row(ts, A_W)] * 6 + [_row(ts, B_QH * HEAD), _row(ts, M_W), _row(ts, D_MODEL), _row(ts, D_MODEL),
                                         _res(w_gate.shape), _res(b_gate.shape), _res(w_oa.shape), _res(w_ob.shape),
                                         _res(w_om.shape), _res(w_out.shape), _res((1, D_MODEL))],
        out_specs=[_row(ts, A_W), _row(ts, D_MODEL), _row(ts, D_MODEL), _row(ts, D_MODEL), _row(ts, CHIPS * ng)],
        out_shape=[_sds((s, A_W)), _sds((s, D_MODEL), _MM), _sds((s, D_MODEL)), _sds((s, D_MODEL), _MM),
                   _sds((s, CHIPS * ng))],
        scratch=[pltpu.VMEM((ts, D_MODEL), F32)])(
            *og, *lg, o_b, o_m, h, x, w_gate, b_gate, w_oa, w_ob, w_om, w_out, g2)


def _k_up(h2, w_up):
    s = h2.shape[0]
    ts = min(256, s)
    nu = w_up.shape[2]

    def body(h_ref, w_ref, u_ref):
        h = h_ref[...]
        for j in range(CHIPS):
            u_ref[:, j * nu:(j + 1) * nu] = jnp.dot(h, w_ref[j], preferred_element_type=F32)

    return _pc(body, name="up_proj", grid=(s // ts,), in_specs=[_row(ts, D_MODEL), _res(w_up.shape)],
               out_specs=[_row(ts, CHIPS * nu)], out_shape=[_sds((s, CHIPS * nu))])(h2, w_up)[0]


def _shift_down(v, halo, k):
    rolled = pltpu.roll(v, k, 0)
    row = lax.broadcasted_iota(jnp.int32, (8, v.shape[1]), 0)
    slab = rolled[0:8]
    for r in range(k):
        slab = jnp.where(row == r, halo[8 - k + r:8 - k + r + 1, :], slab)
    return jnp.concatenate([slab, rolled[8:]], axis=0)


def _shift_up(v, halo, k):
    ts = v.shape[0]
    rolled = pltpu.roll(v, ts - k, 0)
    row = lax.broadcasted_iota(jnp.int32, (8, v.shape[1]), 0)
    slab = rolled[ts - 8:]
    for r in range(k):
        slab = jnp.where(row == 8 - k + r, halo[r:r + 1, :], slab)
    return jnp.concatenate([rolled[:ts - 8], slab], axis=0)


def _k_ffn(u, conv_w, conv_b, w_down, w_down_t, x1, target):
    s = u.shape[0]
    ts = min(256, s)
    nu = conv_w.shape[2]
    half = CHIPS // 2

    def body(u_ref, uh_ref, cw_ref, cb_ref, wd_ref, wdt_ref, x1_ref, t_ref, dy_ref, f_ref, dc_ref, loss_ref, c_scr,
             f_scr, s_scr):
        i = pl.program_id(0)
        halo = jnp.where(i > 0, uh_ref[...], 0.0)
        for j in range(CHIPS):
            cs = slice(j * nu, (j + 1) * nu)
            uj = u_ref[:, cs]
            hj = halo[:, cs]
            c_scr[:, cs] = (cb_ref[:, cs] + cw_ref[j, 0:1, :] * _shift_down(uj, hj, 2)
                            + cw_ref[j, 1:2, :] * _shift_down(uj, hj, 1) + cw_ref[j, 2:3, :] * uj)
        for j in range(half):
            a = c_scr[:, j * nu:(j + 1) * nu]
            g = c_scr[:, (half + j) * nu:(half + j + 1) * nu]
            sa = _sigmoid(a)
            s_scr[:, j * nu:(j + 1) * nu] = sa
            f_scr[:, j * nu:(j + 1) * nu] = (a * sa * g).astype(_MM)
        f = f_scr[...]
        f_ref[...] = f
        y = x1_ref[...] + jnp.dot(f, wd_ref[...], preferred_element_type=F32)
        err = y - t_ref[...]
        dy = err * (1.0 / D_MODEL)
        dy_ref[...] = dy

        @pl.when(i == 0)
        def _():
            loss_ref[...] = jnp.zeros_like(loss_ref)

        loss_ref[...] += _sum8(err * err)
        df = _dot(dy, wdt_ref[...])
        for j in range(half):
            a = c_scr[:, j * nu:(j + 1) * nu]
            g = c_scr[:, (half + j) * nu:(half + j + 1) * nu]
            sa = s_scr[:, j * nu:(j + 1) * nu]
            dfj = df[:, j * nu:(j + 1) * nu]
            dc_ref[:, j * nu:(j + 1) * nu] = dfj * g * (sa * (1.0 + a * (1.0 - sa)))
            dc_ref[:, (half + j) * nu:(half + j + 1) * nu] = dfj * (a * sa)

    wide = CHIPS * nu
    return _pc(
        body, name="conv_ffn", grid=(s // ts,),
        in_specs=[_row(ts, wide), pl.BlockSpec((8, wide), lambda i: (jnp.maximum(i * (ts // 8) - 1, 0), 0)),
                  _res(conv_w.shape), _res((1, wide)), _res(w_down.shape), _res(w_down_t.shape), _row(ts, D_MODEL),
                  _row(ts, D_MODEL)],
        out_specs=[_row(ts, D_MODEL), _row(ts, D_FF), _row(ts, wide), _acc((8, D_MODEL))],
        out_shape=[_sds((s, D_MODEL)), _sds((s, D_FF), _MM), _sds((s, wide)), _sds((8, D_MODEL))],
        scratch=[pltpu.VMEM((ts, wide), F32), pltpu.VMEM((ts, D_FF), _MM), pltpu.VMEM((ts, D_FF), F32)])(
            u, u, conv_w, conv_b, w_down, w_down_t, x1, target)


def _k_conv_bwd(dc, u, conv_w, w_up, x1, g2, dy):
    s = u.shape[0]
    ts = min(256, s)
    nu = conv_w.shape[2]
    wide = CHIPS * nu
    last = s // ts - 1

    def body(dc_ref, dn_ref, u_ref, cw_ref, wu_ref, x1_ref, g_ref, dy_ref, dx1_ref, du_ref, cacc_ref, gacc_ref):
        i = pl.program_id(0)

        @pl.when(i == 0)
        def _():
            cacc_ref[...] = jnp.zeros_like(cacc_ref)
            gacc_ref[...] = jnp.zeros_like(gacc_ref)

        dhalo = jnp.where(i < last, dn_ref[...], 0.0)
        dh2 = jnp.zeros((ts, D_MODEL), F32)
        for j in range(CHIPS):
            cs = slice(j * nu, (j + 1) * nu)
            dcj, uj = dc_ref[:, cs], u_ref[:, cs]
            dc1, dc2 = _shift_up(dcj, dhalo[:, cs], 1), _shift_up(dcj, dhalo[:, cs], 2)
            cacc_ref[0, :, cs] += _sum8(dcj)
            cacc_ref[1, :, cs] += _sum8(dc2 * uj)
            cacc_ref[2, :, cs] += _sum8(dc1 * uj)
            cacc_ref[3, :, cs] += _sum8(dcj * uj)
            du = (cw_ref[j, 2:3, :] * dcj + cw_ref[j, 1:2, :] * dc1 + cw_ref[j, 0:1, :] * dc2).astype(_MM)
            du_ref[:, cs] = du
            dh2 = dh2 + _dot_nt(du, wu_ref[j])
        xh, r = _rms(x1_ref[...])
        gacc_ref[...] += _sum8(dh2 * xh)
        dx1_ref[...] = dy_ref[...] + _rms_bwd(dh2, xh, r, g_ref[...])

    return _pc(
        body, name="conv_up_bwd", grid=(s // ts,),
        in_specs=[_row(ts, wide),
                  pl.BlockSpec((8, wide), lambda i: (jnp.minimum((i + 1) * (ts // 8), s // 8 - 1), 0)),
                  _row(ts, wide), _res(conv_w.shape), _res(w_up.shape), _row(ts, D_MODEL), _res((1, D_MODEL)),
                  _row(ts, D_MODEL)],
        out_specs=[_row(ts, D_MODEL), _row(ts, wide), _acc((4, 8, wide)), _acc((8, D_MODEL))],
        out_shape=[_sds((s, D_MODEL)), _sds((s, wide), _MM), _sds((4, 8, wide)), _sds((8, D_MODEL))])(
            dc, dc, u, conv_w, w_up, x1, g2, dy)


def _k_merge_bwd(dx1, og, lg, o_a, o_b, o_m, gates, merged, h, w_gate_shape, w_oa, w_ob, w_om, w_out, dep):
    s = dx1.shape[0]
    ts = min(256, s)
    nc = w_oa.shape[2]
    ng = w_gate_shape[2]

    def body(dx_ref, o0, o1, o2, l0, l1, l2, oa_ref, ob_ref, om_ref, gt_ref, mer_ref, h_ref, woa, wob, wom, wout, dep_ref,
             dgp_ref, dog0, dog1, dog2, dl0, dl1, dl2, dob_ref, dom_ref, bacc_ref, dwa_ref, dwb_ref, dwm_ref, dwo_ref,
             dwg_ref):
        i = pl.program_id(0)

        @pl.when(i == 0)
        def _():
            for ref in (bacc_ref, dwa_ref, dwb_ref, dwm_ref, dwo_ref, dwg_ref):
                ref[...] = jnp.zeros_like(ref)

        dx = dx_ref[...]
        h = h_ref[...]
        dwo_ref[...] += _dot_tn(mer_ref[...], dx)
        dmer = _dot_nt(dx, wout[...])
        oa, ob, om = oa_ref[...], ob_ref[...], om_ref[...]
        doa = jnp.zeros((ts, A_W), F32)
        dob = jnp.zeros((ts, B_QH * HEAD), F32)
        dom = jnp.zeros((ts, M_W), F32)
        for j in range(CHIPS):
            prods = _branch_products(oa, ob, om, woa, wob, wom, j)
            dmj = dmer[:, j * nc:(j + 1) * nc]
            dps = []
            for br, (p, o, dw_ref) in enumerate(zip(prods, (oa, ob, om), (dwa_ref, dwb_ref, dwm_ref))):
                cs = slice(br * D_MODEL + j * nc, br * D_MODEL + (j + 1) * nc)
                gt = gt_ref[:, cs]
                dgp = dmj * p * gt * (1.0 - gt)
                dgp_ref[:, cs] = dgp.astype(_MM)
                bacc_ref[:, cs] += _sum8(dgp)
                blk, off = divmod(cs.start, ng)
                dwg_ref[blk, :, off:off + nc] += _dot_tn(h, dgp)
                dp = (dmj * gt).astype(_MM)
                dw_ref[j] += _dot_tn(o, dp)
                dps.append(dp)
            doa = doa + _dot_nt(dps[0], woa[j])
            dob = dob + _dot_nt(dps[1], wob[j])
            dom = dom + _dot_nt(dps[2], wom[j])
        dob_ref[...] = dob
        dom_ref[...] = dom
        ws = _group_weights(l0[...], l1[...], l2[...])
        dsum = _seg_mean(doa * oa, HEAD) * float(HEAD)
        for w, dref, lref in zip(ws, (dog0, dog1, dog2), (dl0, dl1, dl2)):
            dref[...] = w * doa
            lref[...] = w * dsum

    return _pc(
        body, name="merge_out_bwd", grid=(s // ts,),
        in_specs=[_row(ts, D_MODEL)] + [_row(ts, A_W)] * 7 + [_row(ts, B_QH * HEAD), _row(ts, M_W), _row(ts, 3 * D_MODEL),
                                                              _row(ts, D_MODEL), _row(ts, D_MODEL), _res(w_oa.shape),
                                                              _res(w_ob.shape), _res(w_om.shape), _res(w_out.shape),
                                                              _res((8, 128))],
        out_specs=[_row(ts, 3 * D_MODEL)] + [_row(ts, A_W)] * 6
        + [_row(ts, B_QH * HEAD), _row(ts, M_W), _acc((8, 3 * D_MODEL)), _acc(w_oa.shape), _acc(w_ob.shape),
           _acc(w_om.shape), _acc(w_out.shape), _acc(w_gate_shape)],
        out_shape=[_sds((s, 3 * D_MODEL), _MM)] + [_sds((s, A_W))] * 6
        + [_sds((s, B_QH * HEAD)), _sds((s, M_W)), _sds((8, 3 * D_MODEL)), _sds(w_oa.shape), _sds(w_ob.shape),
           _sds(w_om.shape), _sds(w_out.shape), _sds(w_gate_shape)])(
            dx1, *og, *lg, o_a, o_b, o_m, gates, merged, h, w_oa, w_ob, w_om, w_out, dep)


def _k_mem_bwd(m_q, gq, mk, mv, o_m, do_m):
    s = m_q.shape[0]
    n = mk.shape[0]
    ts = min(512, s)
    scale = M_HD ** -0.5

    def body(q_ref, g_ref, mk_ref, mv_ref, o_ref, do_ref, dq_ref, dmk_ref, dmv_ref, gacc_ref):
        i = pl.program_id(0)

        @pl.when(i == 0)
        def _():
            dmk_ref[...] = jnp.zeros_like(dmk_ref)
            dmv_ref[...] = jnp.zeros_like(dmv_ref)
            gacc_ref[...] = jnp.zeros_like(gacc_ref)

        gain = g_ref[...]
        qh, r = _seg_norm(q_ref[...], M_HD)
        qn = (qh * gain).astype(_MM)
        do = do_ref[...]
        delta = _seg_mean(do * o_ref[...], M_HD) * float(M_HD)
        dqn = []
        for h in range(M_HEADS):
            hs = slice(h * M_HD, (h + 1) * M_HD)
            p = _mem_probs(qn[:, hs], mk_ref[:, hs])
            dp = _dot_nt(do[:, hs], mv_ref[:, hs])
            ds = (p * (dp - delta[:, hs][:, 0:1]) * scale).astype(_MM)
            dqn.append(_dot(ds, mk_ref[:, hs]))
            dmk_ref[:, hs] += _dot_tn(ds, qn[:, hs])
            dmv_ref[:, hs] += _dot_tn(p, do[:, hs])
        dqn = jnp.concatenate(dqn, axis=1)
        gacc_ref[...] += _sum8(dqn * qh)
        z = dqn * gain
        dq_ref[...] = (r * (z - qh * _seg_mean(z * qh, M_HD))).astype(_MM)

    return _pc(
        body, name="mem_attn_bwd", grid=(s // ts,),
        in_specs=[_row(ts, M_W), _res((1, M_W)), _res((n, M_W)), _res((n, M_W)), _row(ts, M_W), _row(ts, M_W)],
        out_specs=[_row(ts, M_W), _acc((n, M_W)), _acc((n, M_W)), _acc((8, M_W))],
        out_shape=[_sds((s, M_W), _MM), _sds((n, M_W)), _sds((n, M_W)), _sds((8, M_W))])(m_q, gq, mk, mv, o_m, do_m)


def _k_memkv_bwd(mem, mem_norm, w_kv, m_k_norm, mem_n, kv, dmk, dmv):
    n = mem.shape[0]

    def body(m_ref, g_ref, w_ref, gk_ref, mn_ref, kv_ref, dmk_ref, dmv_ref, dw_ref, dg_ref, dgk_ref):
        gk = gk_ref[...]
        kh, r = _seg_norm(kv_ref[:, :M_W], M_HD)
        dmk = dmk_ref[...]
        dgk_ref[...] = _sum8(dmk * kh)
        z = dmk * gk
        dk = r * (z - kh * _seg_mean(z * kh, M_HD))
        dkv = jnp.concatenate([dk, dmv_ref[...]], axis=1).astype(_MM)
        dw_ref[...] = _dot_tn(mn_ref[...], dkv)
        dmn = _dot_nt(dkv, w_ref[...])
        mh, _ = _rms(m_ref[...])
        dg_ref[...] = _sum8(dmn * mh)

    return _pc(body, name="mem_kv_bwd", grid=(1,),
               in_specs=[_acc((n, D_MODEL)), _acc((1, D_MODEL)), _acc(w_kv.shape), _acc((1, M_W)), _acc((n, D_MODEL)),
                         _acc((n, 2 * M_W)), _acc((n, M_W)), _acc((n, M_W))],
               out_specs=[_acc(w_kv.shape), _acc((8, D_MODEL)), _acc((8, M_W))],
               out_shape=[_sds(w_kv.shape), _sds((8, D_MODEL)), _sds((8, M_W))])(
                   mem, mem_norm, w_kv, m_k_norm, mem_n, kv, dmk, dmv)


def _k_band_bwd(qn, kn, vn, do, lse, dl_or_o, *, hq, hk, max_dist, segs, sink, name):
    rows = qn.shape[0]
    nb = rows // BLK
    units = hq // A_HEADS
    shared = hk != hq
    wq, wk = hq * HEAD, hk * HEAD
    scale = HEAD ** -0.5

    def body(*refs):
        (q2_ref, qx_ref, kc_ref, kp_ref, vc_ref, vp_ref, do2_ref, dox_ref, l2_ref, lx_ref, e2_ref, ex_ref) = refs[:12]
        if sink is None:
            dq_ref, dk_ref, dv_ref = refs[12:]
        else:
            sk_ref, dq_ref, dk_ref, dv_ref, sacc_ref = refs[12:]
        i = pl.program_id(0)
        thr = lambda b: jnp.where(_first_flag(b, segs, nb), 1 << 20, BLK - max_dist)
        bias_a, bias_b = _band_bias(thr(2 * i), True), _band_bias(thr(2 * i + 1), True)
        bias_c = _band_bias(thr(2 * i + 2), False)
        if sink is not None:
            @pl.when(i == 0)
            def _():
                sacc_ref[...] = jnp.zeros_like(sacc_ref)

        def tile(q4, do4, l_cols, dlt, kd, vd, bias, width):
            s, dp = _dot_nt(q4, kd) * scale, _dot_nt(do4, vd)
            ps, dss = [], []
            for h in range(A_HEADS):
                seg = slice(h * width, (h + 1) * width)
                p = jnp.exp(s[:, seg] + bias - l_cols[h])
                ps.append(p)
                dss.append(p * (dp[:, seg] - dlt[:, h * HEAD:h * HEAD + 1]) * scale)
            return ps, dss

        cat = lambda parts: jnp.concatenate([t.astype(_MM) for t in parts], axis=1)
        for u in range(units):
            us = slice(u * A_W, (u + 1) * A_W)
            k_a = _unit_kv(((kp_ref, _LO), (kc_ref, _LO)), u, shared)
            v_a = _unit_kv(((vp_ref, _LO), (vc_ref, _LO)), u, shared)
            k_b, v_b = _unit_kv(((kc_ref, _BOTH),), u, shared), _unit_kv(((vc_ref, _BOTH),), u, shared)
            kd_a, vd_a, kd_b, vd_b = _blockdiag(k_a), _blockdiag(v_a), _blockdiag(k_b), _blockdiag(v_b)
            kd_c, vd_c = _blockdiag(k_b[BLK:]), _blockdiag(v_b[BLK:])
            qs = (q2_ref[_LO, us], q2_ref[_HI, us], qx_ref[:, us])
            dos = (do2_ref[_LO, us], do2_ref[_HI, us], dox_ref[:, us])
            lcols = [[ref[rs, u * A_W + h * HEAD:u * A_W + h * HEAD + 1] for h in range(A_HEADS)]
                     for ref, rs in ((l2_ref, _LO), (l2_ref, _HI), (lx_ref, _LO))]
            if sink is None:
                dlts = (e2_ref[_LO, us], e2_ref[_HI, us], ex_ref[:, us])
            else:
                dlts = tuple(_seg_sum64(d.astype(F32) * ref[rs, us])
                             for d, (ref, rs) in zip(dos, ((e2_ref, _LO), (e2_ref, _HI), (ex_ref, _LO))))
                for t in range(2):
                    for h in range(A_HEADS):
                        j = u * A_HEADS + h
                        sacc_ref[:, j:j + 1] += -jnp.exp(sk_ref[j] - lcols[t][h]) * dlts[t][:, h * HEAD:h * HEAD + 1]
            p_a, ds_a = tile(qs[0], dos[0], lcols[0], dlts[0], kd_a, vd_a, bias_a, 2 * BLK)
            p_b, ds_b = tile(qs[1], dos[1], lcols[1], dlts[1], kd_b, vd_b, bias_b, 2 * BLK)
            p_c, ds_c = tile(qs[2], dos[2], lcols[2], dlts[2], kd_c, vd_c, bias_c, BLK)
            dq_ref[_LO, us] = _dot(cat(ds_a), kd_a)
            dq_ref[_HI, us] = _dot(cat(ds_b), kd_b)
            outs = []
            for pa, pb, pc, lhs in ((ds_a, ds_b, ds_c, qs), (p_a, p_b, p_c, dos)):
                from_a = _fold_diag(_dot_tn(cat([t[:, BLK:] for t in pa]), lhs[0]), BLK)
                from_b = _fold_diag(_dot_tn(cat(pb), lhs[1]), 2 * BLK)
                from_c = _fold_diag(_dot_tn(cat(pc), lhs[2]), BLK)
                outs.append(jnp.concatenate([from_a + from_b[:BLK], from_b[BLK:] + from_c], axis=0))
            dk4, dv4 = outs
            if shared:
                fold = lambda t: (t[:, 0:HEAD] + t[:, HEAD:2 * HEAD]) + (t[:, 2 * HEAD:3 * HEAD] + t[:, 3 * HEAD:])
                dk_ref[:, u * HEAD:(u + 1) * HEAD] = fold(dk4)
                dv_ref[:, u * HEAD:(u + 1) * HEAD] = fold(dv4).astype(_MM)
            else:
                dk_ref[:, us] = dk4
                dv_ref[:, us] = dv4.astype(_MM)

    two = lambda w: pl.BlockSpec((2 * BLK, w), lambda i: (i, 0))
    prev = lambda w: pl.BlockSpec((BLK, w), lambda i: (jnp.maximum(2 * i - 1, 0), 0))
    nxt = lambda w: pl.BlockSpec((BLK, w), lambda i: (jnp.minimum(2 * i + 2, nb - 1), 0))
    in_specs = [two(wq), nxt(wq), two(wk), prev(wk), two(wk), prev(wk), two(wq), nxt(wq), two(wq), nxt(wq), two(wq), nxt(wq)]
    args = [qn, qn, kn, kn, vn, vn, do, do, lse, lse, dl_or_o, dl_or_o]
    out_specs = [two(wq), two(wk), two(wk)]
    out_shape = [_sds((rows, wq)), _sds((rows, wk)), _sds((rows, wk), _MM)]
    if sink is not None:
        in_specs.append(pl.BlockSpec(memory_space=pltpu.SMEM))
        args.append(sink)
        out_specs.append(_acc((BLK, 128)))
        out_shape.append(_sds((BLK, 128)))
    return _pc(body, name=name, grid=(nb // 2,), in_specs=in_specs, out_specs=out_specs, out_shape=out_shape)(*args)


def _k_prep_bwd(srcs, dqn, dkn, gq, gk, tabs, tab_row, *, wq, wk, rows_per_gain, name):
    rows = dqn.shape[0]
    ts = min(512, rows)
    ngain = gq.shape[0]

    def body(q_ref, k_ref, dq_ref, dk_ref, gq_ref, gk_ref, c_ref, sa_ref, sb_ref, oq_ref, ok_ref, aq_ref, ak_ref):
        i = pl.program_id(0)

        @pl.when(lax.rem(i * ts, rows_per_gain) == 0)
        def _():
            aq_ref[...] = jnp.zeros_like(aq_ref)
            ak_ref[...] = jnp.zeros_like(ak_ref)

        c, sa, sb = c_ref[...], sa_ref[...], sb_ref[...]
        for x_ref, d_ref, g_ref, o_ref, a_ref in ((q_ref, dq_ref, gq_ref, oq_ref, aq_ref),
                                                   (k_ref, dk_ref, gk_ref, ok_ref, ak_ref)):
            xh, r = _seg_norm(x_ref[...], HEAD)
            dt = _rope_bwd(d_ref[...], c, sa, sb)
            a_ref[...] += _sum8(dt * xh)
            z = dt * g_ref[...]
            o_ref[...] = (r * (z - xh * _seg_mean(z * xh, HEAD))).astype(_MM)

    gspec = lambda w: pl.BlockSpec((None, 1, w), lambda i: ((i * ts) // rows_per_gain, 0, 0))
    aspec = lambda w: pl.BlockSpec((None, 8, w), lambda i: ((i * ts) // rows_per_gain, 0, 0))
    return _pc(
        body, name=name, grid=(rows // ts,),
        in_specs=[_row(ts, wq, srcs[0][1]), _row(ts, wk, srcs[1][1]), _row(ts, wq), _row(ts, wk), gspec(wq), gspec(wk)]
        + [pl.BlockSpec((ts, 128), lambda i: (i + tab_row // ts, 0))] * 3,
        out_specs=[_row(ts, wq), _row(ts, wk), aspec(wq), aspec(wk)],
        out_shape=[_sds((rows, wq), _MM), _sds((rows, wk), _MM), _sds((ngain, 8, wq)), _sds((ngain, 8, wk))])(
            srcs[0][0], srcs[1][0], dqn, dkn, gq, gk, *tabs)


def _k_in_bwd(pieces, dgp, x, g1, dx1, w_in, w_gate):
    s = x.shape[0]
    ts = min(512, s)
    nin, ng = w_in.shape[2], w_gate.shape[2]
    widths = [p.shape[1] for p in pieces]
    ncol = sum(widths)

    def body(*refs):
        p_refs = refs[:len(pieces)]
        dgp_ref, x_ref, g_ref, dx1_ref, wi_ref, wg_ref, gx_ref, dpj_ref, gacc_ref = refs[len(pieces):]
        i = pl.program_id(0)

        @pl.when(i == 0)
        def _():
            gacc_ref[...] = jnp.zeros_like(gacc_ref)

        off = 0
        for p_ref, w in zip(p_refs, widths):
            dpj_ref[:, off:off + w] = p_ref[...]
            off += w
        dh = jnp.zeros((ts, D_MODEL), F32)
        for j in range(CHIPS):
            dh = dh + _dot_nt(dpj_ref[:, j * nin:(j + 1) * nin], wi_ref[j])
            dh = dh + _dot_nt(dgp_ref[:, j * ng:(j + 1) * ng], wg_ref[j])
        xh, r = _rms(x_ref[...])
        gacc_ref[...] += _sum8(dh * xh)
        gx_ref[...] = dx1_ref[...] + _rms_bwd(dh, xh, r, g_ref[...])

    return _pc(
        body, name="in_proj_bwd", grid=(s // ts,),
        in_specs=[_row(ts, w) for w in widths] + [_row(ts, CHIPS * ng), _row(ts, D_MODEL), _res((1, D_MODEL)),
                                                  _row(ts, D_MODEL), _res(w_in.shape), _res(w_gate.shape)],
        out_specs=[_row(ts, D_MODEL), _row(ts, ncol), _acc((8, D_MODEL))],
        out_shape=[_sds((s, D_MODEL)), _sds((s, ncol), _MM), _sds((8, D_MODEL))])(*pieces, dgp, x, g1, dx1, w_in, w_gate)


def _k_wgrad(a, b, *, nblk, stacked, name):
    s, k = a.shape
    n = b.shape[1]
    nb = n // nblk
    ts = min(2048 if k <= 1024 else 1024, s)

    def body(a_ref, b_ref, o_ref):
        @pl.when(pl.program_id(1) == 0)
        def _():
            o_ref[...] = jnp.zeros_like(o_ref)

        o_ref[...] += _dot_tn(a_ref[...], b_ref[...])

    if stacked:
        out_spec, out_shape = pl.BlockSpec((None, k, nb), lambda g, t: (g, 0, 0)), _sds((nblk, k, nb))
    else:
        out_spec, out_shape = pl.BlockSpec((k, nb), lambda g, t: (0, g)), _sds((k, n))
    return _pc(body, name=name, grid=(nblk, s // ts),
               in_specs=[pl.BlockSpec((ts, k), lambda g, t: (t, 0)), pl.BlockSpec((ts, nb), lambda g, t: (t, g))],
               out_specs=[out_spec], out_shape=[out_shape])(a, b)[0]


def _to_res(t, d):
    s, c = t.shape
    return t if d == 1 else t.reshape(s // d, d, c).transpose(1, 0, 2).reshape(s, c)


def _from_res(t, d):
    s, c = t.shape
    return t if d == 1 else t.reshape(d, s // d, c).transpose(1, 0, 2).reshape(s, c)


def _tile_gain(g, heads):
    return jnp.tile(g, (1,) * (g.ndim - 1) + (heads,))[..., None, :]


def _local_step(x, mem, pos, target, small, get_w_in, get_rest, on_grads):
    s = x.shape[0]
    nblk = s // BLK
    g1, g2 = small["attn_norm"], small["ffn_norm"]

    pos_rows = jnp.concatenate([_to_res(pos[:, None], d)[:, 0] for _, d in A_GROUPS] + [pos])
    tabs = _rope_tables(pos_rows)
    w_in = get_w_in(tabs[0])

    h, qa0, qa1, qa2, q_b, k_b, v_b, m_q = _k_in(x, g1, w_in)

    qkv_a = jnp.concatenate([_to_res(t, d) for t, (_, d) in zip((qa0, qa1, qa2), A_GROUPS)], axis=0)
    gq_a = _tile_gain(small["a_q_norm"], A_HEADS)
    gk_a = _tile_gain(small["a_k_norm"], A_HEADS)
    src_a = ((qkv_a, 0), (qkv_a, 1), (qkv_a, 2))
    qn_a, kn_a, vn_a = _k_prep(src_a, gq_a, gk_a, tabs, 0, wq=A_W, wk=A_W, rows_per_gain=s, name="prep_a")
    segs_a = tuple((gi * nblk, nblk // d) for gi, (_, d) in enumerate(A_GROUPS))
    o_res, l_res = _k_band_fwd(qn_a, kn_a, vn_a, hq=A_HEADS, hk=A_HEADS, max_dist=BLK, segs=segs_a, sink=None,
                               name="attn_a")
    og = [_from_res(o_res[gi * s:(gi + 1) * s], d) for gi, (_, d) in enumerate(A_GROUPS)]
    lg = [_from_res(l_res[gi * s:(gi + 1) * s], d) for gi, (_, d) in enumerate(A_GROUPS)]

    gq_b = _tile_gain(small["b_q_norm"], B_QH)
    gk_b = _tile_gain(small["b_k_norm"], B_KVH)
    src_b = ((q_b, 0), (k_b, 0), (v_b, 0))
    qn_b, kn_b, vn_b = _k_prep(src_b, gq_b, gk_b, tabs, 3 * s, wq=B_QH * HEAD, wk=B_KVH * HEAD, rows_per_gain=s,
                               name="prep_b")
    sink_x = small["b_sinks"][0]
    segs_b = ((0, nblk),)
    o_b, l_b = _k_band_fwd(qn_b, kn_b, vn_b, hq=B_QH, hk=B_KVH, max_dist=B_WINDOW - 1, segs=segs_b, sink=sink_x,
                           name="attn_b")

    wts = get_rest(0, o_b)

    gq_m = _tile_gain(small["m_q_norm"], M_HEADS)[0]
    gk_m = _tile_gain(small["m_k_norm"], M_HEADS)[0]
    mem_n, kv, mk, mv = _k_memkv(mem, small["mem_norm"], wts["w_mem_kv"], gk_m)
    o_m = _k_mem_fwd(m_q, gq_m, mk, mv)

    o_a, merged, x1, h2, gates = _k_merge(og, lg, o_b, o_m, h, x, wts["w_gate"], small["b_gate"], wts["w_o_a"],
                                          wts["w_o_b"], wts["w_o_m"], wts["w_out"], g2)
    wts.update(get_rest(1, x1))
    u = _k_up(h2, wts["w_up"])
    dy, f, dc, loss_acc = _k_ffn(u, wts["conv_w"], small["conv_b"], wts["w_down"], wts["w_down"].T, x1, target)
    loss = (0.5 / D_MODEL) * jnp.sum(loss_acc)

    dx1, du, cacc, g2acc = _k_conv_bwd(dc, u, wts["conv_w"], wts["w_up"], x1, g2, dy)
    tok = on_grads({"w_up": _k_wgrad(h2, du, nblk=CHIPS, stacked=True, name="dw_up"),
                    "w_down": _k_wgrad(f, dy, nblk=2, stacked=False, name="dw_down").reshape(CHIPS, -1, D_MODEL)}, dx1)
    (dgp, dog0, dog1, dog2, dl0, dl1, dl2, do_b, do_m, bacc, dw_oa, dw_ob, dw_om, dw_out, dw_gate) = _k_merge_bwd(
        dx1, og, lg, o_a, o_b, o_m, gates, merged, h, wts["w_gate"].shape, wts["w_o_a"], wts["w_o_b"], wts["w_o_m"],
        wts["w_out"], tok)
    tok = on_grads({"w_gate": dw_gate,
                    "w_o_a": dw_oa, "w_o_b": dw_ob, "w_o_m": dw_om, "w_out": dw_out.reshape(CHIPS, -1, D_MODEL)}, do_m)

    dq_m, dmk, dmv, gqm_acc = _k_mem_bwd(m_q, gq_m + tok[0:1, 0:1], mk, mv, o_m, do_m)
    dw_kv, gmem_acc, gkm_acc = _k_memkv_bwd(mem, small["mem_norm"], wts["w_mem_kv"], gk_m, mem_n, kv, dmk, dmv)

    dq_bn, dk_bn, dv_b, sacc = _k_band_bwd(qn_b, kn_b, vn_b, do_b, l_b, o_b, hq=B_QH, hk=B_KVH,
                                           max_dist=B_WINDOW - 1, segs=segs_b, sink=sink_x, name="attn_b_bwd")
    tok = on_grads({}, dq_bn)
    dq_b, dk_b, gqb_acc, gkb_acc = _k_prep_bwd(src_b, dq_bn, dk_bn, gq_b + tok[0:1, 0:1], gk_b, tabs, 3 * s, wq=B_QH * HEAD,
                                               wk=B_KVH * HEAD, rows_per_gain=s, name="prep_b_bwd")

    do_res = jnp.concatenate([_to_res(t, d) for t, (_, d) in zip((dog0, dog1, dog2), A_GROUPS)], axis=0)
    dl_res = jnp.concatenate([_to_res(t, d) for t, (_, d) in zip((dl0, dl1, dl2), A_GROUPS)], axis=0)
    dq_an, dk_an, dv_a = _k_band_bwd(qn_a, kn_a, vn_a, do_res, l_res, dl_res, hq=A_HEADS, hk=A_HEADS, max_dist=BLK,
                                     segs=segs_a, sink=None, name="attn_a_bwd")
    dq_a, dk_a, gqa_acc, gka_acc = _k_prep_bwd(src_a, dq_an, dk_an, gq_a, gk_a, tabs, 0, wq=A_W, wk=A_W,
                                               rows_per_gain=s, name="prep_a_bwd")
    pieces = []
    for gi, (_, d) in enumerate(A_GROUPS):
        rs = slice(gi * s, (gi + 1) * s)
        pieces += [_from_res(t[rs], d) for t in (dq_a, dk_a, dv_a)]
    pieces += [dq_b, dk_b, dv_b, dq_m]
    grad_x, dproj, g1acc = _k_in_bwd(pieces, dgp, x, g1, dx1, w_in, wts["w_gate"])
    on_grads({"w_in": _k_wgrad(h, dproj, nblk=CHIPS, stacked=True, name="dw_in"),
              "w_mem_kv": dw_kv.reshape(CHIPS, -1, 2 * M_W)}, grad_x)

    def fold(acc, heads):
        v = jnp.sum(acc, axis=-2)
        return jnp.sum(v.reshape(v.shape[:-1] + (heads, -1)), axis=-2)

    csum = jnp.sum(cacc, axis=1)
    sml = {
        "attn_norm": jnp.sum(g1acc, axis=0), "a_q_norm": fold(gqa_acc, A_HEADS), "a_k_norm": fold(gka_acc, A_HEADS),
        "b_q_norm": fold(gqb_acc[0], B_QH), "b_k_norm": fold(gkb_acc[0], B_KVH),
        "b_sinks": jnp.sum(sacc, axis=0)[:B_QH], "mem_norm": jnp.sum(gmem_acc, axis=0),
        "m_q_norm": fold(gqm_acc, M_HEADS), "m_k_norm": fold(gkm_acc, M_HEADS),
        "b_gate": jnp.sum(bacc, axis=0), "ffn_norm": jnp.sum(g2acc, axis=0),
        "conv_w": csum[1:], "conv_b": csum[0],
    }
    return loss, grad_x, sml


def _mesh_pos():
    return lax.axis_index("x"), lax.axis_index("y"), lax.axis_index("c")


def _chip_peers(x, y):
    return [(1 - x, y), (x, 1 - y), (1 - x, 1 - y)]


_ANY = pl.BlockSpec(memory_space=pl.ANY)


def _comm_call(body, *, name, n_in, out_shape, scratch):
    return pl.pallas_call(body, name=name, in_specs=[_ANY] * n_in, out_specs=[_ANY] * len(out_shape),
                          out_shape=out_shape, scratch_shapes=scratch)


def _remote(src, dst, send_sem, recv_sem, dev):
    return pltpu.make_async_remote_copy(src_ref=src, dst_ref=dst, send_sem=send_sem, recv_sem=recv_sem,
                                        device_id=dev, device_id_type=MESH)


def _pair_join(halves, name):
    nt = len(halves)

    def body(*refs):
        ins, got = refs[:nt], refs[nt:2 * nt]
        send_sems, recv_sems = refs[2 * nt:]
        x, y, c = _mesh_pos()
        cps = []
        for t in range(nt):
            rc = _remote(ins[t], got[t], send_sems.at[t], recv_sems.at[t], (x, y, 1 - c))
            rc.start()
            cps.append(rc)
        for rc in cps:
            rc.wait()

    out_shape = [_sds(hf.shape, hf.dtype) for hf in halves]
    scratch = [pltpu.SemaphoreType.DMA((nt,)), pltpu.SemaphoreType.DMA((nt,))]
    return _comm_call(body, name=name, n_in=nt, out_shape=out_shape, scratch=scratch)(*halves)


_HBM = pl.BlockSpec(memory_space=pltpu.HBM)
_SEMS = pl.BlockSpec(memory_space=pltpu.SEMAPHORE)
_EFFECT = pltpu.SideEffectType.DATAFLOW_SIDE_EFFECTING


def _bcast_copies(ins, lands, send_sems, recv_sems):
    x, y, c = _mesh_pos()
    me = 2 * x + y
    targets = [((px, py, c), 2 * px + py) for px, py in _chip_peers(x, y)] + [((x, y, 1 - c), me)]
    out = []
    for t in range(len(ins)):
        for k, (dev, idx) in enumerate(targets):
            i = t * len(targets) + k
            arrival = lambda t=t, i=i, idx=idx, dev=dev: _remote(ins[t], lands[t].at[idx], send_sems.at[i],
                                                                 recv_sems.at[i], dev)
            out.append((_remote(ins[t], lands[t].at[me], send_sems.at[i], recv_sems.at[i], dev), arrival))
    return out


def _scatter_copies(ins, lands, send_sems, recv_sems):
    x, y, c = _mesh_pos()
    out = []
    for t in range(len(ins)):
        for k, (px, py) in enumerate(_chip_peers(x, y)):
            i = t * 3 + k
            cp = _remote(ins[t].at[2 * px + py], lands[t].at[k], send_sems.at[i], recv_sems.at[i], (px, py, c))
            out.append((cp, lambda cp=cp: cp))
    return out


def _pair_copies(ins, lands, send_sems, recv_sems):
    x, y, c = _mesh_pos()
    out = []
    for t in range(len(ins)):
        hr = ins[t].shape[1] // 2
        give = ins[t].at[:, pl.ds(pl.multiple_of((1 - c) * hr, 8), hr), :]
        cp = _remote(give, lands[t], send_sems.at[t], recv_sems.at[t], (x, y, 1 - c))
        out.append((cp, lambda cp=cp: cp))
    return out


def _join_copies(ins, lands, send_sems, recv_sems):
    x, y, c = _mesh_pos()
    out = []
    for t in range(len(ins)):
        cp = _remote(ins[t], lands[t], send_sems.at[t], recv_sems.at[t], (x, y, 1 - c))
        out.append((cp, lambda cp=cp: cp))
    return out


def _half_copies(ins, lands, send_sems, recv_sems):
    x, y, c = _mesh_pos()
    me = 2 * x + y
    out = []
    for t in range(len(ins)):
        hr = ins[t].shape[0] // 2
        rows = pl.ds(pl.multiple_of(c * hr, 8), hr)
        for k, (px, py) in enumerate(_chip_peers(x, y)):
            i = t * 3 + k
            arrival = lambda t=t, i=i, px=px, py=py, rows=rows: _remote(
                ins[t].at[rows, :], lands[t].at[2 * px + py].at[rows, :], send_sems.at[i], recv_sems.at[i], (px, py, c))
            out.append((_remote(ins[t].at[rows, :], lands[t].at[me].at[rows, :], send_sems.at[i], recv_sems.at[i],
                                (px, py, c)), arrival))
    return out


def _finish_halves(shards, stacks):
    nt = len(shards)

    def body(*refs):
        ins, held, outs = refs[:nt], refs[nt:2 * nt], refs[2 * nt:3 * nt]
        fwd_s, fwd_r, own_s, own_r = refs[3 * nt:]
        x, y, c = _mesh_pos()
        me = 2 * x + y
        sib = (x, y, 1 - c)
        pending = []
        for t in range(nt):
            hr = shards[t].shape[0] // 2
            half = lambda ref, who: ref.at[pl.ds(pl.multiple_of(who * hr, 8), hr), :]
            own = _remote(ins[t], outs[t].at[me], own_s.at[t], own_r.at[t], sib)
            own.start()
            pending.append(own.wait)
            for k, (px, py) in enumerate(_chip_peers(x, y)):
                pj = 2 * px + py
                fw = _remote(half(held[t].at[pj], c), half(outs[t].at[pj], c), fwd_s.at[t, k], fwd_r.at[t, k], sib)
                fw.start()
                pending.append(fw.wait_send)
                other = half(outs[t].at[pj], 1 - c)
                pending.append(_remote(other, other, fwd_s.at[t, k], fwd_r.at[t, k], sib).wait_recv)
        for wait in pending:
            wait()

    dma = pltpu.SemaphoreType.DMA
    return pl.pallas_call(
        body, name="gather_w_in_finish", in_specs=[_ANY] * (2 * nt), out_specs=[_ANY] * nt,
        out_shape=[_sds(a.shape, a.dtype) for a in stacks], input_output_aliases={nt + i: i for i in range(nt)},
        scratch_shapes=[dma((nt, 3)), dma((nt, 3)), dma((nt,)), dma((nt,))])(*shards, *stacks)


def _split_start(copies, srcs, land_shapes, ncopy, dep, name, lands=None):
    nt = len(srcs)

    def body(*refs):
        ins, lands = refs[:nt], refs[nt:2 * nt]
        send_sems, recv_sems, token = refs[2 * nt + 1], refs[2 * nt + 2], refs[-1]
        for send, _ in copies(ins, lands, send_sems, recv_sems):
            send.start()
        token[...] = jnp.zeros_like(token)

    if lands is None:
        lands = [lax.empty(sh, a.dtype) for sh, a in zip(land_shapes, srcs)]
    lands = [pltpu.with_memory_space_constraint(a, pltpu.HBM) for a in lands]
    srcs = [pltpu.with_memory_space_constraint(a, pltpu.HBM) for a in srcs]
    dma = pltpu.SemaphoreType.DMA
    out_shape = ([dma((nt * ncopy,)), dma((nt * ncopy,))] + [pltpu.HBM(a.shape, a.dtype) for a in srcs + lands]
                 + [_sds((8, 128))])
    outs = pl.pallas_call(
        body, name=name, in_specs=[_HBM] * (2 * nt) + [_ANY],
        out_specs=[_SEMS, _SEMS] + [_HBM] * (2 * nt) + [pl.BlockSpec(memory_space=pltpu.VMEM)], out_shape=out_shape,
        input_output_aliases={i: 2 + i for i in range(2 * nt)},
        compiler_params=pltpu.CompilerParams(has_side_effects=_EFFECT))(*srcs, *lands, dep)
    return outs[0], outs[1], outs[2:2 + nt], outs[2 + nt:2 + 2 * nt], outs[-1]


def _split_wait(copies, send_sems, recv_sems, srcs, lands, after, name):
    nt = len(srcs)

    def body(*refs):
        ins, lnd = refs[:nt], refs[nt:2 * nt]
        for send, arrival in copies(ins, lnd, refs[2 * nt], refs[2 * nt + 1]):
            send.wait_send()
            arrival().wait_recv()

    outs = pl.pallas_call(
        body, name=name, in_specs=[_HBM] * (2 * nt) + [_SEMS, _SEMS, _ANY], out_specs=[_HBM] * (2 * nt),
        out_shape=[pltpu.HBM(a.shape, a.dtype) for a in list(srcs) + list(lands)],
        input_output_aliases={i: i for i in range(2 * nt)},
        compiler_params=pltpu.CompilerParams(has_side_effects=_EFFECT))(*srcs, *lands, send_sems, recv_sems, after)
    return outs[:nt], outs[nt:]


def _small_copies(ins, lands, send_sems, recv_sems):
    x, y, c = _mesh_pos()
    me = 4 * x + 2 * y + c
    out = []
    for k in range(1, NDEV):
        px, py, pc = x ^ (k >> 2), y ^ ((k >> 1) & 1), c ^ (k & 1)
        arrival = lambda k=k, px=px, py=py, pc=pc: _remote(ins[0], lands[0].at[4 * px + 2 * py + pc], send_sems.at[k - 1],
                                                            recv_sems.at[k - 1], (px, py, pc))
        out.append((_remote(ins[0], lands[0].at[me], send_sems.at[k - 1], recv_sems.at[k - 1], (px, py, pc)), arrival))
    return out


def _row_tile(r, c, mib=1):
    t = r
    while t * c * 4 > (mib << 20) and t % 16 == 0:
        t //= 2
    return t


def _k_pair_add(full, got, name):
    g, r, c = full.shape
    hr = r // 2
    tr = _row_tile(hr, c, 4)
    nh = hr // tr

    def body(a_ref, b_ref, o_ref):
        o_ref[...] = (a_ref[...] + b_ref[...]).astype(_WIRE)

    mine = pl.BlockSpec((None, tr, c), lambda i, j: (i, lax.axis_index("c") * nh + j, 0))
    spec = pl.BlockSpec((None, tr, c), lambda i, j: (i, j, 0))
    return _pc(body, name=name, grid=(g, nh), in_specs=[mine, spec], out_specs=[spec],
               out_shape=[_sds((g, hr, c), _WIRE)])(full, got)[0]


def _k_chip_sum(parts, slots, name):
    _, r, c = parts.shape
    tr = _row_tile(r, c, 4)

    def body(a_ref, s_ref, o_ref):
        acc = a_ref[...].astype(F32)
        for k in range(3):
            acc = acc + s_ref[k].astype(F32)
        o_ref[...] = acc

    own = pl.BlockSpec((None, tr, c), lambda i: (2 * lax.axis_index("x") + lax.axis_index("y"), i, 0))
    return _pc(body, name=name, grid=(r // tr,), in_specs=[own, pl.BlockSpec((3, tr, c), lambda i: (0, i, 0))],
               out_specs=[_row(tr, c)], out_shape=[_sds((r, c))])(parts, slots)[0]


def _adam(w, g, m, v):
    m = ADAM_B1 * m + (1.0 - ADAM_B1) * g
    v = ADAM_B2 * v + (1.0 - ADAM_B2) * (g * g)
    m_hat = m / (1.0 - ADAM_B1 ** ADAM_STEP)
    v_hat = v / (1.0 - ADAM_B2 ** ADAM_STEP)
    return -ADAM_LR * (m_hat / (jnp.sqrt(v_hat) + ADAM_EPS) + ADAM_WD * w), m, v


def _k_adam(w, mine, theirs, m, v, dep, name):
    r, c = w.shape
    hr = r // 2
    tr = _row_tile(hr, c, 2)
    nh = hr // tr

    def body(w_ref, a_ref, b_ref, m_ref, v_ref, dep_ref, g_ref, d_ref, mo_ref, vo_ref):
        upper = (pl.program_id(0) >= nh).astype(jnp.int32)
        g = jnp.where(upper == lax.axis_index("c"), a_ref[...], b_ref[...])
        g_ref[...] = g
        d_ref[...], mo_ref[...], vo_ref[...] = _adam(w_ref[...], g, m_ref[...], v_ref[...])

    hspec = pl.BlockSpec((tr, c), lambda i: (jnp.where(i >= nh, i - nh, i), 0))
    return _pc(body, name=name, grid=(r // tr,),
               in_specs=[_row(tr, c), hspec, hspec, _row(tr, c), _row(tr, c), _res((8, 128))],
               out_specs=[_row(tr, c)] * 4, out_shape=[_sds((r, c))] * 4)(w, mine, theirs, m, v, dep)


def _k_sum8(a):
    _, n, _ = a.shape

    def body(a_ref, o_ref):
        acc = a_ref[0]
        for k in range(1, NDEV):
            acc = acc + a_ref[k]
        o_ref[...] = acc

    return _pc(body, name="sum_small_grads", grid=(1,), in_specs=[_acc(a.shape)], out_specs=[_acc((n, 128))],
               out_shape=[_sds((n, 128))])(a)[0]


def _k_adam_small(ws, gs, ms, vs):
    n = len(ws)

    def body(*refs):
        for k in range(n):
            w_ref, g_ref, m_ref, v_ref, d_ref, mo_ref, vo_ref = refs[k::n]
            d_ref[...], mo_ref[...], vo_ref[...] = _adam(w_ref[...], g_ref[...], m_ref[...], v_ref[...])

    specs = [_acc(a.shape) for a in ws]
    outs = _pc(body, name="adam_small", grid=(1,), in_specs=specs * 4, out_specs=specs * 3,
               out_shape=[_sds(a.shape) for a in ws] * 3)(*ws, *gs, *ms, *vs)
    return outs[:n], outs[n:2 * n], outs[2 * n:]


def _pack(vals):
    rows = []
    for a in vals:
        flat = a.reshape(-1)
        n = -(-flat.shape[0] // 1024) * 1024
        rows.append(jnp.pad(flat, (0, n - flat.shape[0])).reshape(n // 128, 128))
    return jnp.concatenate(rows, axis=0)


def _unpack(packed, shapes):
    out, off = [], 0
    for sh in shapes:
        size = int(np.prod(sh))
        n = -(-size // 1024) * 1024
        out.append(packed[off // 128:(off + n) // 128].reshape(-1)[:size].reshape(sh))
        off += n
    return out


_WEIGHTS = ["attn_norm", "w_in", "a_q_norm", "a_k_norm", "b_q_norm", "b_k_norm", "b_sinks", "mem_norm", "w_mem_kv",
            "m_q_norm", "m_k_norm", "w_o_a", "w_o_b", "w_o_m", "w_gate", "b_gate", "w_out", "ffn_norm", "w_up",
            "conv_w", "conv_b", "w_down"]
_BIG = ["w_in", "w_mem_kv", "w_o_a", "w_o_b", "w_o_m", "w_gate", "w_out", "w_up", "w_down"]
_SMALL = [n for n in _WEIGHTS if n not in _BIG]


def kernel(x, mem, positions, attn_norm, w_in, a_q_norm, a_k_norm, b_q_norm, b_k_norm, b_sinks, mem_norm, w_mem_kv, m_q_norm, m_k_norm, w_o_a, w_o_b, w_o_m, w_gate, b_gate, w_out, ffn_norm, w_up, conv_w, conv_b, w_down, loss_target, m_attn_norm, m_w_in, m_a_q_norm, m_a_k_norm, m_b_q_norm, m_b_k_norm, m_b_sinks, m_mem_norm, m_w_mem_kv, m_m_q_norm, m_m_k_norm, m_w_o_a, m_w_o_b, m_w_o_m, m_w_gate, m_b_gate, m_w_out, m_ffn_norm, m_w_up, m_conv_w, m_conv_b, m_w_down, v_attn_norm, v_w_in, v_a_q_norm, v_a_k_norm, v_b_q_norm, v_b_k_norm, v_b_sinks, v_mem_norm, v_w_mem_kv, v_m_q_norm, v_m_k_norm, v_w_o_a, v_w_o_b, v_w_o_m, v_w_gate, v_b_gate, v_w_out, v_ffn_norm, v_w_up, v_conv_w, v_conv_b, v_w_down):
    given = dict(locals())
    w = {n: given[n][0] for n in _WEIGHTS}
    m1 = {n: given["m_" + n][0] for n in _WEIGHTS}
    m2 = {n: given["v_" + n][0] for n in _WEIGHTS}

    zeros = jnp.zeros((8, 128), F32)
    w_in_shard = w["w_in"].astype(_MM)
    *w_in_handles, tok = _split_start(_half_copies, [w_in_shard], [(CHIPS,) + w_in_shard.shape], 3, zeros,
                                      "gather_w_in_start")

    def get_w_in(after):
        send, recv, srcs, lands = w_in_handles
        srcs, lands = _split_wait(_half_copies, send, recv, srcs, lands, after, "gather_w_in_wait")
        return _finish_halves(srcs, lands)[0]

    stages = (["w_gate", "w_mem_kv", "w_o_a", "w_o_b", "w_o_m", "w_out"], ["w_up", "w_down", "conv_w"])
    started = []
    for k, names in enumerate(stages):
        shards = [w[n] if n == "conv_w" else w[n].astype(_MM) for n in names]
        *handles, tok = _split_start(_bcast_copies, shards, [(CHIPS,) + a.shape for a in shards], 4, tok,
                                     "gather_start_%d" % k)
        started.append(handles)
    small = {n: (w[n][None, :] if w[n].ndim == 1 else w[n]) for n in _SMALL if n != "conv_w"}
    positions = positions + tok[0:1, 0:1].astype(positions.dtype)

    def get_rest(stage, after):
        send, recv, srcs, lands = started[stage]
        got = _split_wait(_bcast_copies, send, recv, srcs, lands, after, "gather_wait_%d" % stage)[1]
        wts = dict(zip(stages[stage], got))
        for n in ("w_mem_kv", "w_out", "w_down"):
            if n in wts:
                wts[n] = wts[n].reshape(-1, wts[n].shape[-1])
        return wts

    parts, slots, pair, scat, started_pair = {}, {}, [], [], [None]

    def finish_pair(after):
        names, tag, send, recv, srcs, lands = pair.pop()
        full, got = _split_wait(_pair_copies, send, recv, srcs, lands, after, "pair_wait_" + tag)
        mine = [_k_pair_add(f, b, "pair_add_" + n) for n, f, b in zip(names, full, got)]
        shapes = [(3,) + p.shape[1:] for p in mine]
        send, recv, srcs, lands, token = _split_start(_scatter_copies, mine, shapes, 3, zeros, "scatter_start_" + tag)
        scat.append((names, tag, send, recv, srcs, lands))
        return token

    def on_grads(group, after):
        names = list(group)
        tag = "_".join(names)
        token = finish_pair(after) if pair else zeros
        if not group:
            return token
        grads_g = [group[n] for n in names]
        shapes = [(CHIPS, g.shape[1] // 2, g.shape[2]) for g in grads_g]
        send, recv, srcs, lands, token = _split_start(_pair_copies, grads_g, shapes, 1, token, "pair_start_" + tag)
        pair.append((names, tag, send, recv, srcs, lands))
        started_pair[0] = token
        return token

    loss, grad_x, sml = _local_step(x[0], mem[0], positions[0], loss_target[0], small, get_w_in, get_rest, on_grads)

    packed = _pack([sml[n] for n in _SMALL] + [loss.reshape(1)])
    me = 4 * lax.axis_index("x") + 2 * lax.axis_index("y") + lax.axis_index("c")
    land = lax.dynamic_update_slice(jnp.zeros((NDEV,) + packed.shape, F32), packed[None], (me, 0, 0))
    *small_h, tok = _split_start(_small_copies, [packed], None, NDEV - 1, zeros, "gather_small_start", lands=[land])
    started_pair[0] = started_pair[0] + tok

    early = [n for names, *_ in scat for n in names]
    for names, tag, send, recv, srcs, lands in scat:
        mine, got = _split_wait(_scatter_copies, send, recv, srcs, lands, started_pair[0], "scatter_wait_" + tag)
        parts.update(zip(names, mine))
        slots.update(zip(names, got))
    scat.clear()
    reduced = {n: _k_chip_sum(parts[n], slots[n], "chip_add_" + n) for n in early}
    halves = [reduced[n] for n in early]
    *join, tok = _split_start(_join_copies, halves, [a.shape for a in halves], 1, zeros, "pair_join_start_early")
    grads = {}

    delta, new_m, new_v = {}, {}, {}
    dep = finish_pair(tok)
    mine, got = _split_wait(_join_copies, *join, dep, "pair_join_wait_early")
    reduced.update(zip(early, mine))
    theirs = dict(zip(early, got))
    for n in early:
        grads[n], delta[n], new_m[n], new_v[n] = _k_adam(w[n], reduced[n], theirs[n], m1[n], m2[n], dep, "adam_" + n)
        dep = delta[n]
    gathered = _split_wait(_small_copies, *small_h, dep, "gather_small_wait")[1][0]
    shapes = [sml[n].shape for n in _SMALL] + [(1,)]
    *gsmall, loss = _unpack(_k_sum8(gathered), shapes)
    loss = loss[0]
    gsm = dict(zip(_SMALL, gsmall))
    nu = w["conv_w"].shape[1]
    chip = 2 * lax.axis_index("x") + lax.axis_index("y")
    gsm["conv_w"] = lax.dynamic_slice_in_dim(gsm["conv_w"], chip * nu, nu, axis=1)
    for n in _SMALL:
        grads[n] = gsm[n].reshape(w[n].shape)
    as2d = lambda d: [d[n][None, :] if d[n].ndim == 1 else d[n] for n in _SMALL]
    for dst, outs in zip((delta, new_m, new_v), _k_adam_small(as2d(w), as2d(grads), as2d(m1), as2d(m2))):
        dst.update((n, a.reshape(w[n].shape)) for n, a in zip(_SMALL, outs))
    late, tag, send, recv, srcs, lands = scat.pop()
    mine, got = _split_wait(_scatter_copies, send, recv, srcs, lands, dep, "scatter_wait_" + tag)
    for n, a, b in zip(late, mine, got):
        reduced[n] = _k_chip_sum(a, b, "chip_add_" + n)
    theirs.update(zip(late, _pair_join([reduced[n] for n in late], "grad_pair_join_late")))
    for n in late:
        grads[n], delta[n], new_m[n], new_v[n] = _k_adam(w[n], reduced[n], theirs[n], m1[n], m2[n], zeros, "adam_" + n)

    lead = lambda d: [d[n][None] for n in _WEIGHTS]
    return (loss, grad_x[None], *lead(grads), *lead(delta), *lead(new_m), *lead(new_v))
```

```python
import math

import jax
import jax.numpy as jnp
import numpy as np
from jax import lax
from jax.experimental import pallas as pl
from jax.experimental.pallas import tpu as pltpu

F32 = jnp.float32
_MM = jnp.bfloat16
_WIRE = jnp.bfloat16

D_MODEL = 1024
HEAD = 64
BLK = 128
A_GROUPS = ((128, 1), (512, 4), (2048, 16))
A_HEADS = 4
A_W = A_HEADS * HEAD
B_QH = 8
B_KVH = 2
B_WINDOW = 128
M_HEADS = 4
M_HD = 128
M_W = M_HEADS * M_HD
D_FF = 2816
EPS = 1e-6
NEG = -1e30
ROPE_THETA = 500000.0
ROPE_ROT = 16
CHIPS = 4
NDEV = 8
ADAM_LR, ADAM_B1, ADAM_B2, ADAM_EPS, ADAM_WD, ADAM_STEP = 0.001, 0.9, 0.999, 1e-08, 0.01, 10
VMEM_LIMIT = 58 * 1024 * 1024
MESH = pl.DeviceIdType.MESH


def _pc(body, *, name, grid, in_specs, out_specs, out_shape, scratch=()):
    return pl.pallas_call(
        body, name=name, grid=grid, in_specs=in_specs, out_specs=out_specs, out_shape=out_shape,
        scratch_shapes=list(scratch),
        compiler_params=pltpu.CompilerParams(dimension_semantics=("arbitrary",) * len(grid),
                                             vmem_limit_bytes=VMEM_LIMIT))


def _row(ts, c, col=0):
    return pl.BlockSpec((ts, c), lambda i: (i, col))


def _res(shape):
    n = len(shape)
    return pl.BlockSpec(tuple(shape), lambda i: (0,) * n, pipeline_mode=pl.Buffered(1))


def _acc(shape):
    n = len(shape)
    return pl.BlockSpec(tuple(shape), lambda i: (0,) * n, pipeline_mode=pl.Buffered(1))


def _sds(shape, dtype=F32):
    return jax.ShapeDtypeStruct(tuple(shape), dtype)


def _dot(a, b):
    return jnp.dot(a.astype(_MM), b.astype(_MM), preferred_element_type=F32)


def _dot_nt(a, b):
    return lax.dot_general(a.astype(_MM), b.astype(_MM), (((1,), (1,)), ((), ())), preferred_element_type=F32)


def _dot_tn(a, b):
    return lax.dot_general(a.astype(_MM), b.astype(_MM), (((0,), (0,)), ((), ())), preferred_element_type=F32)


def _sum8(v):
    ts, c = v.shape
    return jnp.sum(v.reshape(ts // 8, 8, c), axis=0)


def _sigmoid(z):
    return 1.0 / (1.0 + jnp.exp(-z))


def _rms(x):
    r = lax.rsqrt(jnp.mean(x * x, axis=-1, keepdims=True) + EPS)
    return x * r, r


def _rms_bwd(dy, xh, r, gain):
    z = dy * gain
    return r * (z - xh * jnp.mean(z * xh, axis=-1, keepdims=True))


def _split_hi_lo(v):
    hi = v.astype(_MM)
    return hi, (v - hi.astype(F32)).astype(_MM)


def _lane_head(shape):
    return lax.shift_right_logical(lax.broadcasted_iota(jnp.int32, shape, len(shape) - 1), 6)


def _seg_sum64(v):
    w = v.shape[1]
    e = jnp.where(_lane_head((w, w)) == lax.shift_right_logical(lax.broadcasted_iota(jnp.int32, (w, w), 0), 6),
                  1.0, 0.0).astype(_MM)
    hi, lo = _split_hi_lo(v)
    return jnp.dot(hi, e, preferred_element_type=F32) + jnp.dot(lo, e, preferred_element_type=F32)


def _seg_norm(x, seg):
    if seg == HEAD:
        r = lax.rsqrt(_seg_sum64(x * x) * (1.0 / HEAD) + EPS)
        return x * r, r
    w = x.shape[1]
    xh, rr = [], []
    for s in range(w // seg):
        xs = x[:, s * seg:(s + 1) * seg]
        r = lax.rsqrt(jnp.mean(xs * xs, axis=-1, keepdims=True) + EPS)
        xh.append(xs * r)
        rr.append(jnp.broadcast_to(r, xs.shape))
    return jnp.concatenate(xh, axis=1), jnp.concatenate(rr, axis=1)


def _seg_mean(v, seg):
    if seg == HEAD:
        return _seg_sum64(v) * (1.0 / HEAD)
    w = v.shape[1]
    out = []
    for s in range(w // seg):
        vs = v[:, s * seg:(s + 1) * seg]
        out.append(jnp.broadcast_to(jnp.mean(vs, axis=-1, keepdims=True), vs.shape))
    return jnp.concatenate(out, axis=1)


def _rope(t, c, sa, sb):
    out = []
    for cb in range(t.shape[1] // 128):
        tc = t[:, cb * 128:(cb + 1) * 128]
        out.append(tc * c + pltpu.roll(tc, 120, 1) * sa + pltpu.roll(tc, 8, 1) * sb)
    return jnp.concatenate(out, axis=1) if len(out) > 1 else out[0]


def _rope_bwd(dy, c, sa, sb):
    out = []
    for cb in range(dy.shape[1] // 128):
        dc = dy[:, cb * 128:(cb + 1) * 128]
        out.append(dc * c + pltpu.roll(dc * sa, 8, 1) + pltpu.roll(dc * sb, 120, 1))
    return jnp.concatenate(out, axis=1) if len(out) > 1 else out[0]


def _rope_consts():
    half = ROPE_ROT // 2
    c = np.float32(-2.0 * math.log(ROPE_THETA) / ROPE_ROT)
    freqs = np.exp(np.arange(half, dtype=np.float32) * c).astype(np.float32)
    place = np.zeros((3, half, 128), np.float32)
    ones = np.zeros((1, 128), np.float32)
    for lane in range(128):
        d = lane % HEAD
        if d < half:
            place[0, d, lane], place[1, d, lane] = 1.0, -1.0
        elif d < ROPE_ROT:
            place[0, d - half, lane], place[2, d - half, lane] = 1.0, 1.0
        else:
            ones[0, lane] = 1.0
    return np.tile(freqs[:, None], (1, 128)), place, ones


def _rope_tables(pos_rows):
    r = pos_rows.shape[0]
    tr = min(1024, r)
    freqs, place, ones = _rope_consts()

    def split3(v):
        hi, mid = _split_hi_lo(v)
        lo = (v - hi.astype(F32) - mid.astype(F32)).astype(_MM)
        return hi, mid, lo

    def body(p_ref, f_ref, e_ref, one_ref, c_ref, sa_ref, sb_ref):
        ang = jnp.concatenate([p_ref[j:j + 1, :].astype(F32) * f_ref[...] for j in range(tr // 128)], axis=1)
        cos, sin = jnp.cos(ang), jnp.sin(ang)
        for ref, k, v in ((c_ref, 0, cos), (sa_ref, 1, sin), (sb_ref, 2, sin)):
            e = e_ref[k].astype(_MM)
            out = sum(_dot_tn(part, e) for part in split3(v))
            ref[...] = out + one_ref[...] if k == 0 else out

    return _pc(body, name="rope_tables", grid=(r // tr,),
               in_specs=[pl.BlockSpec((tr // 128, 128), lambda i: (i, 0)), _acc((ROPE_ROT // 2, 128)),
                         _acc((3, ROPE_ROT // 2, 128)), _acc((1, 128))],
               out_specs=[_row(tr, 128)] * 3, out_shape=[_sds((r, 128))] * 3)(
                   pos_rows.reshape(r // 128, 128), jnp.asarray(freqs), jnp.asarray(place), jnp.asarray(ones))


def _k_in(x, g1, w_in):
    s = x.shape[0]
    ts = min(512, s)
    nin = w_in.shape[2]
    ncol = CHIPS * nin
    a_cols = 3 * A_W
    offs = [0, a_cols, 2 * a_cols, 3 * a_cols, 3 * a_cols + B_QH * HEAD,
            3 * a_cols + (B_QH + B_KVH) * HEAD, 3 * a_cols + (B_QH + 2 * B_KVH) * HEAD, ncol]

    def body(x_ref, g_ref, wi_ref, h_ref, a0, a1, a2, qb, kb, vb, mq, p_scr):
        xh, _ = _rms(x_ref[...])
        h = (xh * g_ref[...]).astype(_MM)
        h_ref[...] = h
        for j in range(CHIPS):
            p_scr[:, j * nin:(j + 1) * nin] = jnp.dot(h, wi_ref[j], preferred_element_type=F32)
        for k, ref in enumerate((a0, a1, a2, qb, kb, vb, mq)):
            ref[...] = p_scr[:, offs[k]:offs[k + 1]]

    widths = [offs[k + 1] - offs[k] for k in range(7)]
    return _pc(
        body, name="in_proj", grid=(s // ts,),
        in_specs=[_row(ts, D_MODEL), _res((1, D_MODEL)), _res(w_in.shape)],
        out_specs=[_row(ts, D_MODEL)] + [_row(ts, w) for w in widths],
        out_shape=[_sds((s, D_MODEL), _MM)] + [_sds((s, w)) for w in widths],
        scratch=[pltpu.VMEM((ts, ncol), F32)])(x, g1, w_in)


def _k_prep(srcs, gq, gk, tabs, tab_row, *, wq, wk, rows_per_gain, name):
    rows = srcs[0][0].shape[0]
    ts = min(512, rows)

    def body(q_ref, k_ref, v_ref, gq_ref, gk_ref, c_ref, sa_ref, sb_ref, qn_ref, kn_ref, vn_ref):
        c, sa, sb = c_ref[...], sa_ref[...], sb_ref[...]
        qh, _ = _seg_norm(q_ref[...], HEAD)
        qn_ref[...] = _rope(qh * gq_ref[...], c, sa, sb).astype(_MM)
        kh, _ = _seg_norm(k_ref[...], HEAD)
        kn_ref[...] = _rope(kh * gk_ref[...], c, sa, sb).astype(_MM)
        vn_ref[...] = v_ref[...].astype(_MM)

    gspec = lambda w: pl.BlockSpec((None, 1, w), lambda i: ((i * ts) // rows_per_gain, 0, 0))
    return _pc(
        body, name=name, grid=(rows // ts,),
        in_specs=[_row(ts, wq, srcs[0][1]), _row(ts, wk, srcs[1][1]), _row(ts, wk, srcs[2][1]),
                  gspec(wq), gspec(wk)] + [pl.BlockSpec((ts, 128), lambda i: (i + tab_row // ts, 0))] * 3,
        out_specs=[_row(ts, wq), _row(ts, wk), _row(ts, wk)],
        out_shape=[_sds((rows, wq), _MM), _sds((rows, wk), _MM), _sds((rows, wk), _MM)])(
            srcs[0][0], srcs[1][0], srcs[2][0], gq, gk, *tabs)


def _first_flag(b, segs, nb):
    first = b >= nb
    for k, (start, period) in enumerate(segs):
        end = segs[k + 1][0] if k + 1 < len(segs) else nb
        first = first | ((b >= start) & (b < end) & (lax.rem(b - start, jnp.int32(period)) == 0))
    return first


def _band_bias(thr, with_cur):
    qi = lax.broadcasted_iota(jnp.int32, (BLK, BLK), 0)
    kj = lax.broadcasted_iota(jnp.int32, (BLK, BLK), 1)
    prev = jnp.where(kj >= qi + thr, 0.0, NEG)
    return jnp.concatenate([prev, jnp.where(kj <= qi, 0.0, NEG)], axis=1) if with_cur else prev


def _blockdiag(t4):
    head = _lane_head((1, A_W))
    return jnp.concatenate([t4 * jnp.where(head == h, 1.0, 0.0).astype(t4.dtype) for h in range(A_HEADS)], axis=0)


def _fold_diag(t, n):
    head = _lane_head((n, A_W))
    out = t[3 * n:4 * n]
    for h in (2, 1, 0):
        out = jnp.where(head == h, t[h * n:(h + 1) * n], out)
    return out


def _expand_heads(cols):
    n = cols[0].shape[0]
    head = _lane_head((n, A_W))
    out = jnp.broadcast_to(cols[3], (n, A_W))
    for h in (2, 1, 0):
        out = jnp.where(head == h, cols[h], out)
    return out


def _unit_kv(pieces, u, shared):
    cols = slice(u * HEAD, (u + 1) * HEAD) if shared else slice(u * A_W, (u + 1) * A_W)
    rows = [ref[rs, cols] for ref, rs in pieces]
    k = rows[0] if len(rows) == 1 else jnp.concatenate(rows, axis=0)
    return jnp.concatenate([k] * A_HEADS, axis=1) if shared else k


_LO, _HI, _BOTH = slice(0, BLK), slice(BLK, 2 * BLK), slice(0, 2 * BLK)


def _k_band_fwd(qn, kn, vn, *, hq, hk, max_dist, segs, sink, name):
    rows = qn.shape[0]
    nb = rows // BLK
    units = hq // A_HEADS
    shared = hk != hq
    wq, wk = hq * HEAD, hk * HEAD
    scale = HEAD ** -0.5

    def body(*refs):
        if sink is None:
            q_ref, kc_ref, kp_ref, vc_ref, vp_ref, o_ref, l_ref = refs
        else:
            q_ref, kc_ref, kp_ref, vc_ref, vp_ref, sk_ref, o_ref, l_ref = refs
        i = pl.program_id(0)
        for half, rs in enumerate((_LO, _HI)):
            bias = _band_bias(jnp.where(_first_flag(2 * i + half, segs, nb), 1 << 20, BLK - max_dist), True)
            kpieces = ((kp_ref, _LO), (kc_ref, _LO)) if half == 0 else ((kc_ref, _BOTH),)
            vpieces = ((vp_ref, _LO), (vc_ref, _LO)) if half == 0 else ((vc_ref, _BOTH),)
            for u in range(units):
                us = slice(u * A_W, (u + 1) * A_W)
                kb = _blockdiag(_unit_kv(kpieces, u, shared))
                vb = _blockdiag(_unit_kv(vpieces, u, shared))
                s_all = _dot_nt(q_ref[rs, us], kb) * scale
                ps, ls = [], []
                for h in range(A_HEADS):
                    s = s_all[:, h * 2 * BLK:(h + 1) * 2 * BLK] + bias
                    m = jnp.max(s, axis=-1, keepdims=True)
                    e = jnp.exp(s - m)
                    lse = m + jnp.log(jnp.sum(e, axis=-1, keepdims=True))
                    if sink is not None:
                        sk = sk_ref[u * A_HEADS + h]
                        mx = jnp.maximum(lse, sk)
                        lse = mx + jnp.log(jnp.exp(lse - mx) + jnp.exp(sk - mx))
                    ps.append((e * jnp.exp(m - lse)).astype(_MM))
                    ls.append(lse)
                o_ref[rs, us] = _dot(jnp.concatenate(ps, axis=1), vb)
                l_ref[rs, us] = _expand_heads(ls)

    two = lambda w: pl.BlockSpec((2 * BLK, w), lambda i: (i, 0))
    prev = lambda w: pl.BlockSpec((BLK, w), lambda i: (jnp.maximum(2 * i - 1, 0), 0))
    in_specs = [two(wq), two(wk), prev(wk), two(wk), prev(wk)]
    args = [qn, kn, kn, vn, vn]
    if sink is not None:
        in_specs.append(pl.BlockSpec(memory_space=pltpu.SMEM))
        args.append(sink)
    return _pc(body, name=name, grid=(nb // 2,), in_specs=in_specs, out_specs=[two(wq), two(wq)],
               out_shape=[_sds((rows, wq)), _sds((rows, wq))])(*args)


def _k_memkv(mem, mem_norm, w_kv, m_k_norm):
    n = mem.shape[0]

    def body(m_ref, g_ref, w_ref, gk_ref, mn_ref, kv_ref, mk_ref, mv_ref):
        mh, _ = _rms(m_ref[...])
        mn = (mh * g_ref[...]).astype(_MM)
        mn_ref[...] = mn
        kv = jnp.dot(mn, w_ref[...], preferred_element_type=F32)
        kv_ref[...] = kv
        kh, _ = _seg_norm(kv[:, :M_W], M_HD)
        mk_ref[...] = (kh * gk_ref[...]).astype(_MM)
        mv_ref[...] = kv[:, M_W:].astype(_MM)

    return _pc(body, name="mem_kv", grid=(1,),
               in_specs=[_acc((n, D_MODEL)), _acc((1, D_MODEL)), _acc(w_kv.shape), _acc((1, M_W))],
               out_specs=[_acc((n, D_MODEL)), _acc((n, 2 * M_W)), _acc((n, M_W)), _acc((n, M_W))],
               out_shape=[_sds((n, D_MODEL), _MM), _sds((n, 2 * M_W)), _sds((n, M_W), _MM), _sds((n, M_W), _MM)])(
                   mem, mem_norm, w_kv, m_k_norm)


def _mem_probs(q, mk):
    sc = _dot_nt(q, mk) * (M_HD ** -0.5)
    e = jnp.exp(sc - jnp.max(sc, axis=-1, keepdims=True))
    return e / jnp.sum(e, axis=-1, keepdims=True)


def _k_mem_fwd(m_q, gq, mk, mv):
    s = m_q.shape[0]
    n = mk.shape[0]
    ts = min(512, s)

    def body(q_ref, g_ref, mk_ref, mv_ref, o_ref):
        qh, _ = _seg_norm(q_ref[...], M_HD)
        qn = (qh * g_ref[...]).astype(_MM)
        for h in range(M_HEADS):
            hs = slice(h * M_HD, (h + 1) * M_HD)
            o_ref[:, hs] = _dot(_mem_probs(qn[:, hs], mk_ref[:, hs]), mv_ref[:, hs])

    return _pc(body, name="mem_attn", grid=(s // ts,),
               in_specs=[_row(ts, M_W), _res((1, M_W)), _res((n, M_W)), _res((n, M_W))],
               out_specs=[_row(ts, M_W)], out_shape=[_sds((s, M_W))])(m_q, gq, mk, mv)[0]


def _group_weights(l0, l1, l2):
    m = jnp.maximum(jnp.maximum(l0, l1), l2)
    e0, e1, e2 = jnp.exp(l0 - m), jnp.exp(l1 - m), jnp.exp(l2 - m)
    inv = 1.0 / (e0 + e1 + e2)
    return e0 * inv, e1 * inv, e2 * inv


def _branch_products(oa, ob, om, woa_ref, wob_ref, wom_ref, j):
    return _dot(oa, woa_ref[j]), _dot(ob, wob_ref[j]), _dot(om, wom_ref[j])


def _k_merge(og, lg, o_b, o_m, h, x, w_gate, b_gate, w_oa, w_ob, w_om, w_out, g2):
    s = x.shape[0]
    ts = min(512, s)
    nc = w_oa.shape[2]
    ng = w_gate.shape[2]

    def body(o0, o1, o2, l0, l1, l2, ob_ref, om_ref, h_ref, x_ref, wg, bg_ref, woa, wob, wom, wout, g_ref,
             oa_ref, mer_ref, x1_ref, h2_ref, gt_ref, m_scr):
        h = h_ref[...]
        for j in range(CHIPS):
            z = jnp.dot(h, wg[j], preferred_element_type=F32) + bg_ref[:, j * ng:(j + 1) * ng]
            gt_ref[:, j * ng:(j + 1) * ng] = _sigmoid(z)
        w0, w1, w2 = _group_weights(l0[...], l1[...], l2[...])
        oa = w0 * o0[...] + w1 * o1[...] + w2 * o2[...]
        oa_ref[...] = oa
        ob, om = ob_ref[...], om_ref[...]
        for j in range(CHIPS):
            pa, pb, pm = _branch_products(oa, ob, om, woa, wob, wom, j)
            cs = lambda br: slice(br * D_MODEL + j * nc, br * D_MODEL + (j + 1) * nc)
            m_scr[:, j * nc:(j + 1) * nc] = gt_ref[:, cs(0)] * pa + gt_ref[:, cs(1)] * pb + gt_ref[:, cs(2)] * pm
        mer = m_scr[...].astype(_MM)
        mer_ref[...] = mer
        x1 = x_ref[...] + jnp.dot(mer, wout[...], preferred_element_type=F32)
        x1_ref[...] = x1
        xh, _ = _rms(x1)
        h2_ref[...] = (xh * g_ref[...]).astype(_MM)

    return _pc(
        body, name="merge_out", grid=(s // ts,),
        in_specs=[_row(ts, A_W)] * 6 + [_row(ts, B_QH * HEAD), _row(ts, M_W), _row(ts, D_MODEL), _row(ts, D_MODEL),
                                         _res(w_gate.shape), _res(b_gate.shape), _res(w_oa.shape), _res(w_ob.shape),
                                         _res(w_om.shape), _res(w_out.shape), _res((1, D_MODEL))],
        out_specs=[_row(ts, A_W), _row(ts, D_MODEL), _row(ts, D_MODEL), _row(ts, D_MODEL), _row(ts, CHIPS * ng)],
        out_shape=[_sds((s, A_W)), _sds((s, D_MODEL), _MM), _sds((s, D_MODEL)), _sds((s, D_MODEL), _MM),
                   _sds((s, CHIPS * ng))],
        scratch=[pltpu.VMEM((ts, D_MODEL), F32)])(
            *og, *lg, o_b, o_m, h, x, w_gate, b_gate, w_oa, w_ob, w_om, w_out, g2)


def _k_up(h2, w_up):
    s = h2.shape[0]
    ts = min(256, s)
    nu = w_up.shape[2]

    def body(h_ref, w_ref, u_ref):
        h = h_ref[...]
        for j in range(CHIPS):
            u_ref[:, j * nu:(j + 1) * nu] = jnp.dot(h, w_ref[j], preferred_element_type=F32)

    return _pc(body, name="up_proj", grid=(s // ts,), in_specs=[_row(ts, D_MODEL), _res(w_up.shape)],
               out_specs=[_row(ts, CHIPS * nu)], out_shape=[_sds((s, CHIPS * nu))])(h2, w_up)[0]


def _shift_down(v, halo, k):
    rolled = pltpu.roll(v, k, 0)
    row = lax.broadcasted_iota(jnp.int32, (8, v.shape[1]), 0)
    slab = rolled[0:8]
    for r in range(k):
        slab = jnp.where(row == r, halo[8 - k + r:8 - k + r + 1, :], slab)
    return jnp.concatenate([slab, rolled[8:]], axis=0)


def _shift_up(v, halo, k):
    ts = v.shape[0]
    rolled = pltpu.roll(v, ts - k, 0)
    row = lax.broadcasted_iota(jnp.int32, (8, v.shape[1]), 0)
    slab = rolled[ts - 8:]
    for r in range(k):
        slab = jnp.where(row == 8 - k + r, halo[r:r + 1, :], slab)
    return jnp.concatenate([rolled[:ts - 8], slab], axis=0)


def _k_ffn(u, conv_w, conv_b, w_down, w_down_t, x1, target):
    s = u.shape[0]
    ts = min(256, s)
    nu = conv_w.shape[2]
    half = CHIPS // 2

    def body(u_ref, uh_ref, cw_ref, cb_ref, wd_ref, wdt_ref, x1_ref, t_ref, dy_ref, f_ref, dc_ref, loss_ref, c_scr,
             f_scr, s_scr):
        i = pl.program_id(0)
        halo = jnp.where(i > 0, uh_ref[...], 0.0)
        for j in range(CHIPS):
            cs = slice(j * nu, (j + 1) * nu)
            uj = u_ref[:, cs]
            hj = halo[:, cs]
            c_scr[:, cs] = (cb_ref[:, cs] + cw_ref[j, 0:1, :] * _shift_down(uj, hj, 2)
                            + cw_ref[j, 1:2, :] * _shift_down(uj, hj, 1) + cw_ref[j, 2:3, :] * uj)
        for j in range(half):
            a = c_scr[:, j * nu:(j + 1) * nu]
            g = c_scr[:, (half + j) * nu:(half + j + 1) * nu]
            sa = _sigmoid(a)
            s_scr[:, j * nu:(j + 1) * nu] = sa
            f_scr[:, j * nu:(j + 1) * nu] = (a * sa * g).astype(_MM)
        f = f_scr[...]
        f_ref[...] = f
        y = x1_ref[...] + jnp.dot(f, wd_ref[...], preferred_element_type=F32)
        err = y - t_ref[...]
        dy = err * (1.0 / D_MODEL)
        dy_ref[...] = dy

        @pl.when(i == 0)
        def _():
            loss_ref[...] = jnp.zeros_like(loss_ref)

        loss_ref[...] += _sum8(err * err)
        df = _dot(dy, wdt_ref[...])
        for j in range(half):
            a = c_scr[:, j * nu:(j + 1) * nu]
            g = c_scr[:, (half + j) * nu:(half + j + 1) * nu]
            sa = s_scr[:, j * nu:(j + 1) * nu]
            dfj = df[:, j * nu:(j + 1) * nu]
            dc_ref[:, j * nu:(j + 1) * nu] = dfj * g * (sa * (1.0 + a * (1.0 - sa)))
            dc_ref[:, (half + j) * nu:(half + j + 1) * nu] = dfj * (a * sa)

    wide = CHIPS * nu
    return _pc(
        body, name="conv_ffn", grid=(s // ts,),
        in_specs=[_row(ts, wide), pl.BlockSpec((8, wide), lambda i: (jnp.maximum(i * (ts // 8) - 1, 0), 0)),
                  _res(conv_w.shape), _res((1, wide)), _res(w_down.shape), _res(w_down_t.shape), _row(ts, D_MODEL),
                  _row(ts, D_MODEL)],
        out_specs=[_row(ts, D_MODEL), _row(ts, D_FF), _row(ts, wide), _acc((8, D_MODEL))],
        out_shape=[_sds((s, D_MODEL)), _sds((s, D_FF), _MM), _sds((s, wide)), _sds((8, D_MODEL))],
        scratch=[pltpu.VMEM((ts, wide), F32), pltpu.VMEM((ts, D_FF), _MM), pltpu.VMEM((ts, D_FF), F32)])(
            u, u, conv_w, conv_b, w_down, w_down_t, x1, target)


def _k_conv_bwd(dc, u, conv_w, w_up, x1, g2, dy):
    s = u.shape[0]
    ts = min(256, s)
    nu = conv_w.shape[2]
    wide = CHIPS * nu
    last = s // ts - 1

    def body(dc_ref, dn_ref, u_ref, cw_ref, wu_ref, x1_ref, g_ref, dy_ref, dx1_ref, du_ref, cacc_ref, gacc_ref):
        i = pl.program_id(0)

        @pl.when(i == 0)
        def _():
            cacc_ref[...] = jnp.zeros_like(cacc_ref)
            gacc_ref[...] = jnp.zeros_like(gacc_ref)

        dhalo = jnp.where(i < last, dn_ref[...], 0.0)
        dh2 = jnp.zeros((ts, D_MODEL), F32)
        for j in range(CHIPS):
            cs = slice(j * nu, (j + 1) * nu)
            dcj, uj = dc_ref[:, cs], u_ref[:, cs]
            dc1, dc2 = _shift_up(dcj, dhalo[:, cs], 1), _shift_up(dcj, dhalo[:, cs], 2)
            cacc_ref[0, :, cs] += _sum8(dcj)
            cacc_ref[1, :, cs] += _sum8(dc2 * uj)
            cacc_ref[2, :, cs] += _sum8(dc1 * uj)
            cacc_ref[3, :, cs] += _sum8(dcj * uj)
            du = (cw_ref[j, 2:3, :] * dcj + cw_ref[j, 1:2, :] * dc1 + cw_ref[j, 0:1, :] * dc2).astype(_MM)
            du_ref[:, cs] = du
            dh2 = dh2 + _dot_nt(du, wu_ref[j])
        xh, r = _rms(x1_ref[...])
        gacc_ref[...] += _sum8(dh2 * xh)
        dx1_ref[...] = dy_ref[...] + _rms_bwd(dh2, xh, r, g_ref[...])

    return _pc(
        body, name="conv_up_bwd", grid=(s // ts,),
        in_specs=[_row(ts, wide),
                  pl.BlockSpec((8, wide), lambda i: (jnp.minimum((i + 1) * (ts // 8), s // 8 - 1), 0)),
                  _row(ts, wide), _res(conv_w.shape), _res(w_up.shape), _row(ts, D_MODEL), _res((1, D_MODEL)),
                  _row(ts, D_MODEL)],
        out_specs=[_row(ts, D_MODEL), _row(ts, wide), _acc((4, 8, wide)), _acc((8, D_MODEL))],
        out_shape=[_sds((s, D_MODEL)), _sds((s, wide), _MM), _sds((4, 8, wide)), _sds((8, D_MODEL))])(
            dc, dc, u, conv_w, w_up, x1, g2, dy)


def _k_merge_bwd(dx1, og, lg, o_a, o_b, o_m, gates, merged, h, w_gate_shape, w_oa, w_ob, w_om, w_out, dep):
    s = dx1.shape[0]
    ts = min(256, s)
    nc = w_oa.shape[2]
    ng = w_gate_shape[2]

    def body(dx_ref, o0, o1, o2, l0, l1, l2, oa_ref, ob_ref, om_ref, gt_ref, mer_ref, h_ref, woa, wob, wom, wout, dep_ref,
             dgp_ref, dog0, dog1, dog2, dl0, dl1, dl2, dob_ref, dom_ref, bacc_ref, dwa_ref, dwb_ref, dwm_ref, dwo_ref,
             dwg_ref):
        i = pl.program_id(0)

        @pl.when(i == 0)
        def _():
            for ref in (bacc_ref, dwa_ref, dwb_ref, dwm_ref, dwo_ref, dwg_ref):
                ref[...] = jnp.zeros_like(ref)

        dx = dx_ref[...]
        h = h_ref[...]
        dwo_ref[...] += _dot_tn(mer_ref[...], dx)
        dmer = _dot_nt(dx, wout[...])
        oa, ob, om = oa_ref[...], ob_ref[...], om_ref[...]
        doa = jnp.zeros((ts, A_W), F32)
        dob = jnp.zeros((ts, B_QH * HEAD), F32)
        dom = jnp.zeros((ts, M_W), F32)
        for j in range(CHIPS):
            prods = _branch_products(oa, ob, om, woa, wob, wom, j)
            dmj = dmer[:, j * nc:(j + 1) * nc]
            dps = []
            for br, (p, o, dw_ref) in enumerate(zip(prods, (oa, ob, om), (dwa_ref, dwb_ref, dwm_ref))):
                cs = slice(br * D_MODEL + j * nc, br * D_MODEL + (j + 1) * nc)
                gt = gt_ref[:, cs]
                dgp = dmj * p * gt * (1.0 - gt)
                dgp_ref[:, cs] = dgp.astype(_MM)
                bacc_ref[:, cs] += _sum8(dgp)
                blk, off = divmod(cs.start, ng)
                dwg_ref[blk, :, off:off + nc] += _dot_tn(h, dgp)
                dp = (dmj * gt).astype(_MM)
                dw_ref[j] += _dot_tn(o, dp)
                dps.append(dp)
            doa = doa + _dot_nt(dps[0], woa[j])
            dob = dob + _dot_nt(dps[1], wob[j])
            dom = dom + _dot_nt(dps[2], wom[j])
        dob_ref[...] = dob
        dom_ref[...] = dom
        ws = _group_weights(l0[...], l1[...], l2[...])
        dsum = _seg_mean(doa * oa, HEAD) * float(HEAD)
        for w, dref, lref in zip(ws, (dog0, dog1, dog2), (dl0, dl1, dl2)):
            dref[...] = w * doa
            lref[...] = w * dsum

    return _pc(
        body, name="merge_out_bwd", grid=(s // ts,),
        in_specs=[_row(ts, D_MODEL)] + [_row(ts, A_W)] * 7 + [_row(ts, B_QH * HEAD), _row(ts, M_W), _row(ts, 3 * D_MODEL),
                                                              _row(ts, D_MODEL), _row(ts, D_MODEL), _res(w_oa.shape),
                                                              _res(w_ob.shape), _res(w_om.shape), _res(w_out.shape),
                                                              _res((8, 128))],
        out_specs=[_row(ts, 3 * D_MODEL)] + [_row(ts, A_W)] * 6
        + [_row(ts, B_QH * HEAD), _row(ts, M_W), _acc((8, 3 * D_MODEL)), _acc(w_oa.shape), _acc(w_ob.shape),
           _acc(w_om.shape), _acc(w_out.shape), _acc(w_gate_shape)],
        out_shape=[_sds((s, 3 * D_MODEL), _MM)] + [_sds((s, A_W))] * 6
        + [_sds((s, B_QH * HEAD)), _sds((s, M_W)), _sds((8, 3 * D_MODEL)), _sds(w_oa.shape), _sds(w_ob.shape),
           _sds(w_om.shape), _sds(w_out.shape), _sds(w_gate_shape)])(
            dx1, *og, *lg, o_a, o_b, o_m, gates, merged, h, w_oa, w_ob, w_om, w_out, dep)


def _k_mem_bwd(m_q, gq, mk, mv, o_m, do_m):
    s = m_q.shape[0]
    n = mk.shape[0]
    ts = min(512, s)
    scale = M_HD ** -0.5

    def body(q_ref, g_ref, mk_ref, mv_ref, o_ref, do_ref, dq_ref, dmk_ref, dmv_ref, gacc_ref):
        i = pl.program_id(0)

        @pl.when(i == 0)
        def _():
            dmk_ref[...] = jnp.zeros_like(dmk_ref)
            dmv_ref[...] = jnp.zeros_like(dmv_ref)
            gacc_ref[...] = jnp.zeros_like(gacc_ref)

        gain = g_ref[...]
        qh, r = _seg_norm(q_ref[...], M_HD)
        qn = (qh * gain).astype(_MM)
        do = do_ref[...]
        delta = _seg_mean(do * o_ref[...], M_HD) * float(M_HD)
        dqn = []
        for h in range(M_HEADS):
            hs = slice(h * M_HD, (h + 1) * M_HD)
            p = _mem_probs(qn[:, hs], mk_ref[:, hs])
            dp = _dot_nt(do[:, hs], mv_ref[:, hs])
            ds = (p * (dp - delta[:, hs][:, 0:1]) * scale).astype(_MM)
            dqn.append(_dot(ds, mk_ref[:, hs]))
            dmk_ref[:, hs] += _dot_tn(ds, qn[:, hs])
            dmv_ref[:, hs] += _dot_tn(p, do[:, hs])
        dqn = jnp.concatenate(dqn, axis=1)
        gacc_ref[...] += _sum8(dqn * qh)
        z = dqn * gain
        dq_ref[...] = (r * (z - qh * _seg_mean(z * qh, M_HD))).astype(_MM)

    return _pc(
        body, name="mem_attn_bwd", grid=(s // ts,),
        in_specs=[_row(ts, M_W), _res((1, M_W)), _res((n, M_W)), _res((n, M_W)), _row(ts, M_W), _row(ts, M_W)],
        out_specs=[_row(ts, M_W), _acc((n, M_W)), _acc((n, M_W)), _acc((8, M_W))],
        out_shape=[_sds((s, M_W), _MM), _sds((n, M_W)), _sds((n, M_W)), _sds((8, M_W))])(m_q, gq, mk, mv, o_m, do_m)


def _k_memkv_bwd(mem, mem_norm, w_kv, m_k_norm, mem_n, kv, dmk, dmv):
    n = mem.shape[0]

    def body(m_ref, g_ref, w_ref, gk_ref, mn_ref, kv_ref, dmk_ref, dmv_ref, dw_ref, dg_ref, dgk_ref):
        gk = gk_ref[...]
        kh, r = _seg_norm(kv_ref[:, :M_W], M_HD)
        dmk = dmk_ref[...]
        dgk_ref[...] = _sum8(dmk * kh)
        z = dmk * gk
        dk = r * (z - kh * _seg_mean(z * kh, M_HD))
        dkv = jnp.concatenate([dk, dmv_ref[...]], axis=1).astype(_MM)
        dw_ref[...] = _dot_tn(mn_ref[...], dkv)
        dmn = _dot_nt(dkv, w_ref[...])
        mh, _ = _rms(m_ref[...])
        dg_ref[...] = _sum8(dmn * mh)

    return _pc(body, name="mem_kv_bwd", grid=(1,),
               in_specs=[_acc((n, D_MODEL)), _acc((1, D_MODEL)), _acc(w_kv.shape), _acc((1, M_W)), _acc((n, D_MODEL)),
                         _acc((n, 2 * M_W)), _acc((n, M_W)), _acc((n, M_W))],
               out_specs=[_acc(w_kv.shape), _acc((8, D_MODEL)), _acc((8, M_W))],
               out_shape=[_sds(w_kv.shape), _sds((8, D_MODEL)), _sds((8, M_W))])(
                   mem, mem_norm, w_kv, m_k_norm, mem_n, kv, dmk, dmv)


def _k_band_bwd(qn, kn, vn, do, lse, dl_or_o, *, hq, hk, max_dist, segs, sink, name):
    rows = qn.shape[0]
    nb = rows // BLK
    units = hq // A_HEADS
    shared = hk != hq
    wq, wk = hq * HEAD, hk * HEAD
    scale = HEAD ** -0.5

    def body(*refs):
        (q2_ref, qx_ref, kc_ref, kp_ref, vc_ref, vp_ref, do2_ref, dox_ref, l2_ref, lx_ref, e2_ref, ex_ref) = refs[:12]
        if sink is None:
            dq_ref, dk_ref, dv_ref = refs[12:]
        else:
            sk_ref, dq_ref, dk_ref, dv_ref, sacc_ref = refs[12:]
        i = pl.program_id(0)
        thr = lambda b: jnp.where(_first_flag(b, segs, nb), 1 << 20, BLK - max_dist)
        bias_a, bias_b = _band_bias(thr(2 * i), True), _band_bias(thr(2 * i + 1), True)
        bias_c = _band_bias(thr(2 * i + 2), False)
        if sink is not None:
            @pl.when(i == 0)
            def _():
                sacc_ref[...] = jnp.zeros_like(sacc_ref)

        def tile(q4, do4, l_cols, dlt, kd, vd, bias, width):
            s, dp = _dot_nt(q4, kd) * scale, _dot_nt(do4, vd)
            ps, dss = [], []
            for h in range(A_HEADS):
                seg = slice(h * width, (h + 1) * width)
                p = jnp.exp(s[:, seg] + bias - l_cols[h])
                ps.append(p)
                dss.append(p * (dp[:, seg] - dlt[:, h * HEAD:h * HEAD + 1]) * scale)
            return ps, dss

        cat = lambda parts: jnp.concatenate([t.astype(_MM) for t in parts], axis=1)
        for u in range(units):
            us = slice(u * A_W, (u + 1) * A_W)
            k_a = _unit_kv(((kp_ref, _LO), (kc_ref, _LO)), u, shared)
            v_a = _unit_kv(((vp_ref, _LO), (vc_ref, _LO)), u, shared)
            k_b, v_b = _unit_kv(((kc_ref, _BOTH),), u, shared), _unit_kv(((vc_ref, _BOTH),), u, shared)
            kd_a, vd_a, kd_b, vd_b = _blockdiag(k_a), _blockdiag(v_a), _blockdiag(k_b), _blockdiag(v_b)
            kd_c, vd_c = _blockdiag(k_b[BLK:]), _blockdiag(v_b[BLK:])
            qs = (q2_ref[_LO, us], q2_ref[_HI, us], qx_ref[:, us])
            dos = (do2_ref[_LO, us], do2_ref[_HI, us], dox_ref[:, us])
            lcols = [[ref[rs, u * A_W + h * HEAD:u * A_W + h * HEAD + 1] for h in range(A_HEADS)]
                     for ref, rs in ((l2_ref, _LO), (l2_ref, _HI), (lx_ref, _LO))]
            if sink is None:
                dlts = (e2_ref[_LO, us], e2_ref[_HI, us], ex_ref[:, us])
            else:
                dlts = tuple(_seg_sum64(d.astype(F32) * ref[rs, us])
                             for d, (ref, rs) in zip(dos, ((e2_ref, _LO), (e2_ref, _HI), (ex_ref, _LO))))
                for t in range(2):
                    for h in range(A_HEADS):
                        j = u * A_HEADS + h
                        sacc_ref[:, j:j + 1] += -jnp.exp(sk_ref[j] - lcols[t][h]) * dlts[t][:, h * HEAD:h * HEAD + 1]
            p_a, ds_a = tile(qs[0], dos[0], lcols[0], dlts[0], kd_a, vd_a, bias_a, 2 * BLK)
            p_b, ds_b = tile(qs[1], dos[1], lcols[1], dlts[1], kd_b, vd_b, bias_b, 2 * BLK)
            p_c, ds_c = tile(qs[2], dos[2], lcols[2], dlts[2], kd_c, vd_c, bias_c, BLK)
            dq_ref[_LO, us] = _dot(cat(ds_a), kd_a)
            dq_ref[_HI, us] = _dot(cat(ds_b), kd_b)
            outs = []
            for pa, pb, pc, lhs in ((ds_a, ds_b, ds_c, qs), (p_a, p_b, p_c, dos)):
                from_a = _fold_diag(_dot_tn(cat([t[:, BLK:] for t in pa]), lhs[0]), BLK)
                from_b = _fold_diag(_dot_tn(cat(pb), lhs[1]), 2 * BLK)
                from_c = _fold_diag(_dot_tn(cat(pc), lhs[2]), BLK)
                outs.append(jnp.concatenate([from_a + from_b[:BLK], from_b[BLK:] + from_c], axis=0))
            dk4, dv4 = outs
            if shared:
                fold = lambda t: (t[:, 0:HEAD] + t[:, HEAD:2 * HEAD]) + (t[:, 2 * HEAD:3 * HEAD] + t[:, 3 * HEAD:])
                dk_ref[:, u * HEAD:(u + 1) * HEAD] = fold(dk4)
                dv_ref[:, u * HEAD:(u + 1) * HEAD] = fold(dv4).astype(_MM)
            else:
                dk_ref[:, us] = dk4
                dv_ref[:, us] = dv4.astype(_MM)

    two = lambda w: pl.BlockSpec((2 * BLK, w), lambda i: (i, 0))
    prev = lambda w: pl.BlockSpec((BLK, w), lambda i: (jnp.maximum(2 * i - 1, 0), 0))
    nxt = lambda w: pl.BlockSpec((BLK, w), lambda i: (jnp.minimum(2 * i + 2, nb - 1), 0))
    in_specs = [two(wq), nxt(wq), two(wk), prev(wk), two(wk), prev(wk), two(wq), nxt(wq), two(wq), nxt(wq), two(wq), nxt(wq)]
    args = [qn, qn, kn, kn, vn, vn, do, do, lse, lse, dl_or_o, dl_or_o]
    out_specs = [two(wq), two(wk), two(wk)]
    out_shape = [_sds((rows, wq)), _sds((rows, wk)), _sds((rows, wk), _MM)]
    if sink is not None:
        in_specs.append(pl.BlockSpec(memory_space=pltpu.SMEM))
        args.append(sink)
        out_specs.append(_acc((BLK, 128)))
        out_shape.append(_sds((BLK, 128)))
    return _pc(body, name=name, grid=(nb // 2,), in_specs=in_specs, out_specs=out_specs, out_shape=out_shape)(*args)


def _k_prep_bwd(srcs, dqn, dkn, gq, gk, tabs, tab_row, *, wq, wk, rows_per_gain, name):
    rows = dqn.shape[0]
    ts = min(512, rows)
    ngain = gq.shape[0]

    def body(q_ref, k_ref, dq_ref, dk_ref, gq_ref, gk_ref, c_ref, sa_ref, sb_ref, oq_ref, ok_ref, aq_ref, ak_ref):
        i = pl.program_id(0)

        @pl.when(lax.rem(i * ts, rows_per_gain) == 0)
        def _():
            aq_ref[...] = jnp.zeros_like(aq_ref)
            ak_ref[...] = jnp.zeros_like(ak_ref)

        c, sa, sb = c_ref[...], sa_ref[...], sb_ref[...]
        for x_ref, d_ref, g_ref, o_ref, a_ref in ((q_ref, dq_ref, gq_ref, oq_ref, aq_ref),
                                                   (k_ref, dk_ref, gk_ref, ok_ref, ak_ref)):
            xh, r = _seg_norm(x_ref[...], HEAD)
            dt = _rope_bwd(d_ref[...], c, sa, sb)
            a_ref[...] += _sum8(dt * xh)
            z = dt * g_ref[...]
            o_ref[...] = (r * (z - xh * _seg_mean(z * xh, HEAD))).astype(_MM)

    gspec = lambda w: pl.BlockSpec((None, 1, w), lambda i: ((i * ts) // rows_per_gain, 0, 0))
    aspec = lambda w: pl.BlockSpec((None, 8, w), lambda i: ((i * ts) // rows_per_gain, 0, 0))
    return _pc(
        body, name=name, grid=(rows // ts,),
        in_specs=[_row(ts, wq, srcs[0][1]), _row(ts, wk, srcs[1][1]), _row(ts, wq), _row(ts, wk), gspec(wq), gspec(wk)]
        + [pl.BlockSpec((ts, 128), lambda i: (i + tab_row // ts, 0))] * 3,
        out_specs=[_row(ts, wq), _row(ts, wk), aspec(wq), aspec(wk)],
        out_shape=[_sds((rows, wq), _MM), _sds((rows, wk), _MM), _sds((ngain, 8, wq)), _sds((ngain, 8, wk))])(
            srcs[0][0], srcs[1][0], dqn, dkn, gq, gk, *tabs)


def _k_in_bwd(pieces, dgp, x, h, g1, dx1, w_in, w_gate):
    s = x.shape[0]
    ts = min(256, s)
    nin, ng = w_in.shape[2], w_gate.shape[2]
    widths = [p.shape[1] for p in pieces]
    ncol = sum(widths)

    def body(*refs):
        p_refs = refs[:len(pieces)]
        dgp_ref, x_ref, h_ref, g_ref, dx1_ref, wi_ref, wg_ref, gx_ref, gacc_ref, dwi_ref, dpj_ref = refs[len(pieces):]
        i = pl.program_id(0)

        @pl.when(i == 0)
        def _():
            gacc_ref[...] = jnp.zeros_like(gacc_ref)
            dwi_ref[...] = jnp.zeros_like(dwi_ref)

        off = 0
        for p_ref, w in zip(p_refs, widths):
            dpj_ref[:, off:off + w] = p_ref[...]
            off += w
        dh = jnp.zeros((ts, D_MODEL), F32)
        h = h_ref[...]
        for j in range(CHIPS):
            dpj = dpj_ref[:, j * nin:(j + 1) * nin]
            dwi_ref[j] += _dot_tn(h, dpj)
            dh = dh + _dot_nt(dpj, wi_ref[j])
            dh = dh + _dot_nt(dgp_ref[:, j * ng:(j + 1) * ng], wg_ref[j])
        xh, r = _rms(x_ref[...])
        gacc_ref[...] += _sum8(dh * xh)
        gx_ref[...] = dx1_ref[...] + _rms_bwd(dh, xh, r, g_ref[...])

    return _pc(
        body, name="in_proj_bwd", grid=(s // ts,),
        in_specs=[_row(ts, w) for w in widths] + [_row(ts, CHIPS * ng), _row(ts, D_MODEL), _row(ts, D_MODEL),
                                                  _res((1, D_MODEL)), _row(ts, D_MODEL), _res(w_in.shape),
                                                  _res(w_gate.shape)],
        out_specs=[_row(ts, D_MODEL), _acc((8, D_MODEL)), _acc(w_in.shape)],
        out_shape=[_sds((s, D_MODEL)), _sds((8, D_MODEL)), _sds(w_in.shape)],
        scratch=[pltpu.VMEM((ts, ncol), _MM)])(*pieces, dgp, x, h, g1, dx1, w_in, w_gate)


def _k_wgrad(a, b, *, nblk, stacked, name):
    s, k = a.shape
    n = b.shape[1]
    nb = n // nblk
    ts = min(2048 if k <= 1024 else 1024, s)

    def body(a_ref, b_ref, o_ref):
        @pl.when(pl.program_id(1) == 0)
        def _():
            o_ref[...] = jnp.zeros_like(o_ref)

        o_ref[...] += _dot_tn(a_ref[...], b_ref[...])

    if stacked:
        out_spec, out_shape = pl.BlockSpec((None, k, nb), lambda g, t: (g, 0, 0)), _sds((nblk, k, nb))
    else:
        out_spec, out_shape = pl.BlockSpec((k, nb), lambda g, t: (0, g)), _sds((k, n))
    return _pc(body, name=name, grid=(nblk, s // ts),
               in_specs=[pl.BlockSpec((ts, k), lambda g, t: (t, 0)), pl.BlockSpec((ts, nb), lambda g, t: (t, g))],
               out_specs=[out_spec], out_shape=[out_shape])(a, b)[0]


def _to_res(t, d):
    s, c = t.shape
    return t if d == 1 else t.reshape(s // d, d, c).transpose(1, 0, 2).reshape(s, c)


def _from_res(t, d):
    s, c = t.shape
    return t if d == 1 else t.reshape(d, s // d, c).transpose(1, 0, 2).reshape(s, c)


def _tile_gain(g, heads):
    return jnp.tile(g, (1,) * (g.ndim - 1) + (heads,))[..., None, :]


def _local_step(x, mem, pos, target, small, get_w_in, get_rest, on_grads):
    s = x.shape[0]
    nblk = s // BLK
    g1, g2 = small["attn_norm"], small["ffn_norm"]

    pos_rows = jnp.concatenate([_to_res(pos[:, None], d)[:, 0] for _, d in A_GROUPS] + [pos])
    tabs = _rope_tables(pos_rows)
    w_in = get_w_in(tabs[0])

    h, qa0, qa1, qa2, q_b, k_b, v_b, m_q = _k_in(x, g1, w_in)

    qkv_a = jnp.concatenate([_to_res(t, d) for t, (_, d) in zip((qa0, qa1, qa2), A_GROUPS)], axis=0)
    gq_a = _tile_gain(small["a_q_norm"], A_HEADS)
    gk_a = _tile_gain(small["a_k_norm"], A_HEADS)
    src_a = ((qkv_a, 0), (qkv_a, 1), (qkv_a, 2))
    qn_a, kn_a, vn_a = _k_prep(src_a, gq_a, gk_a, tabs, 0, wq=A_W, wk=A_W, rows_per_gain=s, name="prep_a")
    segs_a = tuple((gi * nblk, nblk // d) for gi, (_, d) in enumerate(A_GROUPS))
    o_res, l_res = _k_band_fwd(qn_a, kn_a, vn_a, hq=A_HEADS, hk=A_HEADS, max_dist=BLK, segs=segs_a, sink=None,
                               name="attn_a")
    og = [_from_res(o_res[gi * s:(gi + 1) * s], d) for gi, (_, d) in enumerate(A_GROUPS)]
    lg = [_from_res(l_res[gi * s:(gi + 1) * s], d) for gi, (_, d) in enumerate(A_GROUPS)]

    gq_b = _tile_gain(small["b_q_norm"], B_QH)
    gk_b = _tile_gain(small["b_k_norm"], B_KVH)
    src_b = ((q_b, 0), (k_b, 0), (v_b, 0))
    qn_b, kn_b, vn_b = _k_prep(src_b, gq_b, gk_b, tabs, 3 * s, wq=B_QH * HEAD, wk=B_KVH * HEAD, rows_per_gain=s,
                               name="prep_b")
    sink_x = small["b_sinks"][0]
    segs_b = ((0, nblk),)
    o_b, l_b = _k_band_fwd(qn_b, kn_b, vn_b, hq=B_QH, hk=B_KVH, max_dist=B_WINDOW - 1, segs=segs_b, sink=sink_x,
                           name="attn_b")

    wts = get_rest(0, o_b)

    gq_m = _tile_gain(small["m_q_norm"], M_HEADS)[0]
    gk_m = _tile_gain(small["m_k_norm"], M_HEADS)[0]
    mem_n, kv, mk, mv = _k_memkv(mem, small["mem_norm"], wts["w_mem_kv"], gk_m)
    o_m = _k_mem_fwd(m_q, gq_m, mk, mv)

    o_a, merged, x1, h2, gates = _k_merge(og, lg, o_b, o_m, h, x, wts["w_gate"], small["b_gate"], wts["w_o_a"],
                                          wts["w_o_b"], wts["w_o_m"], wts["w_out"], g2)
    wts.update(get_rest(1, x1))
    u = _k_up(h2, wts["w_up"])
    dy, f, dc, loss_acc = _k_ffn(u, wts["conv_w"], small["conv_b"], wts["w_down"], wts["w_down"].T, x1, target)
    loss = (0.5 / D_MODEL) * jnp.sum(loss_acc)

    dx1, du, cacc, g2acc = _k_conv_bwd(dc, u, wts["conv_w"], wts["w_up"], x1, g2, dy)
    tok = on_grads({"w_up": _k_wgrad(h2, du, nblk=CHIPS, stacked=True, name="dw_up"),
                    "w_down": _k_wgrad(f, dy, nblk=2, stacked=False, name="dw_down").reshape(CHIPS, -1, D_MODEL)}, dx1)
    (dgp, dog0, dog1, dog2, dl0, dl1, dl2, do_b, do_m, bacc, dw_oa, dw_ob, dw_om, dw_out, dw_gate) = _k_merge_bwd(
        dx1, og, lg, o_a, o_b, o_m, gates, merged, h, wts["w_gate"].shape, wts["w_o_a"], wts["w_o_b"], wts["w_o_m"],
        wts["w_out"], tok)
    tok = on_grads({"w_gate": dw_gate,
                    "w_o_a": dw_oa, "w_o_b": dw_ob, "w_o_m": dw_om, "w_out": dw_out.reshape(CHIPS, -1, D_MODEL)}, do_m)

    dq_m, dmk, dmv, gqm_acc = _k_mem_bwd(m_q, gq_m + tok[0:1, 0:1], mk, mv, o_m, do_m)
    dw_kv, gmem_acc, gkm_acc = _k_memkv_bwd(mem, small["mem_norm"], wts["w_mem_kv"], gk_m, mem_n, kv, dmk, dmv)

    dq_bn, dk_bn, dv_b, sacc = _k_band_bwd(qn_b, kn_b, vn_b, do_b, l_b, o_b, hq=B_QH, hk=B_KVH,
                                           max_dist=B_WINDOW - 1, segs=segs_b, sink=sink_x, name="attn_b_bwd")
    tok = on_grads({}, dq_bn)
    dq_b, dk_b, gqb_acc, gkb_acc = _k_prep_bwd(src_b, dq_bn, dk_bn, gq_b + tok[0:1, 0:1], gk_b, tabs, 3 * s, wq=B_QH * HEAD,
                                               wk=B_KVH * HEAD, rows_per_gain=s, name="prep_b_bwd")

    do_res = jnp.concatenate([_to_res(t, d) for t, (_, d) in zip((dog0, dog1, dog2), A_GROUPS)], axis=0)
    dl_res = jnp.concatenate([_to_res(t, d) for t, (_, d) in zip((dl0, dl1, dl2), A_GROUPS)], axis=0)
    dq_an, dk_an, dv_a = _k_band_bwd(qn_a, kn_a, vn_a, do_res, l_res, dl_res, hq=A_HEADS, hk=A_HEADS, max_dist=BLK,
                                     segs=segs_a, sink=None, name="attn_a_bwd")
    dq_a, dk_a, gqa_acc, gka_acc = _k_prep_bwd(src_a, dq_an, dk_an, gq_a, gk_a, tabs, 0, wq=A_W, wk=A_W,
                                               rows_per_gain=s, name="prep_a_bwd")
    pieces = []
    for gi, (_, d) in enumerate(A_GROUPS):
        rs = slice(gi * s, (gi + 1) * s)
        pieces += [_from_res(t[rs], d) for t in (dq_a, dk_a, dv_a)]
    pieces += [dq_b, dk_b, dv_b, dq_m]
    grad_x, g1acc, dw_in = _k_in_bwd(pieces, dgp, x, h, g1, dx1, w_in, wts["w_gate"])
    on_grads({"w_in": dw_in,
              "w_mem_kv": dw_kv.reshape(CHIPS, -1, 2 * M_W)}, grad_x)

    def fold(acc, heads):
        v = jnp.sum(acc, axis=-2)
        return jnp.sum(v.reshape(v.shape[:-1] + (heads, -1)), axis=-2)

    csum = jnp.sum(cacc, axis=1)
    sml = {
        "attn_norm": jnp.sum(g1acc, axis=0), "a_q_norm": fold(gqa_acc, A_HEADS), "a_k_norm": fold(gka_acc, A_HEADS),
        "b_q_norm": fold(gqb_acc[0], B_QH), "b_k_norm": fold(gkb_acc[0], B_KVH),
        "b_sinks": jnp.sum(sacc, axis=0)[:B_QH], "mem_norm": jnp.sum(gmem_acc, axis=0),
        "m_q_norm": fold(gqm_acc, M_HEADS), "m_k_norm": fold(gkm_acc, M_HEADS),
        "b_gate": jnp.sum(bacc, axis=0), "ffn_norm": jnp.sum(g2acc, axis=0),
        "conv_w": csum[1:], "conv_b": csum[0],
    }
    return loss, grad_x, sml


def _mesh_pos():
    return lax.axis_index("x"), lax.axis_index("y"), lax.axis_index("c")


def _chip_peers(x, y):
    return [(1 - x, y), (x, 1 - y), (1 - x, 1 - y)]


_ANY = pl.BlockSpec(memory_space=pl.ANY)


def _comm_call(body, *, name, n_in, out_shape, scratch):
    return pl.pallas_call(body, name=name, in_specs=[_ANY] * n_in, out_specs=[_ANY] * len(out_shape),
                          out_shape=out_shape, scratch_shapes=scratch)


def _remote(src, dst, send_sem, recv_sem, dev):
    return pltpu.make_async_remote_copy(src_ref=src, dst_ref=dst, send_sem=send_sem, recv_sem=recv_sem,
                                        device_id=dev, device_id_type=MESH)


def _pair_join(halves, name):
    nt = len(halves)

    def body(*refs):
        ins, got = refs[:nt], refs[nt:2 * nt]
        send_sems, recv_sems = refs[2 * nt:]
        x, y, c = _mesh_pos()
        cps = []
        for t in range(nt):
            rc = _remote(ins[t], got[t], send_sems.at[t], recv_sems.at[t], (x, y, 1 - c))
            rc.start()
            cps.append(rc)
        for rc in cps:
            rc.wait()

    out_shape = [_sds(hf.shape, hf.dtype) for hf in halves]
    scratch = [pltpu.SemaphoreType.DMA((nt,)), pltpu.SemaphoreType.DMA((nt,))]
    return _comm_call(body, name=name, n_in=nt, out_shape=out_shape, scratch=scratch)(*halves)


_HBM = pl.BlockSpec(memory_space=pltpu.HBM)
_SEMS = pl.BlockSpec(memory_space=pltpu.SEMAPHORE)
_EFFECT = pltpu.SideEffectType.DATAFLOW_SIDE_EFFECTING


def _bcast_copies(ins, lands, send_sems, recv_sems):
    x, y, c = _mesh_pos()
    me = 2 * x + y
    targets = [((px, py, c), 2 * px + py) for px, py in _chip_peers(x, y)] + [((x, y, 1 - c), me)]
    out = []
    for t in range(len(ins)):
        for k, (dev, idx) in enumerate(targets):
            i = t * len(targets) + k
            arrival = lambda t=t, i=i, idx=idx, dev=dev: _remote(ins[t], lands[t].at[idx], send_sems.at[i],
                                                                 recv_sems.at[i], dev)
            out.append((_remote(ins[t], lands[t].at[me], send_sems.at[i], recv_sems.at[i], dev), arrival))
    return out


def _scatter_copies(ins, lands, send_sems, recv_sems):
    x, y, c = _mesh_pos()
    out = []
    for t in range(len(ins)):
        for k, (px, py) in enumerate(_chip_peers(x, y)):
            i = t * 3 + k
            cp = _remote(ins[t].at[2 * px + py], lands[t].at[k], send_sems.at[i], recv_sems.at[i], (px, py, c))
            out.append((cp, lambda cp=cp: cp))
    return out


def _pair_copies(ins, lands, send_sems, recv_sems):
    x, y, c = _mesh_pos()
    out = []
    for t in range(len(ins)):
        hr = ins[t].shape[1] // 2
        give = ins[t].at[:, pl.ds(pl.multiple_of((1 - c) * hr, 8), hr), :]
        cp = _remote(give, lands[t], send_sems.at[t], recv_sems.at[t], (x, y, 1 - c))
        out.append((cp, lambda cp=cp: cp))
    return out


def _join_copies(ins, lands, send_sems, recv_sems):
    x, y, c = _mesh_pos()
    out = []
    for t in range(len(ins)):
        cp = _remote(ins[t], lands[t], send_sems.at[t], recv_sems.at[t], (x, y, 1 - c))
        out.append((cp, lambda cp=cp: cp))
    return out


def _half_copies(ins, lands, send_sems, recv_sems):
    x, y, c = _mesh_pos()
    me = 2 * x + y
    out = []
    for t in range(len(ins)):
        hr = ins[t].shape[0] // 2
        rows = pl.ds(pl.multiple_of(c * hr, 8), hr)
        for k, (px, py) in enumerate(_chip_peers(x, y)):
            i = t * 3 + k
            arrival = lambda t=t, i=i, px=px, py=py, rows=rows: _remote(
                ins[t].at[rows, :], lands[t].at[2 * px + py].at[rows, :], send_sems.at[i], recv_sems.at[i], (px, py, c))
            out.append((_remote(ins[t].at[rows, :], lands[t].at[me].at[rows, :], send_sems.at[i], recv_sems.at[i],
                                (px, py, c)), arrival))
    return out


def _finish_halves(shards, stacks):
    nt = len(shards)

    def body(*refs):
        ins, held, outs = refs[:nt], refs[nt:2 * nt], refs[2 * nt:3 * nt]
        fwd_s, fwd_r, own_s, own_r = refs[3 * nt:]
        x, y, c = _mesh_pos()
        me = 2 * x + y
        sib = (x, y, 1 - c)
        pending = []
        for t in range(nt):
            hr = shards[t].shape[0] // 2
            half = lambda ref, who: ref.at[pl.ds(pl.multiple_of(who * hr, 8), hr), :]
            own = _remote(ins[t], outs[t].at[me], own_s.at[t], own_r.at[t], sib)
            own.start()
            pending.append(own.wait)
            for k, (px, py) in enumerate(_chip_peers(x, y)):
                pj = 2 * px + py
                fw = _remote(half(held[t].at[pj], c), half(outs[t].at[pj], c), fwd_s.at[t, k], fwd_r.at[t, k], sib)
                fw.start()
                pending.append(fw.wait_send)
                other = half(outs[t].at[pj], 1 - c)
                pending.append(_remote(other, other, fwd_s.at[t, k], fwd_r.at[t, k], sib).wait_recv)
        for wait in pending:
            wait()

    dma = pltpu.SemaphoreType.DMA
    return pl.pallas_call(
        body, name="gather_w_in_finish", in_specs=[_ANY] * (2 * nt), out_specs=[_ANY] * nt,
        out_shape=[_sds(a.shape, a.dtype) for a in stacks], input_output_aliases={nt + i: i for i in range(nt)},
        scratch_shapes=[dma((nt, 3)), dma((nt, 3)), dma((nt,)), dma((nt,))])(*shards, *stacks)


def _split_start(copies, srcs, land_shapes, ncopy, dep, name, lands=None):
    nt = len(srcs)

    def body(*refs):
        ins, lands = refs[:nt], refs[nt:2 * nt]
        send_sems, recv_sems, token = refs[2 * nt + 1], refs[2 * nt + 2], refs[-1]
        for send, _ in copies(ins, lands, send_sems, recv_sems):
            send.start()
        token[...] = jnp.zeros_like(token)

    if lands is None:
        lands = [lax.empty(sh, a.dtype) for sh, a in zip(land_shapes, srcs)]
    lands = [pltpu.with_memory_space_constraint(a, pltpu.HBM) for a in lands]
    srcs = [pltpu.with_memory_space_constraint(a, pltpu.HBM) for a in srcs]
    dma = pltpu.SemaphoreType.DMA
    out_shape = ([dma((nt * ncopy,)), dma((nt * ncopy,))] + [pltpu.HBM(a.shape, a.dtype) for a in srcs + lands]
                 + [_sds((8, 128))])
    outs = pl.pallas_call(
        body, name=name, in_specs=[_HBM] * (2 * nt) + [_ANY],
        out_specs=[_SEMS, _SEMS] + [_HBM] * (2 * nt) + [pl.BlockSpec(memory_space=pltpu.VMEM)], out_shape=out_shape,
        input_output_aliases={i: 2 + i for i in range(2 * nt)},
        compiler_params=pltpu.CompilerParams(has_side_effects=_EFFECT))(*srcs, *lands, dep)
    return outs[0], outs[1], outs[2:2 + nt], outs[2 + nt:2 + 2 * nt], outs[-1]


def _split_wait(copies, send_sems, recv_sems, srcs, lands, after, name):
    nt = len(srcs)

    def body(*refs):
        ins, lnd = refs[:nt], refs[nt:2 * nt]
        for send, arrival in copies(ins, lnd, refs[2 * nt], refs[2 * nt + 1]):
            send.wait_send()
            arrival().wait_recv()

    outs = pl.pallas_call(
        body, name=name, in_specs=[_HBM] * (2 * nt) + [_SEMS, _SEMS, _ANY], out_specs=[_HBM] * (2 * nt),
        out_shape=[pltpu.HBM(a.shape, a.dtype) for a in list(srcs) + list(lands)],
        input_output_aliases={i: i for i in range(2 * nt)},
        compiler_params=pltpu.CompilerParams(has_side_effects=_EFFECT))(*srcs, *lands, send_sems, recv_sems, after)
    return outs[:nt], outs[nt:]


def _small_copies(ins, lands, send_sems, recv_sems):
    x, y, c = _mesh_pos()
    me = 4 * x + 2 * y + c
    out = []
    for k in range(1, NDEV):
        px, py, pc = x ^ (k >> 2), y ^ ((k >> 1) & 1), c ^ (k & 1)
        arrival = lambda k=k, px=px, py=py, pc=pc: _remote(ins[0], lands[0].at[4 * px + 2 * py + pc], send_sems.at[k - 1],
                                                            recv_sems.at[k - 1], (px, py, pc))
        out.append((_remote(ins[0], lands[0].at[me], send_sems.at[k - 1], recv_sems.at[k - 1], (px, py, pc)), arrival))
    return out


def _row_tile(r, c, mib=1):
    t = r
    while t * c * 4 > (mib << 20) and t % 16 == 0:
        t //= 2
    return t


def _k_pair_add(full, got, name):
    g, r, c = full.shape
    hr = r // 2
    tr = _row_tile(hr, c, 4)
    nh = hr // tr

    def body(a_ref, b_ref, o_ref):
        o_ref[...] = (a_ref[...] + b_ref[...]).astype(_WIRE)

    mine = pl.BlockSpec((None, tr, c), lambda i, j: (i, lax.axis_index("c") * nh + j, 0))
    spec = pl.BlockSpec((None, tr, c), lambda i, j: (i, j, 0))
    return _pc(body, name=name, grid=(g, nh), in_specs=[mine, spec], out_specs=[spec],
               out_shape=[_sds((g, hr, c), _WIRE)])(full, got)[0]


def _k_chip_sum(parts, slots, name):
    _, r, c = parts.shape
    tr = _row_tile(r, c, 4)

    def body(a_ref, s_ref, o_ref):
        acc = a_ref[...].astype(F32)
        for k in range(3):
            acc = acc + s_ref[k].astype(F32)
        o_ref[...] = acc

    own = pl.BlockSpec((None, tr, c), lambda i: (2 * lax.axis_index("x") + lax.axis_index("y"), i, 0))
    return _pc(body, name=name, grid=(r // tr,), in_specs=[own, pl.BlockSpec((3, tr, c), lambda i: (0, i, 0))],
               out_specs=[_row(tr, c)], out_shape=[_sds((r, c))])(parts, slots)[0]


def _adam(w, g, m, v):
    m = ADAM_B1 * m + (1.0 - ADAM_B1) * g
    v = ADAM_B2 * v + (1.0 - ADAM_B2) * (g * g)
    m_hat = m / (1.0 - ADAM_B1 ** ADAM_STEP)
    v_hat = v / (1.0 - ADAM_B2 ** ADAM_STEP)
    return -ADAM_LR * (m_hat / (jnp.sqrt(v_hat) + ADAM_EPS) + ADAM_WD * w), m, v


def _k_adam(w, mine, theirs, m, v, dep, name):
    r, c = w.shape
    hr = r // 2
    tr = _row_tile(hr, c, 2)
    nh = hr // tr

    def body(w_ref, a_ref, b_ref, m_ref, v_ref, dep_ref, g_ref, d_ref, mo_ref, vo_ref):
        upper = (pl.program_id(0) >= nh).astype(jnp.int32)
        g = jnp.where(upper == lax.axis_index("c"), a_ref[...], b_ref[...])
        g_ref[...] = g
        d_ref[...], mo_ref[...], vo_ref[...] = _adam(w_ref[...], g, m_ref[...], v_ref[...])

    hspec = pl.BlockSpec((tr, c), lambda i: (jnp.where(i >= nh, i - nh, i), 0))
    return _pc(body, name=name, grid=(r // tr,),
               in_specs=[_row(tr, c), hspec, hspec, _row(tr, c), _row(tr, c), _res((8, 128))],
               out_specs=[_row(tr, c)] * 4, out_shape=[_sds((r, c))] * 4)(w, mine, theirs, m, v, dep)


def _k_sum8(a):
    _, n, _ = a.shape

    def body(a_ref, o_ref):
        acc = a_ref[0]
        for k in range(1, NDEV):
            acc = acc + a_ref[k]
        o_ref[...] = acc

    return _pc(body, name="sum_small_grads", grid=(1,), in_specs=[_acc(a.shape)], out_specs=[_acc((n, 128))],
               out_shape=[_sds((n, 128))])(a)[0]


def _k_adam_small(ws, gs, ms, vs):
    n = len(ws)

    def body(*refs):
        for k in range(n):
            w_ref, g_ref, m_ref, v_ref, d_ref, mo_ref, vo_ref = refs[k::n]
            d_ref[...], mo_ref[...], vo_ref[...] = _adam(w_ref[...], g_ref[...], m_ref[...], v_ref[...])

    specs = [_acc(a.shape) for a in ws]
    outs = _pc(body, name="adam_small", grid=(1,), in_specs=specs * 4, out_specs=specs * 3,
               out_shape=[_sds(a.shape) for a in ws] * 3)(*ws, *gs, *ms, *vs)
    return outs[:n], outs[n:2 * n], outs[2 * n:]


def _pack(vals):
    rows = []
    for a in vals:
        flat = a.reshape(-1)
        n = -(-flat.shape[0] // 1024) * 1024
        rows.append(jnp.pad(flat, (0, n - flat.shape[0])).reshape(n // 128, 128))
    return jnp.concatenate(rows, axis=0)


def _unpack(packed, shapes):
    out, off = [], 0
    for sh in shapes:
        size = int(np.prod(sh))
        n = -(-size // 1024) * 1024
        out.append(packed[off // 128:(off + n) // 128].reshape(-1)[:size].reshape(sh))
        off += n
    return out


_WEIGHTS = ["attn_norm", "w_in", "a_q_norm", "a_k_norm", "b_q_norm", "b_k_norm", "b_sinks", "mem_norm", "w_mem_kv",
            "m_q_norm", "m_k_norm", "w_o_a", "w_o_b", "w_o_m", "w_gate", "b_gate", "w_out", "ffn_norm", "w_up",
            "conv_w", "conv_b", "w_down"]
_BIG = ["w_in", "w_mem_kv", "w_o_a", "w_o_b", "w_o_m", "w_gate", "w_out", "w_up", "w_down"]
_SMALL = [n for n in _WEIGHTS if n not in _BIG]


def kernel(x, mem, positions, attn_norm, w_in, a_q_norm, a_k_norm, b_q_norm, b_k_norm, b_sinks, mem_norm, w_mem_kv, m_q_norm, m_k_norm, w_o_a, w_o_b, w_o_m, w_gate, b_gate, w_out, ffn_norm, w_up, conv_w, conv_b, w_down, loss_target, m_attn_norm, m_w_in, m_a_q_norm, m_a_k_norm, m_b_q_norm, m_b_k_norm, m_b_sinks, m_mem_norm, m_w_mem_kv, m_m_q_norm, m_m_k_norm, m_w_o_a, m_w_o_b, m_w_o_m, m_w_gate, m_b_gate, m_w_out, m_ffn_norm, m_w_up, m_conv_w, m_conv_b, m_w_down, v_attn_norm, v_w_in, v_a_q_norm, v_a_k_norm, v_b_q_norm, v_b_k_norm, v_b_sinks, v_mem_norm, v_w_mem_kv, v_m_q_norm, v_m_k_norm, v_w_o_a, v_w_o_b, v_w_o_m, v_w_gate, v_b_gate, v_w_out, v_ffn_norm, v_w_up, v_conv_w, v_conv_b, v_w_down):
    given = dict(locals())
    w = {n: given[n][0] for n in _WEIGHTS}
    m1 = {n: given["m_" + n][0] for n in _WEIGHTS}
    m2 = {n: given["v_" + n][0] for n in _WEIGHTS}

    zeros = jnp.zeros((8, 128), F32)
    w_in_shard = w["w_in"].astype(_MM)
    *w_in_handles, tok = _split_start(_half_copies, [w_in_shard], [(CHIPS,) + w_in_shard.shape], 3, zeros,
                                      "gather_w_in_start")

    def get_w_in(after):
        send, recv, srcs, lands = w_in_handles
        srcs, lands = _split_wait(_half_copies, send, recv, srcs, lands, after, "gather_w_in_wait")
        return _finish_halves(srcs, lands)[0]

    stages = (["w_gate", "w_mem_kv", "w_o_a", "w_o_b", "w_o_m", "w_out"], ["w_up", "w_down", "conv_w"])
    started = []
    for k, names in enumerate(stages):
        shards = [w[n] if n == "conv_w" else w[n].astype(_MM) for n in names]
        *handles, tok = _split_start(_bcast_copies, shards, [(CHIPS,) + a.shape for a in shards], 4, tok,
                                     "gather_start_%d" % k)
        started.append(handles)
    small = {n: (w[n][None, :] if w[n].ndim == 1 else w[n]) for n in _SMALL if n != "conv_w"}
    positions = positions + tok[0:1, 0:1].astype(positions.dtype)

    def get_rest(stage, after):
        send, recv, srcs, lands = started[stage]
        got = _split_wait(_bcast_copies, send, recv, srcs, lands, after, "gather_wait_%d" % stage)[1]
        wts = dict(zip(stages[stage], got))
        for n in ("w_mem_kv", "w_out", "w_down"):
            if n in wts:
                wts[n] = wts[n].reshape(-1, wts[n].shape[-1])
        return wts

    parts, slots, pair, scat, started_pair = {}, {}, [], [], [None]

    def finish_pair(after):
        names, tag, send, recv, srcs, lands = pair.pop()
        full, got = _split_wait(_pair_copies, send, recv, srcs, lands, after, "pair_wait_" + tag)
        mine = [_k_pair_add(f, b, "pair_add_" + n) for n, f, b in zip(names, full, got)]
        shapes = [(3,) + p.shape[1:] for p in mine]
        send, recv, srcs, lands, token = _split_start(_scatter_copies, mine, shapes, 3, zeros, "scatter_start_" + tag)
        scat.append((names, tag, send, recv, srcs, lands))
        return token

    def on_grads(group, after):
        names = list(group)
        tag = "_".join(names)
        token = finish_pair(after) if pair else zeros
        if not group:
            return token
        grads_g = [group[n] for n in names]
        shapes = [(CHIPS, g.shape[1] // 2, g.shape[2]) for g in grads_g]
        send, recv, srcs, lands, token = _split_start(_pair_copies, grads_g, shapes, 1, token, "pair_start_" + tag)
        pair.append((names, tag, send, recv, srcs, lands))
        started_pair[0] = token
        return token

    loss, grad_x, sml = _local_step(x[0], mem[0], positions[0], loss_target[0], small, get_w_in, get_rest, on_grads)

    packed = _pack([sml[n] for n in _SMALL] + [loss.reshape(1)])
    me = 4 * lax.axis_index("x") + 2 * lax.axis_index("y") + lax.axis_index("c")
    land = lax.dynamic_update_slice(jnp.zeros((NDEV,) + packed.shape, F32), packed[None], (me, 0, 0))
    *small_h, tok = _split_start(_small_copies, [packed], None, NDEV - 1, zeros, "gather_small_start", lands=[land])
    started_pair[0] = started_pair[0] + tok

    early = [n for names, *_ in scat for n in names]
    for names, tag, send, recv, srcs, lands in scat:
        mine, got = _split_wait(_scatter_copies, send, recv, srcs, lands, started_pair[0], "scatter_wait_" + tag)
        parts.update(zip(names, mine))
        slots.update(zip(names, got))
    scat.clear()
    reduced = {n: _k_chip_sum(parts[n], slots[n], "chip_add_" + n) for n in early}
    halves = [reduced[n] for n in early]
    *join, tok = _split_start(_join_copies, halves, [a.shape for a in halves], 1, zeros, "pair_join_start_early")
    grads = {}

    delta, new_m, new_v = {}, {}, {}
    dep = finish_pair(tok)
    mine, got = _split_wait(_join_copies, *join, dep, "pair_join_wait_early")
    reduced.update(zip(early, mine))
    theirs = dict(zip(early, got))
    for n in early:
        grads[n], delta[n], new_m[n], new_v[n] = _k_adam(w[n], reduced[n], theirs[n], m1[n], m2[n], dep, "adam_" + n)
        dep = delta[n]
    gathered = _split_wait(_small_copies, *small_h, dep, "gather_small_wait")[1][0]
    shapes = [sml[n].shape for n in _SMALL] + [(1,)]
    *gsmall, loss = _unpack(_k_sum8(gathered), shapes)
    loss = loss[0]
    gsm = dict(zip(_SMALL, gsmall))
    nu = w["conv_w"].shape[1]
    chip = 2 * lax.axis_index("x") + lax.axis_index("y")
    gsm["conv_w"] = lax.dynamic_slice_in_dim(gsm["conv_w"], chip * nu, nu, axis=1)
    for n in _SMALL:
        grads[n] = gsm[n].reshape(w[n].shape)
    as2d = lambda d: [d[n][None, :] if d[n].ndim == 1 else d[n] for n in _SMALL]
    for dst, outs in zip((delta, new_m, new_v), _k_adam_small(as2d(w), as2d(grads), as2d(m1), as2d(m2))):
        dst.update((n, a.reshape(w[n].shape)) for n, a in zip(_SMALL, outs))
    late, tag, send, recv, srcs, lands = scat.pop()
    mine, got = _split_wait(_scatter_copies, send, recv, srcs, lands, dep, "scatter_wait_" + tag)
    for n, a, b in zip(late, mine, got):
        reduced[n] = _k_chip_sum(a, b, "chip_add_" + n)
    theirs.update(zip(late, _pair_join([reduced[n] for n in late], "grad_pair_join_late")))
    for n in late:
        grads[n], delta[n], new_m[n], new_v[n] = _k_adam(w[n], reduced[n], theirs[n], m1[n], m2[n], zeros, "adam_" + n)

    lead = lambda d: [d[n][None] for n in _WEIGHTS]
    return (loss, grad_x[None], *lead(grads), *lead(delta), *lead(new_m), *lead(new_v))
```

```python
import math

import jax
import jax.numpy as jnp
import numpy as np
from jax import lax
from jax.experimental import pallas as pl
from jax.experimental.pallas import tpu as pltpu

F32 = jnp.float32
_MM = jnp.bfloat16
_WIRE = jnp.bfloat16

D_MODEL = 1024
HEAD = 64
BLK = 128
A_GROUPS = ((128, 1), (512, 4), (2048, 16))
A_HEADS = 4
A_W = A_HEADS * HEAD
B_QH = 8
B_KVH = 2
B_WINDOW = 128
M_HEADS = 4
M_HD = 128
M_W = M_HEADS * M_HD
D_FF = 2816
EPS = 1e-6
NEG = -1e30
ROPE_THETA = 500000.0
ROPE_ROT = 16
CHIPS = 4
NDEV = 8
ADAM_LR, ADAM_B1, ADAM_B2, ADAM_EPS, ADAM_WD, ADAM_STEP = 0.001, 0.9, 0.999, 1e-08, 0.01, 10
VMEM_LIMIT = 58 * 1024 * 1024
MESH = pl.DeviceIdType.MESH


def _pc(body, *, name, grid, in_specs, out_specs, out_shape, scratch=()):
    return pl.pallas_call(
        body, name=name, grid=grid, in_specs=in_specs, out_specs=out_specs, out_shape=out_shape,
        scratch_shapes=list(scratch),
        compiler_params=pltpu.CompilerParams(dimension_semantics=("arbitrary",) * len(grid),
                                             vmem_limit_bytes=VMEM_LIMIT))


def _row(ts, c, col=0):
    return pl.BlockSpec((ts, c), lambda i: (i, col))


def _res(shape):
    n = len(shape)
    return pl.BlockSpec(tuple(shape), lambda i: (0,) * n, pipeline_mode=pl.Buffered(1))


def _acc(shape):
    n = len(shape)
    return pl.BlockSpec(tuple(shape), lambda i: (0,) * n, pipeline_mode=pl.Buffered(1))


def _sds(shape, dtype=F32):
    return jax.ShapeDtypeStruct(tuple(shape), dtype)


def _dot(a, b):
    return jnp.dot(a.astype(_MM), b.astype(_MM), preferred_element_type=F32)


def _dot_nt(a, b):
    return lax.dot_general(a.astype(_MM), b.astype(_MM), (((1,), (1,)), ((), ())), preferred_element_type=F32)


def _dot_tn(a, b):
    return lax.dot_general(a.astype(_MM), b.astype(_MM), (((0,), (0,)), ((), ())), preferred_element_type=F32)


def _sum8(v):
    ts, c = v.shape
    return jnp.sum(v.reshape(ts // 8, 8, c), axis=0)


def _sigmoid(z):
    return 1.0 / (1.0 + jnp.exp(-z))


def _rms(x):
    r = lax.rsqrt(jnp.mean(x * x, axis=-1, keepdims=True) + EPS)
    return x * r, r


def _rms_bwd(dy, xh, r, gain):
    z = dy * gain
    return r * (z - xh * jnp.mean(z * xh, axis=-1, keepdims=True))


def _split_hi_lo(v):
    hi = v.astype(_MM)
    return hi, (v - hi.astype(F32)).astype(_MM)


def _lane_head(shape):
    return lax.shift_right_logical(lax.broadcasted_iota(jnp.int32, shape, len(shape) - 1), 6)


def _seg_sum64(v):
    w = v.shape[1]
    e = jnp.where(_lane_head((w, w)) == lax.shift_right_logical(lax.broadcasted_iota(jnp.int32, (w, w), 0), 6),
                  1.0, 0.0).astype(_MM)
    hi, lo = _split_hi_lo(v)
    return jnp.dot(hi, e, preferred_element_type=F32) + jnp.dot(lo, e, preferred_element_type=F32)


def _seg_norm(x, seg):
    if seg == HEAD:
        r = lax.rsqrt(_seg_sum64(x * x) * (1.0 / HEAD) + EPS)
        return x * r, r
    w = x.shape[1]
    xh, rr = [], []
    for s in range(w // seg):
        xs = x[:, s * seg:(s + 1) * seg]
        r = lax.rsqrt(jnp.mean(xs * xs, axis=-1, keepdims=True) + EPS)
        xh.append(xs * r)
        rr.append(jnp.broadcast_to(r, xs.shape))
    return jnp.concatenate(xh, axis=1), jnp.concatenate(rr, axis=1)


def _seg_mean(v, seg):
    if seg == HEAD:
        return _seg_sum64(v) * (1.0 / HEAD)
    w = v.shape[1]
    out = []
    for s in range(w // seg):
        vs = v[:, s * seg:(s + 1) * seg]
        out.append(jnp.broadcast_to(jnp.mean(vs, axis=-1, keepdims=True), vs.shape))
    return jnp.concatenate(out, axis=1)


def _rope(t, c, sa, sb):
    out = []
    for cb in range(t.shape[1] // 128):
        tc = t[:, cb * 128:(cb + 1) * 128]
        out.append(tc * c + pltpu.roll(tc, 120, 1) * sa + pltpu.roll(tc, 8, 1) * sb)
    return jnp.concatenate(out, axis=1) if len(out) > 1 else out[0]


def _rope_bwd(dy, c, sa, sb):
    out = []
    for cb in range(dy.shape[1] // 128):
        dc = dy[:, cb * 128:(cb + 1) * 128]
        out.append(dc * c + pltpu.roll(dc * sa, 8, 1) + pltpu.roll(dc * sb, 120, 1))
    return jnp.concatenate(out, axis=1) if len(out) > 1 else out[0]


def _rope_consts():
    half = ROPE_ROT // 2
    c = np.float32(-2.0 * math.log(ROPE_THETA) / ROPE_ROT)
    freqs = np.exp(np.arange(half, dtype=np.float32) * c).astype(np.float32)
    place = np.zeros((3, half, 128), np.float32)
    ones = np.zeros((1, 128), np.float32)
    for lane in range(128):
        d = lane % HEAD
        if d < half:
            place[0, d, lane], place[1, d, lane] = 1.0, -1.0
        elif d < ROPE_ROT:
            place[0, d - half, lane], place[2, d - half, lane] = 1.0, 1.0
        else:
            ones[0, lane] = 1.0
    return np.tile(freqs[:, None], (1, 128)), place, ones


def _rope_tables(pos_rows):
    r = pos_rows.shape[0]
    tr = min(1024, r)
    freqs, place, ones = _rope_consts()

    def split3(v):
        hi, mid = _split_hi_lo(v)
        lo = (v - hi.astype(F32) - mid.astype(F32)).astype(_MM)
        return hi, mid, lo

    def body(p_ref, f_ref, e_ref, one_ref, c_ref, sa_ref, sb_ref):
        ang = jnp.concatenate([p_ref[j:j + 1, :].astype(F32) * f_ref[...] for j in range(tr // 128)], axis=1)
        cos, sin = jnp.cos(ang), jnp.sin(ang)
        for ref, k, v in ((c_ref, 0, cos), (sa_ref, 1, sin), (sb_ref, 2, sin)):
            e = e_ref[k].astype(_MM)
            out = sum(_dot_tn(part, e) for part in split3(v))
            ref[...] = out + one_ref[...] if k == 0 else out

    return _pc(body, name="rope_tables", grid=(r // tr,),
               in_specs=[pl.BlockSpec((tr // 128, 128), lambda i: (i, 0)), _acc((ROPE_ROT // 2, 128)),
                         _acc((3, ROPE_ROT // 2, 128)), _acc((1, 128))],
               out_specs=[_row(tr, 128)] * 3, out_shape=[_sds((r, 128))] * 3)(
                   pos_rows.reshape(r // 128, 128), jnp.asarray(freqs), jnp.asarray(place), jnp.asarray(ones))


def _k_in(x, g1, w_in, gq_b, gk_b, tabs, tab_row):
    s = x.shape[0]
    ts = min(512, s)
    nin = w_in.shape[2]
    ncol = CHIPS * nin
    a_cols = 3 * A_W
    offs = [0, a_cols, 2 * a_cols, 3 * a_cols, 3 * a_cols + B_QH * HEAD,
            3 * a_cols + (B_QH + B_KVH) * HEAD, 3 * a_cols + (B_QH + 2 * B_KVH) * HEAD, ncol]

    def body(x_ref, g_ref, wi_ref, gq_ref, gk_ref, c_ref, sa_ref, sb_ref, h_ref, a0, a1, a2, qb, kb, vb, mq,
             qn_ref, kn_ref, vn_ref, p_scr):
        xh, _ = _rms(x_ref[...])
        h = (xh * g_ref[...]).astype(_MM)
        h_ref[...] = h
        for j in range(CHIPS):
            p_scr[:, j * nin:(j + 1) * nin] = jnp.dot(h, wi_ref[j], preferred_element_type=F32)
        for k, ref in enumerate((a0, a1, a2, qb, kb, vb, mq)):
            ref[...] = p_scr[:, offs[k]:offs[k + 1]]
        c, sa, sb = c_ref[...], sa_ref[...], sb_ref[...]
        qh, _ = _seg_norm(p_scr[:, offs[3]:offs[4]], HEAD)
        qn_ref[...] = _rope(qh * gq_ref[...], c, sa, sb).astype(_MM)
        kh, _ = _seg_norm(p_scr[:, offs[4]:offs[5]], HEAD)
        kn_ref[...] = _rope(kh * gk_ref[...], c, sa, sb).astype(_MM)
        vn_ref[...] = p_scr[:, offs[5]:offs[6]].astype(_MM)

    widths = [offs[k + 1] - offs[k] for k in range(7)]
    tab = pl.BlockSpec((ts, 128), lambda i: (i + tab_row // ts, 0))
    return _pc(
        body, name="in_proj", grid=(s // ts,),
        in_specs=[_row(ts, D_MODEL), _res((1, D_MODEL)), _res(w_in.shape), _res((1, widths[3])), _res((1, widths[4])),
                  tab, tab, tab],
        out_specs=[_row(ts, D_MODEL)] + [_row(ts, w) for w in widths] + [_row(ts, widths[3]), _row(ts, widths[4]),
                                                                       _row(ts, widths[5])],
        out_shape=[_sds((s, D_MODEL), _MM)] + [_sds((s, w)) for w in widths]
        + [_sds((s, widths[3]), _MM), _sds((s, widths[4]), _MM), _sds((s, widths[5]), _MM)],
        scratch=[pltpu.VMEM((ts, ncol), F32)])(x, g1, w_in, gq_b[0], gk_b[0], *tabs)


def _k_prep(srcs, gq, gk, tabs, tab_row, *, wq, wk, rows_per_gain, name):
    rows = srcs[0][0].shape[0]
    ts = min(512, rows)

    def body(q_ref, k_ref, v_ref, gq_ref, gk_ref, c_ref, sa_ref, sb_ref, qn_ref, kn_ref, vn_ref):
        c, sa, sb = c_ref[...], sa_ref[...], sb_ref[...]
        qh, _ = _seg_norm(q_ref[...], HEAD)
        qn_ref[...] = _rope(qh * gq_ref[...], c, sa, sb).astype(_MM)
        kh, _ = _seg_norm(k_ref[...], HEAD)
        kn_ref[...] = _rope(kh * gk_ref[...], c, sa, sb).astype(_MM)
        vn_ref[...] = v_ref[...].astype(_MM)

    gspec = lambda w: pl.BlockSpec((None, 1, w), lambda i: ((i * ts) // rows_per_gain, 0, 0))
    return _pc(
        body, name=name, grid=(rows // ts,),
        in_specs=[_row(ts, wq, srcs[0][1]), _row(ts, wk, srcs[1][1]), _row(ts, wk, srcs[2][1]),
                  gspec(wq), gspec(wk)] + [pl.BlockSpec((ts, 128), lambda i: (i + tab_row // ts, 0))] * 3,
        out_specs=[_row(ts, wq), _row(ts, wk), _row(ts, wk)],
        out_shape=[_sds((rows, wq), _MM), _sds((rows, wk), _MM), _sds((rows, wk), _MM)])(
            srcs[0][0], srcs[1][0], srcs[2][0], gq, gk, *tabs)


def _first_flag(b, segs, nb):
    first = b >= nb
    for k, (start, period) in enumerate(segs):
        end = segs[k + 1][0] if k + 1 < len(segs) else nb
        first = first | ((b >= start) & (b < end) & (lax.rem(b - start, jnp.int32(period)) == 0))
    return first


def _band_bias(thr, with_cur):
    qi = lax.broadcasted_iota(jnp.int32, (BLK, BLK), 0)
    kj = lax.broadcasted_iota(jnp.int32, (BLK, BLK), 1)
    prev = jnp.where(kj >= qi + thr, 0.0, NEG)
    return jnp.concatenate([prev, jnp.where(kj <= qi, 0.0, NEG)], axis=1) if with_cur else prev


def _blockdiag(t4):
    head = _lane_head((1, A_W))
    return jnp.concatenate([t4 * jnp.where(head == h, 1.0, 0.0).astype(t4.dtype) for h in range(A_HEADS)], axis=0)


def _fold_diag(t, n):
    head = _lane_head((n, A_W))
    out = t[3 * n:4 * n]
    for h in (2, 1, 0):
        out = jnp.where(head == h, t[h * n:(h + 1) * n], out)
    return out


def _expand_heads(cols):
    n = cols[0].shape[0]
    head = _lane_head((n, A_W))
    out = jnp.broadcast_to(cols[3], (n, A_W))
    for h in (2, 1, 0):
        out = jnp.where(head == h, cols[h], out)
    return out


def _unit_kv(pieces, u, shared):
    cols = slice(u * HEAD, (u + 1) * HEAD) if shared else slice(u * A_W, (u + 1) * A_W)
    rows = [ref[rs, cols] for ref, rs in pieces]
    k = rows[0] if len(rows) == 1 else jnp.concatenate(rows, axis=0)
    return jnp.concatenate([k] * A_HEADS, axis=1) if shared else k


_LO, _HI, _BOTH = slice(0, BLK), slice(BLK, 2 * BLK), slice(0, 2 * BLK)


def _k_band_fwd(qn, kn, vn, *, hq, hk, max_dist, segs, sink, name):
    rows = qn.shape[0]
    nb = rows // BLK
    units = hq // A_HEADS
    shared = hk != hq
    wq, wk = hq * HEAD, hk * HEAD
    scale = HEAD ** -0.5

    def body(*refs):
        if sink is None:
            q_ref, kc_ref, kp_ref, vc_ref, vp_ref, o_ref, l_ref = refs
        else:
            q_ref, kc_ref, kp_ref, vc_ref, vp_ref, sk_ref, o_ref, l_ref = refs
        i = pl.program_id(0)
        for half, rs in enumerate((_LO, _HI)):
            bias = _band_bias(jnp.where(_first_flag(2 * i + half, segs, nb), 1 << 20, BLK - max_dist), True)
            kpieces = ((kp_ref, _LO), (kc_ref, _LO)) if half == 0 else ((kc_ref, _BOTH),)
            vpieces = ((vp_ref, _LO), (vc_ref, _LO)) if half == 0 else ((vc_ref, _BOTH),)
            for u in range(units):
                us = slice(u * A_W, (u + 1) * A_W)
                kb = _blockdiag(_unit_kv(kpieces, u, shared))
                vb = _blockdiag(_unit_kv(vpieces, u, shared))
                s_all = _dot_nt(q_ref[rs, us], kb) * scale
                ps, ls = [], []
                for h in range(A_HEADS):
                    s = s_all[:, h * 2 * BLK:(h + 1) * 2 * BLK] + bias
                    m = jnp.max(s, axis=-1, keepdims=True)
                    e = jnp.exp(s - m)
                    lse = m + jnp.log(jnp.sum(e, axis=-1, keepdims=True))
                    if sink is not None:
                        sk = sk_ref[u * A_HEADS + h]
                        mx = jnp.maximum(lse, sk)
                        lse = mx + jnp.log(jnp.exp(lse - mx) + jnp.exp(sk - mx))
                    ps.append((e * jnp.exp(m - lse)).astype(_MM))
                    ls.append(lse)
                o_ref[rs, us] = _dot(jnp.concatenate(ps, axis=1), vb)
                l_ref[rs, us] = _expand_heads(ls)

    two = lambda w: pl.BlockSpec((2 * BLK, w), lambda i: (i, 0))
    prev = lambda w: pl.BlockSpec((BLK, w), lambda i: (jnp.maximum(2 * i - 1, 0), 0))
    in_specs = [two(wq), two(wk), prev(wk), two(wk), prev(wk)]
    args = [qn, kn, kn, vn, vn]
    if sink is not None:
        in_specs.append(pl.BlockSpec(memory_space=pltpu.SMEM))
        args.append(sink)
    return _pc(body, name=name, grid=(nb // 2,), in_specs=in_specs, out_specs=[two(wq), two(wq)],
               out_shape=[_sds((rows, wq)), _sds((rows, wq))])(*args)


def _k_memkv(mem, mem_norm, w_kv, m_k_norm):
    n = mem.shape[0]

    def body(m_ref, g_ref, w_ref, gk_ref, mn_ref, kv_ref, mk_ref, mv_ref):
        mh, _ = _rms(m_ref[...])
        mn = (mh * g_ref[...]).astype(_MM)
        mn_ref[...] = mn
        kv = jnp.dot(mn, w_ref[...], preferred_element_type=F32)
        kv_ref[...] = kv
        kh, _ = _seg_norm(kv[:, :M_W], M_HD)
        mk_ref[...] = (kh * gk_ref[...]).astype(_MM)
        mv_ref[...] = kv[:, M_W:].astype(_MM)

    return _pc(body, name="mem_kv", grid=(1,),
               in_specs=[_acc((n, D_MODEL)), _acc((1, D_MODEL)), _acc(w_kv.shape), _acc((1, M_W))],
               out_specs=[_acc((n, D_MODEL)), _acc((n, 2 * M_W)), _acc((n, M_W)), _acc((n, M_W))],
               out_shape=[_sds((n, D_MODEL), _MM), _sds((n, 2 * M_W)), _sds((n, M_W), _MM), _sds((n, M_W), _MM)])(
                   mem, mem_norm, w_kv, m_k_norm)


def _mem_probs(q, mk):
    sc = _dot_nt(q, mk) * (M_HD ** -0.5)
    e = jnp.exp(sc - jnp.max(sc, axis=-1, keepdims=True))
    return e / jnp.sum(e, axis=-1, keepdims=True)


def _k_mem_fwd(m_q, gq, mk, mv):
    s = m_q.shape[0]
    n = mk.shape[0]
    ts = min(512, s)

    def body(q_ref, g_ref, mk_ref, mv_ref, o_ref):
        qh, _ = _seg_norm(q_ref[...], M_HD)
        qn = (qh * g_ref[...]).astype(_MM)
        for h in range(M_HEADS):
            hs = slice(h * M_HD, (h + 1) * M_HD)
            o_ref[:, hs] = _dot(_mem_probs(qn[:, hs], mk_ref[:, hs]), mv_ref[:, hs])

    return _pc(body, name="mem_attn", grid=(s // ts,),
               in_specs=[_row(ts, M_W), _res((1, M_W)), _res((n, M_W)), _res((n, M_W))],
               out_specs=[_row(ts, M_W)], out_shape=[_sds((s, M_W))])(m_q, gq, mk, mv)[0]


def _group_weights(l0, l1, l2):
    m = jnp.maximum(jnp.maximum(l0, l1), l2)
    e0, e1, e2 = jnp.exp(l0 - m), jnp.exp(l1 - m), jnp.exp(l2 - m)
    inv = 1.0 / (e0 + e1 + e2)
    return e0 * inv, e1 * inv, e2 * inv


def _branch_products(oa, ob, om, woa_ref, wob_ref, wom_ref, j):
    return _dot(oa, woa_ref[j]), _dot(ob, wob_ref[j]), _dot(om, wom_ref[j])


def _k_merge(og, lg, o_b, o_m, h, x, w_gate, b_gate, w_oa, w_ob, w_om, w_out, g2):
    s = x.shape[0]
    ts = min(512, s)
    nc = w_oa.shape[2]
    ng = w_gate.shape[2]

    def body(o0, o1, o2, l0, l1, l2, ob_ref, om_ref, h_ref, x_ref, wg, bg_ref, woa, wob, wom, wout, g_ref,
             oa_ref, mer_ref, x1_ref, h2_ref, gt_ref, m_scr):
        h = h_ref[...]
        for j in range(CHIPS):
            z = jnp.dot(h, wg[j], preferred_element_type=F32) + bg_ref[:, j * ng:(j + 1) * ng]
            gt_ref[:, j * ng:(j + 1) * ng] = _sigmoid(z)
        w0, w1, w2 = _group_weights(l0[...], l1[...], l2[...])
        oa = w0 * o0[...] + w1 * o1[...] + w2 * o2[...]
        oa_ref[...] = oa
        ob, om = ob_ref[...], om_ref[...]
        for j in range(CHIPS):
            pa, pb, pm = _branch_products(oa, ob, om, woa, wob, wom, j)
            cs = lambda br: slice(br * D_MODEL + j * nc, br * D_MODEL + (j + 1) * nc)
            m_scr[:, j * nc:(j + 1) * nc] = gt_ref[:, cs(0)] * pa + gt_ref[:, cs(1)] * pb + gt_ref[:, cs(2)] * pm
        mer = m_scr[...].astype(_MM)
        mer_ref[...] = mer
        x1 = x_ref[...] + jnp.dot(mer, wout[...], preferred_element_type=F32)
        x1_ref[...] = x1
        xh, _ = _rms(x1)
        h2_ref[...] = (xh * g_ref[...]).astype(_MM)

    return _pc(
        body, name="merge_out", grid=(s // ts,),
        in_specs=[_row(ts, A_W)] * 6 + [_row(ts, B_QH * HEAD), _row(ts, M_W), _row(ts, D_MODEL), _row(ts, D_MODEL),
                                         _res(w_gate.shape), _res(b_gate.shape), _res(w_oa.shape), _res(w_ob.shape),
                                         _res(w_om.shape), _res(w_out.shape), _res((1, D_MODEL))],
        out_specs=[_row(ts, A_W), _row(ts, D_MODEL), _row(ts, D_MODEL), _row(ts, D_MODEL), _row(ts, CHIPS * ng)],
        out_shape=[_sds((s, A_W)), _sds((s, D_MODEL), _MM), _sds((s, D_MODEL)), _sds((s, D_MODEL), _MM),
                   _sds((s, CHIPS * ng))],
        scratch=[pltpu.VMEM((ts, D_MODEL), F32)])(
            *og, *lg, o_b, o_m, h, x, w_gate, b_gate, w_oa, w_ob, w_om, w_out, g2)


def _k_up(h2, w_up):
    s = h2.shape[0]
    ts = min(256, s)
    nu = w_up.shape[2]

    def body(h_ref, w_ref, u_ref):
        h = h_ref[...]
        for j in range(CHIPS):
            u_ref[:, j * nu:(j + 1) * nu] = jnp.dot(h, w_ref[j], preferred_element_type=F32)

    return _pc(body, name="up_proj", grid=(s // ts,), in_specs=[_row(ts, D_MODEL), _res(w_up.shape)],
               out_specs=[_row(ts, CHIPS * nu)], out_shape=[_sds((s, CHIPS * nu))])(h2, w_up)[0]


def _shift_down(v, halo, k):
    rolled = pltpu.roll(v, k, 0)
    row = lax.broadcasted_iota(jnp.int32, (8, v.shape[1]), 0)
    slab = rolled[0:8]
    for r in range(k):
        slab = jnp.where(row == r, halo[8 - k + r:8 - k + r + 1, :], slab)
    return jnp.concatenate([slab, rolled[8:]], axis=0)


def _shift_up(v, halo, k):
    ts = v.shape[0]
    rolled = pltpu.roll(v, ts - k, 0)
    row = lax.broadcasted_iota(jnp.int32, (8, v.shape[1]), 0)
    slab = rolled[ts - 8:]
    for r in range(k):
        slab = jnp.where(row == 8 - k + r, halo[r:r + 1, :], slab)
    return jnp.concatenate([rolled[:ts - 8], slab], axis=0)


def _k_ffn(u, conv_w, conv_b, w_down, w_down_t, x1, target):
    s = u.shape[0]
    ts = min(256, s)
    nu = conv_w.shape[2]
    half = CHIPS // 2

    def body(u_ref, uh_ref, cw_ref, cb_ref, wd_ref, wdt_ref, x1_ref, t_ref, dy_ref, f_ref, dc_ref, loss_ref, c_scr,
             f_scr, s_scr):
        i = pl.program_id(0)
        halo = jnp.where(i > 0, uh_ref[...], 0.0)
        for j in range(CHIPS):
            cs = slice(j * nu, (j + 1) * nu)
            uj = u_ref[:, cs]
            hj = halo[:, cs]
            c_scr[:, cs] = (cb_ref[:, cs] + cw_ref[j, 0:1, :] * _shift_down(uj, hj, 2)
                            + cw_ref[j, 1:2, :] * _shift_down(uj, hj, 1) + cw_ref[j, 2:3, :] * uj)
        for j in range(half):
            a = c_scr[:, j * nu:(j + 1) * nu]
            g = c_scr[:, (half + j) * nu:(half + j + 1) * nu]
            sa = _sigmoid(a)
            s_scr[:, j * nu:(j + 1) * nu] = sa
            f_scr[:, j * nu:(j + 1) * nu] = (a * sa * g).astype(_MM)
        f = f_scr[...]
        f_ref[...] = f
        y = x1_ref[...] + jnp.dot(f, wd_ref[...], preferred_element_type=F32)
        err = y - t_ref[...]
        dy = err * (1.0 / D_MODEL)
        dy_ref[...] = dy

        @pl.when(i == 0)
        def _():
            loss_ref[...] = jnp.zeros_like(loss_ref)

        loss_ref[...] += _sum8(err * err)
        df = _dot(dy, wdt_ref[...])
        for j in range(half):
            a = c_scr[:, j * nu:(j + 1) * nu]
            g = c_scr[:, (half + j) * nu:(half + j + 1) * nu]
            sa = s_scr[:, j * nu:(j + 1) * nu]
            dfj = df[:, j * nu:(j + 1) * nu]
            dc_ref[:, j * nu:(j + 1) * nu] = dfj * g * (sa * (1.0 + a * (1.0 - sa)))
            dc_ref[:, (half + j) * nu:(half + j + 1) * nu] = dfj * (a * sa)

    wide = CHIPS * nu
    return _pc(
        body, name="conv_ffn", grid=(s // ts,),
        in_specs=[_row(ts, wide), pl.BlockSpec((8, wide), lambda i: (jnp.maximum(i * (ts // 8) - 1, 0), 0)),
                  _res(conv_w.shape), _res((1, wide)), _res(w_down.shape), _res(w_down_t.shape), _row(ts, D_MODEL),
                  _row(ts, D_MODEL)],
        out_specs=[_row(ts, D_MODEL), _row(ts, D_FF), _row(ts, wide), _acc((8, D_MODEL))],
        out_shape=[_sds((s, D_MODEL)), _sds((s, D_FF), _MM), _sds((s, wide)), _sds((8, D_MODEL))],
        scratch=[pltpu.VMEM((ts, wide), F32), pltpu.VMEM((ts, D_FF), _MM), pltpu.VMEM((ts, D_FF), F32)])(
            u, u, conv_w, conv_b, w_down, w_down_t, x1, target)


def _k_conv_bwd(dc, u, conv_w, w_up, x1, g2, dy):
    s = u.shape[0]
    ts = min(256, s)
    nu = conv_w.shape[2]
    wide = CHIPS * nu
    last = s // ts - 1

    def body(dc_ref, dn_ref, u_ref, cw_ref, wu_ref, x1_ref, g_ref, dy_ref, dx1_ref, du_ref, cacc_ref, gacc_ref):
        i = pl.program_id(0)

        @pl.when(i == 0)
        def _():
            cacc_ref[...] = jnp.zeros_like(cacc_ref)
            gacc_ref[...] = jnp.zeros_like(gacc_ref)

        dhalo = jnp.where(i < last, dn_ref[...], 0.0)
        dh2 = jnp.zeros((ts, D_MODEL), F32)
        for j in range(CHIPS):
            cs = slice(j * nu, (j + 1) * nu)
            dcj, uj = dc_ref[:, cs], u_ref[:, cs]
            dc1, dc2 = _shift_up(dcj, dhalo[:, cs], 1), _shift_up(dcj, dhalo[:, cs], 2)
            cacc_ref[0, :, cs] += _sum8(dcj)
            cacc_ref[1, :, cs] += _sum8(dc2 * uj)
            cacc_ref[2, :, cs] += _sum8(dc1 * uj)
            cacc_ref[3, :, cs] += _sum8(dcj * uj)
            du = (cw_ref[j, 2:3, :] * dcj + cw_ref[j, 1:2, :] * dc1 + cw_ref[j, 0:1, :] * dc2).astype(_MM)
            du_ref[:, cs] = du
            dh2 = dh2 + _dot_nt(du, wu_ref[j])
        xh, r = _rms(x1_ref[...])
        gacc_ref[...] += _sum8(dh2 * xh)
        dx1_ref[...] = dy_ref[...] + _rms_bwd(dh2, xh, r, g_ref[...])

    return _pc(
        body, name="conv_up_bwd", grid=(s // ts,),
        in_specs=[_row(ts, wide),
                  pl.BlockSpec((8, wide), lambda i: (jnp.minimum((i + 1) * (ts // 8), s // 8 - 1), 0)),
                  _row(ts, wide), _res(conv_w.shape), _res(w_up.shape), _row(ts, D_MODEL), _res((1, D_MODEL)),
                  _row(ts, D_MODEL)],
        out_specs=[_row(ts, D_MODEL), _row(ts, wide), _acc((4, 8, wide)), _acc((8, D_MODEL))],
        out_shape=[_sds((s, D_MODEL)), _sds((s, wide), _MM), _sds((4, 8, wide)), _sds((8, D_MODEL))])(
            dc, dc, u, conv_w, w_up, x1, g2, dy)


def _k_merge_bwd(dx1, og, lg, o_a, o_b, o_m, gates, merged, h, w_gate_shape, w_oa, w_ob, w_om, w_out, dep):
    s = dx1.shape[0]
    ts = min(256, s)
    nc = w_oa.shape[2]
    ng = w_gate_shape[2]

    def body(dx_ref, o0, o1, o2, l0, l1, l2, oa_ref, ob_ref, om_ref, gt_ref, mer_ref, h_ref, woa, wob, wom, wout, dep_ref,
             dgp_ref, dog0, dog1, dog2, dl0, dl1, dl2, dob_ref, dom_ref, bacc_ref, dwa_ref, dwb_ref, dwm_ref, dwo_ref,
             dwg_ref):
        i = pl.program_id(0)

        @pl.when(i == 0)
        def _():
            for ref in (bacc_ref, dwa_ref, dwb_ref, dwm_ref, dwo_ref, dwg_ref):
                ref[...] = jnp.zeros_like(ref)

        dx = dx_ref[...]
        h = h_ref[...]
        dwo_ref[...] += _dot_tn(mer_ref[...], dx)
        dmer = _dot_nt(dx, wout[...])
        oa, ob, om = oa_ref[...], ob_ref[...], om_ref[...]
        doa = jnp.zeros((ts, A_W), F32)
        dob = jnp.zeros((ts, B_QH * HEAD), F32)
        dom = jnp.zeros((ts, M_W), F32)
        for j in range(CHIPS):
            prods = _branch_products(oa, ob, om, woa, wob, wom, j)
            dmj = dmer[:, j * nc:(j + 1) * nc]
            dps = []
            for br, (p, o, dw_ref) in enumerate(zip(prods, (oa, ob, om), (dwa_ref, dwb_ref, dwm_ref))):
                cs = slice(br * D_MODEL + j * nc, br * D_MODEL + (j + 1) * nc)
                gt = gt_ref[:, cs]
                dgp = dmj * p * gt * (1.0 - gt)
                dgp_ref[:, cs] = dgp.astype(_MM)
                bacc_ref[:, cs] += _sum8(dgp)
                blk, off = divmod(cs.start, ng)
                dwg_ref[blk, :, off:off + nc] += _dot_tn(h, dgp)
                dp = (dmj * gt).astype(_MM)
                dw_ref[j] += _dot_tn(o, dp)
                dps.append(dp)
            doa = doa + _dot_nt(dps[0], woa[j])
            dob = dob + _dot_nt(dps[1], wob[j])
            dom = dom + _dot_nt(dps[2], wom[j])
        dob_ref[...] = dob
        dom_ref[...] = dom
        ws = _group_weights(l0[...], l1[...], l2[...])
        dsum = _seg_mean(doa * oa, HEAD) * float(HEAD)
        for w, dref, lref in zip(ws, (dog0, dog1, dog2), (dl0, dl1, dl2)):
            dref[...] = w * doa
            lref[...] = w * dsum

    return _pc(
        body, name="merge_out_bwd", grid=(s // ts,),
        in_specs=[_row(ts, D_MODEL)] + [_row(ts, A_W)] * 7 + [_row(ts, B_QH * HEAD), _row(ts, M_W), _row(ts, 3 * D_MODEL),
                                                              _row(ts, D_MODEL), _row(ts, D_MODEL), _res(w_oa.shape),
                                                              _res(w_ob.shape), _res(w_om.shape), _res(w_out.shape),
                                                              _res((8, 128))],
        out_specs=[_row(ts, 3 * D_MODEL)] + [_row(ts, A_W)] * 6
        + [_row(ts, B_QH * HEAD), _row(ts, M_W), _acc((8, 3 * D_MODEL)), _acc(w_oa.shape), _acc(w_ob.shape),
           _acc(w_om.shape), _acc(w_out.shape), _acc(w_gate_shape)],
        out_shape=[_sds((s, 3 * D_MODEL), _MM)] + [_sds((s, A_W))] * 6
        + [_sds((s, B_QH * HEAD)), _sds((s, M_W)), _sds((8, 3 * D_MODEL)), _sds(w_oa.shape), _sds(w_ob.shape),
           _sds(w_om.shape), _sds(w_out.shape), _sds(w_gate_shape)])(
            dx1, *og, *lg, o_a, o_b, o_m, gates, merged, h, w_oa, w_ob, w_om, w_out, dep)


def _k_mem_bwd(m_q, gq, mk, mv, o_m, do_m):
    s = m_q.shape[0]
    n = mk.shape[0]
    ts = min(512, s)
    scale = M_HD ** -0.5

    def body(q_ref, g_ref, mk_ref, mv_ref, o_ref, do_ref, dq_ref, dmk_ref, dmv_ref, gacc_ref):
        i = pl.program_id(0)

        @pl.when(i == 0)
        def _():
            dmk_ref[...] = jnp.zeros_like(dmk_ref)
            dmv_ref[...] = jnp.zeros_like(dmv_ref)
            gacc_ref[...] = jnp.zeros_like(gacc_ref)

        gain = g_ref[...]
        qh, r = _seg_norm(q_ref[...], M_HD)
        qn = (qh * gain).astype(_MM)
        do = do_ref[...]
        delta = _seg_mean(do * o_ref[...], M_HD) * float(M_HD)
        dqn = []
        for h in range(M_HEADS):
            hs = slice(h * M_HD, (h + 1) * M_HD)
            p = _mem_probs(qn[:, hs], mk_ref[:, hs])
            dp = _dot_nt(do[:, hs], mv_ref[:, hs])
            ds = (p * (dp - delta[:, hs][:, 0:1]) * scale).astype(_MM)
            dqn.append(_dot(ds, mk_ref[:, hs]))
            dmk_ref[:, hs] += _dot_tn(ds, qn[:, hs])
            dmv_ref[:, hs] += _dot_tn(p, do[:, hs])
        dqn = jnp.concatenate(dqn, axis=1)
        gacc_ref[...] += _sum8(dqn * qh)
        z = dqn * gain
        dq_ref[...] = (r * (z - qh * _seg_mean(z * qh, M_HD))).astype(_MM)

    return _pc(
        body, name="mem_attn_bwd", grid=(s // ts,),
        in_specs=[_row(ts, M_W), _res((1, M_W)), _res((n, M_W)), _res((n, M_W)), _row(ts, M_W), _row(ts, M_W)],
        out_specs=[_row(ts, M_W), _acc((n, M_W)), _acc((n, M_W)), _acc((8, M_W))],
        out_shape=[_sds((s, M_W), _MM), _sds((n, M_W)), _sds((n, M_W)), _sds((8, M_W))])(m_q, gq, mk, mv, o_m, do_m)


def _k_memkv_bwd(mem, mem_norm, w_kv, m_k_norm, mem_n, kv, dmk, dmv):
    n = mem.shape[0]

    def body(m_ref, g_ref, w_ref, gk_ref, mn_ref, kv_ref, dmk_ref, dmv_ref, dw_ref, dg_ref, dgk_ref):
        gk = gk_ref[...]
        kh, r = _seg_norm(kv_ref[:, :M_W], M_HD)
        dmk = dmk_ref[...]
        dgk_ref[...] = _sum8(dmk * kh)
        z = dmk * gk
        dk = r * (z - kh * _seg_mean(z * kh, M_HD))
        dkv = jnp.concatenate([dk, dmv_ref[...]], axis=1).astype(_MM)
        dw_ref[...] = _dot_tn(mn_ref[...], dkv)
        dmn = _dot_nt(dkv, w_ref[...])
        mh, _ = _rms(m_ref[...])
        dg_ref[...] = _sum8(dmn * mh)

    return _pc(body, name="mem_kv_bwd", grid=(1,),
               in_specs=[_acc((n, D_MODEL)), _acc((1, D_MODEL)), _acc(w_kv.shape), _acc((1, M_W)), _acc((n, D_MODEL)),
                         _acc((n, 2 * M_W)), _acc((n, M_W)), _acc((n, M_W))],
               out_specs=[_acc(w_kv.shape), _acc((8, D_MODEL)), _acc((8, M_W))],
               out_shape=[_sds(w_kv.shape), _sds((8, D_MODEL)), _sds((8, M_W))])(
                   mem, mem_norm, w_kv, m_k_norm, mem_n, kv, dmk, dmv)


def _k_band_bwd(qn, kn, vn, do, lse, dl_or_o, *, hq, hk, max_dist, segs, sink, name):
    rows = qn.shape[0]
    nb = rows // BLK
    units = hq // A_HEADS
    shared = hk != hq
    wq, wk = hq * HEAD, hk * HEAD
    scale = HEAD ** -0.5

    def body(*refs):
        (q2_ref, qx_ref, kc_ref, kp_ref, vc_ref, vp_ref, do2_ref, dox_ref, l2_ref, lx_ref, e2_ref, ex_ref) = refs[:12]
        if sink is None:
            dq_ref, dk_ref, dv_ref = refs[12:]
        else:
            sk_ref, dq_ref, dk_ref, dv_ref, sacc_ref = refs[12:]
        i = pl.program_id(0)
        thr = lambda b: jnp.where(_first_flag(b, segs, nb), 1 << 20, BLK - max_dist)
        bias_a, bias_b = _band_bias(thr(2 * i), True), _band_bias(thr(2 * i + 1), True)
        bias_c = _band_bias(thr(2 * i + 2), False)
        if sink is not None:
            @pl.when(i == 0)
            def _():
                sacc_ref[...] = jnp.zeros_like(sacc_ref)

        def tile(q4, do4, l_cols, dlt, kd, vd, bias, width):
            s, dp = _dot_nt(q4, kd) * scale, _dot_nt(do4, vd)
            ps, dss = [], []
            for h in range(A_HEADS):
                seg = slice(h * width, (h + 1) * width)
                p = jnp.exp(s[:, seg] + bias - l_cols[h])
                ps.append(p)
                dss.append(p * (dp[:, seg] - dlt[:, h * HEAD:h * HEAD + 1]) * scale)
            return ps, dss

        cat = lambda parts: jnp.concatenate([t.astype(_MM) for t in parts], axis=1)
        for u in range(units):
            us = slice(u * A_W, (u + 1) * A_W)
            k_a = _unit_kv(((kp_ref, _LO), (kc_ref, _LO)), u, shared)
            v_a = _unit_kv(((vp_ref, _LO), (vc_ref, _LO)), u, shared)
            k_b, v_b = _unit_kv(((kc_ref, _BOTH),), u, shared), _unit_kv(((vc_ref, _BOTH),), u, shared)
            kd_a, vd_a, kd_b, vd_b = _blockdiag(k_a), _blockdiag(v_a), _blockdiag(k_b), _blockdiag(v_b)
            kd_c, vd_c = _blockdiag(k_b[BLK:]), _blockdiag(v_b[BLK:])
            qs = (q2_ref[_LO, us], q2_ref[_HI, us], qx_ref[:, us])
            dos = (do2_ref[_LO, us], do2_ref[_HI, us], dox_ref[:, us])
            lcols = [[ref[rs, u * A_W + h * HEAD:u * A_W + h * HEAD + 1] for h in range(A_HEADS)]
                     for ref, rs in ((l2_ref, _LO), (l2_ref, _HI), (lx_ref, _LO))]
            if sink is None:
                dlts = (e2_ref[_LO, us], e2_ref[_HI, us], ex_ref[:, us])
            else:
                dlts = tuple(_seg_sum64(d.astype(F32) * ref[rs, us])
                             for d, (ref, rs) in zip(dos, ((e2_ref, _LO), (e2_ref, _HI), (ex_ref, _LO))))
                for t in range(2):
                    for h in range(A_HEADS):
                        j = u * A_HEADS + h
                        sacc_ref[:, j:j + 1] += -jnp.exp(sk_ref[j] - lcols[t][h]) * dlts[t][:, h * HEAD:h * HEAD + 1]
            p_a, ds_a = tile(qs[0], dos[0], lcols[0], dlts[0], kd_a, vd_a, bias_a, 2 * BLK)
            p_b, ds_b = tile(qs[1], dos[1], lcols[1], dlts[1], kd_b, vd_b, bias_b, 2 * BLK)
            p_c, ds_c = tile(qs[2], dos[2], lcols[2], dlts[2], kd_c, vd_c, bias_c, BLK)
            dq_ref[_LO, us] = _dot(cat(ds_a), kd_a)
            dq_ref[_HI, us] = _dot(cat(ds_b), kd_b)
            outs = []
            for pa, pb, pc, lhs in ((ds_a, ds_b, ds_c, qs), (p_a, p_b, p_c, dos)):
                from_a = _fold_diag(_dot_tn(cat([t[:, BLK:] for t in pa]), lhs[0]), BLK)
                from_b = _fold_diag(_dot_tn(cat(pb), lhs[1]), 2 * BLK)
                from_c = _fold_diag(_dot_tn(cat(pc), lhs[2]), BLK)
                outs.append(jnp.concatenate([from_a + from_b[:BLK], from_b[BLK:] + from_c], axis=0))
            dk4, dv4 = outs
            if shared:
                fold = lambda t: (t[:, 0:HEAD] + t[:, HEAD:2 * HEAD]) + (t[:, 2 * HEAD:3 * HEAD] + t[:, 3 * HEAD:])
                dk_ref[:, u * HEAD:(u + 1) * HEAD] = fold(dk4)
                dv_ref[:, u * HEAD:(u + 1) * HEAD] = fold(dv4).astype(_MM)
            else:
                dk_ref[:, us] = dk4
                dv_ref[:, us] = dv4.astype(_MM)

    two = lambda w: pl.BlockSpec((2 * BLK, w), lambda i: (i, 0))
    prev = lambda w: pl.BlockSpec((BLK, w), lambda i: (jnp.maximum(2 * i - 1, 0), 0))
    nxt = lambda w: pl.BlockSpec((BLK, w), lambda i: (jnp.minimum(2 * i + 2, nb - 1), 0))
    in_specs = [two(wq), nxt(wq), two(wk), prev(wk), two(wk), prev(wk), two(wq), nxt(wq), two(wq), nxt(wq), two(wq), nxt(wq)]
    args = [qn, qn, kn, kn, vn, vn, do, do, lse, lse, dl_or_o, dl_or_o]
    out_specs = [two(wq), two(wk), two(wk)]
    out_shape = [_sds((rows, wq)), _sds((rows, wk)), _sds((rows, wk), _MM)]
    if sink is not None:
        in_specs.append(pl.BlockSpec(memory_space=pltpu.SMEM))
        args.append(sink)
        out_specs.append(_acc((BLK, 128)))
        out_shape.append(_sds((BLK, 128)))
    return _pc(body, name=name, grid=(nb // 2,), in_specs=in_specs, out_specs=out_specs, out_shape=out_shape)(*args)


def _k_prep_bwd(srcs, dqn, dkn, gq, gk, tabs, tab_row, *, wq, wk, rows_per_gain, name):
    rows = dqn.shape[0]
    ts = min(512, rows)
    ngain = gq.shape[0]

    def body(q_ref, k_ref, dq_ref, dk_ref, gq_ref, gk_ref, c_ref, sa_ref, sb_ref, oq_ref, ok_ref, aq_ref, ak_ref):
        i = pl.program_id(0)

        @pl.when(lax.rem(i * ts, rows_per_gain) == 0)
        def _():
            aq_ref[...] = jnp.zeros_like(aq_ref)
            ak_ref[...] = jnp.zeros_like(ak_ref)

        c, sa, sb = c_ref[...], sa_ref[...], sb_ref[...]
        for x_ref, d_ref, g_ref, o_ref, a_ref in ((q_ref, dq_ref, gq_ref, oq_ref, aq_ref),
                                                   (k_ref, dk_ref, gk_ref, ok_ref, ak_ref)):
            xh, r = _seg_norm(x_ref[...], HEAD)
            dt = _rope_bwd(d_ref[...], c, sa, sb)
            a_ref[...] += _sum8(dt * xh)
            z = dt * g_ref[...]
            o_ref[...] = (r * (z - xh * _seg_mean(z * xh, HEAD))).astype(_MM)

    gspec = lambda w: pl.BlockSpec((None, 1, w), lambda i: ((i * ts) // rows_per_gain, 0, 0))
    aspec = lambda w: pl.BlockSpec((None, 8, w), lambda i: ((i * ts) // rows_per_gain, 0, 0))
    return _pc(
        body, name=name, grid=(rows // ts,),
        in_specs=[_row(ts, wq, srcs[0][1]), _row(ts, wk, srcs[1][1]), _row(ts, wq), _row(ts, wk), gspec(wq), gspec(wk)]
        + [pl.BlockSpec((ts, 128), lambda i: (i + tab_row // ts, 0))] * 3,
        out_specs=[_row(ts, wq), _row(ts, wk), aspec(wq), aspec(wk)],
        out_shape=[_sds((rows, wq), _MM), _sds((rows, wk), _MM), _sds((ngain, 8, wq)), _sds((ngain, 8, wk))])(
            srcs[0][0], srcs[1][0], dqn, dkn, gq, gk, *tabs)


def _k_in_bwd(pieces, dgp, x, h, g1, dx1, w_in, w_gate):
    s = x.shape[0]
    ts = min(256, s)
    nin, ng = w_in.shape[2], w_gate.shape[2]
    widths = [p.shape[1] for p in pieces]
    ncol = sum(widths)

    def body(*refs):
        p_refs = refs[:len(pieces)]
        dgp_ref, x_ref, h_ref, g_ref, dx1_ref, wi_ref, wg_ref, gx_ref, gacc_ref, dwi_ref, dpj_ref = refs[len(pieces):]
        i = pl.program_id(0)

        @pl.when(i == 0)
        def _():
            gacc_ref[...] = jnp.zeros_like(gacc_ref)
            dwi_ref[...] = jnp.zeros_like(dwi_ref)

        off = 0
        for p_ref, w in zip(p_refs, widths):
            dpj_ref[:, off:off + w] = p_ref[...]
            off += w
        dh = jnp.zeros((ts, D_MODEL), F32)
        h = h_ref[...]
        for j in range(CHIPS):
            dpj = dpj_ref[:, j * nin:(j + 1) * nin]
            dwi_ref[j] += _dot_tn(h, dpj)
            dh = dh + _dot_nt(dpj, wi_ref[j])
            dh = dh + _dot_nt(dgp_ref[:, j * ng:(j + 1) * ng], wg_ref[j])
        xh, r = _rms(x_ref[...])
        gacc_ref[...] += _sum8(dh * xh)
        gx_ref[...] = dx1_ref[...] + _rms_bwd(dh, xh, r, g_ref[...])

    return _pc(
        body, name="in_proj_bwd", grid=(s // ts,),
        in_specs=[_row(ts, w) for w in widths] + [_row(ts, CHIPS * ng), _row(ts, D_MODEL), _row(ts, D_MODEL),
                                                  _res((1, D_MODEL)), _row(ts, D_MODEL), _res(w_in.shape),
                                                  _res(w_gate.shape)],
        out_specs=[_row(ts, D_MODEL), _acc((8, D_MODEL)), _acc(w_in.shape)],
        out_shape=[_sds((s, D_MODEL)), _sds((8, D_MODEL)), _sds(w_in.shape)],
        scratch=[pltpu.VMEM((ts, ncol), _MM)])(*pieces, dgp, x, h, g1, dx1, w_in, w_gate)


def _k_wgrad(a, b, *, nblk, stacked, name):
    s, k = a.shape
    n = b.shape[1]
    nb = n // nblk
    ts = min(2048 if k <= 1024 else 1024, s)

    def body(a_ref, b_ref, o_ref):
        @pl.when(pl.program_id(1) == 0)
        def _():
            o_ref[...] = jnp.zeros_like(o_ref)

        o_ref[...] += _dot_tn(a_ref[...], b_ref[...])

    if stacked:
        out_spec, out_shape = pl.BlockSpec((None, k, nb), lambda g, t: (g, 0, 0)), _sds((nblk, k, nb))
    else:
        out_spec, out_shape = pl.BlockSpec((k, nb), lambda g, t: (0, g)), _sds((k, n))
    return _pc(body, name=name, grid=(nblk, s // ts),
               in_specs=[pl.BlockSpec((ts, k), lambda g, t: (t, 0)), pl.BlockSpec((ts, nb), lambda g, t: (t, g))],
               out_specs=[out_spec], out_shape=[out_shape])(a, b)[0]


def _to_res(t, d):
    s, c = t.shape
    return t if d == 1 else t.reshape(s // d, d, c).transpose(1, 0, 2).reshape(s, c)


def _from_res(t, d):
    s, c = t.shape
    return t if d == 1 else t.reshape(d, s // d, c).transpose(1, 0, 2).reshape(s, c)


def _tile_gain(g, heads):
    return jnp.tile(g, (1,) * (g.ndim - 1) + (heads,))[..., None, :]


def _local_step(x, mem, pos, target, small, get_w_in, get_rest, on_grads):
    s = x.shape[0]
    nblk = s // BLK
    g1, g2 = small["attn_norm"], small["ffn_norm"]

    pos_rows = jnp.concatenate([_to_res(pos[:, None], d)[:, 0] for _, d in A_GROUPS] + [pos])
    tabs = _rope_tables(pos_rows)
    w_in = get_w_in(tabs[0])

    gq_b = _tile_gain(small["b_q_norm"], B_QH)
    gk_b = _tile_gain(small["b_k_norm"], B_KVH)
    h, qa0, qa1, qa2, q_b, k_b, v_b, m_q, qn_b, kn_b, vn_b = _k_in(x, g1, w_in, gq_b, gk_b, tabs, 3 * s)

    qkv_a = jnp.concatenate([_to_res(t, d) for t, (_, d) in zip((qa0, qa1, qa2), A_GROUPS)], axis=0)
    gq_a = _tile_gain(small["a_q_norm"], A_HEADS)
    gk_a = _tile_gain(small["a_k_norm"], A_HEADS)
    src_a = ((qkv_a, 0), (qkv_a, 1), (qkv_a, 2))
    qn_a, kn_a, vn_a = _k_prep(src_a, gq_a, gk_a, tabs, 0, wq=A_W, wk=A_W, rows_per_gain=s, name="prep_a")
    segs_a = tuple((gi * nblk, nblk // d) for gi, (_, d) in enumerate(A_GROUPS))
    o_res, l_res = _k_band_fwd(qn_a, kn_a, vn_a, hq=A_HEADS, hk=A_HEADS, max_dist=BLK, segs=segs_a, sink=None,
                               name="attn_a")
    og = [_from_res(o_res[gi * s:(gi + 1) * s], d) for gi, (_, d) in enumerate(A_GROUPS)]
    lg = [_from_res(l_res[gi * s:(gi + 1) * s], d) for gi, (_, d) in enumerate(A_GROUPS)]

    src_b = ((q_b, 0), (k_b, 0), (v_b, 0))
    sink_x = small["b_sinks"][0]
    segs_b = ((0, nblk),)
    o_b, l_b = _k_band_fwd(qn_b, kn_b, vn_b, hq=B_QH, hk=B_KVH, max_dist=B_WINDOW - 1, segs=segs_b, sink=sink_x,
                           name="attn_b")

    wts = get_rest(0, o_b)

    gq_m = _tile_gain(small["m_q_norm"], M_HEADS)[0]
    gk_m = _tile_gain(small["m_k_norm"], M_HEADS)[0]
    mem_n, kv, mk, mv = _k_memkv(mem, small["mem_norm"], wts["w_mem_kv"], gk_m)
    o_m = _k_mem_fwd(m_q, gq_m, mk, mv)

    o_a, merged, x1, h2, gates = _k_merge(og, lg, o_b, o_m, h, x, wts["w_gate"], small["b_gate"], wts["w_o_a"],
                                          wts["w_o_b"], wts["w_o_m"], wts["w_out"], g2)
    wts.update(get_rest(1, x1))
    u = _k_up(h2, wts["w_up"])
    dy, f, dc, loss_acc = _k_ffn(u, wts["conv_w"], small["conv_b"], wts["w_down"], wts["w_down"].T, x1, target)
    loss = (0.5 / D_MODEL) * jnp.sum(loss_acc)

    dx1, du, cacc, g2acc = _k_conv_bwd(dc, u, wts["conv_w"], wts["w_up"], x1, g2, dy)
    tok = on_grads({"w_up": _k_wgrad(h2, du, nblk=CHIPS, stacked=True, name="dw_up"),
                    "w_down": _k_wgrad(f, dy, nblk=2, stacked=False, name="dw_down").reshape(CHIPS, -1, D_MODEL)}, dx1)
    (dgp, dog0, dog1, dog2, dl0, dl1, dl2, do_b, do_m, bacc, dw_oa, dw_ob, dw_om, dw_out, dw_gate) = _k_merge_bwd(
        dx1, og, lg, o_a, o_b, o_m, gates, merged, h, wts["w_gate"].shape, wts["w_o_a"], wts["w_o_b"], wts["w_o_m"],
        wts["w_out"], tok)
    tok = on_grads({"w_gate": dw_gate,
                    "w_o_a": dw_oa, "w_o_b": dw_ob, "w_o_m": dw_om, "w_out": dw_out.reshape(CHIPS, -1, D_MODEL)}, do_m)

    dq_m, dmk, dmv, gqm_acc = _k_mem_bwd(m_q, gq_m + tok[0:1, 0:1], mk, mv, o_m, do_m)
    dw_kv, gmem_acc, gkm_acc = _k_memkv_bwd(mem, small["mem_norm"], wts["w_mem_kv"], gk_m, mem_n, kv, dmk, dmv)

    dq_bn, dk_bn, dv_b, sacc = _k_band_bwd(qn_b, kn_b, vn_b, do_b, l_b, o_b, hq=B_QH, hk=B_KVH,
                                           max_dist=B_WINDOW - 1, segs=segs_b, sink=sink_x, name="attn_b_bwd")
    tok = on_grads({}, dq_bn)
    dq_b, dk_b, gqb_acc, gkb_acc = _k_prep_bwd(src_b, dq_bn, dk_bn, gq_b + tok[0:1, 0:1], gk_b, tabs, 3 * s, wq=B_QH * HEAD,
                                               wk=B_KVH * HEAD, rows_per_gain=s, name="prep_b_bwd")

    do_res = jnp.concatenate([_to_res(t, d) for t, (_, d) in zip((dog0, dog1, dog2), A_GROUPS)], axis=0)
    dl_res = jnp.concatenate([_to_res(t, d) for t, (_, d) in zip((dl0, dl1, dl2), A_GROUPS)], axis=0)
    dq_an, dk_an, dv_a = _k_band_bwd(qn_a, kn_a, vn_a, do_res, l_res, dl_res, hq=A_HEADS, hk=A_HEADS, max_dist=BLK,
                                     segs=segs_a, sink=None, name="attn_a_bwd")
    dq_a, dk_a, gqa_acc, gka_acc = _k_prep_bwd(src_a, dq_an, dk_an, gq_a, gk_a, tabs, 0, wq=A_W, wk=A_W,
                                               rows_per_gain=s, name="prep_a_bwd")
    pieces = []
    for gi, (_, d) in enumerate(A_GROUPS):
        rs = slice(gi * s, (gi + 1) * s)
        pieces += [_from_res(t[rs], d) for t in (dq_a, dk_a, dv_a)]
    pieces += [dq_b, dk_b, dv_b, dq_m]
    grad_x, g1acc, dw_in = _k_in_bwd(pieces, dgp, x, h, g1, dx1, w_in, wts["w_gate"])
    on_grads({"w_in": dw_in,
              "w_mem_kv": dw_kv.reshape(CHIPS, -1, 2 * M_W)}, grad_x)

    def fold(acc, heads):
        v = jnp.sum(acc, axis=-2)
        return jnp.sum(v.reshape(v.shape[:-1] + (heads, -1)), axis=-2)

    csum = jnp.sum(cacc, axis=1)
    sml = {
        "attn_norm": jnp.sum(g1acc, axis=0), "a_q_norm": fold(gqa_acc, A_HEADS), "a_k_norm": fold(gka_acc, A_HEADS),
        "b_q_norm": fold(gqb_acc[0], B_QH), "b_k_norm": fold(gkb_acc[0], B_KVH),
        "b_sinks": jnp.sum(sacc, axis=0)[:B_QH], "mem_norm": jnp.sum(gmem_acc, axis=0),
        "m_q_norm": fold(gqm_acc, M_HEADS), "m_k_norm": fold(gkm_acc, M_HEADS),
        "b_gate": jnp.sum(bacc, axis=0), "ffn_norm": jnp.sum(g2acc, axis=0),
        "conv_w": csum[1:], "conv_b": csum[0],
    }
    return loss, grad_x, sml


def _mesh_pos():
    return lax.axis_index("x"), lax.axis_index("y"), lax.axis_index("c")


def _chip_peers(x, y):
    return [(1 - x, y), (x, 1 - y), (1 - x, 1 - y)]


_ANY = pl.BlockSpec(memory_space=pl.ANY)


def _comm_call(body, *, name, n_in, out_shape, scratch):
    return pl.pallas_call(body, name=name, in_specs=[_ANY] * n_in, out_specs=[_ANY] * len(out_shape),
                          out_shape=out_shape, scratch_shapes=scratch)


def _remote(src, dst, send_sem, recv_sem, dev):
    return pltpu.make_async_remote_copy(src_ref=src, dst_ref=dst, send_sem=send_sem, recv_sem=recv_sem,
                                        device_id=dev, device_id_type=MESH)


def _pair_join(halves, name):
    nt = len(halves)

    def body(*refs):
        ins, got = refs[:nt], refs[nt:2 * nt]
        send_sems, recv_sems = refs[2 * nt:]
        x, y, c = _mesh_pos()
        cps = []
        for t in range(nt):
            rc = _remote(ins[t], got[t], send_sems.at[t], recv_sems.at[t], (x, y, 1 - c))
            rc.start()
            cps.append(rc)
        for rc in cps:
            rc.wait()

    out_shape = [_sds(hf.shape, hf.dtype) for hf in halves]
    scratch = [pltpu.SemaphoreType.DMA((nt,)), pltpu.SemaphoreType.DMA((nt,))]
    return _comm_call(body, name=name, n_in=nt, out_shape=out_shape, scratch=scratch)(*halves)


_HBM = pl.BlockSpec(memory_space=pltpu.HBM)
_SEMS = pl.BlockSpec(memory_space=pltpu.SEMAPHORE)
_EFFECT = pltpu.SideEffectType.DATAFLOW_SIDE_EFFECTING


def _bcast_copies(ins, lands, send_sems, recv_sems):
    x, y, c = _mesh_pos()
    me = 2 * x + y
    targets = [((px, py, c), 2 * px + py) for px, py in _chip_peers(x, y)] + [((x, y, 1 - c), me)]
    out = []
    for t in range(len(ins)):
        for k, (dev, idx) in enumerate(targets):
            i = t * len(targets) + k
            arrival = lambda t=t, i=i, idx=idx, dev=dev: _remote(ins[t], lands[t].at[idx], send_sems.at[i],
                                                                 recv_sems.at[i], dev)
            out.append((_remote(ins[t], lands[t].at[me], send_sems.at[i], recv_sems.at[i], dev), arrival))
    return out


def _scatter_copies(ins, lands, send_sems, recv_sems):
    x, y, c = _mesh_pos()
    out = []
    for t in range(len(ins)):
        for k, (px, py) in enumerate(_chip_peers(x, y)):
            i = t * 3 + k
            cp = _remote(ins[t].at[2 * px + py], lands[t].at[k], send_sems.at[i], recv_sems.at[i], (px, py, c))
            out.append((cp, lambda cp=cp: cp))
    return out


def _pair_copies(ins, lands, send_sems, recv_sems):
    x, y, c = _mesh_pos()
    out = []
    for t in range(len(ins)):
        hr = ins[t].shape[1] // 2
        give = ins[t].at[:, pl.ds(pl.multiple_of((1 - c) * hr, 8), hr), :]
        cp = _remote(give, lands[t], send_sems.at[t], recv_sems.at[t], (x, y, 1 - c))
        out.append((cp, lambda cp=cp: cp))
    return out


def _join_copies(ins, lands, send_sems, recv_sems):
    x, y, c = _mesh_pos()
    out = []
    for t in range(len(ins)):
        cp = _remote(ins[t], lands[t], send_sems.at[t], recv_sems.at[t], (x, y, 1 - c))
        out.append((cp, lambda cp=cp: cp))
    return out


def _half_copies(ins, lands, send_sems, recv_sems):
    x, y, c = _mesh_pos()
    me = 2 * x + y
    out = []
    for t in range(len(ins)):
        hr = ins[t].shape[0] // 2
        rows = pl.ds(pl.multiple_of(c * hr, 8), hr)
        for k, (px, py) in enumerate(_chip_peers(x, y)):
            i = t * 3 + k
            arrival = lambda t=t, i=i, px=px, py=py, rows=rows: _remote(
                ins[t].at[rows, :], lands[t].at[2 * px + py].at[rows, :], send_sems.at[i], recv_sems.at[i], (px, py, c))
            out.append((_remote(ins[t].at[rows, :], lands[t].at[me].at[rows, :], send_sems.at[i], recv_sems.at[i],
                                (px, py, c)), arrival))
    return out


def _finish_halves(shards, stacks):
    nt = len(shards)

    def body(*refs):
        ins, held, outs = refs[:nt], refs[nt:2 * nt], refs[2 * nt:3 * nt]
        fwd_s, fwd_r, own_s, own_r = refs[3 * nt:]
        x, y, c = _mesh_pos()
        me = 2 * x + y
        sib = (x, y, 1 - c)
        pending = []
        for t in range(nt):
            hr = shards[t].shape[0] // 2
            half = lambda ref, who: ref.at[pl.ds(pl.multiple_of(who * hr, 8), hr), :]
            own = _remote(ins[t], outs[t].at[me], own_s.at[t], own_r.at[t], sib)
            own.start()
            pending.append(own.wait)
            for k, (px, py) in enumerate(_chip_peers(x, y)):
                pj = 2 * px + py
                fw = _remote(half(held[t].at[pj], c), half(outs[t].at[pj], c), fwd_s.at[t, k], fwd_r.at[t, k], sib)
                fw.start()
                pending.append(fw.wait_send)
                other = half(outs[t].at[pj], 1 - c)
                pending.append(_remote(other, other, fwd_s.at[t, k], fwd_r.at[t, k], sib).wait_recv)
        for wait in pending:
            wait()

    dma = pltpu.SemaphoreType.DMA
    return pl.pallas_call(
        body, name="gather_w_in_finish", in_specs=[_ANY] * (2 * nt), out_specs=[_ANY] * nt,
        out_shape=[_sds(a.shape, a.dtype) for a in stacks], input_output_aliases={nt + i: i for i in range(nt)},
        scratch_shapes=[dma((nt, 3)), dma((nt, 3)), dma((nt,)), dma((nt,))])(*shards, *stacks)


def _split_start(copies, srcs, land_shapes, ncopy, dep, name, lands=None):
    nt = len(srcs)

    def body(*refs):
        ins, lands = refs[:nt], refs[nt:2 * nt]
        send_sems, recv_sems, token = refs[2 * nt + 1], refs[2 * nt + 2], refs[-1]
        for send, _ in copies(ins, lands, send_sems, recv_sems):
            send.start()
        token[...] = jnp.zeros_like(token)

    if lands is None:
        lands = [lax.empty(sh, a.dtype) for sh, a in zip(land_shapes, srcs)]
    lands = [pltpu.with_memory_space_constraint(a, pltpu.HBM) for a in lands]
    srcs = [pltpu.with_memory_space_constraint(a, pltpu.HBM) for a in srcs]
    dma = pltpu.SemaphoreType.DMA
    out_shape = ([dma((nt * ncopy,)), dma((nt * ncopy,))] + [pltpu.HBM(a.shape, a.dtype) for a in srcs + lands]
                 + [_sds((8, 128))])
    outs = pl.pallas_call(
        body, name=name, in_specs=[_HBM] * (2 * nt) + [_ANY],
        out_specs=[_SEMS, _SEMS] + [_HBM] * (2 * nt) + [pl.BlockSpec(memory_space=pltpu.VMEM)], out_shape=out_shape,
        input_output_aliases={i: 2 + i for i in range(2 * nt)},
        compiler_params=pltpu.CompilerParams(has_side_effects=_EFFECT))(*srcs, *lands, dep)
    return outs[0], outs[1], outs[2:2 + nt], outs[2 + nt:2 + 2 * nt], outs[-1]


def _split_wait(copies, send_sems, recv_sems, srcs, lands, after, name):
    nt = len(srcs)

    def body(*refs):
        ins, lnd = refs[:nt], refs[nt:2 * nt]
        for send, arrival in copies(ins, lnd, refs[2 * nt], refs[2 * nt + 1]):
            send.wait_send()
            arrival().wait_recv()

    outs = pl.pallas_call(
        body, name=name, in_specs=[_HBM] * (2 * nt) + [_SEMS, _SEMS, _ANY], out_specs=[_HBM] * (2 * nt),
        out_shape=[pltpu.HBM(a.shape, a.dtype) for a in list(srcs) + list(lands)],
        input_output_aliases={i: i for i in range(2 * nt)},
        compiler_params=pltpu.CompilerParams(has_side_effects=_EFFECT))(*srcs, *lands, send_sems, recv_sems, after)
    return outs[:nt], outs[nt:]


def _small_copies(ins, lands, send_sems, recv_sems):
    x, y, c = _mesh_pos()
    me = 4 * x + 2 * y + c
    out = []
    for k in range(1, NDEV):
        px, py, pc = x ^ (k >> 2), y ^ ((k >> 1) & 1), c ^ (k & 1)
        arrival = lambda k=k, px=px, py=py, pc=pc: _remote(ins[0], lands[0].at[4 * px + 2 * py + pc], send_sems.at[k - 1],
                                                            recv_sems.at[k - 1], (px, py, pc))
        out.append((_remote(ins[0], lands[0].at[me], send_sems.at[k - 1], recv_sems.at[k - 1], (px, py, pc)), arrival))
    return out


def _row_tile(r, c, mib=1):
    t = r
    while t * c * 4 > (mib << 20) and t % 16 == 0:
        t //= 2
    return t


def _k_pair_add(full, got, name):
    g, r, c = full.shape
    hr = r // 2
    tr = _row_tile(hr, c, 4)
    nh = hr // tr

    def body(a_ref, b_ref, o_ref):
        o_ref[...] = (a_ref[...] + b_ref[...]).astype(_WIRE)

    mine = pl.BlockSpec((None, tr, c), lambda i, j: (i, lax.axis_index("c") * nh + j, 0))
    spec = pl.BlockSpec((None, tr, c), lambda i, j: (i, j, 0))
    return _pc(body, name=name, grid=(g, nh), in_specs=[mine, spec], out_specs=[spec],
               out_shape=[_sds((g, hr, c), _WIRE)])(full, got)[0]


def _k_chip_sum(parts, slots, name):
    _, r, c = parts.shape
    tr = _row_tile(r, c, 4)

    def body(a_ref, s_ref, o_ref):
        acc = a_ref[...].astype(F32)
        for k in range(3):
            acc = acc + s_ref[k].astype(F32)
        o_ref[...] = acc

    own = pl.BlockSpec((None, tr, c), lambda i: (2 * lax.axis_index("x") + lax.axis_index("y"), i, 0))
    return _pc(body, name=name, grid=(r // tr,), in_specs=[own, pl.BlockSpec((3, tr, c), lambda i: (0, i, 0))],
               out_specs=[_row(tr, c)], out_shape=[_sds((r, c))])(parts, slots)[0]


def _adam(w, g, m, v):
    m = ADAM_B1 * m + (1.0 - ADAM_B1) * g
    v = ADAM_B2 * v + (1.0 - ADAM_B2) * (g * g)
    m_hat = m / (1.0 - ADAM_B1 ** ADAM_STEP)
    v_hat = v / (1.0 - ADAM_B2 ** ADAM_STEP)
    return -ADAM_LR * (m_hat / (jnp.sqrt(v_hat) + ADAM_EPS) + ADAM_WD * w), m, v


def _k_adam(w, mine, theirs, m, v, dep, name):
    r, c = w.shape
    hr = r // 2
    tr = _row_tile(hr, c, 2)
    nh = hr // tr

    def body(w_ref, a_ref, b_ref, m_ref, v_ref, dep_ref, g_ref, d_ref, mo_ref, vo_ref):
        upper = (pl.program_id(0) >= nh).astype(jnp.int32)
        g = jnp.where(upper == lax.axis_index("c"), a_ref[...], b_ref[...])
        g_ref[...] = g
        d_ref[...], mo_ref[...], vo_ref[...] = _adam(w_ref[...], g, m_ref[...], v_ref[...])

    hspec = pl.BlockSpec((tr, c), lambda i: (jnp.where(i >= nh, i - nh, i), 0))
    return _pc(body, name=name, grid=(r // tr,),
               in_specs=[_row(tr, c), hspec, hspec, _row(tr, c), _row(tr, c), _res((8, 128))],
               out_specs=[_row(tr, c)] * 4, out_shape=[_sds((r, c))] * 4)(w, mine, theirs, m, v, dep)


def _k_sum8(a):
    _, n, _ = a.shape

    def body(a_ref, o_ref):
        acc = a_ref[0]
        for k in range(1, NDEV):
            acc = acc + a_ref[k]
        o_ref[...] = acc

    return _pc(body, name="sum_small_grads", grid=(1,), in_specs=[_acc(a.shape)], out_specs=[_acc((n, 128))],
               out_shape=[_sds((n, 128))])(a)[0]


def _k_adam_small(ws, gs, ms, vs):
    n = len(ws)

    def body(*refs):
        for k in range(n):
            w_ref, g_ref, m_ref, v_ref, d_ref, mo_ref, vo_ref = refs[k::n]
            d_ref[...], mo_ref[...], vo_ref[...] = _adam(w_ref[...], g_ref[...], m_ref[...], v_ref[...])

    specs = [_acc(a.shape) for a in ws]
    outs = _pc(body, name="adam_small", grid=(1,), in_specs=specs * 4, out_specs=specs * 3,
               out_shape=[_sds(a.shape) for a in ws] * 3)(*ws, *gs, *ms, *vs)
    return outs[:n], outs[n:2 * n], outs[2 * n:]


def _pack(vals):
    rows = []
    for a in vals:
        flat = a.reshape(-1)
        n = -(-flat.shape[0] // 1024) * 1024
        rows.append(jnp.pad(flat, (0, n - flat.shape[0])).reshape(n // 128, 128))
    return jnp.concatenate(rows, axis=0)


def _unpack(packed, shapes):
    out, off = [], 0
    for sh in shapes:
        size = int(np.prod(sh))
        n = -(-size // 1024) * 1024
        out.append(packed[off // 128:(off + n) // 128].reshape(-1)[:size].reshape(sh))
        off += n
    return out


_WEIGHTS = ["attn_norm", "w_in", "a_q_norm", "a_k_norm", "b_q_norm", "b_k_norm", "b_sinks", "mem_norm", "w_mem_kv",
            "m_q_norm", "m_k_norm", "w_o_a", "w_o_b", "w_o_m", "w_gate", "b_gate", "w_out", "ffn_norm", "w_up",
            "conv_w", "conv_b", "w_down"]
_BIG = ["w_in", "w_mem_kv", "w_o_a", "w_o_b", "w_o_m", "w_gate", "w_out", "w_up", "w_down"]
_SMALL = [n for n in _WEIGHTS if n not in _BIG]


def kernel(x, mem, positions, attn_norm, w_in, a_q_norm, a_k_norm, b_q_norm, b_k_norm, b_sinks, mem_norm, w_mem_kv, m_q_norm, m_k_norm, w_o_a, w_o_b, w_o_m, w_gate, b_gate, w_out, ffn_norm, w_up, conv_w, conv_b, w_down, loss_target, m_attn_norm, m_w_in, m_a_q_norm, m_a_k_norm, m_b_q_norm, m_b_k_norm, m_b_sinks, m_mem_norm, m_w_mem_kv, m_m_q_norm, m_m_k_norm, m_w_o_a, m_w_o_b, m_w_o_m, m_w_gate, m_b_gate, m_w_out, m_ffn_norm, m_w_up, m_conv_w, m_conv_b, m_w_down, v_attn_norm, v_w_in, v_a_q_norm, v_a_k_norm, v_b_q_norm, v_b_k_norm, v_b_sinks, v_mem_norm, v_w_mem_kv, v_m_q_norm, v_m_k_norm, v_w_o_a, v_w_o_b, v_w_o_m, v_w_gate, v_b_gate, v_w_out, v_ffn_norm, v_w_up, v_conv_w, v_conv_b, v_w_down):
    given = dict(locals())
    w = {n: given[n][0] for n in _WEIGHTS}
    m1 = {n: given["m_" + n][0] for n in _WEIGHTS}
    m2 = {n: given["v_" + n][0] for n in _WEIGHTS}

    zeros = jnp.zeros((8, 128), F32)
    w_in_shard = w["w_in"].astype(_MM)
    *w_in_handles, tok = _split_start(_half_copies, [w_in_shard], [(CHIPS,) + w_in_shard.shape], 3, zeros,
                                      "gather_w_in_start")

    def get_w_in(after):
        send, recv, srcs, lands = w_in_handles
        srcs, lands = _split_wait(_half_copies, send, recv, srcs, lands, after, "gather_w_in_wait")
        return _finish_halves(srcs, lands)[0]

    stages = (["w_gate", "w_mem_kv", "w_o_a", "w_o_b", "w_o_m", "w_out"], ["w_up", "w_down", "conv_w"])
    started = []
    for k, names in enumerate(stages):
        shards = [w[n] if n == "conv_w" else w[n].astype(_MM) for n in names]
        *handles, tok = _split_start(_bcast_copies, shards, [(CHIPS,) + a.shape for a in shards], 4, tok,
                                     "gather_start_%d" % k)
        started.append(handles)
    small = {n: (w[n][None, :] if w[n].ndim == 1 else w[n]) for n in _SMALL if n != "conv_w"}
    positions = positions + tok[0:1, 0:1].astype(positions.dtype)

    def get_rest(stage, after):
        send, recv, srcs, lands = started[stage]
        got = _split_wait(_bcast_copies, send, recv, srcs, lands, after, "gather_wait_%d" % stage)[1]
        wts = dict(zip(stages[stage], got))
        for n in ("w_mem_kv", "w_out", "w_down"):
            if n in wts:
                wts[n] = wts[n].reshape(-1, wts[n].shape[-1])
        return wts

    parts, slots, pair, scat, started_pair = {}, {}, [], [], [None]

    def finish_pair(after):
        names, tag, send, recv, srcs, lands = pair.pop()
        full, got = _split_wait(_pair_copies, send, recv, srcs, lands, after, "pair_wait_" + tag)
        mine = [_k_pair_add(f, b, "pair_add_" + n) for n, f, b in zip(names, full, got)]
        shapes = [(3,) + p.shape[1:] for p in mine]
        send, recv, srcs, lands, token = _split_start(_scatter_copies, mine, shapes, 3, zeros, "scatter_start_" + tag)
        scat.append((names, tag, send, recv, srcs, lands))
        return token

    def on_grads(group, after):
        names = list(group)
        tag = "_".join(names)
        token = finish_pair(after) if pair else zeros
        if not group:
            return token
        grads_g = [group[n] for n in names]
        shapes = [(CHIPS, g.shape[1] // 2, g.shape[2]) for g in grads_g]
        send, recv, srcs, lands, token = _split_start(_pair_copies, grads_g, shapes, 1, token, "pair_start_" + tag)
        pair.append((names, tag, send, recv, srcs, lands))
        started_pair[0] = token
        return token

    loss, grad_x, sml = _local_step(x[0], mem[0], positions[0], loss_target[0], small, get_w_in, get_rest, on_grads)

    packed = _pack([sml[n] for n in _SMALL] + [loss.reshape(1)])
    me = 4 * lax.axis_index("x") + 2 * lax.axis_index("y") + lax.axis_index("c")
    land = lax.dynamic_update_slice(jnp.zeros((NDEV,) + packed.shape, F32), packed[None], (me, 0, 0))
    *small_h, tok = _split_start(_small_copies, [packed], None, NDEV - 1, zeros, "gather_small_start", lands=[land])
    started_pair[0] = started_pair[0] + tok

    early = [n for names, *_ in scat for n in names]
    for names, tag, send, recv, srcs, lands in scat:
        mine, got = _split_wait(_scatter_copies, send, recv, srcs, lands, started_pair[0], "scatter_wait_" + tag)
        parts.update(zip(names, mine))
        slots.update(zip(names, got))
    scat.clear()
    reduced = {n: _k_chip_sum(parts[n], slots[n], "chip_add_" + n) for n in early}
    halves = [reduced[n] for n in early]
    *join, tok = _split_start(_join_copies, halves, [a.shape for a in halves], 1, zeros, "pair_join_start_early")
    grads = {}

    delta, new_m, new_v = {}, {}, {}
    dep = finish_pair(tok)
    mine, got = _split_wait(_join_copies, *join, dep, "pair_join_wait_early")
    reduced.update(zip(early, mine))
    theirs = dict(zip(early, got))
    for n in early:
        grads[n], delta[n], new_m[n], new_v[n] = _k_adam(w[n], reduced[n], theirs[n], m1[n], m2[n], dep, "adam_" + n)
        dep = delta[n]
    gathered = _split_wait(_small_copies, *small_h, dep, "gather_small_wait")[1][0]
    shapes = [sml[n].shape for n in _SMALL] + [(1,)]
    *gsmall, loss = _unpack(_k_sum8(gathered), shapes)
    loss = loss[0]
    gsm = dict(zip(_SMALL, gsmall))
    nu = w["conv_w"].shape[1]
    chip = 2 * lax.axis_index("x") + lax.axis_index("y")
    gsm["conv_w"] = lax.dynamic_slice_in_dim(gsm["conv_w"], chip * nu, nu, axis=1)
    for n in _SMALL:
        grads[n] = gsm[n].reshape(w[n].shape)
    as2d = lambda d: [d[n][None, :] if d[n].ndim == 1 else d[n] for n in _SMALL]
    for dst, outs in zip((delta, new_m, new_v), _k_adam_small(as2d(w), as2d(grads), as2d(m1), as2d(m2))):
        dst.update((n, a.reshape(w[n].shape)) for n, a in zip(_SMALL, outs))
    late, tag, send, recv, srcs, lands = scat.pop()
    mine, got = _split_wait(_scatter_copies, send, recv, srcs, lands, dep, "scatter_wait_" + tag)
    for n, a, b in zip(late, mine, got):
        reduced[n] = _k_chip_sum(a, b, "chip_add_" + n)
    theirs.update(zip(late, _pair_join([reduced[n] for n in late], "grad_pair_join_late")))
    for n in late:
        grads[n], delta[n], new_m[n], new_v[n] = _k_adam(w[n], reduced[n], theirs[n], m1[n], m2[n], zeros, "adam_" + n)

    lead = lambda d: [d[n][None] for n in _WEIGHTS]
    return (loss, grad_x[None], *lead(grads), *lead(delta), *lead(new_m), *lead(new_v))
```
